```python
import math
import jax, jax.numpy as jnp
from jax import lax
import numpy as np

D_MODEL = 1024
BATCH = 8
SEQ = 4096
DEPTH = 1

D_RNN = 1024
RNN_BLOCKS = 16
RNN_BLOCK_W = D_RNN // RNN_BLOCKS
CONV_W = 4
LRU_C = 8.0
N_Q_HEADS = 16
N_KV_HEADS = 4
GROUP = N_Q_HEADS // N_KV_HEADS
HEAD_DIM = 64
D_ATTN = N_Q_HEADS * HEAD_DIM
D_KV = N_KV_HEADS * HEAD_DIM
WINDOW = 128
BLOCK = 128
ALIBI_MAX_BIAS = 8.0
N_BRANCH = 2
EPS = 1e-6

SPLIT_SIZES = (D_RNN, D_RNN, D_ATTN, D_KV, D_KV, D_ATTN, N_BRANCH * D_MODEL)
D_IN = sum(SPLIT_SIZES)
SPLIT_POINTS = tuple(int(v) for v in np.cumsum(SPLIT_SIZES)[:-1])

kernel_name = "hybrid_rglru_swa_sink_alibi_block"


def rms_norm(x, g):
    xf = x.astype(jnp.float32)
    y = xf * lax.rsqrt(jnp.mean(xf * xf, axis=-1, keepdims=True) + EPS)
    return (y * g.astype(jnp.float32)).astype(x.dtype)


def causal_depthwise_conv(x, w, b):
    y = lax.conv_general_dilated(
        x, w[:, None, :].astype(x.dtype), window_strides=(1,),
        padding=[(CONV_W - 1, 0)], dimension_numbers=("NWC", "WIO", "NWC"),
        feature_group_count=x.shape[-1])
    return y + b


def rg_lru(x, w_a, b_a, w_x, b_x, lam):
    B, T, _ = x.shape
    xb = x.reshape(B, T, RNN_BLOCKS, RNN_BLOCK_W)
    r = jax.nn.sigmoid(jnp.einsum("bthi,hij->bthj", xb, w_a) + b_a).reshape(B, T, D_RNN)
    i = jax.nn.sigmoid(jnp.einsum("bthi,hij->bthj", xb, w_x) + b_x).reshape(B, T, D_RNN)
    log_a = -LRU_C * r.astype(jnp.float32) * jax.nn.softplus(-lam.astype(jnp.float32))
    a = jnp.exp(log_a)
    mult = jnp.sqrt(-jnp.expm1(2.0 * log_a))
    u = mult * (i * x).astype(jnp.float32)

    def combine(left, right):
        a_l, u_l = left
        a_r, u_r = right
        return a_l * a_r, a_r * u_l + u_r

    _, h = lax.associative_scan(combine, (a, u), axis=1)
    return h.astype(x.dtype)


def sliding_window_sink_attention(q, k, v, sinks):
    B, T, _, _ = q.shape
    nb = T // BLOCK
    scale = HEAD_DIM ** -0.5
    qb = q.reshape(B, nb, BLOCK, N_KV_HEADS, GROUP, HEAD_DIM)

    def band(t):
        tp = jnp.pad(t, ((0, 0), (BLOCK, 0), (0, 0), (0, 0)))
        tp = tp.reshape(B, nb + 1, BLOCK, N_KV_HEADS, HEAD_DIM)
        return jnp.concatenate([tp[:, :-1], tp[:, 1:]], axis=2)

    kw, vw = band(k), band(v)
    scores = jnp.einsum("bnqhgd,bnkhd->bnhgqk", qb, kw).astype(jnp.float32) * scale

    q_loc = jnp.arange(BLOCK)[:, None] + BLOCK
    k_loc = jnp.arange(2 * BLOCK)[None, :]
    dist = q_loc - k_loc
    in_window = (dist >= 0) & (dist < WINDOW)
    k_abs = jnp.arange(nb)[:, None] * BLOCK - BLOCK + jnp.arange(2 * BLOCK)[None, :]
    mask = in_window[None, :, :] & (k_abs >= 0)[:, None, :]

    slopes = 2.0 ** (-ALIBI_MAX_BIAS * jnp.arange(1, N_Q_HEADS + 1, dtype=jnp.float32) / N_Q_HEADS)
    slopes = slopes.reshape(N_KV_HEADS, GROUP)
    alibi = -slopes[:, :, None, None] * dist.astype(jnp.float32)

    scores = jnp.where(mask[None, :, None, None], scores + alibi[None, None], jnp.float32(-1e30))
    sink = sinks.astype(jnp.float32).reshape(N_KV_HEADS, GROUP)[None, None, :, :, None, None]
    m = jnp.maximum(jnp.max(scores, axis=-1, keepdims=True), sink)
    p = jnp.exp(scores - m)
    denom = jnp.sum(p, axis=-1, keepdims=True) + jnp.exp(sink - m)
    probs = (p / denom).astype(v.dtype)
    out = jnp.einsum("bnhgqk,bnkhd->bnqhgd", probs, vw)
    return out.reshape(B, T, D_ATTN)


def _fwd_setup_inputs(seed: int = 0) -> dict:
    key = jax.random.key(seed)
    ks = jax.random.split(key, 16)
    f32 = jnp.float32
    x = jax.random.normal(ks[0], (BATCH, SEQ, D_MODEL), f32)
    pre_norm_g = 1.0 + 0.05 * jax.random.normal(ks[1], (DEPTH, D_MODEL), f32)
    w_in = jax.random.normal(ks[2], (DEPTH, D_MODEL, D_IN), f32) * D_MODEL ** -0.5
    b_gate = 0.01 * jax.random.normal(ks[3], (DEPTH, N_BRANCH * D_MODEL), f32)
    conv_w = jax.random.normal(ks[4], (DEPTH, CONV_W, D_RNN), f32) * CONV_W ** -0.5
    conv_b = 0.01 * jax.random.normal(ks[5], (DEPTH, D_RNN), f32)
    w_rg_a = jax.random.normal(ks[6], (DEPTH, RNN_BLOCKS, RNN_BLOCK_W, RNN_BLOCK_W), f32) * RNN_BLOCK_W ** -0.5
    b_rg_a = 0.01 * jax.random.normal(ks[7], (DEPTH, RNN_BLOCKS, RNN_BLOCK_W), f32)
    w_rg_x = jax.random.normal(ks[8], (DEPTH, RNN_BLOCKS, RNN_BLOCK_W, RNN_BLOCK_W), f32) * RNN_BLOCK_W ** -0.5
    b_rg_x = 0.01 * jax.random.normal(ks[9], (DEPTH, RNN_BLOCKS, RNN_BLOCK_W), f32)
    a0 = jax.random.uniform(ks[10], (DEPTH, D_RNN), f32, minval=0.9, maxval=0.999)
    a_base = a0 ** (1.0 / LRU_C)
    lru_lambda = jnp.log(a_base) - jnp.log1p(-a_base)
    attn_sinks = 0.5 * jax.random.normal(ks[11], (DEPTH, N_Q_HEADS), f32)
    w_rnn_out = jax.random.normal(ks[12], (DEPTH, D_RNN, D_MODEL), f32) * D_RNN ** -0.5
    w_attn_out = jax.random.normal(ks[13], (DEPTH, D_ATTN, D_MODEL), f32) * D_ATTN ** -0.5
    w_out = jax.random.normal(ks[14], (DEPTH, D_MODEL, D_MODEL), f32) * D_MODEL ** -0.5
    post_norm_g = 1.0 + 0.05 * jax.random.normal(ks[15], (DEPTH, D_MODEL), f32)
    return {"x": x, "pre_norm_g": pre_norm_g, "w_in": w_in, "b_gate": b_gate,
            "conv_w": conv_w, "conv_b": conv_b, "w_rg_a": w_rg_a, "b_rg_a": b_rg_a,
            "w_rg_x": w_rg_x, "b_rg_x": b_rg_x, "lru_lambda": lru_lambda,
            "attn_sinks": attn_sinks, "w_rnn_out": w_rnn_out, "w_attn_out": w_attn_out,
            "w_out": w_out, "post_norm_g": post_norm_g}


def _fwd_reference(x, pre_norm_g, w_in, b_gate, conv_w, conv_b, w_rg_a, b_rg_a, w_rg_x, b_rg_x,
              lru_lambda, attn_sinks, w_rnn_out, w_attn_out, w_out, post_norm_g):
    B, T, _ = x.shape
    for l in range(DEPTH):
        h = rms_norm(x, pre_norm_g[l])
        proj = h @ w_in[l]
        rnn_x, rnn_gate, q, k, v, attn_gate, merge_logits = jnp.split(proj, SPLIT_POINTS, axis=-1)

        c = causal_depthwise_conv(rnn_x, conv_w[l], conv_b[l])
        y_rnn = rg_lru(c, w_rg_a[l], b_rg_a[l], w_rg_x[l], b_rg_x[l], lru_lambda[l])
        br_rnn = (y_rnn * jax.nn.silu(rnn_gate)) @ w_rnn_out[l]

        qh = q.reshape(B, T, N_Q_HEADS, HEAD_DIM)
        kh = k.reshape(B, T, N_KV_HEADS, HEAD_DIM)
        vh = v.reshape(B, T, N_KV_HEADS, HEAD_DIM)
        y_attn = sliding_window_sink_attention(qh, kh, vh, attn_sinks[l])
        br_attn = (y_attn * jax.nn.silu(attn_gate)) @ w_attn_out[l]

        g_rnn, g_attn = jnp.split(jax.nn.sigmoid(merge_logits + b_gate[l]), N_BRANCH, axis=-1)
        merged = g_rnn * br_rnn + g_attn * br_attn
        out = merged @ w_out[l]
        x = x + rms_norm(out, post_norm_g[l])
    return x


import jax as _jax
import jax.numpy as _jnp

TWIN_FORMAT = 'train_step'
FWD_PARAMS = ['x', 'pre_norm_g', 'w_in', 'b_gate', 'conv_w', 'conv_b', 'w_rg_a', 'b_rg_a', 'w_rg_x', 'b_rg_x', 'lru_lambda', 'attn_sinks', 'w_rnn_out', 'w_attn_out', 'w_out', 'post_norm_g']
TWIN_WEIGHTS = ['pre_norm_g', 'w_in', 'b_gate', 'conv_w', 'conv_b', 'w_rg_a', 'b_rg_a', 'w_rg_x', 'b_rg_x', 'lru_lambda', 'attn_sinks', 'w_rnn_out', 'w_attn_out', 'w_out', 'post_norm_g']
TWIN_DIFF_INPUT = 'x'
TWIN_INPUTS = ['x', 'pre_norm_g', 'w_in', 'b_gate', 'conv_w', 'conv_b', 'w_rg_a', 'b_rg_a', 'w_rg_x', 'b_rg_x', 'lru_lambda', 'attn_sinks', 'w_rnn_out', 'w_attn_out', 'w_out', 'post_norm_g', 'loss_target', 'm_pre_norm_g', 'm_w_in', 'm_b_gate', 'm_conv_w', 'm_conv_b', 'm_w_rg_a', 'm_b_rg_a', 'm_w_rg_x', 'm_b_rg_x', 'm_lru_lambda', 'm_attn_sinks', 'm_w_rnn_out', 'm_w_attn_out', 'm_w_out', 'm_post_norm_g', 'v_pre_norm_g', 'v_w_in', 'v_b_gate', 'v_conv_w', 'v_conv_b', 'v_w_rg_a', 'v_b_rg_a', 'v_w_rg_x', 'v_b_rg_x', 'v_lru_lambda', 'v_attn_sinks', 'v_w_rnn_out', 'v_w_attn_out', 'v_w_out', 'v_post_norm_g']
TWIN_OUTPUTS = ['loss', 'grad_x', 'grad_pre_norm_g', 'grad_w_in', 'grad_b_gate', 'grad_conv_w', 'grad_conv_b', 'grad_w_rg_a', 'grad_b_rg_a', 'grad_w_rg_x', 'grad_b_rg_x', 'grad_lru_lambda', 'grad_attn_sinks', 'grad_w_rnn_out', 'grad_w_attn_out', 'grad_w_out', 'grad_post_norm_g', 'delta_pre_norm_g', 'delta_w_in', 'delta_b_gate', 'delta_conv_w', 'delta_conv_b', 'delta_w_rg_a', 'delta_b_rg_a', 'delta_w_rg_x', 'delta_b_rg_x', 'delta_lru_lambda', 'delta_attn_sinks', 'delta_w_rnn_out', 'delta_w_attn_out', 'delta_w_out', 'delta_post_norm_g', 'new_m_pre_norm_g', 'new_m_w_in', 'new_m_b_gate', 'new_m_conv_w', 'new_m_conv_b', 'new_m_w_rg_a', 'new_m_b_rg_a', 'new_m_w_rg_x', 'new_m_b_rg_x', 'new_m_lru_lambda', 'new_m_attn_sinks', 'new_m_w_rnn_out', 'new_m_w_attn_out', 'new_m_w_out', 'new_m_post_norm_g', 'new_v_pre_norm_g', 'new_v_w_in', 'new_v_b_gate', 'new_v_conv_w', 'new_v_conv_b', 'new_v_w_rg_a', 'new_v_b_rg_a', 'new_v_w_rg_x', 'new_v_b_rg_x', 'new_v_lru_lambda', 'new_v_attn_sinks', 'new_v_w_rnn_out', 'new_v_w_attn_out', 'new_v_w_out', 'new_v_post_norm_g']
TWIN_LEAF_KINDS = {'loss': 'loss', 'grad_x': 'grad_x', 'grad_pre_norm_g': 'grad_w', 'grad_w_in': 'grad_w', 'grad_b_gate': 'grad_w', 'grad_conv_w': 'grad_w', 'grad_conv_b': 'grad_w', 'grad_w_rg_a': 'grad_w', 'grad_b_rg_a': 'grad_w', 'grad_w_rg_x': 'grad_w', 'grad_b_rg_x': 'grad_w', 'grad_lru_lambda': 'grad_w', 'grad_attn_sinks': 'grad_w', 'grad_w_rnn_out': 'grad_w', 'grad_w_attn_out': 'grad_w', 'grad_w_out': 'grad_w', 'grad_post_norm_g': 'grad_w', 'delta_pre_norm_g': 'delta_w', 'delta_w_in': 'delta_w', 'delta_b_gate': 'delta_w', 'delta_conv_w': 'delta_w', 'delta_conv_b': 'delta_w', 'delta_w_rg_a': 'delta_w', 'delta_b_rg_a': 'delta_w', 'delta_w_rg_x': 'delta_w', 'delta_b_rg_x': 'delta_w', 'delta_lru_lambda': 'delta_w', 'delta_attn_sinks': 'delta_w', 'delta_w_rnn_out': 'delta_w', 'delta_w_attn_out': 'delta_w', 'delta_w_out': 'delta_w', 'delta_post_norm_g': 'delta_w', 'new_m_pre_norm_g': 'new_m', 'new_m_w_in': 'new_m', 'new_m_b_gate': 'new_m', 'new_m_conv_w': 'new_m', 'new_m_conv_b': 'new_m', 'new_m_w_rg_a': 'new_m', 'new_m_b_rg_a': 'new_m', 'new_m_w_rg_x': 'new_m', 'new_m_b_rg_x': 'new_m', 'new_m_lru_lambda': 'new_m', 'new_m_attn_sinks': 'new_m', 'new_m_w_rnn_out': 'new_m', 'new_m_w_attn_out': 'new_m', 'new_m_w_out': 'new_m', 'new_m_post_norm_g': 'new_m', 'new_v_pre_norm_g': 'new_v', 'new_v_w_in': 'new_v', 'new_v_b_gate': 'new_v', 'new_v_conv_w': 'new_v', 'new_v_conv_b': 'new_v', 'new_v_w_rg_a': 'new_v', 'new_v_b_rg_a': 'new_v', 'new_v_w_rg_x': 'new_v', 'new_v_b_rg_x': 'new_v', 'new_v_lru_lambda': 'new_v', 'new_v_attn_sinks': 'new_v', 'new_v_w_rnn_out': 'new_v', 'new_v_w_attn_out': 'new_v', 'new_v_w_out': 'new_v', 'new_v_post_norm_g': 'new_v'}


def _forward(args):
    return _fwd_reference(*[args[k] for k in FWD_PARAMS])


def _output_shape():
    out = _jax.eval_shape(lambda: _forward(_fwd_setup_inputs(0)))
    return out.shape, out.dtype

N_MICROBATCH = 1
ADAM_LR = 0.001
ADAM_B1 = 0.9
ADAM_B2 = 0.999
ADAM_EPS = 1e-08
ADAM_WD = 0.01
ADAM_STEP = 10
PER_EXAMPLE_BATCH_AXIS = {'x': 0, 'loss_target': 0}
SHARED_INPUTS = []
_WEIGHT_DTYPES = {'pre_norm_g': _jnp.float32, 'w_in': _jnp.float32, 'b_gate': _jnp.float32, 'conv_w': _jnp.float32, 'conv_b': _jnp.float32, 'w_rg_a': _jnp.float32, 'b_rg_a': _jnp.float32, 'w_rg_x': _jnp.float32, 'b_rg_x': _jnp.float32, 'lru_lambda': _jnp.float32, 'attn_sinks': _jnp.float32, 'w_rnn_out': _jnp.float32, 'w_attn_out': _jnp.float32, 'w_out': _jnp.float32, 'post_norm_g': _jnp.float32}
MOMENT_SCALE = {'pre_norm_g': 3.987135e-01, 'w_in': 1.687490e-01, 'b_gate': 1.224375e-01, 'conv_w': 3.335921e-01, 'conv_b': 5.646521e+00, 'w_rg_a': 1.939383e-01, 'b_rg_a': 1.214772e-01, 'w_rg_x': 3.592738e-01, 'b_rg_x': 9.139338e-02, 'lru_lambda': 1.604733e-01, 'attn_sinks': 1.758621e-01, 'w_rnn_out': 3.733835e-01, 'w_attn_out': 1.245364e-01, 'w_out': 3.704077e-01, 'post_norm_g': 3.209720e+01}


def _to_microbatches(a, axis):
    t = _jnp.moveaxis(a, axis, 0)
    t = t.reshape((N_MICROBATCH, t.shape[0] // N_MICROBATCH) + t.shape[1:])
    return _jnp.moveaxis(t, 1, axis + 1)


def setup_inputs(seed: int = 0) -> dict:
    inp = _fwd_setup_inputs(seed)
    key = _jax.random.fold_in(_jax.random.key(seed), 7919)
    shape, _ = _output_shape()
    out = dict(inp)
    out["loss_target"] = _jax.random.normal(_jax.random.fold_in(key, 0), shape, _jnp.float32)
    for i, name in enumerate(TWIN_WEIGHTS):
        w = inp[name].astype(_jnp.float32)
        if MOMENT_SCALE is None:
            s = _jnp.sqrt(_jnp.mean(_jnp.square(w)) + 1e-30)
        else:
            s = MOMENT_SCALE[name]
        km, kv = _jax.random.split(_jax.random.fold_in(key, i + 1))
        out[name] = w
        out["m_" + name] = s * _jax.random.normal(km, w.shape, _jnp.float32)
        out["v_" + name] = (s * s) * _jax.random.uniform(kv, w.shape, _jnp.float32, 0.5, 1.5)
    if N_MICROBATCH > 1:
        for name, axis in PER_EXAMPLE_BATCH_AXIS.items():
            out[name] = _to_microbatches(out[name], axis)
    return {'x': out['x'], 'pre_norm_g': out['pre_norm_g'], 'w_in': out['w_in'], 'b_gate': out['b_gate'], 'conv_w': out['conv_w'], 'conv_b': out['conv_b'], 'w_rg_a': out['w_rg_a'], 'b_rg_a': out['b_rg_a'], 'w_rg_x': out['w_rg_x'], 'b_rg_x': out['b_rg_x'], 'lru_lambda': out['lru_lambda'], 'attn_sinks': out['attn_sinks'], 'w_rnn_out': out['w_rnn_out'], 'w_attn_out': out['w_attn_out'], 'w_out': out['w_out'], 'post_norm_g': out['post_norm_g'], 'loss_target': out['loss_target'], 'm_pre_norm_g': out['m_pre_norm_g'], 'm_w_in': out['m_w_in'], 'm_b_gate': out['m_b_gate'], 'm_conv_w': out['m_conv_w'], 'm_conv_b': out['m_conv_b'], 'm_w_rg_a': out['m_w_rg_a'], 'm_b_rg_a': out['m_b_rg_a'], 'm_w_rg_x': out['m_w_rg_x'], 'm_b_rg_x': out['m_b_rg_x'], 'm_lru_lambda': out['m_lru_lambda'], 'm_attn_sinks': out['m_attn_sinks'], 'm_w_rnn_out': out['m_w_rnn_out'], 'm_w_attn_out': out['m_w_attn_out'], 'm_w_out': out['m_w_out'], 'm_post_norm_g': out['m_post_norm_g'], 'v_pre_norm_g': out['v_pre_norm_g'], 'v_w_in': out['v_w_in'], 'v_b_gate': out['v_b_gate'], 'v_conv_w': out['v_conv_w'], 'v_conv_b': out['v_conv_b'], 'v_w_rg_a': out['v_w_rg_a'], 'v_b_rg_a': out['v_b_rg_a'], 'v_w_rg_x': out['v_w_rg_x'], 'v_b_rg_x': out['v_b_rg_x'], 'v_lru_lambda': out['v_lru_lambda'], 'v_attn_sinks': out['v_attn_sinks'], 'v_w_rnn_out': out['v_w_rnn_out'], 'v_w_attn_out': out['v_w_attn_out'], 'v_w_out': out['v_w_out'], 'v_post_norm_g': out['v_post_norm_g']}


def _loss(weights, diff, rest, loss_target):
    with _jax.named_scope("forward"):
        args = {**rest, TWIN_DIFF_INPUT: diff, **{k: w.astype(_WEIGHT_DTYPES[k]) for k, w in weights.items()}}
        y = _forward(args)
    with _jax.named_scope("loss_head"):
        err = _jnp.square(y.astype(_jnp.float32) - loss_target)
        return 0.5 * _jnp.sum(_jnp.mean(err, axis=-1)) if err.ndim else 0.5 * err


def _adamw(w, g, m, v):
    m = ADAM_B1 * m + (1.0 - ADAM_B1) * g
    v = ADAM_B2 * v + (1.0 - ADAM_B2) * _jnp.square(g)
    m_hat = m / (1.0 - ADAM_B1 ** ADAM_STEP)
    v_hat = v / (1.0 - ADAM_B2 ** ADAM_STEP)
    delta = -ADAM_LR * (m_hat / (_jnp.sqrt(v_hat) + ADAM_EPS) + ADAM_WD * w)
    return delta, m, v


def reference(x, pre_norm_g, w_in, b_gate, conv_w, conv_b, w_rg_a, b_rg_a, w_rg_x, b_rg_x, lru_lambda, attn_sinks, w_rnn_out, w_attn_out, w_out, post_norm_g, loss_target, m_pre_norm_g, m_w_in, m_b_gate, m_conv_w, m_conv_b, m_w_rg_a, m_b_rg_a, m_w_rg_x, m_b_rg_x, m_lru_lambda, m_attn_sinks, m_w_rnn_out, m_w_attn_out, m_w_out, m_post_norm_g, v_pre_norm_g, v_w_in, v_b_gate, v_conv_w, v_conv_b, v_w_rg_a, v_b_rg_a, v_w_rg_x, v_b_rg_x, v_lru_lambda, v_attn_sinks, v_w_rnn_out, v_w_attn_out, v_w_out, v_post_norm_g):
    given = dict(x=x, pre_norm_g=pre_norm_g, w_in=w_in, b_gate=b_gate, conv_w=conv_w, conv_b=conv_b, w_rg_a=w_rg_a, b_rg_a=b_rg_a, w_rg_x=w_rg_x, b_rg_x=b_rg_x, lru_lambda=lru_lambda, attn_sinks=attn_sinks, w_rnn_out=w_rnn_out, w_attn_out=w_attn_out, w_out=w_out, post_norm_g=post_norm_g, loss_target=loss_target, m_pre_norm_g=m_pre_norm_g, m_w_in=m_w_in, m_b_gate=m_b_gate, m_conv_w=m_conv_w, m_conv_b=m_conv_b, m_w_rg_a=m_w_rg_a, m_b_rg_a=m_b_rg_a, m_w_rg_x=m_w_rg_x, m_b_rg_x=m_b_rg_x, m_lru_lambda=m_lru_lambda, m_attn_sinks=m_attn_sinks, m_w_rnn_out=m_w_rnn_out, m_w_attn_out=m_w_attn_out, m_w_out=m_w_out, m_post_norm_g=m_post_norm_g, v_pre_norm_g=v_pre_norm_g, v_w_in=v_w_in, v_b_gate=v_b_gate, v_conv_w=v_conv_w, v_conv_b=v_conv_b, v_w_rg_a=v_w_rg_a, v_b_rg_a=v_b_rg_a, v_w_rg_x=v_w_rg_x, v_b_rg_x=v_b_rg_x, v_lru_lambda=v_lru_lambda, v_attn_sinks=v_attn_sinks, v_w_rnn_out=v_w_rnn_out, v_w_attn_out=v_w_attn_out, v_w_out=v_w_out, v_post_norm_g=v_post_norm_g)
    weights = {n: given[n] for n in TWIN_WEIGHTS}
    shared = {n: given[n] for n in SHARED_INPUTS}
    per_example = {n: given[n] for n in ['x']}
    grad_fn = _jax.value_and_grad(_loss, argnums=(0, 1))

    def one_microbatch(ex, loss_target):
        ex = dict(ex)
        diff = ex.pop(TWIN_DIFF_INPUT)
        return grad_fn(weights, diff, {**shared, **ex}, loss_target)

    if N_MICROBATCH == 1:
        loss, (grad_w, grad_x) = one_microbatch(per_example, given["loss_target"])
    else:
        def body(carry, xs):
            loss_sum, grad_sum = carry
            l_k, (gw_k, gx_k) = one_microbatch(xs[0], xs[1])
            with _jax.named_scope("update"):
                return (loss_sum + l_k, _jax.tree.map(_jnp.add, grad_sum, gw_k)), gx_k

        init = (_jnp.zeros((), _jnp.float32), _jax.tree.map(_jnp.zeros_like, weights))
        (loss, grad_w), grad_x = _jax.lax.scan(body, init, (per_example, given["loss_target"]))
    with _jax.named_scope("update"):
        delta_w, new_m, new_v = {}, {}, {}
        for n in TWIN_WEIGHTS:
            delta_w[n], new_m[n], new_v[n] = _adamw(weights[n], grad_w[n], given["m_" + n], given["v_" + n])
    return (loss, grad_x, *[grad_w[n] for n in TWIN_WEIGHTS], *[delta_w[n] for n in TWIN_WEIGHTS],
            *[new_m[n] for n in TWIN_WEIGHTS], *[new_v[n] for n in TWIN_WEIGHTS])
```

```python
import functools
import math

import jax
import jax.numpy as jnp
from jax import lax
from jax.experimental import pallas as pl
from jax.experimental.pallas import tpu as pltpu

F32 = jnp.float32
BF16 = jnp.bfloat16

D_MODEL = 1024
D_RNN = 1024
RNN_BLOCKS = 16
RNN_BLOCK_W = 64
CONV_W = 4
LRU_C = 8.0
N_Q_HEADS = 16
N_KV_HEADS = 4
GROUP = 4
HEAD_DIM = 64
D_KV = 256
BLOCK = 128
ALIBI_MAX_BIAS = 8.0
EPS = 1e-6
D_IN = 6656
N_CHIPS = 4
W_IN_SHARD = D_IN // N_CHIPS
OUT_SHARD = D_MODEL // N_CHIPS
ADAM_LR = 0.001
ADAM_B1 = 0.9
ADAM_B2 = 0.999
ADAM_EPS = 1e-08
ADAM_WD = 0.01
ADAM_STEP = 10
NEG_BIG = -1e30
MIB = 1 << 20

COL_RNN_X = 0
COL_RNN_GATE = 4
COL_Q = 8
COL_K = 12
COL_V = 13
COL_ATTN_GATE = 14
COL_MERGE = 18

RNN_TILE = 256
RNN_CHUNK = 256
SMALL_ROWS = 152
MESH = pl.DeviceIdType.MESH


def _sds(shape, dtype):
    return jax.ShapeDtypeStruct(shape, dtype)


def _params(sem=None, vmem_mib=None):
    kw = {}
    if sem is not None:
        kw["dimension_semantics"] = sem
    if vmem_mib is not None:
        kw["vmem_limit_bytes"] = vmem_mib * MIB
    return pltpu.CompilerParams(**kw)


def _dot(a, b):
    return jnp.dot(a, b, preferred_element_type=F32)


def _dot_nt(a, b):
    return lax.dot_general(a, b, (((1,), (1,)), ((), ())), preferred_element_type=F32)


def _dot_tn(a, b):
    return lax.dot_general(a, b, (((0,), (0,)), ((), ())), preferred_element_type=F32)


def _sigmoid(x):
    return 1.0 / (1.0 + jnp.exp(-x))


def _softplus(x):
    return jnp.maximum(x, 0.0) + jnp.log(1.0 + jnp.exp(-jnp.abs(x)))


def _one_minus_exp(z):
    series = -z * (1.0 + z * 0.5 * (1.0 + z * (1.0 / 3.0) * (1.0 + z * 0.25 * (1.0 + z * 0.2 * (1.0 + z * (1.0 / 6.0))))))
    return jnp.where(z > -0.25, series, 1.0 - jnp.exp(z))


def _proj_fwd(x, g_pre, w_in_g):
    T = x.shape[0]
    tm = min(1024, T)

    def body(x_ref, g_ref, w_ref, proj_ref, h_ref):
        @pl.when(pl.program_id(1) == 0)
        def _():
            xv = x_ref[...]
            rstd = lax.rsqrt(jnp.mean(xv * xv, axis=-1, keepdims=True) + EPS)
            h_ref[...] = ((xv * rstd) * g_ref[...]).astype(BF16)

        proj_ref[...] = _dot(h_ref[...], w_ref[...])

    return pl.pallas_call(
        body,
        name="proj_fwd",
        grid=(T // tm, N_CHIPS),
        in_specs=[
            pl.BlockSpec((tm, D_MODEL), lambda i, j: (i, 0)),
            pl.BlockSpec((1, D_MODEL), lambda i, j: (0, 0)),
            pl.BlockSpec((None, D_MODEL, W_IN_SHARD), lambda i, j: (j, 0, 0)),
        ],
        out_specs=[
            pl.BlockSpec((tm, W_IN_SHARD), lambda i, j: (i, j)),
            pl.BlockSpec((tm, D_MODEL), lambda i, j: (i, 0)),
        ],
        out_shape=[_sds((T, D_IN), F32), _sds((T, D_MODEL), BF16)],
        compiler_params=_params(("parallel", "arbitrary"), 48),
    )(x, g_pre, w_in_g)


def _shift_down(x, tail, s, row):
    n = x.shape[0]
    xs = pltpu.roll(x, s, 0)
    tail_t = jnp.tile(pltpu.roll(tail, s, 0), (n // 8, 1))
    return jnp.where(row < s, tail_t, xs)


def _shift_up(x, head, s, row):
    n = x.shape[0]
    xs = pltpu.roll(x, n - s, 0)
    head_t = jnp.tile(pltpu.roll(head, 8 - s, 0), (n // 8, 1))
    return jnp.where(row >= n - s, head_t, xs)


def _conv_taps(x, tail, row):
    return [_shift_down(x, tail, 3, row), _shift_down(x, tail, 2, row), _shift_down(x, tail, 1, row), x]


def _rglru_gates(c, wa, wx, ba, bx, lam):
    cb = c.astype(BF16)
    r = _sigmoid(_dot(cb, wa) + ba)
    i = _sigmoid(_dot(cb, wx) + bx)
    log_a = (-LRU_C) * r * _softplus(-lam)
    a = jnp.exp(log_a)
    mult = jnp.sqrt(_one_minus_exp(2.0 * log_a))
    return cb, r, i, a, mult


def _scan_down(a, u, row):
    n = a.shape[0]
    s = 1
    while s < n:
        a_sh = jnp.where(row >= s, pltpu.roll(a, s, 0), 1.0)
        u_sh = jnp.where(row >= s, pltpu.roll(u, s, 0), 0.0)
        u = a * u_sh + u
        a = a * a_sh
        s *= 2
    return a, u


def _scan_up(b, u, row):
    n = b.shape[0]
    s = 1
    while s < n:
        b_sh = jnp.where(row < n - s, pltpu.roll(b, n - s, 0), 1.0)
        u_sh = jnp.where(row < n - s, pltpu.roll(u, n - s, 0), 0.0)
        u = b * u_sh + u
        b = b * b_sh
        s *= 2
    return b, u


def _rnn_fwd(proj, conv_w, conv_b, wa_bd, wx_bd, b_a, b_x, lam):
    T = proj.shape[0]
    tc, ct = RNN_CHUNK, RNN_TILE
    nt = T // tc

    def body(x_ref, rg_ref, cw_ref, cb_ref, wa_ref, wx_ref, ba_ref, bx_ref, lam_ref, h_ref, z_ref, xtail, hcarry):
        @pl.when(pl.program_id(1) == 0)
        def _():
            xtail[...] = jnp.zeros_like(xtail)
            hcarry[...] = jnp.zeros_like(hcarry)

        row = lax.broadcasted_iota(jnp.int32, (tc, ct), 0)
        x = x_ref[...]
        taps = _conv_taps(x, xtail[...], row)
        c = cb_ref[...] + cw_ref[pl.ds(0, 1), :] * taps[0]
        for k in range(1, CONV_W):
            c = c + cw_ref[pl.ds(k, 1), :] * taps[k]
        xtail[...] = x_ref[pl.ds(tc - 8, 8), :]
        _, _, i, a, mult = _rglru_gates(c, wa_ref[...], wx_ref[...], ba_ref[...], bx_ref[...], lam_ref[...])
        u = mult * (i * c)
        a_cum, h0 = _scan_down(a, u, row)
        h = h0 + a_cum * hcarry[...]
        h_ref[...] = h
        hcarry[...] = h_ref[pl.ds(tc - 1, 1), :]
        rg = rg_ref[...]
        z_ref[...] = (h * (rg * _sigmoid(rg))).astype(BF16)

    col = lambda off: (lambda j, t: (t, off + j))
    vec = pl.BlockSpec((1, ct), lambda j, t: (0, j))
    mat = pl.BlockSpec((None, ct, ct), lambda j, t: (j, 0, 0))
    return pl.pallas_call(
        body,
        name="rnn_fwd",
        grid=(D_RNN // ct, nt),
        in_specs=[
            pl.BlockSpec((tc, ct), col(COL_RNN_X)),
            pl.BlockSpec((tc, ct), col(COL_RNN_GATE)),
            pl.BlockSpec((CONV_W, ct), lambda j, t: (0, j)),
            vec, mat, mat, vec, vec, vec,
        ],
        out_specs=[pl.BlockSpec((tc, ct), lambda j, t: (t, j)), pl.BlockSpec((tc, ct), lambda j, t: (t, j))],
        out_shape=[_sds((T, D_RNN), F32), _sds((T, D_RNN), BF16)],
        scratch_shapes=[pltpu.VMEM((8, ct), F32), pltpu.VMEM((1, ct), F32)],
        compiler_params=_params(("parallel", "arbitrary"), 32),
    )(proj, proj, conv_w, conv_b, wa_bd, wx_bd, b_a, b_x, lam)


def _rnn_bwd(proj, y_rnn, dz_rnn, conv_w, conv_b, wa_bd, wx_bd, b_a, b_x, lam):
    T = proj.shape[0]
    tc, ct = RNN_CHUNK, RNN_TILE
    nt = T // tc
    hb = tc // 8

    def body(x_ref, xh_ref, rg_ref, h_ref, hh_ref, dz_ref, cw_ref, cb_ref, wa_ref, wx_ref, ba_ref, bx_ref, lam_ref,
             dx_ref, drg_ref, dwa_ref, dwx_ref, sm_ref, lam_carry, a_carry, dc_head):
        t = pl.program_id(1)
        first_chunk = t == nt - 1

        @pl.when(t == 0)
        def _():
            lam_carry[...] = jnp.zeros_like(lam_carry)
            a_carry[...] = jnp.zeros_like(a_carry)
            dc_head[...] = jnp.zeros_like(dc_head)
            dwa_ref[...] = jnp.zeros_like(dwa_ref)
            dwx_ref[...] = jnp.zeros_like(dwx_ref)
            sm_ref[...] = jnp.zeros_like(sm_ref)

        row = lax.broadcasted_iota(jnp.int32, (tc, ct), 0)
        keep = jnp.where(first_chunk, 0.0, 1.0)
        x = x_ref[...]
        xtail = xh_ref[...] * keep
        taps = _conv_taps(x, xtail, row)
        c = cb_ref[...] + cw_ref[pl.ds(0, 1), :] * taps[0]
        for k in range(1, CONV_W):
            c = c + cw_ref[pl.ds(k, 1), :] * taps[k]
        lam = lam_ref[...]
        cb, r, i, a, mult = _rglru_gates(c, wa_ref[...], wx_ref[...], ba_ref[...], bx_ref[...], lam)
        h = h_ref[...]
        h_prev = _shift_down(h, hh_ref[...] * keep, 1, row)
        rg = rg_ref[...]
        dz = dz_ref[...]
        sg = _sigmoid(rg)
        drg_ref[...] = (dz * h * (sg * (1.0 + rg * (1.0 - sg)))).astype(BF16)
        dy = dz * (rg * sg)
        b = jnp.where(row >= tc - 1, a_carry[pl.ds(0, 1), :], pltpu.roll(a, tc - 1, 0))
        b_cum, l0 = _scan_up(b, dy, row)
        lt = l0 + b_cum * lam_carry[pl.ds(0, 1), :]
        lam_carry[...] = lt[0:8, :]
        a_carry[...] = a[0:8, :]
        ic = i * c
        dmult = lt * ic
        di = lt * mult * c
        dc = lt * mult * i
        dlog_a = a * (lt * h_prev - dmult * a / mult)
        sp = _softplus(-lam)
        dpre_r = dlog_a * ((-LRU_C) * sp) * (r * (1.0 - r))
        dpre_i = di * (i * (1.0 - i))
        dlam_row = jnp.sum(dlog_a * r, axis=0, keepdims=True) * (LRU_C * _sigmoid(-lam))
        dpr_b = dpre_r.astype(BF16)
        dpi_b = dpre_i.astype(BF16)
        dwa_ref[...] += _dot_tn(cb, dpr_b)
        dwx_ref[...] += _dot_tn(cb, dpi_b)
        dc = dc + _dot_nt(dpr_b, wa_ref[...]) + _dot_nt(dpi_b, wx_ref[...])
        head = dc_head[...]
        dx = cw_ref[pl.ds(3, 1), :] * dc
        for m in range(1, CONV_W):
            dx = dx + cw_ref[pl.ds(3 - m, 1), :] * _shift_up(dc, head, m, row)
        dx_ref[...] = dx.astype(BF16)
        dc_head[...] = dc[0:8, :]
        sm_ref[pl.ds(0, 1), :] += jnp.sum(dpre_r, axis=0, keepdims=True)
        sm_ref[pl.ds(1, 1), :] += jnp.sum(dpre_i, axis=0, keepdims=True)
        sm_ref[pl.ds(2, 1), :] += dlam_row
        sm_ref[pl.ds(3, 1), :] += jnp.sum(dc, axis=0, keepdims=True)
        for k in range(CONV_W):
            sm_ref[pl.ds(4 + k, 1), :] += jnp.sum(dc * taps[k], axis=0, keepdims=True)

    rev = lambda off: (lambda j, t: (nt - 1 - t, off + j))
    halo = lambda off: (lambda j, t: (jnp.maximum((nt - 1 - t) * hb - 1, 0), off + j))
    vec = pl.BlockSpec((1, ct), lambda j, t: (0, j))
    mat = pl.BlockSpec((None, ct, ct), lambda j, t: (j, 0, 0))
    return pl.pallas_call(
        body,
        name="rnn_bwd",
        grid=(D_RNN // ct, nt),
        in_specs=[
            pl.BlockSpec((tc, ct), rev(COL_RNN_X)),
            pl.BlockSpec((8, ct), halo(COL_RNN_X)),
            pl.BlockSpec((tc, ct), rev(COL_RNN_GATE)),
            pl.BlockSpec((tc, ct), rev(0)),
            pl.BlockSpec((8, ct), halo(0)),
            pl.BlockSpec((tc, ct), rev(0)),
            pl.BlockSpec((CONV_W, ct), lambda j, t: (0, j)),
            vec, mat, mat, vec, vec, vec,
        ],
        out_specs=[
            pl.BlockSpec((tc, ct), rev(0)),
            pl.BlockSpec((tc, ct), rev(0)),
            mat, mat,
            pl.BlockSpec((8, ct), lambda j, t: (0, j)),
        ],
        out_shape=[_sds((T, D_RNN), BF16), _sds((T, D_RNN), BF16), _sds((D_RNN // ct, ct, ct), F32),
                   _sds((D_RNN // ct, ct, ct), F32), _sds((8, D_RNN), F32)],
        scratch_shapes=[pltpu.VMEM((8, ct), F32), pltpu.VMEM((8, ct), F32), pltpu.VMEM((8, ct), F32)],
        compiler_params=_params(("parallel", "arbitrary"), 32),
    )(proj, proj, proj, y_rnn, y_rnn, dz_rnn, conv_w, conv_b, wa_bd, wx_bd, b_a, b_x, lam)


def _alibi_slope(h):
    return 2.0 ** (-ALIBI_MAX_BIAS * (h + 1) / N_Q_HEADS)


def _attn_geometry(block_index):
    qi = lax.broadcasted_iota(jnp.int32, (BLOCK, BLOCK), 0)
    kj = lax.broadcasted_iota(jnp.int32, (BLOCK, BLOCK), 1)
    dist_cur = (qi - kj).astype(F32)
    dist_prev = dist_cur + float(BLOCK)
    mask_prev = kj > qi + jnp.where(block_index > 0, 0, BLOCK)
    mask_cur = kj <= qi
    return dist_prev, dist_cur, mask_prev, mask_cur


def _attn_probs(s_prev, s_cur, sink, slope, geo):
    dist_prev, dist_cur, mask_prev, mask_cur = geo
    s_prev = jnp.where(mask_prev, s_prev - slope * dist_prev, NEG_BIG)
    s_cur = jnp.where(mask_cur, s_cur - slope * dist_cur, NEG_BIG)
    m = jnp.maximum(jnp.maximum(jnp.max(s_prev, axis=-1, keepdims=True), jnp.max(s_cur, axis=-1, keepdims=True)), sink)
    p_prev = jnp.exp(s_prev - m)
    p_cur = jnp.exp(s_cur - m)
    p_sink = jnp.exp(sink - m)
    inv = 1.0 / (jnp.sum(p_prev, axis=-1, keepdims=True) + jnp.sum(p_cur, axis=-1, keepdims=True) + p_sink)
    return p_prev * inv, p_cur * inv, p_sink * inv


def _stack_heads(ref_or_val, hk, dtype):
    parts = [ref_or_val[:, (GROUP * hk + g) * HEAD_DIM:(GROUP * hk + g + 1) * HEAD_DIM] for g in range(GROUP)]
    return jnp.concatenate(parts, axis=0).astype(dtype)


def _attn_fwd(proj, sinks):
    T = proj.shape[0]
    nb = T // BLOCK
    scale = HEAD_DIM ** -0.5

    def body(sink_ref, q_ref, kp_ref, kc_ref, vp_ref, vc_ref, ag0_ref, ag1_ref, y_ref, z_ref):
        geo = _attn_geometry(pl.program_id(0))
        for hk in range(N_KV_HEADS):
            ks = slice(hk * HEAD_DIM, (hk + 1) * HEAD_DIM)
            qg = _stack_heads(q_ref, hk, BF16)
            kp = kp_ref[:, ks].astype(BF16)
            kc = kc_ref[:, ks].astype(BF16)
            s_prev = _dot_nt(qg, kp) * scale
            s_cur = _dot_nt(qg, kc) * scale
            pp, pc = [], []
            for g in range(GROUP):
                h = GROUP * hk + g
                rows = slice(g * BLOCK, (g + 1) * BLOCK)
                p_prev, p_cur, _ = _attn_probs(s_prev[rows], s_cur[rows], sink_ref[h], _alibi_slope(h), geo)
                pp.append(p_prev.astype(BF16))
                pc.append(p_cur.astype(BF16))
            og = _dot(jnp.concatenate(pp, axis=0), vp_ref[:, ks].astype(BF16)) + _dot(
                jnp.concatenate(pc, axis=0), vc_ref[:, ks].astype(BF16))
            for g in range(GROUP):
                h = GROUP * hk + g
                y_ref[:, h * HEAD_DIM:(h + 1) * HEAD_DIM] = og[g * BLOCK:(g + 1) * BLOCK]
        ag = jnp.concatenate([ag0_ref[...], ag1_ref[...]], axis=1)
        z_ref[...] = (y_ref[...] * (ag * _sigmoid(ag))).astype(BF16)

    prev = lambda c: (lambda i: (jnp.maximum(i - 1, 0), c))
    cur = lambda c: (lambda i: (i, c))
    return pl.pallas_call(
        body,
        name="attn_fwd",
        grid=(nb,),
        in_specs=[
            pl.BlockSpec(memory_space=pltpu.SMEM),
            pl.BlockSpec((BLOCK, 1024), lambda i: (i, COL_Q // 4)),
            pl.BlockSpec((BLOCK, D_KV), prev(COL_K)),
            pl.BlockSpec((BLOCK, D_KV), cur(COL_K)),
            pl.BlockSpec((BLOCK, D_KV), prev(COL_V)),
            pl.BlockSpec((BLOCK, D_KV), cur(COL_V)),
            pl.BlockSpec((BLOCK, 512), lambda i: (i, COL_ATTN_GATE // 2)),
            pl.BlockSpec((BLOCK, 512), lambda i: (i, COL_ATTN_GATE // 2 + 1)),
        ],
        out_specs=[pl.BlockSpec((BLOCK, 1024), lambda i: (i, 0)), pl.BlockSpec((BLOCK, 1024), lambda i: (i, 0))],
        out_shape=[_sds((T, 1024), F32), _sds((T, 1024), BF16)],
        compiler_params=_params(("arbitrary",), 32),
    )(sinks, proj, proj, proj, proj, proj, proj, proj)


def _attn_bwd(proj, y_attn, dz_attn, sinks):
    T = proj.shape[0]
    nb = T // BLOCK
    scale = HEAD_DIM ** -0.5

    def body(sink_ref, q_ref, kp_ref, kc_ref, vp_ref, vc_ref, ag0_ref, ag1_ref, y_ref, dz_ref,
             dq_ref, dk_ref, dv_ref, dag_ref, ds_ref, dy_s):
        i = pl.program_id(0)

        @pl.when(i == 0)
        def _():
            ds_ref[...] = jnp.zeros_like(ds_ref)

        geo = _attn_geometry(i)
        lane = lax.broadcasted_iota(jnp.int32, (8, 128), 1)
        sub = lax.broadcasted_iota(jnp.int32, (8, 128), 0)
        ag = jnp.concatenate([ag0_ref[...], ag1_ref[...]], axis=1)
        dz = dz_ref[...]
        sg = _sigmoid(ag)
        dag_ref[...] = (dz * y_ref[...] * (sg * (1.0 + ag * (1.0 - sg)))).astype(BF16)
        dy_s[...] = dz * (ag * sg)
        r_cur = pl.multiple_of(i * BLOCK, BLOCK)
        r_prev = pl.multiple_of(jnp.maximum(i - 1, 0) * BLOCK, BLOCK)
        dk_cur, dv_cur, dk_prev, dv_prev = [], [], [], []
        ds_acc = jnp.zeros((8, 128), F32)
        for hk in range(N_KV_HEADS):
            ks = slice(hk * HEAD_DIM, (hk + 1) * HEAD_DIM)
            qg = _stack_heads(q_ref, hk, BF16)
            dog = _stack_heads(dy_s, hk, F32)
            og = _stack_heads(y_ref, hk, F32)
            dog_b = dog.astype(BF16)
            kp = kp_ref[:, ks].astype(BF16)
            kc = kc_ref[:, ks].astype(BF16)
            vp = vp_ref[:, ks].astype(BF16)
            vc = vc_ref[:, ks].astype(BF16)
            s_prev = _dot_nt(qg, kp) * scale
            s_cur = _dot_nt(qg, kc) * scale
            dp_prev = _dot_nt(dog_b, vp)
            dp_cur = _dot_nt(dog_b, vc)
            dvec = jnp.sum(dog * og, axis=-1, keepdims=True)
            pp, pc, dsp, dsc = [], [], [], []
            for g in range(GROUP):
                h = GROUP * hk + g
                rows = slice(g * BLOCK, (g + 1) * BLOCK)
                p_prev, p_cur, p_sink = _attn_probs(s_prev[rows], s_cur[rows], sink_ref[h], _alibi_slope(h), geo)
                d_h = dvec[rows]
                pp.append(p_prev.astype(BF16))
                pc.append(p_cur.astype(BF16))
                dsp.append((p_prev * (dp_prev[rows] - d_h) * scale).astype(BF16))
                dsc.append((p_cur * (dp_cur[rows] - d_h) * scale).astype(BF16))
                dsink = -jnp.sum(p_sink * d_h, axis=0, keepdims=True)
                ds_acc = ds_acc + jnp.where(jnp.logical_and(lane == h, sub == 1), dsink, 0.0)
            pp = jnp.concatenate(pp, axis=0)
            pc = jnp.concatenate(pc, axis=0)
            dsp = jnp.concatenate(dsp, axis=0)
            dsc = jnp.concatenate(dsc, axis=0)
            dqg = _dot(dsp, kp) + _dot(dsc, kc)
            for g in range(GROUP):
                h = GROUP * hk + g
                dq_ref[:, h * HEAD_DIM:(h + 1) * HEAD_DIM] = dqg[g * BLOCK:(g + 1) * BLOCK].astype(BF16)
            dk_ref[pl.ds(r_cur, BLOCK), ks] = _dot_tn(dsc, qg)
            dv_ref[pl.ds(r_cur, BLOCK), ks] = _dot_tn(pc, dog_b)
            dk_prev.append(_dot_tn(dsp, qg))
            dv_prev.append(_dot_tn(pp, dog_b))
        ds_ref[:, 0:128] += ds_acc

        @pl.when(i > 0)
        def _():
            for hk in range(N_KV_HEADS):
                ks = slice(hk * HEAD_DIM, (hk + 1) * HEAD_DIM)
                dk_ref[pl.ds(r_prev, BLOCK), ks] += dk_prev[hk]
                dv_ref[pl.ds(r_prev, BLOCK), ks] += dv_prev[hk]

    prev = lambda c: (lambda i: (jnp.maximum(i - 1, 0), c))
    cur = lambda c: (lambda i: (i, c))
    blk = pl.BlockSpec((BLOCK, 1024), lambda i: (i, 0))
    whole = pl.BlockSpec((T, D_KV), lambda i: (0, 0))
    return pl.pallas_call(
        body,
        name="attn_bwd",
        grid=(nb,),
        in_specs=[
            pl.BlockSpec(memory_space=pltpu.SMEM),
            pl.BlockSpec((BLOCK, 1024), lambda i: (i, COL_Q // 4)),
            pl.BlockSpec((BLOCK, D_KV), prev(COL_K)),
            pl.BlockSpec((BLOCK, D_KV), cur(COL_K)),
            pl.BlockSpec((BLOCK, D_KV), prev(COL_V)),
            pl.BlockSpec((BLOCK, D_KV), cur(COL_V)),
            pl.BlockSpec((BLOCK, 512), lambda i: (i, COL_ATTN_GATE // 2)),
            pl.BlockSpec((BLOCK, 512), lambda i: (i, COL_ATTN_GATE // 2 + 1)),
            blk, blk,
        ],
        out_specs=[blk, whole, whole, blk, pl.BlockSpec((8, 1024), lambda i: (0, 0))],
        out_shape=[_sds((T, 1024), BF16), _sds((T, D_KV), F32), _sds((T, D_KV), F32), _sds((T, 1024), BF16),
                   _sds((8, 1024), F32)],
        scratch_shapes=[pltpu.VMEM((BLOCK, 1024), F32)],
        compiler_params=_params(("arbitrary",), 48),
    )(sinks, proj, proj, proj, proj, proj, proj, proj, y_attn, dz_attn)


def _head(x, target, z_rnn, z_attn, proj, b_gate, g_post, w_rnn_out, w_attn_out, w_out):
    T = x.shape[0]
    tm = 256

    def body(x_ref, t_ref, zr_ref, za_ref, ml0_ref, ml1_ref, ml2_ref, ml3_ref, bg_ref, gp_ref, wr_ref, wa_ref, wo_ref,
             dyx_ref, dzr_ref, dza_ref, dml_ref, mb_ref, dout_ref, dbr_ref, dba_ref, sm_ref):
        @pl.when(pl.program_id(0) == 0)
        def _():
            sm_ref[...] = jnp.zeros_like(sm_ref)

        wr, wa, wo = wr_ref[...], wa_ref[...], wo_ref[...]
        br_rnn = _dot(zr_ref[...], wr)
        br_attn = _dot(za_ref[...], wa)
        ml_rnn = jnp.concatenate([ml0_ref[...], ml1_ref[...]], axis=1)
        ml_attn = jnp.concatenate([ml2_ref[...], ml3_ref[...]], axis=1)
        g_rnn = _sigmoid(ml_rnn + bg_ref[:, 0:D_MODEL])
        g_attn = _sigmoid(ml_attn + bg_ref[:, D_MODEL:2 * D_MODEL])
        mb = (g_rnn * br_rnn + g_attn * br_attn).astype(BF16)
        mb_ref[...] = mb
        out = _dot(mb, wo)
        rstd = lax.rsqrt(jnp.mean(out * out, axis=-1, keepdims=True) + EPS)
        n = out * rstd
        gp = gp_ref[...]
        err = (x_ref[...] + n * gp) - t_ref[...]
        sm_ref[pl.ds(3, 1), :] += 0.5 * jnp.sum(jnp.mean(err * err, axis=-1, keepdims=True), axis=0, keepdims=True)
        dy = err * (1.0 / D_MODEL)
        dyx_ref[...] = dy
        sm_ref[pl.ds(0, 1), :] += jnp.sum(dy * n, axis=0, keepdims=True)
        dn = dy * gp
        dout = (rstd * (dn - n * jnp.mean(dn * n, axis=-1, keepdims=True))).astype(BF16)
        dout_ref[...] = dout
        dmerged = _dot_nt(dout, wo)
        dml_r = (dmerged * br_rnn) * (g_rnn * (1.0 - g_rnn))
        dml_a = (dmerged * br_attn) * (g_attn * (1.0 - g_attn))
        dml_ref[:, 0:D_MODEL] = dml_r.astype(BF16)
        dml_ref[:, D_MODEL:2 * D_MODEL] = dml_a.astype(BF16)
        sm_ref[pl.ds(1, 1), :] += jnp.sum(dml_r, axis=0, keepdims=True)
        sm_ref[pl.ds(2, 1), :] += jnp.sum(dml_a, axis=0, keepdims=True)
        dbr = (dmerged * g_rnn).astype(BF16)
        dba = (dmerged * g_attn).astype(BF16)
        dbr_ref[...] = dbr
        dba_ref[...] = dba
        dzr_ref[...] = _dot_nt(dbr, wr)
        dza_ref[...] = _dot_nt(dba, wa)

    tile = pl.BlockSpec((tm, D_MODEL), lambda i: (i, 0))
    wspec = pl.BlockSpec((D_MODEL, D_MODEL), lambda i: (0, 0))
    ml = lambda q: pl.BlockSpec((tm, 512), lambda i: (i, COL_MERGE // 2 + q))
    return pl.pallas_call(
        body,
        name="head",
        grid=(T // tm,),
        in_specs=[
            tile, tile, tile, tile,
            ml(0), ml(1), ml(2), ml(3),
            pl.BlockSpec((1, 2 * D_MODEL), lambda i: (0, 0)),
            pl.BlockSpec((1, D_MODEL), lambda i: (0, 0)),
            wspec, wspec, wspec,
        ],
        out_specs=[
            tile, tile, tile,
            pl.BlockSpec((tm, 2 * D_MODEL), lambda i: (i, 0)),
            tile, tile, tile, tile,
            pl.BlockSpec((8, D_MODEL), lambda i: (0, 0)),
        ],
        out_shape=[
            _sds((T, D_MODEL), F32), _sds((T, D_MODEL), F32), _sds((T, D_MODEL), F32),
            _sds((T, 2 * D_MODEL), BF16),
            _sds((T, D_MODEL), BF16), _sds((T, D_MODEL), BF16), _sds((T, D_MODEL), BF16), _sds((T, D_MODEL), BF16),
            _sds((8, D_MODEL), F32),
        ],
        compiler_params=_params(("arbitrary",), 56),
    )(x, target, z_rnn, z_attn, proj, proj, proj, proj, b_gate, g_post, w_rnn_out, w_attn_out, w_out)


def _matmul_tn(a, b, name, nblk, blocked):
    T, M = a.shape
    N = b.shape[1]
    tn = N // nblk
    tk = min(512, T)
    if blocked:
        out_spec, out_shape = pl.BlockSpec((None, M, tn), lambda n, t: (n, 0, 0)), _sds((nblk, M, tn), F32)
    else:
        out_spec, out_shape = pl.BlockSpec((M, tn), lambda n, t: (0, n)), _sds((M, N), F32)

    def body(a_ref, b_ref, o_ref):
        @pl.when(pl.program_id(1) == 0)
        def _():
            o_ref[...] = jnp.zeros_like(o_ref)

        o_ref[...] += _dot_tn(a_ref[...], b_ref[...])

    return pl.pallas_call(
        body,
        name=name,
        grid=(nblk, T // tk),
        in_specs=[pl.BlockSpec((tk, M), lambda n, t: (t, 0)), pl.BlockSpec((tk, tn), lambda n, t: (t, n))],
        out_specs=out_spec,
        out_shape=out_shape,
        compiler_params=_params(("parallel", "arbitrary"), 48),
    )(a, b)


def _dh_bwd(dproj, w_in_g, x, dyx, g_pre):
    T = x.shape[0]
    tm = min(512, T)

    def body(dp_ref, w_ref, x_ref, dyx_ref, g_ref, gx_ref, dg_ref, acc):
        i, k = pl.program_id(0), pl.program_id(1)

        @pl.when(jnp.logical_and(i == 0, k == 0))
        def _():
            dg_ref[...] = jnp.zeros_like(dg_ref)

        part = _dot_nt(dp_ref[...], w_ref[...])

        @pl.when(k == 0)
        def _():
            acc[...] = part

        @pl.when(k > 0)
        def _():
            acc[...] += part

        @pl.when(k == N_CHIPS - 1)
        def _():
            xv = x_ref[...]
            dh = acc[...]
            rstd = lax.rsqrt(jnp.mean(xv * xv, axis=-1, keepdims=True) + EPS)
            nx = xv * rstd
            dhg = dh * g_ref[...]
            gx_ref[...] = dyx_ref[...] + rstd * (dhg - nx * jnp.mean(dhg * nx, axis=-1, keepdims=True))
            dg_ref[pl.ds(0, 1), :] += jnp.sum(dh * nx, axis=0, keepdims=True)

    tile = pl.BlockSpec((tm, D_MODEL), lambda i, k: (i, 0))
    return pl.pallas_call(
        body,
        name="dh_bwd",
        grid=(T // tm, N_CHIPS),
        in_specs=[
            pl.BlockSpec((tm, W_IN_SHARD), lambda i, k: (i, k)),
            pl.BlockSpec((None, D_MODEL, W_IN_SHARD), lambda i, k: (k, 0, 0)),
            tile, tile,
            pl.BlockSpec((1, D_MODEL), lambda i, k: (0, 0)),
        ],
        out_specs=[tile, pl.BlockSpec((8, D_MODEL), lambda i, k: (0, 0))],
        out_shape=[_sds((T, D_MODEL), F32), _sds((8, D_MODEL), F32)],
        scratch_shapes=[pltpu.VMEM((tm, D_MODEL), F32)],
        compiler_params=_params(("arbitrary", "arbitrary"), 48),
    )(dproj, w_in_g, x, dyx, g_pre)


ELEMENTWISE_TILE_BYTES = MIB


def _row_tile(rows, cols):
    for t in (512, 256, 128, 64, 32, 16, 8):
        if rows % t == 0 and t * cols * 4 <= ELEMENTWISE_TILE_BYTES:
            return t
    return rows


def _sum_parts(parts, name):
    R, C = parts[0].shape
    tr = _row_tile(R, C)

    def body(*refs):
        o_ref = refs[-1]
        acc = refs[0][...]
        for r in refs[1:-1]:
            acc = acc + r[...]
        o_ref[...] = acc

    spec = pl.BlockSpec((tr, C), lambda i: (i, 0))
    return pl.pallas_call(
        body, name=name, grid=(R // tr,), in_specs=[spec] * len(parts), out_specs=spec,
        out_shape=_sds((R, C), F32), compiler_params=_params(("parallel",), 48),
    )(*parts)


def _adamw(w, g, m, v, name):
    R, C = w.shape
    tr = _row_tile(R, C)
    c1 = 1.0 - ADAM_B1 ** ADAM_STEP
    c2 = 1.0 - ADAM_B2 ** ADAM_STEP

    def body(w_ref, g_ref, m_ref, v_ref, d_ref, nm_ref, nv_ref):
        g = g_ref[...]
        nm = ADAM_B1 * m_ref[...] + (1.0 - ADAM_B1) * g
        nv = ADAM_B2 * v_ref[...] + (1.0 - ADAM_B2) * (g * g)
        nm_ref[...] = nm
        nv_ref[...] = nv
        d_ref[...] = (-ADAM_LR) * ((nm / c1) / (jnp.sqrt(nv / c2) + ADAM_EPS) + ADAM_WD * w_ref[...])

    spec = pl.BlockSpec((tr, C), lambda i: (i, 0))
    return pl.pallas_call(
        body, name=name, grid=(R // tr,), in_specs=[spec] * 4, out_specs=[spec] * 3,
        out_shape=[_sds((R, C), F32)] * 3, compiler_params=_params(("parallel",), 48),
    )(w, g, m, v)


def _place():
    return lax.axis_index("x"), lax.axis_index("y"), lax.axis_index("c")


def _chip_of(x, y, r):
    return (x ^ (r >> 1), y ^ (r & 1))


ANY = pl.BlockSpec(memory_space=pl.ANY)


def _gather_weights(win_b, wr_b, wa_b, wo_b, cw8):
    shards = [win_b, wr_b, wa_b, wo_b]
    nbig = len(shards)
    halves = [s.shape[0] // 2 for s in shards]

    def body(win_ref, wr_ref, wa_ref, wo_ref, cw_ref, gin_ref, gr_ref, ga_ref, go_ref, gcw_ref,
             send_sems, recv_sems, local_sems):
        x, y, c = _place()
        j = 2 * x + y
        srcs = [win_ref, wr_ref, wa_ref, wo_ref]
        dsts = [gin_ref, gr_ref, ga_ref, go_ref]
        locals_ = [pltpu.make_async_copy(srcs[a], dsts[a].at[j], local_sems.at[a]) for a in range(nbig)]
        locals_.append(pltpu.make_async_copy(cw_ref, gcw_ref.at[j], local_sems.at[nbig]))
        for cp in locals_:
            cp.start()

        def half(ref, a, chip_idx, core):
            return ref.at[chip_idx, pl.ds(pl.multiple_of(core * halves[a], 8), halves[a]), :]

        def ici(a, r):
            tx, ty = _chip_of(x, y, r)
            k = a * 3 + (r - 1)
            return pltpu.make_async_remote_copy(
                src_ref=srcs[a].at[pl.ds(pl.multiple_of(c * halves[a], 8), halves[a]), :],
                dst_ref=half(dsts[a], a, j, c), send_sem=send_sems.at[k], recv_sem=recv_sems.at[k],
                device_id=(tx, ty, c), device_id_type=MESH)

        def ici_landed(a, r):
            tx, ty = _chip_of(x, y, r)
            k = a * 3 + (r - 1)
            region = half(dsts[a], a, 2 * tx + ty, c)
            return pltpu.make_async_remote_copy(
                src_ref=region, dst_ref=region, send_sem=send_sems.at[k], recv_sem=recv_sems.at[k],
                device_id=(tx, ty, c), device_id_type=MESH)

        def d2d(a, r, core):
            tx, ty = _chip_of(x, y, r)
            k = 3 * nbig + 3 + a * 3 + (r - 1)
            region = half(dsts[a], a, 2 * tx + ty, core)
            return pltpu.make_async_remote_copy(
                src_ref=region, dst_ref=region, send_sem=send_sems.at[k], recv_sem=recv_sems.at[k],
                device_id=(x, y, 1 - c), device_id_type=MESH)

        def cw_copy(r):
            tx, ty = _chip_of(x, y, r)
            k = 3 * nbig + (r - 1)
            return pltpu.make_async_remote_copy(
                src_ref=cw_ref, dst_ref=gcw_ref.at[j], send_sem=send_sems.at[k], recv_sem=recv_sems.at[k],
                device_id=(tx, ty, c), device_id_type=MESH)

        def cw_landed(r):
            tx, ty = _chip_of(x, y, r)
            k = 3 * nbig + (r - 1)
            region = gcw_ref.at[2 * tx + ty]
            return pltpu.make_async_remote_copy(
                src_ref=region, dst_ref=region, send_sem=send_sems.at[k], recv_sem=recv_sems.at[k],
                device_id=(tx, ty, c), device_id_type=MESH)

        first = [ici(a, r) for a in range(nbig) for r in (1, 2, 3)] + [cw_copy(r) for r in (1, 2, 3)]
        for cp in first:
            cp.start()
        passed = []
        for a in range(nbig):
            for r in (1, 2, 3):
                ici_landed(a, r).wait_recv()
                cp = d2d(a, r, c)
                cp.start()
                passed.append(cp)
        for r in (1, 2, 3):
            cw_landed(r).wait_recv()
        for a in range(nbig):
            for r in (1, 2, 3):
                d2d(a, r, 1 - c).wait_recv()
        for cp in first + passed:
            cp.wait_send()
        for cp in locals_:
            cp.wait()

    nsem = 3 * nbig + 3 + 3 * nbig
    return pl.pallas_call(
        body,
        name="gather_weights",
        in_specs=[ANY] * 5,
        out_specs=[ANY] * 5,
        out_shape=[_sds((N_CHIPS,) + s.shape, s.dtype) for s in shards] + [_sds((N_CHIPS,) + cw8.shape, cw8.dtype)],
        scratch_shapes=[pltpu.SemaphoreType.DMA((nsem,)), pltpu.SemaphoreType.DMA((nsem,)),
                        pltpu.SemaphoreType.DMA((nbig + 1,))],
    )(win_b, wr_b, wa_b, wo_b, cw8)


def _pair_exchange(grads):
    n = len(grads)
    halves = [g.shape[1] // 2 for g in grads]

    def body(*refs):
        g_refs, own_refs, got_refs = refs[0:n], refs[n:2 * n], refs[2 * n:3 * n]
        send_sems, recv_sems, local_sems = refs[3 * n:]
        x, y, c = _place()
        copies = []
        for a in range(n):
            mine = g_refs[a].at[:, pl.ds(pl.multiple_of(c * halves[a], 8), halves[a]), :]
            other = g_refs[a].at[:, pl.ds(pl.multiple_of((1 - c) * halves[a], 8), halves[a]), :]
            loc = pltpu.make_async_copy(mine, own_refs[a], local_sems.at[a])
            rem = pltpu.make_async_remote_copy(
                src_ref=other, dst_ref=got_refs[a], send_sem=send_sems.at[a], recv_sem=recv_sems.at[a],
                device_id=(x, y, 1 - c), device_id_type=MESH)
            loc.start()
            rem.start()
            copies.append((loc, rem))
        for loc, rem in copies:
            rem.wait()
            loc.wait()

    half_shapes = [_sds((N_CHIPS, h, g.shape[2]), F32) for g, h in zip(grads, halves)]
    outs = pl.pallas_call(
        body,
        name="pair_exchange",
        in_specs=[ANY] * n,
        out_specs=[ANY] * (2 * n),
        out_shape=half_shapes + half_shapes,
        scratch_shapes=[pltpu.SemaphoreType.DMA((n,)), pltpu.SemaphoreType.DMA((n,)), pltpu.SemaphoreType.DMA((n,))],
    )(*grads)
    return outs[:n], outs[n:]


def _chip_exchange(psums):
    n = len(psums)

    def body(*refs):
        p_refs, own_refs, got_refs = refs[0:n], refs[n:2 * n], refs[2 * n:3 * n]
        send_sems, recv_sems, local_sems = refs[3 * n:]
        x, y, c = _place()
        j = 2 * x + y
        copies = []
        for a in range(n):
            loc = pltpu.make_async_copy(p_refs[a].at[j], own_refs[a], local_sems.at[a])
            loc.start()
            copies.append(loc)
            for r in (1, 2, 3):
                tx, ty = _chip_of(x, y, r)
                k = a * 3 + (r - 1)
                rem = pltpu.make_async_remote_copy(
                    src_ref=p_refs[a].at[2 * tx + ty], dst_ref=got_refs[a].at[r - 1],
                    send_sem=send_sems.at[k], recv_sem=recv_sems.at[k], device_id=(tx, ty, c), device_id_type=MESH)
                rem.start()
                copies.append(rem)
        for cp in copies:
            cp.wait()

    own_shapes = [_sds(p.shape[1:], F32) for p in psums]
    got_shapes = [_sds((3,) + p.shape[1:], F32) for p in psums]
    outs = pl.pallas_call(
        body,
        name="chip_exchange",
        in_specs=[ANY] * n,
        out_specs=[ANY] * (2 * n),
        out_shape=own_shapes + got_shapes,
        scratch_shapes=[pltpu.SemaphoreType.DMA((3 * n,)), pltpu.SemaphoreType.DMA((3 * n,)),
                        pltpu.SemaphoreType.DMA((n,))],
    )(*psums)
    return outs[:n], outs[n:]


def _pair_share(finals):
    n = len(finals)
    halves = [f.shape[0] for f in finals]

    def body(*refs):
        f_refs, full_refs = refs[0:n], refs[n:2 * n]
        send_sems, recv_sems, local_sems = refs[2 * n:]
        x, y, c = _place()
        copies = []
        for a in range(n):
            region = full_refs[a].at[pl.ds(pl.multiple_of(c * halves[a], 8), halves[a]), :]
            loc = pltpu.make_async_copy(f_refs[a], region, local_sems.at[a])
            rem = pltpu.make_async_remote_copy(
                src_ref=f_refs[a], dst_ref=region, send_sem=send_sems.at[a], recv_sem=recv_sems.at[a],
                device_id=(x, y, 1 - c), device_id_type=MESH)
            loc.start()
            rem.start()
            copies.append((loc, rem))
        for loc, rem in copies:
            rem.wait_send()
            loc.wait()
        for a in range(n):
            theirs = full_refs[a].at[pl.ds(pl.multiple_of((1 - c) * halves[a], 8), halves[a]), :]
            pltpu.make_async_remote_copy(
                src_ref=theirs, dst_ref=theirs, send_sem=send_sems.at[a], recv_sem=recv_sems.at[a],
                device_id=(x, y, 1 - c), device_id_type=MESH).wait_recv()

    return pl.pallas_call(
        body,
        name="pair_share",
        in_specs=[ANY] * n,
        out_specs=[ANY] * n,
        out_shape=[_sds((2 * f.shape[0], f.shape[1]), F32) for f in finals],
        scratch_shapes=[pltpu.SemaphoreType.DMA((n,)), pltpu.SemaphoreType.DMA((n,)), pltpu.SemaphoreType.DMA((n,))],
    )(*finals)


def _allreduce_small(s):
    R, C = s.shape

    def body(s_ref, o_ref, sib, chips, send_sems, recv_sems):
        x, y, c = _place()
        j = 2 * x + y
        to_sib = pltpu.make_async_remote_copy(
            src_ref=s_ref, dst_ref=sib, send_sem=send_sems.at[0], recv_sem=recv_sems.at[0],
            device_id=(x, y, 1 - c), device_id_type=MESH)
        to_sib.start()
        to_sib.wait()
        chips[j] = s_ref[...] + sib[...]
        sends = []
        for r in (1, 2, 3):
            tx, ty = _chip_of(x, y, r)
            cp = pltpu.make_async_remote_copy(
                src_ref=chips.at[j], dst_ref=chips.at[j], send_sem=send_sems.at[r], recv_sem=recv_sems.at[r],
                device_id=(tx, ty, c), device_id_type=MESH)
            cp.start()
            sends.append(cp)
        for r in (1, 2, 3):
            tx, ty = _chip_of(x, y, r)
            region = chips.at[2 * tx + ty]
            pltpu.make_async_remote_copy(
                src_ref=region, dst_ref=region, send_sem=send_sems.at[r], recv_sem=recv_sems.at[r],
                device_id=(tx, ty, c), device_id_type=MESH).wait_recv()
        for cp in sends:
            cp.wait_send()
        o_ref[...] = (chips[0] + chips[1]) + (chips[2] + chips[3])

    return pl.pallas_call(
        body,
        name="allreduce_small",
        in_specs=[pl.BlockSpec(memory_space=pltpu.VMEM)],
        out_specs=pl.BlockSpec(memory_space=pltpu.VMEM),
        out_shape=_sds((R, C), F32),
        scratch_shapes=[pltpu.VMEM((R, C), F32), pltpu.VMEM((N_CHIPS, R, C), F32),
                        pltpu.SemaphoreType.DMA((4,)), pltpu.SemaphoreType.DMA((4,))],
    )(s)


def _block_diag(w):
    w4 = w.reshape(4, 4, RNN_BLOCK_W, RNN_BLOCK_W)
    eye = jnp.eye(4, dtype=w.dtype)
    return jnp.einsum("jaik,ab->jaibk", w4, eye).reshape(4, RNN_TILE, RNN_TILE)


def _block_diag_part(d):
    d5 = d.reshape(4, 4, RNN_BLOCK_W, 4, RNN_BLOCK_W)
    return jnp.stack([d5[:, a, :, a, :] for a in range(4)], axis=1).reshape(RNN_BLOCKS, RNN_BLOCK_W, RNN_BLOCK_W)


def _local_grads(x, target, g_pre, w_in_g, b_gate, conv_w, conv_b, w_rg_a, b_rg_a, w_rg_x, b_rg_x, lam, sinks,
                 w_rnn_out, w_attn_out, w_out, g_post):
    wa_bd = _block_diag(w_rg_a).astype(BF16)
    wx_bd = _block_diag(w_rg_x).astype(BF16)
    b_a = b_rg_a.reshape(1, D_RNN)
    b_x = b_rg_x.reshape(1, D_RNN)

    proj, h = _proj_fwd(x, g_pre, w_in_g)
    y_rnn, z_rnn = _rnn_fwd(proj, conv_w, conv_b, wa_bd, wx_bd, b_a, b_x, lam)
    y_attn, z_attn = _attn_fwd(proj, sinks)
    dyx, dz_rnn, dz_attn, dml, merged, dout, dbr_rnn, dbr_attn, head_small = _head(
        x, target, z_rnn, z_attn, proj, b_gate, g_post, w_rnn_out, w_attn_out, w_out)
    dw_out = _matmul_tn(merged, dout, "dw_out", 2, False)
    dw_rnn_out = _matmul_tn(z_rnn, dbr_rnn, "dw_rnn_out", 2, False)
    dw_attn_out = _matmul_tn(z_attn, dbr_attn, "dw_attn_out", 2, False)
    dq, dk, dv, dag, attn_small = _attn_bwd(proj, y_attn, dz_attn, sinks)
    drx, drg, dwa_t, dwx_t, rnn_small = _rnn_bwd(proj, y_rnn, dz_rnn, conv_w, conv_b, wa_bd, wx_bd, b_a, b_x, lam)
    dproj = jnp.concatenate([drx, drg, dq, dk.astype(BF16), dv.astype(BF16), dag, dml], axis=1)
    grad_x, dh_small = _dh_bwd(dproj, w_in_g, x, dyx, g_pre)
    dw_in = _matmul_tn(h, dproj, "dw_in", N_CHIPS, True)
    shard_rows = lambda d: d.reshape(N_CHIPS, OUT_SHARD, D_MODEL)
    big = [dw_in, shard_rows(dw_rnn_out), shard_rows(dw_attn_out), shard_rows(dw_out)]
    small = jnp.concatenate([rnn_small, head_small, dh_small + attn_small,
                             _block_diag_part(dwa_t).reshape(64, 1024), _block_diag_part(dwx_t).reshape(64, 1024)], axis=0)
    return grad_x, big, small


ROW_LOSS = 11


def _rows8(parts):
    out = None
    for r, a in parts:
        p = jnp.pad(a, ((r, 8 - r - a.shape[0]), (0, 1024 - a.shape[1])))
        out = p if out is None else out + p
    return out


def _pack_small(p):
    g0 = _rows8([(0, p["b_rg_a"].reshape(1, 1024)), (1, p["b_rg_x"].reshape(1, 1024)), (2, p["lru_lambda"]),
                 (3, p["conv_b"]), (4, p["conv_w"][0])])
    g1 = _rows8([(0, p["post_norm_g"]), (1, p["b_gate"].reshape(2, 1024))])
    g2 = _rows8([(0, p["pre_norm_g"]), (1, p["attn_sinks"])])
    return jnp.concatenate([g0, g1, g2, p["w_rg_a"].reshape(64, 1024), p["w_rg_x"].reshape(64, 1024)], axis=0)


def _unpack_small(s, conv_cols):
    return {
        "b_rg_a": s[0:1].reshape(1, 16, 64), "b_rg_x": s[1:2].reshape(1, 16, 64), "lru_lambda": s[2:3],
        "conv_b": s[3:4], "conv_w": s[4:8, 0:conv_cols].reshape(1, CONV_W, conv_cols),
        "post_norm_g": s[8:9], "b_gate": s[9:11].reshape(1, 2048),
        "pre_norm_g": s[16:17], "attn_sinks": s[17:18, 0:N_Q_HEADS],
        "w_rg_a": s[24:88].reshape(1, 16, 64, 64), "w_rg_x": s[88:152].reshape(1, 16, 64, 64),
    }


WEIGHTS = ["pre_norm_g", "w_in", "b_gate", "conv_w", "conv_b", "w_rg_a", "b_rg_a", "w_rg_x", "b_rg_x", "lru_lambda",
           "attn_sinks", "w_rnn_out", "w_attn_out", "w_out", "post_norm_g"]
BIG = ["w_in", "w_rnn_out", "w_attn_out", "w_out"]


def kernel(x, pre_norm_g, w_in, b_gate, conv_w, conv_b, w_rg_a, b_rg_a, w_rg_x, b_rg_x, lru_lambda, attn_sinks, w_rnn_out, w_attn_out, w_out, post_norm_g, loss_target, m_pre_norm_g, m_w_in, m_b_gate, m_conv_w, m_conv_b, m_w_rg_a, m_b_rg_a, m_w_rg_x, m_b_rg_x, m_lru_lambda, m_attn_sinks, m_w_rnn_out, m_w_attn_out, m_w_out, m_post_norm_g, v_pre_norm_g, v_w_in, v_b_gate, v_conv_w, v_conv_b, v_w_rg_a, v_b_rg_a, v_w_rg_x, v_b_rg_x, v_lru_lambda, v_attn_sinks, v_w_rnn_out, v_w_attn_out, v_w_out, v_post_norm_g):
    w = dict(pre_norm_g=pre_norm_g, w_in=w_in, b_gate=b_gate, conv_w=conv_w, conv_b=conv_b, w_rg_a=w_rg_a,
             b_rg_a=b_rg_a, w_rg_x=w_rg_x, b_rg_x=b_rg_x, lru_lambda=lru_lambda, attn_sinks=attn_sinks,
             w_rnn_out=w_rnn_out, w_attn_out=w_attn_out, w_out=w_out, post_norm_g=post_norm_g)
    m = dict(pre_norm_g=m_pre_norm_g, w_in=m_w_in, b_gate=m_b_gate, conv_w=m_conv_w, conv_b=m_conv_b, w_rg_a=m_w_rg_a,
             b_rg_a=m_b_rg_a, w_rg_x=m_w_rg_x, b_rg_x=m_b_rg_x, lru_lambda=m_lru_lambda, attn_sinks=m_attn_sinks,
             w_rnn_out=m_w_rnn_out, w_attn_out=m_w_attn_out, w_out=m_w_out, post_norm_g=m_post_norm_g)
    v = dict(pre_norm_g=v_pre_norm_g, w_in=v_w_in, b_gate=v_b_gate, conv_w=v_conv_w, conv_b=v_conv_b, w_rg_a=v_w_rg_a,
             b_rg_a=v_b_rg_a, w_rg_x=v_w_rg_x, b_rg_x=v_b_rg_x, lru_lambda=v_lru_lambda, attn_sinks=v_attn_sinks,
             w_rnn_out=v_w_rnn_out, w_attn_out=v_w_attn_out, w_out=v_w_out, post_norm_g=v_post_norm_g)
    chip = 2 * lax.axis_index("x") + lax.axis_index("y")

    cw8 = jnp.pad(conv_w[0], ((0, 8 - CONV_W), (0, 0)))
    win_g, wr_g, wa_g, wo_g, cw_g = _gather_weights(
        w_in[0].astype(BF16), w_rnn_out[0].astype(BF16), w_attn_out[0].astype(BF16), w_out[0].astype(BF16), cw8)
    conv_w_full = jnp.transpose(cw_g[:, 0:CONV_W, :], (1, 0, 2)).reshape(CONV_W, D_RNN)

    grad_x, big, small = _local_grads(
        x[0], loss_target[0], pre_norm_g, win_g, b_gate, conv_w_full, conv_b, w_rg_a[0], b_rg_a[0], w_rg_x[0],
        b_rg_x[0], lru_lambda, attn_sinks[0], wr_g.reshape(D_MODEL, D_MODEL), wa_g.reshape(D_MODEL, D_MODEL),
        wo_g.reshape(D_MODEL, D_MODEL), post_norm_g)

    own, got = _pair_exchange(big)
    psums = []
    for a, (o, g) in enumerate(zip(own, got)):
        s = _sum_parts([o.reshape(-1, o.shape[2]), g.reshape(-1, g.shape[2])], "pair_sum_%d" % a)
        psums.append(s.reshape(o.shape))
    own2, got2 = _chip_exchange(psums)
    finals = [_sum_parts([o, g[0], g[1], g[2]], "chip_sum_%d" % a) for a, (o, g) in enumerate(zip(own2, got2))]
    gbig = dict(zip(BIG, _pair_share(finals)))

    small_sum = _allreduce_small(small)
    total_loss = small_sum[ROW_LOSS, 0]
    gsmall = _unpack_small(small_sum, D_RNN)
    conv_shard = D_RNN // N_CHIPS
    gsmall["conv_w"] = lax.dynamic_slice_in_dim(gsmall["conv_w"], chip * conv_shard, conv_shard, axis=2)

    grads, delta, new_m, new_v = {}, {}, {}, {}
    for n in BIG:
        grads[n] = gbig[n][None]
        d, nm, nv = _adamw(w[n][0], gbig[n], m[n][0], v[n][0], "adamw_" + n)
        delta[n], new_m[n], new_v[n] = d[None], nm[None], nv[None]
    pick = lambda t: {k: t[k] for k in gsmall}
    d, nm, nv = _adamw(_pack_small(pick(w)), _pack_small(gsmall), _pack_small(pick(m)), _pack_small(pick(v)),
                       "adamw_small")
    ud, um, uv = _unpack_small(d, conv_shard), _unpack_small(nm, conv_shard), _unpack_small(nv, conv_shard)
    for n in gsmall:
        grads[n] = gsmall[n].reshape(w[n].shape)
        delta[n] = ud[n].reshape(w[n].shape)
        new_m[n] = um[n].reshape(w[n].shape)
        new_v[n] = uv[n].reshape(w[n].shape)

    return (total_loss, grad_x[None], *[grads[n] for n in WEIGHTS], *[delta[n] for n in WEIGHTS],
            *[new_m[n] for n in WEIGHTS], *[new_v[n] for n in WEIGHTS])
```

```python
import functools
import math

import jax
import jax.numpy as jnp
from jax import lax
from jax.experimental import pallas as pl
from jax.experimental.pallas import tpu as pltpu

F32 = jnp.float32
BF16 = jnp.bfloat16

D_MODEL = 1024
D_RNN = 1024
RNN_BLOCKS = 16
RNN_BLOCK_W = 64
CONV_W = 4
LRU_C = 8.0
N_Q_HEADS = 16
N_KV_HEADS = 4
GROUP = 4
HEAD_DIM = 64
D_KV = 256
BLOCK = 128
ALIBI_MAX_BIAS = 8.0
EPS = 1e-6
D_IN = 6656
N_CHIPS = 4
W_IN_SHARD = D_IN // N_CHIPS
OUT_SHARD = D_MODEL // N_CHIPS
ADAM_LR = 0.001
ADAM_B1 = 0.9
ADAM_B2 = 0.999
ADAM_EPS = 1e-08
ADAM_WD = 0.01
ADAM_STEP = 10
NEG_BIG = -1e30
MIB = 1 << 20

COL_RNN_X = 0
COL_RNN_GATE = 4
COL_Q = 8
COL_K = 12
COL_V = 13
COL_ATTN_GATE = 14
COL_MERGE = 18

RNN_TILE = 256
RNN_CHUNK = 256
SMALL_ROWS = 152
MESH = pl.DeviceIdType.MESH


def _sds(shape, dtype):
    return jax.ShapeDtypeStruct(shape, dtype)


def _params(sem=None, vmem_mib=None):
    kw = {}
    if sem is not None:
        kw["dimension_semantics"] = sem
    if vmem_mib is not None:
        kw["vmem_limit_bytes"] = vmem_mib * MIB
    return pltpu.CompilerParams(**kw)


def _dot(a, b):
    return jnp.dot(a, b, preferred_element_type=F32)


def _dot_nt(a, b):
    return lax.dot_general(a, b, (((1,), (1,)), ((), ())), preferred_element_type=F32)


def _dot_tn(a, b):
    return lax.dot_general(a, b, (((0,), (0,)), ((), ())), preferred_element_type=F32)


def _sigmoid(x):
    return 1.0 / (1.0 + jnp.exp(-x))


def _softplus(x):
    return jnp.maximum(x, 0.0) + jnp.log(1.0 + jnp.exp(-jnp.abs(x)))


def _one_minus_exp(z):
    series = -z * (1.0 + z * 0.5 * (1.0 + z * (1.0 / 3.0) * (1.0 + z * 0.25 * (1.0 + z * 0.2 * (1.0 + z * (1.0 / 6.0))))))
    return jnp.where(z > -0.25, series, 1.0 - jnp.exp(z))


def _proj_fwd(x, g_pre, w_in_g):
    T = x.shape[0]
    tm = min(1024, T)

    def body(x_ref, g_ref, w_ref, proj_ref, h_ref):
        @pl.when(pl.program_id(1) == 0)
        def _():
            xv = x_ref[...]
            rstd = lax.rsqrt(jnp.mean(xv * xv, axis=-1, keepdims=True) + EPS)
            h_ref[...] = ((xv * rstd) * g_ref[...]).astype(BF16)

        proj_ref[...] = _dot(h_ref[...], w_ref[...])

    return pl.pallas_call(
        body,
        name="proj_fwd",
        grid=(T // tm, N_CHIPS),
        in_specs=[
            pl.BlockSpec((tm, D_MODEL), lambda i, j: (i, 0)),
            pl.BlockSpec((1, D_MODEL), lambda i, j: (0, 0)),
            pl.BlockSpec((None, D_MODEL, W_IN_SHARD), lambda i, j: (j, 0, 0)),
        ],
        out_specs=[
            pl.BlockSpec((tm, W_IN_SHARD), lambda i, j: (i, j)),
            pl.BlockSpec((tm, D_MODEL), lambda i, j: (i, 0)),
        ],
        out_shape=[_sds((T, D_IN), F32), _sds((T, D_MODEL), BF16)],
        compiler_params=_params(("parallel", "arbitrary"), 48),
    )(x, g_pre, w_in_g)


def _shift_down(x, tail, s, row):
    n = x.shape[0]
    xs = pltpu.roll(x, s, 0)
    tail_t = jnp.tile(pltpu.roll(tail, s, 0), (n // 8, 1))
    return jnp.where(row < s, tail_t, xs)


def _shift_up(x, head, s, row):
    n = x.shape[0]
    xs = pltpu.roll(x, n - s, 0)
    head_t = jnp.tile(pltpu.roll(head, 8 - s, 0), (n // 8, 1))
    return jnp.where(row >= n - s, head_t, xs)


def _conv_taps(x, tail, row):
    return [_shift_down(x, tail, 3, row), _shift_down(x, tail, 2, row), _shift_down(x, tail, 1, row), x]


def _rglru_gates(c, wa, wx, ba, bx, lam):
    cb = c.astype(BF16)
    r = _sigmoid(_dot(cb, wa) + ba)
    i = _sigmoid(_dot(cb, wx) + bx)
    log_a = (-LRU_C) * r * _softplus(-lam)
    a = jnp.exp(log_a)
    mult = jnp.sqrt(_one_minus_exp(2.0 * log_a))
    return cb, r, i, a, mult


def _scan_down(a, u, row):
    n = a.shape[0]
    s = 1
    while s < n:
        a_sh = jnp.where(row >= s, pltpu.roll(a, s, 0), 1.0)
        u_sh = jnp.where(row >= s, pltpu.roll(u, s, 0), 0.0)
        u = a * u_sh + u
        a = a * a_sh
        s *= 2
    return a, u


def _scan_up(b, u, row):
    n = b.shape[0]
    s = 1
    while s < n:
        b_sh = jnp.where(row < n - s, pltpu.roll(b, n - s, 0), 1.0)
        u_sh = jnp.where(row < n - s, pltpu.roll(u, n - s, 0), 0.0)
        u = b * u_sh + u
        b = b * b_sh
        s *= 2
    return b, u


def _rnn_fwd(proj, conv_w, conv_b, wa_bd, wx_bd, b_a, b_x, lam):
    T = proj.shape[0]
    tc, ct = RNN_CHUNK, RNN_TILE
    nt = T // tc

    def body(x_ref, rg_ref, cw_ref, cb_ref, wa_ref, wx_ref, ba_ref, bx_ref, lam_ref, h_ref, z_ref, xtail, hcarry):
        @pl.when(pl.program_id(1) == 0)
        def _():
            xtail[...] = jnp.zeros_like(xtail)
            hcarry[...] = jnp.zeros_like(hcarry)

        row = lax.broadcasted_iota(jnp.int32, (tc, ct), 0)
        x = x_ref[...]
        taps = _conv_taps(x, xtail[...], row)
        c = cb_ref[...] + cw_ref[pl.ds(0, 1), :] * taps[0]
        for k in range(1, CONV_W):
            c = c + cw_ref[pl.ds(k, 1), :] * taps[k]
        xtail[...] = x_ref[pl.ds(tc - 8, 8), :]
        _, _, i, a, mult = _rglru_gates(c, wa_ref[...], wx_ref[...], ba_ref[...], bx_ref[...], lam_ref[...])
        u = mult * (i * c)
        a_cum, h0 = _scan_down(a, u, row)
        h = h0 + a_cum * hcarry[...]
        h_ref[...] = h
        hcarry[...] = h_ref[pl.ds(tc - 1, 1), :]
        rg = rg_ref[...]
        z_ref[...] = (h * (rg * _sigmoid(rg))).astype(BF16)

    col = lambda off: (lambda j, t: (t, off + j))
    vec = pl.BlockSpec((1, ct), lambda j, t: (0, j))
    mat = pl.BlockSpec((None, ct, ct), lambda j, t: (j, 0, 0))
    return pl.pallas_call(
        body,
        name="rnn_fwd",
        grid=(D_RNN // ct, nt),
        in_specs=[
            pl.BlockSpec((tc, ct), col(COL_RNN_X)),
            pl.BlockSpec((tc, ct), col(COL_RNN_GATE)),
            pl.BlockSpec((CONV_W, ct), lambda j, t: (0, j)),
            vec, mat, mat, vec, vec, vec,
        ],
        out_specs=[pl.BlockSpec((tc, ct), lambda j, t: (t, j)), pl.BlockSpec((tc, ct), lambda j, t: (t, j))],
        out_shape=[_sds((T, D_RNN), F32), _sds((T, D_RNN), BF16)],
        scratch_shapes=[pltpu.VMEM((8, ct), F32), pltpu.VMEM((1, ct), F32)],
        compiler_params=_params(("parallel", "arbitrary"), 32),
    )(proj, proj, conv_w, conv_b, wa_bd, wx_bd, b_a, b_x, lam)


def _rnn_bwd(proj, y_rnn, dz_rnn, conv_w, conv_b, wa_bd, wx_bd, b_a, b_x, lam):
    T = proj.shape[0]
    tc, ct = RNN_CHUNK, RNN_TILE
    nt = T // tc
    hb = tc // 8

    def body(x_ref, xh_ref, rg_ref, h_ref, hh_ref, dz_ref, cw_ref, cb_ref, wa_ref, wx_ref, ba_ref, bx_ref, lam_ref,
             dx_ref, drg_ref, dwa_ref, dwx_ref, sm_ref, lam_carry, a_carry, dc_head):
        t = pl.program_id(1)
        first_chunk = t == nt - 1

        @pl.when(t == 0)
        def _():
            lam_carry[...] = jnp.zeros_like(lam_carry)
            a_carry[...] = jnp.zeros_like(a_carry)
            dc_head[...] = jnp.zeros_like(dc_head)
            dwa_ref[...] = jnp.zeros_like(dwa_ref)
            dwx_ref[...] = jnp.zeros_like(dwx_ref)
            sm_ref[...] = jnp.zeros_like(sm_ref)

        row = lax.broadcasted_iota(jnp.int32, (tc, ct), 0)
        keep = jnp.where(first_chunk, 0.0, 1.0)
        x = x_ref[...]
        xtail = xh_ref[...] * keep
        taps = _conv_taps(x, xtail, row)
        c = cb_ref[...] + cw_ref[pl.ds(0, 1), :] * taps[0]
        for k in range(1, CONV_W):
            c = c + cw_ref[pl.ds(k, 1), :] * taps[k]
        lam = lam_ref[...]
        cb, r, i, a, mult = _rglru_gates(c, wa_ref[...], wx_ref[...], ba_ref[...], bx_ref[...], lam)
        h = h_ref[...]
        h_prev = _shift_down(h, hh_ref[...] * keep, 1, row)
        rg = rg_ref[...]
        dz = dz_ref[...]
        sg = _sigmoid(rg)
        drg_ref[...] = (dz * h * (sg * (1.0 + rg * (1.0 - sg)))).astype(BF16)
        dy = dz * (rg * sg)
        b = jnp.where(row >= tc - 1, a_carry[pl.ds(0, 1), :], pltpu.roll(a, tc - 1, 0))
        b_cum, l0 = _scan_up(b, dy, row)
        lt = l0 + b_cum * lam_carry[pl.ds(0, 1), :]
        lam_carry[...] = lt[0:8, :]
        a_carry[...] = a[0:8, :]
        ic = i * c
        dmult = lt * ic
        di = lt * mult * c
        dc = lt * mult * i
        dlog_a = a * (lt * h_prev - dmult * a / mult)
        sp = _softplus(-lam)
        dpre_r = dlog_a * ((-LRU_C) * sp) * (r * (1.0 - r))
        dpre_i = di * (i * (1.0 - i))
        dlam_row = jnp.sum(dlog_a * r, axis=0, keepdims=True) * (LRU_C * _sigmoid(-lam))
        dpr_b = dpre_r.astype(BF16)
        dpi_b = dpre_i.astype(BF16)
        dwa_ref[...] += _dot_tn(cb, dpr_b)
        dwx_ref[...] += _dot_tn(cb, dpi_b)
        dc = dc + _dot_nt(dpr_b, wa_ref[...]) + _dot_nt(dpi_b, wx_ref[...])
        head = dc_head[...]
        dx = cw_ref[pl.ds(3, 1), :] * dc
        for m in range(1, CONV_W):
            dx = dx + cw_ref[pl.ds(3 - m, 1), :] * _shift_up(dc, head, m, row)
        dx_ref[...] = dx.astype(BF16)
        dc_head[...] = dc[0:8, :]
        sm_ref[pl.ds(0, 1), :] += jnp.sum(dpre_r, axis=0, keepdims=True)
        sm_ref[pl.ds(1, 1), :] += jnp.sum(dpre_i, axis=0, keepdims=True)
        sm_ref[pl.ds(2, 1), :] += dlam_row
        sm_ref[pl.ds(3, 1), :] += jnp.sum(dc, axis=0, keepdims=True)
        for k in range(CONV_W):
            sm_ref[pl.ds(4 + k, 1), :] += jnp.sum(dc * taps[k], axis=0, keepdims=True)

    rev = lambda off: (lambda j, t: (nt - 1 - t, off + j))
    halo = lambda off: (lambda j, t: (jnp.maximum((nt - 1 - t) * hb - 1, 0), off + j))
    vec = pl.BlockSpec((1, ct), lambda j, t: (0, j))
    mat = pl.BlockSpec((None, ct, ct), lambda j, t: (j, 0, 0))
    return pl.pallas_call(
        body,
        name="rnn_bwd",
        grid=(D_RNN // ct, nt),
        in_specs=[
            pl.BlockSpec((tc, ct), rev(COL_RNN_X)),
            pl.BlockSpec((8, ct), halo(COL_RNN_X)),
            pl.BlockSpec((tc, ct), rev(COL_RNN_GATE)),
            pl.BlockSpec((tc, ct), rev(0)),
            pl.BlockSpec((8, ct), halo(0)),
            pl.BlockSpec((tc, ct), rev(0)),
            pl.BlockSpec((CONV_W, ct), lambda j, t: (0, j)),
            vec, mat, mat, vec, vec, vec,
        ],
        out_specs=[
            pl.BlockSpec((tc, ct), rev(0)),
            pl.BlockSpec((tc, ct), rev(0)),
            mat, mat,
            pl.BlockSpec((8, ct), lambda j, t: (0, j)),
        ],
        out_shape=[_sds((T, D_RNN), BF16), _sds((T, D_RNN), BF16), _sds((D_RNN // ct, ct, ct), F32),
                   _sds((D_RNN // ct, ct, ct), F32), _sds((8, D_RNN), F32)],
        scratch_shapes=[pltpu.VMEM((8, ct), F32), pltpu.VMEM((8, ct), F32), pltpu.VMEM((8, ct), F32)],
        compiler_params=_params(("parallel", "arbitrary"), 32),
    )(proj, proj, proj, y_rnn, y_rnn, dz_rnn, conv_w, conv_b, wa_bd, wx_bd, b_a, b_x, lam)


def _alibi_slope(h):
    return 2.0 ** (-ALIBI_MAX_BIAS * (h + 1) / N_Q_HEADS)


def _attn_geometry(block_index):
    qi = lax.broadcasted_iota(jnp.int32, (BLOCK, BLOCK), 0)
    kj = lax.broadcasted_iota(jnp.int32, (BLOCK, BLOCK), 1)
    dist_cur = (qi - kj).astype(F32)
    dist_prev = dist_cur + float(BLOCK)
    mask_prev = kj > qi + jnp.where(block_index > 0, 0, BLOCK)
    mask_cur = kj <= qi
    return dist_prev, dist_cur, mask_prev, mask_cur


def _attn_probs(s_prev, s_cur, sink, slope, geo):
    dist_prev, dist_cur, mask_prev, mask_cur = geo
    s_prev = jnp.where(mask_prev, s_prev - slope * dist_prev, NEG_BIG)
    s_cur = jnp.where(mask_cur, s_cur - slope * dist_cur, NEG_BIG)
    m = jnp.maximum(jnp.maximum(jnp.max(s_prev, axis=-1, keepdims=True), jnp.max(s_cur, axis=-1, keepdims=True)), sink)
    p_prev = jnp.exp(s_prev - m)
    p_cur = jnp.exp(s_cur - m)
    p_sink = jnp.exp(sink - m)
    inv = 1.0 / (jnp.sum(p_prev, axis=-1, keepdims=True) + jnp.sum(p_cur, axis=-1, keepdims=True) + p_sink)
    return p_prev * inv, p_cur * inv, p_sink * inv


def _stack_heads(ref_or_val, hk, dtype):
    parts = [ref_or_val[:, (GROUP * hk + g) * HEAD_DIM:(GROUP * hk + g + 1) * HEAD_DIM] for g in range(GROUP)]
    return jnp.concatenate(parts, axis=0).astype(dtype)


def _attn_fwd(proj, sinks):
    T = proj.shape[0]
    nb = T // BLOCK
    scale = HEAD_DIM ** -0.5

    def body(sink_ref, q_ref, kp_ref, kc_ref, vp_ref, vc_ref, ag0_ref, ag1_ref, y_ref, z_ref):
        geo = _attn_geometry(pl.program_id(0))
        for hk in range(N_KV_HEADS):
            ks = slice(hk * HEAD_DIM, (hk + 1) * HEAD_DIM)
            qg = _stack_heads(q_ref, hk, BF16)
            kp = kp_ref[:, ks].astype(BF16)
            kc = kc_ref[:, ks].astype(BF16)
            s_prev = _dot_nt(qg, kp) * scale
            s_cur = _dot_nt(qg, kc) * scale
            pp, pc = [], []
            for g in range(GROUP):
                h = GROUP * hk + g
                rows = slice(g * BLOCK, (g + 1) * BLOCK)
                p_prev, p_cur, _ = _attn_probs(s_prev[rows], s_cur[rows], sink_ref[h], _alibi_slope(h), geo)
                pp.append(p_prev.astype(BF16))
                pc.append(p_cur.astype(BF16))
            og = _dot(jnp.concatenate(pp, axis=0), vp_ref[:, ks].astype(BF16)) + _dot(
                jnp.concatenate(pc, axis=0), vc_ref[:, ks].astype(BF16))
            for g in range(GROUP):
                h = GROUP * hk + g
                y_ref[:, h * HEAD_DIM:(h + 1) * HEAD_DIM] = og[g * BLOCK:(g + 1) * BLOCK]
        ag = jnp.concatenate([ag0_ref[...], ag1_ref[...]], axis=1)
        z_ref[...] = (y_ref[...] * (ag * _sigmoid(ag))).astype(BF16)

    prev = lambda c: (lambda i: (jnp.maximum(i - 1, 0), c))
    cur = lambda c: (lambda i: (i, c))
    return pl.pallas_call(
        body,
        name="attn_fwd",
        grid=(nb,),
        in_specs=[
            pl.BlockSpec(memory_space=pltpu.SMEM),
            pl.BlockSpec((BLOCK, 1024), lambda i: (i, COL_Q // 4)),
            pl.BlockSpec((BLOCK, D_KV), prev(COL_K)),
            pl.BlockSpec((BLOCK, D_KV), cur(COL_K)),
            pl.BlockSpec((BLOCK, D_KV), prev(COL_V)),
            pl.BlockSpec((BLOCK, D_KV), cur(COL_V)),
            pl.BlockSpec((BLOCK, 512), lambda i: (i, COL_ATTN_GATE // 2)),
            pl.BlockSpec((BLOCK, 512), lambda i: (i, COL_ATTN_GATE // 2 + 1)),
        ],
        out_specs=[pl.BlockSpec((BLOCK, 1024), lambda i: (i, 0)), pl.BlockSpec((BLOCK, 1024), lambda i: (i, 0))],
        out_shape=[_sds((T, 1024), F32), _sds((T, 1024), BF16)],
        compiler_params=_params(("arbitrary",), 32),
    )(sinks, proj, proj, proj, proj, proj, proj, proj)


def _attn_bwd(proj, y_attn, dz_attn, sinks):
    T = proj.shape[0]
    nb = T // BLOCK
    scale = HEAD_DIM ** -0.5

    def body(sink_ref, q_ref, kp_ref, kc_ref, vp_ref, vc_ref, ag0_ref, ag1_ref, y_ref, dz_ref,
             dq_ref, dk_ref, dv_ref, dag_ref, ds_ref, dy_s):
        i = pl.program_id(0)

        @pl.when(i == 0)
        def _():
            ds_ref[...] = jnp.zeros_like(ds_ref)

        geo = _attn_geometry(i)
        lane = lax.broadcasted_iota(jnp.int32, (8, 128), 1)
        sub = lax.broadcasted_iota(jnp.int32, (8, 128), 0)
        ag = jnp.concatenate([ag0_ref[...], ag1_ref[...]], axis=1)
        dz = dz_ref[...]
        sg = _sigmoid(ag)
        dag_ref[...] = (dz * y_ref[...] * (sg * (1.0 + ag * (1.0 - sg)))).astype(BF16)
        dy_s[...] = dz * (ag * sg)
        r_cur = pl.multiple_of(i * BLOCK, BLOCK)
        r_prev = pl.multiple_of(jnp.maximum(i - 1, 0) * BLOCK, BLOCK)
        dk_cur, dv_cur, dk_prev, dv_prev = [], [], [], []
        ds_acc = jnp.zeros((8, 128), F32)
        for hk in range(N_KV_HEADS):
            ks = slice(hk * HEAD_DIM, (hk + 1) * HEAD_DIM)
            qg = _stack_heads(q_ref, hk, BF16)
            dog = _stack_heads(dy_s, hk, F32)
            og = _stack_heads(y_ref, hk, F32)
            dog_b = dog.astype(BF16)
            kp = kp_ref[:, ks].astype(BF16)
            kc = kc_ref[:, ks].astype(BF16)
            vp = vp_ref[:, ks].astype(BF16)
            vc = vc_ref[:, ks].astype(BF16)
            s_prev = _dot_nt(qg, kp) * scale
            s_cur = _dot_nt(qg, kc) * scale
            dp_prev = _dot_nt(dog_b, vp)
            dp_cur = _dot_nt(dog_b, vc)
            dvec = jnp.sum(dog * og, axis=-1, keepdims=True)
            pp, pc, dsp, dsc = [], [], [], []
            for g in range(GROUP):
                h = GROUP * hk + g
                rows = slice(g * BLOCK, (g + 1) * BLOCK)
                p_prev, p_cur, p_sink = _attn_probs(s_prev[rows], s_cur[rows], sink_ref[h], _alibi_slope(h), geo)
                d_h = dvec[rows]
                pp.append(p_prev.astype(BF16))
                pc.append(p_cur.astype(BF16))
                dsp.append((p_prev * (dp_prev[rows] - d_h) * scale).astype(BF16))
                dsc.append((p_cur * (dp_cur[rows] - d_h) * scale).astype(BF16))
                dsink = -jnp.sum(p_sink * d_h, axis=0, keepdims=True)
                ds_acc = ds_acc + jnp.where(jnp.logical_and(lane == h, sub == 1), dsink, 0.0)
            pp = jnp.concatenate(pp, axis=0)
            pc = jnp.concatenate(pc, axis=0)
            dsp = jnp.concatenate(dsp, axis=0)
            dsc = jnp.concatenate(dsc, axis=0)
            dqg = _dot(dsp, kp) + _dot(dsc, kc)
            for g in range(GROUP):
                h = GROUP * hk + g
                dq_ref[:, h * HEAD_DIM:(h + 1) * HEAD_DIM] = dqg[g * BLOCK:(g + 1) * BLOCK].astype(BF16)
            dk_ref[pl.ds(r_cur, BLOCK), ks] = _dot_tn(dsc, qg)
            dv_ref[pl.ds(r_cur, BLOCK), ks] = _dot_tn(pc, dog_b)
            dk_prev.append(_dot_tn(dsp, qg))
            dv_prev.append(_dot_tn(pp, dog_b))
        ds_ref[:, 0:128] += ds_acc

        @pl.when(i > 0)
        def _():
            for hk in range(N_KV_HEADS):
                ks = slice(hk * HEAD_DIM, (hk + 1) * HEAD_DIM)
                dk_ref[pl.ds(r_prev, BLOCK), ks] += dk_prev[hk]
                dv_ref[pl.ds(r_prev, BLOCK), ks] += dv_prev[hk]

    prev = lambda c: (lambda i: (jnp.maximum(i - 1, 0), c))
    cur = lambda c: (lambda i: (i, c))
    blk = pl.BlockSpec((BLOCK, 1024), lambda i: (i, 0))
    whole = pl.BlockSpec((T, D_KV), lambda i: (0, 0))
    return pl.pallas_call(
        body,
        name="attn_bwd",
        grid=(nb,),
        in_specs=[
            pl.BlockSpec(memory_space=pltpu.SMEM),
            pl.BlockSpec((BLOCK, 1024), lambda i: (i, COL_Q // 4)),
            pl.BlockSpec((BLOCK, D_KV), prev(COL_K)),
            pl.BlockSpec((BLOCK, D_KV), cur(COL_K)),
            pl.BlockSpec((BLOCK, D_KV), prev(COL_V)),
            pl.BlockSpec((BLOCK, D_KV), cur(COL_V)),
            pl.BlockSpec((BLOCK, 512), lambda i: (i, COL_ATTN_GATE // 2)),
            pl.BlockSpec((BLOCK, 512), lambda i: (i, COL_ATTN_GATE // 2 + 1)),
            blk, blk,
        ],
        out_specs=[blk, whole, whole, blk, pl.BlockSpec((8, 1024), lambda i: (0, 0))],
        out_shape=[_sds((T, 1024), BF16), _sds((T, D_KV), F32), _sds((T, D_KV), F32), _sds((T, 1024), BF16),
                   _sds((8, 1024), F32)],
        scratch_shapes=[pltpu.VMEM((BLOCK, 1024), F32)],
        compiler_params=_params(("arbitrary",), 48),
    )(sinks, proj, proj, proj, proj, proj, proj, proj, y_attn, dz_attn)


def _head(x, target, z_rnn, z_attn, proj, b_gate, g_post, w_rnn_out, w_attn_out, w_out):
    T = x.shape[0]
    tm = 256

    def body(x_ref, t_ref, zr_ref, za_ref, ml0_ref, ml1_ref, ml2_ref, ml3_ref, bg_ref, gp_ref, wr_ref, wa_ref, wo_ref,
             dyx_ref, dzr_ref, dza_ref, dml_ref, mb_ref, dout_ref, dbr_ref, dba_ref, sm_ref):
        @pl.when(pl.program_id(0) == 0)
        def _():
            sm_ref[...] = jnp.zeros_like(sm_ref)

        wr, wa, wo = wr_ref[...], wa_ref[...], wo_ref[...]
        br_rnn = _dot(zr_ref[...], wr)
        br_attn = _dot(za_ref[...], wa)
        ml_rnn = jnp.concatenate([ml0_ref[...], ml1_ref[...]], axis=1)
        ml_attn = jnp.concatenate([ml2_ref[...], ml3_ref[...]], axis=1)
        g_rnn = _sigmoid(ml_rnn + bg_ref[:, 0:D_MODEL])
        g_attn = _sigmoid(ml_attn + bg_ref[:, D_MODEL:2 * D_MODEL])
        mb = (g_rnn * br_rnn + g_attn * br_attn).astype(BF16)
        mb_ref[...] = mb
        out = _dot(mb, wo)
        rstd = lax.rsqrt(jnp.mean(out * out, axis=-1, keepdims=True) + EPS)
        n = out * rstd
        gp = gp_ref[...]
        err = (x_ref[...] + n * gp) - t_ref[...]
        sm_ref[pl.ds(3, 1), :] += 0.5 * jnp.sum(jnp.mean(err * err, axis=-1, keepdims=True), axis=0, keepdims=True)
        dy = err * (1.0 / D_MODEL)
        dyx_ref[...] = dy
        sm_ref[pl.ds(0, 1), :] += jnp.sum(dy * n, axis=0, keepdims=True)
        dn = dy * gp
        dout = (rstd * (dn - n * jnp.mean(dn * n, axis=-1, keepdims=True))).astype(BF16)
        dout_ref[...] = dout
        dmerged = _dot_nt(dout, wo)
        dml_r = (dmerged * br_rnn) * (g_rnn * (1.0 - g_rnn))
        dml_a = (dmerged * br_attn) * (g_attn * (1.0 - g_attn))
        dml_ref[:, 0:D_MODEL] = dml_r.astype(BF16)
        dml_ref[:, D_MODEL:2 * D_MODEL] = dml_a.astype(BF16)
        sm_ref[pl.ds(1, 1), :] += jnp.sum(dml_r, axis=0, keepdims=True)
        sm_ref[pl.ds(2, 1), :] += jnp.sum(dml_a, axis=0, keepdims=True)
        dbr = (dmerged * g_rnn).astype(BF16)
        dba = (dmerged * g_attn).astype(BF16)
        dbr_ref[...] = dbr
        dba_ref[...] = dba
        dzr_ref[...] = _dot_nt(dbr, wr)
        dza_ref[...] = _dot_nt(dba, wa)

    tile = pl.BlockSpec((tm, D_MODEL), lambda i: (i, 0))
    wspec = pl.BlockSpec((D_MODEL, D_MODEL), lambda i: (0, 0))
    ml = lambda q: pl.BlockSpec((tm, 512), lambda i: (i, COL_MERGE // 2 + q))
    return pl.pallas_call(
        body,
        name="head",
        grid=(T // tm,),
        in_specs=[
            tile, tile, tile, tile,
            ml(0), ml(1), ml(2), ml(3),
            pl.BlockSpec((1, 2 * D_MODEL), lambda i: (0, 0)),
            pl.BlockSpec((1, D_MODEL), lambda i: (0, 0)),
            wspec, wspec, wspec,
        ],
        out_specs=[
            tile, tile, tile,
            pl.BlockSpec((tm, 2 * D_MODEL), lambda i: (i, 0)),
            tile, tile, tile, tile,
            pl.BlockSpec((8, D_MODEL), lambda i: (0, 0)),
        ],
        out_shape=[
            _sds((T, D_MODEL), F32), _sds((T, D_MODEL), F32), _sds((T, D_MODEL), F32),
            _sds((T, 2 * D_MODEL), BF16),
            _sds((T, D_MODEL), BF16), _sds((T, D_MODEL), BF16), _sds((T, D_MODEL), BF16), _sds((T, D_MODEL), BF16),
            _sds((8, D_MODEL), F32),
        ],
        compiler_params=_params(("arbitrary",), 56),
    )(x, target, z_rnn, z_attn, proj, proj, proj, proj, b_gate, g_post, w_rnn_out, w_attn_out, w_out)


def _matmul_tn(a, b, name, nblk, blocked):
    T, M = a.shape
    N = b.shape[1]
    tn = N // nblk
    tk = min(512, T)
    if blocked:
        out_spec, out_shape = pl.BlockSpec((None, M, tn), lambda n, t: (n, 0, 0)), _sds((nblk, M, tn), F32)
    else:
        out_spec, out_shape = pl.BlockSpec((M, tn), lambda n, t: (0, n)), _sds((M, N), F32)

    def body(a_ref, b_ref, o_ref):
        @pl.when(pl.program_id(1) == 0)
        def _():
            o_ref[...] = jnp.zeros_like(o_ref)

        o_ref[...] += _dot_tn(a_ref[...], b_ref[...])

    return pl.pallas_call(
        body,
        name=name,
        grid=(nblk, T // tk),
        in_specs=[pl.BlockSpec((tk, M), lambda n, t: (t, 0)), pl.BlockSpec((tk, tn), lambda n, t: (t, n))],
        out_specs=out_spec,
        out_shape=out_shape,
        compiler_params=_params(("parallel", "arbitrary"), 48),
    )(a, b)


def _dh_bwd(dproj, w_in_g, x, dyx, g_pre):
    T = x.shape[0]
    tm = min(512, T)

    def body(dp_ref, w_ref, x_ref, dyx_ref, g_ref, gx_ref, dg_ref, acc):
        i, k = pl.program_id(0), pl.program_id(1)

        @pl.when(jnp.logical_and(i == 0, k == 0))
        def _():
            dg_ref[...] = jnp.zeros_like(dg_ref)

        part = _dot_nt(dp_ref[...], w_ref[...])

        @pl.when(k == 0)
        def _():
            acc[...] = part

        @pl.when(k > 0)
        def _():
            acc[...] += part

        @pl.when(k == N_CHIPS - 1)
        def _():
            xv = x_ref[...]
            dh = acc[...]
            rstd = lax.rsqrt(jnp.mean(xv * xv, axis=-1, keepdims=True) + EPS)
            nx = xv * rstd
            dhg = dh * g_ref[...]
            gx_ref[...] = dyx_ref[...] + rstd * (dhg - nx * jnp.mean(dhg * nx, axis=-1, keepdims=True))
            dg_ref[pl.ds(0, 1), :] += jnp.sum(dh * nx, axis=0, keepdims=True)

    tile = pl.BlockSpec((tm, D_MODEL), lambda i, k: (i, 0))
    return pl.pallas_call(
        body,
        name="dh_bwd",
        grid=(T // tm, N_CHIPS),
        in_specs=[
            pl.BlockSpec((tm, W_IN_SHARD), lambda i, k: (i, k)),
            pl.BlockSpec((None, D_MODEL, W_IN_SHARD), lambda i, k: (k, 0, 0)),
            tile, tile,
            pl.BlockSpec((1, D_MODEL), lambda i, k: (0, 0)),
        ],
        out_specs=[tile, pl.BlockSpec((8, D_MODEL), lambda i, k: (0, 0))],
        out_shape=[_sds((T, D_MODEL), F32), _sds((8, D_MODEL), F32)],
        scratch_shapes=[pltpu.VMEM((tm, D_MODEL), F32)],
        compiler_params=_params(("arbitrary", "arbitrary"), 48),
    )(dproj, w_in_g, x, dyx, g_pre)


ELEMENTWISE_TILE_BYTES = MIB


def _row_tile(rows, cols):
    for t in (512, 256, 128, 64, 32, 16, 8):
        if rows % t == 0 and t * cols * 4 <= ELEMENTWISE_TILE_BYTES:
            return t
    return rows


def _pair_sum(g, got, core, name):
    nch, R, C = g.shape
    h = R // 2
    tr = _row_tile(h, C)
    nt = h // tr

    def body(c_ref, g_ref, got_ref, p_ref, pb_ref):
        s = g_ref[...] + got_ref[...]
        p_ref[...] = s
        pb_ref[...] = s.astype(BF16)

    blk = pl.BlockSpec((None, tr, C), lambda j, i, c_ref: (j, i, 0))
    return pl.pallas_call(
        body,
        name=name,
        grid_spec=pltpu.PrefetchScalarGridSpec(
            num_scalar_prefetch=1,
            grid=(nch, nt),
            in_specs=[pl.BlockSpec((None, tr, C), lambda j, i, c_ref: (j, c_ref[0] * nt + i, 0)), blk],
            out_specs=[blk, blk],
        ),
        out_shape=[_sds((nch, h, C), F32), _sds((nch, h, C), BF16)],
        compiler_params=_params(("parallel", "parallel"), 48),
    )(core, g, got)


def _chip_sum(p, got, chip, name):
    _, h, C = p.shape
    tr = _row_tile(h, C)

    def body(j_ref, p_ref, g0_ref, g1_ref, g2_ref, o_ref):
        o_ref[...] = ((p_ref[...] + g0_ref[...].astype(F32)) + g1_ref[...].astype(F32)) + g2_ref[...].astype(F32)

    rel = lambda r: pl.BlockSpec((None, tr, C), lambda i, j_ref: (r, i, 0))
    return pl.pallas_call(
        body,
        name=name,
        grid_spec=pltpu.PrefetchScalarGridSpec(
            num_scalar_prefetch=1,
            grid=(h // tr,),
            in_specs=[pl.BlockSpec((None, tr, C), lambda i, j_ref: (j_ref[0], i, 0)), rel(0), rel(1), rel(2)],
            out_specs=pl.BlockSpec((tr, C), lambda i, j_ref: (i, 0)),
        ),
        out_shape=_sds((h, C), F32),
        compiler_params=_params(("parallel",), 48),
    )(chip, p, got, got, got)


def _adamw(w, g, m, v, name):
    R, C = w.shape
    tr = _row_tile(R, C)
    c1 = 1.0 - ADAM_B1 ** ADAM_STEP
    c2 = 1.0 - ADAM_B2 ** ADAM_STEP

    def body(w_ref, g_ref, m_ref, v_ref, d_ref, nm_ref, nv_ref):
        g = g_ref[...]
        nm = ADAM_B1 * m_ref[...] + (1.0 - ADAM_B1) * g
        nv = ADAM_B2 * v_ref[...] + (1.0 - ADAM_B2) * (g * g)
        nm_ref[...] = nm
        nv_ref[...] = nv
        d_ref[...] = (-ADAM_LR) * ((nm / c1) / (jnp.sqrt(nv / c2) + ADAM_EPS) + ADAM_WD * w_ref[...])

    spec = pl.BlockSpec((tr, C), lambda i: (i, 0))
    return pl.pallas_call(
        body, name=name, grid=(R // tr,), in_specs=[spec] * 4, out_specs=[spec] * 3,
        out_shape=[_sds((R, C), F32)] * 3, compiler_params=_params(("parallel",), 48),
    )(w, g, m, v)


def _place():
    return lax.axis_index("x"), lax.axis_index("y"), lax.axis_index("c")


def _chip_of(x, y, r):
    return (x ^ (r >> 1), y ^ (r & 1))


ANY = pl.BlockSpec(memory_space=pl.ANY)


def _gather_weights(win_b, wr_b, wa_b, wo_b, cw8):
    shards = [win_b, wr_b, wa_b, wo_b]
    nbig = len(shards)
    halves = [s.shape[0] // 2 for s in shards]
    pieces = [4, 1, 1, 1]
    rows = [h // p for h, p in zip(halves, pieces)]
    order = [(a, q) for q in range(max(pieces)) for a in range(nbig) if q < pieces[a]]
    ici_sem = {(a, q, r): 3 * i + (r - 1) for i, (a, q) in enumerate(order) for r in (1, 2, 3)}
    cw_sem = {r: 3 * len(order) + (r - 1) for r in (1, 2, 3)}
    d2d_sem = {key: 3 * len(order) + 3 + k for key, k in ici_sem.items()}
    nsem = 6 * len(order) + 3

    def body(win_ref, wr_ref, wa_ref, wo_ref, cw_ref, gin_ref, gr_ref, ga_ref, go_ref, gcw_ref,
             send_sems, recv_sems, local_sems):
        x, y, c = _place()
        j = 2 * x + y
        srcs = [win_ref, wr_ref, wa_ref, wo_ref]
        dsts = [gin_ref, gr_ref, ga_ref, go_ref]
        locals_ = [pltpu.make_async_copy(srcs[a], dsts[a].at[j], local_sems.at[a]) for a in range(nbig)]
        locals_.append(pltpu.make_async_copy(cw_ref, gcw_ref.at[j], local_sems.at[nbig]))
        for cp in locals_:
            cp.start()

        def piece_rows(a, q, core):
            return pl.ds(pl.multiple_of(core * halves[a] + q * rows[a], 16), rows[a])

        def ici(a, q, r):
            tx, ty = _chip_of(x, y, r)
            k = ici_sem[(a, q, r)]
            return pltpu.make_async_remote_copy(
                src_ref=srcs[a].at[piece_rows(a, q, c), :], dst_ref=dsts[a].at[j, piece_rows(a, q, c), :],
                send_sem=send_sems.at[k], recv_sem=recv_sems.at[k], device_id=(tx, ty, c), device_id_type=MESH)

        def ici_landed(a, q, r):
            tx, ty = _chip_of(x, y, r)
            k = ici_sem[(a, q, r)]
            region = dsts[a].at[2 * tx + ty, piece_rows(a, q, c), :]
            return pltpu.make_async_remote_copy(
                src_ref=region, dst_ref=region, send_sem=send_sems.at[k], recv_sem=recv_sems.at[k],
                device_id=(tx, ty, c), device_id_type=MESH)

        def d2d(a, q, r, core):
            tx, ty = _chip_of(x, y, r)
            k = d2d_sem[(a, q, r)]
            region = dsts[a].at[2 * tx + ty, piece_rows(a, q, core), :]
            return pltpu.make_async_remote_copy(
                src_ref=region, dst_ref=region, send_sem=send_sems.at[k], recv_sem=recv_sems.at[k],
                device_id=(x, y, 1 - c), device_id_type=MESH)

        def cw_copy(r):
            tx, ty = _chip_of(x, y, r)
            k = cw_sem[r]
            return pltpu.make_async_remote_copy(
                src_ref=cw_ref, dst_ref=gcw_ref.at[j], send_sem=send_sems.at[k], recv_sem=recv_sems.at[k],
                device_id=(tx, ty, c), device_id_type=MESH)

        def cw_landed(r):
            tx, ty = _chip_of(x, y, r)
            k = cw_sem[r]
            region = gcw_ref.at[2 * tx + ty]
            return pltpu.make_async_remote_copy(
                src_ref=region, dst_ref=region, send_sem=send_sems.at[k], recv_sem=recv_sems.at[k],
                device_id=(tx, ty, c), device_id_type=MESH)

        first = [ici(a, q, r) for (a, q) in order for r in (1, 2, 3)] + [cw_copy(r) for r in (1, 2, 3)]
        for cp in first:
            cp.start()
        passed = []
        for (a, q) in order:
            for r in (1, 2, 3):
                ici_landed(a, q, r).wait_recv()
                cp = d2d(a, q, r, c)
                cp.start()
                passed.append(cp)
        for r in (1, 2, 3):
            cw_landed(r).wait_recv()
        for (a, q) in order:
            for r in (1, 2, 3):
                d2d(a, q, r, 1 - c).wait_recv()
        for cp in first + passed:
            cp.wait_send()
        for cp in locals_:
            cp.wait()

    return pl.pallas_call(
        body,
        name="gather_weights",
        in_specs=[ANY] * 5,
        out_specs=[ANY] * 5,
        out_shape=[_sds((N_CHIPS,) + s.shape, s.dtype) for s in shards] + [_sds((N_CHIPS,) + cw8.shape, cw8.dtype)],
        scratch_shapes=[pltpu.SemaphoreType.DMA((nsem,)), pltpu.SemaphoreType.DMA((nsem,)),
                        pltpu.SemaphoreType.DMA((nbig + 1,))],
    )(win_b, wr_b, wa_b, wo_b, cw8)


D2D_PIECE_ROWS = 64


def _pair_exchange(grads):
    n = len(grads)
    halves = [g.shape[1] // 2 for g in grads]

    def body(*refs):
        g_refs, got_refs = refs[0:n], refs[n:2 * n]
        send_sems, recv_sems = refs[2 * n:]
        x, y, c = _place()

        def copy(a, src, dst):
            return pltpu.make_async_remote_copy(
                src_ref=src, dst_ref=dst, send_sem=send_sems.at[a], recv_sem=recv_sems.at[a],
                device_id=(x, y, 1 - c), device_id_type=MESH)

        for a in range(n):
            for jj in range(N_CHIPS):
                for q in range(halves[a] // D2D_PIECE_ROWS):
                    src_rows = pl.ds(pl.multiple_of((1 - c) * halves[a] + q * D2D_PIECE_ROWS, 8), D2D_PIECE_ROWS)
                    dst_rows = pl.ds(q * D2D_PIECE_ROWS, D2D_PIECE_ROWS)
                    copy(a, g_refs[a].at[jj, src_rows, :], got_refs[a].at[jj, dst_rows, :]).start()
        for a in range(n):
            sent = g_refs[a].at[:, pl.ds(pl.multiple_of((1 - c) * halves[a], 8), halves[a]), :]
            copy(a, sent, got_refs[a]).wait()

    return pl.pallas_call(
        body,
        name="pair_exchange",
        in_specs=[ANY] * n,
        out_specs=[ANY] * n,
        out_shape=[_sds((N_CHIPS, h, g.shape[2]), F32) for g, h in zip(grads, halves)],
        scratch_shapes=[pltpu.SemaphoreType.DMA((n,)), pltpu.SemaphoreType.DMA((n,))],
    )(*grads)


def _chip_exchange(psums):
    n = len(psums)

    def body(*refs):
        p_refs, got_refs = refs[0:n], refs[n:2 * n]
        send_sems, recv_sems = refs[2 * n:]
        x, y, c = _place()
        copies = []
        for a in range(n):
            for r in (1, 2, 3):
                tx, ty = _chip_of(x, y, r)
                k = a * 3 + (r - 1)
                rem = pltpu.make_async_remote_copy(
                    src_ref=p_refs[a].at[2 * tx + ty], dst_ref=got_refs[a].at[r - 1],
                    send_sem=send_sems.at[k], recv_sem=recv_sems.at[k], device_id=(tx, ty, c), device_id_type=MESH)
                rem.start()
                copies.append(rem)
        for cp in copies:
            cp.wait()

    return pl.pallas_call(
        body,
        name="chip_exchange",
        in_specs=[ANY] * n,
        out_specs=[ANY] * n,
        out_shape=[_sds((3,) + p.shape[1:], p.dtype) for p in psums],
        scratch_shapes=[pltpu.SemaphoreType.DMA((3 * n,)), pltpu.SemaphoreType.DMA((3 * n,))],
    )(*psums)


def _pair_share(finals):
    n = len(finals)
    halves = [f.shape[0] for f in finals]

    def body(*refs):
        f_refs, full_refs = refs[0:n], refs[n:2 * n]
        send_sems, recv_sems, local_sems = refs[2 * n:]
        x, y, c = _place()

        def half_of(a, core):
            return full_refs[a].at[pl.ds(pl.multiple_of(core * halves[a], 8), halves[a]), :]

        def remote(a, src, dst):
            return pltpu.make_async_remote_copy(
                src_ref=src, dst_ref=dst, send_sem=send_sems.at[a], recv_sem=recv_sems.at[a],
                device_id=(x, y, 1 - c), device_id_type=MESH)

        for a in range(n):
            for q in range(halves[a] // D2D_PIECE_ROWS):
                src = f_refs[a].at[pl.ds(q * D2D_PIECE_ROWS, D2D_PIECE_ROWS), :]
                dst = full_refs[a].at[pl.ds(pl.multiple_of(c * halves[a] + q * D2D_PIECE_ROWS, 8), D2D_PIECE_ROWS), :]
                remote(a, src, dst).start()
                pltpu.make_async_copy(src, dst, local_sems.at[a]).start()
        for a in range(n):
            remote(a, f_refs[a], half_of(a, c)).wait_send()
            pltpu.make_async_copy(f_refs[a], half_of(a, c), local_sems.at[a]).wait()
            remote(a, half_of(a, 1 - c), half_of(a, 1 - c)).wait_recv()

    return pl.pallas_call(
        body,
        name="pair_share",
        in_specs=[ANY] * n,
        out_specs=[ANY] * n,
        out_shape=[_sds((2 * f.shape[0], f.shape[1]), F32) for f in finals],
        scratch_shapes=[pltpu.SemaphoreType.DMA((n,)), pltpu.SemaphoreType.DMA((n,)), pltpu.SemaphoreType.DMA((n,))],
    )(*finals)


def _allreduce_small(s):
    R, C = s.shape

    def body(s_ref, o_ref, sib, chips, send_sems, recv_sems):
        x, y, c = _place()
        j = 2 * x + y
        to_sib = pltpu.make_async_remote_copy(
            src_ref=s_ref, dst_ref=sib, send_sem=send_sems.at[0], recv_sem=recv_sems.at[0],
            device_id=(x, y, 1 - c), device_id_type=MESH)
        to_sib.start()
        to_sib.wait()
        chips[j] = s_ref[...] + sib[...]
        sends = []
        for r in (1, 2, 3):
            tx, ty = _chip_of(x, y, r)
            cp = pltpu.make_async_remote_copy(
                src_ref=chips.at[j], dst_ref=chips.at[j], send_sem=send_sems.at[r], recv_sem=recv_sems.at[r],
                device_id=(tx, ty, c), device_id_type=MESH)
            cp.start()
            sends.append(cp)
        for r in (1, 2, 3):
            tx, ty = _chip_of(x, y, r)
            region = chips.at[2 * tx + ty]
            pltpu.make_async_remote_copy(
                src_ref=region, dst_ref=region, send_sem=send_sems.at[r], recv_sem=recv_sems.at[r],
                device_id=(tx, ty, c), device_id_type=MESH).wait_recv()
        for cp in sends:
            cp.wait_send()
        o_ref[...] = (chips[0] + chips[1]) + (chips[2] + chips[3])

    return pl.pallas_call(
        body,
        name="allreduce_small",
        in_specs=[pl.BlockSpec(memory_space=pltpu.VMEM)],
        out_specs=pl.BlockSpec(memory_space=pltpu.VMEM),
        out_shape=_sds((R, C), F32),
        scratch_shapes=[pltpu.VMEM((R, C), F32), pltpu.VMEM((N_CHIPS, R, C), F32),
                        pltpu.SemaphoreType.DMA((4,)), pltpu.SemaphoreType.DMA((4,))],
    )(s)


def _block_diag(w):
    w4 = w.reshape(4, 4, RNN_BLOCK_W, RNN_BLOCK_W)
    eye = jnp.eye(4, dtype=w.dtype)
    return jnp.einsum("jaik,ab->jaibk", w4, eye).reshape(4, RNN_TILE, RNN_TILE)


def _block_diag_part(d):
    d5 = d.reshape(4, 4, RNN_BLOCK_W, 4, RNN_BLOCK_W)
    return jnp.stack([d5[:, a, :, a, :] for a in range(4)], axis=1).reshape(RNN_BLOCKS, RNN_BLOCK_W, RNN_BLOCK_W)


def _local_grads(x, target, g_pre, w_in_g, b_gate, conv_w, conv_b, w_rg_a, b_rg_a, w_rg_x, b_rg_x, lam, sinks,
                 w_rnn_out, w_attn_out, w_out, g_post):
    wa_bd = _block_diag(w_rg_a).astype(BF16)
    wx_bd = _block_diag(w_rg_x).astype(BF16)
    b_a = b_rg_a.reshape(1, D_RNN)
    b_x = b_rg_x.reshape(1, D_RNN)

    proj, h = _proj_fwd(x, g_pre, w_in_g)
    y_rnn, z_rnn = _rnn_fwd(proj, conv_w, conv_b, wa_bd, wx_bd, b_a, b_x, lam)
    y_attn, z_attn = _attn_fwd(proj, sinks)
    dyx, dz_rnn, dz_attn, dml, merged, dout, dbr_rnn, dbr_attn, head_small = _head(
        x, target, z_rnn, z_attn, proj, b_gate, g_post, w_rnn_out, w_attn_out, w_out)
    dw_out = _matmul_tn(merged, dout, "dw_out", 2, False)
    dw_rnn_out = _matmul_tn(z_rnn, dbr_rnn, "dw_rnn_out", 2, False)
    dw_attn_out = _matmul_tn(z_attn, dbr_attn, "dw_attn_out", 2, False)
    dq, dk, dv, dag, attn_small = _attn_bwd(proj, y_attn, dz_attn, sinks)
    drx, drg, dwa_t, dwx_t, rnn_small = _rnn_bwd(proj, y_rnn, dz_rnn, conv_w, conv_b, wa_bd, wx_bd, b_a, b_x, lam)
    dproj = jnp.concatenate([drx, drg, dq, dk.astype(BF16), dv.astype(BF16), dag, dml], axis=1)
    grad_x, dh_small = _dh_bwd(dproj, w_in_g, x, dyx, g_pre)
    dw_in = _matmul_tn(h, dproj, "dw_in", N_CHIPS, True)
    shard_rows = lambda d: d.reshape(N_CHIPS, OUT_SHARD, D_MODEL)
    big = [dw_in, shard_rows(dw_rnn_out), shard_rows(dw_attn_out), shard_rows(dw_out)]
    small = jnp.concatenate([rnn_small, head_small, dh_small + attn_small,
                             _block_diag_part(dwa_t).reshape(64, 1024), _block_diag_part(dwx_t).reshape(64, 1024)], axis=0)
    return grad_x, big, small


ROW_LOSS = 11


def _rows8(parts):
    out = None
    for r, a in parts:
        p = jnp.pad(a, ((r, 8 - r - a.shape[0]), (0, 1024 - a.shape[1])))
        out = p if out is None else out + p
    return out


def _pack_small(p):
    g0 = _rows8([(0, p["b_rg_a"].reshape(1, 1024)), (1, p["b_rg_x"].reshape(1, 1024)), (2, p["lru_lambda"]),
                 (3, p["conv_b"]), (4, p["conv_w"][0])])
    g1 = _rows8([(0, p["post_norm_g"]), (1, p["b_gate"].reshape(2, 1024))])
    g2 = _rows8([(0, p["pre_norm_g"]), (1, p["attn_sinks"])])
    return jnp.concatenate([g0, g1, g2, p["w_rg_a"].reshape(64, 1024), p["w_rg_x"].reshape(64, 1024)], axis=0)


def _unpack_small(s, conv_cols):
    return {
        "b_rg_a": s[0:1].reshape(1, 16, 64), "b_rg_x": s[1:2].reshape(1, 16, 64), "lru_lambda": s[2:3],
        "conv_b": s[3:4], "conv_w": s[4:8, 0:conv_cols].reshape(1, CONV_W, conv_cols),
        "post_norm_g": s[8:9], "b_gate": s[9:11].reshape(1, 2048),
        "pre_norm_g": s[16:17], "attn_sinks": s[17:18, 0:N_Q_HEADS],
        "w_rg_a": s[24:88].reshape(1, 16, 64, 64), "w_rg_x": s[88:152].reshape(1, 16, 64, 64),
    }


WEIGHTS = ["pre_norm_g", "w_in", "b_gate", "conv_w", "conv_b", "w_rg_a", "b_rg_a", "w_rg_x", "b_rg_x", "lru_lambda",
           "attn_sinks", "w_rnn_out", "w_attn_out", "w_out", "post_norm_g"]
BIG = ["w_in", "w_rnn_out", "w_attn_out", "w_out"]


def kernel(x, pre_norm_g, w_in, b_gate, conv_w, conv_b, w_rg_a, b_rg_a, w_rg_x, b_rg_x, lru_lambda, attn_sinks, w_rnn_out, w_attn_out, w_out, post_norm_g, loss_target, m_pre_norm_g, m_w_in, m_b_gate, m_conv_w, m_conv_b, m_w_rg_a, m_b_rg_a, m_w_rg_x, m_b_rg_x, m_lru_lambda, m_attn_sinks, m_w_rnn_out, m_w_attn_out, m_w_out, m_post_norm_g, v_pre_norm_g, v_w_in, v_b_gate, v_conv_w, v_conv_b, v_w_rg_a, v_b_rg_a, v_w_rg_x, v_b_rg_x, v_lru_lambda, v_attn_sinks, v_w_rnn_out, v_w_attn_out, v_w_out, v_post_norm_g):
    w = dict(pre_norm_g=pre_norm_g, w_in=w_in, b_gate=b_gate, conv_w=conv_w, conv_b=conv_b, w_rg_a=w_rg_a,
             b_rg_a=b_rg_a, w_rg_x=w_rg_x, b_rg_x=b_rg_x, lru_lambda=lru_lambda, attn_sinks=attn_sinks,
             w_rnn_out=w_rnn_out, w_attn_out=w_attn_out, w_out=w_out, post_norm_g=post_norm_g)
    m = dict(pre_norm_g=m_pre_norm_g, w_in=m_w_in, b_gate=m_b_gate, conv_w=m_conv_w, conv_b=m_conv_b, w_rg_a=m_w_rg_a,
             b_rg_a=m_b_rg_a, w_rg_x=m_w_rg_x, b_rg_x=m_b_rg_x, lru_lambda=m_lru_lambda, attn_sinks=m_attn_sinks,
             w_rnn_out=m_w_rnn_out, w_attn_out=m_w_attn_out, w_out=m_w_out, post_norm_g=m_post_norm_g)
    v = dict(pre_norm_g=v_pre_norm_g, w_in=v_w_in, b_gate=v_b_gate, conv_w=v_conv_w, conv_b=v_conv_b, w_rg_a=v_w_rg_a,
             b_rg_a=v_b_rg_a, w_rg_x=v_w_rg_x, b_rg_x=v_b_rg_x, lru_lambda=v_lru_lambda, attn_sinks=v_attn_sinks,
             w_rnn_out=v_w_rnn_out, w_attn_out=v_w_attn_out, w_out=v_w_out, post_norm_g=v_post_norm_g)
    chip = 2 * lax.axis_index("x") + lax.axis_index("y")

    cw8 = jnp.pad(conv_w[0], ((0, 8 - CONV_W), (0, 0)))
    win_g, wr_g, wa_g, wo_g, cw_g = _gather_weights(
        w_in[0].astype(BF16), w_rnn_out[0].astype(BF16), w_attn_out[0].astype(BF16), w_out[0].astype(BF16), cw8)
    conv_w_full = jnp.transpose(cw_g[:, 0:CONV_W, :], (1, 0, 2)).reshape(CONV_W, D_RNN)

    grad_x, big, small = _local_grads(
        x[0], loss_target[0], pre_norm_g, win_g, b_gate, conv_w_full, conv_b, w_rg_a[0], b_rg_a[0], w_rg_x[0],
        b_rg_x[0], lru_lambda, attn_sinks[0], wr_g.reshape(D_MODEL, D_MODEL), wa_g.reshape(D_MODEL, D_MODEL),
        wo_g.reshape(D_MODEL, D_MODEL), post_norm_g)

    core_idx = lax.axis_index("c").astype(jnp.int32).reshape(1)
    chip_idx = chip.astype(jnp.int32).reshape(1)
    got = _pair_exchange(big)
    psums = [_pair_sum(g, o, core_idx, "pair_sum_%d" % a) for a, (g, o) in enumerate(zip(big, got))]
    landed = _chip_exchange([pb for _, pb in psums])
    finals = [_chip_sum(p, l, chip_idx, "chip_sum_%d" % a) for a, ((p, _), l) in enumerate(zip(psums, landed))]
    gbig = dict(zip(BIG, _pair_share(finals)))

    small_sum = _allreduce_small(small)
    total_loss = small_sum[ROW_LOSS, 0]
    gsmall = _unpack_small(small_sum, D_RNN)
    conv_shard = D_RNN // N_CHIPS
    gsmall["conv_w"] = lax.dynamic_slice_in_dim(gsmall["conv_w"], chip * conv_shard, conv_shard, axis=2)

    grads, delta, new_m, new_v = {}, {}, {}, {}
    for n in BIG:
        grads[n] = gbig[n][None]
        d, nm, nv = _adamw(w[n][0], gbig[n], m[n][0], v[n][0], "adamw_" + n)
        delta[n], new_m[n], new_v[n] = d[None], nm[None], nv[None]
    pick = lambda t: {k: t[k] for k in gsmall}
    d, nm, nv = _adamw(_pack_small(pick(w)), _pack_small(gsmall), _pack_small(pick(m)), _pack_small(pick(v)),
                       "adamw_small")
    ud, um, uv = _unpack_small(d, conv_shard), _unpack_small(nm, conv_shard), _unpack_small(nv, conv_shard)
    for n in gsmall:
        grads[n] = gsmall[n].reshape(w[n].shape)
        delta[n] = ud[n].reshape(w[n].shape)
        new_m[n] = um[n].reshape(w[n].shape)
        new_v[n] = uv[n].reshape(w[n].shape)

    return (total_loss, grad_x[None], *[grads[n] for n in WEIGHTS], *[delta[n] for n in WEIGHTS],
            *[new_m[n] for n in WEIGHTS], *[new_v[n] for n in WEIGHTS])
```

```python
import functools
import math

import jax
import jax.numpy as jnp
from jax import lax
from jax.experimental import pallas as pl
from jax.experimental.pallas import tpu as pltpu

F32 = jnp.float32
BF16 = jnp.bfloat16

D_MODEL = 1024
D_RNN = 1024
RNN_BLOCKS = 16
RNN_BLOCK_W = 64
CONV_W = 4
LRU_C = 8.0
N_Q_HEADS = 16
N_KV_HEADS = 4
GROUP = 4
HEAD_DIM = 64
D_KV = 256
BLOCK = 128
ALIBI_MAX_BIAS = 8.0
EPS = 1e-6
D_IN = 6656
N_CHIPS = 4
W_IN_SHARD = D_IN // N_CHIPS
OUT_SHARD = D_MODEL // N_CHIPS
ADAM_LR = 0.001
ADAM_B1 = 0.9
ADAM_B2 = 0.999
ADAM_EPS = 1e-08
ADAM_WD = 0.01
ADAM_STEP = 10
NEG_BIG = -1e30
MIB = 1 << 20

COL_RNN_X = 0
COL_RNN_GATE = 4
COL_Q = 8
COL_K = 12
COL_V = 13
COL_ATTN_GATE = 14
COL_MERGE = 18

RNN_TILE = 256
RNN_CHUNK = 256
SMALL_ROWS = 152
MESH = pl.DeviceIdType.MESH


def _sds(shape, dtype):
    return jax.ShapeDtypeStruct(shape, dtype)


def _params(sem=None, vmem_mib=None):
    kw = {}
    if sem is not None:
        kw["dimension_semantics"] = sem
    if vmem_mib is not None:
        kw["vmem_limit_bytes"] = vmem_mib * MIB
    return pltpu.CompilerParams(**kw)


def _dot(a, b):
    return jnp.dot(a, b, preferred_element_type=F32)


def _dot_nt(a, b):
    return lax.dot_general(a, b, (((1,), (1,)), ((), ())), preferred_element_type=F32)


def _dot_tn(a, b):
    return lax.dot_general(a, b, (((0,), (0,)), ((), ())), preferred_element_type=F32)


def _sigmoid(x):
    return 1.0 / (1.0 + jnp.exp(-x))


def _softplus(x):
    return jnp.maximum(x, 0.0) + jnp.log(1.0 + jnp.exp(-jnp.abs(x)))


def _one_minus_exp(z):
    series = -z * (1.0 + z * 0.5 * (1.0 + z * (1.0 / 3.0) * (1.0 + z * 0.25 * (1.0 + z * 0.2 * (1.0 + z * (1.0 / 6.0))))))
    return jnp.where(z > -0.25, series, 1.0 - jnp.exp(z))


def _proj_fwd(x, g_pre, w_in_g):
    T = x.shape[0]
    tm = min(1024, T)

    def body(x_ref, g_ref, w_ref, proj_ref, h_ref):
        @pl.when(pl.program_id(1) == 0)
        def _():
            xv = x_ref[...]
            rstd = lax.rsqrt(jnp.mean(xv * xv, axis=-1, keepdims=True) + EPS)
            h_ref[...] = ((xv * rstd) * g_ref[...]).astype(BF16)

        proj_ref[...] = _dot(h_ref[...], w_ref[...])

    return pl.pallas_call(
        body,
        name="proj_fwd",
        grid=(T // tm, N_CHIPS),
        in_specs=[
            pl.BlockSpec((tm, D_MODEL), lambda i, j: (i, 0)),
            pl.BlockSpec((1, D_MODEL), lambda i, j: (0, 0)),
            pl.BlockSpec((None, D_MODEL, W_IN_SHARD), lambda i, j: (j, 0, 0)),
        ],
        out_specs=[
            pl.BlockSpec((tm, W_IN_SHARD), lambda i, j: (i, j)),
            pl.BlockSpec((tm, D_MODEL), lambda i, j: (i, 0)),
        ],
        out_shape=[_sds((T, D_IN), F32), _sds((T, D_MODEL), BF16)],
        compiler_params=_params(("parallel", "arbitrary"), 48),
    )(x, g_pre, w_in_g)


def _shift_down(x, tail, s, row):
    n = x.shape[0]
    xs = pltpu.roll(x, s, 0)
    tail_t = jnp.tile(pltpu.roll(tail, s, 0), (n // 8, 1))
    return jnp.where(row < s, tail_t, xs)


def _shift_up(x, head, s, row):
    n = x.shape[0]
    xs = pltpu.roll(x, n - s, 0)
    head_t = jnp.tile(pltpu.roll(head, 8 - s, 0), (n // 8, 1))
    return jnp.where(row >= n - s, head_t, xs)


def _conv_taps(x, tail, row):
    return [_shift_down(x, tail, 3, row), _shift_down(x, tail, 2, row), _shift_down(x, tail, 1, row), x]


def _rglru_gates(c, wa, wx, ba, bx, lam):
    cb = c.astype(BF16)
    r = _sigmoid(_dot(cb, wa) + ba)
    i = _sigmoid(_dot(cb, wx) + bx)
    log_a = (-LRU_C) * r * _softplus(-lam)
    a = jnp.exp(log_a)
    mult = jnp.sqrt(_one_minus_exp(2.0 * log_a))
    return cb, r, i, a, mult


def _scan_down(a, u, row):
    n = a.shape[0]
    s = 1
    while s < n:
        a_sh = jnp.where(row >= s, pltpu.roll(a, s, 0), 1.0)
        u_sh = jnp.where(row >= s, pltpu.roll(u, s, 0), 0.0)
        u = a * u_sh + u
        a = a * a_sh
        s *= 2
    return a, u


def _scan_up(b, u, row):
    n = b.shape[0]
    s = 1
    while s < n:
        b_sh = jnp.where(row < n - s, pltpu.roll(b, n - s, 0), 1.0)
        u_sh = jnp.where(row < n - s, pltpu.roll(u, n - s, 0), 0.0)
        u = b * u_sh + u
        b = b * b_sh
        s *= 2
    return b, u


def _rnn_fwd(proj, conv_w, conv_b, wa_bd, wx_bd, b_a, b_x, lam):
    T = proj.shape[0]
    tc, ct = RNN_CHUNK, RNN_TILE
    nt = T // tc

    def body(x_ref, rg_ref, cw_ref, cb_ref, wa_ref, wx_ref, ba_ref, bx_ref, lam_ref, h_ref, z_ref, xtail, hcarry):
        @pl.when(pl.program_id(1) == 0)
        def _():
            xtail[...] = jnp.zeros_like(xtail)
            hcarry[...] = jnp.zeros_like(hcarry)

        row = lax.broadcasted_iota(jnp.int32, (tc, ct), 0)
        x = x_ref[...]
        taps = _conv_taps(x, xtail[...], row)
        c = cb_ref[...] + cw_ref[pl.ds(0, 1), :] * taps[0]
        for k in range(1, CONV_W):
            c = c + cw_ref[pl.ds(k, 1), :] * taps[k]
        xtail[...] = x_ref[pl.ds(tc - 8, 8), :]
        _, _, i, a, mult = _rglru_gates(c, wa_ref[...], wx_ref[...], ba_ref[...], bx_ref[...], lam_ref[...])
        u = mult * (i * c)
        a_cum, h0 = _scan_down(a, u, row)
        h = h0 + a_cum * hcarry[...]
        h_ref[...] = h
        hcarry[...] = h_ref[pl.ds(tc - 1, 1), :]
        rg = rg_ref[...]
        z_ref[...] = (h * (rg * _sigmoid(rg))).astype(BF16)

    col = lambda off: (lambda j, t: (t, off + j))
    vec = pl.BlockSpec((1, ct), lambda j, t: (0, j))
    mat = pl.BlockSpec((None, ct, ct), lambda j, t: (j, 0, 0))
    return pl.pallas_call(
        body,
        name="rnn_fwd",
        grid=(D_RNN // ct, nt),
        in_specs=[
            pl.BlockSpec((tc, ct), col(COL_RNN_X)),
            pl.BlockSpec((tc, ct), col(COL_RNN_GATE)),
            pl.BlockSpec((CONV_W, ct), lambda j, t: (0, j)),
            vec, mat, mat, vec, vec, vec,
        ],
        out_specs=[pl.BlockSpec((tc, ct), lambda j, t: (t, j)), pl.BlockSpec((tc, ct), lambda j, t: (t, j))],
        out_shape=[_sds((T, D_RNN), F32), _sds((T, D_RNN), BF16)],
        scratch_shapes=[pltpu.VMEM((8, ct), F32), pltpu.VMEM((1, ct), F32)],
        compiler_params=_params(("parallel", "arbitrary"), 32),
    )(proj, proj, conv_w, conv_b, wa_bd, wx_bd, b_a, b_x, lam)


def _rnn_bwd(proj, y_rnn, dz_rnn, conv_w, conv_b, wa_bd, wx_bd, b_a, b_x, lam):
    T = proj.shape[0]
    tc, ct = RNN_CHUNK, RNN_TILE
    nt = T // tc
    hb = tc // 8

    def body(x_ref, xh_ref, rg_ref, h_ref, hh_ref, dz_ref, cw_ref, cb_ref, wa_ref, wx_ref, ba_ref, bx_ref, lam_ref,
             dx_ref, drg_ref, dwa_ref, dwx_ref, sm_ref, lam_carry, a_carry, dc_head):
        t = pl.program_id(1)
        first_chunk = t == nt - 1

        @pl.when(t == 0)
        def _():
            lam_carry[...] = jnp.zeros_like(lam_carry)
            a_carry[...] = jnp.zeros_like(a_carry)
            dc_head[...] = jnp.zeros_like(dc_head)
            dwa_ref[...] = jnp.zeros_like(dwa_ref)
            dwx_ref[...] = jnp.zeros_like(dwx_ref)
            sm_ref[...] = jnp.zeros_like(sm_ref)

        row = lax.broadcasted_iota(jnp.int32, (tc, ct), 0)
        keep = jnp.where(first_chunk, 0.0, 1.0)
        x = x_ref[...]
        xtail = xh_ref[...] * keep
        taps = _conv_taps(x, xtail, row)
        c = cb_ref[...] + cw_ref[pl.ds(0, 1), :] * taps[0]
        for k in range(1, CONV_W):
            c = c + cw_ref[pl.ds(k, 1), :] * taps[k]
        lam = lam_ref[...]
        cb, r, i, a, mult = _rglru_gates(c, wa_ref[...], wx_ref[...], ba_ref[...], bx_ref[...], lam)
        h = h_ref[...]
        h_prev = _shift_down(h, hh_ref[...] * keep, 1, row)
        rg = rg_ref[...]
        dz = dz_ref[...]
        sg = _sigmoid(rg)
        drg_ref[...] = (dz * h * (sg * (1.0 + rg * (1.0 - sg)))).astype(BF16)
        dy = dz * (rg * sg)
        b = jnp.where(row >= tc - 1, a_carry[pl.ds(0, 1), :], pltpu.roll(a, tc - 1, 0))
        b_cum, l0 = _scan_up(b, dy, row)
        lt = l0 + b_cum * lam_carry[pl.ds(0, 1), :]
        lam_carry[...] = lt[0:8, :]
        a_carry[...] = a[0:8, :]
        ic = i * c
        dmult = lt * ic
        di = lt * mult * c
        dc = lt * mult * i
        dlog_a = a * (lt * h_prev - dmult * a / mult)
        sp = _softplus(-lam)
        dpre_r = dlog_a * ((-LRU_C) * sp) * (r * (1.0 - r))
        dpre_i = di * (i * (1.0 - i))
        dlam_row = jnp.sum(dlog_a * r, axis=0, keepdims=True) * (LRU_C * _sigmoid(-lam))
        dpr_b = dpre_r.astype(BF16)
        dpi_b = dpre_i.astype(BF16)
        dwa_ref[...] += _dot_tn(cb, dpr_b)
        dwx_ref[...] += _dot_tn(cb, dpi_b)
        dc = dc + _dot_nt(dpr_b, wa_ref[...]) + _dot_nt(dpi_b, wx_ref[...])
        head = dc_head[...]
        dx = cw_ref[pl.ds(3, 1), :] * dc
        for m in range(1, CONV_W):
            dx = dx + cw_ref[pl.ds(3 - m, 1), :] * _shift_up(dc, head, m, row)
        dx_ref[...] = dx.astype(BF16)
        dc_head[...] = dc[0:8, :]
        sm_ref[pl.ds(0, 1), :] += jnp.sum(dpre_r, axis=0, keepdims=True)
        sm_ref[pl.ds(1, 1), :] += jnp.sum(dpre_i, axis=0, keepdims=True)
        sm_ref[pl.ds(2, 1), :] += dlam_row
        sm_ref[pl.ds(3, 1), :] += jnp.sum(dc, axis=0, keepdims=True)
        for k in range(CONV_W):
            sm_ref[pl.ds(4 + k, 1), :] += jnp.sum(dc * taps[k], axis=0, keepdims=True)

    rev = lambda off: (lambda j, t: (nt - 1 - t, off + j))
    halo = lambda off: (lambda j, t: (jnp.maximum((nt - 1 - t) * hb - 1, 0), off + j))
    vec = pl.BlockSpec((1, ct), lambda j, t: (0, j))
    mat = pl.BlockSpec((None, ct, ct), lambda j, t: (j, 0, 0))
    return pl.pallas_call(
        body,
        name="rnn_bwd",
        grid=(D_RNN // ct, nt),
        in_specs=[
            pl.BlockSpec((tc, ct), rev(COL_RNN_X)),
            pl.BlockSpec((8, ct), halo(COL_RNN_X)),
            pl.BlockSpec((tc, ct), rev(COL_RNN_GATE)),
            pl.BlockSpec((tc, ct), rev(0)),
            pl.BlockSpec((8, ct), halo(0)),
            pl.BlockSpec((tc, ct), rev(0)),
            pl.BlockSpec((CONV_W, ct), lambda j, t: (0, j)),
            vec, mat, mat, vec, vec, vec,
        ],
        out_specs=[
            pl.BlockSpec((tc, ct), rev(0)),
            pl.BlockSpec((tc, ct), rev(0)),
            mat, mat,
            pl.BlockSpec((8, ct), lambda j, t: (0, j)),
        ],
        out_shape=[_sds((T, D_RNN), BF16), _sds((T, D_RNN), BF16), _sds((D_RNN // ct, ct, ct), F32),
                   _sds((D_RNN // ct, ct, ct), F32), _sds((8, D_RNN), F32)],
        scratch_shapes=[pltpu.VMEM((8, ct), F32), pltpu.VMEM((8, ct), F32), pltpu.VMEM((8, ct), F32)],
        compiler_params=_params(("parallel", "arbitrary"), 32),
    )(proj, proj, proj, y_rnn, y_rnn, dz_rnn, conv_w, conv_b, wa_bd, wx_bd, b_a, b_x, lam)


def _alibi_slope(h):
    return 2.0 ** (-ALIBI_MAX_BIAS * (h + 1) / N_Q_HEADS)


def _attn_geometry(block_index):
    qi = lax.broadcasted_iota(jnp.int32, (BLOCK, BLOCK), 0)
    kj = lax.broadcasted_iota(jnp.int32, (BLOCK, BLOCK), 1)
    dist_cur = (qi - kj).astype(F32)
    dist_prev = dist_cur + float(BLOCK)
    mask_prev = kj > qi + jnp.where(block_index > 0, 0, BLOCK)
    mask_cur = kj <= qi
    return dist_prev, dist_cur, mask_prev, mask_cur


def _attn_probs(s_prev, s_cur, sink, slope, geo):
    dist_prev, dist_cur, mask_prev, mask_cur = geo
    s_prev = jnp.where(mask_prev, s_prev - slope * dist_prev, NEG_BIG)
    s_cur = jnp.where(mask_cur, s_cur - slope * dist_cur, NEG_BIG)
    m = jnp.maximum(jnp.maximum(jnp.max(s_prev, axis=-1, keepdims=True), jnp.max(s_cur, axis=-1, keepdims=True)), sink)
    p_prev = jnp.exp(s_prev - m)
    p_cur = jnp.exp(s_cur - m)
    p_sink = jnp.exp(sink - m)
    inv = 1.0 / (jnp.sum(p_prev, axis=-1, keepdims=True) + jnp.sum(p_cur, axis=-1, keepdims=True) + p_sink)
    return p_prev * inv, p_cur * inv, p_sink * inv


def _stack_heads(ref_or_val, hk, dtype):
    parts = [ref_or_val[:, (GROUP * hk + g) * HEAD_DIM:(GROUP * hk + g + 1) * HEAD_DIM] for g in range(GROUP)]
    return jnp.concatenate(parts, axis=0).astype(dtype)


def _attn_fwd(proj, sinks):
    T = proj.shape[0]
    nb = T // BLOCK
    scale = HEAD_DIM ** -0.5

    def body(sink_ref, q_ref, kp_ref, kc_ref, vp_ref, vc_ref, ag0_ref, ag1_ref, y_ref, z_ref):
        geo = _attn_geometry(pl.program_id(0))
        for hk in range(N_KV_HEADS):
            ks = slice(hk * HEAD_DIM, (hk + 1) * HEAD_DIM)
            qg = _stack_heads(q_ref, hk, BF16)
            kp = kp_ref[:, ks].astype(BF16)
            kc = kc_ref[:, ks].astype(BF16)
            s_prev = _dot_nt(qg, kp) * scale
            s_cur = _dot_nt(qg, kc) * scale
            pp, pc = [], []
            for g in range(GROUP):
                h = GROUP * hk + g
                rows = slice(g * BLOCK, (g + 1) * BLOCK)
                p_prev, p_cur, _ = _attn_probs(s_prev[rows], s_cur[rows], sink_ref[h], _alibi_slope(h), geo)
                pp.append(p_prev.astype(BF16))
                pc.append(p_cur.astype(BF16))
            og = _dot(jnp.concatenate(pp, axis=0), vp_ref[:, ks].astype(BF16)) + _dot(
                jnp.concatenate(pc, axis=0), vc_ref[:, ks].astype(BF16))
            for g in range(GROUP):
                h = GROUP * hk + g
                y_ref[:, h * HEAD_DIM:(h + 1) * HEAD_DIM] = og[g * BLOCK:(g + 1) * BLOCK]
        ag = jnp.concatenate([ag0_ref[...], ag1_ref[...]], axis=1)
        z_ref[...] = (y_ref[...] * (ag * _sigmoid(ag))).astype(BF16)

    prev = lambda c: (lambda i: (jnp.maximum(i - 1, 0), c))
    cur = lambda c: (lambda i: (i, c))
    return pl.pallas_call(
        body,
        name="attn_fwd",
        grid=(nb,),
        in_specs=[
            pl.BlockSpec(memory_space=pltpu.SMEM),
            pl.BlockSpec((BLOCK, 1024), lambda i: (i, COL_Q // 4)),
            pl.BlockSpec((BLOCK, D_KV), prev(COL_K)),
            pl.BlockSpec((BLOCK, D_KV), cur(COL_K)),
            pl.BlockSpec((BLOCK, D_KV), prev(COL_V)),
            pl.BlockSpec((BLOCK, D_KV), cur(COL_V)),
            pl.BlockSpec((BLOCK, 512), lambda i: (i, COL_ATTN_GATE // 2)),
            pl.BlockSpec((BLOCK, 512), lambda i: (i, COL_ATTN_GATE // 2 + 1)),
        ],
        out_specs=[pl.BlockSpec((BLOCK, 1024), lambda i: (i, 0)), pl.BlockSpec((BLOCK, 1024), lambda i: (i, 0))],
        out_shape=[_sds((T, 1024), F32), _sds((T, 1024), BF16)],
        compiler_params=_params(("arbitrary",), 32),
    )(sinks, proj, proj, proj, proj, proj, proj, proj)


def _attn_bwd(proj, y_attn, dz_attn, sinks):
    T = proj.shape[0]
    nb = T // BLOCK
    scale = HEAD_DIM ** -0.5

    def body(sink_ref, q_ref, kp_ref, kc_ref, vp_ref, vc_ref, ag0_ref, ag1_ref, y_ref, dz_ref,
             dq_ref, dk_ref, dv_ref, dag_ref, ds_ref, dy_s):
        i = pl.program_id(0)

        @pl.when(i == 0)
        def _():
            ds_ref[...] = jnp.zeros_like(ds_ref)

        geo = _attn_geometry(i)
        lane = lax.broadcasted_iota(jnp.int32, (8, 128), 1)
        sub = lax.broadcasted_iota(jnp.int32, (8, 128), 0)
        ag = jnp.concatenate([ag0_ref[...], ag1_ref[...]], axis=1)
        dz = dz_ref[...]
        sg = _sigmoid(ag)
        dag_ref[...] = (dz * y_ref[...] * (sg * (1.0 + ag * (1.0 - sg)))).astype(BF16)
        dy_s[...] = dz * (ag * sg)
        r_cur = pl.multiple_of(i * BLOCK, BLOCK)
        r_prev = pl.multiple_of(jnp.maximum(i - 1, 0) * BLOCK, BLOCK)
        dk_cur, dv_cur, dk_prev, dv_prev = [], [], [], []
        ds_acc = jnp.zeros((8, 128), F32)
        for hk in range(N_KV_HEADS):
            ks = slice(hk * HEAD_DIM, (hk + 1) * HEAD_DIM)
            qg = _stack_heads(q_ref, hk, BF16)
            dog = _stack_heads(dy_s, hk, F32)
            og = _stack_heads(y_ref, hk, F32)
            dog_b = dog.astype(BF16)
            kp = kp_ref[:, ks].astype(BF16)
            kc = kc_ref[:, ks].astype(BF16)
            vp = vp_ref[:, ks].astype(BF16)
            vc = vc_ref[:, ks].astype(BF16)
            s_prev = _dot_nt(qg, kp) * scale
            s_cur = _dot_nt(qg, kc) * scale
            dp_prev = _dot_nt(dog_b, vp)
            dp_cur = _dot_nt(dog_b, vc)
            dvec = jnp.sum(dog * og, axis=-1, keepdims=True)
            pp, pc, dsp, dsc = [], [], [], []
            for g in range(GROUP):
                h = GROUP * hk + g
                rows = slice(g * BLOCK, (g + 1) * BLOCK)
                p_prev, p_cur, p_sink = _attn_probs(s_prev[rows], s_cur[rows], sink_ref[h], _alibi_slope(h), geo)
                d_h = dvec[rows]
                pp.append(p_prev.astype(BF16))
                pc.append(p_cur.astype(BF16))
                dsp.append((p_prev * (dp_prev[rows] - d_h) * scale).astype(BF16))
                dsc.append((p_cur * (dp_cur[rows] - d_h) * scale).astype(BF16))
                dsink = -jnp.sum(p_sink * d_h, axis=0, keepdims=True)
                ds_acc = ds_acc + jnp.where(jnp.logical_and(lane == h, sub == 1), dsink, 0.0)
            pp = jnp.concatenate(pp, axis=0)
            pc = jnp.concatenate(pc, axis=0)
            dsp = jnp.concatenate(dsp, axis=0)
            dsc = jnp.concatenate(dsc, axis=0)
            dqg = _dot(dsp, kp) + _dot(dsc, kc)
            for g in range(GROUP):
                h = GROUP * hk + g
                dq_ref[:, h * HEAD_DIM:(h + 1) * HEAD_DIM] = dqg[g * BLOCK:(g + 1) * BLOCK].astype(BF16)
            dk_ref[pl.ds(r_cur, BLOCK), ks] = _dot_tn(dsc, qg)
            dv_ref[pl.ds(r_cur, BLOCK), ks] = _dot_tn(pc, dog_b)
            dk_prev.append(_dot_tn(dsp, qg))
            dv_prev.append(_dot_tn(pp, dog_b))
        ds_ref[:, 0:128] += ds_acc

        @pl.when(i > 0)
        def _():
            for hk in range(N_KV_HEADS):
                ks = slice(hk * HEAD_DIM, (hk + 1) * HEAD_DIM)
                dk_ref[pl.ds(r_prev, BLOCK), ks] += dk_prev[hk]
                dv_ref[pl.ds(r_prev, BLOCK), ks] += dv_prev[hk]

    prev = lambda c: (lambda i: (jnp.maximum(i - 1, 0), c))
    cur = lambda c: (lambda i: (i, c))
    blk = pl.BlockSpec((BLOCK, 1024), lambda i: (i, 0))
    whole = pl.BlockSpec((T, D_KV), lambda i: (0, 0))
    return pl.pallas_call(
        body,
        name="attn_bwd",
        grid=(nb,),
        in_specs=[
            pl.BlockSpec(memory_space=pltpu.SMEM),
            pl.BlockSpec((BLOCK, 1024), lambda i: (i, COL_Q // 4)),
            pl.BlockSpec((BLOCK, D_KV), prev(COL_K)),
            pl.BlockSpec((BLOCK, D_KV), cur(COL_K)),
            pl.BlockSpec((BLOCK, D_KV), prev(COL_V)),
            pl.BlockSpec((BLOCK, D_KV), cur(COL_V)),
            pl.BlockSpec((BLOCK, 512), lambda i: (i, COL_ATTN_GATE // 2)),
            pl.BlockSpec((BLOCK, 512), lambda i: (i, COL_ATTN_GATE // 2 + 1)),
            blk, blk,
        ],
        out_specs=[blk, whole, whole, blk, pl.BlockSpec((8, 1024), lambda i: (0, 0))],
        out_shape=[_sds((T, 1024), BF16), _sds((T, D_KV), F32), _sds((T, D_KV), F32), _sds((T, 1024), BF16),
                   _sds((8, 1024), F32)],
        scratch_shapes=[pltpu.VMEM((BLOCK, 1024), F32)],
        compiler_params=_params(("arbitrary",), 48),
    )(sinks, proj, proj, proj, proj, proj, proj, proj, y_attn, dz_attn)


def _head(x, target, z_rnn, z_attn, proj, b_gate, g_post, w_rnn_out, w_attn_out, w_out):
    T = x.shape[0]
    tm = 256

    def body(x_ref, t_ref, zr_ref, za_ref, ml0_ref, ml1_ref, ml2_ref, ml3_ref, bg_ref, gp_ref, wr_ref, wa_ref, wo_ref,
             dyx_ref, dzr_ref, dza_ref, dml_ref, mb_ref, dout_ref, dbr_ref, dba_ref, sm_ref):
        @pl.when(pl.program_id(0) == 0)
        def _():
            sm_ref[...] = jnp.zeros_like(sm_ref)

        wr, wa, wo = wr_ref[...], wa_ref[...], wo_ref[...]
        br_rnn = _dot(zr_ref[...], wr)
        br_attn = _dot(za_ref[...], wa)
        ml_rnn = jnp.concatenate([ml0_ref[...], ml1_ref[...]], axis=1)
        ml_attn = jnp.concatenate([ml2_ref[...], ml3_ref[...]], axis=1)
        g_rnn = _sigmoid(ml_rnn + bg_ref[:, 0:D_MODEL])
        g_attn = _sigmoid(ml_attn + bg_ref[:, D_MODEL:2 * D_MODEL])
        mb = (g_rnn * br_rnn + g_attn * br_attn).astype(BF16)
        mb_ref[...] = mb
        out = _dot(mb, wo)
        rstd = lax.rsqrt(jnp.mean(out * out, axis=-1, keepdims=True) + EPS)
        n = out * rstd
        gp = gp_ref[...]
        err = (x_ref[...] + n * gp) - t_ref[...]
        sm_ref[pl.ds(3, 1), :] += 0.5 * jnp.sum(jnp.mean(err * err, axis=-1, keepdims=True), axis=0, keepdims=True)
        dy = err * (1.0 / D_MODEL)
        dyx_ref[...] = dy
        sm_ref[pl.ds(0, 1), :] += jnp.sum(dy * n, axis=0, keepdims=True)
        dn = dy * gp
        dout = (rstd * (dn - n * jnp.mean(dn * n, axis=-1, keepdims=True))).astype(BF16)
        dout_ref[...] = dout
        dmerged = _dot_nt(dout, wo)
        dml_r = (dmerged * br_rnn) * (g_rnn * (1.0 - g_rnn))
        dml_a = (dmerged * br_attn) * (g_attn * (1.0 - g_attn))
        dml_ref[:, 0:D_MODEL] = dml_r.astype(BF16)
        dml_ref[:, D_MODEL:2 * D_MODEL] = dml_a.astype(BF16)
        sm_ref[pl.ds(1, 1), :] += jnp.sum(dml_r, axis=0, keepdims=True)
        sm_ref[pl.ds(2, 1), :] += jnp.sum(dml_a, axis=0, keepdims=True)
        dbr = (dmerged * g_rnn).astype(BF16)
        dba = (dmerged * g_attn).astype(BF16)
        dbr_ref[...] = dbr
        dba_ref[...] = dba
        dzr_ref[...] = _dot_nt(dbr, wr)
        dza_ref[...] = _dot_nt(dba, wa)

    tile = pl.BlockSpec((tm, D_MODEL), lambda i: (i, 0))
    wspec = pl.BlockSpec((D_MODEL, D_MODEL), lambda i: (0, 0))
    ml = lambda q: pl.BlockSpec((tm, 512), lambda i: (i, COL_MERGE // 2 + q))
    return pl.pallas_call(
        body,
        name="head",
        grid=(T // tm,),
        in_specs=[
            tile, tile, tile, tile,
            ml(0), ml(1), ml(2), ml(3),
            pl.BlockSpec((1, 2 * D_MODEL), lambda i: (0, 0)),
            pl.BlockSpec((1, D_MODEL), lambda i: (0, 0)),
            wspec, wspec, wspec,
        ],
        out_specs=[
            tile, tile, tile,
            pl.BlockSpec((tm, 2 * D_MODEL), lambda i: (i, 0)),
            tile, tile, tile, tile,
            pl.BlockSpec((8, D_MODEL), lambda i: (0, 0)),
        ],
        out_shape=[
            _sds((T, D_MODEL), F32), _sds((T, D_MODEL), F32), _sds((T, D_MODEL), F32),
            _sds((T, 2 * D_MODEL), BF16),
            _sds((T, D_MODEL), BF16), _sds((T, D_MODEL), BF16), _sds((T, D_MODEL), BF16), _sds((T, D_MODEL), BF16),
            _sds((8, D_MODEL), F32),
        ],
        compiler_params=_params(("arbitrary",), 56),
    )(x, target, z_rnn, z_attn, proj, proj, proj, proj, b_gate, g_post, w_rnn_out, w_attn_out, w_out)


def _matmul_tn(a, b, name, nblk, blocked):
    T, M = a.shape
    N = b.shape[1]
    tn = N // nblk
    tk = min(512, T)
    if blocked:
        out_spec, out_shape = pl.BlockSpec((None, M, tn), lambda n, t: (n, 0, 0)), _sds((nblk, M, tn), F32)
    else:
        out_spec, out_shape = pl.BlockSpec((M, tn), lambda n, t: (0, n)), _sds((M, N), F32)

    def body(a_ref, b_ref, o_ref):
        @pl.when(pl.program_id(1) == 0)
        def _():
            o_ref[...] = jnp.zeros_like(o_ref)

        o_ref[...] += _dot_tn(a_ref[...], b_ref[...])

    return pl.pallas_call(
        body,
        name=name,
        grid=(nblk, T // tk),
        in_specs=[pl.BlockSpec((tk, M), lambda n, t: (t, 0)), pl.BlockSpec((tk, tn), lambda n, t: (t, n))],
        out_specs=out_spec,
        out_shape=out_shape,
        compiler_params=_params(("parallel", "arbitrary"), 48),
    )(a, b)


def _dh_bwd(dproj, w_in_g, x, dyx, g_pre):
    T = x.shape[0]
    tm = min(512, T)

    def body(dp_ref, w_ref, x_ref, dyx_ref, g_ref, gx_ref, dg_ref, acc):
        i, k = pl.program_id(0), pl.program_id(1)

        @pl.when(jnp.logical_and(i == 0, k == 0))
        def _():
            dg_ref[...] = jnp.zeros_like(dg_ref)

        part = _dot_nt(dp_ref[...], w_ref[...])

        @pl.when(k == 0)
        def _():
            acc[...] = part

        @pl.when(k > 0)
        def _():
            acc[...] += part

        @pl.when(k == N_CHIPS - 1)
        def _():
            xv = x_ref[...]
            dh = acc[...]
            rstd = lax.rsqrt(jnp.mean(xv * xv, axis=-1, keepdims=True) + EPS)
            nx = xv * rstd
            dhg = dh * g_ref[...]
            gx_ref[...] = dyx_ref[...] + rstd * (dhg - nx * jnp.mean(dhg * nx, axis=-1, keepdims=True))
            dg_ref[pl.ds(0, 1), :] += jnp.sum(dh * nx, axis=0, keepdims=True)

    tile = pl.BlockSpec((tm, D_MODEL), lambda i, k: (i, 0))
    return pl.pallas_call(
        body,
        name="dh_bwd",
        grid=(T // tm, N_CHIPS),
        in_specs=[
            pl.BlockSpec((tm, W_IN_SHARD), lambda i, k: (i, k)),
            pl.BlockSpec((None, D_MODEL, W_IN_SHARD), lambda i, k: (k, 0, 0)),
            tile, tile,
            pl.BlockSpec((1, D_MODEL), lambda i, k: (0, 0)),
        ],
        out_specs=[tile, pl.BlockSpec((8, D_MODEL), lambda i, k: (0, 0))],
        out_shape=[_sds((T, D_MODEL), F32), _sds((8, D_MODEL), F32)],
        scratch_shapes=[pltpu.VMEM((tm, D_MODEL), F32)],
        compiler_params=_params(("arbitrary", "arbitrary"), 48),
    )(dproj, w_in_g, x, dyx, g_pre)


ELEMENTWISE_TILE_BYTES = MIB


def _row_tile(rows, cols):
    for t in (512, 256, 128, 64, 32, 16, 8):
        if rows % t == 0 and t * cols * 4 <= ELEMENTWISE_TILE_BYTES:
            return t
    return rows


def _pair_sum(g, got, core, name):
    nch, R, C = g.shape
    h = R // 2
    tr = _row_tile(h, C)
    nt = h // tr

    def body(c_ref, g_ref, got_ref, p_ref, pb_ref):
        s = g_ref[...] + got_ref[...]
        p_ref[...] = s
        pb_ref[...] = s.astype(BF16)

    blk = pl.BlockSpec((None, tr, C), lambda j, i, c_ref: (j, i, 0))
    return pl.pallas_call(
        body,
        name=name,
        grid_spec=pltpu.PrefetchScalarGridSpec(
            num_scalar_prefetch=1,
            grid=(nch, nt),
            in_specs=[pl.BlockSpec((None, tr, C), lambda j, i, c_ref: (j, c_ref[0] * nt + i, 0)), blk],
            out_specs=[blk, blk],
        ),
        out_shape=[_sds((nch, h, C), F32), _sds((nch, h, C), BF16)],
        compiler_params=_params(("parallel", "parallel"), 48),
    )(core, g, got)


def _chip_sum(p, got, chip_core, name):
    _, h, C = p.shape
    tr = _row_tile(h, C)
    nt = h // tr

    def body(jc_ref, p_ref, g0_ref, g1_ref, g2_ref, o_ref):
        o_ref[...] = ((p_ref[...] + g0_ref[...].astype(F32)) + g1_ref[...].astype(F32)) + g2_ref[...].astype(F32)

    rel = lambda r: pl.BlockSpec((None, tr, C), lambda i, jc_ref: (r, i, 0))
    return pl.pallas_call(
        body,
        name=name,
        grid_spec=pltpu.PrefetchScalarGridSpec(
            num_scalar_prefetch=1,
            grid=(nt,),
            in_specs=[pl.BlockSpec((None, tr, C), lambda i, jc_ref: (jc_ref[0], i, 0)), rel(0), rel(1), rel(2)],
            out_specs=pl.BlockSpec((tr, C), lambda i, jc_ref: (jc_ref[1] * nt + i, 0)),
        ),
        out_shape=_sds((2 * h, C), F32),
        compiler_params=_params(("parallel",), 48),
    )(chip_core, p, got, got, got)


def _place_shards(shards, chip, name):
    n = len(shards)
    tiles = [_row_tile(s.shape[0], s.shape[1]) for s in shards]
    steps = max(s.shape[0] // t for s, t in zip(shards, tiles))
    tiles = [s.shape[0] // steps for s in shards]

    def body(j_ref, *refs):
        for a in range(n):
            refs[n + a][...] = refs[a][...].astype(BF16)

    return pl.pallas_call(
        body,
        name=name,
        grid_spec=pltpu.PrefetchScalarGridSpec(
            num_scalar_prefetch=1,
            grid=(steps,),
            in_specs=[pl.BlockSpec((t, s.shape[1]), lambda i, j_ref: (i, 0)) for s, t in zip(shards, tiles)],
            out_specs=[pl.BlockSpec((None, t, s.shape[1]), lambda i, j_ref: (j_ref[0], i, 0))
                       for s, t in zip(shards, tiles)],
        ),
        out_shape=[_sds((N_CHIPS,) + s.shape, BF16) for s in shards],
        compiler_params=_params(("parallel",), 48),
    )(chip, *shards)


def _adamw(w, g, m, v, name):
    R, C = w.shape
    tr = _row_tile(R, C)
    c1 = 1.0 - ADAM_B1 ** ADAM_STEP
    c2 = 1.0 - ADAM_B2 ** ADAM_STEP

    def body(w_ref, g_ref, m_ref, v_ref, d_ref, nm_ref, nv_ref):
        g = g_ref[...]
        nm = ADAM_B1 * m_ref[...] + (1.0 - ADAM_B1) * g
        nv = ADAM_B2 * v_ref[...] + (1.0 - ADAM_B2) * (g * g)
        nm_ref[...] = nm
        nv_ref[...] = nv
        d_ref[...] = (-ADAM_LR) * ((nm / c1) / (jnp.sqrt(nv / c2) + ADAM_EPS) + ADAM_WD * w_ref[...])

    spec = pl.BlockSpec((tr, C), lambda i: (i, 0))
    return pl.pallas_call(
        body, name=name, grid=(R // tr,), in_specs=[spec] * 4, out_specs=[spec] * 3,
        out_shape=[_sds((R, C), F32)] * 3, compiler_params=_params(("parallel",), 48),
    )(w, g, m, v)


def _place():
    return lax.axis_index("x"), lax.axis_index("y"), lax.axis_index("c")


def _chip_of(x, y, r):
    return (x ^ (r >> 1), y ^ (r & 1))


ANY = pl.BlockSpec(memory_space=pl.ANY)


def _gather_weights(placed, cw8):
    nbig = len(placed)
    halves = [s.shape[1] // 2 for s in placed]
    pieces = [4, 1, 1, 1]
    rows = [h // p for h, p in zip(halves, pieces)]
    order = [(a, q) for q in range(max(pieces)) for a in range(nbig) if q < pieces[a]]
    ici_sem = {(a, q, r): 3 * i + (r - 1) for i, (a, q) in enumerate(order) for r in (1, 2, 3)}
    cw_sem = {r: 3 * len(order) + (r - 1) for r in (1, 2, 3)}
    d2d_sem = {key: 3 * len(order) + 3 + k for key, k in ici_sem.items()}
    nsem = 6 * len(order) + 3

    def body(pin_ref, pr_ref, pa_ref, po_ref, cw_ref, gin_ref, gr_ref, ga_ref, go_ref, gcw_ref, send_sems, recv_sems):
        x, y, c = _place()
        j = 2 * x + y
        dsts = [gin_ref, gr_ref, ga_ref, go_ref]

        def piece_rows(a, q, core):
            return pl.ds(pl.multiple_of(core * halves[a] + q * rows[a], 16), rows[a])

        def ici(a, q, r):
            tx, ty = _chip_of(x, y, r)
            k = ici_sem[(a, q, r)]
            region = dsts[a].at[j, piece_rows(a, q, c), :]
            return pltpu.make_async_remote_copy(
                src_ref=region, dst_ref=region, send_sem=send_sems.at[k], recv_sem=recv_sems.at[k],
                device_id=(tx, ty, c), device_id_type=MESH)

        def ici_landed(a, q, r):
            tx, ty = _chip_of(x, y, r)
            k = ici_sem[(a, q, r)]
            region = dsts[a].at[2 * tx + ty, piece_rows(a, q, c), :]
            return pltpu.make_async_remote_copy(
                src_ref=region, dst_ref=region, send_sem=send_sems.at[k], recv_sem=recv_sems.at[k],
                device_id=(tx, ty, c), device_id_type=MESH)

        def d2d(a, q, r, core):
            tx, ty = _chip_of(x, y, r)
            k = d2d_sem[(a, q, r)]
            region = dsts[a].at[2 * tx + ty, piece_rows(a, q, core), :]
            return pltpu.make_async_remote_copy(
                src_ref=region, dst_ref=region, send_sem=send_sems.at[k], recv_sem=recv_sems.at[k],
                device_id=(x, y, 1 - c), device_id_type=MESH)

        def cw_copy(r):
            tx, ty = _chip_of(x, y, r)
            k = cw_sem[r]
            return pltpu.make_async_remote_copy(
                src_ref=cw_ref, dst_ref=gcw_ref.at[j], send_sem=send_sems.at[k], recv_sem=recv_sems.at[k],
                device_id=(tx, ty, c), device_id_type=MESH)

        def cw_landed(r):
            tx, ty = _chip_of(x, y, r)
            k = cw_sem[r]
            region = gcw_ref.at[2 * tx + ty]
            return pltpu.make_async_remote_copy(
                src_ref=region, dst_ref=region, send_sem=send_sems.at[k], recv_sem=recv_sems.at[k],
                device_id=(tx, ty, c), device_id_type=MESH)

        first = [ici(a, q, r) for (a, q) in order for r in (1, 2, 3)] + [cw_copy(r) for r in (1, 2, 3)]
        for cp in first:
            cp.start()
        passed = []
        for (a, q) in order:
            for r in (1, 2, 3):
                ici_landed(a, q, r).wait_recv()
                cp = d2d(a, q, r, c)
                cp.start()
                passed.append(cp)
        for r in (1, 2, 3):
            cw_landed(r).wait_recv()
        for (a, q) in order:
            for r in (1, 2, 3):
                d2d(a, q, r, 1 - c).wait_recv()
        for cp in first + passed:
            cp.wait_send()

    return pl.pallas_call(
        body,
        name="gather_weights",
        in_specs=[ANY] * 5,
        out_specs=[ANY] * 5,
        out_shape=[_sds(s.shape, s.dtype) for s in placed] + [_sds((N_CHIPS,) + cw8.shape, cw8.dtype)],
        input_output_aliases={a: a for a in range(nbig)},
        scratch_shapes=[pltpu.SemaphoreType.DMA((nsem,)), pltpu.SemaphoreType.DMA((nsem,))],
    )(*placed, cw8)


D2D_PIECE_ROWS = 64


def _pair_exchange(grads):
    n = len(grads)
    halves = [g.shape[1] // 2 for g in grads]

    def body(*refs):
        g_refs, got_refs = refs[0:n], refs[n:2 * n]
        send_sems, recv_sems = refs[2 * n:]
        x, y, c = _place()

        def copy(a, src, dst):
            return pltpu.make_async_remote_copy(
                src_ref=src, dst_ref=dst, send_sem=send_sems.at[a], recv_sem=recv_sems.at[a],
                device_id=(x, y, 1 - c), device_id_type=MESH)

        for a in range(n):
            for jj in range(N_CHIPS):
                for q in range(halves[a] // D2D_PIECE_ROWS):
                    src_rows = pl.ds(pl.multiple_of((1 - c) * halves[a] + q * D2D_PIECE_ROWS, 8), D2D_PIECE_ROWS)
                    dst_rows = pl.ds(q * D2D_PIECE_ROWS, D2D_PIECE_ROWS)
                    copy(a, g_refs[a].at[jj, src_rows, :], got_refs[a].at[jj, dst_rows, :]).start()
        for a in range(n):
            sent = g_refs[a].at[:, pl.ds(pl.multiple_of((1 - c) * halves[a], 8), halves[a]), :]
            copy(a, sent, got_refs[a]).wait()

    return pl.pallas_call(
        body,
        name="pair_exchange",
        in_specs=[ANY] * n,
        out_specs=[ANY] * n,
        out_shape=[_sds((N_CHIPS, h, g.shape[2]), F32) for g, h in zip(grads, halves)],
        scratch_shapes=[pltpu.SemaphoreType.DMA((n,)), pltpu.SemaphoreType.DMA((n,))],
    )(*grads)


def _chip_exchange(psums):
    n = len(psums)

    def body(*refs):
        p_refs, got_refs = refs[0:n], refs[n:2 * n]
        send_sems, recv_sems = refs[2 * n:]
        x, y, c = _place()
        copies = []
        for a in range(n):
            for r in (1, 2, 3):
                tx, ty = _chip_of(x, y, r)
                k = a * 3 + (r - 1)
                rem = pltpu.make_async_remote_copy(
                    src_ref=p_refs[a].at[2 * tx + ty], dst_ref=got_refs[a].at[r - 1],
                    send_sem=send_sems.at[k], recv_sem=recv_sems.at[k], device_id=(tx, ty, c), device_id_type=MESH)
                rem.start()
                copies.append(rem)
        for cp in copies:
            cp.wait()

    return pl.pallas_call(
        body,
        name="chip_exchange",
        in_specs=[ANY] * n,
        out_specs=[ANY] * n,
        out_shape=[_sds((3,) + p.shape[1:], p.dtype) for p in psums],
        scratch_shapes=[pltpu.SemaphoreType.DMA((3 * n,)), pltpu.SemaphoreType.DMA((3 * n,))],
    )(*psums)


def _pair_share(fulls):
    n = len(fulls)
    halves = [f.shape[0] // 2 for f in fulls]

    def body(*refs):
        full_refs = refs[n:2 * n]
        send_sems, recv_sems = refs[2 * n:]
        x, y, c = _place()

        def half_of(a, core):
            return full_refs[a].at[pl.ds(pl.multiple_of(core * halves[a], 8), halves[a]), :]

        def remote(a, src, dst):
            return pltpu.make_async_remote_copy(
                src_ref=src, dst_ref=dst, send_sem=send_sems.at[a], recv_sem=recv_sems.at[a],
                device_id=(x, y, 1 - c), device_id_type=MESH)

        for a in range(n):
            for q in range(halves[a] // D2D_PIECE_ROWS):
                piece = full_refs[a].at[
                    pl.ds(pl.multiple_of(c * halves[a] + q * D2D_PIECE_ROWS, 8), D2D_PIECE_ROWS), :]
                remote(a, piece, piece).start()
        for a in range(n):
            remote(a, half_of(a, c), half_of(a, c)).wait_send()
            remote(a, half_of(a, 1 - c), half_of(a, 1 - c)).wait_recv()

    return pl.pallas_call(
        body,
        name="pair_share",
        in_specs=[ANY] * n,
        out_specs=[ANY] * n,
        out_shape=[_sds(f.shape, F32) for f in fulls],
        input_output_aliases={a: a for a in range(n)},
        scratch_shapes=[pltpu.SemaphoreType.DMA((n,)), pltpu.SemaphoreType.DMA((n,))],
    )(*fulls)


def _allreduce_small(s):
    R, C = s.shape

    def body(s_ref, o_ref, sib, chips, send_sems, recv_sems):
        x, y, c = _place()
        j = 2 * x + y
        to_sib = pltpu.make_async_remote_copy(
            src_ref=s_ref, dst_ref=sib, send_sem=send_sems.at[0], recv_sem=recv_sems.at[0],
            device_id=(x, y, 1 - c), device_id_type=MESH)
        to_sib.start()
        to_sib.wait()
        chips[j] = s_ref[...] + sib[...]
        sends = []
        for r in (1, 2, 3):
            tx, ty = _chip_of(x, y, r)
            cp = pltpu.make_async_remote_copy(
                src_ref=chips.at[j], dst_ref=chips.at[j], send_sem=send_sems.at[r], recv_sem=recv_sems.at[r],
                device_id=(tx, ty, c), device_id_type=MESH)
            cp.start()
            sends.append(cp)
        for r in (1, 2, 3):
            tx, ty = _chip_of(x, y, r)
            region = chips.at[2 * tx + ty]
            pltpu.make_async_remote_copy(
                src_ref=region, dst_ref=region, send_sem=send_sems.at[r], recv_sem=recv_sems.at[r],
                device_id=(tx, ty, c), device_id_type=MESH).wait_recv()
        for cp in sends:
            cp.wait_send()
        o_ref[...] = (chips[0] + chips[1]) + (chips[2] + chips[3])

    return pl.pallas_call(
        body,
        name="allreduce_small",
        in_specs=[pl.BlockSpec(memory_space=pltpu.VMEM)],
        out_specs=pl.BlockSpec(memory_space=pltpu.VMEM),
        out_shape=_sds((R, C), F32),
        scratch_shapes=[pltpu.VMEM((R, C), F32), pltpu.VMEM((N_CHIPS, R, C), F32),
                        pltpu.SemaphoreType.DMA((4,)), pltpu.SemaphoreType.DMA((4,))],
    )(s)


def _block_diag(w):
    w4 = w.reshape(4, 4, RNN_BLOCK_W, RNN_BLOCK_W)
    eye = jnp.eye(4, dtype=w.dtype)
    return jnp.einsum("jaik,ab->jaibk", w4, eye).reshape(4, RNN_TILE, RNN_TILE)


def _block_diag_part(d):
    d5 = d.reshape(4, 4, RNN_BLOCK_W, 4, RNN_BLOCK_W)
    return jnp.stack([d5[:, a, :, a, :] for a in range(4)], axis=1).reshape(RNN_BLOCKS, RNN_BLOCK_W, RNN_BLOCK_W)


def _local_grads(x, target, g_pre, w_in_g, b_gate, conv_w, conv_b, w_rg_a, b_rg_a, w_rg_x, b_rg_x, lam, sinks,
                 w_rnn_out, w_attn_out, w_out, g_post):
    wa_bd = _block_diag(w_rg_a).astype(BF16)
    wx_bd = _block_diag(w_rg_x).astype(BF16)
    b_a = b_rg_a.reshape(1, D_RNN)
    b_x = b_rg_x.reshape(1, D_RNN)

    proj, h = _proj_fwd(x, g_pre, w_in_g)
    y_rnn, z_rnn = _rnn_fwd(proj, conv_w, conv_b, wa_bd, wx_bd, b_a, b_x, lam)
    y_attn, z_attn = _attn_fwd(proj, sinks)
    dyx, dz_rnn, dz_attn, dml, merged, dout, dbr_rnn, dbr_attn, head_small = _head(
        x, target, z_rnn, z_attn, proj, b_gate, g_post, w_rnn_out, w_attn_out, w_out)
    dw_out = _matmul_tn(merged, dout, "dw_out", 2, False)
    dw_rnn_out = _matmul_tn(z_rnn, dbr_rnn, "dw_rnn_out", 2, False)
    dw_attn_out = _matmul_tn(z_attn, dbr_attn, "dw_attn_out", 2, False)
    dq, dk, dv, dag, attn_small = _attn_bwd(proj, y_attn, dz_attn, sinks)
    drx, drg, dwa_t, dwx_t, rnn_small = _rnn_bwd(proj, y_rnn, dz_rnn, conv_w, conv_b, wa_bd, wx_bd, b_a, b_x, lam)
    dproj = jnp.concatenate([drx, drg, dq, dk.astype(BF16), dv.astype(BF16), dag, dml], axis=1)
    grad_x, dh_small = _dh_bwd(dproj, w_in_g, x, dyx, g_pre)
    dw_in = _matmul_tn(h, dproj, "dw_in", N_CHIPS, True)
    shard_rows = lambda d: d.reshape(N_CHIPS, OUT_SHARD, D_MODEL)
    big = [dw_in, shard_rows(dw_rnn_out), shard_rows(dw_attn_out), shard_rows(dw_out)]
    small = jnp.concatenate([rnn_small, head_small, dh_small + attn_small,
                             _block_diag_part(dwa_t).reshape(64, 1024), _block_diag_part(dwx_t).reshape(64, 1024)], axis=0)
    return grad_x, big, small


ROW_LOSS = 11


def _rows8(parts):
    out = None
    for r, a in parts:
        p = jnp.pad(a, ((r, 8 - r - a.shape[0]), (0, 1024 - a.shape[1])))
        out = p if out is None else out + p
    return out


def _pack_small(p):
    g0 = _rows8([(0, p["b_rg_a"].reshape(1, 1024)), (1, p["b_rg_x"].reshape(1, 1024)), (2, p["lru_lambda"]),
                 (3, p["conv_b"]), (4, p["conv_w"][0])])
    g1 = _rows8([(0, p["post_norm_g"]), (1, p["b_gate"].reshape(2, 1024))])
    g2 = _rows8([(0, p["pre_norm_g"]), (1, p["attn_sinks"])])
    return jnp.concatenate([g0, g1, g2, p["w_rg_a"].reshape(64, 1024), p["w_rg_x"].reshape(64, 1024)], axis=0)


def _unpack_small(s, conv_cols):
    return {
        "b_rg_a": s[0:1].reshape(1, 16, 64), "b_rg_x": s[1:2].reshape(1, 16, 64), "lru_lambda": s[2:3],
        "conv_b": s[3:4], "conv_w": s[4:8, 0:conv_cols].reshape(1, CONV_W, conv_cols),
        "post_norm_g": s[8:9], "b_gate": s[9:11].reshape(1, 2048),
        "pre_norm_g": s[16:17], "attn_sinks": s[17:18, 0:N_Q_HEADS],
        "w_rg_a": s[24:88].reshape(1, 16, 64, 64), "w_rg_x": s[88:152].reshape(1, 16, 64, 64),
    }


WEIGHTS = ["pre_norm_g", "w_in", "b_gate", "conv_w", "conv_b", "w_rg_a", "b_rg_a", "w_rg_x", "b_rg_x", "lru_lambda",
           "attn_sinks", "w_rnn_out", "w_attn_out", "w_out", "post_norm_g"]
BIG = ["w_in", "w_rnn_out", "w_attn_out", "w_out"]


def kernel(x, pre_norm_g, w_in, b_gate, conv_w, conv_b, w_rg_a, b_rg_a, w_rg_x, b_rg_x, lru_lambda, attn_sinks, w_rnn_out, w_attn_out, w_out, post_norm_g, loss_target, m_pre_norm_g, m_w_in, m_b_gate, m_conv_w, m_conv_b, m_w_rg_a, m_b_rg_a, m_w_rg_x, m_b_rg_x, m_lru_lambda, m_attn_sinks, m_w_rnn_out, m_w_attn_out, m_w_out, m_post_norm_g, v_pre_norm_g, v_w_in, v_b_gate, v_conv_w, v_conv_b, v_w_rg_a, v_b_rg_a, v_w_rg_x, v_b_rg_x, v_lru_lambda, v_attn_sinks, v_w_rnn_out, v_w_attn_out, v_w_out, v_post_norm_g):
    w = dict(pre_norm_g=pre_norm_g, w_in=w_in, b_gate=b_gate, conv_w=conv_w, conv_b=conv_b, w_rg_a=w_rg_a,
             b_rg_a=b_rg_a, w_rg_x=w_rg_x, b_rg_x=b_rg_x, lru_lambda=lru_lambda, attn_sinks=attn_sinks,
             w_rnn_out=w_rnn_out, w_attn_out=w_attn_out, w_out=w_out, post_norm_g=post_norm_g)
    m = dict(pre_norm_g=m_pre_norm_g, w_in=m_w_in, b_gate=m_b_gate, conv_w=m_conv_w, conv_b=m_conv_b, w_rg_a=m_w_rg_a,
             b_rg_a=m_b_rg_a, w_rg_x=m_w_rg_x, b_rg_x=m_b_rg_x, lru_lambda=m_lru_lambda, attn_sinks=m_attn_sinks,
             w_rnn_out=m_w_rnn_out, w_attn_out=m_w_attn_out, w_out=m_w_out, post_norm_g=m_post_norm_g)
    v = dict(pre_norm_g=v_pre_norm_g, w_in=v_w_in, b_gate=v_b_gate, conv_w=v_conv_w, conv_b=v_conv_b, w_rg_a=v_w_rg_a,
             b_rg_a=v_b_rg_a, w_rg_x=v_w_rg_x, b_rg_x=v_b_rg_x, lru_lambda=v_lru_lambda, attn_sinks=v_attn_sinks,
             w_rnn_out=v_w_rnn_out, w_attn_out=v_w_attn_out, w_out=v_w_out, post_norm_g=v_post_norm_g)
    chip = 2 * lax.axis_index("x") + lax.axis_index("y")

    chip_idx = chip.astype(jnp.int32).reshape(1)
    chip_core = jnp.stack([chip, lax.axis_index("c")]).astype(jnp.int32)
    cw8 = jnp.pad(conv_w[0], ((0, 8 - CONV_W), (0, 0)))
    placed = _place_shards([w_in[0], w_rnn_out[0], w_attn_out[0], w_out[0]], chip_idx, "place_shards")
    win_g, wr_g, wa_g, wo_g, cw_g = _gather_weights(placed, cw8)
    cw_g = lax.dynamic_update_slice_in_dim(cw_g, cw8[None], chip, axis=0)
    conv_w_full = jnp.transpose(cw_g[:, 0:CONV_W, :], (1, 0, 2)).reshape(CONV_W, D_RNN)

    grad_x, big, small = _local_grads(
        x[0], loss_target[0], pre_norm_g, win_g, b_gate, conv_w_full, conv_b, w_rg_a[0], b_rg_a[0], w_rg_x[0],
        b_rg_x[0], lru_lambda, attn_sinks[0], wr_g.reshape(D_MODEL, D_MODEL), wa_g.reshape(D_MODEL, D_MODEL),
        wo_g.reshape(D_MODEL, D_MODEL), post_norm_g)

    core_idx = lax.axis_index("c").astype(jnp.int32).reshape(1)
    got = _pair_exchange(big)
    psums = [_pair_sum(g, o, core_idx, "pair_sum_%d" % a) for a, (g, o) in enumerate(zip(big, got))]
    landed = _chip_exchange([pb for _, pb in psums])
    halves = [_chip_sum(p, l, chip_core, "chip_sum_%d" % a) for a, ((p, _), l) in enumerate(zip(psums, landed))]
    gbig = dict(zip(BIG, _pair_share(halves)))

    small_sum = _allreduce_small(small)
    total_loss = small_sum[ROW_LOSS, 0]
    gsmall = _unpack_small(small_sum, D_RNN)
    conv_shard = D_RNN // N_CHIPS
    gsmall["conv_w"] = lax.dynamic_slice_in_dim(gsmall["conv_w"], chip * conv_shard, conv_shard, axis=2)

    grads, delta, new_m, new_v = {}, {}, {}, {}
    for n in BIG:
        grads[n] = gbig[n][None]
        d, nm, nv = _adamw(w[n][0], gbig[n], m[n][0], v[n][0], "adamw_" + n)
        delta[n], new_m[n], new_v[n] = d[None], nm[None], nv[None]
    pick = lambda t: {k: t[k] for k in gsmall}
    d, nm, nv = _adamw(_pack_small(pick(w)), _pack_small(gsmall), _pack_small(pick(m)), _pack_small(pick(v)),
                       "adamw_small")
    ud, um, uv = _unpack_small(d, conv_shard), _unpack_small(nm, conv_shard), _unpack_small(nv, conv_shard)
    for n in gsmall:
        grads[n] = gsmall[n].reshape(w[n].shape)
        delta[n] = ud[n].reshape(w[n].shape)
        new_m[n] = um[n].reshape(w[n].shape)
        new_v[n] = uv[n].reshape(w[n].shape)

    return (total_loss, grad_x[None], *[grads[n] for n in WEIGHTS], *[delta[n] for n in WEIGHTS],
            *[new_m[n] for n in WEIGHTS], *[new_v[n] for n in WEIGHTS])
```

```python
import functools
import math

import jax
import jax.numpy as jnp
from jax import lax
from jax.experimental import pallas as pl
from jax.experimental.pallas import tpu as pltpu

F32 = jnp.float32
BF16 = jnp.bfloat16

D_MODEL = 1024
D_RNN = 1024
RNN_BLOCKS = 16
RNN_BLOCK_W = 64
CONV_W = 4
LRU_C = 8.0
N_Q_HEADS = 16
N_KV_HEADS = 4
GROUP = 4
HEAD_DIM = 64
D_KV = 256
BLOCK = 128
ALIBI_MAX_BIAS = 8.0
EPS = 1e-6
D_IN = 6656
N_CHIPS = 4
W_IN_SHARD = D_IN // N_CHIPS
OUT_SHARD = D_MODEL // N_CHIPS
ADAM_LR = 0.001
ADAM_B1 = 0.9
ADAM_B2 = 0.999
ADAM_EPS = 1e-08
ADAM_WD = 0.01
ADAM_STEP = 10
NEG_BIG = -1e30
MIB = 1 << 20

COL_RNN_X = 0
COL_RNN_GATE = 4
COL_Q = 8
COL_K = 12
COL_V = 13
COL_ATTN_GATE = 14
COL_MERGE = 18

RNN_TILE = 256
RNN_CHUNK = 256
SMALL_ROWS = 152
MESH = pl.DeviceIdType.MESH


def _sds(shape, dtype):
    return jax.ShapeDtypeStruct(shape, dtype)


def _params(sem=None, vmem_mib=None):
    kw = {}
    if sem is not None:
        kw["dimension_semantics"] = sem
    if vmem_mib is not None:
        kw["vmem_limit_bytes"] = vmem_mib * MIB
    return pltpu.CompilerParams(**kw)


def _dot(a, b):
    return jnp.dot(a, b, preferred_element_type=F32)


def _dot_nt(a, b):
    return lax.dot_general(a, b, (((1,), (1,)), ((), ())), preferred_element_type=F32)


def _dot_tn(a, b):
    return lax.dot_general(a, b, (((0,), (0,)), ((), ())), preferred_element_type=F32)


def _sigmoid(x):
    return 1.0 / (1.0 + jnp.exp(-x))


def _softplus(x):
    return jnp.maximum(x, 0.0) + jnp.log(1.0 + jnp.exp(-jnp.abs(x)))


def _one_minus_exp(z):
    series = -z * (1.0 + z * 0.5 * (1.0 + z * (1.0 / 3.0) * (1.0 + z * 0.25 * (1.0 + z * 0.2 * (1.0 + z * (1.0 / 6.0))))))
    return jnp.where(z > -0.25, series, 1.0 - jnp.exp(z))


def _proj_fwd(x, g_pre, w_in_g):
    T = x.shape[0]
    tm = min(1024, T)

    def body(x_ref, g_ref, w_ref, proj_ref, h_ref):
        @pl.when(pl.program_id(1) == 0)
        def _():
            xv = x_ref[...]
            rstd = lax.rsqrt(jnp.mean(xv * xv, axis=-1, keepdims=True) + EPS)
            h_ref[...] = ((xv * rstd) * g_ref[...]).astype(BF16)

        proj_ref[...] = _dot(h_ref[...], w_ref[...])

    return pl.pallas_call(
        body,
        name="proj_fwd",
        grid=(T // tm, N_CHIPS),
        in_specs=[
            pl.BlockSpec((tm, D_MODEL), lambda i, j: (i, 0)),
            pl.BlockSpec((1, D_MODEL), lambda i, j: (0, 0)),
            pl.BlockSpec((None, D_MODEL, W_IN_SHARD), lambda i, j: (j, 0, 0)),
        ],
        out_specs=[
            pl.BlockSpec((tm, W_IN_SHARD), lambda i, j: (i, j)),
            pl.BlockSpec((tm, D_MODEL), lambda i, j: (i, 0)),
        ],
        out_shape=[_sds((T, D_IN), F32), _sds((T, D_MODEL), BF16)],
        compiler_params=_params(("parallel", "arbitrary"), 48),
    )(x, g_pre, w_in_g)


def _shift_down(x, tail, s, row):
    n = x.shape[0]
    xs = pltpu.roll(x, s, 0)
    tail_t = jnp.tile(pltpu.roll(tail, s, 0), (n // 8, 1))
    return jnp.where(row < s, tail_t, xs)


def _shift_up(x, head, s, row):
    n = x.shape[0]
    xs = pltpu.roll(x, n - s, 0)
    head_t = jnp.tile(pltpu.roll(head, 8 - s, 0), (n // 8, 1))
    return jnp.where(row >= n - s, head_t, xs)


def _conv_taps(x, tail, row):
    return [_shift_down(x, tail, 3, row), _shift_down(x, tail, 2, row), _shift_down(x, tail, 1, row), x]


def _rglru_gates(c, wa, wx, ba, bx, lam):
    cb = c.astype(BF16)
    r = _sigmoid(_dot(cb, wa) + ba)
    i = _sigmoid(_dot(cb, wx) + bx)
    log_a = (-LRU_C) * r * _softplus(-lam)
    a = jnp.exp(log_a)
    mult = jnp.sqrt(_one_minus_exp(2.0 * log_a))
    return cb, r, i, a, mult


def _scan_down(a, u, row):
    n = a.shape[0]
    s = 1
    while s < n:
        a_sh = jnp.where(row >= s, pltpu.roll(a, s, 0), 1.0)
        u_sh = jnp.where(row >= s, pltpu.roll(u, s, 0), 0.0)
        u = a * u_sh + u
        a = a * a_sh
        s *= 2
    return a, u


def _scan_up(b, u, row):
    n = b.shape[0]
    s = 1
    while s < n:
        b_sh = jnp.where(row < n - s, pltpu.roll(b, n - s, 0), 1.0)
        u_sh = jnp.where(row < n - s, pltpu.roll(u, n - s, 0), 0.0)
        u = b * u_sh + u
        b = b * b_sh
        s *= 2
    return b, u


def _rnn_fwd(proj, conv_w, conv_b, wa_bd, wx_bd, b_a, b_x, lam):
    T = proj.shape[0]
    tc, ct = RNN_CHUNK, RNN_TILE
    nt = T // tc

    def body(x_ref, rg_ref, cw_ref, cb_ref, wa_ref, wx_ref, ba_ref, bx_ref, lam_ref, h_ref, z_ref, xtail, hcarry):
        @pl.when(pl.program_id(1) == 0)
        def _():
            xtail[...] = jnp.zeros_like(xtail)
            hcarry[...] = jnp.zeros_like(hcarry)

        row = lax.broadcasted_iota(jnp.int32, (tc, ct), 0)
        x = x_ref[...]
        taps = _conv_taps(x, xtail[...], row)
        c = cb_ref[...] + cw_ref[pl.ds(0, 1), :] * taps[0]
        for k in range(1, CONV_W):
            c = c + cw_ref[pl.ds(k, 1), :] * taps[k]
        xtail[...] = x_ref[pl.ds(tc - 8, 8), :]
        _, _, i, a, mult = _rglru_gates(c, wa_ref[...], wx_ref[...], ba_ref[...], bx_ref[...], lam_ref[...])
        u = mult * (i * c)
        a_cum, h0 = _scan_down(a, u, row)
        h = h0 + a_cum * hcarry[...]
        h_ref[...] = h
        hcarry[...] = h_ref[pl.ds(tc - 1, 1), :]
        rg = rg_ref[...]
        z_ref[...] = (h * (rg * _sigmoid(rg))).astype(BF16)

    col = lambda off: (lambda j, t: (t, off + j))
    vec = pl.BlockSpec((1, ct), lambda j, t: (0, j))
    mat = pl.BlockSpec((None, ct, ct), lambda j, t: (j, 0, 0))
    return pl.pallas_call(
        body,
        name="rnn_fwd",
        grid=(D_RNN // ct, nt),
        in_specs=[
            pl.BlockSpec((tc, ct), col(COL_RNN_X)),
            pl.BlockSpec((tc, ct), col(COL_RNN_GATE)),
            pl.BlockSpec((CONV_W, ct), lambda j, t: (0, j)),
            vec, mat, mat, vec, vec, vec,
        ],
        out_specs=[pl.BlockSpec((tc, ct), lambda j, t: (t, j)), pl.BlockSpec((tc, ct), lambda j, t: (t, j))],
        out_shape=[_sds((T, D_RNN), F32), _sds((T, D_RNN), BF16)],
        scratch_shapes=[pltpu.VMEM((8, ct), F32), pltpu.VMEM((1, ct), F32)],
        compiler_params=_params(("parallel", "arbitrary"), 32),
    )(proj, proj, conv_w, conv_b, wa_bd, wx_bd, b_a, b_x, lam)


def _rnn_bwd(proj, y_rnn, dz_rnn, conv_w, conv_b, wa_bd, wx_bd, b_a, b_x, lam):
    T = proj.shape[0]
    tc, ct = RNN_CHUNK, RNN_TILE
    nt = T // tc
    hb = tc // 8

    def body(x_ref, xh_ref, rg_ref, h_ref, hh_ref, dz_ref, cw_ref, cb_ref, wa_ref, wx_ref, ba_ref, bx_ref, lam_ref,
             dx_ref, drg_ref, dwa_ref, dwx_ref, sm_ref, lam_carry, a_carry, dc_head):
        t = pl.program_id(1)
        first_chunk = t == nt - 1

        @pl.when(t == 0)
        def _():
            lam_carry[...] = jnp.zeros_like(lam_carry)
            a_carry[...] = jnp.zeros_like(a_carry)
            dc_head[...] = jnp.zeros_like(dc_head)
            dwa_ref[...] = jnp.zeros_like(dwa_ref)
            dwx_ref[...] = jnp.zeros_like(dwx_ref)
            sm_ref[...] = jnp.zeros_like(sm_ref)

        row = lax.broadcasted_iota(jnp.int32, (tc, ct), 0)
        keep = jnp.where(first_chunk, 0.0, 1.0)
        x = x_ref[...]
        xtail = xh_ref[...] * keep
        taps = _conv_taps(x, xtail, row)
        c = cb_ref[...] + cw_ref[pl.ds(0, 1), :] * taps[0]
        for k in range(1, CONV_W):
            c = c + cw_ref[pl.ds(k, 1), :] * taps[k]
        lam = lam_ref[...]
        cb, r, i, a, mult = _rglru_gates(c, wa_ref[...], wx_ref[...], ba_ref[...], bx_ref[...], lam)
        h = h_ref[...]
        h_prev = _shift_down(h, hh_ref[...] * keep, 1, row)
        rg = rg_ref[...]
        dz = dz_ref[...]
        sg = _sigmoid(rg)
        drg_ref[...] = (dz * h * (sg * (1.0 + rg * (1.0 - sg)))).astype(BF16)
        dy = dz * (rg * sg)
        b = jnp.where(row >= tc - 1, a_carry[pl.ds(0, 1), :], pltpu.roll(a, tc - 1, 0))
        b_cum, l0 = _scan_up(b, dy, row)
        lt = l0 + b_cum * lam_carry[pl.ds(0, 1), :]
        lam_carry[...] = lt[0:8, :]
        a_carry[...] = a[0:8, :]
        ic = i * c
        dmult = lt * ic
        di = lt * mult * c
        dc = lt * mult * i
        dlog_a = a * (lt * h_prev - dmult * a / mult)
        sp = _softplus(-lam)
        dpre_r = dlog_a * ((-LRU_C) * sp) * (r * (1.0 - r))
        dpre_i = di * (i * (1.0 - i))
        dlam_row = jnp.sum(dlog_a * r, axis=0, keepdims=True) * (LRU_C * _sigmoid(-lam))
        dpr_b = dpre_r.astype(BF16)
        dpi_b = dpre_i.astype(BF16)
        dwa_ref[...] += _dot_tn(cb, dpr_b)
        dwx_ref[...] += _dot_tn(cb, dpi_b)
        dc = dc + _dot_nt(dpr_b, wa_ref[...]) + _dot_nt(dpi_b, wx_ref[...])
        head = dc_head[...]
        dx = cw_ref[pl.ds(3, 1), :] * dc
        for m in range(1, CONV_W):
            dx = dx + cw_ref[pl.ds(3 - m, 1), :] * _shift_up(dc, head, m, row)
        dx_ref[...] = dx.astype(BF16)
        dc_head[...] = dc[0:8, :]
        sm_ref[pl.ds(0, 1), :] += jnp.sum(dpre_r, axis=0, keepdims=True)
        sm_ref[pl.ds(1, 1), :] += jnp.sum(dpre_i, axis=0, keepdims=True)
        sm_ref[pl.ds(2, 1), :] += dlam_row
        sm_ref[pl.ds(3, 1), :] += jnp.sum(dc, axis=0, keepdims=True)
        for k in range(CONV_W):
            sm_ref[pl.ds(4 + k, 1), :] += jnp.sum(dc * taps[k], axis=0, keepdims=True)

    rev = lambda off: (lambda j, t: (nt - 1 - t, off + j))
    halo = lambda off: (lambda j, t: (jnp.maximum((nt - 1 - t) * hb - 1, 0), off + j))
    vec = pl.BlockSpec((1, ct), lambda j, t: (0, j))
    mat = pl.BlockSpec((None, ct, ct), lambda j, t: (j, 0, 0))
    return pl.pallas_call(
        body,
        name="rnn_bwd",
        grid=(D_RNN // ct, nt),
        in_specs=[
            pl.BlockSpec((tc, ct), rev(COL_RNN_X)),
            pl.BlockSpec((8, ct), halo(COL_RNN_X)),
            pl.BlockSpec((tc, ct), rev(COL_RNN_GATE)),
            pl.BlockSpec((tc, ct), rev(0)),
            pl.BlockSpec((8, ct), halo(0)),
            pl.BlockSpec((tc, ct), rev(0)),
            pl.BlockSpec((CONV_W, ct), lambda j, t: (0, j)),
            vec, mat, mat, vec, vec, vec,
        ],
        out_specs=[
            pl.BlockSpec((tc, ct), rev(0)),
            pl.BlockSpec((tc, ct), rev(0)),
            mat, mat,
            pl.BlockSpec((8, ct), lambda j, t: (0, j)),
        ],
        out_shape=[_sds((T, D_RNN), BF16), _sds((T, D_RNN), BF16), _sds((D_RNN // ct, ct, ct), F32),
                   _sds((D_RNN // ct, ct, ct), F32), _sds((8, D_RNN), F32)],
        scratch_shapes=[pltpu.VMEM((8, ct), F32), pltpu.VMEM((8, ct), F32), pltpu.VMEM((8, ct), F32)],
        compiler_params=_params(("parallel", "arbitrary"), 32),
    )(proj, proj, proj, y_rnn, y_rnn, dz_rnn, conv_w, conv_b, wa_bd, wx_bd, b_a, b_x, lam)


def _alibi_slope(h):
    return 2.0 ** (-ALIBI_MAX_BIAS * (h + 1) / N_Q_HEADS)


def _attn_geometry(block_index):
    qi = lax.broadcasted_iota(jnp.int32, (BLOCK, BLOCK), 0)
    kj = lax.broadcasted_iota(jnp.int32, (BLOCK, BLOCK), 1)
    dist_cur = (qi - kj).astype(F32)
    dist_prev = dist_cur + float(BLOCK)
    mask_prev = kj > qi + jnp.where(block_index > 0, 0, BLOCK)
    mask_cur = kj <= qi
    return dist_prev, dist_cur, mask_prev, mask_cur


def _attn_probs(s_prev, s_cur, sink, slope, geo):
    dist_prev, dist_cur, mask_prev, mask_cur = geo
    s_prev = jnp.where(mask_prev, s_prev - slope * dist_prev, NEG_BIG)
    s_cur = jnp.where(mask_cur, s_cur - slope * dist_cur, NEG_BIG)
    m = jnp.maximum(jnp.maximum(jnp.max(s_prev, axis=-1, keepdims=True), jnp.max(s_cur, axis=-1, keepdims=True)), sink)
    p_prev = jnp.exp(s_prev - m)
    p_cur = jnp.exp(s_cur - m)
    p_sink = jnp.exp(sink - m)
    inv = 1.0 / (jnp.sum(p_prev, axis=-1, keepdims=True) + jnp.sum(p_cur, axis=-1, keepdims=True) + p_sink)
    return p_prev * inv, p_cur * inv, p_sink * inv


def _stack_heads(ref_or_val, hk, dtype):
    parts = [ref_or_val[:, (GROUP * hk + g) * HEAD_DIM:(GROUP * hk + g + 1) * HEAD_DIM] for g in range(GROUP)]
    return jnp.concatenate(parts, axis=0).astype(dtype)


def _attn_fwd(proj, sinks):
    T = proj.shape[0]
    nb = T // BLOCK
    scale = HEAD_DIM ** -0.5

    def body(sink_ref, q_ref, kp_ref, kc_ref, vp_ref, vc_ref, ag0_ref, ag1_ref, y_ref, z_ref):
        geo = _attn_geometry(pl.program_id(0))
        for hk in range(N_KV_HEADS):
            ks = slice(hk * HEAD_DIM, (hk + 1) * HEAD_DIM)
            qg = _stack_heads(q_ref, hk, BF16)
            kp = kp_ref[:, ks].astype(BF16)
            kc = kc_ref[:, ks].astype(BF16)
            s_prev = _dot_nt(qg, kp) * scale
            s_cur = _dot_nt(qg, kc) * scale
            pp, pc = [], []
            for g in range(GROUP):
                h = GROUP * hk + g
                rows = slice(g * BLOCK, (g + 1) * BLOCK)
                p_prev, p_cur, _ = _attn_probs(s_prev[rows], s_cur[rows], sink_ref[h], _alibi_slope(h), geo)
                pp.append(p_prev.astype(BF16))
                pc.append(p_cur.astype(BF16))
            og = _dot(jnp.concatenate(pp, axis=0), vp_ref[:, ks].astype(BF16)) + _dot(
                jnp.concatenate(pc, axis=0), vc_ref[:, ks].astype(BF16))
            for g in range(GROUP):
                h = GROUP * hk + g
                y_ref[:, h * HEAD_DIM:(h + 1) * HEAD_DIM] = og[g * BLOCK:(g + 1) * BLOCK]
        ag = jnp.concatenate([ag0_ref[...], ag1_ref[...]], axis=1)
        z_ref[...] = (y_ref[...] * (ag * _sigmoid(ag))).astype(BF16)

    prev = lambda c: (lambda i: (jnp.maximum(i - 1, 0), c))
    cur = lambda c: (lambda i: (i, c))
    return pl.pallas_call(
        body,
        name="attn_fwd",
        grid=(nb,),
        in_specs=[
            pl.BlockSpec(memory_space=pltpu.SMEM),
            pl.BlockSpec((BLOCK, 1024), lambda i: (i, COL_Q // 4)),
            pl.BlockSpec((BLOCK, D_KV), prev(COL_K)),
            pl.BlockSpec((BLOCK, D_KV), cur(COL_K)),
            pl.BlockSpec((BLOCK, D_KV), prev(COL_V)),
            pl.BlockSpec((BLOCK, D_KV), cur(COL_V)),
            pl.BlockSpec((BLOCK, 512), lambda i: (i, COL_ATTN_GATE // 2)),
            pl.BlockSpec((BLOCK, 512), lambda i: (i, COL_ATTN_GATE // 2 + 1)),
        ],
        out_specs=[pl.BlockSpec((BLOCK, 1024), lambda i: (i, 0)), pl.BlockSpec((BLOCK, 1024), lambda i: (i, 0))],
        out_shape=[_sds((T, 1024), F32), _sds((T, 1024), BF16)],
        compiler_params=_params(("arbitrary",), 32),
    )(sinks, proj, proj, proj, proj, proj, proj, proj)


def _attn_bwd(proj, y_attn, dz_attn, sinks, token):
    T = proj.shape[0]
    nb = T // BLOCK
    scale = HEAD_DIM ** -0.5

    def body(sink_ref, q_ref, kp_ref, kc_ref, vp_ref, vc_ref, ag0_ref, ag1_ref, y_ref, dz_ref, token_ref,
             dq_ref, dk_ref, dv_ref, dag_ref, ds_ref, dy_s):
        i = pl.program_id(0)

        @pl.when(i == 0)
        def _():
            ds_ref[...] = jnp.zeros_like(ds_ref)

        geo = _attn_geometry(i)
        lane = lax.broadcasted_iota(jnp.int32, (8, 128), 1)
        sub = lax.broadcasted_iota(jnp.int32, (8, 128), 0)
        ag = jnp.concatenate([ag0_ref[...], ag1_ref[...]], axis=1)
        dz = dz_ref[...]
        sg = _sigmoid(ag)
        dag_ref[...] = (dz * y_ref[...] * (sg * (1.0 + ag * (1.0 - sg)))).astype(BF16)
        dy_s[...] = dz * (ag * sg)
        r_cur = pl.multiple_of(i * BLOCK, BLOCK)
        r_prev = pl.multiple_of(jnp.maximum(i - 1, 0) * BLOCK, BLOCK)
        dk_cur, dv_cur, dk_prev, dv_prev = [], [], [], []
        ds_acc = jnp.zeros((8, 128), F32)
        for hk in range(N_KV_HEADS):
            ks = slice(hk * HEAD_DIM, (hk + 1) * HEAD_DIM)
            qg = _stack_heads(q_ref, hk, BF16)
            dog = _stack_heads(dy_s, hk, F32)
            og = _stack_heads(y_ref, hk, F32)
            dog_b = dog.astype(BF16)
            kp = kp_ref[:, ks].astype(BF16)
            kc = kc_ref[:, ks].astype(BF16)
            vp = vp_ref[:, ks].astype(BF16)
            vc = vc_ref[:, ks].astype(BF16)
            s_prev = _dot_nt(qg, kp) * scale
            s_cur = _dot_nt(qg, kc) * scale
            dp_prev = _dot_nt(dog_b, vp)
            dp_cur = _dot_nt(dog_b, vc)
            dvec = jnp.sum(dog * og, axis=-1, keepdims=True)
            pp, pc, dsp, dsc = [], [], [], []
            for g in range(GROUP):
                h = GROUP * hk + g
                rows = slice(g * BLOCK, (g + 1) * BLOCK)
                p_prev, p_cur, p_sink = _attn_probs(s_prev[rows], s_cur[rows], sink_ref[h], _alibi_slope(h), geo)
                d_h = dvec[rows]
                pp.append(p_prev.astype(BF16))
                pc.append(p_cur.astype(BF16))
                dsp.append((p_prev * (dp_prev[rows] - d_h) * scale).astype(BF16))
                dsc.append((p_cur * (dp_cur[rows] - d_h) * scale).astype(BF16))
                dsink = -jnp.sum(p_sink * d_h, axis=0, keepdims=True)
                ds_acc = ds_acc + jnp.where(jnp.logical_and(lane == h, sub == 1), dsink, 0.0)
            pp = jnp.concatenate(pp, axis=0)
            pc = jnp.concatenate(pc, axis=0)
            dsp = jnp.concatenate(dsp, axis=0)
            dsc = jnp.concatenate(dsc, axis=0)
            dqg = _dot(dsp, kp) + _dot(dsc, kc)
            for g in range(GROUP):
                h = GROUP * hk + g
                dq_ref[:, h * HEAD_DIM:(h + 1) * HEAD_DIM] = dqg[g * BLOCK:(g + 1) * BLOCK].astype(BF16)
            dk_ref[pl.ds(r_cur, BLOCK), ks] = _dot_tn(dsc, qg)
            dv_ref[pl.ds(r_cur, BLOCK), ks] = _dot_tn(pc, dog_b)
            dk_prev.append(_dot_tn(dsp, qg))
            dv_prev.append(_dot_tn(pp, dog_b))
        ds_ref[:, 0:128] += ds_acc

        @pl.when(i > 0)
        def _():
            for hk in range(N_KV_HEADS):
                ks = slice(hk * HEAD_DIM, (hk + 1) * HEAD_DIM)
                dk_ref[pl.ds(r_prev, BLOCK), ks] += dk_prev[hk]
                dv_ref[pl.ds(r_prev, BLOCK), ks] += dv_prev[hk]

    prev = lambda c: (lambda i: (jnp.maximum(i - 1, 0), c))
    cur = lambda c: (lambda i: (i, c))
    blk = pl.BlockSpec((BLOCK, 1024), lambda i: (i, 0))
    whole = pl.BlockSpec((T, D_KV), lambda i: (0, 0))
    return pl.pallas_call(
        body,
        name="attn_bwd",
        grid=(nb,),
        in_specs=[
            pl.BlockSpec(memory_space=pltpu.SMEM),
            pl.BlockSpec((BLOCK, 1024), lambda i: (i, COL_Q // 4)),
            pl.BlockSpec((BLOCK, D_KV), prev(COL_K)),
            pl.BlockSpec((BLOCK, D_KV), cur(COL_K)),
            pl.BlockSpec((BLOCK, D_KV), prev(COL_V)),
            pl.BlockSpec((BLOCK, D_KV), cur(COL_V)),
            pl.BlockSpec((BLOCK, 512), lambda i: (i, COL_ATTN_GATE // 2)),
            pl.BlockSpec((BLOCK, 512), lambda i: (i, COL_ATTN_GATE // 2 + 1)),
            blk, blk,
            pl.BlockSpec((8, 128), lambda i: (0, 0)),
        ],
        out_specs=[blk, whole, whole, blk, pl.BlockSpec((8, 1024), lambda i: (0, 0))],
        out_shape=[_sds((T, 1024), BF16), _sds((T, D_KV), F32), _sds((T, D_KV), F32), _sds((T, 1024), BF16),
                   _sds((8, 1024), F32)],
        scratch_shapes=[pltpu.VMEM((BLOCK, 1024), F32)],
        compiler_params=_params(("arbitrary",), 48),
    )(sinks, proj, proj, proj, proj, proj, proj, proj, y_attn, dz_attn, token)


def _head(x, target, z_rnn, z_attn, proj, b_gate, g_post, w_rnn_out, w_attn_out, w_out):
    T = x.shape[0]
    tm = 256

    def body(x_ref, t_ref, zr_ref, za_ref, ml0_ref, ml1_ref, ml2_ref, ml3_ref, bg_ref, gp_ref, wr_ref, wa_ref, wo_ref,
             dyx_ref, dzr_ref, dza_ref, dml_ref, mb_ref, dout_ref, dbr_ref, dba_ref, sm_ref):
        @pl.when(pl.program_id(0) == 0)
        def _():
            sm_ref[...] = jnp.zeros_like(sm_ref)

        wr, wa, wo = wr_ref[...], wa_ref[...], wo_ref[...]
        br_rnn = _dot(zr_ref[...], wr)
        br_attn = _dot(za_ref[...], wa)
        ml_rnn = jnp.concatenate([ml0_ref[...], ml1_ref[...]], axis=1)
        ml_attn = jnp.concatenate([ml2_ref[...], ml3_ref[...]], axis=1)
        g_rnn = _sigmoid(ml_rnn + bg_ref[:, 0:D_MODEL])
        g_attn = _sigmoid(ml_attn + bg_ref[:, D_MODEL:2 * D_MODEL])
        mb = (g_rnn * br_rnn + g_attn * br_attn).astype(BF16)
        mb_ref[...] = mb
        out = _dot(mb, wo)
        rstd = lax.rsqrt(jnp.mean(out * out, axis=-1, keepdims=True) + EPS)
        n = out * rstd
        gp = gp_ref[...]
        err = (x_ref[...] + n * gp) - t_ref[...]
        sm_ref[pl.ds(3, 1), :] += 0.5 * jnp.sum(jnp.mean(err * err, axis=-1, keepdims=True), axis=0, keepdims=True)
        dy = err * (1.0 / D_MODEL)
        dyx_ref[...] = dy
        sm_ref[pl.ds(0, 1), :] += jnp.sum(dy * n, axis=0, keepdims=True)
        dn = dy * gp
        dout = (rstd * (dn - n * jnp.mean(dn * n, axis=-1, keepdims=True))).astype(BF16)
        dout_ref[...] = dout
        dmerged = _dot_nt(dout, wo)
        dml_r = (dmerged * br_rnn) * (g_rnn * (1.0 - g_rnn))
        dml_a = (dmerged * br_attn) * (g_attn * (1.0 - g_attn))
        dml_ref[:, 0:D_MODEL] = dml_r.astype(BF16)
        dml_ref[:, D_MODEL:2 * D_MODEL] = dml_a.astype(BF16)
        sm_ref[pl.ds(1, 1), :] += jnp.sum(dml_r, axis=0, keepdims=True)
        sm_ref[pl.ds(2, 1), :] += jnp.sum(dml_a, axis=0, keepdims=True)
        dbr = (dmerged * g_rnn).astype(BF16)
        dba = (dmerged * g_attn).astype(BF16)
        dbr_ref[...] = dbr
        dba_ref[...] = dba
        dzr_ref[...] = _dot_nt(dbr, wr)
        dza_ref[...] = _dot_nt(dba, wa)

    tile = pl.BlockSpec((tm, D_MODEL), lambda i: (i, 0))
    wspec = pl.BlockSpec((D_MODEL, D_MODEL), lambda i: (0, 0))
    ml = lambda q: pl.BlockSpec((tm, 512), lambda i: (i, COL_MERGE // 2 + q))
    return pl.pallas_call(
        body,
        name="head",
        grid=(T // tm,),
        in_specs=[
            tile, tile, tile, tile,
            ml(0), ml(1), ml(2), ml(3),
            pl.BlockSpec((1, 2 * D_MODEL), lambda i: (0, 0)),
            pl.BlockSpec((1, D_MODEL), lambda i: (0, 0)),
            wspec, wspec, wspec,
        ],
        out_specs=[
            tile, tile, tile,
            pl.BlockSpec((tm, 2 * D_MODEL), lambda i: (i, 0)),
            tile, tile, tile, tile,
            pl.BlockSpec((8, D_MODEL), lambda i: (0, 0)),
        ],
        out_shape=[
            _sds((T, D_MODEL), F32), _sds((T, D_MODEL), F32), _sds((T, D_MODEL), F32),
            _sds((T, 2 * D_MODEL), BF16),
            _sds((T, D_MODEL), BF16), _sds((T, D_MODEL), BF16), _sds((T, D_MODEL), BF16), _sds((T, D_MODEL), BF16),
            _sds((8, D_MODEL), F32),
        ],
        compiler_params=_params(("arbitrary",), 56),
    )(x, target, z_rnn, z_attn, proj, proj, proj, proj, b_gate, g_post, w_rnn_out, w_attn_out, w_out)


def _matmul_tn(a, b, name, nblk, blocked):
    T, M = a.shape
    N = b.shape[1]
    tn = N // nblk
    tk = min(512, T)
    if blocked:
        out_spec, out_shape = pl.BlockSpec((None, M, tn), lambda n, t: (n, 0, 0)), _sds((nblk, M, tn), F32)
    else:
        out_spec, out_shape = pl.BlockSpec((M, tn), lambda n, t: (0, n)), _sds((M, N), F32)

    def body(a_ref, b_ref, o_ref):
        @pl.when(pl.program_id(1) == 0)
        def _():
            o_ref[...] = jnp.zeros_like(o_ref)

        o_ref[...] += _dot_tn(a_ref[...], b_ref[...])

    return pl.pallas_call(
        body,
        name=name,
        grid=(nblk, T // tk),
        in_specs=[pl.BlockSpec((tk, M), lambda n, t: (t, 0)), pl.BlockSpec((tk, tn), lambda n, t: (t, n))],
        out_specs=out_spec,
        out_shape=out_shape,
        compiler_params=_params(("parallel", "arbitrary"), 48),
    )(a, b)


def _dh_bwd(dproj, w_in_g, x, dyx, g_pre, token):
    T = x.shape[0]
    tm = min(512, T)

    def body(dp_ref, w_ref, x_ref, dyx_ref, g_ref, token_ref, gx_ref, dg_ref, acc):
        i, k = pl.program_id(0), pl.program_id(1)

        @pl.when(jnp.logical_and(i == 0, k == 0))
        def _():
            dg_ref[...] = jnp.zeros_like(dg_ref)

        part = _dot_nt(dp_ref[...], w_ref[...])

        @pl.when(k == 0)
        def _():
            acc[...] = part

        @pl.when(k > 0)
        def _():
            acc[...] += part

        @pl.when(k == N_CHIPS - 1)
        def _():
            xv = x_ref[...]
            dh = acc[...]
            rstd = lax.rsqrt(jnp.mean(xv * xv, axis=-1, keepdims=True) + EPS)
            nx = xv * rstd
            dhg = dh * g_ref[...]
            gx_ref[...] = dyx_ref[...] + rstd * (dhg - nx * jnp.mean(dhg * nx, axis=-1, keepdims=True))
            dg_ref[pl.ds(0, 1), :] += jnp.sum(dh * nx, axis=0, keepdims=True)

    tile = pl.BlockSpec((tm, D_MODEL), lambda i, k: (i, 0))
    return pl.pallas_call(
        body,
        name="dh_bwd",
        grid=(T // tm, N_CHIPS),
        in_specs=[
            pl.BlockSpec((tm, W_IN_SHARD), lambda i, k: (i, k)),
            pl.BlockSpec((None, D_MODEL, W_IN_SHARD), lambda i, k: (k, 0, 0)),
            tile, tile,
            pl.BlockSpec((1, D_MODEL), lambda i, k: (0, 0)),
            pl.BlockSpec((8, 128), lambda i, k: (0, 0)),
        ],
        out_specs=[tile, pl.BlockSpec((8, D_MODEL), lambda i, k: (0, 0))],
        out_shape=[_sds((T, D_MODEL), F32), _sds((8, D_MODEL), F32)],
        scratch_shapes=[pltpu.VMEM((tm, D_MODEL), F32)],
        compiler_params=_params(("arbitrary", "arbitrary"), 48),
    )(dproj, w_in_g, x, dyx, g_pre, token)


ELEMENTWISE_TILE_BYTES = MIB


def _row_tile(rows, cols):
    for t in (512, 256, 128, 64, 32, 16, 8):
        if rows % t == 0 and t * cols * 4 <= ELEMENTWISE_TILE_BYTES:
            return t
    return rows


def _pair_sum(g, got, core, name):
    nch, R, C = g.shape
    h = R // 2
    tr = _row_tile(h, C)
    nt = h // tr

    def body(c_ref, g_ref, got_ref, p_ref, pb_ref):
        s = g_ref[...] + got_ref[...]
        p_ref[...] = s
        pb_ref[...] = s.astype(BF16)

    blk = pl.BlockSpec((None, tr, C), lambda j, i, c_ref: (j, i, 0))
    return pl.pallas_call(
        body,
        name=name,
        grid_spec=pltpu.PrefetchScalarGridSpec(
            num_scalar_prefetch=1,
            grid=(nch, nt),
            in_specs=[pl.BlockSpec((None, tr, C), lambda j, i, c_ref: (j, c_ref[0] * nt + i, 0)), blk],
            out_specs=[blk, blk],
        ),
        out_shape=[_sds((nch, h, C), F32), _sds((nch, h, C), BF16)],
        compiler_params=_params(("parallel", "parallel"), 48),
    )(core, g, got)


def _chip_sum(p, got, chip_core, name):
    _, h, C = p.shape
    tr = _row_tile(h, C)
    nt = h // tr

    def body(jc_ref, p_ref, g0_ref, g1_ref, g2_ref, o_ref):
        o_ref[...] = ((p_ref[...] + g0_ref[...].astype(F32)) + g1_ref[...].astype(F32)) + g2_ref[...].astype(F32)

    rel = lambda r: pl.BlockSpec((None, tr, C), lambda i, jc_ref: (r, i, 0))
    return pl.pallas_call(
        body,
        name=name,
        grid_spec=pltpu.PrefetchScalarGridSpec(
            num_scalar_prefetch=1,
            grid=(nt,),
            in_specs=[pl.BlockSpec((None, tr, C), lambda i, jc_ref: (jc_ref[0], i, 0)), rel(0), rel(1), rel(2)],
            out_specs=pl.BlockSpec((tr, C), lambda i, jc_ref: (jc_ref[1] * nt + i, 0)),
        ),
        out_shape=_sds((2 * h, C), F32),
        compiler_params=_params(("parallel",), 48),
    )(chip_core, p, got, got, got)


def _place_shards(shards, chip, name):
    n = len(shards)
    tiles = [_row_tile(s.shape[0], s.shape[1]) for s in shards]
    steps = max(s.shape[0] // t for s, t in zip(shards, tiles))
    tiles = [s.shape[0] // steps for s in shards]

    def body(j_ref, *refs):
        for a in range(n):
            refs[n + a][...] = refs[a][...].astype(BF16)

    return pl.pallas_call(
        body,
        name=name,
        grid_spec=pltpu.PrefetchScalarGridSpec(
            num_scalar_prefetch=1,
            grid=(steps,),
            in_specs=[pl.BlockSpec((t, s.shape[1]), lambda i, j_ref: (i, 0)) for s, t in zip(shards, tiles)],
            out_specs=[pl.BlockSpec((None, t, s.shape[1]), lambda i, j_ref: (j_ref[0], i, 0))
                       for s, t in zip(shards, tiles)],
        ),
        out_shape=[_sds((N_CHIPS,) + s.shape, BF16) for s in shards],
        compiler_params=_params(("parallel",), 48),
    )(chip, *shards)


def _adamw(w, g, m, v, name):
    R, C = w.shape
    tr = _row_tile(R, C)
    c1 = 1.0 - ADAM_B1 ** ADAM_STEP
    c2 = 1.0 - ADAM_B2 ** ADAM_STEP

    def body(w_ref, g_ref, m_ref, v_ref, d_ref, nm_ref, nv_ref):
        g = g_ref[...]
        nm = ADAM_B1 * m_ref[...] + (1.0 - ADAM_B1) * g
        nv = ADAM_B2 * v_ref[...] + (1.0 - ADAM_B2) * (g * g)
        nm_ref[...] = nm
        nv_ref[...] = nv
        d_ref[...] = (-ADAM_LR) * ((nm / c1) / (jnp.sqrt(nv / c2) + ADAM_EPS) + ADAM_WD * w_ref[...])

    spec = pl.BlockSpec((tr, C), lambda i: (i, 0))
    return pl.pallas_call(
        body, name=name, grid=(R // tr,), in_specs=[spec] * 4, out_specs=[spec] * 3,
        out_shape=[_sds((R, C), F32)] * 3, compiler_params=_params(("parallel",), 48),
    )(w, g, m, v)


def _place():
    return lax.axis_index("x"), lax.axis_index("y"), lax.axis_index("c")


def _chip_of(x, y, r):
    return (x ^ (r >> 1), y ^ (r & 1))


ANY = pl.BlockSpec(memory_space=pl.ANY)


def _gather_weights(placed, cw8):
    nbig = len(placed)
    halves = [s.shape[1] // 2 for s in placed]
    pieces = [4, 1, 1, 1]
    rows = [h // p for h, p in zip(halves, pieces)]
    order = [(a, q) for q in range(max(pieces)) for a in range(nbig) if q < pieces[a]]
    ici_sem = {(a, q, r): 3 * i + (r - 1) for i, (a, q) in enumerate(order) for r in (1, 2, 3)}
    cw_sem = {r: 3 * len(order) + (r - 1) for r in (1, 2, 3)}
    d2d_sem = {key: 3 * len(order) + 3 + k for key, k in ici_sem.items()}
    nsem = 6 * len(order) + 3

    def body(pin_ref, pr_ref, pa_ref, po_ref, cw_ref, gin_ref, gr_ref, ga_ref, go_ref, gcw_ref, send_sems, recv_sems):
        x, y, c = _place()
        j = 2 * x + y
        dsts = [gin_ref, gr_ref, ga_ref, go_ref]

        def piece_rows(a, q, core):
            return pl.ds(pl.multiple_of(core * halves[a] + q * rows[a], 16), rows[a])

        def ici(a, q, r):
            tx, ty = _chip_of(x, y, r)
            k = ici_sem[(a, q, r)]
            region = dsts[a].at[j, piece_rows(a, q, c), :]
            return pltpu.make_async_remote_copy(
                src_ref=region, dst_ref=region, send_sem=send_sems.at[k], recv_sem=recv_sems.at[k],
                device_id=(tx, ty, c), device_id_type=MESH)

        def ici_landed(a, q, r):
            tx, ty = _chip_of(x, y, r)
            k = ici_sem[(a, q, r)]
            region = dsts[a].at[2 * tx + ty, piece_rows(a, q, c), :]
            return pltpu.make_async_remote_copy(
                src_ref=region, dst_ref=region, send_sem=send_sems.at[k], recv_sem=recv_sems.at[k],
                device_id=(tx, ty, c), device_id_type=MESH)

        def d2d(a, q, r, core):
            tx, ty = _chip_of(x, y, r)
            k = d2d_sem[(a, q, r)]
            region = dsts[a].at[2 * tx + ty, piece_rows(a, q, core), :]
            return pltpu.make_async_remote_copy(
                src_ref=region, dst_ref=region, send_sem=send_sems.at[k], recv_sem=recv_sems.at[k],
                device_id=(x, y, 1 - c), device_id_type=MESH)

        def cw_copy(r):
            tx, ty = _chip_of(x, y, r)
            k = cw_sem[r]
            return pltpu.make_async_remote_copy(
                src_ref=cw_ref, dst_ref=gcw_ref.at[j], send_sem=send_sems.at[k], recv_sem=recv_sems.at[k],
                device_id=(tx, ty, c), device_id_type=MESH)

        def cw_landed(r):
            tx, ty = _chip_of(x, y, r)
            k = cw_sem[r]
            region = gcw_ref.at[2 * tx + ty]
            return pltpu.make_async_remote_copy(
                src_ref=region, dst_ref=region, send_sem=send_sems.at[k], recv_sem=recv_sems.at[k],
                device_id=(tx, ty, c), device_id_type=MESH)

        first = [ici(a, q, r) for (a, q) in order for r in (1, 2, 3)] + [cw_copy(r) for r in (1, 2, 3)]
        for cp in first:
            cp.start()
        passed = []
        for (a, q) in order:
            for r in (1, 2, 3):
                ici_landed(a, q, r).wait_recv()
                cp = d2d(a, q, r, c)
                cp.start()
                passed.append(cp)
        for r in (1, 2, 3):
            cw_landed(r).wait_recv()
        for (a, q) in order:
            for r in (1, 2, 3):
                d2d(a, q, r, 1 - c).wait_recv()
        for cp in first + passed:
            cp.wait_send()

    return pl.pallas_call(
        body,
        name="gather_weights",
        in_specs=[ANY] * 5,
        out_specs=[ANY] * 5,
        out_shape=[_sds(s.shape, s.dtype) for s in placed] + [_sds((N_CHIPS,) + cw8.shape, cw8.dtype)],
        input_output_aliases={a: a for a in range(nbig)},
        scratch_shapes=[pltpu.SemaphoreType.DMA((nsem,)), pltpu.SemaphoreType.DMA((nsem,))],
    )(*placed, cw8)


D2D_PIECE_ROWS = 64


def _pair_exchange(grads, name):
    n = len(grads)
    halves = [g.shape[1] // 2 for g in grads]

    def body(*refs):
        g_refs, got_refs = refs[0:n], refs[n:2 * n]
        send_sems, recv_sems = refs[2 * n:]
        x, y, c = _place()

        def copy(a, src, dst):
            return pltpu.make_async_remote_copy(
                src_ref=src, dst_ref=dst, send_sem=send_sems.at[a], recv_sem=recv_sems.at[a],
                device_id=(x, y, 1 - c), device_id_type=MESH)

        for a in range(n):
            for jj in range(N_CHIPS):
                for q in range(halves[a] // D2D_PIECE_ROWS):
                    src_rows = pl.ds(pl.multiple_of((1 - c) * halves[a] + q * D2D_PIECE_ROWS, 8), D2D_PIECE_ROWS)
                    dst_rows = pl.ds(q * D2D_PIECE_ROWS, D2D_PIECE_ROWS)
                    copy(a, g_refs[a].at[jj, src_rows, :], got_refs[a].at[jj, dst_rows, :]).start()
        for a in range(n):
            sent = g_refs[a].at[:, pl.ds(pl.multiple_of((1 - c) * halves[a], 8), halves[a]), :]
            copy(a, sent, got_refs[a]).wait()

    return pl.pallas_call(
        body,
        name=name,
        in_specs=[ANY] * n,
        out_specs=[ANY] * n,
        out_shape=[_sds((N_CHIPS, h, g.shape[2]), F32) for g, h in zip(grads, halves)],
        scratch_shapes=[pltpu.SemaphoreType.DMA((n,)), pltpu.SemaphoreType.DMA((n,))],
    )(*grads)


HBM = pl.BlockSpec(memory_space=pltpu.HBM)
SEM = pl.BlockSpec(memory_space=pltpu.SEMAPHORE)
DATAFLOW = pltpu.SideEffectType.DATAFLOW_SIDE_EFFECTING


def _chip_copy(p_refs, land_refs, send_sems, recv_sems, a, r):
    x, y, c = _place()
    tx, ty = _chip_of(x, y, r)
    k = a * 3 + (r - 1)
    return pltpu.make_async_remote_copy(
        src_ref=p_refs[a].at[2 * tx + ty], dst_ref=land_refs[a].at[r - 1],
        send_sem=send_sems.at[k], recv_sem=recv_sems.at[k], device_id=(tx, ty, c), device_id_type=MESH)


def _chip_exchange_start(psums, name):
    n = len(psums)
    lands = [lax.empty((3,) + p.shape[1:], p.dtype) for p in psums]

    def body(*refs):
        p_refs, land_refs = refs[0:n], refs[n:2 * n]
        send_sems, recv_sems, token = refs[2 * n], refs[2 * n + 1], refs[-1]
        for a in range(n):
            for r in (1, 2, 3):
                _chip_copy(p_refs, land_refs, send_sems, recv_sems, a, r).start()
        token[...] = jnp.zeros_like(token)

    hbm = lambda t: pltpu.HBM(t.shape, t.dtype)
    keep = lambda t: pltpu.with_memory_space_constraint(t, pltpu.HBM)
    outs = pl.pallas_call(
        body,
        name=name,
        in_specs=[HBM] * (2 * n),
        out_specs=(SEM, SEM, *[HBM] * (2 * n), pl.BlockSpec(memory_space=pltpu.VMEM)),
        out_shape=(pltpu.SemaphoreType.DMA((3 * n,)), pltpu.SemaphoreType.DMA((3 * n,)),
                   *[hbm(p) for p in psums], *[hbm(l) for l in lands], _sds((8, 128), F32)),
        input_output_aliases={i: 2 + i for i in range(2 * n)},
        compiler_params=pltpu.CompilerParams(has_side_effects=DATAFLOW),
    )(*[keep(p) for p in psums], *[keep(l) for l in lands])
    return outs[0], outs[1], list(outs[2:2 + n]), list(outs[2 + n:2 + 2 * n]), outs[-1]


def _chip_exchange_wait(send_sems, recv_sems, p_thru, land_thru, after, name):
    n = len(p_thru)

    def body(*refs):
        p_refs, land_refs = refs[0:n], refs[n:2 * n]
        send_sems, recv_sems = refs[2 * n], refs[2 * n + 1]
        for a in range(n):
            for r in (1, 2, 3):
                copy = _chip_copy(p_refs, land_refs, send_sems, recv_sems, a, r)
                copy.wait_send()
                copy.wait_recv()

    hbm = lambda t: pltpu.HBM(t.shape, t.dtype)
    outs = pl.pallas_call(
        body,
        name=name,
        in_specs=[HBM] * (2 * n) + [SEM, SEM, ANY],
        out_specs=[HBM] * (2 * n),
        out_shape=[hbm(p) for p in p_thru] + [hbm(l) for l in land_thru],
        input_output_aliases={i: i for i in range(2 * n)},
        compiler_params=pltpu.CompilerParams(has_side_effects=DATAFLOW),
    )(*p_thru, *land_thru, send_sems, recv_sems, after)
    return list(outs[n:2 * n])


def _pair_share(fulls):
    n = len(fulls)
    halves = [f.shape[0] // 2 for f in fulls]

    def body(*refs):
        full_refs = refs[n:2 * n]
        send_sems, recv_sems = refs[2 * n:]
        x, y, c = _place()

        def half_of(a, core):
            return full_refs[a].at[pl.ds(pl.multiple_of(core * halves[a], 8), halves[a]), :]

        def remote(a, src, dst):
            return pltpu.make_async_remote_copy(
                src_ref=src, dst_ref=dst, send_sem=send_sems.at[a], recv_sem=recv_sems.at[a],
                device_id=(x, y, 1 - c), device_id_type=MESH)

        for a in range(n):
            for q in range(halves[a] // D2D_PIECE_ROWS):
                piece = full_refs[a].at[
                    pl.ds(pl.multiple_of(c * halves[a] + q * D2D_PIECE_ROWS, 8), D2D_PIECE_ROWS), :]
                remote(a, piece, piece).start()
        for a in range(n):
            remote(a, half_of(a, c), half_of(a, c)).wait_send()
            remote(a, half_of(a, 1 - c), half_of(a, 1 - c)).wait_recv()

    return pl.pallas_call(
        body,
        name="pair_share",
        in_specs=[ANY] * n,
        out_specs=[ANY] * n,
        out_shape=[_sds(f.shape, F32) for f in fulls],
        input_output_aliases={a: a for a in range(n)},
        scratch_shapes=[pltpu.SemaphoreType.DMA((n,)), pltpu.SemaphoreType.DMA((n,))],
    )(*fulls)


def _allreduce_small(s):
    R, C = s.shape

    def body(s_ref, o_ref, sib, chips, send_sems, recv_sems):
        x, y, c = _place()
        j = 2 * x + y
        def to_sib(src, dst):
            return pltpu.make_async_remote_copy(
                src_ref=src, dst_ref=dst, send_sem=send_sems.at[0], recv_sem=recv_sems.at[0],
                device_id=(x, y, 1 - c), device_id_type=MESH)

        for q in range(R // 8):
            to_sib(s_ref.at[pl.ds(8 * q, 8), :], sib.at[pl.ds(8 * q, 8), :]).start()
        to_sib(s_ref, sib).wait()
        chips[j] = s_ref[...] + sib[...]
        sends = []
        for r in (1, 2, 3):
            tx, ty = _chip_of(x, y, r)
            cp = pltpu.make_async_remote_copy(
                src_ref=chips.at[j], dst_ref=chips.at[j], send_sem=send_sems.at[r], recv_sem=recv_sems.at[r],
                device_id=(tx, ty, c), device_id_type=MESH)
            cp.start()
            sends.append(cp)
        for r in (1, 2, 3):
            tx, ty = _chip_of(x, y, r)
            region = chips.at[2 * tx + ty]
            pltpu.make_async_remote_copy(
                src_ref=region, dst_ref=region, send_sem=send_sems.at[r], recv_sem=recv_sems.at[r],
                device_id=(tx, ty, c), device_id_type=MESH).wait_recv()
        for cp in sends:
            cp.wait_send()
        o_ref[...] = (chips[0] + chips[1]) + (chips[2] + chips[3])

    return pl.pallas_call(
        body,
        name="allreduce_small",
        in_specs=[pl.BlockSpec(memory_space=pltpu.VMEM)],
        out_specs=pl.BlockSpec(memory_space=pltpu.VMEM),
        out_shape=_sds((R, C), F32),
        scratch_shapes=[pltpu.VMEM((R, C), F32), pltpu.VMEM((N_CHIPS, R, C), F32),
                        pltpu.SemaphoreType.DMA((4,)), pltpu.SemaphoreType.DMA((4,))],
    )(s)


def _block_diag(w):
    w4 = w.reshape(4, 4, RNN_BLOCK_W, RNN_BLOCK_W)
    eye = jnp.eye(4, dtype=w.dtype)
    return jnp.einsum("jaik,ab->jaibk", w4, eye).reshape(4, RNN_TILE, RNN_TILE)


def _block_diag_part(d):
    d5 = d.reshape(4, 4, RNN_BLOCK_W, 4, RNN_BLOCK_W)
    return jnp.stack([d5[:, a, :, a, :] for a in range(4)], axis=1).reshape(RNN_BLOCKS, RNN_BLOCK_W, RNN_BLOCK_W)


def _local_grads(x, target, g_pre, w_in_g, b_gate, conv_w, conv_b, w_rg_a, b_rg_a, w_rg_x, b_rg_x, lam, sinks,
                 w_rnn_out, w_attn_out, w_out, g_post, on_out_grads, on_w_in_grad):
    wa_bd = _block_diag(w_rg_a).astype(BF16)
    wx_bd = _block_diag(w_rg_x).astype(BF16)
    b_a = b_rg_a.reshape(1, D_RNN)
    b_x = b_rg_x.reshape(1, D_RNN)

    proj, h = _proj_fwd(x, g_pre, w_in_g)
    y_rnn, z_rnn = _rnn_fwd(proj, conv_w, conv_b, wa_bd, wx_bd, b_a, b_x, lam)
    y_attn, z_attn = _attn_fwd(proj, sinks)
    dyx, dz_rnn, dz_attn, dml, merged, dout, dbr_rnn, dbr_attn, head_small = _head(
        x, target, z_rnn, z_attn, proj, b_gate, g_post, w_rnn_out, w_attn_out, w_out)
    dw_out = _matmul_tn(merged, dout, "dw_out", 2, False)
    dw_rnn_out = _matmul_tn(z_rnn, dbr_rnn, "dw_rnn_out", 2, False)
    dw_attn_out = _matmul_tn(z_attn, dbr_attn, "dw_attn_out", 2, False)
    shard_rows = lambda d: d.reshape(N_CHIPS, OUT_SHARD, D_MODEL)
    token = on_out_grads([shard_rows(dw_rnn_out), shard_rows(dw_attn_out), shard_rows(dw_out)])
    dq, dk, dv, dag, attn_small = _attn_bwd(proj, y_attn, dz_attn, sinks, token)
    drx, drg, dwa_t, dwx_t, rnn_small = _rnn_bwd(proj, y_rnn, dz_rnn, conv_w, conv_b, wa_bd, wx_bd, b_a, b_x, lam)
    dproj = jnp.concatenate([drx, drg, dq, dk.astype(BF16), dv.astype(BF16), dag, dml], axis=1)
    token = on_w_in_grad(_matmul_tn(h, dproj, "dw_in", N_CHIPS, True))
    grad_x, dh_small = _dh_bwd(dproj, w_in_g, x, dyx, g_pre, token)
    small = jnp.concatenate([rnn_small, head_small, dh_small + attn_small,
                             _block_diag_part(dwa_t).reshape(64, 1024), _block_diag_part(dwx_t).reshape(64, 1024)], axis=0)
    return grad_x, small


ROW_LOSS = 11


def _rows8(parts):
    out = None
    for r, a in parts:
        p = jnp.pad(a, ((r, 8 - r - a.shape[0]), (0, 1024 - a.shape[1])))
        out = p if out is None else out + p
    return out


def _pack_small(p):
    g0 = _rows8([(0, p["b_rg_a"].reshape(1, 1024)), (1, p["b_rg_x"].reshape(1, 1024)), (2, p["lru_lambda"]),
                 (3, p["conv_b"]), (4, p["conv_w"][0])])
    g1 = _rows8([(0, p["post_norm_g"]), (1, p["b_gate"].reshape(2, 1024))])
    g2 = _rows8([(0, p["pre_norm_g"]), (1, p["attn_sinks"])])
    return jnp.concatenate([g0, g1, g2, p["w_rg_a"].reshape(64, 1024), p["w_rg_x"].reshape(64, 1024)], axis=0)


def _unpack_small(s, conv_cols):
    return {
        "b_rg_a": s[0:1].reshape(1, 16, 64), "b_rg_x": s[1:2].reshape(1, 16, 64), "lru_lambda": s[2:3],
        "conv_b": s[3:4], "conv_w": s[4:8, 0:conv_cols].reshape(1, CONV_W, conv_cols),
        "post_norm_g": s[8:9], "b_gate": s[9:11].reshape(1, 2048),
        "pre_norm_g": s[16:17], "attn_sinks": s[17:18, 0:N_Q_HEADS],
        "w_rg_a": s[24:88].reshape(1, 16, 64, 64), "w_rg_x": s[88:152].reshape(1, 16, 64, 64),
    }


WEIGHTS = ["pre_norm_g", "w_in", "b_gate", "conv_w", "conv_b", "w_rg_a", "b_rg_a", "w_rg_x", "b_rg_x", "lru_lambda",
           "attn_sinks", "w_rnn_out", "w_attn_out", "w_out", "post_norm_g"]
BIG = ["w_in", "w_rnn_out", "w_attn_out", "w_out"]


def kernel(x, pre_norm_g, w_in, b_gate, conv_w, conv_b, w_rg_a, b_rg_a, w_rg_x, b_rg_x, lru_lambda, attn_sinks, w_rnn_out, w_attn_out, w_out, post_norm_g, loss_target, m_pre_norm_g, m_w_in, m_b_gate, m_conv_w, m_conv_b, m_w_rg_a, m_b_rg_a, m_w_rg_x, m_b_rg_x, m_lru_lambda, m_attn_sinks, m_w_rnn_out, m_w_attn_out, m_w_out, m_post_norm_g, v_pre_norm_g, v_w_in, v_b_gate, v_conv_w, v_conv_b, v_w_rg_a, v_b_rg_a, v_w_rg_x, v_b_rg_x, v_lru_lambda, v_attn_sinks, v_w_rnn_out, v_w_attn_out, v_w_out, v_post_norm_g):
    w = dict(pre_norm_g=pre_norm_g, w_in=w_in, b_gate=b_gate, conv_w=conv_w, conv_b=conv_b, w_rg_a=w_rg_a,
             b_rg_a=b_rg_a, w_rg_x=w_rg_x, b_rg_x=b_rg_x, lru_lambda=lru_lambda, attn_sinks=attn_sinks,
             w_rnn_out=w_rnn_out, w_attn_out=w_attn_out, w_out=w_out, post_norm_g=post_norm_g)
    m = dict(pre_norm_g=m_pre_norm_g, w_in=m_w_in, b_gate=m_b_gate, conv_w=m_conv_w, conv_b=m_conv_b, w_rg_a=m_w_rg_a,
             b_rg_a=m_b_rg_a, w_rg_x=m_w_rg_x, b_rg_x=m_b_rg_x, lru_lambda=m_lru_lambda, attn_sinks=m_attn_sinks,
             w_rnn_out=m_w_rnn_out, w_attn_out=m_w_attn_out, w_out=m_w_out, post_norm_g=m_post_norm_g)
    v = dict(pre_norm_g=v_pre_norm_g, w_in=v_w_in, b_gate=v_b_gate, conv_w=v_conv_w, conv_b=v_conv_b, w_rg_a=v_w_rg_a,
             b_rg_a=v_b_rg_a, w_rg_x=v_w_rg_x, b_rg_x=v_b_rg_x, lru_lambda=v_lru_lambda, attn_sinks=v_attn_sinks,
             w_rnn_out=v_w_rnn_out, w_attn_out=v_w_attn_out, w_out=v_w_out, post_norm_g=v_post_norm_g)
    chip = 2 * lax.axis_index("x") + lax.axis_index("y")

    chip_idx = chip.astype(jnp.int32).reshape(1)
    chip_core = jnp.stack([chip, lax.axis_index("c")]).astype(jnp.int32)
    cw8 = jnp.pad(conv_w[0], ((0, 8 - CONV_W), (0, 0)))
    placed = _place_shards([w_in[0], w_rnn_out[0], w_attn_out[0], w_out[0]], chip_idx, "place_shards")
    win_g, wr_g, wa_g, wo_g, cw_g = _gather_weights(placed, cw8)
    cw_g = lax.dynamic_update_slice_in_dim(cw_g, cw8[None], chip, axis=0)
    conv_w_full = jnp.transpose(cw_g[:, 0:CONV_W, :], (1, 0, 2)).reshape(CONV_W, D_RNN)

    core_idx = lax.axis_index("c").astype(jnp.int32).reshape(1)
    started = {}

    def start_reduction(tag, grads):
        got = _pair_exchange(grads, "pair_exchange_" + tag)
        sums = [_pair_sum(g, o, core_idx, "pair_sum_%s_%d" % (tag, a)) for a, (g, o) in enumerate(zip(grads, got))]
        send_sems, recv_sems, p_thru, land_thru, token = _chip_exchange_start(
            [pb for _, pb in sums], "chip_exchange_start_" + tag)
        started[tag] = ([p for p, _ in sums], send_sems, recv_sems, p_thru, land_thru)
        return token

    def end_reduction(tag, after):
        psums, send_sems, recv_sems, p_thru, land_thru = started[tag]
        landed = _chip_exchange_wait(send_sems, recv_sems, p_thru, land_thru, after, "chip_exchange_wait_" + tag)
        return [_chip_sum(p, l, chip_core, "chip_sum_%s_%d" % (tag, a)) for a, (p, l) in enumerate(zip(psums, landed))]

    grad_x, small = _local_grads(
        x[0], loss_target[0], pre_norm_g, win_g, b_gate, conv_w_full, conv_b, w_rg_a[0], b_rg_a[0], w_rg_x[0],
        b_rg_x[0], lru_lambda, attn_sinks[0], wr_g.reshape(D_MODEL, D_MODEL), wa_g.reshape(D_MODEL, D_MODEL),
        wo_g.reshape(D_MODEL, D_MODEL), post_norm_g,
        on_out_grads=lambda grads: start_reduction("out", grads),
        on_w_in_grad=lambda grad: start_reduction("in", [grad]))

    halves = end_reduction("in", grad_x) + end_reduction("out", grad_x)
    gbig = dict(zip(BIG, _pair_share(halves)))

    small_sum = _allreduce_small(small)
    total_loss = small_sum[ROW_LOSS, 0]
    gsmall = _unpack_small(small_sum, D_RNN)
    conv_shard = D_RNN // N_CHIPS
    gsmall["conv_w"] = lax.dynamic_slice_in_dim(gsmall["conv_w"], chip * conv_shard, conv_shard, axis=2)

    grads, delta, new_m, new_v = {}, {}, {}, {}
    for n in BIG:
        grads[n] = gbig[n][None]
        d, nm, nv = _adamw(w[n][0], gbig[n], m[n][0], v[n][0], "adamw_" + n)
        delta[n], new_m[n], new_v[n] = d[None], nm[None], nv[None]
    pick = lambda t: {k: t[k] for k in gsmall}
    d, nm, nv = _adamw(_pack_small(pick(w)), _pack_small(gsmall), _pack_small(pick(m)), _pack_small(pick(v)),
                       "adamw_small")
    ud, um, uv = _unpack_small(d, conv_shard), _unpack_small(nm, conv_shard), _unpack_small(nv, conv_shard)
    for n in gsmall:
        grads[n] = gsmall[n].reshape(w[n].shape)
        delta[n] = ud[n].reshape(w[n].shape)
        new_m[n] = um[n].reshape(w[n].shape)
        new_v[n] = uv[n].reshape(w[n].shape)

    return (total_loss, grad_x[None], *[grads[n] for n in WEIGHTS], *[delta[n] for n in WEIGHTS],
            *[new_m[n] for n in WEIGHTS], *[new_v[n] for n in WEIGHTS])
```

```python
import functools
import math

import jax
import jax.numpy as jnp
from jax import lax
from jax.experimental import pallas as pl
from jax.experimental.pallas import tpu as pltpu

F32 = jnp.float32
BF16 = jnp.bfloat16

D_MODEL = 1024
D_RNN = 1024
RNN_BLOCKS = 16
RNN_BLOCK_W = 64
CONV_W = 4
LRU_C = 8.0
N_Q_HEADS = 16
N_KV_HEADS = 4
GROUP = 4
HEAD_DIM = 64
D_KV = 256
BLOCK = 128
ALIBI_MAX_BIAS = 8.0
EPS = 1e-6
D_IN = 6656
N_CHIPS = 4
W_IN_SHARD = D_IN // N_CHIPS
OUT_SHARD = D_MODEL // N_CHIPS
ADAM_LR = 0.001
ADAM_B1 = 0.9
ADAM_B2 = 0.999
ADAM_EPS = 1e-08
ADAM_WD = 0.01
ADAM_STEP = 10
NEG_BIG = -1e30
MIB = 1 << 20

COL_RNN_X = 0
COL_RNN_GATE = 4
COL_Q = 8
COL_K = 12
COL_V = 13
COL_ATTN_GATE = 14
COL_MERGE = 18

RNN_TILE = 256
RNN_CHUNK = 256
SMALL_ROWS = 152
MESH = pl.DeviceIdType.MESH


def _sds(shape, dtype):
    return jax.ShapeDtypeStruct(shape, dtype)


def _params(sem=None, vmem_mib=None):
    kw = {}
    if sem is not None:
        kw["dimension_semantics"] = sem
    if vmem_mib is not None:
        kw["vmem_limit_bytes"] = vmem_mib * MIB
    return pltpu.CompilerParams(**kw)


def _dot(a, b):
    return jnp.dot(a, b, preferred_element_type=F32)


def _dot_nt(a, b):
    return lax.dot_general(a, b, (((1,), (1,)), ((), ())), preferred_element_type=F32)


def _dot_tn(a, b):
    return lax.dot_general(a, b, (((0,), (0,)), ((), ())), preferred_element_type=F32)


def _sigmoid(x):
    return 0.5 * jnp.tanh(0.5 * x) + 0.5


def _sigmoid_small(x):
    return 1.0 / (1.0 + jnp.exp(-x))


def _softplus(x):
    return jnp.maximum(x, 0.0) + jnp.log(1.0 + jnp.exp(-jnp.abs(x)))


def _one_minus_square(a, log_a):
    return -jnp.tanh(log_a) * (a * a + 1.0)


def _proj_fwd(x, g_pre, w_in_g):
    T = x.shape[0]
    tm = min(1024, T)

    def body(x_ref, g_ref, w_ref, proj_ref, h_ref):
        @pl.when(pl.program_id(1) == 0)
        def _():
            xv = x_ref[...]
            rstd = lax.rsqrt(jnp.mean(xv * xv, axis=-1, keepdims=True) + EPS)
            h_ref[...] = ((xv * rstd) * g_ref[...]).astype(BF16)

        proj_ref[...] = _dot(h_ref[...], w_ref[...])

    return pl.pallas_call(
        body,
        name="proj_fwd",
        grid=(T // tm, N_CHIPS),
        in_specs=[
            pl.BlockSpec((tm, D_MODEL), lambda i, j: (i, 0)),
            pl.BlockSpec((1, D_MODEL), lambda i, j: (0, 0)),
            pl.BlockSpec((None, D_MODEL, W_IN_SHARD), lambda i, j: (j, 0, 0)),
        ],
        out_specs=[
            pl.BlockSpec((tm, W_IN_SHARD), lambda i, j: (i, j)),
            pl.BlockSpec((tm, D_MODEL), lambda i, j: (i, 0)),
        ],
        out_shape=[_sds((T, D_IN), F32), _sds((T, D_MODEL), BF16)],
        compiler_params=_params(("parallel", "arbitrary"), 48),
    )(x, g_pre, w_in_g)


def _shift_down(x, tail, s, row):
    n = x.shape[0]
    xs = pltpu.roll(x, s, 0)
    tail_t = jnp.tile(pltpu.roll(tail, s, 0), (n // 8, 1))
    return jnp.where(row < s, tail_t, xs)


def _shift_up(x, head, s, row):
    n = x.shape[0]
    xs = pltpu.roll(x, n - s, 0)
    head_t = jnp.tile(pltpu.roll(head, 8 - s, 0), (n // 8, 1))
    return jnp.where(row >= n - s, head_t, xs)


def _conv_taps(x, tail, row):
    return [_shift_down(x, tail, 3, row), _shift_down(x, tail, 2, row), _shift_down(x, tail, 1, row), x]


def _rglru_gates(c, wa, wx, ba, bx, lam):
    cb = c.astype(BF16)
    r = _sigmoid_small(_dot(cb, wa) + ba)
    i = _sigmoid(_dot(cb, wx) + bx)
    log_a = (-LRU_C) * r * _softplus(-lam)
    a = jnp.exp(log_a)
    mult = jnp.sqrt(_one_minus_square(a, log_a))
    return cb, r, i, a, mult


SUBLANES = 8


def _scan_down(a, u, row):
    n = a.shape[0]
    s = 1
    while s < SUBLANES:
        a_sh = jnp.where(row >= s, pltpu.roll(a, s, 0), 1.0)
        u_sh = jnp.where(row >= s, pltpu.roll(u, s, 0), 0.0)
        u = a * u_sh + u
        a = a * a_sh
        s *= 2
    while s < n:
        u = jnp.concatenate([u[:s], a[s:] * u[:n - s] + u[s:]], axis=0)
        a = jnp.concatenate([a[:s], a[s:] * a[:n - s]], axis=0)
        s *= 2
    return a, u


def _scan_up(b, u, row):
    n = b.shape[0]
    s = 1
    while s < SUBLANES:
        b_sh = jnp.where(row < n - s, pltpu.roll(b, n - s, 0), 1.0)
        u_sh = jnp.where(row < n - s, pltpu.roll(u, n - s, 0), 0.0)
        u = b * u_sh + u
        b = b * b_sh
        s *= 2
    while s < n:
        u = jnp.concatenate([b[:n - s] * u[s:] + u[:n - s], u[n - s:]], axis=0)
        b = jnp.concatenate([b[:n - s] * b[s:], b[n - s:]], axis=0)
        s *= 2
    return b, u


def _rnn_fwd(proj, conv_w, conv_b, wa_bd, wx_bd, b_a, b_x, lam):
    T = proj.shape[0]
    tc, ct = RNN_CHUNK, RNN_TILE
    nt = T // tc

    def body(x_ref, rg_ref, cw_ref, cb_ref, wa_ref, wx_ref, ba_ref, bx_ref, lam_ref, h_ref, z_ref, xtail, hcarry):
        @pl.when(pl.program_id(1) == 0)
        def _():
            xtail[...] = jnp.zeros_like(xtail)
            hcarry[...] = jnp.zeros_like(hcarry)

        row = lax.broadcasted_iota(jnp.int32, (tc, ct), 0)
        x = x_ref[...]
        taps = _conv_taps(x, xtail[...], row)
        c = cb_ref[...] + cw_ref[pl.ds(0, 1), :] * taps[0]
        for k in range(1, CONV_W):
            c = c + cw_ref[pl.ds(k, 1), :] * taps[k]
        xtail[...] = x_ref[pl.ds(tc - 8, 8), :]
        _, _, i, a, mult = _rglru_gates(c, wa_ref[...], wx_ref[...], ba_ref[...], bx_ref[...], lam_ref[...])
        u = mult * (i * c)
        a_cum, h0 = _scan_down(a, u, row)
        h = h0 + a_cum * hcarry[...]
        h_ref[...] = h
        hcarry[...] = h_ref[pl.ds(tc - 1, 1), :]
        rg = rg_ref[...]
        z_ref[...] = (h * (rg * _sigmoid(rg))).astype(BF16)

    col = lambda off: (lambda j, t: (t, off + j))
    vec = pl.BlockSpec((1, ct), lambda j, t: (0, j))
    mat = pl.BlockSpec((None, ct, ct), lambda j, t: (j, 0, 0))
    return pl.pallas_call(
        body,
        name="rnn_fwd",
        grid=(D_RNN // ct, nt),
        in_specs=[
            pl.BlockSpec((tc, ct), col(COL_RNN_X)),
            pl.BlockSpec((tc, ct), col(COL_RNN_GATE)),
            pl.BlockSpec((CONV_W, ct), lambda j, t: (0, j)),
            vec, mat, mat, vec, vec, vec,
        ],
        out_specs=[pl.BlockSpec((tc, ct), lambda j, t: (t, j)), pl.BlockSpec((tc, ct), lambda j, t: (t, j))],
        out_shape=[_sds((T, D_RNN), F32), _sds((T, D_RNN), BF16)],
        scratch_shapes=[pltpu.VMEM((8, ct), F32), pltpu.VMEM((1, ct), F32)],
        compiler_params=_params(("parallel", "arbitrary"), 32),
    )(proj, proj, conv_w, conv_b, wa_bd, wx_bd, b_a, b_x, lam)


def _rnn_bwd(proj, y_rnn, dz_rnn, conv_w, conv_b, wa_bd, wx_bd, b_a, b_x, lam):
    T = proj.shape[0]
    tc, ct = RNN_CHUNK, RNN_TILE
    nt = T // tc
    hb = tc // 8

    def body(x_ref, xh_ref, rg_ref, h_ref, hh_ref, dz_ref, cw_ref, cb_ref, wa_ref, wx_ref, ba_ref, bx_ref, lam_ref,
             dx_ref, drg_ref, dwa_ref, dwx_ref, sm_ref, lam_carry, a_carry, dc_head):
        t = pl.program_id(1)
        first_chunk = t == nt - 1

        @pl.when(t == 0)
        def _():
            lam_carry[...] = jnp.zeros_like(lam_carry)
            a_carry[...] = jnp.zeros_like(a_carry)
            dc_head[...] = jnp.zeros_like(dc_head)
            dwa_ref[...] = jnp.zeros_like(dwa_ref)
            dwx_ref[...] = jnp.zeros_like(dwx_ref)
            sm_ref[...] = jnp.zeros_like(sm_ref)

        row = lax.broadcasted_iota(jnp.int32, (tc, ct), 0)
        keep = jnp.where(first_chunk, 0.0, 1.0)
        x = x_ref[...]
        xtail = xh_ref[...] * keep
        taps = _conv_taps(x, xtail, row)
        c = cb_ref[...] + cw_ref[pl.ds(0, 1), :] * taps[0]
        for k in range(1, CONV_W):
            c = c + cw_ref[pl.ds(k, 1), :] * taps[k]
        lam = lam_ref[...]
        cb, r, i, a, mult = _rglru_gates(c, wa_ref[...], wx_ref[...], ba_ref[...], bx_ref[...], lam)
        h = h_ref[...]
        h_prev = _shift_down(h, hh_ref[...] * keep, 1, row)
        rg = rg_ref[...]
        dz = dz_ref[...]
        sg = _sigmoid(rg)
        drg_ref[...] = (dz * h * (sg * (1.0 + rg * (1.0 - sg)))).astype(BF16)
        dy = dz * (rg * sg)
        b = jnp.where(row >= tc - 1, a_carry[pl.ds(0, 1), :], pltpu.roll(a, tc - 1, 0))
        b_cum, l0 = _scan_up(b, dy, row)
        lt = l0 + b_cum * lam_carry[pl.ds(0, 1), :]
        lam_carry[...] = lt[0:8, :]
        a_carry[...] = a[0:8, :]
        ic = i * c
        dmult = lt * ic
        di = lt * mult * c
        dc = lt * mult * i
        dlog_a = a * (lt * h_prev - dmult * a / mult)
        sp = _softplus(-lam)
        dpre_r = dlog_a * ((-LRU_C) * sp) * (r * (1.0 - r))
        dpre_i = di * (i * (1.0 - i))
        dlam_row = jnp.sum(dlog_a * r, axis=0, keepdims=True) * (LRU_C * _sigmoid(-lam))
        dpr_b = dpre_r.astype(BF16)
        dpi_b = dpre_i.astype(BF16)
        dwa_ref[...] += _dot_tn(cb, dpr_b)
        dwx_ref[...] += _dot_tn(cb, dpi_b)
        dc = dc + _dot_nt(dpr_b, wa_ref[...]) + _dot_nt(dpi_b, wx_ref[...])
        head = dc_head[...]
        dx = cw_ref[pl.ds(3, 1), :] * dc
        for m in range(1, CONV_W):
            dx = dx + cw_ref[pl.ds(3 - m, 1), :] * _shift_up(dc, head, m, row)
        dx_ref[...] = dx.astype(BF16)
        dc_head[...] = dc[0:8, :]
        sm_ref[pl.ds(0, 1), :] += jnp.sum(dpre_r, axis=0, keepdims=True)
        sm_ref[pl.ds(1, 1), :] += jnp.sum(dpre_i, axis=0, keepdims=True)
        sm_ref[pl.ds(2, 1), :] += dlam_row
        sm_ref[pl.ds(3, 1), :] += jnp.sum(dc, axis=0, keepdims=True)
        for k in range(CONV_W):
            sm_ref[pl.ds(4 + k, 1), :] += jnp.sum(dc * taps[k], axis=0, keepdims=True)

    rev = lambda off: (lambda j, t: (nt - 1 - t, off + j))
    halo = lambda off: (lambda j, t: (jnp.maximum((nt - 1 - t) * hb - 1, 0), off + j))
    vec = pl.BlockSpec((1, ct), lambda j, t: (0, j))
    mat = pl.BlockSpec((None, ct, ct), lambda j, t: (j, 0, 0))
    return pl.pallas_call(
        body,
        name="rnn_bwd",
        grid=(D_RNN // ct, nt),
        in_specs=[
            pl.BlockSpec((tc, ct), rev(COL_RNN_X)),
            pl.BlockSpec((8, ct), halo(COL_RNN_X)),
            pl.BlockSpec((tc, ct), rev(COL_RNN_GATE)),
            pl.BlockSpec((tc, ct), rev(0)),
            pl.BlockSpec((8, ct), halo(0)),
            pl.BlockSpec((tc, ct), rev(0)),
            pl.BlockSpec((CONV_W, ct), lambda j, t: (0, j)),
            vec, mat, mat, vec, vec, vec,
        ],
        out_specs=[
            pl.BlockSpec((tc, ct), rev(0)),
            pl.BlockSpec((tc, ct), rev(0)),
            mat, mat,
            pl.BlockSpec((8, ct), lambda j, t: (0, j)),
        ],
        out_shape=[_sds((T, D_RNN), BF16), _sds((T, D_RNN), BF16), _sds((D_RNN // ct, ct, ct), F32),
                   _sds((D_RNN // ct, ct, ct), F32), _sds((8, D_RNN), F32)],
        scratch_shapes=[pltpu.VMEM((8, ct), F32), pltpu.VMEM((8, ct), F32), pltpu.VMEM((8, ct), F32)],
        compiler_params=_params(("parallel", "arbitrary"), 32),
    )(proj, proj, proj, y_rnn, y_rnn, dz_rnn, conv_w, conv_b, wa_bd, wx_bd, b_a, b_x, lam)


def _attn_bias():
    qi = jnp.arange(BLOCK)[:, None]
    kj = jnp.arange(BLOCK)[None, :]
    dist_cur = (qi - kj).astype(F32)
    slopes = 2.0 ** (-ALIBI_MAX_BIAS * jnp.arange(1, N_Q_HEADS + 1, dtype=F32) / N_Q_HEADS)
    slopes = slopes[:, None, None]
    prev = jnp.where(kj > qi, -slopes * (dist_cur + float(BLOCK)), NEG_BIG)
    cur = jnp.where(kj <= qi, -slopes * dist_cur, NEG_BIG)
    later = jnp.concatenate([prev, cur], axis=-1)
    first = jnp.concatenate([jnp.full_like(prev, NEG_BIG), cur], axis=-1)
    return jnp.stack([first, later])


def _attn_exps(s_prev, s_cur, sink, bias):
    s_prev = s_prev + bias[:, 0:BLOCK]
    s_cur = s_cur + bias[:, BLOCK:2 * BLOCK]
    m = jnp.maximum(jnp.max(jnp.maximum(s_prev, s_cur), axis=-1, keepdims=True), sink)
    p_prev = jnp.exp(s_prev - m)
    p_cur = jnp.exp(s_cur - m)
    total = jnp.sum(p_prev + p_cur, axis=-1, keepdims=True) + jnp.exp(sink - m)
    return p_prev, p_cur, 1.0 / total, m + jnp.log(total)


def _attn_probs(s_prev, s_cur, sink, bias, lse):
    p_prev = jnp.exp((s_prev + bias[:, 0:BLOCK]) - lse)
    p_cur = jnp.exp((s_cur + bias[:, BLOCK:2 * BLOCK]) - lse)
    return p_prev, p_cur, jnp.exp(sink - lse)


def _stack_heads(ref_or_val, hk, dtype):
    parts = [ref_or_val[:, (GROUP * hk + g) * HEAD_DIM:(GROUP * hk + g + 1) * HEAD_DIM] for g in range(GROUP)]
    return jnp.concatenate(parts, axis=0).astype(dtype)


ATTN_SCALE = HEAD_DIM ** -0.5


def _bias_spec():
    return pl.BlockSpec((None, N_Q_HEADS, BLOCK, 2 * BLOCK), lambda i: (jnp.minimum(i, 1), 0, 0, 0))


def _attn_fwd(proj, sinks, bias):
    T = proj.shape[0]
    nb = T // BLOCK

    def body(sink_ref, bias_ref, q_ref, kp_ref, kc_ref, vp_ref, vc_ref, ag0_ref, ag1_ref, y_ref, z_ref, lse_ref):
        kvs = [slice(hk * HEAD_DIM, (hk + 1) * HEAD_DIM) for hk in range(N_KV_HEADS)]
        qgs = [(_stack_heads(q_ref, hk, F32) * ATTN_SCALE).astype(BF16) for hk in range(N_KV_HEADS)]
        s_prev = [_dot_nt(qgs[hk], kp_ref[:, kvs[hk]].astype(BF16)) for hk in range(N_KV_HEADS)]
        s_cur = [_dot_nt(qgs[hk], kc_ref[:, kvs[hk]].astype(BF16)) for hk in range(N_KV_HEADS)]
        for hk in range(N_KV_HEADS):
            pp, pc, invs = [], [], []
            for g in range(GROUP):
                h = GROUP * hk + g
                rows = slice(g * BLOCK, (g + 1) * BLOCK)
                p_prev, p_cur, inv, lse = _attn_exps(s_prev[hk][rows], s_cur[hk][rows], sink_ref[h], bias_ref[h])
                pp.append(p_prev.astype(BF16))
                pc.append(p_cur.astype(BF16))
                invs.append(inv)
                lse_ref[:, h:h + 1] = lse
            og = _dot(jnp.concatenate(pp, axis=0), vp_ref[:, kvs[hk]].astype(BF16)) + _dot(
                jnp.concatenate(pc, axis=0), vc_ref[:, kvs[hk]].astype(BF16))
            for g in range(GROUP):
                h = GROUP * hk + g
                y_ref[:, h * HEAD_DIM:(h + 1) * HEAD_DIM] = og[g * BLOCK:(g + 1) * BLOCK] * invs[g]
        ag = jnp.concatenate([ag0_ref[...], ag1_ref[...]], axis=1)
        z_ref[...] = (y_ref[...] * (ag * _sigmoid(ag))).astype(BF16)

    prev = lambda c: (lambda i: (jnp.maximum(i - 1, 0), c))
    cur = lambda c: (lambda i: (i, c))
    return pl.pallas_call(
        body,
        name="attn_fwd",
        grid=(nb,),
        in_specs=[
            pl.BlockSpec(memory_space=pltpu.SMEM),
            _bias_spec(),
            pl.BlockSpec((BLOCK, 1024), lambda i: (i, COL_Q // 4)),
            pl.BlockSpec((BLOCK, D_KV), prev(COL_K)),
            pl.BlockSpec((BLOCK, D_KV), cur(COL_K)),
            pl.BlockSpec((BLOCK, D_KV), prev(COL_V)),
            pl.BlockSpec((BLOCK, D_KV), cur(COL_V)),
            pl.BlockSpec((BLOCK, 512), lambda i: (i, COL_ATTN_GATE // 2)),
            pl.BlockSpec((BLOCK, 512), lambda i: (i, COL_ATTN_GATE // 2 + 1)),
        ],
        out_specs=[pl.BlockSpec((BLOCK, 1024), lambda i: (i, 0)), pl.BlockSpec((BLOCK, 1024), lambda i: (i, 0)),
                   pl.BlockSpec((BLOCK, N_Q_HEADS), lambda i: (i, 0))],
        out_shape=[_sds((T, 1024), F32), _sds((T, 1024), BF16), _sds((T, N_Q_HEADS), F32)],
        compiler_params=_params(("arbitrary",), 32),
    )(sinks, bias, proj, proj, proj, proj, proj, proj, proj)


def _attn_bwd(proj, y_attn, lse, dz_attn, sinks, bias, token):
    T = proj.shape[0]
    nb = T // BLOCK

    def body(sink_ref, bias_ref, q_ref, kp_ref, kc_ref, vp_ref, vc_ref, ag0_ref, ag1_ref, y_ref, lse_ref, dz_ref,
             token_ref, dq_ref, dk_ref, dv_ref, dag_ref, ds_ref, dy_s):
        i = pl.program_id(0)

        @pl.when(i == 0)
        def _():
            ds_ref[...] = jnp.zeros_like(ds_ref)

        lane = lax.broadcasted_iota(jnp.int32, (8, 128), 1)
        sub = lax.broadcasted_iota(jnp.int32, (8, 128), 0)
        ag = jnp.concatenate([ag0_ref[...], ag1_ref[...]], axis=1)
        dz = dz_ref[...]
        sg = _sigmoid(ag)
        dag_ref[...] = (dz * y_ref[...] * (sg * (1.0 + ag * (1.0 - sg)))).astype(BF16)
        dy_s[...] = dz * (ag * sg)
        r_cur = pl.multiple_of(i * BLOCK, BLOCK)
        r_prev = pl.multiple_of(jnp.maximum(i - 1, 0) * BLOCK, BLOCK)
        dk_cur, dv_cur, dk_prev, dv_prev = [], [], [], []
        ds_acc = jnp.zeros((8, 128), F32)
        for hk in range(N_KV_HEADS):
            ks = slice(hk * HEAD_DIM, (hk + 1) * HEAD_DIM)
            qg = (_stack_heads(q_ref, hk, F32) * ATTN_SCALE).astype(BF16)
            dog = _stack_heads(dy_s, hk, F32)
            og = _stack_heads(y_ref, hk, F32)
            dog_b = dog.astype(BF16)
            kp = kp_ref[:, ks].astype(BF16)
            kc = kc_ref[:, ks].astype(BF16)
            vp = vp_ref[:, ks].astype(BF16)
            vc = vc_ref[:, ks].astype(BF16)
            s_prev = _dot_nt(qg, kp)
            s_cur = _dot_nt(qg, kc)
            dp_prev = _dot_nt(dog_b, vp)
            dp_cur = _dot_nt(dog_b, vc)
            dvec = jnp.sum(dog * og, axis=-1, keepdims=True)
            pp, pc, dsp, dsc = [], [], [], []
            for g in range(GROUP):
                h = GROUP * hk + g
                rows = slice(g * BLOCK, (g + 1) * BLOCK)
                p_prev, p_cur, p_sink = _attn_probs(
                    s_prev[rows], s_cur[rows], sink_ref[h], bias_ref[h], lse_ref[:, h:h + 1])
                d_h = dvec[rows]
                pp.append(p_prev.astype(BF16))
                pc.append(p_cur.astype(BF16))
                dsp.append((p_prev * (dp_prev[rows] - d_h)).astype(BF16))
                dsc.append((p_cur * (dp_cur[rows] - d_h)).astype(BF16))
                dsink = -jnp.sum(p_sink * d_h, axis=0, keepdims=True)
                ds_acc = ds_acc + jnp.where(jnp.logical_and(lane == h, sub == 1), dsink, 0.0)
            pp = jnp.concatenate(pp, axis=0)
            pc = jnp.concatenate(pc, axis=0)
            dsp = jnp.concatenate(dsp, axis=0)
            dsc = jnp.concatenate(dsc, axis=0)
            dqg = (_dot(dsp, kp) + _dot(dsc, kc)) * ATTN_SCALE
            for g in range(GROUP):
                h = GROUP * hk + g
                dq_ref[:, h * HEAD_DIM:(h + 1) * HEAD_DIM] = dqg[g * BLOCK:(g + 1) * BLOCK].astype(BF16)
            dk_ref[pl.ds(r_cur, BLOCK), ks] = _dot_tn(dsc, qg)
            dv_ref[pl.ds(r_cur, BLOCK), ks] = _dot_tn(pc, dog_b)
            dk_prev.append(_dot_tn(dsp, qg))
            dv_prev.append(_dot_tn(pp, dog_b))
        ds_ref[:, 0:128] += ds_acc

        @pl.when(i > 0)
        def _():
            for hk in range(N_KV_HEADS):
                ks = slice(hk * HEAD_DIM, (hk + 1) * HEAD_DIM)
                dk_ref[pl.ds(r_prev, BLOCK), ks] += dk_prev[hk]
                dv_ref[pl.ds(r_prev, BLOCK), ks] += dv_prev[hk]

    prev = lambda c: (lambda i: (jnp.maximum(i - 1, 0), c))
    cur = lambda c: (lambda i: (i, c))
    blk = pl.BlockSpec((BLOCK, 1024), lambda i: (i, 0))
    whole = pl.BlockSpec((T, D_KV), lambda i: (0, 0))
    return pl.pallas_call(
        body,
        name="attn_bwd",
        grid=(nb,),
        in_specs=[
            pl.BlockSpec(memory_space=pltpu.SMEM),
            _bias_spec(),
            pl.BlockSpec((BLOCK, 1024), lambda i: (i, COL_Q // 4)),
            pl.BlockSpec((BLOCK, D_KV), prev(COL_K)),
            pl.BlockSpec((BLOCK, D_KV), cur(COL_K)),
            pl.BlockSpec((BLOCK, D_KV), prev(COL_V)),
            pl.BlockSpec((BLOCK, D_KV), cur(COL_V)),
            pl.BlockSpec((BLOCK, 512), lambda i: (i, COL_ATTN_GATE // 2)),
            pl.BlockSpec((BLOCK, 512), lambda i: (i, COL_ATTN_GATE // 2 + 1)),
            blk,
            pl.BlockSpec((BLOCK, N_Q_HEADS), lambda i: (i, 0)),
            blk,
            pl.BlockSpec((8, 128), lambda i: (0, 0)),
        ],
        out_specs=[blk, whole, whole, blk, pl.BlockSpec((8, 1024), lambda i: (0, 0))],
        out_shape=[_sds((T, 1024), BF16), _sds((T, D_KV), F32), _sds((T, D_KV), F32), _sds((T, 1024), BF16),
                   _sds((8, 1024), F32)],
        scratch_shapes=[pltpu.VMEM((BLOCK, 1024), F32)],
        compiler_params=_params(("arbitrary",), 48),
    )(sinks, bias, proj, proj, proj, proj, proj, proj, proj, y_attn, lse, dz_attn, token)


def _head(x, target, z_rnn, z_attn, proj, b_gate, g_post, w_rnn_out, w_attn_out, w_out):
    T = x.shape[0]
    tm = 256

    def body(x_ref, t_ref, zr_ref, za_ref, ml0_ref, ml1_ref, ml2_ref, ml3_ref, bg_ref, gp_ref, wr_ref, wa_ref, wo_ref,
             dyx_ref, dzr_ref, dza_ref, dml_ref, mb_ref, dout_ref, dbr_ref, dba_ref, sm_ref):
        @pl.when(pl.program_id(0) == 0)
        def _():
            sm_ref[...] = jnp.zeros_like(sm_ref)

        wr, wa, wo = wr_ref[...], wa_ref[...], wo_ref[...]
        br_rnn = _dot(zr_ref[...], wr)
        br_attn = _dot(za_ref[...], wa)
        ml_rnn = jnp.concatenate([ml0_ref[...], ml1_ref[...]], axis=1)
        ml_attn = jnp.concatenate([ml2_ref[...], ml3_ref[...]], axis=1)
        g_rnn = _sigmoid(ml_rnn + bg_ref[:, 0:D_MODEL])
        g_attn = _sigmoid(ml_attn + bg_ref[:, D_MODEL:2 * D_MODEL])
        mb = (g_rnn * br_rnn + g_attn * br_attn).astype(BF16)
        mb_ref[...] = mb
        out = _dot(mb, wo)
        rstd = lax.rsqrt(jnp.mean(out * out, axis=-1, keepdims=True) + EPS)
        n = out * rstd
        gp = gp_ref[...]
        err = (x_ref[...] + n * gp) - t_ref[...]
        sm_ref[pl.ds(3, 1), :] += 0.5 * jnp.sum(jnp.mean(err * err, axis=-1, keepdims=True), axis=0, keepdims=True)
        dy = err * (1.0 / D_MODEL)
        dyx_ref[...] = dy
        sm_ref[pl.ds(0, 1), :] += jnp.sum(dy * n, axis=0, keepdims=True)
        dn = dy * gp
        dout = (rstd * (dn - n * jnp.mean(dn * n, axis=-1, keepdims=True))).astype(BF16)
        dout_ref[...] = dout
        dmerged = _dot_nt(dout, wo)
        dml_r = (dmerged * br_rnn) * (g_rnn * (1.0 - g_rnn))
        dml_a = (dmerged * br_attn) * (g_attn * (1.0 - g_attn))
        dml_ref[:, 0:D_MODEL] = dml_r.astype(BF16)
        dml_ref[:, D_MODEL:2 * D_MODEL] = dml_a.astype(BF16)
        sm_ref[pl.ds(1, 1), :] += jnp.sum(dml_r, axis=0, keepdims=True)
        sm_ref[pl.ds(2, 1), :] += jnp.sum(dml_a, axis=0, keepdims=True)
        dbr = (dmerged * g_rnn).astype(BF16)
        dba = (dmerged * g_attn).astype(BF16)
        dbr_ref[...] = dbr
        dba_ref[...] = dba
        dzr_ref[...] = _dot_nt(dbr, wr)
        dza_ref[...] = _dot_nt(dba, wa)

    tile = pl.BlockSpec((tm, D_MODEL), lambda i: (i, 0))
    wspec = pl.BlockSpec((D_MODEL, D_MODEL), lambda i: (0, 0))
    ml = lambda q: pl.BlockSpec((tm, 512), lambda i: (i, COL_MERGE // 2 + q))
    return pl.pallas_call(
        body,
        name="head",
        grid=(T // tm,),
        in_specs=[
            tile, tile, tile, tile,
            ml(0), ml(1), ml(2), ml(3),
            pl.BlockSpec((1, 2 * D_MODEL), lambda i: (0, 0)),
            pl.BlockSpec((1, D_MODEL), lambda i: (0, 0)),
            wspec, wspec, wspec,
        ],
        out_specs=[
            tile, tile, tile,
            pl.BlockSpec((tm, 2 * D_MODEL), lambda i: (i, 0)),
            tile, tile, tile, tile,
            pl.BlockSpec((8, D_MODEL), lambda i: (0, 0)),
        ],
        out_shape=[
            _sds((T, D_MODEL), F32), _sds((T, D_MODEL), F32), _sds((T, D_MODEL), F32),
            _sds((T, 2 * D_MODEL), BF16),
            _sds((T, D_MODEL), BF16), _sds((T, D_MODEL), BF16), _sds((T, D_MODEL), BF16), _sds((T, D_MODEL), BF16),
            _sds((8, D_MODEL), F32),
        ],
        compiler_params=_params(("arbitrary",), 56),
    )(x, target, z_rnn, z_attn, proj, proj, proj, proj, b_gate, g_post, w_rnn_out, w_attn_out, w_out)


def _matmul_tn(a, b, name, nblk, blocked):
    T, M = a.shape
    N = b.shape[1]
    tn = N // nblk
    tk = min(512, T)
    if blocked:
        out_spec, out_shape = pl.BlockSpec((None, M, tn), lambda n, t: (n, 0, 0)), _sds((nblk, M, tn), F32)
    else:
        out_spec, out_shape = pl.BlockSpec((M, tn), lambda n, t: (0, n)), _sds((M, N), F32)

    def body(a_ref, b_ref, o_ref):
        @pl.when(pl.program_id(1) == 0)
        def _():
            o_ref[...] = jnp.zeros_like(o_ref)

        o_ref[...] += _dot_tn(a_ref[...], b_ref[...])

    return pl.pallas_call(
        body,
        name=name,
        grid=(nblk, T // tk),
        in_specs=[pl.BlockSpec((tk, M), lambda n, t: (t, 0)), pl.BlockSpec((tk, tn), lambda n, t: (t, n))],
        out_specs=out_spec,
        out_shape=out_shape,
        compiler_params=_params(("parallel", "arbitrary"), 48),
    )(a, b)


def _dh_bwd(dproj, w_in_g, x, dyx, g_pre, token):
    T = x.shape[0]
    tm = min(512, T)

    def body(dp_ref, w_ref, x_ref, dyx_ref, g_ref, token_ref, gx_ref, dg_ref, acc):
        i, k = pl.program_id(0), pl.program_id(1)

        @pl.when(jnp.logical_and(i == 0, k == 0))
        def _():
            dg_ref[...] = jnp.zeros_like(dg_ref)

        part = _dot_nt(dp_ref[...], w_ref[...])

        @pl.when(k == 0)
        def _():
            acc[...] = part

        @pl.when(k > 0)
        def _():
            acc[...] += part

        @pl.when(k == N_CHIPS - 1)
        def _():
            xv = x_ref[...]
            dh = acc[...]
            rstd = lax.rsqrt(jnp.mean(xv * xv, axis=-1, keepdims=True) + EPS)
            nx = xv * rstd
            dhg = dh * g_ref[...]
            gx_ref[...] = dyx_ref[...] + rstd * (dhg - nx * jnp.mean(dhg * nx, axis=-1, keepdims=True))
            dg_ref[pl.ds(0, 1), :] += jnp.sum(dh * nx, axis=0, keepdims=True)

    tile = pl.BlockSpec((tm, D_MODEL), lambda i, k: (i, 0))
    return pl.pallas_call(
        body,
        name="dh_bwd",
        grid=(T // tm, N_CHIPS),
        in_specs=[
            pl.BlockSpec((tm, W_IN_SHARD), lambda i, k: (i, k)),
            pl.BlockSpec((None, D_MODEL, W_IN_SHARD), lambda i, k: (k, 0, 0)),
            tile, tile,
            pl.BlockSpec((1, D_MODEL), lambda i, k: (0, 0)),
            pl.BlockSpec((8, 128), lambda i, k: (0, 0)),
        ],
        out_specs=[tile, pl.BlockSpec((8, D_MODEL), lambda i, k: (0, 0))],
        out_shape=[_sds((T, D_MODEL), F32), _sds((8, D_MODEL), F32)],
        scratch_shapes=[pltpu.VMEM((tm, D_MODEL), F32)],
        compiler_params=_params(("arbitrary", "arbitrary"), 48),
    )(dproj, w_in_g, x, dyx, g_pre, token)


ELEMENTWISE_TILE_BYTES = MIB


def _row_tile(rows, cols):
    for t in (512, 256, 128, 64, 32, 16, 8):
        if rows % t == 0 and t * cols * 4 <= ELEMENTWISE_TILE_BYTES:
            return t
    return rows


def _pair_sum(g, got, core, name):
    nch, R, C = g.shape
    h = R // 2
    tr = _row_tile(h, C)
    nt = h // tr

    def body(c_ref, g_ref, got_ref, p_ref, pb_ref):
        s = g_ref[...] + got_ref[...]
        p_ref[...] = s
        pb_ref[...] = s.astype(BF16)

    blk = pl.BlockSpec((None, tr, C), lambda j, i, c_ref: (j, i, 0))
    return pl.pallas_call(
        body,
        name=name,
        grid_spec=pltpu.PrefetchScalarGridSpec(
            num_scalar_prefetch=1,
            grid=(nch, nt),
            in_specs=[pl.BlockSpec((None, tr, C), lambda j, i, c_ref: (j, c_ref[0] * nt + i, 0)), blk],
            out_specs=[blk, blk],
        ),
        out_shape=[_sds((nch, h, C), F32), _sds((nch, h, C), BF16)],
        compiler_params=_params(("parallel", "parallel"), 48),
    )(core, g, got)


def _chip_sum(p, got, chip_core, name):
    _, h, C = p.shape
    tr = _row_tile(h, C)
    nt = h // tr

    def body(jc_ref, p_ref, g0_ref, g1_ref, g2_ref, o_ref):
        o_ref[...] = ((p_ref[...] + g0_ref[...].astype(F32)) + g1_ref[...].astype(F32)) + g2_ref[...].astype(F32)

    rel = lambda r: pl.BlockSpec((None, tr, C), lambda i, jc_ref: (r, i, 0))
    return pl.pallas_call(
        body,
        name=name,
        grid_spec=pltpu.PrefetchScalarGridSpec(
            num_scalar_prefetch=1,
            grid=(nt,),
            in_specs=[pl.BlockSpec((None, tr, C), lambda i, jc_ref: (jc_ref[0], i, 0)), rel(0), rel(1), rel(2)],
            out_specs=pl.BlockSpec((tr, C), lambda i, jc_ref: (jc_ref[1] * nt + i, 0)),
        ),
        out_shape=_sds((2 * h, C), F32),
        compiler_params=_params(("parallel",), 48),
    )(chip_core, p, got, got, got)


def _place_shards(shards, chip, name):
    n = len(shards)
    tiles = [_row_tile(s.shape[0], s.shape[1]) for s in shards]
    steps = max(s.shape[0] // t for s, t in zip(shards, tiles))
    tiles = [s.shape[0] // steps for s in shards]

    def body(j_ref, *refs):
        for a in range(n):
            refs[n + a][...] = refs[a][...].astype(BF16)

    return pl.pallas_call(
        body,
        name=name,
        grid_spec=pltpu.PrefetchScalarGridSpec(
            num_scalar_prefetch=1,
            grid=(steps,),
            in_specs=[pl.BlockSpec((t, s.shape[1]), lambda i, j_ref: (i, 0)) for s, t in zip(shards, tiles)],
            out_specs=[pl.BlockSpec((None, t, s.shape[1]), lambda i, j_ref: (j_ref[0], i, 0))
                       for s, t in zip(shards, tiles)],
        ),
        out_shape=[_sds((N_CHIPS,) + s.shape, BF16) for s in shards],
        compiler_params=_params(("parallel",), 48),
    )(chip, *shards)


def _adamw(w, g, m, v, name):
    R, C = w.shape
    tr = _row_tile(R, C)
    c1 = 1.0 - ADAM_B1 ** ADAM_STEP
    c2 = 1.0 - ADAM_B2 ** ADAM_STEP

    def body(w_ref, g_ref, m_ref, v_ref, d_ref, nm_ref, nv_ref):
        g = g_ref[...]
        nm = ADAM_B1 * m_ref[...] + (1.0 - ADAM_B1) * g
        nv = ADAM_B2 * v_ref[...] + (1.0 - ADAM_B2) * (g * g)
        nm_ref[...] = nm
        nv_ref[...] = nv
        d_ref[...] = (-ADAM_LR) * ((nm / c1) / (jnp.sqrt(nv / c2) + ADAM_EPS) + ADAM_WD * w_ref[...])

    spec = pl.BlockSpec((tr, C), lambda i: (i, 0))
    return pl.pallas_call(
        body, name=name, grid=(R // tr,), in_specs=[spec] * 4, out_specs=[spec] * 3,
        out_shape=[_sds((R, C), F32)] * 3, compiler_params=_params(("parallel",), 48),
    )(w, g, m, v)


def _place():
    return lax.axis_index("x"), lax.axis_index("y"), lax.axis_index("c")


def _chip_of(x, y, r):
    return (x ^ (r >> 1), y ^ (r & 1))


ANY = pl.BlockSpec(memory_space=pl.ANY)


def _gather_weights(placed, cw8):
    nbig = len(placed)
    halves = [s.shape[1] // 2 for s in placed]
    pieces = [4, 1, 1, 1]
    rows = [h // p for h, p in zip(halves, pieces)]
    order = [(a, q) for q in range(max(pieces)) for a in range(nbig) if q < pieces[a]]
    ici_sem = {(a, q, r): 3 * i + (r - 1) for i, (a, q) in enumerate(order) for r in (1, 2, 3)}
    cw_sem = {r: 3 * len(order) + (r - 1) for r in (1, 2, 3)}
    d2d_sem = {key: 3 * len(order) + 3 + k for key, k in ici_sem.items()}
    nsem = 6 * len(order) + 3

    def body(pin_ref, pr_ref, pa_ref, po_ref, cw_ref, gin_ref, gr_ref, ga_ref, go_ref, gcw_ref, send_sems, recv_sems):
        x, y, c = _place()
        j = 2 * x + y
        dsts = [gin_ref, gr_ref, ga_ref, go_ref]

        def piece_rows(a, q, core):
            return pl.ds(pl.multiple_of(core * halves[a] + q * rows[a], 16), rows[a])

        def ici(a, q, r):
            tx, ty = _chip_of(x, y, r)
            k = ici_sem[(a, q, r)]
            region = dsts[a].at[j, piece_rows(a, q, c), :]
            return pltpu.make_async_remote_copy(
                src_ref=region, dst_ref=region, send_sem=send_sems.at[k], recv_sem=recv_sems.at[k],
                device_id=(tx, ty, c), device_id_type=MESH)

        def ici_landed(a, q, r):
            tx, ty = _chip_of(x, y, r)
            k = ici_sem[(a, q, r)]
            region = dsts[a].at[2 * tx + ty, piece_rows(a, q, c), :]
            return pltpu.make_async_remote_copy(
                src_ref=region, dst_ref=region, send_sem=send_sems.at[k], recv_sem=recv_sems.at[k],
                device_id=(tx, ty, c), device_id_type=MESH)

        def d2d(a, q, r, core):
            tx, ty = _chip_of(x, y, r)
            k = d2d_sem[(a, q, r)]
            region = dsts[a].at[2 * tx + ty, piece_rows(a, q, core), :]
            return pltpu.make_async_remote_copy(
                src_ref=region, dst_ref=region, send_sem=send_sems.at[k], recv_sem=recv_sems.at[k],
                device_id=(x, y, 1 - c), device_id_type=MESH)

        def cw_copy(r):
            tx, ty = _chip_of(x, y, r)
            k = cw_sem[r]
            return pltpu.make_async_remote_copy(
                src_ref=cw_ref, dst_ref=gcw_ref.at[j], send_sem=send_sems.at[k], recv_sem=recv_sems.at[k],
                device_id=(tx, ty, c), device_id_type=MESH)

        def cw_landed(r):
            tx, ty = _chip_of(x, y, r)
            k = cw_sem[r]
            region = gcw_ref.at[2 * tx + ty]
            return pltpu.make_async_remote_copy(
                src_ref=region, dst_ref=region, send_sem=send_sems.at[k], recv_sem=recv_sems.at[k],
                device_id=(tx, ty, c), device_id_type=MESH)

        first = [ici(a, q, r) for (a, q) in order for r in (1, 2, 3)] + [cw_copy(r) for r in (1, 2, 3)]
        for cp in first:
            cp.start()
        passed = []
        for (a, q) in order:
            for r in (1, 2, 3):
                ici_landed(a, q, r).wait_recv()
                cp = d2d(a, q, r, c)
                cp.start()
                passed.append(cp)
        for r in (1, 2, 3):
            cw_landed(r).wait_recv()
        for (a, q) in order:
            for r in (1, 2, 3):
                d2d(a, q, r, 1 - c).wait_recv()
        for cp in first + passed:
            cp.wait_send()

    return pl.pallas_call(
        body,
        name="gather_weights",
        in_specs=[ANY] * 5,
        out_specs=[ANY] * 5,
        out_shape=[_sds(s.shape, s.dtype) for s in placed] + [_sds((N_CHIPS,) + cw8.shape, cw8.dtype)],
        input_output_aliases={a: a for a in range(nbig)},
        scratch_shapes=[pltpu.SemaphoreType.DMA((nsem,)), pltpu.SemaphoreType.DMA((nsem,))],
    )(*placed, cw8)


D2D_PIECE_ROWS = 64


def _pair_exchange(grads, name):
    n = len(grads)
    halves = [g.shape[1] // 2 for g in grads]

    def body(*refs):
        g_refs, got_refs = refs[0:n], refs[n:2 * n]
        send_sems, recv_sems = refs[2 * n:]
        x, y, c = _place()

        def copy(a, src, dst):
            return pltpu.make_async_remote_copy(
                src_ref=src, dst_ref=dst, send_sem=send_sems.at[a], recv_sem=recv_sems.at[a],
                device_id=(x, y, 1 - c), device_id_type=MESH)

        for a in range(n):
            for jj in range(N_CHIPS):
                for q in range(halves[a] // D2D_PIECE_ROWS):
                    src_rows = pl.ds(pl.multiple_of((1 - c) * halves[a] + q * D2D_PIECE_ROWS, 8), D2D_PIECE_ROWS)
                    dst_rows = pl.ds(q * D2D_PIECE_ROWS, D2D_PIECE_ROWS)
                    copy(a, g_refs[a].at[jj, src_rows, :], got_refs[a].at[jj, dst_rows, :]).start()
        for a in range(n):
            sent = g_refs[a].at[:, pl.ds(pl.multiple_of((1 - c) * halves[a], 8), halves[a]), :]
            copy(a, sent, got_refs[a]).wait()

    return pl.pallas_call(
        body,
        name=name,
        in_specs=[ANY] * n,
        out_specs=[ANY] * n,
        out_shape=[_sds((N_CHIPS, h, g.shape[2]), F32) for g, h in zip(grads, halves)],
        scratch_shapes=[pltpu.SemaphoreType.DMA((n,)), pltpu.SemaphoreType.DMA((n,))],
    )(*grads)


HBM = pl.BlockSpec(memory_space=pltpu.HBM)
SEM = pl.BlockSpec(memory_space=pltpu.SEMAPHORE)
DATAFLOW = pltpu.SideEffectType.DATAFLOW_SIDE_EFFECTING


def _chip_copy(p_refs, land_refs, send_sems, recv_sems, a, r):
    x, y, c = _place()
    tx, ty = _chip_of(x, y, r)
    k = a * 3 + (r - 1)
    return pltpu.make_async_remote_copy(
        src_ref=p_refs[a].at[2 * tx + ty], dst_ref=land_refs[a].at[r - 1],
        send_sem=send_sems.at[k], recv_sem=recv_sems.at[k], device_id=(tx, ty, c), device_id_type=MESH)


def _chip_exchange_start(psums, name):
    n = len(psums)
    lands = [lax.empty((3,) + p.shape[1:], p.dtype) for p in psums]

    def body(*refs):
        p_refs, land_refs = refs[0:n], refs[n:2 * n]
        send_sems, recv_sems, token = refs[2 * n], refs[2 * n + 1], refs[-1]
        for a in range(n):
            for r in (1, 2, 3):
                _chip_copy(p_refs, land_refs, send_sems, recv_sems, a, r).start()
        token[...] = jnp.zeros_like(token)

    hbm = lambda t: pltpu.HBM(t.shape, t.dtype)
    keep = lambda t: pltpu.with_memory_space_constraint(t, pltpu.HBM)
    outs = pl.pallas_call(
        body,
        name=name,
        in_specs=[HBM] * (2 * n),
        out_specs=(SEM, SEM, *[HBM] * (2 * n), pl.BlockSpec(memory_space=pltpu.VMEM)),
        out_shape=(pltpu.SemaphoreType.DMA((3 * n,)), pltpu.SemaphoreType.DMA((3 * n,)),
                   *[hbm(p) for p in psums], *[hbm(l) for l in lands], _sds((8, 128), F32)),
        input_output_aliases={i: 2 + i for i in range(2 * n)},
        compiler_params=pltpu.CompilerParams(has_side_effects=DATAFLOW),
    )(*[keep(p) for p in psums], *[keep(l) for l in lands])
    return outs[0], outs[1], list(outs[2:2 + n]), list(outs[2 + n:2 + 2 * n]), outs[-1]


def _chip_exchange_wait(send_sems, recv_sems, p_thru, land_thru, after, name):
    n = len(p_thru)

    def body(*refs):
        p_refs, land_refs = refs[0:n], refs[n:2 * n]
        send_sems, recv_sems = refs[2 * n], refs[2 * n + 1]
        for a in range(n):
            for r in (1, 2, 3):
                copy = _chip_copy(p_refs, land_refs, send_sems, recv_sems, a, r)
                copy.wait_send()
                copy.wait_recv()

    hbm = lambda t: pltpu.HBM(t.shape, t.dtype)
    outs = pl.pallas_call(
        body,
        name=name,
        in_specs=[HBM] * (2 * n) + [SEM, SEM, ANY],
        out_specs=[HBM] * (2 * n),
        out_shape=[hbm(p) for p in p_thru] + [hbm(l) for l in land_thru],
        input_output_aliases={i: i for i in range(2 * n)},
        compiler_params=pltpu.CompilerParams(has_side_effects=DATAFLOW),
    )(*p_thru, *land_thru, send_sems, recv_sems, after)
    return list(outs[n:2 * n])


def _pair_share(fulls):
    n = len(fulls)
    halves = [f.shape[0] // 2 for f in fulls]

    def body(*refs):
        full_refs = refs[n:2 * n]
        send_sems, recv_sems = refs[2 * n:]
        x, y, c = _place()

        def half_of(a, core):
            return full_refs[a].at[pl.ds(pl.multiple_of(core * halves[a], 8), halves[a]), :]

        def remote(a, src, dst):
            return pltpu.make_async_remote_copy(
                src_ref=src, dst_ref=dst, send_sem=send_sems.at[a], recv_sem=recv_sems.at[a],
                device_id=(x, y, 1 - c), device_id_type=MESH)

        for a in range(n):
            for q in range(halves[a] // D2D_PIECE_ROWS):
                piece = full_refs[a].at[
                    pl.ds(pl.multiple_of(c * halves[a] + q * D2D_PIECE_ROWS, 8), D2D_PIECE_ROWS), :]
                remote(a, piece, piece).start()
        for a in range(n):
            remote(a, half_of(a, c), half_of(a, c)).wait_send()
            remote(a, half_of(a, 1 - c), half_of(a, 1 - c)).wait_recv()

    return pl.pallas_call(
        body,
        name="pair_share",
        in_specs=[ANY] * n,
        out_specs=[ANY] * n,
        out_shape=[_sds(f.shape, F32) for f in fulls],
        input_output_aliases={a: a for a in range(n)},
        scratch_shapes=[pltpu.SemaphoreType.DMA((n,)), pltpu.SemaphoreType.DMA((n,))],
    )(*fulls)


def _allreduce_small(s):
    R, C = s.shape

    def body(s_ref, o_ref, sib, chips, send_sems, recv_sems):
        x, y, c = _place()
        j = 2 * x + y
        def to_sib(src, dst):
            return pltpu.make_async_remote_copy(
                src_ref=src, dst_ref=dst, send_sem=send_sems.at[0], recv_sem=recv_sems.at[0],
                device_id=(x, y, 1 - c), device_id_type=MESH)

        for q in range(R // 8):
            to_sib(s_ref.at[pl.ds(8 * q, 8), :], sib.at[pl.ds(8 * q, 8), :]).start()
        to_sib(s_ref, sib).wait()
        chips[j] = s_ref[...] + sib[...]
        sends = []
        for r in (1, 2, 3):
            tx, ty = _chip_of(x, y, r)
            cp = pltpu.make_async_remote_copy(
                src_ref=chips.at[j], dst_ref=chips.at[j], send_sem=send_sems.at[r], recv_sem=recv_sems.at[r],
                device_id=(tx, ty, c), device_id_type=MESH)
            cp.start()
            sends.append(cp)
        for r in (1, 2, 3):
            tx, ty = _chip_of(x, y, r)
            region = chips.at[2 * tx + ty]
            pltpu.make_async_remote_copy(
                src_ref=region, dst_ref=region, send_sem=send_sems.at[r], recv_sem=recv_sems.at[r],
                device_id=(tx, ty, c), device_id_type=MESH).wait_recv()
        for cp in sends:
            cp.wait_send()
        o_ref[...] = (chips[0] + chips[1]) + (chips[2] + chips[3])

    return pl.pallas_call(
        body,
        name="allreduce_small",
        in_specs=[pl.BlockSpec(memory_space=pltpu.VMEM)],
        out_specs=pl.BlockSpec(memory_space=pltpu.VMEM),
        out_shape=_sds((R, C), F32),
        scratch_shapes=[pltpu.VMEM((R, C), F32), pltpu.VMEM((N_CHIPS, R, C), F32),
                        pltpu.SemaphoreType.DMA((4,)), pltpu.SemaphoreType.DMA((4,))],
    )(s)


def _block_diag(w):
    w4 = w.reshape(4, 4, RNN_BLOCK_W, RNN_BLOCK_W)
    eye = jnp.eye(4, dtype=w.dtype)
    return jnp.einsum("jaik,ab->jaibk", w4, eye).reshape(4, RNN_TILE, RNN_TILE)


def _block_diag_part(d):
    d5 = d.reshape(4, 4, RNN_BLOCK_W, 4, RNN_BLOCK_W)
    return jnp.stack([d5[:, a, :, a, :] for a in range(4)], axis=1).reshape(RNN_BLOCKS, RNN_BLOCK_W, RNN_BLOCK_W)


def _local_grads(x, target, g_pre, w_in_g, b_gate, conv_w, conv_b, w_rg_a, b_rg_a, w_rg_x, b_rg_x, lam, sinks,
                 w_rnn_out, w_attn_out, w_out, g_post, on_out_grads, on_w_in_grad):
    wa_bd = _block_diag(w_rg_a).astype(BF16)
    wx_bd = _block_diag(w_rg_x).astype(BF16)
    b_a = b_rg_a.reshape(1, D_RNN)
    b_x = b_rg_x.reshape(1, D_RNN)

    proj, h = _proj_fwd(x, g_pre, w_in_g)
    y_rnn, z_rnn = _rnn_fwd(proj, conv_w, conv_b, wa_bd, wx_bd, b_a, b_x, lam)
    bias = _attn_bias()
    y_attn, z_attn, lse = _attn_fwd(proj, sinks, bias)
    dyx, dz_rnn, dz_attn, dml, merged, dout, dbr_rnn, dbr_attn, head_small = _head(
        x, target, z_rnn, z_attn, proj, b_gate, g_post, w_rnn_out, w_attn_out, w_out)
    dw_out = _matmul_tn(merged, dout, "dw_out", 2, False)
    dw_rnn_out = _matmul_tn(z_rnn, dbr_rnn, "dw_rnn_out", 2, False)
    dw_attn_out = _matmul_tn(z_attn, dbr_attn, "dw_attn_out", 2, False)
    shard_rows = lambda d: d.reshape(N_CHIPS, OUT_SHARD, D_MODEL)
    token = on_out_grads([shard_rows(dw_rnn_out), shard_rows(dw_attn_out), shard_rows(dw_out)])
    dq, dk, dv, dag, attn_small = _attn_bwd(proj, y_attn, lse, dz_attn, sinks, bias, token)
    drx, drg, dwa_t, dwx_t, rnn_small = _rnn_bwd(proj, y_rnn, dz_rnn, conv_w, conv_b, wa_bd, wx_bd, b_a, b_x, lam)
    dproj = jnp.concatenate([drx, drg, dq, dk.astype(BF16), dv.astype(BF16), dag, dml], axis=1)
    token = on_w_in_grad(_matmul_tn(h, dproj, "dw_in", N_CHIPS, True))
    grad_x, dh_small = _dh_bwd(dproj, w_in_g, x, dyx, g_pre, token)
    small = jnp.concatenate([rnn_small, head_small, dh_small + attn_small,
                             _block_diag_part(dwa_t).reshape(64, 1024), _block_diag_part(dwx_t).reshape(64, 1024)], axis=0)
    return grad_x, small


ROW_LOSS = 11


def _rows8(parts):
    out = None
    for r, a in parts:
        p = jnp.pad(a, ((r, 8 - r - a.shape[0]), (0, 1024 - a.shape[1])))
        out = p if out is None else out + p
    return out


def _pack_small(p):
    g0 = _rows8([(0, p["b_rg_a"].reshape(1, 1024)), (1, p["b_rg_x"].reshape(1, 1024)), (2, p["lru_lambda"]),
                 (3, p["conv_b"]), (4, p["conv_w"][0])])
    g1 = _rows8([(0, p["post_norm_g"]), (1, p["b_gate"].reshape(2, 1024))])
    g2 = _rows8([(0, p["pre_norm_g"]), (1, p["attn_sinks"])])
    return jnp.concatenate([g0, g1, g2, p["w_rg_a"].reshape(64, 1024), p["w_rg_x"].reshape(64, 1024)], axis=0)


def _unpack_small(s, conv_cols):
    return {
        "b_rg_a": s[0:1].reshape(1, 16, 64), "b_rg_x": s[1:2].reshape(1, 16, 64), "lru_lambda": s[2:3],
        "conv_b": s[3:4], "conv_w": s[4:8, 0:conv_cols].reshape(1, CONV_W, conv_cols),
        "post_norm_g": s[8:9], "b_gate": s[9:11].reshape(1, 2048),
        "pre_norm_g": s[16:17], "attn_sinks": s[17:18, 0:N_Q_HEADS],
        "w_rg_a": s[24:88].reshape(1, 16, 64, 64), "w_rg_x": s[88:152].reshape(1, 16, 64, 64),
    }


WEIGHTS = ["pre_norm_g", "w_in", "b_gate", "conv_w", "conv_b", "w_rg_a", "b_rg_a", "w_rg_x", "b_rg_x", "lru_lambda",
           "attn_sinks", "w_rnn_out", "w_attn_out", "w_out", "post_norm_g"]
BIG = ["w_in", "w_rnn_out", "w_attn_out", "w_out"]


def kernel(x, pre_norm_g, w_in, b_gate, conv_w, conv_b, w_rg_a, b_rg_a, w_rg_x, b_rg_x, lru_lambda, attn_sinks, w_rnn_out, w_attn_out, w_out, post_norm_g, loss_target, m_pre_norm_g, m_w_in, m_b_gate, m_conv_w, m_conv_b, m_w_rg_a, m_b_rg_a, m_w_rg_x, m_b_rg_x, m_lru_lambda, m_attn_sinks, m_w_rnn_out, m_w_attn_out, m_w_out, m_post_norm_g, v_pre_norm_g, v_w_in, v_b_gate, v_conv_w, v_conv_b, v_w_rg_a, v_b_rg_a, v_w_rg_x, v_b_rg_x, v_lru_lambda, v_attn_sinks, v_w_rnn_out, v_w_attn_out, v_w_out, v_post_norm_g):
    w = dict(pre_norm_g=pre_norm_g, w_in=w_in, b_gate=b_gate, conv_w=conv_w, conv_b=conv_b, w_rg_a=w_rg_a,
             b_rg_a=b_rg_a, w_rg_x=w_rg_x, b_rg_x=b_rg_x, lru_lambda=lru_lambda, attn_sinks=attn_sinks,
             w_rnn_out=w_rnn_out, w_attn_out=w_attn_out, w_out=w_out, post_norm_g=post_norm_g)
    m = dict(pre_norm_g=m_pre_norm_g, w_in=m_w_in, b_gate=m_b_gate, conv_w=m_conv_w, conv_b=m_conv_b, w_rg_a=m_w_rg_a,
             b_rg_a=m_b_rg_a, w_rg_x=m_w_rg_x, b_rg_x=m_b_rg_x, lru_lambda=m_lru_lambda, attn_sinks=m_attn_sinks,
             w_rnn_out=m_w_rnn_out, w_attn_out=m_w_attn_out, w_out=m_w_out, post_norm_g=m_post_norm_g)
    v = dict(pre_norm_g=v_pre_norm_g, w_in=v_w_in, b_gate=v_b_gate, conv_w=v_conv_w, conv_b=v_conv_b, w_rg_a=v_w_rg_a,
             b_rg_a=v_b_rg_a, w_rg_x=v_w_rg_x, b_rg_x=v_b_rg_x, lru_lambda=v_lru_lambda, attn_sinks=v_attn_sinks,
             w_rnn_out=v_w_rnn_out, w_attn_out=v_w_attn_out, w_out=v_w_out, post_norm_g=v_post_norm_g)
    chip = 2 * lax.axis_index("x") + lax.axis_index("y")

    chip_idx = chip.astype(jnp.int32).reshape(1)
    chip_core = jnp.stack([chip, lax.axis_index("c")]).astype(jnp.int32)
    cw8 = jnp.pad(conv_w[0], ((0, 8 - CONV_W), (0, 0)))
    placed = _place_shards([w_in[0], w_rnn_out[0], w_attn_out[0], w_out[0]], chip_idx, "place_shards")
    win_g, wr_g, wa_g, wo_g, cw_g = _gather_weights(placed, cw8)
    cw_g = lax.dynamic_update_slice_in_dim(cw_g, cw8[None], chip, axis=0)
    conv_w_full = jnp.transpose(cw_g[:, 0:CONV_W, :], (1, 0, 2)).reshape(CONV_W, D_RNN)

    core_idx = lax.axis_index("c").astype(jnp.int32).reshape(1)
    started = {}

    def start_reduction(tag, grads):
        got = _pair_exchange(grads, "pair_exchange_" + tag)
        sums = [_pair_sum(g, o, core_idx, "pair_sum_%s_%d" % (tag, a)) for a, (g, o) in enumerate(zip(grads, got))]
        send_sems, recv_sems, p_thru, land_thru, token = _chip_exchange_start(
            [pb for _, pb in sums], "chip_exchange_start_" + tag)
        started[tag] = ([p for p, _ in sums], send_sems, recv_sems, p_thru, land_thru)
        return token

    def end_reduction(tag, after):
        psums, send_sems, recv_sems, p_thru, land_thru = started[tag]
        landed = _chip_exchange_wait(send_sems, recv_sems, p_thru, land_thru, after, "chip_exchange_wait_" + tag)
        return [_chip_sum(p, l, chip_core, "chip_sum_%s_%d" % (tag, a)) for a, (p, l) in enumerate(zip(psums, landed))]

    grad_x, small = _local_grads(
        x[0], loss_target[0], pre_norm_g, win_g, b_gate, conv_w_full, conv_b, w_rg_a[0], b_rg_a[0], w_rg_x[0],
        b_rg_x[0], lru_lambda, attn_sinks[0], wr_g.reshape(D_MODEL, D_MODEL), wa_g.reshape(D_MODEL, D_MODEL),
        wo_g.reshape(D_MODEL, D_MODEL), post_norm_g,
        on_out_grads=lambda grads: start_reduction("out", grads),
        on_w_in_grad=lambda grad: start_reduction("in", [grad]))

    halves = end_reduction("in", grad_x) + end_reduction("out", grad_x)
    gbig = dict(zip(BIG, _pair_share(halves)))

    small_sum = _allreduce_small(small)
    total_loss = small_sum[ROW_LOSS, 0]
    gsmall = _unpack_small(small_sum, D_RNN)
    conv_shard = D_RNN // N_CHIPS
    gsmall["conv_w"] = lax.dynamic_slice_in_dim(gsmall["conv_w"], chip * conv_shard, conv_shard, axis=2)

    grads, delta, new_m, new_v = {}, {}, {}, {}
    for n in BIG:
        grads[n] = gbig[n][None]
        d, nm, nv = _adamw(w[n][0], gbig[n], m[n][0], v[n][0], "adamw_" + n)
        delta[n], new_m[n], new_v[n] = d[None], nm[None], nv[None]
    pick = lambda t: {k: t[k] for k in gsmall}
    d, nm, nv = _adamw(_pack_small(pick(w)), _pack_small(gsmall), _pack_small(pick(m)), _pack_small(pick(v)),
                       "adamw_small")
    ud, um, uv = _unpack_small(d, conv_shard), _unpack_small(nm, conv_shard), _unpack_small(nv, conv_shard)
    for n in gsmall:
        grads[n] = gsmall[n].reshape(w[n].shape)
        delta[n] = ud[n].reshape(w[n].shape)
        new_m[n] = um[n].reshape(w[n].shape)
        new_v[n] = uv[n].reshape(w[n].shape)

    return (total_loss, grad_x[None], *[grads[n] for n in WEIGHTS], *[delta[n] for n in WEIGHTS],
            *[new_m[n] for n in WEIGHTS], *[new_v[n] for n in WEIGHTS])
```

```python
import functools
import math

import jax
import jax.numpy as jnp
from jax import lax
from jax.experimental import pallas as pl
from jax.experimental.pallas import tpu as pltpu

F32 = jnp.float32
BF16 = jnp.bfloat16

D_MODEL = 1024
D_RNN = 1024
RNN_BLOCKS = 16
RNN_BLOCK_W = 64
CONV_W = 4
LRU_C = 8.0
N_Q_HEADS = 16
N_KV_HEADS = 4
GROUP = 4
HEAD_DIM = 64
D_KV = 256
BLOCK = 128
ALIBI_MAX_BIAS = 8.0
EPS = 1e-6
D_IN = 6656
N_CHIPS = 4
W_IN_SHARD = D_IN // N_CHIPS
OUT_SHARD = D_MODEL // N_CHIPS
ADAM_LR = 0.001
ADAM_B1 = 0.9
ADAM_B2 = 0.999
ADAM_EPS = 1e-08
ADAM_WD = 0.01
ADAM_STEP = 10
NEG_BIG = -1e30
MIB = 1 << 20

COL_RNN_X = 0
COL_RNN_GATE = 4
COL_Q = 8
COL_K = 12
COL_V = 13
COL_ATTN_GATE = 14
COL_MERGE = 18

RNN_TILE = 256
RNN_CHUNK = 256
SMALL_ROWS = 152
MESH = pl.DeviceIdType.MESH


def _sds(shape, dtype):
    return pltpu.HBM(shape, dtype)


def _params(sem=None, vmem_mib=None):
    kw = {}
    if sem is not None:
        kw["dimension_semantics"] = sem
    if vmem_mib is not None:
        kw["vmem_limit_bytes"] = vmem_mib * MIB
    return pltpu.CompilerParams(**kw)


def _hbm(*arrays):
    return [pltpu.with_memory_space_constraint(a, pltpu.HBM) for a in arrays]


def _dot(a, b):
    return jnp.dot(a, b, preferred_element_type=F32)


def _dot_nt(a, b):
    return lax.dot_general(a, b, (((1,), (1,)), ((), ())), preferred_element_type=F32)


def _dot_tn(a, b):
    return lax.dot_general(a, b, (((0,), (0,)), ((), ())), preferred_element_type=F32)


def _sigmoid(x):
    return 0.5 * jnp.tanh(0.5 * x) + 0.5


def _sigmoid_small(x):
    return 1.0 / (1.0 + jnp.exp(-x))


def _softplus(x):
    return jnp.maximum(x, 0.0) + jnp.log(1.0 + jnp.exp(-jnp.abs(x)))


def _one_minus_square(a, log_a):
    return -jnp.tanh(log_a) * (a * a + 1.0)


def _proj_fwd(x, g_pre, w_in_g):
    T = x.shape[0]
    tm = min(1024, T)

    def body(x_ref, g_ref, w_ref, proj_ref, h_ref):
        @pl.when(pl.program_id(1) == 0)
        def _():
            xv = x_ref[...]
            rstd = lax.rsqrt(jnp.mean(xv * xv, axis=-1, keepdims=True) + EPS)
            h_ref[...] = ((xv * rstd) * g_ref[...]).astype(BF16)

        proj_ref[...] = _dot(h_ref[...], w_ref[...])

    return pl.pallas_call(
        body,
        name="proj_fwd",
        grid=(T // tm, N_CHIPS),
        in_specs=[
            pl.BlockSpec((tm, D_MODEL), lambda i, j: (i, 0)),
            pl.BlockSpec((1, D_MODEL), lambda i, j: (0, 0)),
            pl.BlockSpec((None, D_MODEL, W_IN_SHARD), lambda i, j: (j, 0, 0)),
        ],
        out_specs=[
            pl.BlockSpec((tm, W_IN_SHARD), lambda i, j: (i, j)),
            pl.BlockSpec((tm, D_MODEL), lambda i, j: (i, 0)),
        ],
        out_shape=[_sds((T, D_IN), F32), _sds((T, D_MODEL), BF16)],
        compiler_params=_params(("parallel", "arbitrary"), 48),
    )(*_hbm(x, g_pre, w_in_g))


def _shift_down(x, tail, s, row):
    n = x.shape[0]
    xs = pltpu.roll(x, s, 0)
    tail_t = jnp.tile(pltpu.roll(tail, s, 0), (n // 8, 1))
    return jnp.where(row < s, tail_t, xs)


def _shift_up(x, head, s, row):
    n = x.shape[0]
    xs = pltpu.roll(x, n - s, 0)
    head_t = jnp.tile(pltpu.roll(head, 8 - s, 0), (n // 8, 1))
    return jnp.where(row >= n - s, head_t, xs)


def _conv_taps(x, tail, row):
    return [_shift_down(x, tail, 3, row), _shift_down(x, tail, 2, row), _shift_down(x, tail, 1, row), x]


def _rglru_gates(c, wa, wx, ba, bx, lam):
    cb = c.astype(BF16)
    r = _sigmoid_small(_dot(cb, wa) + ba)
    i = _sigmoid(_dot(cb, wx) + bx)
    log_a = (-LRU_C) * r * _softplus(-lam)
    a = jnp.exp(log_a)
    mult = jnp.sqrt(_one_minus_square(a, log_a))
    return cb, r, i, a, mult


SUBLANES = 8


def _scan_down(a, u, row):
    n = a.shape[0]
    s = 1
    while s < SUBLANES:
        a_sh = jnp.where(row >= s, pltpu.roll(a, s, 0), 1.0)
        u_sh = jnp.where(row >= s, pltpu.roll(u, s, 0), 0.0)
        u = a * u_sh + u
        a = a * a_sh
        s *= 2
    while s < n:
        u = jnp.concatenate([u[:s], a[s:] * u[:n - s] + u[s:]], axis=0)
        a = jnp.concatenate([a[:s], a[s:] * a[:n - s]], axis=0)
        s *= 2
    return a, u


def _scan_up(b, u, row):
    n = b.shape[0]
    s = 1
    while s < SUBLANES:
        b_sh = jnp.where(row < n - s, pltpu.roll(b, n - s, 0), 1.0)
        u_sh = jnp.where(row < n - s, pltpu.roll(u, n - s, 0), 0.0)
        u = b * u_sh + u
        b = b * b_sh
        s *= 2
    while s < n:
        u = jnp.concatenate([b[:n - s] * u[s:] + u[:n - s], u[n - s:]], axis=0)
        b = jnp.concatenate([b[:n - s] * b[s:], b[n - s:]], axis=0)
        s *= 2
    return b, u


def _rnn_fwd(proj, conv_w, conv_b, wa_bd, wx_bd, b_a, b_x, lam):
    T = proj.shape[0]
    tc, ct = RNN_CHUNK, RNN_TILE
    nt = T // tc

    def body(x_ref, rg_ref, cw_ref, cb_ref, wa_ref, wx_ref, ba_ref, bx_ref, lam_ref, h_ref, z_ref, xtail, hcarry):
        @pl.when(pl.program_id(1) == 0)
        def _():
            xtail[...] = jnp.zeros_like(xtail)
            hcarry[...] = jnp.zeros_like(hcarry)

        row = lax.broadcasted_iota(jnp.int32, (tc, ct), 0)
        x = x_ref[...]
        taps = _conv_taps(x, xtail[...], row)
        c = cb_ref[...] + cw_ref[pl.ds(0, 1), :] * taps[0]
        for k in range(1, CONV_W):
            c = c + cw_ref[pl.ds(k, 1), :] * taps[k]
        xtail[...] = x_ref[pl.ds(tc - 8, 8), :]
        _, _, i, a, mult = _rglru_gates(c, wa_ref[...], wx_ref[...], ba_ref[...], bx_ref[...], lam_ref[...])
        u = mult * (i * c)
        a_cum, h0 = _scan_down(a, u, row)
        h = h0 + a_cum * hcarry[...]
        h_ref[...] = h
        hcarry[...] = h_ref[pl.ds(tc - 1, 1), :]
        rg = rg_ref[...]
        z_ref[...] = (h * (rg * _sigmoid(rg))).astype(BF16)

    col = lambda off: (lambda j, t: (t, off + j))
    vec = pl.BlockSpec((1, ct), lambda j, t: (0, j))
    mat = pl.BlockSpec((None, ct, ct), lambda j, t: (j, 0, 0))
    return pl.pallas_call(
        body,
        name="rnn_fwd",
        grid=(D_RNN // ct, nt),
        in_specs=[
            pl.BlockSpec((tc, ct), col(COL_RNN_X)),
            pl.BlockSpec((tc, ct), col(COL_RNN_GATE)),
            pl.BlockSpec((CONV_W, ct), lambda j, t: (0, j)),
            vec, mat, mat, vec, vec, vec,
        ],
        out_specs=[pl.BlockSpec((tc, ct), lambda j, t: (t, j)), pl.BlockSpec((tc, ct), lambda j, t: (t, j))],
        out_shape=[_sds((T, D_RNN), F32), _sds((T, D_RNN), BF16)],
        scratch_shapes=[pltpu.VMEM((8, ct), F32), pltpu.VMEM((1, ct), F32)],
        compiler_params=_params(("parallel", "arbitrary"), 32),
    )(*_hbm(proj, proj, conv_w, conv_b, wa_bd, wx_bd, b_a, b_x, lam))


def _rnn_bwd(proj, y_rnn, dz_rnn, conv_w, conv_b, wa_bd, wx_bd, b_a, b_x, lam):
    T = proj.shape[0]
    tc, ct = RNN_CHUNK, RNN_TILE
    nt = T // tc
    hb = tc // 8

    def body(x_ref, xh_ref, rg_ref, h_ref, hh_ref, dz_ref, cw_ref, cb_ref, wa_ref, wx_ref, ba_ref, bx_ref, lam_ref,
             dx_ref, drg_ref, dwa_ref, dwx_ref, sm_ref, lam_carry, a_carry, dc_head):
        t = pl.program_id(1)
        first_chunk = t == nt - 1

        @pl.when(t == 0)
        def _():
            lam_carry[...] = jnp.zeros_like(lam_carry)
            a_carry[...] = jnp.zeros_like(a_carry)
            dc_head[...] = jnp.zeros_like(dc_head)
            dwa_ref[...] = jnp.zeros_like(dwa_ref)
            dwx_ref[...] = jnp.zeros_like(dwx_ref)
            sm_ref[...] = jnp.zeros_like(sm_ref)

        row = lax.broadcasted_iota(jnp.int32, (tc, ct), 0)
        keep = jnp.where(first_chunk, 0.0, 1.0)
        x = x_ref[...]
        xtail = xh_ref[...] * keep
        taps = _conv_taps(x, xtail, row)
        c = cb_ref[...] + cw_ref[pl.ds(0, 1), :] * taps[0]
        for k in range(1, CONV_W):
            c = c + cw_ref[pl.ds(k, 1), :] * taps[k]
        lam = lam_ref[...]
        cb, r, i, a, mult = _rglru_gates(c, wa_ref[...], wx_ref[...], ba_ref[...], bx_ref[...], lam)
        h = h_ref[...]
        h_prev = _shift_down(h, hh_ref[...] * keep, 1, row)
        rg = rg_ref[...]
        dz = dz_ref[...]
        sg = _sigmoid(rg)
        drg_ref[...] = (dz * h * (sg * (1.0 + rg * (1.0 - sg)))).astype(BF16)
        dy = dz * (rg * sg)
        b = jnp.where(row >= tc - 1, a_carry[pl.ds(0, 1), :], pltpu.roll(a, tc - 1, 0))
        b_cum, l0 = _scan_up(b, dy, row)
        lt = l0 + b_cum * lam_carry[pl.ds(0, 1), :]
        lam_carry[...] = lt[0:8, :]
        a_carry[...] = a[0:8, :]
        ic = i * c
        dmult = lt * ic
        di = lt * mult * c
        dc = lt * mult * i
        dlog_a = a * (lt * h_prev - dmult * a / mult)
        sp = _softplus(-lam)
        dpre_r = dlog_a * ((-LRU_C) * sp) * (r * (1.0 - r))
        dpre_i = di * (i * (1.0 - i))
        dlam_row = jnp.sum(dlog_a * r, axis=0, keepdims=True) * (LRU_C * _sigmoid(-lam))
        dpr_b = dpre_r.astype(BF16)
        dpi_b = dpre_i.astype(BF16)
        dwa_ref[...] += _dot_tn(cb, dpr_b)
        dwx_ref[...] += _dot_tn(cb, dpi_b)
        dc = dc + _dot_nt(dpr_b, wa_ref[...]) + _dot_nt(dpi_b, wx_ref[...])
        head = dc_head[...]
        dx = cw_ref[pl.ds(3, 1), :] * dc
        for m in range(1, CONV_W):
            dx = dx + cw_ref[pl.ds(3 - m, 1), :] * _shift_up(dc, head, m, row)
        dx_ref[...] = dx.astype(BF16)
        dc_head[...] = dc[0:8, :]
        sm_ref[pl.ds(0, 1), :] += jnp.sum(dpre_r, axis=0, keepdims=True)
        sm_ref[pl.ds(1, 1), :] += jnp.sum(dpre_i, axis=0, keepdims=True)
        sm_ref[pl.ds(2, 1), :] += dlam_row
        sm_ref[pl.ds(3, 1), :] += jnp.sum(dc, axis=0, keepdims=True)
        for k in range(CONV_W):
            sm_ref[pl.ds(4 + k, 1), :] += jnp.sum(dc * taps[k], axis=0, keepdims=True)

    rev = lambda off: (lambda j, t: (nt - 1 - t, off + j))
    halo = lambda off: (lambda j, t: (jnp.maximum((nt - 1 - t) * hb - 1, 0), off + j))
    vec = pl.BlockSpec((1, ct), lambda j, t: (0, j))
    mat = pl.BlockSpec((None, ct, ct), lambda j, t: (j, 0, 0))
    return pl.pallas_call(
        body,
        name="rnn_bwd",
        grid=(D_RNN // ct, nt),
        in_specs=[
            pl.BlockSpec((tc, ct), rev(COL_RNN_X)),
            pl.BlockSpec((8, ct), halo(COL_RNN_X)),
            pl.BlockSpec((tc, ct), rev(COL_RNN_GATE)),
            pl.BlockSpec((tc, ct), rev(0)),
            pl.BlockSpec((8, ct), halo(0)),
            pl.BlockSpec((tc, ct), rev(0)),
            pl.BlockSpec((CONV_W, ct), lambda j, t: (0, j)),
            vec, mat, mat, vec, vec, vec,
        ],
        out_specs=[
            pl.BlockSpec((tc, ct), rev(0)),
            pl.BlockSpec((tc, ct), rev(0)),
            mat, mat,
            pl.BlockSpec((8, ct), lambda j, t: (0, j)),
        ],
        out_shape=[_sds((T, D_RNN), BF16), _sds((T, D_RNN), BF16), _sds((D_RNN // ct, ct, ct), F32),
                   _sds((D_RNN // ct, ct, ct), F32), _sds((8, D_RNN), F32)],
        scratch_shapes=[pltpu.VMEM((8, ct), F32), pltpu.VMEM((8, ct), F32), pltpu.VMEM((8, ct), F32)],
        compiler_params=_params(("parallel", "arbitrary"), 32),
    )(*_hbm(proj, proj, proj, y_rnn, y_rnn, dz_rnn, conv_w, conv_b, wa_bd, wx_bd, b_a, b_x, lam))


def _attn_bias():
    qi = jnp.arange(BLOCK)[:, None]
    kj = jnp.arange(BLOCK)[None, :]
    dist_cur = (qi - kj).astype(F32)
    slopes = 2.0 ** (-ALIBI_MAX_BIAS * jnp.arange(1, N_Q_HEADS + 1, dtype=F32) / N_Q_HEADS)
    slopes = slopes[:, None, None]
    prev = jnp.where(kj > qi, -slopes * (dist_cur + float(BLOCK)), NEG_BIG)
    cur = jnp.where(kj <= qi, -slopes * dist_cur, NEG_BIG)
    later = jnp.concatenate([prev, cur], axis=-1)
    first = jnp.concatenate([jnp.full_like(prev, NEG_BIG), cur], axis=-1)
    return jnp.stack([first, later])


def _attn_exps(s_prev, s_cur, sink, bias):
    s_prev = s_prev + bias[:, 0:BLOCK]
    s_cur = s_cur + bias[:, BLOCK:2 * BLOCK]
    m = jnp.maximum(jnp.max(jnp.maximum(s_prev, s_cur), axis=-1, keepdims=True), sink)
    p_prev = jnp.exp(s_prev - m)
    p_cur = jnp.exp(s_cur - m)
    total = jnp.sum(p_prev + p_cur, axis=-1, keepdims=True) + jnp.exp(sink - m)
    return p_prev, p_cur, 1.0 / total, m + jnp.log(total)


def _attn_probs(s_prev, s_cur, sink, bias, lse):
    p_prev = jnp.exp((s_prev + bias[:, 0:BLOCK]) - lse)
    p_cur = jnp.exp((s_cur + bias[:, BLOCK:2 * BLOCK]) - lse)
    return p_prev, p_cur, jnp.exp(sink - lse)


def _stack_heads(ref_or_val, hk, dtype):
    parts = [ref_or_val[:, (GROUP * hk + g) * HEAD_DIM:(GROUP * hk + g + 1) * HEAD_DIM] for g in range(GROUP)]
    return jnp.concatenate(parts, axis=0).astype(dtype)


ATTN_SCALE = HEAD_DIM ** -0.5


def _bias_spec():
    return pl.BlockSpec((None, N_Q_HEADS, BLOCK, 2 * BLOCK), lambda i: (jnp.minimum(i, 1), 0, 0, 0))


def _attn_fwd(proj, sinks, bias):
    T = proj.shape[0]
    nb = T // BLOCK

    def body(sink_ref, bias_ref, q_ref, kp_ref, kc_ref, vp_ref, vc_ref, ag0_ref, ag1_ref, y_ref, z_ref, lse_ref):
        kvs = [slice(hk * HEAD_DIM, (hk + 1) * HEAD_DIM) for hk in range(N_KV_HEADS)]
        qgs = [(_stack_heads(q_ref, hk, F32) * ATTN_SCALE).astype(BF16) for hk in range(N_KV_HEADS)]
        s_prev = [_dot_nt(qgs[hk], kp_ref[:, kvs[hk]].astype(BF16)) for hk in range(N_KV_HEADS)]
        s_cur = [_dot_nt(qgs[hk], kc_ref[:, kvs[hk]].astype(BF16)) for hk in range(N_KV_HEADS)]
        for hk in range(N_KV_HEADS):
            pp, pc, invs = [], [], []
            for g in range(GROUP):
                h = GROUP * hk + g
                rows = slice(g * BLOCK, (g + 1) * BLOCK)
                p_prev, p_cur, inv, lse = _attn_exps(s_prev[hk][rows], s_cur[hk][rows], sink_ref[h], bias_ref[h])
                pp.append(p_prev.astype(BF16))
                pc.append(p_cur.astype(BF16))
                invs.append(inv)
                lse_ref[:, h:h + 1] = lse
            og = _dot(jnp.concatenate(pp, axis=0), vp_ref[:, kvs[hk]].astype(BF16)) + _dot(
                jnp.concatenate(pc, axis=0), vc_ref[:, kvs[hk]].astype(BF16))
            for g in range(GROUP):
                h = GROUP * hk + g
                y_ref[:, h * HEAD_DIM:(h + 1) * HEAD_DIM] = og[g * BLOCK:(g + 1) * BLOCK] * invs[g]
        ag = jnp.concatenate([ag0_ref[...], ag1_ref[...]], axis=1)
        z_ref[...] = (y_ref[...] * (ag * _sigmoid(ag))).astype(BF16)

    prev = lambda c: (lambda i: (jnp.maximum(i - 1, 0), c))
    cur = lambda c: (lambda i: (i, c))
    return pl.pallas_call(
        body,
        name="attn_fwd",
        grid=(nb,),
        in_specs=[
            pl.BlockSpec(memory_space=pltpu.SMEM),
            _bias_spec(),
            pl.BlockSpec((BLOCK, 1024), lambda i: (i, COL_Q // 4)),
            pl.BlockSpec((BLOCK, D_KV), prev(COL_K)),
            pl.BlockSpec((BLOCK, D_KV), cur(COL_K)),
            pl.BlockSpec((BLOCK, D_KV), prev(COL_V)),
            pl.BlockSpec((BLOCK, D_KV), cur(COL_V)),
            pl.BlockSpec((BLOCK, 512), lambda i: (i, COL_ATTN_GATE // 2)),
            pl.BlockSpec((BLOCK, 512), lambda i: (i, COL_ATTN_GATE // 2 + 1)),
        ],
        out_specs=[pl.BlockSpec((BLOCK, 1024), lambda i: (i, 0)), pl.BlockSpec((BLOCK, 1024), lambda i: (i, 0)),
                   pl.BlockSpec((BLOCK, N_Q_HEADS), lambda i: (i, 0))],
        out_shape=[_sds((T, 1024), F32), _sds((T, 1024), BF16), _sds((T, N_Q_HEADS), F32)],
        compiler_params=_params(("arbitrary",), 32),
    )(sinks, *_hbm(bias, proj, proj, proj, proj, proj, proj, proj))


def _attn_bwd(proj, y_attn, lse, dz_attn, sinks, bias, token):
    T = proj.shape[0]
    nb = T // BLOCK

    def body(sink_ref, bias_ref, q_ref, kp_ref, kc_ref, vp_ref, vc_ref, ag0_ref, ag1_ref, y_ref, lse_ref, dz_ref,
             token_ref, dq_ref, dk_ref, dv_ref, dag_ref, ds_ref, dy_s):
        i = pl.program_id(0)

        @pl.when(i == 0)
        def _():
            ds_ref[...] = jnp.zeros_like(ds_ref)

        lane = lax.broadcasted_iota(jnp.int32, (8, 128), 1)
        sub = lax.broadcasted_iota(jnp.int32, (8, 128), 0)
        ag = jnp.concatenate([ag0_ref[...], ag1_ref[...]], axis=1)
        dz = dz_ref[...]
        sg = _sigmoid(ag)
        dag_ref[...] = (dz * y_ref[...] * (sg * (1.0 + ag * (1.0 - sg)))).astype(BF16)
        dy_s[...] = dz * (ag * sg)
        r_cur = pl.multiple_of(i * BLOCK, BLOCK)
        r_prev = pl.multiple_of(jnp.maximum(i - 1, 0) * BLOCK, BLOCK)
        dk_cur, dv_cur, dk_prev, dv_prev = [], [], [], []
        ds_acc = jnp.zeros((8, 128), F32)
        for hk in range(N_KV_HEADS):
            ks = slice(hk * HEAD_DIM, (hk + 1) * HEAD_DIM)
            qg = (_stack_heads(q_ref, hk, F32) * ATTN_SCALE).astype(BF16)
            dog = _stack_heads(dy_s, hk, F32)
            og = _stack_heads(y_ref, hk, F32)
            dog_b = dog.astype(BF16)
            kp = kp_ref[:, ks].astype(BF16)
            kc = kc_ref[:, ks].astype(BF16)
            vp = vp_ref[:, ks].astype(BF16)
            vc = vc_ref[:, ks].astype(BF16)
            s_prev = _dot_nt(qg, kp)
            s_cur = _dot_nt(qg, kc)
            dp_prev = _dot_nt(dog_b, vp)
            dp_cur = _dot_nt(dog_b, vc)
            dvec = jnp.sum(dog * og, axis=-1, keepdims=True)
            pp, pc, dsp, dsc = [], [], [], []
            for g in range(GROUP):
                h = GROUP * hk + g
                rows = slice(g * BLOCK, (g + 1) * BLOCK)
                p_prev, p_cur, p_sink = _attn_probs(
                    s_prev[rows], s_cur[rows], sink_ref[h], bias_ref[h], lse_ref[:, h:h + 1])
                d_h = dvec[rows]
                pp.append(p_prev.astype(BF16))
                pc.append(p_cur.astype(BF16))
                dsp.append((p_prev * (dp_prev[rows] - d_h)).astype(BF16))
                dsc.append((p_cur * (dp_cur[rows] - d_h)).astype(BF16))
                dsink = -jnp.sum(p_sink * d_h, axis=0, keepdims=True)
                ds_acc = ds_acc + jnp.where(jnp.logical_and(lane == h, sub == 1), dsink, 0.0)
            pp = jnp.concatenate(pp, axis=0)
            pc = jnp.concatenate(pc, axis=0)
            dsp = jnp.concatenate(dsp, axis=0)
            dsc = jnp.concatenate(dsc, axis=0)
            dqg = (_dot(dsp, kp) + _dot(dsc, kc)) * ATTN_SCALE
            for g in range(GROUP):
                h = GROUP * hk + g
                dq_ref[:, h * HEAD_DIM:(h + 1) * HEAD_DIM] = dqg[g * BLOCK:(g + 1) * BLOCK].astype(BF16)
            dk_ref[pl.ds(r_cur, BLOCK), ks] = _dot_tn(dsc, qg)
            dv_ref[pl.ds(r_cur, BLOCK), ks] = _dot_tn(pc, dog_b)
            dk_prev.append(_dot_tn(dsp, qg))
            dv_prev.append(_dot_tn(pp, dog_b))
        ds_ref[:, 0:128] += ds_acc

        @pl.when(i > 0)
        def _():
            for hk in range(N_KV_HEADS):
                ks = slice(hk * HEAD_DIM, (hk + 1) * HEAD_DIM)
                dk_ref[pl.ds(r_prev, BLOCK), ks] += dk_prev[hk]
                dv_ref[pl.ds(r_prev, BLOCK), ks] += dv_prev[hk]

    prev = lambda c: (lambda i: (jnp.maximum(i - 1, 0), c))
    cur = lambda c: (lambda i: (i, c))
    blk = pl.BlockSpec((BLOCK, 1024), lambda i: (i, 0))
    whole = pl.BlockSpec((T, D_KV), lambda i: (0, 0))
    return pl.pallas_call(
        body,
        name="attn_bwd",
        grid=(nb,),
        in_specs=[
            pl.BlockSpec(memory_space=pltpu.SMEM),
            _bias_spec(),
            pl.BlockSpec((BLOCK, 1024), lambda i: (i, COL_Q // 4)),
            pl.BlockSpec((BLOCK, D_KV), prev(COL_K)),
            pl.BlockSpec((BLOCK, D_KV), cur(COL_K)),
            pl.BlockSpec((BLOCK, D_KV), prev(COL_V)),
            pl.BlockSpec((BLOCK, D_KV), cur(COL_V)),
            pl.BlockSpec((BLOCK, 512), lambda i: (i, COL_ATTN_GATE // 2)),
            pl.BlockSpec((BLOCK, 512), lambda i: (i, COL_ATTN_GATE // 2 + 1)),
            blk,
            pl.BlockSpec((BLOCK, N_Q_HEADS), lambda i: (i, 0)),
            blk,
            pl.BlockSpec((8, 128), lambda i: (0, 0)),
        ],
        out_specs=[blk, whole, whole, blk, pl.BlockSpec((8, 1024), lambda i: (0, 0))],
        out_shape=[_sds((T, 1024), BF16), _sds((T, D_KV), F32), _sds((T, D_KV), F32), _sds((T, 1024), BF16),
                   _sds((8, 1024), F32)],
        scratch_shapes=[pltpu.VMEM((BLOCK, 1024), F32)],
        compiler_params=_params(("arbitrary",), 48),
    )(sinks, *_hbm(bias, proj, proj, proj, proj, proj, proj, proj, y_attn, lse, dz_attn, token))


def _head(x, target, z_rnn, z_attn, proj, b_gate, g_post, w_rnn_out, w_attn_out, w_out):
    T = x.shape[0]
    tm = 256

    def body(x_ref, t_ref, zr_ref, za_ref, ml0_ref, ml1_ref, ml2_ref, ml3_ref, bg_ref, gp_ref, wr_ref, wa_ref, wo_ref,
             dyx_ref, dzr_ref, dza_ref, dml_ref, mb_ref, dout_ref, dbr_ref, dba_ref, sm_ref):
        @pl.when(pl.program_id(0) == 0)
        def _():
            sm_ref[...] = jnp.zeros_like(sm_ref)

        wr, wa, wo = wr_ref[...], wa_ref[...], wo_ref[...]
        br_rnn = _dot(zr_ref[...], wr)
        br_attn = _dot(za_ref[...], wa)
        ml_rnn = jnp.concatenate([ml0_ref[...], ml1_ref[...]], axis=1)
        ml_attn = jnp.concatenate([ml2_ref[...], ml3_ref[...]], axis=1)
        g_rnn = _sigmoid(ml_rnn + bg_ref[:, 0:D_MODEL])
        g_attn = _sigmoid(ml_attn + bg_ref[:, D_MODEL:2 * D_MODEL])
        mb = (g_rnn * br_rnn + g_attn * br_attn).astype(BF16)
        mb_ref[...] = mb
        out = _dot(mb, wo)
        rstd = lax.rsqrt(jnp.mean(out * out, axis=-1, keepdims=True) + EPS)
        n = out * rstd
        gp = gp_ref[...]
        err = (x_ref[...] + n * gp) - t_ref[...]
        sm_ref[pl.ds(3, 1), :] += 0.5 * jnp.sum(jnp.mean(err * err, axis=-1, keepdims=True), axis=0, keepdims=True)
        dy = err * (1.0 / D_MODEL)
        dyx_ref[...] = dy
        sm_ref[pl.ds(0, 1), :] += jnp.sum(dy * n, axis=0, keepdims=True)
        dn = dy * gp
        dout = (rstd * (dn - n * jnp.mean(dn * n, axis=-1, keepdims=True))).astype(BF16)
        dout_ref[...] = dout
        dmerged = _dot_nt(dout, wo)
        dml_r = (dmerged * br_rnn) * (g_rnn * (1.0 - g_rnn))
        dml_a = (dmerged * br_attn) * (g_attn * (1.0 - g_attn))
        dml_ref[:, 0:D_MODEL] = dml_r.astype(BF16)
        dml_ref[:, D_MODEL:2 * D_MODEL] = dml_a.astype(BF16)
        sm_ref[pl.ds(1, 1), :] += jnp.sum(dml_r, axis=0, keepdims=True)
        sm_ref[pl.ds(2, 1), :] += jnp.sum(dml_a, axis=0, keepdims=True)
        dbr = (dmerged * g_rnn).astype(BF16)
        dba = (dmerged * g_attn).astype(BF16)
        dbr_ref[...] = dbr
        dba_ref[...] = dba
        dzr_ref[...] = _dot_nt(dbr, wr)
        dza_ref[...] = _dot_nt(dba, wa)

    tile = pl.BlockSpec((tm, D_MODEL), lambda i: (i, 0))
    wspec = pl.BlockSpec((D_MODEL, D_MODEL), lambda i: (0, 0))
    ml = lambda q: pl.BlockSpec((tm, 512), lambda i: (i, COL_MERGE // 2 + q))
    return pl.pallas_call(
        body,
        name="head",
        grid=(T // tm,),
        in_specs=[
            tile, tile, tile, tile,
            ml(0), ml(1), ml(2), ml(3),
            pl.BlockSpec((1, 2 * D_MODEL), lambda i: (0, 0)),
            pl.BlockSpec((1, D_MODEL), lambda i: (0, 0)),
            wspec, wspec, wspec,
        ],
        out_specs=[
            tile, tile, tile,
            pl.BlockSpec((tm, 2 * D_MODEL), lambda i: (i, 0)),
            tile, tile, tile, tile,
            pl.BlockSpec((8, D_MODEL), lambda i: (0, 0)),
        ],
        out_shape=[
            _sds((T, D_MODEL), F32), _sds((T, D_MODEL), F32), _sds((T, D_MODEL), F32),
            _sds((T, 2 * D_MODEL), BF16),
            _sds((T, D_MODEL), BF16), _sds((T, D_MODEL), BF16), _sds((T, D_MODEL), BF16), _sds((T, D_MODEL), BF16),
            _sds((8, D_MODEL), F32),
        ],
        compiler_params=_params(("arbitrary",), 56),
    )(*_hbm(x, target, z_rnn, z_attn, proj, proj, proj, proj, b_gate, g_post, w_rnn_out, w_attn_out, w_out))


def _matmul_tn(a, b, name, nblk, blocked):
    T, M = a.shape
    N = b.shape[1]
    tn = N // nblk
    tk = min(512, T)
    if blocked:
        out_spec, out_shape = pl.BlockSpec((None, M, tn), lambda n, t: (n, 0, 0)), _sds((nblk, M, tn), F32)
    else:
        out_spec, out_shape = pl.BlockSpec((M, tn), lambda n, t: (0, n)), _sds((M, N), F32)

    def body(a_ref, b_ref, o_ref):
        @pl.when(pl.program_id(1) == 0)
        def _():
            o_ref[...] = jnp.zeros_like(o_ref)

        o_ref[...] += _dot_tn(a_ref[...], b_ref[...])

    return pl.pallas_call(
        body,
        name=name,
        grid=(nblk, T // tk),
        in_specs=[pl.BlockSpec((tk, M), lambda n, t: (t, 0)), pl.BlockSpec((tk, tn), lambda n, t: (t, n))],
        out_specs=out_spec,
        out_shape=out_shape,
        compiler_params=_params(("parallel", "arbitrary"), 48),
    )(*_hbm(a, b))


def _dh_bwd(dproj, w_in_g, x, dyx, g_pre, token):
    T = x.shape[0]
    tm = min(512, T)

    def body(dp_ref, w_ref, x_ref, dyx_ref, g_ref, token_ref, gx_ref, dg_ref, acc):
        i, k = pl.program_id(0), pl.program_id(1)

        @pl.when(jnp.logical_and(i == 0, k == 0))
        def _():
            dg_ref[...] = jnp.zeros_like(dg_ref)

        part = _dot_nt(dp_ref[...], w_ref[...])

        @pl.when(k == 0)
        def _():
            acc[...] = part

        @pl.when(k > 0)
        def _():
            acc[...] += part

        @pl.when(k == N_CHIPS - 1)
        def _():
            xv = x_ref[...]
            dh = acc[...]
            rstd = lax.rsqrt(jnp.mean(xv * xv, axis=-1, keepdims=True) + EPS)
            nx = xv * rstd
            dhg = dh * g_ref[...]
            gx_ref[...] = dyx_ref[...] + rstd * (dhg - nx * jnp.mean(dhg * nx, axis=-1, keepdims=True))
            dg_ref[pl.ds(0, 1), :] += jnp.sum(dh * nx, axis=0, keepdims=True)

    tile = pl.BlockSpec((tm, D_MODEL), lambda i, k: (i, 0))
    return pl.pallas_call(
        body,
        name="dh_bwd",
        grid=(T // tm, N_CHIPS),
        in_specs=[
            pl.BlockSpec((tm, W_IN_SHARD), lambda i, k: (i, k)),
            pl.BlockSpec((None, D_MODEL, W_IN_SHARD), lambda i, k: (k, 0, 0)),
            tile, tile,
            pl.BlockSpec((1, D_MODEL), lambda i, k: (0, 0)),
            pl.BlockSpec((8, 128), lambda i, k: (0, 0)),
        ],
        out_specs=[tile, pl.BlockSpec((8, D_MODEL), lambda i, k: (0, 0))],
        out_shape=[_sds((T, D_MODEL), F32), _sds((8, D_MODEL), F32)],
        scratch_shapes=[pltpu.VMEM((tm, D_MODEL), F32)],
        compiler_params=_params(("arbitrary", "arbitrary"), 48),
    )(*_hbm(dproj, w_in_g, x, dyx, g_pre, token))


ELEMENTWISE_TILE_BYTES = MIB


def _row_tile(rows, cols):
    for t in (512, 256, 128, 64, 32, 16, 8):
        if rows % t == 0 and t * cols * 4 <= ELEMENTWISE_TILE_BYTES:
            return t
    return rows


def _pair_sum(g, got, core, name):
    nch, R, C = g.shape
    h = R // 2
    tr = _row_tile(h, C)
    nt = h // tr

    def body(c_ref, g_ref, got_ref, p_ref, pb_ref):
        s = g_ref[...] + got_ref[...]
        p_ref[...] = s
        pb_ref[...] = s.astype(BF16)

    blk = pl.BlockSpec((None, tr, C), lambda j, i, c_ref: (j, i, 0))
    return pl.pallas_call(
        body,
        name=name,
        grid_spec=pltpu.PrefetchScalarGridSpec(
            num_scalar_prefetch=1,
            grid=(nch, nt),
            in_specs=[pl.BlockSpec((None, tr, C), lambda j, i, c_ref: (j, c_ref[0] * nt + i, 0)), blk],
            out_specs=[blk, blk],
        ),
        out_shape=[_sds((nch, h, C), F32), _sds((nch, h, C), BF16)],
        compiler_params=_params(("parallel", "parallel"), 48),
    )(core, *_hbm(g, got))


def _chip_sum(p, got, chip_core, name):
    _, h, C = p.shape
    tr = _row_tile(h, C)
    nt = h // tr

    def body(jc_ref, p_ref, g0_ref, g1_ref, g2_ref, o_ref):
        o_ref[...] = ((p_ref[...] + g0_ref[...].astype(F32)) + g1_ref[...].astype(F32)) + g2_ref[...].astype(F32)

    rel = lambda r: pl.BlockSpec((None, tr, C), lambda i, jc_ref: (r, i, 0))
    return pl.pallas_call(
        body,
        name=name,
        grid_spec=pltpu.PrefetchScalarGridSpec(
            num_scalar_prefetch=1,
            grid=(nt,),
            in_specs=[pl.BlockSpec((None, tr, C), lambda i, jc_ref: (jc_ref[0], i, 0)), rel(0), rel(1), rel(2)],
            out_specs=pl.BlockSpec((tr, C), lambda i, jc_ref: (jc_ref[1] * nt + i, 0)),
        ),
        out_shape=_sds((2 * h, C), F32),
        compiler_params=_params(("parallel",), 48),
    )(chip_core, *_hbm(p, got, got, got))


def _place_shards(shards, chip, name):
    n = len(shards)
    tiles = [_row_tile(s.shape[0], s.shape[1]) for s in shards]
    steps = max(s.shape[0] // t for s, t in zip(shards, tiles))
    tiles = [s.shape[0] // steps for s in shards]

    def body(j_ref, *refs):
        for a in range(n):
            refs[n + a][...] = refs[a][...].astype(BF16)

    return pl.pallas_call(
        body,
        name=name,
        grid_spec=pltpu.PrefetchScalarGridSpec(
            num_scalar_prefetch=1,
            grid=(steps,),
            in_specs=[pl.BlockSpec((t, s.shape[1]), lambda i, j_ref: (i, 0)) for s, t in zip(shards, tiles)],
            out_specs=[pl.BlockSpec((None, t, s.shape[1]), lambda i, j_ref: (j_ref[0], i, 0))
                       for s, t in zip(shards, tiles)],
        ),
        out_shape=[_sds((N_CHIPS,) + s.shape, BF16) for s in shards],
        compiler_params=_params(("parallel",), 48),
    )(chip, *_hbm(*shards))


def _adamw(w, g, m, v, name):
    R, C = w.shape
    tr = _row_tile(R, C)
    c1 = 1.0 - ADAM_B1 ** ADAM_STEP
    c2 = 1.0 - ADAM_B2 ** ADAM_STEP

    def body(w_ref, g_ref, m_ref, v_ref, d_ref, nm_ref, nv_ref):
        g = g_ref[...]
        nm = ADAM_B1 * m_ref[...] + (1.0 - ADAM_B1) * g
        nv = ADAM_B2 * v_ref[...] + (1.0 - ADAM_B2) * (g * g)
        nm_ref[...] = nm
        nv_ref[...] = nv
        d_ref[...] = (-ADAM_LR) * ((nm / c1) / (jnp.sqrt(nv / c2) + ADAM_EPS) + ADAM_WD * w_ref[...])

    spec = pl.BlockSpec((tr, C), lambda i: (i, 0))
    return pl.pallas_call(
        body, name=name, grid=(R // tr,), in_specs=[spec] * 4, out_specs=[spec] * 3,
        out_shape=[_sds((R, C), F32)] * 3, compiler_params=_params(("parallel",), 48),
    )(*_hbm(w, g, m, v))


def _place():
    return lax.axis_index("x"), lax.axis_index("y"), lax.axis_index("c")


def _chip_of(x, y, r):
    return (x ^ (r >> 1), y ^ (r & 1))


ANY = pl.BlockSpec(memory_space=pl.ANY)


def _gather_weights(placed, cw8):
    nbig = len(placed)
    halves = [s.shape[1] // 2 for s in placed]
    pieces = [4, 1, 1, 1]
    rows = [h // p for h, p in zip(halves, pieces)]
    order = [(a, q) for q in range(max(pieces)) for a in range(nbig) if q < pieces[a]]
    ici_sem = {(a, q, r): 3 * i + (r - 1) for i, (a, q) in enumerate(order) for r in (1, 2, 3)}
    cw_sem = {r: 3 * len(order) + (r - 1) for r in (1, 2, 3)}
    d2d_sem = {key: 3 * len(order) + 3 + k for key, k in ici_sem.items()}
    nsem = 6 * len(order) + 3

    def body(pin_ref, pr_ref, pa_ref, po_ref, cw_ref, gin_ref, gr_ref, ga_ref, go_ref, gcw_ref, send_sems, recv_sems):
        x, y, c = _place()
        j = 2 * x + y
        dsts = [gin_ref, gr_ref, ga_ref, go_ref]

        def piece_rows(a, q, core):
            return pl.ds(pl.multiple_of(core * halves[a] + q * rows[a], 16), rows[a])

        def ici(a, q, r):
            tx, ty = _chip_of(x, y, r)
            k = ici_sem[(a, q, r)]
            region = dsts[a].at[j, piece_rows(a, q, c), :]
            return pltpu.make_async_remote_copy(
                src_ref=region, dst_ref=region, send_sem=send_sems.at[k], recv_sem=recv_sems.at[k],
                device_id=(tx, ty, c), device_id_type=MESH)

        def ici_landed(a, q, r):
            tx, ty = _chip_of(x, y, r)
            k = ici_sem[(a, q, r)]
            region = dsts[a].at[2 * tx + ty, piece_rows(a, q, c), :]
            return pltpu.make_async_remote_copy(
                src_ref=region, dst_ref=region, send_sem=send_sems.at[k], recv_sem=recv_sems.at[k],
                device_id=(tx, ty, c), device_id_type=MESH)

        def d2d(a, q, r, core):
            tx, ty = _chip_of(x, y, r)
            k = d2d_sem[(a, q, r)]
            region = dsts[a].at[2 * tx + ty, piece_rows(a, q, core), :]
            return pltpu.make_async_remote_copy(
                src_ref=region, dst_ref=region, send_sem=send_sems.at[k], recv_sem=recv_sems.at[k],
                device_id=(x, y, 1 - c), device_id_type=MESH)

        def cw_copy(r):
            tx, ty = _chip_of(x, y, r)
            k = cw_sem[r]
            return pltpu.make_async_remote_copy(
                src_ref=cw_ref, dst_ref=gcw_ref.at[j], send_sem=send_sems.at[k], recv_sem=recv_sems.at[k],
                device_id=(tx, ty, c), device_id_type=MESH)

        def cw_landed(r):
            tx, ty = _chip_of(x, y, r)
            k = cw_sem[r]
            region = gcw_ref.at[2 * tx + ty]
            return pltpu.make_async_remote_copy(
                src_ref=region, dst_ref=region, send_sem=send_sems.at[k], recv_sem=recv_sems.at[k],
                device_id=(tx, ty, c), device_id_type=MESH)

        first = [ici(a, q, r) for (a, q) in order for r in (1, 2, 3)] + [cw_copy(r) for r in (1, 2, 3)]
        for cp in first:
            cp.start()
        passed = []
        for (a, q) in order:
            for r in (1, 2, 3):
                ici_landed(a, q, r).wait_recv()
                cp = d2d(a, q, r, c)
                cp.start()
                passed.append(cp)
        for r in (1, 2, 3):
            cw_landed(r).wait_recv()
        for (a, q) in order:
            for r in (1, 2, 3):
                d2d(a, q, r, 1 - c).wait_recv()
        for cp in first + passed:
            cp.wait_send()

    return pl.pallas_call(
        body,
        name="gather_weights",
        in_specs=[ANY] * 5,
        out_specs=[ANY] * 5,
        out_shape=[_sds(s.shape, s.dtype) for s in placed] + [_sds((N_CHIPS,) + cw8.shape, cw8.dtype)],
        input_output_aliases={a: a for a in range(nbig)},
        scratch_shapes=[pltpu.SemaphoreType.DMA((nsem,)), pltpu.SemaphoreType.DMA((nsem,))],
    )(*placed, cw8)


D2D_PIECE_ROWS = 64


def _pair_exchange(grads, name):
    n = len(grads)
    halves = [g.shape[1] // 2 for g in grads]

    def body(*refs):
        g_refs, got_refs = refs[0:n], refs[n:2 * n]
        send_sems, recv_sems = refs[2 * n:]
        x, y, c = _place()

        def copy(a, src, dst):
            return pltpu.make_async_remote_copy(
                src_ref=src, dst_ref=dst, send_sem=send_sems.at[a], recv_sem=recv_sems.at[a],
                device_id=(x, y, 1 - c), device_id_type=MESH)

        for a in range(n):
            for jj in range(N_CHIPS):
                for q in range(halves[a] // D2D_PIECE_ROWS):
                    src_rows = pl.ds(pl.multiple_of((1 - c) * halves[a] + q * D2D_PIECE_ROWS, 8), D2D_PIECE_ROWS)
                    dst_rows = pl.ds(q * D2D_PIECE_ROWS, D2D_PIECE_ROWS)
                    copy(a, g_refs[a].at[jj, src_rows, :], got_refs[a].at[jj, dst_rows, :]).start()
        for a in range(n):
            sent = g_refs[a].at[:, pl.ds(pl.multiple_of((1 - c) * halves[a], 8), halves[a]), :]
            copy(a, sent, got_refs[a]).wait()

    return pl.pallas_call(
        body,
        name=name,
        in_specs=[ANY] * n,
        out_specs=[ANY] * n,
        out_shape=[_sds((N_CHIPS, h, g.shape[2]), F32) for g, h in zip(grads, halves)],
        scratch_shapes=[pltpu.SemaphoreType.DMA((n,)), pltpu.SemaphoreType.DMA((n,))],
    )(*grads)


HBM = pl.BlockSpec(memory_space=pltpu.HBM)
SEM = pl.BlockSpec(memory_space=pltpu.SEMAPHORE)
DATAFLOW = pltpu.SideEffectType.DATAFLOW_SIDE_EFFECTING


def _chip_copy(p_refs, land_refs, send_sems, recv_sems, a, r):
    x, y, c = _place()
    tx, ty = _chip_of(x, y, r)
    k = a * 3 + (r - 1)
    return pltpu.make_async_remote_copy(
        src_ref=p_refs[a].at[2 * tx + ty], dst_ref=land_refs[a].at[r - 1],
        send_sem=send_sems.at[k], recv_sem=recv_sems.at[k], device_id=(tx, ty, c), device_id_type=MESH)


def _chip_exchange_start(psums, name):
    n = len(psums)
    lands = [lax.empty((3,) + p.shape[1:], p.dtype) for p in psums]

    def body(*refs):
        p_refs, land_refs = refs[0:n], refs[n:2 * n]
        send_sems, recv_sems, token = refs[2 * n], refs[2 * n + 1], refs[-1]
        for a in range(n):
            for r in (1, 2, 3):
                _chip_copy(p_refs, land_refs, send_sems, recv_sems, a, r).start()
        token[...] = jnp.zeros_like(token)

    hbm = lambda t: pltpu.HBM(t.shape, t.dtype)
    keep = lambda t: pltpu.with_memory_space_constraint(t, pltpu.HBM)
    outs = pl.pallas_call(
        body,
        name=name,
        in_specs=[HBM] * (2 * n),
        out_specs=(SEM, SEM, *[HBM] * (2 * n), pl.BlockSpec(memory_space=pltpu.VMEM)),
        out_shape=(pltpu.SemaphoreType.DMA((3 * n,)), pltpu.SemaphoreType.DMA((3 * n,)),
                   *[hbm(p) for p in psums], *[hbm(l) for l in lands], _sds((8, 128), F32)),
        input_output_aliases={i: 2 + i for i in range(2 * n)},
        compiler_params=pltpu.CompilerParams(has_side_effects=DATAFLOW),
    )(*[keep(p) for p in psums], *[keep(l) for l in lands])
    return outs[0], outs[1], list(outs[2:2 + n]), list(outs[2 + n:2 + 2 * n]), outs[-1]


def _chip_exchange_wait(send_sems, recv_sems, p_thru, land_thru, after, name):
    n = len(p_thru)

    def body(*refs):
        p_refs, land_refs = refs[0:n], refs[n:2 * n]
        send_sems, recv_sems = refs[2 * n], refs[2 * n + 1]
        for a in range(n):
            for r in (1, 2, 3):
                copy = _chip_copy(p_refs, land_refs, send_sems, recv_sems, a, r)
                copy.wait_send()
                copy.wait_recv()

    hbm = lambda t: pltpu.HBM(t.shape, t.dtype)
    outs = pl.pallas_call(
        body,
        name=name,
        in_specs=[HBM] * (2 * n) + [SEM, SEM, ANY],
        out_specs=[HBM] * (2 * n),
        out_shape=[hbm(p) for p in p_thru] + [hbm(l) for l in land_thru],
        input_output_aliases={i: i for i in range(2 * n)},
        compiler_params=pltpu.CompilerParams(has_side_effects=DATAFLOW),
    )(*p_thru, *land_thru, send_sems, recv_sems, after)
    return list(outs[n:2 * n])


def _pair_share(fulls):
    n = len(fulls)
    halves = [f.shape[0] // 2 for f in fulls]

    def body(*refs):
        full_refs = refs[n:2 * n]
        send_sems, recv_sems = refs[2 * n:]
        x, y, c = _place()

        def half_of(a, core):
            return full_refs[a].at[pl.ds(pl.multiple_of(core * halves[a], 8), halves[a]), :]

        def remote(a, src, dst):
            return pltpu.make_async_remote_copy(
                src_ref=src, dst_ref=dst, send_sem=send_sems.at[a], recv_sem=recv_sems.at[a],
                device_id=(x, y, 1 - c), device_id_type=MESH)

        for a in range(n):
            for q in range(halves[a] // D2D_PIECE_ROWS):
                piece = full_refs[a].at[
                    pl.ds(pl.multiple_of(c * halves[a] + q * D2D_PIECE_ROWS, 8), D2D_PIECE_ROWS), :]
                remote(a, piece, piece).start()
        for a in range(n):
            remote(a, half_of(a, c), half_of(a, c)).wait_send()
            remote(a, half_of(a, 1 - c), half_of(a, 1 - c)).wait_recv()

    return pl.pallas_call(
        body,
        name="pair_share",
        in_specs=[ANY] * n,
        out_specs=[ANY] * n,
        out_shape=[_sds(f.shape, F32) for f in fulls],
        input_output_aliases={a: a for a in range(n)},
        scratch_shapes=[pltpu.SemaphoreType.DMA((n,)), pltpu.SemaphoreType.DMA((n,))],
    )(*fulls)


def _allreduce_small(s):
    R, C = s.shape

    def body(s_ref, o_ref, sib, chips, send_sems, recv_sems):
        x, y, c = _place()
        j = 2 * x + y
        def to_sib(src, dst):
            return pltpu.make_async_remote_copy(
                src_ref=src, dst_ref=dst, send_sem=send_sems.at[0], recv_sem=recv_sems.at[0],
                device_id=(x, y, 1 - c), device_id_type=MESH)

        for q in range(R // 8):
            to_sib(s_ref.at[pl.ds(8 * q, 8), :], sib.at[pl.ds(8 * q, 8), :]).start()
        to_sib(s_ref, sib).wait()
        chips[j] = s_ref[...] + sib[...]
        sends = []
        for r in (1, 2, 3):
            tx, ty = _chip_of(x, y, r)
            cp = pltpu.make_async_remote_copy(
                src_ref=chips.at[j], dst_ref=chips.at[j], send_sem=send_sems.at[r], recv_sem=recv_sems.at[r],
                device_id=(tx, ty, c), device_id_type=MESH)
            cp.start()
            sends.append(cp)
        for r in (1, 2, 3):
            tx, ty = _chip_of(x, y, r)
            region = chips.at[2 * tx + ty]
            pltpu.make_async_remote_copy(
                src_ref=region, dst_ref=region, send_sem=send_sems.at[r], recv_sem=recv_sems.at[r],
                device_id=(tx, ty, c), device_id_type=MESH).wait_recv()
        for cp in sends:
            cp.wait_send()
        o_ref[...] = (chips[0] + chips[1]) + (chips[2] + chips[3])

    return pl.pallas_call(
        body,
        name="allreduce_small",
        in_specs=[pl.BlockSpec(memory_space=pltpu.VMEM)],
        out_specs=pl.BlockSpec(memory_space=pltpu.VMEM),
        out_shape=jax.ShapeDtypeStruct((R, C), F32),
        scratch_shapes=[pltpu.VMEM((R, C), F32), pltpu.VMEM((N_CHIPS, R, C), F32),
                        pltpu.SemaphoreType.DMA((4,)), pltpu.SemaphoreType.DMA((4,))],
    )(s)


def _block_diag(w):
    w4 = w.reshape(4, 4, RNN_BLOCK_W, RNN_BLOCK_W)
    eye = jnp.eye(4, dtype=w.dtype)
    return jnp.einsum("jaik,ab->jaibk", w4, eye).reshape(4, RNN_TILE, RNN_TILE)


def _block_diag_part(d):
    d5 = d.reshape(4, 4, RNN_BLOCK_W, 4, RNN_BLOCK_W)
    return jnp.stack([d5[:, a, :, a, :] for a in range(4)], axis=1).reshape(RNN_BLOCKS, RNN_BLOCK_W, RNN_BLOCK_W)


def _local_grads(x, target, g_pre, w_in_g, b_gate, conv_w, conv_b, w_rg_a, b_rg_a, w_rg_x, b_rg_x, lam, sinks,
                 w_rnn_out, w_attn_out, w_out, g_post, on_out_grads, on_w_in_grad):
    wa_bd = _block_diag(w_rg_a).astype(BF16)
    wx_bd = _block_diag(w_rg_x).astype(BF16)
    b_a = b_rg_a.reshape(1, D_RNN)
    b_x = b_rg_x.reshape(1, D_RNN)

    proj, h = _proj_fwd(x, g_pre, w_in_g)
    y_rnn, z_rnn = _rnn_fwd(proj, conv_w, conv_b, wa_bd, wx_bd, b_a, b_x, lam)
    bias = _attn_bias()
    y_attn, z_attn, lse = _attn_fwd(proj, sinks, bias)
    dyx, dz_rnn, dz_attn, dml, merged, dout, dbr_rnn, dbr_attn, head_small = _head(
        x, target, z_rnn, z_attn, proj, b_gate, g_post, w_rnn_out, w_attn_out, w_out)
    dw_out = _matmul_tn(merged, dout, "dw_out", 2, False)
    dw_rnn_out = _matmul_tn(z_rnn, dbr_rnn, "dw_rnn_out", 2, False)
    dw_attn_out = _matmul_tn(z_attn, dbr_attn, "dw_attn_out", 2, False)
    shard_rows = lambda d: d.reshape(N_CHIPS, OUT_SHARD, D_MODEL)
    token = on_out_grads([shard_rows(dw_rnn_out), shard_rows(dw_attn_out), shard_rows(dw_out)])
    dq, dk, dv, dag, attn_small = _attn_bwd(proj, y_attn, lse, dz_attn, sinks, bias, token)
    drx, drg, dwa_t, dwx_t, rnn_small = _rnn_bwd(proj, y_rnn, dz_rnn, conv_w, conv_b, wa_bd, wx_bd, b_a, b_x, lam)
    dproj = jnp.concatenate([drx, drg, dq, dk.astype(BF16), dv.astype(BF16), dag, dml], axis=1)
    token = on_w_in_grad(_matmul_tn(h, dproj, "dw_in", N_CHIPS, True))
    grad_x, dh_small = _dh_bwd(dproj, w_in_g, x, dyx, g_pre, token)
    small = jnp.concatenate([rnn_small, head_small, dh_small + attn_small,
                             _block_diag_part(dwa_t).reshape(64, 1024), _block_diag_part(dwx_t).reshape(64, 1024)], axis=0)
    return grad_x, small


ROW_LOSS = 11


def _rows8(parts):
    out = None
    for r, a in parts:
        p = jnp.pad(a, ((r, 8 - r - a.shape[0]), (0, 1024 - a.shape[1])))
        out = p if out is None else out + p
    return out


def _pack_small(p):
    g0 = _rows8([(0, p["b_rg_a"].reshape(1, 1024)), (1, p["b_rg_x"].reshape(1, 1024)), (2, p["lru_lambda"]),
                 (3, p["conv_b"]), (4, p["conv_w"][0])])
    g1 = _rows8([(0, p["post_norm_g"]), (1, p["b_gate"].reshape(2, 1024))])
    g2 = _rows8([(0, p["pre_norm_g"]), (1, p["attn_sinks"])])
    return jnp.concatenate([g0, g1, g2, p["w_rg_a"].reshape(64, 1024), p["w_rg_x"].reshape(64, 1024)], axis=0)


def _unpack_small(s, conv_cols):
    return {
        "b_rg_a": s[0:1].reshape(1, 16, 64), "b_rg_x": s[1:2].reshape(1, 16, 64), "lru_lambda": s[2:3],
        "conv_b": s[3:4], "conv_w": s[4:8, 0:conv_cols].reshape(1, CONV_W, conv_cols),
        "post_norm_g": s[8:9], "b_gate": s[9:11].reshape(1, 2048),
        "pre_norm_g": s[16:17], "attn_sinks": s[17:18, 0:N_Q_HEADS],
        "w_rg_a": s[24:88].reshape(1, 16, 64, 64), "w_rg_x": s[88:152].reshape(1, 16, 64, 64),
    }


WEIGHTS = ["pre_norm_g", "w_in", "b_gate", "conv_w", "conv_b", "w_rg_a", "b_rg_a", "w_rg_x", "b_rg_x", "lru_lambda",
           "attn_sinks", "w_rnn_out", "w_attn_out", "w_out", "post_norm_g"]
BIG = ["w_in", "w_rnn_out", "w_attn_out", "w_out"]


def kernel(x, pre_norm_g, w_in, b_gate, conv_w, conv_b, w_rg_a, b_rg_a, w_rg_x, b_rg_x, lru_lambda, attn_sinks, w_rnn_out, w_attn_out, w_out, post_norm_g, loss_target, m_pre_norm_g, m_w_in, m_b_gate, m_conv_w, m_conv_b, m_w_rg_a, m_b_rg_a, m_w_rg_x, m_b_rg_x, m_lru_lambda, m_attn_sinks, m_w_rnn_out, m_w_attn_out, m_w_out, m_post_norm_g, v_pre_norm_g, v_w_in, v_b_gate, v_conv_w, v_conv_b, v_w_rg_a, v_b_rg_a, v_w_rg_x, v_b_rg_x, v_lru_lambda, v_attn_sinks, v_w_rnn_out, v_w_attn_out, v_w_out, v_post_norm_g):
    w = dict(pre_norm_g=pre_norm_g, w_in=w_in, b_gate=b_gate, conv_w=conv_w, conv_b=conv_b, w_rg_a=w_rg_a,
             b_rg_a=b_rg_a, w_rg_x=w_rg_x, b_rg_x=b_rg_x, lru_lambda=lru_lambda, attn_sinks=attn_sinks,
             w_rnn_out=w_rnn_out, w_attn_out=w_attn_out, w_out=w_out, post_norm_g=post_norm_g)
    m = dict(pre_norm_g=m_pre_norm_g, w_in=m_w_in, b_gate=m_b_gate, conv_w=m_conv_w, conv_b=m_conv_b, w_rg_a=m_w_rg_a,
             b_rg_a=m_b_rg_a, w_rg_x=m_w_rg_x, b_rg_x=m_b_rg_x, lru_lambda=m_lru_lambda, attn_sinks=m_attn_sinks,
             w_rnn_out=m_w_rnn_out, w_attn_out=m_w_attn_out, w_out=m_w_out, post_norm_g=m_post_norm_g)
    v = dict(pre_norm_g=v_pre_norm_g, w_in=v_w_in, b_gate=v_b_gate, conv_w=v_conv_w, conv_b=v_conv_b, w_rg_a=v_w_rg_a,
             b_rg_a=v_b_rg_a, w_rg_x=v_w_rg_x, b_rg_x=v_b_rg_x, lru_lambda=v_lru_lambda, attn_sinks=v_attn_sinks,
             w_rnn_out=v_w_rnn_out, w_attn_out=v_w_attn_out, w_out=v_w_out, post_norm_g=v_post_norm_g)
    chip = 2 * lax.axis_index("x") + lax.axis_index("y")

    chip_idx = chip.astype(jnp.int32).reshape(1)
    chip_core = jnp.stack([chip, lax.axis_index("c")]).astype(jnp.int32)
    cw8 = jnp.pad(conv_w[0], ((0, 8 - CONV_W), (0, 0)))
    placed = _place_shards([w_in[0], w_rnn_out[0], w_attn_out[0], w_out[0]], chip_idx, "place_shards")
    win_g, wr_g, wa_g, wo_g, cw_g = _gather_weights(placed, cw8)
    cw_g = lax.dynamic_update_slice_in_dim(cw_g, cw8[None], chip, axis=0)
    conv_w_full = jnp.transpose(cw_g[:, 0:CONV_W, :], (1, 0, 2)).reshape(CONV_W, D_RNN)

    core_idx = lax.axis_index("c").astype(jnp.int32).reshape(1)
    started = {}

    def start_reduction(tag, grads):
        got = _pair_exchange(grads, "pair_exchange_" + tag)
        sums = [_pair_sum(g, o, core_idx, "pair_sum_%s_%d" % (tag, a)) for a, (g, o) in enumerate(zip(grads, got))]
        send_sems, recv_sems, p_thru, land_thru, token = _chip_exchange_start(
            [pb for _, pb in sums], "chip_exchange_start_" + tag)
        started[tag] = ([p for p, _ in sums], send_sems, recv_sems, p_thru, land_thru)
        return token

    def end_reduction(tag, after):
        psums, send_sems, recv_sems, p_thru, land_thru = started[tag]
        landed = _chip_exchange_wait(send_sems, recv_sems, p_thru, land_thru, after, "chip_exchange_wait_" + tag)
        return [_chip_sum(p, l, chip_core, "chip_sum_%s_%d" % (tag, a)) for a, (p, l) in enumerate(zip(psums, landed))]

    grad_x, small = _local_grads(
        x[0], loss_target[0], pre_norm_g, win_g, b_gate, conv_w_full, conv_b, w_rg_a[0], b_rg_a[0], w_rg_x[0],
        b_rg_x[0], lru_lambda, attn_sinks[0], wr_g.reshape(D_MODEL, D_MODEL), wa_g.reshape(D_MODEL, D_MODEL),
        wo_g.reshape(D_MODEL, D_MODEL), post_norm_g,
        on_out_grads=lambda grads: start_reduction("out", grads),
        on_w_in_grad=lambda grad: start_reduction("in", [grad]))

    halves = end_reduction("in", grad_x) + end_reduction("out", grad_x)
    gbig = dict(zip(BIG, _pair_share(halves)))

    small_sum = _allreduce_small(small)
    total_loss = small_sum[ROW_LOSS, 0]
    gsmall = _unpack_small(small_sum, D_RNN)
    conv_shard = D_RNN // N_CHIPS
    gsmall["conv_w"] = lax.dynamic_slice_in_dim(gsmall["conv_w"], chip * conv_shard, conv_shard, axis=2)

    grads, delta, new_m, new_v = {}, {}, {}, {}
    for n in BIG:
        grads[n] = gbig[n][None]
        d, nm, nv = _adamw(w[n][0], gbig[n], m[n][0], v[n][0], "adamw_" + n)
        delta[n], new_m[n], new_v[n] = d[None], nm[None], nv[None]
    pick = lambda t: {k: t[k] for k in gsmall}
    d, nm, nv = _adamw(_pack_small(pick(w)), _pack_small(gsmall), _pack_small(pick(m)), _pack_small(pick(v)),
                       "adamw_small")
    ud, um, uv = _unpack_small(d, conv_shard), _unpack_small(nm, conv_shard), _unpack_small(nv, conv_shard)
    for n in gsmall:
        grads[n] = gsmall[n].reshape(w[n].shape)
        delta[n] = ud[n].reshape(w[n].shape)
        new_m[n] = um[n].reshape(w[n].shape)
        new_v[n] = uv[n].reshape(w[n].shape)

    return (total_loss, grad_x[None], *[grads[n] for n in WEIGHTS], *[delta[n] for n in WEIGHTS],
            *[new_m[n] for n in WEIGHTS], *[new_v[n] for n in WEIGHTS])
```

```python
import functools
import math

import jax
import jax.numpy as jnp
from jax import lax
from jax.experimental import pallas as pl
from jax.experimental.pallas import tpu as pltpu

F32 = jnp.float32
BF16 = jnp.bfloat16

D_MODEL = 1024
D_RNN = 1024
RNN_BLOCKS = 16
RNN_BLOCK_W = 64
CONV_W = 4
LRU_C = 8.0
N_Q_HEADS = 16
N_KV_HEADS = 4
GROUP = 4
HEAD_DIM = 64
D_KV = 256
BLOCK = 128
ALIBI_MAX_BIAS = 8.0
EPS = 1e-6
D_IN = 6656
N_CHIPS = 4
W_IN_SHARD = D_IN // N_CHIPS
OUT_SHARD = D_MODEL // N_CHIPS
ADAM_LR = 0.001
ADAM_B1 = 0.9
ADAM_B2 = 0.999
ADAM_EPS = 1e-08
ADAM_WD = 0.01
ADAM_STEP = 10
NEG_BIG = -1e30
MIB = 1 << 20

COL_RNN_X = 0
COL_RNN_GATE = 4
COL_Q = 8
COL_K = 12
COL_V = 13
COL_ATTN_GATE = 14
COL_MERGE = 18

RNN_TILE = 256
RNN_CHUNK = 256
SMALL_ROWS = 152
MESH = pl.DeviceIdType.MESH


def _sds(shape, dtype):
    return pltpu.HBM(shape, dtype)


def _params(sem=None, vmem_mib=None):
    kw = {}
    if sem is not None:
        kw["dimension_semantics"] = sem
    if vmem_mib is not None:
        kw["vmem_limit_bytes"] = vmem_mib * MIB
    return pltpu.CompilerParams(**kw)


def _hbm(*arrays):
    return [pltpu.with_memory_space_constraint(a, pltpu.HBM) for a in arrays]


def _dot(a, b):
    return jnp.dot(a, b, preferred_element_type=F32)


def _dot_nt(a, b):
    return lax.dot_general(a, b, (((1,), (1,)), ((), ())), preferred_element_type=F32)


def _dot_tn(a, b):
    return lax.dot_general(a, b, (((0,), (0,)), ((), ())), preferred_element_type=F32)


def _sigmoid(x):
    return 0.5 * jnp.tanh(0.5 * x) + 0.5


def _sigmoid_small(x):
    return 1.0 / (1.0 + jnp.exp(-x))


def _softplus(x):
    return jnp.maximum(x, 0.0) + jnp.log(1.0 + jnp.exp(-jnp.abs(x)))


def _one_minus_square(a, log_a):
    return -jnp.tanh(log_a) * (a * a + 1.0)


def _proj_fwd(x, g_pre, w_in_g):
    T = x.shape[0]
    tm = min(1024, T)

    def body(x_ref, g_ref, w_ref, proj_ref, h_ref):
        @pl.when(pl.program_id(1) == 0)
        def _():
            xv = x_ref[...]
            rstd = lax.rsqrt(jnp.mean(xv * xv, axis=-1, keepdims=True) + EPS)
            h_ref[...] = ((xv * rstd) * g_ref[...]).astype(BF16)

        proj_ref[...] = _dot(h_ref[...], w_ref[...])

    return pl.pallas_call(
        body,
        name="proj_fwd",
        grid=(T // tm, N_CHIPS),
        in_specs=[
            pl.BlockSpec((tm, D_MODEL), lambda i, j: (i, 0)),
            pl.BlockSpec((1, D_MODEL), lambda i, j: (0, 0)),
            pl.BlockSpec((None, D_MODEL, W_IN_SHARD), lambda i, j: (j, 0, 0)),
        ],
        out_specs=[
            pl.BlockSpec((tm, W_IN_SHARD), lambda i, j: (i, j)),
            pl.BlockSpec((tm, D_MODEL), lambda i, j: (i, 0)),
        ],
        out_shape=[_sds((T, D_IN), F32), _sds((T, D_MODEL), BF16)],
        compiler_params=_params(("parallel", "arbitrary"), 48),
    )(*_hbm(x, g_pre, w_in_g))


def _shift_down(x, tail, s, row):
    n = x.shape[0]
    xs = pltpu.roll(x, s, 0)
    tail_t = jnp.tile(pltpu.roll(tail, s, 0), (n // 8, 1))
    return jnp.where(row < s, tail_t, xs)


def _shift_up(x, head, s, row):
    n = x.shape[0]
    xs = pltpu.roll(x, n - s, 0)
    head_t = jnp.tile(pltpu.roll(head, 8 - s, 0), (n // 8, 1))
    return jnp.where(row >= n - s, head_t, xs)


def _conv_taps(x, tail, row):
    return [_shift_down(x, tail, 3, row), _shift_down(x, tail, 2, row), _shift_down(x, tail, 1, row), x]


def _rglru_gates(c, wa, wx, ba, bx, lam):
    cb = c.astype(BF16)
    r = _sigmoid_small(_dot(cb, wa) + ba)
    i = _sigmoid(_dot(cb, wx) + bx)
    log_a = (-LRU_C) * r * _softplus(-lam)
    a = jnp.exp(log_a)
    mult = jnp.sqrt(_one_minus_square(a, log_a))
    return cb, r, i, a, mult


SUBLANES = 8


def _scan_down(a, u, row):
    n = a.shape[0]
    s = 1
    while s < SUBLANES:
        a_sh = jnp.where(row >= s, pltpu.roll(a, s, 0), 1.0)
        u_sh = jnp.where(row >= s, pltpu.roll(u, s, 0), 0.0)
        u = a * u_sh + u
        a = a * a_sh
        s *= 2
    while s < n:
        u = jnp.concatenate([u[:s], a[s:] * u[:n - s] + u[s:]], axis=0)
        a = jnp.concatenate([a[:s], a[s:] * a[:n - s]], axis=0)
        s *= 2
    return a, u


def _scan_up(b, u, row):
    n = b.shape[0]
    s = 1
    while s < SUBLANES:
        b_sh = jnp.where(row < n - s, pltpu.roll(b, n - s, 0), 1.0)
        u_sh = jnp.where(row < n - s, pltpu.roll(u, n - s, 0), 0.0)
        u = b * u_sh + u
        b = b * b_sh
        s *= 2
    while s < n:
        u = jnp.concatenate([b[:n - s] * u[s:] + u[:n - s], u[n - s:]], axis=0)
        b = jnp.concatenate([b[:n - s] * b[s:], b[n - s:]], axis=0)
        s *= 2
    return b, u


def _rnn_fwd(proj, conv_w, conv_b, wa_bd, wx_bd, b_a, b_x, lam, token):
    T = proj.shape[0]
    tc, ct = RNN_CHUNK, RNN_TILE
    nt = T // tc

    def body(x_ref, rg_ref, cw_ref, cb_ref, wa_ref, wx_ref, ba_ref, bx_ref, lam_ref, token_ref, h_ref, z_ref, xtail,
             hcarry):
        @pl.when(pl.program_id(1) == 0)
        def _():
            xtail[...] = jnp.zeros_like(xtail)
            hcarry[...] = jnp.zeros_like(hcarry)

        row = lax.broadcasted_iota(jnp.int32, (tc, ct), 0)
        x = x_ref[...]
        taps = _conv_taps(x, xtail[...], row)
        c = cb_ref[...] + cw_ref[pl.ds(0, 1), :] * taps[0]
        for k in range(1, CONV_W):
            c = c + cw_ref[pl.ds(k, 1), :] * taps[k]
        xtail[...] = x_ref[pl.ds(tc - 8, 8), :]
        _, _, i, a, mult = _rglru_gates(c, wa_ref[...], wx_ref[...], ba_ref[...], bx_ref[...], lam_ref[...])
        u = mult * (i * c)
        a_cum, h0 = _scan_down(a, u, row)
        h = h0 + a_cum * hcarry[...]
        h_ref[...] = h
        hcarry[...] = h_ref[pl.ds(tc - 1, 1), :]
        rg = rg_ref[...]
        z_ref[...] = (h * (rg * _sigmoid(rg))).astype(BF16)

    col = lambda off: (lambda j, t: (t, off + j))
    vec = pl.BlockSpec((1, ct), lambda j, t: (0, j))
    mat = pl.BlockSpec((None, ct, ct), lambda j, t: (j, 0, 0))
    return pl.pallas_call(
        body,
        name="rnn_fwd",
        grid=(D_RNN // ct, nt),
        in_specs=[
            pl.BlockSpec((tc, ct), col(COL_RNN_X)),
            pl.BlockSpec((tc, ct), col(COL_RNN_GATE)),
            pl.BlockSpec((CONV_W, ct), lambda j, t: (0, j)),
            vec, mat, mat, vec, vec, vec,
            pl.BlockSpec((8, 128), lambda j, t: (0, 0)),
        ],
        out_specs=[pl.BlockSpec((tc, ct), lambda j, t: (t, j)), pl.BlockSpec((tc, ct), lambda j, t: (t, j))],
        out_shape=[_sds((T, D_RNN), F32), _sds((T, D_RNN), BF16)],
        scratch_shapes=[pltpu.VMEM((8, ct), F32), pltpu.VMEM((1, ct), F32)],
        compiler_params=_params(("parallel", "arbitrary"), 32),
    )(*_hbm(proj, proj, conv_w, conv_b, wa_bd, wx_bd, b_a, b_x, lam, token))


def _rnn_bwd(proj, y_rnn, dz_rnn, conv_w, conv_b, wa_bd, wx_bd, b_a, b_x, lam):
    T = proj.shape[0]
    tc, ct = RNN_CHUNK, RNN_TILE
    nt = T // tc
    hb = tc // 8

    def body(x_ref, xh_ref, rg_ref, h_ref, hh_ref, dz_ref, cw_ref, cb_ref, wa_ref, wx_ref, ba_ref, bx_ref, lam_ref,
             dx_ref, drg_ref, dwa_ref, dwx_ref, sm_ref, lam_carry, a_carry, dc_head):
        t = pl.program_id(1)
        first_chunk = t == nt - 1

        @pl.when(t == 0)
        def _():
            lam_carry[...] = jnp.zeros_like(lam_carry)
            a_carry[...] = jnp.zeros_like(a_carry)
            dc_head[...] = jnp.zeros_like(dc_head)
            dwa_ref[...] = jnp.zeros_like(dwa_ref)
            dwx_ref[...] = jnp.zeros_like(dwx_ref)
            sm_ref[...] = jnp.zeros_like(sm_ref)

        row = lax.broadcasted_iota(jnp.int32, (tc, ct), 0)
        keep = jnp.where(first_chunk, 0.0, 1.0)
        x = x_ref[...]
        xtail = xh_ref[...] * keep
        taps = _conv_taps(x, xtail, row)
        c = cb_ref[...] + cw_ref[pl.ds(0, 1), :] * taps[0]
        for k in range(1, CONV_W):
            c = c + cw_ref[pl.ds(k, 1), :] * taps[k]
        lam = lam_ref[...]
        cb, r, i, a, mult = _rglru_gates(c, wa_ref[...], wx_ref[...], ba_ref[...], bx_ref[...], lam)
        h = h_ref[...]
        h_prev = _shift_down(h, hh_ref[...] * keep, 1, row)
        rg = rg_ref[...]
        dz = dz_ref[...]
        sg = _sigmoid(rg)
        drg_ref[...] = (dz * h * (sg * (1.0 + rg * (1.0 - sg)))).astype(BF16)
        dy = dz * (rg * sg)
        b = jnp.where(row >= tc - 1, a_carry[pl.ds(0, 1), :], pltpu.roll(a, tc - 1, 0))
        b_cum, l0 = _scan_up(b, dy, row)
        lt = l0 + b_cum * lam_carry[pl.ds(0, 1), :]
        lam_carry[...] = lt[0:8, :]
        a_carry[...] = a[0:8, :]
        ic = i * c
        dmult = lt * ic
        di = lt * mult * c
        dc = lt * mult * i
        dlog_a = a * (lt * h_prev - dmult * a / mult)
        sp = _softplus(-lam)
        dpre_r = dlog_a * ((-LRU_C) * sp) * (r * (1.0 - r))
        dpre_i = di * (i * (1.0 - i))
        dlam_row = jnp.sum(dlog_a * r, axis=0, keepdims=True) * (LRU_C * _sigmoid(-lam))
        dpr_b = dpre_r.astype(BF16)
        dpi_b = dpre_i.astype(BF16)
        dwa_ref[...] += _dot_tn(cb, dpr_b)
        dwx_ref[...] += _dot_tn(cb, dpi_b)
        dc = dc + _dot_nt(dpr_b, wa_ref[...]) + _dot_nt(dpi_b, wx_ref[...])
        head = dc_head[...]
        dx = cw_ref[pl.ds(3, 1), :] * dc
        for m in range(1, CONV_W):
            dx = dx + cw_ref[pl.ds(3 - m, 1), :] * _shift_up(dc, head, m, row)
        dx_ref[...] = dx.astype(BF16)
        dc_head[...] = dc[0:8, :]
        sm_ref[pl.ds(0, 1), :] += jnp.sum(dpre_r, axis=0, keepdims=True)
        sm_ref[pl.ds(1, 1), :] += jnp.sum(dpre_i, axis=0, keepdims=True)
        sm_ref[pl.ds(2, 1), :] += dlam_row
        sm_ref[pl.ds(3, 1), :] += jnp.sum(dc, axis=0, keepdims=True)
        for k in range(CONV_W):
            sm_ref[pl.ds(4 + k, 1), :] += jnp.sum(dc * taps[k], axis=0, keepdims=True)

    rev = lambda off: (lambda j, t: (nt - 1 - t, off + j))
    halo = lambda off: (lambda j, t: (jnp.maximum((nt - 1 - t) * hb - 1, 0), off + j))
    vec = pl.BlockSpec((1, ct), lambda j, t: (0, j))
    mat = pl.BlockSpec((None, ct, ct), lambda j, t: (j, 0, 0))
    return pl.pallas_call(
        body,
        name="rnn_bwd",
        grid=(D_RNN // ct, nt),
        in_specs=[
            pl.BlockSpec((tc, ct), rev(COL_RNN_X)),
            pl.BlockSpec((8, ct), halo(COL_RNN_X)),
            pl.BlockSpec((tc, ct), rev(COL_RNN_GATE)),
            pl.BlockSpec((tc, ct), rev(0)),
            pl.BlockSpec((8, ct), halo(0)),
            pl.BlockSpec((tc, ct), rev(0)),
            pl.BlockSpec((CONV_W, ct), lambda j, t: (0, j)),
            vec, mat, mat, vec, vec, vec,
        ],
        out_specs=[
            pl.BlockSpec((tc, ct), rev(0)),
            pl.BlockSpec((tc, ct), rev(0)),
            mat, mat,
            pl.BlockSpec((8, ct), lambda j, t: (0, j)),
        ],
        out_shape=[_sds((T, D_RNN), BF16), _sds((T, D_RNN), BF16), _sds((D_RNN // ct, ct, ct), F32),
                   _sds((D_RNN // ct, ct, ct), F32), _sds((8, D_RNN), F32)],
        scratch_shapes=[pltpu.VMEM((8, ct), F32), pltpu.VMEM((8, ct), F32), pltpu.VMEM((8, ct), F32)],
        compiler_params=_params(("parallel", "arbitrary"), 32),
    )(*_hbm(proj, proj, proj, y_rnn, y_rnn, dz_rnn, conv_w, conv_b, wa_bd, wx_bd, b_a, b_x, lam))


def _attn_bias():
    qi = jnp.arange(BLOCK)[:, None]
    kj = jnp.arange(BLOCK)[None, :]
    dist_cur = (qi - kj).astype(F32)
    slopes = 2.0 ** (-ALIBI_MAX_BIAS * jnp.arange(1, N_Q_HEADS + 1, dtype=F32) / N_Q_HEADS)
    slopes = slopes[:, None, None]
    prev = jnp.where(kj > qi, -slopes * (dist_cur + float(BLOCK)), NEG_BIG)
    cur = jnp.where(kj <= qi, -slopes * dist_cur, NEG_BIG)
    later = jnp.concatenate([prev, cur], axis=-1)
    first = jnp.concatenate([jnp.full_like(prev, NEG_BIG), cur], axis=-1)
    return jnp.stack([first, later])


def _attn_exps(s_prev, s_cur, sink, bias):
    s_prev = s_prev + bias[:, 0:BLOCK]
    s_cur = s_cur + bias[:, BLOCK:2 * BLOCK]
    m = jnp.maximum(jnp.max(jnp.maximum(s_prev, s_cur), axis=-1, keepdims=True), sink)
    p_prev = jnp.exp(s_prev - m)
    p_cur = jnp.exp(s_cur - m)
    total = jnp.sum(p_prev + p_cur, axis=-1, keepdims=True) + jnp.exp(sink - m)
    return p_prev, p_cur, 1.0 / total, m + jnp.log(total)


def _attn_probs(s_prev, s_cur, sink, bias, lse):
    p_prev = jnp.exp((s_prev + bias[:, 0:BLOCK]) - lse)
    p_cur = jnp.exp((s_cur + bias[:, BLOCK:2 * BLOCK]) - lse)
    return p_prev, p_cur, jnp.exp(sink - lse)


def _stack_heads(ref_or_val, hk, dtype):
    parts = [ref_or_val[:, (GROUP * hk + g) * HEAD_DIM:(GROUP * hk + g + 1) * HEAD_DIM] for g in range(GROUP)]
    return jnp.concatenate(parts, axis=0).astype(dtype)


ATTN_SCALE = HEAD_DIM ** -0.5


def _bias_spec():
    return pl.BlockSpec((None, N_Q_HEADS, BLOCK, 2 * BLOCK), lambda i: (jnp.minimum(i, 1), 0, 0, 0))


def _attn_fwd(proj, sinks, bias):
    T = proj.shape[0]
    nb = T // BLOCK

    def body(sink_ref, bias_ref, q_ref, kp_ref, kc_ref, vp_ref, vc_ref, ag0_ref, ag1_ref, y_ref, z_ref, lse_ref):
        kvs = [slice(hk * HEAD_DIM, (hk + 1) * HEAD_DIM) for hk in range(N_KV_HEADS)]
        qgs = [(_stack_heads(q_ref, hk, F32) * ATTN_SCALE).astype(BF16) for hk in range(N_KV_HEADS)]
        s_prev = [_dot_nt(qgs[hk], kp_ref[:, kvs[hk]].astype(BF16)) for hk in range(N_KV_HEADS)]
        s_cur = [_dot_nt(qgs[hk], kc_ref[:, kvs[hk]].astype(BF16)) for hk in range(N_KV_HEADS)]
        for hk in range(N_KV_HEADS):
            pp, pc, invs = [], [], []
            for g in range(GROUP):
                h = GROUP * hk + g
                rows = slice(g * BLOCK, (g + 1) * BLOCK)
                p_prev, p_cur, inv, lse = _attn_exps(s_prev[hk][rows], s_cur[hk][rows], sink_ref[h], bias_ref[h])
                pp.append(p_prev.astype(BF16))
                pc.append(p_cur.astype(BF16))
                invs.append(inv)
                lse_ref[:, h:h + 1] = lse
            og = _dot(jnp.concatenate(pp, axis=0), vp_ref[:, kvs[hk]].astype(BF16)) + _dot(
                jnp.concatenate(pc, axis=0), vc_ref[:, kvs[hk]].astype(BF16))
            for g in range(GROUP):
                h = GROUP * hk + g
                y_ref[:, h * HEAD_DIM:(h + 1) * HEAD_DIM] = og[g * BLOCK:(g + 1) * BLOCK] * invs[g]
        ag = jnp.concatenate([ag0_ref[...], ag1_ref[...]], axis=1)
        z_ref[...] = (y_ref[...] * (ag * _sigmoid(ag))).astype(BF16)

    prev = lambda c: (lambda i: (jnp.maximum(i - 1, 0), c))
    cur = lambda c: (lambda i: (i, c))
    return pl.pallas_call(
        body,
        name="attn_fwd",
        grid=(nb,),
        in_specs=[
            pl.BlockSpec(memory_space=pltpu.SMEM),
            _bias_spec(),
            pl.BlockSpec((BLOCK, 1024), lambda i: (i, COL_Q // 4)),
            pl.BlockSpec((BLOCK, D_KV), prev(COL_K)),
            pl.BlockSpec((BLOCK, D_KV), cur(COL_K)),
            pl.BlockSpec((BLOCK, D_KV), prev(COL_V)),
            pl.BlockSpec((BLOCK, D_KV), cur(COL_V)),
            pl.BlockSpec((BLOCK, 512), lambda i: (i, COL_ATTN_GATE // 2)),
            pl.BlockSpec((BLOCK, 512), lambda i: (i, COL_ATTN_GATE // 2 + 1)),
        ],
        out_specs=[pl.BlockSpec((BLOCK, 1024), lambda i: (i, 0)), pl.BlockSpec((BLOCK, 1024), lambda i: (i, 0)),
                   pl.BlockSpec((BLOCK, N_Q_HEADS), lambda i: (i, 0))],
        out_shape=[_sds((T, 1024), F32), _sds((T, 1024), BF16), _sds((T, N_Q_HEADS), F32)],
        compiler_params=_params(("arbitrary",), 32),
    )(sinks, *_hbm(bias, proj, proj, proj, proj, proj, proj, proj))


def _attn_bwd(proj, y_attn, lse, dz_attn, sinks, bias, token):
    T = proj.shape[0]
    nb = T // BLOCK

    def body(sink_ref, bias_ref, q_ref, kp_ref, kc_ref, vp_ref, vc_ref, ag0_ref, ag1_ref, y_ref, lse_ref, dz_ref,
             token_ref, dq_ref, dk_ref, dv_ref, dag_ref, ds_ref, dy_s):
        i = pl.program_id(0)

        @pl.when(i == 0)
        def _():
            ds_ref[...] = jnp.zeros_like(ds_ref)

        lane = lax.broadcasted_iota(jnp.int32, (8, 128), 1)
        sub = lax.broadcasted_iota(jnp.int32, (8, 128), 0)
        ag = jnp.concatenate([ag0_ref[...], ag1_ref[...]], axis=1)
        dz = dz_ref[...]
        sg = _sigmoid(ag)
        dag_ref[...] = (dz * y_ref[...] * (sg * (1.0 + ag * (1.0 - sg)))).astype(BF16)
        dy_s[...] = dz * (ag * sg)
        r_cur = pl.multiple_of(i * BLOCK, BLOCK)
        r_prev = pl.multiple_of(jnp.maximum(i - 1, 0) * BLOCK, BLOCK)
        dk_cur, dv_cur, dk_prev, dv_prev = [], [], [], []
        ds_acc = jnp.zeros((8, 128), F32)
        for hk in range(N_KV_HEADS):
            ks = slice(hk * HEAD_DIM, (hk + 1) * HEAD_DIM)
            qg = (_stack_heads(q_ref, hk, F32) * ATTN_SCALE).astype(BF16)
            dog = _stack_heads(dy_s, hk, F32)
            og = _stack_heads(y_ref, hk, F32)
            dog_b = dog.astype(BF16)
            kp = kp_ref[:, ks].astype(BF16)
            kc = kc_ref[:, ks].astype(BF16)
            vp = vp_ref[:, ks].astype(BF16)
            vc = vc_ref[:, ks].astype(BF16)
            s_prev = _dot_nt(qg, kp)
            s_cur = _dot_nt(qg, kc)
            dp_prev = _dot_nt(dog_b, vp)
            dp_cur = _dot_nt(dog_b, vc)
            dvec = jnp.sum(dog * og, axis=-1, keepdims=True)
            pp, pc, dsp, dsc = [], [], [], []
            for g in range(GROUP):
                h = GROUP * hk + g
                rows = slice(g * BLOCK, (g + 1) * BLOCK)
                p_prev, p_cur, p_sink = _attn_probs(
                    s_prev[rows], s_cur[rows], sink_ref[h], bias_ref[h], lse_ref[:, h:h + 1])
                d_h = dvec[rows]
                pp.append(p_prev.astype(BF16))
                pc.append(p_cur.astype(BF16))
                dsp.append((p_prev * (dp_prev[rows] - d_h)).astype(BF16))
                dsc.append((p_cur * (dp_cur[rows] - d_h)).astype(BF16))
                dsink = -jnp.sum(p_sink * d_h, axis=0, keepdims=True)
                ds_acc = ds_acc + jnp.where(jnp.logical_and(lane == h, sub == 1), dsink, 0.0)
            pp = jnp.concatenate(pp, axis=0)
            pc = jnp.concatenate(pc, axis=0)
            dsp = jnp.concatenate(dsp, axis=0)
            dsc = jnp.concatenate(dsc, axis=0)
            dqg = (_dot(dsp, kp) + _dot(dsc, kc)) * ATTN_SCALE
            for g in range(GROUP):
                h = GROUP * hk + g
                dq_ref[:, h * HEAD_DIM:(h + 1) * HEAD_DIM] = dqg[g * BLOCK:(g + 1) * BLOCK].astype(BF16)
            dk_ref[pl.ds(r_cur, BLOCK), ks] = _dot_tn(dsc, qg)
            dv_ref[pl.ds(r_cur, BLOCK), ks] = _dot_tn(pc, dog_b)
            dk_prev.append(_dot_tn(dsp, qg))
            dv_prev.append(_dot_tn(pp, dog_b))
        ds_ref[:, 0:128] += ds_acc

        @pl.when(i > 0)
        def _():
            for hk in range(N_KV_HEADS):
                ks = slice(hk * HEAD_DIM, (hk + 1) * HEAD_DIM)
                dk_ref[pl.ds(r_prev, BLOCK), ks] += dk_prev[hk]
                dv_ref[pl.ds(r_prev, BLOCK), ks] += dv_prev[hk]

    prev = lambda c: (lambda i: (jnp.maximum(i - 1, 0), c))
    cur = lambda c: (lambda i: (i, c))
    blk = pl.BlockSpec((BLOCK, 1024), lambda i: (i, 0))
    whole = pl.BlockSpec((T, D_KV), lambda i: (0, 0))
    return pl.pallas_call(
        body,
        name="attn_bwd",
        grid=(nb,),
        in_specs=[
            pl.BlockSpec(memory_space=pltpu.SMEM),
            _bias_spec(),
            pl.BlockSpec((BLOCK, 1024), lambda i: (i, COL_Q // 4)),
            pl.BlockSpec((BLOCK, D_KV), prev(COL_K)),
            pl.BlockSpec((BLOCK, D_KV), cur(COL_K)),
            pl.BlockSpec((BLOCK, D_KV), prev(COL_V)),
            pl.BlockSpec((BLOCK, D_KV), cur(COL_V)),
            pl.BlockSpec((BLOCK, 512), lambda i: (i, COL_ATTN_GATE // 2)),
            pl.BlockSpec((BLOCK, 512), lambda i: (i, COL_ATTN_GATE // 2 + 1)),
            blk,
            pl.BlockSpec((BLOCK, N_Q_HEADS), lambda i: (i, 0)),
            blk,
            pl.BlockSpec((8, 128), lambda i: (0, 0)),
        ],
        out_specs=[blk, whole, whole, blk, pl.BlockSpec((8, 1024), lambda i: (0, 0))],
        out_shape=[_sds((T, 1024), BF16), _sds((T, D_KV), F32), _sds((T, D_KV), F32), _sds((T, 1024), BF16),
                   _sds((8, 1024), F32)],
        scratch_shapes=[pltpu.VMEM((BLOCK, 1024), F32)],
        compiler_params=_params(("arbitrary",), 48),
    )(sinks, *_hbm(bias, proj, proj, proj, proj, proj, proj, proj, y_attn, lse, dz_attn, token))


def _head(x, target, z_rnn, z_attn, proj, b_gate, g_post, w_rnn_out, w_attn_out, w_out):
    T = x.shape[0]
    tm = 256

    def body(x_ref, t_ref, zr_ref, za_ref, ml0_ref, ml1_ref, ml2_ref, ml3_ref, bg_ref, gp_ref, wr_ref, wa_ref, wo_ref,
             dyx_ref, dzr_ref, dza_ref, dml_ref, mb_ref, dout_ref, dbr_ref, dba_ref, sm_ref):
        @pl.when(pl.program_id(0) == 0)
        def _():
            sm_ref[...] = jnp.zeros_like(sm_ref)

        wr, wa, wo = wr_ref[...], wa_ref[...], wo_ref[...]
        br_rnn = _dot(zr_ref[...], wr)
        br_attn = _dot(za_ref[...], wa)
        ml_rnn = jnp.concatenate([ml0_ref[...], ml1_ref[...]], axis=1)
        ml_attn = jnp.concatenate([ml2_ref[...], ml3_ref[...]], axis=1)
        g_rnn = _sigmoid(ml_rnn + bg_ref[:, 0:D_MODEL])
        g_attn = _sigmoid(ml_attn + bg_ref[:, D_MODEL:2 * D_MODEL])
        mb = (g_rnn * br_rnn + g_attn * br_attn).astype(BF16)
        mb_ref[...] = mb
        out = _dot(mb, wo)
        rstd = lax.rsqrt(jnp.mean(out * out, axis=-1, keepdims=True) + EPS)
        n = out * rstd
        gp = gp_ref[...]
        err = (x_ref[...] + n * gp) - t_ref[...]
        sm_ref[pl.ds(3, 1), :] += 0.5 * jnp.sum(jnp.mean(err * err, axis=-1, keepdims=True), axis=0, keepdims=True)
        dy = err * (1.0 / D_MODEL)
        dyx_ref[...] = dy
        sm_ref[pl.ds(0, 1), :] += jnp.sum(dy * n, axis=0, keepdims=True)
        dn = dy * gp
        dout = (rstd * (dn - n * jnp.mean(dn * n, axis=-1, keepdims=True))).astype(BF16)
        dout_ref[...] = dout
        dmerged = _dot_nt(dout, wo)
        dml_r = (dmerged * br_rnn) * (g_rnn * (1.0 - g_rnn))
        dml_a = (dmerged * br_attn) * (g_attn * (1.0 - g_attn))
        dml_ref[:, 0:D_MODEL] = dml_r.astype(BF16)
        dml_ref[:, D_MODEL:2 * D_MODEL] = dml_a.astype(BF16)
        sm_ref[pl.ds(1, 1), :] += jnp.sum(dml_r, axis=0, keepdims=True)
        sm_ref[pl.ds(2, 1), :] += jnp.sum(dml_a, axis=0, keepdims=True)
        dbr = (dmerged * g_rnn).astype(BF16)
        dba = (dmerged * g_attn).astype(BF16)
        dbr_ref[...] = dbr
        dba_ref[...] = dba
        dzr_ref[...] = _dot_nt(dbr, wr)
        dza_ref[...] = _dot_nt(dba, wa)

    tile = pl.BlockSpec((tm, D_MODEL), lambda i: (i, 0))
    wspec = pl.BlockSpec((D_MODEL, D_MODEL), lambda i: (0, 0))
    ml = lambda q: pl.BlockSpec((tm, 512), lambda i: (i, COL_MERGE // 2 + q))
    return pl.pallas_call(
        body,
        name="head",
        grid=(T // tm,),
        in_specs=[
            tile, tile, tile, tile,
            ml(0), ml(1), ml(2), ml(3),
            pl.BlockSpec((1, 2 * D_MODEL), lambda i: (0, 0)),
            pl.BlockSpec((1, D_MODEL), lambda i: (0, 0)),
            wspec, wspec, wspec,
        ],
        out_specs=[
            tile, tile, tile,
            pl.BlockSpec((tm, 2 * D_MODEL), lambda i: (i, 0)),
            tile, tile, tile, tile,
            pl.BlockSpec((8, D_MODEL), lambda i: (0, 0)),
        ],
        out_shape=[
            _sds((T, D_MODEL), F32), _sds((T, D_MODEL), F32), _sds((T, D_MODEL), F32),
            _sds((T, 2 * D_MODEL), BF16),
            _sds((T, D_MODEL), BF16), _sds((T, D_MODEL), BF16), _sds((T, D_MODEL), BF16), _sds((T, D_MODEL), BF16),
            _sds((8, D_MODEL), F32),
        ],
        compiler_params=_params(("arbitrary",), 56),
    )(*_hbm(x, target, z_rnn, z_attn, proj, proj, proj, proj, b_gate, g_post, w_rnn_out, w_attn_out, w_out))


def _matmul_tn(a, b, name, nblk, blocked):
    T, M = a.shape
    N = b.shape[1]
    tn = N // nblk
    tk = min(512, T)
    if blocked:
        out_spec, out_shape = pl.BlockSpec((None, M, tn), lambda n, t: (n, 0, 0)), _sds((nblk, M, tn), F32)
    else:
        out_spec, out_shape = pl.BlockSpec((M, tn), lambda n, t: (0, n)), _sds((M, N), F32)

    def body(a_ref, b_ref, o_ref):
        @pl.when(pl.program_id(1) == 0)
        def _():
            o_ref[...] = jnp.zeros_like(o_ref)

        o_ref[...] += _dot_tn(a_ref[...], b_ref[...])

    return pl.pallas_call(
        body,
        name=name,
        grid=(nblk, T // tk),
        in_specs=[pl.BlockSpec((tk, M), lambda n, t: (t, 0)), pl.BlockSpec((tk, tn), lambda n, t: (t, n))],
        out_specs=out_spec,
        out_shape=out_shape,
        compiler_params=_params(("parallel", "arbitrary"), 48),
    )(*_hbm(a, b))


def _dh_bwd(dproj, w_in_g, x, dyx, g_pre, token):
    T = x.shape[0]
    tm = min(512, T)

    def body(dp_ref, w_ref, x_ref, dyx_ref, g_ref, token_ref, gx_ref, dg_ref, acc):
        i, k = pl.program_id(0), pl.program_id(1)

        @pl.when(jnp.logical_and(i == 0, k == 0))
        def _():
            dg_ref[...] = jnp.zeros_like(dg_ref)

        part = _dot_nt(dp_ref[...], w_ref[...])

        @pl.when(k == 0)
        def _():
            acc[...] = part

        @pl.when(k > 0)
        def _():
            acc[...] += part

        @pl.when(k == N_CHIPS - 1)
        def _():
            xv = x_ref[...]
            dh = acc[...]
            rstd = lax.rsqrt(jnp.mean(xv * xv, axis=-1, keepdims=True) + EPS)
            nx = xv * rstd
            dhg = dh * g_ref[...]
            gx_ref[...] = dyx_ref[...] + rstd * (dhg - nx * jnp.mean(dhg * nx, axis=-1, keepdims=True))
            dg_ref[pl.ds(0, 1), :] += jnp.sum(dh * nx, axis=0, keepdims=True)

    tile = pl.BlockSpec((tm, D_MODEL), lambda i, k: (i, 0))
    return pl.pallas_call(
        body,
        name="dh_bwd",
        grid=(T // tm, N_CHIPS),
        in_specs=[
            pl.BlockSpec((tm, W_IN_SHARD), lambda i, k: (i, k)),
            pl.BlockSpec((None, D_MODEL, W_IN_SHARD), lambda i, k: (k, 0, 0)),
            tile, tile,
            pl.BlockSpec((1, D_MODEL), lambda i, k: (0, 0)),
            pl.BlockSpec((8, 128), lambda i, k: (0, 0)),
        ],
        out_specs=[tile, pl.BlockSpec((8, D_MODEL), lambda i, k: (0, 0))],
        out_shape=[_sds((T, D_MODEL), F32), _sds((8, D_MODEL), F32)],
        scratch_shapes=[pltpu.VMEM((tm, D_MODEL), F32)],
        compiler_params=_params(("arbitrary", "arbitrary"), 48),
    )(*_hbm(dproj, w_in_g, x, dyx, g_pre, token))


ELEMENTWISE_TILE_BYTES = MIB


def _row_tile(rows, cols):
    for t in (512, 256, 128, 64, 32, 16, 8):
        if rows % t == 0 and t * cols * 4 <= ELEMENTWISE_TILE_BYTES:
            return t
    return rows


def _pair_sum(g, got, core, name):
    nch, R, C = g.shape
    h = R // 2
    tr = _row_tile(h, C)
    nt = h // tr

    def body(c_ref, g_ref, got_ref, p_ref, pb_ref):
        s = g_ref[...] + got_ref[...]
        p_ref[...] = s
        pb_ref[...] = s.astype(BF16)

    blk = pl.BlockSpec((None, tr, C), lambda j, i, c_ref: (j, i, 0))
    return pl.pallas_call(
        body,
        name=name,
        grid_spec=pltpu.PrefetchScalarGridSpec(
            num_scalar_prefetch=1,
            grid=(nch, nt),
            in_specs=[pl.BlockSpec((None, tr, C), lambda j, i, c_ref: (j, c_ref[0] * nt + i, 0)), blk],
            out_specs=[blk, blk],
        ),
        out_shape=[_sds((nch, h, C), F32), _sds((nch, h, C), BF16)],
        compiler_params=_params(("parallel", "parallel"), 48),
    )(core, *_hbm(g, got))


def _chip_sum(p, got, chip_core, name):
    _, h, C = p.shape
    tr = _row_tile(h, C)
    nt = h // tr

    def body(jc_ref, p_ref, g0_ref, g1_ref, g2_ref, o_ref):
        o_ref[...] = ((p_ref[...] + g0_ref[...].astype(F32)) + g1_ref[...].astype(F32)) + g2_ref[...].astype(F32)

    rel = lambda r: pl.BlockSpec((None, tr, C), lambda i, jc_ref: (r, i, 0))
    return pl.pallas_call(
        body,
        name=name,
        grid_spec=pltpu.PrefetchScalarGridSpec(
            num_scalar_prefetch=1,
            grid=(nt,),
            in_specs=[pl.BlockSpec((None, tr, C), lambda i, jc_ref: (jc_ref[0], i, 0)), rel(0), rel(1), rel(2)],
            out_specs=pl.BlockSpec((tr, C), lambda i, jc_ref: (jc_ref[1] * nt + i, 0)),
        ),
        out_shape=_sds((2 * h, C), F32),
        compiler_params=_params(("parallel",), 48),
    )(chip_core, *_hbm(p, got, got, got))


def _place_shards(shards, chip, name):
    n = len(shards)
    tiles = [_row_tile(s.shape[0], s.shape[1]) for s in shards]
    steps = max(s.shape[0] // t for s, t in zip(shards, tiles))
    tiles = [s.shape[0] // steps for s in shards]

    def body(j_ref, *refs):
        for a in range(n):
            refs[n + a][...] = refs[a][...].astype(BF16)

    return pl.pallas_call(
        body,
        name=name,
        grid_spec=pltpu.PrefetchScalarGridSpec(
            num_scalar_prefetch=1,
            grid=(steps,),
            in_specs=[pl.BlockSpec((t, s.shape[1]), lambda i, j_ref: (i, 0)) for s, t in zip(shards, tiles)],
            out_specs=[pl.BlockSpec((None, t, s.shape[1]), lambda i, j_ref: (j_ref[0], i, 0))
                       for s, t in zip(shards, tiles)],
        ),
        out_shape=[_sds((N_CHIPS,) + s.shape, BF16) for s in shards],
        compiler_params=_params(("parallel",), 48),
    )(chip, *_hbm(*shards))


def _adamw(w, g, m, v, name):
    R, C = w.shape
    tr = _row_tile(R, C)
    c1 = 1.0 - ADAM_B1 ** ADAM_STEP
    c2 = 1.0 - ADAM_B2 ** ADAM_STEP

    def body(w_ref, g_ref, m_ref, v_ref, d_ref, nm_ref, nv_ref):
        g = g_ref[...]
        nm = ADAM_B1 * m_ref[...] + (1.0 - ADAM_B1) * g
        nv = ADAM_B2 * v_ref[...] + (1.0 - ADAM_B2) * (g * g)
        nm_ref[...] = nm
        nv_ref[...] = nv
        d_ref[...] = (-ADAM_LR) * ((nm / c1) / (jnp.sqrt(nv / c2) + ADAM_EPS) + ADAM_WD * w_ref[...])

    spec = pl.BlockSpec((tr, C), lambda i: (i, 0))
    return pl.pallas_call(
        body, name=name, grid=(R // tr,), in_specs=[spec] * 4, out_specs=[spec] * 3,
        out_shape=[_sds((R, C), F32)] * 3, compiler_params=_params(("parallel",), 48),
    )(*_hbm(w, g, m, v))


def _place():
    return lax.axis_index("x"), lax.axis_index("y"), lax.axis_index("c")


def _chip_of(x, y, r):
    return (x ^ (r >> 1), y ^ (r & 1))


ANY = pl.BlockSpec(memory_space=pl.ANY)


def _gather_weights(placed, cw8):
    nbig = len(placed)
    halves = [s.shape[1] // 2 for s in placed]
    pieces = [max(1, h // 128) for h in halves]
    rows = [h // p for h, p in zip(halves, pieces)]
    order = [(a, q) for q in range(max(pieces)) for a in range(nbig) if q < pieces[a]]
    ici_sem = {(a, q, r): 3 * i + (r - 1) for i, (a, q) in enumerate(order) for r in (1, 2, 3)}
    cw_sem = {r: 3 * len(order) + (r - 1) for r in (1, 2, 3)}
    d2d_sem = {key: 3 * len(order) + 3 + k for key, k in ici_sem.items()}
    nsem = 6 * len(order) + 3

    def body(*refs):
        cw_ref, dsts, gcw_ref = refs[nbig], refs[nbig + 1:2 * nbig + 1], refs[2 * nbig + 1]
        send_sems, recv_sems = refs[2 * nbig + 2:]
        x, y, c = _place()
        j = 2 * x + y

        def piece_rows(a, q, core):
            return pl.ds(pl.multiple_of(core * halves[a] + q * rows[a], 16), rows[a])

        def ici(a, q, r):
            tx, ty = _chip_of(x, y, r)
            k = ici_sem[(a, q, r)]
            region = dsts[a].at[j, piece_rows(a, q, c), :]
            return pltpu.make_async_remote_copy(
                src_ref=region, dst_ref=region, send_sem=send_sems.at[k], recv_sem=recv_sems.at[k],
                device_id=(tx, ty, c), device_id_type=MESH)

        def ici_landed(a, q, r):
            tx, ty = _chip_of(x, y, r)
            k = ici_sem[(a, q, r)]
            region = dsts[a].at[2 * tx + ty, piece_rows(a, q, c), :]
            return pltpu.make_async_remote_copy(
                src_ref=region, dst_ref=region, send_sem=send_sems.at[k], recv_sem=recv_sems.at[k],
                device_id=(tx, ty, c), device_id_type=MESH)

        def d2d(a, q, r, core):
            tx, ty = _chip_of(x, y, r)
            k = d2d_sem[(a, q, r)]
            region = dsts[a].at[2 * tx + ty, piece_rows(a, q, core), :]
            return pltpu.make_async_remote_copy(
                src_ref=region, dst_ref=region, send_sem=send_sems.at[k], recv_sem=recv_sems.at[k],
                device_id=(x, y, 1 - c), device_id_type=MESH)

        def cw_copy(r):
            tx, ty = _chip_of(x, y, r)
            k = cw_sem[r]
            return pltpu.make_async_remote_copy(
                src_ref=cw_ref, dst_ref=gcw_ref.at[j], send_sem=send_sems.at[k], recv_sem=recv_sems.at[k],
                device_id=(tx, ty, c), device_id_type=MESH)

        def cw_landed(r):
            tx, ty = _chip_of(x, y, r)
            k = cw_sem[r]
            region = gcw_ref.at[2 * tx + ty]
            return pltpu.make_async_remote_copy(
                src_ref=region, dst_ref=region, send_sem=send_sems.at[k], recv_sem=recv_sems.at[k],
                device_id=(tx, ty, c), device_id_type=MESH)

        first = [ici(a, q, r) for (a, q) in order for r in (1, 2, 3)] + [cw_copy(r) for r in (1, 2, 3)]
        for cp in first:
            cp.start()
        passed = []
        for (a, q) in order:
            for r in (1, 2, 3):
                ici_landed(a, q, r).wait_recv()
                cp = d2d(a, q, r, c)
                cp.start()
                passed.append(cp)
        for r in (1, 2, 3):
            cw_landed(r).wait_recv()
        for (a, q) in order:
            for r in (1, 2, 3):
                d2d(a, q, r, 1 - c).wait_recv()
        for cp in first + passed:
            cp.wait_send()

    return pl.pallas_call(
        body,
        name="gather_weights",
        in_specs=[ANY] * (nbig + 1),
        out_specs=[ANY] * (nbig + 1),
        out_shape=[_sds(s.shape, s.dtype) for s in placed] + [_sds((N_CHIPS,) + cw8.shape, cw8.dtype)],
        input_output_aliases={a: a for a in range(nbig)},
        scratch_shapes=[pltpu.SemaphoreType.DMA((nsem,)), pltpu.SemaphoreType.DMA((nsem,))],
    )(*placed, cw8)


def _gather_late_start(placed, after, name):
    n = len(placed)
    halves = [s.shape[1] // 2 for s in placed]

    def body(*refs):
        g_refs = refs[0:n]
        send_sems, recv_sems, token = refs[n + 1], refs[n + 2], refs[-1]
        x, y, c = _place()
        j = 2 * x + y
        for a in range(n):
            mine = g_refs[a].at[j, pl.ds(pl.multiple_of(c * halves[a], 16), halves[a]), :]
            for r in (1, 2, 3):
                tx, ty = _chip_of(x, y, r)
                for to_core in (0, 1):
                    k = ((a * 3 + (r - 1)) * 2 + c) * 2 + to_core
                    pltpu.make_async_remote_copy(
                        src_ref=mine, dst_ref=mine, send_sem=send_sems.at[k], recv_sem=recv_sems.at[k],
                        device_id=(tx, ty, to_core), device_id_type=MESH).start()
        token[...] = jnp.zeros_like(token)

    hbm = lambda t: pltpu.HBM(t.shape, t.dtype)
    keep = lambda t: pltpu.with_memory_space_constraint(t, pltpu.HBM)
    nsem = 12 * n
    outs = pl.pallas_call(
        body,
        name=name,
        in_specs=[HBM] * n + [ANY],
        out_specs=(SEM, SEM, *[HBM] * n, pl.BlockSpec(memory_space=pltpu.VMEM)),
        out_shape=(pltpu.SemaphoreType.DMA((nsem,)), pltpu.SemaphoreType.DMA((nsem,)), *[hbm(p) for p in placed],
                   jax.ShapeDtypeStruct((8, 128), F32)),
        input_output_aliases={i: 2 + i for i in range(n)},
        compiler_params=pltpu.CompilerParams(has_side_effects=DATAFLOW),
    )(*[keep(p) for p in placed], after)
    return outs[0], outs[1], list(outs[2:2 + n]), outs[-1]


def _gather_late_wait(send_sems, recv_sems, thru, after, name):
    n = len(thru)
    halves = [s.shape[1] // 2 for s in thru]

    def body(*refs):
        g_refs = refs[0:n]
        send_sems, recv_sems = refs[n], refs[n + 1]
        x, y, c = _place()
        j = 2 * x + y
        for a in range(n):
            mine = g_refs[a].at[j, pl.ds(pl.multiple_of(c * halves[a], 16), halves[a]), :]
            for r in (1, 2, 3):
                tx, ty = _chip_of(x, y, r)
                for other in (0, 1):
                    k_out = ((a * 3 + (r - 1)) * 2 + c) * 2 + other
                    pltpu.make_async_remote_copy(
                        src_ref=mine, dst_ref=mine, send_sem=send_sems.at[k_out], recv_sem=recv_sems.at[k_out],
                        device_id=(tx, ty, other), device_id_type=MESH).wait_send()
                    k_in = ((a * 3 + (r - 1)) * 2 + other) * 2 + c
                    theirs = g_refs[a].at[2 * tx + ty, pl.ds(other * halves[a], halves[a]), :]
                    pltpu.make_async_remote_copy(
                        src_ref=theirs, dst_ref=theirs, send_sem=send_sems.at[k_in], recv_sem=recv_sems.at[k_in],
                        device_id=(tx, ty, other), device_id_type=MESH).wait_recv()

    hbm = lambda t: pltpu.HBM(t.shape, t.dtype)
    outs = pl.pallas_call(
        body,
        name=name,
        in_specs=[HBM] * n + [SEM, SEM, ANY],
        out_specs=[HBM] * n,
        out_shape=[hbm(t) for t in thru],
        input_output_aliases={i: i for i in range(n)},
        compiler_params=pltpu.CompilerParams(has_side_effects=DATAFLOW),
    )(*thru, send_sems, recv_sems, after)
    return list(outs)


D2D_PIECE_ROWS = 64


def _pair_exchange(grads, name):
    n = len(grads)
    halves = [g.shape[1] // 2 for g in grads]

    def body(*refs):
        g_refs, got_refs = refs[0:n], refs[n:2 * n]
        send_sems, recv_sems = refs[2 * n:]
        x, y, c = _place()

        def copy(a, src, dst):
            return pltpu.make_async_remote_copy(
                src_ref=src, dst_ref=dst, send_sem=send_sems.at[a], recv_sem=recv_sems.at[a],
                device_id=(x, y, 1 - c), device_id_type=MESH)

        for a in range(n):
            for jj in range(N_CHIPS):
                for q in range(halves[a] // D2D_PIECE_ROWS):
                    src_rows = pl.ds(pl.multiple_of((1 - c) * halves[a] + q * D2D_PIECE_ROWS, 8), D2D_PIECE_ROWS)
                    dst_rows = pl.ds(q * D2D_PIECE_ROWS, D2D_PIECE_ROWS)
                    copy(a, g_refs[a].at[jj, src_rows, :], got_refs[a].at[jj, dst_rows, :]).start()
        for a in range(n):
            sent = g_refs[a].at[:, pl.ds(pl.multiple_of((1 - c) * halves[a], 8), halves[a]), :]
            copy(a, sent, got_refs[a]).wait()

    return pl.pallas_call(
        body,
        name=name,
        in_specs=[ANY] * n,
        out_specs=[ANY] * n,
        out_shape=[_sds((N_CHIPS, h, g.shape[2]), F32) for g, h in zip(grads, halves)],
        scratch_shapes=[pltpu.SemaphoreType.DMA((n,)), pltpu.SemaphoreType.DMA((n,))],
    )(*grads)


HBM = pl.BlockSpec(memory_space=pltpu.HBM)
SEM = pl.BlockSpec(memory_space=pltpu.SEMAPHORE)
DATAFLOW = pltpu.SideEffectType.DATAFLOW_SIDE_EFFECTING


def _chip_copy(p_refs, land_refs, send_sems, recv_sems, a, r):
    x, y, c = _place()
    tx, ty = _chip_of(x, y, r)
    k = a * 3 + (r - 1)
    return pltpu.make_async_remote_copy(
        src_ref=p_refs[a].at[2 * tx + ty], dst_ref=land_refs[a].at[r - 1],
        send_sem=send_sems.at[k], recv_sem=recv_sems.at[k], device_id=(tx, ty, c), device_id_type=MESH)


def _chip_exchange_start(psums, name):
    n = len(psums)
    lands = [lax.empty((3,) + p.shape[1:], p.dtype) for p in psums]

    def body(*refs):
        p_refs, land_refs = refs[0:n], refs[n:2 * n]
        send_sems, recv_sems, token = refs[2 * n], refs[2 * n + 1], refs[-1]
        for a in range(n):
            for r in (1, 2, 3):
                _chip_copy(p_refs, land_refs, send_sems, recv_sems, a, r).start()
        token[...] = jnp.zeros_like(token)

    hbm = lambda t: pltpu.HBM(t.shape, t.dtype)
    keep = lambda t: pltpu.with_memory_space_constraint(t, pltpu.HBM)
    outs = pl.pallas_call(
        body,
        name=name,
        in_specs=[HBM] * (2 * n),
        out_specs=(SEM, SEM, *[HBM] * (2 * n), pl.BlockSpec(memory_space=pltpu.VMEM)),
        out_shape=(pltpu.SemaphoreType.DMA((3 * n,)), pltpu.SemaphoreType.DMA((3 * n,)),
                   *[hbm(p) for p in psums], *[hbm(l) for l in lands], _sds((8, 128), F32)),
        input_output_aliases={i: 2 + i for i in range(2 * n)},
        compiler_params=pltpu.CompilerParams(has_side_effects=DATAFLOW),
    )(*[keep(p) for p in psums], *[keep(l) for l in lands])
    return outs[0], outs[1], list(outs[2:2 + n]), list(outs[2 + n:2 + 2 * n]), outs[-1]


def _chip_exchange_wait(send_sems, recv_sems, p_thru, land_thru, after, name):
    n = len(p_thru)

    def body(*refs):
        p_refs, land_refs = refs[0:n], refs[n:2 * n]
        send_sems, recv_sems = refs[2 * n], refs[2 * n + 1]
        for a in range(n):
            for r in (1, 2, 3):
                copy = _chip_copy(p_refs, land_refs, send_sems, recv_sems, a, r)
                copy.wait_send()
                copy.wait_recv()

    hbm = lambda t: pltpu.HBM(t.shape, t.dtype)
    outs = pl.pallas_call(
        body,
        name=name,
        in_specs=[HBM] * (2 * n) + [SEM, SEM, ANY],
        out_specs=[HBM] * (2 * n),
        out_shape=[hbm(p) for p in p_thru] + [hbm(l) for l in land_thru],
        input_output_aliases={i: i for i in range(2 * n)},
        compiler_params=pltpu.CompilerParams(has_side_effects=DATAFLOW),
    )(*p_thru, *land_thru, send_sems, recv_sems, after)
    return list(outs[n:2 * n])


def _pair_share(fulls):
    n = len(fulls)
    halves = [f.shape[0] // 2 for f in fulls]

    def body(*refs):
        full_refs = refs[n:2 * n]
        send_sems, recv_sems = refs[2 * n:]
        x, y, c = _place()

        def half_of(a, core):
            return full_refs[a].at[pl.ds(pl.multiple_of(core * halves[a], 8), halves[a]), :]

        def remote(a, src, dst):
            return pltpu.make_async_remote_copy(
                src_ref=src, dst_ref=dst, send_sem=send_sems.at[a], recv_sem=recv_sems.at[a],
                device_id=(x, y, 1 - c), device_id_type=MESH)

        for a in range(n):
            for q in range(halves[a] // D2D_PIECE_ROWS):
                piece = full_refs[a].at[
                    pl.ds(pl.multiple_of(c * halves[a] + q * D2D_PIECE_ROWS, 8), D2D_PIECE_ROWS), :]
                remote(a, piece, piece).start()
        for a in range(n):
            remote(a, half_of(a, c), half_of(a, c)).wait_send()
            remote(a, half_of(a, 1 - c), half_of(a, 1 - c)).wait_recv()

    return pl.pallas_call(
        body,
        name="pair_share",
        in_specs=[ANY] * n,
        out_specs=[ANY] * n,
        out_shape=[_sds(f.shape, F32) for f in fulls],
        input_output_aliases={a: a for a in range(n)},
        scratch_shapes=[pltpu.SemaphoreType.DMA((n,)), pltpu.SemaphoreType.DMA((n,))],
    )(*fulls)


def _allreduce_small(s):
    R, C = s.shape

    def body(s_ref, o_ref, sib, chips, send_sems, recv_sems):
        x, y, c = _place()
        j = 2 * x + y
        def to_sib(src, dst):
            return pltpu.make_async_remote_copy(
                src_ref=src, dst_ref=dst, send_sem=send_sems.at[0], recv_sem=recv_sems.at[0],
                device_id=(x, y, 1 - c), device_id_type=MESH)

        for q in range(R // 8):
            to_sib(s_ref.at[pl.ds(8 * q, 8), :], sib.at[pl.ds(8 * q, 8), :]).start()
        to_sib(s_ref, sib).wait()
        chips[j] = s_ref[...] + sib[...]
        sends = []
        for r in (1, 2, 3):
            tx, ty = _chip_of(x, y, r)
            cp = pltpu.make_async_remote_copy(
                src_ref=chips.at[j], dst_ref=chips.at[j], send_sem=send_sems.at[r], recv_sem=recv_sems.at[r],
                device_id=(tx, ty, c), device_id_type=MESH)
            cp.start()
            sends.append(cp)
        for r in (1, 2, 3):
            tx, ty = _chip_of(x, y, r)
            region = chips.at[2 * tx + ty]
            pltpu.make_async_remote_copy(
                src_ref=region, dst_ref=region, send_sem=send_sems.at[r], recv_sem=recv_sems.at[r],
                device_id=(tx, ty, c), device_id_type=MESH).wait_recv()
        for cp in sends:
            cp.wait_send()
        o_ref[...] = (chips[0] + chips[1]) + (chips[2] + chips[3])

    return pl.pallas_call(
        body,
        name="allreduce_small",
        in_specs=[pl.BlockSpec(memory_space=pltpu.VMEM)],
        out_specs=pl.BlockSpec(memory_space=pltpu.VMEM),
        out_shape=jax.ShapeDtypeStruct((R, C), F32),
        scratch_shapes=[pltpu.VMEM((R, C), F32), pltpu.VMEM((N_CHIPS, R, C), F32),
                        pltpu.SemaphoreType.DMA((4,)), pltpu.SemaphoreType.DMA((4,))],
    )(s)


def _block_diag(w):
    w4 = w.reshape(4, 4, RNN_BLOCK_W, RNN_BLOCK_W)
    eye = jnp.eye(4, dtype=w.dtype)
    return jnp.einsum("jaik,ab->jaibk", w4, eye).reshape(4, RNN_TILE, RNN_TILE)


def _block_diag_part(d):
    d5 = d.reshape(4, 4, RNN_BLOCK_W, 4, RNN_BLOCK_W)
    return jnp.stack([d5[:, a, :, a, :] for a in range(4)], axis=1).reshape(RNN_BLOCKS, RNN_BLOCK_W, RNN_BLOCK_W)


def _local_grads(x, target, g_pre, w_in_g, b_gate, conv_w, conv_b, w_rg_a, b_rg_a, w_rg_x, b_rg_x, lam, sinks,
                 out_weights, fwd_token, g_post, on_out_grads, on_w_in_grad):
    wa_bd = _block_diag(w_rg_a).astype(BF16)
    wx_bd = _block_diag(w_rg_x).astype(BF16)
    b_a = b_rg_a.reshape(1, D_RNN)
    b_x = b_rg_x.reshape(1, D_RNN)

    proj, h = _proj_fwd(x, g_pre, w_in_g)
    y_rnn, z_rnn = _rnn_fwd(proj, conv_w, conv_b, wa_bd, wx_bd, b_a, b_x, lam, fwd_token)
    bias = _attn_bias()
    y_attn, z_attn, lse = _attn_fwd(proj, sinks, bias)
    w_rnn_out, w_attn_out, w_out = out_weights(z_attn)
    dyx, dz_rnn, dz_attn, dml, merged, dout, dbr_rnn, dbr_attn, head_small = _head(
        x, target, z_rnn, z_attn, proj, b_gate, g_post, w_rnn_out, w_attn_out, w_out)
    dw_out = _matmul_tn(merged, dout, "dw_out", 2, False)
    dw_rnn_out = _matmul_tn(z_rnn, dbr_rnn, "dw_rnn_out", 2, False)
    dw_attn_out = _matmul_tn(z_attn, dbr_attn, "dw_attn_out", 2, False)
    shard_rows = lambda d: d.reshape(N_CHIPS, OUT_SHARD, D_MODEL)
    token = on_out_grads([shard_rows(dw_rnn_out), shard_rows(dw_attn_out), shard_rows(dw_out)])
    dq, dk, dv, dag, attn_small = _attn_bwd(proj, y_attn, lse, dz_attn, sinks, bias, token)
    drx, drg, dwa_t, dwx_t, rnn_small = _rnn_bwd(proj, y_rnn, dz_rnn, conv_w, conv_b, wa_bd, wx_bd, b_a, b_x, lam)
    dproj = jnp.concatenate([drx, drg, dq, dk.astype(BF16), dv.astype(BF16), dag, dml], axis=1)
    token = on_w_in_grad(_matmul_tn(h, dproj, "dw_in", N_CHIPS, True))
    grad_x, dh_small = _dh_bwd(dproj, w_in_g, x, dyx, g_pre, token)
    small = jnp.concatenate([rnn_small, head_small, dh_small + attn_small,
                             _block_diag_part(dwa_t).reshape(64, 1024), _block_diag_part(dwx_t).reshape(64, 1024)], axis=0)
    return grad_x, small


ROW_LOSS = 11


def _rows8(parts):
    out = None
    for r, a in parts:
        p = jnp.pad(a, ((r, 8 - r - a.shape[0]), (0, 1024 - a.shape[1])))
        out = p if out is None else out + p
    return out


def _pack_small(p):
    g0 = _rows8([(0, p["b_rg_a"].reshape(1, 1024)), (1, p["b_rg_x"].reshape(1, 1024)), (2, p["lru_lambda"]),
                 (3, p["conv_b"]), (4, p["conv_w"][0])])
    g1 = _rows8([(0, p["post_norm_g"]), (1, p["b_gate"].reshape(2, 1024))])
    g2 = _rows8([(0, p["pre_norm_g"]), (1, p["attn_sinks"])])
    return jnp.concatenate([g0, g1, g2, p["w_rg_a"].reshape(64, 1024), p["w_rg_x"].reshape(64, 1024)], axis=0)


def _unpack_small(s, conv_cols):
    return {
        "b_rg_a": s[0:1].reshape(1, 16, 64), "b_rg_x": s[1:2].reshape(1, 16, 64), "lru_lambda": s[2:3],
        "conv_b": s[3:4], "conv_w": s[4:8, 0:conv_cols].reshape(1, CONV_W, conv_cols),
        "post_norm_g": s[8:9], "b_gate": s[9:11].reshape(1, 2048),
        "pre_norm_g": s[16:17], "attn_sinks": s[17:18, 0:N_Q_HEADS],
        "w_rg_a": s[24:88].reshape(1, 16, 64, 64), "w_rg_x": s[88:152].reshape(1, 16, 64, 64),
    }


WEIGHTS = ["pre_norm_g", "w_in", "b_gate", "conv_w", "conv_b", "w_rg_a", "b_rg_a", "w_rg_x", "b_rg_x", "lru_lambda",
           "attn_sinks", "w_rnn_out", "w_attn_out", "w_out", "post_norm_g"]
BIG = ["w_in", "w_rnn_out", "w_attn_out", "w_out"]


def kernel(x, pre_norm_g, w_in, b_gate, conv_w, conv_b, w_rg_a, b_rg_a, w_rg_x, b_rg_x, lru_lambda, attn_sinks, w_rnn_out, w_attn_out, w_out, post_norm_g, loss_target, m_pre_norm_g, m_w_in, m_b_gate, m_conv_w, m_conv_b, m_w_rg_a, m_b_rg_a, m_w_rg_x, m_b_rg_x, m_lru_lambda, m_attn_sinks, m_w_rnn_out, m_w_attn_out, m_w_out, m_post_norm_g, v_pre_norm_g, v_w_in, v_b_gate, v_conv_w, v_conv_b, v_w_rg_a, v_b_rg_a, v_w_rg_x, v_b_rg_x, v_lru_lambda, v_attn_sinks, v_w_rnn_out, v_w_attn_out, v_w_out, v_post_norm_g):
    w = dict(pre_norm_g=pre_norm_g, w_in=w_in, b_gate=b_gate, conv_w=conv_w, conv_b=conv_b, w_rg_a=w_rg_a,
             b_rg_a=b_rg_a, w_rg_x=w_rg_x, b_rg_x=b_rg_x, lru_lambda=lru_lambda, attn_sinks=attn_sinks,
             w_rnn_out=w_rnn_out, w_attn_out=w_attn_out, w_out=w_out, post_norm_g=post_norm_g)
    m = dict(pre_norm_g=m_pre_norm_g, w_in=m_w_in, b_gate=m_b_gate, conv_w=m_conv_w, conv_b=m_conv_b, w_rg_a=m_w_rg_a,
             b_rg_a=m_b_rg_a, w_rg_x=m_w_rg_x, b_rg_x=m_b_rg_x, lru_lambda=m_lru_lambda, attn_sinks=m_attn_sinks,
             w_rnn_out=m_w_rnn_out, w_attn_out=m_w_attn_out, w_out=m_w_out, post_norm_g=m_post_norm_g)
    v = dict(pre_norm_g=v_pre_norm_g, w_in=v_w_in, b_gate=v_b_gate, conv_w=v_conv_w, conv_b=v_conv_b, w_rg_a=v_w_rg_a,
             b_rg_a=v_b_rg_a, w_rg_x=v_w_rg_x, b_rg_x=v_b_rg_x, lru_lambda=v_lru_lambda, attn_sinks=v_attn_sinks,
             w_rnn_out=v_w_rnn_out, w_attn_out=v_w_attn_out, w_out=v_w_out, post_norm_g=v_post_norm_g)
    chip = 2 * lax.axis_index("x") + lax.axis_index("y")

    chip_idx = chip.astype(jnp.int32).reshape(1)
    chip_core = jnp.stack([chip, lax.axis_index("c")]).astype(jnp.int32)
    cw8 = jnp.pad(conv_w[0], ((0, 8 - CONV_W), (0, 0)))
    placed = _place_shards([w_in[0], w_rnn_out[0], w_attn_out[0], w_out[0]], chip_idx, "place_shards")
    win_g, cw_g = _gather_weights(placed[:1], cw8)
    late_send, late_recv, late_thru, late_token = _gather_late_start(placed[1:], win_g, "gather_late_start")
    cw_g = lax.dynamic_update_slice_in_dim(cw_g, cw8[None], chip, axis=0)
    conv_w_full = jnp.transpose(cw_g[:, 0:CONV_W, :], (1, 0, 2)).reshape(CONV_W, D_RNN)

    core_idx = lax.axis_index("c").astype(jnp.int32).reshape(1)
    started = {}

    def start_reduction(tag, grads):
        got = _pair_exchange(grads, "pair_exchange_" + tag)
        sums = [_pair_sum(g, o, core_idx, "pair_sum_%s_%d" % (tag, a)) for a, (g, o) in enumerate(zip(grads, got))]
        send_sems, recv_sems, p_thru, land_thru, token = _chip_exchange_start(
            [pb for _, pb in sums], "chip_exchange_start_" + tag)
        started[tag] = ([p for p, _ in sums], send_sems, recv_sems, p_thru, land_thru)
        return token

    def end_reduction(tag, after):
        psums, send_sems, recv_sems, p_thru, land_thru = started[tag]
        landed = _chip_exchange_wait(send_sems, recv_sems, p_thru, land_thru, after, "chip_exchange_wait_" + tag)
        return [_chip_sum(p, l, chip_core, "chip_sum_%s_%d" % (tag, a)) for a, (p, l) in enumerate(zip(psums, landed))]

    def out_weights(after):
        gathered = _gather_late_wait(late_send, late_recv, late_thru, after, "gather_late_wait")
        return [g.reshape(D_MODEL, D_MODEL) for g in gathered]

    grad_x, small = _local_grads(
        x[0], loss_target[0], pre_norm_g, win_g, b_gate, conv_w_full, conv_b, w_rg_a[0], b_rg_a[0], w_rg_x[0],
        b_rg_x[0], lru_lambda, attn_sinks[0], out_weights, late_token, post_norm_g,
        on_out_grads=lambda grads: start_reduction("out", grads),
        on_w_in_grad=lambda grad: start_reduction("in", [grad]))

    halves = end_reduction("in", grad_x) + end_reduction("out", grad_x)
    gbig = dict(zip(BIG, _pair_share(halves)))

    small_sum = _allreduce_small(small)
    total_loss = small_sum[ROW_LOSS, 0]
    gsmall = _unpack_small(small_sum, D_RNN)
    conv_shard = D_RNN // N_CHIPS
    gsmall["conv_w"] = lax.dynamic_slice_in_dim(gsmall["conv_w"], chip * conv_shard, conv_shard, axis=2)

    grads, delta, new_m, new_v = {}, {}, {}, {}
    for n in BIG:
        grads[n] = gbig[n][None]
        d, nm, nv = _adamw(w[n][0], gbig[n], m[n][0], v[n][0], "adamw_" + n)
        delta[n], new_m[n], new_v[n] = d[None], nm[None], nv[None]
    pick = lambda t: {k: t[k] for k in gsmall}
    d, nm, nv = _adamw(_pack_small(pick(w)), _pack_small(gsmall), _pack_small(pick(m)), _pack_small(pick(v)),
                       "adamw_small")
    ud, um, uv = _unpack_small(d, conv_shard), _unpack_small(nm, conv_shard), _unpack_small(nv, conv_shard)
    for n in gsmall:
        grads[n] = gsmall[n].reshape(w[n].shape)
        delta[n] = ud[n].reshape(w[n].shape)
        new_m[n] = um[n].reshape(w[n].shape)
        new_v[n] = uv[n].reshape(w[n].shape)

    return (total_loss, grad_x[None], *[grads[n] for n in WEIGHTS], *[delta[n] for n in WEIGHTS],
            *[new_m[n] for n in WEIGHTS], *[new_v[n] for n in WEIGHTS])
```

```python
import functools
import math

import jax
import jax.numpy as jnp
from jax import lax
from jax.experimental import pallas as pl
from jax.experimental.pallas import tpu as pltpu

F32 = jnp.float32
BF16 = jnp.bfloat16

D_MODEL = 1024
D_RNN = 1024
RNN_BLOCKS = 16
RNN_BLOCK_W = 64
CONV_W = 4
LRU_C = 8.0
N_Q_HEADS = 16
N_KV_HEADS = 4
GROUP = 4
HEAD_DIM = 64
D_KV = 256
BLOCK = 128
ALIBI_MAX_BIAS = 8.0
EPS = 1e-6
D_IN = 6656
N_CHIPS = 4
W_IN_SHARD = D_IN // N_CHIPS
OUT_SHARD = D_MODEL // N_CHIPS
ADAM_LR = 0.001
ADAM_B1 = 0.9
ADAM_B2 = 0.999
ADAM_EPS = 1e-08
ADAM_WD = 0.01
ADAM_STEP = 10
NEG_BIG = -1e30
MIB = 1 << 20

COL_RNN_X = 0
COL_RNN_GATE = 4
COL_Q = 8
COL_K = 12
COL_V = 13
COL_ATTN_GATE = 14
COL_MERGE = 18

RNN_TILE = 256
RNN_CHUNK = 256
SMALL_ROWS = 152
MESH = pl.DeviceIdType.MESH


def _sds(shape, dtype):
    return pltpu.HBM(shape, dtype)


def _params(sem=None, vmem_mib=None):
    kw = {}
    if sem is not None:
        kw["dimension_semantics"] = sem
    if vmem_mib is not None:
        kw["vmem_limit_bytes"] = vmem_mib * MIB
    return pltpu.CompilerParams(**kw)


def _hbm(*arrays):
    return [pltpu.with_memory_space_constraint(a, pltpu.HBM) for a in arrays]


def _dot(a, b):
    return jnp.dot(a, b, preferred_element_type=F32)


def _dot_nt(a, b):
    return lax.dot_general(a, b, (((1,), (1,)), ((), ())), preferred_element_type=F32)


def _dot_tn(a, b):
    return lax.dot_general(a, b, (((0,), (0,)), ((), ())), preferred_element_type=F32)


def _sigmoid(x):
    return 0.5 * jnp.tanh(0.5 * x) + 0.5


def _sigmoid_small(x):
    return 1.0 / (1.0 + jnp.exp(-x))


def _softplus(x):
    return jnp.maximum(x, 0.0) + jnp.log(1.0 + jnp.exp(-jnp.abs(x)))


def _one_minus_square(a, log_a):
    return -jnp.tanh(log_a) * (a * a + 1.0)


def _proj_fwd(x, g_pre, w_in_g):
    T = x.shape[0]
    tm = min(1024, T)

    def body(x_ref, g_ref, w_ref, proj_ref, h_ref):
        @pl.when(pl.program_id(1) == 0)
        def _():
            xv = x_ref[...]
            rstd = lax.rsqrt(jnp.mean(xv * xv, axis=-1, keepdims=True) + EPS)
            h_ref[...] = ((xv * rstd) * g_ref[...]).astype(BF16)

        proj_ref[...] = _dot(h_ref[...], w_ref[...])

    return pl.pallas_call(
        body,
        name="proj_fwd",
        grid=(T // tm, N_CHIPS),
        in_specs=[
            pl.BlockSpec((tm, D_MODEL), lambda i, j: (i, 0)),
            pl.BlockSpec((1, D_MODEL), lambda i, j: (0, 0)),
            pl.BlockSpec((None, D_MODEL, W_IN_SHARD), lambda i, j: (j, 0, 0)),
        ],
        out_specs=[
            pl.BlockSpec((tm, W_IN_SHARD), lambda i, j: (i, j)),
            pl.BlockSpec((tm, D_MODEL), lambda i, j: (i, 0)),
        ],
        out_shape=[_sds((T, D_IN), F32), _sds((T, D_MODEL), BF16)],
        compiler_params=_params(("parallel", "arbitrary"), 48),
    )(*_hbm(x, g_pre, w_in_g))


def _shift_down(x, tail, s, row):
    n = x.shape[0]
    xs = pltpu.roll(x, s, 0)
    tail_t = jnp.tile(pltpu.roll(tail, s, 0), (n // 8, 1))
    return jnp.where(row < s, tail_t, xs)


def _shift_up(x, head, s, row):
    n = x.shape[0]
    xs = pltpu.roll(x, n - s, 0)
    head_t = jnp.tile(pltpu.roll(head, 8 - s, 0), (n // 8, 1))
    return jnp.where(row >= n - s, head_t, xs)


def _conv_taps(x, tail, row):
    return [_shift_down(x, tail, 3, row), _shift_down(x, tail, 2, row), _shift_down(x, tail, 1, row), x]


def _rglru_gates(c, wa, wx, ba, bx, lam):
    cb = c.astype(BF16)
    r = _sigmoid_small(_dot(cb, wa) + ba)
    i = _sigmoid(_dot(cb, wx) + bx)
    log_a = (-LRU_C) * r * _softplus(-lam)
    a = jnp.exp(log_a)
    mult = jnp.sqrt(_one_minus_square(a, log_a))
    return cb, r, i, a, mult


SUBLANES = 8


def _scan_down(a, u, row):
    n = a.shape[0]
    s = 1
    while s < SUBLANES:
        a_sh = jnp.where(row >= s, pltpu.roll(a, s, 0), 1.0)
        u_sh = jnp.where(row >= s, pltpu.roll(u, s, 0), 0.0)
        u = a * u_sh + u
        a = a * a_sh
        s *= 2
    while s < n:
        u = jnp.concatenate([u[:s], a[s:] * u[:n - s] + u[s:]], axis=0)
        a = jnp.concatenate([a[:s], a[s:] * a[:n - s]], axis=0)
        s *= 2
    return a, u


def _scan_up(b, u, row):
    n = b.shape[0]
    s = 1
    while s < SUBLANES:
        b_sh = jnp.where(row < n - s, pltpu.roll(b, n - s, 0), 1.0)
        u_sh = jnp.where(row < n - s, pltpu.roll(u, n - s, 0), 0.0)
        u = b * u_sh + u
        b = b * b_sh
        s *= 2
    while s < n:
        u = jnp.concatenate([b[:n - s] * u[s:] + u[:n - s], u[n - s:]], axis=0)
        b = jnp.concatenate([b[:n - s] * b[s:], b[n - s:]], axis=0)
        s *= 2
    return b, u


def _rnn_fwd(proj, conv_w, conv_b, wa_bd, wx_bd, b_a, b_x, lam, token):
    T = proj.shape[0]
    tc, ct = RNN_CHUNK, RNN_TILE
    nt = T // tc

    def body(x_ref, rg_ref, cw_ref, cb_ref, wa_ref, wx_ref, ba_ref, bx_ref, lam_ref, token_ref, h_ref, z_ref, xtail,
             hcarry):
        @pl.when(pl.program_id(1) == 0)
        def _():
            xtail[...] = jnp.zeros_like(xtail)
            hcarry[...] = jnp.zeros_like(hcarry)

        row = lax.broadcasted_iota(jnp.int32, (tc, ct), 0)
        x = x_ref[...]
        taps = _conv_taps(x, xtail[...], row)
        c = cb_ref[...] + cw_ref[pl.ds(0, 1), :] * taps[0]
        for k in range(1, CONV_W):
            c = c + cw_ref[pl.ds(k, 1), :] * taps[k]
        xtail[...] = x_ref[pl.ds(tc - 8, 8), :]
        _, _, i, a, mult = _rglru_gates(c, wa_ref[...], wx_ref[...], ba_ref[...], bx_ref[...], lam_ref[...])
        u = mult * (i * c)
        a_cum, h0 = _scan_down(a, u, row)
        h = h0 + a_cum * hcarry[...]
        h_ref[...] = h
        hcarry[...] = h_ref[pl.ds(tc - 1, 1), :]
        rg = rg_ref[...]
        z_ref[...] = (h * (rg * _sigmoid(rg))).astype(BF16)

    col = lambda off: (lambda j, t: (t, off + j))
    vec = pl.BlockSpec((1, ct), lambda j, t: (0, j))
    mat = pl.BlockSpec((None, ct, ct), lambda j, t: (j, 0, 0))
    return pl.pallas_call(
        body,
        name="rnn_fwd",
        grid=(D_RNN // ct, nt),
        in_specs=[
            pl.BlockSpec((tc, ct), col(COL_RNN_X)),
            pl.BlockSpec((tc, ct), col(COL_RNN_GATE)),
            pl.BlockSpec((CONV_W, ct), lambda j, t: (0, j)),
            vec, mat, mat, vec, vec, vec,
            pl.BlockSpec((8, 128), lambda j, t: (0, 0)),
        ],
        out_specs=[pl.BlockSpec((tc, ct), lambda j, t: (t, j)), pl.BlockSpec((tc, ct), lambda j, t: (t, j))],
        out_shape=[_sds((T, D_RNN), F32), _sds((T, D_RNN), BF16)],
        scratch_shapes=[pltpu.VMEM((8, ct), F32), pltpu.VMEM((1, ct), F32)],
        compiler_params=_params(("parallel", "arbitrary"), 32),
    )(*_hbm(proj, proj, conv_w, conv_b, wa_bd, wx_bd, b_a, b_x, lam, token))


def _rnn_bwd(proj, y_rnn, dz_rnn, conv_w, conv_b, wa_bd, wx_bd, b_a, b_x, lam):
    T = proj.shape[0]
    tc, ct = RNN_CHUNK, RNN_TILE
    nt = T // tc
    hb = tc // 8

    def body(x_ref, xh_ref, rg_ref, h_ref, hh_ref, dz_ref, cw_ref, cb_ref, wa_ref, wx_ref, ba_ref, bx_ref, lam_ref,
             dx_ref, drg_ref, dwa_ref, dwx_ref, sm_ref, lam_carry, a_carry, dc_head):
        t = pl.program_id(1)
        first_chunk = t == nt - 1

        @pl.when(t == 0)
        def _():
            lam_carry[...] = jnp.zeros_like(lam_carry)
            a_carry[...] = jnp.zeros_like(a_carry)
            dc_head[...] = jnp.zeros_like(dc_head)
            dwa_ref[...] = jnp.zeros_like(dwa_ref)
            dwx_ref[...] = jnp.zeros_like(dwx_ref)
            sm_ref[...] = jnp.zeros_like(sm_ref)

        row = lax.broadcasted_iota(jnp.int32, (tc, ct), 0)
        keep = jnp.where(first_chunk, 0.0, 1.0)
        x = x_ref[...]
        xtail = xh_ref[...] * keep
        taps = _conv_taps(x, xtail, row)
        c = cb_ref[...] + cw_ref[pl.ds(0, 1), :] * taps[0]
        for k in range(1, CONV_W):
            c = c + cw_ref[pl.ds(k, 1), :] * taps[k]
        lam = lam_ref[...]
        cb, r, i, a, mult = _rglru_gates(c, wa_ref[...], wx_ref[...], ba_ref[...], bx_ref[...], lam)
        h = h_ref[...]
        h_prev = _shift_down(h, hh_ref[...] * keep, 1, row)
        rg = rg_ref[...]
        dz = dz_ref[...]
        sg = _sigmoid(rg)
        drg_ref[...] = (dz * h * (sg * (1.0 + rg * (1.0 - sg)))).astype(BF16)
        dy = dz * (rg * sg)
        b = jnp.where(row >= tc - 1, a_carry[pl.ds(0, 1), :], pltpu.roll(a, tc - 1, 0))
        b_cum, l0 = _scan_up(b, dy, row)
        lt = l0 + b_cum * lam_carry[pl.ds(0, 1), :]
        lam_carry[...] = lt[0:8, :]
        a_carry[...] = a[0:8, :]
        ic = i * c
        dmult = lt * ic
        di = lt * mult * c
        dc = lt * mult * i
        dlog_a = a * (lt * h_prev - dmult * a / mult)
        sp = _softplus(-lam)
        dpre_r = dlog_a * ((-LRU_C) * sp) * (r * (1.0 - r))
        dpre_i = di * (i * (1.0 - i))
        dlam_row = jnp.sum(dlog_a * r, axis=0, keepdims=True) * (LRU_C * _sigmoid(-lam))
        dpr_b = dpre_r.astype(BF16)
        dpi_b = dpre_i.astype(BF16)
        dwa_ref[...] += _dot_tn(cb, dpr_b)
        dwx_ref[...] += _dot_tn(cb, dpi_b)
        dc = dc + _dot_nt(dpr_b, wa_ref[...]) + _dot_nt(dpi_b, wx_ref[...])
        head = dc_head[...]
        dx = cw_ref[pl.ds(3, 1), :] * dc
        for m in range(1, CONV_W):
            dx = dx + cw_ref[pl.ds(3 - m, 1), :] * _shift_up(dc, head, m, row)
        dx_ref[...] = dx.astype(BF16)
        dc_head[...] = dc[0:8, :]
        sm_ref[pl.ds(0, 1), :] += jnp.sum(dpre_r, axis=0, keepdims=True)
        sm_ref[pl.ds(1, 1), :] += jnp.sum(dpre_i, axis=0, keepdims=True)
        sm_ref[pl.ds(2, 1), :] += dlam_row
        sm_ref[pl.ds(3, 1), :] += jnp.sum(dc, axis=0, keepdims=True)
        for k in range(CONV_W):
            sm_ref[pl.ds(4 + k, 1), :] += jnp.sum(dc * taps[k], axis=0, keepdims=True)

    rev = lambda off: (lambda j, t: (nt - 1 - t, off + j))
    halo = lambda off: (lambda j, t: (jnp.maximum((nt - 1 - t) * hb - 1, 0), off + j))
    vec = pl.BlockSpec((1, ct), lambda j, t: (0, j))
    mat = pl.BlockSpec((None, ct, ct), lambda j, t: (j, 0, 0))
    return pl.pallas_call(
        body,
        name="rnn_bwd",
        grid=(D_RNN // ct, nt),
        in_specs=[
            pl.BlockSpec((tc, ct), rev(COL_RNN_X)),
            pl.BlockSpec((8, ct), halo(COL_RNN_X)),
            pl.BlockSpec((tc, ct), rev(COL_RNN_GATE)),
            pl.BlockSpec((tc, ct), rev(0)),
            pl.BlockSpec((8, ct), halo(0)),
            pl.BlockSpec((tc, ct), rev(0)),
            pl.BlockSpec((CONV_W, ct), lambda j, t: (0, j)),
            vec, mat, mat, vec, vec, vec,
        ],
        out_specs=[
            pl.BlockSpec((tc, ct), rev(0)),
            pl.BlockSpec((tc, ct), rev(0)),
            mat, mat,
            pl.BlockSpec((8, ct), lambda j, t: (0, j)),
        ],
        out_shape=[_sds((T, D_RNN), BF16), _sds((T, D_RNN), BF16), _sds((D_RNN // ct, ct, ct), F32),
                   _sds((D_RNN // ct, ct, ct), F32), _sds((8, D_RNN), F32)],
        scratch_shapes=[pltpu.VMEM((8, ct), F32), pltpu.VMEM((8, ct), F32), pltpu.VMEM((8, ct), F32)],
        compiler_params=_params(("parallel", "arbitrary"), 32),
    )(*_hbm(proj, proj, proj, y_rnn, y_rnn, dz_rnn, conv_w, conv_b, wa_bd, wx_bd, b_a, b_x, lam))


def _attn_bias():
    qi = jnp.arange(BLOCK)[:, None]
    kj = jnp.arange(BLOCK)[None, :]
    dist_cur = (qi - kj).astype(F32)
    slopes = 2.0 ** (-ALIBI_MAX_BIAS * jnp.arange(1, N_Q_HEADS + 1, dtype=F32) / N_Q_HEADS)
    slopes = slopes[:, None, None]
    prev = jnp.where(kj > qi, -slopes * (dist_cur + float(BLOCK)), NEG_BIG)
    cur = jnp.where(kj <= qi, -slopes * dist_cur, NEG_BIG)
    later = jnp.concatenate([prev, cur], axis=-1)
    first = jnp.concatenate([jnp.full_like(prev, NEG_BIG), cur], axis=-1)
    return jnp.stack([first, later])


def _attn_exps(s_prev, s_cur, sink, bias):
    s_prev = s_prev + bias[:, 0:BLOCK]
    s_cur = s_cur + bias[:, BLOCK:2 * BLOCK]
    m = jnp.maximum(jnp.max(jnp.maximum(s_prev, s_cur), axis=-1, keepdims=True), sink)
    p_prev = jnp.exp(s_prev - m)
    p_cur = jnp.exp(s_cur - m)
    total = jnp.sum(p_prev + p_cur, axis=-1, keepdims=True) + jnp.exp(sink - m)
    return p_prev, p_cur, 1.0 / total, m + jnp.log(total)


def _attn_probs(s_prev, s_cur, sink, bias, lse):
    p_prev = jnp.exp((s_prev + bias[:, 0:BLOCK]) - lse)
    p_cur = jnp.exp((s_cur + bias[:, BLOCK:2 * BLOCK]) - lse)
    return p_prev, p_cur, jnp.exp(sink - lse)


def _stack_heads(ref_or_val, hk, dtype):
    parts = [ref_or_val[:, (GROUP * hk + g) * HEAD_DIM:(GROUP * hk + g + 1) * HEAD_DIM] for g in range(GROUP)]
    return jnp.concatenate(parts, axis=0).astype(dtype)


ATTN_SCALE = HEAD_DIM ** -0.5


def _bias_spec():
    return pl.BlockSpec((None, N_Q_HEADS, BLOCK, 2 * BLOCK), lambda i: (jnp.minimum(i, 1), 0, 0, 0))


def _attn_fwd(proj, sinks, bias):
    T = proj.shape[0]
    nb = T // BLOCK

    def body(sink_ref, bias_ref, q_ref, kp_ref, kc_ref, vp_ref, vc_ref, ag0_ref, ag1_ref, y_ref, z_ref, lse_ref):
        kvs = [slice(hk * HEAD_DIM, (hk + 1) * HEAD_DIM) for hk in range(N_KV_HEADS)]
        qgs = [(_stack_heads(q_ref, hk, F32) * ATTN_SCALE).astype(BF16) for hk in range(N_KV_HEADS)]
        s_prev = [_dot_nt(qgs[hk], kp_ref[:, kvs[hk]].astype(BF16)) for hk in range(N_KV_HEADS)]
        s_cur = [_dot_nt(qgs[hk], kc_ref[:, kvs[hk]].astype(BF16)) for hk in range(N_KV_HEADS)]
        for hk in range(N_KV_HEADS):
            pp, pc, invs = [], [], []
            for g in range(GROUP):
                h = GROUP * hk + g
                rows = slice(g * BLOCK, (g + 1) * BLOCK)
                p_prev, p_cur, inv, lse = _attn_exps(s_prev[hk][rows], s_cur[hk][rows], sink_ref[h], bias_ref[h])
                pp.append(p_prev.astype(BF16))
                pc.append(p_cur.astype(BF16))
                invs.append(inv)
                lse_ref[:, h:h + 1] = lse
            og = _dot(jnp.concatenate(pp, axis=0), vp_ref[:, kvs[hk]].astype(BF16)) + _dot(
                jnp.concatenate(pc, axis=0), vc_ref[:, kvs[hk]].astype(BF16))
            for g in range(GROUP):
                h = GROUP * hk + g
                y_ref[:, h * HEAD_DIM:(h + 1) * HEAD_DIM] = og[g * BLOCK:(g + 1) * BLOCK] * invs[g]
        ag = jnp.concatenate([ag0_ref[...], ag1_ref[...]], axis=1)
        z_ref[...] = (y_ref[...] * (ag * _sigmoid(ag))).astype(BF16)

    prev = lambda c: (lambda i: (jnp.maximum(i - 1, 0), c))
    cur = lambda c: (lambda i: (i, c))
    return pl.pallas_call(
        body,
        name="attn_fwd",
        grid=(nb,),
        in_specs=[
            pl.BlockSpec(memory_space=pltpu.SMEM),
            _bias_spec(),
            pl.BlockSpec((BLOCK, 1024), lambda i: (i, COL_Q // 4)),
            pl.BlockSpec((BLOCK, D_KV), prev(COL_K)),
            pl.BlockSpec((BLOCK, D_KV), cur(COL_K)),
            pl.BlockSpec((BLOCK, D_KV), prev(COL_V)),
            pl.BlockSpec((BLOCK, D_KV), cur(COL_V)),
            pl.BlockSpec((BLOCK, 512), lambda i: (i, COL_ATTN_GATE // 2)),
            pl.BlockSpec((BLOCK, 512), lambda i: (i, COL_ATTN_GATE // 2 + 1)),
        ],
        out_specs=[pl.BlockSpec((BLOCK, 1024), lambda i: (i, 0)), pl.BlockSpec((BLOCK, 1024), lambda i: (i, 0)),
                   pl.BlockSpec((BLOCK, N_Q_HEADS), lambda i: (i, 0))],
        out_shape=[_sds((T, 1024), F32), _sds((T, 1024), BF16), _sds((T, N_Q_HEADS), F32)],
        compiler_params=_params(("arbitrary",), 32),
    )(sinks, *_hbm(bias, proj, proj, proj, proj, proj, proj, proj))


def _attn_bwd(proj, y_attn, lse, dz_attn, sinks, bias, token):
    T = proj.shape[0]
    nb = T // BLOCK

    def body(sink_ref, bias_ref, q_ref, kp_ref, kc_ref, vp_ref, vc_ref, ag0_ref, ag1_ref, y_ref, lse_ref, dz_ref,
             token_ref, dq_ref, dk_ref, dv_ref, dag_ref, ds_ref, dy_s):
        i = pl.program_id(0)

        @pl.when(i == 0)
        def _():
            ds_ref[...] = jnp.zeros_like(ds_ref)

        lane = lax.broadcasted_iota(jnp.int32, (8, 128), 1)
        sub = lax.broadcasted_iota(jnp.int32, (8, 128), 0)
        ag = jnp.concatenate([ag0_ref[...], ag1_ref[...]], axis=1)
        dz = dz_ref[...]
        sg = _sigmoid(ag)
        dag_ref[...] = (dz * y_ref[...] * (sg * (1.0 + ag * (1.0 - sg)))).astype(BF16)
        dy_s[...] = dz * (ag * sg)
        r_cur = pl.multiple_of(i * BLOCK, BLOCK)
        r_prev = pl.multiple_of(jnp.maximum(i - 1, 0) * BLOCK, BLOCK)
        dk_cur, dv_cur, dk_prev, dv_prev = [], [], [], []
        ds_acc = jnp.zeros((8, 128), F32)
        for hk in range(N_KV_HEADS):
            ks = slice(hk * HEAD_DIM, (hk + 1) * HEAD_DIM)
            qg = (_stack_heads(q_ref, hk, F32) * ATTN_SCALE).astype(BF16)
            dog = _stack_heads(dy_s, hk, F32)
            og = _stack_heads(y_ref, hk, F32)
            dog_b = dog.astype(BF16)
            kp = kp_ref[:, ks].astype(BF16)
            kc = kc_ref[:, ks].astype(BF16)
            vp = vp_ref[:, ks].astype(BF16)
            vc = vc_ref[:, ks].astype(BF16)
            s_prev = _dot_nt(qg, kp)
            s_cur = _dot_nt(qg, kc)
            dp_prev = _dot_nt(dog_b, vp)
            dp_cur = _dot_nt(dog_b, vc)
            dvec = jnp.sum(dog * og, axis=-1, keepdims=True)
            pp, pc, dsp, dsc = [], [], [], []
            for g in range(GROUP):
                h = GROUP * hk + g
                rows = slice(g * BLOCK, (g + 1) * BLOCK)
                p_prev, p_cur, p_sink = _attn_probs(
                    s_prev[rows], s_cur[rows], sink_ref[h], bias_ref[h], lse_ref[:, h:h + 1])
                d_h = dvec[rows]
                pp.append(p_prev.astype(BF16))
                pc.append(p_cur.astype(BF16))
                dsp.append((p_prev * (dp_prev[rows] - d_h)).astype(BF16))
                dsc.append((p_cur * (dp_cur[rows] - d_h)).astype(BF16))
                dsink = -jnp.sum(p_sink * d_h, axis=0, keepdims=True)
                ds_acc = ds_acc + jnp.where(jnp.logical_and(lane == h, sub == 1), dsink, 0.0)
            pp = jnp.concatenate(pp, axis=0)
            pc = jnp.concatenate(pc, axis=0)
            dsp = jnp.concatenate(dsp, axis=0)
            dsc = jnp.concatenate(dsc, axis=0)
            dqg = (_dot(dsp, kp) + _dot(dsc, kc)) * ATTN_SCALE
            for g in range(GROUP):
                h = GROUP * hk + g
                dq_ref[:, h * HEAD_DIM:(h + 1) * HEAD_DIM] = dqg[g * BLOCK:(g + 1) * BLOCK].astype(BF16)
            dk_ref[pl.ds(r_cur, BLOCK), ks] = _dot_tn(dsc, qg)
            dv_ref[pl.ds(r_cur, BLOCK), ks] = _dot_tn(pc, dog_b)
            dk_prev.append(_dot_tn(dsp, qg))
            dv_prev.append(_dot_tn(pp, dog_b))
        ds_ref[:, 0:128] += ds_acc

        @pl.when(i > 0)
        def _():
            for hk in range(N_KV_HEADS):
                ks = slice(hk * HEAD_DIM, (hk + 1) * HEAD_DIM)
                dk_ref[pl.ds(r_prev, BLOCK), ks] += dk_prev[hk]
                dv_ref[pl.ds(r_prev, BLOCK), ks] += dv_prev[hk]

    prev = lambda c: (lambda i: (jnp.maximum(i - 1, 0), c))
    cur = lambda c: (lambda i: (i, c))
    blk = pl.BlockSpec((BLOCK, 1024), lambda i: (i, 0))
    whole = pl.BlockSpec((T, D_KV), lambda i: (0, 0))
    return pl.pallas_call(
        body,
        name="attn_bwd",
        grid=(nb,),
        in_specs=[
            pl.BlockSpec(memory_space=pltpu.SMEM),
            _bias_spec(),
            pl.BlockSpec((BLOCK, 1024), lambda i: (i, COL_Q // 4)),
            pl.BlockSpec((BLOCK, D_KV), prev(COL_K)),
            pl.BlockSpec((BLOCK, D_KV), cur(COL_K)),
            pl.BlockSpec((BLOCK, D_KV), prev(COL_V)),
            pl.BlockSpec((BLOCK, D_KV), cur(COL_V)),
            pl.BlockSpec((BLOCK, 512), lambda i: (i, COL_ATTN_GATE // 2)),
            pl.BlockSpec((BLOCK, 512), lambda i: (i, COL_ATTN_GATE // 2 + 1)),
            blk,
            pl.BlockSpec((BLOCK, N_Q_HEADS), lambda i: (i, 0)),
            blk,
            pl.BlockSpec((8, 128), lambda i: (0, 0)),
        ],
        out_specs=[blk, whole, whole, blk, pl.BlockSpec((8, 1024), lambda i: (0, 0))],
        out_shape=[_sds((T, 1024), BF16), _sds((T, D_KV), F32), _sds((T, D_KV), F32), _sds((T, 1024), BF16),
                   _sds((8, 1024), F32)],
        scratch_shapes=[pltpu.VMEM((BLOCK, 1024), F32)],
        compiler_params=_params(("arbitrary",), 48),
    )(sinks, *_hbm(bias, proj, proj, proj, proj, proj, proj, proj, y_attn, lse, dz_attn, token))


def _head(x, target, z_rnn, z_attn, proj, b_gate, g_post, w_rnn_out, w_attn_out, w_out):
    T = x.shape[0]
    tm = 256

    def body(x_ref, t_ref, zr_ref, za_ref, ml0_ref, ml1_ref, ml2_ref, ml3_ref, bg_ref, gp_ref, wr_ref, wa_ref, wo_ref,
             dyx_ref, dzr_ref, dza_ref, dml_ref, mb_ref, dout_ref, dbr_ref, dba_ref, sm_ref):
        @pl.when(pl.program_id(0) == 0)
        def _():
            sm_ref[...] = jnp.zeros_like(sm_ref)

        wr, wa, wo = wr_ref[...], wa_ref[...], wo_ref[...]
        br_rnn = _dot(zr_ref[...], wr)
        br_attn = _dot(za_ref[...], wa)
        ml_rnn = jnp.concatenate([ml0_ref[...], ml1_ref[...]], axis=1)
        ml_attn = jnp.concatenate([ml2_ref[...], ml3_ref[...]], axis=1)
        g_rnn = _sigmoid(ml_rnn + bg_ref[:, 0:D_MODEL])
        g_attn = _sigmoid(ml_attn + bg_ref[:, D_MODEL:2 * D_MODEL])
        mb = (g_rnn * br_rnn + g_attn * br_attn).astype(BF16)
        mb_ref[...] = mb
        out = _dot(mb, wo)
        rstd = lax.rsqrt(jnp.mean(out * out, axis=-1, keepdims=True) + EPS)
        n = out * rstd
        gp = gp_ref[...]
        err = (x_ref[...] + n * gp) - t_ref[...]
        sm_ref[pl.ds(3, 1), :] += 0.5 * jnp.sum(jnp.mean(err * err, axis=-1, keepdims=True), axis=0, keepdims=True)
        dy = err * (1.0 / D_MODEL)
        dyx_ref[...] = dy
        sm_ref[pl.ds(0, 1), :] += jnp.sum(dy * n, axis=0, keepdims=True)
        dn = dy * gp
        dout = (rstd * (dn - n * jnp.mean(dn * n, axis=-1, keepdims=True))).astype(BF16)
        dout_ref[...] = dout
        dmerged = _dot_nt(dout, wo)
        dml_r = (dmerged * br_rnn) * (g_rnn * (1.0 - g_rnn))
        dml_a = (dmerged * br_attn) * (g_attn * (1.0 - g_attn))
        dml_ref[:, 0:D_MODEL] = dml_r.astype(BF16)
        dml_ref[:, D_MODEL:2 * D_MODEL] = dml_a.astype(BF16)
        sm_ref[pl.ds(1, 1), :] += jnp.sum(dml_r, axis=0, keepdims=True)
        sm_ref[pl.ds(2, 1), :] += jnp.sum(dml_a, axis=0, keepdims=True)
        dbr = (dmerged * g_rnn).astype(BF16)
        dba = (dmerged * g_attn).astype(BF16)
        dbr_ref[...] = dbr
        dba_ref[...] = dba
        dzr_ref[...] = _dot_nt(dbr, wr)
        dza_ref[...] = _dot_nt(dba, wa)

    tile = pl.BlockSpec((tm, D_MODEL), lambda i: (i, 0))
    wspec = pl.BlockSpec((D_MODEL, D_MODEL), lambda i: (0, 0))
    ml = lambda q: pl.BlockSpec((tm, 512), lambda i: (i, COL_MERGE // 2 + q))
    return pl.pallas_call(
        body,
        name="head",
        grid=(T // tm,),
        in_specs=[
            tile, tile, tile, tile,
            ml(0), ml(1), ml(2), ml(3),
            pl.BlockSpec((1, 2 * D_MODEL), lambda i: (0, 0)),
            pl.BlockSpec((1, D_MODEL), lambda i: (0, 0)),
            wspec, wspec, wspec,
        ],
        out_specs=[
            tile, tile, tile,
            pl.BlockSpec((tm, 2 * D_MODEL), lambda i: (i, 0)),
            tile, tile, tile, tile,
            pl.BlockSpec((8, D_MODEL), lambda i: (0, 0)),
        ],
        out_shape=[
            _sds((T, D_MODEL), F32), _sds((T, D_MODEL), F32), _sds((T, D_MODEL), F32),
            _sds((T, 2 * D_MODEL), BF16),
            _sds((T, D_MODEL), BF16), _sds((T, D_MODEL), BF16), _sds((T, D_MODEL), BF16), _sds((T, D_MODEL), BF16),
            _sds((8, D_MODEL), F32),
        ],
        compiler_params=_params(("arbitrary",), 56),
    )(*_hbm(x, target, z_rnn, z_attn, proj, proj, proj, proj, b_gate, g_post, w_rnn_out, w_attn_out, w_out))


def _matmul_tn(a, b, name):
    T, M = a.shape
    N = b.shape[1]
    tk = min(512, T)

    def body(a_ref, b_ref, o_ref):
        @pl.when(pl.program_id(0) == 0)
        def _():
            o_ref[...] = jnp.zeros_like(o_ref)

        o_ref[...] += _dot_tn(a_ref[...], b_ref[...])

    return pl.pallas_call(
        body,
        name=name,
        grid=(T // tk,),
        in_specs=[pl.BlockSpec((tk, M), lambda t: (t, 0)), pl.BlockSpec((tk, N), lambda t: (t, 0))],
        out_specs=pl.BlockSpec((M, N), lambda t: (0, 0)),
        out_shape=_sds((M, N), F32),
        compiler_params=_params(("arbitrary",), 48),
    )(*_hbm(a, b))


DPROJ_WIDTHS = (D_RNN, D_RNN, 1024, D_KV, D_KV, 1024, 2 * D_MODEL)


def _dproj_segments():
    segs, start = [[] for _ in range(N_CHIPS)], 0
    for p, width in enumerate(DPROJ_WIDTHS):
        for c in range(N_CHIPS):
            lo, hi = max(start, c * W_IN_SHARD), min(start + width, (c + 1) * W_IN_SHARD)
            if lo < hi:
                segs[c].append((p, lo - start, hi - start, lo - c * W_IN_SHARD, hi - c * W_IN_SHARD))
        start += width
    return segs


def _dh_bwd(pieces, w_in_g, x, dyx, g_pre, token):
    T = x.shape[0]
    tm = min(512, T)
    n = len(pieces)
    segs = _dproj_segments()

    def body(*refs):
        p_refs, w_hbm, x_ref, dyx_ref, g_ref = refs[0:n], refs[n], refs[n + 1], refs[n + 2], refs[n + 3]
        gx_ref, dg_ref, w_ref = refs[n + 5], refs[n + 6], refs[n + 7]

        @pl.when(pl.program_id(0) == 0)
        def _():
            pltpu.sync_copy(w_hbm, w_ref)
            dg_ref[...] = jnp.zeros_like(dg_ref)

        dh = None
        for c in range(N_CHIPS):
            for p, a0, a1, u0, u1 in segs[c]:
                part = _dot_nt(p_refs[p][:, a0:a1].astype(BF16), w_ref[c, :, u0:u1])
                dh = part if dh is None else dh + part
        xv = x_ref[...]
        rstd = lax.rsqrt(jnp.mean(xv * xv, axis=-1, keepdims=True) + EPS)
        nx = xv * rstd
        dhg = dh * g_ref[...]
        gx_ref[...] = dyx_ref[...] + rstd * (dhg - nx * jnp.mean(dhg * nx, axis=-1, keepdims=True))
        dg_ref[pl.ds(0, 1), :] += jnp.sum(dh * nx, axis=0, keepdims=True)

    tile = pl.BlockSpec((tm, D_MODEL), lambda i: (i, 0))
    return pl.pallas_call(
        body,
        name="dh_bwd",
        grid=(T // tm,),
        in_specs=[pl.BlockSpec((tm, w), lambda i: (i, 0)) for w in DPROJ_WIDTHS] + [
            ANY, tile, tile,
            pl.BlockSpec((1, D_MODEL), lambda i: (0, 0)),
            pl.BlockSpec((8, 128), lambda i: (0, 0)),
        ],
        out_specs=[tile, pl.BlockSpec((8, D_MODEL), lambda i: (0, 0))],
        out_shape=[_sds((T, D_MODEL), F32), _sds((8, D_MODEL), F32)],
        scratch_shapes=[pltpu.VMEM(w_in_g.shape, BF16)],
        compiler_params=_params(("arbitrary",), 56),
    )(*_hbm(*pieces, w_in_g, x, dyx, g_pre, token))


def _dw_in(h, pieces):
    T = h.shape[0]
    tk = min(512, T)
    n = len(pieces)
    segs = _dproj_segments()

    def body(*refs):
        h_ref, p_refs, o_ref = refs[0], refs[1:n + 1], refs[n + 1]

        @pl.when(pl.program_id(1) == 0)
        def _():
            o_ref[...] = jnp.zeros_like(o_ref)

        for c in range(N_CHIPS):
            @pl.when(pl.program_id(0) == c)
            def _():
                cols = jnp.concatenate([p_refs[p][:, a0:a1].astype(BF16) for p, a0, a1, _, _ in segs[c]], axis=1)
                o_ref[...] += _dot_tn(h_ref[...], cols)

    def piece_spec(p):
        chips = [c for c in range(N_CHIPS) if any(s[0] == p for s in segs[c])]

        def index(c, t):
            used = functools.reduce(jnp.logical_or, [c == k for k in chips])
            return (jnp.where(used, t, 0), 0)

        return pl.BlockSpec((tk, DPROJ_WIDTHS[p]), index)

    return pl.pallas_call(
        body,
        name="dw_in",
        grid=(N_CHIPS, T // tk),
        in_specs=[pl.BlockSpec((tk, D_MODEL), lambda c, t: (t, 0))] + [piece_spec(p) for p in range(n)],
        out_specs=pl.BlockSpec((None, D_MODEL, W_IN_SHARD), lambda c, t: (c, 0, 0)),
        out_shape=_sds((N_CHIPS, D_MODEL, W_IN_SHARD), F32),
        compiler_params=_params(("parallel", "arbitrary"), 56),
    )(*_hbm(h, *pieces))


ELEMENTWISE_TILE_BYTES = MIB


def _row_tile(rows, cols):
    for t in (512, 256, 128, 64, 32, 16, 8):
        if rows % t == 0 and t * cols * 4 <= ELEMENTWISE_TILE_BYTES:
            return t
    return rows


def _pair_sum(g, got, core, name):
    nch, R, C = g.shape
    h = R // 2
    tr = _row_tile(h, C)
    nt = h // tr

    def body(c_ref, g_ref, got_ref, p_ref, pb_ref):
        s = g_ref[...] + got_ref[...]
        p_ref[...] = s
        pb_ref[...] = s.astype(BF16)

    blk = pl.BlockSpec((None, tr, C), lambda j, i, c_ref: (j, i, 0))
    return pl.pallas_call(
        body,
        name=name,
        grid_spec=pltpu.PrefetchScalarGridSpec(
            num_scalar_prefetch=1,
            grid=(nch, nt),
            in_specs=[pl.BlockSpec((None, tr, C), lambda j, i, c_ref: (j, c_ref[0] * nt + i, 0)), blk],
            out_specs=[blk, blk],
        ),
        out_shape=[_sds((nch, h, C), F32), _sds((nch, h, C), BF16)],
        compiler_params=_params(("parallel", "parallel"), 48),
    )(core, *_hbm(g, got))


def _chip_sum(p, got, chip_core, name):
    _, h, C = p.shape
    tr = _row_tile(h, C)
    nt = h // tr

    def body(jc_ref, p_ref, g0_ref, g1_ref, g2_ref, o_ref):
        o_ref[...] = ((p_ref[...] + g0_ref[...].astype(F32)) + g1_ref[...].astype(F32)) + g2_ref[...].astype(F32)

    rel = lambda r: pl.BlockSpec((None, tr, C), lambda i, jc_ref: (r, i, 0))
    return pl.pallas_call(
        body,
        name=name,
        grid_spec=pltpu.PrefetchScalarGridSpec(
            num_scalar_prefetch=1,
            grid=(nt,),
            in_specs=[pl.BlockSpec((None, tr, C), lambda i, jc_ref: (jc_ref[0], i, 0)), rel(0), rel(1), rel(2)],
            out_specs=pl.BlockSpec((tr, C), lambda i, jc_ref: (jc_ref[1] * nt + i, 0)),
        ),
        out_shape=_sds((2 * h, C), F32),
        compiler_params=_params(("parallel",), 48),
    )(chip_core, *_hbm(p, got, got, got))


def _place_shards(shards, chip, name):
    n = len(shards)
    tiles = [_row_tile(s.shape[0], s.shape[1]) for s in shards]
    steps = max(s.shape[0] // t for s, t in zip(shards, tiles))
    tiles = [s.shape[0] // steps for s in shards]

    def body(j_ref, *refs):
        for a in range(n):
            refs[n + a][...] = refs[a][...].astype(BF16)

    return pl.pallas_call(
        body,
        name=name,
        grid_spec=pltpu.PrefetchScalarGridSpec(
            num_scalar_prefetch=1,
            grid=(steps,),
            in_specs=[pl.BlockSpec((t, s.shape[1]), lambda i, j_ref: (i, 0)) for s, t in zip(shards, tiles)],
            out_specs=[pl.BlockSpec((None, t, s.shape[1]), lambda i, j_ref: (j_ref[0], i, 0))
                       for s, t in zip(shards, tiles)],
        ),
        out_shape=[_sds((N_CHIPS,) + s.shape, BF16) for s in shards],
        compiler_params=_params(("parallel",), 48),
    )(chip, *_hbm(*shards))


def _adamw(w, g, m, v, name):
    R, C = w.shape
    tr = _row_tile(R, C)
    c1 = 1.0 - ADAM_B1 ** ADAM_STEP
    c2 = 1.0 - ADAM_B2 ** ADAM_STEP

    def body(w_ref, g_ref, m_ref, v_ref, d_ref, nm_ref, nv_ref):
        g = g_ref[...]
        nm = ADAM_B1 * m_ref[...] + (1.0 - ADAM_B1) * g
        nv = ADAM_B2 * v_ref[...] + (1.0 - ADAM_B2) * (g * g)
        nm_ref[...] = nm
        nv_ref[...] = nv
        d_ref[...] = (-ADAM_LR) * ((nm / c1) / (jnp.sqrt(nv / c2) + ADAM_EPS) + ADAM_WD * w_ref[...])

    spec = pl.BlockSpec((tr, C), lambda i: (i, 0))
    return pl.pallas_call(
        body, name=name, grid=(R // tr,), in_specs=[spec] * 4, out_specs=[spec] * 3,
        out_shape=[_sds((R, C), F32)] * 3, compiler_params=_params(("parallel",), 48),
    )(*_hbm(w, g, m, v))


def _place():
    return lax.axis_index("x"), lax.axis_index("y"), lax.axis_index("c")


def _chip_of(x, y, r):
    return (x ^ (r >> 1), y ^ (r & 1))


ANY = pl.BlockSpec(memory_space=pl.ANY)


def _gather_weights(placed, cw8):
    nbig = len(placed)
    halves = [s.shape[1] // 2 for s in placed]
    pieces = [max(1, h // 128) for h in halves]
    rows = [h // p for h, p in zip(halves, pieces)]
    order = [(a, q) for q in range(max(pieces)) for a in range(nbig) if q < pieces[a]]
    ici_sem = {(a, q, r): 3 * i + (r - 1) for i, (a, q) in enumerate(order) for r in (1, 2, 3)}
    cw_sem = {r: 3 * len(order) + (r - 1) for r in (1, 2, 3)}
    d2d_sem = {key: 3 * len(order) + 3 + k for key, k in ici_sem.items()}
    nsem = 6 * len(order) + 3

    def body(*refs):
        cw_ref, dsts, gcw_ref = refs[nbig], refs[nbig + 1:2 * nbig + 1], refs[2 * nbig + 1]
        send_sems, recv_sems = refs[2 * nbig + 2:]
        x, y, c = _place()
        j = 2 * x + y

        def piece_rows(a, q, core):
            return pl.ds(pl.multiple_of(core * halves[a] + q * rows[a], 16), rows[a])

        def ici(a, q, r):
            tx, ty = _chip_of(x, y, r)
            k = ici_sem[(a, q, r)]
            region = dsts[a].at[j, piece_rows(a, q, c), :]
            return pltpu.make_async_remote_copy(
                src_ref=region, dst_ref=region, send_sem=send_sems.at[k], recv_sem=recv_sems.at[k],
                device_id=(tx, ty, c), device_id_type=MESH)

        def ici_landed(a, q, r):
            tx, ty = _chip_of(x, y, r)
            k = ici_sem[(a, q, r)]
            region = dsts[a].at[2 * tx + ty, piece_rows(a, q, c), :]
            return pltpu.make_async_remote_copy(
                src_ref=region, dst_ref=region, send_sem=send_sems.at[k], recv_sem=recv_sems.at[k],
                device_id=(tx, ty, c), device_id_type=MESH)

        def d2d(a, q, r, core):
            tx, ty = _chip_of(x, y, r)
            k = d2d_sem[(a, q, r)]
            region = dsts[a].at[2 * tx + ty, piece_rows(a, q, core), :]
            return pltpu.make_async_remote_copy(
                src_ref=region, dst_ref=region, send_sem=send_sems.at[k], recv_sem=recv_sems.at[k],
                device_id=(x, y, 1 - c), device_id_type=MESH)

        def cw_copy(r):
            tx, ty = _chip_of(x, y, r)
            k = cw_sem[r]
            return pltpu.make_async_remote_copy(
                src_ref=cw_ref, dst_ref=gcw_ref.at[j], send_sem=send_sems.at[k], recv_sem=recv_sems.at[k],
                device_id=(tx, ty, c), device_id_type=MESH)

        def cw_landed(r):
            tx, ty = _chip_of(x, y, r)
            k = cw_sem[r]
            region = gcw_ref.at[2 * tx + ty]
            return pltpu.make_async_remote_copy(
                src_ref=region, dst_ref=region, send_sem=send_sems.at[k], recv_sem=recv_sems.at[k],
                device_id=(tx, ty, c), device_id_type=MESH)

        first = [ici(a, q, r) for (a, q) in order for r in (1, 2, 3)] + [cw_copy(r) for r in (1, 2, 3)]
        for cp in first:
            cp.start()
        passed = []
        for (a, q) in order:
            for r in (1, 2, 3):
                ici_landed(a, q, r).wait_recv()
                cp = d2d(a, q, r, c)
                cp.start()
                passed.append(cp)
        for r in (1, 2, 3):
            cw_landed(r).wait_recv()
        for (a, q) in order:
            for r in (1, 2, 3):
                d2d(a, q, r, 1 - c).wait_recv()
        for cp in first + passed:
            cp.wait_send()

    return pl.pallas_call(
        body,
        name="gather_weights",
        in_specs=[ANY] * (nbig + 1),
        out_specs=[ANY] * (nbig + 1),
        out_shape=[_sds(s.shape, s.dtype) for s in placed] + [_sds((N_CHIPS,) + cw8.shape, cw8.dtype)],
        input_output_aliases={a: a for a in range(nbig)},
        scratch_shapes=[pltpu.SemaphoreType.DMA((nsem,)), pltpu.SemaphoreType.DMA((nsem,))],
    )(*placed, cw8)


def _gather_late_start(placed, after, name):
    n = len(placed)
    halves = [s.shape[1] // 2 for s in placed]

    def body(*refs):
        g_refs = refs[0:n]
        send_sems, recv_sems, token = refs[n + 1], refs[n + 2], refs[-1]
        x, y, c = _place()
        j = 2 * x + y
        for a in range(n):
            mine = g_refs[a].at[j, pl.ds(pl.multiple_of(c * halves[a], 16), halves[a]), :]
            for r in (1, 2, 3):
                tx, ty = _chip_of(x, y, r)
                for to_core in (0, 1):
                    k = ((a * 3 + (r - 1)) * 2 + c) * 2 + to_core
                    pltpu.make_async_remote_copy(
                        src_ref=mine, dst_ref=mine, send_sem=send_sems.at[k], recv_sem=recv_sems.at[k],
                        device_id=(tx, ty, to_core), device_id_type=MESH).start()
        token[...] = jnp.zeros_like(token)

    hbm = lambda t: pltpu.HBM(t.shape, t.dtype)
    keep = lambda t: pltpu.with_memory_space_constraint(t, pltpu.HBM)
    nsem = 12 * n
    outs = pl.pallas_call(
        body,
        name=name,
        in_specs=[HBM] * n + [ANY],
        out_specs=(SEM, SEM, *[HBM] * n, pl.BlockSpec(memory_space=pltpu.VMEM)),
        out_shape=(pltpu.SemaphoreType.DMA((nsem,)), pltpu.SemaphoreType.DMA((nsem,)), *[hbm(p) for p in placed],
                   jax.ShapeDtypeStruct((8, 128), F32)),
        input_output_aliases={i: 2 + i for i in range(n)},
        compiler_params=pltpu.CompilerParams(has_side_effects=DATAFLOW),
    )(*[keep(p) for p in placed], after)
    return outs[0], outs[1], list(outs[2:2 + n]), outs[-1]


def _gather_late_wait(send_sems, recv_sems, thru, after, name):
    n = len(thru)
    halves = [s.shape[1] // 2 for s in thru]

    def body(*refs):
        g_refs = refs[0:n]
        send_sems, recv_sems = refs[n], refs[n + 1]
        x, y, c = _place()
        j = 2 * x + y
        for a in range(n):
            mine = g_refs[a].at[j, pl.ds(pl.multiple_of(c * halves[a], 16), halves[a]), :]
            for r in (1, 2, 3):
                tx, ty = _chip_of(x, y, r)
                for other in (0, 1):
                    k_out = ((a * 3 + (r - 1)) * 2 + c) * 2 + other
                    pltpu.make_async_remote_copy(
                        src_ref=mine, dst_ref=mine, send_sem=send_sems.at[k_out], recv_sem=recv_sems.at[k_out],
                        device_id=(tx, ty, other), device_id_type=MESH).wait_send()
                    k_in = ((a * 3 + (r - 1)) * 2 + other) * 2 + c
                    theirs = g_refs[a].at[2 * tx + ty, pl.ds(other * halves[a], halves[a]), :]
                    pltpu.make_async_remote_copy(
                        src_ref=theirs, dst_ref=theirs, send_sem=send_sems.at[k_in], recv_sem=recv_sems.at[k_in],
                        device_id=(tx, ty, other), device_id_type=MESH).wait_recv()

    hbm = lambda t: pltpu.HBM(t.shape, t.dtype)
    outs = pl.pallas_call(
        body,
        name=name,
        in_specs=[HBM] * n + [SEM, SEM, ANY],
        out_specs=[HBM] * n,
        out_shape=[hbm(t) for t in thru],
        input_output_aliases={i: i for i in range(n)},
        compiler_params=pltpu.CompilerParams(has_side_effects=DATAFLOW),
    )(*thru, send_sems, recv_sems, after)
    return list(outs)


D2D_PIECE_ROWS = 64


def _pair_exchange(grads, name):
    n = len(grads)
    halves = [g.shape[1] // 2 for g in grads]

    def body(*refs):
        g_refs, got_refs = refs[0:n], refs[n:2 * n]
        send_sems, recv_sems = refs[2 * n:]
        x, y, c = _place()

        def copy(a, src, dst):
            return pltpu.make_async_remote_copy(
                src_ref=src, dst_ref=dst, send_sem=send_sems.at[a], recv_sem=recv_sems.at[a],
                device_id=(x, y, 1 - c), device_id_type=MESH)

        for a in range(n):
            for jj in range(N_CHIPS):
                for q in range(halves[a] // D2D_PIECE_ROWS):
                    src_rows = pl.ds(pl.multiple_of((1 - c) * halves[a] + q * D2D_PIECE_ROWS, 8), D2D_PIECE_ROWS)
                    dst_rows = pl.ds(q * D2D_PIECE_ROWS, D2D_PIECE_ROWS)
                    copy(a, g_refs[a].at[jj, src_rows, :], got_refs[a].at[jj, dst_rows, :]).start()
        for a in range(n):
            sent = g_refs[a].at[:, pl.ds(pl.multiple_of((1 - c) * halves[a], 8), halves[a]), :]
            copy(a, sent, got_refs[a]).wait()

    return pl.pallas_call(
        body,
        name=name,
        in_specs=[ANY] * n,
        out_specs=[ANY] * n,
        out_shape=[_sds((N_CHIPS, h, g.shape[2]), F32) for g, h in zip(grads, halves)],
        scratch_shapes=[pltpu.SemaphoreType.DMA((n,)), pltpu.SemaphoreType.DMA((n,))],
    )(*grads)


HBM = pl.BlockSpec(memory_space=pltpu.HBM)
SEM = pl.BlockSpec(memory_space=pltpu.SEMAPHORE)
DATAFLOW = pltpu.SideEffectType.DATAFLOW_SIDE_EFFECTING


def _chip_copy(p_refs, land_refs, send_sems, recv_sems, a, r):
    x, y, c = _place()
    tx, ty = _chip_of(x, y, r)
    k = a * 3 + (r - 1)
    return pltpu.make_async_remote_copy(
        src_ref=p_refs[a].at[2 * tx + ty], dst_ref=land_refs[a].at[r - 1],
        send_sem=send_sems.at[k], recv_sem=recv_sems.at[k], device_id=(tx, ty, c), device_id_type=MESH)


def _chip_exchange_start(psums, name):
    n = len(psums)
    lands = [lax.empty((3,) + p.shape[1:], p.dtype) for p in psums]

    def body(*refs):
        p_refs, land_refs = refs[0:n], refs[n:2 * n]
        send_sems, recv_sems, token = refs[2 * n], refs[2 * n + 1], refs[-1]
        for a in range(n):
            for r in (1, 2, 3):
                _chip_copy(p_refs, land_refs, send_sems, recv_sems, a, r).start()
        token[...] = jnp.zeros_like(token)

    hbm = lambda t: pltpu.HBM(t.shape, t.dtype)
    keep = lambda t: pltpu.with_memory_space_constraint(t, pltpu.HBM)
    outs = pl.pallas_call(
        body,
        name=name,
        in_specs=[HBM] * (2 * n),
        out_specs=(SEM, SEM, *[HBM] * (2 * n), pl.BlockSpec(memory_space=pltpu.VMEM)),
        out_shape=(pltpu.SemaphoreType.DMA((3 * n,)), pltpu.SemaphoreType.DMA((3 * n,)),
                   *[hbm(p) for p in psums], *[hbm(l) for l in lands], _sds((8, 128), F32)),
        input_output_aliases={i: 2 + i for i in range(2 * n)},
        compiler_params=pltpu.CompilerParams(has_side_effects=DATAFLOW),
    )(*[keep(p) for p in psums], *[keep(l) for l in lands])
    return outs[0], outs[1], list(outs[2:2 + n]), list(outs[2 + n:2 + 2 * n]), outs[-1]


def _chip_exchange_wait(send_sems, recv_sems, p_thru, land_thru, after, name):
    n = len(p_thru)

    def body(*refs):
        p_refs, land_refs = refs[0:n], refs[n:2 * n]
        send_sems, recv_sems = refs[2 * n], refs[2 * n + 1]
        for a in range(n):
            for r in (1, 2, 3):
                copy = _chip_copy(p_refs, land_refs, send_sems, recv_sems, a, r)
                copy.wait_send()
                copy.wait_recv()

    hbm = lambda t: pltpu.HBM(t.shape, t.dtype)
    outs = pl.pallas_call(
        body,
        name=name,
        in_specs=[HBM] * (2 * n) + [SEM, SEM, ANY],
        out_specs=[HBM] * (2 * n),
        out_shape=[hbm(p) for p in p_thru] + [hbm(l) for l in land_thru],
        input_output_aliases={i: i for i in range(2 * n)},
        compiler_params=pltpu.CompilerParams(has_side_effects=DATAFLOW),
    )(*p_thru, *land_thru, send_sems, recv_sems, after)
    return list(outs[n:2 * n])


def _pair_share(fulls):
    n = len(fulls)
    halves = [f.shape[0] // 2 for f in fulls]

    def body(*refs):
        full_refs = refs[n:2 * n]
        send_sems, recv_sems = refs[2 * n:]
        x, y, c = _place()

        def half_of(a, core):
            return full_refs[a].at[pl.ds(pl.multiple_of(core * halves[a], 8), halves[a]), :]

        def remote(a, src, dst):
            return pltpu.make_async_remote_copy(
                src_ref=src, dst_ref=dst, send_sem=send_sems.at[a], recv_sem=recv_sems.at[a],
                device_id=(x, y, 1 - c), device_id_type=MESH)

        for a in range(n):
            for q in range(halves[a] // D2D_PIECE_ROWS):
                piece = full_refs[a].at[
                    pl.ds(pl.multiple_of(c * halves[a] + q * D2D_PIECE_ROWS, 8), D2D_PIECE_ROWS), :]
                remote(a, piece, piece).start()
        for a in range(n):
            remote(a, half_of(a, c), half_of(a, c)).wait_send()
            remote(a, half_of(a, 1 - c), half_of(a, 1 - c)).wait_recv()

    return pl.pallas_call(
        body,
        name="pair_share",
        in_specs=[ANY] * n,
        out_specs=[ANY] * n,
        out_shape=[_sds(f.shape, F32) for f in fulls],
        input_output_aliases={a: a for a in range(n)},
        scratch_shapes=[pltpu.SemaphoreType.DMA((n,)), pltpu.SemaphoreType.DMA((n,))],
    )(*fulls)


def _allreduce_small(s):
    R, C = s.shape

    def body(s_ref, o_ref, sib, chips, send_sems, recv_sems):
        x, y, c = _place()
        j = 2 * x + y
        def to_sib(src, dst):
            return pltpu.make_async_remote_copy(
                src_ref=src, dst_ref=dst, send_sem=send_sems.at[0], recv_sem=recv_sems.at[0],
                device_id=(x, y, 1 - c), device_id_type=MESH)

        for q in range(R // 8):
            to_sib(s_ref.at[pl.ds(8 * q, 8), :], sib.at[pl.ds(8 * q, 8), :]).start()
        to_sib(s_ref, sib).wait()
        chips[j] = s_ref[...] + sib[...]
        sends = []
        for r in (1, 2, 3):
            tx, ty = _chip_of(x, y, r)
            cp = pltpu.make_async_remote_copy(
                src_ref=chips.at[j], dst_ref=chips.at[j], send_sem=send_sems.at[r], recv_sem=recv_sems.at[r],
                device_id=(tx, ty, c), device_id_type=MESH)
            cp.start()
            sends.append(cp)
        for r in (1, 2, 3):
            tx, ty = _chip_of(x, y, r)
            region = chips.at[2 * tx + ty]
            pltpu.make_async_remote_copy(
                src_ref=region, dst_ref=region, send_sem=send_sems.at[r], recv_sem=recv_sems.at[r],
                device_id=(tx, ty, c), device_id_type=MESH).wait_recv()
        for cp in sends:
            cp.wait_send()
        o_ref[...] = (chips[0] + chips[1]) + (chips[2] + chips[3])

    return pl.pallas_call(
        body,
        name="allreduce_small",
        in_specs=[pl.BlockSpec(memory_space=pltpu.VMEM)],
        out_specs=pl.BlockSpec(memory_space=pltpu.VMEM),
        out_shape=jax.ShapeDtypeStruct((R, C), F32),
        scratch_shapes=[pltpu.VMEM((R, C), F32), pltpu.VMEM((N_CHIPS, R, C), F32),
                        pltpu.SemaphoreType.DMA((4,)), pltpu.SemaphoreType.DMA((4,))],
    )(s)


def _block_diag(w):
    w4 = w.reshape(4, 4, RNN_BLOCK_W, RNN_BLOCK_W)
    eye = jnp.eye(4, dtype=w.dtype)
    return jnp.einsum("jaik,ab->jaibk", w4, eye).reshape(4, RNN_TILE, RNN_TILE)


def _block_diag_part(d):
    d5 = d.reshape(4, 4, RNN_BLOCK_W, 4, RNN_BLOCK_W)
    return jnp.stack([d5[:, a, :, a, :] for a in range(4)], axis=1).reshape(RNN_BLOCKS, RNN_BLOCK_W, RNN_BLOCK_W)


def _local_grads(x, target, g_pre, w_in_g, b_gate, conv_w, conv_b, w_rg_a, b_rg_a, w_rg_x, b_rg_x, lam, sinks,
                 out_weights, fwd_token, g_post, on_out_grads, on_w_in_grad):
    wa_bd = _block_diag(w_rg_a).astype(BF16)
    wx_bd = _block_diag(w_rg_x).astype(BF16)
    b_a = b_rg_a.reshape(1, D_RNN)
    b_x = b_rg_x.reshape(1, D_RNN)

    proj, h = _proj_fwd(x, g_pre, w_in_g)
    y_rnn, z_rnn = _rnn_fwd(proj, conv_w, conv_b, wa_bd, wx_bd, b_a, b_x, lam, fwd_token)
    bias = _attn_bias()
    y_attn, z_attn, lse = _attn_fwd(proj, sinks, bias)
    w_rnn_out, w_attn_out, w_out = out_weights(z_attn)
    dyx, dz_rnn, dz_attn, dml, merged, dout, dbr_rnn, dbr_attn, head_small = _head(
        x, target, z_rnn, z_attn, proj, b_gate, g_post, w_rnn_out, w_attn_out, w_out)
    dw_out = _matmul_tn(merged, dout, "dw_out")
    dw_rnn_out = _matmul_tn(z_rnn, dbr_rnn, "dw_rnn_out")
    dw_attn_out = _matmul_tn(z_attn, dbr_attn, "dw_attn_out")
    shard_rows = lambda d: d.reshape(N_CHIPS, OUT_SHARD, D_MODEL)
    token = on_out_grads([shard_rows(dw_rnn_out), shard_rows(dw_attn_out), shard_rows(dw_out)])
    dq, dk, dv, dag, attn_small = _attn_bwd(proj, y_attn, lse, dz_attn, sinks, bias, token)
    drx, drg, dwa_t, dwx_t, rnn_small = _rnn_bwd(proj, y_rnn, dz_rnn, conv_w, conv_b, wa_bd, wx_bd, b_a, b_x, lam)
    dproj = [drx, drg, dq, dk, dv, dag, dml]
    token = on_w_in_grad(_dw_in(h, dproj))
    grad_x, dh_small = _dh_bwd(dproj, w_in_g, x, dyx, g_pre, token)
    small = jnp.concatenate([rnn_small, head_small, dh_small + attn_small,
                             _block_diag_part(dwa_t).reshape(64, 1024), _block_diag_part(dwx_t).reshape(64, 1024)], axis=0)
    return grad_x, small


ROW_LOSS = 11


def _rows8(parts):
    out = None
    for r, a in parts:
        p = jnp.pad(a, ((r, 8 - r - a.shape[0]), (0, 1024 - a.shape[1])))
        out = p if out is None else out + p
    return out


def _pack_small(p):
    g0 = _rows8([(0, p["b_rg_a"].reshape(1, 1024)), (1, p["b_rg_x"].reshape(1, 1024)), (2, p["lru_lambda"]),
                 (3, p["conv_b"]), (4, p["conv_w"][0])])
    g1 = _rows8([(0, p["post_norm_g"]), (1, p["b_gate"].reshape(2, 1024))])
    g2 = _rows8([(0, p["pre_norm_g"]), (1, p["attn_sinks"])])
    return jnp.concatenate([g0, g1, g2, p["w_rg_a"].reshape(64, 1024), p["w_rg_x"].reshape(64, 1024)], axis=0)


def _unpack_small(s, conv_cols):
    return {
        "b_rg_a": s[0:1].reshape(1, 16, 64), "b_rg_x": s[1:2].reshape(1, 16, 64), "lru_lambda": s[2:3],
        "conv_b": s[3:4], "conv_w": s[4:8, 0:conv_cols].reshape(1, CONV_W, conv_cols),
        "post_norm_g": s[8:9], "b_gate": s[9:11].reshape(1, 2048),
        "pre_norm_g": s[16:17], "attn_sinks": s[17:18, 0:N_Q_HEADS],
        "w_rg_a": s[24:88].reshape(1, 16, 64, 64), "w_rg_x": s[88:152].reshape(1, 16, 64, 64),
    }


WEIGHTS = ["pre_norm_g", "w_in", "b_gate", "conv_w", "conv_b", "w_rg_a", "b_rg_a", "w_rg_x", "b_rg_x", "lru_lambda",
           "attn_sinks", "w_rnn_out", "w_attn_out", "w_out", "post_norm_g"]
BIG = ["w_in", "w_rnn_out", "w_attn_out", "w_out"]


def kernel(x, pre_norm_g, w_in, b_gate, conv_w, conv_b, w_rg_a, b_rg_a, w_rg_x, b_rg_x, lru_lambda, attn_sinks, w_rnn_out, w_attn_out, w_out, post_norm_g, loss_target, m_pre_norm_g, m_w_in, m_b_gate, m_conv_w, m_conv_b, m_w_rg_a, m_b_rg_a, m_w_rg_x, m_b_rg_x, m_lru_lambda, m_attn_sinks, m_w_rnn_out, m_w_attn_out, m_w_out, m_post_norm_g, v_pre_norm_g, v_w_in, v_b_gate, v_conv_w, v_conv_b, v_w_rg_a, v_b_rg_a, v_w_rg_x, v_b_rg_x, v_lru_lambda, v_attn_sinks, v_w_rnn_out, v_w_attn_out, v_w_out, v_post_norm_g):
    w = dict(pre_norm_g=pre_norm_g, w_in=w_in, b_gate=b_gate, conv_w=conv_w, conv_b=conv_b, w_rg_a=w_rg_a,
             b_rg_a=b_rg_a, w_rg_x=w_rg_x, b_rg_x=b_rg_x, lru_lambda=lru_lambda, attn_sinks=attn_sinks,
             w_rnn_out=w_rnn_out, w_attn_out=w_attn_out, w_out=w_out, post_norm_g=post_norm_g)
    m = dict(pre_norm_g=m_pre_norm_g, w_in=m_w_in, b_gate=m_b_gate, conv_w=m_conv_w, conv_b=m_conv_b, w_rg_a=m_w_rg_a,
             b_rg_a=m_b_rg_a, w_rg_x=m_w_rg_x, b_rg_x=m_b_rg_x, lru_lambda=m_lru_lambda, attn_sinks=m_attn_sinks,
             w_rnn_out=m_w_rnn_out, w_attn_out=m_w_attn_out, w_out=m_w_out, post_norm_g=m_post_norm_g)
    v = dict(pre_norm_g=v_pre_norm_g, w_in=v_w_in, b_gate=v_b_gate, conv_w=v_conv_w, conv_b=v_conv_b, w_rg_a=v_w_rg_a,
             b_rg_a=v_b_rg_a, w_rg_x=v_w_rg_x, b_rg_x=v_b_rg_x, lru_lambda=v_lru_lambda, attn_sinks=v_attn_sinks,
             w_rnn_out=v_w_rnn_out, w_attn_out=v_w_attn_out, w_out=v_w_out, post_norm_g=v_post_norm_g)
    chip = 2 * lax.axis_index("x") + lax.axis_index("y")

    chip_idx = chip.astype(jnp.int32).reshape(1)
    chip_core = jnp.stack([chip, lax.axis_index("c")]).astype(jnp.int32)
    cw8 = jnp.pad(conv_w[0], ((0, 8 - CONV_W), (0, 0)))
    placed = _place_shards([w_in[0], w_rnn_out[0], w_attn_out[0], w_out[0]], chip_idx, "place_shards")
    win_g, cw_g = _gather_weights(placed[:1], cw8)
    late_send, late_recv, late_thru, late_token = _gather_late_start(placed[1:], win_g, "gather_late_start")
    cw_g = lax.dynamic_update_slice_in_dim(cw_g, cw8[None], chip, axis=0)
    conv_w_full = jnp.transpose(cw_g[:, 0:CONV_W, :], (1, 0, 2)).reshape(CONV_W, D_RNN)

    core_idx = lax.axis_index("c").astype(jnp.int32).reshape(1)
    started = {}

    def start_reduction(tag, grads):
        got = _pair_exchange(grads, "pair_exchange_" + tag)
        sums = [_pair_sum(g, o, core_idx, "pair_sum_%s_%d" % (tag, a)) for a, (g, o) in enumerate(zip(grads, got))]
        send_sems, recv_sems, p_thru, land_thru, token = _chip_exchange_start(
            [pb for _, pb in sums], "chip_exchange_start_" + tag)
        started[tag] = ([p for p, _ in sums], send_sems, recv_sems, p_thru, land_thru)
        return token

    def end_reduction(tag, after):
        psums, send_sems, recv_sems, p_thru, land_thru = started[tag]
        landed = _chip_exchange_wait(send_sems, recv_sems, p_thru, land_thru, after, "chip_exchange_wait_" + tag)
        return [_chip_sum(p, l, chip_core, "chip_sum_%s_%d" % (tag, a)) for a, (p, l) in enumerate(zip(psums, landed))]

    def out_weights(after):
        gathered = _gather_late_wait(late_send, late_recv, late_thru, after, "gather_late_wait")
        return [g.reshape(D_MODEL, D_MODEL) for g in gathered]

    grad_x, small = _local_grads(
        x[0], loss_target[0], pre_norm_g, win_g, b_gate, conv_w_full, conv_b, w_rg_a[0], b_rg_a[0], w_rg_x[0],
        b_rg_x[0], lru_lambda, attn_sinks[0], out_weights, late_token, post_norm_g,
        on_out_grads=lambda grads: start_reduction("out", grads),
        on_w_in_grad=lambda grad: start_reduction("in", [grad]))

    halves = end_reduction("in", grad_x) + end_reduction("out", grad_x)
    gbig = dict(zip(BIG, _pair_share(halves)))

    small_sum = _allreduce_small(small)
    total_loss = small_sum[ROW_LOSS, 0]
    gsmall = _unpack_small(small_sum, D_RNN)
    conv_shard = D_RNN // N_CHIPS
    gsmall["conv_w"] = lax.dynamic_slice_in_dim(gsmall["conv_w"], chip * conv_shard, conv_shard, axis=2)

    grads, delta, new_m, new_v = {}, {}, {}, {}
    for n in BIG:
        grads[n] = gbig[n][None]
        d, nm, nv = _adamw(w[n][0], gbig[n], m[n][0], v[n][0], "adamw_" + n)
        delta[n], new_m[n], new_v[n] = d[None], nm[None], nv[None]
    pick = lambda t: {k: t[k] for k in gsmall}
    d, nm, nv = _adamw(_pack_small(pick(w)), _pack_small(gsmall), _pack_small(pick(m)), _pack_small(pick(v)),
                       "adamw_small")
    ud, um, uv = _unpack_small(d, conv_shard), _unpack_small(nm, conv_shard), _unpack_small(nv, conv_shard)
    for n in gsmall:
        grads[n] = gsmall[n].reshape(w[n].shape)
        delta[n] = ud[n].reshape(w[n].shape)
        new_m[n] = um[n].reshape(w[n].shape)
        new_v[n] = uv[n].reshape(w[n].shape)

    return (total_loss, grad_x[None], *[grads[n] for n in WEIGHTS], *[delta[n] for n in WEIGHTS],
            *[new_m[n] for n in WEIGHTS], *[new_v[n] for n in WEIGHTS])
```

```python
import functools
import math

import jax
import jax.numpy as jnp
from jax import lax
from jax.experimental import pallas as pl
from jax.experimental.pallas import tpu as pltpu

F32 = jnp.float32
BF16 = jnp.bfloat16

D_MODEL = 1024
D_RNN = 1024
RNN_BLOCKS = 16
RNN_BLOCK_W = 64
CONV_W = 4
LRU_C = 8.0
N_Q_HEADS = 16
N_KV_HEADS = 4
GROUP = 4
HEAD_DIM = 64
D_KV = 256
BLOCK = 128
ALIBI_MAX_BIAS = 8.0
EPS = 1e-6
D_IN = 6656
N_CHIPS = 4
W_IN_SHARD = D_IN // N_CHIPS
OUT_SHARD = D_MODEL // N_CHIPS
ADAM_LR = 0.001
ADAM_B1 = 0.9
ADAM_B2 = 0.999
ADAM_EPS = 1e-08
ADAM_WD = 0.01
ADAM_STEP = 10
NEG_BIG = -1e30
MIB = 1 << 20

COL_RNN_X = 0
COL_RNN_GATE = 4
COL_Q = 8
COL_K = 12
COL_V = 13
COL_ATTN_GATE = 14
COL_MERGE = 18

RNN_TILE = 256
RNN_CHUNK = 256
SMALL_ROWS = 152
MESH = pl.DeviceIdType.MESH


def _sds(shape, dtype):
    return pltpu.HBM(shape, dtype)


def _params(sem=None, vmem_mib=None):
    kw = {}
    if sem is not None:
        kw["dimension_semantics"] = sem
    if vmem_mib is not None:
        kw["vmem_limit_bytes"] = vmem_mib * MIB
    return pltpu.CompilerParams(**kw)


def _hbm(*arrays):
    return [pltpu.with_memory_space_constraint(a, pltpu.HBM) for a in arrays]


def _dot(a, b):
    return jnp.dot(a, b, preferred_element_type=F32)


def _dot_nt(a, b):
    return lax.dot_general(a, b, (((1,), (1,)), ((), ())), preferred_element_type=F32)


def _dot_tn(a, b):
    return lax.dot_general(a, b, (((0,), (0,)), ((), ())), preferred_element_type=F32)


def _sigmoid(x):
    return 0.5 * jnp.tanh(0.5 * x) + 0.5


def _sigmoid_small(x):
    return 1.0 / (1.0 + jnp.exp(-x))


def _softplus(x):
    return jnp.maximum(x, 0.0) + jnp.log(1.0 + jnp.exp(-jnp.abs(x)))


def _one_minus_square(a, log_a):
    return -jnp.tanh(log_a) * (a * a + 1.0)


def _proj_fwd(x, g_pre, w_in_g):
    T = x.shape[0]
    tm = min(1024, T)

    def body(x_ref, g_ref, w_ref, proj_ref, ht_ref, h_s):
        @pl.when(pl.program_id(1) == 0)
        def _():
            xv = x_ref[...]
            rstd = lax.rsqrt(jnp.mean(xv * xv, axis=-1, keepdims=True) + EPS)
            hf = (xv * rstd) * g_ref[...]
            h_s[...] = hf.astype(BF16)
            ht_ref[...] = hf.T.astype(BF16)

        proj_ref[...] = _dot(h_s[...], w_ref[...]).astype(BF16)

    return pl.pallas_call(
        body,
        name="proj_fwd",
        grid=(T // tm, N_CHIPS),
        in_specs=[
            pl.BlockSpec((tm, D_MODEL), lambda i, j: (i, 0)),
            pl.BlockSpec((1, D_MODEL), lambda i, j: (0, 0)),
            pl.BlockSpec((None, D_MODEL, W_IN_SHARD), lambda i, j: (j, 0, 0)),
        ],
        out_specs=[
            pl.BlockSpec((tm, W_IN_SHARD), lambda i, j: (i, j)),
            pl.BlockSpec((D_MODEL, tm), lambda i, j: (0, i)),
        ],
        out_shape=[_sds((T, D_IN), BF16), _sds((D_MODEL, T), BF16)],
        scratch_shapes=[pltpu.VMEM((tm, D_MODEL), BF16)],
        compiler_params=_params(("parallel", "arbitrary"), 48),
    )(*_hbm(x, g_pre, w_in_g))


def _shift_down(x, tail, s, row):
    n = x.shape[0]
    xs = pltpu.roll(x, s, 0)
    tail_t = jnp.tile(pltpu.roll(tail, s, 0), (n // 8, 1))
    return jnp.where(row < s, tail_t, xs)


def _shift_up(x, head, s, row):
    n = x.shape[0]
    xs = pltpu.roll(x, n - s, 0)
    head_t = jnp.tile(pltpu.roll(head, 8 - s, 0), (n // 8, 1))
    return jnp.where(row >= n - s, head_t, xs)


def _conv_taps(x, tail, row):
    return [_shift_down(x, tail, 3, row), _shift_down(x, tail, 2, row), _shift_down(x, tail, 1, row), x]


def _rglru_gates(c, wa, wx, ba, bx, lam):
    cb = c.astype(BF16)
    r = _sigmoid_small(_dot(cb, wa) + ba)
    i = _sigmoid(_dot(cb, wx) + bx)
    log_a = (-LRU_C) * r * _softplus(-lam)
    a = jnp.exp(log_a)
    mult = jnp.sqrt(_one_minus_square(a, log_a))
    return cb, r, i, a, mult


SUBLANES = 8


def _scan_down(a, u, row):
    n = a.shape[0]
    s = 1
    while s < SUBLANES:
        a_sh = jnp.where(row >= s, pltpu.roll(a, s, 0), 1.0)
        u_sh = jnp.where(row >= s, pltpu.roll(u, s, 0), 0.0)
        u = a * u_sh + u
        a = a * a_sh
        s *= 2
    while s < n:
        u = jnp.concatenate([u[:s], a[s:] * u[:n - s] + u[s:]], axis=0)
        a = jnp.concatenate([a[:s], a[s:] * a[:n - s]], axis=0)
        s *= 2
    return a, u


def _scan_up(b, u, row):
    n = b.shape[0]
    s = 1
    while s < SUBLANES:
        b_sh = jnp.where(row < n - s, pltpu.roll(b, n - s, 0), 1.0)
        u_sh = jnp.where(row < n - s, pltpu.roll(u, n - s, 0), 0.0)
        u = b * u_sh + u
        b = b * b_sh
        s *= 2
    while s < n:
        u = jnp.concatenate([b[:n - s] * u[s:] + u[:n - s], u[n - s:]], axis=0)
        b = jnp.concatenate([b[:n - s] * b[s:], b[n - s:]], axis=0)
        s *= 2
    return b, u


def _rnn_fwd(proj, conv_w, conv_b, wa_bd, wx_bd, b_a, b_x, lam, token):
    T = proj.shape[0]
    tc, ct = RNN_CHUNK, RNN_TILE
    nt = T // tc

    def body(x_ref, rg_ref, cw_ref, cb_ref, wa_ref, wx_ref, ba_ref, bx_ref, lam_ref, token_ref, h_ref, z_ref, xtail,
             hcarry):
        @pl.when(pl.program_id(1) == 0)
        def _():
            xtail[...] = jnp.zeros_like(xtail)
            hcarry[...] = jnp.zeros_like(hcarry)

        row = lax.broadcasted_iota(jnp.int32, (tc, ct), 0)
        x = x_ref[...].astype(F32)
        taps = _conv_taps(x, xtail[...], row)
        c = cb_ref[...] + cw_ref[pl.ds(0, 1), :] * taps[0]
        for k in range(1, CONV_W):
            c = c + cw_ref[pl.ds(k, 1), :] * taps[k]
        xtail[...] = x[tc - 8:, :]
        _, _, i, a, mult = _rglru_gates(c, wa_ref[...], wx_ref[...], ba_ref[...], bx_ref[...], lam_ref[...])
        u = mult * (i * c)
        a_cum, h0 = _scan_down(a, u, row)
        h = h0 + a_cum * hcarry[...]
        h_ref[...] = h
        hcarry[...] = h_ref[pl.ds(tc - 1, 1), :]
        rg = rg_ref[...].astype(F32)
        z_ref[...] = (h * (rg * _sigmoid(rg))).astype(BF16)

    col = lambda off: (lambda j, t: (t, off + j))
    vec = pl.BlockSpec((1, ct), lambda j, t: (0, j))
    mat = pl.BlockSpec((None, ct, ct), lambda j, t: (j, 0, 0))
    return pl.pallas_call(
        body,
        name="rnn_fwd",
        grid=(D_RNN // ct, nt),
        in_specs=[
            pl.BlockSpec((tc, ct), col(COL_RNN_X)),
            pl.BlockSpec((tc, ct), col(COL_RNN_GATE)),
            pl.BlockSpec((CONV_W, ct), lambda j, t: (0, j)),
            vec, mat, mat, vec, vec, vec,
            pl.BlockSpec((8, 128), lambda j, t: (0, 0)),
        ],
        out_specs=[pl.BlockSpec((tc, ct), lambda j, t: (t, j)), pl.BlockSpec((tc, ct), lambda j, t: (t, j))],
        out_shape=[_sds((T, D_RNN), F32), _sds((T, D_RNN), BF16)],
        scratch_shapes=[pltpu.VMEM((8, ct), F32), pltpu.VMEM((1, ct), F32)],
        compiler_params=_params(("parallel", "arbitrary"), 32),
    )(*_hbm(proj, proj, conv_w, conv_b, wa_bd, wx_bd, b_a, b_x, lam, token))


def _rnn_bwd(proj, y_rnn, dz_rnn, conv_w, conv_b, wa_bd, wx_bd, b_a, b_x, lam):
    T = proj.shape[0]
    tc, ct = RNN_CHUNK, RNN_TILE
    nt = T // tc
    hb = tc // 8

    def body(x_ref, xh_ref, rg_ref, h_ref, hh_ref, dz_ref, cw_ref, cb_ref, wa_ref, wx_ref, ba_ref, bx_ref, lam_ref,
             dx_ref, drg_ref, dwa_ref, dwx_ref, sm_ref, lam_carry, a_carry, dc_head):
        t = pl.program_id(1)
        first_chunk = t == nt - 1

        @pl.when(t == 0)
        def _():
            lam_carry[...] = jnp.zeros_like(lam_carry)
            a_carry[...] = jnp.zeros_like(a_carry)
            dc_head[...] = jnp.zeros_like(dc_head)
            dwa_ref[...] = jnp.zeros_like(dwa_ref)
            dwx_ref[...] = jnp.zeros_like(dwx_ref)
            sm_ref[...] = jnp.zeros_like(sm_ref)

        row = lax.broadcasted_iota(jnp.int32, (tc, ct), 0)
        keep = jnp.where(first_chunk, 0.0, 1.0)
        x = x_ref[...].astype(F32)
        xtail = xh_ref[...].astype(F32)[8:16, :] * keep
        taps = _conv_taps(x, xtail, row)
        c = cb_ref[...] + cw_ref[pl.ds(0, 1), :] * taps[0]
        for k in range(1, CONV_W):
            c = c + cw_ref[pl.ds(k, 1), :] * taps[k]
        lam = lam_ref[...]
        cb, r, i, a, mult = _rglru_gates(c, wa_ref[...], wx_ref[...], ba_ref[...], bx_ref[...], lam)
        h = h_ref[...]
        h_prev = _shift_down(h, hh_ref[...] * keep, 1, row)
        rg = rg_ref[...].astype(F32)
        dz = dz_ref[...]
        sg = _sigmoid(rg)
        drg_ref[...] = (dz * h * (sg * (1.0 + rg * (1.0 - sg)))).astype(BF16)
        dy = dz * (rg * sg)
        b = jnp.where(row >= tc - 1, a_carry[pl.ds(0, 1), :], pltpu.roll(a, tc - 1, 0))
        b_cum, l0 = _scan_up(b, dy, row)
        lt = l0 + b_cum * lam_carry[pl.ds(0, 1), :]
        lam_carry[...] = lt[0:8, :]
        a_carry[...] = a[0:8, :]
        ic = i * c
        dmult = lt * ic
        di = lt * mult * c
        dc = lt * mult * i
        dlog_a = a * (lt * h_prev - dmult * a / mult)
        sp = _softplus(-lam)
        dpre_r = dlog_a * ((-LRU_C) * sp) * (r * (1.0 - r))
        dpre_i = di * (i * (1.0 - i))
        dlam_row = jnp.sum(dlog_a * r, axis=0, keepdims=True) * (LRU_C * _sigmoid(-lam))
        dpr_b = dpre_r.astype(BF16)
        dpi_b = dpre_i.astype(BF16)
        dwa_ref[...] += _dot_tn(cb, dpr_b)
        dwx_ref[...] += _dot_tn(cb, dpi_b)
        dc = dc + _dot_nt(dpr_b, wa_ref[...]) + _dot_nt(dpi_b, wx_ref[...])
        head = dc_head[...]
        dx = cw_ref[pl.ds(3, 1), :] * dc
        for m in range(1, CONV_W):
            dx = dx + cw_ref[pl.ds(3 - m, 1), :] * _shift_up(dc, head, m, row)
        dx_ref[...] = dx.astype(BF16)
        dc_head[...] = dc[0:8, :]
        sm_ref[pl.ds(0, 1), :] += jnp.sum(dpre_r, axis=0, keepdims=True)
        sm_ref[pl.ds(1, 1), :] += jnp.sum(dpre_i, axis=0, keepdims=True)
        sm_ref[pl.ds(2, 1), :] += dlam_row
        sm_ref[pl.ds(3, 1), :] += jnp.sum(dc, axis=0, keepdims=True)
        for k in range(CONV_W):
            sm_ref[pl.ds(4 + k, 1), :] += jnp.sum(dc * taps[k], axis=0, keepdims=True)

    rev = lambda off: (lambda j, t: (nt - 1 - t, off + j))
    halo = lambda off: (lambda j, t: (jnp.maximum((nt - 1 - t) * hb - 1, 0), off + j))
    halo16 = lambda off: (lambda j, t: (jnp.maximum((nt - 1 - t) * (hb // 2) - 1, 0), off + j))
    vec = pl.BlockSpec((1, ct), lambda j, t: (0, j))
    mat = pl.BlockSpec((None, ct, ct), lambda j, t: (j, 0, 0))
    return pl.pallas_call(
        body,
        name="rnn_bwd",
        grid=(D_RNN // ct, nt),
        in_specs=[
            pl.BlockSpec((tc, ct), rev(COL_RNN_X)),
            pl.BlockSpec((16, ct), halo16(COL_RNN_X)),
            pl.BlockSpec((tc, ct), rev(COL_RNN_GATE)),
            pl.BlockSpec((tc, ct), rev(0)),
            pl.BlockSpec((8, ct), halo(0)),
            pl.BlockSpec((tc, ct), rev(0)),
            pl.BlockSpec((CONV_W, ct), lambda j, t: (0, j)),
            vec, mat, mat, vec, vec, vec,
        ],
        out_specs=[
            pl.BlockSpec((tc, ct), rev(0)),
            pl.BlockSpec((tc, ct), rev(0)),
            mat, mat,
            pl.BlockSpec((8, ct), lambda j, t: (0, j)),
        ],
        out_shape=[_sds((T, D_RNN), BF16), _sds((T, D_RNN), BF16), _sds((D_RNN // ct, ct, ct), F32),
                   _sds((D_RNN // ct, ct, ct), F32), _sds((8, D_RNN), F32)],
        scratch_shapes=[pltpu.VMEM((8, ct), F32), pltpu.VMEM((8, ct), F32), pltpu.VMEM((8, ct), F32)],
        compiler_params=_params(("parallel", "arbitrary"), 32),
    )(*_hbm(proj, proj, proj, y_rnn, y_rnn, dz_rnn, conv_w, conv_b, wa_bd, wx_bd, b_a, b_x, lam))


def _attn_bias():
    qi = jnp.arange(BLOCK)[:, None]
    kj = jnp.arange(BLOCK)[None, :]
    dist_cur = (qi - kj).astype(F32)
    slopes = 2.0 ** (-ALIBI_MAX_BIAS * jnp.arange(1, N_Q_HEADS + 1, dtype=F32) / N_Q_HEADS)
    slopes = slopes[:, None, None]
    prev = jnp.where(kj > qi, -slopes * (dist_cur + float(BLOCK)), NEG_BIG)
    cur = jnp.where(kj <= qi, -slopes * dist_cur, NEG_BIG)
    later = jnp.concatenate([prev, cur], axis=-1)
    first = jnp.concatenate([jnp.full_like(prev, NEG_BIG), cur], axis=-1)
    return jnp.stack([first, later])


def _attn_exps(s_prev, s_cur, sink, bias):
    s_prev = s_prev + bias[:, 0:BLOCK]
    s_cur = s_cur + bias[:, BLOCK:2 * BLOCK]
    m = jnp.maximum(jnp.max(jnp.maximum(s_prev, s_cur), axis=-1, keepdims=True), sink)
    p_prev = jnp.exp(s_prev - m)
    p_cur = jnp.exp(s_cur - m)
    total = jnp.sum(p_prev + p_cur, axis=-1, keepdims=True) + jnp.exp(sink - m)
    return p_prev, p_cur, 1.0 / total, m + jnp.log(total)


def _attn_probs(s_prev, s_cur, sink, bias, lse):
    p_prev = jnp.exp((s_prev + bias[:, 0:BLOCK]) - lse)
    p_cur = jnp.exp((s_cur + bias[:, BLOCK:2 * BLOCK]) - lse)
    return p_prev, p_cur, jnp.exp(sink - lse)


def _stack_heads(ref_or_val, hk, dtype):
    parts = [ref_or_val[:, (GROUP * hk + g) * HEAD_DIM:(GROUP * hk + g + 1) * HEAD_DIM] for g in range(GROUP)]
    return jnp.concatenate(parts, axis=0).astype(dtype)


ATTN_SCALE = HEAD_DIM ** -0.5


def _bias_spec():
    return pl.BlockSpec((None, N_Q_HEADS, BLOCK, 2 * BLOCK), lambda i: (jnp.minimum(i, 1), 0, 0, 0))


def _attn_fwd(proj, sinks, bias):
    T = proj.shape[0]
    nb = T // BLOCK

    def body(sink_ref, bias_ref, q_ref, kp_ref, kc_ref, vp_ref, vc_ref, ag0_ref, ag1_ref, y_ref, z_ref, lse_ref):
        kvs = [slice(hk * HEAD_DIM, (hk + 1) * HEAD_DIM) for hk in range(N_KV_HEADS)]
        qgs = [(_stack_heads(q_ref, hk, F32) * ATTN_SCALE).astype(BF16) for hk in range(N_KV_HEADS)]
        s_prev = [_dot_nt(qgs[hk], kp_ref[:, kvs[hk]].astype(BF16)) for hk in range(N_KV_HEADS)]
        s_cur = [_dot_nt(qgs[hk], kc_ref[:, kvs[hk]].astype(BF16)) for hk in range(N_KV_HEADS)]
        for hk in range(N_KV_HEADS):
            pp, pc, invs = [], [], []
            for g in range(GROUP):
                h = GROUP * hk + g
                rows = slice(g * BLOCK, (g + 1) * BLOCK)
                p_prev, p_cur, inv, lse = _attn_exps(s_prev[hk][rows], s_cur[hk][rows], sink_ref[h], bias_ref[h])
                pp.append(p_prev.astype(BF16))
                pc.append(p_cur.astype(BF16))
                invs.append(inv)
                lse_ref[:, h:h + 1] = lse
            og = _dot(jnp.concatenate(pp, axis=0), vp_ref[:, kvs[hk]].astype(BF16)) + _dot(
                jnp.concatenate(pc, axis=0), vc_ref[:, kvs[hk]].astype(BF16))
            for g in range(GROUP):
                h = GROUP * hk + g
                y_ref[:, h * HEAD_DIM:(h + 1) * HEAD_DIM] = og[g * BLOCK:(g + 1) * BLOCK] * invs[g]
        ag = jnp.concatenate([ag0_ref[...], ag1_ref[...]], axis=1).astype(F32)
        z_ref[...] = (y_ref[...] * (ag * _sigmoid(ag))).astype(BF16)

    prev = lambda c: (lambda i: (jnp.maximum(i - 1, 0), c))
    cur = lambda c: (lambda i: (i, c))
    return pl.pallas_call(
        body,
        name="attn_fwd",
        grid=(nb,),
        in_specs=[
            pl.BlockSpec(memory_space=pltpu.SMEM),
            _bias_spec(),
            pl.BlockSpec((BLOCK, 1024), lambda i: (i, COL_Q // 4)),
            pl.BlockSpec((BLOCK, D_KV), prev(COL_K)),
            pl.BlockSpec((BLOCK, D_KV), cur(COL_K)),
            pl.BlockSpec((BLOCK, D_KV), prev(COL_V)),
            pl.BlockSpec((BLOCK, D_KV), cur(COL_V)),
            pl.BlockSpec((BLOCK, 512), lambda i: (i, COL_ATTN_GATE // 2)),
            pl.BlockSpec((BLOCK, 512), lambda i: (i, COL_ATTN_GATE // 2 + 1)),
        ],
        out_specs=[pl.BlockSpec((BLOCK, 1024), lambda i: (i, 0)), pl.BlockSpec((BLOCK, 1024), lambda i: (i, 0)),
                   pl.BlockSpec((BLOCK, N_Q_HEADS), lambda i: (i, 0))],
        out_shape=[_sds((T, 1024), F32), _sds((T, 1024), BF16), _sds((T, N_Q_HEADS), F32)],
        compiler_params=_params(("arbitrary",), 32),
    )(sinks, *_hbm(bias, proj, proj, proj, proj, proj, proj, proj))


def _attn_bwd(proj, y_attn, lse, dz_attn, sinks, bias, token):
    T = proj.shape[0]
    nb = T // BLOCK

    def body(sink_ref, bias_ref, q_ref, kp_ref, kc_ref, vp_ref, vc_ref, ag0_ref, ag1_ref, y_ref, lse_ref, dz_ref,
             token_ref, dq_ref, dk_ref, dv_ref, dag_ref, ds_ref, dy_s):
        i = pl.program_id(0)

        @pl.when(i == 0)
        def _():
            ds_ref[...] = jnp.zeros_like(ds_ref)

        lane = lax.broadcasted_iota(jnp.int32, (8, 128), 1)
        sub = lax.broadcasted_iota(jnp.int32, (8, 128), 0)
        ag = jnp.concatenate([ag0_ref[...], ag1_ref[...]], axis=1).astype(F32)
        dz = dz_ref[...]
        sg = _sigmoid(ag)
        dag_ref[...] = (dz * y_ref[...] * (sg * (1.0 + ag * (1.0 - sg)))).astype(BF16)
        dy_s[...] = dz * (ag * sg)
        r_cur = pl.multiple_of(i * BLOCK, BLOCK)
        r_prev = pl.multiple_of(jnp.maximum(i - 1, 0) * BLOCK, BLOCK)
        dk_cur, dv_cur, dk_prev, dv_prev = [], [], [], []
        ds_acc = jnp.zeros((8, 128), F32)
        for hk in range(N_KV_HEADS):
            ks = slice(hk * HEAD_DIM, (hk + 1) * HEAD_DIM)
            qg = (_stack_heads(q_ref, hk, F32) * ATTN_SCALE).astype(BF16)
            dog = _stack_heads(dy_s, hk, F32)
            og = _stack_heads(y_ref, hk, F32)
            dog_b = dog.astype(BF16)
            kp = kp_ref[:, ks].astype(BF16)
            kc = kc_ref[:, ks].astype(BF16)
            vp = vp_ref[:, ks].astype(BF16)
            vc = vc_ref[:, ks].astype(BF16)
            s_prev = _dot_nt(qg, kp)
            s_cur = _dot_nt(qg, kc)
            dp_prev = _dot_nt(dog_b, vp)
            dp_cur = _dot_nt(dog_b, vc)
            dvec = jnp.sum(dog * og, axis=-1, keepdims=True)
            pp, pc, dsp, dsc = [], [], [], []
            for g in range(GROUP):
                h = GROUP * hk + g
                rows = slice(g * BLOCK, (g + 1) * BLOCK)
                p_prev, p_cur, p_sink = _attn_probs(
                    s_prev[rows], s_cur[rows], sink_ref[h], bias_ref[h], lse_ref[:, h:h + 1])
                d_h = dvec[rows]
                pp.append(p_prev.astype(BF16))
                pc.append(p_cur.astype(BF16))
                dsp.append((p_prev * (dp_prev[rows] - d_h)).astype(BF16))
                dsc.append((p_cur * (dp_cur[rows] - d_h)).astype(BF16))
                dsink = -jnp.sum(p_sink * d_h, axis=0, keepdims=True)
                ds_acc = ds_acc + jnp.where(jnp.logical_and(lane == h, sub == 1), dsink, 0.0)
            pp = jnp.concatenate(pp, axis=0)
            pc = jnp.concatenate(pc, axis=0)
            dsp = jnp.concatenate(dsp, axis=0)
            dsc = jnp.concatenate(dsc, axis=0)
            dqg = (_dot(dsp, kp) + _dot(dsc, kc)) * ATTN_SCALE
            for g in range(GROUP):
                h = GROUP * hk + g
                dq_ref[:, h * HEAD_DIM:(h + 1) * HEAD_DIM] = dqg[g * BLOCK:(g + 1) * BLOCK].astype(BF16)
            dk_ref[pl.ds(r_cur, BLOCK), ks] = _dot_tn(dsc, qg)
            dv_ref[pl.ds(r_cur, BLOCK), ks] = _dot_tn(pc, dog_b)
            dk_prev.append(_dot_tn(dsp, qg))
            dv_prev.append(_dot_tn(pp, dog_b))
        ds_ref[:, 0:128] += ds_acc

        @pl.when(i > 0)
        def _():
            for hk in range(N_KV_HEADS):
                ks = slice(hk * HEAD_DIM, (hk + 1) * HEAD_DIM)
                dk_ref[pl.ds(r_prev, BLOCK), ks] += dk_prev[hk]
                dv_ref[pl.ds(r_prev, BLOCK), ks] += dv_prev[hk]

    prev = lambda c: (lambda i: (jnp.maximum(i - 1, 0), c))
    cur = lambda c: (lambda i: (i, c))
    blk = pl.BlockSpec((BLOCK, 1024), lambda i: (i, 0))
    whole = pl.BlockSpec((T, D_KV), lambda i: (0, 0))
    return pl.pallas_call(
        body,
        name="attn_bwd",
        grid=(nb,),
        in_specs=[
            pl.BlockSpec(memory_space=pltpu.SMEM),
            _bias_spec(),
            pl.BlockSpec((BLOCK, 1024), lambda i: (i, COL_Q // 4)),
            pl.BlockSpec((BLOCK, D_KV), prev(COL_K)),
            pl.BlockSpec((BLOCK, D_KV), cur(COL_K)),
            pl.BlockSpec((BLOCK, D_KV), prev(COL_V)),
            pl.BlockSpec((BLOCK, D_KV), cur(COL_V)),
            pl.BlockSpec((BLOCK, 512), lambda i: (i, COL_ATTN_GATE // 2)),
            pl.BlockSpec((BLOCK, 512), lambda i: (i, COL_ATTN_GATE // 2 + 1)),
            blk,
            pl.BlockSpec((BLOCK, N_Q_HEADS), lambda i: (i, 0)),
            blk,
            pl.BlockSpec((8, 128), lambda i: (0, 0)),
        ],
        out_specs=[blk, whole, whole, blk, pl.BlockSpec((8, 1024), lambda i: (0, 0))],
        out_shape=[_sds((T, 1024), BF16), _sds((T, D_KV), F32), _sds((T, D_KV), F32), _sds((T, 1024), BF16),
                   _sds((8, 1024), F32)],
        scratch_shapes=[pltpu.VMEM((BLOCK, 1024), F32)],
        compiler_params=_params(("arbitrary",), 48),
    )(sinks, *_hbm(bias, proj, proj, proj, proj, proj, proj, proj, y_attn, lse, dz_attn, token))


def _head(x, target, z_rnn, z_attn, proj, b_gate, g_post, w_rnn_out, w_attn_out, w_out):
    T = x.shape[0]
    tm = 256

    def body(x_ref, t_ref, zr_ref, za_ref, ml0_ref, ml1_ref, ml2_ref, ml3_ref, bg_ref, gp_ref, wr_ref, wa_ref, wo_ref,
             dyx_ref, dzr_ref, dza_ref, dml_ref, mb_ref, dout_ref, dbr_ref, dba_ref, sm_ref):
        @pl.when(pl.program_id(0) == 0)
        def _():
            sm_ref[...] = jnp.zeros_like(sm_ref)

        wr, wa, wo = wr_ref[...], wa_ref[...], wo_ref[...]
        br_rnn = _dot(zr_ref[...], wr)
        br_attn = _dot(za_ref[...], wa)
        ml_rnn = jnp.concatenate([ml0_ref[...], ml1_ref[...]], axis=1).astype(F32)
        ml_attn = jnp.concatenate([ml2_ref[...], ml3_ref[...]], axis=1).astype(F32)
        g_rnn = _sigmoid(ml_rnn + bg_ref[:, 0:D_MODEL])
        g_attn = _sigmoid(ml_attn + bg_ref[:, D_MODEL:2 * D_MODEL])
        mb = (g_rnn * br_rnn + g_attn * br_attn).astype(BF16)
        mb_ref[...] = mb
        out = _dot(mb, wo)
        rstd = lax.rsqrt(jnp.mean(out * out, axis=-1, keepdims=True) + EPS)
        n = out * rstd
        gp = gp_ref[...]
        err = (x_ref[...] + n * gp) - t_ref[...]
        sm_ref[pl.ds(3, 1), :] += 0.5 * jnp.sum(jnp.mean(err * err, axis=-1, keepdims=True), axis=0, keepdims=True)
        dy = err * (1.0 / D_MODEL)
        dyx_ref[...] = dy
        sm_ref[pl.ds(0, 1), :] += jnp.sum(dy * n, axis=0, keepdims=True)
        dn = dy * gp
        dout = (rstd * (dn - n * jnp.mean(dn * n, axis=-1, keepdims=True))).astype(BF16)
        dout_ref[...] = dout
        dmerged = _dot_nt(dout, wo)
        dml_r = (dmerged * br_rnn) * (g_rnn * (1.0 - g_rnn))
        dml_a = (dmerged * br_attn) * (g_attn * (1.0 - g_attn))
        dml_ref[:, 0:D_MODEL] = dml_r.astype(BF16)
        dml_ref[:, D_MODEL:2 * D_MODEL] = dml_a.astype(BF16)
        sm_ref[pl.ds(1, 1), :] += jnp.sum(dml_r, axis=0, keepdims=True)
        sm_ref[pl.ds(2, 1), :] += jnp.sum(dml_a, axis=0, keepdims=True)
        dbr = (dmerged * g_rnn).astype(BF16)
        dba = (dmerged * g_attn).astype(BF16)
        dbr_ref[...] = dbr
        dba_ref[...] = dba
        dzr_ref[...] = _dot_nt(dbr, wr)
        dza_ref[...] = _dot_nt(dba, wa)

    tile = pl.BlockSpec((tm, D_MODEL), lambda i: (i, 0))
    wspec = pl.BlockSpec((D_MODEL, D_MODEL), lambda i: (0, 0))
    ml = lambda q: pl.BlockSpec((tm, 512), lambda i: (i, COL_MERGE // 2 + q))
    return pl.pallas_call(
        body,
        name="head",
        grid=(T // tm,),
        in_specs=[
            tile, tile, tile, tile,
            ml(0), ml(1), ml(2), ml(3),
            pl.BlockSpec((1, 2 * D_MODEL), lambda i: (0, 0)),
            pl.BlockSpec((1, D_MODEL), lambda i: (0, 0)),
            wspec, wspec, wspec,
        ],
        out_specs=[
            tile, tile, tile,
            pl.BlockSpec((tm, 2 * D_MODEL), lambda i: (i, 0)),
            tile, tile, tile, tile,
            pl.BlockSpec((8, D_MODEL), lambda i: (0, 0)),
        ],
        out_shape=[
            _sds((T, D_MODEL), F32), _sds((T, D_MODEL), F32), _sds((T, D_MODEL), F32),
            _sds((T, 2 * D_MODEL), BF16),
            _sds((T, D_MODEL), BF16), _sds((T, D_MODEL), BF16), _sds((T, D_MODEL), BF16), _sds((T, D_MODEL), BF16),
            _sds((8, D_MODEL), F32),
        ],
        compiler_params=_params(("arbitrary",), 56),
    )(*_hbm(x, target, z_rnn, z_attn, proj, proj, proj, proj, b_gate, g_post, w_rnn_out, w_attn_out, w_out))


def _matmul_tn(a, b, name):
    T, M = a.shape
    N = b.shape[1]
    tk = min(512, T)

    def body(a_ref, b_ref, o_ref):
        @pl.when(pl.program_id(0) == 0)
        def _():
            o_ref[...] = jnp.zeros_like(o_ref)

        o_ref[...] += _dot_tn(a_ref[...], b_ref[...])

    return pl.pallas_call(
        body,
        name=name,
        grid=(T // tk,),
        in_specs=[pl.BlockSpec((tk, M), lambda t: (t, 0)), pl.BlockSpec((tk, N), lambda t: (t, 0))],
        out_specs=pl.BlockSpec((M, N), lambda t: (0, 0)),
        out_shape=_sds((M, N), F32),
        compiler_params=_params(("arbitrary",), 48),
    )(*_hbm(a, b))


DPROJ_WIDTHS = (D_RNN, D_RNN, 1024, D_KV, D_KV, 1024, 2 * D_MODEL)


def _dproj_segments():
    segs, start = [[] for _ in range(N_CHIPS)], 0
    for p, width in enumerate(DPROJ_WIDTHS):
        for c in range(N_CHIPS):
            lo, hi = max(start, c * W_IN_SHARD), min(start + width, (c + 1) * W_IN_SHARD)
            if lo < hi:
                segs[c].append((p, lo - start, hi - start, lo - c * W_IN_SHARD, hi - c * W_IN_SHARD))
        start += width
    return segs


def _dh_bwd(pieces, w_in_g, x, dyx, g_pre, token):
    T = x.shape[0]
    tm = min(512, T)
    n = len(pieces)
    segs = _dproj_segments()

    def body(*refs):
        p_refs, w_hbm, x_ref, dyx_ref, g_ref = refs[0:n], refs[n], refs[n + 1], refs[n + 2], refs[n + 3]
        gx_ref, dg_ref, w_ref = refs[n + 5], refs[n + 6], refs[n + 7]

        @pl.when(pl.program_id(0) == 0)
        def _():
            pltpu.sync_copy(w_hbm, w_ref)
            dg_ref[...] = jnp.zeros_like(dg_ref)

        dh = None
        for c in range(N_CHIPS):
            for p, a0, a1, u0, u1 in segs[c]:
                part = _dot_nt(p_refs[p][:, a0:a1].astype(BF16), w_ref[c, :, u0:u1])
                dh = part if dh is None else dh + part
        xv = x_ref[...]
        rstd = lax.rsqrt(jnp.mean(xv * xv, axis=-1, keepdims=True) + EPS)
        nx = xv * rstd
        dhg = dh * g_ref[...]
        gx_ref[...] = dyx_ref[...] + rstd * (dhg - nx * jnp.mean(dhg * nx, axis=-1, keepdims=True))
        dg_ref[pl.ds(0, 1), :] += jnp.sum(dh * nx, axis=0, keepdims=True)

    tile = pl.BlockSpec((tm, D_MODEL), lambda i: (i, 0))
    return pl.pallas_call(
        body,
        name="dh_bwd",
        grid=(T // tm,),
        in_specs=[pl.BlockSpec((tm, w), lambda i: (i, 0)) for w in DPROJ_WIDTHS] + [
            ANY, tile, tile,
            pl.BlockSpec((1, D_MODEL), lambda i: (0, 0)),
            pl.BlockSpec((8, 128), lambda i: (0, 0)),
        ],
        out_specs=[tile, pl.BlockSpec((8, D_MODEL), lambda i: (0, 0))],
        out_shape=[_sds((T, D_MODEL), F32), _sds((8, D_MODEL), F32)],
        scratch_shapes=[pltpu.VMEM(w_in_g.shape, BF16)],
        compiler_params=_params(("arbitrary",), 56),
    )(*_hbm(*pieces, w_in_g, x, dyx, g_pre, token))


def _dw_in(ht, pieces):
    T = ht.shape[1]
    tk = min(512, T)
    n = len(pieces)
    segs = _dproj_segments()

    def body(*refs):
        h_ref, p_refs, o_ref = refs[0], refs[1:n + 1], refs[n + 1]

        @pl.when(pl.program_id(1) == 0)
        def _():
            o_ref[...] = jnp.zeros_like(o_ref)

        for c in range(N_CHIPS):
            @pl.when(pl.program_id(0) == c)
            def _():
                cols = jnp.concatenate([p_refs[p][:, a0:a1].astype(BF16) for p, a0, a1, _, _ in segs[c]], axis=1)
                o_ref[...] += _dot(h_ref[...], cols)

    def piece_spec(p):
        chips = [c for c in range(N_CHIPS) if any(s[0] == p for s in segs[c])]

        def index(c, t):
            used = functools.reduce(jnp.logical_or, [c == k for k in chips])
            return (jnp.where(used, t, 0), 0)

        return pl.BlockSpec((tk, DPROJ_WIDTHS[p]), index)

    return pl.pallas_call(
        body,
        name="dw_in",
        grid=(N_CHIPS, T // tk),
        in_specs=[pl.BlockSpec((D_MODEL, tk), lambda c, t: (0, t))] + [piece_spec(p) for p in range(n)],
        out_specs=pl.BlockSpec((None, D_MODEL, W_IN_SHARD), lambda c, t: (c, 0, 0)),
        out_shape=_sds((N_CHIPS, D_MODEL, W_IN_SHARD), F32),
        compiler_params=_params(("parallel", "arbitrary"), 56),
    )(*_hbm(ht, *pieces))


ELEMENTWISE_TILE_BYTES = MIB


def _row_tile(rows, cols):
    if rows * cols * 4 <= ELEMENTWISE_TILE_BYTES:
        return rows
    for t in (512, 256, 128, 64, 32, 16, 8):
        if rows % t == 0 and t * cols * 4 <= ELEMENTWISE_TILE_BYTES:
            return t
    return rows


def _pair_sum(g, got, core, name):
    nch, R, C = g.shape
    h = R // 2
    tr = _row_tile(h, C)
    nt = h // tr

    def body(c_ref, g_ref, got_ref, p_ref, pb_ref):
        s = g_ref[...] + got_ref[...]
        p_ref[...] = s
        pb_ref[...] = s.astype(BF16)

    blk = pl.BlockSpec((None, tr, C), lambda j, i, c_ref: (j, i, 0))
    return pl.pallas_call(
        body,
        name=name,
        grid_spec=pltpu.PrefetchScalarGridSpec(
            num_scalar_prefetch=1,
            grid=(nch, nt),
            in_specs=[pl.BlockSpec((None, tr, C), lambda j, i, c_ref: (j, c_ref[0] * nt + i, 0)), blk],
            out_specs=[blk, blk],
        ),
        out_shape=[_sds((nch, h, C), F32), _sds((nch, h, C), BF16)],
        compiler_params=_params(("parallel", "parallel"), 48),
    )(core, *_hbm(g, got))


def _chip_sum(p, got, chip_core, name):
    _, h, C = p.shape
    tr = _row_tile(h, C)
    nt = h // tr

    def body(jc_ref, p_ref, g0_ref, g1_ref, g2_ref, o_ref):
        o_ref[...] = ((p_ref[...] + g0_ref[...].astype(F32)) + g1_ref[...].astype(F32)) + g2_ref[...].astype(F32)

    rel = lambda r: pl.BlockSpec((None, tr, C), lambda i, jc_ref: (r, i, 0))
    return pl.pallas_call(
        body,
        name=name,
        grid_spec=pltpu.PrefetchScalarGridSpec(
            num_scalar_prefetch=1,
            grid=(nt,),
            in_specs=[pl.BlockSpec((None, tr, C), lambda i, jc_ref: (jc_ref[0], i, 0)), rel(0), rel(1), rel(2)],
            out_specs=pl.BlockSpec((tr, C), lambda i, jc_ref: (jc_ref[1] * nt + i, 0)),
        ),
        out_shape=_sds((2 * h, C), F32),
        compiler_params=_params(("parallel",), 48),
    )(chip_core, *_hbm(p, got, got, got))


def _place_shards(shards, chip, name):
    n = len(shards)
    tiles = [_row_tile(s.shape[0], s.shape[1]) for s in shards]
    steps = max(s.shape[0] // t for s, t in zip(shards, tiles))
    tiles = [s.shape[0] // steps for s in shards]

    def body(j_ref, *refs):
        for a in range(n):
            refs[n + a][...] = refs[a][...].astype(BF16)

    return pl.pallas_call(
        body,
        name=name,
        grid_spec=pltpu.PrefetchScalarGridSpec(
            num_scalar_prefetch=1,
            grid=(steps,),
            in_specs=[pl.BlockSpec((t, s.shape[1]), lambda i, j_ref: (i, 0)) for s, t in zip(shards, tiles)],
            out_specs=[pl.BlockSpec((None, t, s.shape[1]), lambda i, j_ref: (j_ref[0], i, 0))
                       for s, t in zip(shards, tiles)],
        ),
        out_shape=[_sds((N_CHIPS,) + s.shape, BF16) for s in shards],
        compiler_params=_params(("parallel",), 48),
    )(chip, *_hbm(*shards))


def _adamw(w, g, m, v, name):
    R, C = w.shape
    tr = _row_tile(R, C)
    c1 = 1.0 - ADAM_B1 ** ADAM_STEP
    c2 = 1.0 - ADAM_B2 ** ADAM_STEP

    def body(w_ref, g_ref, m_ref, v_ref, d_ref, nm_ref, nv_ref):
        g = g_ref[...]
        nm = ADAM_B1 * m_ref[...] + (1.0 - ADAM_B1) * g
        nv = ADAM_B2 * v_ref[...] + (1.0 - ADAM_B2) * (g * g)
        nm_ref[...] = nm
        nv_ref[...] = nv
        d_ref[...] = (-ADAM_LR) * ((nm / c1) / (jnp.sqrt(nv / c2) + ADAM_EPS) + ADAM_WD * w_ref[...])

    spec = pl.BlockSpec((tr, C), lambda i: (i, 0))
    return pl.pallas_call(
        body, name=name, grid=(R // tr,), in_specs=[spec] * 4, out_specs=[spec] * 3,
        out_shape=[_sds((R, C), F32)] * 3, compiler_params=_params(("parallel",), 48),
    )(*_hbm(w, g, m, v))


def _place():
    return lax.axis_index("x"), lax.axis_index("y"), lax.axis_index("c")


def _chip_of(x, y, r):
    return (x ^ (r >> 1), y ^ (r & 1))


ANY = pl.BlockSpec(memory_space=pl.ANY)


def _gather_weights(placed, cw8):
    nbig = len(placed)
    halves = [s.shape[1] // 2 for s in placed]
    pieces = [max(1, h // 128) for h in halves]
    rows = [h // p for h, p in zip(halves, pieces)]
    order = [(a, q) for q in range(max(pieces)) for a in range(nbig) if q < pieces[a]]
    ici_sem = {(a, q, r): 3 * i + (r - 1) for i, (a, q) in enumerate(order) for r in (1, 2, 3)}
    cw_sem = {r: 3 * len(order) + (r - 1) for r in (1, 2, 3)}
    d2d_sem = {key: 3 * len(order) + 3 + k for key, k in ici_sem.items()}
    nsem = 6 * len(order) + 3

    def body(*refs):
        cw_ref, dsts, gcw_ref = refs[nbig], refs[nbig + 1:2 * nbig + 1], refs[2 * nbig + 1]
        send_sems, recv_sems = refs[2 * nbig + 2:]
        x, y, c = _place()
        j = 2 * x + y

        def piece_rows(a, q, core):
            return pl.ds(pl.multiple_of(core * halves[a] + q * rows[a], 16), rows[a])

        def ici(a, q, r):
            tx, ty = _chip_of(x, y, r)
            k = ici_sem[(a, q, r)]
            region = dsts[a].at[j, piece_rows(a, q, c), :]
            return pltpu.make_async_remote_copy(
                src_ref=region, dst_ref=region, send_sem=send_sems.at[k], recv_sem=recv_sems.at[k],
                device_id=(tx, ty, c), device_id_type=MESH)

        def ici_landed(a, q, r):
            tx, ty = _chip_of(x, y, r)
            k = ici_sem[(a, q, r)]
            region = dsts[a].at[2 * tx + ty, piece_rows(a, q, c), :]
            return pltpu.make_async_remote_copy(
                src_ref=region, dst_ref=region, send_sem=send_sems.at[k], recv_sem=recv_sems.at[k],
                device_id=(tx, ty, c), device_id_type=MESH)

        def d2d(a, q, r, core):
            tx, ty = _chip_of(x, y, r)
            k = d2d_sem[(a, q, r)]
            region = dsts[a].at[2 * tx + ty, piece_rows(a, q, core), :]
            return pltpu.make_async_remote_copy(
                src_ref=region, dst_ref=region, send_sem=send_sems.at[k], recv_sem=recv_sems.at[k],
                device_id=(x, y, 1 - c), device_id_type=MESH)

        def cw_copy(r):
            tx, ty = _chip_of(x, y, r)
            k = cw_sem[r]
            return pltpu.make_async_remote_copy(
                src_ref=cw_ref, dst_ref=gcw_ref.at[j], send_sem=send_sems.at[k], recv_sem=recv_sems.at[k],
                device_id=(tx, ty, c), device_id_type=MESH)

        def cw_landed(r):
            tx, ty = _chip_of(x, y, r)
            k = cw_sem[r]
            region = gcw_ref.at[2 * tx + ty]
            return pltpu.make_async_remote_copy(
                src_ref=region, dst_ref=region, send_sem=send_sems.at[k], recv_sem=recv_sems.at[k],
                device_id=(tx, ty, c), device_id_type=MESH)

        first = [ici(a, q, r) for (a, q) in order for r in (1, 2, 3)] + [cw_copy(r) for r in (1, 2, 3)]
        for cp in first:
            cp.start()
        passed = []
        for (a, q) in order:
            for r in (1, 2, 3):
                ici_landed(a, q, r).wait_recv()
                cp = d2d(a, q, r, c)
                cp.start()
                passed.append(cp)
        for r in (1, 2, 3):
            cw_landed(r).wait_recv()
        for (a, q) in order:
            for r in (1, 2, 3):
                d2d(a, q, r, 1 - c).wait_recv()
        for cp in first + passed:
            cp.wait_send()

    return pl.pallas_call(
        body,
        name="gather_weights",
        in_specs=[ANY] * (nbig + 1),
        out_specs=[ANY] * (nbig + 1),
        out_shape=[_sds(s.shape, s.dtype) for s in placed] + [_sds((N_CHIPS,) + cw8.shape, cw8.dtype)],
        input_output_aliases={a: a for a in range(nbig)},
        scratch_shapes=[pltpu.SemaphoreType.DMA((nsem,)), pltpu.SemaphoreType.DMA((nsem,))],
    )(*placed, cw8)


def _gather_late_start(placed, after, name):
    n = len(placed)
    halves = [s.shape[1] // 2 for s in placed]

    def body(*refs):
        g_refs = refs[0:n]
        send_sems, recv_sems, token = refs[n + 1], refs[n + 2], refs[-1]
        x, y, c = _place()
        j = 2 * x + y
        for a in range(n):
            mine = g_refs[a].at[j, pl.ds(pl.multiple_of(c * halves[a], 16), halves[a]), :]
            for r in (1, 2, 3):
                tx, ty = _chip_of(x, y, r)
                for to_core in (0, 1):
                    k = ((a * 3 + (r - 1)) * 2 + c) * 2 + to_core
                    pltpu.make_async_remote_copy(
                        src_ref=mine, dst_ref=mine, send_sem=send_sems.at[k], recv_sem=recv_sems.at[k],
                        device_id=(tx, ty, to_core), device_id_type=MESH).start()
        token[...] = jnp.zeros_like(token)

    hbm = lambda t: pltpu.HBM(t.shape, t.dtype)
    keep = lambda t: pltpu.with_memory_space_constraint(t, pltpu.HBM)
    nsem = 12 * n
    outs = pl.pallas_call(
        body,
        name=name,
        in_specs=[HBM] * n + [ANY],
        out_specs=(SEM, SEM, *[HBM] * n, pl.BlockSpec(memory_space=pltpu.VMEM)),
        out_shape=(pltpu.SemaphoreType.DMA((nsem,)), pltpu.SemaphoreType.DMA((nsem,)), *[hbm(p) for p in placed],
                   jax.ShapeDtypeStruct((8, 128), F32)),
        input_output_aliases={i: 2 + i for i in range(n)},
        compiler_params=pltpu.CompilerParams(has_side_effects=DATAFLOW),
    )(*[keep(p) for p in placed], after)
    return outs[0], outs[1], list(outs[2:2 + n]), outs[-1]


def _gather_late_wait(send_sems, recv_sems, thru, after, name):
    n = len(thru)
    halves = [s.shape[1] // 2 for s in thru]

    def body(*refs):
        g_refs = refs[0:n]
        send_sems, recv_sems = refs[n], refs[n + 1]
        x, y, c = _place()
        j = 2 * x + y
        for a in range(n):
            mine = g_refs[a].at[j, pl.ds(pl.multiple_of(c * halves[a], 16), halves[a]), :]
            for r in (1, 2, 3):
                tx, ty = _chip_of(x, y, r)
                for other in (0, 1):
                    k_out = ((a * 3 + (r - 1)) * 2 + c) * 2 + other
                    pltpu.make_async_remote_copy(
                        src_ref=mine, dst_ref=mine, send_sem=send_sems.at[k_out], recv_sem=recv_sems.at[k_out],
                        device_id=(tx, ty, other), device_id_type=MESH).wait_send()
                    k_in = ((a * 3 + (r - 1)) * 2 + other) * 2 + c
                    theirs = g_refs[a].at[2 * tx + ty, pl.ds(other * halves[a], halves[a]), :]
                    pltpu.make_async_remote_copy(
                        src_ref=theirs, dst_ref=theirs, send_sem=send_sems.at[k_in], recv_sem=recv_sems.at[k_in],
                        device_id=(tx, ty, other), device_id_type=MESH).wait_recv()

    hbm = lambda t: pltpu.HBM(t.shape, t.dtype)
    outs = pl.pallas_call(
        body,
        name=name,
        in_specs=[HBM] * n + [SEM, SEM, ANY],
        out_specs=[HBM] * n,
        out_shape=[hbm(t) for t in thru],
        input_output_aliases={i: i for i in range(n)},
        compiler_params=pltpu.CompilerParams(has_side_effects=DATAFLOW),
    )(*thru, send_sems, recv_sems, after)
    return list(outs)


D2D_PIECE_ROWS = 64


def _pair_exchange(grads, name):
    n = len(grads)
    halves = [g.shape[1] // 2 for g in grads]

    def body(*refs):
        g_refs, got_refs = refs[0:n], refs[n:2 * n]
        send_sems, recv_sems = refs[2 * n:]
        x, y, c = _place()

        def copy(a, src, dst):
            return pltpu.make_async_remote_copy(
                src_ref=src, dst_ref=dst, send_sem=send_sems.at[a], recv_sem=recv_sems.at[a],
                device_id=(x, y, 1 - c), device_id_type=MESH)

        for a in range(n):
            for jj in range(N_CHIPS):
                for q in range(halves[a] // D2D_PIECE_ROWS):
                    src_rows = pl.ds(pl.multiple_of((1 - c) * halves[a] + q * D2D_PIECE_ROWS, 8), D2D_PIECE_ROWS)
                    dst_rows = pl.ds(q * D2D_PIECE_ROWS, D2D_PIECE_ROWS)
                    copy(a, g_refs[a].at[jj, src_rows, :], got_refs[a].at[jj, dst_rows, :]).start()
        for a in range(n):
            sent = g_refs[a].at[:, pl.ds(pl.multiple_of((1 - c) * halves[a], 8), halves[a]), :]
            copy(a, sent, got_refs[a]).wait()

    return pl.pallas_call(
        body,
        name=name,
        in_specs=[ANY] * n,
        out_specs=[ANY] * n,
        out_shape=[_sds((N_CHIPS, h, g.shape[2]), F32) for g, h in zip(grads, halves)],
        scratch_shapes=[pltpu.SemaphoreType.DMA((n,)), pltpu.SemaphoreType.DMA((n,))],
    )(*grads)


HBM = pl.BlockSpec(memory_space=pltpu.HBM)
SEM = pl.BlockSpec(memory_space=pltpu.SEMAPHORE)
DATAFLOW = pltpu.SideEffectType.DATAFLOW_SIDE_EFFECTING


def _chip_copy(p_refs, land_refs, send_sems, recv_sems, a, r):
    x, y, c = _place()
    tx, ty = _chip_of(x, y, r)
    k = a * 3 + (r - 1)
    return pltpu.make_async_remote_copy(
        src_ref=p_refs[a].at[2 * tx + ty], dst_ref=land_refs[a].at[r - 1],
        send_sem=send_sems.at[k], recv_sem=recv_sems.at[k], device_id=(tx, ty, c), device_id_type=MESH)


def _chip_exchange_start(psums, name):
    n = len(psums)
    lands = [lax.empty((3,) + p.shape[1:], p.dtype) for p in psums]

    def body(*refs):
        p_refs, land_refs = refs[0:n], refs[n:2 * n]
        send_sems, recv_sems, token = refs[2 * n], refs[2 * n + 1], refs[-1]
        for a in range(n):
            for r in (1, 2, 3):
                _chip_copy(p_refs, land_refs, send_sems, recv_sems, a, r).start()
        token[...] = jnp.zeros_like(token)

    hbm = lambda t: pltpu.HBM(t.shape, t.dtype)
    keep = lambda t: pltpu.with_memory_space_constraint(t, pltpu.HBM)
    outs = pl.pallas_call(
        body,
        name=name,
        in_specs=[HBM] * (2 * n),
        out_specs=(SEM, SEM, *[HBM] * (2 * n), pl.BlockSpec(memory_space=pltpu.VMEM)),
        out_shape=(pltpu.SemaphoreType.DMA((3 * n,)), pltpu.SemaphoreType.DMA((3 * n,)),
                   *[hbm(p) for p in psums], *[hbm(l) for l in lands], _sds((8, 128), F32)),
        input_output_aliases={i: 2 + i for i in range(2 * n)},
        compiler_params=pltpu.CompilerParams(has_side_effects=DATAFLOW),
    )(*[keep(p) for p in psums], *[keep(l) for l in lands])
    return outs[0], outs[1], list(outs[2:2 + n]), list(outs[2 + n:2 + 2 * n]), outs[-1]


def _chip_exchange_wait(send_sems, recv_sems, p_thru, land_thru, after, name):
    n = len(p_thru)

    def body(*refs):
        p_refs, land_refs = refs[0:n], refs[n:2 * n]
        send_sems, recv_sems = refs[2 * n], refs[2 * n + 1]
        for a in range(n):
            for r in (1, 2, 3):
                copy = _chip_copy(p_refs, land_refs, send_sems, recv_sems, a, r)
                copy.wait_send()
                copy.wait_recv()

    hbm = lambda t: pltpu.HBM(t.shape, t.dtype)
    outs = pl.pallas_call(
        body,
        name=name,
        in_specs=[HBM] * (2 * n) + [SEM, SEM, ANY],
        out_specs=[HBM] * (2 * n),
        out_shape=[hbm(p) for p in p_thru] + [hbm(l) for l in land_thru],
        input_output_aliases={i: i for i in range(2 * n)},
        compiler_params=pltpu.CompilerParams(has_side_effects=DATAFLOW),
    )(*p_thru, *land_thru, send_sems, recv_sems, after)
    return list(outs[n:2 * n])


def _pair_share(fulls):
    n = len(fulls)
    halves = [f.shape[0] // 2 for f in fulls]

    def body(*refs):
        full_refs = refs[n:2 * n]
        send_sems, recv_sems = refs[2 * n:]
        x, y, c = _place()

        def half_of(a, core):
            return full_refs[a].at[pl.ds(pl.multiple_of(core * halves[a], 8), halves[a]), :]

        def remote(a, src, dst):
            return pltpu.make_async_remote_copy(
                src_ref=src, dst_ref=dst, send_sem=send_sems.at[a], recv_sem=recv_sems.at[a],
                device_id=(x, y, 1 - c), device_id_type=MESH)

        for a in range(n):
            for q in range(halves[a] // D2D_PIECE_ROWS):
                piece = full_refs[a].at[
                    pl.ds(pl.multiple_of(c * halves[a] + q * D2D_PIECE_ROWS, 8), D2D_PIECE_ROWS), :]
                remote(a, piece, piece).start()
        for a in range(n):
            remote(a, half_of(a, c), half_of(a, c)).wait_send()
            remote(a, half_of(a, 1 - c), half_of(a, 1 - c)).wait_recv()

    return pl.pallas_call(
        body,
        name="pair_share",
        in_specs=[ANY] * n,
        out_specs=[ANY] * n,
        out_shape=[_sds(f.shape, F32) for f in fulls],
        input_output_aliases={a: a for a in range(n)},
        scratch_shapes=[pltpu.SemaphoreType.DMA((n,)), pltpu.SemaphoreType.DMA((n,))],
    )(*fulls)


def _allreduce_small(s):
    R, C = s.shape

    def body(s_ref, o_ref, sib, chips, send_sems, recv_sems):
        x, y, c = _place()
        j = 2 * x + y
        def to_sib(src, dst):
            return pltpu.make_async_remote_copy(
                src_ref=src, dst_ref=dst, send_sem=send_sems.at[0], recv_sem=recv_sems.at[0],
                device_id=(x, y, 1 - c), device_id_type=MESH)

        for q in range(R // 8):
            to_sib(s_ref.at[pl.ds(8 * q, 8), :], sib.at[pl.ds(8 * q, 8), :]).start()
        to_sib(s_ref, sib).wait()
        chips[j] = s_ref[...] + sib[...]
        sends = []
        for r in (1, 2, 3):
            tx, ty = _chip_of(x, y, r)
            cp = pltpu.make_async_remote_copy(
                src_ref=chips.at[j], dst_ref=chips.at[j], send_sem=send_sems.at[r], recv_sem=recv_sems.at[r],
                device_id=(tx, ty, c), device_id_type=MESH)
            cp.start()
            sends.append(cp)
        for r in (1, 2, 3):
            tx, ty = _chip_of(x, y, r)
            region = chips.at[2 * tx + ty]
            pltpu.make_async_remote_copy(
                src_ref=region, dst_ref=region, send_sem=send_sems.at[r], recv_sem=recv_sems.at[r],
                device_id=(tx, ty, c), device_id_type=MESH).wait_recv()
        for cp in sends:
            cp.wait_send()
        o_ref[...] = (chips[0] + chips[1]) + (chips[2] + chips[3])

    return pl.pallas_call(
        body,
        name="allreduce_small",
        in_specs=[pl.BlockSpec(memory_space=pltpu.VMEM)],
        out_specs=pl.BlockSpec(memory_space=pltpu.VMEM),
        out_shape=jax.ShapeDtypeStruct((R, C), F32),
        scratch_shapes=[pltpu.VMEM((R, C), F32), pltpu.VMEM((N_CHIPS, R, C), F32),
                        pltpu.SemaphoreType.DMA((4,)), pltpu.SemaphoreType.DMA((4,))],
    )(s)


def _block_diag(w):
    w4 = w.reshape(4, 4, RNN_BLOCK_W, RNN_BLOCK_W)
    eye = jnp.eye(4, dtype=w.dtype)
    return jnp.einsum("jaik,ab->jaibk", w4, eye).reshape(4, RNN_TILE, RNN_TILE)


def _block_diag_part(d):
    d5 = d.reshape(4, 4, RNN_BLOCK_W, 4, RNN_BLOCK_W)
    return jnp.stack([d5[:, a, :, a, :] for a in range(4)], axis=1).reshape(RNN_BLOCKS, RNN_BLOCK_W, RNN_BLOCK_W)


def _local_grads(x, target, g_pre, w_in_g, b_gate, conv_w, conv_b, w_rg_a, b_rg_a, w_rg_x, b_rg_x, lam, sinks,
                 out_weights, fwd_token, g_post, on_out_grads, on_w_in_grad):
    wa_bd = _block_diag(w_rg_a).astype(BF16)
    wx_bd = _block_diag(w_rg_x).astype(BF16)
    b_a = b_rg_a.reshape(1, D_RNN)
    b_x = b_rg_x.reshape(1, D_RNN)

    proj, ht = _proj_fwd(x, g_pre, w_in_g)
    y_rnn, z_rnn = _rnn_fwd(proj, conv_w, conv_b, wa_bd, wx_bd, b_a, b_x, lam, fwd_token)
    bias = _attn_bias()
    y_attn, z_attn, lse = _attn_fwd(proj, sinks, bias)
    w_rnn_out, w_attn_out, w_out = out_weights(z_attn)
    dyx, dz_rnn, dz_attn, dml, merged, dout, dbr_rnn, dbr_attn, head_small = _head(
        x, target, z_rnn, z_attn, proj, b_gate, g_post, w_rnn_out, w_attn_out, w_out)
    dw_out = _matmul_tn(merged, dout, "dw_out")
    dw_rnn_out = _matmul_tn(z_rnn, dbr_rnn, "dw_rnn_out")
    dw_attn_out = _matmul_tn(z_attn, dbr_attn, "dw_attn_out")
    shard_rows = lambda d: d.reshape(N_CHIPS, OUT_SHARD, D_MODEL)
    token = on_out_grads([shard_rows(dw_rnn_out), shard_rows(dw_attn_out), shard_rows(dw_out)])
    dq, dk, dv, dag, attn_small = _attn_bwd(proj, y_attn, lse, dz_attn, sinks, bias, token)
    drx, drg, dwa_t, dwx_t, rnn_small = _rnn_bwd(proj, y_rnn, dz_rnn, conv_w, conv_b, wa_bd, wx_bd, b_a, b_x, lam)
    dproj = [drx, drg, dq, dk, dv, dag, dml]
    token = on_w_in_grad(_dw_in(ht, dproj))
    grad_x, dh_small = _dh_bwd(dproj, w_in_g, x, dyx, g_pre, token)
    small = jnp.concatenate([rnn_small, head_small, dh_small + attn_small,
                             _block_diag_part(dwa_t).reshape(64, 1024), _block_diag_part(dwx_t).reshape(64, 1024)], axis=0)
    return grad_x, small


ROW_LOSS = 11


def _rows8(parts):
    out = None
    for r, a in parts:
        p = jnp.pad(a, ((r, 8 - r - a.shape[0]), (0, 1024 - a.shape[1])))
        out = p if out is None else out + p
    return out


def _pack_small(p):
    g0 = _rows8([(0, p["b_rg_a"].reshape(1, 1024)), (1, p["b_rg_x"].reshape(1, 1024)), (2, p["lru_lambda"]),
                 (3, p["conv_b"]), (4, p["conv_w"][0])])
    g1 = _rows8([(0, p["post_norm_g"]), (1, p["b_gate"].reshape(2, 1024))])
    g2 = _rows8([(0, p["pre_norm_g"]), (1, p["attn_sinks"])])
    return jnp.concatenate([g0, g1, g2, p["w_rg_a"].reshape(64, 1024), p["w_rg_x"].reshape(64, 1024)], axis=0)


def _unpack_small(s, conv_cols):
    return {
        "b_rg_a": s[0:1].reshape(1, 16, 64), "b_rg_x": s[1:2].reshape(1, 16, 64), "lru_lambda": s[2:3],
        "conv_b": s[3:4], "conv_w": s[4:8, 0:conv_cols].reshape(1, CONV_W, conv_cols),
        "post_norm_g": s[8:9], "b_gate": s[9:11].reshape(1, 2048),
        "pre_norm_g": s[16:17], "attn_sinks": s[17:18, 0:N_Q_HEADS],
        "w_rg_a": s[24:88].reshape(1, 16, 64, 64), "w_rg_x": s[88:152].reshape(1, 16, 64, 64),
    }


WEIGHTS = ["pre_norm_g", "w_in", "b_gate", "conv_w", "conv_b", "w_rg_a", "b_rg_a", "w_rg_x", "b_rg_x", "lru_lambda",
           "attn_sinks", "w_rnn_out", "w_attn_out", "w_out", "post_norm_g"]
BIG = ["w_in", "w_rnn_out", "w_attn_out", "w_out"]


def kernel(x, pre_norm_g, w_in, b_gate, conv_w, conv_b, w_rg_a, b_rg_a, w_rg_x, b_rg_x, lru_lambda, attn_sinks, w_rnn_out, w_attn_out, w_out, post_norm_g, loss_target, m_pre_norm_g, m_w_in, m_b_gate, m_conv_w, m_conv_b, m_w_rg_a, m_b_rg_a, m_w_rg_x, m_b_rg_x, m_lru_lambda, m_attn_sinks, m_w_rnn_out, m_w_attn_out, m_w_out, m_post_norm_g, v_pre_norm_g, v_w_in, v_b_gate, v_conv_w, v_conv_b, v_w_rg_a, v_b_rg_a, v_w_rg_x, v_b_rg_x, v_lru_lambda, v_attn_sinks, v_w_rnn_out, v_w_attn_out, v_w_out, v_post_norm_g):
    w = dict(pre_norm_g=pre_norm_g, w_in=w_in, b_gate=b_gate, conv_w=conv_w, conv_b=conv_b, w_rg_a=w_rg_a,
             b_rg_a=b_rg_a, w_rg_x=w_rg_x, b_rg_x=b_rg_x, lru_lambda=lru_lambda, attn_sinks=attn_sinks,
             w_rnn_out=w_rnn_out, w_attn_out=w_attn_out, w_out=w_out, post_norm_g=post_norm_g)
    m = dict(pre_norm_g=m_pre_norm_g, w_in=m_w_in, b_gate=m_b_gate, conv_w=m_conv_w, conv_b=m_conv_b, w_rg_a=m_w_rg_a,
             b_rg_a=m_b_rg_a, w_rg_x=m_w_rg_x, b_rg_x=m_b_rg_x, lru_lambda=m_lru_lambda, attn_sinks=m_attn_sinks,
             w_rnn_out=m_w_rnn_out, w_attn_out=m_w_attn_out, w_out=m_w_out, post_norm_g=m_post_norm_g)
    v = dict(pre_norm_g=v_pre_norm_g, w_in=v_w_in, b_gate=v_b_gate, conv_w=v_conv_w, conv_b=v_conv_b, w_rg_a=v_w_rg_a,
             b_rg_a=v_b_rg_a, w_rg_x=v_w_rg_x, b_rg_x=v_b_rg_x, lru_lambda=v_lru_lambda, attn_sinks=v_attn_sinks,
             w_rnn_out=v_w_rnn_out, w_attn_out=v_w_attn_out, w_out=v_w_out, post_norm_g=v_post_norm_g)
    chip = 2 * lax.axis_index("x") + lax.axis_index("y")

    chip_idx = chip.astype(jnp.int32).reshape(1)
    chip_core = jnp.stack([chip, lax.axis_index("c")]).astype(jnp.int32)
    cw8 = jnp.pad(conv_w[0], ((0, 8 - CONV_W), (0, 0)))
    placed = _place_shards([w_in[0], w_rnn_out[0], w_attn_out[0], w_out[0]], chip_idx, "place_shards")
    win_g, cw_g = _gather_weights(placed[:1], cw8)
    late_send, late_recv, late_thru, late_token = _gather_late_start(placed[1:], win_g, "gather_late_start")
    cw_g = lax.dynamic_update_slice_in_dim(cw_g, cw8[None], chip, axis=0)
    conv_w_full = jnp.transpose(cw_g[:, 0:CONV_W, :], (1, 0, 2)).reshape(CONV_W, D_RNN)

    core_idx = lax.axis_index("c").astype(jnp.int32).reshape(1)
    started = {}

    def start_reduction(tag, grads):
        got = _pair_exchange(grads, "pair_exchange_" + tag)
        sums = [_pair_sum(g, o, core_idx, "pair_sum_%s_%d" % (tag, a)) for a, (g, o) in enumerate(zip(grads, got))]
        send_sems, recv_sems, p_thru, land_thru, token = _chip_exchange_start(
            [pb for _, pb in sums], "chip_exchange_start_" + tag)
        started[tag] = ([p for p, _ in sums], send_sems, recv_sems, p_thru, land_thru)
        return token

    def end_reduction(tag, after):
        psums, send_sems, recv_sems, p_thru, land_thru = started[tag]
        landed = _chip_exchange_wait(send_sems, recv_sems, p_thru, land_thru, after, "chip_exchange_wait_" + tag)
        return [_chip_sum(p, l, chip_core, "chip_sum_%s_%d" % (tag, a)) for a, (p, l) in enumerate(zip(psums, landed))]

    def out_weights(after):
        gathered = _gather_late_wait(late_send, late_recv, late_thru, after, "gather_late_wait")
        return [g.reshape(D_MODEL, D_MODEL) for g in gathered]

    grad_x, small = _local_grads(
        x[0], loss_target[0], pre_norm_g, win_g, b_gate, conv_w_full, conv_b, w_rg_a[0], b_rg_a[0], w_rg_x[0],
        b_rg_x[0], lru_lambda, attn_sinks[0], out_weights, late_token, post_norm_g,
        on_out_grads=lambda grads: start_reduction("out", grads),
        on_w_in_grad=lambda grad: start_reduction("in", [grad]))

    halves = end_reduction("in", grad_x) + end_reduction("out", grad_x)
    gbig = dict(zip(BIG, _pair_share(halves)))

    small_sum = _allreduce_small(small)
    total_loss = small_sum[ROW_LOSS, 0]
    gsmall = _unpack_small(small_sum, D_RNN)
    conv_shard = D_RNN // N_CHIPS
    gsmall["conv_w"] = lax.dynamic_slice_in_dim(gsmall["conv_w"], chip * conv_shard, conv_shard, axis=2)

    grads, delta, new_m, new_v = {}, {}, {}, {}
    for n in BIG:
        grads[n] = gbig[n][None]
        d, nm, nv = _adamw(w[n][0], gbig[n], m[n][0], v[n][0], "adamw_" + n)
        delta[n], new_m[n], new_v[n] = d[None], nm[None], nv[None]
    pick = lambda t: {k: t[k] for k in gsmall}
    d, nm, nv = _adamw(_pack_small(pick(w)), _pack_small(gsmall), _pack_small(pick(m)), _pack_small(pick(v)),
                       "adamw_small")
    ud, um, uv = _unpack_small(d, conv_shard), _unpack_small(nm, conv_shard), _unpack_small(nv, conv_shard)
    for n in gsmall:
        grads[n] = gsmall[n].reshape(w[n].shape)
        delta[n] = ud[n].reshape(w[n].shape)
        new_m[n] = um[n].reshape(w[n].shape)
        new_v[n] = uv[n].reshape(w[n].shape)

    return (total_loss, grad_x[None], *[grads[n] for n in WEIGHTS], *[delta[n] for n in WEIGHTS],
            *[new_m[n] for n in WEIGHTS], *[new_v[n] for n in WEIGHTS])
```

```python
import functools
import math

import jax
import jax.numpy as jnp
from jax import lax
from jax.experimental import pallas as pl
from jax.experimental.pallas import tpu as pltpu

F32 = jnp.float32
BF16 = jnp.bfloat16

D_MODEL = 1024
D_RNN = 1024
RNN_BLOCKS = 16
RNN_BLOCK_W = 64
CONV_W = 4
LRU_C = 8.0
N_Q_HEADS = 16
N_KV_HEADS = 4
GROUP = 4
HEAD_DIM = 64
D_KV = 256
BLOCK = 128
ALIBI_MAX_BIAS = 8.0
EPS = 1e-6
D_IN = 6656
N_CHIPS = 4
W_IN_SHARD = D_IN // N_CHIPS
OUT_SHARD = D_MODEL // N_CHIPS
ADAM_LR = 0.001
ADAM_B1 = 0.9
ADAM_B2 = 0.999
ADAM_EPS = 1e-08
ADAM_WD = 0.01
ADAM_STEP = 10
NEG_BIG = -1e30
MIB = 1 << 20

COL_RNN_X = 0
COL_RNN_GATE = 4
COL_Q = 8
COL_K = 12
COL_V = 13
COL_ATTN_GATE = 14
COL_MERGE = 18

RNN_TILE = 256
RNN_CHUNK = 256
SMALL_ROWS = 152
SMALL_VECTOR_ROWS = 24
MESH = pl.DeviceIdType.MESH


def _sds(shape, dtype):
    return pltpu.HBM(shape, dtype)


def _params(sem=None, vmem_mib=None):
    kw = {}
    if sem is not None:
        kw["dimension_semantics"] = sem
    if vmem_mib is not None:
        kw["vmem_limit_bytes"] = vmem_mib * MIB
    return pltpu.CompilerParams(**kw)


def _hbm(*arrays):
    return [pltpu.with_memory_space_constraint(a, pltpu.HBM) for a in arrays]


def _dot(a, b):
    return jnp.dot(a, b, preferred_element_type=F32)


def _dot_nt(a, b):
    return lax.dot_general(a, b, (((1,), (1,)), ((), ())), preferred_element_type=F32)


def _dot_tn(a, b):
    return lax.dot_general(a, b, (((0,), (0,)), ((), ())), preferred_element_type=F32)


def _sigmoid(x):
    return 0.5 * jnp.tanh(0.5 * x) + 0.5


def _sigmoid_small(x):
    return 1.0 / (1.0 + jnp.exp(-x))


def _softplus(x):
    return jnp.maximum(x, 0.0) + jnp.log(1.0 + jnp.exp(-jnp.abs(x)))


def _one_minus_square(a, log_a):
    return -jnp.tanh(log_a) * (a * a + 1.0)


def _proj_fwd(x, g_pre, w_in_g):
    T = x.shape[0]
    tm = min(1024, T)

    def body(x_ref, g_ref, w_ref, proj_ref, ht_ref, h_s):
        @pl.when(pl.program_id(1) == 0)
        def _():
            xv = x_ref[...]
            rstd = lax.rsqrt(jnp.mean(xv * xv, axis=-1, keepdims=True) + EPS)
            hf = (xv * rstd) * g_ref[...]
            h_s[...] = hf.astype(BF16)
            ht_ref[...] = hf.T.astype(BF16)

        proj_ref[...] = _dot(h_s[...], w_ref[...]).astype(BF16)

    return pl.pallas_call(
        body,
        name="proj_fwd",
        grid=(T // tm, N_CHIPS),
        in_specs=[
            pl.BlockSpec((tm, D_MODEL), lambda i, j: (i, 0)),
            pl.BlockSpec((1, D_MODEL), lambda i, j: (0, 0)),
            pl.BlockSpec((None, D_MODEL, W_IN_SHARD), lambda i, j: (j, 0, 0)),
        ],
        out_specs=[
            pl.BlockSpec((tm, W_IN_SHARD), lambda i, j: (i, j)),
            pl.BlockSpec((D_MODEL, tm), lambda i, j: (0, i)),
        ],
        out_shape=[_sds((T, D_IN), BF16), _sds((D_MODEL, T), BF16)],
        scratch_shapes=[pltpu.VMEM((tm, D_MODEL), BF16)],
        compiler_params=_params(("parallel", "arbitrary"), 48),
    )(*_hbm(x, g_pre, w_in_g))


def _shift_down(x, tail, s, row):
    n = x.shape[0]
    xs = pltpu.roll(x, s, 0)
    tail_t = jnp.tile(pltpu.roll(tail, s, 0), (n // 8, 1))
    return jnp.where(row < s, tail_t, xs)


def _shift_up(x, head, s, row):
    n = x.shape[0]
    xs = pltpu.roll(x, n - s, 0)
    head_t = jnp.tile(pltpu.roll(head, 8 - s, 0), (n // 8, 1))
    return jnp.where(row >= n - s, head_t, xs)


def _conv_taps(x, tail, row):
    return [_shift_down(x, tail, 3, row), _shift_down(x, tail, 2, row), _shift_down(x, tail, 1, row), x]


def _rglru_gates(c, wa, wx, ba, bx, lam):
    cb = c.astype(BF16)
    r = _sigmoid_small(_dot(cb, wa) + ba)
    i = _sigmoid(_dot(cb, wx) + bx)
    log_a = (-LRU_C) * r * _softplus(-lam)
    a = jnp.exp(log_a)
    mult = jnp.sqrt(_one_minus_square(a, log_a))
    return cb, r, i, a, mult


SUBLANES = 8


def _scan_down(a, u, row):
    n = a.shape[0]
    s = 1
    while s < SUBLANES:
        a_sh = jnp.where(row >= s, pltpu.roll(a, s, 0), 1.0)
        u_sh = jnp.where(row >= s, pltpu.roll(u, s, 0), 0.0)
        u = a * u_sh + u
        a = a * a_sh
        s *= 2
    while s < n:
        u = jnp.concatenate([u[:s], a[s:] * u[:n - s] + u[s:]], axis=0)
        a = jnp.concatenate([a[:s], a[s:] * a[:n - s]], axis=0)
        s *= 2
    return a, u


def _scan_up(b, u, row):
    n = b.shape[0]
    s = 1
    while s < SUBLANES:
        b_sh = jnp.where(row < n - s, pltpu.roll(b, n - s, 0), 1.0)
        u_sh = jnp.where(row < n - s, pltpu.roll(u, n - s, 0), 0.0)
        u = b * u_sh + u
        b = b * b_sh
        s *= 2
    while s < n:
        u = jnp.concatenate([b[:n - s] * u[s:] + u[:n - s], u[n - s:]], axis=0)
        b = jnp.concatenate([b[:n - s] * b[s:], b[n - s:]], axis=0)
        s *= 2
    return b, u


def _rnn_fwd(proj, conv_w, conv_b, wa_bd, wx_bd, b_a, b_x, lam, token):
    T = proj.shape[0]
    tc, ct = RNN_CHUNK, RNN_TILE
    nt = T // tc

    def body(x_ref, rg_ref, cw_ref, cb_ref, wa_ref, wx_ref, ba_ref, bx_ref, lam_ref, token_ref, h_ref, z_ref, xtail,
             hcarry):
        @pl.when(pl.program_id(1) == 0)
        def _():
            xtail[...] = jnp.zeros_like(xtail)
            hcarry[...] = jnp.zeros_like(hcarry)

        row = lax.broadcasted_iota(jnp.int32, (tc, ct), 0)
        x = x_ref[...].astype(F32)
        taps = _conv_taps(x, xtail[...], row)
        c = cb_ref[...] + cw_ref[pl.ds(0, 1), :] * taps[0]
        for k in range(1, CONV_W):
            c = c + cw_ref[pl.ds(k, 1), :] * taps[k]
        xtail[...] = x[tc - 8:, :]
        _, _, i, a, mult = _rglru_gates(c, wa_ref[...], wx_ref[...], ba_ref[...], bx_ref[...], lam_ref[...])
        u = mult * (i * c)
        a_cum, h0 = _scan_down(a, u, row)
        h = h0 + a_cum * hcarry[...]
        h_ref[...] = h
        hcarry[...] = h_ref[pl.ds(tc - 1, 1), :]
        rg = rg_ref[...].astype(F32)
        z_ref[...] = (h * (rg * _sigmoid(rg))).astype(BF16)

    col = lambda off: (lambda j, t: (t, off + j))
    vec = pl.BlockSpec((1, ct), lambda j, t: (0, j))
    mat = pl.BlockSpec((None, ct, ct), lambda j, t: (j, 0, 0))
    return pl.pallas_call(
        body,
        name="rnn_fwd",
        grid=(D_RNN // ct, nt),
        in_specs=[
            pl.BlockSpec((tc, ct), col(COL_RNN_X)),
            pl.BlockSpec((tc, ct), col(COL_RNN_GATE)),
            pl.BlockSpec((CONV_W, ct), lambda j, t: (0, j)),
            vec, mat, mat, vec, vec, vec,
            pl.BlockSpec((8, 128), lambda j, t: (0, 0)),
        ],
        out_specs=[pl.BlockSpec((tc, ct), lambda j, t: (t, j)), pl.BlockSpec((tc, ct), lambda j, t: (t, j))],
        out_shape=[_sds((T, D_RNN), F32), _sds((T, D_RNN), BF16)],
        scratch_shapes=[pltpu.VMEM((8, ct), F32), pltpu.VMEM((1, ct), F32)],
        compiler_params=_params(("parallel", "arbitrary"), 32),
    )(*_hbm(proj, proj, conv_w, conv_b, wa_bd, wx_bd, b_a, b_x, lam, token))


def _rnn_bwd(proj, y_rnn, dz_rnn, conv_w, conv_b, wa_bd, wx_bd, b_a, b_x, lam):
    T = proj.shape[0]
    tc, ct = RNN_CHUNK, RNN_TILE
    nt = T // tc
    hb = tc // 8

    def body(x_ref, xh_ref, rg_ref, h_ref, hh_ref, dz_ref, cw_ref, cb_ref, wa_ref, wx_ref, ba_ref, bx_ref, lam_ref,
             dx_ref, drg_ref, dwa_ref, dwx_ref, sm_ref, lam_carry, a_carry, dc_head):
        t = pl.program_id(1)
        first_chunk = t == nt - 1

        @pl.when(t == 0)
        def _():
            lam_carry[...] = jnp.zeros_like(lam_carry)
            a_carry[...] = jnp.zeros_like(a_carry)
            dc_head[...] = jnp.zeros_like(dc_head)
            dwa_ref[...] = jnp.zeros_like(dwa_ref)
            dwx_ref[...] = jnp.zeros_like(dwx_ref)
            sm_ref[...] = jnp.zeros_like(sm_ref)

        row = lax.broadcasted_iota(jnp.int32, (tc, ct), 0)
        keep = jnp.where(first_chunk, 0.0, 1.0)
        x = x_ref[...].astype(F32)
        xtail = xh_ref[...].astype(F32)[8:16, :] * keep
        taps = _conv_taps(x, xtail, row)
        c = cb_ref[...] + cw_ref[pl.ds(0, 1), :] * taps[0]
        for k in range(1, CONV_W):
            c = c + cw_ref[pl.ds(k, 1), :] * taps[k]
        lam = lam_ref[...]
        cb, r, i, a, mult = _rglru_gates(c, wa_ref[...], wx_ref[...], ba_ref[...], bx_ref[...], lam)
        h = h_ref[...]
        h_prev = _shift_down(h, hh_ref[...] * keep, 1, row)
        rg = rg_ref[...].astype(F32)
        dz = dz_ref[...]
        sg = _sigmoid(rg)
        drg_ref[...] = (dz * h * (sg * (1.0 + rg * (1.0 - sg)))).astype(BF16)
        dy = dz * (rg * sg)
        b = jnp.where(row >= tc - 1, a_carry[pl.ds(0, 1), :], pltpu.roll(a, tc - 1, 0))
        b_cum, l0 = _scan_up(b, dy, row)
        lt = l0 + b_cum * lam_carry[pl.ds(0, 1), :]
        lam_carry[...] = lt[0:8, :]
        a_carry[...] = a[0:8, :]
        ic = i * c
        dmult = lt * ic
        di = lt * mult * c
        dc = lt * mult * i
        dlog_a = a * (lt * h_prev - dmult * a / mult)
        sp = _softplus(-lam)
        dpre_r = dlog_a * ((-LRU_C) * sp) * (r * (1.0 - r))
        dpre_i = di * (i * (1.0 - i))
        dlam_row = jnp.sum(dlog_a * r, axis=0, keepdims=True) * (LRU_C * _sigmoid(-lam))
        dpr_b = dpre_r.astype(BF16)
        dpi_b = dpre_i.astype(BF16)
        dwa_ref[...] += _dot_tn(cb, dpr_b)
        dwx_ref[...] += _dot_tn(cb, dpi_b)
        dc = dc + _dot_nt(dpr_b, wa_ref[...]) + _dot_nt(dpi_b, wx_ref[...])
        head = dc_head[...]
        dx = cw_ref[pl.ds(3, 1), :] * dc
        for m in range(1, CONV_W):
            dx = dx + cw_ref[pl.ds(3 - m, 1), :] * _shift_up(dc, head, m, row)
        dx_ref[...] = dx.astype(BF16)
        dc_head[...] = dc[0:8, :]
        sm_ref[pl.ds(0, 1), :] += jnp.sum(dpre_r, axis=0, keepdims=True)
        sm_ref[pl.ds(1, 1), :] += jnp.sum(dpre_i, axis=0, keepdims=True)
        sm_ref[pl.ds(2, 1), :] += dlam_row
        sm_ref[pl.ds(3, 1), :] += jnp.sum(dc, axis=0, keepdims=True)
        for k in range(CONV_W):
            sm_ref[pl.ds(4 + k, 1), :] += jnp.sum(dc * taps[k], axis=0, keepdims=True)

    rev = lambda off: (lambda j, t: (nt - 1 - t, off + j))
    halo = lambda off: (lambda j, t: (jnp.maximum((nt - 1 - t) * hb - 1, 0), off + j))
    halo16 = lambda off: (lambda j, t: (jnp.maximum((nt - 1 - t) * (hb // 2) - 1, 0), off + j))
    vec = pl.BlockSpec((1, ct), lambda j, t: (0, j))
    mat = pl.BlockSpec((None, ct, ct), lambda j, t: (j, 0, 0))
    return pl.pallas_call(
        body,
        name="rnn_bwd",
        grid=(D_RNN // ct, nt),
        in_specs=[
            pl.BlockSpec((tc, ct), rev(COL_RNN_X)),
            pl.BlockSpec((16, ct), halo16(COL_RNN_X)),
            pl.BlockSpec((tc, ct), rev(COL_RNN_GATE)),
            pl.BlockSpec((tc, ct), rev(0)),
            pl.BlockSpec((8, ct), halo(0)),
            pl.BlockSpec((tc, ct), rev(0)),
            pl.BlockSpec((CONV_W, ct), lambda j, t: (0, j)),
            vec, mat, mat, vec, vec, vec,
        ],
        out_specs=[
            pl.BlockSpec((tc, ct), rev(0)),
            pl.BlockSpec((tc, ct), rev(0)),
            mat, mat,
            pl.BlockSpec((8, ct), lambda j, t: (0, j)),
        ],
        out_shape=[_sds((T, D_RNN), BF16), _sds((T, D_RNN), BF16), _sds((D_RNN // ct, ct, ct), F32),
                   _sds((D_RNN // ct, ct, ct), F32), _sds((8, D_RNN), F32)],
        scratch_shapes=[pltpu.VMEM((8, ct), F32), pltpu.VMEM((8, ct), F32), pltpu.VMEM((8, ct), F32)],
        compiler_params=_params(("parallel", "arbitrary"), 32),
    )(*_hbm(proj, proj, proj, y_rnn, y_rnn, dz_rnn, conv_w, conv_b, wa_bd, wx_bd, b_a, b_x, lam))


def _attn_bias():
    qi = jnp.arange(BLOCK)[:, None]
    kj = jnp.arange(BLOCK)[None, :]
    dist_cur = (qi - kj).astype(F32)
    slopes = 2.0 ** (-ALIBI_MAX_BIAS * jnp.arange(1, N_Q_HEADS + 1, dtype=F32) / N_Q_HEADS)
    slopes = slopes[:, None, None]
    prev = jnp.where(kj > qi, -slopes * (dist_cur + float(BLOCK)), NEG_BIG)
    cur = jnp.where(kj <= qi, -slopes * dist_cur, NEG_BIG)
    later = jnp.concatenate([prev, cur], axis=-1)
    first = jnp.concatenate([jnp.full_like(prev, NEG_BIG), cur], axis=-1)
    return jnp.stack([first, later])


def _attn_exps(s_prev, s_cur, sink, bias):
    s_prev = s_prev + bias[:, 0:BLOCK]
    s_cur = s_cur + bias[:, BLOCK:2 * BLOCK]
    m = jnp.maximum(jnp.max(jnp.maximum(s_prev, s_cur), axis=-1, keepdims=True), sink)
    p_prev = jnp.exp(s_prev - m)
    p_cur = jnp.exp(s_cur - m)
    total = jnp.sum(p_prev + p_cur, axis=-1, keepdims=True) + jnp.exp(sink - m)
    return p_prev, p_cur, 1.0 / total, m + jnp.log(total)


def _attn_probs(s_prev, s_cur, sink, bias, lse):
    p_prev = jnp.exp((s_prev + bias[:, 0:BLOCK]) - lse)
    p_cur = jnp.exp((s_cur + bias[:, BLOCK:2 * BLOCK]) - lse)
    return p_prev, p_cur, jnp.exp(sink - lse)


def _stack_heads(ref_or_val, hk, dtype):
    parts = [ref_or_val[:, (GROUP * hk + g) * HEAD_DIM:(GROUP * hk + g + 1) * HEAD_DIM] for g in range(GROUP)]
    return jnp.concatenate(parts, axis=0).astype(dtype)


ATTN_SCALE = HEAD_DIM ** -0.5


def _bias_spec():
    return pl.BlockSpec((None, N_Q_HEADS, BLOCK, 2 * BLOCK), lambda i: (jnp.minimum(i, 1), 0, 0, 0))


def _attn_fwd(proj, sinks, bias):
    T = proj.shape[0]
    nb = T // BLOCK

    def body(sink_ref, bias_ref, q_ref, kp_ref, kc_ref, vp_ref, vc_ref, ag0_ref, ag1_ref, y_ref, z_ref, lse_ref):
        kvs = [slice(hk * HEAD_DIM, (hk + 1) * HEAD_DIM) for hk in range(N_KV_HEADS)]
        qgs = [(_stack_heads(q_ref, hk, F32) * ATTN_SCALE).astype(BF16) for hk in range(N_KV_HEADS)]
        s_prev = [_dot_nt(qgs[hk], kp_ref[:, kvs[hk]].astype(BF16)) for hk in range(N_KV_HEADS)]
        s_cur = [_dot_nt(qgs[hk], kc_ref[:, kvs[hk]].astype(BF16)) for hk in range(N_KV_HEADS)]
        for hk in range(N_KV_HEADS):
            pp, pc, invs = [], [], []
            for g in range(GROUP):
                h = GROUP * hk + g
                rows = slice(g * BLOCK, (g + 1) * BLOCK)
                p_prev, p_cur, inv, lse = _attn_exps(s_prev[hk][rows], s_cur[hk][rows], sink_ref[h], bias_ref[h])
                pp.append(p_prev.astype(BF16))
                pc.append(p_cur.astype(BF16))
                invs.append(inv)
                lse_ref[:, h:h + 1] = lse
            og = _dot(jnp.concatenate(pp, axis=0), vp_ref[:, kvs[hk]].astype(BF16)) + _dot(
                jnp.concatenate(pc, axis=0), vc_ref[:, kvs[hk]].astype(BF16))
            for g in range(GROUP):
                h = GROUP * hk + g
                y_ref[:, h * HEAD_DIM:(h + 1) * HEAD_DIM] = og[g * BLOCK:(g + 1) * BLOCK] * invs[g]
        ag = jnp.concatenate([ag0_ref[...], ag1_ref[...]], axis=1).astype(F32)
        z_ref[...] = (y_ref[...] * (ag * _sigmoid(ag))).astype(BF16)

    prev = lambda c: (lambda i: (jnp.maximum(i - 1, 0), c))
    cur = lambda c: (lambda i: (i, c))
    return pl.pallas_call(
        body,
        name="attn_fwd",
        grid=(nb,),
        in_specs=[
            pl.BlockSpec(memory_space=pltpu.SMEM),
            _bias_spec(),
            pl.BlockSpec((BLOCK, 1024), lambda i: (i, COL_Q // 4)),
            pl.BlockSpec((BLOCK, D_KV), prev(COL_K)),
            pl.BlockSpec((BLOCK, D_KV), cur(COL_K)),
            pl.BlockSpec((BLOCK, D_KV), prev(COL_V)),
            pl.BlockSpec((BLOCK, D_KV), cur(COL_V)),
            pl.BlockSpec((BLOCK, 512), lambda i: (i, COL_ATTN_GATE // 2)),
            pl.BlockSpec((BLOCK, 512), lambda i: (i, COL_ATTN_GATE // 2 + 1)),
        ],
        out_specs=[pl.BlockSpec((BLOCK, 1024), lambda i: (i, 0)), pl.BlockSpec((BLOCK, 1024), lambda i: (i, 0)),
                   pl.BlockSpec((BLOCK, N_Q_HEADS), lambda i: (i, 0))],
        out_shape=[_sds((T, 1024), F32), _sds((T, 1024), BF16), _sds((T, N_Q_HEADS), F32)],
        compiler_params=_params(("arbitrary",), 32),
    )(sinks, *_hbm(bias, proj, proj, proj, proj, proj, proj, proj))


def _attn_bwd(proj, y_attn, lse, dz_attn, sinks, bias, token):
    T = proj.shape[0]
    nb = T // BLOCK

    def body(sink_ref, bias_ref, q_ref, kp_ref, kc_ref, vp_ref, vc_ref, ag0_ref, ag1_ref, y_ref, lse_ref, dz_ref,
             token_ref, dq_ref, dk_ref, dv_ref, dag_ref, ds_ref, dy_s):
        i = pl.program_id(0)

        @pl.when(i == 0)
        def _():
            ds_ref[...] = jnp.zeros_like(ds_ref)

        lane = lax.broadcasted_iota(jnp.int32, (8, 128), 1)
        sub = lax.broadcasted_iota(jnp.int32, (8, 128), 0)
        ag = jnp.concatenate([ag0_ref[...], ag1_ref[...]], axis=1).astype(F32)
        dz = dz_ref[...]
        sg = _sigmoid(ag)
        dag_ref[...] = (dz * y_ref[...] * (sg * (1.0 + ag * (1.0 - sg)))).astype(BF16)
        dy_s[...] = dz * (ag * sg)
        r_cur = pl.multiple_of(i * BLOCK, BLOCK)
        r_prev = pl.multiple_of(jnp.maximum(i - 1, 0) * BLOCK, BLOCK)
        dk_cur, dv_cur, dk_prev, dv_prev = [], [], [], []
        ds_acc = jnp.zeros((8, 128), F32)
        for hk in range(N_KV_HEADS):
            ks = slice(hk * HEAD_DIM, (hk + 1) * HEAD_DIM)
            qg = (_stack_heads(q_ref, hk, F32) * ATTN_SCALE).astype(BF16)
            dog = _stack_heads(dy_s, hk, F32)
            og = _stack_heads(y_ref, hk, F32)
            dog_b = dog.astype(BF16)
            kp = kp_ref[:, ks].astype(BF16)
            kc = kc_ref[:, ks].astype(BF16)
            vp = vp_ref[:, ks].astype(BF16)
            vc = vc_ref[:, ks].astype(BF16)
            s_prev = _dot_nt(qg, kp)
            s_cur = _dot_nt(qg, kc)
            dp_prev = _dot_nt(dog_b, vp)
            dp_cur = _dot_nt(dog_b, vc)
            dvec = jnp.sum(dog * og, axis=-1, keepdims=True)
            pp, pc, dsp, dsc = [], [], [], []
            for g in range(GROUP):
                h = GROUP * hk + g
                rows = slice(g * BLOCK, (g + 1) * BLOCK)
                p_prev, p_cur, p_sink = _attn_probs(
                    s_prev[rows], s_cur[rows], sink_ref[h], bias_ref[h], lse_ref[:, h:h + 1])
                d_h = dvec[rows]
                pp.append(p_prev.astype(BF16))
                pc.append(p_cur.astype(BF16))
                dsp.append((p_prev * (dp_prev[rows] - d_h)).astype(BF16))
                dsc.append((p_cur * (dp_cur[rows] - d_h)).astype(BF16))
                dsink = -jnp.sum(p_sink * d_h, axis=0, keepdims=True)
                ds_acc = ds_acc + jnp.where(jnp.logical_and(lane == h, sub == 1), dsink, 0.0)
            pp = jnp.concatenate(pp, axis=0)
            pc = jnp.concatenate(pc, axis=0)
            dsp = jnp.concatenate(dsp, axis=0)
            dsc = jnp.concatenate(dsc, axis=0)
            dqg = (_dot(dsp, kp) + _dot(dsc, kc)) * ATTN_SCALE
            for g in range(GROUP):
                h = GROUP * hk + g
                dq_ref[:, h * HEAD_DIM:(h + 1) * HEAD_DIM] = dqg[g * BLOCK:(g + 1) * BLOCK].astype(BF16)
            dk_ref[pl.ds(r_cur, BLOCK), ks] = _dot_tn(dsc, qg)
            dv_ref[pl.ds(r_cur, BLOCK), ks] = _dot_tn(pc, dog_b)
            dk_prev.append(_dot_tn(dsp, qg))
            dv_prev.append(_dot_tn(pp, dog_b))
        ds_ref[:, 0:128] += ds_acc

        @pl.when(i > 0)
        def _():
            for hk in range(N_KV_HEADS):
                ks = slice(hk * HEAD_DIM, (hk + 1) * HEAD_DIM)
                dk_ref[pl.ds(r_prev, BLOCK), ks] += dk_prev[hk]
                dv_ref[pl.ds(r_prev, BLOCK), ks] += dv_prev[hk]

    prev = lambda c: (lambda i: (jnp.maximum(i - 1, 0), c))
    cur = lambda c: (lambda i: (i, c))
    blk = pl.BlockSpec((BLOCK, 1024), lambda i: (i, 0))
    whole = pl.BlockSpec((T, D_KV), lambda i: (0, 0))
    return pl.pallas_call(
        body,
        name="attn_bwd",
        grid=(nb,),
        in_specs=[
            pl.BlockSpec(memory_space=pltpu.SMEM),
            _bias_spec(),
            pl.BlockSpec((BLOCK, 1024), lambda i: (i, COL_Q // 4)),
            pl.BlockSpec((BLOCK, D_KV), prev(COL_K)),
            pl.BlockSpec((BLOCK, D_KV), cur(COL_K)),
            pl.BlockSpec((BLOCK, D_KV), prev(COL_V)),
            pl.BlockSpec((BLOCK, D_KV), cur(COL_V)),
            pl.BlockSpec((BLOCK, 512), lambda i: (i, COL_ATTN_GATE // 2)),
            pl.BlockSpec((BLOCK, 512), lambda i: (i, COL_ATTN_GATE // 2 + 1)),
            blk,
            pl.BlockSpec((BLOCK, N_Q_HEADS), lambda i: (i, 0)),
            blk,
            pl.BlockSpec((8, 128), lambda i: (0, 0)),
        ],
        out_specs=[blk, whole, whole, blk, pl.BlockSpec((8, 1024), lambda i: (0, 0))],
        out_shape=[_sds((T, 1024), BF16), _sds((T, D_KV), F32), _sds((T, D_KV), F32), _sds((T, 1024), BF16),
                   _sds((8, 1024), F32)],
        scratch_shapes=[pltpu.VMEM((BLOCK, 1024), F32)],
        compiler_params=_params(("arbitrary",), 48),
    )(sinks, *_hbm(bias, proj, proj, proj, proj, proj, proj, proj, y_attn, lse, dz_attn, token))


def _head(x, target, z_rnn, z_attn, proj, b_gate, g_post, w_rnn_out, w_attn_out, w_out):
    T = x.shape[0]
    tm = 256

    def body(x_ref, t_ref, zr_ref, za_ref, ml0_ref, ml1_ref, ml2_ref, ml3_ref, bg_ref, gp_ref, wr_ref, wa_ref, wo_ref,
             dyx_ref, dzr_ref, dza_ref, dml_ref, mb_ref, dout_ref, dbr_ref, dba_ref, sm_ref):
        @pl.when(pl.program_id(0) == 0)
        def _():
            sm_ref[...] = jnp.zeros_like(sm_ref)

        wr, wa, wo = wr_ref[...], wa_ref[...], wo_ref[...]
        br_rnn = _dot(zr_ref[...], wr)
        br_attn = _dot(za_ref[...], wa)
        ml_rnn = jnp.concatenate([ml0_ref[...], ml1_ref[...]], axis=1).astype(F32)
        ml_attn = jnp.concatenate([ml2_ref[...], ml3_ref[...]], axis=1).astype(F32)
        g_rnn = _sigmoid(ml_rnn + bg_ref[:, 0:D_MODEL])
        g_attn = _sigmoid(ml_attn + bg_ref[:, D_MODEL:2 * D_MODEL])
        mb = (g_rnn * br_rnn + g_attn * br_attn).astype(BF16)
        mb_ref[...] = mb
        out = _dot(mb, wo)
        rstd = lax.rsqrt(jnp.mean(out * out, axis=-1, keepdims=True) + EPS)
        n = out * rstd
        gp = gp_ref[...]
        err = (x_ref[...] + n * gp) - t_ref[...]
        sm_ref[pl.ds(3, 1), :] += 0.5 * jnp.sum(jnp.mean(err * err, axis=-1, keepdims=True), axis=0, keepdims=True)
        dy = err * (1.0 / D_MODEL)
        dyx_ref[...] = dy
        sm_ref[pl.ds(0, 1), :] += jnp.sum(dy * n, axis=0, keepdims=True)
        dn = dy * gp
        dout = (rstd * (dn - n * jnp.mean(dn * n, axis=-1, keepdims=True))).astype(BF16)
        dout_ref[...] = dout
        dmerged = _dot_nt(dout, wo)
        dml_r = (dmerged * br_rnn) * (g_rnn * (1.0 - g_rnn))
        dml_a = (dmerged * br_attn) * (g_attn * (1.0 - g_attn))
        dml_ref[:, 0:D_MODEL] = dml_r.astype(BF16)
        dml_ref[:, D_MODEL:2 * D_MODEL] = dml_a.astype(BF16)
        sm_ref[pl.ds(1, 1), :] += jnp.sum(dml_r, axis=0, keepdims=True)
        sm_ref[pl.ds(2, 1), :] += jnp.sum(dml_a, axis=0, keepdims=True)
        dbr = (dmerged * g_rnn).astype(BF16)
        dba = (dmerged * g_attn).astype(BF16)
        dbr_ref[...] = dbr
        dba_ref[...] = dba
        dzr_ref[...] = _dot_nt(dbr, wr)
        dza_ref[...] = _dot_nt(dba, wa)

    tile = pl.BlockSpec((tm, D_MODEL), lambda i: (i, 0))
    wspec = pl.BlockSpec((D_MODEL, D_MODEL), lambda i: (0, 0))
    ml = lambda q: pl.BlockSpec((tm, 512), lambda i: (i, COL_MERGE // 2 + q))
    return pl.pallas_call(
        body,
        name="head",
        grid=(T // tm,),
        in_specs=[
            tile, tile, tile, tile,
            ml(0), ml(1), ml(2), ml(3),
            pl.BlockSpec((1, 2 * D_MODEL), lambda i: (0, 0)),
            pl.BlockSpec((1, D_MODEL), lambda i: (0, 0)),
            wspec, wspec, wspec,
        ],
        out_specs=[
            tile, tile, tile,
            pl.BlockSpec((tm, 2 * D_MODEL), lambda i: (i, 0)),
            tile, tile, tile, tile,
            pl.BlockSpec((8, D_MODEL), lambda i: (0, 0)),
        ],
        out_shape=[
            _sds((T, D_MODEL), F32), _sds((T, D_MODEL), F32), _sds((T, D_MODEL), F32),
            _sds((T, 2 * D_MODEL), BF16),
            _sds((T, D_MODEL), BF16), _sds((T, D_MODEL), BF16), _sds((T, D_MODEL), BF16), _sds((T, D_MODEL), BF16),
            _sds((8, D_MODEL), F32),
        ],
        compiler_params=_params(("arbitrary",), 56),
    )(*_hbm(x, target, z_rnn, z_attn, proj, proj, proj, proj, b_gate, g_post, w_rnn_out, w_attn_out, w_out))


def _matmul_tn(a, b, name):
    T, M = a.shape
    N = b.shape[1]
    tk = min(512, T)
    nt = T // tk

    def body(a_ref, b_ref, o_ref, ob_ref):
        @pl.when(pl.program_id(0) == 0)
        def _():
            o_ref[...] = jnp.zeros_like(o_ref)

        o_ref[...] += _dot_tn(a_ref[...], b_ref[...])

        @pl.when(pl.program_id(0) == nt - 1)
        def _():
            ob_ref[...] = o_ref[...].astype(BF16)

    whole = pl.BlockSpec((M, N), lambda t: (0, 0))
    return pl.pallas_call(
        body,
        name=name,
        grid=(nt,),
        in_specs=[pl.BlockSpec((tk, M), lambda t: (t, 0)), pl.BlockSpec((tk, N), lambda t: (t, 0))],
        out_specs=[whole, whole],
        out_shape=[_sds((M, N), F32), _sds((M, N), BF16)],
        compiler_params=_params(("arbitrary",), 48),
    )(*_hbm(a, b))


DPROJ_WIDTHS = (D_RNN, D_RNN, 1024, D_KV, D_KV, 1024, 2 * D_MODEL)


def _dproj_segments():
    segs, start = [[] for _ in range(N_CHIPS)], 0
    for p, width in enumerate(DPROJ_WIDTHS):
        for c in range(N_CHIPS):
            lo, hi = max(start, c * W_IN_SHARD), min(start + width, (c + 1) * W_IN_SHARD)
            if lo < hi:
                segs[c].append((p, lo - start, hi - start, lo - c * W_IN_SHARD, hi - c * W_IN_SHARD))
        start += width
    return segs


def _dh_bwd(pieces, w_in_g, x, dyx, g_pre, token):
    T = x.shape[0]
    tm = min(512, T)
    n = len(pieces)
    segs = _dproj_segments()

    def body(*refs):
        p_refs, w_hbm, x_ref, dyx_ref, g_ref = refs[0:n], refs[n], refs[n + 1], refs[n + 2], refs[n + 3]
        gx_ref, dg_ref, w_ref = refs[n + 5], refs[n + 6], refs[n + 7]

        @pl.when(pl.program_id(0) == 0)
        def _():
            pltpu.sync_copy(w_hbm, w_ref)
            dg_ref[...] = jnp.zeros_like(dg_ref)

        dh = None
        for c in range(N_CHIPS):
            for p, a0, a1, u0, u1 in segs[c]:
                part = _dot_nt(p_refs[p][:, a0:a1].astype(BF16), w_ref[c, :, u0:u1])
                dh = part if dh is None else dh + part
        xv = x_ref[...]
        rstd = lax.rsqrt(jnp.mean(xv * xv, axis=-1, keepdims=True) + EPS)
        nx = xv * rstd
        dhg = dh * g_ref[...]
        gx_ref[...] = dyx_ref[...] + rstd * (dhg - nx * jnp.mean(dhg * nx, axis=-1, keepdims=True))
        dg_ref[pl.ds(0, 1), :] += jnp.sum(dh * nx, axis=0, keepdims=True)

    tile = pl.BlockSpec((tm, D_MODEL), lambda i: (i, 0))
    return pl.pallas_call(
        body,
        name="dh_bwd",
        grid=(T // tm,),
        in_specs=[pl.BlockSpec((tm, w), lambda i: (i, 0)) for w in DPROJ_WIDTHS] + [
            ANY, tile, tile,
            pl.BlockSpec((1, D_MODEL), lambda i: (0, 0)),
            pl.BlockSpec((8, 128), lambda i: (0, 0)),
        ],
        out_specs=[tile, pl.BlockSpec((8, D_MODEL), lambda i: (0, 0))],
        out_shape=[_sds((T, D_MODEL), F32), _sds((8, D_MODEL), F32)],
        scratch_shapes=[pltpu.VMEM(w_in_g.shape, BF16)],
        compiler_params=_params(("arbitrary",), 56),
    )(*_hbm(*pieces, w_in_g, x, dyx, g_pre, token))


def _dw_in(ht, pieces):
    T = ht.shape[1]
    tk = min(512, T)
    nt = T // tk
    n = len(pieces)
    segs = _dproj_segments()

    def body(*refs):
        h_ref, p_refs, o_ref, ob_ref = refs[0], refs[1:n + 1], refs[n + 1], refs[n + 2]

        @pl.when(pl.program_id(1) == 0)
        def _():
            o_ref[...] = jnp.zeros_like(o_ref)

        for c in range(N_CHIPS):
            @pl.when(pl.program_id(0) == c)
            def _():
                cols = jnp.concatenate([p_refs[p][:, a0:a1].astype(BF16) for p, a0, a1, _, _ in segs[c]], axis=1)
                o_ref[...] += _dot(h_ref[...], cols)

        @pl.when(pl.program_id(1) == nt - 1)
        def _():
            ob_ref[...] = o_ref[...].astype(BF16)

    def piece_spec(p):
        chips = [c for c in range(N_CHIPS) if any(s[0] == p for s in segs[c])]

        def index(c, t):
            used = functools.reduce(jnp.logical_or, [c == k for k in chips])
            return (jnp.where(used, t, 0), 0)

        return pl.BlockSpec((tk, DPROJ_WIDTHS[p]), index)

    return pl.pallas_call(
        body,
        name="dw_in",
        grid=(N_CHIPS, nt),
        in_specs=[pl.BlockSpec((D_MODEL, tk), lambda c, t: (0, t))] + [piece_spec(p) for p in range(n)],
        out_specs=[pl.BlockSpec((None, D_MODEL, W_IN_SHARD), lambda c, t: (c, 0, 0))] * 2,
        out_shape=[_sds((N_CHIPS, D_MODEL, W_IN_SHARD), F32), _sds((N_CHIPS, D_MODEL, W_IN_SHARD), BF16)],
        compiler_params=_params(("parallel", "arbitrary"), 56),
    )(*_hbm(ht, *pieces))


ELEMENTWISE_TILE_BYTES = MIB


def _row_tile(rows, cols):
    if rows * cols * 4 <= ELEMENTWISE_TILE_BYTES:
        return rows
    for t in (512, 256, 128, 64, 32, 16, 8):
        if rows % t == 0 and t * cols * 4 <= ELEMENTWISE_TILE_BYTES:
            return t
    return rows


def _pair_sum(g, got, core, name):
    nch, R, C = g.shape
    h = R // 2
    tr = _row_tile(h, C)
    nt = h // tr

    def body(c_ref, g_ref, got_ref, p_ref, pb_ref):
        s = g_ref[...] + got_ref[...].astype(F32)
        p_ref[...] = s
        pb_ref[...] = s.astype(BF16)

    blk = pl.BlockSpec((None, tr, C), lambda j, i, c_ref: (j, i, 0))
    return pl.pallas_call(
        body,
        name=name,
        grid_spec=pltpu.PrefetchScalarGridSpec(
            num_scalar_prefetch=1,
            grid=(nch, nt),
            in_specs=[pl.BlockSpec((None, tr, C), lambda j, i, c_ref: (j, c_ref[0] * nt + i, 0)), blk],
            out_specs=[blk, blk],
        ),
        out_shape=[_sds((nch, h, C), F32), _sds((nch, h, C), BF16)],
        compiler_params=_params(("parallel", "parallel"), 48),
    )(core, *_hbm(g, got))


def _chip_sum(p, got, chip_core, name):
    _, h, C = p.shape
    tr = _row_tile(h, C)
    nt = h // tr

    def body(jc_ref, p_ref, g0_ref, g1_ref, g2_ref, o_ref):
        o_ref[...] = ((p_ref[...] + g0_ref[...].astype(F32)) + g1_ref[...].astype(F32)) + g2_ref[...].astype(F32)

    rel = lambda r: pl.BlockSpec((None, tr, C), lambda i, jc_ref: (r, i, 0))
    return pl.pallas_call(
        body,
        name=name,
        grid_spec=pltpu.PrefetchScalarGridSpec(
            num_scalar_prefetch=1,
            grid=(nt,),
            in_specs=[pl.BlockSpec((None, tr, C), lambda i, jc_ref: (jc_ref[0], i, 0)), rel(0), rel(1), rel(2)],
            out_specs=pl.BlockSpec((tr, C), lambda i, jc_ref: (jc_ref[1] * nt + i, 0)),
        ),
        out_shape=_sds((2 * h, C), F32),
        compiler_params=_params(("parallel",), 48),
    )(chip_core, *_hbm(p, got, got, got))


def _place_shards(shards, chip, name):
    n = len(shards)
    tiles = [_row_tile(s.shape[0], s.shape[1]) for s in shards]
    steps = max(s.shape[0] // t for s, t in zip(shards, tiles))
    tiles = [s.shape[0] // steps for s in shards]

    def body(j_ref, *refs):
        for a in range(n):
            refs[n + a][...] = refs[a][...].astype(BF16)

    return pl.pallas_call(
        body,
        name=name,
        grid_spec=pltpu.PrefetchScalarGridSpec(
            num_scalar_prefetch=1,
            grid=(steps,),
            in_specs=[pl.BlockSpec((t, s.shape[1]), lambda i, j_ref: (i, 0)) for s, t in zip(shards, tiles)],
            out_specs=[pl.BlockSpec((None, t, s.shape[1]), lambda i, j_ref: (j_ref[0], i, 0))
                       for s, t in zip(shards, tiles)],
        ),
        out_shape=[_sds((N_CHIPS,) + s.shape, BF16) for s in shards],
        compiler_params=_params(("parallel",), 48),
    )(chip, *_hbm(*shards))


def _adamw(w, g, m, v, name):
    R, C = w.shape
    tr = _row_tile(R, C)
    c1 = 1.0 - ADAM_B1 ** ADAM_STEP
    c2 = 1.0 - ADAM_B2 ** ADAM_STEP

    def body(w_ref, g_ref, m_ref, v_ref, d_ref, nm_ref, nv_ref):
        g = g_ref[...]
        nm = ADAM_B1 * m_ref[...] + (1.0 - ADAM_B1) * g
        nv = ADAM_B2 * v_ref[...] + (1.0 - ADAM_B2) * (g * g)
        nm_ref[...] = nm
        nv_ref[...] = nv
        d_ref[...] = (-ADAM_LR) * ((nm / c1) / (jnp.sqrt(nv / c2) + ADAM_EPS) + ADAM_WD * w_ref[...])

    spec = pl.BlockSpec((tr, C), lambda i: (i, 0))
    return pl.pallas_call(
        body, name=name, grid=(R // tr,), in_specs=[spec] * 4, out_specs=[spec] * 3,
        out_shape=[_sds((R, C), F32)] * 3, compiler_params=_params(("parallel",), 48),
    )(*_hbm(w, g, m, v))


def _place():
    return lax.axis_index("x"), lax.axis_index("y"), lax.axis_index("c")


def _chip_of(x, y, r):
    return (x ^ (r >> 1), y ^ (r & 1))


ANY = pl.BlockSpec(memory_space=pl.ANY)


def _gather_weights(placed, cw8):
    nbig = len(placed)
    halves = [s.shape[1] // 2 for s in placed]
    pieces = [max(1, h // 128) for h in halves]
    rows = [h // p for h, p in zip(halves, pieces)]
    order = [(a, q) for q in range(max(pieces)) for a in range(nbig) if q < pieces[a]]
    ici_sem = {(a, q, r): 3 * i + (r - 1) for i, (a, q) in enumerate(order) for r in (1, 2, 3)}
    cw_sem = {r: 3 * len(order) + (r - 1) for r in (1, 2, 3)}
    d2d_sem = {key: 3 * len(order) + 3 + k for key, k in ici_sem.items()}
    nsem = 6 * len(order) + 3

    def body(*refs):
        cw_ref, dsts, gcw_ref = refs[nbig], refs[nbig + 1:2 * nbig + 1], refs[2 * nbig + 1]
        send_sems, recv_sems = refs[2 * nbig + 2:]
        x, y, c = _place()
        j = 2 * x + y

        def piece_rows(a, q, core):
            return pl.ds(pl.multiple_of(core * halves[a] + q * rows[a], 16), rows[a])

        def ici(a, q, r):
            tx, ty = _chip_of(x, y, r)
            k = ici_sem[(a, q, r)]
            region = dsts[a].at[j, piece_rows(a, q, c), :]
            return pltpu.make_async_remote_copy(
                src_ref=region, dst_ref=region, send_sem=send_sems.at[k], recv_sem=recv_sems.at[k],
                device_id=(tx, ty, c), device_id_type=MESH)

        def ici_landed(a, q, r):
            tx, ty = _chip_of(x, y, r)
            k = ici_sem[(a, q, r)]
            region = dsts[a].at[2 * tx + ty, piece_rows(a, q, c), :]
            return pltpu.make_async_remote_copy(
                src_ref=region, dst_ref=region, send_sem=send_sems.at[k], recv_sem=recv_sems.at[k],
                device_id=(tx, ty, c), device_id_type=MESH)

        def d2d(a, q, r, core):
            tx, ty = _chip_of(x, y, r)
            k = d2d_sem[(a, q, r)]
            region = dsts[a].at[2 * tx + ty, piece_rows(a, q, core), :]
            return pltpu.make_async_remote_copy(
                src_ref=region, dst_ref=region, send_sem=send_sems.at[k], recv_sem=recv_sems.at[k],
                device_id=(x, y, 1 - c), device_id_type=MESH)

        def cw_copy(r):
            tx, ty = _chip_of(x, y, r)
            k = cw_sem[r]
            return pltpu.make_async_remote_copy(
                src_ref=cw_ref, dst_ref=gcw_ref.at[j], send_sem=send_sems.at[k], recv_sem=recv_sems.at[k],
                device_id=(tx, ty, c), device_id_type=MESH)

        def cw_landed(r):
            tx, ty = _chip_of(x, y, r)
            k = cw_sem[r]
            region = gcw_ref.at[2 * tx + ty]
            return pltpu.make_async_remote_copy(
                src_ref=region, dst_ref=region, send_sem=send_sems.at[k], recv_sem=recv_sems.at[k],
                device_id=(tx, ty, c), device_id_type=MESH)

        first = [ici(a, q, r) for (a, q) in order for r in (1, 2, 3)] + [cw_copy(r) for r in (1, 2, 3)]
        for cp in first:
            cp.start()
        passed = []
        for (a, q) in order:
            for r in (1, 2, 3):
                ici_landed(a, q, r).wait_recv()
                cp = d2d(a, q, r, c)
                cp.start()
                passed.append(cp)
        for r in (1, 2, 3):
            cw_landed(r).wait_recv()
        for (a, q) in order:
            for r in (1, 2, 3):
                d2d(a, q, r, 1 - c).wait_recv()
        for cp in first + passed:
            cp.wait_send()

    return pl.pallas_call(
        body,
        name="gather_weights",
        in_specs=[ANY] * (nbig + 1),
        out_specs=[ANY] * (nbig + 1),
        out_shape=[_sds(s.shape, s.dtype) for s in placed] + [_sds((N_CHIPS,) + cw8.shape, cw8.dtype)],
        input_output_aliases={a: a for a in range(nbig)},
        scratch_shapes=[pltpu.SemaphoreType.DMA((nsem,)), pltpu.SemaphoreType.DMA((nsem,))],
    )(*placed, cw8)


def _gather_late_start(placed, after, name):
    n = len(placed)
    halves = [s.shape[1] // 2 for s in placed]

    def body(*refs):
        g_refs = refs[0:n]
        send_sems, recv_sems, token = refs[n + 1], refs[n + 2], refs[-1]
        x, y, c = _place()
        j = 2 * x + y
        for a in range(n):
            mine = g_refs[a].at[j, pl.ds(pl.multiple_of(c * halves[a], 16), halves[a]), :]
            for r in (1, 2, 3):
                tx, ty = _chip_of(x, y, r)
                for to_core in (0, 1):
                    k = ((a * 3 + (r - 1)) * 2 + c) * 2 + to_core
                    pltpu.make_async_remote_copy(
                        src_ref=mine, dst_ref=mine, send_sem=send_sems.at[k], recv_sem=recv_sems.at[k],
                        device_id=(tx, ty, to_core), device_id_type=MESH).start()
        token[...] = jnp.zeros_like(token)

    hbm = lambda t: pltpu.HBM(t.shape, t.dtype)
    keep = lambda t: pltpu.with_memory_space_constraint(t, pltpu.HBM)
    nsem = 12 * n
    outs = pl.pallas_call(
        body,
        name=name,
        in_specs=[HBM] * n + [ANY],
        out_specs=(SEM, SEM, *[HBM] * n, pl.BlockSpec(memory_space=pltpu.VMEM)),
        out_shape=(pltpu.SemaphoreType.DMA((nsem,)), pltpu.SemaphoreType.DMA((nsem,)), *[hbm(p) for p in placed],
                   jax.ShapeDtypeStruct((8, 128), F32)),
        input_output_aliases={i: 2 + i for i in range(n)},
        compiler_params=pltpu.CompilerParams(has_side_effects=DATAFLOW),
    )(*[keep(p) for p in placed], after)
    return outs[0], outs[1], list(outs[2:2 + n]), outs[-1]


def _gather_late_wait(send_sems, recv_sems, thru, after, name):
    n = len(thru)
    halves = [s.shape[1] // 2 for s in thru]

    def body(*refs):
        g_refs = refs[0:n]
        send_sems, recv_sems = refs[n], refs[n + 1]
        x, y, c = _place()
        j = 2 * x + y
        for a in range(n):
            mine = g_refs[a].at[j, pl.ds(pl.multiple_of(c * halves[a], 16), halves[a]), :]
            for r in (1, 2, 3):
                tx, ty = _chip_of(x, y, r)
                for other in (0, 1):
                    k_out = ((a * 3 + (r - 1)) * 2 + c) * 2 + other
                    pltpu.make_async_remote_copy(
                        src_ref=mine, dst_ref=mine, send_sem=send_sems.at[k_out], recv_sem=recv_sems.at[k_out],
                        device_id=(tx, ty, other), device_id_type=MESH).wait_send()
                    k_in = ((a * 3 + (r - 1)) * 2 + other) * 2 + c
                    theirs = g_refs[a].at[2 * tx + ty, pl.ds(other * halves[a], halves[a]), :]
                    pltpu.make_async_remote_copy(
                        src_ref=theirs, dst_ref=theirs, send_sem=send_sems.at[k_in], recv_sem=recv_sems.at[k_in],
                        device_id=(tx, ty, other), device_id_type=MESH).wait_recv()

    hbm = lambda t: pltpu.HBM(t.shape, t.dtype)
    outs = pl.pallas_call(
        body,
        name=name,
        in_specs=[HBM] * n + [SEM, SEM, ANY],
        out_specs=[HBM] * n,
        out_shape=[hbm(t) for t in thru],
        input_output_aliases={i: i for i in range(n)},
        compiler_params=pltpu.CompilerParams(has_side_effects=DATAFLOW),
    )(*thru, send_sems, recv_sems, after)
    return list(outs)


D2D_PIECE_ROWS = 64


def _pair_exchange(grads, name):
    n = len(grads)
    halves = [g.shape[1] // 2 for g in grads]

    def body(*refs):
        g_refs, got_refs = refs[0:n], refs[n:2 * n]
        send_sems, recv_sems = refs[2 * n:]
        x, y, c = _place()

        def copy(a, src, dst):
            return pltpu.make_async_remote_copy(
                src_ref=src, dst_ref=dst, send_sem=send_sems.at[a], recv_sem=recv_sems.at[a],
                device_id=(x, y, 1 - c), device_id_type=MESH)

        for a in range(n):
            for jj in range(N_CHIPS):
                for q in range(halves[a] // D2D_PIECE_ROWS):
                    src_rows = pl.ds(pl.multiple_of((1 - c) * halves[a] + q * D2D_PIECE_ROWS, 16), D2D_PIECE_ROWS)
                    dst_rows = pl.ds(q * D2D_PIECE_ROWS, D2D_PIECE_ROWS)
                    copy(a, g_refs[a].at[jj, src_rows, :], got_refs[a].at[jj, dst_rows, :]).start()
        for a in range(n):
            sent = g_refs[a].at[:, pl.ds(pl.multiple_of((1 - c) * halves[a], 16), halves[a]), :]
            copy(a, sent, got_refs[a]).wait()

    return pl.pallas_call(
        body,
        name=name,
        in_specs=[ANY] * n,
        out_specs=[ANY] * n,
        out_shape=[_sds((N_CHIPS, h, g.shape[2]), g.dtype) for g, h in zip(grads, halves)],
        scratch_shapes=[pltpu.SemaphoreType.DMA((n,)), pltpu.SemaphoreType.DMA((n,))],
    )(*grads)


HBM = pl.BlockSpec(memory_space=pltpu.HBM)
SEM = pl.BlockSpec(memory_space=pltpu.SEMAPHORE)
DATAFLOW = pltpu.SideEffectType.DATAFLOW_SIDE_EFFECTING


def _chip_copy(p_refs, land_refs, send_sems, recv_sems, a, r):
    x, y, c = _place()
    tx, ty = _chip_of(x, y, r)
    k = a * 3 + (r - 1)
    return pltpu.make_async_remote_copy(
        src_ref=p_refs[a].at[2 * tx + ty], dst_ref=land_refs[a].at[r - 1],
        send_sem=send_sems.at[k], recv_sem=recv_sems.at[k], device_id=(tx, ty, c), device_id_type=MESH)


def _chip_exchange_start(psums, name):
    n = len(psums)
    lands = [lax.empty((3,) + p.shape[1:], p.dtype) for p in psums]

    def body(*refs):
        p_refs, land_refs = refs[0:n], refs[n:2 * n]
        send_sems, recv_sems, token = refs[2 * n], refs[2 * n + 1], refs[-1]
        for a in range(n):
            for r in (1, 2, 3):
                _chip_copy(p_refs, land_refs, send_sems, recv_sems, a, r).start()
        token[...] = jnp.zeros_like(token)

    hbm = lambda t: pltpu.HBM(t.shape, t.dtype)
    keep = lambda t: pltpu.with_memory_space_constraint(t, pltpu.HBM)
    outs = pl.pallas_call(
        body,
        name=name,
        in_specs=[HBM] * (2 * n),
        out_specs=(SEM, SEM, *[HBM] * (2 * n), pl.BlockSpec(memory_space=pltpu.VMEM)),
        out_shape=(pltpu.SemaphoreType.DMA((3 * n,)), pltpu.SemaphoreType.DMA((3 * n,)),
                   *[hbm(p) for p in psums], *[hbm(l) for l in lands], _sds((8, 128), F32)),
        input_output_aliases={i: 2 + i for i in range(2 * n)},
        compiler_params=pltpu.CompilerParams(has_side_effects=DATAFLOW),
    )(*[keep(p) for p in psums], *[keep(l) for l in lands])
    return outs[0], outs[1], list(outs[2:2 + n]), list(outs[2 + n:2 + 2 * n]), outs[-1]


def _chip_exchange_wait(send_sems, recv_sems, p_thru, land_thru, after, name):
    n = len(p_thru)

    def body(*refs):
        p_refs, land_refs = refs[0:n], refs[n:2 * n]
        send_sems, recv_sems = refs[2 * n], refs[2 * n + 1]
        for a in range(n):
            for r in (1, 2, 3):
                copy = _chip_copy(p_refs, land_refs, send_sems, recv_sems, a, r)
                copy.wait_send()
                copy.wait_recv()

    hbm = lambda t: pltpu.HBM(t.shape, t.dtype)
    outs = pl.pallas_call(
        body,
        name=name,
        in_specs=[HBM] * (2 * n) + [SEM, SEM, ANY],
        out_specs=[HBM] * (2 * n),
        out_shape=[hbm(p) for p in p_thru] + [hbm(l) for l in land_thru],
        input_output_aliases={i: i for i in range(2 * n)},
        compiler_params=pltpu.CompilerParams(has_side_effects=DATAFLOW),
    )(*p_thru, *land_thru, send_sems, recv_sems, after)
    return list(outs[n:2 * n])


def _pair_share(fulls):
    n = len(fulls)
    halves = [f.shape[0] // 2 for f in fulls]

    def body(*refs):
        full_refs = refs[n:2 * n]
        send_sems, recv_sems = refs[2 * n:]
        x, y, c = _place()

        def half_of(a, core):
            return full_refs[a].at[pl.ds(pl.multiple_of(core * halves[a], 8), halves[a]), :]

        def remote(a, src, dst):
            return pltpu.make_async_remote_copy(
                src_ref=src, dst_ref=dst, send_sem=send_sems.at[a], recv_sem=recv_sems.at[a],
                device_id=(x, y, 1 - c), device_id_type=MESH)

        for a in range(n):
            for q in range(halves[a] // D2D_PIECE_ROWS):
                piece = full_refs[a].at[
                    pl.ds(pl.multiple_of(c * halves[a] + q * D2D_PIECE_ROWS, 8), D2D_PIECE_ROWS), :]
                remote(a, piece, piece).start()
        for a in range(n):
            remote(a, half_of(a, c), half_of(a, c)).wait_send()
            remote(a, half_of(a, 1 - c), half_of(a, 1 - c)).wait_recv()

    return pl.pallas_call(
        body,
        name="pair_share",
        in_specs=[ANY] * n,
        out_specs=[ANY] * n,
        out_shape=[_sds(f.shape, F32) for f in fulls],
        input_output_aliases={a: a for a in range(n)},
        scratch_shapes=[pltpu.SemaphoreType.DMA((n,)), pltpu.SemaphoreType.DMA((n,))],
    )(*fulls)


def _allreduce_small(s):
    R, C = s.shape
    V = SMALL_VECTOR_ROWS

    def body(s_ref, o_ref, sib, chips_v, chips_m, send_sems, recv_sems):
        x, y, c = _place()
        j = 2 * x + y
        def to_sib(src, dst):
            return pltpu.make_async_remote_copy(
                src_ref=src, dst_ref=dst, send_sem=send_sems.at[0], recv_sem=recv_sems.at[0],
                device_id=(x, y, 1 - c), device_id_type=MESH)

        for q in range(R // 8):
            to_sib(s_ref.at[pl.ds(8 * q, 8), :], sib.at[pl.ds(8 * q, 8), :]).start()
        to_sib(s_ref, sib).wait()
        chips_v[j] = s_ref[pl.ds(0, V), :] + sib[pl.ds(0, V), :]
        chips_m[j] = (s_ref[pl.ds(V, R - V), :] + sib[pl.ds(V, R - V), :]).astype(BF16)

        def to_chip(r, k, buf, block):
            tx, ty = _chip_of(x, y, r)
            return pltpu.make_async_remote_copy(
                src_ref=buf.at[block], dst_ref=buf.at[block], send_sem=send_sems.at[k], recv_sem=recv_sems.at[k],
                device_id=(tx, ty, c), device_id_type=MESH)

        sends = [to_chip(r, 2 * r - 1 + k, buf, j) for r in (1, 2, 3) for k, buf in enumerate((chips_v, chips_m))]
        for cp in sends:
            cp.start()
        for r in (1, 2, 3):
            tx, ty = _chip_of(x, y, r)
            for k, buf in enumerate((chips_v, chips_m)):
                to_chip(r, 2 * r - 1 + k, buf, 2 * tx + ty).wait_recv()
        for cp in sends:
            cp.wait_send()
        o_ref[pl.ds(0, V), :] = (chips_v[0] + chips_v[1]) + (chips_v[2] + chips_v[3])
        o_ref[pl.ds(V, R - V), :] = (chips_m[0].astype(F32) + chips_m[1].astype(F32)) + (
            chips_m[2].astype(F32) + chips_m[3].astype(F32))

    return pl.pallas_call(
        body,
        name="allreduce_small",
        in_specs=[pl.BlockSpec(memory_space=pltpu.VMEM)],
        out_specs=pl.BlockSpec(memory_space=pltpu.VMEM),
        out_shape=jax.ShapeDtypeStruct((R, C), F32),
        scratch_shapes=[pltpu.VMEM((R, C), F32), pltpu.VMEM((N_CHIPS, V, C), F32),
                        pltpu.VMEM((N_CHIPS, R - V, C), BF16),
                        pltpu.SemaphoreType.DMA((7,)), pltpu.SemaphoreType.DMA((7,))],
    )(s)


def _block_diag(w):
    w4 = w.reshape(4, 4, RNN_BLOCK_W, RNN_BLOCK_W)
    eye = jnp.eye(4, dtype=w.dtype)
    return jnp.einsum("jaik,ab->jaibk", w4, eye).reshape(4, RNN_TILE, RNN_TILE)


def _block_diag_part(d):
    d5 = d.reshape(4, 4, RNN_BLOCK_W, 4, RNN_BLOCK_W)
    return jnp.stack([d5[:, a, :, a, :] for a in range(4)], axis=1).reshape(RNN_BLOCKS, RNN_BLOCK_W, RNN_BLOCK_W)


def _local_grads(x, target, g_pre, w_in_g, b_gate, conv_w, conv_b, w_rg_a, b_rg_a, w_rg_x, b_rg_x, lam, sinks,
                 out_weights, fwd_token, g_post, on_out_grads, on_w_in_grad):
    wa_bd = _block_diag(w_rg_a).astype(BF16)
    wx_bd = _block_diag(w_rg_x).astype(BF16)
    b_a = b_rg_a.reshape(1, D_RNN)
    b_x = b_rg_x.reshape(1, D_RNN)

    proj, ht = _proj_fwd(x, g_pre, w_in_g)
    y_rnn, z_rnn = _rnn_fwd(proj, conv_w, conv_b, wa_bd, wx_bd, b_a, b_x, lam, fwd_token)
    bias = _attn_bias()
    y_attn, z_attn, lse = _attn_fwd(proj, sinks, bias)
    w_rnn_out, w_attn_out, w_out = out_weights(z_attn)
    dyx, dz_rnn, dz_attn, dml, merged, dout, dbr_rnn, dbr_attn, head_small = _head(
        x, target, z_rnn, z_attn, proj, b_gate, g_post, w_rnn_out, w_attn_out, w_out)
    out_grads = [_matmul_tn(z_rnn, dbr_rnn, "dw_rnn_out"), _matmul_tn(z_attn, dbr_attn, "dw_attn_out"),
                 _matmul_tn(merged, dout, "dw_out")]
    shard_rows = lambda d: d.reshape(N_CHIPS, OUT_SHARD, D_MODEL)
    token = on_out_grads([shard_rows(g) for g, _ in out_grads], [shard_rows(gb) for _, gb in out_grads])
    dq, dk, dv, dag, attn_small = _attn_bwd(proj, y_attn, lse, dz_attn, sinks, bias, token)
    drx, drg, dwa_t, dwx_t, rnn_small = _rnn_bwd(proj, y_rnn, dz_rnn, conv_w, conv_b, wa_bd, wx_bd, b_a, b_x, lam)
    dproj = [drx, drg, dq, dk, dv, dag, dml]
    token = on_w_in_grad(*_dw_in(ht, dproj))
    grad_x, dh_small = _dh_bwd(dproj, w_in_g, x, dyx, g_pre, token)
    small = jnp.concatenate([rnn_small, head_small, dh_small + attn_small,
                             _block_diag_part(dwa_t).reshape(64, 1024), _block_diag_part(dwx_t).reshape(64, 1024)], axis=0)
    return grad_x, small


ROW_LOSS = 11


def _rows8(parts):
    out = None
    for r, a in parts:
        p = jnp.pad(a, ((r, 8 - r - a.shape[0]), (0, 1024 - a.shape[1])))
        out = p if out is None else out + p
    return out


def _pack_small(p):
    g0 = _rows8([(0, p["b_rg_a"].reshape(1, 1024)), (1, p["b_rg_x"].reshape(1, 1024)), (2, p["lru_lambda"]),
                 (3, p["conv_b"]), (4, p["conv_w"][0])])
    g1 = _rows8([(0, p["post_norm_g"]), (1, p["b_gate"].reshape(2, 1024))])
    g2 = _rows8([(0, p["pre_norm_g"]), (1, p["attn_sinks"])])
    return jnp.concatenate([g0, g1, g2, p["w_rg_a"].reshape(64, 1024), p["w_rg_x"].reshape(64, 1024)], axis=0)


def _unpack_small(s, conv_cols):
    return {
        "b_rg_a": s[0:1].reshape(1, 16, 64), "b_rg_x": s[1:2].reshape(1, 16, 64), "lru_lambda": s[2:3],
        "conv_b": s[3:4], "conv_w": s[4:8, 0:conv_cols].reshape(1, CONV_W, conv_cols),
        "post_norm_g": s[8:9], "b_gate": s[9:11].reshape(1, 2048),
        "pre_norm_g": s[16:17], "attn_sinks": s[17:18, 0:N_Q_HEADS],
        "w_rg_a": s[24:88].reshape(1, 16, 64, 64), "w_rg_x": s[88:152].reshape(1, 16, 64, 64),
    }


WEIGHTS = ["pre_norm_g", "w_in", "b_gate", "conv_w", "conv_b", "w_rg_a", "b_rg_a", "w_rg_x", "b_rg_x", "lru_lambda",
           "attn_sinks", "w_rnn_out", "w_attn_out", "w_out", "post_norm_g"]
BIG = ["w_in", "w_rnn_out", "w_attn_out", "w_out"]


def kernel(x, pre_norm_g, w_in, b_gate, conv_w, conv_b, w_rg_a, b_rg_a, w_rg_x, b_rg_x, lru_lambda, attn_sinks, w_rnn_out, w_attn_out, w_out, post_norm_g, loss_target, m_pre_norm_g, m_w_in, m_b_gate, m_conv_w, m_conv_b, m_w_rg_a, m_b_rg_a, m_w_rg_x, m_b_rg_x, m_lru_lambda, m_attn_sinks, m_w_rnn_out, m_w_attn_out, m_w_out, m_post_norm_g, v_pre_norm_g, v_w_in, v_b_gate, v_conv_w, v_conv_b, v_w_rg_a, v_b_rg_a, v_w_rg_x, v_b_rg_x, v_lru_lambda, v_attn_sinks, v_w_rnn_out, v_w_attn_out, v_w_out, v_post_norm_g):
    w = dict(pre_norm_g=pre_norm_g, w_in=w_in, b_gate=b_gate, conv_w=conv_w, conv_b=conv_b, w_rg_a=w_rg_a,
             b_rg_a=b_rg_a, w_rg_x=w_rg_x, b_rg_x=b_rg_x, lru_lambda=lru_lambda, attn_sinks=attn_sinks,
             w_rnn_out=w_rnn_out, w_attn_out=w_attn_out, w_out=w_out, post_norm_g=post_norm_g)
    m = dict(pre_norm_g=m_pre_norm_g, w_in=m_w_in, b_gate=m_b_gate, conv_w=m_conv_w, conv_b=m_conv_b, w_rg_a=m_w_rg_a,
             b_rg_a=m_b_rg_a, w_rg_x=m_w_rg_x, b_rg_x=m_b_rg_x, lru_lambda=m_lru_lambda, attn_sinks=m_attn_sinks,
             w_rnn_out=m_w_rnn_out, w_attn_out=m_w_attn_out, w_out=m_w_out, post_norm_g=m_post_norm_g)
    v = dict(pre_norm_g=v_pre_norm_g, w_in=v_w_in, b_gate=v_b_gate, conv_w=v_conv_w, conv_b=v_conv_b, w_rg_a=v_w_rg_a,
             b_rg_a=v_b_rg_a, w_rg_x=v_w_rg_x, b_rg_x=v_b_rg_x, lru_lambda=v_lru_lambda, attn_sinks=v_attn_sinks,
             w_rnn_out=v_w_rnn_out, w_attn_out=v_w_attn_out, w_out=v_w_out, post_norm_g=v_post_norm_g)
    chip = 2 * lax.axis_index("x") + lax.axis_index("y")

    chip_idx = chip.astype(jnp.int32).reshape(1)
    chip_core = jnp.stack([chip, lax.axis_index("c")]).astype(jnp.int32)
    cw8 = jnp.pad(conv_w[0], ((0, 8 - CONV_W), (0, 0)))
    placed = _place_shards([w_in[0], w_rnn_out[0], w_attn_out[0], w_out[0]], chip_idx, "place_shards")
    win_g, cw_g = _gather_weights(placed[:1], cw8)
    late_send, late_recv, late_thru, late_token = _gather_late_start(placed[1:], win_g, "gather_late_start")
    cw_g = lax.dynamic_update_slice_in_dim(cw_g, cw8[None], chip, axis=0)
    conv_w_full = jnp.transpose(cw_g[:, 0:CONV_W, :], (1, 0, 2)).reshape(CONV_W, D_RNN)

    core_idx = lax.axis_index("c").astype(jnp.int32).reshape(1)
    started = {}

    def start_reduction(tag, grads, grads_b16):
        got = _pair_exchange(grads_b16, "pair_exchange_" + tag)
        sums = [_pair_sum(g, o, core_idx, "pair_sum_%s_%d" % (tag, a)) for a, (g, o) in enumerate(zip(grads, got))]
        send_sems, recv_sems, p_thru, land_thru, token = _chip_exchange_start(
            [pb for _, pb in sums], "chip_exchange_start_" + tag)
        started[tag] = ([p for p, _ in sums], send_sems, recv_sems, p_thru, land_thru)
        return token

    def end_reduction(tag, after):
        psums, send_sems, recv_sems, p_thru, land_thru = started[tag]
        landed = _chip_exchange_wait(send_sems, recv_sems, p_thru, land_thru, after, "chip_exchange_wait_" + tag)
        return [_chip_sum(p, l, chip_core, "chip_sum_%s_%d" % (tag, a)) for a, (p, l) in enumerate(zip(psums, landed))]

    def out_weights(after):
        gathered = _gather_late_wait(late_send, late_recv, late_thru, after, "gather_late_wait")
        return [g.reshape(D_MODEL, D_MODEL) for g in gathered]

    grad_x, small = _local_grads(
        x[0], loss_target[0], pre_norm_g, win_g, b_gate, conv_w_full, conv_b, w_rg_a[0], b_rg_a[0], w_rg_x[0],
        b_rg_x[0], lru_lambda, attn_sinks[0], out_weights, late_token, post_norm_g,
        on_out_grads=lambda grads, grads_b16: start_reduction("out", grads, grads_b16),
        on_w_in_grad=lambda grad, grad_b16: start_reduction("in", [grad], [grad_b16]))

    halves = end_reduction("in", grad_x) + end_reduction("out", grad_x)
    gbig = dict(zip(BIG, _pair_share(halves)))

    small_sum = _allreduce_small(small)
    total_loss = small_sum[ROW_LOSS, 0]
    gsmall = _unpack_small(small_sum, D_RNN)
    conv_shard = D_RNN // N_CHIPS
    gsmall["conv_w"] = lax.dynamic_slice_in_dim(gsmall["conv_w"], chip * conv_shard, conv_shard, axis=2)

    grads, delta, new_m, new_v = {}, {}, {}, {}
    for n in BIG:
        grads[n] = gbig[n][None]
        d, nm, nv = _adamw(w[n][0], gbig[n], m[n][0], v[n][0], "adamw_" + n)
        delta[n], new_m[n], new_v[n] = d[None], nm[None], nv[None]
    pick = lambda t: {k: t[k] for k in gsmall}
    d, nm, nv = _adamw(_pack_small(pick(w)), _pack_small(gsmall), _pack_small(pick(m)), _pack_small(pick(v)),
                       "adamw_small")
    ud, um, uv = _unpack_small(d, conv_shard), _unpack_small(nm, conv_shard), _unpack_small(nv, conv_shard)
    for n in gsmall:
        grads[n] = gsmall[n].reshape(w[n].shape)
        delta[n] = ud[n].reshape(w[n].shape)
        new_m[n] = um[n].reshape(w[n].shape)
        new_v[n] = uv[n].reshape(w[n].shape)

    return (total_loss, grad_x[None], *[grads[n] for n in WEIGHTS], *[delta[n] for n in WEIGHTS],
            *[new_m[n] for n in WEIGHTS], *[new_v[n] for n in WEIGHTS])
```

```python
import functools
import math

import jax
import jax.numpy as jnp
from jax import lax
from jax.experimental import pallas as pl
from jax.experimental.pallas import tpu as pltpu

F32 = jnp.float32
BF16 = jnp.bfloat16

D_MODEL = 1024
D_RNN = 1024
RNN_BLOCKS = 16
RNN_BLOCK_W = 64
CONV_W = 4
LRU_C = 8.0
N_Q_HEADS = 16
N_KV_HEADS = 4
GROUP = 4
HEAD_DIM = 64
D_KV = 256
BLOCK = 128
ALIBI_MAX_BIAS = 8.0
EPS = 1e-6
D_IN = 6656
N_CHIPS = 4
W_IN_SHARD = D_IN // N_CHIPS
OUT_SHARD = D_MODEL // N_CHIPS
ADAM_LR = 0.001
ADAM_B1 = 0.9
ADAM_B2 = 0.999
ADAM_EPS = 1e-08
ADAM_WD = 0.01
ADAM_STEP = 10
NEG_BIG = -1e30
MIB = 1 << 20

COL_RNN_X = 0
COL_RNN_GATE = 4
COL_Q = 8
COL_K = 12
COL_V = 13
COL_ATTN_GATE = 14
COL_MERGE = 18

RNN_TILE = 256
RNN_CHUNK = 512
SMALL_ROWS = 152
SMALL_VECTOR_ROWS = 24
MESH = pl.DeviceIdType.MESH


def _sds(shape, dtype):
    return pltpu.HBM(shape, dtype)


def _params(sem=None, vmem_mib=None):
    kw = {}
    if sem is not None:
        kw["dimension_semantics"] = sem
    if vmem_mib is not None:
        kw["vmem_limit_bytes"] = vmem_mib * MIB
    return pltpu.CompilerParams(**kw)


def _hbm(*arrays):
    return [pltpu.with_memory_space_constraint(a, pltpu.HBM) for a in arrays]


def _dot(a, b):
    return jnp.dot(a, b, preferred_element_type=F32)


def _dot_nt(a, b):
    return lax.dot_general(a, b, (((1,), (1,)), ((), ())), preferred_element_type=F32)


def _dot_tn(a, b):
    return lax.dot_general(a, b, (((0,), (0,)), ((), ())), preferred_element_type=F32)


def _sigmoid(x):
    return 0.5 * jnp.tanh(0.5 * x) + 0.5


def _sigmoid_small(x):
    return 1.0 / (1.0 + jnp.exp(-x))


def _softplus(x):
    return jnp.maximum(x, 0.0) + jnp.log(1.0 + jnp.exp(-jnp.abs(x)))


def _one_minus_square(a, log_a):
    return -jnp.tanh(log_a) * (a * a + 1.0)


def _proj_fwd(x, g_pre, w_in_g):
    T = x.shape[0]
    tm = min(1024, T)

    def body(x_ref, g_ref, w_ref, proj_ref, ht_ref, h_s):
        @pl.when(pl.program_id(1) == 0)
        def _():
            xv = x_ref[...]
            rstd = lax.rsqrt(jnp.mean(xv * xv, axis=-1, keepdims=True) + EPS)
            hf = (xv * rstd) * g_ref[...]
            h_s[...] = hf.astype(BF16)
            ht_ref[...] = hf.T.astype(BF16)

        proj_ref[...] = _dot(h_s[...], w_ref[...]).astype(BF16)

    return pl.pallas_call(
        body,
        name="proj_fwd",
        grid=(T // tm, N_CHIPS),
        in_specs=[
            pl.BlockSpec((tm, D_MODEL), lambda i, j: (i, 0)),
            pl.BlockSpec((1, D_MODEL), lambda i, j: (0, 0)),
            pl.BlockSpec((None, D_MODEL, W_IN_SHARD), lambda i, j: (j, 0, 0)),
        ],
        out_specs=[
            pl.BlockSpec((tm, W_IN_SHARD), lambda i, j: (i, j)),
            pl.BlockSpec((D_MODEL, tm), lambda i, j: (0, i)),
        ],
        out_shape=[_sds((T, D_IN), BF16), _sds((D_MODEL, T), BF16)],
        scratch_shapes=[pltpu.VMEM((tm, D_MODEL), BF16)],
        compiler_params=_params(("parallel", "arbitrary"), 48),
    )(*_hbm(x, g_pre, w_in_g))


def _shift_down(x, tail, s, row):
    n = x.shape[0]
    xs = pltpu.roll(x, s, 0)
    tail_t = jnp.tile(pltpu.roll(tail, s, 0), (n // 8, 1))
    return jnp.where(row < s, tail_t, xs)


def _shift_up(x, head, s, row):
    n = x.shape[0]
    xs = pltpu.roll(x, n - s, 0)
    head_t = jnp.tile(pltpu.roll(head, 8 - s, 0), (n // 8, 1))
    return jnp.where(row >= n - s, head_t, xs)


def _conv_taps(x, tail, row):
    return [_shift_down(x, tail, 3, row), _shift_down(x, tail, 2, row), _shift_down(x, tail, 1, row), x]


def _rglru_gates(c, wa, wx, ba, bx, lam):
    cb = c.astype(BF16)
    r = _sigmoid_small(_dot(cb, wa) + ba)
    i = _sigmoid(_dot(cb, wx) + bx)
    log_a = (-LRU_C) * r * _softplus(-lam)
    a = jnp.exp(log_a)
    mult = jnp.sqrt(_one_minus_square(a, log_a))
    return cb, r, i, a, mult


SUBLANES = 8


def _scan_down(a, u, row):
    n = a.shape[0]
    s = 1
    while s < SUBLANES:
        a_sh = jnp.where(row >= s, pltpu.roll(a, s, 0), 1.0)
        u_sh = jnp.where(row >= s, pltpu.roll(u, s, 0), 0.0)
        u = a * u_sh + u
        a = a * a_sh
        s *= 2
    while s < n:
        u = jnp.concatenate([u[:s], a[s:] * u[:n - s] + u[s:]], axis=0)
        a = jnp.concatenate([a[:s], a[s:] * a[:n - s]], axis=0)
        s *= 2
    return a, u


def _scan_up(b, u, row):
    n = b.shape[0]
    s = 1
    while s < SUBLANES:
        b_sh = jnp.where(row < n - s, pltpu.roll(b, n - s, 0), 1.0)
        u_sh = jnp.where(row < n - s, pltpu.roll(u, n - s, 0), 0.0)
        u = b * u_sh + u
        b = b * b_sh
        s *= 2
    while s < n:
        u = jnp.concatenate([b[:n - s] * u[s:] + u[:n - s], u[n - s:]], axis=0)
        b = jnp.concatenate([b[:n - s] * b[s:], b[n - s:]], axis=0)
        s *= 2
    return b, u


LANES = 128


def _chunk_scan(a, u, a_s, u_s, hl_s, al_s, carry, reverse):
    n, width = a.shape
    groups = n // SUBLANES
    order = range(SUBLANES - 1, -1, -1) if reverse else range(SUBLANES)
    row = lax.broadcasted_iota(jnp.int32, (groups, LANES), 0)
    for l in range(width // LANES):
        lanes = slice(l * LANES, (l + 1) * LANES)
        a_l, u_l, hl_l, al_l = a_s.at[l], u_s.at[l], hl_s.at[l], al_s.at[l]
        a_l[...] = a[:, lanes]
        u_l[...] = u[:, lanes]
        h_loc = a_loc = None
        for r in order:
            rows = pl.ds(r, groups, stride=SUBLANES)
            a_r, u_r = a_l[rows, :], u_l[rows, :]
            h_loc, a_loc = (u_r, a_r) if h_loc is None else (a_r * h_loc + u_r, a_r * a_loc)
            hl_l[rows, :] = h_loc
            al_l[rows, :] = a_loc
        if reverse:
            a_cum, ends = _scan_up(a_loc, h_loc, row)
            ends = ends + a_cum * carry[:, lanes]
            enters = jnp.where(row == groups - 1, carry[:, lanes], pltpu.roll(ends, groups - 1, 0))
        else:
            a_cum, ends = _scan_down(a_loc, h_loc, row)
            ends = ends + a_cum * carry[:, lanes]
            enters = jnp.where(row == 0, carry[:, lanes], pltpu.roll(ends, 1, 0))
        for r in range(SUBLANES):
            rows = pl.ds(r, groups, stride=SUBLANES)
            hl_l[rows, :] = hl_l[rows, :] + al_l[rows, :] * enters
    return jnp.concatenate([hl_s[l] for l in range(width // LANES)], axis=1)


def _rnn_fwd(proj, conv_w, conv_b, wa_bd, wx_bd, b_a, b_x, lam, token):
    T = proj.shape[0]
    tc, ct = RNN_CHUNK, RNN_TILE
    nt = T // tc

    def body(x_ref, rg_ref, cw_ref, cb_ref, wa_ref, wx_ref, ba_ref, bx_ref, lam_ref, token_ref, h_ref, z_ref, xtail,
             hcarry, a_s, u_s, hl_s, al_s):
        @pl.when(pl.program_id(1) == 0)
        def _():
            xtail[...] = jnp.zeros_like(xtail)
            hcarry[...] = jnp.zeros_like(hcarry)

        row = lax.broadcasted_iota(jnp.int32, (tc, ct), 0)
        x = x_ref[...].astype(F32)
        taps = _conv_taps(x, xtail[...], row)
        c = cb_ref[...] + cw_ref[pl.ds(0, 1), :] * taps[0]
        for k in range(1, CONV_W):
            c = c + cw_ref[pl.ds(k, 1), :] * taps[k]
        xtail[...] = x[tc - 8:, :]
        _, _, i, a, mult = _rglru_gates(c, wa_ref[...], wx_ref[...], ba_ref[...], bx_ref[...], lam_ref[...])
        h = _chunk_scan(a, mult * (i * c), a_s, u_s, hl_s, al_s, hcarry[...], reverse=False)
        h_ref[...] = h
        hcarry[...] = h_ref[pl.ds(tc - 1, 1), :]
        rg = rg_ref[...].astype(F32)
        z_ref[...] = (h * (rg * _sigmoid(rg))).astype(BF16)

    col = lambda off: (lambda j, t: (t, off + j))
    vec = pl.BlockSpec((1, ct), lambda j, t: (0, j))
    mat = pl.BlockSpec((None, ct, ct), lambda j, t: (j, 0, 0))
    return pl.pallas_call(
        body,
        name="rnn_fwd",
        grid=(D_RNN // ct, nt),
        in_specs=[
            pl.BlockSpec((tc, ct), col(COL_RNN_X)),
            pl.BlockSpec((tc, ct), col(COL_RNN_GATE)),
            pl.BlockSpec((CONV_W, ct), lambda j, t: (0, j)),
            vec, mat, mat, vec, vec, vec,
            pl.BlockSpec((8, 128), lambda j, t: (0, 0)),
        ],
        out_specs=[pl.BlockSpec((tc, ct), lambda j, t: (t, j)), pl.BlockSpec((tc, ct), lambda j, t: (t, j))],
        out_shape=[_sds((T, D_RNN), F32), _sds((T, D_RNN), BF16)],
        scratch_shapes=[pltpu.VMEM((8, ct), F32), pltpu.VMEM((1, ct), F32)] + [
            pltpu.VMEM((ct // LANES, tc, LANES), F32)] * 4,
        compiler_params=_params(("parallel", "arbitrary"), 32),
    )(*_hbm(proj, proj, conv_w, conv_b, wa_bd, wx_bd, b_a, b_x, lam, token))


def _rnn_bwd(proj, y_rnn, dz_rnn, conv_w, conv_b, wa_bd, wx_bd, b_a, b_x, lam):
    T = proj.shape[0]
    tc, ct = RNN_CHUNK, RNN_TILE
    nt = T // tc
    hb = tc // 8

    def body(x_ref, xh_ref, rg_ref, h_ref, hh_ref, dz_ref, cw_ref, cb_ref, wa_ref, wx_ref, ba_ref, bx_ref, lam_ref,
             dx_ref, drg_ref, dwa_ref, dwx_ref, sm_ref, lam_carry, a_carry, dc_head, b_s, dy_s, hl_s, al_s):
        t = pl.program_id(1)
        first_chunk = t == nt - 1

        @pl.when(t == 0)
        def _():
            lam_carry[...] = jnp.zeros_like(lam_carry)
            a_carry[...] = jnp.zeros_like(a_carry)
            dc_head[...] = jnp.zeros_like(dc_head)
            dwa_ref[...] = jnp.zeros_like(dwa_ref)
            dwx_ref[...] = jnp.zeros_like(dwx_ref)
            sm_ref[...] = jnp.zeros_like(sm_ref)

        row = lax.broadcasted_iota(jnp.int32, (tc, ct), 0)
        keep = jnp.where(first_chunk, 0.0, 1.0)
        x = x_ref[...].astype(F32)
        xtail = xh_ref[...].astype(F32)[8:16, :] * keep
        taps = _conv_taps(x, xtail, row)
        c = cb_ref[...] + cw_ref[pl.ds(0, 1), :] * taps[0]
        for k in range(1, CONV_W):
            c = c + cw_ref[pl.ds(k, 1), :] * taps[k]
        lam = lam_ref[...]
        cb, r, i, a, mult = _rglru_gates(c, wa_ref[...], wx_ref[...], ba_ref[...], bx_ref[...], lam)
        h = h_ref[...]
        h_prev = _shift_down(h, hh_ref[...] * keep, 1, row)
        rg = rg_ref[...].astype(F32)
        dz = dz_ref[...]
        sg = _sigmoid(rg)
        drg_ref[...] = (dz * h * (sg * (1.0 + rg * (1.0 - sg)))).astype(BF16)
        dy = dz * (rg * sg)
        b = jnp.where(row >= tc - 1, a_carry[pl.ds(0, 1), :], pltpu.roll(a, tc - 1, 0))
        lt = _chunk_scan(b, dy, b_s, dy_s, hl_s, al_s, lam_carry[pl.ds(0, 1), :], reverse=True)
        lam_carry[...] = lt[0:8, :]
        a_carry[...] = a[0:8, :]
        ic = i * c
        dmult = lt * ic
        di = lt * mult * c
        dc = lt * mult * i
        dlog_a = a * (lt * h_prev - dmult * a / mult)
        sp = _softplus(-lam)
        dpre_r = dlog_a * ((-LRU_C) * sp) * (r * (1.0 - r))
        dpre_i = di * (i * (1.0 - i))
        dlam_row = jnp.sum(dlog_a * r, axis=0, keepdims=True) * (LRU_C * _sigmoid(-lam))
        dpr_b = dpre_r.astype(BF16)
        dpi_b = dpre_i.astype(BF16)
        dwa_ref[...] += _dot_tn(cb, dpr_b)
        dwx_ref[...] += _dot_tn(cb, dpi_b)
        dc = dc + _dot_nt(dpr_b, wa_ref[...]) + _dot_nt(dpi_b, wx_ref[...])
        head = dc_head[...]
        dx = cw_ref[pl.ds(3, 1), :] * dc
        for m in range(1, CONV_W):
            dx = dx + cw_ref[pl.ds(3 - m, 1), :] * _shift_up(dc, head, m, row)
        dx_ref[...] = dx.astype(BF16)
        dc_head[...] = dc[0:8, :]
        sm_ref[pl.ds(0, 1), :] += jnp.sum(dpre_r, axis=0, keepdims=True)
        sm_ref[pl.ds(1, 1), :] += jnp.sum(dpre_i, axis=0, keepdims=True)
        sm_ref[pl.ds(2, 1), :] += dlam_row
        sm_ref[pl.ds(3, 1), :] += jnp.sum(dc, axis=0, keepdims=True)
        for k in range(CONV_W):
            sm_ref[pl.ds(4 + k, 1), :] += jnp.sum(dc * taps[k], axis=0, keepdims=True)

    rev = lambda off: (lambda j, t: (nt - 1 - t, off + j))
    halo = lambda off: (lambda j, t: (jnp.maximum((nt - 1 - t) * hb - 1, 0), off + j))
    halo16 = lambda off: (lambda j, t: (jnp.maximum((nt - 1 - t) * (hb // 2) - 1, 0), off + j))
    vec = pl.BlockSpec((1, ct), lambda j, t: (0, j))
    mat = pl.BlockSpec((None, ct, ct), lambda j, t: (j, 0, 0))
    return pl.pallas_call(
        body,
        name="rnn_bwd",
        grid=(D_RNN // ct, nt),
        in_specs=[
            pl.BlockSpec((tc, ct), rev(COL_RNN_X)),
            pl.BlockSpec((16, ct), halo16(COL_RNN_X)),
            pl.BlockSpec((tc, ct), rev(COL_RNN_GATE)),
            pl.BlockSpec((tc, ct), rev(0)),
            pl.BlockSpec((8, ct), halo(0)),
            pl.BlockSpec((tc, ct), rev(0)),
            pl.BlockSpec((CONV_W, ct), lambda j, t: (0, j)),
            vec, mat, mat, vec, vec, vec,
        ],
        out_specs=[
            pl.BlockSpec((tc, ct), rev(0)),
            pl.BlockSpec((tc, ct), rev(0)),
            mat, mat,
            pl.BlockSpec((8, ct), lambda j, t: (0, j)),
        ],
        out_shape=[_sds((T, D_RNN), BF16), _sds((T, D_RNN), BF16), _sds((D_RNN // ct, ct, ct), F32),
                   _sds((D_RNN // ct, ct, ct), F32), _sds((8, D_RNN), F32)],
        scratch_shapes=[pltpu.VMEM((8, ct), F32)] * 3 + [pltpu.VMEM((ct // LANES, tc, LANES), F32)] * 4,
        compiler_params=_params(("parallel", "arbitrary"), 32),
    )(*_hbm(proj, proj, proj, y_rnn, y_rnn, dz_rnn, conv_w, conv_b, wa_bd, wx_bd, b_a, b_x, lam))


def _attn_bias():
    qi = jnp.arange(BLOCK)[:, None]
    kj = jnp.arange(BLOCK)[None, :]
    dist_cur = (qi - kj).astype(F32)
    slopes = 2.0 ** (-ALIBI_MAX_BIAS * jnp.arange(1, N_Q_HEADS + 1, dtype=F32) / N_Q_HEADS)
    slopes = slopes[:, None, None]
    prev = jnp.where(kj > qi, -slopes * (dist_cur + float(BLOCK)), NEG_BIG)
    cur = jnp.where(kj <= qi, -slopes * dist_cur, NEG_BIG)
    later = jnp.concatenate([prev, cur], axis=-1)
    first = jnp.concatenate([jnp.full_like(prev, NEG_BIG), cur], axis=-1)
    return jnp.stack([first, later])


def _attn_exps(s_prev, s_cur, sink, bias):
    s_prev = s_prev + bias[:, 0:BLOCK]
    s_cur = s_cur + bias[:, BLOCK:2 * BLOCK]
    m = jnp.maximum(jnp.max(jnp.maximum(s_prev, s_cur), axis=-1, keepdims=True), sink)
    p_prev = jnp.exp(s_prev - m)
    p_cur = jnp.exp(s_cur - m)
    total = jnp.sum(p_prev + p_cur, axis=-1, keepdims=True) + jnp.exp(sink - m)
    return p_prev, p_cur, 1.0 / total, m + jnp.log(total)


def _attn_probs(s_prev, s_cur, sink, bias, lse):
    p_prev = jnp.exp((s_prev + bias[:, 0:BLOCK]) - lse)
    p_cur = jnp.exp((s_cur + bias[:, BLOCK:2 * BLOCK]) - lse)
    return p_prev, p_cur, jnp.exp(sink - lse)


def _stack_heads(ref_or_val, hk, dtype):
    parts = [ref_or_val[:, (GROUP * hk + g) * HEAD_DIM:(GROUP * hk + g + 1) * HEAD_DIM] for g in range(GROUP)]
    return jnp.concatenate(parts, axis=0).astype(dtype)


ATTN_SCALE = HEAD_DIM ** -0.5


def _bias_spec():
    return pl.BlockSpec((None, N_Q_HEADS, BLOCK, 2 * BLOCK), lambda i: (jnp.minimum(i, 1), 0, 0, 0))


def _attn_fwd(proj, sinks, bias):
    T = proj.shape[0]
    nb = T // BLOCK

    def body(sink_ref, bias_ref, q_ref, kp_ref, kc_ref, vp_ref, vc_ref, ag0_ref, ag1_ref, y_ref, z_ref, lse_ref):
        kvs = [slice(hk * HEAD_DIM, (hk + 1) * HEAD_DIM) for hk in range(N_KV_HEADS)]
        qgs = [(_stack_heads(q_ref, hk, F32) * ATTN_SCALE).astype(BF16) for hk in range(N_KV_HEADS)]
        s_prev = [_dot_nt(qgs[hk], kp_ref[:, kvs[hk]].astype(BF16)) for hk in range(N_KV_HEADS)]
        s_cur = [_dot_nt(qgs[hk], kc_ref[:, kvs[hk]].astype(BF16)) for hk in range(N_KV_HEADS)]
        for hk in range(N_KV_HEADS):
            pp, pc, invs = [], [], []
            for g in range(GROUP):
                h = GROUP * hk + g
                rows = slice(g * BLOCK, (g + 1) * BLOCK)
                p_prev, p_cur, inv, lse = _attn_exps(s_prev[hk][rows], s_cur[hk][rows], sink_ref[h], bias_ref[h])
                pp.append(p_prev.astype(BF16))
                pc.append(p_cur.astype(BF16))
                invs.append(inv)
                lse_ref[:, h:h + 1] = lse
            og = _dot(jnp.concatenate(pp, axis=0), vp_ref[:, kvs[hk]].astype(BF16)) + _dot(
                jnp.concatenate(pc, axis=0), vc_ref[:, kvs[hk]].astype(BF16))
            for g in range(GROUP):
                h = GROUP * hk + g
                y_ref[:, h * HEAD_DIM:(h + 1) * HEAD_DIM] = og[g * BLOCK:(g + 1) * BLOCK] * invs[g]
        ag = jnp.concatenate([ag0_ref[...], ag1_ref[...]], axis=1).astype(F32)
        z_ref[...] = (y_ref[...] * (ag * _sigmoid(ag))).astype(BF16)

    prev = lambda c: (lambda i: (jnp.maximum(i - 1, 0), c))
    cur = lambda c: (lambda i: (i, c))
    return pl.pallas_call(
        body,
        name="attn_fwd",
        grid=(nb,),
        in_specs=[
            pl.BlockSpec(memory_space=pltpu.SMEM),
            _bias_spec(),
            pl.BlockSpec((BLOCK, 1024), lambda i: (i, COL_Q // 4)),
            pl.BlockSpec((BLOCK, D_KV), prev(COL_K)),
            pl.BlockSpec((BLOCK, D_KV), cur(COL_K)),
            pl.BlockSpec((BLOCK, D_KV), prev(COL_V)),
            pl.BlockSpec((BLOCK, D_KV), cur(COL_V)),
            pl.BlockSpec((BLOCK, 512), lambda i: (i, COL_ATTN_GATE // 2)),
            pl.BlockSpec((BLOCK, 512), lambda i: (i, COL_ATTN_GATE // 2 + 1)),
        ],
        out_specs=[pl.BlockSpec((BLOCK, 1024), lambda i: (i, 0)), pl.BlockSpec((BLOCK, 1024), lambda i: (i, 0)),
                   pl.BlockSpec((BLOCK, N_Q_HEADS), lambda i: (i, 0))],
        out_shape=[_sds((T, 1024), F32), _sds((T, 1024), BF16), _sds((T, N_Q_HEADS), F32)],
        compiler_params=_params(("arbitrary",), 32),
    )(sinks, *_hbm(bias, proj, proj, proj, proj, proj, proj, proj))


def _attn_bwd(proj, y_attn, lse, dz_attn, sinks, bias, token):
    T = proj.shape[0]
    nb = T // BLOCK

    def body(sink_ref, bias_ref, q_ref, kp_ref, kc_ref, vp_ref, vc_ref, ag0_ref, ag1_ref, y_ref, lse_ref, dz_ref,
             token_ref, dq_ref, dk_ref, dv_ref, dag_ref, ds_ref, dy_s):
        i = pl.program_id(0)

        @pl.when(i == 0)
        def _():
            ds_ref[...] = jnp.zeros_like(ds_ref)

        lane = lax.broadcasted_iota(jnp.int32, (8, 128), 1)
        sub = lax.broadcasted_iota(jnp.int32, (8, 128), 0)
        ag = jnp.concatenate([ag0_ref[...], ag1_ref[...]], axis=1).astype(F32)
        dz = dz_ref[...]
        sg = _sigmoid(ag)
        dag_ref[...] = (dz * y_ref[...] * (sg * (1.0 + ag * (1.0 - sg)))).astype(BF16)
        dy_s[...] = dz * (ag * sg)
        r_cur = pl.multiple_of(i * BLOCK, BLOCK)
        r_prev = pl.multiple_of(jnp.maximum(i - 1, 0) * BLOCK, BLOCK)
        dk_cur, dv_cur, dk_prev, dv_prev = [], [], [], []
        ds_acc = jnp.zeros((8, 128), F32)
        for hk in range(N_KV_HEADS):
            ks = slice(hk * HEAD_DIM, (hk + 1) * HEAD_DIM)
            qg = (_stack_heads(q_ref, hk, F32) * ATTN_SCALE).astype(BF16)
            dog = _stack_heads(dy_s, hk, F32)
            og = _stack_heads(y_ref, hk, F32)
            dog_b = dog.astype(BF16)
            kp = kp_ref[:, ks].astype(BF16)
            kc = kc_ref[:, ks].astype(BF16)
            vp = vp_ref[:, ks].astype(BF16)
            vc = vc_ref[:, ks].astype(BF16)
            s_prev = _dot_nt(qg, kp)
            s_cur = _dot_nt(qg, kc)
            dp_prev = _dot_nt(dog_b, vp)
            dp_cur = _dot_nt(dog_b, vc)
            dvec = jnp.sum(dog * og, axis=-1, keepdims=True)
            pp, pc, dsp, dsc = [], [], [], []
            for g in range(GROUP):
                h = GROUP * hk + g
                rows = slice(g * BLOCK, (g + 1) * BLOCK)
                p_prev, p_cur, p_sink = _attn_probs(
                    s_prev[rows], s_cur[rows], sink_ref[h], bias_ref[h], lse_ref[:, h:h + 1])
                d_h = dvec[rows]
                pp.append(p_prev.astype(BF16))
                pc.append(p_cur.astype(BF16))
                dsp.append((p_prev * (dp_prev[rows] - d_h)).astype(BF16))
                dsc.append((p_cur * (dp_cur[rows] - d_h)).astype(BF16))
                dsink = -jnp.sum(p_sink * d_h, axis=0, keepdims=True)
                ds_acc = ds_acc + jnp.where(jnp.logical_and(lane == h, sub == 1), dsink, 0.0)
            pp = jnp.concatenate(pp, axis=0)
            pc = jnp.concatenate(pc, axis=0)
            dsp = jnp.concatenate(dsp, axis=0)
            dsc = jnp.concatenate(dsc, axis=0)
            dqg = (_dot(dsp, kp) + _dot(dsc, kc)) * ATTN_SCALE
            for g in range(GROUP):
                h = GROUP * hk + g
                dq_ref[:, h * HEAD_DIM:(h + 1) * HEAD_DIM] = dqg[g * BLOCK:(g + 1) * BLOCK].astype(BF16)
            dk_ref[pl.ds(r_cur, BLOCK), ks] = _dot_tn(dsc, qg)
            dv_ref[pl.ds(r_cur, BLOCK), ks] = _dot_tn(pc, dog_b)
            dk_prev.append(_dot_tn(dsp, qg))
            dv_prev.append(_dot_tn(pp, dog_b))
        ds_ref[:, 0:128] += ds_acc

        @pl.when(i > 0)
        def _():
            for hk in range(N_KV_HEADS):
                ks = slice(hk * HEAD_DIM, (hk + 1) * HEAD_DIM)
                dk_ref[pl.ds(r_prev, BLOCK), ks] += dk_prev[hk]
                dv_ref[pl.ds(r_prev, BLOCK), ks] += dv_prev[hk]

    prev = lambda c: (lambda i: (jnp.maximum(i - 1, 0), c))
    cur = lambda c: (lambda i: (i, c))
    blk = pl.BlockSpec((BLOCK, 1024), lambda i: (i, 0))
    whole = pl.BlockSpec((T, D_KV), lambda i: (0, 0))
    return pl.pallas_call(
        body,
        name="attn_bwd",
        grid=(nb,),
        in_specs=[
            pl.BlockSpec(memory_space=pltpu.SMEM),
            _bias_spec(),
            pl.BlockSpec((BLOCK, 1024), lambda i: (i, COL_Q // 4)),
            pl.BlockSpec((BLOCK, D_KV), prev(COL_K)),
            pl.BlockSpec((BLOCK, D_KV), cur(COL_K)),
            pl.BlockSpec((BLOCK, D_KV), prev(COL_V)),
            pl.BlockSpec((BLOCK, D_KV), cur(COL_V)),
            pl.BlockSpec((BLOCK, 512), lambda i: (i, COL_ATTN_GATE // 2)),
            pl.BlockSpec((BLOCK, 512), lambda i: (i, COL_ATTN_GATE // 2 + 1)),
            blk,
            pl.BlockSpec((BLOCK, N_Q_HEADS), lambda i: (i, 0)),
            blk,
            pl.BlockSpec((8, 128), lambda i: (0, 0)),
        ],
        out_specs=[blk, whole, whole, blk, pl.BlockSpec((8, 1024), lambda i: (0, 0))],
        out_shape=[_sds((T, 1024), BF16), _sds((T, D_KV), F32), _sds((T, D_KV), F32), _sds((T, 1024), BF16),
                   _sds((8, 1024), F32)],
        scratch_shapes=[pltpu.VMEM((BLOCK, 1024), F32)],
        compiler_params=_params(("arbitrary",), 48),
    )(sinks, *_hbm(bias, proj, proj, proj, proj, proj, proj, proj, y_attn, lse, dz_attn, token))


def _head(x, target, z_rnn, z_attn, proj, b_gate, g_post, w_rnn_out, w_attn_out, w_out):
    T = x.shape[0]
    tm = 256

    def body(x_ref, t_ref, zr_ref, za_ref, ml0_ref, ml1_ref, ml2_ref, ml3_ref, bg_ref, gp_ref, wr_ref, wa_ref, wo_ref,
             dyx_ref, dzr_ref, dza_ref, dml_ref, mb_ref, dout_ref, dbr_ref, dba_ref, sm_ref):
        @pl.when(pl.program_id(0) == 0)
        def _():
            sm_ref[...] = jnp.zeros_like(sm_ref)

        wr, wa, wo = wr_ref[...], wa_ref[...], wo_ref[...]
        br_rnn = _dot(zr_ref[...], wr)
        br_attn = _dot(za_ref[...], wa)
        ml_rnn = jnp.concatenate([ml0_ref[...], ml1_ref[...]], axis=1).astype(F32)
        ml_attn = jnp.concatenate([ml2_ref[...], ml3_ref[...]], axis=1).astype(F32)
        g_rnn = _sigmoid(ml_rnn + bg_ref[:, 0:D_MODEL])
        g_attn = _sigmoid(ml_attn + bg_ref[:, D_MODEL:2 * D_MODEL])
        mb = (g_rnn * br_rnn + g_attn * br_attn).astype(BF16)
        mb_ref[...] = mb
        out = _dot(mb, wo)
        rstd = lax.rsqrt(jnp.mean(out * out, axis=-1, keepdims=True) + EPS)
        n = out * rstd
        gp = gp_ref[...]
        err = (x_ref[...] + n * gp) - t_ref[...]
        sm_ref[pl.ds(3, 1), :] += 0.5 * jnp.sum(jnp.mean(err * err, axis=-1, keepdims=True), axis=0, keepdims=True)
        dy = err * (1.0 / D_MODEL)
        dyx_ref[...] = dy
        sm_ref[pl.ds(0, 1), :] += jnp.sum(dy * n, axis=0, keepdims=True)
        dn = dy * gp
        dout = (rstd * (dn - n * jnp.mean(dn * n, axis=-1, keepdims=True))).astype(BF16)
        dout_ref[...] = dout
        dmerged = _dot_nt(dout, wo)
        dml_r = (dmerged * br_rnn) * (g_rnn * (1.0 - g_rnn))
        dml_a = (dmerged * br_attn) * (g_attn * (1.0 - g_attn))
        dml_ref[:, 0:D_MODEL] = dml_r.astype(BF16)
        dml_ref[:, D_MODEL:2 * D_MODEL] = dml_a.astype(BF16)
        sm_ref[pl.ds(1, 1), :] += jnp.sum(dml_r, axis=0, keepdims=True)
        sm_ref[pl.ds(2, 1), :] += jnp.sum(dml_a, axis=0, keepdims=True)
        dbr = (dmerged * g_rnn).astype(BF16)
        dba = (dmerged * g_attn).astype(BF16)
        dbr_ref[...] = dbr
        dba_ref[...] = dba
        dzr_ref[...] = _dot_nt(dbr, wr)
        dza_ref[...] = _dot_nt(dba, wa)

    tile = pl.BlockSpec((tm, D_MODEL), lambda i: (i, 0))
    wspec = pl.BlockSpec((D_MODEL, D_MODEL), lambda i: (0, 0))
    ml = lambda q: pl.BlockSpec((tm, 512), lambda i: (i, COL_MERGE // 2 + q))
    return pl.pallas_call(
        body,
        name="head",
        grid=(T // tm,),
        in_specs=[
            tile, tile, tile, tile,
            ml(0), ml(1), ml(2), ml(3),
            pl.BlockSpec((1, 2 * D_MODEL), lambda i: (0, 0)),
            pl.BlockSpec((1, D_MODEL), lambda i: (0, 0)),
            wspec, wspec, wspec,
        ],
        out_specs=[
            tile, tile, tile,
            pl.BlockSpec((tm, 2 * D_MODEL), lambda i: (i, 0)),
            tile, tile, tile, tile,
            pl.BlockSpec((8, D_MODEL), lambda i: (0, 0)),
        ],
        out_shape=[
            _sds((T, D_MODEL), F32), _sds((T, D_MODEL), F32), _sds((T, D_MODEL), F32),
            _sds((T, 2 * D_MODEL), BF16),
            _sds((T, D_MODEL), BF16), _sds((T, D_MODEL), BF16), _sds((T, D_MODEL), BF16), _sds((T, D_MODEL), BF16),
            _sds((8, D_MODEL), F32),
        ],
        compiler_params=_params(("arbitrary",), 56),
    )(*_hbm(x, target, z_rnn, z_attn, proj, proj, proj, proj, b_gate, g_post, w_rnn_out, w_attn_out, w_out))


def _matmul_tn(a, b, name):
    T, M = a.shape
    N = b.shape[1]
    tk = min(512, T)
    nt = T // tk

    def body(a_ref, b_ref, o_ref, ob_ref):
        @pl.when(pl.program_id(0) == 0)
        def _():
            o_ref[...] = jnp.zeros_like(o_ref)

        o_ref[...] += _dot_tn(a_ref[...], b_ref[...])

        @pl.when(pl.program_id(0) == nt - 1)
        def _():
            ob_ref[...] = o_ref[...].astype(BF16)

    whole = pl.BlockSpec((M, N), lambda t: (0, 0))
    return pl.pallas_call(
        body,
        name=name,
        grid=(nt,),
        in_specs=[pl.BlockSpec((tk, M), lambda t: (t, 0)), pl.BlockSpec((tk, N), lambda t: (t, 0))],
        out_specs=[whole, whole],
        out_shape=[_sds((M, N), F32), _sds((M, N), BF16)],
        compiler_params=_params(("arbitrary",), 48),
    )(*_hbm(a, b))


DPROJ_WIDTHS = (D_RNN, D_RNN, 1024, D_KV, D_KV, 1024, 2 * D_MODEL)


def _dproj_segments():
    segs, start = [[] for _ in range(N_CHIPS)], 0
    for p, width in enumerate(DPROJ_WIDTHS):
        for c in range(N_CHIPS):
            lo, hi = max(start, c * W_IN_SHARD), min(start + width, (c + 1) * W_IN_SHARD)
            if lo < hi:
                segs[c].append((p, lo - start, hi - start, lo - c * W_IN_SHARD, hi - c * W_IN_SHARD))
        start += width
    return segs


def _dh_bwd(pieces, w_in_g, x, dyx, g_pre, token):
    T = x.shape[0]
    tm = min(512, T)
    n = len(pieces)
    segs = _dproj_segments()

    def body(*refs):
        p_refs, w_hbm, x_ref, dyx_ref, g_ref = refs[0:n], refs[n], refs[n + 1], refs[n + 2], refs[n + 3]
        gx_ref, dg_ref, w_ref = refs[n + 5], refs[n + 6], refs[n + 7]

        @pl.when(pl.program_id(0) == 0)
        def _():
            pltpu.sync_copy(w_hbm, w_ref)
            dg_ref[...] = jnp.zeros_like(dg_ref)

        dh = None
        for c in range(N_CHIPS):
            for p, a0, a1, u0, u1 in segs[c]:
                part = _dot_nt(p_refs[p][:, a0:a1].astype(BF16), w_ref[c, :, u0:u1])
                dh = part if dh is None else dh + part
        xv = x_ref[...]
        rstd = lax.rsqrt(jnp.mean(xv * xv, axis=-1, keepdims=True) + EPS)
        nx = xv * rstd
        dhg = dh * g_ref[...]
        gx_ref[...] = dyx_ref[...] + rstd * (dhg - nx * jnp.mean(dhg * nx, axis=-1, keepdims=True))
        dg_ref[pl.ds(0, 1), :] += jnp.sum(dh * nx, axis=0, keepdims=True)

    tile = pl.BlockSpec((tm, D_MODEL), lambda i: (i, 0))
    return pl.pallas_call(
        body,
        name="dh_bwd",
        grid=(T // tm,),
        in_specs=[pl.BlockSpec((tm, w), lambda i: (i, 0)) for w in DPROJ_WIDTHS] + [
            ANY, tile, tile,
            pl.BlockSpec((1, D_MODEL), lambda i: (0, 0)),
            pl.BlockSpec((8, 128), lambda i: (0, 0)),
        ],
        out_specs=[tile, pl.BlockSpec((8, D_MODEL), lambda i: (0, 0))],
        out_shape=[_sds((T, D_MODEL), F32), _sds((8, D_MODEL), F32)],
        scratch_shapes=[pltpu.VMEM(w_in_g.shape, BF16)],
        compiler_params=_params(("arbitrary",), 56),
    )(*_hbm(*pieces, w_in_g, x, dyx, g_pre, token))


def _dw_in(ht, pieces):
    T = ht.shape[1]
    tk = min(512, T)
    nt = T // tk
    n = len(pieces)
    segs = _dproj_segments()

    def body(*refs):
        h_ref, p_refs, o_ref, ob_ref = refs[0], refs[1:n + 1], refs[n + 1], refs[n + 2]

        @pl.when(pl.program_id(1) == 0)
        def _():
            o_ref[...] = jnp.zeros_like(o_ref)

        for c in range(N_CHIPS):
            @pl.when(pl.program_id(0) == c)
            def _():
                cols = jnp.concatenate([p_refs[p][:, a0:a1].astype(BF16) for p, a0, a1, _, _ in segs[c]], axis=1)
                o_ref[...] += _dot(h_ref[...], cols)

        @pl.when(pl.program_id(1) == nt - 1)
        def _():
            ob_ref[...] = o_ref[...].astype(BF16)

    def piece_spec(p):
        chips = [c for c in range(N_CHIPS) if any(s[0] == p for s in segs[c])]

        def index(c, t):
            used = functools.reduce(jnp.logical_or, [c == k for k in chips])
            return (jnp.where(used, t, 0), 0)

        return pl.BlockSpec((tk, DPROJ_WIDTHS[p]), index)

    return pl.pallas_call(
        body,
        name="dw_in",
        grid=(N_CHIPS, nt),
        in_specs=[pl.BlockSpec((D_MODEL, tk), lambda c, t: (0, t))] + [piece_spec(p) for p in range(n)],
        out_specs=[pl.BlockSpec((None, D_MODEL, W_IN_SHARD), lambda c, t: (c, 0, 0))] * 2,
        out_shape=[_sds((N_CHIPS, D_MODEL, W_IN_SHARD), F32), _sds((N_CHIPS, D_MODEL, W_IN_SHARD), BF16)],
        compiler_params=_params(("parallel", "arbitrary"), 56),
    )(*_hbm(ht, *pieces))


ELEMENTWISE_TILE_BYTES = MIB


def _row_tile(rows, cols):
    if rows * cols * 4 <= ELEMENTWISE_TILE_BYTES:
        return rows
    for t in (512, 256, 128, 64, 32, 16, 8):
        if rows % t == 0 and t * cols * 4 <= ELEMENTWISE_TILE_BYTES:
            return t
    return rows


def _pair_sum(g, got, core, name):
    nch, R, C = g.shape
    h = R // 2
    tr = _row_tile(h, C)
    nt = h // tr

    def body(c_ref, g_ref, got_ref, p_ref, pb_ref):
        s = g_ref[...] + got_ref[...].astype(F32)
        p_ref[...] = s
        pb_ref[...] = s.astype(BF16)

    blk = pl.BlockSpec((None, tr, C), lambda j, i, c_ref: (j, i, 0))
    return pl.pallas_call(
        body,
        name=name,
        grid_spec=pltpu.PrefetchScalarGridSpec(
            num_scalar_prefetch=1,
            grid=(nch, nt),
            in_specs=[pl.BlockSpec((None, tr, C), lambda j, i, c_ref: (j, c_ref[0] * nt + i, 0)), blk],
            out_specs=[blk, blk],
        ),
        out_shape=[_sds((nch, h, C), F32), _sds((nch, h, C), BF16)],
        compiler_params=_params(("parallel", "parallel"), 48),
    )(core, *_hbm(g, got))


def _chip_sum(p, got, chip_core, name):
    _, h, C = p.shape
    tr = _row_tile(h, C)
    nt = h // tr

    def body(jc_ref, p_ref, g0_ref, g1_ref, g2_ref, o_ref):
        o_ref[...] = ((p_ref[...] + g0_ref[...].astype(F32)) + g1_ref[...].astype(F32)) + g2_ref[...].astype(F32)

    rel = lambda r: pl.BlockSpec((None, tr, C), lambda i, jc_ref: (r, i, 0))
    return pl.pallas_call(
        body,
        name=name,
        grid_spec=pltpu.PrefetchScalarGridSpec(
            num_scalar_prefetch=1,
            grid=(nt,),
            in_specs=[pl.BlockSpec((None, tr, C), lambda i, jc_ref: (jc_ref[0], i, 0)), rel(0), rel(1), rel(2)],
            out_specs=pl.BlockSpec((tr, C), lambda i, jc_ref: (jc_ref[1] * nt + i, 0)),
        ),
        out_shape=_sds((2 * h, C), F32),
        compiler_params=_params(("parallel",), 48),
    )(chip_core, *_hbm(p, got, got, got))


def _place_shards(shards, chip, name):
    n = len(shards)
    tiles = [_row_tile(s.shape[0], s.shape[1]) for s in shards]
    steps = max(s.shape[0] // t for s, t in zip(shards, tiles))
    tiles = [s.shape[0] // steps for s in shards]

    def body(j_ref, *refs):
        for a in range(n):
            refs[n + a][...] = refs[a][...].astype(BF16)

    return pl.pallas_call(
        body,
        name=name,
        grid_spec=pltpu.PrefetchScalarGridSpec(
            num_scalar_prefetch=1,
            grid=(steps,),
            in_specs=[pl.BlockSpec((t, s.shape[1]), lambda i, j_ref: (i, 0)) for s, t in zip(shards, tiles)],
            out_specs=[pl.BlockSpec((None, t, s.shape[1]), lambda i, j_ref: (j_ref[0], i, 0))
                       for s, t in zip(shards, tiles)],
        ),
        out_shape=[_sds((N_CHIPS,) + s.shape, BF16) for s in shards],
        compiler_params=_params(("parallel",), 48),
    )(chip, *_hbm(*shards))


def _adamw(w, g, m, v, name):
    R, C = w.shape
    tr = _row_tile(R, C)
    c1 = 1.0 - ADAM_B1 ** ADAM_STEP
    c2 = 1.0 - ADAM_B2 ** ADAM_STEP

    def body(w_ref, g_ref, m_ref, v_ref, d_ref, nm_ref, nv_ref):
        g = g_ref[...]
        nm = ADAM_B1 * m_ref[...] + (1.0 - ADAM_B1) * g
        nv = ADAM_B2 * v_ref[...] + (1.0 - ADAM_B2) * (g * g)
        nm_ref[...] = nm
        nv_ref[...] = nv
        d_ref[...] = (-ADAM_LR) * ((nm / c1) / (jnp.sqrt(nv / c2) + ADAM_EPS) + ADAM_WD * w_ref[...])

    spec = pl.BlockSpec((tr, C), lambda i: (i, 0))
    return pl.pallas_call(
        body, name=name, grid=(R // tr,), in_specs=[spec] * 4, out_specs=[spec] * 3,
        out_shape=[_sds((R, C), F32)] * 3, compiler_params=_params(("parallel",), 48),
    )(*_hbm(w, g, m, v))


def _place():
    return lax.axis_index("x"), lax.axis_index("y"), lax.axis_index("c")


def _chip_of(x, y, r):
    return (x ^ (r >> 1), y ^ (r & 1))


ANY = pl.BlockSpec(memory_space=pl.ANY)


def _gather_weights(placed, cw8):
    nbig = len(placed)
    halves = [s.shape[1] // 2 for s in placed]
    pieces = [max(1, h // 128) for h in halves]
    rows = [h // p for h, p in zip(halves, pieces)]
    order = [(a, q) for q in range(max(pieces)) for a in range(nbig) if q < pieces[a]]
    ici_sem = {(a, q, r): 3 * i + (r - 1) for i, (a, q) in enumerate(order) for r in (1, 2, 3)}
    cw_sem = {r: 3 * len(order) + (r - 1) for r in (1, 2, 3)}
    d2d_sem = {key: 3 * len(order) + 3 + k for key, k in ici_sem.items()}
    nsem = 6 * len(order) + 3

    def body(*refs):
        cw_ref, dsts, gcw_ref = refs[nbig], refs[nbig + 1:2 * nbig + 1], refs[2 * nbig + 1]
        send_sems, recv_sems = refs[2 * nbig + 2:]
        x, y, c = _place()
        j = 2 * x + y

        def piece_rows(a, q, core):
            return pl.ds(pl.multiple_of(core * halves[a] + q * rows[a], 16), rows[a])

        def ici(a, q, r):
            tx, ty = _chip_of(x, y, r)
            k = ici_sem[(a, q, r)]
            region = dsts[a].at[j, piece_rows(a, q, c), :]
            return pltpu.make_async_remote_copy(
                src_ref=region, dst_ref=region, send_sem=send_sems.at[k], recv_sem=recv_sems.at[k],
                device_id=(tx, ty, c), device_id_type=MESH)

        def ici_landed(a, q, r):
            tx, ty = _chip_of(x, y, r)
            k = ici_sem[(a, q, r)]
            region = dsts[a].at[2 * tx + ty, piece_rows(a, q, c), :]
            return pltpu.make_async_remote_copy(
                src_ref=region, dst_ref=region, send_sem=send_sems.at[k], recv_sem=recv_sems.at[k],
                device_id=(tx, ty, c), device_id_type=MESH)

        def d2d(a, q, r, core):
            tx, ty = _chip_of(x, y, r)
            k = d2d_sem[(a, q, r)]
            region = dsts[a].at[2 * tx + ty, piece_rows(a, q, core), :]
            return pltpu.make_async_remote_copy(
                src_ref=region, dst_ref=region, send_sem=send_sems.at[k], recv_sem=recv_sems.at[k],
                device_id=(x, y, 1 - c), device_id_type=MESH)

        def cw_copy(r):
            tx, ty = _chip_of(x, y, r)
            k = cw_sem[r]
            return pltpu.make_async_remote_copy(
                src_ref=cw_ref, dst_ref=gcw_ref.at[j], send_sem=send_sems.at[k], recv_sem=recv_sems.at[k],
                device_id=(tx, ty, c), device_id_type=MESH)

        def cw_landed(r):
            tx, ty = _chip_of(x, y, r)
            k = cw_sem[r]
            region = gcw_ref.at[2 * tx + ty]
            return pltpu.make_async_remote_copy(
                src_ref=region, dst_ref=region, send_sem=send_sems.at[k], recv_sem=recv_sems.at[k],
                device_id=(tx, ty, c), device_id_type=MESH)

        first = [ici(a, q, r) for (a, q) in order for r in (1, 2, 3)] + [cw_copy(r) for r in (1, 2, 3)]
        for cp in first:
            cp.start()
        passed = []
        for (a, q) in order:
            for r in (1, 2, 3):
                ici_landed(a, q, r).wait_recv()
                cp = d2d(a, q, r, c)
                cp.start()
                passed.append(cp)
        for r in (1, 2, 3):
            cw_landed(r).wait_recv()
        for (a, q) in order:
            for r in (1, 2, 3):
                d2d(a, q, r, 1 - c).wait_recv()
        for cp in first + passed:
            cp.wait_send()

    return pl.pallas_call(
        body,
        name="gather_weights",
        in_specs=[ANY] * (nbig + 1),
        out_specs=[ANY] * (nbig + 1),
        out_shape=[_sds(s.shape, s.dtype) for s in placed] + [_sds((N_CHIPS,) + cw8.shape, cw8.dtype)],
        input_output_aliases={a: a for a in range(nbig)},
        scratch_shapes=[pltpu.SemaphoreType.DMA((nsem,)), pltpu.SemaphoreType.DMA((nsem,))],
    )(*placed, cw8)


def _gather_late_start(placed, after, name):
    n = len(placed)
    halves = [s.shape[1] // 2 for s in placed]

    def body(*refs):
        g_refs = refs[0:n]
        send_sems, recv_sems, token = refs[n + 1], refs[n + 2], refs[-1]
        x, y, c = _place()
        j = 2 * x + y
        for a in range(n):
            mine = g_refs[a].at[j, pl.ds(pl.multiple_of(c * halves[a], 16), halves[a]), :]
            for r in (1, 2, 3):
                tx, ty = _chip_of(x, y, r)
                for to_core in (0, 1):
                    k = ((a * 3 + (r - 1)) * 2 + c) * 2 + to_core
                    pltpu.make_async_remote_copy(
                        src_ref=mine, dst_ref=mine, send_sem=send_sems.at[k], recv_sem=recv_sems.at[k],
                        device_id=(tx, ty, to_core), device_id_type=MESH).start()
        token[...] = jnp.zeros_like(token)

    hbm = lambda t: pltpu.HBM(t.shape, t.dtype)
    keep = lambda t: pltpu.with_memory_space_constraint(t, pltpu.HBM)
    nsem = 12 * n
    outs = pl.pallas_call(
        body,
        name=name,
        in_specs=[HBM] * n + [ANY],
        out_specs=(SEM, SEM, *[HBM] * n, pl.BlockSpec(memory_space=pltpu.VMEM)),
        out_shape=(pltpu.SemaphoreType.DMA((nsem,)), pltpu.SemaphoreType.DMA((nsem,)), *[hbm(p) for p in placed],
                   jax.ShapeDtypeStruct((8, 128), F32)),
        input_output_aliases={i: 2 + i for i in range(n)},
        compiler_params=pltpu.CompilerParams(has_side_effects=DATAFLOW),
    )(*[keep(p) for p in placed], after)
    return outs[0], outs[1], list(outs[2:2 + n]), outs[-1]


def _gather_late_wait(send_sems, recv_sems, thru, after, name):
    n = len(thru)
    halves = [s.shape[1] // 2 for s in thru]

    def body(*refs):
        g_refs = refs[0:n]
        send_sems, recv_sems = refs[n], refs[n + 1]
        x, y, c = _place()
        j = 2 * x + y
        for a in range(n):
            mine = g_refs[a].at[j, pl.ds(pl.multiple_of(c * halves[a], 16), halves[a]), :]
            for r in (1, 2, 3):
                tx, ty = _chip_of(x, y, r)
                for other in (0, 1):
                    k_out = ((a * 3 + (r - 1)) * 2 + c) * 2 + other
                    pltpu.make_async_remote_copy(
                        src_ref=mine, dst_ref=mine, send_sem=send_sems.at[k_out], recv_sem=recv_sems.at[k_out],
                        device_id=(tx, ty, other), device_id_type=MESH).wait_send()
                    k_in = ((a * 3 + (r - 1)) * 2 + other) * 2 + c
                    theirs = g_refs[a].at[2 * tx + ty, pl.ds(other * halves[a], halves[a]), :]
                    pltpu.make_async_remote_copy(
                        src_ref=theirs, dst_ref=theirs, send_sem=send_sems.at[k_in], recv_sem=recv_sems.at[k_in],
                        device_id=(tx, ty, other), device_id_type=MESH).wait_recv()

    hbm = lambda t: pltpu.HBM(t.shape, t.dtype)
    outs = pl.pallas_call(
        body,
        name=name,
        in_specs=[HBM] * n + [SEM, SEM, ANY],
        out_specs=[HBM] * n,
        out_shape=[hbm(t) for t in thru],
        input_output_aliases={i: i for i in range(n)},
        compiler_params=pltpu.CompilerParams(has_side_effects=DATAFLOW),
    )(*thru, send_sems, recv_sems, after)
    return list(outs)


D2D_PIECE_ROWS = 64


def _pair_exchange(grads, name):
    n = len(grads)
    halves = [g.shape[1] // 2 for g in grads]

    def body(*refs):
        g_refs, got_refs = refs[0:n], refs[n:2 * n]
        send_sems, recv_sems = refs[2 * n:]
        x, y, c = _place()

        def copy(a, src, dst):
            return pltpu.make_async_remote_copy(
                src_ref=src, dst_ref=dst, send_sem=send_sems.at[a], recv_sem=recv_sems.at[a],
                device_id=(x, y, 1 - c), device_id_type=MESH)

        for a in range(n):
            for jj in range(N_CHIPS):
                for q in range(halves[a] // D2D_PIECE_ROWS):
                    src_rows = pl.ds(pl.multiple_of((1 - c) * halves[a] + q * D2D_PIECE_ROWS, 16), D2D_PIECE_ROWS)
                    dst_rows = pl.ds(q * D2D_PIECE_ROWS, D2D_PIECE_ROWS)
                    copy(a, g_refs[a].at[jj, src_rows, :], got_refs[a].at[jj, dst_rows, :]).start()
        for a in range(n):
            sent = g_refs[a].at[:, pl.ds(pl.multiple_of((1 - c) * halves[a], 16), halves[a]), :]
            copy(a, sent, got_refs[a]).wait()

    return pl.pallas_call(
        body,
        name=name,
        in_specs=[ANY] * n,
        out_specs=[ANY] * n,
        out_shape=[_sds((N_CHIPS, h, g.shape[2]), g.dtype) for g, h in zip(grads, halves)],
        scratch_shapes=[pltpu.SemaphoreType.DMA((n,)), pltpu.SemaphoreType.DMA((n,))],
    )(*grads)


HBM = pl.BlockSpec(memory_space=pltpu.HBM)
SEM = pl.BlockSpec(memory_space=pltpu.SEMAPHORE)
DATAFLOW = pltpu.SideEffectType.DATAFLOW_SIDE_EFFECTING


def _chip_copy(p_refs, land_refs, send_sems, recv_sems, a, r):
    x, y, c = _place()
    tx, ty = _chip_of(x, y, r)
    k = a * 3 + (r - 1)
    return pltpu.make_async_remote_copy(
        src_ref=p_refs[a].at[2 * tx + ty], dst_ref=land_refs[a].at[r - 1],
        send_sem=send_sems.at[k], recv_sem=recv_sems.at[k], device_id=(tx, ty, c), device_id_type=MESH)


def _chip_exchange_start(psums, name):
    n = len(psums)
    lands = [lax.empty((3,) + p.shape[1:], p.dtype) for p in psums]

    def body(*refs):
        p_refs, land_refs = refs[0:n], refs[n:2 * n]
        send_sems, recv_sems, token = refs[2 * n], refs[2 * n + 1], refs[-1]
        for a in range(n):
            for r in (1, 2, 3):
                _chip_copy(p_refs, land_refs, send_sems, recv_sems, a, r).start()
        token[...] = jnp.zeros_like(token)

    hbm = lambda t: pltpu.HBM(t.shape, t.dtype)
    keep = lambda t: pltpu.with_memory_space_constraint(t, pltpu.HBM)
    outs = pl.pallas_call(
        body,
        name=name,
        in_specs=[HBM] * (2 * n),
        out_specs=(SEM, SEM, *[HBM] * (2 * n), pl.BlockSpec(memory_space=pltpu.VMEM)),
        out_shape=(pltpu.SemaphoreType.DMA((3 * n,)), pltpu.SemaphoreType.DMA((3 * n,)),
                   *[hbm(p) for p in psums], *[hbm(l) for l in lands], _sds((8, 128), F32)),
        input_output_aliases={i: 2 + i for i in range(2 * n)},
        compiler_params=pltpu.CompilerParams(has_side_effects=DATAFLOW),
    )(*[keep(p) for p in psums], *[keep(l) for l in lands])
    return outs[0], outs[1], list(outs[2:2 + n]), list(outs[2 + n:2 + 2 * n]), outs[-1]


def _chip_exchange_wait(send_sems, recv_sems, p_thru, land_thru, after, name):
    n = len(p_thru)

    def body(*refs):
        p_refs, land_refs = refs[0:n], refs[n:2 * n]
        send_sems, recv_sems = refs[2 * n], refs[2 * n + 1]
        for a in range(n):
            for r in (1, 2, 3):
                copy = _chip_copy(p_refs, land_refs, send_sems, recv_sems, a, r)
                copy.wait_send()
                copy.wait_recv()

    hbm = lambda t: pltpu.HBM(t.shape, t.dtype)
    outs = pl.pallas_call(
        body,
        name=name,
        in_specs=[HBM] * (2 * n) + [SEM, SEM, ANY],
        out_specs=[HBM] * (2 * n),
        out_shape=[hbm(p) for p in p_thru] + [hbm(l) for l in land_thru],
        input_output_aliases={i: i for i in range(2 * n)},
        compiler_params=pltpu.CompilerParams(has_side_effects=DATAFLOW),
    )(*p_thru, *land_thru, send_sems, recv_sems, after)
    return list(outs[n:2 * n])


def _pair_share(fulls):
    n = len(fulls)
    halves = [f.shape[0] // 2 for f in fulls]

    def body(*refs):
        full_refs = refs[n:2 * n]
        send_sems, recv_sems = refs[2 * n:]
        x, y, c = _place()

        def half_of(a, core):
            return full_refs[a].at[pl.ds(pl.multiple_of(core * halves[a], 8), halves[a]), :]

        def remote(a, src, dst):
            return pltpu.make_async_remote_copy(
                src_ref=src, dst_ref=dst, send_sem=send_sems.at[a], recv_sem=recv_sems.at[a],
                device_id=(x, y, 1 - c), device_id_type=MESH)

        for a in range(n):
            for q in range(halves[a] // D2D_PIECE_ROWS):
                piece = full_refs[a].at[
                    pl.ds(pl.multiple_of(c * halves[a] + q * D2D_PIECE_ROWS, 8), D2D_PIECE_ROWS), :]
                remote(a, piece, piece).start()
        for a in range(n):
            remote(a, half_of(a, c), half_of(a, c)).wait_send()
            remote(a, half_of(a, 1 - c), half_of(a, 1 - c)).wait_recv()

    return pl.pallas_call(
        body,
        name="pair_share",
        in_specs=[ANY] * n,
        out_specs=[ANY] * n,
        out_shape=[_sds(f.shape, F32) for f in fulls],
        input_output_aliases={a: a for a in range(n)},
        scratch_shapes=[pltpu.SemaphoreType.DMA((n,)), pltpu.SemaphoreType.DMA((n,))],
    )(*fulls)


def _allreduce_small(s):
    R, C = s.shape
    V = SMALL_VECTOR_ROWS

    def body(s_ref, o_ref, sib, chips_v, chips_m, send_sems, recv_sems):
        x, y, c = _place()
        j = 2 * x + y
        def to_sib(src, dst):
            return pltpu.make_async_remote_copy(
                src_ref=src, dst_ref=dst, send_sem=send_sems.at[0], recv_sem=recv_sems.at[0],
                device_id=(x, y, 1 - c), device_id_type=MESH)

        for q in range(R // 8):
            to_sib(s_ref.at[pl.ds(8 * q, 8), :], sib.at[pl.ds(8 * q, 8), :]).start()
        to_sib(s_ref, sib).wait()
        chips_v[j] = s_ref[pl.ds(0, V), :] + sib[pl.ds(0, V), :]
        chips_m[j] = (s_ref[pl.ds(V, R - V), :] + sib[pl.ds(V, R - V), :]).astype(BF16)

        def to_chip(r, k, buf, block):
            tx, ty = _chip_of(x, y, r)
            return pltpu.make_async_remote_copy(
                src_ref=buf.at[block], dst_ref=buf.at[block], send_sem=send_sems.at[k], recv_sem=recv_sems.at[k],
                device_id=(tx, ty, c), device_id_type=MESH)

        sends = [to_chip(r, 2 * r - 1 + k, buf, j) for r in (1, 2, 3) for k, buf in enumerate((chips_v, chips_m))]
        for cp in sends:
            cp.start()
        for r in (1, 2, 3):
            tx, ty = _chip_of(x, y, r)
            for k, buf in enumerate((chips_v, chips_m)):
                to_chip(r, 2 * r - 1 + k, buf, 2 * tx + ty).wait_recv()
        for cp in sends:
            cp.wait_send()
        o_ref[pl.ds(0, V), :] = (chips_v[0] + chips_v[1]) + (chips_v[2] + chips_v[3])
        o_ref[pl.ds(V, R - V), :] = (chips_m[0].astype(F32) + chips_m[1].astype(F32)) + (
            chips_m[2].astype(F32) + chips_m[3].astype(F32))

    return pl.pallas_call(
        body,
        name="allreduce_small",
        in_specs=[pl.BlockSpec(memory_space=pltpu.VMEM)],
        out_specs=pl.BlockSpec(memory_space=pltpu.VMEM),
        out_shape=jax.ShapeDtypeStruct((R, C), F32),
        scratch_shapes=[pltpu.VMEM((R, C), F32), pltpu.VMEM((N_CHIPS, V, C), F32),
                        pltpu.VMEM((N_CHIPS, R - V, C), BF16),
                        pltpu.SemaphoreType.DMA((7,)), pltpu.SemaphoreType.DMA((7,))],
    )(s)


def _block_diag(w):
    w4 = w.reshape(4, 4, RNN_BLOCK_W, RNN_BLOCK_W)
    eye = jnp.eye(4, dtype=w.dtype)
    return jnp.einsum("jaik,ab->jaibk", w4, eye).reshape(4, RNN_TILE, RNN_TILE)


def _block_diag_part(d):
    d5 = d.reshape(4, 4, RNN_BLOCK_W, 4, RNN_BLOCK_W)
    return jnp.stack([d5[:, a, :, a, :] for a in range(4)], axis=1).reshape(RNN_BLOCKS, RNN_BLOCK_W, RNN_BLOCK_W)


def _local_grads(x, target, g_pre, w_in_g, b_gate, conv_w, conv_b, w_rg_a, b_rg_a, w_rg_x, b_rg_x, lam, sinks,
                 out_weights, fwd_token, g_post, on_out_grads, on_w_in_grad):
    wa_bd = _block_diag(w_rg_a).astype(BF16)
    wx_bd = _block_diag(w_rg_x).astype(BF16)
    b_a = b_rg_a.reshape(1, D_RNN)
    b_x = b_rg_x.reshape(1, D_RNN)

    proj, ht = _proj_fwd(x, g_pre, w_in_g)
    y_rnn, z_rnn = _rnn_fwd(proj, conv_w, conv_b, wa_bd, wx_bd, b_a, b_x, lam, fwd_token)
    bias = _attn_bias()
    y_attn, z_attn, lse = _attn_fwd(proj, sinks, bias)
    w_rnn_out, w_attn_out, w_out = out_weights(z_attn)
    dyx, dz_rnn, dz_attn, dml, merged, dout, dbr_rnn, dbr_attn, head_small = _head(
        x, target, z_rnn, z_attn, proj, b_gate, g_post, w_rnn_out, w_attn_out, w_out)
    out_grads = [_matmul_tn(z_rnn, dbr_rnn, "dw_rnn_out"), _matmul_tn(z_attn, dbr_attn, "dw_attn_out"),
                 _matmul_tn(merged, dout, "dw_out")]
    shard_rows = lambda d: d.reshape(N_CHIPS, OUT_SHARD, D_MODEL)
    token = on_out_grads([shard_rows(g) for g, _ in out_grads], [shard_rows(gb) for _, gb in out_grads])
    dq, dk, dv, dag, attn_small = _attn_bwd(proj, y_attn, lse, dz_attn, sinks, bias, token)
    drx, drg, dwa_t, dwx_t, rnn_small = _rnn_bwd(proj, y_rnn, dz_rnn, conv_w, conv_b, wa_bd, wx_bd, b_a, b_x, lam)
    dproj = [drx, drg, dq, dk, dv, dag, dml]
    token = on_w_in_grad(*_dw_in(ht, dproj))
    grad_x, dh_small = _dh_bwd(dproj, w_in_g, x, dyx, g_pre, token)
    small = jnp.concatenate([rnn_small, head_small, dh_small + attn_small,
                             _block_diag_part(dwa_t).reshape(64, 1024), _block_diag_part(dwx_t).reshape(64, 1024)], axis=0)
    return grad_x, small


ROW_LOSS = 11


def _rows8(parts):
    out = None
    for r, a in parts:
        p = jnp.pad(a, ((r, 8 - r - a.shape[0]), (0, 1024 - a.shape[1])))
        out = p if out is None else out + p
    return out


def _pack_small(p):
    g0 = _rows8([(0, p["b_rg_a"].reshape(1, 1024)), (1, p["b_rg_x"].reshape(1, 1024)), (2, p["lru_lambda"]),
                 (3, p["conv_b"]), (4, p["conv_w"][0])])
    g1 = _rows8([(0, p["post_norm_g"]), (1, p["b_gate"].reshape(2, 1024))])
    g2 = _rows8([(0, p["pre_norm_g"]), (1, p["attn_sinks"])])
    return jnp.concatenate([g0, g1, g2, p["w_rg_a"].reshape(64, 1024), p["w_rg_x"].reshape(64, 1024)], axis=0)


def _unpack_small(s, conv_cols):
    return {
        "b_rg_a": s[0:1].reshape(1, 16, 64), "b_rg_x": s[1:2].reshape(1, 16, 64), "lru_lambda": s[2:3],
        "conv_b": s[3:4], "conv_w": s[4:8, 0:conv_cols].reshape(1, CONV_W, conv_cols),
        "post_norm_g": s[8:9], "b_gate": s[9:11].reshape(1, 2048),
        "pre_norm_g": s[16:17], "attn_sinks": s[17:18, 0:N_Q_HEADS],
        "w_rg_a": s[24:88].reshape(1, 16, 64, 64), "w_rg_x": s[88:152].reshape(1, 16, 64, 64),
    }


WEIGHTS = ["pre_norm_g", "w_in", "b_gate", "conv_w", "conv_b", "w_rg_a", "b_rg_a", "w_rg_x", "b_rg_x", "lru_lambda",
           "attn_sinks", "w_rnn_out", "w_attn_out", "w_out", "post_norm_g"]
BIG = ["w_in", "w_rnn_out", "w_attn_out", "w_out"]


def kernel(x, pre_norm_g, w_in, b_gate, conv_w, conv_b, w_rg_a, b_rg_a, w_rg_x, b_rg_x, lru_lambda, attn_sinks, w_rnn_out, w_attn_out, w_out, post_norm_g, loss_target, m_pre_norm_g, m_w_in, m_b_gate, m_conv_w, m_conv_b, m_w_rg_a, m_b_rg_a, m_w_rg_x, m_b_rg_x, m_lru_lambda, m_attn_sinks, m_w_rnn_out, m_w_attn_out, m_w_out, m_post_norm_g, v_pre_norm_g, v_w_in, v_b_gate, v_conv_w, v_conv_b, v_w_rg_a, v_b_rg_a, v_w_rg_x, v_b_rg_x, v_lru_lambda, v_attn_sinks, v_w_rnn_out, v_w_attn_out, v_w_out, v_post_norm_g):
    w = dict(pre_norm_g=pre_norm_g, w_in=w_in, b_gate=b_gate, conv_w=conv_w, conv_b=conv_b, w_rg_a=w_rg_a,
             b_rg_a=b_rg_a, w_rg_x=w_rg_x, b_rg_x=b_rg_x, lru_lambda=lru_lambda, attn_sinks=attn_sinks,
             w_rnn_out=w_rnn_out, w_attn_out=w_attn_out, w_out=w_out, post_norm_g=post_norm_g)
    m = dict(pre_norm_g=m_pre_norm_g, w_in=m_w_in, b_gate=m_b_gate, conv_w=m_conv_w, conv_b=m_conv_b, w_rg_a=m_w_rg_a,
             b_rg_a=m_b_rg_a, w_rg_x=m_w_rg_x, b_rg_x=m_b_rg_x, lru_lambda=m_lru_lambda, attn_sinks=m_attn_sinks,
             w_rnn_out=m_w_rnn_out, w_attn_out=m_w_attn_out, w_out=m_w_out, post_norm_g=m_post_norm_g)
    v = dict(pre_norm_g=v_pre_norm_g, w_in=v_w_in, b_gate=v_b_gate, conv_w=v_conv_w, conv_b=v_conv_b, w_rg_a=v_w_rg_a,
             b_rg_a=v_b_rg_a, w_rg_x=v_w_rg_x, b_rg_x=v_b_rg_x, lru_lambda=v_lru_lambda, attn_sinks=v_attn_sinks,
             w_rnn_out=v_w_rnn_out, w_attn_out=v_w_attn_out, w_out=v_w_out, post_norm_g=v_post_norm_g)
    chip = 2 * lax.axis_index("x") + lax.axis_index("y")

    chip_idx = chip.astype(jnp.int32).reshape(1)
    chip_core = jnp.stack([chip, lax.axis_index("c")]).astype(jnp.int32)
    cw8 = jnp.pad(conv_w[0], ((0, 8 - CONV_W), (0, 0)))
    placed = _place_shards([w_in[0], w_rnn_out[0], w_attn_out[0], w_out[0]], chip_idx, "place_shards")
    win_g, cw_g = _gather_weights(placed[:1], cw8)
    late_send, late_recv, late_thru, late_token = _gather_late_start(placed[1:], win_g, "gather_late_start")
    cw_g = lax.dynamic_update_slice_in_dim(cw_g, cw8[None], chip, axis=0)
    conv_w_full = jnp.transpose(cw_g[:, 0:CONV_W, :], (1, 0, 2)).reshape(CONV_W, D_RNN)

    core_idx = lax.axis_index("c").astype(jnp.int32).reshape(1)
    started = {}

    def start_reduction(tag, grads, grads_b16):
        got = _pair_exchange(grads_b16, "pair_exchange_" + tag)
        sums = [_pair_sum(g, o, core_idx, "pair_sum_%s_%d" % (tag, a)) for a, (g, o) in enumerate(zip(grads, got))]
        send_sems, recv_sems, p_thru, land_thru, token = _chip_exchange_start(
            [pb for _, pb in sums], "chip_exchange_start_" + tag)
        started[tag] = ([p for p, _ in sums], send_sems, recv_sems, p_thru, land_thru)
        return token

    def end_reduction(tag, after):
        psums, send_sems, recv_sems, p_thru, land_thru = started[tag]
        landed = _chip_exchange_wait(send_sems, recv_sems, p_thru, land_thru, after, "chip_exchange_wait_" + tag)
        return [_chip_sum(p, l, chip_core, "chip_sum_%s_%d" % (tag, a)) for a, (p, l) in enumerate(zip(psums, landed))]

    def out_weights(after):
        gathered = _gather_late_wait(late_send, late_recv, late_thru, after, "gather_late_wait")
        return [g.reshape(D_MODEL, D_MODEL) for g in gathered]

    grad_x, small = _local_grads(
        x[0], loss_target[0], pre_norm_g, win_g, b_gate, conv_w_full, conv_b, w_rg_a[0], b_rg_a[0], w_rg_x[0],
        b_rg_x[0], lru_lambda, attn_sinks[0], out_weights, late_token, post_norm_g,
        on_out_grads=lambda grads, grads_b16: start_reduction("out", grads, grads_b16),
        on_w_in_grad=lambda grad, grad_b16: start_reduction("in", [grad], [grad_b16]))

    halves = end_reduction("in", grad_x) + end_reduction("out", grad_x)
    gbig = dict(zip(BIG, _pair_share(halves)))

    small_sum = _allreduce_small(small)
    total_loss = small_sum[ROW_LOSS, 0]
    gsmall = _unpack_small(small_sum, D_RNN)
    conv_shard = D_RNN // N_CHIPS
    gsmall["conv_w"] = lax.dynamic_slice_in_dim(gsmall["conv_w"], chip * conv_shard, conv_shard, axis=2)

    grads, delta, new_m, new_v = {}, {}, {}, {}
    for n in BIG:
        grads[n] = gbig[n][None]
        d, nm, nv = _adamw(w[n][0], gbig[n], m[n][0], v[n][0], "adamw_" + n)
        delta[n], new_m[n], new_v[n] = d[None], nm[None], nv[None]
    pick = lambda t: {k: t[k] for k in gsmall}
    d, nm, nv = _adamw(_pack_small(pick(w)), _pack_small(gsmall), _pack_small(pick(m)), _pack_small(pick(v)),
                       "adamw_small")
    ud, um, uv = _unpack_small(d, conv_shard), _unpack_small(nm, conv_shard), _unpack_small(nv, conv_shard)
    for n in gsmall:
        grads[n] = gsmall[n].reshape(w[n].shape)
        delta[n] = ud[n].reshape(w[n].shape)
        new_m[n] = um[n].reshape(w[n].shape)
        new_v[n] = uv[n].reshape(w[n].shape)

    return (total_loss, grad_x[None], *[grads[n] for n in WEIGHTS], *[delta[n] for n in WEIGHTS],
            *[new_m[n] for n in WEIGHTS], *[new_v[n] for n in WEIGHTS])
```

```python
import functools
import math

import jax
import jax.numpy as jnp
from jax import lax
from jax.experimental import pallas as pl
from jax.experimental.pallas import tpu as pltpu

F32 = jnp.float32
BF16 = jnp.bfloat16

D_MODEL = 1024
D_RNN = 1024
RNN_BLOCKS = 16
RNN_BLOCK_W = 64
CONV_W = 4
LRU_C = 8.0
N_Q_HEADS = 16
N_KV_HEADS = 4
GROUP = 4
HEAD_DIM = 64
D_KV = 256
BLOCK = 128
ALIBI_MAX_BIAS = 8.0
EPS = 1e-6
D_IN = 6656
N_CHIPS = 4
W_IN_SHARD = D_IN // N_CHIPS
OUT_SHARD = D_MODEL // N_CHIPS
ADAM_LR = 0.001
ADAM_B1 = 0.9
ADAM_B2 = 0.999
ADAM_EPS = 1e-08
ADAM_WD = 0.01
ADAM_STEP = 10
NEG_BIG = -1e30
MIB = 1 << 20

COL_RNN_X = 0
COL_RNN_GATE = 4
COL_Q = 8
COL_K = 12
COL_V = 13
COL_ATTN_GATE = 14
COL_MERGE = 18

RNN_TILE = 256
RNN_CHUNK = 512
SMALL_ROWS = 152
SMALL_VECTOR_ROWS = 24
MESH = pl.DeviceIdType.MESH


def _sds(shape, dtype):
    return pltpu.HBM(shape, dtype)


def _params(sem=None, vmem_mib=None):
    kw = {}
    if sem is not None:
        kw["dimension_semantics"] = sem
    if vmem_mib is not None:
        kw["vmem_limit_bytes"] = vmem_mib * MIB
    return pltpu.CompilerParams(**kw)


def _hbm(*arrays):
    return [pltpu.with_memory_space_constraint(a, pltpu.HBM) for a in arrays]


def _dot(a, b):
    return jnp.dot(a, b, preferred_element_type=F32)


def _dot_nt(a, b):
    return lax.dot_general(a, b, (((1,), (1,)), ((), ())), preferred_element_type=F32)


def _dot_tn(a, b):
    return lax.dot_general(a, b, (((0,), (0,)), ((), ())), preferred_element_type=F32)


def _sigmoid(x):
    return 0.5 * jnp.tanh(0.5 * x) + 0.5


def _sigmoid_small(x):
    return 1.0 / (1.0 + jnp.exp(-x))


def _softplus(x):
    return jnp.maximum(x, 0.0) + jnp.log(1.0 + jnp.exp(-jnp.abs(x)))


def _one_minus_square(a, log_a):
    return -jnp.tanh(log_a) * (a * a + 1.0)


def _proj_fwd(x, g_pre, w_in_g):
    T = x.shape[0]
    tm = min(1024, T)

    def body(x_ref, g_ref, w_ref, proj_ref, ht_ref, h_s):
        @pl.when(pl.program_id(1) == 0)
        def _():
            xv = x_ref[...]
            rstd = lax.rsqrt(jnp.mean(xv * xv, axis=-1, keepdims=True) + EPS)
            hf = (xv * rstd) * g_ref[...]
            h_s[...] = hf.astype(BF16)
            ht_ref[...] = hf.T.astype(BF16)

        proj_ref[...] = _dot(h_s[...], w_ref[...]).astype(BF16)

    return pl.pallas_call(
        body,
        name="proj_fwd",
        grid=(T // tm, N_CHIPS),
        in_specs=[
            pl.BlockSpec((tm, D_MODEL), lambda i, j: (i, 0)),
            pl.BlockSpec((1, D_MODEL), lambda i, j: (0, 0)),
            pl.BlockSpec((None, D_MODEL, W_IN_SHARD), lambda i, j: (j, 0, 0)),
        ],
        out_specs=[
            pl.BlockSpec((tm, W_IN_SHARD), lambda i, j: (i, j)),
            pl.BlockSpec((D_MODEL, tm), lambda i, j: (0, i)),
        ],
        out_shape=[_sds((T, D_IN), BF16), _sds((D_MODEL, T), BF16)],
        scratch_shapes=[pltpu.VMEM((tm, D_MODEL), BF16)],
        compiler_params=_params(("parallel", "arbitrary"), 48),
    )(*_hbm(x, g_pre, w_in_g))


def _shift_down(x, tail, s, row):
    n = x.shape[0]
    xs = pltpu.roll(x, s, 0)
    tail_t = jnp.tile(pltpu.roll(tail, s, 0), (n // 8, 1))
    return jnp.where(row < s, tail_t, xs)


def _shift_up(x, head, s, row):
    n = x.shape[0]
    xs = pltpu.roll(x, n - s, 0)
    head_t = jnp.tile(pltpu.roll(head, 8 - s, 0), (n // 8, 1))
    return jnp.where(row >= n - s, head_t, xs)


def _conv_taps(x, tail, row):
    return [_shift_down(x, tail, 3, row), _shift_down(x, tail, 2, row), _shift_down(x, tail, 1, row), x]


def _rglru_gates(c, wa, wx, ba, bx, lam):
    cb = c.astype(BF16)
    r = _sigmoid_small(_dot(cb, wa) + ba)
    i = _sigmoid(_dot(cb, wx) + bx)
    log_a = (-LRU_C) * r * _softplus(-lam)
    a = jnp.exp(log_a)
    mult = jnp.sqrt(_one_minus_square(a, log_a))
    return cb, r, i, a, mult


SUBLANES = 8


def _scan_down(a, u, row):
    n = a.shape[0]
    s = 1
    while s < SUBLANES:
        a_sh = jnp.where(row >= s, pltpu.roll(a, s, 0), 1.0)
        u_sh = jnp.where(row >= s, pltpu.roll(u, s, 0), 0.0)
        u = a * u_sh + u
        a = a * a_sh
        s *= 2
    while s < n:
        u = jnp.concatenate([u[:s], a[s:] * u[:n - s] + u[s:]], axis=0)
        a = jnp.concatenate([a[:s], a[s:] * a[:n - s]], axis=0)
        s *= 2
    return a, u


def _scan_up(b, u, row):
    n = b.shape[0]
    s = 1
    while s < SUBLANES:
        b_sh = jnp.where(row < n - s, pltpu.roll(b, n - s, 0), 1.0)
        u_sh = jnp.where(row < n - s, pltpu.roll(u, n - s, 0), 0.0)
        u = b * u_sh + u
        b = b * b_sh
        s *= 2
    while s < n:
        u = jnp.concatenate([b[:n - s] * u[s:] + u[:n - s], u[n - s:]], axis=0)
        b = jnp.concatenate([b[:n - s] * b[s:], b[n - s:]], axis=0)
        s *= 2
    return b, u


LANES = 128


def _chunk_scan(a, u, a_s, u_s, hl_s, al_s, carry, reverse):
    n, width = a.shape
    groups = n // SUBLANES
    order = range(SUBLANES - 1, -1, -1) if reverse else range(SUBLANES)
    row = lax.broadcasted_iota(jnp.int32, (groups, LANES), 0)
    for l in range(width // LANES):
        lanes = slice(l * LANES, (l + 1) * LANES)
        a_l, u_l, hl_l, al_l = a_s.at[l], u_s.at[l], hl_s.at[l], al_s.at[l]
        a_l[...] = a[:, lanes]
        u_l[...] = u[:, lanes]
        h_loc = a_loc = None
        for r in order:
            rows = pl.ds(r, groups, stride=SUBLANES)
            a_r, u_r = a_l[rows, :], u_l[rows, :]
            h_loc, a_loc = (u_r, a_r) if h_loc is None else (a_r * h_loc + u_r, a_r * a_loc)
            hl_l[rows, :] = h_loc
            al_l[rows, :] = a_loc
        if reverse:
            a_cum, ends = _scan_up(a_loc, h_loc, row)
            ends = ends + a_cum * carry[:, lanes]
            enters = jnp.where(row == groups - 1, carry[:, lanes], pltpu.roll(ends, groups - 1, 0))
        else:
            a_cum, ends = _scan_down(a_loc, h_loc, row)
            ends = ends + a_cum * carry[:, lanes]
            enters = jnp.where(row == 0, carry[:, lanes], pltpu.roll(ends, 1, 0))
        for r in range(SUBLANES):
            rows = pl.ds(r, groups, stride=SUBLANES)
            hl_l[rows, :] = hl_l[rows, :] + al_l[rows, :] * enters
    return jnp.concatenate([hl_s[l] for l in range(width // LANES)], axis=1)


def _rnn_fwd(proj, conv_w, conv_b, wa_bd, wx_bd, b_a, b_x, lam, token):
    T = proj.shape[0]
    tc, ct = RNN_CHUNK, RNN_TILE
    nt = T // tc

    def body(x_ref, rg_ref, cw_ref, cb_ref, wa_ref, wx_ref, ba_ref, bx_ref, lam_ref, token_ref, h_ref, z_ref, xtail,
             hcarry, a_s, u_s, hl_s, al_s):
        @pl.when(pl.program_id(1) == 0)
        def _():
            xtail[...] = jnp.zeros_like(xtail)
            hcarry[...] = jnp.zeros_like(hcarry)

        row = lax.broadcasted_iota(jnp.int32, (tc, ct), 0)
        x = x_ref[...].astype(F32)
        taps = _conv_taps(x, xtail[...], row)
        c = cb_ref[...] + cw_ref[pl.ds(0, 1), :] * taps[0]
        for k in range(1, CONV_W):
            c = c + cw_ref[pl.ds(k, 1), :] * taps[k]
        xtail[...] = x[tc - 8:, :]
        _, _, i, a, mult = _rglru_gates(c, wa_ref[...], wx_ref[...], ba_ref[...], bx_ref[...], lam_ref[...])
        h = _chunk_scan(a, mult * (i * c), a_s, u_s, hl_s, al_s, hcarry[...], reverse=False)
        h_ref[...] = h
        hcarry[...] = h_ref[pl.ds(tc - 1, 1), :]
        rg = rg_ref[...].astype(F32)
        z_ref[...] = (h * (rg * _sigmoid(rg))).astype(BF16)

    col = lambda off: (lambda j, t: (t, off + j))
    vec = pl.BlockSpec((1, ct), lambda j, t: (0, j))
    mat = pl.BlockSpec((None, ct, ct), lambda j, t: (j, 0, 0))
    return pl.pallas_call(
        body,
        name="rnn_fwd",
        grid=(D_RNN // ct, nt),
        in_specs=[
            pl.BlockSpec((tc, ct), col(COL_RNN_X)),
            pl.BlockSpec((tc, ct), col(COL_RNN_GATE)),
            pl.BlockSpec((CONV_W, ct), lambda j, t: (0, j)),
            vec, mat, mat, vec, vec, vec,
            pl.BlockSpec((8, 128), lambda j, t: (0, 0)),
        ],
        out_specs=[pl.BlockSpec((tc, ct), lambda j, t: (t, j)), pl.BlockSpec((tc, ct), lambda j, t: (t, j))],
        out_shape=[_sds((T, D_RNN), F32), _sds((T, D_RNN), BF16)],
        scratch_shapes=[pltpu.VMEM((8, ct), F32), pltpu.VMEM((1, ct), F32)] + [
            pltpu.VMEM((ct // LANES, tc, LANES), F32)] * 4,
        compiler_params=_params(("parallel", "arbitrary"), 32),
    )(*_hbm(proj, proj, conv_w, conv_b, wa_bd, wx_bd, b_a, b_x, lam, token))


def _rnn_bwd(proj, y_rnn, dz_rnn, conv_w, conv_b, wa_bd, wx_bd, b_a, b_x, lam):
    T = proj.shape[0]
    tc, ct = RNN_CHUNK, RNN_TILE
    nt = T // tc
    hb = tc // 8

    def body(x_ref, xh_ref, rg_ref, h_ref, hh_ref, dz_ref, cw_ref, cb_ref, wa_ref, wx_ref, ba_ref, bx_ref, lam_ref,
             dx_ref, drg_ref, dwa_ref, dwx_ref, sm_ref, lam_carry, a_carry, dc_head, b_s, dy_s, hl_s, al_s):
        t = pl.program_id(1)
        first_chunk = t == nt - 1

        @pl.when(t == 0)
        def _():
            lam_carry[...] = jnp.zeros_like(lam_carry)
            a_carry[...] = jnp.zeros_like(a_carry)
            dc_head[...] = jnp.zeros_like(dc_head)
            dwa_ref[...] = jnp.zeros_like(dwa_ref)
            dwx_ref[...] = jnp.zeros_like(dwx_ref)
            sm_ref[...] = jnp.zeros_like(sm_ref)

        row = lax.broadcasted_iota(jnp.int32, (tc, ct), 0)
        keep = jnp.where(first_chunk, 0.0, 1.0)
        x = x_ref[...].astype(F32)
        xtail = xh_ref[...].astype(F32)[8:16, :] * keep
        taps = _conv_taps(x, xtail, row)
        c = cb_ref[...] + cw_ref[pl.ds(0, 1), :] * taps[0]
        for k in range(1, CONV_W):
            c = c + cw_ref[pl.ds(k, 1), :] * taps[k]
        lam = lam_ref[...]
        cb, r, i, a, mult = _rglru_gates(c, wa_ref[...], wx_ref[...], ba_ref[...], bx_ref[...], lam)
        h = h_ref[...]
        h_prev = _shift_down(h, hh_ref[...] * keep, 1, row)
        rg = rg_ref[...].astype(F32)
        dz = dz_ref[...]
        sg = _sigmoid(rg)
        drg_ref[...] = (dz * h * (sg * (1.0 + rg * (1.0 - sg)))).astype(BF16)
        dy = dz * (rg * sg)
        b = jnp.where(row >= tc - 1, a_carry[pl.ds(0, 1), :], pltpu.roll(a, tc - 1, 0))
        lt = _chunk_scan(b, dy, b_s, dy_s, hl_s, al_s, lam_carry[pl.ds(0, 1), :], reverse=True)
        lam_carry[...] = lt[0:8, :]
        a_carry[...] = a[0:8, :]
        ic = i * c
        dmult = lt * ic
        di = lt * mult * c
        dc = lt * mult * i
        dlog_a = a * (lt * h_prev - dmult * a / mult)
        sp = _softplus(-lam)
        dpre_r = dlog_a * ((-LRU_C) * sp) * (r * (1.0 - r))
        dpre_i = di * (i * (1.0 - i))
        dlam_row = jnp.sum(dlog_a * r, axis=0, keepdims=True) * (LRU_C * _sigmoid(-lam))
        dpr_b = dpre_r.astype(BF16)
        dpi_b = dpre_i.astype(BF16)
        dwa_ref[...] += _dot_tn(cb, dpr_b)
        dwx_ref[...] += _dot_tn(cb, dpi_b)
        dc = dc + _dot_nt(dpr_b, wa_ref[...]) + _dot_nt(dpi_b, wx_ref[...])
        head = dc_head[...]
        dx = cw_ref[pl.ds(3, 1), :] * dc
        for m in range(1, CONV_W):
            dx = dx + cw_ref[pl.ds(3 - m, 1), :] * _shift_up(dc, head, m, row)
        dx_ref[...] = dx.astype(BF16)
        dc_head[...] = dc[0:8, :]
        sm_ref[pl.ds(0, 1), :] += jnp.sum(dpre_r, axis=0, keepdims=True)
        sm_ref[pl.ds(1, 1), :] += jnp.sum(dpre_i, axis=0, keepdims=True)
        sm_ref[pl.ds(2, 1), :] += dlam_row
        sm_ref[pl.ds(3, 1), :] += jnp.sum(dc, axis=0, keepdims=True)
        for k in range(CONV_W):
            sm_ref[pl.ds(4 + k, 1), :] += jnp.sum(dc * taps[k], axis=0, keepdims=True)

    rev = lambda off: (lambda j, t: (nt - 1 - t, off + j))
    halo = lambda off: (lambda j, t: (jnp.maximum((nt - 1 - t) * hb - 1, 0), off + j))
    halo16 = lambda off: (lambda j, t: (jnp.maximum((nt - 1 - t) * (hb // 2) - 1, 0), off + j))
    vec = pl.BlockSpec((1, ct), lambda j, t: (0, j))
    mat = pl.BlockSpec((None, ct, ct), lambda j, t: (j, 0, 0))
    return pl.pallas_call(
        body,
        name="rnn_bwd",
        grid=(D_RNN // ct, nt),
        in_specs=[
            pl.BlockSpec((tc, ct), rev(COL_RNN_X)),
            pl.BlockSpec((16, ct), halo16(COL_RNN_X)),
            pl.BlockSpec((tc, ct), rev(COL_RNN_GATE)),
            pl.BlockSpec((tc, ct), rev(0)),
            pl.BlockSpec((8, ct), halo(0)),
            pl.BlockSpec((tc, ct), rev(0)),
            pl.BlockSpec((CONV_W, ct), lambda j, t: (0, j)),
            vec, mat, mat, vec, vec, vec,
        ],
        out_specs=[
            pl.BlockSpec((tc, ct), rev(0)),
            pl.BlockSpec((tc, ct), rev(0)),
            mat, mat,
            pl.BlockSpec((8, ct), lambda j, t: (0, j)),
        ],
        out_shape=[_sds((T, D_RNN), BF16), _sds((T, D_RNN), BF16), _sds((D_RNN // ct, ct, ct), F32),
                   _sds((D_RNN // ct, ct, ct), F32), _sds((8, D_RNN), F32)],
        scratch_shapes=[pltpu.VMEM((8, ct), F32)] * 3 + [pltpu.VMEM((ct // LANES, tc, LANES), F32)] * 4,
        compiler_params=_params(("parallel", "arbitrary"), 32),
    )(*_hbm(proj, proj, proj, y_rnn, y_rnn, dz_rnn, conv_w, conv_b, wa_bd, wx_bd, b_a, b_x, lam))


def _attn_bias():
    qi = jnp.arange(BLOCK)[:, None]
    kj = jnp.arange(BLOCK)[None, :]
    dist_cur = (qi - kj).astype(F32)
    slopes = 2.0 ** (-ALIBI_MAX_BIAS * jnp.arange(1, N_Q_HEADS + 1, dtype=F32) / N_Q_HEADS)
    slopes = slopes[:, None, None]
    prev = jnp.where(kj > qi, -slopes * (dist_cur + float(BLOCK)), NEG_BIG)
    cur = jnp.where(kj <= qi, -slopes * dist_cur, NEG_BIG)
    later = jnp.concatenate([prev, cur], axis=-1)
    first = jnp.concatenate([jnp.full_like(prev, NEG_BIG), cur], axis=-1)
    return jnp.stack([first, later])


def _attn_exps(s_prev, s_cur, sink, bias):
    s_prev = s_prev + bias[:, 0:BLOCK]
    s_cur = s_cur + bias[:, BLOCK:2 * BLOCK]
    m = jnp.maximum(jnp.max(jnp.maximum(s_prev, s_cur), axis=-1, keepdims=True), sink)
    p_prev = jnp.exp(s_prev - m)
    p_cur = jnp.exp(s_cur - m)
    total = jnp.sum(p_prev + p_cur, axis=-1, keepdims=True) + jnp.exp(sink - m)
    return p_prev, p_cur, 1.0 / total, m + jnp.log(total)


def _attn_probs(s_prev, s_cur, sink, bias, lse):
    p_prev = jnp.exp((s_prev + bias[:, 0:BLOCK]) - lse)
    p_cur = jnp.exp((s_cur + bias[:, BLOCK:2 * BLOCK]) - lse)
    return p_prev, p_cur, jnp.exp(sink - lse)


def _stack_heads(ref_or_val, hk, dtype):
    parts = [ref_or_val[:, (GROUP * hk + g) * HEAD_DIM:(GROUP * hk + g + 1) * HEAD_DIM] for g in range(GROUP)]
    return jnp.concatenate(parts, axis=0).astype(dtype)


ATTN_SCALE = HEAD_DIM ** -0.5


def _bias_spec():
    return pl.BlockSpec((None, N_Q_HEADS, BLOCK, 2 * BLOCK), lambda i: (jnp.minimum(i, 1), 0, 0, 0))


def _attn_fwd(proj, sinks, bias):
    T = proj.shape[0]
    nb = T // BLOCK

    def body(sink_ref, bias_ref, q_ref, kp_ref, kc_ref, vp_ref, vc_ref, ag0_ref, ag1_ref, y_ref, z_ref, lse_ref):
        kvs = [slice(hk * HEAD_DIM, (hk + 1) * HEAD_DIM) for hk in range(N_KV_HEADS)]
        qgs = [(_stack_heads(q_ref, hk, F32) * ATTN_SCALE).astype(BF16) for hk in range(N_KV_HEADS)]
        s_prev = [_dot_nt(qgs[hk], kp_ref[:, kvs[hk]].astype(BF16)) for hk in range(N_KV_HEADS)]
        s_cur = [_dot_nt(qgs[hk], kc_ref[:, kvs[hk]].astype(BF16)) for hk in range(N_KV_HEADS)]
        for hk in range(N_KV_HEADS):
            pp, pc, invs = [], [], []
            for g in range(GROUP):
                h = GROUP * hk + g
                rows = slice(g * BLOCK, (g + 1) * BLOCK)
                p_prev, p_cur, inv, lse = _attn_exps(s_prev[hk][rows], s_cur[hk][rows], sink_ref[h], bias_ref[h])
                pp.append(p_prev.astype(BF16))
                pc.append(p_cur.astype(BF16))
                invs.append(inv)
                lse_ref[:, h:h + 1] = lse
            og = _dot(jnp.concatenate(pp, axis=0), vp_ref[:, kvs[hk]].astype(BF16)) + _dot(
                jnp.concatenate(pc, axis=0), vc_ref[:, kvs[hk]].astype(BF16))
            for g in range(GROUP):
                h = GROUP * hk + g
                y_ref[:, h * HEAD_DIM:(h + 1) * HEAD_DIM] = og[g * BLOCK:(g + 1) * BLOCK] * invs[g]
        ag = jnp.concatenate([ag0_ref[...], ag1_ref[...]], axis=1).astype(F32)
        z_ref[...] = (y_ref[...] * (ag * _sigmoid(ag))).astype(BF16)

    prev = lambda c: (lambda i: (jnp.maximum(i - 1, 0), c))
    cur = lambda c: (lambda i: (i, c))
    return pl.pallas_call(
        body,
        name="attn_fwd",
        grid=(nb,),
        in_specs=[
            pl.BlockSpec(memory_space=pltpu.SMEM),
            _bias_spec(),
            pl.BlockSpec((BLOCK, 1024), lambda i: (i, COL_Q // 4)),
            pl.BlockSpec((BLOCK, D_KV), prev(COL_K)),
            pl.BlockSpec((BLOCK, D_KV), cur(COL_K)),
            pl.BlockSpec((BLOCK, D_KV), prev(COL_V)),
            pl.BlockSpec((BLOCK, D_KV), cur(COL_V)),
            pl.BlockSpec((BLOCK, 512), lambda i: (i, COL_ATTN_GATE // 2)),
            pl.BlockSpec((BLOCK, 512), lambda i: (i, COL_ATTN_GATE // 2 + 1)),
        ],
        out_specs=[pl.BlockSpec((BLOCK, 1024), lambda i: (i, 0)), pl.BlockSpec((BLOCK, 1024), lambda i: (i, 0)),
                   pl.BlockSpec((BLOCK, N_Q_HEADS), lambda i: (i, 0))],
        out_shape=[_sds((T, 1024), F32), _sds((T, 1024), BF16), _sds((T, N_Q_HEADS), F32)],
        compiler_params=_params(("arbitrary",), 32),
    )(sinks, *_hbm(bias, proj, proj, proj, proj, proj, proj, proj))


def _attn_bwd(proj, y_attn, lse, dz_attn, sinks, bias, token):
    T = proj.shape[0]
    nb = T // BLOCK

    def body(sink_ref, bias_ref, q_ref, kp_ref, kc_ref, vp_ref, vc_ref, ag0_ref, ag1_ref, y_ref, lse_ref, dz_ref,
             token_ref, dq_ref, dk_ref, dv_ref, dag_ref, ds_ref, dy_s):
        i = pl.program_id(0)

        @pl.when(i == 0)
        def _():
            ds_ref[...] = jnp.zeros_like(ds_ref)

        lane = lax.broadcasted_iota(jnp.int32, (8, 128), 1)
        sub = lax.broadcasted_iota(jnp.int32, (8, 128), 0)
        ag = jnp.concatenate([ag0_ref[...], ag1_ref[...]], axis=1).astype(F32)
        dz = dz_ref[...]
        sg = _sigmoid(ag)
        dag_ref[...] = (dz * y_ref[...] * (sg * (1.0 + ag * (1.0 - sg)))).astype(BF16)
        dy_s[...] = dz * (ag * sg)
        r_cur = pl.multiple_of(i * BLOCK, BLOCK)
        r_prev = pl.multiple_of(jnp.maximum(i - 1, 0) * BLOCK, BLOCK)
        dk_cur, dv_cur, dk_prev, dv_prev = [], [], [], []
        ds_acc = jnp.zeros((8, 128), F32)
        for hk in range(N_KV_HEADS):
            ks = slice(hk * HEAD_DIM, (hk + 1) * HEAD_DIM)
            qg = (_stack_heads(q_ref, hk, F32) * ATTN_SCALE).astype(BF16)
            dog = _stack_heads(dy_s, hk, F32)
            og = _stack_heads(y_ref, hk, F32)
            dog_b = dog.astype(BF16)
            kp = kp_ref[:, ks].astype(BF16)
            kc = kc_ref[:, ks].astype(BF16)
            vp = vp_ref[:, ks].astype(BF16)
            vc = vc_ref[:, ks].astype(BF16)
            s_prev = _dot_nt(qg, kp)
            s_cur = _dot_nt(qg, kc)
            dp_prev = _dot_nt(dog_b, vp)
            dp_cur = _dot_nt(dog_b, vc)
            dvec = jnp.sum(dog * og, axis=-1, keepdims=True)
            pp, pc, dsp, dsc = [], [], [], []
            for g in range(GROUP):
                h = GROUP * hk + g
                rows = slice(g * BLOCK, (g + 1) * BLOCK)
                p_prev, p_cur, p_sink = _attn_probs(
                    s_prev[rows], s_cur[rows], sink_ref[h], bias_ref[h], lse_ref[:, h:h + 1])
                d_h = dvec[rows]
                pp.append(p_prev.astype(BF16))
                pc.append(p_cur.astype(BF16))
                dsp.append((p_prev * (dp_prev[rows] - d_h)).astype(BF16))
                dsc.append((p_cur * (dp_cur[rows] - d_h)).astype(BF16))
                dsink = -jnp.sum(p_sink * d_h, axis=0, keepdims=True)
                ds_acc = ds_acc + jnp.where(jnp.logical_and(lane == h, sub == 1), dsink, 0.0)
            pp = jnp.concatenate(pp, axis=0)
            pc = jnp.concatenate(pc, axis=0)
            dsp = jnp.concatenate(dsp, axis=0)
            dsc = jnp.concatenate(dsc, axis=0)
            dqg = (_dot(dsp, kp) + _dot(dsc, kc)) * ATTN_SCALE
            for g in range(GROUP):
                h = GROUP * hk + g
                dq_ref[:, h * HEAD_DIM:(h + 1) * HEAD_DIM] = dqg[g * BLOCK:(g + 1) * BLOCK].astype(BF16)
            dk_ref[pl.ds(r_cur, BLOCK), ks] = _dot_tn(dsc, qg)
            dv_ref[pl.ds(r_cur, BLOCK), ks] = _dot_tn(pc, dog_b)
            dk_prev.append(_dot_tn(dsp, qg))
            dv_prev.append(_dot_tn(pp, dog_b))
        ds_ref[:, 0:128] += ds_acc

        @pl.when(i > 0)
        def _():
            for hk in range(N_KV_HEADS):
                ks = slice(hk * HEAD_DIM, (hk + 1) * HEAD_DIM)
                dk_ref[pl.ds(r_prev, BLOCK), ks] += dk_prev[hk]
                dv_ref[pl.ds(r_prev, BLOCK), ks] += dv_prev[hk]

    prev = lambda c: (lambda i: (jnp.maximum(i - 1, 0), c))
    cur = lambda c: (lambda i: (i, c))
    blk = pl.BlockSpec((BLOCK, 1024), lambda i: (i, 0))
    whole = pl.BlockSpec((T, D_KV), lambda i: (0, 0))
    return pl.pallas_call(
        body,
        name="attn_bwd",
        grid=(nb,),
        in_specs=[
            pl.BlockSpec(memory_space=pltpu.SMEM),
            _bias_spec(),
            pl.BlockSpec((BLOCK, 1024), lambda i: (i, COL_Q // 4)),
            pl.BlockSpec((BLOCK, D_KV), prev(COL_K)),
            pl.BlockSpec((BLOCK, D_KV), cur(COL_K)),
            pl.BlockSpec((BLOCK, D_KV), prev(COL_V)),
            pl.BlockSpec((BLOCK, D_KV), cur(COL_V)),
            pl.BlockSpec((BLOCK, 512), lambda i: (i, COL_ATTN_GATE // 2)),
            pl.BlockSpec((BLOCK, 512), lambda i: (i, COL_ATTN_GATE // 2 + 1)),
            blk,
            pl.BlockSpec((BLOCK, N_Q_HEADS), lambda i: (i, 0)),
            blk,
            pl.BlockSpec((8, 128), lambda i: (0, 0)),
        ],
        out_specs=[blk, whole, whole, blk, pl.BlockSpec((8, 1024), lambda i: (0, 0))],
        out_shape=[_sds((T, 1024), BF16), _sds((T, D_KV), F32), _sds((T, D_KV), F32), _sds((T, 1024), BF16),
                   _sds((8, 1024), F32)],
        scratch_shapes=[pltpu.VMEM((BLOCK, 1024), F32)],
        compiler_params=_params(("arbitrary",), 48),
    )(sinks, *_hbm(bias, proj, proj, proj, proj, proj, proj, proj, y_attn, lse, dz_attn, token))


def _head(x, target, z_rnn, z_attn, proj, b_gate, g_post, w_rnn_out, w_attn_out, w_out):
    T = x.shape[0]
    tm = 256

    def body(x_ref, t_ref, zr_ref, za_ref, ml0_ref, ml1_ref, ml2_ref, ml3_ref, bg_ref, gp_ref, wr_ref, wa_ref, wo_ref,
             dyx_ref, dzr_ref, dza_ref, dml_ref, mb_ref, dout_ref, dbr_ref, dba_ref, sm_ref):
        @pl.when(pl.program_id(0) == 0)
        def _():
            sm_ref[...] = jnp.zeros_like(sm_ref)

        wr, wa, wo = wr_ref[...], wa_ref[...], wo_ref[...]
        br_rnn = _dot(zr_ref[...], wr)
        br_attn = _dot(za_ref[...], wa)
        ml_rnn = jnp.concatenate([ml0_ref[...], ml1_ref[...]], axis=1).astype(F32)
        ml_attn = jnp.concatenate([ml2_ref[...], ml3_ref[...]], axis=1).astype(F32)
        g_rnn = _sigmoid(ml_rnn + bg_ref[:, 0:D_MODEL])
        g_attn = _sigmoid(ml_attn + bg_ref[:, D_MODEL:2 * D_MODEL])
        mb = (g_rnn * br_rnn + g_attn * br_attn).astype(BF16)
        mb_ref[...] = mb
        out = _dot(mb, wo)
        rstd = lax.rsqrt(jnp.mean(out * out, axis=-1, keepdims=True) + EPS)
        n = out * rstd
        gp = gp_ref[...]
        err = (x_ref[...] + n * gp) - t_ref[...]
        sm_ref[pl.ds(3, 1), :] += 0.5 * jnp.sum(jnp.mean(err * err, axis=-1, keepdims=True), axis=0, keepdims=True)
        dy = err * (1.0 / D_MODEL)
        dyx_ref[...] = dy
        sm_ref[pl.ds(0, 1), :] += jnp.sum(dy * n, axis=0, keepdims=True)
        dn = dy * gp
        dout = (rstd * (dn - n * jnp.mean(dn * n, axis=-1, keepdims=True))).astype(BF16)
        dout_ref[...] = dout
        dmerged = _dot_nt(dout, wo)
        dml_r = (dmerged * br_rnn) * (g_rnn * (1.0 - g_rnn))
        dml_a = (dmerged * br_attn) * (g_attn * (1.0 - g_attn))
        dml_ref[:, 0:D_MODEL] = dml_r.astype(BF16)
        dml_ref[:, D_MODEL:2 * D_MODEL] = dml_a.astype(BF16)
        sm_ref[pl.ds(1, 1), :] += jnp.sum(dml_r, axis=0, keepdims=True)
        sm_ref[pl.ds(2, 1), :] += jnp.sum(dml_a, axis=0, keepdims=True)
        dbr = (dmerged * g_rnn).astype(BF16)
        dba = (dmerged * g_attn).astype(BF16)
        dbr_ref[...] = dbr
        dba_ref[...] = dba
        dzr_ref[...] = _dot_nt(dbr, wr)
        dza_ref[...] = _dot_nt(dba, wa)

    tile = pl.BlockSpec((tm, D_MODEL), lambda i: (i, 0))
    wspec = pl.BlockSpec((D_MODEL, D_MODEL), lambda i: (0, 0))
    ml = lambda q: pl.BlockSpec((tm, 512), lambda i: (i, COL_MERGE // 2 + q))
    return pl.pallas_call(
        body,
        name="head",
        grid=(T // tm,),
        in_specs=[
            tile, tile, tile, tile,
            ml(0), ml(1), ml(2), ml(3),
            pl.BlockSpec((1, 2 * D_MODEL), lambda i: (0, 0)),
            pl.BlockSpec((1, D_MODEL), lambda i: (0, 0)),
            wspec, wspec, wspec,
        ],
        out_specs=[
            tile, tile, tile,
            pl.BlockSpec((tm, 2 * D_MODEL), lambda i: (i, 0)),
            tile, tile, tile, tile,
            pl.BlockSpec((8, D_MODEL), lambda i: (0, 0)),
        ],
        out_shape=[
            _sds((T, D_MODEL), F32), _sds((T, D_MODEL), F32), _sds((T, D_MODEL), F32),
            _sds((T, 2 * D_MODEL), BF16),
            _sds((T, D_MODEL), BF16), _sds((T, D_MODEL), BF16), _sds((T, D_MODEL), BF16), _sds((T, D_MODEL), BF16),
            _sds((8, D_MODEL), F32),
        ],
        compiler_params=_params(("arbitrary",), 56),
    )(*_hbm(x, target, z_rnn, z_attn, proj, proj, proj, proj, b_gate, g_post, w_rnn_out, w_attn_out, w_out))


def _matmul_tn(a, b, name):
    T, M = a.shape
    N = b.shape[1]
    tk = min(512, T)
    nt = T // tk

    def body(a_ref, b_ref, o_ref, ob_ref):
        @pl.when(pl.program_id(0) == 0)
        def _():
            o_ref[...] = jnp.zeros_like(o_ref)

        o_ref[...] += _dot_tn(a_ref[...], b_ref[...])

        @pl.when(pl.program_id(0) == nt - 1)
        def _():
            ob_ref[...] = o_ref[...].astype(BF16)

    whole = pl.BlockSpec((M, N), lambda t: (0, 0))
    return pl.pallas_call(
        body,
        name=name,
        grid=(nt,),
        in_specs=[pl.BlockSpec((tk, M), lambda t: (t, 0)), pl.BlockSpec((tk, N), lambda t: (t, 0))],
        out_specs=[whole, whole],
        out_shape=[_sds((M, N), F32), _sds((M, N), BF16)],
        compiler_params=_params(("arbitrary",), 48),
    )(*_hbm(a, b))


DPROJ_WIDTHS = (D_RNN, D_RNN, 1024, D_KV, D_KV, 1024, 2 * D_MODEL)


def _dproj_segments():
    segs, start = [[] for _ in range(N_CHIPS)], 0
    for p, width in enumerate(DPROJ_WIDTHS):
        for c in range(N_CHIPS):
            lo, hi = max(start, c * W_IN_SHARD), min(start + width, (c + 1) * W_IN_SHARD)
            if lo < hi:
                segs[c].append((p, lo - start, hi - start, lo - c * W_IN_SHARD, hi - c * W_IN_SHARD))
        start += width
    return segs


def _dh_bwd(pieces, w_in_g, x, dyx, g_pre, token):
    T = x.shape[0]
    tm = min(512, T)
    n = len(pieces)
    segs = _dproj_segments()

    def body(*refs):
        p_refs, w_hbm, x_ref, dyx_ref, g_ref = refs[0:n], refs[n], refs[n + 1], refs[n + 2], refs[n + 3]
        gx_ref, dg_ref, w_ref = refs[n + 5], refs[n + 6], refs[n + 7]

        @pl.when(pl.program_id(0) == 0)
        def _():
            pltpu.sync_copy(w_hbm, w_ref)
            dg_ref[...] = jnp.zeros_like(dg_ref)

        dh = None
        for c in range(N_CHIPS):
            for p, a0, a1, u0, u1 in segs[c]:
                part = _dot_nt(p_refs[p][:, a0:a1].astype(BF16), w_ref[c, :, u0:u1])
                dh = part if dh is None else dh + part
        xv = x_ref[...]
        rstd = lax.rsqrt(jnp.mean(xv * xv, axis=-1, keepdims=True) + EPS)
        nx = xv * rstd
        dhg = dh * g_ref[...]
        gx_ref[...] = dyx_ref[...] + rstd * (dhg - nx * jnp.mean(dhg * nx, axis=-1, keepdims=True))
        dg_ref[pl.ds(0, 1), :] += jnp.sum(dh * nx, axis=0, keepdims=True)

    tile = pl.BlockSpec((tm, D_MODEL), lambda i: (i, 0))
    return pl.pallas_call(
        body,
        name="dh_bwd",
        grid=(T // tm,),
        in_specs=[pl.BlockSpec((tm, w), lambda i: (i, 0)) for w in DPROJ_WIDTHS] + [
            ANY, tile, tile,
            pl.BlockSpec((1, D_MODEL), lambda i: (0, 0)),
            pl.BlockSpec((8, 128), lambda i: (0, 0)),
        ],
        out_specs=[tile, pl.BlockSpec((8, D_MODEL), lambda i: (0, 0))],
        out_shape=[_sds((T, D_MODEL), F32), _sds((8, D_MODEL), F32)],
        scratch_shapes=[pltpu.VMEM(w_in_g.shape, BF16)],
        compiler_params=_params(("arbitrary",), 56),
    )(*_hbm(*pieces, w_in_g, x, dyx, g_pre, token))


def _dw_in(ht, pieces):
    T = ht.shape[1]
    tk = min(512, T)
    nt = T // tk
    n = len(pieces)
    segs = _dproj_segments()

    def body(*refs):
        h_ref, p_refs, o_ref, ob_ref = refs[0], refs[1:n + 1], refs[n + 1], refs[n + 2]

        @pl.when(pl.program_id(1) == 0)
        def _():
            o_ref[...] = jnp.zeros_like(o_ref)

        for c in range(N_CHIPS):
            @pl.when(pl.program_id(0) == c)
            def _():
                for p, a0, a1, u0, u1 in segs[c]:
                    o_ref[:, u0:u1] += _dot(h_ref[...], p_refs[p][:, a0:a1].astype(BF16))

        @pl.when(pl.program_id(1) == nt - 1)
        def _():
            ob_ref[...] = o_ref[...].astype(BF16)

    def piece_spec(p):
        chips = [c for c in range(N_CHIPS) if any(s[0] == p for s in segs[c])]

        def index(c, t):
            used = functools.reduce(jnp.logical_or, [c == k for k in chips])
            return (jnp.where(used, t, 0), 0)

        return pl.BlockSpec((tk, DPROJ_WIDTHS[p]), index)

    return pl.pallas_call(
        body,
        name="dw_in",
        grid=(N_CHIPS, nt),
        in_specs=[pl.BlockSpec((D_MODEL, tk), lambda c, t: (0, t))] + [piece_spec(p) for p in range(n)],
        out_specs=[pl.BlockSpec((None, D_MODEL, W_IN_SHARD), lambda c, t: (c, 0, 0))] * 2,
        out_shape=[_sds((N_CHIPS, D_MODEL, W_IN_SHARD), F32), _sds((N_CHIPS, D_MODEL, W_IN_SHARD), BF16)],
        compiler_params=_params(("parallel", "arbitrary"), 56),
    )(*_hbm(ht, *pieces))


ELEMENTWISE_TILE_BYTES = MIB


def _row_tile(rows, cols):
    if rows * cols * 4 <= ELEMENTWISE_TILE_BYTES:
        return rows
    for t in (512, 256, 128, 64, 32, 16, 8):
        if rows % t == 0 and t * cols * 4 <= ELEMENTWISE_TILE_BYTES:
            return t
    return rows


def _pair_sum(g, got, core, name):
    nch, R, C = g.shape
    h = R // 2
    tr = _row_tile(h, C)
    nt = h // tr

    def body(c_ref, g_ref, got_ref, p_ref, pb_ref):
        s = g_ref[...] + got_ref[...].astype(F32)
        p_ref[...] = s
        pb_ref[...] = s.astype(BF16)

    blk = pl.BlockSpec((None, tr, C), lambda j, i, c_ref: (j, i, 0))
    return pl.pallas_call(
        body,
        name=name,
        grid_spec=pltpu.PrefetchScalarGridSpec(
            num_scalar_prefetch=1,
            grid=(nch, nt),
            in_specs=[pl.BlockSpec((None, tr, C), lambda j, i, c_ref: (j, c_ref[0] * nt + i, 0)), blk],
            out_specs=[blk, blk],
        ),
        out_shape=[_sds((nch, h, C), F32), _sds((nch, h, C), BF16)],
        compiler_params=_params(("parallel", "parallel"), 48),
    )(core, *_hbm(g, got))


def _chip_sum(p, got, chip_core, name):
    _, h, C = p.shape
    tr = _row_tile(h, C)
    nt = h // tr

    def body(jc_ref, p_ref, g0_ref, g1_ref, g2_ref, o_ref):
        o_ref[...] = ((p_ref[...] + g0_ref[...].astype(F32)) + g1_ref[...].astype(F32)) + g2_ref[...].astype(F32)

    rel = lambda r: pl.BlockSpec((None, tr, C), lambda i, jc_ref: (r, i, 0))
    return pl.pallas_call(
        body,
        name=name,
        grid_spec=pltpu.PrefetchScalarGridSpec(
            num_scalar_prefetch=1,
            grid=(nt,),
            in_specs=[pl.BlockSpec((None, tr, C), lambda i, jc_ref: (jc_ref[0], i, 0)), rel(0), rel(1), rel(2)],
            out_specs=pl.BlockSpec((tr, C), lambda i, jc_ref: (jc_ref[1] * nt + i, 0)),
        ),
        out_shape=_sds((2 * h, C), F32),
        compiler_params=_params(("parallel",), 48),
    )(chip_core, *_hbm(p, got, got, got))


def _place_shards(shards, chip, name):
    n = len(shards)
    tiles = [_row_tile(s.shape[0], s.shape[1]) for s in shards]
    steps = max(s.shape[0] // t for s, t in zip(shards, tiles))
    tiles = [s.shape[0] // steps for s in shards]

    def body(j_ref, *refs):
        for a in range(n):
            refs[n + a][...] = refs[a][...].astype(BF16)

    return pl.pallas_call(
        body,
        name=name,
        grid_spec=pltpu.PrefetchScalarGridSpec(
            num_scalar_prefetch=1,
            grid=(steps,),
            in_specs=[pl.BlockSpec((t, s.shape[1]), lambda i, j_ref: (i, 0)) for s, t in zip(shards, tiles)],
            out_specs=[pl.BlockSpec((None, t, s.shape[1]), lambda i, j_ref: (j_ref[0], i, 0))
                       for s, t in zip(shards, tiles)],
        ),
        out_shape=[_sds((N_CHIPS,) + s.shape, BF16) for s in shards],
        compiler_params=_params(("parallel",), 48),
    )(chip, *_hbm(*shards))


def _adamw(w, g, m, v, name):
    R, C = w.shape
    tr = _row_tile(R, C)
    c1 = 1.0 - ADAM_B1 ** ADAM_STEP
    c2 = 1.0 - ADAM_B2 ** ADAM_STEP

    def body(w_ref, g_ref, m_ref, v_ref, d_ref, nm_ref, nv_ref):
        g = g_ref[...]
        nm = ADAM_B1 * m_ref[...] + (1.0 - ADAM_B1) * g
        nv = ADAM_B2 * v_ref[...] + (1.0 - ADAM_B2) * (g * g)
        nm_ref[...] = nm
        nv_ref[...] = nv
        d_ref[...] = (-ADAM_LR) * ((nm / c1) / (jnp.sqrt(nv / c2) + ADAM_EPS) + ADAM_WD * w_ref[...])

    spec = pl.BlockSpec((tr, C), lambda i: (i, 0))
    return pl.pallas_call(
        body, name=name, grid=(R // tr,), in_specs=[spec] * 4, out_specs=[spec] * 3,
        out_shape=[_sds((R, C), F32)] * 3, compiler_params=_params(("parallel",), 48),
    )(*_hbm(w, g, m, v))


def _place():
    return lax.axis_index("x"), lax.axis_index("y"), lax.axis_index("c")


def _chip_of(x, y, r):
    return (x ^ (r >> 1), y ^ (r & 1))


ANY = pl.BlockSpec(memory_space=pl.ANY)


def _gather_weights(placed, cw8):
    nbig = len(placed)
    halves = [s.shape[1] // 2 for s in placed]
    pieces = [max(1, h // 128) for h in halves]
    rows = [h // p for h, p in zip(halves, pieces)]
    order = [(a, q) for q in range(max(pieces)) for a in range(nbig) if q < pieces[a]]
    ici_sem = {(a, q, r): 3 * i + (r - 1) for i, (a, q) in enumerate(order) for r in (1, 2, 3)}
    cw_sem = {r: 3 * len(order) + (r - 1) for r in (1, 2, 3)}
    d2d_sem = {key: 3 * len(order) + 3 + k for key, k in ici_sem.items()}
    nsem = 6 * len(order) + 3

    def body(*refs):
        cw_ref, dsts, gcw_ref = refs[nbig], refs[nbig + 1:2 * nbig + 1], refs[2 * nbig + 1]
        send_sems, recv_sems = refs[2 * nbig + 2:]
        x, y, c = _place()
        j = 2 * x + y

        def piece_rows(a, q, core):
            return pl.ds(pl.multiple_of(core * halves[a] + q * rows[a], 16), rows[a])

        def ici(a, q, r):
            tx, ty = _chip_of(x, y, r)
            k = ici_sem[(a, q, r)]
            region = dsts[a].at[j, piece_rows(a, q, c), :]
            return pltpu.make_async_remote_copy(
                src_ref=region, dst_ref=region, send_sem=send_sems.at[k], recv_sem=recv_sems.at[k],
                device_id=(tx, ty, c), device_id_type=MESH)

        def ici_landed(a, q, r):
            tx, ty = _chip_of(x, y, r)
            k = ici_sem[(a, q, r)]
            region = dsts[a].at[2 * tx + ty, piece_rows(a, q, c), :]
            return pltpu.make_async_remote_copy(
                src_ref=region, dst_ref=region, send_sem=send_sems.at[k], recv_sem=recv_sems.at[k],
                device_id=(tx, ty, c), device_id_type=MESH)

        def d2d(a, q, r, core):
            tx, ty = _chip_of(x, y, r)
            k = d2d_sem[(a, q, r)]
            region = dsts[a].at[2 * tx + ty, piece_rows(a, q, core), :]
            return pltpu.make_async_remote_copy(
                src_ref=region, dst_ref=region, send_sem=send_sems.at[k], recv_sem=recv_sems.at[k],
                device_id=(x, y, 1 - c), device_id_type=MESH)

        def cw_copy(r):
            tx, ty = _chip_of(x, y, r)
            k = cw_sem[r]
            return pltpu.make_async_remote_copy(
                src_ref=cw_ref, dst_ref=gcw_ref.at[j], send_sem=send_sems.at[k], recv_sem=recv_sems.at[k],
                device_id=(tx, ty, c), device_id_type=MESH)

        def cw_landed(r):
            tx, ty = _chip_of(x, y, r)
            k = cw_sem[r]
            region = gcw_ref.at[2 * tx + ty]
            return pltpu.make_async_remote_copy(
                src_ref=region, dst_ref=region, send_sem=send_sems.at[k], recv_sem=recv_sems.at[k],
                device_id=(tx, ty, c), device_id_type=MESH)

        first = [ici(a, q, r) for (a, q) in order for r in (1, 2, 3)] + [cw_copy(r) for r in (1, 2, 3)]
        for cp in first:
            cp.start()
        passed = []
        for (a, q) in order:
            for r in (1, 2, 3):
                ici_landed(a, q, r).wait_recv()
                cp = d2d(a, q, r, c)
                cp.start()
                passed.append(cp)
        for r in (1, 2, 3):
            cw_landed(r).wait_recv()
        for (a, q) in order:
            for r in (1, 2, 3):
                d2d(a, q, r, 1 - c).wait_recv()
        for cp in first + passed:
            cp.wait_send()

    return pl.pallas_call(
        body,
        name="gather_weights",
        in_specs=[ANY] * (nbig + 1),
        out_specs=[ANY] * (nbig + 1),
        out_shape=[_sds(s.shape, s.dtype) for s in placed] + [_sds((N_CHIPS,) + cw8.shape, cw8.dtype)],
        input_output_aliases={a: a for a in range(nbig)},
        scratch_shapes=[pltpu.SemaphoreType.DMA((nsem,)), pltpu.SemaphoreType.DMA((nsem,))],
    )(*placed, cw8)


def _gather_late_start(placed, after, name):
    n = len(placed)
    halves = [s.shape[1] // 2 for s in placed]

    def body(*refs):
        g_refs = refs[0:n]
        send_sems, recv_sems, token = refs[n + 1], refs[n + 2], refs[-1]
        x, y, c = _place()
        j = 2 * x + y
        for a in range(n):
            mine = g_refs[a].at[j, pl.ds(pl.multiple_of(c * halves[a], 16), halves[a]), :]
            for r in (1, 2, 3):
                tx, ty = _chip_of(x, y, r)
                for to_core in (0, 1):
                    k = ((a * 3 + (r - 1)) * 2 + c) * 2 + to_core
                    pltpu.make_async_remote_copy(
                        src_ref=mine, dst_ref=mine, send_sem=send_sems.at[k], recv_sem=recv_sems.at[k],
                        device_id=(tx, ty, to_core), device_id_type=MESH).start()
        token[...] = jnp.zeros_like(token)

    hbm = lambda t: pltpu.HBM(t.shape, t.dtype)
    keep = lambda t: pltpu.with_memory_space_constraint(t, pltpu.HBM)
    nsem = 12 * n
    outs = pl.pallas_call(
        body,
        name=name,
        in_specs=[HBM] * n + [ANY],
        out_specs=(SEM, SEM, *[HBM] * n, pl.BlockSpec(memory_space=pltpu.VMEM)),
        out_shape=(pltpu.SemaphoreType.DMA((nsem,)), pltpu.SemaphoreType.DMA((nsem,)), *[hbm(p) for p in placed],
                   jax.ShapeDtypeStruct((8, 128), F32)),
        input_output_aliases={i: 2 + i for i in range(n)},
        compiler_params=pltpu.CompilerParams(has_side_effects=DATAFLOW),
    )(*[keep(p) for p in placed], after)
    return outs[0], outs[1], list(outs[2:2 + n]), outs[-1]


def _gather_late_wait(send_sems, recv_sems, thru, after, name):
    n = len(thru)
    halves = [s.shape[1] // 2 for s in thru]

    def body(*refs):
        g_refs = refs[0:n]
        send_sems, recv_sems = refs[n], refs[n + 1]
        x, y, c = _place()
        j = 2 * x + y
        for a in range(n):
            mine = g_refs[a].at[j, pl.ds(pl.multiple_of(c * halves[a], 16), halves[a]), :]
            for r in (1, 2, 3):
                tx, ty = _chip_of(x, y, r)
                for other in (0, 1):
                    k_out = ((a * 3 + (r - 1)) * 2 + c) * 2 + other
                    pltpu.make_async_remote_copy(
                        src_ref=mine, dst_ref=mine, send_sem=send_sems.at[k_out], recv_sem=recv_sems.at[k_out],
                        device_id=(tx, ty, other), device_id_type=MESH).wait_send()
                    k_in = ((a * 3 + (r - 1)) * 2 + other) * 2 + c
                    theirs = g_refs[a].at[2 * tx + ty, pl.ds(other * halves[a], halves[a]), :]
                    pltpu.make_async_remote_copy(
                        src_ref=theirs, dst_ref=theirs, send_sem=send_sems.at[k_in], recv_sem=recv_sems.at[k_in],
                        device_id=(tx, ty, other), device_id_type=MESH).wait_recv()

    hbm = lambda t: pltpu.HBM(t.shape, t.dtype)
    outs = pl.pallas_call(
        body,
        name=name,
        in_specs=[HBM] * n + [SEM, SEM, ANY],
        out_specs=[HBM] * n,
        out_shape=[hbm(t) for t in thru],
        input_output_aliases={i: i for i in range(n)},
        compiler_params=pltpu.CompilerParams(has_side_effects=DATAFLOW),
    )(*thru, send_sems, recv_sems, after)
    return list(outs)


D2D_PIECE_ROWS = 64


def _pair_exchange(grads, name):
    n = len(grads)
    halves = [g.shape[1] // 2 for g in grads]

    def body(*refs):
        g_refs, got_refs = refs[0:n], refs[n:2 * n]
        send_sems, recv_sems = refs[2 * n:]
        x, y, c = _place()

        def copy(a, src, dst):
            return pltpu.make_async_remote_copy(
                src_ref=src, dst_ref=dst, send_sem=send_sems.at[a], recv_sem=recv_sems.at[a],
                device_id=(x, y, 1 - c), device_id_type=MESH)

        for a in range(n):
            for jj in range(N_CHIPS):
                for q in range(halves[a] // D2D_PIECE_ROWS):
                    src_rows = pl.ds(pl.multiple_of((1 - c) * halves[a] + q * D2D_PIECE_ROWS, 16), D2D_PIECE_ROWS)
                    dst_rows = pl.ds(q * D2D_PIECE_ROWS, D2D_PIECE_ROWS)
                    copy(a, g_refs[a].at[jj, src_rows, :], got_refs[a].at[jj, dst_rows, :]).start()
        for a in range(n):
            sent = g_refs[a].at[:, pl.ds(pl.multiple_of((1 - c) * halves[a], 16), halves[a]), :]
            copy(a, sent, got_refs[a]).wait()

    return pl.pallas_call(
        body,
        name=name,
        in_specs=[ANY] * n,
        out_specs=[ANY] * n,
        out_shape=[_sds((N_CHIPS, h, g.shape[2]), g.dtype) for g, h in zip(grads, halves)],
        scratch_shapes=[pltpu.SemaphoreType.DMA((n,)), pltpu.SemaphoreType.DMA((n,))],
    )(*grads)


HBM = pl.BlockSpec(memory_space=pltpu.HBM)
SEM = pl.BlockSpec(memory_space=pltpu.SEMAPHORE)
DATAFLOW = pltpu.SideEffectType.DATAFLOW_SIDE_EFFECTING


def _chip_copy(p_refs, land_refs, send_sems, recv_sems, a, r, blocked):
    x, y, c = _place()
    tx, ty = _chip_of(x, y, r)
    k = a * 3 + (r - 1)
    return pltpu.make_async_remote_copy(
        src_ref=p_refs[a].at[2 * tx + ty] if blocked else p_refs[a], dst_ref=land_refs[a].at[r - 1],
        send_sem=send_sems.at[k], recv_sem=recv_sems.at[k], device_id=(tx, ty, c), device_id_type=MESH)


def _chip_exchange_start(psums, name, blocked=True):
    n = len(psums)
    lands = [lax.empty((3,) + (p.shape[1:] if blocked else p.shape), p.dtype) for p in psums]

    def body(*refs):
        p_refs, land_refs = refs[0:n], refs[n:2 * n]
        send_sems, recv_sems, token = refs[2 * n], refs[2 * n + 1], refs[-1]
        for a in range(n):
            for r in (1, 2, 3):
                _chip_copy(p_refs, land_refs, send_sems, recv_sems, a, r, blocked).start()
        token[...] = jnp.zeros_like(token)

    hbm = lambda t: pltpu.HBM(t.shape, t.dtype)
    keep = lambda t: pltpu.with_memory_space_constraint(t, pltpu.HBM)
    outs = pl.pallas_call(
        body,
        name=name,
        in_specs=[HBM] * (2 * n),
        out_specs=(SEM, SEM, *[HBM] * (2 * n), pl.BlockSpec(memory_space=pltpu.VMEM)),
        out_shape=(pltpu.SemaphoreType.DMA((3 * n,)), pltpu.SemaphoreType.DMA((3 * n,)),
                   *[hbm(p) for p in psums], *[hbm(l) for l in lands], _sds((8, 128), F32)),
        input_output_aliases={i: 2 + i for i in range(2 * n)},
        compiler_params=pltpu.CompilerParams(has_side_effects=DATAFLOW),
    )(*[keep(p) for p in psums], *[keep(l) for l in lands])
    return outs[0], outs[1], list(outs[2:2 + n]), list(outs[2 + n:2 + 2 * n]), outs[-1]


def _chip_exchange_wait(send_sems, recv_sems, p_thru, land_thru, after, name, blocked=True):
    n = len(p_thru)

    def body(*refs):
        p_refs, land_refs = refs[0:n], refs[n:2 * n]
        send_sems, recv_sems = refs[2 * n], refs[2 * n + 1]
        for a in range(n):
            for r in (1, 2, 3):
                copy = _chip_copy(p_refs, land_refs, send_sems, recv_sems, a, r, blocked)
                copy.wait_send()
                copy.wait_recv()

    hbm = lambda t: pltpu.HBM(t.shape, t.dtype)
    outs = pl.pallas_call(
        body,
        name=name,
        in_specs=[HBM] * (2 * n) + [SEM, SEM, ANY],
        out_specs=[HBM] * (2 * n),
        out_shape=[hbm(p) for p in p_thru] + [hbm(l) for l in land_thru],
        input_output_aliases={i: i for i in range(2 * n)},
        compiler_params=pltpu.CompilerParams(has_side_effects=DATAFLOW),
    )(*p_thru, *land_thru, send_sems, recv_sems, after)
    return list(outs[n:2 * n])


def _pair_share(fulls):
    n = len(fulls)
    halves = [f.shape[0] // 2 for f in fulls]

    def body(*refs):
        full_refs = refs[n:2 * n]
        send_sems, recv_sems = refs[2 * n:]
        x, y, c = _place()

        def half_of(a, core):
            return full_refs[a].at[pl.ds(pl.multiple_of(core * halves[a], 8), halves[a]), :]

        def remote(a, src, dst):
            return pltpu.make_async_remote_copy(
                src_ref=src, dst_ref=dst, send_sem=send_sems.at[a], recv_sem=recv_sems.at[a],
                device_id=(x, y, 1 - c), device_id_type=MESH)

        for a in range(n):
            for q in range(halves[a] // D2D_PIECE_ROWS):
                piece = full_refs[a].at[
                    pl.ds(pl.multiple_of(c * halves[a] + q * D2D_PIECE_ROWS, 8), D2D_PIECE_ROWS), :]
                remote(a, piece, piece).start()
        for a in range(n):
            remote(a, half_of(a, c), half_of(a, c)).wait_send()
            remote(a, half_of(a, 1 - c), half_of(a, 1 - c)).wait_recv()

    return pl.pallas_call(
        body,
        name="pair_share",
        in_specs=[ANY] * n,
        out_specs=[ANY] * n,
        out_shape=[_sds(f.shape, F32) for f in fulls],
        input_output_aliases={a: a for a in range(n)},
        scratch_shapes=[pltpu.SemaphoreType.DMA((n,)), pltpu.SemaphoreType.DMA((n,))],
    )(*fulls)


def _small_pair_sum(s):
    R, C = s.shape
    V = SMALL_VECTOR_ROWS

    def body(s_ref, v_ref, m_ref, sib, send_sem, recv_sem):
        x, y, c = _place()

        def to_sib(src, dst):
            return pltpu.make_async_remote_copy(
                src_ref=src, dst_ref=dst, send_sem=send_sem, recv_sem=recv_sem,
                device_id=(x, y, 1 - c), device_id_type=MESH)

        for q in range(R // 8):
            to_sib(s_ref.at[pl.ds(8 * q, 8), :], sib.at[pl.ds(8 * q, 8), :]).start()
        to_sib(s_ref, sib).wait()
        v_ref[...] = s_ref[pl.ds(0, V), :] + sib[pl.ds(0, V), :]
        m_ref[...] = (s_ref[pl.ds(V, R - V), :] + sib[pl.ds(V, R - V), :]).astype(BF16)

    return pl.pallas_call(
        body,
        name="small_pair_sum",
        in_specs=[pl.BlockSpec(memory_space=pltpu.VMEM)],
        out_specs=[pl.BlockSpec(memory_space=pltpu.VMEM)] * 2,
        out_shape=[jax.ShapeDtypeStruct((V, C), F32), jax.ShapeDtypeStruct((R - V, C), BF16)],
        scratch_shapes=[pltpu.VMEM((R, C), F32), pltpu.SemaphoreType.DMA, pltpu.SemaphoreType.DMA],
    )(s)


def _small_total(chip, own, landed):
    V, C = own[0].shape
    M = own[1].shape[0]

    def body(j_ref, v_ref, m_ref, lv_ref, lm_ref, o_ref, chips_v, chips_m):
        j = j_ref[0]
        chips_v[j] = v_ref[...]
        chips_m[j] = m_ref[...]
        for r in (1, 2, 3):
            chips_v[j ^ r] = lv_ref[r - 1]
            chips_m[j ^ r] = lm_ref[r - 1]
        o_ref[pl.ds(0, V), :] = (chips_v[0] + chips_v[1]) + (chips_v[2] + chips_v[3])
        o_ref[pl.ds(V, M), :] = (chips_m[0].astype(F32) + chips_m[1].astype(F32)) + (
            chips_m[2].astype(F32) + chips_m[3].astype(F32))

    vmem = pl.BlockSpec(memory_space=pltpu.VMEM)
    return pl.pallas_call(
        body,
        name="small_total",
        in_specs=[pl.BlockSpec(memory_space=pltpu.SMEM), vmem, vmem, vmem, vmem],
        out_specs=vmem,
        out_shape=jax.ShapeDtypeStruct((V + M, C), F32),
        scratch_shapes=[pltpu.VMEM((N_CHIPS, V, C), F32), pltpu.VMEM((N_CHIPS, M, C), BF16)],
    )(chip, own[0], own[1], landed[0], landed[1])


def _block_diag(w):
    w4 = w.reshape(4, 4, RNN_BLOCK_W, RNN_BLOCK_W)
    eye = jnp.eye(4, dtype=w.dtype)
    return jnp.einsum("jaik,ab->jaibk", w4, eye).reshape(4, RNN_TILE, RNN_TILE)


def _block_diag_part(d):
    d5 = d.reshape(4, 4, RNN_BLOCK_W, 4, RNN_BLOCK_W)
    return jnp.stack([d5[:, a, :, a, :] for a in range(4)], axis=1).reshape(RNN_BLOCKS, RNN_BLOCK_W, RNN_BLOCK_W)


def _local_grads(x, target, g_pre, w_in_g, b_gate, conv_w, conv_b, w_rg_a, b_rg_a, w_rg_x, b_rg_x, lam, sinks,
                 out_weights, fwd_token, g_post, on_out_grads, on_w_in_grad):
    wa_bd = _block_diag(w_rg_a).astype(BF16)
    wx_bd = _block_diag(w_rg_x).astype(BF16)
    b_a = b_rg_a.reshape(1, D_RNN)
    b_x = b_rg_x.reshape(1, D_RNN)

    proj, ht = _proj_fwd(x, g_pre, w_in_g)
    y_rnn, z_rnn = _rnn_fwd(proj, conv_w, conv_b, wa_bd, wx_bd, b_a, b_x, lam, fwd_token)
    bias = _attn_bias()
    y_attn, z_attn, lse = _attn_fwd(proj, sinks, bias)
    w_rnn_out, w_attn_out, w_out = out_weights(z_attn)
    dyx, dz_rnn, dz_attn, dml, merged, dout, dbr_rnn, dbr_attn, head_small = _head(
        x, target, z_rnn, z_attn, proj, b_gate, g_post, w_rnn_out, w_attn_out, w_out)
    out_grads = [_matmul_tn(z_rnn, dbr_rnn, "dw_rnn_out"), _matmul_tn(z_attn, dbr_attn, "dw_attn_out"),
                 _matmul_tn(merged, dout, "dw_out")]
    shard_rows = lambda d: d.reshape(N_CHIPS, OUT_SHARD, D_MODEL)
    token = on_out_grads([shard_rows(g) for g, _ in out_grads], [shard_rows(gb) for _, gb in out_grads])
    dq, dk, dv, dag, attn_small = _attn_bwd(proj, y_attn, lse, dz_attn, sinks, bias, token)
    drx, drg, dwa_t, dwx_t, rnn_small = _rnn_bwd(proj, y_rnn, dz_rnn, conv_w, conv_b, wa_bd, wx_bd, b_a, b_x, lam)
    dproj = [drx, drg, dq, dk, dv, dag, dml]
    token = on_w_in_grad(*_dw_in(ht, dproj))
    grad_x, dh_small = _dh_bwd(dproj, w_in_g, x, dyx, g_pre, token)
    small = jnp.concatenate([rnn_small, head_small, dh_small + attn_small,
                             _block_diag_part(dwa_t).reshape(64, 1024), _block_diag_part(dwx_t).reshape(64, 1024)], axis=0)
    return grad_x, small


ROW_LOSS = 11


def _rows8(parts):
    out = None
    for r, a in parts:
        p = jnp.pad(a, ((r, 8 - r - a.shape[0]), (0, 1024 - a.shape[1])))
        out = p if out is None else out + p
    return out


def _pack_small(p):
    g0 = _rows8([(0, p["b_rg_a"].reshape(1, 1024)), (1, p["b_rg_x"].reshape(1, 1024)), (2, p["lru_lambda"]),
                 (3, p["conv_b"]), (4, p["conv_w"][0])])
    g1 = _rows8([(0, p["post_norm_g"]), (1, p["b_gate"].reshape(2, 1024))])
    g2 = _rows8([(0, p["pre_norm_g"]), (1, p["attn_sinks"])])
    return jnp.concatenate([g0, g1, g2, p["w_rg_a"].reshape(64, 1024), p["w_rg_x"].reshape(64, 1024)], axis=0)


def _unpack_small(s, conv_cols):
    return {
        "b_rg_a": s[0:1].reshape(1, 16, 64), "b_rg_x": s[1:2].reshape(1, 16, 64), "lru_lambda": s[2:3],
        "conv_b": s[3:4], "conv_w": s[4:8, 0:conv_cols].reshape(1, CONV_W, conv_cols),
        "post_norm_g": s[8:9], "b_gate": s[9:11].reshape(1, 2048),
        "pre_norm_g": s[16:17], "attn_sinks": s[17:18, 0:N_Q_HEADS],
        "w_rg_a": s[24:88].reshape(1, 16, 64, 64), "w_rg_x": s[88:152].reshape(1, 16, 64, 64),
    }


WEIGHTS = ["pre_norm_g", "w_in", "b_gate", "conv_w", "conv_b", "w_rg_a", "b_rg_a", "w_rg_x", "b_rg_x", "lru_lambda",
           "attn_sinks", "w_rnn_out", "w_attn_out", "w_out", "post_norm_g"]
BIG = ["w_in", "w_rnn_out", "w_attn_out", "w_out"]


def kernel(x, pre_norm_g, w_in, b_gate, conv_w, conv_b, w_rg_a, b_rg_a, w_rg_x, b_rg_x, lru_lambda, attn_sinks, w_rnn_out, w_attn_out, w_out, post_norm_g, loss_target, m_pre_norm_g, m_w_in, m_b_gate, m_conv_w, m_conv_b, m_w_rg_a, m_b_rg_a, m_w_rg_x, m_b_rg_x, m_lru_lambda, m_attn_sinks, m_w_rnn_out, m_w_attn_out, m_w_out, m_post_norm_g, v_pre_norm_g, v_w_in, v_b_gate, v_conv_w, v_conv_b, v_w_rg_a, v_b_rg_a, v_w_rg_x, v_b_rg_x, v_lru_lambda, v_attn_sinks, v_w_rnn_out, v_w_attn_out, v_w_out, v_post_norm_g):
    w = dict(pre_norm_g=pre_norm_g, w_in=w_in, b_gate=b_gate, conv_w=conv_w, conv_b=conv_b, w_rg_a=w_rg_a,
             b_rg_a=b_rg_a, w_rg_x=w_rg_x, b_rg_x=b_rg_x, lru_lambda=lru_lambda, attn_sinks=attn_sinks,
             w_rnn_out=w_rnn_out, w_attn_out=w_attn_out, w_out=w_out, post_norm_g=post_norm_g)
    m = dict(pre_norm_g=m_pre_norm_g, w_in=m_w_in, b_gate=m_b_gate, conv_w=m_conv_w, conv_b=m_conv_b, w_rg_a=m_w_rg_a,
             b_rg_a=m_b_rg_a, w_rg_x=m_w_rg_x, b_rg_x=m_b_rg_x, lru_lambda=m_lru_lambda, attn_sinks=m_attn_sinks,
             w_rnn_out=m_w_rnn_out, w_attn_out=m_w_attn_out, w_out=m_w_out, post_norm_g=m_post_norm_g)
    v = dict(pre_norm_g=v_pre_norm_g, w_in=v_w_in, b_gate=v_b_gate, conv_w=v_conv_w, conv_b=v_conv_b, w_rg_a=v_w_rg_a,
             b_rg_a=v_b_rg_a, w_rg_x=v_w_rg_x, b_rg_x=v_b_rg_x, lru_lambda=v_lru_lambda, attn_sinks=v_attn_sinks,
             w_rnn_out=v_w_rnn_out, w_attn_out=v_w_attn_out, w_out=v_w_out, post_norm_g=v_post_norm_g)
    chip = 2 * lax.axis_index("x") + lax.axis_index("y")

    chip_idx = chip.astype(jnp.int32).reshape(1)
    chip_core = jnp.stack([chip, lax.axis_index("c")]).astype(jnp.int32)
    cw8 = jnp.pad(conv_w[0], ((0, 8 - CONV_W), (0, 0)))
    placed = _place_shards([w_in[0], w_rnn_out[0], w_attn_out[0], w_out[0]], chip_idx, "place_shards")
    win_g, cw_g = _gather_weights(placed[:1], cw8)
    late_send, late_recv, late_thru, late_token = _gather_late_start(placed[1:], win_g, "gather_late_start")
    cw_g = lax.dynamic_update_slice_in_dim(cw_g, cw8[None], chip, axis=0)
    conv_w_full = jnp.transpose(cw_g[:, 0:CONV_W, :], (1, 0, 2)).reshape(CONV_W, D_RNN)

    core_idx = lax.axis_index("c").astype(jnp.int32).reshape(1)
    started = {}

    def start_reduction(tag, grads, grads_b16):
        got = _pair_exchange(grads_b16, "pair_exchange_" + tag)
        sums = [_pair_sum(g, o, core_idx, "pair_sum_%s_%d" % (tag, a)) for a, (g, o) in enumerate(zip(grads, got))]
        send_sems, recv_sems, p_thru, land_thru, token = _chip_exchange_start(
            [pb for _, pb in sums], "chip_exchange_start_" + tag)
        started[tag] = ([p for p, _ in sums], send_sems, recv_sems, p_thru, land_thru)
        return token

    def end_reduction(tag, after):
        psums, send_sems, recv_sems, p_thru, land_thru = started[tag]
        landed = _chip_exchange_wait(send_sems, recv_sems, p_thru, land_thru, after, "chip_exchange_wait_" + tag)
        return [_chip_sum(p, l, chip_core, "chip_sum_%s_%d" % (tag, a)) for a, (p, l) in enumerate(zip(psums, landed))]

    def out_weights(after):
        gathered = _gather_late_wait(late_send, late_recv, late_thru, after, "gather_late_wait")
        return [g.reshape(D_MODEL, D_MODEL) for g in gathered]

    grad_x, small = _local_grads(
        x[0], loss_target[0], pre_norm_g, win_g, b_gate, conv_w_full, conv_b, w_rg_a[0], b_rg_a[0], w_rg_x[0],
        b_rg_x[0], lru_lambda, attn_sinks[0], out_weights, late_token, post_norm_g,
        on_out_grads=lambda grads, grads_b16: start_reduction("out", grads, grads_b16),
        on_w_in_grad=lambda grad, grad_b16: start_reduction("in", [grad], [grad_b16]))

    small_chip = _small_pair_sum(small)
    small_send, small_recv, small_thru, small_land, small_token = _chip_exchange_start(
        list(small_chip), "small_exchange_start", blocked=False)

    halves = end_reduction("in", small_token) + end_reduction("out", small_token)
    gbig = dict(zip(BIG, _pair_share(halves)))

    grads, delta, new_m, new_v = {}, {}, {}, {}
    for n in BIG:
        grads[n] = gbig[n][None]
        d, nm, nv = _adamw(w[n][0], gbig[n], m[n][0], v[n][0], "adamw_" + n)
        delta[n], new_m[n], new_v[n] = d[None], nm[None], nv[None]

    small_landed = _chip_exchange_wait(small_send, small_recv, small_thru, small_land, delta[BIG[-1]],
                                       "small_exchange_wait", blocked=False)
    small_sum = _small_total(chip_idx, small_thru, small_landed)
    total_loss = small_sum[ROW_LOSS, 0]
    gsmall = _unpack_small(small_sum, D_RNN)
    conv_shard = D_RNN // N_CHIPS
    gsmall["conv_w"] = lax.dynamic_slice_in_dim(gsmall["conv_w"], chip * conv_shard, conv_shard, axis=2)
    pick = lambda t: {k: t[k] for k in gsmall}
    d, nm, nv = _adamw(_pack_small(pick(w)), _pack_small(gsmall), _pack_small(pick(m)), _pack_small(pick(v)),
                       "adamw_small")
    ud, um, uv = _unpack_small(d, conv_shard), _unpack_small(nm, conv_shard), _unpack_small(nv, conv_shard)
    for n in gsmall:
        grads[n] = gsmall[n].reshape(w[n].shape)
        delta[n] = ud[n].reshape(w[n].shape)
        new_m[n] = um[n].reshape(w[n].shape)
        new_v[n] = uv[n].reshape(w[n].shape)

    return (total_loss, grad_x[None], *[grads[n] for n in WEIGHTS], *[delta[n] for n in WEIGHTS],
            *[new_m[n] for n in WEIGHTS], *[new_v[n] for n in WEIGHTS])
```

```python
import functools
import math

import jax
import jax.numpy as jnp
from jax import lax
from jax.experimental import pallas as pl
from jax.experimental.pallas import tpu as pltpu

F32 = jnp.float32
BF16 = jnp.bfloat16

D_MODEL = 1024
D_RNN = 1024
RNN_BLOCKS = 16
RNN_BLOCK_W = 64
CONV_W = 4
LRU_C = 8.0
N_Q_HEADS = 16
N_KV_HEADS = 4
GROUP = 4
HEAD_DIM = 64
D_KV = 256
BLOCK = 128
ALIBI_MAX_BIAS = 8.0
EPS = 1e-6
D_IN = 6656
N_CHIPS = 4
W_IN_SHARD = D_IN // N_CHIPS
OUT_SHARD = D_MODEL // N_CHIPS
ADAM_LR = 0.001
ADAM_B1 = 0.9
ADAM_B2 = 0.999
ADAM_EPS = 1e-08
ADAM_WD = 0.01
ADAM_STEP = 10
NEG_BIG = -1e30
MIB = 1 << 20

COL_RNN_X = 0
COL_RNN_GATE = 4
COL_Q = 8
COL_K = 12
COL_V = 13
COL_ATTN_GATE = 14
COL_MERGE = 18

RNN_TILE = 256
RNN_CHUNK = 512
SMALL_ROWS = 152
SMALL_VECTOR_ROWS = 24
MESH = pl.DeviceIdType.MESH


def _sds(shape, dtype):
    return pltpu.HBM(shape, dtype)


def _params(sem=None, vmem_mib=None):
    kw = {}
    if sem is not None:
        kw["dimension_semantics"] = sem
    if vmem_mib is not None:
        kw["vmem_limit_bytes"] = vmem_mib * MIB
    return pltpu.CompilerParams(**kw)


def _hbm(*arrays):
    return [pltpu.with_memory_space_constraint(a, pltpu.HBM) for a in arrays]


def _dot(a, b):
    return jnp.dot(a, b, preferred_element_type=F32)


def _dot_nt(a, b):
    return lax.dot_general(a, b, (((1,), (1,)), ((), ())), preferred_element_type=F32)


def _dot_tn(a, b):
    return lax.dot_general(a, b, (((0,), (0,)), ((), ())), preferred_element_type=F32)


def _sigmoid(x):
    return 0.5 * jnp.tanh(0.5 * x) + 0.5


def _sigmoid_small(x):
    return 1.0 / (1.0 + jnp.exp(-x))


def _softplus(x):
    return jnp.maximum(x, 0.0) + jnp.log(1.0 + jnp.exp(-jnp.abs(x)))


def _one_minus_square(a, log_a):
    return -jnp.tanh(log_a) * (a * a + 1.0)


def _proj_fwd(x, g_pre, w_in_g):
    T = x.shape[0]
    tm = min(1024, T)

    def body(x_ref, g_ref, w_ref, proj_ref, ht_ref, h_s):
        @pl.when(pl.program_id(1) == 0)
        def _():
            xv = x_ref[...]
            rstd = lax.rsqrt(jnp.mean(xv * xv, axis=-1, keepdims=True) + EPS)
            hf = (xv * rstd) * g_ref[...]
            h_s[...] = hf.astype(BF16)
            ht_ref[...] = hf.T.astype(BF16)

        proj_ref[...] = _dot(h_s[...], w_ref[...]).astype(BF16)

    return pl.pallas_call(
        body,
        name="proj_fwd",
        grid=(T // tm, N_CHIPS),
        in_specs=[
            pl.BlockSpec((tm, D_MODEL), lambda i, j: (i, 0)),
            pl.BlockSpec((1, D_MODEL), lambda i, j: (0, 0)),
            pl.BlockSpec((None, D_MODEL, W_IN_SHARD), lambda i, j: (j, 0, 0)),
        ],
        out_specs=[
            pl.BlockSpec((tm, W_IN_SHARD), lambda i, j: (i, j)),
            pl.BlockSpec((D_MODEL, tm), lambda i, j: (0, i)),
        ],
        out_shape=[_sds((T, D_IN), BF16), _sds((D_MODEL, T), BF16)],
        scratch_shapes=[pltpu.VMEM((tm, D_MODEL), BF16)],
        compiler_params=_params(("parallel", "arbitrary"), 48),
    )(*_hbm(x, g_pre, w_in_g))


def _shift_down(x, tail, s, row):
    n = x.shape[0]
    xs = pltpu.roll(x, s, 0)
    tail_t = jnp.tile(pltpu.roll(tail, s, 0), (n // 8, 1))
    return jnp.where(row < s, tail_t, xs)


def _shift_up(x, head, s, row):
    n = x.shape[0]
    xs = pltpu.roll(x, n - s, 0)
    head_t = jnp.tile(pltpu.roll(head, 8 - s, 0), (n // 8, 1))
    return jnp.where(row >= n - s, head_t, xs)


def _conv_taps(x, tail, row):
    return [_shift_down(x, tail, 3, row), _shift_down(x, tail, 2, row), _shift_down(x, tail, 1, row), x]


def _rglru_gates(c, wa, wx, ba, bx, lam):
    cb = c.astype(BF16)
    r = _sigmoid_small(_dot(cb, wa) + ba)
    i = _sigmoid(_dot(cb, wx) + bx)
    log_a = (-LRU_C) * r * _softplus(-lam)
    a = jnp.exp(log_a)
    mult = jnp.sqrt(_one_minus_square(a, log_a))
    return cb, r, i, a, mult


SUBLANES = 8


def _scan_down(a, u, row):
    n = a.shape[0]
    s = 1
    while s < SUBLANES:
        a_sh = jnp.where(row >= s, pltpu.roll(a, s, 0), 1.0)
        u_sh = jnp.where(row >= s, pltpu.roll(u, s, 0), 0.0)
        u = a * u_sh + u
        a = a * a_sh
        s *= 2
    while s < n:
        u = jnp.concatenate([u[:s], a[s:] * u[:n - s] + u[s:]], axis=0)
        a = jnp.concatenate([a[:s], a[s:] * a[:n - s]], axis=0)
        s *= 2
    return a, u


def _scan_up(b, u, row):
    n = b.shape[0]
    s = 1
    while s < SUBLANES:
        b_sh = jnp.where(row < n - s, pltpu.roll(b, n - s, 0), 1.0)
        u_sh = jnp.where(row < n - s, pltpu.roll(u, n - s, 0), 0.0)
        u = b * u_sh + u
        b = b * b_sh
        s *= 2
    while s < n:
        u = jnp.concatenate([b[:n - s] * u[s:] + u[:n - s], u[n - s:]], axis=0)
        b = jnp.concatenate([b[:n - s] * b[s:], b[n - s:]], axis=0)
        s *= 2
    return b, u


LANES = 128


def _chunk_scan(a, u, a_s, u_s, hl_s, al_s, carry, reverse):
    n, width = a.shape
    groups = n // SUBLANES
    order = range(SUBLANES - 1, -1, -1) if reverse else range(SUBLANES)
    row = lax.broadcasted_iota(jnp.int32, (groups, LANES), 0)
    for l in range(width // LANES):
        lanes = slice(l * LANES, (l + 1) * LANES)
        a_l, u_l, hl_l, al_l = a_s.at[l], u_s.at[l], hl_s.at[l], al_s.at[l]
        a_l[...] = a[:, lanes]
        u_l[...] = u[:, lanes]
        h_loc = a_loc = None
        for r in order:
            rows = pl.ds(r, groups, stride=SUBLANES)
            a_r, u_r = a_l[rows, :], u_l[rows, :]
            h_loc, a_loc = (u_r, a_r) if h_loc is None else (a_r * h_loc + u_r, a_r * a_loc)
            hl_l[rows, :] = h_loc
            al_l[rows, :] = a_loc
        if reverse:
            a_cum, ends = _scan_up(a_loc, h_loc, row)
            ends = ends + a_cum * carry[:, lanes]
            enters = jnp.where(row == groups - 1, carry[:, lanes], pltpu.roll(ends, groups - 1, 0))
        else:
            a_cum, ends = _scan_down(a_loc, h_loc, row)
            ends = ends + a_cum * carry[:, lanes]
            enters = jnp.where(row == 0, carry[:, lanes], pltpu.roll(ends, 1, 0))
        for r in range(SUBLANES):
            rows = pl.ds(r, groups, stride=SUBLANES)
            hl_l[rows, :] = hl_l[rows, :] + al_l[rows, :] * enters
    return jnp.concatenate([hl_s[l] for l in range(width // LANES)], axis=1)


def _rnn_fwd(proj, conv_w, conv_b, wa_bd, wx_bd, b_a, b_x, lam, token):
    T = proj.shape[0]
    tc, ct = RNN_CHUNK, RNN_TILE
    nt = T // tc

    def body(x_ref, rg_ref, cw_ref, cb_ref, wa_ref, wx_ref, ba_ref, bx_ref, lam_ref, token_ref, h_ref, z_ref, xtail,
             hcarry, a_s, u_s, hl_s, al_s):
        @pl.when(pl.program_id(1) == 0)
        def _():
            xtail[...] = jnp.zeros_like(xtail)
            hcarry[...] = jnp.zeros_like(hcarry)

        row = lax.broadcasted_iota(jnp.int32, (tc, ct), 0)
        x = x_ref[...].astype(F32)
        taps = _conv_taps(x, xtail[...], row)
        c = cb_ref[...] + cw_ref[pl.ds(0, 1), :] * taps[0]
        for k in range(1, CONV_W):
            c = c + cw_ref[pl.ds(k, 1), :] * taps[k]
        xtail[...] = x[tc - 8:, :]
        _, _, i, a, mult = _rglru_gates(c, wa_ref[...], wx_ref[...], ba_ref[...], bx_ref[...], lam_ref[...])
        h = _chunk_scan(a, mult * (i * c), a_s, u_s, hl_s, al_s, hcarry[...], reverse=False)
        h_ref[...] = h
        hcarry[...] = h_ref[pl.ds(tc - 1, 1), :]
        rg = rg_ref[...].astype(F32)
        z_ref[...] = (h * (rg * _sigmoid(rg))).astype(BF16)

    col = lambda off: (lambda j, t: (t, off + j))
    vec = pl.BlockSpec((1, ct), lambda j, t: (0, j))
    mat = pl.BlockSpec((None, ct, ct), lambda j, t: (j, 0, 0))
    return pl.pallas_call(
        body,
        name="rnn_fwd",
        grid=(D_RNN // ct, nt),
        in_specs=[
            pl.BlockSpec((tc, ct), col(COL_RNN_X)),
            pl.BlockSpec((tc, ct), col(COL_RNN_GATE)),
            pl.BlockSpec((CONV_W, ct), lambda j, t: (0, j)),
            vec, mat, mat, vec, vec, vec,
            pl.BlockSpec((8, 128), lambda j, t: (0, 0)),
        ],
        out_specs=[pl.BlockSpec((tc, ct), lambda j, t: (t, j)), pl.BlockSpec((tc, ct), lambda j, t: (t, j))],
        out_shape=[_sds((T, D_RNN), F32), _sds((T, D_RNN), BF16)],
        scratch_shapes=[pltpu.VMEM((8, ct), F32), pltpu.VMEM((1, ct), F32)] + [
            pltpu.VMEM((ct // LANES, tc, LANES), F32)] * 4,
        compiler_params=_params(("parallel", "arbitrary"), 32),
    )(*_hbm(proj, proj, conv_w, conv_b, wa_bd, wx_bd, b_a, b_x, lam, token))


def _rnn_bwd(proj, y_rnn, dz_rnn, conv_w, conv_b, wa_bd, wx_bd, b_a, b_x, lam):
    T = proj.shape[0]
    tc, ct = RNN_CHUNK, RNN_TILE
    nt = T // tc
    hb = tc // 8

    def body(x_ref, xh_ref, rg_ref, h_ref, hh_ref, dz_ref, cw_ref, cb_ref, wa_ref, wx_ref, ba_ref, bx_ref, lam_ref,
             dx_ref, drg_ref, dwa_ref, dwx_ref, sm_ref, lam_carry, a_carry, dc_head, b_s, dy_s, hl_s, al_s):
        t = pl.program_id(1)
        first_chunk = t == nt - 1

        @pl.when(t == 0)
        def _():
            lam_carry[...] = jnp.zeros_like(lam_carry)
            a_carry[...] = jnp.zeros_like(a_carry)
            dc_head[...] = jnp.zeros_like(dc_head)
            dwa_ref[...] = jnp.zeros_like(dwa_ref)
            dwx_ref[...] = jnp.zeros_like(dwx_ref)
            sm_ref[...] = jnp.zeros_like(sm_ref)

        row = lax.broadcasted_iota(jnp.int32, (tc, ct), 0)
        keep = jnp.where(first_chunk, 0.0, 1.0)
        x = x_ref[...].astype(F32)
        xtail = xh_ref[...].astype(F32)[8:16, :] * keep
        taps = _conv_taps(x, xtail, row)
        c = cb_ref[...] + cw_ref[pl.ds(0, 1), :] * taps[0]
        for k in range(1, CONV_W):
            c = c + cw_ref[pl.ds(k, 1), :] * taps[k]
        lam = lam_ref[...]
        cb, r, i, a, mult = _rglru_gates(c, wa_ref[...], wx_ref[...], ba_ref[...], bx_ref[...], lam)
        h = h_ref[...]
        h_prev = _shift_down(h, hh_ref[...] * keep, 1, row)
        rg = rg_ref[...].astype(F32)
        dz = dz_ref[...]
        sg = _sigmoid(rg)
        drg_ref[...] = (dz * h * (sg * (1.0 + rg * (1.0 - sg)))).astype(BF16)
        dy = dz * (rg * sg)
        b = jnp.where(row >= tc - 1, a_carry[pl.ds(0, 1), :], pltpu.roll(a, tc - 1, 0))
        lt = _chunk_scan(b, dy, b_s, dy_s, hl_s, al_s, lam_carry[pl.ds(0, 1), :], reverse=True)
        lam_carry[...] = lt[0:8, :]
        a_carry[...] = a[0:8, :]
        ic = i * c
        dmult = lt * ic
        di = lt * mult * c
        dc = lt * mult * i
        dlog_a = a * (lt * h_prev - dmult * a / mult)
        sp = _softplus(-lam)
        dpre_r = dlog_a * ((-LRU_C) * sp) * (r * (1.0 - r))
        dpre_i = di * (i * (1.0 - i))
        dlam_row = jnp.sum(dlog_a * r, axis=0, keepdims=True) * (LRU_C * _sigmoid(-lam))
        dpr_b = dpre_r.astype(BF16)
        dpi_b = dpre_i.astype(BF16)
        dwa_ref[...] += _dot_tn(cb, dpr_b)
        dwx_ref[...] += _dot_tn(cb, dpi_b)
        dc = dc + _dot_nt(dpr_b, wa_ref[...]) + _dot_nt(dpi_b, wx_ref[...])
        head = dc_head[...]
        dx = cw_ref[pl.ds(3, 1), :] * dc
        for m in range(1, CONV_W):
            dx = dx + cw_ref[pl.ds(3 - m, 1), :] * _shift_up(dc, head, m, row)
        dx_ref[...] = dx.astype(BF16)
        dc_head[...] = dc[0:8, :]
        sm_ref[pl.ds(0, 1), :] += jnp.sum(dpre_r, axis=0, keepdims=True)
        sm_ref[pl.ds(1, 1), :] += jnp.sum(dpre_i, axis=0, keepdims=True)
        sm_ref[pl.ds(2, 1), :] += dlam_row
        sm_ref[pl.ds(3, 1), :] += jnp.sum(dc, axis=0, keepdims=True)
        for k in range(CONV_W):
            sm_ref[pl.ds(4 + k, 1), :] += jnp.sum(dc * taps[k], axis=0, keepdims=True)

    rev = lambda off: (lambda j, t: (nt - 1 - t, off + j))
    halo = lambda off: (lambda j, t: (jnp.maximum((nt - 1 - t) * hb - 1, 0), off + j))
    halo16 = lambda off: (lambda j, t: (jnp.maximum((nt - 1 - t) * (hb // 2) - 1, 0), off + j))
    vec = pl.BlockSpec((1, ct), lambda j, t: (0, j))
    mat = pl.BlockSpec((None, ct, ct), lambda j, t: (j, 0, 0))
    return pl.pallas_call(
        body,
        name="rnn_bwd",
        grid=(D_RNN // ct, nt),
        in_specs=[
            pl.BlockSpec((tc, ct), rev(COL_RNN_X)),
            pl.BlockSpec((16, ct), halo16(COL_RNN_X)),
            pl.BlockSpec((tc, ct), rev(COL_RNN_GATE)),
            pl.BlockSpec((tc, ct), rev(0)),
            pl.BlockSpec((8, ct), halo(0)),
            pl.BlockSpec((tc, ct), rev(0)),
            pl.BlockSpec((CONV_W, ct), lambda j, t: (0, j)),
            vec, mat, mat, vec, vec, vec,
        ],
        out_specs=[
            pl.BlockSpec((tc, ct), rev(0)),
            pl.BlockSpec((tc, ct), rev(0)),
            mat, mat,
            pl.BlockSpec((8, ct), lambda j, t: (0, j)),
        ],
        out_shape=[_sds((T, D_RNN), BF16), _sds((T, D_RNN), BF16), _sds((D_RNN // ct, ct, ct), F32),
                   _sds((D_RNN // ct, ct, ct), F32), _sds((8, D_RNN), F32)],
        scratch_shapes=[pltpu.VMEM((8, ct), F32)] * 3 + [pltpu.VMEM((ct // LANES, tc, LANES), F32)] * 4,
        compiler_params=_params(("parallel", "arbitrary"), 32),
    )(*_hbm(proj, proj, proj, y_rnn, y_rnn, dz_rnn, conv_w, conv_b, wa_bd, wx_bd, b_a, b_x, lam))


def _attn_bias():
    qi = jnp.arange(BLOCK)[:, None]
    kj = jnp.arange(BLOCK)[None, :]
    dist_cur = (qi - kj).astype(F32)
    slopes = 2.0 ** (-ALIBI_MAX_BIAS * jnp.arange(1, N_Q_HEADS + 1, dtype=F32) / N_Q_HEADS)
    slopes = slopes[:, None, None]
    prev = jnp.where(kj > qi, -slopes * (dist_cur + float(BLOCK)), NEG_BIG)
    cur = jnp.where(kj <= qi, -slopes * dist_cur, NEG_BIG)
    later = jnp.concatenate([prev, cur], axis=-1)
    first = jnp.concatenate([jnp.full_like(prev, NEG_BIG), cur], axis=-1)
    return jnp.stack([first, later])


def _attn_exps(s_prev, s_cur, sink, bias):
    s_prev = s_prev + bias[:, 0:BLOCK]
    s_cur = s_cur + bias[:, BLOCK:2 * BLOCK]
    m = jnp.maximum(jnp.max(jnp.maximum(s_prev, s_cur), axis=-1, keepdims=True), sink)
    p_prev = jnp.exp(s_prev - m)
    p_cur = jnp.exp(s_cur - m)
    total = jnp.sum(p_prev + p_cur, axis=-1, keepdims=True) + jnp.exp(sink - m)
    return p_prev, p_cur, 1.0 / total, m + jnp.log(total)


def _attn_probs(s_prev, s_cur, sink, bias, lse):
    p_prev = jnp.exp((s_prev + bias[:, 0:BLOCK]) - lse)
    p_cur = jnp.exp((s_cur + bias[:, BLOCK:2 * BLOCK]) - lse)
    return p_prev, p_cur, jnp.exp(sink - lse)


def _stack_heads(ref_or_val, hk, dtype):
    parts = [ref_or_val[:, (GROUP * hk + g) * HEAD_DIM:(GROUP * hk + g + 1) * HEAD_DIM] for g in range(GROUP)]
    return jnp.concatenate(parts, axis=0).astype(dtype)


ATTN_SCALE = HEAD_DIM ** -0.5


def _bias_spec():
    return pl.BlockSpec((None, N_Q_HEADS, BLOCK, 2 * BLOCK), lambda i: (jnp.minimum(i, 1), 0, 0, 0))


def _attn_fwd(proj, sinks, bias):
    T = proj.shape[0]
    nb = T // BLOCK

    def body(sink_ref, bias_ref, q_ref, kp_ref, kc_ref, vp_ref, vc_ref, ag0_ref, ag1_ref, y_ref, z_ref, lse_ref):
        kvs = [slice(hk * HEAD_DIM, (hk + 1) * HEAD_DIM) for hk in range(N_KV_HEADS)]
        qgs = [(_stack_heads(q_ref, hk, F32) * ATTN_SCALE).astype(BF16) for hk in range(N_KV_HEADS)]
        s_prev = [_dot_nt(qgs[hk], kp_ref[:, kvs[hk]].astype(BF16)) for hk in range(N_KV_HEADS)]
        s_cur = [_dot_nt(qgs[hk], kc_ref[:, kvs[hk]].astype(BF16)) for hk in range(N_KV_HEADS)]
        for hk in range(N_KV_HEADS):
            pp, pc, invs = [], [], []
            for g in range(GROUP):
                h = GROUP * hk + g
                rows = slice(g * BLOCK, (g + 1) * BLOCK)
                p_prev, p_cur, inv, lse = _attn_exps(s_prev[hk][rows], s_cur[hk][rows], sink_ref[h], bias_ref[h])
                pp.append(p_prev.astype(BF16))
                pc.append(p_cur.astype(BF16))
                invs.append(inv)
                lse_ref[:, h:h + 1] = lse
            og = _dot(jnp.concatenate(pp, axis=0), vp_ref[:, kvs[hk]].astype(BF16)) + _dot(
                jnp.concatenate(pc, axis=0), vc_ref[:, kvs[hk]].astype(BF16))
            for g in range(GROUP):
                h = GROUP * hk + g
                y_ref[:, h * HEAD_DIM:(h + 1) * HEAD_DIM] = og[g * BLOCK:(g + 1) * BLOCK] * invs[g]
        ag = jnp.concatenate([ag0_ref[...], ag1_ref[...]], axis=1).astype(F32)
        z_ref[...] = (y_ref[...] * (ag * _sigmoid(ag))).astype(BF16)

    prev = lambda c: (lambda i: (jnp.maximum(i - 1, 0), c))
    cur = lambda c: (lambda i: (i, c))
    return pl.pallas_call(
        body,
        name="attn_fwd",
        grid=(nb,),
        in_specs=[
            pl.BlockSpec(memory_space=pltpu.SMEM),
            _bias_spec(),
            pl.BlockSpec((BLOCK, 1024), lambda i: (i, COL_Q // 4)),
            pl.BlockSpec((BLOCK, D_KV), prev(COL_K)),
            pl.BlockSpec((BLOCK, D_KV), cur(COL_K)),
            pl.BlockSpec((BLOCK, D_KV), prev(COL_V)),
            pl.BlockSpec((BLOCK, D_KV), cur(COL_V)),
            pl.BlockSpec((BLOCK, 512), lambda i: (i, COL_ATTN_GATE // 2)),
            pl.BlockSpec((BLOCK, 512), lambda i: (i, COL_ATTN_GATE // 2 + 1)),
        ],
        out_specs=[pl.BlockSpec((BLOCK, 1024), lambda i: (i, 0)), pl.BlockSpec((BLOCK, 1024), lambda i: (i, 0)),
                   pl.BlockSpec((BLOCK, N_Q_HEADS), lambda i: (i, 0))],
        out_shape=[_sds((T, 1024), F32), _sds((T, 1024), BF16), _sds((T, N_Q_HEADS), F32)],
        compiler_params=_params(("arbitrary",), 32),
    )(sinks, *_hbm(bias, proj, proj, proj, proj, proj, proj, proj))


def _attn_bwd(proj, y_attn, lse, dz_attn, sinks, bias, token):
    T = proj.shape[0]
    nb = T // BLOCK

    def body(sink_ref, bias_ref, q_ref, kp_ref, kc_ref, vp_ref, vc_ref, ag0_ref, ag1_ref, y_ref, lse_ref, dz_ref,
             token_ref, dq_ref, dk_ref, dv_ref, dag_ref, ds_ref, dy_s):
        i = pl.program_id(0)

        @pl.when(i == 0)
        def _():
            ds_ref[...] = jnp.zeros_like(ds_ref)

        lane = lax.broadcasted_iota(jnp.int32, (8, 128), 1)
        sub = lax.broadcasted_iota(jnp.int32, (8, 128), 0)
        ag = jnp.concatenate([ag0_ref[...], ag1_ref[...]], axis=1).astype(F32)
        dz = dz_ref[...]
        sg = _sigmoid(ag)
        dag_ref[...] = (dz * y_ref[...] * (sg * (1.0 + ag * (1.0 - sg)))).astype(BF16)
        dy_s[...] = dz * (ag * sg)
        r_cur = pl.multiple_of(i * BLOCK, BLOCK)
        r_prev = pl.multiple_of(jnp.maximum(i - 1, 0) * BLOCK, BLOCK)
        dk_cur, dv_cur, dk_prev, dv_prev = [], [], [], []
        ds_acc = jnp.zeros((8, 128), F32)
        for hk in range(N_KV_HEADS):
            ks = slice(hk * HEAD_DIM, (hk + 1) * HEAD_DIM)
            qg = (_stack_heads(q_ref, hk, F32) * ATTN_SCALE).astype(BF16)
            dog = _stack_heads(dy_s, hk, F32)
            og = _stack_heads(y_ref, hk, F32)
            dog_b = dog.astype(BF16)
            kp = kp_ref[:, ks].astype(BF16)
            kc = kc_ref[:, ks].astype(BF16)
            vp = vp_ref[:, ks].astype(BF16)
            vc = vc_ref[:, ks].astype(BF16)
            s_prev = _dot_nt(qg, kp)
            s_cur = _dot_nt(qg, kc)
            dp_prev = _dot_nt(dog_b, vp)
            dp_cur = _dot_nt(dog_b, vc)
            dvec = jnp.sum(dog * og, axis=-1, keepdims=True)
            pp, pc, dsp, dsc = [], [], [], []
            for g in range(GROUP):
                h = GROUP * hk + g
                rows = slice(g * BLOCK, (g + 1) * BLOCK)
                p_prev, p_cur, p_sink = _attn_probs(
                    s_prev[rows], s_cur[rows], sink_ref[h], bias_ref[h], lse_ref[:, h:h + 1])
                d_h = dvec[rows]
                pp.append(p_prev.astype(BF16))
                pc.append(p_cur.astype(BF16))
                dsp.append((p_prev * (dp_prev[rows] - d_h)).astype(BF16))
                dsc.append((p_cur * (dp_cur[rows] - d_h)).astype(BF16))
                dsink = -jnp.sum(p_sink * d_h, axis=0, keepdims=True)
                ds_acc = ds_acc + jnp.where(jnp.logical_and(lane == h, sub == 1), dsink, 0.0)
            pp = jnp.concatenate(pp, axis=0)
            pc = jnp.concatenate(pc, axis=0)
            dsp = jnp.concatenate(dsp, axis=0)
            dsc = jnp.concatenate(dsc, axis=0)
            dqg = (_dot(dsp, kp) + _dot(dsc, kc)) * ATTN_SCALE
            for g in range(GROUP):
                h = GROUP * hk + g
                dq_ref[:, h * HEAD_DIM:(h + 1) * HEAD_DIM] = dqg[g * BLOCK:(g + 1) * BLOCK].astype(BF16)
            dk_ref[pl.ds(r_cur, BLOCK), ks] = _dot_tn(dsc, qg)
            dv_ref[pl.ds(r_cur, BLOCK), ks] = _dot_tn(pc, dog_b)
            dk_prev.append(_dot_tn(dsp, qg))
            dv_prev.append(_dot_tn(pp, dog_b))
        ds_ref[:, 0:128] += ds_acc

        @pl.when(i > 0)
        def _():
            for hk in range(N_KV_HEADS):
                ks = slice(hk * HEAD_DIM, (hk + 1) * HEAD_DIM)
                dk_ref[pl.ds(r_prev, BLOCK), ks] += dk_prev[hk]
                dv_ref[pl.ds(r_prev, BLOCK), ks] += dv_prev[hk]

    prev = lambda c: (lambda i: (jnp.maximum(i - 1, 0), c))
    cur = lambda c: (lambda i: (i, c))
    blk = pl.BlockSpec((BLOCK, 1024), lambda i: (i, 0))
    whole = pl.BlockSpec((T, D_KV), lambda i: (0, 0))
    return pl.pallas_call(
        body,
        name="attn_bwd",
        grid=(nb,),
        in_specs=[
            pl.BlockSpec(memory_space=pltpu.SMEM),
            _bias_spec(),
            pl.BlockSpec((BLOCK, 1024), lambda i: (i, COL_Q // 4)),
            pl.BlockSpec((BLOCK, D_KV), prev(COL_K)),
            pl.BlockSpec((BLOCK, D_KV), cur(COL_K)),
            pl.BlockSpec((BLOCK, D_KV), prev(COL_V)),
            pl.BlockSpec((BLOCK, D_KV), cur(COL_V)),
            pl.BlockSpec((BLOCK, 512), lambda i: (i, COL_ATTN_GATE // 2)),
            pl.BlockSpec((BLOCK, 512), lambda i: (i, COL_ATTN_GATE // 2 + 1)),
            blk,
            pl.BlockSpec((BLOCK, N_Q_HEADS), lambda i: (i, 0)),
            blk,
            pl.BlockSpec((8, 128), lambda i: (0, 0)),
        ],
        out_specs=[blk, whole, whole, blk, pl.BlockSpec((8, 1024), lambda i: (0, 0))],
        out_shape=[_sds((T, 1024), BF16), _sds((T, D_KV), F32), _sds((T, D_KV), F32), _sds((T, 1024), BF16),
                   _sds((8, 1024), F32)],
        scratch_shapes=[pltpu.VMEM((BLOCK, 1024), F32)],
        compiler_params=_params(("arbitrary",), 48),
    )(sinks, *_hbm(bias, proj, proj, proj, proj, proj, proj, proj, y_attn, lse, dz_attn, token))


def _head(x, target, z_rnn, z_attn, proj, b_gate, g_post, w_rnn_out, w_attn_out, w_out):
    T = x.shape[0]
    tm = 256

    def body(x_ref, t_ref, zr_ref, za_ref, ml0_ref, ml1_ref, ml2_ref, ml3_ref, bg_ref, gp_ref, wr_ref, wa_ref, wo_ref,
             dyx_ref, dzr_ref, dza_ref, dml_ref, mb_ref, dout_ref, dbr_ref, dba_ref, sm_ref):
        @pl.when(pl.program_id(0) == 0)
        def _():
            sm_ref[...] = jnp.zeros_like(sm_ref)

        wr, wa, wo = wr_ref[...], wa_ref[...], wo_ref[...]
        br_rnn = _dot(zr_ref[...], wr)
        br_attn = _dot(za_ref[...], wa)
        ml_rnn = jnp.concatenate([ml0_ref[...], ml1_ref[...]], axis=1).astype(F32)
        ml_attn = jnp.concatenate([ml2_ref[...], ml3_ref[...]], axis=1).astype(F32)
        g_rnn = _sigmoid(ml_rnn + bg_ref[:, 0:D_MODEL])
        g_attn = _sigmoid(ml_attn + bg_ref[:, D_MODEL:2 * D_MODEL])
        mb = (g_rnn * br_rnn + g_attn * br_attn).astype(BF16)
        mb_ref[...] = mb
        out = _dot(mb, wo)
        rstd = lax.rsqrt(jnp.mean(out * out, axis=-1, keepdims=True) + EPS)
        n = out * rstd
        gp = gp_ref[...]
        err = (x_ref[...] + n * gp) - t_ref[...]
        sm_ref[pl.ds(3, 1), :] += 0.5 * jnp.sum(jnp.mean(err * err, axis=-1, keepdims=True), axis=0, keepdims=True)
        dy = err * (1.0 / D_MODEL)
        dyx_ref[...] = dy
        sm_ref[pl.ds(0, 1), :] += jnp.sum(dy * n, axis=0, keepdims=True)
        dn = dy * gp
        dout = (rstd * (dn - n * jnp.mean(dn * n, axis=-1, keepdims=True))).astype(BF16)
        dout_ref[...] = dout
        dmerged = _dot_nt(dout, wo)
        dml_r = (dmerged * br_rnn) * (g_rnn * (1.0 - g_rnn))
        dml_a = (dmerged * br_attn) * (g_attn * (1.0 - g_attn))
        dml_ref[:, 0:D_MODEL] = dml_r.astype(BF16)
        dml_ref[:, D_MODEL:2 * D_MODEL] = dml_a.astype(BF16)
        sm_ref[pl.ds(1, 1), :] += jnp.sum(dml_r, axis=0, keepdims=True)
        sm_ref[pl.ds(2, 1), :] += jnp.sum(dml_a, axis=0, keepdims=True)
        dbr = (dmerged * g_rnn).astype(BF16)
        dba = (dmerged * g_attn).astype(BF16)
        dbr_ref[...] = dbr
        dba_ref[...] = dba
        dzr_ref[...] = _dot_nt(dbr, wr)
        dza_ref[...] = _dot_nt(dba, wa)

    tile = pl.BlockSpec((tm, D_MODEL), lambda i: (i, 0))
    wspec = pl.BlockSpec((D_MODEL, D_MODEL), lambda i: (0, 0))
    ml = lambda q: pl.BlockSpec((tm, 512), lambda i: (i, COL_MERGE // 2 + q))
    return pl.pallas_call(
        body,
        name="head",
        grid=(T // tm,),
        in_specs=[
            tile, tile, tile, tile,
            ml(0), ml(1), ml(2), ml(3),
            pl.BlockSpec((1, 2 * D_MODEL), lambda i: (0, 0)),
            pl.BlockSpec((1, D_MODEL), lambda i: (0, 0)),
            wspec, wspec, wspec,
        ],
        out_specs=[
            tile, tile, tile,
            pl.BlockSpec((tm, 2 * D_MODEL), lambda i: (i, 0)),
            tile, tile, tile, tile,
            pl.BlockSpec((8, D_MODEL), lambda i: (0, 0)),
        ],
        out_shape=[
            _sds((T, D_MODEL), F32), _sds((T, D_MODEL), F32), _sds((T, D_MODEL), F32),
            _sds((T, 2 * D_MODEL), BF16),
            _sds((T, D_MODEL), BF16), _sds((T, D_MODEL), BF16), _sds((T, D_MODEL), BF16), _sds((T, D_MODEL), BF16),
            _sds((8, D_MODEL), F32),
        ],
        compiler_params=_params(("arbitrary",), 56),
    )(*_hbm(x, target, z_rnn, z_attn, proj, proj, proj, proj, b_gate, g_post, w_rnn_out, w_attn_out, w_out))


def _matmul_tn(a, b, name):
    T, M = a.shape
    N = b.shape[1]
    tk = min(512, T)
    nt = T // tk

    def body(a_ref, b_ref, o_ref, ob_ref):
        @pl.when(pl.program_id(0) == 0)
        def _():
            o_ref[...] = jnp.zeros_like(o_ref)

        o_ref[...] += _dot_tn(a_ref[...], b_ref[...])

        @pl.when(pl.program_id(0) == nt - 1)
        def _():
            ob_ref[...] = o_ref[...].astype(BF16)

    whole = pl.BlockSpec((M, N), lambda t: (0, 0))
    return pl.pallas_call(
        body,
        name=name,
        grid=(nt,),
        in_specs=[pl.BlockSpec((tk, M), lambda t: (t, 0)), pl.BlockSpec((tk, N), lambda t: (t, 0))],
        out_specs=[whole, whole],
        out_shape=[_sds((M, N), F32), _sds((M, N), BF16)],
        compiler_params=_params(("arbitrary",), 48),
    )(*_hbm(a, b))


DPROJ_WIDTHS = (D_RNN, D_RNN, 1024, D_KV, D_KV, 1024, 2 * D_MODEL)


def _dproj_segments():
    segs, start = [[] for _ in range(N_CHIPS)], 0
    for p, width in enumerate(DPROJ_WIDTHS):
        for c in range(N_CHIPS):
            lo, hi = max(start, c * W_IN_SHARD), min(start + width, (c + 1) * W_IN_SHARD)
            if lo < hi:
                segs[c].append((p, lo - start, hi - start, lo - c * W_IN_SHARD, hi - c * W_IN_SHARD))
        start += width
    return segs


def _dh_bwd(pieces, w_in_g, x, dyx, g_pre, token):
    T = x.shape[0]
    tm = min(512, T)
    n = len(pieces)
    segs = _dproj_segments()

    def body(*refs):
        p_refs, w_hbm, x_ref, dyx_ref, g_ref = refs[0:n], refs[n], refs[n + 1], refs[n + 2], refs[n + 3]
        gx_ref, dg_ref, w_ref = refs[n + 5], refs[n + 6], refs[n + 7]

        @pl.when(pl.program_id(0) == 0)
        def _():
            pltpu.sync_copy(w_hbm, w_ref)
            dg_ref[...] = jnp.zeros_like(dg_ref)

        dh = None
        for c in range(N_CHIPS):
            for p, a0, a1, u0, u1 in segs[c]:
                part = _dot_nt(p_refs[p][:, a0:a1].astype(BF16), w_ref[c, :, u0:u1])
                dh = part if dh is None else dh + part
        xv = x_ref[...]
        rstd = lax.rsqrt(jnp.mean(xv * xv, axis=-1, keepdims=True) + EPS)
        nx = xv * rstd
        dhg = dh * g_ref[...]
        gx_ref[...] = dyx_ref[...] + rstd * (dhg - nx * jnp.mean(dhg * nx, axis=-1, keepdims=True))
        dg_ref[pl.ds(0, 1), :] += jnp.sum(dh * nx, axis=0, keepdims=True)

    tile = pl.BlockSpec((tm, D_MODEL), lambda i: (i, 0))
    return pl.pallas_call(
        body,
        name="dh_bwd",
        grid=(T // tm,),
        in_specs=[pl.BlockSpec((tm, w), lambda i: (i, 0)) for w in DPROJ_WIDTHS] + [
            ANY, tile, tile,
            pl.BlockSpec((1, D_MODEL), lambda i: (0, 0)),
            pl.BlockSpec((8, 128), lambda i: (0, 0)),
        ],
        out_specs=[tile, pl.BlockSpec((8, D_MODEL), lambda i: (0, 0))],
        out_shape=[_sds((T, D_MODEL), F32), _sds((8, D_MODEL), F32)],
        scratch_shapes=[pltpu.VMEM(w_in_g.shape, BF16)],
        compiler_params=_params(("arbitrary",), 56),
    )(*_hbm(*pieces, w_in_g, x, dyx, g_pre, token))


def _dw_in(ht, pieces):
    T = ht.shape[1]
    tk = min(512, T)
    nt = T // tk
    n = len(pieces)
    segs = _dproj_segments()

    def body(*refs):
        h_ref, p_refs, o_ref, ob_ref = refs[0], refs[1:n + 1], refs[n + 1], refs[n + 2]

        @pl.when(pl.program_id(1) == 0)
        def _():
            o_ref[...] = jnp.zeros_like(o_ref)

        for c in range(N_CHIPS):
            @pl.when(pl.program_id(0) == c)
            def _():
                for p, a0, a1, u0, u1 in segs[c]:
                    o_ref[:, u0:u1] += _dot(h_ref[...], p_refs[p][:, a0:a1].astype(BF16))

        @pl.when(pl.program_id(1) == nt - 1)
        def _():
            ob_ref[...] = o_ref[...].astype(BF16)

    def piece_spec(p):
        chips = [c for c in range(N_CHIPS) if any(s[0] == p for s in segs[c])]

        def index(c, t):
            used = functools.reduce(jnp.logical_or, [c == k for k in chips])
            return (jnp.where(used, t, 0), 0)

        return pl.BlockSpec((tk, DPROJ_WIDTHS[p]), index)

    return pl.pallas_call(
        body,
        name="dw_in",
        grid=(N_CHIPS, nt),
        in_specs=[pl.BlockSpec((D_MODEL, tk), lambda c, t: (0, t))] + [piece_spec(p) for p in range(n)],
        out_specs=[pl.BlockSpec((None, D_MODEL, W_IN_SHARD), lambda c, t: (c, 0, 0))] * 2,
        out_shape=[_sds((N_CHIPS, D_MODEL, W_IN_SHARD), F32), _sds((N_CHIPS, D_MODEL, W_IN_SHARD), BF16)],
        compiler_params=_params(("parallel", "arbitrary"), 56),
    )(*_hbm(ht, *pieces))


ELEMENTWISE_TILE_BYTES = MIB


def _row_tile(rows, cols):
    if rows * cols * 4 <= ELEMENTWISE_TILE_BYTES:
        return rows
    for t in (512, 256, 128, 64, 32, 16, 8):
        if rows % t == 0 and t * cols * 4 <= ELEMENTWISE_TILE_BYTES:
            return t
    return rows


def _pair_sum(g, got, core, name):
    nch, R, C = g.shape
    h = R // 2
    tr = _row_tile(h, C)
    nt = h // tr

    def body(c_ref, g_ref, got_ref, p_ref, pb_ref):
        s = g_ref[...] + got_ref[...].astype(F32)
        p_ref[...] = s
        pb_ref[...] = s.astype(BF16)

    blk = pl.BlockSpec((None, tr, C), lambda j, i, c_ref: (j, i, 0))
    return pl.pallas_call(
        body,
        name=name,
        grid_spec=pltpu.PrefetchScalarGridSpec(
            num_scalar_prefetch=1,
            grid=(nch, nt),
            in_specs=[pl.BlockSpec((None, tr, C), lambda j, i, c_ref: (j, c_ref[0] * nt + i, 0)), blk],
            out_specs=[blk, blk],
        ),
        out_shape=[_sds((nch, h, C), F32), _sds((nch, h, C), BF16)],
        compiler_params=_params(("parallel", "parallel"), 48),
    )(core, *_hbm(g, got))


def _chip_sum(p, got, chip_core, name):
    _, h, C = p.shape
    tr = _row_tile(h, C)
    nt = h // tr

    def body(jc_ref, p_ref, g0_ref, g1_ref, g2_ref, o_ref):
        o_ref[...] = ((p_ref[...] + g0_ref[...].astype(F32)) + g1_ref[...].astype(F32)) + g2_ref[...].astype(F32)

    rel = lambda r: pl.BlockSpec((None, tr, C), lambda i, jc_ref: (r, i, 0))
    return pl.pallas_call(
        body,
        name=name,
        grid_spec=pltpu.PrefetchScalarGridSpec(
            num_scalar_prefetch=1,
            grid=(nt,),
            in_specs=[pl.BlockSpec((None, tr, C), lambda i, jc_ref: (jc_ref[0], i, 0)), rel(0), rel(1), rel(2)],
            out_specs=pl.BlockSpec((tr, C), lambda i, jc_ref: (jc_ref[1] * nt + i, 0)),
        ),
        out_shape=_sds((2 * h, C), F32),
        compiler_params=_params(("parallel",), 48),
    )(chip_core, *_hbm(p, got, got, got))


def _place_shards(shards, chip, name):
    n = len(shards)
    tiles = [_row_tile(s.shape[0], s.shape[1]) for s in shards]
    steps = max(s.shape[0] // t for s, t in zip(shards, tiles))
    tiles = [s.shape[0] // steps for s in shards]

    def body(j_ref, *refs):
        for a in range(n):
            refs[n + a][...] = refs[a][...].astype(BF16)

    return pl.pallas_call(
        body,
        name=name,
        grid_spec=pltpu.PrefetchScalarGridSpec(
            num_scalar_prefetch=1,
            grid=(steps,),
            in_specs=[pl.BlockSpec((t, s.shape[1]), lambda i, j_ref: (i, 0)) for s, t in zip(shards, tiles)],
            out_specs=[pl.BlockSpec((None, t, s.shape[1]), lambda i, j_ref: (j_ref[0], i, 0))
                       for s, t in zip(shards, tiles)],
        ),
        out_shape=[_sds((N_CHIPS,) + s.shape, BF16) for s in shards],
        compiler_params=_params(("parallel",), 48),
    )(chip, *_hbm(*shards))


def _adamw(w, g, m, v, name):
    R, C = w.shape
    tr = _row_tile(R, C)
    c1 = 1.0 - ADAM_B1 ** ADAM_STEP
    c2 = 1.0 - ADAM_B2 ** ADAM_STEP

    def body(w_ref, g_ref, m_ref, v_ref, d_ref, nm_ref, nv_ref):
        g = g_ref[...]
        nm = ADAM_B1 * m_ref[...] + (1.0 - ADAM_B1) * g
        nv = ADAM_B2 * v_ref[...] + (1.0 - ADAM_B2) * (g * g)
        nm_ref[...] = nm
        nv_ref[...] = nv
        d_ref[...] = (-ADAM_LR) * ((nm / c1) / (jnp.sqrt(nv / c2) + ADAM_EPS) + ADAM_WD * w_ref[...])

    spec = pl.BlockSpec((tr, C), lambda i: (i, 0))
    return pl.pallas_call(
        body, name=name, grid=(R // tr,), in_specs=[spec] * 4, out_specs=[spec] * 3,
        out_shape=[_sds((R, C), F32)] * 3, compiler_params=_params(("parallel",), 48),
    )(*_hbm(w, g, m, v))


def _place():
    return lax.axis_index("x"), lax.axis_index("y"), lax.axis_index("c")


def _chip_of(x, y, r):
    return (x ^ (r >> 1), y ^ (r & 1))


ANY = pl.BlockSpec(memory_space=pl.ANY)


def _gather_weights(placed, cw8):
    nbig = len(placed)
    halves = [s.shape[1] // 2 for s in placed]
    pieces = [max(1, h // 128) for h in halves]
    rows = [h // p for h, p in zip(halves, pieces)]
    order = [(a, q) for q in range(max(pieces)) for a in range(nbig) if q < pieces[a]]
    ici_sem = {(a, q, r): 3 * i + (r - 1) for i, (a, q) in enumerate(order) for r in (1, 2, 3)}
    cw_sem = {r: 3 * len(order) + (r - 1) for r in (1, 2, 3)}
    d2d_sem = {key: 3 * len(order) + 3 + k for key, k in ici_sem.items()}
    nsem = 6 * len(order) + 3

    def body(*refs):
        cw_ref, dsts, gcw_ref = refs[nbig], refs[nbig + 1:2 * nbig + 1], refs[2 * nbig + 1]
        send_sems, recv_sems = refs[2 * nbig + 2:]
        x, y, c = _place()
        j = 2 * x + y

        def piece_rows(a, q, core):
            return pl.ds(pl.multiple_of(core * halves[a] + q * rows[a], 16), rows[a])

        def ici(a, q, r):
            tx, ty = _chip_of(x, y, r)
            k = ici_sem[(a, q, r)]
            region = dsts[a].at[j, piece_rows(a, q, c), :]
            return pltpu.make_async_remote_copy(
                src_ref=region, dst_ref=region, send_sem=send_sems.at[k], recv_sem=recv_sems.at[k],
                device_id=(tx, ty, c), device_id_type=MESH)

        def ici_landed(a, q, r):
            tx, ty = _chip_of(x, y, r)
            k = ici_sem[(a, q, r)]
            region = dsts[a].at[2 * tx + ty, piece_rows(a, q, c), :]
            return pltpu.make_async_remote_copy(
                src_ref=region, dst_ref=region, send_sem=send_sems.at[k], recv_sem=recv_sems.at[k],
                device_id=(tx, ty, c), device_id_type=MESH)

        def d2d(a, q, r, core):
            tx, ty = _chip_of(x, y, r)
            k = d2d_sem[(a, q, r)]
            region = dsts[a].at[2 * tx + ty, piece_rows(a, q, core), :]
            return pltpu.make_async_remote_copy(
                src_ref=region, dst_ref=region, send_sem=send_sems.at[k], recv_sem=recv_sems.at[k],
                device_id=(x, y, 1 - c), device_id_type=MESH)

        def cw_copy(r):
            tx, ty = _chip_of(x, y, r)
            k = cw_sem[r]
            return pltpu.make_async_remote_copy(
                src_ref=cw_ref, dst_ref=gcw_ref.at[j], send_sem=send_sems.at[k], recv_sem=recv_sems.at[k],
                device_id=(tx, ty, c), device_id_type=MESH)

        def cw_landed(r):
            tx, ty = _chip_of(x, y, r)
            k = cw_sem[r]
            region = gcw_ref.at[2 * tx + ty]
            return pltpu.make_async_remote_copy(
                src_ref=region, dst_ref=region, send_sem=send_sems.at[k], recv_sem=recv_sems.at[k],
                device_id=(tx, ty, c), device_id_type=MESH)

        def relay(a, q, origin, to):
            ox, oy = _chip_of(x, y, origin)
            tx, ty = _chip_of(x, y, to)
            k = ici_sem[(a, q, 3)]
            region = dsts[a].at[2 * ox + oy, piece_rows(a, q, c), :]
            return pltpu.make_async_remote_copy(
                src_ref=region, dst_ref=region, send_sem=send_sems.at[k], recv_sem=recv_sems.at[k],
                device_id=(tx, ty, c), device_id_type=MESH)

        first = [ici(a, q, r) for (a, q) in order for r in (1, 2)] + [cw_copy(r) for r in (1, 2, 3)]
        for cp in first:
            cp.start()
        passed = []
        for (a, q) in order:
            for r in (1, 2):
                ici_landed(a, q, r).wait_recv()
                if q % 2 == r - 1:
                    cp = relay(a, q, r, 3 - r)
                    cp.start()
                    passed.append(cp)
                cp = d2d(a, q, r, c)
                cp.start()
                passed.append(cp)
        for (a, q) in order:
            ici_landed(a, q, 3).wait_recv()
            cp = d2d(a, q, 3, c)
            cp.start()
            passed.append(cp)
        for r in (1, 2, 3):
            cw_landed(r).wait_recv()
        for (a, q) in order:
            for r in (1, 2, 3):
                d2d(a, q, r, 1 - c).wait_recv()
        for cp in first + passed:
            cp.wait_send()

    return pl.pallas_call(
        body,
        name="gather_weights",
        in_specs=[ANY] * (nbig + 1),
        out_specs=[ANY] * (nbig + 1),
        out_shape=[_sds(s.shape, s.dtype) for s in placed] + [_sds((N_CHIPS,) + cw8.shape, cw8.dtype)],
        input_output_aliases={a: a for a in range(nbig)},
        scratch_shapes=[pltpu.SemaphoreType.DMA((nsem,)), pltpu.SemaphoreType.DMA((nsem,))],
    )(*placed, cw8)


def _gather_late_start(placed, after, name):
    n = len(placed)
    halves = [s.shape[1] // 2 for s in placed]

    def body(*refs):
        g_refs = refs[0:n]
        send_sems, recv_sems, token = refs[n + 1], refs[n + 2], refs[-1]
        x, y, c = _place()
        j = 2 * x + y
        for a in range(n):
            mine = g_refs[a].at[j, pl.ds(pl.multiple_of(c * halves[a], 16), halves[a]), :]
            for r in (1, 2, 3):
                tx, ty = _chip_of(x, y, r)
                for to_core in (0, 1):
                    k = ((a * 3 + (r - 1)) * 2 + c) * 2 + to_core
                    pltpu.make_async_remote_copy(
                        src_ref=mine, dst_ref=mine, send_sem=send_sems.at[k], recv_sem=recv_sems.at[k],
                        device_id=(tx, ty, to_core), device_id_type=MESH).start()
        token[...] = jnp.zeros_like(token)

    hbm = lambda t: pltpu.HBM(t.shape, t.dtype)
    keep = lambda t: pltpu.with_memory_space_constraint(t, pltpu.HBM)
    nsem = 12 * n
    outs = pl.pallas_call(
        body,
        name=name,
        in_specs=[HBM] * n + [ANY],
        out_specs=(SEM, SEM, *[HBM] * n, pl.BlockSpec(memory_space=pltpu.VMEM)),
        out_shape=(pltpu.SemaphoreType.DMA((nsem,)), pltpu.SemaphoreType.DMA((nsem,)), *[hbm(p) for p in placed],
                   jax.ShapeDtypeStruct((8, 128), F32)),
        input_output_aliases={i: 2 + i for i in range(n)},
        compiler_params=pltpu.CompilerParams(has_side_effects=DATAFLOW),
    )(*[keep(p) for p in placed], after)
    return outs[0], outs[1], list(outs[2:2 + n]), outs[-1]


def _gather_late_wait(send_sems, recv_sems, thru, after, name):
    n = len(thru)
    halves = [s.shape[1] // 2 for s in thru]

    def body(*refs):
        g_refs = refs[0:n]
        send_sems, recv_sems = refs[n], refs[n + 1]
        x, y, c = _place()
        j = 2 * x + y
        for a in range(n):
            mine = g_refs[a].at[j, pl.ds(pl.multiple_of(c * halves[a], 16), halves[a]), :]
            for r in (1, 2, 3):
                tx, ty = _chip_of(x, y, r)
                for other in (0, 1):
                    k_out = ((a * 3 + (r - 1)) * 2 + c) * 2 + other
                    pltpu.make_async_remote_copy(
                        src_ref=mine, dst_ref=mine, send_sem=send_sems.at[k_out], recv_sem=recv_sems.at[k_out],
                        device_id=(tx, ty, other), device_id_type=MESH).wait_send()
                    k_in = ((a * 3 + (r - 1)) * 2 + other) * 2 + c
                    theirs = g_refs[a].at[2 * tx + ty, pl.ds(other * halves[a], halves[a]), :]
                    pltpu.make_async_remote_copy(
                        src_ref=theirs, dst_ref=theirs, send_sem=send_sems.at[k_in], recv_sem=recv_sems.at[k_in],
                        device_id=(tx, ty, other), device_id_type=MESH).wait_recv()

    hbm = lambda t: pltpu.HBM(t.shape, t.dtype)
    outs = pl.pallas_call(
        body,
        name=name,
        in_specs=[HBM] * n + [SEM, SEM, ANY],
        out_specs=[HBM] * n,
        out_shape=[hbm(t) for t in thru],
        input_output_aliases={i: i for i in range(n)},
        compiler_params=pltpu.CompilerParams(has_side_effects=DATAFLOW),
    )(*thru, send_sems, recv_sems, after)
    return list(outs)


D2D_PIECE_ROWS = 64


def _pair_exchange(grads, name):
    n = len(grads)
    halves = [g.shape[1] // 2 for g in grads]

    def body(*refs):
        g_refs, got_refs = refs[0:n], refs[n:2 * n]
        send_sems, recv_sems = refs[2 * n:]
        x, y, c = _place()

        def copy(a, src, dst):
            return pltpu.make_async_remote_copy(
                src_ref=src, dst_ref=dst, send_sem=send_sems.at[a], recv_sem=recv_sems.at[a],
                device_id=(x, y, 1 - c), device_id_type=MESH)

        for a in range(n):
            for jj in range(N_CHIPS):
                for q in range(halves[a] // D2D_PIECE_ROWS):
                    src_rows = pl.ds(pl.multiple_of((1 - c) * halves[a] + q * D2D_PIECE_ROWS, 16), D2D_PIECE_ROWS)
                    dst_rows = pl.ds(q * D2D_PIECE_ROWS, D2D_PIECE_ROWS)
                    copy(a, g_refs[a].at[jj, src_rows, :], got_refs[a].at[jj, dst_rows, :]).start()
        for a in range(n):
            sent = g_refs[a].at[:, pl.ds(pl.multiple_of((1 - c) * halves[a], 16), halves[a]), :]
            copy(a, sent, got_refs[a]).wait()

    return pl.pallas_call(
        body,
        name=name,
        in_specs=[ANY] * n,
        out_specs=[ANY] * n,
        out_shape=[_sds((N_CHIPS, h, g.shape[2]), g.dtype) for g, h in zip(grads, halves)],
        scratch_shapes=[pltpu.SemaphoreType.DMA((n,)), pltpu.SemaphoreType.DMA((n,))],
    )(*grads)


HBM = pl.BlockSpec(memory_space=pltpu.HBM)
SEM = pl.BlockSpec(memory_space=pltpu.SEMAPHORE)
DATAFLOW = pltpu.SideEffectType.DATAFLOW_SIDE_EFFECTING


def _chip_copy(p_refs, land_refs, send_sems, recv_sems, a, r, blocked):
    x, y, c = _place()
    tx, ty = _chip_of(x, y, r)
    k = a * 3 + (r - 1)
    return pltpu.make_async_remote_copy(
        src_ref=p_refs[a].at[2 * tx + ty] if blocked else p_refs[a], dst_ref=land_refs[a].at[r - 1],
        send_sem=send_sems.at[k], recv_sem=recv_sems.at[k], device_id=(tx, ty, c), device_id_type=MESH)


def _chip_exchange_start(psums, name, blocked=True):
    n = len(psums)
    lands = [lax.empty((3,) + (p.shape[1:] if blocked else p.shape), p.dtype) for p in psums]

    def body(*refs):
        p_refs, land_refs = refs[0:n], refs[n:2 * n]
        send_sems, recv_sems, token = refs[2 * n], refs[2 * n + 1], refs[-1]
        for a in range(n):
            for r in (1, 2, 3):
                _chip_copy(p_refs, land_refs, send_sems, recv_sems, a, r, blocked).start()
        token[...] = jnp.zeros_like(token)

    hbm = lambda t: pltpu.HBM(t.shape, t.dtype)
    keep = lambda t: pltpu.with_memory_space_constraint(t, pltpu.HBM)
    outs = pl.pallas_call(
        body,
        name=name,
        in_specs=[HBM] * (2 * n),
        out_specs=(SEM, SEM, *[HBM] * (2 * n), pl.BlockSpec(memory_space=pltpu.VMEM)),
        out_shape=(pltpu.SemaphoreType.DMA((3 * n,)), pltpu.SemaphoreType.DMA((3 * n,)),
                   *[hbm(p) for p in psums], *[hbm(l) for l in lands], _sds((8, 128), F32)),
        input_output_aliases={i: 2 + i for i in range(2 * n)},
        compiler_params=pltpu.CompilerParams(has_side_effects=DATAFLOW),
    )(*[keep(p) for p in psums], *[keep(l) for l in lands])
    return outs[0], outs[1], list(outs[2:2 + n]), list(outs[2 + n:2 + 2 * n]), outs[-1]


def _chip_exchange_wait(send_sems, recv_sems, p_thru, land_thru, after, name, blocked=True):
    n = len(p_thru)

    def body(*refs):
        p_refs, land_refs = refs[0:n], refs[n:2 * n]
        send_sems, recv_sems = refs[2 * n], refs[2 * n + 1]
        for a in range(n):
            for r in (1, 2, 3):
                copy = _chip_copy(p_refs, land_refs, send_sems, recv_sems, a, r, blocked)
                copy.wait_send()
                copy.wait_recv()

    hbm = lambda t: pltpu.HBM(t.shape, t.dtype)
    outs = pl.pallas_call(
        body,
        name=name,
        in_specs=[HBM] * (2 * n) + [SEM, SEM, ANY],
        out_specs=[HBM] * (2 * n),
        out_shape=[hbm(p) for p in p_thru] + [hbm(l) for l in land_thru],
        input_output_aliases={i: i for i in range(2 * n)},
        compiler_params=pltpu.CompilerParams(has_side_effects=DATAFLOW),
    )(*p_thru, *land_thru, send_sems, recv_sems, after)
    return list(outs[n:2 * n])


def _pair_share(fulls):
    n = len(fulls)
    halves = [f.shape[0] // 2 for f in fulls]

    def body(*refs):
        full_refs = refs[n:2 * n]
        send_sems, recv_sems = refs[2 * n:]
        x, y, c = _place()

        def half_of(a, core):
            return full_refs[a].at[pl.ds(pl.multiple_of(core * halves[a], 8), halves[a]), :]

        def remote(a, src, dst):
            return pltpu.make_async_remote_copy(
                src_ref=src, dst_ref=dst, send_sem=send_sems.at[a], recv_sem=recv_sems.at[a],
                device_id=(x, y, 1 - c), device_id_type=MESH)

        for a in range(n):
            for q in range(halves[a] // D2D_PIECE_ROWS):
                piece = full_refs[a].at[
                    pl.ds(pl.multiple_of(c * halves[a] + q * D2D_PIECE_ROWS, 8), D2D_PIECE_ROWS), :]
                remote(a, piece, piece).start()
        for a in range(n):
            remote(a, half_of(a, c), half_of(a, c)).wait_send()
            remote(a, half_of(a, 1 - c), half_of(a, 1 - c)).wait_recv()

    return pl.pallas_call(
        body,
        name="pair_share",
        in_specs=[ANY] * n,
        out_specs=[ANY] * n,
        out_shape=[_sds(f.shape, F32) for f in fulls],
        input_output_aliases={a: a for a in range(n)},
        scratch_shapes=[pltpu.SemaphoreType.DMA((n,)), pltpu.SemaphoreType.DMA((n,))],
    )(*fulls)


def _small_pair_sum(s):
    R, C = s.shape
    V = SMALL_VECTOR_ROWS

    def body(s_ref, v_ref, m_ref, sib, send_sem, recv_sem):
        x, y, c = _place()

        def to_sib(src, dst):
            return pltpu.make_async_remote_copy(
                src_ref=src, dst_ref=dst, send_sem=send_sem, recv_sem=recv_sem,
                device_id=(x, y, 1 - c), device_id_type=MESH)

        for q in range(R // 8):
            to_sib(s_ref.at[pl.ds(8 * q, 8), :], sib.at[pl.ds(8 * q, 8), :]).start()
        to_sib(s_ref, sib).wait()
        v_ref[...] = s_ref[pl.ds(0, V), :] + sib[pl.ds(0, V), :]
        m_ref[...] = (s_ref[pl.ds(V, R - V), :] + sib[pl.ds(V, R - V), :]).astype(BF16)

    return pl.pallas_call(
        body,
        name="small_pair_sum",
        in_specs=[pl.BlockSpec(memory_space=pltpu.VMEM)],
        out_specs=[pl.BlockSpec(memory_space=pltpu.VMEM)] * 2,
        out_shape=[jax.ShapeDtypeStruct((V, C), F32), jax.ShapeDtypeStruct((R - V, C), BF16)],
        scratch_shapes=[pltpu.VMEM((R, C), F32), pltpu.SemaphoreType.DMA, pltpu.SemaphoreType.DMA],
    )(s)


def _small_total(chip, own, landed):
    V, C = own[0].shape
    M = own[1].shape[0]

    def body(j_ref, v_ref, m_ref, lv_ref, lm_ref, o_ref, chips_v, chips_m):
        j = j_ref[0]
        chips_v[j] = v_ref[...]
        chips_m[j] = m_ref[...]
        for r in (1, 2, 3):
            chips_v[j ^ r] = lv_ref[r - 1]
            chips_m[j ^ r] = lm_ref[r - 1]
        o_ref[pl.ds(0, V), :] = (chips_v[0] + chips_v[1]) + (chips_v[2] + chips_v[3])
        o_ref[pl.ds(V, M), :] = (chips_m[0].astype(F32) + chips_m[1].astype(F32)) + (
            chips_m[2].astype(F32) + chips_m[3].astype(F32))

    vmem = pl.BlockSpec(memory_space=pltpu.VMEM)
    return pl.pallas_call(
        body,
        name="small_total",
        in_specs=[pl.BlockSpec(memory_space=pltpu.SMEM), vmem, vmem, vmem, vmem],
        out_specs=vmem,
        out_shape=jax.ShapeDtypeStruct((V + M, C), F32),
        scratch_shapes=[pltpu.VMEM((N_CHIPS, V, C), F32), pltpu.VMEM((N_CHIPS, M, C), BF16)],
    )(chip, own[0], own[1], landed[0], landed[1])


def _block_diag(w):
    w4 = w.reshape(4, 4, RNN_BLOCK_W, RNN_BLOCK_W)
    eye = jnp.eye(4, dtype=w.dtype)
    return jnp.einsum("jaik,ab->jaibk", w4, eye).reshape(4, RNN_TILE, RNN_TILE)


def _block_diag_part(d):
    d5 = d.reshape(4, 4, RNN_BLOCK_W, 4, RNN_BLOCK_W)
    return jnp.stack([d5[:, a, :, a, :] for a in range(4)], axis=1).reshape(RNN_BLOCKS, RNN_BLOCK_W, RNN_BLOCK_W)


def _local_grads(x, target, g_pre, w_in_g, b_gate, conv_w, conv_b, w_rg_a, b_rg_a, w_rg_x, b_rg_x, lam, sinks,
                 out_weights, fwd_token, g_post, on_out_grads, on_w_in_grad):
    wa_bd = _block_diag(w_rg_a).astype(BF16)
    wx_bd = _block_diag(w_rg_x).astype(BF16)
    b_a = b_rg_a.reshape(1, D_RNN)
    b_x = b_rg_x.reshape(1, D_RNN)

    proj, ht = _proj_fwd(x, g_pre, w_in_g)
    y_rnn, z_rnn = _rnn_fwd(proj, conv_w, conv_b, wa_bd, wx_bd, b_a, b_x, lam, fwd_token)
    bias = _attn_bias()
    y_attn, z_attn, lse = _attn_fwd(proj, sinks, bias)
    w_rnn_out, w_attn_out, w_out = out_weights(z_attn)
    dyx, dz_rnn, dz_attn, dml, merged, dout, dbr_rnn, dbr_attn, head_small = _head(
        x, target, z_rnn, z_attn, proj, b_gate, g_post, w_rnn_out, w_attn_out, w_out)
    out_grads = [_matmul_tn(z_rnn, dbr_rnn, "dw_rnn_out"), _matmul_tn(z_attn, dbr_attn, "dw_attn_out"),
                 _matmul_tn(merged, dout, "dw_out")]
    shard_rows = lambda d: d.reshape(N_CHIPS, OUT_SHARD, D_MODEL)
    token = on_out_grads([shard_rows(g) for g, _ in out_grads], [shard_rows(gb) for _, gb in out_grads])
    dq, dk, dv, dag, attn_small = _attn_bwd(proj, y_attn, lse, dz_attn, sinks, bias, token)
    drx, drg, dwa_t, dwx_t, rnn_small = _rnn_bwd(proj, y_rnn, dz_rnn, conv_w, conv_b, wa_bd, wx_bd, b_a, b_x, lam)
    dproj = [drx, drg, dq, dk, dv, dag, dml]
    token = on_w_in_grad(*_dw_in(ht, dproj))
    grad_x, dh_small = _dh_bwd(dproj, w_in_g, x, dyx, g_pre, token)
    small = jnp.concatenate([rnn_small, head_small, dh_small + attn_small,
                             _block_diag_part(dwa_t).reshape(64, 1024), _block_diag_part(dwx_t).reshape(64, 1024)], axis=0)
    return grad_x, small


ROW_LOSS = 11


def _rows8(parts):
    out = None
    for r, a in parts:
        p = jnp.pad(a, ((r, 8 - r - a.shape[0]), (0, 1024 - a.shape[1])))
        out = p if out is None else out + p
    return out


def _pack_small(p):
    g0 = _rows8([(0, p["b_rg_a"].reshape(1, 1024)), (1, p["b_rg_x"].reshape(1, 1024)), (2, p["lru_lambda"]),
                 (3, p["conv_b"]), (4, p["conv_w"][0])])
    g1 = _rows8([(0, p["post_norm_g"]), (1, p["b_gate"].reshape(2, 1024))])
    g2 = _rows8([(0, p["pre_norm_g"]), (1, p["attn_sinks"])])
    return jnp.concatenate([g0, g1, g2, p["w_rg_a"].reshape(64, 1024), p["w_rg_x"].reshape(64, 1024)], axis=0)


def _unpack_small(s, conv_cols):
    return {
        "b_rg_a": s[0:1].reshape(1, 16, 64), "b_rg_x": s[1:2].reshape(1, 16, 64), "lru_lambda": s[2:3],
        "conv_b": s[3:4], "conv_w": s[4:8, 0:conv_cols].reshape(1, CONV_W, conv_cols),
        "post_norm_g": s[8:9], "b_gate": s[9:11].reshape(1, 2048),
        "pre_norm_g": s[16:17], "attn_sinks": s[17:18, 0:N_Q_HEADS],
        "w_rg_a": s[24:88].reshape(1, 16, 64, 64), "w_rg_x": s[88:152].reshape(1, 16, 64, 64),
    }


WEIGHTS = ["pre_norm_g", "w_in", "b_gate", "conv_w", "conv_b", "w_rg_a", "b_rg_a", "w_rg_x", "b_rg_x", "lru_lambda",
           "attn_sinks", "w_rnn_out", "w_attn_out", "w_out", "post_norm_g"]
BIG = ["w_in", "w_rnn_out", "w_attn_out", "w_out"]


def kernel(x, pre_norm_g, w_in, b_gate, conv_w, conv_b, w_rg_a, b_rg_a, w_rg_x, b_rg_x, lru_lambda, attn_sinks, w_rnn_out, w_attn_out, w_out, post_norm_g, loss_target, m_pre_norm_g, m_w_in, m_b_gate, m_conv_w, m_conv_b, m_w_rg_a, m_b_rg_a, m_w_rg_x, m_b_rg_x, m_lru_lambda, m_attn_sinks, m_w_rnn_out, m_w_attn_out, m_w_out, m_post_norm_g, v_pre_norm_g, v_w_in, v_b_gate, v_conv_w, v_conv_b, v_w_rg_a, v_b_rg_a, v_w_rg_x, v_b_rg_x, v_lru_lambda, v_attn_sinks, v_w_rnn_out, v_w_attn_out, v_w_out, v_post_norm_g):
    w = dict(pre_norm_g=pre_norm_g, w_in=w_in, b_gate=b_gate, conv_w=conv_w, conv_b=conv_b, w_rg_a=w_rg_a,
             b_rg_a=b_rg_a, w_rg_x=w_rg_x, b_rg_x=b_rg_x, lru_lambda=lru_lambda, attn_sinks=attn_sinks,
             w_rnn_out=w_rnn_out, w_attn_out=w_attn_out, w_out=w_out, post_norm_g=post_norm_g)
    m = dict(pre_norm_g=m_pre_norm_g, w_in=m_w_in, b_gate=m_b_gate, conv_w=m_conv_w, conv_b=m_conv_b, w_rg_a=m_w_rg_a,
             b_rg_a=m_b_rg_a, w_rg_x=m_w_rg_x, b_rg_x=m_b_rg_x, lru_lambda=m_lru_lambda, attn_sinks=m_attn_sinks,
             w_rnn_out=m_w_rnn_out, w_attn_out=m_w_attn_out, w_out=m_w_out, post_norm_g=m_post_norm_g)
    v = dict(pre_norm_g=v_pre_norm_g, w_in=v_w_in, b_gate=v_b_gate, conv_w=v_conv_w, conv_b=v_conv_b, w_rg_a=v_w_rg_a,
             b_rg_a=v_b_rg_a, w_rg_x=v_w_rg_x, b_rg_x=v_b_rg_x, lru_lambda=v_lru_lambda, attn_sinks=v_attn_sinks,
             w_rnn_out=v_w_rnn_out, w_attn_out=v_w_attn_out, w_out=v_w_out, post_norm_g=v_post_norm_g)
    chip = 2 * lax.axis_index("x") + lax.axis_index("y")

    chip_idx = chip.astype(jnp.int32).reshape(1)
    chip_core = jnp.stack([chip, lax.axis_index("c")]).astype(jnp.int32)
    cw8 = jnp.pad(conv_w[0], ((0, 8 - CONV_W), (0, 0)))
    placed = _place_shards([w_in[0], w_rnn_out[0], w_attn_out[0], w_out[0]], chip_idx, "place_shards")
    win_g, cw_g = _gather_weights(placed[:1], cw8)
    late_send, late_recv, late_thru, late_token = _gather_late_start(placed[1:], win_g, "gather_late_start")
    cw_g = lax.dynamic_update_slice_in_dim(cw_g, cw8[None], chip, axis=0)
    conv_w_full = jnp.transpose(cw_g[:, 0:CONV_W, :], (1, 0, 2)).reshape(CONV_W, D_RNN)

    core_idx = lax.axis_index("c").astype(jnp.int32).reshape(1)
    started = {}

    def start_reduction(tag, grads, grads_b16):
        got = _pair_exchange(grads_b16, "pair_exchange_" + tag)
        sums = [_pair_sum(g, o, core_idx, "pair_sum_%s_%d" % (tag, a)) for a, (g, o) in enumerate(zip(grads, got))]
        send_sems, recv_sems, p_thru, land_thru, token = _chip_exchange_start(
            [pb for _, pb in sums], "chip_exchange_start_" + tag)
        started[tag] = ([p for p, _ in sums], send_sems, recv_sems, p_thru, land_thru)
        return token

    def end_reduction(tag, after):
        psums, send_sems, recv_sems, p_thru, land_thru = started[tag]
        landed = _chip_exchange_wait(send_sems, recv_sems, p_thru, land_thru, after, "chip_exchange_wait_" + tag)
        return [_chip_sum(p, l, chip_core, "chip_sum_%s_%d" % (tag, a)) for a, (p, l) in enumerate(zip(psums, landed))]

    def out_weights(after):
        gathered = _gather_late_wait(late_send, late_recv, late_thru, after, "gather_late_wait")
        return [g.reshape(D_MODEL, D_MODEL) for g in gathered]

    grad_x, small = _local_grads(
        x[0], loss_target[0], pre_norm_g, win_g, b_gate, conv_w_full, conv_b, w_rg_a[0], b_rg_a[0], w_rg_x[0],
        b_rg_x[0], lru_lambda, attn_sinks[0], out_weights, late_token, post_norm_g,
        on_out_grads=lambda grads, grads_b16: start_reduction("out", grads, grads_b16),
        on_w_in_grad=lambda grad, grad_b16: start_reduction("in", [grad], [grad_b16]))

    small_chip = _small_pair_sum(small)
    small_send, small_recv, small_thru, small_land, small_token = _chip_exchange_start(
        list(small_chip), "small_exchange_start", blocked=False)

    halves = end_reduction("in", small_token) + end_reduction("out", small_token)
    gbig = dict(zip(BIG, _pair_share(halves)))

    grads, delta, new_m, new_v = {}, {}, {}, {}
    for n in BIG:
        grads[n] = gbig[n][None]
        d, nm, nv = _adamw(w[n][0], gbig[n], m[n][0], v[n][0], "adamw_" + n)
        delta[n], new_m[n], new_v[n] = d[None], nm[None], nv[None]

    small_landed = _chip_exchange_wait(small_send, small_recv, small_thru, small_land, delta[BIG[-1]],
                                       "small_exchange_wait", blocked=False)
    small_sum = _small_total(chip_idx, small_thru, small_landed)
    total_loss = small_sum[ROW_LOSS, 0]
    gsmall = _unpack_small(small_sum, D_RNN)
    conv_shard = D_RNN // N_CHIPS
    gsmall["conv_w"] = lax.dynamic_slice_in_dim(gsmall["conv_w"], chip * conv_shard, conv_shard, axis=2)
    pick = lambda t: {k: t[k] for k in gsmall}
    d, nm, nv = _adamw(_pack_small(pick(w)), _pack_small(gsmall), _pack_small(pick(m)), _pack_small(pick(v)),
                       "adamw_small")
    ud, um, uv = _unpack_small(d, conv_shard), _unpack_small(nm, conv_shard), _unpack_small(nv, conv_shard)
    for n in gsmall:
        grads[n] = gsmall[n].reshape(w[n].shape)
        delta[n] = ud[n].reshape(w[n].shape)
        new_m[n] = um[n].reshape(w[n].shape)
        new_v[n] = uv[n].reshape(w[n].shape)

    return (total_loss, grad_x[None], *[grads[n] for n in WEIGHTS], *[delta[n] for n in WEIGHTS],
            *[new_m[n] for n in WEIGHTS], *[new_v[n] for n in WEIGHTS])
```

```python
import functools
import math

import jax
import jax.numpy as jnp
from jax import lax
from jax.experimental import pallas as pl
from jax.experimental.pallas import tpu as pltpu

F32 = jnp.float32
BF16 = jnp.bfloat16

D_MODEL = 1024
D_RNN = 1024
RNN_BLOCKS = 16
RNN_BLOCK_W = 64
CONV_W = 4
LRU_C = 8.0
N_Q_HEADS = 16
N_KV_HEADS = 4
GROUP = 4
HEAD_DIM = 64
D_KV = 256
BLOCK = 128
ALIBI_MAX_BIAS = 8.0
EPS = 1e-6
D_IN = 6656
N_CHIPS = 4
W_IN_SHARD = D_IN // N_CHIPS
OUT_SHARD = D_MODEL // N_CHIPS
ADAM_LR = 0.001
ADAM_B1 = 0.9
ADAM_B2 = 0.999
ADAM_EPS = 1e-08
ADAM_WD = 0.01
ADAM_STEP = 10
NEG_BIG = -1e30
MIB = 1 << 20

COL_RNN_X = 0
COL_RNN_GATE = 4
COL_Q = 8
COL_K = 12
COL_V = 13
COL_ATTN_GATE = 14
COL_MERGE = 18

RNN_TILE = 256
RNN_CHUNK = 512
SMALL_ROWS = 152
SMALL_VECTOR_ROWS = 24
MESH = pl.DeviceIdType.MESH


def _sds(shape, dtype):
    return pltpu.HBM(shape, dtype)


def _params(sem=None, vmem_mib=None):
    kw = {}
    if sem is not None:
        kw["dimension_semantics"] = sem
    if vmem_mib is not None:
        kw["vmem_limit_bytes"] = vmem_mib * MIB
    return pltpu.CompilerParams(**kw)


def _hbm(*arrays):
    return [pltpu.with_memory_space_constraint(a, pltpu.HBM) for a in arrays]


def _dot(a, b):
    return jnp.dot(a, b, preferred_element_type=F32)


def _dot_nt(a, b):
    return lax.dot_general(a, b, (((1,), (1,)), ((), ())), preferred_element_type=F32)


def _dot_tn(a, b):
    return lax.dot_general(a, b, (((0,), (0,)), ((), ())), preferred_element_type=F32)


def _sigmoid(x):
    return 0.5 * jnp.tanh(0.5 * x) + 0.5


def _sigmoid_small(x):
    return 1.0 / (1.0 + jnp.exp(-x))


def _softplus(x):
    return jnp.maximum(x, 0.0) + jnp.log(1.0 + jnp.exp(-jnp.abs(x)))


def _one_minus_square(a, log_a):
    return -jnp.tanh(log_a) * (a * a + 1.0)


def _proj_fwd(x, g_pre, w_in_g):
    T = x.shape[0]
    tm = min(1024, T)

    def body(x_ref, g_ref, w_ref, proj_ref, ht_ref, h_s):
        @pl.when(pl.program_id(1) == 0)
        def _():
            xv = x_ref[...]
            rstd = lax.rsqrt(jnp.mean(xv * xv, axis=-1, keepdims=True) + EPS)
            hf = (xv * rstd) * g_ref[...]
            h_s[...] = hf.astype(BF16)
            ht_ref[...] = hf.T.astype(BF16)

        proj_ref[...] = _dot(h_s[...], w_ref[...]).astype(BF16)

    return pl.pallas_call(
        body,
        name="proj_fwd",
        grid=(T // tm, N_CHIPS),
        in_specs=[
            pl.BlockSpec((tm, D_MODEL), lambda i, j: (i, 0)),
            pl.BlockSpec((1, D_MODEL), lambda i, j: (0, 0)),
            pl.BlockSpec((None, D_MODEL, W_IN_SHARD), lambda i, j: (j, 0, 0)),
        ],
        out_specs=[
            pl.BlockSpec((tm, W_IN_SHARD), lambda i, j: (i, j)),
            pl.BlockSpec((D_MODEL, tm), lambda i, j: (0, i)),
        ],
        out_shape=[_sds((T, D_IN), BF16), _sds((D_MODEL, T), BF16)],
        scratch_shapes=[pltpu.VMEM((tm, D_MODEL), BF16)],
        compiler_params=_params(("parallel", "arbitrary"), 48),
    )(*_hbm(x, g_pre, w_in_g))


def _shift_down(x, tail, s, row):
    n = x.shape[0]
    xs = pltpu.roll(x, s, 0)
    tail_t = jnp.tile(pltpu.roll(tail, s, 0), (n // 8, 1))
    return jnp.where(row < s, tail_t, xs)


def _shift_up(x, head, s, row):
    n = x.shape[0]
    xs = pltpu.roll(x, n - s, 0)
    head_t = jnp.tile(pltpu.roll(head, 8 - s, 0), (n // 8, 1))
    return jnp.where(row >= n - s, head_t, xs)


def _conv_taps(x, tail, row):
    return [_shift_down(x, tail, 3, row), _shift_down(x, tail, 2, row), _shift_down(x, tail, 1, row), x]


def _rglru_gates(c, wa, wx, ba, bx, lam):
    cb = c.astype(BF16)
    r = _sigmoid_small(_dot(cb, wa) + ba)
    i = _sigmoid(_dot(cb, wx) + bx)
    log_a = (-LRU_C) * r * _softplus(-lam)
    a = jnp.exp(log_a)
    mult = jnp.sqrt(_one_minus_square(a, log_a))
    return cb, r, i, a, mult


SUBLANES = 8


def _scan_down(a, u, row):
    n = a.shape[0]
    s = 1
    while s < SUBLANES:
        a_sh = jnp.where(row >= s, pltpu.roll(a, s, 0), 1.0)
        u_sh = jnp.where(row >= s, pltpu.roll(u, s, 0), 0.0)
        u = a * u_sh + u
        a = a * a_sh
        s *= 2
    while s < n:
        u = jnp.concatenate([u[:s], a[s:] * u[:n - s] + u[s:]], axis=0)
        a = jnp.concatenate([a[:s], a[s:] * a[:n - s]], axis=0)
        s *= 2
    return a, u


def _scan_up(b, u, row):
    n = b.shape[0]
    s = 1
    while s < SUBLANES:
        b_sh = jnp.where(row < n - s, pltpu.roll(b, n - s, 0), 1.0)
        u_sh = jnp.where(row < n - s, pltpu.roll(u, n - s, 0), 0.0)
        u = b * u_sh + u
        b = b * b_sh
        s *= 2
    while s < n:
        u = jnp.concatenate([b[:n - s] * u[s:] + u[:n - s], u[n - s:]], axis=0)
        b = jnp.concatenate([b[:n - s] * b[s:], b[n - s:]], axis=0)
        s *= 2
    return b, u


LANES = 128


def _chunk_scan(a, u, a_s, u_s, hl_s, al_s, carry, reverse):
    n, width = a.shape
    groups = n // SUBLANES
    order = range(SUBLANES - 1, -1, -1) if reverse else range(SUBLANES)
    row = lax.broadcasted_iota(jnp.int32, (groups, LANES), 0)
    for l in range(width // LANES):
        lanes = slice(l * LANES, (l + 1) * LANES)
        a_l, u_l, hl_l, al_l = a_s.at[l], u_s.at[l], hl_s.at[l], al_s.at[l]
        a_l[...] = a[:, lanes]
        u_l[...] = u[:, lanes]
        h_loc = a_loc = None
        for r in order:
            rows = pl.ds(r, groups, stride=SUBLANES)
            a_r, u_r = a_l[rows, :], u_l[rows, :]
            h_loc, a_loc = (u_r, a_r) if h_loc is None else (a_r * h_loc + u_r, a_r * a_loc)
            hl_l[rows, :] = h_loc
            al_l[rows, :] = a_loc
        if reverse:
            a_cum, ends = _scan_up(a_loc, h_loc, row)
            ends = ends + a_cum * carry[:, lanes]
            enters = jnp.where(row == groups - 1, carry[:, lanes], pltpu.roll(ends, groups - 1, 0))
        else:
            a_cum, ends = _scan_down(a_loc, h_loc, row)
            ends = ends + a_cum * carry[:, lanes]
            enters = jnp.where(row == 0, carry[:, lanes], pltpu.roll(ends, 1, 0))
        for r in range(SUBLANES):
            rows = pl.ds(r, groups, stride=SUBLANES)
            hl_l[rows, :] = hl_l[rows, :] + al_l[rows, :] * enters
    return jnp.concatenate([hl_s[l] for l in range(width // LANES)], axis=1)


def _rnn_fwd(proj, conv_w, conv_b, wa_bd, wx_bd, b_a, b_x, lam, token):
    T = proj.shape[0]
    tc, ct = RNN_CHUNK, RNN_TILE
    nt = T // tc

    def body(x_ref, rg_ref, cw_ref, cb_ref, wa_ref, wx_ref, ba_ref, bx_ref, lam_ref, token_ref, h_ref, z_ref, xtail,
             hcarry, a_s, u_s, hl_s, al_s):
        @pl.when(pl.program_id(1) == 0)
        def _():
            xtail[...] = jnp.zeros_like(xtail)
            hcarry[...] = jnp.zeros_like(hcarry)

        row = lax.broadcasted_iota(jnp.int32, (tc, ct), 0)
        x = x_ref[...].astype(F32)
        taps = _conv_taps(x, xtail[...], row)
        c = cb_ref[...] + cw_ref[pl.ds(0, 1), :] * taps[0]
        for k in range(1, CONV_W):
            c = c + cw_ref[pl.ds(k, 1), :] * taps[k]
        xtail[...] = x[tc - 8:, :]
        _, _, i, a, mult = _rglru_gates(c, wa_ref[...], wx_ref[...], ba_ref[...], bx_ref[...], lam_ref[...])
        h = _chunk_scan(a, mult * (i * c), a_s, u_s, hl_s, al_s, hcarry[...], reverse=False)
        h_ref[...] = h
        hcarry[...] = h_ref[pl.ds(tc - 1, 1), :]
        rg = rg_ref[...].astype(F32)
        z_ref[...] = (h * (rg * _sigmoid(rg))).astype(BF16)

    col = lambda off: (lambda j, t: (t, off + j))
    vec = pl.BlockSpec((1, ct), lambda j, t: (0, j))
    mat = pl.BlockSpec((None, ct, ct), lambda j, t: (j, 0, 0))
    return pl.pallas_call(
        body,
        name="rnn_fwd",
        grid=(D_RNN // ct, nt),
        in_specs=[
            pl.BlockSpec((tc, ct), col(COL_RNN_X)),
            pl.BlockSpec((tc, ct), col(COL_RNN_GATE)),
            pl.BlockSpec((CONV_W, ct), lambda j, t: (0, j)),
            vec, mat, mat, vec, vec, vec,
            pl.BlockSpec((8, 128), lambda j, t: (0, 0)),
        ],
        out_specs=[pl.BlockSpec((tc, ct), lambda j, t: (t, j)), pl.BlockSpec((tc, ct), lambda j, t: (t, j))],
        out_shape=[_sds((T, D_RNN), F32), _sds((T, D_RNN), BF16)],
        scratch_shapes=[pltpu.VMEM((8, ct), F32), pltpu.VMEM((1, ct), F32)] + [
            pltpu.VMEM((ct // LANES, tc, LANES), F32)] * 4,
        compiler_params=_params(("parallel", "arbitrary"), 32),
    )(*_hbm(proj, proj, conv_w, conv_b, wa_bd, wx_bd, b_a, b_x, lam, token))


def _rnn_bwd(proj, y_rnn, dz_rnn, conv_w, conv_b, wa_bd, wx_bd, b_a, b_x, lam):
    T = proj.shape[0]
    tc, ct = RNN_CHUNK, RNN_TILE
    nt = T // tc
    hb = tc // 8

    def body(x_ref, xh_ref, rg_ref, h_ref, hh_ref, dz_ref, cw_ref, cb_ref, wa_ref, wx_ref, ba_ref, bx_ref, lam_ref,
             dx_ref, drg_ref, dwa_ref, dwx_ref, sm_ref, lam_carry, a_carry, dc_head, b_s, dy_s, hl_s, al_s):
        t = pl.program_id(1)
        first_chunk = t == nt - 1

        @pl.when(t == 0)
        def _():
            lam_carry[...] = jnp.zeros_like(lam_carry)
            a_carry[...] = jnp.zeros_like(a_carry)
            dc_head[...] = jnp.zeros_like(dc_head)
            dwa_ref[...] = jnp.zeros_like(dwa_ref)
            dwx_ref[...] = jnp.zeros_like(dwx_ref)
            sm_ref[...] = jnp.zeros_like(sm_ref)

        row = lax.broadcasted_iota(jnp.int32, (tc, ct), 0)
        keep = jnp.where(first_chunk, 0.0, 1.0)
        x = x_ref[...].astype(F32)
        xtail = xh_ref[...].astype(F32)[8:16, :] * keep
        taps = _conv_taps(x, xtail, row)
        c = cb_ref[...] + cw_ref[pl.ds(0, 1), :] * taps[0]
        for k in range(1, CONV_W):
            c = c + cw_ref[pl.ds(k, 1), :] * taps[k]
        lam = lam_ref[...]
        cb, r, i, a, mult = _rglru_gates(c, wa_ref[...], wx_ref[...], ba_ref[...], bx_ref[...], lam)
        h = h_ref[...]
        h_prev = _shift_down(h, hh_ref[...] * keep, 1, row)
        rg = rg_ref[...].astype(F32)
        dz = dz_ref[...]
        sg = _sigmoid(rg)
        drg_ref[...] = (dz * h * (sg * (1.0 + rg * (1.0 - sg)))).astype(BF16)
        dy = dz * (rg * sg)
        b = jnp.where(row >= tc - 1, a_carry[pl.ds(0, 1), :], pltpu.roll(a, tc - 1, 0))
        lt = _chunk_scan(b, dy, b_s, dy_s, hl_s, al_s, lam_carry[pl.ds(0, 1), :], reverse=True)
        lam_carry[...] = lt[0:8, :]
        a_carry[...] = a[0:8, :]
        ic = i * c
        dmult = lt * ic
        di = lt * mult * c
        dc = lt * mult * i
        dlog_a = a * (lt * h_prev - dmult * a / mult)
        sp = _softplus(-lam)
        dpre_r = dlog_a * ((-LRU_C) * sp) * (r * (1.0 - r))
        dpre_i = di * (i * (1.0 - i))
        dlam_row = jnp.sum(dlog_a * r, axis=0, keepdims=True) * (LRU_C * _sigmoid(-lam))
        dpr_b = dpre_r.astype(BF16)
        dpi_b = dpre_i.astype(BF16)
        dwa_ref[...] += _dot_tn(cb, dpr_b)
        dwx_ref[...] += _dot_tn(cb, dpi_b)
        dc = dc + _dot_nt(dpr_b, wa_ref[...]) + _dot_nt(dpi_b, wx_ref[...])
        head = dc_head[...]
        dx = cw_ref[pl.ds(3, 1), :] * dc
        for m in range(1, CONV_W):
            dx = dx + cw_ref[pl.ds(3 - m, 1), :] * _shift_up(dc, head, m, row)
        dx_ref[...] = dx.astype(BF16)
        dc_head[...] = dc[0:8, :]
        sm_ref[pl.ds(0, 1), :] += jnp.sum(dpre_r, axis=0, keepdims=True)
        sm_ref[pl.ds(1, 1), :] += jnp.sum(dpre_i, axis=0, keepdims=True)
        sm_ref[pl.ds(2, 1), :] += dlam_row
        sm_ref[pl.ds(3, 1), :] += jnp.sum(dc, axis=0, keepdims=True)
        for k in range(CONV_W):
            sm_ref[pl.ds(4 + k, 1), :] += jnp.sum(dc * taps[k], axis=0, keepdims=True)

    rev = lambda off: (lambda j, t: (nt - 1 - t, off + j))
    halo = lambda off: (lambda j, t: (jnp.maximum((nt - 1 - t) * hb - 1, 0), off + j))
    halo16 = lambda off: (lambda j, t: (jnp.maximum((nt - 1 - t) * (hb // 2) - 1, 0), off + j))
    vec = pl.BlockSpec((1, ct), lambda j, t: (0, j))
    mat = pl.BlockSpec((None, ct, ct), lambda j, t: (j, 0, 0))
    return pl.pallas_call(
        body,
        name="rnn_bwd",
        grid=(D_RNN // ct, nt),
        in_specs=[
            pl.BlockSpec((tc, ct), rev(COL_RNN_X)),
            pl.BlockSpec((16, ct), halo16(COL_RNN_X)),
            pl.BlockSpec((tc, ct), rev(COL_RNN_GATE)),
            pl.BlockSpec((tc, ct), rev(0)),
            pl.BlockSpec((8, ct), halo(0)),
            pl.BlockSpec((tc, ct), rev(0)),
            pl.BlockSpec((CONV_W, ct), lambda j, t: (0, j)),
            vec, mat, mat, vec, vec, vec,
        ],
        out_specs=[
            pl.BlockSpec((tc, ct), rev(0)),
            pl.BlockSpec((tc, ct), rev(0)),
            mat, mat,
            pl.BlockSpec((8, ct), lambda j, t: (0, j)),
        ],
        out_shape=[_sds((T, D_RNN), BF16), _sds((T, D_RNN), BF16), _sds((D_RNN // ct, ct, ct), F32),
                   _sds((D_RNN // ct, ct, ct), F32), _sds((8, D_RNN), F32)],
        scratch_shapes=[pltpu.VMEM((8, ct), F32)] * 3 + [pltpu.VMEM((ct // LANES, tc, LANES), F32)] * 4,
        compiler_params=_params(("parallel", "arbitrary"), 32),
    )(*_hbm(proj, proj, proj, y_rnn, y_rnn, dz_rnn, conv_w, conv_b, wa_bd, wx_bd, b_a, b_x, lam))


def _attn_bias():
    qi = jnp.arange(BLOCK)[:, None]
    kj = jnp.arange(BLOCK)[None, :]
    dist_cur = (qi - kj).astype(F32)
    slopes = 2.0 ** (-ALIBI_MAX_BIAS * jnp.arange(1, N_Q_HEADS + 1, dtype=F32) / N_Q_HEADS)
    slopes = slopes[:, None, None]
    prev = jnp.where(kj > qi, -slopes * (dist_cur + float(BLOCK)), NEG_BIG)
    cur = jnp.where(kj <= qi, -slopes * dist_cur, NEG_BIG)
    later = jnp.concatenate([prev, cur], axis=-1)
    first = jnp.concatenate([jnp.full_like(prev, NEG_BIG), cur], axis=-1)
    return jnp.stack([first, later])


def _attn_exps(s_prev, s_cur, sink, bias):
    s_prev = s_prev + bias[:, 0:BLOCK]
    s_cur = s_cur + bias[:, BLOCK:2 * BLOCK]
    m = jnp.maximum(jnp.max(jnp.maximum(s_prev, s_cur), axis=-1, keepdims=True), sink)
    p_prev = jnp.exp(s_prev - m)
    p_cur = jnp.exp(s_cur - m)
    total = jnp.sum(p_prev + p_cur, axis=-1, keepdims=True) + jnp.exp(sink - m)
    return p_prev, p_cur, 1.0 / total, m + jnp.log(total)


def _attn_probs(s_prev, s_cur, sink, bias, lse):
    p_prev = jnp.exp((s_prev + bias[:, 0:BLOCK]) - lse)
    p_cur = jnp.exp((s_cur + bias[:, BLOCK:2 * BLOCK]) - lse)
    return p_prev, p_cur, jnp.exp(sink - lse)


def _stack_heads(ref, rows, hk, dtype):
    parts = [ref[rows, (GROUP * hk + g) * HEAD_DIM:(GROUP * hk + g + 1) * HEAD_DIM] for g in range(GROUP)]
    return jnp.concatenate(parts, axis=0).astype(dtype)


ATTN_SCALE = HEAD_DIM ** -0.5
ATTN_STEP_BLOCKS = 2
ATTN_STEP = ATTN_STEP_BLOCKS * BLOCK


def _bias_spec():
    return pl.BlockSpec((2, N_Q_HEADS, BLOCK, 2 * BLOCK), lambda i: (0, 0, 0, 0))


def _attn_kv_specs():
    prev = lambda c: (lambda i: (jnp.maximum(ATTN_STEP_BLOCKS * i - 1, 0), c))
    cur = lambda c: (lambda i: (i, c))
    return [pl.BlockSpec((BLOCK, D_KV), prev(COL_K)), pl.BlockSpec((ATTN_STEP, D_KV), cur(COL_K)),
            pl.BlockSpec((BLOCK, D_KV), prev(COL_V)), pl.BlockSpec((ATTN_STEP, D_KV), cur(COL_V))]


def _attn_block_views(sb, prev_ref, cur_ref, ks):
    own = cur_ref[sb * BLOCK:(sb + 1) * BLOCK, ks]
    before = prev_ref[:, ks] if sb == 0 else cur_ref[(sb - 1) * BLOCK:sb * BLOCK, ks]
    return before.astype(BF16), own.astype(BF16)


def _attn_fwd(proj, sinks, bias):
    T = proj.shape[0]

    def body(sink_ref, bias_ref, q_ref, kp_ref, kc_ref, vp_ref, vc_ref, ag0_ref, ag1_ref, y_ref, z_ref, lse_ref):
        first_variant = jnp.minimum(pl.program_id(0), 1)
        for sb in range(ATTN_STEP_BLOCKS):
            qrows = slice(sb * BLOCK, (sb + 1) * BLOCK)
            variant = first_variant if sb == 0 else 1
            for hk in range(N_KV_HEADS):
                ks = slice(hk * HEAD_DIM, (hk + 1) * HEAD_DIM)
                qg = (_stack_heads(q_ref, qrows, hk, F32) * ATTN_SCALE).astype(BF16)
                kp, kc = _attn_block_views(sb, kp_ref, kc_ref, ks)
                vp, vc = _attn_block_views(sb, vp_ref, vc_ref, ks)
                s_prev = _dot_nt(qg, kp)
                s_cur = _dot_nt(qg, kc)
                pp, pc, invs = [], [], []
                for g in range(GROUP):
                    h = GROUP * hk + g
                    rows = slice(g * BLOCK, (g + 1) * BLOCK)
                    p_prev, p_cur, inv, lse = _attn_exps(s_prev[rows], s_cur[rows], sink_ref[h], bias_ref[variant, h])
                    pp.append(p_prev.astype(BF16))
                    pc.append(p_cur.astype(BF16))
                    invs.append(inv)
                    lse_ref[qrows, h:h + 1] = lse
                og = _dot(jnp.concatenate(pp, axis=0), vp) + _dot(jnp.concatenate(pc, axis=0), vc)
                for g in range(GROUP):
                    h = GROUP * hk + g
                    y_ref[qrows, h * HEAD_DIM:(h + 1) * HEAD_DIM] = og[g * BLOCK:(g + 1) * BLOCK] * invs[g]
        ag = jnp.concatenate([ag0_ref[...], ag1_ref[...]], axis=1).astype(F32)
        z_ref[...] = (y_ref[...] * (ag * _sigmoid(ag))).astype(BF16)

    blk = pl.BlockSpec((ATTN_STEP, 1024), lambda i: (i, 0))
    return pl.pallas_call(
        body,
        name="attn_fwd",
        grid=(T // ATTN_STEP,),
        in_specs=[
            pl.BlockSpec(memory_space=pltpu.SMEM),
            _bias_spec(),
            pl.BlockSpec((ATTN_STEP, 1024), lambda i: (i, COL_Q // 4)),
            *_attn_kv_specs(),
            pl.BlockSpec((ATTN_STEP, 512), lambda i: (i, COL_ATTN_GATE // 2)),
            pl.BlockSpec((ATTN_STEP, 512), lambda i: (i, COL_ATTN_GATE // 2 + 1)),
        ],
        out_specs=[blk, blk, pl.BlockSpec((ATTN_STEP, N_Q_HEADS), lambda i: (i, 0))],
        out_shape=[_sds((T, 1024), F32), _sds((T, 1024), BF16), _sds((T, N_Q_HEADS), F32)],
        compiler_params=_params(("arbitrary",), 40),
    )(sinks, *_hbm(bias, proj, proj, proj, proj, proj, proj, proj))


def _attn_bwd(proj, y_attn, lse, dz_attn, sinks, bias, token):
    T = proj.shape[0]

    def body(sink_ref, bias_ref, q_ref, kp_ref, kc_ref, vp_ref, vc_ref, ag0_ref, ag1_ref, y_ref, lse_ref, dz_ref,
             token_ref, dq_ref, dk_ref, dv_ref, dag_ref, ds_ref, dy_s):
        i = pl.program_id(0)

        @pl.when(i == 0)
        def _():
            ds_ref[...] = jnp.zeros_like(ds_ref)

        lane = lax.broadcasted_iota(jnp.int32, (8, 128), 1)
        sub = lax.broadcasted_iota(jnp.int32, (8, 128), 0)
        ag = jnp.concatenate([ag0_ref[...], ag1_ref[...]], axis=1).astype(F32)
        dz = dz_ref[...]
        sg = _sigmoid(ag)
        dag_ref[...] = (dz * y_ref[...] * (sg * (1.0 + ag * (1.0 - sg)))).astype(BF16)
        dy_s[...] = dz * (ag * sg)
        first_variant = jnp.minimum(i, 1)
        ds_acc = jnp.zeros((8, 128), F32)
        for sb in range(ATTN_STEP_BLOCKS):
            qrows = slice(sb * BLOCK, (sb + 1) * BLOCK)
            variant = first_variant if sb == 0 else 1
            r_cur = pl.multiple_of((ATTN_STEP_BLOCKS * i + sb) * BLOCK, BLOCK)
            r_prev = pl.multiple_of(jnp.maximum(ATTN_STEP_BLOCKS * i + sb - 1, 0) * BLOCK, BLOCK)
            dk_prev, dv_prev = [], []
            for hk in range(N_KV_HEADS):
                ks = slice(hk * HEAD_DIM, (hk + 1) * HEAD_DIM)
                qg = (_stack_heads(q_ref, qrows, hk, F32) * ATTN_SCALE).astype(BF16)
                dog = _stack_heads(dy_s, qrows, hk, F32)
                og = _stack_heads(y_ref, qrows, hk, F32)
                dog_b = dog.astype(BF16)
                kp, kc = _attn_block_views(sb, kp_ref, kc_ref, ks)
                vp, vc = _attn_block_views(sb, vp_ref, vc_ref, ks)
                s_prev = _dot_nt(qg, kp)
                s_cur = _dot_nt(qg, kc)
                dp_prev = _dot_nt(dog_b, vp)
                dp_cur = _dot_nt(dog_b, vc)
                dvec = jnp.sum(dog * og, axis=-1, keepdims=True)
                pp, pc, dsp, dsc = [], [], [], []
                for g in range(GROUP):
                    h = GROUP * hk + g
                    rows = slice(g * BLOCK, (g + 1) * BLOCK)
                    p_prev, p_cur, p_sink = _attn_probs(
                        s_prev[rows], s_cur[rows], sink_ref[h], bias_ref[variant, h], lse_ref[qrows, h:h + 1])
                    d_h = dvec[rows]
                    pp.append(p_prev.astype(BF16))
                    pc.append(p_cur.astype(BF16))
                    dsp.append((p_prev * (dp_prev[rows] - d_h)).astype(BF16))
                    dsc.append((p_cur * (dp_cur[rows] - d_h)).astype(BF16))
                    dsink = -jnp.sum(p_sink * d_h, axis=0, keepdims=True)
                    ds_acc = ds_acc + jnp.where(jnp.logical_and(lane == h, sub == 1), dsink, 0.0)
                pp = jnp.concatenate(pp, axis=0)
                pc = jnp.concatenate(pc, axis=0)
                dsp = jnp.concatenate(dsp, axis=0)
                dsc = jnp.concatenate(dsc, axis=0)
                dqg = (_dot(dsp, kp) + _dot(dsc, kc)) * ATTN_SCALE
                for g in range(GROUP):
                    h = GROUP * hk + g
                    dq_ref[qrows, h * HEAD_DIM:(h + 1) * HEAD_DIM] = dqg[g * BLOCK:(g + 1) * BLOCK].astype(BF16)
                dk_ref[pl.ds(r_cur, BLOCK), ks] = _dot_tn(dsc, qg)
                dv_ref[pl.ds(r_cur, BLOCK), ks] = _dot_tn(pc, dog_b)
                dk_prev.append(_dot_tn(dsp, qg))
                dv_prev.append(_dot_tn(pp, dog_b))

            def add_prev(dk_prev=dk_prev, dv_prev=dv_prev, r_prev=r_prev):
                for hk in range(N_KV_HEADS):
                    ks = slice(hk * HEAD_DIM, (hk + 1) * HEAD_DIM)
                    dk_ref[pl.ds(r_prev, BLOCK), ks] += dk_prev[hk]
                    dv_ref[pl.ds(r_prev, BLOCK), ks] += dv_prev[hk]

            if sb == 0:
                pl.when(i > 0)(add_prev)
            else:
                add_prev()
        ds_ref[:, 0:128] += ds_acc

    blk = pl.BlockSpec((ATTN_STEP, 1024), lambda i: (i, 0))
    whole = pl.BlockSpec((T, D_KV), lambda i: (0, 0))
    return pl.pallas_call(
        body,
        name="attn_bwd",
        grid=(T // ATTN_STEP,),
        in_specs=[
            pl.BlockSpec(memory_space=pltpu.SMEM),
            _bias_spec(),
            pl.BlockSpec((ATTN_STEP, 1024), lambda i: (i, COL_Q // 4)),
            *_attn_kv_specs(),
            pl.BlockSpec((ATTN_STEP, 512), lambda i: (i, COL_ATTN_GATE // 2)),
            pl.BlockSpec((ATTN_STEP, 512), lambda i: (i, COL_ATTN_GATE // 2 + 1)),
            blk,
            pl.BlockSpec((ATTN_STEP, N_Q_HEADS), lambda i: (i, 0)),
            blk,
            pl.BlockSpec((8, 128), lambda i: (0, 0)),
        ],
        out_specs=[blk, whole, whole, blk, pl.BlockSpec((8, 1024), lambda i: (0, 0))],
        out_shape=[_sds((T, 1024), BF16), _sds((T, D_KV), F32), _sds((T, D_KV), F32), _sds((T, 1024), BF16),
                   _sds((8, 1024), F32)],
        scratch_shapes=[pltpu.VMEM((ATTN_STEP, 1024), F32)],
        compiler_params=_params(("arbitrary",), 48),
    )(sinks, *_hbm(bias, proj, proj, proj, proj, proj, proj, proj, y_attn, lse, dz_attn, token))


def _head(x, target, z_rnn, z_attn, proj, b_gate, g_post, w_rnn_out, w_attn_out, w_out):
    T = x.shape[0]
    tm = 256

    def body(x_ref, t_ref, zr_ref, za_ref, ml0_ref, ml1_ref, ml2_ref, ml3_ref, bg_ref, gp_ref, wr_ref, wa_ref, wo_ref,
             dyx_ref, dzr_ref, dza_ref, dml_ref, mb_ref, dout_ref, dbr_ref, dba_ref, sm_ref):
        @pl.when(pl.program_id(0) == 0)
        def _():
            sm_ref[...] = jnp.zeros_like(sm_ref)

        wr, wa, wo = wr_ref[...], wa_ref[...], wo_ref[...]
        br_rnn = _dot(zr_ref[...], wr)
        br_attn = _dot(za_ref[...], wa)
        ml_rnn = jnp.concatenate([ml0_ref[...], ml1_ref[...]], axis=1).astype(F32)
        ml_attn = jnp.concatenate([ml2_ref[...], ml3_ref[...]], axis=1).astype(F32)
        g_rnn = _sigmoid(ml_rnn + bg_ref[:, 0:D_MODEL])
        g_attn = _sigmoid(ml_attn + bg_ref[:, D_MODEL:2 * D_MODEL])
        mb = (g_rnn * br_rnn + g_attn * br_attn).astype(BF16)
        mb_ref[...] = mb
        out = _dot(mb, wo)
        rstd = lax.rsqrt(jnp.mean(out * out, axis=-1, keepdims=True) + EPS)
        n = out * rstd
        gp = gp_ref[...]
        err = (x_ref[...] + n * gp) - t_ref[...]
        sm_ref[pl.ds(3, 1), :] += 0.5 * jnp.sum(jnp.mean(err * err, axis=-1, keepdims=True), axis=0, keepdims=True)
        dy = err * (1.0 / D_MODEL)
        dyx_ref[...] = dy
        sm_ref[pl.ds(0, 1), :] += jnp.sum(dy * n, axis=0, keepdims=True)
        dn = dy * gp
        dout = (rstd * (dn - n * jnp.mean(dn * n, axis=-1, keepdims=True))).astype(BF16)
        dout_ref[...] = dout
        dmerged = _dot_nt(dout, wo)
        dml_r = (dmerged * br_rnn) * (g_rnn * (1.0 - g_rnn))
        dml_a = (dmerged * br_attn) * (g_attn * (1.0 - g_attn))
        dml_ref[:, 0:D_MODEL] = dml_r.astype(BF16)
        dml_ref[:, D_MODEL:2 * D_MODEL] = dml_a.astype(BF16)
        sm_ref[pl.ds(1, 1), :] += jnp.sum(dml_r, axis=0, keepdims=True)
        sm_ref[pl.ds(2, 1), :] += jnp.sum(dml_a, axis=0, keepdims=True)
        dbr = (dmerged * g_rnn).astype(BF16)
        dba = (dmerged * g_attn).astype(BF16)
        dbr_ref[...] = dbr
        dba_ref[...] = dba
        dzr_ref[...] = _dot_nt(dbr, wr)
        dza_ref[...] = _dot_nt(dba, wa)

    tile = pl.BlockSpec((tm, D_MODEL), lambda i: (i, 0))
    wspec = pl.BlockSpec((D_MODEL, D_MODEL), lambda i: (0, 0))
    ml = lambda q: pl.BlockSpec((tm, 512), lambda i: (i, COL_MERGE // 2 + q))
    return pl.pallas_call(
        body,
        name="head",
        grid=(T // tm,),
        in_specs=[
            tile, tile, tile, tile,
            ml(0), ml(1), ml(2), ml(3),
            pl.BlockSpec((1, 2 * D_MODEL), lambda i: (0, 0)),
            pl.BlockSpec((1, D_MODEL), lambda i: (0, 0)),
            wspec, wspec, wspec,
        ],
        out_specs=[
            tile, tile, tile,
            pl.BlockSpec((tm, 2 * D_MODEL), lambda i: (i, 0)),
            tile, tile, tile, tile,
            pl.BlockSpec((8, D_MODEL), lambda i: (0, 0)),
        ],
        out_shape=[
            _sds((T, D_MODEL), F32), _sds((T, D_MODEL), F32), _sds((T, D_MODEL), F32),
            _sds((T, 2 * D_MODEL), BF16),
            _sds((T, D_MODEL), BF16), _sds((T, D_MODEL), BF16), _sds((T, D_MODEL), BF16), _sds((T, D_MODEL), BF16),
            _sds((8, D_MODEL), F32),
        ],
        compiler_params=_params(("arbitrary",), 56),
    )(*_hbm(x, target, z_rnn, z_attn, proj, proj, proj, proj, b_gate, g_post, w_rnn_out, w_attn_out, w_out))


def _matmul_tn(a, b, name):
    T, M = a.shape
    N = b.shape[1]
    tk = min(512, T)
    nt = T // tk

    def body(a_ref, b_ref, o_ref, ob_ref):
        @pl.when(pl.program_id(0) == 0)
        def _():
            o_ref[...] = jnp.zeros_like(o_ref)

        o_ref[...] += _dot_tn(a_ref[...], b_ref[...])

        @pl.when(pl.program_id(0) == nt - 1)
        def _():
            ob_ref[...] = o_ref[...].astype(BF16)

    whole = pl.BlockSpec((M, N), lambda t: (0, 0))
    return pl.pallas_call(
        body,
        name=name,
        grid=(nt,),
        in_specs=[pl.BlockSpec((tk, M), lambda t: (t, 0)), pl.BlockSpec((tk, N), lambda t: (t, 0))],
        out_specs=[whole, whole],
        out_shape=[_sds((M, N), F32), _sds((M, N), BF16)],
        compiler_params=_params(("arbitrary",), 48),
    )(*_hbm(a, b))


DPROJ_WIDTHS = (D_RNN, D_RNN, 1024, D_KV, D_KV, 1024, 2 * D_MODEL)


def _dproj_segments():
    segs, start = [[] for _ in range(N_CHIPS)], 0
    for p, width in enumerate(DPROJ_WIDTHS):
        for c in range(N_CHIPS):
            lo, hi = max(start, c * W_IN_SHARD), min(start + width, (c + 1) * W_IN_SHARD)
            if lo < hi:
                segs[c].append((p, lo - start, hi - start, lo - c * W_IN_SHARD, hi - c * W_IN_SHARD))
        start += width
    return segs


def _dh_bwd(pieces, w_in_g, x, dyx, g_pre, token):
    T = x.shape[0]
    tm = min(512, T)
    n = len(pieces)
    segs = _dproj_segments()

    def body(*refs):
        p_refs, w_hbm, x_ref, dyx_ref, g_ref = refs[0:n], refs[n], refs[n + 1], refs[n + 2], refs[n + 3]
        gx_ref, dg_ref, w_ref = refs[n + 5], refs[n + 6], refs[n + 7]

        @pl.when(pl.program_id(0) == 0)
        def _():
            pltpu.sync_copy(w_hbm, w_ref)
            dg_ref[...] = jnp.zeros_like(dg_ref)

        dh = None
        for c in range(N_CHIPS):
            for p, a0, a1, u0, u1 in segs[c]:
                part = _dot_nt(p_refs[p][:, a0:a1].astype(BF16), w_ref[c, :, u0:u1])
                dh = part if dh is None else dh + part
        xv = x_ref[...]
        rstd = lax.rsqrt(jnp.mean(xv * xv, axis=-1, keepdims=True) + EPS)
        nx = xv * rstd
        dhg = dh * g_ref[...]
        gx_ref[...] = dyx_ref[...] + rstd * (dhg - nx * jnp.mean(dhg * nx, axis=-1, keepdims=True))
        dg_ref[pl.ds(0, 1), :] += jnp.sum(dh * nx, axis=0, keepdims=True)

    tile = pl.BlockSpec((tm, D_MODEL), lambda i: (i, 0))
    return pl.pallas_call(
        body,
        name="dh_bwd",
        grid=(T // tm,),
        in_specs=[pl.BlockSpec((tm, w), lambda i: (i, 0)) for w in DPROJ_WIDTHS] + [
            ANY, tile, tile,
            pl.BlockSpec((1, D_MODEL), lambda i: (0, 0)),
            pl.BlockSpec((8, 128), lambda i: (0, 0)),
        ],
        out_specs=[tile, pl.BlockSpec((8, D_MODEL), lambda i: (0, 0))],
        out_shape=[_sds((T, D_MODEL), F32), _sds((8, D_MODEL), F32)],
        scratch_shapes=[pltpu.VMEM(w_in_g.shape, BF16)],
        compiler_params=_params(("arbitrary",), 56),
    )(*_hbm(*pieces, w_in_g, x, dyx, g_pre, token))


def _dw_in(ht, pieces):
    T = ht.shape[1]
    tk = min(512, T)
    nt = T // tk
    n = len(pieces)
    segs = _dproj_segments()

    def body(*refs):
        h_ref, p_refs, o_ref, ob_ref = refs[0], refs[1:n + 1], refs[n + 1], refs[n + 2]

        @pl.when(pl.program_id(1) == 0)
        def _():
            o_ref[...] = jnp.zeros_like(o_ref)

        for c in range(N_CHIPS):
            @pl.when(pl.program_id(0) == c)
            def _():
                for p, a0, a1, u0, u1 in segs[c]:
                    o_ref[:, u0:u1] += _dot(h_ref[...], p_refs[p][:, a0:a1].astype(BF16))

        @pl.when(pl.program_id(1) == nt - 1)
        def _():
            ob_ref[...] = o_ref[...].astype(BF16)

    def piece_spec(p):
        chips = [c for c in range(N_CHIPS) if any(s[0] == p for s in segs[c])]

        def index(c, t):
            used = functools.reduce(jnp.logical_or, [c == k for k in chips])
            return (jnp.where(used, t, 0), 0)

        return pl.BlockSpec((tk, DPROJ_WIDTHS[p]), index)

    return pl.pallas_call(
        body,
        name="dw_in",
        grid=(N_CHIPS, nt),
        in_specs=[pl.BlockSpec((D_MODEL, tk), lambda c, t: (0, t))] + [piece_spec(p) for p in range(n)],
        out_specs=[pl.BlockSpec((None, D_MODEL, W_IN_SHARD), lambda c, t: (c, 0, 0))] * 2,
        out_shape=[_sds((N_CHIPS, D_MODEL, W_IN_SHARD), F32), _sds((N_CHIPS, D_MODEL, W_IN_SHARD), BF16)],
        compiler_params=_params(("parallel", "arbitrary"), 56),
    )(*_hbm(ht, *pieces))


ELEMENTWISE_TILE_BYTES = MIB


def _row_tile(rows, cols):
    if rows * cols * 4 <= ELEMENTWISE_TILE_BYTES:
        return rows
    for t in (512, 256, 128, 64, 32, 16, 8):
        if rows % t == 0 and t * cols * 4 <= ELEMENTWISE_TILE_BYTES:
            return t
    return rows


def _pair_sum(g, got, chip_core, name):
    nch, R, C = g.shape
    h = R // 2
    tr = _row_tile(h, C)
    nt = h // tr

    def body(jc_ref, g_ref, got_ref, p_ref, pb_ref):
        s = g_ref[...] + got_ref[...].astype(F32)
        pb_ref[...] = s.astype(BF16)

        @pl.when(pl.program_id(1) == jc_ref[0])
        def _():
            p_ref[...] = s

    return pl.pallas_call(
        body,
        name=name,
        grid_spec=pltpu.PrefetchScalarGridSpec(
            num_scalar_prefetch=1,
            grid=(nt, nch),
            in_specs=[pl.BlockSpec((None, tr, C), lambda i, j, jc_ref: (j, jc_ref[1] * nt + i, 0)),
                      pl.BlockSpec((None, tr, C), lambda i, j, jc_ref: (j, i, 0))],
            out_specs=[pl.BlockSpec((tr, C), lambda i, j, jc_ref: (i, 0)),
                       pl.BlockSpec((None, tr, C), lambda i, j, jc_ref: (j, i, 0))],
        ),
        out_shape=[_sds((h, C), F32), _sds((nch, h, C), BF16)],
        compiler_params=_params(("parallel", "arbitrary"), 48),
    )(chip_core, *_hbm(g, got))


def _chip_sum(p, got, chip_core, name):
    h, C = p.shape
    tr = _row_tile(h, C)
    nt = h // tr

    def body(jc_ref, p_ref, g0_ref, g1_ref, g2_ref, o_ref):
        o_ref[...] = ((p_ref[...] + g0_ref[...].astype(F32)) + g1_ref[...].astype(F32)) + g2_ref[...].astype(F32)

    rel = lambda r: pl.BlockSpec((None, tr, C), lambda i, jc_ref: (r, i, 0))
    return pl.pallas_call(
        body,
        name=name,
        grid_spec=pltpu.PrefetchScalarGridSpec(
            num_scalar_prefetch=1,
            grid=(nt,),
            in_specs=[pl.BlockSpec((tr, C), lambda i, jc_ref: (i, 0)), rel(0), rel(1), rel(2)],
            out_specs=pl.BlockSpec((tr, C), lambda i, jc_ref: (jc_ref[1] * nt + i, 0)),
        ),
        out_shape=_sds((2 * h, C), F32),
        compiler_params=_params(("parallel",), 48),
    )(chip_core, *_hbm(p, got, got, got))


def _place_shards(shards, chip, name):
    n = len(shards)
    tiles = [_row_tile(s.shape[0], s.shape[1]) for s in shards]
    steps = max(s.shape[0] // t for s, t in zip(shards, tiles))
    tiles = [s.shape[0] // steps for s in shards]

    def body(j_ref, *refs):
        for a in range(n):
            refs[n + a][...] = refs[a][...].astype(BF16)

    return pl.pallas_call(
        body,
        name=name,
        grid_spec=pltpu.PrefetchScalarGridSpec(
            num_scalar_prefetch=1,
            grid=(steps,),
            in_specs=[pl.BlockSpec((t, s.shape[1]), lambda i, j_ref: (i, 0)) for s, t in zip(shards, tiles)],
            out_specs=[pl.BlockSpec((None, t, s.shape[1]), lambda i, j_ref: (j_ref[0], i, 0))
                       for s, t in zip(shards, tiles)],
        ),
        out_shape=[_sds((N_CHIPS,) + s.shape, BF16) for s in shards],
        compiler_params=_params(("parallel",), 48),
    )(chip, *_hbm(*shards))


def _adamw(w, g, m, v, name):
    R, C = w.shape
    tr = _row_tile(R, C)
    c1 = 1.0 - ADAM_B1 ** ADAM_STEP
    c2 = 1.0 - ADAM_B2 ** ADAM_STEP

    def body(w_ref, g_ref, m_ref, v_ref, d_ref, nm_ref, nv_ref):
        g = g_ref[...]
        nm = ADAM_B1 * m_ref[...] + (1.0 - ADAM_B1) * g
        nv = ADAM_B2 * v_ref[...] + (1.0 - ADAM_B2) * (g * g)
        nm_ref[...] = nm
        nv_ref[...] = nv
        d_ref[...] = (-ADAM_LR) * ((nm / c1) / (jnp.sqrt(nv / c2) + ADAM_EPS) + ADAM_WD * w_ref[...])

    spec = pl.BlockSpec((tr, C), lambda i: (i, 0))
    return pl.pallas_call(
        body, name=name, grid=(R // tr,), in_specs=[spec] * 4, out_specs=[spec] * 3,
        out_shape=[_sds((R, C), F32)] * 3, compiler_params=_params(("parallel",), 48),
    )(*_hbm(w, g, m, v))


def _place():
    return lax.axis_index("x"), lax.axis_index("y"), lax.axis_index("c")


def _chip_of(x, y, r):
    return (x ^ (r >> 1), y ^ (r & 1))


ANY = pl.BlockSpec(memory_space=pl.ANY)


def _gather_weights(placed, cw8):
    nbig = len(placed)
    halves = [s.shape[1] // 2 for s in placed]
    pieces = [max(1, h // 64) for h in halves]
    rows = [h // p for h, p in zip(halves, pieces)]
    order = [(a, q) for q in range(max(pieces)) for a in range(nbig) if q < pieces[a]]
    ici_sem = {(a, q, r): 3 * i + (r - 1) for i, (a, q) in enumerate(order) for r in (1, 2, 3)}
    cw_sem = {r: 3 * len(order) + (r - 1) for r in (1, 2, 3)}
    d2d_sem = {key: 3 * len(order) + 3 + k for key, k in ici_sem.items()}
    nsem = 6 * len(order) + 3

    def body(*refs):
        cw_ref, dsts, gcw_ref = refs[nbig], refs[nbig + 1:2 * nbig + 1], refs[2 * nbig + 1]
        send_sems, recv_sems = refs[2 * nbig + 2:]
        x, y, c = _place()
        j = 2 * x + y

        def piece_rows(a, q, core):
            return pl.ds(pl.multiple_of(core * halves[a] + q * rows[a], 16), rows[a])

        def ici(a, q, r):
            tx, ty = _chip_of(x, y, r)
            k = ici_sem[(a, q, r)]
            region = dsts[a].at[j, piece_rows(a, q, c), :]
            return pltpu.make_async_remote_copy(
                src_ref=region, dst_ref=region, send_sem=send_sems.at[k], recv_sem=recv_sems.at[k],
                device_id=(tx, ty, c), device_id_type=MESH)

        def ici_landed(a, q, r):
            tx, ty = _chip_of(x, y, r)
            k = ici_sem[(a, q, r)]
            region = dsts[a].at[2 * tx + ty, piece_rows(a, q, c), :]
            return pltpu.make_async_remote_copy(
                src_ref=region, dst_ref=region, send_sem=send_sems.at[k], recv_sem=recv_sems.at[k],
                device_id=(tx, ty, c), device_id_type=MESH)

        def d2d(a, q, r, core):
            tx, ty = _chip_of(x, y, r)
            k = d2d_sem[(a, q, r)]
            region = dsts[a].at[2 * tx + ty, piece_rows(a, q, core), :]
            return pltpu.make_async_remote_copy(
                src_ref=region, dst_ref=region, send_sem=send_sems.at[k], recv_sem=recv_sems.at[k],
                device_id=(x, y, 1 - c), device_id_type=MESH)

        def cw_copy(r):
            tx, ty = _chip_of(x, y, r)
            k = cw_sem[r]
            return pltpu.make_async_remote_copy(
                src_ref=cw_ref, dst_ref=gcw_ref.at[j], send_sem=send_sems.at[k], recv_sem=recv_sems.at[k],
                device_id=(tx, ty, c), device_id_type=MESH)

        def cw_landed(r):
            tx, ty = _chip_of(x, y, r)
            k = cw_sem[r]
            region = gcw_ref.at[2 * tx + ty]
            return pltpu.make_async_remote_copy(
                src_ref=region, dst_ref=region, send_sem=send_sems.at[k], recv_sem=recv_sems.at[k],
                device_id=(tx, ty, c), device_id_type=MESH)

        def relay(a, q, origin, to):
            ox, oy = _chip_of(x, y, origin)
            tx, ty = _chip_of(x, y, to)
            k = ici_sem[(a, q, 3)]
            region = dsts[a].at[2 * ox + oy, piece_rows(a, q, c), :]
            return pltpu.make_async_remote_copy(
                src_ref=region, dst_ref=region, send_sem=send_sems.at[k], recv_sem=recv_sems.at[k],
                device_id=(tx, ty, c), device_id_type=MESH)

        first = [ici(a, q, r) for (a, q) in order for r in (1, 2)] + [cw_copy(r) for r in (1, 2, 3)]
        for cp in first:
            cp.start()
        passed = []
        for (a, q) in order:
            for r in (1, 2):
                ici_landed(a, q, r).wait_recv()
                if q % 2 == r - 1:
                    cp = relay(a, q, r, 3 - r)
                    cp.start()
                    passed.append(cp)
                cp = d2d(a, q, r, c)
                cp.start()
                passed.append(cp)
        for (a, q) in order:
            ici_landed(a, q, 3).wait_recv()
            cp = d2d(a, q, 3, c)
            cp.start()
            passed.append(cp)
        for r in (1, 2, 3):
            cw_landed(r).wait_recv()
        for (a, q) in order:
            for r in (1, 2, 3):
                d2d(a, q, r, 1 - c).wait_recv()
        for cp in first + passed:
            cp.wait_send()

    return pl.pallas_call(
        body,
        name="gather_weights",
        in_specs=[ANY] * (nbig + 1),
        out_specs=[ANY] * (nbig + 1),
        out_shape=[_sds(s.shape, s.dtype) for s in placed] + [_sds((N_CHIPS,) + cw8.shape, cw8.dtype)],
        input_output_aliases={a: a for a in range(nbig)},
        scratch_shapes=[pltpu.SemaphoreType.DMA((nsem,)), pltpu.SemaphoreType.DMA((nsem,))],
    )(*placed, cw8)


def _gather_late_start(placed, after, name):
    n = len(placed)
    halves = [s.shape[1] // 2 for s in placed]

    def body(*refs):
        g_refs = refs[0:n]
        send_sems, recv_sems, token = refs[n + 1], refs[n + 2], refs[-1]
        x, y, c = _place()
        j = 2 * x + y
        for a in range(n):
            mine = g_refs[a].at[j, pl.ds(pl.multiple_of(c * halves[a], 16), halves[a]), :]
            for r in (1, 2, 3):
                tx, ty = _chip_of(x, y, r)
                for to_core in (0, 1):
                    k = ((a * 3 + (r - 1)) * 2 + c) * 2 + to_core
                    pltpu.make_async_remote_copy(
                        src_ref=mine, dst_ref=mine, send_sem=send_sems.at[k], recv_sem=recv_sems.at[k],
                        device_id=(tx, ty, to_core), device_id_type=MESH).start()
        token[...] = jnp.zeros_like(token)

    hbm = lambda t: pltpu.HBM(t.shape, t.dtype)
    keep = lambda t: pltpu.with_memory_space_constraint(t, pltpu.HBM)
    nsem = 12 * n
    outs = pl.pallas_call(
        body,
        name=name,
        in_specs=[HBM] * n + [ANY],
        out_specs=(SEM, SEM, *[HBM] * n, pl.BlockSpec(memory_space=pltpu.VMEM)),
        out_shape=(pltpu.SemaphoreType.DMA((nsem,)), pltpu.SemaphoreType.DMA((nsem,)), *[hbm(p) for p in placed],
                   jax.ShapeDtypeStruct((8, 128), F32)),
        input_output_aliases={i: 2 + i for i in range(n)},
        compiler_params=pltpu.CompilerParams(has_side_effects=DATAFLOW),
    )(*[keep(p) for p in placed], after)
    return outs[0], outs[1], list(outs[2:2 + n]), outs[-1]


def _gather_late_wait(send_sems, recv_sems, thru, after, name):
    n = len(thru)
    halves = [s.shape[1] // 2 for s in thru]

    def body(*refs):
        g_refs = refs[0:n]
        send_sems, recv_sems = refs[n], refs[n + 1]
        x, y, c = _place()
        j = 2 * x + y
        for a in range(n):
            mine = g_refs[a].at[j, pl.ds(pl.multiple_of(c * halves[a], 16), halves[a]), :]
            for r in (1, 2, 3):
                tx, ty = _chip_of(x, y, r)
                for other in (0, 1):
                    k_out = ((a * 3 + (r - 1)) * 2 + c) * 2 + other
                    pltpu.make_async_remote_copy(
                        src_ref=mine, dst_ref=mine, send_sem=send_sems.at[k_out], recv_sem=recv_sems.at[k_out],
                        device_id=(tx, ty, other), device_id_type=MESH).wait_send()
                    k_in = ((a * 3 + (r - 1)) * 2 + other) * 2 + c
                    theirs = g_refs[a].at[2 * tx + ty, pl.ds(other * halves[a], halves[a]), :]
                    pltpu.make_async_remote_copy(
                        src_ref=theirs, dst_ref=theirs, send_sem=send_sems.at[k_in], recv_sem=recv_sems.at[k_in],
                        device_id=(tx, ty, other), device_id_type=MESH).wait_recv()

    hbm = lambda t: pltpu.HBM(t.shape, t.dtype)
    outs = pl.pallas_call(
        body,
        name=name,
        in_specs=[HBM] * n + [SEM, SEM, ANY],
        out_specs=[HBM] * n,
        out_shape=[hbm(t) for t in thru],
        input_output_aliases={i: i for i in range(n)},
        compiler_params=pltpu.CompilerParams(has_side_effects=DATAFLOW),
    )(*thru, send_sems, recv_sems, after)
    return list(outs)


D2D_PIECE_ROWS = 64


def _pair_exchange(grads, name):
    n = len(grads)
    halves = [g.shape[1] // 2 for g in grads]

    def body(*refs):
        g_refs, got_refs = refs[0:n], refs[n:2 * n]
        send_sems, recv_sems = refs[2 * n:]
        x, y, c = _place()

        def copy(a, src, dst):
            return pltpu.make_async_remote_copy(
                src_ref=src, dst_ref=dst, send_sem=send_sems.at[a], recv_sem=recv_sems.at[a],
                device_id=(x, y, 1 - c), device_id_type=MESH)

        for a in range(n):
            for jj in range(N_CHIPS):
                for q in range(halves[a] // D2D_PIECE_ROWS):
                    src_rows = pl.ds(pl.multiple_of((1 - c) * halves[a] + q * D2D_PIECE_ROWS, 16), D2D_PIECE_ROWS)
                    dst_rows = pl.ds(q * D2D_PIECE_ROWS, D2D_PIECE_ROWS)
                    copy(a, g_refs[a].at[jj, src_rows, :], got_refs[a].at[jj, dst_rows, :]).start()
        for a in range(n):
            sent = g_refs[a].at[:, pl.ds(pl.multiple_of((1 - c) * halves[a], 16), halves[a]), :]
            copy(a, sent, got_refs[a]).wait()

    return pl.pallas_call(
        body,
        name=name,
        in_specs=[ANY] * n,
        out_specs=[ANY] * n,
        out_shape=[_sds((N_CHIPS, h, g.shape[2]), g.dtype) for g, h in zip(grads, halves)],
        scratch_shapes=[pltpu.SemaphoreType.DMA((n,)), pltpu.SemaphoreType.DMA((n,))],
    )(*grads)


HBM = pl.BlockSpec(memory_space=pltpu.HBM)
SEM = pl.BlockSpec(memory_space=pltpu.SEMAPHORE)
DATAFLOW = pltpu.SideEffectType.DATAFLOW_SIDE_EFFECTING


def _chip_copy(p_refs, land_refs, send_sems, recv_sems, a, r, blocked):
    x, y, c = _place()
    tx, ty = _chip_of(x, y, r)
    k = a * 3 + (r - 1)
    return pltpu.make_async_remote_copy(
        src_ref=p_refs[a].at[2 * tx + ty] if blocked else p_refs[a], dst_ref=land_refs[a].at[r - 1],
        send_sem=send_sems.at[k], recv_sem=recv_sems.at[k], device_id=(tx, ty, c), device_id_type=MESH)


def _chip_exchange_start(psums, name, blocked=True):
    n = len(psums)
    lands = [lax.empty((3,) + (p.shape[1:] if blocked else p.shape), p.dtype) for p in psums]

    def body(*refs):
        p_refs, land_refs = refs[0:n], refs[n:2 * n]
        send_sems, recv_sems, token = refs[2 * n], refs[2 * n + 1], refs[-1]
        for a in range(n):
            for r in (1, 2, 3):
                _chip_copy(p_refs, land_refs, send_sems, recv_sems, a, r, blocked).start()
        token[...] = jnp.zeros_like(token)

    hbm = lambda t: pltpu.HBM(t.shape, t.dtype)
    keep = lambda t: pltpu.with_memory_space_constraint(t, pltpu.HBM)
    outs = pl.pallas_call(
        body,
        name=name,
        in_specs=[HBM] * (2 * n),
        out_specs=(SEM, SEM, *[HBM] * (2 * n), pl.BlockSpec(memory_space=pltpu.VMEM)),
        out_shape=(pltpu.SemaphoreType.DMA((3 * n,)), pltpu.SemaphoreType.DMA((3 * n,)),
                   *[hbm(p) for p in psums], *[hbm(l) for l in lands], _sds((8, 128), F32)),
        input_output_aliases={i: 2 + i for i in range(2 * n)},
        compiler_params=pltpu.CompilerParams(has_side_effects=DATAFLOW),
    )(*[keep(p) for p in psums], *[keep(l) for l in lands])
    return outs[0], outs[1], list(outs[2:2 + n]), list(outs[2 + n:2 + 2 * n]), outs[-1]


def _chip_exchange_wait(send_sems, recv_sems, p_thru, land_thru, after, name, blocked=True):
    n = len(p_thru)

    def body(*refs):
        p_refs, land_refs = refs[0:n], refs[n:2 * n]
        send_sems, recv_sems = refs[2 * n], refs[2 * n + 1]
        for a in range(n):
            for r in (1, 2, 3):
                copy = _chip_copy(p_refs, land_refs, send_sems, recv_sems, a, r, blocked)
                copy.wait_send()
                copy.wait_recv()

    hbm = lambda t: pltpu.HBM(t.shape, t.dtype)
    outs = pl.pallas_call(
        body,
        name=name,
        in_specs=[HBM] * (2 * n) + [SEM, SEM, ANY],
        out_specs=[HBM] * (2 * n),
        out_shape=[hbm(p) for p in p_thru] + [hbm(l) for l in land_thru],
        input_output_aliases={i: i for i in range(2 * n)},
        compiler_params=pltpu.CompilerParams(has_side_effects=DATAFLOW),
    )(*p_thru, *land_thru, send_sems, recv_sems, after)
    return list(outs[n:2 * n])


def _pair_share(fulls):
    n = len(fulls)
    halves = [f.shape[0] // 2 for f in fulls]

    def body(*refs):
        full_refs = refs[n:2 * n]
        send_sems, recv_sems = refs[2 * n:]
        x, y, c = _place()

        def half_of(a, core):
            return full_refs[a].at[pl.ds(pl.multiple_of(core * halves[a], 8), halves[a]), :]

        def remote(a, src, dst):
            return pltpu.make_async_remote_copy(
                src_ref=src, dst_ref=dst, send_sem=send_sems.at[a], recv_sem=recv_sems.at[a],
                device_id=(x, y, 1 - c), device_id_type=MESH)

        for a in range(n):
            for q in range(halves[a] // D2D_PIECE_ROWS):
                piece = full_refs[a].at[
                    pl.ds(pl.multiple_of(c * halves[a] + q * D2D_PIECE_ROWS, 8), D2D_PIECE_ROWS), :]
                remote(a, piece, piece).start()
        for a in range(n):
            remote(a, half_of(a, c), half_of(a, c)).wait_send()
            remote(a, half_of(a, 1 - c), half_of(a, 1 - c)).wait_recv()

    return pl.pallas_call(
        body,
        name="pair_share",
        in_specs=[ANY] * n,
        out_specs=[ANY] * n,
        out_shape=[_sds(f.shape, F32) for f in fulls],
        input_output_aliases={a: a for a in range(n)},
        scratch_shapes=[pltpu.SemaphoreType.DMA((n,)), pltpu.SemaphoreType.DMA((n,))],
    )(*fulls)


def _small_pair_sum(s):
    R, C = s.shape
    V = SMALL_VECTOR_ROWS

    def body(s_ref, v_ref, m_ref, sib, send_sem, recv_sem):
        x, y, c = _place()

        def to_sib(src, dst):
            return pltpu.make_async_remote_copy(
                src_ref=src, dst_ref=dst, send_sem=send_sem, recv_sem=recv_sem,
                device_id=(x, y, 1 - c), device_id_type=MESH)

        for q in range(R // 8):
            to_sib(s_ref.at[pl.ds(8 * q, 8), :], sib.at[pl.ds(8 * q, 8), :]).start()
        to_sib(s_ref, sib).wait()
        v_ref[...] = s_ref[pl.ds(0, V), :] + sib[pl.ds(0, V), :]
        m_ref[...] = (s_ref[pl.ds(V, R - V), :] + sib[pl.ds(V, R - V), :]).astype(BF16)

    return pl.pallas_call(
        body,
        name="small_pair_sum",
        in_specs=[pl.BlockSpec(memory_space=pltpu.VMEM)],
        out_specs=[pl.BlockSpec(memory_space=pltpu.VMEM)] * 2,
        out_shape=[jax.ShapeDtypeStruct((V, C), F32), jax.ShapeDtypeStruct((R - V, C), BF16)],
        scratch_shapes=[pltpu.VMEM((R, C), F32), pltpu.SemaphoreType.DMA, pltpu.SemaphoreType.DMA],
    )(s)


def _small_total(chip, own, landed):
    V, C = own[0].shape
    M = own[1].shape[0]

    def body(j_ref, v_ref, m_ref, lv_ref, lm_ref, o_ref, chips_v, chips_m):
        j = j_ref[0]
        chips_v[j] = v_ref[...]
        chips_m[j] = m_ref[...]
        for r in (1, 2, 3):
            chips_v[j ^ r] = lv_ref[r - 1]
            chips_m[j ^ r] = lm_ref[r - 1]
        o_ref[pl.ds(0, V), :] = (chips_v[0] + chips_v[1]) + (chips_v[2] + chips_v[3])
        o_ref[pl.ds(V, M), :] = (chips_m[0].astype(F32) + chips_m[1].astype(F32)) + (
            chips_m[2].astype(F32) + chips_m[3].astype(F32))

    vmem = pl.BlockSpec(memory_space=pltpu.VMEM)
    return pl.pallas_call(
        body,
        name="small_total",
        in_specs=[pl.BlockSpec(memory_space=pltpu.SMEM), vmem, vmem, vmem, vmem],
        out_specs=vmem,
        out_shape=jax.ShapeDtypeStruct((V + M, C), F32),
        scratch_shapes=[pltpu.VMEM((N_CHIPS, V, C), F32), pltpu.VMEM((N_CHIPS, M, C), BF16)],
    )(chip, own[0], own[1], landed[0], landed[1])


def _block_diag(w):
    w4 = w.reshape(4, 4, RNN_BLOCK_W, RNN_BLOCK_W)
    eye = jnp.eye(4, dtype=w.dtype)
    return jnp.einsum("jaik,ab->jaibk", w4, eye).reshape(4, RNN_TILE, RNN_TILE)


def _block_diag_part(d):
    d5 = d.reshape(4, 4, RNN_BLOCK_W, 4, RNN_BLOCK_W)
    return jnp.stack([d5[:, a, :, a, :] for a in range(4)], axis=1).reshape(RNN_BLOCKS, RNN_BLOCK_W, RNN_BLOCK_W)


def _local_grads(x, target, g_pre, w_in_g, b_gate, conv_w, conv_b, w_rg_a, b_rg_a, w_rg_x, b_rg_x, lam, sinks,
                 out_weights, fwd_token, g_post, on_out_grads, on_w_in_grad):
    wa_bd = _block_diag(w_rg_a).astype(BF16)
    wx_bd = _block_diag(w_rg_x).astype(BF16)
    b_a = b_rg_a.reshape(1, D_RNN)
    b_x = b_rg_x.reshape(1, D_RNN)

    proj, ht = _proj_fwd(x, g_pre, w_in_g)
    y_rnn, z_rnn = _rnn_fwd(proj, conv_w, conv_b, wa_bd, wx_bd, b_a, b_x, lam, fwd_token)
    bias = _attn_bias()
    y_attn, z_attn, lse = _attn_fwd(proj, sinks, bias)
    w_rnn_out, w_attn_out, w_out = out_weights(z_attn)
    dyx, dz_rnn, dz_attn, dml, merged, dout, dbr_rnn, dbr_attn, head_small = _head(
        x, target, z_rnn, z_attn, proj, b_gate, g_post, w_rnn_out, w_attn_out, w_out)
    out_grads = [_matmul_tn(z_rnn, dbr_rnn, "dw_rnn_out"), _matmul_tn(z_attn, dbr_attn, "dw_attn_out"),
                 _matmul_tn(merged, dout, "dw_out")]
    shard_rows = lambda d: d.reshape(N_CHIPS, OUT_SHARD, D_MODEL)
    token = on_out_grads([shard_rows(g) for g, _ in out_grads], [shard_rows(gb) for _, gb in out_grads])
    dq, dk, dv, dag, attn_small = _attn_bwd(proj, y_attn, lse, dz_attn, sinks, bias, token)
    drx, drg, dwa_t, dwx_t, rnn_small = _rnn_bwd(proj, y_rnn, dz_rnn, conv_w, conv_b, wa_bd, wx_bd, b_a, b_x, lam)
    dproj = [drx, drg, dq, dk, dv, dag, dml]
    token = on_w_in_grad(*_dw_in(ht, dproj))
    grad_x, dh_small = _dh_bwd(dproj, w_in_g, x, dyx, g_pre, token)
    small = jnp.concatenate([rnn_small, head_small, dh_small + attn_small,
                             _block_diag_part(dwa_t).reshape(64, 1024), _block_diag_part(dwx_t).reshape(64, 1024)], axis=0)
    return grad_x, small


ROW_LOSS = 11


def _rows8(parts):
    out = None
    for r, a in parts:
        p = jnp.pad(a, ((r, 8 - r - a.shape[0]), (0, 1024 - a.shape[1])))
        out = p if out is None else out + p
    return out


def _pack_small(p):
    g0 = _rows8([(0, p["b_rg_a"].reshape(1, 1024)), (1, p["b_rg_x"].reshape(1, 1024)), (2, p["lru_lambda"]),
                 (3, p["conv_b"]), (4, p["conv_w"][0])])
    g1 = _rows8([(0, p["post_norm_g"]), (1, p["b_gate"].reshape(2, 1024))])
    g2 = _rows8([(0, p["pre_norm_g"]), (1, p["attn_sinks"])])
    return jnp.concatenate([g0, g1, g2, p["w_rg_a"].reshape(64, 1024), p["w_rg_x"].reshape(64, 1024)], axis=0)


def _unpack_small(s, conv_cols):
    return {
        "b_rg_a": s[0:1].reshape(1, 16, 64), "b_rg_x": s[1:2].reshape(1, 16, 64), "lru_lambda": s[2:3],
        "conv_b": s[3:4], "conv_w": s[4:8, 0:conv_cols].reshape(1, CONV_W, conv_cols),
        "post_norm_g": s[8:9], "b_gate": s[9:11].reshape(1, 2048),
        "pre_norm_g": s[16:17], "attn_sinks": s[17:18, 0:N_Q_HEADS],
        "w_rg_a": s[24:88].reshape(1, 16, 64, 64), "w_rg_x": s[88:152].reshape(1, 16, 64, 64),
    }


WEIGHTS = ["pre_norm_g", "w_in", "b_gate", "conv_w", "conv_b", "w_rg_a", "b_rg_a", "w_rg_x", "b_rg_x", "lru_lambda",
           "attn_sinks", "w_rnn_out", "w_attn_out", "w_out", "post_norm_g"]
BIG = ["w_in", "w_rnn_out", "w_attn_out", "w_out"]


def kernel(x, pre_norm_g, w_in, b_gate, conv_w, conv_b, w_rg_a, b_rg_a, w_rg_x, b_rg_x, lru_lambda, attn_sinks, w_rnn_out, w_attn_out, w_out, post_norm_g, loss_target, m_pre_norm_g, m_w_in, m_b_gate, m_conv_w, m_conv_b, m_w_rg_a, m_b_rg_a, m_w_rg_x, m_b_rg_x, m_lru_lambda, m_attn_sinks, m_w_rnn_out, m_w_attn_out, m_w_out, m_post_norm_g, v_pre_norm_g, v_w_in, v_b_gate, v_conv_w, v_conv_b, v_w_rg_a, v_b_rg_a, v_w_rg_x, v_b_rg_x, v_lru_lambda, v_attn_sinks, v_w_rnn_out, v_w_attn_out, v_w_out, v_post_norm_g):
    w = dict(pre_norm_g=pre_norm_g, w_in=w_in, b_gate=b_gate, conv_w=conv_w, conv_b=conv_b, w_rg_a=w_rg_a,
             b_rg_a=b_rg_a, w_rg_x=w_rg_x, b_rg_x=b_rg_x, lru_lambda=lru_lambda, attn_sinks=attn_sinks,
             w_rnn_out=w_rnn_out, w_attn_out=w_attn_out, w_out=w_out, post_norm_g=post_norm_g)
    m = dict(pre_norm_g=m_pre_norm_g, w_in=m_w_in, b_gate=m_b_gate, conv_w=m_conv_w, conv_b=m_conv_b, w_rg_a=m_w_rg_a,
             b_rg_a=m_b_rg_a, w_rg_x=m_w_rg_x, b_rg_x=m_b_rg_x, lru_lambda=m_lru_lambda, attn_sinks=m_attn_sinks,
             w_rnn_out=m_w_rnn_out, w_attn_out=m_w_attn_out, w_out=m_w_out, post_norm_g=m_post_norm_g)
    v = dict(pre_norm_g=v_pre_norm_g, w_in=v_w_in, b_gate=v_b_gate, conv_w=v_conv_w, conv_b=v_conv_b, w_rg_a=v_w_rg_a,
             b_rg_a=v_b_rg_a, w_rg_x=v_w_rg_x, b_rg_x=v_b_rg_x, lru_lambda=v_lru_lambda, attn_sinks=v_attn_sinks,
             w_rnn_out=v_w_rnn_out, w_attn_out=v_w_attn_out, w_out=v_w_out, post_norm_g=v_post_norm_g)
    chip = 2 * lax.axis_index("x") + lax.axis_index("y")

    chip_idx = chip.astype(jnp.int32).reshape(1)
    chip_core = jnp.stack([chip, lax.axis_index("c")]).astype(jnp.int32)
    cw8 = jnp.pad(conv_w[0], ((0, 8 - CONV_W), (0, 0)))
    placed = _place_shards([w_in[0], w_rnn_out[0], w_attn_out[0], w_out[0]], chip_idx, "place_shards")
    win_g, cw_g = _gather_weights(placed[:1], cw8)
    late_send, late_recv, late_thru, late_token = _gather_late_start(placed[1:], win_g, "gather_late_start")
    cw_g = lax.dynamic_update_slice_in_dim(cw_g, cw8[None], chip, axis=0)
    conv_w_full = jnp.transpose(cw_g[:, 0:CONV_W, :], (1, 0, 2)).reshape(CONV_W, D_RNN)

    started = {}

    def start_reduction(tag, grads, grads_b16):
        got = _pair_exchange(grads_b16, "pair_exchange_" + tag)
        sums = [_pair_sum(g, o, chip_core, "pair_sum_%s_%d" % (tag, a)) for a, (g, o) in enumerate(zip(grads, got))]
        send_sems, recv_sems, p_thru, land_thru, token = _chip_exchange_start(
            [pb for _, pb in sums], "chip_exchange_start_" + tag)
        started[tag] = ([p for p, _ in sums], send_sems, recv_sems, p_thru, land_thru)
        return token

    def end_reduction(tag, after):
        psums, send_sems, recv_sems, p_thru, land_thru = started[tag]
        landed = _chip_exchange_wait(send_sems, recv_sems, p_thru, land_thru, after, "chip_exchange_wait_" + tag)
        return [_chip_sum(p, l, chip_core, "chip_sum_%s_%d" % (tag, a)) for a, (p, l) in enumerate(zip(psums, landed))]

    def out_weights(after):
        gathered = _gather_late_wait(late_send, late_recv, late_thru, after, "gather_late_wait")
        return [g.reshape(D_MODEL, D_MODEL) for g in gathered]

    grad_x, small = _local_grads(
        x[0], loss_target[0], pre_norm_g, win_g, b_gate, conv_w_full, conv_b, w_rg_a[0], b_rg_a[0], w_rg_x[0],
        b_rg_x[0], lru_lambda, attn_sinks[0], out_weights, late_token, post_norm_g,
        on_out_grads=lambda grads, grads_b16: start_reduction("out", grads, grads_b16),
        on_w_in_grad=lambda grad, grad_b16: start_reduction("in", [grad], [grad_b16]))

    small_chip = _small_pair_sum(small)
    small_send, small_recv, small_thru, small_land, small_token = _chip_exchange_start(
        list(small_chip), "small_exchange_start", blocked=False)

    halves = end_reduction("in", small_token) + end_reduction("out", small_token)
    gbig = dict(zip(BIG, _pair_share(halves)))

    grads, delta, new_m, new_v = {}, {}, {}, {}
    for n in BIG:
        grads[n] = gbig[n][None]
        d, nm, nv = _adamw(w[n][0], gbig[n], m[n][0], v[n][0], "adamw_" + n)
        delta[n], new_m[n], new_v[n] = d[None], nm[None], nv[None]

    small_landed = _chip_exchange_wait(small_send, small_recv, small_thru, small_land, delta[BIG[-1]],
                                       "small_exchange_wait", blocked=False)
    small_sum = _small_total(chip_idx, small_thru, small_landed)
    total_loss = small_sum[ROW_LOSS, 0]
    gsmall = _unpack_small(small_sum, D_RNN)
    conv_shard = D_RNN // N_CHIPS
    gsmall["conv_w"] = lax.dynamic_slice_in_dim(gsmall["conv_w"], chip * conv_shard, conv_shard, axis=2)
    pick = lambda t: {k: t[k] for k in gsmall}
    d, nm, nv = _adamw(_pack_small(pick(w)), _pack_small(gsmall), _pack_small(pick(m)), _pack_small(pick(v)),
                       "adamw_small")
    ud, um, uv = _unpack_small(d, conv_shard), _unpack_small(nm, conv_shard), _unpack_small(nv, conv_shard)
    for n in gsmall:
        grads[n] = gsmall[n].reshape(w[n].shape)
        delta[n] = ud[n].reshape(w[n].shape)
        new_m[n] = um[n].reshape(w[n].shape)
        new_v[n] = uv[n].reshape(w[n].shape)

    return (total_loss, grad_x[None], *[grads[n] for n in WEIGHTS], *[delta[n] for n in WEIGHTS],
            *[new_m[n] for n in WEIGHTS], *[new_v[n] for n in WEIGHTS])
```

```python
import functools
import math

import jax
import jax.numpy as jnp
from jax import lax
from jax.experimental import pallas as pl
from jax.experimental.pallas import tpu as pltpu

F32 = jnp.float32
BF16 = jnp.bfloat16

D_MODEL = 1024
D_RNN = 1024
RNN_BLOCKS = 16
RNN_BLOCK_W = 64
CONV_W = 4
LRU_C = 8.0
N_Q_HEADS = 16
N_KV_HEADS = 4
GROUP = 4
HEAD_DIM = 64
D_KV = 256
BLOCK = 128
ALIBI_MAX_BIAS = 8.0
EPS = 1e-6
D_IN = 6656
N_CHIPS = 4
W_IN_SHARD = D_IN // N_CHIPS
OUT_SHARD = D_MODEL // N_CHIPS
ADAM_LR = 0.001
ADAM_B1 = 0.9
ADAM_B2 = 0.999
ADAM_EPS = 1e-08
ADAM_WD = 0.01
ADAM_STEP = 10
NEG_BIG = -1e30
MIB = 1 << 20

COL_RNN_X = 0
COL_RNN_GATE = 4
COL_Q = 8
COL_K = 12
COL_V = 13
COL_ATTN_GATE = 14
COL_MERGE = 18

RNN_TILE = 256
RNN_CHUNK = 512
SMALL_ROWS = 152
SMALL_VECTOR_ROWS = 24
MESH = pl.DeviceIdType.MESH


def _sds(shape, dtype):
    return pltpu.HBM(shape, dtype)


def _params(sem=None, vmem_mib=None):
    kw = {}
    if sem is not None:
        kw["dimension_semantics"] = sem
    if vmem_mib is not None:
        kw["vmem_limit_bytes"] = vmem_mib * MIB
    return pltpu.CompilerParams(**kw)


def _hbm(*arrays):
    return [pltpu.with_memory_space_constraint(a, pltpu.HBM) for a in arrays]


def _dot(a, b):
    return jnp.dot(a, b, preferred_element_type=F32)


def _dot_nt(a, b):
    return lax.dot_general(a, b, (((1,), (1,)), ((), ())), preferred_element_type=F32)


def _dot_tn(a, b):
    return lax.dot_general(a, b, (((0,), (0,)), ((), ())), preferred_element_type=F32)


def _sigmoid(x):
    return 0.5 * jnp.tanh(0.5 * x) + 0.5


def _sigmoid_small(x):
    return 1.0 / (1.0 + jnp.exp(-x))


def _softplus(x):
    return jnp.maximum(x, 0.0) + jnp.log(1.0 + jnp.exp(-jnp.abs(x)))


def _one_minus_square(a, log_a):
    return -jnp.tanh(log_a) * (a * a + 1.0)


def _proj_fwd(x, g_pre, w_in_g):
    T = x.shape[0]
    tm = min(1024, T)

    def body(x_ref, g_ref, w_ref, proj_ref, ht_ref, h_s):
        @pl.when(pl.program_id(1) == 0)
        def _():
            xv = x_ref[...]
            rstd = lax.rsqrt(jnp.mean(xv * xv, axis=-1, keepdims=True) + EPS)
            hf = (xv * rstd) * g_ref[...]
            h_s[...] = hf.astype(BF16)
            ht_ref[...] = hf.T.astype(BF16)

        proj_ref[...] = _dot(h_s[...], w_ref[...]).astype(BF16)

    return pl.pallas_call(
        body,
        name="proj_fwd",
        grid=(T // tm, N_CHIPS),
        in_specs=[
            pl.BlockSpec((tm, D_MODEL), lambda i, j: (i, 0)),
            pl.BlockSpec((1, D_MODEL), lambda i, j: (0, 0)),
            pl.BlockSpec((None, D_MODEL, W_IN_SHARD), lambda i, j: (j, 0, 0)),
        ],
        out_specs=[
            pl.BlockSpec((tm, W_IN_SHARD), lambda i, j: (i, j)),
            pl.BlockSpec((D_MODEL, tm), lambda i, j: (0, i)),
        ],
        out_shape=[_sds((T, D_IN), BF16), _sds((D_MODEL, T), BF16)],
        scratch_shapes=[pltpu.VMEM((tm, D_MODEL), BF16)],
        compiler_params=_params(("parallel", "arbitrary"), 48),
    )(*_hbm(x, g_pre, w_in_g))


def _shift_down(x, tail, s, row):
    n = x.shape[0]
    xs = pltpu.roll(x, s, 0)
    tail_t = jnp.tile(pltpu.roll(tail, s, 0), (n // 8, 1))
    return jnp.where(row < s, tail_t, xs)


def _shift_up(x, head, s, row):
    n = x.shape[0]
    xs = pltpu.roll(x, n - s, 0)
    head_t = jnp.tile(pltpu.roll(head, 8 - s, 0), (n // 8, 1))
    return jnp.where(row >= n - s, head_t, xs)


def _conv_taps(x, tail, row):
    return [_shift_down(x, tail, 3, row), _shift_down(x, tail, 2, row), _shift_down(x, tail, 1, row), x]


def _rglru_gates(c, wa, wx, ba, bx, lam):
    cb = c.astype(BF16)
    r = _sigmoid_small(_dot(cb, wa) + ba)
    i = _sigmoid(_dot(cb, wx) + bx)
    log_a = (-LRU_C) * r * _softplus(-lam)
    a = jnp.exp(log_a)
    w = _one_minus_square(a, log_a)
    inv_mult = lax.rsqrt(w)
    return cb, r, i, a, w * inv_mult, inv_mult


SUBLANES = 8


def _scan_down(a, u, row):
    n = a.shape[0]
    s = 1
    while s < SUBLANES:
        a_sh = jnp.where(row >= s, pltpu.roll(a, s, 0), 1.0)
        u_sh = jnp.where(row >= s, pltpu.roll(u, s, 0), 0.0)
        u = a * u_sh + u
        a = a * a_sh
        s *= 2
    while s < n:
        u = jnp.concatenate([u[:s], a[s:] * u[:n - s] + u[s:]], axis=0)
        a = jnp.concatenate([a[:s], a[s:] * a[:n - s]], axis=0)
        s *= 2
    return a, u


def _scan_up(b, u, row):
    n = b.shape[0]
    s = 1
    while s < SUBLANES:
        b_sh = jnp.where(row < n - s, pltpu.roll(b, n - s, 0), 1.0)
        u_sh = jnp.where(row < n - s, pltpu.roll(u, n - s, 0), 0.0)
        u = b * u_sh + u
        b = b * b_sh
        s *= 2
    while s < n:
        u = jnp.concatenate([b[:n - s] * u[s:] + u[:n - s], u[n - s:]], axis=0)
        b = jnp.concatenate([b[:n - s] * b[s:], b[n - s:]], axis=0)
        s *= 2
    return b, u


LANES = 128


def _chunk_scan(a, u, a_s, u_s, hl_s, al_s, carry, reverse):
    n, width = a.shape
    groups = n // SUBLANES
    order = range(SUBLANES - 1, -1, -1) if reverse else range(SUBLANES)
    row = lax.broadcasted_iota(jnp.int32, (groups, LANES), 0)
    for l in range(width // LANES):
        lanes = slice(l * LANES, (l + 1) * LANES)
        a_l, u_l, hl_l, al_l = a_s.at[l], u_s.at[l], hl_s.at[l], al_s.at[l]
        a_l[...] = a[:, lanes]
        u_l[...] = u[:, lanes]
        h_loc = a_loc = None
        for r in order:
            rows = pl.ds(r, groups, stride=SUBLANES)
            a_r, u_r = a_l[rows, :], u_l[rows, :]
            h_loc, a_loc = (u_r, a_r) if h_loc is None else (a_r * h_loc + u_r, a_r * a_loc)
            hl_l[rows, :] = h_loc
            al_l[rows, :] = a_loc
        if reverse:
            a_cum, ends = _scan_up(a_loc, h_loc, row)
            ends = ends + a_cum * carry[:, lanes]
            enters = jnp.where(row == groups - 1, carry[:, lanes], pltpu.roll(ends, groups - 1, 0))
        else:
            a_cum, ends = _scan_down(a_loc, h_loc, row)
            ends = ends + a_cum * carry[:, lanes]
            enters = jnp.where(row == 0, carry[:, lanes], pltpu.roll(ends, 1, 0))
        for r in range(SUBLANES):
            rows = pl.ds(r, groups, stride=SUBLANES)
            hl_l[rows, :] = hl_l[rows, :] + al_l[rows, :] * enters
    return jnp.concatenate([hl_s[l] for l in range(width // LANES)], axis=1)


def _rnn_fwd(proj, conv_w, conv_b, wa_bd, wx_bd, b_a, b_x, lam, token):
    T = proj.shape[0]
    tc, ct = RNN_CHUNK, RNN_TILE
    nt = T // tc

    def body(x_ref, rg_ref, cw_ref, cb_ref, wa_ref, wx_ref, ba_ref, bx_ref, lam_ref, token_ref, h_ref, z_ref, c_ref,
             xtail, hcarry, a_s, u_s, hl_s, al_s):
        @pl.when(pl.program_id(1) == 0)
        def _():
            xtail[...] = jnp.zeros_like(xtail)
            hcarry[...] = jnp.zeros_like(hcarry)

        row = lax.broadcasted_iota(jnp.int32, (tc, ct), 0)
        x = x_ref[...].astype(F32)
        taps = _conv_taps(x, xtail[...], row)
        c = cb_ref[...] + cw_ref[pl.ds(0, 1), :] * taps[0]
        for k in range(1, CONV_W):
            c = c + cw_ref[pl.ds(k, 1), :] * taps[k]
        xtail[...] = x[tc - 8:, :]
        c_ref[...] = c
        _, _, i, a, mult, _ = _rglru_gates(c, wa_ref[...], wx_ref[...], ba_ref[...], bx_ref[...], lam_ref[...])
        h = _chunk_scan(a, mult * (i * c), a_s, u_s, hl_s, al_s, hcarry[...], reverse=False)
        h_ref[...] = h
        hcarry[...] = h_ref[pl.ds(tc - 1, 1), :]
        rg = rg_ref[...].astype(F32)
        z_ref[...] = (h * (rg * _sigmoid(rg))).astype(BF16)

    col = lambda off: (lambda j, t: (t, off + j))
    vec = pl.BlockSpec((1, ct), lambda j, t: (0, j))
    mat = pl.BlockSpec((None, ct, ct), lambda j, t: (j, 0, 0))
    return pl.pallas_call(
        body,
        name="rnn_fwd",
        grid=(D_RNN // ct, nt),
        in_specs=[
            pl.BlockSpec((tc, ct), col(COL_RNN_X)),
            pl.BlockSpec((tc, ct), col(COL_RNN_GATE)),
            pl.BlockSpec((CONV_W, ct), lambda j, t: (0, j)),
            vec, mat, mat, vec, vec, vec,
            pl.BlockSpec((8, 128), lambda j, t: (0, 0)),
        ],
        out_specs=[pl.BlockSpec((tc, ct), lambda j, t: (t, j))] * 3,
        out_shape=[_sds((T, D_RNN), F32), _sds((T, D_RNN), BF16), _sds((T, D_RNN), F32)],
        scratch_shapes=[pltpu.VMEM((8, ct), F32), pltpu.VMEM((1, ct), F32)] + [
            pltpu.VMEM((ct // LANES, tc, LANES), F32)] * 4,
        compiler_params=_params(("parallel", "arbitrary"), 32),
    )(*_hbm(proj, proj, conv_w, conv_b, wa_bd, wx_bd, b_a, b_x, lam, token))


def _rnn_bwd(proj, conv, y_rnn, dz_rnn, conv_w, wa_bd, wx_bd, b_a, b_x, lam):
    T = proj.shape[0]
    tc, ct = RNN_CHUNK, RNN_TILE
    nt = T // tc
    hb = tc // 8

    def body(x_ref, c_ref, rg_ref, h_ref, hh_ref, dz_ref, cw_ref, wa_ref, wx_ref, ba_ref, bx_ref, lam_ref,
             dx_ref, drg_ref, dwa_ref, dwx_ref, sm_ref, lam_carry, a_carry, dc_head, b_s, dy_s, hl_s, al_s):
        t = pl.program_id(1)
        first_chunk = t == nt - 1

        @pl.when(t == 0)
        def _():
            lam_carry[...] = jnp.zeros_like(lam_carry)
            a_carry[...] = jnp.zeros_like(a_carry)
            dc_head[...] = jnp.zeros_like(dc_head)
            dwa_ref[...] = jnp.zeros_like(dwa_ref)
            dwx_ref[...] = jnp.zeros_like(dwx_ref)
            sm_ref[...] = jnp.zeros_like(sm_ref)

        row = lax.broadcasted_iota(jnp.int32, (tc, ct), 0)
        keep = jnp.where(first_chunk, 0.0, 1.0)
        x = x_ref[...].astype(F32)
        c = c_ref[...]
        lam = lam_ref[...]
        cb, r, i, a, mult, inv_mult = _rglru_gates(c, wa_ref[...], wx_ref[...], ba_ref[...], bx_ref[...], lam)
        h = h_ref[...]
        h_prev = _shift_down(h, hh_ref[...] * keep, 1, row)
        rg = rg_ref[...].astype(F32)
        dz = dz_ref[...]
        sg = _sigmoid(rg)
        drg_ref[...] = (dz * h * (sg * (1.0 + rg * (1.0 - sg)))).astype(BF16)
        dy = dz * (rg * sg)
        b = jnp.where(row >= tc - 1, a_carry[pl.ds(0, 1), :], pltpu.roll(a, tc - 1, 0))
        lt = _chunk_scan(b, dy, b_s, dy_s, hl_s, al_s, lam_carry[pl.ds(0, 1), :], reverse=True)
        lam_carry[...] = lt[0:8, :]
        a_carry[...] = a[0:8, :]
        ic = i * c
        dmult = lt * ic
        di = lt * mult * c
        dc = lt * mult * i
        dlog_a = a * (lt * h_prev - dmult * a * inv_mult)
        sp = _softplus(-lam)
        dpre_r = dlog_a * ((-LRU_C) * sp) * (r * (1.0 - r))
        dpre_i = di * (i * (1.0 - i))
        dlam_row = jnp.sum(dlog_a * r, axis=0, keepdims=True) * (LRU_C * _sigmoid(-lam))
        dpr_b = dpre_r.astype(BF16)
        dpi_b = dpre_i.astype(BF16)
        dwa_ref[...] += _dot_tn(cb, dpr_b)
        dwx_ref[...] += _dot_tn(cb, dpi_b)
        dc = dc + _dot_nt(dpr_b, wa_ref[...]) + _dot_nt(dpi_b, wx_ref[...])
        head = dc_head[...]
        dx = cw_ref[pl.ds(3, 1), :] * dc
        sm_ref[pl.ds(4 + 3, 1), :] += jnp.sum(dc * x, axis=0, keepdims=True)
        for m in range(1, CONV_W):
            up = _shift_up(dc, head, m, row)
            dx = dx + cw_ref[pl.ds(3 - m, 1), :] * up
            sm_ref[pl.ds(4 + 3 - m, 1), :] += jnp.sum(up * x, axis=0, keepdims=True)
        dx_ref[...] = dx.astype(BF16)
        dc_head[...] = dc[0:8, :]
        sm_ref[pl.ds(0, 1), :] += jnp.sum(dpre_r, axis=0, keepdims=True)
        sm_ref[pl.ds(1, 1), :] += jnp.sum(dpre_i, axis=0, keepdims=True)
        sm_ref[pl.ds(2, 1), :] += dlam_row
        sm_ref[pl.ds(3, 1), :] += jnp.sum(dc, axis=0, keepdims=True)

    rev = lambda off: (lambda j, t: (nt - 1 - t, off + j))
    halo = lambda off: (lambda j, t: (jnp.maximum((nt - 1 - t) * hb - 1, 0), off + j))
    vec = pl.BlockSpec((1, ct), lambda j, t: (0, j))
    mat = pl.BlockSpec((None, ct, ct), lambda j, t: (j, 0, 0))
    return pl.pallas_call(
        body,
        name="rnn_bwd",
        grid=(D_RNN // ct, nt),
        in_specs=[
            pl.BlockSpec((tc, ct), rev(COL_RNN_X)),
            pl.BlockSpec((tc, ct), rev(0)),
            pl.BlockSpec((tc, ct), rev(COL_RNN_GATE)),
            pl.BlockSpec((tc, ct), rev(0)),
            pl.BlockSpec((8, ct), halo(0)),
            pl.BlockSpec((tc, ct), rev(0)),
            pl.BlockSpec((CONV_W, ct), lambda j, t: (0, j)),
            mat, mat, vec, vec, vec,
        ],
        out_specs=[
            pl.BlockSpec((tc, ct), rev(0)),
            pl.BlockSpec((tc, ct), rev(0)),
            mat, mat,
            pl.BlockSpec((8, ct), lambda j, t: (0, j)),
        ],
        out_shape=[_sds((T, D_RNN), BF16), _sds((T, D_RNN), BF16), _sds((D_RNN // ct, ct, ct), F32),
                   _sds((D_RNN // ct, ct, ct), F32), _sds((8, D_RNN), F32)],
        scratch_shapes=[pltpu.VMEM((8, ct), F32)] * 3 + [pltpu.VMEM((ct // LANES, tc, LANES), F32)] * 4,
        compiler_params=_params(("parallel", "arbitrary"), 32),
    )(*_hbm(proj, conv, proj, y_rnn, y_rnn, dz_rnn, conv_w, wa_bd, wx_bd, b_a, b_x, lam))


def _attn_bias():
    qi = jnp.arange(BLOCK)[:, None]
    kj = jnp.arange(BLOCK)[None, :]
    dist_cur = (qi - kj).astype(F32)
    slopes = 2.0 ** (-ALIBI_MAX_BIAS * jnp.arange(1, N_Q_HEADS + 1, dtype=F32) / N_Q_HEADS)
    slopes = slopes[:, None, None]
    prev = jnp.where(kj > qi, -slopes * (dist_cur + float(BLOCK)), NEG_BIG)
    cur = jnp.where(kj <= qi, -slopes * dist_cur, NEG_BIG)
    later = jnp.concatenate([prev, cur], axis=-1)
    first = jnp.concatenate([jnp.full_like(prev, NEG_BIG), cur], axis=-1)
    return jnp.stack([first, later])


def _attn_exps(s_prev, s_cur, sink, bias):
    s_prev = s_prev + bias[:, 0:BLOCK]
    s_cur = s_cur + bias[:, BLOCK:2 * BLOCK]
    m = jnp.maximum(jnp.max(jnp.maximum(s_prev, s_cur), axis=-1, keepdims=True), sink)
    p_prev = jnp.exp(s_prev - m)
    p_cur = jnp.exp(s_cur - m)
    total = jnp.sum(p_prev + p_cur, axis=-1, keepdims=True) + jnp.exp(sink - m)
    return p_prev, p_cur, 1.0 / total, m + jnp.log(total)


def _attn_probs(s_prev, s_cur, sink, bias, lse):
    p_prev = jnp.exp((s_prev + bias[:, 0:BLOCK]) - lse)
    p_cur = jnp.exp((s_cur + bias[:, BLOCK:2 * BLOCK]) - lse)
    return p_prev, p_cur, jnp.exp(sink - lse)


def _stack_heads(ref_or_val, hk, dtype):
    parts = [ref_or_val[:, (GROUP * hk + g) * HEAD_DIM:(GROUP * hk + g + 1) * HEAD_DIM] for g in range(GROUP)]
    return jnp.concatenate(parts, axis=0).astype(dtype)


ATTN_SCALE = HEAD_DIM ** -0.5


def _bias_spec():
    return pl.BlockSpec((None, N_Q_HEADS, BLOCK, 2 * BLOCK), lambda i: (jnp.minimum(i, 1), 0, 0, 0))


def _attn_fwd(proj, sinks, bias):
    T = proj.shape[0]
    nb = T // BLOCK

    def body(sink_ref, bias_ref, q_ref, kp_ref, kc_ref, vp_ref, vc_ref, ag0_ref, ag1_ref, y_ref, z_ref, lse_ref):
        kvs = [slice(hk * HEAD_DIM, (hk + 1) * HEAD_DIM) for hk in range(N_KV_HEADS)]
        qgs = [(_stack_heads(q_ref, hk, F32) * ATTN_SCALE).astype(BF16) for hk in range(N_KV_HEADS)]
        s_prev = [_dot_nt(qgs[hk], kp_ref[:, kvs[hk]].astype(BF16)) for hk in range(N_KV_HEADS)]
        s_cur = [_dot_nt(qgs[hk], kc_ref[:, kvs[hk]].astype(BF16)) for hk in range(N_KV_HEADS)]
        for hk in range(N_KV_HEADS):
            pp, pc, invs = [], [], []
            for g in range(GROUP):
                h = GROUP * hk + g
                rows = slice(g * BLOCK, (g + 1) * BLOCK)
                p_prev, p_cur, inv, lse = _attn_exps(s_prev[hk][rows], s_cur[hk][rows], sink_ref[h], bias_ref[h])
                pp.append(p_prev.astype(BF16))
                pc.append(p_cur.astype(BF16))
                invs.append(inv)
                lse_ref[:, h:h + 1] = lse
            og = _dot(jnp.concatenate(pp, axis=0), vp_ref[:, kvs[hk]].astype(BF16)) + _dot(
                jnp.concatenate(pc, axis=0), vc_ref[:, kvs[hk]].astype(BF16))
            for g in range(GROUP):
                h = GROUP * hk + g
                y_ref[:, h * HEAD_DIM:(h + 1) * HEAD_DIM] = og[g * BLOCK:(g + 1) * BLOCK] * invs[g]
        ag = jnp.concatenate([ag0_ref[...], ag1_ref[...]], axis=1).astype(F32)
        z_ref[...] = (y_ref[...] * (ag * _sigmoid(ag))).astype(BF16)

    prev = lambda c: (lambda i: (jnp.maximum(i - 1, 0), c))
    cur = lambda c: (lambda i: (i, c))
    return pl.pallas_call(
        body,
        name="attn_fwd",
        grid=(nb,),
        in_specs=[
            pl.BlockSpec(memory_space=pltpu.SMEM),
            _bias_spec(),
            pl.BlockSpec((BLOCK, 1024), lambda i: (i, COL_Q // 4)),
            pl.BlockSpec((BLOCK, D_KV), prev(COL_K)),
            pl.BlockSpec((BLOCK, D_KV), cur(COL_K)),
            pl.BlockSpec((BLOCK, D_KV), prev(COL_V)),
            pl.BlockSpec((BLOCK, D_KV), cur(COL_V)),
            pl.BlockSpec((BLOCK, 512), lambda i: (i, COL_ATTN_GATE // 2)),
            pl.BlockSpec((BLOCK, 512), lambda i: (i, COL_ATTN_GATE // 2 + 1)),
        ],
        out_specs=[pl.BlockSpec((BLOCK, 1024), lambda i: (i, 0)), pl.BlockSpec((BLOCK, 1024), lambda i: (i, 0)),
                   pl.BlockSpec((BLOCK, N_Q_HEADS), lambda i: (i, 0))],
        out_shape=[_sds((T, 1024), F32), _sds((T, 1024), BF16), _sds((T, N_Q_HEADS), F32)],
        compiler_params=_params(("arbitrary",), 32),
    )(sinks, *_hbm(bias, proj, proj, proj, proj, proj, proj, proj))


def _attn_bwd(proj, y_attn, lse, dz_attn, sinks, bias, token):
    T = proj.shape[0]
    nb = T // BLOCK

    def body(sink_ref, bias_ref, q_ref, kp_ref, kc_ref, vp_ref, vc_ref, ag0_ref, ag1_ref, y_ref, lse_ref, dz_ref,
             token_ref, dq_ref, dk_ref, dv_ref, dag_ref, ds_ref, dy_s):
        i = pl.program_id(0)

        @pl.when(i == 0)
        def _():
            ds_ref[...] = jnp.zeros_like(ds_ref)

        lane = lax.broadcasted_iota(jnp.int32, (8, 128), 1)
        sub = lax.broadcasted_iota(jnp.int32, (8, 128), 0)
        ag = jnp.concatenate([ag0_ref[...], ag1_ref[...]], axis=1).astype(F32)
        dz = dz_ref[...]
        sg = _sigmoid(ag)
        dag_ref[...] = (dz * y_ref[...] * (sg * (1.0 + ag * (1.0 - sg)))).astype(BF16)
        dy_s[...] = dz * (ag * sg)
        r_cur = pl.multiple_of(i * BLOCK, BLOCK)
        r_prev = pl.multiple_of(jnp.maximum(i - 1, 0) * BLOCK, BLOCK)
        dk_cur, dv_cur, dk_prev, dv_prev = [], [], [], []
        ds_acc = jnp.zeros((8, 128), F32)
        for hk in range(N_KV_HEADS):
            ks = slice(hk * HEAD_DIM, (hk + 1) * HEAD_DIM)
            qg = (_stack_heads(q_ref, hk, F32) * ATTN_SCALE).astype(BF16)
            dog = _stack_heads(dy_s, hk, F32)
            og = _stack_heads(y_ref, hk, F32)
            dog_b = dog.astype(BF16)
            kp = kp_ref[:, ks].astype(BF16)
            kc = kc_ref[:, ks].astype(BF16)
            vp = vp_ref[:, ks].astype(BF16)
            vc = vc_ref[:, ks].astype(BF16)
            s_prev = _dot_nt(qg, kp)
            s_cur = _dot_nt(qg, kc)
            dp_prev = _dot_nt(dog_b, vp)
            dp_cur = _dot_nt(dog_b, vc)
            dvec = jnp.sum(dog * og, axis=-1, keepdims=True)
            pp, pc, dsp, dsc = [], [], [], []
            for g in range(GROUP):
                h = GROUP * hk + g
                rows = slice(g * BLOCK, (g + 1) * BLOCK)
                p_prev, p_cur, p_sink = _attn_probs(
                    s_prev[rows], s_cur[rows], sink_ref[h], bias_ref[h], lse_ref[:, h:h + 1])
                d_h = dvec[rows]
                pp.append(p_prev.astype(BF16))
                pc.append(p_cur.astype(BF16))
                dsp.append((p_prev * (dp_prev[rows] - d_h)).astype(BF16))
                dsc.append((p_cur * (dp_cur[rows] - d_h)).astype(BF16))
                dsink = -jnp.sum(p_sink * d_h, axis=0, keepdims=True)
                ds_acc = ds_acc + jnp.where(jnp.logical_and(lane == h, sub == 1), dsink, 0.0)
            pp = jnp.concatenate(pp, axis=0)
            pc = jnp.concatenate(pc, axis=0)
            dsp = jnp.concatenate(dsp, axis=0)
            dsc = jnp.concatenate(dsc, axis=0)
            dqg = (_dot(dsp, kp) + _dot(dsc, kc)) * ATTN_SCALE
            for g in range(GROUP):
                h = GROUP * hk + g
                dq_ref[:, h * HEAD_DIM:(h + 1) * HEAD_DIM] = dqg[g * BLOCK:(g + 1) * BLOCK].astype(BF16)
            dk_ref[pl.ds(r_cur, BLOCK), ks] = _dot_tn(dsc, qg)
            dv_ref[pl.ds(r_cur, BLOCK), ks] = _dot_tn(pc, dog_b)
            dk_prev.append(_dot_tn(dsp, qg))
            dv_prev.append(_dot_tn(pp, dog_b))
        ds_ref[:, 0:128] += ds_acc

        @pl.when(i > 0)
        def _():
            for hk in range(N_KV_HEADS):
                ks = slice(hk * HEAD_DIM, (hk + 1) * HEAD_DIM)
                dk_ref[pl.ds(r_prev, BLOCK), ks] += dk_prev[hk]
                dv_ref[pl.ds(r_prev, BLOCK), ks] += dv_prev[hk]

    prev = lambda c: (lambda i: (jnp.maximum(i - 1, 0), c))
    cur = lambda c: (lambda i: (i, c))
    blk = pl.BlockSpec((BLOCK, 1024), lambda i: (i, 0))
    whole = pl.BlockSpec((T, D_KV), lambda i: (0, 0))
    return pl.pallas_call(
        body,
        name="attn_bwd",
        grid=(nb,),
        in_specs=[
            pl.BlockSpec(memory_space=pltpu.SMEM),
            _bias_spec(),
            pl.BlockSpec((BLOCK, 1024), lambda i: (i, COL_Q // 4)),
            pl.BlockSpec((BLOCK, D_KV), prev(COL_K)),
            pl.BlockSpec((BLOCK, D_KV), cur(COL_K)),
            pl.BlockSpec((BLOCK, D_KV), prev(COL_V)),
            pl.BlockSpec((BLOCK, D_KV), cur(COL_V)),
            pl.BlockSpec((BLOCK, 512), lambda i: (i, COL_ATTN_GATE // 2)),
            pl.BlockSpec((BLOCK, 512), lambda i: (i, COL_ATTN_GATE // 2 + 1)),
            blk,
            pl.BlockSpec((BLOCK, N_Q_HEADS), lambda i: (i, 0)),
            blk,
            pl.BlockSpec((8, 128), lambda i: (0, 0)),
        ],
        out_specs=[blk, whole, whole, blk, pl.BlockSpec((8, 1024), lambda i: (0, 0))],
        out_shape=[_sds((T, 1024), BF16), _sds((T, D_KV), F32), _sds((T, D_KV), F32), _sds((T, 1024), BF16),
                   _sds((8, 1024), F32)],
        scratch_shapes=[pltpu.VMEM((BLOCK, 1024), F32)],
        compiler_params=_params(("arbitrary",), 48),
    )(sinks, *_hbm(bias, proj, proj, proj, proj, proj, proj, proj, y_attn, lse, dz_attn, token))


def _head(x, target, z_rnn, z_attn, proj, b_gate, g_post, w_rnn_out, w_attn_out, w_out):
    T = x.shape[0]
    tm = 256

    def body(x_ref, t_ref, zr_ref, za_ref, ml0_ref, ml1_ref, ml2_ref, ml3_ref, bg_ref, gp_ref, wr_ref, wa_ref, wo_ref,
             dyx_ref, dzr_ref, dza_ref, dml_ref, mb_ref, dout_ref, dbr_ref, dba_ref, sm_ref):
        @pl.when(pl.program_id(0) == 0)
        def _():
            sm_ref[...] = jnp.zeros_like(sm_ref)

        wr, wa, wo = wr_ref[...], wa_ref[...], wo_ref[...]
        br_rnn = _dot(zr_ref[...], wr)
        br_attn = _dot(za_ref[...], wa)
        ml_rnn = jnp.concatenate([ml0_ref[...], ml1_ref[...]], axis=1).astype(F32)
        ml_attn = jnp.concatenate([ml2_ref[...], ml3_ref[...]], axis=1).astype(F32)
        g_rnn = _sigmoid(ml_rnn + bg_ref[:, 0:D_MODEL])
        g_attn = _sigmoid(ml_attn + bg_ref[:, D_MODEL:2 * D_MODEL])
        mb = (g_rnn * br_rnn + g_attn * br_attn).astype(BF16)
        mb_ref[...] = mb
        out = _dot(mb, wo)
        rstd = lax.rsqrt(jnp.mean(out * out, axis=-1, keepdims=True) + EPS)
        n = out * rstd
        gp = gp_ref[...]
        err = (x_ref[...] + n * gp) - t_ref[...]
        sm_ref[pl.ds(3, 1), :] += 0.5 * jnp.sum(jnp.mean(err * err, axis=-1, keepdims=True), axis=0, keepdims=True)
        dy = err * (1.0 / D_MODEL)
        dyx_ref[...] = dy
        sm_ref[pl.ds(0, 1), :] += jnp.sum(dy * n, axis=0, keepdims=True)
        dn = dy * gp
        dout = (rstd * (dn - n * jnp.mean(dn * n, axis=-1, keepdims=True))).astype(BF16)
        dout_ref[...] = dout
        dmerged = _dot_nt(dout, wo)
        dml_r = (dmerged * br_rnn) * (g_rnn * (1.0 - g_rnn))
        dml_a = (dmerged * br_attn) * (g_attn * (1.0 - g_attn))
        dml_ref[:, 0:D_MODEL] = dml_r.astype(BF16)
        dml_ref[:, D_MODEL:2 * D_MODEL] = dml_a.astype(BF16)
        sm_ref[pl.ds(1, 1), :] += jnp.sum(dml_r, axis=0, keepdims=True)
        sm_ref[pl.ds(2, 1), :] += jnp.sum(dml_a, axis=0, keepdims=True)
        dbr = (dmerged * g_rnn).astype(BF16)
        dba = (dmerged * g_attn).astype(BF16)
        dbr_ref[...] = dbr
        dba_ref[...] = dba
        dzr_ref[...] = _dot_nt(dbr, wr)
        dza_ref[...] = _dot_nt(dba, wa)

    tile = pl.BlockSpec((tm, D_MODEL), lambda i: (i, 0))
    wspec = pl.BlockSpec((D_MODEL, D_MODEL), lambda i: (0, 0))
    ml = lambda q: pl.BlockSpec((tm, 512), lambda i: (i, COL_MERGE // 2 + q))
    return pl.pallas_call(
        body,
        name="head",
        grid=(T // tm,),
        in_specs=[
            tile, tile, tile, tile,
            ml(0), ml(1), ml(2), ml(3),
            pl.BlockSpec((1, 2 * D_MODEL), lambda i: (0, 0)),
            pl.BlockSpec((1, D_MODEL), lambda i: (0, 0)),
            wspec, wspec, wspec,
        ],
        out_specs=[
            tile, tile, tile,
            pl.BlockSpec((tm, 2 * D_MODEL), lambda i: (i, 0)),
            tile, tile, tile, tile,
            pl.BlockSpec((8, D_MODEL), lambda i: (0, 0)),
        ],
        out_shape=[
            _sds((T, D_MODEL), F32), _sds((T, D_MODEL), F32), _sds((T, D_MODEL), F32),
            _sds((T, 2 * D_MODEL), BF16),
            _sds((T, D_MODEL), BF16), _sds((T, D_MODEL), BF16), _sds((T, D_MODEL), BF16), _sds((T, D_MODEL), BF16),
            _sds((8, D_MODEL), F32),
        ],
        compiler_params=_params(("arbitrary",), 56),
    )(*_hbm(x, target, z_rnn, z_attn, proj, proj, proj, proj, b_gate, g_post, w_rnn_out, w_attn_out, w_out))


def _matmul_tn(a, b, name):
    T, M = a.shape
    N = b.shape[1]
    tk = min(512, T)
    nt = T // tk

    def body(a_ref, b_ref, o_ref, ob_ref):
        @pl.when(pl.program_id(0) == 0)
        def _():
            o_ref[...] = jnp.zeros_like(o_ref)

        o_ref[...] += _dot_tn(a_ref[...], b_ref[...])

        @pl.when(pl.program_id(0) == nt - 1)
        def _():
            ob_ref[...] = o_ref[...].astype(BF16)

    whole = pl.BlockSpec((M, N), lambda t: (0, 0))
    return pl.pallas_call(
        body,
        name=name,
        grid=(nt,),
        in_specs=[pl.BlockSpec((tk, M), lambda t: (t, 0)), pl.BlockSpec((tk, N), lambda t: (t, 0))],
        out_specs=[whole, whole],
        out_shape=[_sds((M, N), F32), _sds((M, N), BF16)],
        compiler_params=_params(("arbitrary",), 48),
    )(*_hbm(a, b))


DPROJ_WIDTHS = (D_RNN, D_RNN, 1024, D_KV, D_KV, 1024, 2 * D_MODEL)


def _dproj_segments():
    segs, start = [[] for _ in range(N_CHIPS)], 0
    for p, width in enumerate(DPROJ_WIDTHS):
        for c in range(N_CHIPS):
            lo, hi = max(start, c * W_IN_SHARD), min(start + width, (c + 1) * W_IN_SHARD)
            if lo < hi:
                segs[c].append((p, lo - start, hi - start, lo - c * W_IN_SHARD, hi - c * W_IN_SHARD))
        start += width
    return segs


def _dh_bwd(pieces, w_in_g, x, dyx, g_pre, token):
    T = x.shape[0]
    tm = min(512, T)
    n = len(pieces)
    segs = _dproj_segments()

    def body(*refs):
        p_refs, w_hbm, x_ref, dyx_ref, g_ref = refs[0:n], refs[n], refs[n + 1], refs[n + 2], refs[n + 3]
        gx_ref, dg_ref, w_ref = refs[n + 5], refs[n + 6], refs[n + 7]

        @pl.when(pl.program_id(0) == 0)
        def _():
            pltpu.sync_copy(w_hbm, w_ref)
            dg_ref[...] = jnp.zeros_like(dg_ref)

        dh = None
        for c in range(N_CHIPS):
            for p, a0, a1, u0, u1 in segs[c]:
                part = _dot_nt(p_refs[p][:, a0:a1].astype(BF16), w_ref[c, :, u0:u1])
                dh = part if dh is None else dh + part
        xv = x_ref[...]
        rstd = lax.rsqrt(jnp.mean(xv * xv, axis=-1, keepdims=True) + EPS)
        nx = xv * rstd
        dhg = dh * g_ref[...]
        gx_ref[...] = dyx_ref[...] + rstd * (dhg - nx * jnp.mean(dhg * nx, axis=-1, keepdims=True))
        dg_ref[pl.ds(0, 1), :] += jnp.sum(dh * nx, axis=0, keepdims=True)

    tile = pl.BlockSpec((tm, D_MODEL), lambda i: (i, 0))
    return pl.pallas_call(
        body,
        name="dh_bwd",
        grid=(T // tm,),
        in_specs=[pl.BlockSpec((tm, w), lambda i: (i, 0)) for w in DPROJ_WIDTHS] + [
            ANY, tile, tile,
            pl.BlockSpec((1, D_MODEL), lambda i: (0, 0)),
            pl.BlockSpec((8, 128), lambda i: (0, 0)),
        ],
        out_specs=[tile, pl.BlockSpec((8, D_MODEL), lambda i: (0, 0))],
        out_shape=[_sds((T, D_MODEL), F32), _sds((8, D_MODEL), F32)],
        scratch_shapes=[pltpu.VMEM(w_in_g.shape, BF16)],
        compiler_params=_params(("arbitrary",), 56),
    )(*_hbm(*pieces, w_in_g, x, dyx, g_pre, token))


def _dw_in(ht, pieces):
    T = ht.shape[1]
    tk = min(512, T)
    nt = T // tk
    n = len(pieces)
    segs = _dproj_segments()

    def body(*refs):
        h_ref, p_refs, o_ref, ob_ref = refs[0], refs[1:n + 1], refs[n + 1], refs[n + 2]

        @pl.when(pl.program_id(1) == 0)
        def _():
            o_ref[...] = jnp.zeros_like(o_ref)

        for c in range(N_CHIPS):
            @pl.when(pl.program_id(0) == c)
            def _():
                for p, a0, a1, u0, u1 in segs[c]:
                    o_ref[:, u0:u1] += _dot(h_ref[...], p_refs[p][:, a0:a1].astype(BF16))

        @pl.when(pl.program_id(1) == nt - 1)
        def _():
            ob_ref[...] = o_ref[...].astype(BF16)

    def piece_spec(p):
        chips = [c for c in range(N_CHIPS) if any(s[0] == p for s in segs[c])]

        def index(c, t):
            used = functools.reduce(jnp.logical_or, [c == k for k in chips])
            return (jnp.where(used, t, 0), 0)

        return pl.BlockSpec((tk, DPROJ_WIDTHS[p]), index)

    return pl.pallas_call(
        body,
        name="dw_in",
        grid=(N_CHIPS, nt),
        in_specs=[pl.BlockSpec((D_MODEL, tk), lambda c, t: (0, t))] + [piece_spec(p) for p in range(n)],
        out_specs=[pl.BlockSpec((None, D_MODEL, W_IN_SHARD), lambda c, t: (c, 0, 0))] * 2,
        out_shape=[_sds((N_CHIPS, D_MODEL, W_IN_SHARD), F32), _sds((N_CHIPS, D_MODEL, W_IN_SHARD), BF16)],
        compiler_params=_params(("parallel", "arbitrary"), 56),
    )(*_hbm(ht, *pieces))


ELEMENTWISE_TILE_BYTES = MIB


def _row_tile(rows, cols):
    if rows * cols * 4 <= ELEMENTWISE_TILE_BYTES:
        return rows
    for t in (512, 256, 128, 64, 32, 16, 8):
        if rows % t == 0 and t * cols * 4 <= ELEMENTWISE_TILE_BYTES:
            return t
    return rows


def _pair_sum(g, got, chip_core, name):
    nch, R, C = g.shape
    h = R // 2
    tr = _row_tile(h, C)
    nt = h // tr

    def body(jc_ref, g_ref, got_ref, p_ref, pb_ref):
        s = g_ref[...] + got_ref[...].astype(F32)
        pb_ref[...] = s.astype(BF16)

        @pl.when(pl.program_id(1) == jc_ref[0])
        def _():
            p_ref[...] = s

    return pl.pallas_call(
        body,
        name=name,
        grid_spec=pltpu.PrefetchScalarGridSpec(
            num_scalar_prefetch=1,
            grid=(nt, nch),
            in_specs=[pl.BlockSpec((None, tr, C), lambda i, j, jc_ref: (j, jc_ref[1] * nt + i, 0)),
                      pl.BlockSpec((None, tr, C), lambda i, j, jc_ref: (j, i, 0))],
            out_specs=[pl.BlockSpec((tr, C), lambda i, j, jc_ref: (i, 0)),
                       pl.BlockSpec((None, tr, C), lambda i, j, jc_ref: (j, i, 0))],
        ),
        out_shape=[_sds((h, C), F32), _sds((nch, h, C), BF16)],
        compiler_params=_params(("parallel", "arbitrary"), 48),
    )(chip_core, *_hbm(g, got))


def _chip_sum(p, got, chip_core, name):
    h, C = p.shape
    tr = _row_tile(h, C)
    nt = h // tr

    def body(jc_ref, p_ref, g0_ref, g1_ref, g2_ref, o_ref):
        o_ref[...] = ((p_ref[...] + g0_ref[...].astype(F32)) + g1_ref[...].astype(F32)) + g2_ref[...].astype(F32)

    rel = lambda r: pl.BlockSpec((None, tr, C), lambda i, jc_ref: (r, i, 0))
    return pl.pallas_call(
        body,
        name=name,
        grid_spec=pltpu.PrefetchScalarGridSpec(
            num_scalar_prefetch=1,
            grid=(nt,),
            in_specs=[pl.BlockSpec((tr, C), lambda i, jc_ref: (i, 0)), rel(0), rel(1), rel(2)],
            out_specs=pl.BlockSpec((tr, C), lambda i, jc_ref: (jc_ref[1] * nt + i, 0)),
        ),
        out_shape=_sds((2 * h, C), F32),
        compiler_params=_params(("parallel",), 48),
    )(chip_core, *_hbm(p, got, got, got))


def _place_shards(shards, chip, name):
    n = len(shards)
    tiles = [_row_tile(s.shape[0], s.shape[1]) for s in shards]
    steps = max(s.shape[0] // t for s, t in zip(shards, tiles))
    tiles = [s.shape[0] // steps for s in shards]

    def body(j_ref, *refs):
        for a in range(n):
            refs[n + a][...] = refs[a][...].astype(BF16)

    return pl.pallas_call(
        body,
        name=name,
        grid_spec=pltpu.PrefetchScalarGridSpec(
            num_scalar_prefetch=1,
            grid=(steps,),
            in_specs=[pl.BlockSpec((t, s.shape[1]), lambda i, j_ref: (i, 0)) for s, t in zip(shards, tiles)],
            out_specs=[pl.BlockSpec((None, t, s.shape[1]), lambda i, j_ref: (j_ref[0], i, 0))
                       for s, t in zip(shards, tiles)],
        ),
        out_shape=[_sds((N_CHIPS,) + s.shape, BF16) for s in shards],
        compiler_params=_params(("parallel",), 48),
    )(chip, *_hbm(*shards))


def _adamw(w, g, m, v, name):
    R, C = w.shape
    tr = _row_tile(R, C)
    c1 = 1.0 - ADAM_B1 ** ADAM_STEP
    c2 = 1.0 - ADAM_B2 ** ADAM_STEP

    def body(w_ref, g_ref, m_ref, v_ref, d_ref, nm_ref, nv_ref):
        g = g_ref[...]
        nm = ADAM_B1 * m_ref[...] + (1.0 - ADAM_B1) * g
        nv = ADAM_B2 * v_ref[...] + (1.0 - ADAM_B2) * (g * g)
        nm_ref[...] = nm
        nv_ref[...] = nv
        d_ref[...] = (-ADAM_LR) * ((nm / c1) / (jnp.sqrt(nv / c2) + ADAM_EPS) + ADAM_WD * w_ref[...])

    spec = pl.BlockSpec((tr, C), lambda i: (i, 0))
    return pl.pallas_call(
        body, name=name, grid=(R // tr,), in_specs=[spec] * 4, out_specs=[spec] * 3,
        out_shape=[_sds((R, C), F32)] * 3, compiler_params=_params(("parallel",), 48),
    )(*_hbm(w, g, m, v))


def _place():
    return lax.axis_index("x"), lax.axis_index("y"), lax.axis_index("c")


def _chip_of(x, y, r):
    return (x ^ (r >> 1), y ^ (r & 1))


ANY = pl.BlockSpec(memory_space=pl.ANY)


def _gather_weights(placed, cw8):
    nbig = len(placed)
    halves = [s.shape[1] // 2 for s in placed]
    pieces = [max(1, h // 64) for h in halves]
    rows = [h // p for h, p in zip(halves, pieces)]
    order = [(a, q) for q in range(max(pieces)) for a in range(nbig) if q < pieces[a]]
    ici_sem = {(a, q, r): 3 * i + (r - 1) for i, (a, q) in enumerate(order) for r in (1, 2, 3)}
    cw_sem = {r: 3 * len(order) + (r - 1) for r in (1, 2, 3)}
    d2d_sem = {key: 3 * len(order) + 3 + k for key, k in ici_sem.items()}
    nsem = 6 * len(order) + 3

    def body(*refs):
        cw_ref, dsts, gcw_ref = refs[nbig], refs[nbig + 1:2 * nbig + 1], refs[2 * nbig + 1]
        send_sems, recv_sems = refs[2 * nbig + 2:]
        x, y, c = _place()
        j = 2 * x + y

        def piece_rows(a, q, core):
            return pl.ds(pl.multiple_of(core * halves[a] + q * rows[a], 16), rows[a])

        def ici(a, q, r):
            tx, ty = _chip_of(x, y, r)
            k = ici_sem[(a, q, r)]
            region = dsts[a].at[j, piece_rows(a, q, c), :]
            return pltpu.make_async_remote_copy(
                src_ref=region, dst_ref=region, send_sem=send_sems.at[k], recv_sem=recv_sems.at[k],
                device_id=(tx, ty, c), device_id_type=MESH)

        def ici_landed(a, q, r):
            tx, ty = _chip_of(x, y, r)
            k = ici_sem[(a, q, r)]
            region = dsts[a].at[2 * tx + ty, piece_rows(a, q, c), :]
            return pltpu.make_async_remote_copy(
                src_ref=region, dst_ref=region, send_sem=send_sems.at[k], recv_sem=recv_sems.at[k],
                device_id=(tx, ty, c), device_id_type=MESH)

        def d2d(a, q, r, core):
            tx, ty = _chip_of(x, y, r)
            k = d2d_sem[(a, q, r)]
            region = dsts[a].at[2 * tx + ty, piece_rows(a, q, core), :]
            return pltpu.make_async_remote_copy(
                src_ref=region, dst_ref=region, send_sem=send_sems.at[k], recv_sem=recv_sems.at[k],
                device_id=(x, y, 1 - c), device_id_type=MESH)

        def cw_copy(r):
            tx, ty = _chip_of(x, y, r)
            k = cw_sem[r]
            return pltpu.make_async_remote_copy(
                src_ref=cw_ref, dst_ref=gcw_ref.at[j], send_sem=send_sems.at[k], recv_sem=recv_sems.at[k],
                device_id=(tx, ty, c), device_id_type=MESH)

        def cw_landed(r):
            tx, ty = _chip_of(x, y, r)
            k = cw_sem[r]
            region = gcw_ref.at[2 * tx + ty]
            return pltpu.make_async_remote_copy(
                src_ref=region, dst_ref=region, send_sem=send_sems.at[k], recv_sem=recv_sems.at[k],
                device_id=(tx, ty, c), device_id_type=MESH)

        def relay(a, q, origin, to):
            ox, oy = _chip_of(x, y, origin)
            tx, ty = _chip_of(x, y, to)
            k = ici_sem[(a, q, 3)]
            region = dsts[a].at[2 * ox + oy, piece_rows(a, q, c), :]
            return pltpu.make_async_remote_copy(
                src_ref=region, dst_ref=region, send_sem=send_sems.at[k], recv_sem=recv_sems.at[k],
                device_id=(tx, ty, c), device_id_type=MESH)

        first = [ici(a, q, r) for (a, q) in order for r in (1, 2)] + [cw_copy(r) for r in (1, 2, 3)]
        for cp in first:
            cp.start()
        passed = []
        for (a, q) in order:
            for r in (1, 2):
                ici_landed(a, q, r).wait_recv()
                if q % 2 == r - 1:
                    cp = relay(a, q, r, 3 - r)
                    cp.start()
                    passed.append(cp)
                cp = d2d(a, q, r, c)
                cp.start()
                passed.append(cp)
        for (a, q) in order:
            ici_landed(a, q, 3).wait_recv()
            cp = d2d(a, q, 3, c)
            cp.start()
            passed.append(cp)
        for r in (1, 2, 3):
            cw_landed(r).wait_recv()
        for (a, q) in order:
            for r in (1, 2, 3):
                d2d(a, q, r, 1 - c).wait_recv()
        for cp in first + passed:
            cp.wait_send()

    return pl.pallas_call(
        body,
        name="gather_weights",
        in_specs=[ANY] * (nbig + 1),
        out_specs=[ANY] * (nbig + 1),
        out_shape=[_sds(s.shape, s.dtype) for s in placed] + [_sds((N_CHIPS,) + cw8.shape, cw8.dtype)],
        input_output_aliases={a: a for a in range(nbig)},
        scratch_shapes=[pltpu.SemaphoreType.DMA((nsem,)), pltpu.SemaphoreType.DMA((nsem,))],
    )(*placed, cw8)


def _gather_late_start(placed, after, name):
    n = len(placed)
    halves = [s.shape[1] // 2 for s in placed]

    def body(*refs):
        g_refs = refs[0:n]
        send_sems, recv_sems, token = refs[n + 1], refs[n + 2], refs[-1]
        x, y, c = _place()
        j = 2 * x + y
        for a in range(n):
            mine = g_refs[a].at[j, pl.ds(pl.multiple_of(c * halves[a], 16), halves[a]), :]
            for r in (1, 2, 3):
                tx, ty = _chip_of(x, y, r)
                for to_core in (0, 1):
                    k = ((a * 3 + (r - 1)) * 2 + c) * 2 + to_core
                    pltpu.make_async_remote_copy(
                        src_ref=mine, dst_ref=mine, send_sem=send_sems.at[k], recv_sem=recv_sems.at[k],
                        device_id=(tx, ty, to_core), device_id_type=MESH).start()
        token[...] = jnp.zeros_like(token)

    hbm = lambda t: pltpu.HBM(t.shape, t.dtype)
    keep = lambda t: pltpu.with_memory_space_constraint(t, pltpu.HBM)
    nsem = 12 * n
    outs = pl.pallas_call(
        body,
        name=name,
        in_specs=[HBM] * n + [ANY],
        out_specs=(SEM, SEM, *[HBM] * n, pl.BlockSpec(memory_space=pltpu.VMEM)),
        out_shape=(pltpu.SemaphoreType.DMA((nsem,)), pltpu.SemaphoreType.DMA((nsem,)), *[hbm(p) for p in placed],
                   jax.ShapeDtypeStruct((8, 128), F32)),
        input_output_aliases={i: 2 + i for i in range(n)},
        compiler_params=pltpu.CompilerParams(has_side_effects=DATAFLOW),
    )(*[keep(p) for p in placed], after)
    return outs[0], outs[1], list(outs[2:2 + n]), outs[-1]


def _gather_late_wait(send_sems, recv_sems, thru, after, name):
    n = len(thru)
    halves = [s.shape[1] // 2 for s in thru]

    def body(*refs):
        g_refs = refs[0:n]
        send_sems, recv_sems = refs[n], refs[n + 1]
        x, y, c = _place()
        j = 2 * x + y
        for a in range(n):
            mine = g_refs[a].at[j, pl.ds(pl.multiple_of(c * halves[a], 16), halves[a]), :]
            for r in (1, 2, 3):
                tx, ty = _chip_of(x, y, r)
                for other in (0, 1):
                    k_out = ((a * 3 + (r - 1)) * 2 + c) * 2 + other
                    pltpu.make_async_remote_copy(
                        src_ref=mine, dst_ref=mine, send_sem=send_sems.at[k_out], recv_sem=recv_sems.at[k_out],
                        device_id=(tx, ty, other), device_id_type=MESH).wait_send()
                    k_in = ((a * 3 + (r - 1)) * 2 + other) * 2 + c
                    theirs = g_refs[a].at[2 * tx + ty, pl.ds(other * halves[a], halves[a]), :]
                    pltpu.make_async_remote_copy(
                        src_ref=theirs, dst_ref=theirs, send_sem=send_sems.at[k_in], recv_sem=recv_sems.at[k_in],
                        device_id=(tx, ty, other), device_id_type=MESH).wait_recv()

    hbm = lambda t: pltpu.HBM(t.shape, t.dtype)
    outs = pl.pallas_call(
        body,
        name=name,
        in_specs=[HBM] * n + [SEM, SEM, ANY],
        out_specs=[HBM] * n,
        out_shape=[hbm(t) for t in thru],
        input_output_aliases={i: i for i in range(n)},
        compiler_params=pltpu.CompilerParams(has_side_effects=DATAFLOW),
    )(*thru, send_sems, recv_sems, after)
    return list(outs)


D2D_PIECE_ROWS = 64


def _pair_exchange(grads, name):
    n = len(grads)
    halves = [g.shape[1] // 2 for g in grads]

    def body(*refs):
        g_refs, got_refs = refs[0:n], refs[n:2 * n]
        send_sems, recv_sems = refs[2 * n:]
        x, y, c = _place()

        def copy(a, src, dst):
            return pltpu.make_async_remote_copy(
                src_ref=src, dst_ref=dst, send_sem=send_sems.at[a], recv_sem=recv_sems.at[a],
                device_id=(x, y, 1 - c), device_id_type=MESH)

        for a in range(n):
            for jj in range(N_CHIPS):
                for q in range(halves[a] // D2D_PIECE_ROWS):
                    src_rows = pl.ds(pl.multiple_of((1 - c) * halves[a] + q * D2D_PIECE_ROWS, 16), D2D_PIECE_ROWS)
                    dst_rows = pl.ds(q * D2D_PIECE_ROWS, D2D_PIECE_ROWS)
                    copy(a, g_refs[a].at[jj, src_rows, :], got_refs[a].at[jj, dst_rows, :]).start()
        for a in range(n):
            sent = g_refs[a].at[:, pl.ds(pl.multiple_of((1 - c) * halves[a], 16), halves[a]), :]
            copy(a, sent, got_refs[a]).wait()

    return pl.pallas_call(
        body,
        name=name,
        in_specs=[ANY] * n,
        out_specs=[ANY] * n,
        out_shape=[_sds((N_CHIPS, h, g.shape[2]), g.dtype) for g, h in zip(grads, halves)],
        scratch_shapes=[pltpu.SemaphoreType.DMA((n,)), pltpu.SemaphoreType.DMA((n,))],
    )(*grads)


HBM = pl.BlockSpec(memory_space=pltpu.HBM)
SEM = pl.BlockSpec(memory_space=pltpu.SEMAPHORE)
DATAFLOW = pltpu.SideEffectType.DATAFLOW_SIDE_EFFECTING


def _chip_copy(p_refs, land_refs, send_sems, recv_sems, a, r, blocked):
    x, y, c = _place()
    tx, ty = _chip_of(x, y, r)
    k = a * 3 + (r - 1)
    return pltpu.make_async_remote_copy(
        src_ref=p_refs[a].at[2 * tx + ty] if blocked else p_refs[a], dst_ref=land_refs[a].at[r - 1],
        send_sem=send_sems.at[k], recv_sem=recv_sems.at[k], device_id=(tx, ty, c), device_id_type=MESH)


def _chip_exchange_start(psums, name, blocked=True):
    n = len(psums)
    lands = [lax.empty((3,) + (p.shape[1:] if blocked else p.shape), p.dtype) for p in psums]

    def body(*refs):
        p_refs, land_refs = refs[0:n], refs[n:2 * n]
        send_sems, recv_sems, token = refs[2 * n], refs[2 * n + 1], refs[-1]
        for a in range(n):
            for r in (1, 2, 3):
                _chip_copy(p_refs, land_refs, send_sems, recv_sems, a, r, blocked).start()
        token[...] = jnp.zeros_like(token)

    hbm = lambda t: pltpu.HBM(t.shape, t.dtype)
    keep = lambda t: pltpu.with_memory_space_constraint(t, pltpu.HBM)
    outs = pl.pallas_call(
        body,
        name=name,
        in_specs=[HBM] * (2 * n),
        out_specs=(SEM, SEM, *[HBM] * (2 * n), pl.BlockSpec(memory_space=pltpu.VMEM)),
        out_shape=(pltpu.SemaphoreType.DMA((3 * n,)), pltpu.SemaphoreType.DMA((3 * n,)),
                   *[hbm(p) for p in psums], *[hbm(l) for l in lands], _sds((8, 128), F32)),
        input_output_aliases={i: 2 + i for i in range(2 * n)},
        compiler_params=pltpu.CompilerParams(has_side_effects=DATAFLOW),
    )(*[keep(p) for p in psums], *[keep(l) for l in lands])
    return outs[0], outs[1], list(outs[2:2 + n]), list(outs[2 + n:2 + 2 * n]), outs[-1]


def _chip_exchange_wait(send_sems, recv_sems, p_thru, land_thru, after, name, blocked=True):
    n = len(p_thru)

    def body(*refs):
        p_refs, land_refs = refs[0:n], refs[n:2 * n]
        send_sems, recv_sems = refs[2 * n], refs[2 * n + 1]
        for a in range(n):
            for r in (1, 2, 3):
                copy = _chip_copy(p_refs, land_refs, send_sems, recv_sems, a, r, blocked)
                copy.wait_send()
                copy.wait_recv()

    hbm = lambda t: pltpu.HBM(t.shape, t.dtype)
    outs = pl.pallas_call(
        body,
        name=name,
        in_specs=[HBM] * (2 * n) + [SEM, SEM, ANY],
        out_specs=[HBM] * (2 * n),
        out_shape=[hbm(p) for p in p_thru] + [hbm(l) for l in land_thru],
        input_output_aliases={i: i for i in range(2 * n)},
        compiler_params=pltpu.CompilerParams(has_side_effects=DATAFLOW),
    )(*p_thru, *land_thru, send_sems, recv_sems, after)
    return list(outs[n:2 * n])


def _pair_share(fulls):
    n = len(fulls)
    halves = [f.shape[0] // 2 for f in fulls]

    def body(*refs):
        full_refs = refs[n:2 * n]
        send_sems, recv_sems = refs[2 * n:]
        x, y, c = _place()

        def half_of(a, core):
            return full_refs[a].at[pl.ds(pl.multiple_of(core * halves[a], 8), halves[a]), :]

        def remote(a, src, dst):
            return pltpu.make_async_remote_copy(
                src_ref=src, dst_ref=dst, send_sem=send_sems.at[a], recv_sem=recv_sems.at[a],
                device_id=(x, y, 1 - c), device_id_type=MESH)

        for a in range(n):
            for q in range(halves[a] // D2D_PIECE_ROWS):
                piece = full_refs[a].at[
                    pl.ds(pl.multiple_of(c * halves[a] + q * D2D_PIECE_ROWS, 8), D2D_PIECE_ROWS), :]
                remote(a, piece, piece).start()
        for a in range(n):
            remote(a, half_of(a, c), half_of(a, c)).wait_send()
            remote(a, half_of(a, 1 - c), half_of(a, 1 - c)).wait_recv()

    return pl.pallas_call(
        body,
        name="pair_share",
        in_specs=[ANY] * n,
        out_specs=[ANY] * n,
        out_shape=[_sds(f.shape, F32) for f in fulls],
        input_output_aliases={a: a for a in range(n)},
        scratch_shapes=[pltpu.SemaphoreType.DMA((n,)), pltpu.SemaphoreType.DMA((n,))],
    )(*fulls)


def _small_pair_sum(s):
    R, C = s.shape
    V = SMALL_VECTOR_ROWS

    def body(s_ref, v_ref, m_ref, sib, send_sem, recv_sem):
        x, y, c = _place()

        def to_sib(src, dst):
            return pltpu.make_async_remote_copy(
                src_ref=src, dst_ref=dst, send_sem=send_sem, recv_sem=recv_sem,
                device_id=(x, y, 1 - c), device_id_type=MESH)

        for q in range(R // 8):
            to_sib(s_ref.at[pl.ds(8 * q, 8), :], sib.at[pl.ds(8 * q, 8), :]).start()
        to_sib(s_ref, sib).wait()
        v_ref[...] = s_ref[pl.ds(0, V), :] + sib[pl.ds(0, V), :]
        m_ref[...] = (s_ref[pl.ds(V, R - V), :] + sib[pl.ds(V, R - V), :]).astype(BF16)

    return pl.pallas_call(
        body,
        name="small_pair_sum",
        in_specs=[pl.BlockSpec(memory_space=pltpu.VMEM)],
        out_specs=[pl.BlockSpec(memory_space=pltpu.VMEM)] * 2,
        out_shape=[jax.ShapeDtypeStruct((V, C), F32), jax.ShapeDtypeStruct((R - V, C), BF16)],
        scratch_shapes=[pltpu.VMEM((R, C), F32), pltpu.SemaphoreType.DMA, pltpu.SemaphoreType.DMA],
    )(s)


def _small_total(chip, own, landed):
    V, C = own[0].shape
    M = own[1].shape[0]

    def body(j_ref, v_ref, m_ref, lv_ref, lm_ref, o_ref, chips_v, chips_m):
        j = j_ref[0]
        chips_v[j] = v_ref[...]
        chips_m[j] = m_ref[...]
        for r in (1, 2, 3):
            chips_v[j ^ r] = lv_ref[r - 1]
            chips_m[j ^ r] = lm_ref[r - 1]
        o_ref[pl.ds(0, V), :] = (chips_v[0] + chips_v[1]) + (chips_v[2] + chips_v[3])
        o_ref[pl.ds(V, M), :] = (chips_m[0].astype(F32) + chips_m[1].astype(F32)) + (
            chips_m[2].astype(F32) + chips_m[3].astype(F32))

    vmem = pl.BlockSpec(memory_space=pltpu.VMEM)
    return pl.pallas_call(
        body,
        name="small_total",
        in_specs=[pl.BlockSpec(memory_space=pltpu.SMEM), vmem, vmem, vmem, vmem],
        out_specs=vmem,
        out_shape=jax.ShapeDtypeStruct((V + M, C), F32),
        scratch_shapes=[pltpu.VMEM((N_CHIPS, V, C), F32), pltpu.VMEM((N_CHIPS, M, C), BF16)],
    )(chip, own[0], own[1], landed[0], landed[1])


def _block_diag(w):
    w4 = w.reshape(4, 4, RNN_BLOCK_W, RNN_BLOCK_W)
    eye = jnp.eye(4, dtype=w.dtype)
    return jnp.einsum("jaik,ab->jaibk", w4, eye).reshape(4, RNN_TILE, RNN_TILE)


def _block_diag_part(d):
    d5 = d.reshape(4, 4, RNN_BLOCK_W, 4, RNN_BLOCK_W)
    return jnp.stack([d5[:, a, :, a, :] for a in range(4)], axis=1).reshape(RNN_BLOCKS, RNN_BLOCK_W, RNN_BLOCK_W)


def _local_grads(x, target, g_pre, w_in_g, b_gate, conv_w, conv_b, w_rg_a, b_rg_a, w_rg_x, b_rg_x, lam, sinks,
                 out_weights, fwd_token, g_post, on_out_grads, on_w_in_grad):
    wa_bd = _block_diag(w_rg_a).astype(BF16)
    wx_bd = _block_diag(w_rg_x).astype(BF16)
    b_a = b_rg_a.reshape(1, D_RNN)
    b_x = b_rg_x.reshape(1, D_RNN)

    proj, ht = _proj_fwd(x, g_pre, w_in_g)
    y_rnn, z_rnn, conv = _rnn_fwd(proj, conv_w, conv_b, wa_bd, wx_bd, b_a, b_x, lam, fwd_token)
    bias = _attn_bias()
    y_attn, z_attn, lse = _attn_fwd(proj, sinks, bias)
    w_rnn_out, w_attn_out, w_out = out_weights(z_attn)
    dyx, dz_rnn, dz_attn, dml, merged, dout, dbr_rnn, dbr_attn, head_small = _head(
        x, target, z_rnn, z_attn, proj, b_gate, g_post, w_rnn_out, w_attn_out, w_out)
    out_grads = [_matmul_tn(z_rnn, dbr_rnn, "dw_rnn_out"), _matmul_tn(z_attn, dbr_attn, "dw_attn_out"),
                 _matmul_tn(merged, dout, "dw_out")]
    shard_rows = lambda d: d.reshape(N_CHIPS, OUT_SHARD, D_MODEL)
    token = on_out_grads([shard_rows(g) for g, _ in out_grads], [shard_rows(gb) for _, gb in out_grads])
    dq, dk, dv, dag, attn_small = _attn_bwd(proj, y_attn, lse, dz_attn, sinks, bias, token)
    drx, drg, dwa_t, dwx_t, rnn_small = _rnn_bwd(proj, conv, y_rnn, dz_rnn, conv_w, wa_bd, wx_bd, b_a, b_x, lam)
    dproj = [drx, drg, dq, dk, dv, dag, dml]
    token = on_w_in_grad(*_dw_in(ht, dproj))
    grad_x, dh_small = _dh_bwd(dproj, w_in_g, x, dyx, g_pre, token)
    small = jnp.concatenate([rnn_small, head_small, dh_small + attn_small,
                             _block_diag_part(dwa_t).reshape(64, 1024), _block_diag_part(dwx_t).reshape(64, 1024)], axis=0)
    return grad_x, small


ROW_LOSS = 11


def _rows8(parts):
    out = None
    for r, a in parts:
        p = jnp.pad(a, ((r, 8 - r - a.shape[0]), (0, 1024 - a.shape[1])))
        out = p if out is None else out + p
    return out


def _pack_small(p):
    g0 = _rows8([(0, p["b_rg_a"].reshape(1, 1024)), (1, p["b_rg_x"].reshape(1, 1024)), (2, p["lru_lambda"]),
                 (3, p["conv_b"]), (4, p["conv_w"][0])])
    g1 = _rows8([(0, p["post_norm_g"]), (1, p["b_gate"].reshape(2, 1024))])
    g2 = _rows8([(0, p["pre_norm_g"]), (1, p["attn_sinks"])])
    return jnp.concatenate([g0, g1, g2, p["w_rg_a"].reshape(64, 1024), p["w_rg_x"].reshape(64, 1024)], axis=0)


def _unpack_small(s, conv_cols):
    return {
        "b_rg_a": s[0:1].reshape(1, 16, 64), "b_rg_x": s[1:2].reshape(1, 16, 64), "lru_lambda": s[2:3],
        "conv_b": s[3:4], "conv_w": s[4:8, 0:conv_cols].reshape(1, CONV_W, conv_cols),
        "post_norm_g": s[8:9], "b_gate": s[9:11].reshape(1, 2048),
        "pre_norm_g": s[16:17], "attn_sinks": s[17:18, 0:N_Q_HEADS],
        "w_rg_a": s[24:88].reshape(1, 16, 64, 64), "w_rg_x": s[88:152].reshape(1, 16, 64, 64),
    }


WEIGHTS = ["pre_norm_g", "w_in", "b_gate", "conv_w", "conv_b", "w_rg_a", "b_rg_a", "w_rg_x", "b_rg_x", "lru_lambda",
           "attn_sinks", "w_rnn_out", "w_attn_out", "w_out", "post_norm_g"]
BIG = ["w_in", "w_rnn_out", "w_attn_out", "w_out"]


def kernel(x, pre_norm_g, w_in, b_gate, conv_w, conv_b, w_rg_a, b_rg_a, w_rg_x, b_rg_x, lru_lambda, attn_sinks, w_rnn_out, w_attn_out, w_out, post_norm_g, loss_target, m_pre_norm_g, m_w_in, m_b_gate, m_conv_w, m_conv_b, m_w_rg_a, m_b_rg_a, m_w_rg_x, m_b_rg_x, m_lru_lambda, m_attn_sinks, m_w_rnn_out, m_w_attn_out, m_w_out, m_post_norm_g, v_pre_norm_g, v_w_in, v_b_gate, v_conv_w, v_conv_b, v_w_rg_a, v_b_rg_a, v_w_rg_x, v_b_rg_x, v_lru_lambda, v_attn_sinks, v_w_rnn_out, v_w_attn_out, v_w_out, v_post_norm_g):
    w = dict(pre_norm_g=pre_norm_g, w_in=w_in, b_gate=b_gate, conv_w=conv_w, conv_b=conv_b, w_rg_a=w_rg_a,
             b_rg_a=b_rg_a, w_rg_x=w_rg_x, b_rg_x=b_rg_x, lru_lambda=lru_lambda, attn_sinks=attn_sinks,
             w_rnn_out=w_rnn_out, w_attn_out=w_attn_out, w_out=w_out, post_norm_g=post_norm_g)
    m = dict(pre_norm_g=m_pre_norm_g, w_in=m_w_in, b_gate=m_b_gate, conv_w=m_conv_w, conv_b=m_conv_b, w_rg_a=m_w_rg_a,
             b_rg_a=m_b_rg_a, w_rg_x=m_w_rg_x, b_rg_x=m_b_rg_x, lru_lambda=m_lru_lambda, attn_sinks=m_attn_sinks,
             w_rnn_out=m_w_rnn_out, w_attn_out=m_w_attn_out, w_out=m_w_out, post_norm_g=m_post_norm_g)
    v = dict(pre_norm_g=v_pre_norm_g, w_in=v_w_in, b_gate=v_b_gate, conv_w=v_conv_w, conv_b=v_conv_b, w_rg_a=v_w_rg_a,
             b_rg_a=v_b_rg_a, w_rg_x=v_w_rg_x, b_rg_x=v_b_rg_x, lru_lambda=v_lru_lambda, attn_sinks=v_attn_sinks,
             w_rnn_out=v_w_rnn_out, w_attn_out=v_w_attn_out, w_out=v_w_out, post_norm_g=v_post_norm_g)
    chip = 2 * lax.axis_index("x") + lax.axis_index("y")

    chip_idx = chip.astype(jnp.int32).reshape(1)
    chip_core = jnp.stack([chip, lax.axis_index("c")]).astype(jnp.int32)
    cw8 = jnp.pad(conv_w[0], ((0, 8 - CONV_W), (0, 0)))
    placed = _place_shards([w_in[0], w_rnn_out[0], w_attn_out[0], w_out[0]], chip_idx, "place_shards")
    win_g, cw_g = _gather_weights(placed[:1], cw8)
    late_send, late_recv, late_thru, late_token = _gather_late_start(placed[1:], win_g, "gather_late_start")
    cw_g = lax.dynamic_update_slice_in_dim(cw_g, cw8[None], chip, axis=0)
    conv_w_full = jnp.transpose(cw_g[:, 0:CONV_W, :], (1, 0, 2)).reshape(CONV_W, D_RNN)

    started = {}

    def start_reduction(tag, grads, grads_b16):
        got = _pair_exchange(grads_b16, "pair_exchange_" + tag)
        sums = [_pair_sum(g, o, chip_core, "pair_sum_%s_%d" % (tag, a)) for a, (g, o) in enumerate(zip(grads, got))]
        send_sems, recv_sems, p_thru, land_thru, token = _chip_exchange_start(
            [pb for _, pb in sums], "chip_exchange_start_" + tag)
        started[tag] = ([p for p, _ in sums], send_sems, recv_sems, p_thru, land_thru)
        return token

    def end_reduction(tag, after):
        psums, send_sems, recv_sems, p_thru, land_thru = started[tag]
        landed = _chip_exchange_wait(send_sems, recv_sems, p_thru, land_thru, after, "chip_exchange_wait_" + tag)
        return [_chip_sum(p, l, chip_core, "chip_sum_%s_%d" % (tag, a)) for a, (p, l) in enumerate(zip(psums, landed))]

    def out_weights(after):
        gathered = _gather_late_wait(late_send, late_recv, late_thru, after, "gather_late_wait")
        return [g.reshape(D_MODEL, D_MODEL) for g in gathered]

    grad_x, small = _local_grads(
        x[0], loss_target[0], pre_norm_g, win_g, b_gate, conv_w_full, conv_b, w_rg_a[0], b_rg_a[0], w_rg_x[0],
        b_rg_x[0], lru_lambda, attn_sinks[0], out_weights, late_token, post_norm_g,
        on_out_grads=lambda grads, grads_b16: start_reduction("out", grads, grads_b16),
        on_w_in_grad=lambda grad, grad_b16: start_reduction("in", [grad], [grad_b16]))

    small_chip = _small_pair_sum(small)
    small_send, small_recv, small_thru, small_land, small_token = _chip_exchange_start(
        list(small_chip), "small_exchange_start", blocked=False)

    halves = end_reduction("in", small_token) + end_reduction("out", small_token)
    gbig = dict(zip(BIG, _pair_share(halves)))

    grads, delta, new_m, new_v = {}, {}, {}, {}
    for n in BIG:
        grads[n] = gbig[n][None]
        d, nm, nv = _adamw(w[n][0], gbig[n], m[n][0], v[n][0], "adamw_" + n)
        delta[n], new_m[n], new_v[n] = d[None], nm[None], nv[None]

    small_landed = _chip_exchange_wait(small_send, small_recv, small_thru, small_land, delta[BIG[-1]],
                                       "small_exchange_wait", blocked=False)
    small_sum = _small_total(chip_idx, small_thru, small_landed)
    total_loss = small_sum[ROW_LOSS, 0]
    gsmall = _unpack_small(small_sum, D_RNN)
    conv_shard = D_RNN // N_CHIPS
    gsmall["conv_w"] = lax.dynamic_slice_in_dim(gsmall["conv_w"], chip * conv_shard, conv_shard, axis=2)
    pick = lambda t: {k: t[k] for k in gsmall}
    d, nm, nv = _adamw(_pack_small(pick(w)), _pack_small(gsmall), _pack_small(pick(m)), _pack_small(pick(v)),
                       "adamw_small")
    ud, um, uv = _unpack_small(d, conv_shard), _unpack_small(nm, conv_shard), _unpack_small(nv, conv_shard)
    for n in gsmall:
        grads[n] = gsmall[n].reshape(w[n].shape)
        delta[n] = ud[n].reshape(w[n].shape)
        new_m[n] = um[n].reshape(w[n].shape)
        new_v[n] = uv[n].reshape(w[n].shape)

    return (total_loss, grad_x[None], *[grads[n] for n in WEIGHTS], *[delta[n] for n in WEIGHTS],
            *[new_m[n] for n in WEIGHTS], *[new_v[n] for n in WEIGHTS])
```

```python
import functools
import math

import jax
import jax.numpy as jnp
from jax import lax
from jax.experimental import pallas as pl
from jax.experimental.pallas import tpu as pltpu

F32 = jnp.float32
BF16 = jnp.bfloat16

D_MODEL = 1024
D_RNN = 1024
RNN_BLOCKS = 16
RNN_BLOCK_W = 64
CONV_W = 4
LRU_C = 8.0
N_Q_HEADS = 16
N_KV_HEADS = 4
GROUP = 4
HEAD_DIM = 64
D_KV = 256
BLOCK = 128
ALIBI_MAX_BIAS = 8.0
EPS = 1e-6
D_IN = 6656
N_CHIPS = 4
W_IN_SHARD = D_IN // N_CHIPS
OUT_SHARD = D_MODEL // N_CHIPS
ADAM_LR = 0.001
ADAM_B1 = 0.9
ADAM_B2 = 0.999
ADAM_EPS = 1e-08
ADAM_WD = 0.01
ADAM_STEP = 10
NEG_BIG = -1e30
MIB = 1 << 20

COL_RNN_X = 0
COL_RNN_GATE = 4
COL_Q = 8
COL_K = 12
COL_V = 13
COL_ATTN_GATE = 14
COL_MERGE = 18

RNN_TILE = 256
RNN_CHUNK = 512
SMALL_ROWS = 152
SMALL_VECTOR_ROWS = 24
MESH = pl.DeviceIdType.MESH


def _sds(shape, dtype):
    return pltpu.HBM(shape, dtype)


def _params(sem=None, vmem_mib=None):
    kw = {}
    if sem is not None:
        kw["dimension_semantics"] = sem
    if vmem_mib is not None:
        kw["vmem_limit_bytes"] = vmem_mib * MIB
    return pltpu.CompilerParams(**kw)


def _hbm(*arrays):
    return [pltpu.with_memory_space_constraint(a, pltpu.HBM) for a in arrays]


def _dot(a, b):
    return jnp.dot(a, b, preferred_element_type=F32)


def _dot_nt(a, b):
    return lax.dot_general(a, b, (((1,), (1,)), ((), ())), preferred_element_type=F32)


def _dot_tn(a, b):
    return lax.dot_general(a, b, (((0,), (0,)), ((), ())), preferred_element_type=F32)


def _sigmoid(x):
    return 0.5 * jnp.tanh(0.5 * x) + 0.5


def _sigmoid_small(x):
    return 1.0 / (1.0 + jnp.exp(-x))


def _softplus(x):
    return jnp.maximum(x, 0.0) + jnp.log(1.0 + jnp.exp(-jnp.abs(x)))


def _one_minus_square(a, log_a):
    return -jnp.tanh(log_a) * (a * a + 1.0)


def _proj_fwd(x, g_pre, w_in_g):
    T = x.shape[0]
    tm = min(1024, T)

    def body(x_ref, g_ref, w_ref, proj_ref, ht_ref, h_s):
        @pl.when(pl.program_id(1) == 0)
        def _():
            xv = x_ref[...]
            rstd = lax.rsqrt(jnp.mean(xv * xv, axis=-1, keepdims=True) + EPS)
            hf = (xv * rstd) * g_ref[...]
            h_s[...] = hf.astype(BF16)
            ht_ref[...] = hf.T.astype(BF16)

        proj_ref[...] = _dot(h_s[...], w_ref[...]).astype(BF16)

    return pl.pallas_call(
        body,
        name="proj_fwd",
        grid=(T // tm, N_CHIPS),
        in_specs=[
            pl.BlockSpec((tm, D_MODEL), lambda i, j: (i, 0)),
            pl.BlockSpec((1, D_MODEL), lambda i, j: (0, 0)),
            pl.BlockSpec((None, D_MODEL, W_IN_SHARD), lambda i, j: (j, 0, 0)),
        ],
        out_specs=[
            pl.BlockSpec((tm, W_IN_SHARD), lambda i, j: (i, j)),
            pl.BlockSpec((D_MODEL, tm), lambda i, j: (0, i)),
        ],
        out_shape=[_sds((T, D_IN), BF16), _sds((D_MODEL, T), BF16)],
        scratch_shapes=[pltpu.VMEM((tm, D_MODEL), BF16)],
        compiler_params=_params(("parallel", "arbitrary"), 48),
    )(*_hbm(x, g_pre, w_in_g))


def _shift_down(x, tail, s, row):
    n = x.shape[0]
    xs = pltpu.roll(x, s, 0)
    tail_t = jnp.tile(pltpu.roll(tail, s, 0), (n // 8, 1))
    return jnp.where(row < s, tail_t, xs)


def _shift_up(x, head, s, row):
    n = x.shape[0]
    xs = pltpu.roll(x, n - s, 0)
    head_t = jnp.tile(pltpu.roll(head, 8 - s, 0), (n // 8, 1))
    return jnp.where(row >= n - s, head_t, xs)


def _conv_taps(x, tail, row):
    return [_shift_down(x, tail, 3, row), _shift_down(x, tail, 2, row), _shift_down(x, tail, 1, row), x]


def _rglru_gates(c, wa, wx, ba, bx, lam):
    cb = c.astype(BF16)
    r = _sigmoid_small(_dot(cb, wa) + ba)
    i = _sigmoid(_dot(cb, wx) + bx)
    log_a = (-LRU_C) * r * _softplus(-lam)
    a = jnp.exp(log_a)
    w = _one_minus_square(a, log_a)
    inv_mult = lax.rsqrt(w)
    return cb, r, i, a, w * inv_mult, inv_mult


SUBLANES = 8


def _scan_down(a, u, row):
    n = a.shape[0]
    s = 1
    while s < SUBLANES:
        a_sh = jnp.where(row >= s, pltpu.roll(a, s, 0), 1.0)
        u_sh = jnp.where(row >= s, pltpu.roll(u, s, 0), 0.0)
        u = a * u_sh + u
        a = a * a_sh
        s *= 2
    while s < n:
        u = jnp.concatenate([u[:s], a[s:] * u[:n - s] + u[s:]], axis=0)
        a = jnp.concatenate([a[:s], a[s:] * a[:n - s]], axis=0)
        s *= 2
    return a, u


def _scan_up(b, u, row):
    n = b.shape[0]
    s = 1
    while s < SUBLANES:
        b_sh = jnp.where(row < n - s, pltpu.roll(b, n - s, 0), 1.0)
        u_sh = jnp.where(row < n - s, pltpu.roll(u, n - s, 0), 0.0)
        u = b * u_sh + u
        b = b * b_sh
        s *= 2
    while s < n:
        u = jnp.concatenate([b[:n - s] * u[s:] + u[:n - s], u[n - s:]], axis=0)
        b = jnp.concatenate([b[:n - s] * b[s:], b[n - s:]], axis=0)
        s *= 2
    return b, u


LANES = 128


def _chunk_scan(a, u, a_s, u_s, hl_s, al_s, carry, reverse):
    n, width = a.shape
    groups = n // SUBLANES
    order = range(SUBLANES - 1, -1, -1) if reverse else range(SUBLANES)
    row = lax.broadcasted_iota(jnp.int32, (groups, LANES), 0)
    for l in range(width // LANES):
        lanes = slice(l * LANES, (l + 1) * LANES)
        a_l, u_l, hl_l, al_l = a_s.at[l], u_s.at[l], hl_s.at[l], al_s.at[l]
        a_l[...] = a[:, lanes]
        u_l[...] = u[:, lanes]
        h_loc = a_loc = None
        for r in order:
            rows = pl.ds(r, groups, stride=SUBLANES)
            a_r, u_r = a_l[rows, :], u_l[rows, :]
            h_loc, a_loc = (u_r, a_r) if h_loc is None else (a_r * h_loc + u_r, a_r * a_loc)
            hl_l[rows, :] = h_loc
            al_l[rows, :] = a_loc
        if reverse:
            a_cum, ends = _scan_up(a_loc, h_loc, row)
            ends = ends + a_cum * carry[:, lanes]
            enters = jnp.where(row == groups - 1, carry[:, lanes], pltpu.roll(ends, groups - 1, 0))
        else:
            a_cum, ends = _scan_down(a_loc, h_loc, row)
            ends = ends + a_cum * carry[:, lanes]
            enters = jnp.where(row == 0, carry[:, lanes], pltpu.roll(ends, 1, 0))
        for r in range(SUBLANES):
            rows = pl.ds(r, groups, stride=SUBLANES)
            hl_l[rows, :] = hl_l[rows, :] + al_l[rows, :] * enters
    return jnp.concatenate([hl_s[l] for l in range(width // LANES)], axis=1)


def _rnn_fwd(proj, conv_w, conv_b, wa_bd, wx_bd, b_a, b_x, lam, token):
    T = proj.shape[0]
    tc, ct = RNN_CHUNK, RNN_TILE
    nt = T // tc

    def body(x_ref, rg_ref, cw_ref, cb_ref, wa_ref, wx_ref, ba_ref, bx_ref, lam_ref, token_ref, h_ref, z_ref, c_ref,
             zt_ref, xtail, hcarry, a_s, u_s, hl_s, al_s):
        @pl.when(pl.program_id(1) == 0)
        def _():
            xtail[...] = jnp.zeros_like(xtail)
            hcarry[...] = jnp.zeros_like(hcarry)

        row = lax.broadcasted_iota(jnp.int32, (tc, ct), 0)
        x = x_ref[...].astype(F32)
        taps = _conv_taps(x, xtail[...], row)
        c = cb_ref[...] + cw_ref[pl.ds(0, 1), :] * taps[0]
        for k in range(1, CONV_W):
            c = c + cw_ref[pl.ds(k, 1), :] * taps[k]
        xtail[...] = x[tc - 8:, :]
        c_ref[...] = c
        _, _, i, a, mult, _ = _rglru_gates(c, wa_ref[...], wx_ref[...], ba_ref[...], bx_ref[...], lam_ref[...])
        h = _chunk_scan(a, mult * (i * c), a_s, u_s, hl_s, al_s, hcarry[...], reverse=False)
        h_ref[...] = h
        hcarry[...] = h_ref[pl.ds(tc - 1, 1), :]
        rg = rg_ref[...].astype(F32)
        z = h * (rg * _sigmoid(rg))
        z_ref[...] = z.astype(BF16)
        zt_ref[...] = z.T.astype(BF16)

    col = lambda off: (lambda j, t: (t, off + j))
    vec = pl.BlockSpec((1, ct), lambda j, t: (0, j))
    mat = pl.BlockSpec((None, ct, ct), lambda j, t: (j, 0, 0))
    return pl.pallas_call(
        body,
        name="rnn_fwd",
        grid=(D_RNN // ct, nt),
        in_specs=[
            pl.BlockSpec((tc, ct), col(COL_RNN_X)),
            pl.BlockSpec((tc, ct), col(COL_RNN_GATE)),
            pl.BlockSpec((CONV_W, ct), lambda j, t: (0, j)),
            vec, mat, mat, vec, vec, vec,
            pl.BlockSpec((8, 128), lambda j, t: (0, 0)),
        ],
        out_specs=[pl.BlockSpec((tc, ct), lambda j, t: (t, j))] * 3 + [pl.BlockSpec((ct, tc), lambda j, t: (j, t))],
        out_shape=[_sds((T, D_RNN), F32), _sds((T, D_RNN), BF16), _sds((T, D_RNN), F32), _sds((D_RNN, T), BF16)],
        scratch_shapes=[pltpu.VMEM((8, ct), F32), pltpu.VMEM((1, ct), F32)] + [
            pltpu.VMEM((ct // LANES, tc, LANES), F32)] * 4,
        compiler_params=_params(("parallel", "arbitrary"), 32),
    )(*_hbm(proj, proj, conv_w, conv_b, wa_bd, wx_bd, b_a, b_x, lam, token))


def _rnn_bwd(proj, conv, y_rnn, dz_rnn, conv_w, wa_bd, wx_bd, b_a, b_x, lam):
    T = proj.shape[0]
    tc, ct = RNN_CHUNK, RNN_TILE
    nt = T // tc
    hb = tc // 8

    def body(x_ref, c_ref, rg_ref, h_ref, hh_ref, dz_ref, cw_ref, wa_ref, wx_ref, ba_ref, bx_ref, lam_ref,
             dx_ref, drg_ref, dwa_ref, dwx_ref, sm_ref, lam_carry, a_carry, dc_head, b_s, dy_s, hl_s, al_s):
        t = pl.program_id(1)
        first_chunk = t == nt - 1

        @pl.when(t == 0)
        def _():
            lam_carry[...] = jnp.zeros_like(lam_carry)
            a_carry[...] = jnp.zeros_like(a_carry)
            dc_head[...] = jnp.zeros_like(dc_head)
            dwa_ref[...] = jnp.zeros_like(dwa_ref)
            dwx_ref[...] = jnp.zeros_like(dwx_ref)
            sm_ref[...] = jnp.zeros_like(sm_ref)

        row = lax.broadcasted_iota(jnp.int32, (tc, ct), 0)
        keep = jnp.where(first_chunk, 0.0, 1.0)
        x = x_ref[...].astype(F32)
        c = c_ref[...]
        lam = lam_ref[...]
        cb, r, i, a, mult, inv_mult = _rglru_gates(c, wa_ref[...], wx_ref[...], ba_ref[...], bx_ref[...], lam)
        h = h_ref[...]
        h_prev = _shift_down(h, hh_ref[...] * keep, 1, row)
        rg = rg_ref[...].astype(F32)
        dz = dz_ref[...]
        sg = _sigmoid(rg)
        drg_ref[...] = (dz * h * (sg * (1.0 + rg * (1.0 - sg)))).astype(BF16)
        dy = dz * (rg * sg)
        b = jnp.where(row >= tc - 1, a_carry[pl.ds(0, 1), :], pltpu.roll(a, tc - 1, 0))
        lt = _chunk_scan(b, dy, b_s, dy_s, hl_s, al_s, lam_carry[pl.ds(0, 1), :], reverse=True)
        lam_carry[...] = lt[0:8, :]
        a_carry[...] = a[0:8, :]
        ic = i * c
        dmult = lt * ic
        di = lt * mult * c
        dc = lt * mult * i
        dlog_a = a * (lt * h_prev - dmult * a * inv_mult)
        sp = _softplus(-lam)
        dpre_r = dlog_a * ((-LRU_C) * sp) * (r * (1.0 - r))
        dpre_i = di * (i * (1.0 - i))
        dlam_row = jnp.sum(dlog_a * r, axis=0, keepdims=True) * (LRU_C * _sigmoid(-lam))
        dpr_b = dpre_r.astype(BF16)
        dpi_b = dpre_i.astype(BF16)
        dwa_ref[...] += _dot_tn(cb, dpr_b)
        dwx_ref[...] += _dot_tn(cb, dpi_b)
        dc = dc + _dot_nt(dpr_b, wa_ref[...]) + _dot_nt(dpi_b, wx_ref[...])
        head = dc_head[...]
        dx = cw_ref[pl.ds(3, 1), :] * dc
        sm_ref[pl.ds(4 + 3, 1), :] += jnp.sum(dc * x, axis=0, keepdims=True)
        for m in range(1, CONV_W):
            up = _shift_up(dc, head, m, row)
            dx = dx + cw_ref[pl.ds(3 - m, 1), :] * up
            sm_ref[pl.ds(4 + 3 - m, 1), :] += jnp.sum(up * x, axis=0, keepdims=True)
        dx_ref[...] = dx.astype(BF16)
        dc_head[...] = dc[0:8, :]
        sm_ref[pl.ds(0, 1), :] += jnp.sum(dpre_r, axis=0, keepdims=True)
        sm_ref[pl.ds(1, 1), :] += jnp.sum(dpre_i, axis=0, keepdims=True)
        sm_ref[pl.ds(2, 1), :] += dlam_row
        sm_ref[pl.ds(3, 1), :] += jnp.sum(dc, axis=0, keepdims=True)

    rev = lambda off: (lambda j, t: (nt - 1 - t, off + j))
    halo = lambda off: (lambda j, t: (jnp.maximum((nt - 1 - t) * hb - 1, 0), off + j))
    vec = pl.BlockSpec((1, ct), lambda j, t: (0, j))
    mat = pl.BlockSpec((None, ct, ct), lambda j, t: (j, 0, 0))
    return pl.pallas_call(
        body,
        name="rnn_bwd",
        grid=(D_RNN // ct, nt),
        in_specs=[
            pl.BlockSpec((tc, ct), rev(COL_RNN_X)),
            pl.BlockSpec((tc, ct), rev(0)),
            pl.BlockSpec((tc, ct), rev(COL_RNN_GATE)),
            pl.BlockSpec((tc, ct), rev(0)),
            pl.BlockSpec((8, ct), halo(0)),
            pl.BlockSpec((tc, ct), rev(0)),
            pl.BlockSpec((CONV_W, ct), lambda j, t: (0, j)),
            mat, mat, vec, vec, vec,
        ],
        out_specs=[
            pl.BlockSpec((tc, ct), rev(0)),
            pl.BlockSpec((tc, ct), rev(0)),
            mat, mat,
            pl.BlockSpec((8, ct), lambda j, t: (0, j)),
        ],
        out_shape=[_sds((T, D_RNN), BF16), _sds((T, D_RNN), BF16), _sds((D_RNN // ct, ct, ct), F32),
                   _sds((D_RNN // ct, ct, ct), F32), _sds((8, D_RNN), F32)],
        scratch_shapes=[pltpu.VMEM((8, ct), F32)] * 3 + [pltpu.VMEM((ct // LANES, tc, LANES), F32)] * 4,
        compiler_params=_params(("parallel", "arbitrary"), 32),
    )(*_hbm(proj, conv, proj, y_rnn, y_rnn, dz_rnn, conv_w, wa_bd, wx_bd, b_a, b_x, lam))


def _attn_bias():
    qi = jnp.arange(BLOCK)[:, None]
    kj = jnp.arange(BLOCK)[None, :]
    dist_cur = (qi - kj).astype(F32)
    slopes = 2.0 ** (-ALIBI_MAX_BIAS * jnp.arange(1, N_Q_HEADS + 1, dtype=F32) / N_Q_HEADS)
    slopes = slopes[:, None, None]
    prev = jnp.where(kj > qi, -slopes * (dist_cur + float(BLOCK)), NEG_BIG)
    cur = jnp.where(kj <= qi, -slopes * dist_cur, NEG_BIG)
    later = jnp.concatenate([prev, cur], axis=-1)
    first = jnp.concatenate([jnp.full_like(prev, NEG_BIG), cur], axis=-1)
    return jnp.stack([first, later])


def _attn_exps(s_prev, s_cur, sink, bias):
    s_prev = s_prev + bias[:, 0:BLOCK]
    s_cur = s_cur + bias[:, BLOCK:2 * BLOCK]
    m = jnp.maximum(jnp.max(jnp.maximum(s_prev, s_cur), axis=-1, keepdims=True), sink)
    p_prev = jnp.exp(s_prev - m)
    p_cur = jnp.exp(s_cur - m)
    total = jnp.sum(p_prev + p_cur, axis=-1, keepdims=True) + jnp.exp(sink - m)
    return p_prev, p_cur, 1.0 / total, m + jnp.log(total)


def _attn_probs(s_prev, s_cur, sink, bias, lse):
    p_prev = jnp.exp((s_prev + bias[:, 0:BLOCK]) - lse)
    p_cur = jnp.exp((s_cur + bias[:, BLOCK:2 * BLOCK]) - lse)
    return p_prev, p_cur, jnp.exp(sink - lse)


def _stack_heads(ref_or_val, hk, dtype):
    parts = [ref_or_val[:, (GROUP * hk + g) * HEAD_DIM:(GROUP * hk + g + 1) * HEAD_DIM] for g in range(GROUP)]
    return jnp.concatenate(parts, axis=0).astype(dtype)


ATTN_SCALE = HEAD_DIM ** -0.5


def _bias_spec():
    return pl.BlockSpec((None, N_Q_HEADS, BLOCK, 2 * BLOCK), lambda i: (jnp.minimum(i, 1), 0, 0, 0))


def _attn_fwd(proj, sinks, bias):
    T = proj.shape[0]
    nb = T // BLOCK

    def body(sink_ref, bias_ref, q_ref, kp_ref, kc_ref, vp_ref, vc_ref, ag0_ref, ag1_ref, y_ref, z_ref, lse_ref):
        kvs = [slice(hk * HEAD_DIM, (hk + 1) * HEAD_DIM) for hk in range(N_KV_HEADS)]
        qgs = [(_stack_heads(q_ref, hk, F32) * ATTN_SCALE).astype(BF16) for hk in range(N_KV_HEADS)]
        s_prev = [_dot_nt(qgs[hk], kp_ref[:, kvs[hk]].astype(BF16)) for hk in range(N_KV_HEADS)]
        s_cur = [_dot_nt(qgs[hk], kc_ref[:, kvs[hk]].astype(BF16)) for hk in range(N_KV_HEADS)]
        for hk in range(N_KV_HEADS):
            pp, pc, invs = [], [], []
            for g in range(GROUP):
                h = GROUP * hk + g
                rows = slice(g * BLOCK, (g + 1) * BLOCK)
                p_prev, p_cur, inv, lse = _attn_exps(s_prev[hk][rows], s_cur[hk][rows], sink_ref[h], bias_ref[h])
                pp.append(p_prev.astype(BF16))
                pc.append(p_cur.astype(BF16))
                invs.append(inv)
                lse_ref[:, h:h + 1] = lse
            og = _dot(jnp.concatenate(pp, axis=0), vp_ref[:, kvs[hk]].astype(BF16)) + _dot(
                jnp.concatenate(pc, axis=0), vc_ref[:, kvs[hk]].astype(BF16))
            for g in range(GROUP):
                h = GROUP * hk + g
                y_ref[:, h * HEAD_DIM:(h + 1) * HEAD_DIM] = og[g * BLOCK:(g + 1) * BLOCK] * invs[g]
        ag = jnp.concatenate([ag0_ref[...], ag1_ref[...]], axis=1).astype(F32)
        z_ref[...] = (y_ref[...] * (ag * _sigmoid(ag))).astype(BF16)

    prev = lambda c: (lambda i: (jnp.maximum(i - 1, 0), c))
    cur = lambda c: (lambda i: (i, c))
    return pl.pallas_call(
        body,
        name="attn_fwd",
        grid=(nb,),
        in_specs=[
            pl.BlockSpec(memory_space=pltpu.SMEM),
            _bias_spec(),
            pl.BlockSpec((BLOCK, 1024), lambda i: (i, COL_Q // 4)),
            pl.BlockSpec((BLOCK, D_KV), prev(COL_K)),
            pl.BlockSpec((BLOCK, D_KV), cur(COL_K)),
            pl.BlockSpec((BLOCK, D_KV), prev(COL_V)),
            pl.BlockSpec((BLOCK, D_KV), cur(COL_V)),
            pl.BlockSpec((BLOCK, 512), lambda i: (i, COL_ATTN_GATE // 2)),
            pl.BlockSpec((BLOCK, 512), lambda i: (i, COL_ATTN_GATE // 2 + 1)),
        ],
        out_specs=[pl.BlockSpec((BLOCK, 1024), lambda i: (i, 0)), pl.BlockSpec((BLOCK, 1024), lambda i: (i, 0)),
                   pl.BlockSpec((BLOCK, N_Q_HEADS), lambda i: (i, 0))],
        out_shape=[_sds((T, 1024), F32), _sds((T, 1024), BF16), _sds((T, N_Q_HEADS), F32)],
        compiler_params=_params(("arbitrary",), 32),
    )(sinks, *_hbm(bias, proj, proj, proj, proj, proj, proj, proj))


def _attn_bwd(proj, y_attn, lse, dz_attn, sinks, bias, token):
    T = proj.shape[0]
    nb = T // BLOCK

    def body(sink_ref, bias_ref, q_ref, kp_ref, kc_ref, vp_ref, vc_ref, ag0_ref, ag1_ref, y_ref, lse_ref, dz_ref,
             token_ref, dq_ref, dk_ref, dv_ref, dag_ref, ds_ref, dy_s):
        i = pl.program_id(0)

        @pl.when(i == 0)
        def _():
            ds_ref[...] = jnp.zeros_like(ds_ref)

        lane = lax.broadcasted_iota(jnp.int32, (8, 128), 1)
        sub = lax.broadcasted_iota(jnp.int32, (8, 128), 0)
        ag = jnp.concatenate([ag0_ref[...], ag1_ref[...]], axis=1).astype(F32)
        dz = dz_ref[...]
        sg = _sigmoid(ag)
        dag_ref[...] = (dz * y_ref[...] * (sg * (1.0 + ag * (1.0 - sg)))).astype(BF16)
        dy_s[...] = dz * (ag * sg)
        r_cur = pl.multiple_of(i * BLOCK, BLOCK)
        r_prev = pl.multiple_of(jnp.maximum(i - 1, 0) * BLOCK, BLOCK)
        dk_cur, dv_cur, dk_prev, dv_prev = [], [], [], []
        ds_acc = jnp.zeros((8, 128), F32)
        for hk in range(N_KV_HEADS):
            ks = slice(hk * HEAD_DIM, (hk + 1) * HEAD_DIM)
            qg = (_stack_heads(q_ref, hk, F32) * ATTN_SCALE).astype(BF16)
            dog = _stack_heads(dy_s, hk, F32)
            og = _stack_heads(y_ref, hk, F32)
            dog_b = dog.astype(BF16)
            kp = kp_ref[:, ks].astype(BF16)
            kc = kc_ref[:, ks].astype(BF16)
            vp = vp_ref[:, ks].astype(BF16)
            vc = vc_ref[:, ks].astype(BF16)
            s_prev = _dot_nt(qg, kp)
            s_cur = _dot_nt(qg, kc)
            dp_prev = _dot_nt(dog_b, vp)
            dp_cur = _dot_nt(dog_b, vc)
            dvec = jnp.sum(dog * og, axis=-1, keepdims=True)
            pp, pc, dsp, dsc = [], [], [], []
            for g in range(GROUP):
                h = GROUP * hk + g
                rows = slice(g * BLOCK, (g + 1) * BLOCK)
                p_prev, p_cur, p_sink = _attn_probs(
                    s_prev[rows], s_cur[rows], sink_ref[h], bias_ref[h], lse_ref[:, h:h + 1])
                d_h = dvec[rows]
                pp.append(p_prev.astype(BF16))
                pc.append(p_cur.astype(BF16))
                dsp.append((p_prev * (dp_prev[rows] - d_h)).astype(BF16))
                dsc.append((p_cur * (dp_cur[rows] - d_h)).astype(BF16))
                dsink = -jnp.sum(p_sink * d_h, axis=0, keepdims=True)
                ds_acc = ds_acc + jnp.where(jnp.logical_and(lane == h, sub == 1), dsink, 0.0)
            pp = jnp.concatenate(pp, axis=0)
            pc = jnp.concatenate(pc, axis=0)
            dsp = jnp.concatenate(dsp, axis=0)
            dsc = jnp.concatenate(dsc, axis=0)
            dqg = (_dot(dsp, kp) + _dot(dsc, kc)) * ATTN_SCALE
            for g in range(GROUP):
                h = GROUP * hk + g
                dq_ref[:, h * HEAD_DIM:(h + 1) * HEAD_DIM] = dqg[g * BLOCK:(g + 1) * BLOCK].astype(BF16)
            dk_ref[pl.ds(r_cur, BLOCK), ks] = _dot_tn(dsc, qg)
            dv_ref[pl.ds(r_cur, BLOCK), ks] = _dot_tn(pc, dog_b)
            dk_prev.append(_dot_tn(dsp, qg))
            dv_prev.append(_dot_tn(pp, dog_b))
        ds_ref[:, 0:128] += ds_acc

        @pl.when(i > 0)
        def _():
            for hk in range(N_KV_HEADS):
                ks = slice(hk * HEAD_DIM, (hk + 1) * HEAD_DIM)
                dk_ref[pl.ds(r_prev, BLOCK), ks] += dk_prev[hk]
                dv_ref[pl.ds(r_prev, BLOCK), ks] += dv_prev[hk]

    prev = lambda c: (lambda i: (jnp.maximum(i - 1, 0), c))
    cur = lambda c: (lambda i: (i, c))
    blk = pl.BlockSpec((BLOCK, 1024), lambda i: (i, 0))
    whole = pl.BlockSpec((T, D_KV), lambda i: (0, 0))
    return pl.pallas_call(
        body,
        name="attn_bwd",
        grid=(nb,),
        in_specs=[
            pl.BlockSpec(memory_space=pltpu.SMEM),
            _bias_spec(),
            pl.BlockSpec((BLOCK, 1024), lambda i: (i, COL_Q // 4)),
            pl.BlockSpec((BLOCK, D_KV), prev(COL_K)),
            pl.BlockSpec((BLOCK, D_KV), cur(COL_K)),
            pl.BlockSpec((BLOCK, D_KV), prev(COL_V)),
            pl.BlockSpec((BLOCK, D_KV), cur(COL_V)),
            pl.BlockSpec((BLOCK, 512), lambda i: (i, COL_ATTN_GATE // 2)),
            pl.BlockSpec((BLOCK, 512), lambda i: (i, COL_ATTN_GATE // 2 + 1)),
            blk,
            pl.BlockSpec((BLOCK, N_Q_HEADS), lambda i: (i, 0)),
            blk,
            pl.BlockSpec((8, 128), lambda i: (0, 0)),
        ],
        out_specs=[blk, whole, whole, blk, pl.BlockSpec((8, 1024), lambda i: (0, 0))],
        out_shape=[_sds((T, 1024), BF16), _sds((T, D_KV), F32), _sds((T, D_KV), F32), _sds((T, 1024), BF16),
                   _sds((8, 1024), F32)],
        scratch_shapes=[pltpu.VMEM((BLOCK, 1024), F32)],
        compiler_params=_params(("arbitrary",), 48),
    )(sinks, *_hbm(bias, proj, proj, proj, proj, proj, proj, proj, y_attn, lse, dz_attn, token))


def _head(x, target, z_rnn, z_attn, proj, b_gate, g_post, w_rnn_out, w_attn_out, w_out):
    T = x.shape[0]
    tm = 256

    def body(x_ref, t_ref, zr_ref, za_ref, ml0_ref, ml1_ref, ml2_ref, ml3_ref, bg_ref, gp_ref, wr_ref, wa_ref, wo_ref,
             dyx_ref, dzr_ref, dza_ref, dml_ref, dout_ref, dbr_ref, dba_ref, mt_ref, zat_ref, sm_ref):
        @pl.when(pl.program_id(0) == 0)
        def _():
            sm_ref[...] = jnp.zeros_like(sm_ref)

        wr, wa, wo = wr_ref[...], wa_ref[...], wo_ref[...]
        br_rnn = _dot(zr_ref[...], wr)
        br_attn = _dot(za_ref[...], wa)
        zat_ref[...] = za_ref[...].astype(F32).T.astype(BF16)
        ml_rnn = jnp.concatenate([ml0_ref[...], ml1_ref[...]], axis=1).astype(F32)
        ml_attn = jnp.concatenate([ml2_ref[...], ml3_ref[...]], axis=1).astype(F32)
        g_rnn = _sigmoid(ml_rnn + bg_ref[:, 0:D_MODEL])
        g_attn = _sigmoid(ml_attn + bg_ref[:, D_MODEL:2 * D_MODEL])
        merged = g_rnn * br_rnn + g_attn * br_attn
        mb = merged.astype(BF16)
        mt_ref[...] = merged.T.astype(BF16)
        out = _dot(mb, wo)
        rstd = lax.rsqrt(jnp.mean(out * out, axis=-1, keepdims=True) + EPS)
        n = out * rstd
        gp = gp_ref[...]
        err = (x_ref[...] + n * gp) - t_ref[...]
        sm_ref[pl.ds(3, 1), :] += 0.5 * jnp.sum(jnp.mean(err * err, axis=-1, keepdims=True), axis=0, keepdims=True)
        dy = err * (1.0 / D_MODEL)
        dyx_ref[...] = dy
        sm_ref[pl.ds(0, 1), :] += jnp.sum(dy * n, axis=0, keepdims=True)
        dn = dy * gp
        dout = (rstd * (dn - n * jnp.mean(dn * n, axis=-1, keepdims=True))).astype(BF16)
        dout_ref[...] = dout
        dmerged = _dot_nt(dout, wo)
        dml_r = (dmerged * br_rnn) * (g_rnn * (1.0 - g_rnn))
        dml_a = (dmerged * br_attn) * (g_attn * (1.0 - g_attn))
        dml_ref[:, 0:D_MODEL] = dml_r.astype(BF16)
        dml_ref[:, D_MODEL:2 * D_MODEL] = dml_a.astype(BF16)
        sm_ref[pl.ds(1, 1), :] += jnp.sum(dml_r, axis=0, keepdims=True)
        sm_ref[pl.ds(2, 1), :] += jnp.sum(dml_a, axis=0, keepdims=True)
        dbr = (dmerged * g_rnn).astype(BF16)
        dba = (dmerged * g_attn).astype(BF16)
        dbr_ref[...] = dbr
        dba_ref[...] = dba
        dzr_ref[...] = _dot_nt(dbr, wr)
        dza_ref[...] = _dot_nt(dba, wa)

    tile = pl.BlockSpec((tm, D_MODEL), lambda i: (i, 0))
    wspec = pl.BlockSpec((D_MODEL, D_MODEL), lambda i: (0, 0))
    ml = lambda q: pl.BlockSpec((tm, 512), lambda i: (i, COL_MERGE // 2 + q))
    return pl.pallas_call(
        body,
        name="head",
        grid=(T // tm,),
        in_specs=[
            tile, tile, tile, tile,
            ml(0), ml(1), ml(2), ml(3),
            pl.BlockSpec((1, 2 * D_MODEL), lambda i: (0, 0)),
            pl.BlockSpec((1, D_MODEL), lambda i: (0, 0)),
            wspec, wspec, wspec,
        ],
        out_specs=[
            tile, tile, tile,
            pl.BlockSpec((tm, 2 * D_MODEL), lambda i: (i, 0)),
            tile, tile, tile,
            pl.BlockSpec((D_MODEL, tm), lambda i: (0, i)), pl.BlockSpec((D_MODEL, tm), lambda i: (0, i)),
            pl.BlockSpec((8, D_MODEL), lambda i: (0, 0)),
        ],
        out_shape=[
            _sds((T, D_MODEL), F32), _sds((T, D_MODEL), F32), _sds((T, D_MODEL), F32),
            _sds((T, 2 * D_MODEL), BF16),
            _sds((T, D_MODEL), BF16), _sds((T, D_MODEL), BF16), _sds((T, D_MODEL), BF16),
            _sds((D_MODEL, T), BF16), _sds((D_MODEL, T), BF16),
            _sds((8, D_MODEL), F32),
        ],
        compiler_params=_params(("arbitrary",), 56),
    )(*_hbm(x, target, z_rnn, z_attn, proj, proj, proj, proj, b_gate, g_post, w_rnn_out, w_attn_out, w_out))


def _matmul_t(at, b, name):
    M, T = at.shape
    N = b.shape[1]
    tk = min(512, T)
    nt = T // tk

    def body(a_ref, b_ref, o_ref, ob_ref):
        @pl.when(pl.program_id(0) == 0)
        def _():
            o_ref[...] = jnp.zeros_like(o_ref)

        o_ref[...] += _dot(a_ref[...], b_ref[...])

        @pl.when(pl.program_id(0) == nt - 1)
        def _():
            ob_ref[...] = o_ref[...].astype(BF16)

    whole = pl.BlockSpec((M, N), lambda t: (0, 0))
    return pl.pallas_call(
        body,
        name=name,
        grid=(nt,),
        in_specs=[pl.BlockSpec((M, tk), lambda t: (0, t)), pl.BlockSpec((tk, N), lambda t: (t, 0))],
        out_specs=[whole, whole],
        out_shape=[_sds((M, N), F32), _sds((M, N), BF16)],
        compiler_params=_params(("arbitrary",), 48),
    )(*_hbm(at, b))


DPROJ_WIDTHS = (D_RNN, D_RNN, 1024, D_KV, D_KV, 1024, 2 * D_MODEL)


def _dproj_segments():
    segs, start = [[] for _ in range(N_CHIPS)], 0
    for p, width in enumerate(DPROJ_WIDTHS):
        for c in range(N_CHIPS):
            lo, hi = max(start, c * W_IN_SHARD), min(start + width, (c + 1) * W_IN_SHARD)
            if lo < hi:
                segs[c].append((p, lo - start, hi - start, lo - c * W_IN_SHARD, hi - c * W_IN_SHARD))
        start += width
    return segs


def _dh_bwd(pieces, w_in_g, x, dyx, g_pre, token):
    T = x.shape[0]
    tm = min(512, T)
    n = len(pieces)
    segs = _dproj_segments()

    def body(*refs):
        p_refs, w_hbm, x_ref, dyx_ref, g_ref = refs[0:n], refs[n], refs[n + 1], refs[n + 2], refs[n + 3]
        gx_ref, dg_ref, w_ref, w_sems = refs[n + 5], refs[n + 6], refs[n + 7], refs[n + 8]
        first = pl.program_id(0) == 0
        w_copies = [pltpu.make_async_copy(w_hbm.at[c], w_ref.at[c], w_sems.at[c]) for c in range(N_CHIPS)]

        @pl.when(first)
        def _():
            for cp in w_copies:
                cp.start()
            dg_ref[...] = jnp.zeros_like(dg_ref)

        dh = None
        for c in range(N_CHIPS):
            pl.when(first)(w_copies[c].wait)
            for p, a0, a1, u0, u1 in segs[c]:
                part = _dot_nt(p_refs[p][:, a0:a1].astype(BF16), w_ref[c, :, u0:u1])
                dh = part if dh is None else dh + part
        xv = x_ref[...]
        rstd = lax.rsqrt(jnp.mean(xv * xv, axis=-1, keepdims=True) + EPS)
        nx = xv * rstd
        dhg = dh * g_ref[...]
        gx_ref[...] = dyx_ref[...] + rstd * (dhg - nx * jnp.mean(dhg * nx, axis=-1, keepdims=True))
        dg_ref[pl.ds(0, 1), :] += jnp.sum(dh * nx, axis=0, keepdims=True)

    tile = pl.BlockSpec((tm, D_MODEL), lambda i: (i, 0))
    return pl.pallas_call(
        body,
        name="dh_bwd",
        grid=(T // tm,),
        in_specs=[pl.BlockSpec((tm, w), lambda i: (i, 0)) for w in DPROJ_WIDTHS] + [
            ANY, tile, tile,
            pl.BlockSpec((1, D_MODEL), lambda i: (0, 0)),
            pl.BlockSpec((8, 128), lambda i: (0, 0)),
        ],
        out_specs=[tile, pl.BlockSpec((8, D_MODEL), lambda i: (0, 0))],
        out_shape=[_sds((T, D_MODEL), F32), _sds((8, D_MODEL), F32)],
        scratch_shapes=[pltpu.VMEM(w_in_g.shape, BF16), pltpu.SemaphoreType.DMA((N_CHIPS,))],
        compiler_params=_params(("arbitrary",), 56),
    )(*_hbm(*pieces, w_in_g, x, dyx, g_pre, token))


def _dw_in(ht, pieces):
    T = ht.shape[1]
    tk = min(512, T)
    nt = T // tk
    n = len(pieces)
    segs = _dproj_segments()

    def body(*refs):
        h_ref, p_refs, o_ref, ob_ref = refs[0], refs[1:n + 1], refs[n + 1], refs[n + 2]

        @pl.when(pl.program_id(1) == 0)
        def _():
            o_ref[...] = jnp.zeros_like(o_ref)

        for c in range(N_CHIPS):
            @pl.when(pl.program_id(0) == c)
            def _():
                for p, a0, a1, u0, u1 in segs[c]:
                    o_ref[:, u0:u1] += _dot(h_ref[...], p_refs[p][:, a0:a1].astype(BF16))

        @pl.when(pl.program_id(1) == nt - 1)
        def _():
            ob_ref[...] = o_ref[...].astype(BF16)

    def piece_spec(p):
        chips = [c for c in range(N_CHIPS) if any(s[0] == p for s in segs[c])]

        def index(c, t):
            used = functools.reduce(jnp.logical_or, [c == k for k in chips])
            return (jnp.where(used, t, 0), 0)

        return pl.BlockSpec((tk, DPROJ_WIDTHS[p]), index)

    return pl.pallas_call(
        body,
        name="dw_in",
        grid=(N_CHIPS, nt),
        in_specs=[pl.BlockSpec((D_MODEL, tk), lambda c, t: (0, t))] + [piece_spec(p) for p in range(n)],
        out_specs=[pl.BlockSpec((None, D_MODEL, W_IN_SHARD), lambda c, t: (c, 0, 0))] * 2,
        out_shape=[_sds((N_CHIPS, D_MODEL, W_IN_SHARD), F32), _sds((N_CHIPS, D_MODEL, W_IN_SHARD), BF16)],
        compiler_params=_params(("parallel", "arbitrary"), 56),
    )(*_hbm(ht, *pieces))


ELEMENTWISE_TILE_BYTES = MIB


def _row_tile(rows, cols):
    if rows * cols * 4 <= ELEMENTWISE_TILE_BYTES:
        return rows
    for t in (512, 256, 128, 64, 32, 16, 8):
        if rows % t == 0 and t * cols * 4 <= ELEMENTWISE_TILE_BYTES:
            return t
    return rows


def _pair_sum(g, got, chip_core, name):
    nch, R, C = g.shape
    h = R // 2
    tr = _row_tile(h, C)
    nt = h // tr

    def body(jc_ref, g_ref, got_ref, p_ref, pb_ref):
        s = g_ref[...] + got_ref[...].astype(F32)
        pb_ref[...] = s.astype(BF16)

        @pl.when(pl.program_id(1) == jc_ref[0])
        def _():
            p_ref[...] = s

    return pl.pallas_call(
        body,
        name=name,
        grid_spec=pltpu.PrefetchScalarGridSpec(
            num_scalar_prefetch=1,
            grid=(nt, nch),
            in_specs=[pl.BlockSpec((None, tr, C), lambda i, j, jc_ref: (j, jc_ref[1] * nt + i, 0)),
                      pl.BlockSpec((None, tr, C), lambda i, j, jc_ref: (j, i, 0))],
            out_specs=[pl.BlockSpec((tr, C), lambda i, j, jc_ref: (i, 0)),
                       pl.BlockSpec((None, tr, C), lambda i, j, jc_ref: (j, i, 0))],
        ),
        out_shape=[_sds((h, C), F32), _sds((nch, h, C), BF16)],
        compiler_params=_params(("parallel", "arbitrary"), 48),
    )(chip_core, *_hbm(g, got))


def _chip_sum(p, got, chip_core, name):
    h, C = p.shape
    tr = _row_tile(h, C)
    nt = h // tr

    def body(jc_ref, p_ref, g0_ref, g1_ref, g2_ref, o_ref):
        o_ref[...] = ((p_ref[...] + g0_ref[...].astype(F32)) + g1_ref[...].astype(F32)) + g2_ref[...].astype(F32)

    rel = lambda r: pl.BlockSpec((None, tr, C), lambda i, jc_ref: (r, i, 0))
    return pl.pallas_call(
        body,
        name=name,
        grid_spec=pltpu.PrefetchScalarGridSpec(
            num_scalar_prefetch=1,
            grid=(nt,),
            in_specs=[pl.BlockSpec((tr, C), lambda i, jc_ref: (i, 0)), rel(0), rel(1), rel(2)],
            out_specs=pl.BlockSpec((tr, C), lambda i, jc_ref: (jc_ref[1] * nt + i, 0)),
        ),
        out_shape=_sds((2 * h, C), F32),
        compiler_params=_params(("parallel",), 48),
    )(chip_core, *_hbm(p, got, got, got))


def _place_shards(shards, chip, name):
    n = len(shards)
    tiles = [_row_tile(s.shape[0], s.shape[1]) for s in shards]
    steps = max(s.shape[0] // t for s, t in zip(shards, tiles))
    tiles = [s.shape[0] // steps for s in shards]

    def body(j_ref, *refs):
        for a in range(n):
            refs[n + a][...] = refs[a][...].astype(BF16)

    return pl.pallas_call(
        body,
        name=name,
        grid_spec=pltpu.PrefetchScalarGridSpec(
            num_scalar_prefetch=1,
            grid=(steps,),
            in_specs=[pl.BlockSpec((t, s.shape[1]), lambda i, j_ref: (i, 0)) for s, t in zip(shards, tiles)],
            out_specs=[pl.BlockSpec((None, t, s.shape[1]), lambda i, j_ref: (j_ref[0], i, 0))
                       for s, t in zip(shards, tiles)],
        ),
        out_shape=[_sds((N_CHIPS,) + s.shape, BF16) for s in shards],
        compiler_params=_params(("parallel",), 48),
    )(chip, *_hbm(*shards))


def _adamw(w, g, m, v, name):
    R, C = w.shape
    tr = _row_tile(R, C)
    c1 = 1.0 - ADAM_B1 ** ADAM_STEP
    c2 = 1.0 - ADAM_B2 ** ADAM_STEP

    def body(w_ref, g_ref, m_ref, v_ref, d_ref, nm_ref, nv_ref):
        g = g_ref[...]
        nm = ADAM_B1 * m_ref[...] + (1.0 - ADAM_B1) * g
        nv = ADAM_B2 * v_ref[...] + (1.0 - ADAM_B2) * (g * g)
        nm_ref[...] = nm
        nv_ref[...] = nv
        d_ref[...] = (-ADAM_LR) * ((nm / c1) / (jnp.sqrt(nv / c2) + ADAM_EPS) + ADAM_WD * w_ref[...])

    spec = pl.BlockSpec((tr, C), lambda i: (i, 0))
    return pl.pallas_call(
        body, name=name, grid=(R // tr,), in_specs=[spec] * 4, out_specs=[spec] * 3,
        out_shape=[_sds((R, C), F32)] * 3, compiler_params=_params(("parallel",), 48),
    )(*_hbm(w, g, m, v))


def _place():
    return lax.axis_index("x"), lax.axis_index("y"), lax.axis_index("c")


def _chip_of(x, y, r):
    return (x ^ (r >> 1), y ^ (r & 1))


ANY = pl.BlockSpec(memory_space=pl.ANY)


def _gather_weights(placed, cw8):
    nbig = len(placed)
    halves = [s.shape[1] // 2 for s in placed]
    pieces = [max(1, h // 64) for h in halves]
    rows = [h // p for h, p in zip(halves, pieces)]
    order = [(a, q) for q in range(max(pieces)) for a in range(nbig) if q < pieces[a]]
    ici_sem = {(a, q, r): 3 * i + (r - 1) for i, (a, q) in enumerate(order) for r in (1, 2, 3)}
    cw_sem = {r: 3 * len(order) + (r - 1) for r in (1, 2, 3)}
    d2d_sem = {key: 3 * len(order) + 3 + k for key, k in ici_sem.items()}
    nsem = 6 * len(order) + 3

    def body(*refs):
        cw_ref, dsts, gcw_ref = refs[nbig], refs[nbig + 1:2 * nbig + 1], refs[2 * nbig + 1]
        send_sems, recv_sems = refs[2 * nbig + 2:]
        x, y, c = _place()
        j = 2 * x + y

        def piece_rows(a, q, core):
            return pl.ds(pl.multiple_of(core * halves[a] + q * rows[a], 16), rows[a])

        def ici(a, q, r):
            tx, ty = _chip_of(x, y, r)
            k = ici_sem[(a, q, r)]
            region = dsts[a].at[j, piece_rows(a, q, c), :]
            return pltpu.make_async_remote_copy(
                src_ref=region, dst_ref=region, send_sem=send_sems.at[k], recv_sem=recv_sems.at[k],
                device_id=(tx, ty, c), device_id_type=MESH)

        def ici_landed(a, q, r):
            tx, ty = _chip_of(x, y, r)
            k = ici_sem[(a, q, r)]
            region = dsts[a].at[2 * tx + ty, piece_rows(a, q, c), :]
            return pltpu.make_async_remote_copy(
                src_ref=region, dst_ref=region, send_sem=send_sems.at[k], recv_sem=recv_sems.at[k],
                device_id=(tx, ty, c), device_id_type=MESH)

        def d2d(a, q, r, core):
            tx, ty = _chip_of(x, y, r)
            k = d2d_sem[(a, q, r)]
            region = dsts[a].at[2 * tx + ty, piece_rows(a, q, core), :]
            return pltpu.make_async_remote_copy(
                src_ref=region, dst_ref=region, send_sem=send_sems.at[k], recv_sem=recv_sems.at[k],
                device_id=(x, y, 1 - c), device_id_type=MESH)

        def cw_copy(r):
            tx, ty = _chip_of(x, y, r)
            k = cw_sem[r]
            return pltpu.make_async_remote_copy(
                src_ref=cw_ref, dst_ref=gcw_ref.at[j], send_sem=send_sems.at[k], recv_sem=recv_sems.at[k],
                device_id=(tx, ty, c), device_id_type=MESH)

        def cw_landed(r):
            tx, ty = _chip_of(x, y, r)
            k = cw_sem[r]
            region = gcw_ref.at[2 * tx + ty]
            return pltpu.make_async_remote_copy(
                src_ref=region, dst_ref=region, send_sem=send_sems.at[k], recv_sem=recv_sems.at[k],
                device_id=(tx, ty, c), device_id_type=MESH)

        def relay(a, q, origin, to):
            ox, oy = _chip_of(x, y, origin)
            tx, ty = _chip_of(x, y, to)
            k = ici_sem[(a, q, 3)]
            region = dsts[a].at[2 * ox + oy, piece_rows(a, q, c), :]
            return pltpu.make_async_remote_copy(
                src_ref=region, dst_ref=region, send_sem=send_sems.at[k], recv_sem=recv_sems.at[k],
                device_id=(tx, ty, c), device_id_type=MESH)

        first = [ici(a, q, r) for (a, q) in order for r in (1, 2)] + [cw_copy(r) for r in (1, 2, 3)]
        for cp in first:
            cp.start()
        passed = []
        for (a, q) in order:
            for r in (1, 2):
                ici_landed(a, q, r).wait_recv()
                if q % 2 == r - 1:
                    cp = relay(a, q, r, 3 - r)
                    cp.start()
                    passed.append(cp)
                cp = d2d(a, q, r, c)
                cp.start()
                passed.append(cp)
        for (a, q) in order:
            ici_landed(a, q, 3).wait_recv()
            cp = d2d(a, q, 3, c)
            cp.start()
            passed.append(cp)
        for r in (1, 2, 3):
            cw_landed(r).wait_recv()
        for (a, q) in order:
            for r in (1, 2, 3):
                d2d(a, q, r, 1 - c).wait_recv()
        for cp in first + passed:
            cp.wait_send()

    return pl.pallas_call(
        body,
        name="gather_weights",
        in_specs=[ANY] * (nbig + 1),
        out_specs=[ANY] * (nbig + 1),
        out_shape=[_sds(s.shape, s.dtype) for s in placed] + [_sds((N_CHIPS,) + cw8.shape, cw8.dtype)],
        input_output_aliases={a: a for a in range(nbig)},
        scratch_shapes=[pltpu.SemaphoreType.DMA((nsem,)), pltpu.SemaphoreType.DMA((nsem,))],
    )(*placed, cw8)


def _gather_late_start(placed, after, name):
    n = len(placed)
    halves = [s.shape[1] // 2 for s in placed]

    def body(*refs):
        g_refs = refs[0:n]
        send_sems, recv_sems, token = refs[n + 1], refs[n + 2], refs[-1]
        x, y, c = _place()
        j = 2 * x + y
        for a in range(n):
            mine = g_refs[a].at[j, pl.ds(pl.multiple_of(c * halves[a], 16), halves[a]), :]
            for r in (1, 2, 3):
                tx, ty = _chip_of(x, y, r)
                for to_core in (0, 1):
                    k = ((a * 3 + (r - 1)) * 2 + c) * 2 + to_core
                    pltpu.make_async_remote_copy(
                        src_ref=mine, dst_ref=mine, send_sem=send_sems.at[k], recv_sem=recv_sems.at[k],
                        device_id=(tx, ty, to_core), device_id_type=MESH).start()
        token[...] = jnp.zeros_like(token)

    hbm = lambda t: pltpu.HBM(t.shape, t.dtype)
    keep = lambda t: pltpu.with_memory_space_constraint(t, pltpu.HBM)
    nsem = 12 * n
    outs = pl.pallas_call(
        body,
        name=name,
        in_specs=[HBM] * n + [ANY],
        out_specs=(SEM, SEM, *[HBM] * n, pl.BlockSpec(memory_space=pltpu.VMEM)),
        out_shape=(pltpu.SemaphoreType.DMA((nsem,)), pltpu.SemaphoreType.DMA((nsem,)), *[hbm(p) for p in placed],
                   jax.ShapeDtypeStruct((8, 128), F32)),
        input_output_aliases={i: 2 + i for i in range(n)},
        compiler_params=pltpu.CompilerParams(has_side_effects=DATAFLOW),
    )(*[keep(p) for p in placed], after)
    return outs[0], outs[1], list(outs[2:2 + n]), outs[-1]


def _gather_late_wait(send_sems, recv_sems, thru, after, name):
    n = len(thru)
    halves = [s.shape[1] // 2 for s in thru]

    def body(*refs):
        g_refs = refs[0:n]
        send_sems, recv_sems = refs[n], refs[n + 1]
        x, y, c = _place()
        j = 2 * x + y
        for a in range(n):
            mine = g_refs[a].at[j, pl.ds(pl.multiple_of(c * halves[a], 16), halves[a]), :]
            for r in (1, 2, 3):
                tx, ty = _chip_of(x, y, r)
                for other in (0, 1):
                    k_out = ((a * 3 + (r - 1)) * 2 + c) * 2 + other
                    pltpu.make_async_remote_copy(
                        src_ref=mine, dst_ref=mine, send_sem=send_sems.at[k_out], recv_sem=recv_sems.at[k_out],
                        device_id=(tx, ty, other), device_id_type=MESH).wait_send()
                    k_in = ((a * 3 + (r - 1)) * 2 + other) * 2 + c
                    theirs = g_refs[a].at[2 * tx + ty, pl.ds(other * halves[a], halves[a]), :]
                    pltpu.make_async_remote_copy(
                        src_ref=theirs, dst_ref=theirs, send_sem=send_sems.at[k_in], recv_sem=recv_sems.at[k_in],
                        device_id=(tx, ty, other), device_id_type=MESH).wait_recv()

    hbm = lambda t: pltpu.HBM(t.shape, t.dtype)
    outs = pl.pallas_call(
        body,
        name=name,
        in_specs=[HBM] * n + [SEM, SEM, ANY],
        out_specs=[HBM] * n,
        out_shape=[hbm(t) for t in thru],
        input_output_aliases={i: i for i in range(n)},
        compiler_params=pltpu.CompilerParams(has_side_effects=DATAFLOW),
    )(*thru, send_sems, recv_sems, after)
    return list(outs)


D2D_PIECE_ROWS = 64


def _pair_exchange(grads, name):
    n = len(grads)
    halves = [g.shape[1] // 2 for g in grads]

    def body(*refs):
        g_refs, got_refs = refs[0:n], refs[n:2 * n]
        send_sems, recv_sems = refs[2 * n:]
        x, y, c = _place()

        def copy(a, src, dst):
            return pltpu.make_async_remote_copy(
                src_ref=src, dst_ref=dst, send_sem=send_sems.at[a], recv_sem=recv_sems.at[a],
                device_id=(x, y, 1 - c), device_id_type=MESH)

        for a in range(n):
            for jj in range(N_CHIPS):
                for q in range(halves[a] // D2D_PIECE_ROWS):
                    src_rows = pl.ds(pl.multiple_of((1 - c) * halves[a] + q * D2D_PIECE_ROWS, 16), D2D_PIECE_ROWS)
                    dst_rows = pl.ds(q * D2D_PIECE_ROWS, D2D_PIECE_ROWS)
                    copy(a, g_refs[a].at[jj, src_rows, :], got_refs[a].at[jj, dst_rows, :]).start()
        for a in range(n):
            sent = g_refs[a].at[:, pl.ds(pl.multiple_of((1 - c) * halves[a], 16), halves[a]), :]
            copy(a, sent, got_refs[a]).wait()

    return pl.pallas_call(
        body,
        name=name,
        in_specs=[ANY] * n,
        out_specs=[ANY] * n,
        out_shape=[_sds((N_CHIPS, h, g.shape[2]), g.dtype) for g, h in zip(grads, halves)],
        scratch_shapes=[pltpu.SemaphoreType.DMA((n,)), pltpu.SemaphoreType.DMA((n,))],
    )(*grads)


HBM = pl.BlockSpec(memory_space=pltpu.HBM)
SEM = pl.BlockSpec(memory_space=pltpu.SEMAPHORE)
DATAFLOW = pltpu.SideEffectType.DATAFLOW_SIDE_EFFECTING


def _chip_copy(p_refs, land_refs, send_sems, recv_sems, a, r, blocked):
    x, y, c = _place()
    tx, ty = _chip_of(x, y, r)
    k = a * 3 + (r - 1)
    return pltpu.make_async_remote_copy(
        src_ref=p_refs[a].at[2 * tx + ty] if blocked else p_refs[a], dst_ref=land_refs[a].at[r - 1],
        send_sem=send_sems.at[k], recv_sem=recv_sems.at[k], device_id=(tx, ty, c), device_id_type=MESH)


def _chip_exchange_start(psums, name, blocked=True):
    n = len(psums)
    lands = [lax.empty((3,) + (p.shape[1:] if blocked else p.shape), p.dtype) for p in psums]

    def body(*refs):
        p_refs, land_refs = refs[0:n], refs[n:2 * n]
        send_sems, recv_sems, token = refs[2 * n], refs[2 * n + 1], refs[-1]
        for a in range(n):
            for r in (1, 2, 3):
                _chip_copy(p_refs, land_refs, send_sems, recv_sems, a, r, blocked).start()
        token[...] = jnp.zeros_like(token)

    hbm = lambda t: pltpu.HBM(t.shape, t.dtype)
    keep = lambda t: pltpu.with_memory_space_constraint(t, pltpu.HBM)
    outs = pl.pallas_call(
        body,
        name=name,
        in_specs=[HBM] * (2 * n),
        out_specs=(SEM, SEM, *[HBM] * (2 * n), pl.BlockSpec(memory_space=pltpu.VMEM)),
        out_shape=(pltpu.SemaphoreType.DMA((3 * n,)), pltpu.SemaphoreType.DMA((3 * n,)),
                   *[hbm(p) for p in psums], *[hbm(l) for l in lands], _sds((8, 128), F32)),
        input_output_aliases={i: 2 + i for i in range(2 * n)},
        compiler_params=pltpu.CompilerParams(has_side_effects=DATAFLOW),
    )(*[keep(p) for p in psums], *[keep(l) for l in lands])
    return outs[0], outs[1], list(outs[2:2 + n]), list(outs[2 + n:2 + 2 * n]), outs[-1]


def _chip_exchange_wait(send_sems, recv_sems, p_thru, land_thru, after, name, blocked=True):
    n = len(p_thru)

    def body(*refs):
        p_refs, land_refs = refs[0:n], refs[n:2 * n]
        send_sems, recv_sems = refs[2 * n], refs[2 * n + 1]
        for a in range(n):
            for r in (1, 2, 3):
                copy = _chip_copy(p_refs, land_refs, send_sems, recv_sems, a, r, blocked)
                copy.wait_send()
                copy.wait_recv()

    hbm = lambda t: pltpu.HBM(t.shape, t.dtype)
    outs = pl.pallas_call(
        body,
        name=name,
        in_specs=[HBM] * (2 * n) + [SEM, SEM, ANY],
        out_specs=[HBM] * (2 * n),
        out_shape=[hbm(p) for p in p_thru] + [hbm(l) for l in land_thru],
        input_output_aliases={i: i for i in range(2 * n)},
        compiler_params=pltpu.CompilerParams(has_side_effects=DATAFLOW),
    )(*p_thru, *land_thru, send_sems, recv_sems, after)
    return list(outs[n:2 * n])


def _pair_share(fulls):
    n = len(fulls)
    halves = [f.shape[0] // 2 for f in fulls]

    def body(*refs):
        full_refs = refs[n:2 * n]
        send_sems, recv_sems = refs[2 * n:]
        x, y, c = _place()

        def half_of(a, core):
            return full_refs[a].at[pl.ds(pl.multiple_of(core * halves[a], 8), halves[a]), :]

        def remote(a, src, dst):
            return pltpu.make_async_remote_copy(
                src_ref=src, dst_ref=dst, send_sem=send_sems.at[a], recv_sem=recv_sems.at[a],
                device_id=(x, y, 1 - c), device_id_type=MESH)

        for a in range(n):
            for q in range(halves[a] // D2D_PIECE_ROWS):
                piece = full_refs[a].at[
                    pl.ds(pl.multiple_of(c * halves[a] + q * D2D_PIECE_ROWS, 8), D2D_PIECE_ROWS), :]
                remote(a, piece, piece).start()
        for a in range(n):
            remote(a, half_of(a, c), half_of(a, c)).wait_send()
            remote(a, half_of(a, 1 - c), half_of(a, 1 - c)).wait_recv()

    return pl.pallas_call(
        body,
        name="pair_share",
        in_specs=[ANY] * n,
        out_specs=[ANY] * n,
        out_shape=[_sds(f.shape, F32) for f in fulls],
        input_output_aliases={a: a for a in range(n)},
        scratch_shapes=[pltpu.SemaphoreType.DMA((n,)), pltpu.SemaphoreType.DMA((n,))],
    )(*fulls)


def _small_pair_sum(s):
    R, C = s.shape
    V = SMALL_VECTOR_ROWS

    def body(s_ref, v_ref, m_ref, sib, send_sem, recv_sem):
        x, y, c = _place()

        def to_sib(src, dst):
            return pltpu.make_async_remote_copy(
                src_ref=src, dst_ref=dst, send_sem=send_sem, recv_sem=recv_sem,
                device_id=(x, y, 1 - c), device_id_type=MESH)

        for q in range(R // 8):
            to_sib(s_ref.at[pl.ds(8 * q, 8), :], sib.at[pl.ds(8 * q, 8), :]).start()
        to_sib(s_ref, sib).wait()
        v_ref[...] = s_ref[pl.ds(0, V), :] + sib[pl.ds(0, V), :]
        m_ref[...] = (s_ref[pl.ds(V, R - V), :] + sib[pl.ds(V, R - V), :]).astype(BF16)

    return pl.pallas_call(
        body,
        name="small_pair_sum",
        in_specs=[pl.BlockSpec(memory_space=pltpu.VMEM)],
        out_specs=[pl.BlockSpec(memory_space=pltpu.VMEM)] * 2,
        out_shape=[jax.ShapeDtypeStruct((V, C), F32), jax.ShapeDtypeStruct((R - V, C), BF16)],
        scratch_shapes=[pltpu.VMEM((R, C), F32), pltpu.SemaphoreType.DMA, pltpu.SemaphoreType.DMA],
    )(s)


def _small_total(chip, own, landed):
    V, C = own[0].shape
    M = own[1].shape[0]

    def body(j_ref, v_ref, m_ref, lv_ref, lm_ref, o_ref, chips_v, chips_m):
        j = j_ref[0]
        chips_v[j] = v_ref[...]
        chips_m[j] = m_ref[...]
        for r in (1, 2, 3):
            chips_v[j ^ r] = lv_ref[r - 1]
            chips_m[j ^ r] = lm_ref[r - 1]
        o_ref[pl.ds(0, V), :] = (chips_v[0] + chips_v[1]) + (chips_v[2] + chips_v[3])
        o_ref[pl.ds(V, M), :] = (chips_m[0].astype(F32) + chips_m[1].astype(F32)) + (
            chips_m[2].astype(F32) + chips_m[3].astype(F32))

    vmem = pl.BlockSpec(memory_space=pltpu.VMEM)
    return pl.pallas_call(
        body,
        name="small_total",
        in_specs=[pl.BlockSpec(memory_space=pltpu.SMEM), vmem, vmem, vmem, vmem],
        out_specs=vmem,
        out_shape=jax.ShapeDtypeStruct((V + M, C), F32),
        scratch_shapes=[pltpu.VMEM((N_CHIPS, V, C), F32), pltpu.VMEM((N_CHIPS, M, C), BF16)],
    )(chip, own[0], own[1], landed[0], landed[1])


def _block_diag(w):
    w4 = w.reshape(4, 4, RNN_BLOCK_W, RNN_BLOCK_W)
    eye = jnp.eye(4, dtype=w.dtype)
    return jnp.einsum("jaik,ab->jaibk", w4, eye).reshape(4, RNN_TILE, RNN_TILE)


def _block_diag_part(d):
    d5 = d.reshape(4, 4, RNN_BLOCK_W, 4, RNN_BLOCK_W)
    return jnp.stack([d5[:, a, :, a, :] for a in range(4)], axis=1).reshape(RNN_BLOCKS, RNN_BLOCK_W, RNN_BLOCK_W)


def _local_grads(x, target, g_pre, w_in_g, b_gate, conv_w, conv_b, w_rg_a, b_rg_a, w_rg_x, b_rg_x, lam, sinks,
                 out_weights, fwd_token, g_post, on_out_grads, on_w_in_grad):
    wa_bd = _block_diag(w_rg_a).astype(BF16)
    wx_bd = _block_diag(w_rg_x).astype(BF16)
    b_a = b_rg_a.reshape(1, D_RNN)
    b_x = b_rg_x.reshape(1, D_RNN)

    proj, ht = _proj_fwd(x, g_pre, w_in_g)
    y_rnn, z_rnn, conv, z_rnn_t = _rnn_fwd(proj, conv_w, conv_b, wa_bd, wx_bd, b_a, b_x, lam, fwd_token)
    bias = _attn_bias()
    y_attn, z_attn, lse = _attn_fwd(proj, sinks, bias)
    w_rnn_out, w_attn_out, w_out = out_weights(z_attn)
    dyx, dz_rnn, dz_attn, dml, dout, dbr_rnn, dbr_attn, merged_t, z_attn_t, head_small = _head(
        x, target, z_rnn, z_attn, proj, b_gate, g_post, w_rnn_out, w_attn_out, w_out)
    out_grads = [_matmul_t(z_rnn_t, dbr_rnn, "dw_rnn_out"), _matmul_t(z_attn_t, dbr_attn, "dw_attn_out"),
                 _matmul_t(merged_t, dout, "dw_out")]
    shard_rows = lambda d: d.reshape(N_CHIPS, OUT_SHARD, D_MODEL)
    token = on_out_grads([shard_rows(g) for g, _ in out_grads], [shard_rows(gb) for _, gb in out_grads])
    dq, dk, dv, dag, attn_small = _attn_bwd(proj, y_attn, lse, dz_attn, sinks, bias, token)
    drx, drg, dwa_t, dwx_t, rnn_small = _rnn_bwd(proj, conv, y_rnn, dz_rnn, conv_w, wa_bd, wx_bd, b_a, b_x, lam)
    dproj = [drx, drg, dq, dk, dv, dag, dml]
    token = on_w_in_grad(*_dw_in(ht, dproj))
    grad_x, dh_small = _dh_bwd(dproj, w_in_g, x, dyx, g_pre, token)
    small = jnp.concatenate([rnn_small, head_small, dh_small + attn_small,
                             _block_diag_part(dwa_t).reshape(64, 1024), _block_diag_part(dwx_t).reshape(64, 1024)], axis=0)
    return grad_x, small


ROW_LOSS = 11


def _rows8(parts):
    out = None
    for r, a in parts:
        p = jnp.pad(a, ((r, 8 - r - a.shape[0]), (0, 1024 - a.shape[1])))
        out = p if out is None else out + p
    return out


def _pack_small(p):
    g0 = _rows8([(0, p["b_rg_a"].reshape(1, 1024)), (1, p["b_rg_x"].reshape(1, 1024)), (2, p["lru_lambda"]),
                 (3, p["conv_b"]), (4, p["conv_w"][0])])
    g1 = _rows8([(0, p["post_norm_g"]), (1, p["b_gate"].reshape(2, 1024))])
    g2 = _rows8([(0, p["pre_norm_g"]), (1, p["attn_sinks"])])
    return jnp.concatenate([g0, g1, g2, p["w_rg_a"].reshape(64, 1024), p["w_rg_x"].reshape(64, 1024)], axis=0)


def _unpack_small(s, conv_cols):
    return {
        "b_rg_a": s[0:1].reshape(1, 16, 64), "b_rg_x": s[1:2].reshape(1, 16, 64), "lru_lambda": s[2:3],
        "conv_b": s[3:4], "conv_w": s[4:8, 0:conv_cols].reshape(1, CONV_W, conv_cols),
        "post_norm_g": s[8:9], "b_gate": s[9:11].reshape(1, 2048),
        "pre_norm_g": s[16:17], "attn_sinks": s[17:18, 0:N_Q_HEADS],
        "w_rg_a": s[24:88].reshape(1, 16, 64, 64), "w_rg_x": s[88:152].reshape(1, 16, 64, 64),
    }


WEIGHTS = ["pre_norm_g", "w_in", "b_gate", "conv_w", "conv_b", "w_rg_a", "b_rg_a", "w_rg_x", "b_rg_x", "lru_lambda",
           "attn_sinks", "w_rnn_out", "w_attn_out", "w_out", "post_norm_g"]
BIG = ["w_in", "w_rnn_out", "w_attn_out", "w_out"]


def kernel(x, pre_norm_g, w_in, b_gate, conv_w, conv_b, w_rg_a, b_rg_a, w_rg_x, b_rg_x, lru_lambda, attn_sinks, w_rnn_out, w_attn_out, w_out, post_norm_g, loss_target, m_pre_norm_g, m_w_in, m_b_gate, m_conv_w, m_conv_b, m_w_rg_a, m_b_rg_a, m_w_rg_x, m_b_rg_x, m_lru_lambda, m_attn_sinks, m_w_rnn_out, m_w_attn_out, m_w_out, m_post_norm_g, v_pre_norm_g, v_w_in, v_b_gate, v_conv_w, v_conv_b, v_w_rg_a, v_b_rg_a, v_w_rg_x, v_b_rg_x, v_lru_lambda, v_attn_sinks, v_w_rnn_out, v_w_attn_out, v_w_out, v_post_norm_g):
    w = dict(pre_norm_g=pre_norm_g, w_in=w_in, b_gate=b_gate, conv_w=conv_w, conv_b=conv_b, w_rg_a=w_rg_a,
             b_rg_a=b_rg_a, w_rg_x=w_rg_x, b_rg_x=b_rg_x, lru_lambda=lru_lambda, attn_sinks=attn_sinks,
             w_rnn_out=w_rnn_out, w_attn_out=w_attn_out, w_out=w_out, post_norm_g=post_norm_g)
    m = dict(pre_norm_g=m_pre_norm_g, w_in=m_w_in, b_gate=m_b_gate, conv_w=m_conv_w, conv_b=m_conv_b, w_rg_a=m_w_rg_a,
             b_rg_a=m_b_rg_a, w_rg_x=m_w_rg_x, b_rg_x=m_b_rg_x, lru_lambda=m_lru_lambda, attn_sinks=m_attn_sinks,
             w_rnn_out=m_w_rnn_out, w_attn_out=m_w_attn_out, w_out=m_w_out, post_norm_g=m_post_norm_g)
    v = dict(pre_norm_g=v_pre_norm_g, w_in=v_w_in, b_gate=v_b_gate, conv_w=v_conv_w, conv_b=v_conv_b, w_rg_a=v_w_rg_a,
             b_rg_a=v_b_rg_a, w_rg_x=v_w_rg_x, b_rg_x=v_b_rg_x, lru_lambda=v_lru_lambda, attn_sinks=v_attn_sinks,
             w_rnn_out=v_w_rnn_out, w_attn_out=v_w_attn_out, w_out=v_w_out, post_norm_g=v_post_norm_g)
    chip = 2 * lax.axis_index("x") + lax.axis_index("y")

    chip_idx = chip.astype(jnp.int32).reshape(1)
    chip_core = jnp.stack([chip, lax.axis_index("c")]).astype(jnp.int32)
    cw8 = jnp.pad(conv_w[0], ((0, 8 - CONV_W), (0, 0)))
    placed = _place_shards([w_in[0], w_rnn_out[0], w_attn_out[0], w_out[0]], chip_idx, "place_shards")
    win_g, cw_g = _gather_weights(placed[:1], cw8)
    late_send, late_recv, late_thru, late_token = _gather_late_start(placed[1:], win_g, "gather_late_start")
    cw_g = lax.dynamic_update_slice_in_dim(cw_g, cw8[None], chip, axis=0)
    conv_w_full = jnp.transpose(cw_g[:, 0:CONV_W, :], (1, 0, 2)).reshape(CONV_W, D_RNN)

    started = {}

    def start_reduction(tag, grads, grads_b16):
        got = _pair_exchange(grads_b16, "pair_exchange_" + tag)
        sums = [_pair_sum(g, o, chip_core, "pair_sum_%s_%d" % (tag, a)) for a, (g, o) in enumerate(zip(grads, got))]
        send_sems, recv_sems, p_thru, land_thru, token = _chip_exchange_start(
            [pb for _, pb in sums], "chip_exchange_start_" + tag)
        started[tag] = ([p for p, _ in sums], send_sems, recv_sems, p_thru, land_thru)
        return token

    def end_reduction(tag, after):
        psums, send_sems, recv_sems, p_thru, land_thru = started[tag]
        landed = _chip_exchange_wait(send_sems, recv_sems, p_thru, land_thru, after, "chip_exchange_wait_" + tag)
        return [_chip_sum(p, l, chip_core, "chip_sum_%s_%d" % (tag, a)) for a, (p, l) in enumerate(zip(psums, landed))]

    def out_weights(after):
        gathered = _gather_late_wait(late_send, late_recv, late_thru, after, "gather_late_wait")
        return [g.reshape(D_MODEL, D_MODEL) for g in gathered]

    grad_x, small = _local_grads(
        x[0], loss_target[0], pre_norm_g, win_g, b_gate, conv_w_full, conv_b, w_rg_a[0], b_rg_a[0], w_rg_x[0],
        b_rg_x[0], lru_lambda, attn_sinks[0], out_weights, late_token, post_norm_g,
        on_out_grads=lambda grads, grads_b16: start_reduction("out", grads, grads_b16),
        on_w_in_grad=lambda grad, grad_b16: start_reduction("in", [grad], [grad_b16]))

    small_chip = _small_pair_sum(small)
    small_send, small_recv, small_thru, small_land, small_token = _chip_exchange_start(
        list(small_chip), "small_exchange_start", blocked=False)

    halves = end_reduction("in", small_token) + end_reduction("out", small_token)
    gbig = dict(zip(BIG, _pair_share(halves)))

    grads, delta, new_m, new_v = {}, {}, {}, {}
    for n in BIG:
        grads[n] = gbig[n][None]
        d, nm, nv = _adamw(w[n][0], gbig[n], m[n][0], v[n][0], "adamw_" + n)
        delta[n], new_m[n], new_v[n] = d[None], nm[None], nv[None]

    small_landed = _chip_exchange_wait(small_send, small_recv, small_thru, small_land, delta[BIG[-1]],
                                       "small_exchange_wait", blocked=False)
    small_sum = _small_total(chip_idx, small_thru, small_landed)
    total_loss = small_sum[ROW_LOSS, 0]
    gsmall = _unpack_small(small_sum, D_RNN)
    conv_shard = D_RNN // N_CHIPS
    gsmall["conv_w"] = lax.dynamic_slice_in_dim(gsmall["conv_w"], chip * conv_shard, conv_shard, axis=2)
    pick = lambda t: {k: t[k] for k in gsmall}
    d, nm, nv = _adamw(_pack_small(pick(w)), _pack_small(gsmall), _pack_small(pick(m)), _pack_small(pick(v)),
                       "adamw_small")
    ud, um, uv = _unpack_small(d, conv_shard), _unpack_small(nm, conv_shard), _unpack_small(nv, conv_shard)
    for n in gsmall:
        grads[n] = gsmall[n].reshape(w[n].shape)
        delta[n] = ud[n].reshape(w[n].shape)
        new_m[n] = um[n].reshape(w[n].shape)
        new_v[n] = uv[n].reshape(w[n].shape)

    return (total_loss, grad_x[None], *[grads[n] for n in WEIGHTS], *[delta[n] for n in WEIGHTS],
            *[new_m[n] for n in WEIGHTS], *[new_v[n] for n in WEIGHTS])
```

```python
import functools
import math

import jax
import jax.numpy as jnp
from jax import lax
from jax.experimental import pallas as pl
from jax.experimental.pallas import tpu as pltpu

F32 = jnp.float32
BF16 = jnp.bfloat16

D_MODEL = 1024
D_RNN = 1024
RNN_BLOCKS = 16
RNN_BLOCK_W = 64
CONV_W = 4
LRU_C = 8.0
N_Q_HEADS = 16
N_KV_HEADS = 4
GROUP = 4
HEAD_DIM = 64
D_KV = 256
BLOCK = 128
ALIBI_MAX_BIAS = 8.0
EPS = 1e-6
D_IN = 6656
N_CHIPS = 4
W_IN_SHARD = D_IN // N_CHIPS
OUT_SHARD = D_MODEL // N_CHIPS
ADAM_LR = 0.001
ADAM_B1 = 0.9
ADAM_B2 = 0.999
ADAM_EPS = 1e-08
ADAM_WD = 0.01
ADAM_STEP = 10
NEG_BIG = -1e30
MIB = 1 << 20

COL_RNN_X = 0
COL_RNN_GATE = 4
COL_Q = 8
COL_K = 12
COL_V = 13
COL_ATTN_GATE = 14
COL_MERGE = 18

RNN_TILE = 256
RNN_CHUNK = 512
SMALL_ROWS = 152
SMALL_VECTOR_ROWS = 24
MESH = pl.DeviceIdType.MESH


def _sds(shape, dtype):
    return pltpu.HBM(shape, dtype)


def _params(sem=None, vmem_mib=None):
    kw = {}
    if sem is not None:
        kw["dimension_semantics"] = sem
    if vmem_mib is not None:
        kw["vmem_limit_bytes"] = vmem_mib * MIB
    return pltpu.CompilerParams(**kw)


def _hbm(*arrays):
    return [pltpu.with_memory_space_constraint(a, pltpu.HBM) for a in arrays]


def _dot(a, b):
    return jnp.dot(a, b, preferred_element_type=F32)


def _dot_nt(a, b):
    return lax.dot_general(a, b, (((1,), (1,)), ((), ())), preferred_element_type=F32)


def _dot_tn(a, b):
    return lax.dot_general(a, b, (((0,), (0,)), ((), ())), preferred_element_type=F32)


def _sigmoid(x):
    return 0.5 * jnp.tanh(0.5 * x) + 0.5


def _sigmoid_small(x):
    return 1.0 / (1.0 + jnp.exp(-x))


def _softplus(x):
    return jnp.maximum(x, 0.0) + jnp.log(1.0 + jnp.exp(-jnp.abs(x)))


def _one_minus_square(a, log_a):
    return -jnp.tanh(log_a) * (a * a + 1.0)


def _proj_fwd(x, g_pre, w_in_g):
    T = x.shape[0]
    tm = min(1024, T)

    def body(x_ref, g_ref, w_ref, proj_ref, ht_ref, h_s):
        @pl.when(pl.program_id(1) == 0)
        def _():
            xv = x_ref[...]
            rstd = lax.rsqrt(jnp.mean(xv * xv, axis=-1, keepdims=True) + EPS)
            hf = (xv * rstd) * g_ref[...]
            h_s[...] = hf.astype(BF16)
            ht_ref[...] = hf.T.astype(BF16)

        proj_ref[...] = _dot(h_s[...], w_ref[...]).astype(BF16)

    return pl.pallas_call(
        body,
        name="proj_fwd",
        grid=(T // tm, N_CHIPS),
        in_specs=[
            pl.BlockSpec((tm, D_MODEL), lambda i, j: (i, 0)),
            pl.BlockSpec((1, D_MODEL), lambda i, j: (0, 0)),
            pl.BlockSpec((None, D_MODEL, W_IN_SHARD), lambda i, j: (j, 0, 0)),
        ],
        out_specs=[
            pl.BlockSpec((tm, W_IN_SHARD), lambda i, j: (i, j)),
            pl.BlockSpec((D_MODEL, tm), lambda i, j: (0, i)),
        ],
        out_shape=[_sds((T, D_IN), BF16), _sds((D_MODEL, T), BF16)],
        scratch_shapes=[pltpu.VMEM((tm, D_MODEL), BF16)],
        compiler_params=_params(("parallel", "arbitrary"), 48),
    )(*_hbm(x, g_pre, w_in_g))


def _shift_down(x, tail, s, row):
    n = x.shape[0]
    xs = pltpu.roll(x, s, 0)
    tail_t = jnp.tile(pltpu.roll(tail, s, 0), (n // 8, 1))
    return jnp.where(row < s, tail_t, xs)


def _shift_up(x, head, s, row):
    n = x.shape[0]
    xs = pltpu.roll(x, n - s, 0)
    head_t = jnp.tile(pltpu.roll(head, 8 - s, 0), (n // 8, 1))
    return jnp.where(row >= n - s, head_t, xs)


def _conv_taps(x, tail, row):
    return [_shift_down(x, tail, 3, row), _shift_down(x, tail, 2, row), _shift_down(x, tail, 1, row), x]


def _rglru_gates(c, wa, wx, ba, bx, lam):
    cb = c.astype(BF16)
    r = _sigmoid_small(_dot(cb, wa) + ba)
    i = _sigmoid(_dot(cb, wx) + bx)
    log_a = (-LRU_C) * r * _softplus(-lam)
    a = jnp.exp(log_a)
    w = _one_minus_square(a, log_a)
    inv_mult = lax.rsqrt(w)
    return cb, r, i, a, w * inv_mult, inv_mult


SUBLANES = 8


def _scan_down(a, u, row):
    n = a.shape[0]
    s = 1
    while s < SUBLANES:
        a_sh = jnp.where(row >= s, pltpu.roll(a, s, 0), 1.0)
        u_sh = jnp.where(row >= s, pltpu.roll(u, s, 0), 0.0)
        u = a * u_sh + u
        a = a * a_sh
        s *= 2
    while s < n:
        u = jnp.concatenate([u[:s], a[s:] * u[:n - s] + u[s:]], axis=0)
        a = jnp.concatenate([a[:s], a[s:] * a[:n - s]], axis=0)
        s *= 2
    return a, u


def _scan_up(b, u, row):
    n = b.shape[0]
    s = 1
    while s < SUBLANES:
        b_sh = jnp.where(row < n - s, pltpu.roll(b, n - s, 0), 1.0)
        u_sh = jnp.where(row < n - s, pltpu.roll(u, n - s, 0), 0.0)
        u = b * u_sh + u
        b = b * b_sh
        s *= 2
    while s < n:
        u = jnp.concatenate([b[:n - s] * u[s:] + u[:n - s], u[n - s:]], axis=0)
        b = jnp.concatenate([b[:n - s] * b[s:], b[n - s:]], axis=0)
        s *= 2
    return b, u


LANES = 128


def _chunk_scan(a, u, a_s, u_s, hl_s, al_s, carry, reverse):
    n, width = a.shape
    groups = n // SUBLANES
    order = range(SUBLANES - 1, -1, -1) if reverse else range(SUBLANES)
    row = lax.broadcasted_iota(jnp.int32, (groups, LANES), 0)
    for l in range(width // LANES):
        lanes = slice(l * LANES, (l + 1) * LANES)
        a_l, u_l, hl_l, al_l = a_s.at[l], u_s.at[l], hl_s.at[l], al_s.at[l]
        a_l[...] = a[:, lanes]
        u_l[...] = u[:, lanes]
        h_loc = a_loc = None
        for r in order:
            rows = pl.ds(r, groups, stride=SUBLANES)
            a_r, u_r = a_l[rows, :], u_l[rows, :]
            h_loc, a_loc = (u_r, a_r) if h_loc is None else (a_r * h_loc + u_r, a_r * a_loc)
            hl_l[rows, :] = h_loc
            al_l[rows, :] = a_loc
        if reverse:
            a_cum, ends = _scan_up(a_loc, h_loc, row)
            ends = ends + a_cum * carry[:, lanes]
            enters = jnp.where(row == groups - 1, carry[:, lanes], pltpu.roll(ends, groups - 1, 0))
        else:
            a_cum, ends = _scan_down(a_loc, h_loc, row)
            ends = ends + a_cum * carry[:, lanes]
            enters = jnp.where(row == 0, carry[:, lanes], pltpu.roll(ends, 1, 0))
        for r in range(SUBLANES):
            rows = pl.ds(r, groups, stride=SUBLANES)
            hl_l[rows, :] = hl_l[rows, :] + al_l[rows, :] * enters
    return jnp.concatenate([hl_s[l] for l in range(width // LANES)], axis=1)


def _rnn_fwd(proj, conv_w, conv_b, wa_bd, wx_bd, b_a, b_x, lam, token):
    T = proj.shape[0]
    tc, ct = RNN_CHUNK, RNN_TILE
    nt = T // tc

    def body(x_ref, rg_ref, cw_ref, cb_ref, wa_ref, wx_ref, ba_ref, bx_ref, lam_ref, token_ref, h_ref, z_ref, c_ref,
             zt_ref, xtail, hcarry, a_s, u_s, hl_s, al_s):
        @pl.when(pl.program_id(1) == 0)
        def _():
            xtail[...] = jnp.zeros_like(xtail)
            hcarry[...] = jnp.zeros_like(hcarry)

        row = lax.broadcasted_iota(jnp.int32, (tc, ct), 0)
        x = x_ref[...].astype(F32)
        taps = _conv_taps(x, xtail[...], row)
        c = cb_ref[...] + cw_ref[pl.ds(0, 1), :] * taps[0]
        for k in range(1, CONV_W):
            c = c + cw_ref[pl.ds(k, 1), :] * taps[k]
        xtail[...] = x[tc - 8:, :]
        c_ref[...] = c
        _, _, i, a, mult, _ = _rglru_gates(c, wa_ref[...], wx_ref[...], ba_ref[...], bx_ref[...], lam_ref[...])
        h = _chunk_scan(a, mult * (i * c), a_s, u_s, hl_s, al_s, hcarry[...], reverse=False)
        h_ref[...] = h
        hcarry[...] = h_ref[pl.ds(tc - 1, 1), :]
        rg = rg_ref[...].astype(F32)
        z = h * (rg * _sigmoid(rg))
        z_ref[...] = z.astype(BF16)
        zt_ref[...] = z.T.astype(BF16)

    col = lambda off: (lambda j, t: (t, off + j))
    vec = pl.BlockSpec((1, ct), lambda j, t: (0, j))
    mat = pl.BlockSpec((None, ct, ct), lambda j, t: (j, 0, 0))
    return pl.pallas_call(
        body,
        name="rnn_fwd",
        grid=(D_RNN // ct, nt),
        in_specs=[
            pl.BlockSpec((tc, ct), col(COL_RNN_X)),
            pl.BlockSpec((tc, ct), col(COL_RNN_GATE)),
            pl.BlockSpec((CONV_W, ct), lambda j, t: (0, j)),
            vec, mat, mat, vec, vec, vec,
            pl.BlockSpec((8, 128), lambda j, t: (0, 0)),
        ],
        out_specs=[pl.BlockSpec((tc, ct), lambda j, t: (t, j))] * 3 + [pl.BlockSpec((ct, tc), lambda j, t: (j, t))],
        out_shape=[_sds((T, D_RNN), F32), _sds((T, D_RNN), BF16), _sds((T, D_RNN), F32), _sds((D_RNN, T), BF16)],
        scratch_shapes=[pltpu.VMEM((8, ct), F32), pltpu.VMEM((1, ct), F32)] + [
            pltpu.VMEM((ct // LANES, tc, LANES), F32)] * 4,
        compiler_params=_params(("parallel", "arbitrary"), 32),
    )(*_hbm(proj, proj, conv_w, conv_b, wa_bd, wx_bd, b_a, b_x, lam, token))


def _rnn_bwd(proj, conv, y_rnn, dz_rnn, conv_w, wa_bd, wx_bd, b_a, b_x, lam):
    T = proj.shape[0]
    tc, ct = RNN_CHUNK, RNN_TILE
    nt = T // tc
    hb = tc // 8

    def body(x_ref, c_ref, rg_ref, h_ref, hh_ref, dz_ref, cw_ref, wa_ref, wx_ref, ba_ref, bx_ref, lam_ref,
             dx_ref, drg_ref, dwa_ref, dwx_ref, sm_ref, lam_carry, a_carry, dc_head, b_s, dy_s, hl_s, al_s):
        t = pl.program_id(1)
        first_chunk = t == nt - 1

        @pl.when(t == 0)
        def _():
            lam_carry[...] = jnp.zeros_like(lam_carry)
            a_carry[...] = jnp.zeros_like(a_carry)
            dc_head[...] = jnp.zeros_like(dc_head)
            dwa_ref[...] = jnp.zeros_like(dwa_ref)
            dwx_ref[...] = jnp.zeros_like(dwx_ref)
            sm_ref[...] = jnp.zeros_like(sm_ref)

        row = lax.broadcasted_iota(jnp.int32, (tc, ct), 0)
        keep = jnp.where(first_chunk, 0.0, 1.0)
        x = x_ref[...].astype(F32)
        c = c_ref[...]
        lam = lam_ref[...]
        cb, r, i, a, mult, inv_mult = _rglru_gates(c, wa_ref[...], wx_ref[...], ba_ref[...], bx_ref[...], lam)
        h = h_ref[...]
        h_prev = _shift_down(h, hh_ref[...] * keep, 1, row)
        rg = rg_ref[...].astype(F32)
        dz = dz_ref[...]
        sg = _sigmoid(rg)
        drg_ref[...] = (dz * h * (sg * (1.0 + rg * (1.0 - sg)))).astype(BF16)
        dy = dz * (rg * sg)
        b = jnp.where(row >= tc - 1, a_carry[pl.ds(0, 1), :], pltpu.roll(a, tc - 1, 0))
        lt = _chunk_scan(b, dy, b_s, dy_s, hl_s, al_s, lam_carry[pl.ds(0, 1), :], reverse=True)
        lam_carry[...] = lt[0:8, :]
        a_carry[...] = a[0:8, :]
        ic = i * c
        dmult = lt * ic
        di = lt * mult * c
        dc = lt * mult * i
        dlog_a = a * (lt * h_prev - dmult * a * inv_mult)
        sp = _softplus(-lam)
        dpre_r = dlog_a * ((-LRU_C) * sp) * (r * (1.0 - r))
        dpre_i = di * (i * (1.0 - i))
        dlam_row = jnp.sum(dlog_a * r, axis=0, keepdims=True) * (LRU_C * _sigmoid(-lam))
        dpr_b = dpre_r.astype(BF16)
        dpi_b = dpre_i.astype(BF16)
        dwa_ref[...] += _dot_tn(cb, dpr_b)
        dwx_ref[...] += _dot_tn(cb, dpi_b)
        dc = dc + _dot_nt(dpr_b, wa_ref[...]) + _dot_nt(dpi_b, wx_ref[...])
        head = dc_head[...]
        dx = cw_ref[pl.ds(3, 1), :] * dc
        sm_ref[pl.ds(4 + 3, 1), :] += jnp.sum(dc * x, axis=0, keepdims=True)
        for m in range(1, CONV_W):
            up = _shift_up(dc, head, m, row)
            dx = dx + cw_ref[pl.ds(3 - m, 1), :] * up
            sm_ref[pl.ds(4 + 3 - m, 1), :] += jnp.sum(up * x, axis=0, keepdims=True)
        dx_ref[...] = dx.astype(BF16)
        dc_head[...] = dc[0:8, :]
        sm_ref[pl.ds(0, 1), :] += jnp.sum(dpre_r, axis=0, keepdims=True)
        sm_ref[pl.ds(1, 1), :] += jnp.sum(dpre_i, axis=0, keepdims=True)
        sm_ref[pl.ds(2, 1), :] += dlam_row
        sm_ref[pl.ds(3, 1), :] += jnp.sum(dc, axis=0, keepdims=True)

    rev = lambda off: (lambda j, t: (nt - 1 - t, off + j))
    halo = lambda off: (lambda j, t: (jnp.maximum((nt - 1 - t) * hb - 1, 0), off + j))
    vec = pl.BlockSpec((1, ct), lambda j, t: (0, j))
    mat = pl.BlockSpec((None, ct, ct), lambda j, t: (j, 0, 0))
    return pl.pallas_call(
        body,
        name="rnn_bwd",
        grid=(D_RNN // ct, nt),
        in_specs=[
            pl.BlockSpec((tc, ct), rev(COL_RNN_X)),
            pl.BlockSpec((tc, ct), rev(0)),
            pl.BlockSpec((tc, ct), rev(COL_RNN_GATE)),
            pl.BlockSpec((tc, ct), rev(0)),
            pl.BlockSpec((8, ct), halo(0)),
            pl.BlockSpec((tc, ct), rev(0)),
            pl.BlockSpec((CONV_W, ct), lambda j, t: (0, j)),
            mat, mat, vec, vec, vec,
        ],
        out_specs=[
            pl.BlockSpec((tc, ct), rev(0)),
            pl.BlockSpec((tc, ct), rev(0)),
            mat, mat,
            pl.BlockSpec((8, ct), lambda j, t: (0, j)),
        ],
        out_shape=[_sds((T, D_RNN), BF16), _sds((T, D_RNN), BF16), _sds((D_RNN // ct, ct, ct), F32),
                   _sds((D_RNN // ct, ct, ct), F32), _sds((8, D_RNN), F32)],
        scratch_shapes=[pltpu.VMEM((8, ct), F32)] * 3 + [pltpu.VMEM((ct // LANES, tc, LANES), F32)] * 4,
        compiler_params=_params(("parallel", "arbitrary"), 32),
    )(*_hbm(proj, conv, proj, y_rnn, y_rnn, dz_rnn, conv_w, wa_bd, wx_bd, b_a, b_x, lam))


def _attn_bias():
    qi = jnp.arange(BLOCK)[:, None]
    kj = jnp.arange(BLOCK)[None, :]
    dist_cur = (qi - kj).astype(F32)
    slopes = 2.0 ** (-ALIBI_MAX_BIAS * jnp.arange(1, N_Q_HEADS + 1, dtype=F32) / N_Q_HEADS)
    slopes = slopes[:, None, None]
    prev = jnp.where(kj > qi, -slopes * (dist_cur + float(BLOCK)), NEG_BIG)
    cur = jnp.where(kj <= qi, -slopes * dist_cur, NEG_BIG)
    later = jnp.concatenate([prev, cur], axis=-1)
    first = jnp.concatenate([jnp.full_like(prev, NEG_BIG), cur], axis=-1)
    return jnp.stack([first, later])


def _attn_exps(s_prev, s_cur, sink, bias):
    s_prev = s_prev + bias[:, 0:BLOCK]
    s_cur = s_cur + bias[:, BLOCK:2 * BLOCK]
    m = jnp.maximum(jnp.max(jnp.maximum(s_prev, s_cur), axis=-1, keepdims=True), sink)
    p_prev = jnp.exp(s_prev - m)
    p_cur = jnp.exp(s_cur - m)
    total = jnp.sum(p_prev + p_cur, axis=-1, keepdims=True) + jnp.exp(sink - m)
    return p_prev, p_cur, 1.0 / total, m + jnp.log(total)


def _attn_probs(s_prev, s_cur, sink, bias, lse):
    p_prev = jnp.exp((s_prev + bias[:, 0:BLOCK]) - lse)
    p_cur = jnp.exp((s_cur + bias[:, BLOCK:2 * BLOCK]) - lse)
    return p_prev, p_cur, jnp.exp(sink - lse)


def _stack_heads(ref_or_val, hk, dtype):
    parts = [ref_or_val[:, (GROUP * hk + g) * HEAD_DIM:(GROUP * hk + g + 1) * HEAD_DIM] for g in range(GROUP)]
    return jnp.concatenate(parts, axis=0).astype(dtype)


ATTN_SCALE = HEAD_DIM ** -0.5


def _bias_spec():
    return pl.BlockSpec((None, N_Q_HEADS, BLOCK, 2 * BLOCK), lambda i: (jnp.minimum(i, 1), 0, 0, 0))


def _attn_fwd(proj, sinks, bias):
    T = proj.shape[0]
    nb = T // BLOCK

    def body(sink_ref, bias_ref, q_ref, kp_ref, kc_ref, vp_ref, vc_ref, ag0_ref, ag1_ref, y_ref, z_ref, lse_ref):
        kvs = [slice(hk * HEAD_DIM, (hk + 1) * HEAD_DIM) for hk in range(N_KV_HEADS)]
        qgs = [(_stack_heads(q_ref, hk, F32) * ATTN_SCALE).astype(BF16) for hk in range(N_KV_HEADS)]
        s_prev = [_dot_nt(qgs[hk], kp_ref[:, kvs[hk]].astype(BF16)) for hk in range(N_KV_HEADS)]
        s_cur = [_dot_nt(qgs[hk], kc_ref[:, kvs[hk]].astype(BF16)) for hk in range(N_KV_HEADS)]
        for hk in range(N_KV_HEADS):
            pp, pc, invs = [], [], []
            for g in range(GROUP):
                h = GROUP * hk + g
                rows = slice(g * BLOCK, (g + 1) * BLOCK)
                p_prev, p_cur, inv, lse = _attn_exps(s_prev[hk][rows], s_cur[hk][rows], sink_ref[h], bias_ref[h])
                pp.append(p_prev.astype(BF16))
                pc.append(p_cur.astype(BF16))
                invs.append(inv)
                lse_ref[:, h:h + 1] = lse
            og = _dot(jnp.concatenate(pp, axis=0), vp_ref[:, kvs[hk]].astype(BF16)) + _dot(
                jnp.concatenate(pc, axis=0), vc_ref[:, kvs[hk]].astype(BF16))
            for g in range(GROUP):
                h = GROUP * hk + g
                y_ref[:, h * HEAD_DIM:(h + 1) * HEAD_DIM] = og[g * BLOCK:(g + 1) * BLOCK] * invs[g]
        ag = jnp.concatenate([ag0_ref[...], ag1_ref[...]], axis=1).astype(F32)
        z_ref[...] = (y_ref[...] * (ag * _sigmoid(ag))).astype(BF16)

    prev = lambda c: (lambda i: (jnp.maximum(i - 1, 0), c))
    cur = lambda c: (lambda i: (i, c))
    return pl.pallas_call(
        body,
        name="attn_fwd",
        grid=(nb,),
        in_specs=[
            pl.BlockSpec(memory_space=pltpu.SMEM),
            _bias_spec(),
            pl.BlockSpec((BLOCK, 1024), lambda i: (i, COL_Q // 4)),
            pl.BlockSpec((BLOCK, D_KV), prev(COL_K)),
            pl.BlockSpec((BLOCK, D_KV), cur(COL_K)),
            pl.BlockSpec((BLOCK, D_KV), prev(COL_V)),
            pl.BlockSpec((BLOCK, D_KV), cur(COL_V)),
            pl.BlockSpec((BLOCK, 512), lambda i: (i, COL_ATTN_GATE // 2)),
            pl.BlockSpec((BLOCK, 512), lambda i: (i, COL_ATTN_GATE // 2 + 1)),
        ],
        out_specs=[pl.BlockSpec((BLOCK, 1024), lambda i: (i, 0)), pl.BlockSpec((BLOCK, 1024), lambda i: (i, 0)),
                   pl.BlockSpec((BLOCK, N_Q_HEADS), lambda i: (i, 0))],
        out_shape=[_sds((T, 1024), F32), _sds((T, 1024), BF16), _sds((T, N_Q_HEADS), F32)],
        compiler_params=_params(("arbitrary",), 32),
    )(sinks, *_hbm(bias, proj, proj, proj, proj, proj, proj, proj))


def _attn_bwd(proj, y_attn, lse, dz_attn, sinks, bias, token):
    T = proj.shape[0]
    nb = T // BLOCK

    def body(sink_ref, bias_ref, q_ref, kp_ref, kc_ref, vp_ref, vc_ref, ag0_ref, ag1_ref, y_ref, lse_ref, dz_ref,
             token_ref, dq_ref, dk_ref, dv_ref, dag_ref, ds_ref, dy_s):
        i = pl.program_id(0)

        @pl.when(i == 0)
        def _():
            ds_ref[...] = jnp.zeros_like(ds_ref)

        lane = lax.broadcasted_iota(jnp.int32, (8, 128), 1)
        sub = lax.broadcasted_iota(jnp.int32, (8, 128), 0)
        ag = jnp.concatenate([ag0_ref[...], ag1_ref[...]], axis=1).astype(F32)
        dz = dz_ref[...]
        sg = _sigmoid(ag)
        dag_ref[...] = (dz * y_ref[...] * (sg * (1.0 + ag * (1.0 - sg)))).astype(BF16)
        dy_s[...] = dz * (ag * sg)
        r_cur = pl.multiple_of(i * BLOCK, BLOCK)
        r_prev = pl.multiple_of(jnp.maximum(i - 1, 0) * BLOCK, BLOCK)
        dk_cur, dv_cur, dk_prev, dv_prev = [], [], [], []
        ds_acc = jnp.zeros((8, 128), F32)
        for hk in range(N_KV_HEADS):
            ks = slice(hk * HEAD_DIM, (hk + 1) * HEAD_DIM)
            qg = (_stack_heads(q_ref, hk, F32) * ATTN_SCALE).astype(BF16)
            dog = _stack_heads(dy_s, hk, F32)
            og = _stack_heads(y_ref, hk, F32)
            dog_b = dog.astype(BF16)
            kp = kp_ref[:, ks].astype(BF16)
            kc = kc_ref[:, ks].astype(BF16)
            vp = vp_ref[:, ks].astype(BF16)
            vc = vc_ref[:, ks].astype(BF16)
            s_prev = _dot_nt(qg, kp)
            s_cur = _dot_nt(qg, kc)
            dp_prev = _dot_nt(dog_b, vp)
            dp_cur = _dot_nt(dog_b, vc)
            dvec = jnp.sum(dog * og, axis=-1, keepdims=True)
            pp, pc, dsp, dsc = [], [], [], []
            for g in range(GROUP):
                h = GROUP * hk + g
                rows = slice(g * BLOCK, (g + 1) * BLOCK)
                p_prev, p_cur, p_sink = _attn_probs(
                    s_prev[rows], s_cur[rows], sink_ref[h], bias_ref[h], lse_ref[:, h:h + 1])
                d_h = dvec[rows]
                pp.append(p_prev.astype(BF16))
                pc.append(p_cur.astype(BF16))
                dsp.append((p_prev * (dp_prev[rows] - d_h)).astype(BF16))
                dsc.append((p_cur * (dp_cur[rows] - d_h)).astype(BF16))
                dsink = -jnp.sum(p_sink * d_h, axis=0, keepdims=True)
                ds_acc = ds_acc + jnp.where(jnp.logical_and(lane == h, sub == 1), dsink, 0.0)
            pp = jnp.concatenate(pp, axis=0)
            pc = jnp.concatenate(pc, axis=0)
            dsp = jnp.concatenate(dsp, axis=0)
            dsc = jnp.concatenate(dsc, axis=0)
            dqg = (_dot(dsp, kp) + _dot(dsc, kc)) * ATTN_SCALE
            for g in range(GROUP):
                h = GROUP * hk + g
                dq_ref[:, h * HEAD_DIM:(h + 1) * HEAD_DIM] = dqg[g * BLOCK:(g + 1) * BLOCK].astype(BF16)
            dk_ref[pl.ds(r_cur, BLOCK), ks] = _dot_tn(dsc, qg)
            dv_ref[pl.ds(r_cur, BLOCK), ks] = _dot_tn(pc, dog_b)
            dk_prev.append(_dot_tn(dsp, qg))
            dv_prev.append(_dot_tn(pp, dog_b))
        ds_ref[:, 0:128] += ds_acc

        @pl.when(i > 0)
        def _():
            for hk in range(N_KV_HEADS):
                ks = slice(hk * HEAD_DIM, (hk + 1) * HEAD_DIM)
                dk_ref[pl.ds(r_prev, BLOCK), ks] += dk_prev[hk]
                dv_ref[pl.ds(r_prev, BLOCK), ks] += dv_prev[hk]

    prev = lambda c: (lambda i: (jnp.maximum(i - 1, 0), c))
    cur = lambda c: (lambda i: (i, c))
    blk = pl.BlockSpec((BLOCK, 1024), lambda i: (i, 0))
    whole = pl.BlockSpec((T, D_KV), lambda i: (0, 0))
    return pl.pallas_call(
        body,
        name="attn_bwd",
        grid=(nb,),
        in_specs=[
            pl.BlockSpec(memory_space=pltpu.SMEM),
            _bias_spec(),
            pl.BlockSpec((BLOCK, 1024), lambda i: (i, COL_Q // 4)),
            pl.BlockSpec((BLOCK, D_KV), prev(COL_K)),
            pl.BlockSpec((BLOCK, D_KV), cur(COL_K)),
            pl.BlockSpec((BLOCK, D_KV), prev(COL_V)),
            pl.BlockSpec((BLOCK, D_KV), cur(COL_V)),
            pl.BlockSpec((BLOCK, 512), lambda i: (i, COL_ATTN_GATE // 2)),
            pl.BlockSpec((BLOCK, 512), lambda i: (i, COL_ATTN_GATE // 2 + 1)),
            blk,
            pl.BlockSpec((BLOCK, N_Q_HEADS), lambda i: (i, 0)),
            blk,
            pl.BlockSpec((8, 128), lambda i: (0, 0)),
        ],
        out_specs=[blk, whole, whole, blk, pl.BlockSpec((8, 1024), lambda i: (0, 0))],
        out_shape=[_sds((T, 1024), BF16), _sds((T, D_KV), F32), _sds((T, D_KV), F32), _sds((T, 1024), BF16),
                   _sds((8, 1024), F32)],
        scratch_shapes=[pltpu.VMEM((BLOCK, 1024), F32)],
        compiler_params=_params(("arbitrary",), 48),
    )(sinks, *_hbm(bias, proj, proj, proj, proj, proj, proj, proj, y_attn, lse, dz_attn, token))


def _head(x, target, z_rnn, z_attn, proj, b_gate, g_post, w_rnn_out, w_attn_out, w_out):
    T = x.shape[0]
    tm = 256

    def body(x_ref, t_ref, zr_ref, za_ref, ml0_ref, ml1_ref, ml2_ref, ml3_ref, bg_ref, gp_ref, wr_ref, wa_ref, wo_ref,
             dyx_ref, dzr_ref, dza_ref, dml_ref, dout_ref, dbr_ref, dba_ref, mt_ref, zat_ref, sm_ref):
        @pl.when(pl.program_id(0) == 0)
        def _():
            sm_ref[...] = jnp.zeros_like(sm_ref)

        wr, wa, wo = wr_ref[...], wa_ref[...], wo_ref[...]
        br_rnn = _dot(zr_ref[...], wr)
        br_attn = _dot(za_ref[...], wa)
        zat_ref[...] = za_ref[...].astype(F32).T.astype(BF16)
        ml_rnn = jnp.concatenate([ml0_ref[...], ml1_ref[...]], axis=1).astype(F32)
        ml_attn = jnp.concatenate([ml2_ref[...], ml3_ref[...]], axis=1).astype(F32)
        g_rnn = _sigmoid(ml_rnn + bg_ref[:, 0:D_MODEL])
        g_attn = _sigmoid(ml_attn + bg_ref[:, D_MODEL:2 * D_MODEL])
        merged = g_rnn * br_rnn + g_attn * br_attn
        mb = merged.astype(BF16)
        mt_ref[...] = merged.T.astype(BF16)
        out = _dot(mb, wo)
        rstd = lax.rsqrt(jnp.mean(out * out, axis=-1, keepdims=True) + EPS)
        n = out * rstd
        gp = gp_ref[...]
        err = (x_ref[...] + n * gp) - t_ref[...]
        sm_ref[pl.ds(3, 1), :] += 0.5 * jnp.sum(jnp.mean(err * err, axis=-1, keepdims=True), axis=0, keepdims=True)
        dy = err * (1.0 / D_MODEL)
        dyx_ref[...] = dy
        sm_ref[pl.ds(0, 1), :] += jnp.sum(dy * n, axis=0, keepdims=True)
        dn = dy * gp
        dout = (rstd * (dn - n * jnp.mean(dn * n, axis=-1, keepdims=True))).astype(BF16)
        dout_ref[...] = dout
        dmerged = _dot_nt(dout, wo)
        dml_r = (dmerged * br_rnn) * (g_rnn * (1.0 - g_rnn))
        dml_a = (dmerged * br_attn) * (g_attn * (1.0 - g_attn))
        dml_ref[:, 0:D_MODEL] = dml_r.astype(BF16)
        dml_ref[:, D_MODEL:2 * D_MODEL] = dml_a.astype(BF16)
        sm_ref[pl.ds(1, 1), :] += jnp.sum(dml_r, axis=0, keepdims=True)
        sm_ref[pl.ds(2, 1), :] += jnp.sum(dml_a, axis=0, keepdims=True)
        dbr = (dmerged * g_rnn).astype(BF16)
        dba = (dmerged * g_attn).astype(BF16)
        dbr_ref[...] = dbr
        dba_ref[...] = dba
        dzr_ref[...] = _dot_nt(dbr, wr)
        dza_ref[...] = _dot_nt(dba, wa)

    tile = pl.BlockSpec((tm, D_MODEL), lambda i: (i, 0))
    wspec = pl.BlockSpec((D_MODEL, D_MODEL), lambda i: (0, 0))
    ml = lambda q: pl.BlockSpec((tm, 512), lambda i: (i, COL_MERGE // 2 + q))
    return pl.pallas_call(
        body,
        name="head",
        grid=(T // tm,),
        in_specs=[
            tile, tile, tile, tile,
            ml(0), ml(1), ml(2), ml(3),
            pl.BlockSpec((1, 2 * D_MODEL), lambda i: (0, 0)),
            pl.BlockSpec((1, D_MODEL), lambda i: (0, 0)),
            wspec, wspec, wspec,
        ],
        out_specs=[
            tile, tile, tile,
            pl.BlockSpec((tm, 2 * D_MODEL), lambda i: (i, 0)),
            tile, tile, tile,
            pl.BlockSpec((D_MODEL, tm), lambda i: (0, i)), pl.BlockSpec((D_MODEL, tm), lambda i: (0, i)),
            pl.BlockSpec((8, D_MODEL), lambda i: (0, 0)),
        ],
        out_shape=[
            _sds((T, D_MODEL), F32), _sds((T, D_MODEL), F32), _sds((T, D_MODEL), F32),
            _sds((T, 2 * D_MODEL), BF16),
            _sds((T, D_MODEL), BF16), _sds((T, D_MODEL), BF16), _sds((T, D_MODEL), BF16),
            _sds((D_MODEL, T), BF16), _sds((D_MODEL, T), BF16),
            _sds((8, D_MODEL), F32),
        ],
        compiler_params=_params(("arbitrary",), 56),
    )(*_hbm(x, target, z_rnn, z_attn, proj, proj, proj, proj, b_gate, g_post, w_rnn_out, w_attn_out, w_out))


def _matmul_t(at, b, name):
    M, T = at.shape
    N = b.shape[1]
    tk = min(1024, T)
    nt = T // tk

    def body(a_ref, b_ref, o_ref, ob_ref):
        @pl.when(pl.program_id(0) == 0)
        def _():
            o_ref[...] = jnp.zeros_like(o_ref)

        o_ref[...] += _dot(a_ref[...], b_ref[...])

        @pl.when(pl.program_id(0) == nt - 1)
        def _():
            ob_ref[...] = o_ref[...].astype(BF16)

    whole = pl.BlockSpec((M, N), lambda t: (0, 0))
    return pl.pallas_call(
        body,
        name=name,
        grid=(nt,),
        in_specs=[pl.BlockSpec((M, tk), lambda t: (0, t)), pl.BlockSpec((tk, N), lambda t: (t, 0))],
        out_specs=[whole, whole],
        out_shape=[_sds((M, N), F32), _sds((M, N), BF16)],
        compiler_params=_params(("arbitrary",), 48),
    )(*_hbm(at, b))


DPROJ_WIDTHS = (D_RNN, D_RNN, 1024, D_KV, D_KV, 1024, 2 * D_MODEL)


def _dproj_segments():
    segs, start = [[] for _ in range(N_CHIPS)], 0
    for p, width in enumerate(DPROJ_WIDTHS):
        for c in range(N_CHIPS):
            lo, hi = max(start, c * W_IN_SHARD), min(start + width, (c + 1) * W_IN_SHARD)
            if lo < hi:
                segs[c].append((p, lo - start, hi - start, lo - c * W_IN_SHARD, hi - c * W_IN_SHARD))
        start += width
    return segs


def _dh_bwd(pieces, w_in_g, x, dyx, g_pre, token):
    T = x.shape[0]
    tm = min(512, T)
    n = len(pieces)
    segs = _dproj_segments()

    def body(*refs):
        p_refs, w_hbm, x_ref, dyx_ref, g_ref = refs[0:n], refs[n], refs[n + 1], refs[n + 2], refs[n + 3]
        gx_ref, dg_ref, w_ref, w_sems = refs[n + 5], refs[n + 6], refs[n + 7], refs[n + 8]
        first = pl.program_id(0) == 0
        w_copies = [pltpu.make_async_copy(w_hbm.at[c], w_ref.at[c], w_sems.at[c]) for c in range(N_CHIPS)]

        @pl.when(first)
        def _():
            for cp in w_copies:
                cp.start()
            dg_ref[...] = jnp.zeros_like(dg_ref)

        dh = None
        for c in range(N_CHIPS):
            pl.when(first)(w_copies[c].wait)
            for p, a0, a1, u0, u1 in segs[c]:
                part = _dot_nt(p_refs[p][:, a0:a1].astype(BF16), w_ref[c, :, u0:u1])
                dh = part if dh is None else dh + part
        xv = x_ref[...]
        rstd = lax.rsqrt(jnp.mean(xv * xv, axis=-1, keepdims=True) + EPS)
        nx = xv * rstd
        dhg = dh * g_ref[...]
        gx_ref[...] = dyx_ref[...] + rstd * (dhg - nx * jnp.mean(dhg * nx, axis=-1, keepdims=True))
        dg_ref[pl.ds(0, 1), :] += jnp.sum(dh * nx, axis=0, keepdims=True)

    tile = pl.BlockSpec((tm, D_MODEL), lambda i: (i, 0))
    return pl.pallas_call(
        body,
        name="dh_bwd",
        grid=(T // tm,),
        in_specs=[pl.BlockSpec((tm, w), lambda i: (i, 0)) for w in DPROJ_WIDTHS] + [
            ANY, tile, tile,
            pl.BlockSpec((1, D_MODEL), lambda i: (0, 0)),
            pl.BlockSpec((8, 128), lambda i: (0, 0)),
        ],
        out_specs=[tile, pl.BlockSpec((8, D_MODEL), lambda i: (0, 0))],
        out_shape=[_sds((T, D_MODEL), F32), _sds((8, D_MODEL), F32)],
        scratch_shapes=[pltpu.VMEM(w_in_g.shape, BF16), pltpu.SemaphoreType.DMA((N_CHIPS,))],
        compiler_params=_params(("arbitrary",), 56),
    )(*_hbm(*pieces, w_in_g, x, dyx, g_pre, token))


def _dw_in(ht, pieces):
    T = ht.shape[1]
    tk = min(512, T)
    nt = T // tk
    n = len(pieces)
    segs = _dproj_segments()

    def body(*refs):
        h_ref, p_refs, o_ref, ob_ref = refs[0], refs[1:n + 1], refs[n + 1], refs[n + 2]

        @pl.when(pl.program_id(1) == 0)
        def _():
            o_ref[...] = jnp.zeros_like(o_ref)

        for c in range(N_CHIPS):
            @pl.when(pl.program_id(0) == c)
            def _():
                for p, a0, a1, u0, u1 in segs[c]:
                    o_ref[:, u0:u1] += _dot(h_ref[...], p_refs[p][:, a0:a1].astype(BF16))

        @pl.when(pl.program_id(1) == nt - 1)
        def _():
            ob_ref[...] = o_ref[...].astype(BF16)

    def piece_spec(p):
        chips = [c for c in range(N_CHIPS) if any(s[0] == p for s in segs[c])]

        def index(c, t):
            used = functools.reduce(jnp.logical_or, [c == k for k in chips])
            return (jnp.where(used, t, 0), 0)

        return pl.BlockSpec((tk, DPROJ_WIDTHS[p]), index)

    return pl.pallas_call(
        body,
        name="dw_in",
        grid=(N_CHIPS, nt),
        in_specs=[pl.BlockSpec((D_MODEL, tk), lambda c, t: (0, t))] + [piece_spec(p) for p in range(n)],
        out_specs=[pl.BlockSpec((None, D_MODEL, W_IN_SHARD), lambda c, t: (c, 0, 0))] * 2,
        out_shape=[_sds((N_CHIPS, D_MODEL, W_IN_SHARD), F32), _sds((N_CHIPS, D_MODEL, W_IN_SHARD), BF16)],
        compiler_params=_params(("parallel", "arbitrary"), 56),
    )(*_hbm(ht, *pieces))


ELEMENTWISE_TILE_BYTES = MIB


def _row_tile(rows, cols):
    if rows * cols * 4 <= ELEMENTWISE_TILE_BYTES:
        return rows
    for t in (512, 256, 128, 64, 32, 16, 8):
        if rows % t == 0 and t * cols * 4 <= ELEMENTWISE_TILE_BYTES:
            return t
    return rows


def _pair_sum(g, got, chip_core, name):
    nch, R, C = g.shape
    h = R // 2
    tr = _row_tile(h, C)
    nt = h // tr

    def body(jc_ref, g_ref, got_ref, p_ref, pb_ref):
        s = g_ref[...] + got_ref[...].astype(F32)
        pb_ref[...] = s.astype(BF16)

        @pl.when(pl.program_id(1) == jc_ref[0])
        def _():
            p_ref[...] = s

    return pl.pallas_call(
        body,
        name=name,
        grid_spec=pltpu.PrefetchScalarGridSpec(
            num_scalar_prefetch=1,
            grid=(nt, nch),
            in_specs=[pl.BlockSpec((None, tr, C), lambda i, j, jc_ref: (j, jc_ref[1] * nt + i, 0)),
                      pl.BlockSpec((None, tr, C), lambda i, j, jc_ref: (j, i, 0))],
            out_specs=[pl.BlockSpec((tr, C), lambda i, j, jc_ref: (i, 0)),
                       pl.BlockSpec((None, tr, C), lambda i, j, jc_ref: (j, i, 0))],
        ),
        out_shape=[_sds((h, C), F32), _sds((nch, h, C), BF16)],
        compiler_params=_params(("parallel", "arbitrary"), 48),
    )(chip_core, *_hbm(g, got))


def _chip_sum(p, got, chip_core, name):
    h, C = p.shape
    tr = _row_tile(h, C)
    nt = h // tr

    def body(jc_ref, p_ref, g0_ref, g1_ref, g2_ref, o_ref):
        o_ref[...] = ((p_ref[...] + g0_ref[...].astype(F32)) + g1_ref[...].astype(F32)) + g2_ref[...].astype(F32)

    rel = lambda r: pl.BlockSpec((None, tr, C), lambda i, jc_ref: (r, i, 0))
    return pl.pallas_call(
        body,
        name=name,
        grid_spec=pltpu.PrefetchScalarGridSpec(
            num_scalar_prefetch=1,
            grid=(nt,),
            in_specs=[pl.BlockSpec((tr, C), lambda i, jc_ref: (i, 0)), rel(0), rel(1), rel(2)],
            out_specs=pl.BlockSpec((tr, C), lambda i, jc_ref: (jc_ref[1] * nt + i, 0)),
        ),
        out_shape=_sds((2 * h, C), F32),
        compiler_params=_params(("parallel",), 48),
    )(chip_core, *_hbm(p, got, got, got))


def _place_shards(shards, chip, name):
    n = len(shards)
    tiles = [_row_tile(s.shape[0], s.shape[1]) for s in shards]
    steps = max(s.shape[0] // t for s, t in zip(shards, tiles))
    tiles = [s.shape[0] // steps for s in shards]

    def body(j_ref, *refs):
        for a in range(n):
            refs[n + a][...] = refs[a][...].astype(BF16)

    return pl.pallas_call(
        body,
        name=name,
        grid_spec=pltpu.PrefetchScalarGridSpec(
            num_scalar_prefetch=1,
            grid=(steps,),
            in_specs=[pl.BlockSpec((t, s.shape[1]), lambda i, j_ref: (i, 0)) for s, t in zip(shards, tiles)],
            out_specs=[pl.BlockSpec((None, t, s.shape[1]), lambda i, j_ref: (j_ref[0], i, 0))
                       for s, t in zip(shards, tiles)],
        ),
        out_shape=[_sds((N_CHIPS,) + s.shape, BF16) for s in shards],
        compiler_params=_params(("parallel",), 48),
    )(chip, *_hbm(*shards))


def _adamw(w, g, m, v, name):
    R, C = w.shape
    tr = _row_tile(R, C)
    c1 = 1.0 - ADAM_B1 ** ADAM_STEP
    c2 = 1.0 - ADAM_B2 ** ADAM_STEP

    def body(w_ref, g_ref, m_ref, v_ref, d_ref, nm_ref, nv_ref):
        g = g_ref[...]
        nm = ADAM_B1 * m_ref[...] + (1.0 - ADAM_B1) * g
        nv = ADAM_B2 * v_ref[...] + (1.0 - ADAM_B2) * (g * g)
        nm_ref[...] = nm
        nv_ref[...] = nv
        d_ref[...] = (-ADAM_LR) * ((nm / c1) / (jnp.sqrt(nv / c2) + ADAM_EPS) + ADAM_WD * w_ref[...])

    spec = pl.BlockSpec((tr, C), lambda i: (i, 0))
    return pl.pallas_call(
        body, name=name, grid=(R // tr,), in_specs=[spec] * 4, out_specs=[spec] * 3,
        out_shape=[_sds((R, C), F32)] * 3, compiler_params=_params(("parallel",), 48),
    )(*_hbm(w, g, m, v))


def _place():
    return lax.axis_index("x"), lax.axis_index("y"), lax.axis_index("c")


def _chip_of(x, y, r):
    return (x ^ (r >> 1), y ^ (r & 1))


ANY = pl.BlockSpec(memory_space=pl.ANY)


def _gather_weights(placed, cw8):
    nbig = len(placed)
    halves = [s.shape[1] // 2 for s in placed]
    pieces = [max(1, h // 64) for h in halves]
    rows = [h // p for h, p in zip(halves, pieces)]
    order = [(a, q) for q in range(max(pieces)) for a in range(nbig) if q < pieces[a]]
    ici_sem = {(a, q, r): 3 * i + (r - 1) for i, (a, q) in enumerate(order) for r in (1, 2, 3)}
    cw_sem = {r: 3 * len(order) + (r - 1) for r in (1, 2, 3)}
    d2d_sem = {key: 3 * len(order) + 3 + k for key, k in ici_sem.items()}
    nsem = 6 * len(order) + 3

    def body(*refs):
        cw_ref, dsts, gcw_ref = refs[nbig], refs[nbig + 1:2 * nbig + 1], refs[2 * nbig + 1]
        send_sems, recv_sems = refs[2 * nbig + 2:]
        x, y, c = _place()
        j = 2 * x + y

        def piece_rows(a, q, core):
            return pl.ds(pl.multiple_of(core * halves[a] + q * rows[a], 16), rows[a])

        def ici(a, q, r):
            tx, ty = _chip_of(x, y, r)
            k = ici_sem[(a, q, r)]
            region = dsts[a].at[j, piece_rows(a, q, c), :]
            return pltpu.make_async_remote_copy(
                src_ref=region, dst_ref=region, send_sem=send_sems.at[k], recv_sem=recv_sems.at[k],
                device_id=(tx, ty, c), device_id_type=MESH)

        def ici_landed(a, q, r):
            tx, ty = _chip_of(x, y, r)
            k = ici_sem[(a, q, r)]
            region = dsts[a].at[2 * tx + ty, piece_rows(a, q, c), :]
            return pltpu.make_async_remote_copy(
                src_ref=region, dst_ref=region, send_sem=send_sems.at[k], recv_sem=recv_sems.at[k],
                device_id=(tx, ty, c), device_id_type=MESH)

        def d2d(a, q, r, core):
            tx, ty = _chip_of(x, y, r)
            k = d2d_sem[(a, q, r)]
            region = dsts[a].at[2 * tx + ty, piece_rows(a, q, core), :]
            return pltpu.make_async_remote_copy(
                src_ref=region, dst_ref=region, send_sem=send_sems.at[k], recv_sem=recv_sems.at[k],
                device_id=(x, y, 1 - c), device_id_type=MESH)

        def cw_copy(r):
            tx, ty = _chip_of(x, y, r)
            k = cw_sem[r]
            return pltpu.make_async_remote_copy(
                src_ref=cw_ref, dst_ref=gcw_ref.at[j], send_sem=send_sems.at[k], recv_sem=recv_sems.at[k],
                device_id=(tx, ty, c), device_id_type=MESH)

        def cw_landed(r):
            tx, ty = _chip_of(x, y, r)
            k = cw_sem[r]
            region = gcw_ref.at[2 * tx + ty]
            return pltpu.make_async_remote_copy(
                src_ref=region, dst_ref=region, send_sem=send_sems.at[k], recv_sem=recv_sems.at[k],
                device_id=(tx, ty, c), device_id_type=MESH)

        def relay(a, q, origin, to):
            ox, oy = _chip_of(x, y, origin)
            tx, ty = _chip_of(x, y, to)
            k = ici_sem[(a, q, 3)]
            region = dsts[a].at[2 * ox + oy, piece_rows(a, q, c), :]
            return pltpu.make_async_remote_copy(
                src_ref=region, dst_ref=region, send_sem=send_sems.at[k], recv_sem=recv_sems.at[k],
                device_id=(tx, ty, c), device_id_type=MESH)

        first = [ici(a, q, r) for (a, q) in order for r in (1, 2)] + [cw_copy(r) for r in (1, 2, 3)]
        for cp in first:
            cp.start()
        passed = []
        for (a, q) in order:
            for r in (1, 2):
                ici_landed(a, q, r).wait_recv()
                if q % 2 == r - 1:
                    cp = relay(a, q, r, 3 - r)
                    cp.start()
                    passed.append(cp)
                cp = d2d(a, q, r, c)
                cp.start()
                passed.append(cp)
        for (a, q) in order:
            ici_landed(a, q, 3).wait_recv()
            cp = d2d(a, q, 3, c)
            cp.start()
            passed.append(cp)
        for r in (1, 2, 3):
            cw_landed(r).wait_recv()
        for (a, q) in order:
            for r in (1, 2, 3):
                d2d(a, q, r, 1 - c).wait_recv()
        for cp in first + passed:
            cp.wait_send()

    return pl.pallas_call(
        body,
        name="gather_weights",
        in_specs=[ANY] * (nbig + 1),
        out_specs=[ANY] * (nbig + 1),
        out_shape=[_sds(s.shape, s.dtype) for s in placed] + [_sds((N_CHIPS,) + cw8.shape, cw8.dtype)],
        input_output_aliases={a: a for a in range(nbig)},
        scratch_shapes=[pltpu.SemaphoreType.DMA((nsem,)), pltpu.SemaphoreType.DMA((nsem,))],
    )(*placed, cw8)


def _gather_late_start(placed, after, name):
    n = len(placed)
    halves = [s.shape[1] // 2 for s in placed]

    def body(*refs):
        g_refs = refs[0:n]
        send_sems, recv_sems, token = refs[n + 1], refs[n + 2], refs[-1]
        x, y, c = _place()
        j = 2 * x + y
        for a in range(n):
            mine = g_refs[a].at[j, pl.ds(pl.multiple_of(c * halves[a], 16), halves[a]), :]
            for r in (1, 2, 3):
                tx, ty = _chip_of(x, y, r)
                for to_core in (0, 1):
                    k = ((a * 3 + (r - 1)) * 2 + c) * 2 + to_core
                    pltpu.make_async_remote_copy(
                        src_ref=mine, dst_ref=mine, send_sem=send_sems.at[k], recv_sem=recv_sems.at[k],
                        device_id=(tx, ty, to_core), device_id_type=MESH).start()
        token[...] = jnp.zeros_like(token)

    hbm = lambda t: pltpu.HBM(t.shape, t.dtype)
    keep = lambda t: pltpu.with_memory_space_constraint(t, pltpu.HBM)
    nsem = 12 * n
    outs = pl.pallas_call(
        body,
        name=name,
        in_specs=[HBM] * n + [ANY],
        out_specs=(SEM, SEM, *[HBM] * n, pl.BlockSpec(memory_space=pltpu.VMEM)),
        out_shape=(pltpu.SemaphoreType.DMA((nsem,)), pltpu.SemaphoreType.DMA((nsem,)), *[hbm(p) for p in placed],
                   jax.ShapeDtypeStruct((8, 128), F32)),
        input_output_aliases={i: 2 + i for i in range(n)},
        compiler_params=pltpu.CompilerParams(has_side_effects=DATAFLOW),
    )(*[keep(p) for p in placed], after)
    return outs[0], outs[1], list(outs[2:2 + n]), outs[-1]


def _gather_late_wait(send_sems, recv_sems, thru, after, name):
    n = len(thru)
    halves = [s.shape[1] // 2 for s in thru]

    def body(*refs):
        g_refs = refs[0:n]
        send_sems, recv_sems = refs[n], refs[n + 1]
        x, y, c = _place()
        j = 2 * x + y
        for a in range(n):
            mine = g_refs[a].at[j, pl.ds(pl.multiple_of(c * halves[a], 16), halves[a]), :]
            for r in (1, 2, 3):
                tx, ty = _chip_of(x, y, r)
                for other in (0, 1):
                    k_out = ((a * 3 + (r - 1)) * 2 + c) * 2 + other
                    pltpu.make_async_remote_copy(
                        src_ref=mine, dst_ref=mine, send_sem=send_sems.at[k_out], recv_sem=recv_sems.at[k_out],
                        device_id=(tx, ty, other), device_id_type=MESH).wait_send()
                    k_in = ((a * 3 + (r - 1)) * 2 + other) * 2 + c
                    theirs = g_refs[a].at[2 * tx + ty, pl.ds(other * halves[a], halves[a]), :]
                    pltpu.make_async_remote_copy(
                        src_ref=theirs, dst_ref=theirs, send_sem=send_sems.at[k_in], recv_sem=recv_sems.at[k_in],
                        device_id=(tx, ty, other), device_id_type=MESH).wait_recv()

    hbm = lambda t: pltpu.HBM(t.shape, t.dtype)
    outs = pl.pallas_call(
        body,
        name=name,
        in_specs=[HBM] * n + [SEM, SEM, ANY],
        out_specs=[HBM] * n,
        out_shape=[hbm(t) for t in thru],
        input_output_aliases={i: i for i in range(n)},
        compiler_params=pltpu.CompilerParams(has_side_effects=DATAFLOW),
    )(*thru, send_sems, recv_sems, after)
    return list(outs)


D2D_PIECE_ROWS = 64


def _pair_exchange(grads, name):
    n = len(grads)
    halves = [g.shape[1] // 2 for g in grads]

    def body(*refs):
        g_refs, got_refs = refs[0:n], refs[n:2 * n]
        send_sems, recv_sems = refs[2 * n:]
        x, y, c = _place()

        def copy(a, src, dst):
            return pltpu.make_async_remote_copy(
                src_ref=src, dst_ref=dst, send_sem=send_sems.at[a], recv_sem=recv_sems.at[a],
                device_id=(x, y, 1 - c), device_id_type=MESH)

        for a in range(n):
            for jj in range(N_CHIPS):
                for q in range(halves[a] // D2D_PIECE_ROWS):
                    src_rows = pl.ds(pl.multiple_of((1 - c) * halves[a] + q * D2D_PIECE_ROWS, 16), D2D_PIECE_ROWS)
                    dst_rows = pl.ds(q * D2D_PIECE_ROWS, D2D_PIECE_ROWS)
                    copy(a, g_refs[a].at[jj, src_rows, :], got_refs[a].at[jj, dst_rows, :]).start()
        for a in range(n):
            sent = g_refs[a].at[:, pl.ds(pl.multiple_of((1 - c) * halves[a], 16), halves[a]), :]
            copy(a, sent, got_refs[a]).wait()

    return pl.pallas_call(
        body,
        name=name,
        in_specs=[ANY] * n,
        out_specs=[ANY] * n,
        out_shape=[_sds((N_CHIPS, h, g.shape[2]), g.dtype) for g, h in zip(grads, halves)],
        scratch_shapes=[pltpu.SemaphoreType.DMA((n,)), pltpu.SemaphoreType.DMA((n,))],
    )(*grads)


HBM = pl.BlockSpec(memory_space=pltpu.HBM)
SEM = pl.BlockSpec(memory_space=pltpu.SEMAPHORE)
DATAFLOW = pltpu.SideEffectType.DATAFLOW_SIDE_EFFECTING


def _chip_copy(p_refs, land_refs, send_sems, recv_sems, a, r, blocked):
    x, y, c = _place()
    tx, ty = _chip_of(x, y, r)
    k = a * 3 + (r - 1)
    return pltpu.make_async_remote_copy(
        src_ref=p_refs[a].at[2 * tx + ty] if blocked else p_refs[a], dst_ref=land_refs[a].at[r - 1],
        send_sem=send_sems.at[k], recv_sem=recv_sems.at[k], device_id=(tx, ty, c), device_id_type=MESH)


def _chip_exchange_start(psums, name, blocked=True):
    n = len(psums)
    lands = [lax.empty((3,) + (p.shape[1:] if blocked else p.shape), p.dtype) for p in psums]

    def body(*refs):
        p_refs, land_refs = refs[0:n], refs[n:2 * n]
        send_sems, recv_sems, token = refs[2 * n], refs[2 * n + 1], refs[-1]
        for a in range(n):
            for r in (1, 2, 3):
                _chip_copy(p_refs, land_refs, send_sems, recv_sems, a, r, blocked).start()
        token[...] = jnp.zeros_like(token)

    hbm = lambda t: pltpu.HBM(t.shape, t.dtype)
    keep = lambda t: pltpu.with_memory_space_constraint(t, pltpu.HBM)
    outs = pl.pallas_call(
        body,
        name=name,
        in_specs=[HBM] * (2 * n),
        out_specs=(SEM, SEM, *[HBM] * (2 * n), pl.BlockSpec(memory_space=pltpu.VMEM)),
        out_shape=(pltpu.SemaphoreType.DMA((3 * n,)), pltpu.SemaphoreType.DMA((3 * n,)),
                   *[hbm(p) for p in psums], *[hbm(l) for l in lands], _sds((8, 128), F32)),
        input_output_aliases={i: 2 + i for i in range(2 * n)},
        compiler_params=pltpu.CompilerParams(has_side_effects=DATAFLOW),
    )(*[keep(p) for p in psums], *[keep(l) for l in lands])
    return outs[0], outs[1], list(outs[2:2 + n]), list(outs[2 + n:2 + 2 * n]), outs[-1]


def _chip_exchange_wait(send_sems, recv_sems, p_thru, land_thru, after, name, blocked=True):
    n = len(p_thru)

    def body(*refs):
        p_refs, land_refs = refs[0:n], refs[n:2 * n]
        send_sems, recv_sems = refs[2 * n], refs[2 * n + 1]
        for a in range(n):
            for r in (1, 2, 3):
                copy = _chip_copy(p_refs, land_refs, send_sems, recv_sems, a, r, blocked)
                copy.wait_send()
                copy.wait_recv()

    hbm = lambda t: pltpu.HBM(t.shape, t.dtype)
    outs = pl.pallas_call(
        body,
        name=name,
        in_specs=[HBM] * (2 * n) + [SEM, SEM, ANY],
        out_specs=[HBM] * (2 * n),
        out_shape=[hbm(p) for p in p_thru] + [hbm(l) for l in land_thru],
        input_output_aliases={i: i for i in range(2 * n)},
        compiler_params=pltpu.CompilerParams(has_side_effects=DATAFLOW),
    )(*p_thru, *land_thru, send_sems, recv_sems, after)
    return list(outs[n:2 * n])


def _pair_share(fulls):
    n = len(fulls)
    halves = [f.shape[0] // 2 for f in fulls]

    def body(*refs):
        full_refs = refs[n:2 * n]
        send_sems, recv_sems = refs[2 * n:]
        x, y, c = _place()

        def half_of(a, core):
            return full_refs[a].at[pl.ds(pl.multiple_of(core * halves[a], 8), halves[a]), :]

        def remote(a, src, dst):
            return pltpu.make_async_remote_copy(
                src_ref=src, dst_ref=dst, send_sem=send_sems.at[a], recv_sem=recv_sems.at[a],
                device_id=(x, y, 1 - c), device_id_type=MESH)

        for a in range(n):
            for q in range(halves[a] // D2D_PIECE_ROWS):
                piece = full_refs[a].at[
                    pl.ds(pl.multiple_of(c * halves[a] + q * D2D_PIECE_ROWS, 8), D2D_PIECE_ROWS), :]
                remote(a, piece, piece).start()
        for a in range(n):
            remote(a, half_of(a, c), half_of(a, c)).wait_send()
            remote(a, half_of(a, 1 - c), half_of(a, 1 - c)).wait_recv()

    return pl.pallas_call(
        body,
        name="pair_share",
        in_specs=[ANY] * n,
        out_specs=[ANY] * n,
        out_shape=[_sds(f.shape, F32) for f in fulls],
        input_output_aliases={a: a for a in range(n)},
        scratch_shapes=[pltpu.SemaphoreType.DMA((n,)), pltpu.SemaphoreType.DMA((n,))],
    )(*fulls)


def _small_pair_sum(s):
    R, C = s.shape
    V = SMALL_VECTOR_ROWS

    def body(s_ref, v_ref, m_ref, sib, send_sem, recv_sem):
        x, y, c = _place()

        def to_sib(src, dst):
            return pltpu.make_async_remote_copy(
                src_ref=src, dst_ref=dst, send_sem=send_sem, recv_sem=recv_sem,
                device_id=(x, y, 1 - c), device_id_type=MESH)

        for q in range(R // 8):
            to_sib(s_ref.at[pl.ds(8 * q, 8), :], sib.at[pl.ds(8 * q, 8), :]).start()
        to_sib(s_ref, sib).wait()
        v_ref[...] = s_ref[pl.ds(0, V), :] + sib[pl.ds(0, V), :]
        m_ref[...] = (s_ref[pl.ds(V, R - V), :] + sib[pl.ds(V, R - V), :]).astype(BF16)

    return pl.pallas_call(
        body,
        name="small_pair_sum",
        in_specs=[pl.BlockSpec(memory_space=pltpu.VMEM)],
        out_specs=[pl.BlockSpec(memory_space=pltpu.VMEM)] * 2,
        out_shape=[jax.ShapeDtypeStruct((V, C), F32), jax.ShapeDtypeStruct((R - V, C), BF16)],
        scratch_shapes=[pltpu.VMEM((R, C), F32), pltpu.SemaphoreType.DMA, pltpu.SemaphoreType.DMA],
    )(s)


def _small_total(chip, own, landed):
    V, C = own[0].shape
    M = own[1].shape[0]

    def body(j_ref, v_ref, m_ref, lv_ref, lm_ref, o_ref, chips_v, chips_m):
        j = j_ref[0]
        chips_v[j] = v_ref[...]
        chips_m[j] = m_ref[...]
        for r in (1, 2, 3):
            chips_v[j ^ r] = lv_ref[r - 1]
            chips_m[j ^ r] = lm_ref[r - 1]
        o_ref[pl.ds(0, V), :] = (chips_v[0] + chips_v[1]) + (chips_v[2] + chips_v[3])
        o_ref[pl.ds(V, M), :] = (chips_m[0].astype(F32) + chips_m[1].astype(F32)) + (
            chips_m[2].astype(F32) + chips_m[3].astype(F32))

    vmem = pl.BlockSpec(memory_space=pltpu.VMEM)
    return pl.pallas_call(
        body,
        name="small_total",
        in_specs=[pl.BlockSpec(memory_space=pltpu.SMEM), vmem, vmem, vmem, vmem],
        out_specs=vmem,
        out_shape=jax.ShapeDtypeStruct((V + M, C), F32),
        scratch_shapes=[pltpu.VMEM((N_CHIPS, V, C), F32), pltpu.VMEM((N_CHIPS, M, C), BF16)],
    )(chip, own[0], own[1], landed[0], landed[1])


def _block_diag(w):
    w4 = w.reshape(4, 4, RNN_BLOCK_W, RNN_BLOCK_W)
    eye = jnp.eye(4, dtype=w.dtype)
    return jnp.einsum("jaik,ab->jaibk", w4, eye).reshape(4, RNN_TILE, RNN_TILE)


def _block_diag_part(d):
    d5 = d.reshape(4, 4, RNN_BLOCK_W, 4, RNN_BLOCK_W)
    return jnp.stack([d5[:, a, :, a, :] for a in range(4)], axis=1).reshape(RNN_BLOCKS, RNN_BLOCK_W, RNN_BLOCK_W)


def _local_grads(x, target, g_pre, w_in_g, b_gate, conv_w, conv_b, w_rg_a, b_rg_a, w_rg_x, b_rg_x, lam, sinks,
                 out_weights, fwd_token, g_post, on_out_grads, on_w_in_grad):
    wa_bd = _block_diag(w_rg_a).astype(BF16)
    wx_bd = _block_diag(w_rg_x).astype(BF16)
    b_a = b_rg_a.reshape(1, D_RNN)
    b_x = b_rg_x.reshape(1, D_RNN)

    proj, ht = _proj_fwd(x, g_pre, w_in_g)
    y_rnn, z_rnn, conv, z_rnn_t = _rnn_fwd(proj, conv_w, conv_b, wa_bd, wx_bd, b_a, b_x, lam, fwd_token)
    bias = _attn_bias()
    y_attn, z_attn, lse = _attn_fwd(proj, sinks, bias)
    w_rnn_out, w_attn_out, w_out = out_weights(z_attn)
    dyx, dz_rnn, dz_attn, dml, dout, dbr_rnn, dbr_attn, merged_t, z_attn_t, head_small = _head(
        x, target, z_rnn, z_attn, proj, b_gate, g_post, w_rnn_out, w_attn_out, w_out)
    out_grads = [_matmul_t(z_rnn_t, dbr_rnn, "dw_rnn_out"), _matmul_t(z_attn_t, dbr_attn, "dw_attn_out"),
                 _matmul_t(merged_t, dout, "dw_out")]
    shard_rows = lambda d: d.reshape(N_CHIPS, OUT_SHARD, D_MODEL)
    token = on_out_grads([shard_rows(g) for g, _ in out_grads], [shard_rows(gb) for _, gb in out_grads])
    dq, dk, dv, dag, attn_small = _attn_bwd(proj, y_attn, lse, dz_attn, sinks, bias, token)
    drx, drg, dwa_t, dwx_t, rnn_small = _rnn_bwd(proj, conv, y_rnn, dz_rnn, conv_w, wa_bd, wx_bd, b_a, b_x, lam)
    dproj = [drx, drg, dq, dk, dv, dag, dml]
    token = on_w_in_grad(*_dw_in(ht, dproj))
    grad_x, dh_small = _dh_bwd(dproj, w_in_g, x, dyx, g_pre, token)
    small = jnp.concatenate([rnn_small, head_small, dh_small + attn_small,
                             _block_diag_part(dwa_t).reshape(64, 1024), _block_diag_part(dwx_t).reshape(64, 1024)], axis=0)
    return grad_x, small


ROW_LOSS = 11


def _rows8(parts):
    out = None
    for r, a in parts:
        p = jnp.pad(a, ((r, 8 - r - a.shape[0]), (0, 1024 - a.shape[1])))
        out = p if out is None else out + p
    return out


def _pack_small(p):
    g0 = _rows8([(0, p["b_rg_a"].reshape(1, 1024)), (1, p["b_rg_x"].reshape(1, 1024)), (2, p["lru_lambda"]),
                 (3, p["conv_b"]), (4, p["conv_w"][0])])
    g1 = _rows8([(0, p["post_norm_g"]), (1, p["b_gate"].reshape(2, 1024))])
    g2 = _rows8([(0, p["pre_norm_g"]), (1, p["attn_sinks"])])
    return jnp.concatenate([g0, g1, g2, p["w_rg_a"].reshape(64, 1024), p["w_rg_x"].reshape(64, 1024)], axis=0)


def _unpack_small(s, conv_cols):
    return {
        "b_rg_a": s[0:1].reshape(1, 16, 64), "b_rg_x": s[1:2].reshape(1, 16, 64), "lru_lambda": s[2:3],
        "conv_b": s[3:4], "conv_w": s[4:8, 0:conv_cols].reshape(1, CONV_W, conv_cols),
        "post_norm_g": s[8:9], "b_gate": s[9:11].reshape(1, 2048),
        "pre_norm_g": s[16:17], "attn_sinks": s[17:18, 0:N_Q_HEADS],
        "w_rg_a": s[24:88].reshape(1, 16, 64, 64), "w_rg_x": s[88:152].reshape(1, 16, 64, 64),
    }


WEIGHTS = ["pre_norm_g", "w_in", "b_gate", "conv_w", "conv_b", "w_rg_a", "b_rg_a", "w_rg_x", "b_rg_x", "lru_lambda",
           "attn_sinks", "w_rnn_out", "w_attn_out", "w_out", "post_norm_g"]
BIG = ["w_in", "w_rnn_out", "w_attn_out", "w_out"]


def kernel(x, pre_norm_g, w_in, b_gate, conv_w, conv_b, w_rg_a, b_rg_a, w_rg_x, b_rg_x, lru_lambda, attn_sinks, w_rnn_out, w_attn_out, w_out, post_norm_g, loss_target, m_pre_norm_g, m_w_in, m_b_gate, m_conv_w, m_conv_b, m_w_rg_a, m_b_rg_a, m_w_rg_x, m_b_rg_x, m_lru_lambda, m_attn_sinks, m_w_rnn_out, m_w_attn_out, m_w_out, m_post_norm_g, v_pre_norm_g, v_w_in, v_b_gate, v_conv_w, v_conv_b, v_w_rg_a, v_b_rg_a, v_w_rg_x, v_b_rg_x, v_lru_lambda, v_attn_sinks, v_w_rnn_out, v_w_attn_out, v_w_out, v_post_norm_g):
    w = dict(pre_norm_g=pre_norm_g, w_in=w_in, b_gate=b_gate, conv_w=conv_w, conv_b=conv_b, w_rg_a=w_rg_a,
             b_rg_a=b_rg_a, w_rg_x=w_rg_x, b_rg_x=b_rg_x, lru_lambda=lru_lambda, attn_sinks=attn_sinks,
             w_rnn_out=w_rnn_out, w_attn_out=w_attn_out, w_out=w_out, post_norm_g=post_norm_g)
    m = dict(pre_norm_g=m_pre_norm_g, w_in=m_w_in, b_gate=m_b_gate, conv_w=m_conv_w, conv_b=m_conv_b, w_rg_a=m_w_rg_a,
             b_rg_a=m_b_rg_a, w_rg_x=m_w_rg_x, b_rg_x=m_b_rg_x, lru_lambda=m_lru_lambda, attn_sinks=m_attn_sinks,
             w_rnn_out=m_w_rnn_out, w_attn_out=m_w_attn_out, w_out=m_w_out, post_norm_g=m_post_norm_g)
    v = dict(pre_norm_g=v_pre_norm_g, w_in=v_w_in, b_gate=v_b_gate, conv_w=v_conv_w, conv_b=v_conv_b, w_rg_a=v_w_rg_a,
             b_rg_a=v_b_rg_a, w_rg_x=v_w_rg_x, b_rg_x=v_b_rg_x, lru_lambda=v_lru_lambda, attn_sinks=v_attn_sinks,
             w_rnn_out=v_w_rnn_out, w_attn_out=v_w_attn_out, w_out=v_w_out, post_norm_g=v_post_norm_g)
    chip = 2 * lax.axis_index("x") + lax.axis_index("y")

    chip_idx = chip.astype(jnp.int32).reshape(1)
    chip_core = jnp.stack([chip, lax.axis_index("c")]).astype(jnp.int32)
    cw8 = jnp.pad(conv_w[0], ((0, 8 - CONV_W), (0, 0)))
    placed = _place_shards([w_in[0], w_rnn_out[0], w_attn_out[0], w_out[0]], chip_idx, "place_shards")
    win_g, cw_g = _gather_weights(placed[:1], cw8)
    late_send, late_recv, late_thru, late_token = _gather_late_start(placed[1:], win_g, "gather_late_start")
    cw_g = lax.dynamic_update_slice_in_dim(cw_g, cw8[None], chip, axis=0)
    conv_w_full = jnp.transpose(cw_g[:, 0:CONV_W, :], (1, 0, 2)).reshape(CONV_W, D_RNN)

    started = {}

    def start_reduction(tag, grads, grads_b16):
        got = _pair_exchange(grads_b16, "pair_exchange_" + tag)
        sums = [_pair_sum(g, o, chip_core, "pair_sum_%s_%d" % (tag, a)) for a, (g, o) in enumerate(zip(grads, got))]
        send_sems, recv_sems, p_thru, land_thru, token = _chip_exchange_start(
            [pb for _, pb in sums], "chip_exchange_start_" + tag)
        started[tag] = ([p for p, _ in sums], send_sems, recv_sems, p_thru, land_thru)
        return token

    def end_reduction(tag, after):
        psums, send_sems, recv_sems, p_thru, land_thru = started[tag]
        landed = _chip_exchange_wait(send_sems, recv_sems, p_thru, land_thru, after, "chip_exchange_wait_" + tag)
        return [_chip_sum(p, l, chip_core, "chip_sum_%s_%d" % (tag, a)) for a, (p, l) in enumerate(zip(psums, landed))]

    def out_weights(after):
        gathered = _gather_late_wait(late_send, late_recv, late_thru, after, "gather_late_wait")
        return [g.reshape(D_MODEL, D_MODEL) for g in gathered]

    grad_x, small = _local_grads(
        x[0], loss_target[0], pre_norm_g, win_g, b_gate, conv_w_full, conv_b, w_rg_a[0], b_rg_a[0], w_rg_x[0],
        b_rg_x[0], lru_lambda, attn_sinks[0], out_weights, late_token, post_norm_g,
        on_out_grads=lambda grads, grads_b16: start_reduction("out", grads, grads_b16),
        on_w_in_grad=lambda grad, grad_b16: start_reduction("in", [grad], [grad_b16]))

    small_chip = _small_pair_sum(small)
    small_send, small_recv, small_thru, small_land, small_token = _chip_exchange_start(
        list(small_chip), "small_exchange_start", blocked=False)

    halves = end_reduction("in", small_token) + end_reduction("out", small_token)
    gbig = dict(zip(BIG, _pair_share(halves)))

    grads, delta, new_m, new_v = {}, {}, {}, {}
    for n in BIG:
        grads[n] = gbig[n][None]
        d, nm, nv = _adamw(w[n][0], gbig[n], m[n][0], v[n][0], "adamw_" + n)
        delta[n], new_m[n], new_v[n] = d[None], nm[None], nv[None]

    small_landed = _chip_exchange_wait(small_send, small_recv, small_thru, small_land, delta[BIG[-1]],
                                       "small_exchange_wait", blocked=False)
    small_sum = _small_total(chip_idx, small_thru, small_landed)
    total_loss = small_sum[ROW_LOSS, 0]
    gsmall = _unpack_small(small_sum, D_RNN)
    conv_shard = D_RNN // N_CHIPS
    gsmall["conv_w"] = lax.dynamic_slice_in_dim(gsmall["conv_w"], chip * conv_shard, conv_shard, axis=2)
    pick = lambda t: {k: t[k] for k in gsmall}
    d, nm, nv = _adamw(_pack_small(pick(w)), _pack_small(gsmall), _pack_small(pick(m)), _pack_small(pick(v)),
                       "adamw_small")
    ud, um, uv = _unpack_small(d, conv_shard), _unpack_small(nm, conv_shard), _unpack_small(nv, conv_shard)
    for n in gsmall:
        grads[n] = gsmall[n].reshape(w[n].shape)
        delta[n] = ud[n].reshape(w[n].shape)
        new_m[n] = um[n].reshape(w[n].shape)
        new_v[n] = uv[n].reshape(w[n].shape)

    return (total_loss, grad_x[None], *[grads[n] for n in WEIGHTS], *[delta[n] for n in WEIGHTS],
            *[new_m[n] for n in WEIGHTS], *[new_v[n] for n in WEIGHTS])
```

```python
import functools
import math

import jax
import jax.numpy as jnp
from jax import lax
from jax.experimental import pallas as pl
from jax.experimental.pallas import tpu as pltpu

F32 = jnp.float32
BF16 = jnp.bfloat16

D_MODEL = 1024
D_RNN = 1024
RNN_BLOCKS = 16
RNN_BLOCK_W = 64
CONV_W = 4
LRU_C = 8.0
N_Q_HEADS = 16
N_KV_HEADS = 4
GROUP = 4
HEAD_DIM = 64
D_KV = 256
BLOCK = 128
ALIBI_MAX_BIAS = 8.0
EPS = 1e-6
D_IN = 6656
N_CHIPS = 4
W_IN_SHARD = D_IN // N_CHIPS
OUT_SHARD = D_MODEL // N_CHIPS
ADAM_LR = 0.001
ADAM_B1 = 0.9
ADAM_B2 = 0.999
ADAM_EPS = 1e-08
ADAM_WD = 0.01
ADAM_STEP = 10
NEG_BIG = -1e30
MIB = 1 << 20

COL_RNN_X = 0
COL_RNN_GATE = 4
COL_Q = 8
COL_K = 12
COL_V = 13
COL_ATTN_GATE = 14
COL_MERGE = 18

RNN_TILE = 256
RNN_CHUNK = 512
SMALL_ROWS = 152
SMALL_VECTOR_ROWS = 24
MESH = pl.DeviceIdType.MESH


def _sds(shape, dtype):
    return pltpu.HBM(shape, dtype)


def _params(sem=None, vmem_mib=None):
    kw = {}
    if sem is not None:
        kw["dimension_semantics"] = sem
    if vmem_mib is not None:
        kw["vmem_limit_bytes"] = vmem_mib * MIB
    return pltpu.CompilerParams(**kw)


def _hbm(*arrays):
    return [pltpu.with_memory_space_constraint(a, pltpu.HBM) for a in arrays]


def _dot(a, b):
    return jnp.dot(a, b, preferred_element_type=F32)


def _dot_nt(a, b):
    return lax.dot_general(a, b, (((1,), (1,)), ((), ())), preferred_element_type=F32)


def _dot_tn(a, b):
    return lax.dot_general(a, b, (((0,), (0,)), ((), ())), preferred_element_type=F32)


def _sigmoid(x):
    return 0.5 * jnp.tanh(0.5 * x) + 0.5


def _sigmoid_small(x):
    return 1.0 / (1.0 + jnp.exp(-x))


def _softplus(x):
    return jnp.maximum(x, 0.0) + jnp.log(1.0 + jnp.exp(-jnp.abs(x)))


def _one_minus_square(a, log_a):
    return -jnp.tanh(log_a) * (a * a + 1.0)


def _proj_fwd(x, g_pre, w_in_g):
    T = x.shape[0]
    tm = min(1024, T)

    def body(x_ref, g_ref, w_ref, proj_ref, ht_ref, h_s):
        @pl.when(pl.program_id(1) == 0)
        def _():
            xv = x_ref[...]
            rstd = lax.rsqrt(jnp.mean(xv * xv, axis=-1, keepdims=True) + EPS)
            hf = (xv * rstd) * g_ref[...]
            h_s[...] = hf.astype(BF16)
            ht_ref[...] = hf.T.astype(BF16)

        proj_ref[...] = _dot(h_s[...], w_ref[...]).astype(BF16)

    return pl.pallas_call(
        body,
        name="proj_fwd",
        grid=(T // tm, N_CHIPS),
        in_specs=[
            pl.BlockSpec((tm, D_MODEL), lambda i, j: (i, 0)),
            pl.BlockSpec((1, D_MODEL), lambda i, j: (0, 0)),
            pl.BlockSpec((None, D_MODEL, W_IN_SHARD), lambda i, j: (j, 0, 0)),
        ],
        out_specs=[
            pl.BlockSpec((tm, W_IN_SHARD), lambda i, j: (i, j)),
            pl.BlockSpec((D_MODEL, tm), lambda i, j: (0, i)),
        ],
        out_shape=[_sds((T, D_IN), BF16), _sds((D_MODEL, T), BF16)],
        scratch_shapes=[pltpu.VMEM((tm, D_MODEL), BF16)],
        compiler_params=_params(("parallel", "arbitrary"), 48),
    )(*_hbm(x, g_pre, w_in_g))


def _proj_own(x, g_pre, w_own, chip, token):
    T = x.shape[0]
    tm = min(1024, T)

    def body(j_ref, x_ref, g_ref, w_ref, token_ref, proj_ref, h_ref, ht_ref):
        xv = x_ref[...]
        rstd = lax.rsqrt(jnp.mean(xv * xv, axis=-1, keepdims=True) + EPS)
        hf = (xv * rstd) * g_ref[...]
        hb = hf.astype(BF16)
        h_ref[...] = hb
        ht_ref[...] = hf.T.astype(BF16)
        proj_ref[...] = _dot(hb, w_ref[...]).astype(BF16)

    return pl.pallas_call(
        body,
        name="proj_own",
        grid_spec=pltpu.PrefetchScalarGridSpec(
            num_scalar_prefetch=1,
            grid=(T // tm,),
            in_specs=[
                pl.BlockSpec((tm, D_MODEL), lambda i, j_ref: (i, 0)),
                pl.BlockSpec((1, D_MODEL), lambda i, j_ref: (0, 0)),
                pl.BlockSpec((D_MODEL, W_IN_SHARD), lambda i, j_ref: (0, 0)),
                pl.BlockSpec((8, 128), lambda i, j_ref: (0, 0)),
            ],
            out_specs=[
                pl.BlockSpec((tm, W_IN_SHARD), lambda i, j_ref: (i, j_ref[0])),
                pl.BlockSpec((tm, D_MODEL), lambda i, j_ref: (i, 0)),
                pl.BlockSpec((D_MODEL, tm), lambda i, j_ref: (0, i)),
            ],
        ),
        out_shape=[_sds((T, D_IN), BF16), _sds((T, D_MODEL), BF16), _sds((D_MODEL, T), BF16)],
        compiler_params=_params(("parallel",), 48),
    )(chip, *_hbm(x, g_pre, w_own, token))


def _proj_more(h, w_in_g, proj, blocks, name):
    T = h.shape[0]
    tm = min(1024, T)
    n = blocks.shape[0]

    def body(b_ref, h_ref, w_ref, proj_in, proj_ref):
        proj_ref[...] = _dot(h_ref[...], w_ref[...]).astype(BF16)

    return pl.pallas_call(
        body,
        name=name,
        grid_spec=pltpu.PrefetchScalarGridSpec(
            num_scalar_prefetch=1,
            grid=(n, T // tm),
            in_specs=[
                pl.BlockSpec((tm, D_MODEL), lambda b, i, b_ref: (i, 0)),
                pl.BlockSpec((None, D_MODEL, W_IN_SHARD), lambda b, i, b_ref: (b_ref[b], 0, 0)),
                ANY,
            ],
            out_specs=pl.BlockSpec((tm, W_IN_SHARD), lambda b, i, b_ref: (i, b_ref[b])),
        ),
        out_shape=_sds((T, D_IN), BF16),
        input_output_aliases={3: 0},
        compiler_params=_params(("arbitrary", "parallel"), 48),
    )(blocks, *_hbm(h, w_in_g, proj))


def _shift_down(x, tail, s, row):
    n = x.shape[0]
    xs = pltpu.roll(x, s, 0)
    tail_t = jnp.tile(pltpu.roll(tail, s, 0), (n // 8, 1))
    return jnp.where(row < s, tail_t, xs)


def _shift_up(x, head, s, row):
    n = x.shape[0]
    xs = pltpu.roll(x, n - s, 0)
    head_t = jnp.tile(pltpu.roll(head, 8 - s, 0), (n // 8, 1))
    return jnp.where(row >= n - s, head_t, xs)


def _conv_taps(x, tail, row):
    return [_shift_down(x, tail, 3, row), _shift_down(x, tail, 2, row), _shift_down(x, tail, 1, row), x]


def _rglru_gates(c, wa, wx, ba, bx, lam):
    cb = c.astype(BF16)
    r = _sigmoid_small(_dot(cb, wa) + ba)
    i = _sigmoid(_dot(cb, wx) + bx)
    log_a = (-LRU_C) * r * _softplus(-lam)
    a = jnp.exp(log_a)
    w = _one_minus_square(a, log_a)
    inv_mult = lax.rsqrt(w)
    return cb, r, i, a, w * inv_mult, inv_mult


SUBLANES = 8


def _scan_down(a, u, row):
    n = a.shape[0]
    s = 1
    while s < SUBLANES:
        a_sh = jnp.where(row >= s, pltpu.roll(a, s, 0), 1.0)
        u_sh = jnp.where(row >= s, pltpu.roll(u, s, 0), 0.0)
        u = a * u_sh + u
        a = a * a_sh
        s *= 2
    while s < n:
        u = jnp.concatenate([u[:s], a[s:] * u[:n - s] + u[s:]], axis=0)
        a = jnp.concatenate([a[:s], a[s:] * a[:n - s]], axis=0)
        s *= 2
    return a, u


def _scan_up(b, u, row):
    n = b.shape[0]
    s = 1
    while s < SUBLANES:
        b_sh = jnp.where(row < n - s, pltpu.roll(b, n - s, 0), 1.0)
        u_sh = jnp.where(row < n - s, pltpu.roll(u, n - s, 0), 0.0)
        u = b * u_sh + u
        b = b * b_sh
        s *= 2
    while s < n:
        u = jnp.concatenate([b[:n - s] * u[s:] + u[:n - s], u[n - s:]], axis=0)
        b = jnp.concatenate([b[:n - s] * b[s:], b[n - s:]], axis=0)
        s *= 2
    return b, u


LANES = 128


def _chunk_scan(a, u, a_s, u_s, hl_s, al_s, carry, reverse):
    n, width = a.shape
    groups = n // SUBLANES
    order = range(SUBLANES - 1, -1, -1) if reverse else range(SUBLANES)
    row = lax.broadcasted_iota(jnp.int32, (groups, LANES), 0)
    for l in range(width // LANES):
        lanes = slice(l * LANES, (l + 1) * LANES)
        a_l, u_l, hl_l, al_l = a_s.at[l], u_s.at[l], hl_s.at[l], al_s.at[l]
        a_l[...] = a[:, lanes]
        u_l[...] = u[:, lanes]
        h_loc = a_loc = None
        for r in order:
            rows = pl.ds(r, groups, stride=SUBLANES)
            a_r, u_r = a_l[rows, :], u_l[rows, :]
            h_loc, a_loc = (u_r, a_r) if h_loc is None else (a_r * h_loc + u_r, a_r * a_loc)
            hl_l[rows, :] = h_loc
            al_l[rows, :] = a_loc
        if reverse:
            a_cum, ends = _scan_up(a_loc, h_loc, row)
            ends = ends + a_cum * carry[:, lanes]
            enters = jnp.where(row == groups - 1, carry[:, lanes], pltpu.roll(ends, groups - 1, 0))
        else:
            a_cum, ends = _scan_down(a_loc, h_loc, row)
            ends = ends + a_cum * carry[:, lanes]
            enters = jnp.where(row == 0, carry[:, lanes], pltpu.roll(ends, 1, 0))
        for r in range(SUBLANES):
            rows = pl.ds(r, groups, stride=SUBLANES)
            hl_l[rows, :] = hl_l[rows, :] + al_l[rows, :] * enters
    return jnp.concatenate([hl_s[l] for l in range(width // LANES)], axis=1)


def _rnn_fwd(proj, conv_w, conv_b, wa_bd, wx_bd, b_a, b_x, lam, token):
    T = proj.shape[0]
    tc, ct = RNN_CHUNK, RNN_TILE
    nt = T // tc

    def body(x_ref, rg_ref, cw_ref, cb_ref, wa_ref, wx_ref, ba_ref, bx_ref, lam_ref, token_ref, h_ref, z_ref, c_ref,
             zt_ref, xtail, hcarry, a_s, u_s, hl_s, al_s):
        @pl.when(pl.program_id(1) == 0)
        def _():
            xtail[...] = jnp.zeros_like(xtail)
            hcarry[...] = jnp.zeros_like(hcarry)

        row = lax.broadcasted_iota(jnp.int32, (tc, ct), 0)
        x = x_ref[...].astype(F32)
        taps = _conv_taps(x, xtail[...], row)
        c = cb_ref[...] + cw_ref[pl.ds(0, 1), :] * taps[0]
        for k in range(1, CONV_W):
            c = c + cw_ref[pl.ds(k, 1), :] * taps[k]
        xtail[...] = x[tc - 8:, :]
        c_ref[...] = c
        _, _, i, a, mult, _ = _rglru_gates(c, wa_ref[...], wx_ref[...], ba_ref[...], bx_ref[...], lam_ref[...])
        h = _chunk_scan(a, mult * (i * c), a_s, u_s, hl_s, al_s, hcarry[...], reverse=False)
        h_ref[...] = h
        hcarry[...] = h_ref[pl.ds(tc - 1, 1), :]
        rg = rg_ref[...].astype(F32)
        z = h * (rg * _sigmoid(rg))
        z_ref[...] = z.astype(BF16)
        zt_ref[...] = z.T.astype(BF16)

    col = lambda off: (lambda j, t: (t, off + j))
    vec = pl.BlockSpec((1, ct), lambda j, t: (0, j))
    mat = pl.BlockSpec((None, ct, ct), lambda j, t: (j, 0, 0))
    return pl.pallas_call(
        body,
        name="rnn_fwd",
        grid=(D_RNN // ct, nt),
        in_specs=[
            pl.BlockSpec((tc, ct), col(COL_RNN_X)),
            pl.BlockSpec((tc, ct), col(COL_RNN_GATE)),
            pl.BlockSpec((CONV_W, ct), lambda j, t: (0, j)),
            vec, mat, mat, vec, vec, vec,
            pl.BlockSpec((8, 128), lambda j, t: (0, 0)),
        ],
        out_specs=[pl.BlockSpec((tc, ct), lambda j, t: (t, j))] * 3 + [pl.BlockSpec((ct, tc), lambda j, t: (j, t))],
        out_shape=[_sds((T, D_RNN), F32), _sds((T, D_RNN), BF16), _sds((T, D_RNN), F32), _sds((D_RNN, T), BF16)],
        scratch_shapes=[pltpu.VMEM((8, ct), F32), pltpu.VMEM((1, ct), F32)] + [
            pltpu.VMEM((ct // LANES, tc, LANES), F32)] * 4,
        compiler_params=_params(("parallel", "arbitrary"), 32),
    )(*_hbm(proj, proj, conv_w, conv_b, wa_bd, wx_bd, b_a, b_x, lam, token))


def _rnn_bwd(proj, conv, y_rnn, dz_rnn, conv_w, wa_bd, wx_bd, b_a, b_x, lam):
    T = proj.shape[0]
    tc, ct = RNN_CHUNK, RNN_TILE
    nt = T // tc
    hb = tc // 8

    def body(x_ref, c_ref, rg_ref, h_ref, hh_ref, dz_ref, cw_ref, wa_ref, wx_ref, ba_ref, bx_ref, lam_ref,
             dx_ref, drg_ref, dwa_ref, dwx_ref, sm_ref, lam_carry, a_carry, dc_head, b_s, dy_s, hl_s, al_s):
        t = pl.program_id(1)
        first_chunk = t == nt - 1

        @pl.when(t == 0)
        def _():
            lam_carry[...] = jnp.zeros_like(lam_carry)
            a_carry[...] = jnp.zeros_like(a_carry)
            dc_head[...] = jnp.zeros_like(dc_head)
            dwa_ref[...] = jnp.zeros_like(dwa_ref)
            dwx_ref[...] = jnp.zeros_like(dwx_ref)
            sm_ref[...] = jnp.zeros_like(sm_ref)

        row = lax.broadcasted_iota(jnp.int32, (tc, ct), 0)
        keep = jnp.where(first_chunk, 0.0, 1.0)
        x = x_ref[...].astype(F32)
        c = c_ref[...]
        lam = lam_ref[...]
        cb, r, i, a, mult, inv_mult = _rglru_gates(c, wa_ref[...], wx_ref[...], ba_ref[...], bx_ref[...], lam)
        h = h_ref[...]
        h_prev = _shift_down(h, hh_ref[...] * keep, 1, row)
        rg = rg_ref[...].astype(F32)
        dz = dz_ref[...]
        sg = _sigmoid(rg)
        drg_ref[...] = (dz * h * (sg * (1.0 + rg * (1.0 - sg)))).astype(BF16)
        dy = dz * (rg * sg)
        b = jnp.where(row >= tc - 1, a_carry[pl.ds(0, 1), :], pltpu.roll(a, tc - 1, 0))
        lt = _chunk_scan(b, dy, b_s, dy_s, hl_s, al_s, lam_carry[pl.ds(0, 1), :], reverse=True)
        lam_carry[...] = lt[0:8, :]
        a_carry[...] = a[0:8, :]
        ic = i * c
        dmult = lt * ic
        di = lt * mult * c
        dc = lt * mult * i
        dlog_a = a * (lt * h_prev - dmult * a * inv_mult)
        sp = _softplus(-lam)
        dpre_r = dlog_a * ((-LRU_C) * sp) * (r * (1.0 - r))
        dpre_i = di * (i * (1.0 - i))
        dlam_row = jnp.sum(dlog_a * r, axis=0, keepdims=True) * (LRU_C * _sigmoid(-lam))
        dpr_b = dpre_r.astype(BF16)
        dpi_b = dpre_i.astype(BF16)
        dwa_ref[...] += _dot_tn(cb, dpr_b)
        dwx_ref[...] += _dot_tn(cb, dpi_b)
        dc = dc + _dot_nt(dpr_b, wa_ref[...]) + _dot_nt(dpi_b, wx_ref[...])
        head = dc_head[...]
        dx = cw_ref[pl.ds(3, 1), :] * dc
        sm_ref[pl.ds(4 + 3, 1), :] += jnp.sum(dc * x, axis=0, keepdims=True)
        for m in range(1, CONV_W):
            up = _shift_up(dc, head, m, row)
            dx = dx + cw_ref[pl.ds(3 - m, 1), :] * up
            sm_ref[pl.ds(4 + 3 - m, 1), :] += jnp.sum(up * x, axis=0, keepdims=True)
        dx_ref[...] = dx.astype(BF16)
        dc_head[...] = dc[0:8, :]
        sm_ref[pl.ds(0, 1), :] += jnp.sum(dpre_r, axis=0, keepdims=True)
        sm_ref[pl.ds(1, 1), :] += jnp.sum(dpre_i, axis=0, keepdims=True)
        sm_ref[pl.ds(2, 1), :] += dlam_row
        sm_ref[pl.ds(3, 1), :] += jnp.sum(dc, axis=0, keepdims=True)

    rev = lambda off: (lambda j, t: (nt - 1 - t, off + j))
    halo = lambda off: (lambda j, t: (jnp.maximum((nt - 1 - t) * hb - 1, 0), off + j))
    vec = pl.BlockSpec((1, ct), lambda j, t: (0, j))
    mat = pl.BlockSpec((None, ct, ct), lambda j, t: (j, 0, 0))
    return pl.pallas_call(
        body,
        name="rnn_bwd",
        grid=(D_RNN // ct, nt),
        in_specs=[
            pl.BlockSpec((tc, ct), rev(COL_RNN_X)),
            pl.BlockSpec((tc, ct), rev(0)),
            pl.BlockSpec((tc, ct), rev(COL_RNN_GATE)),
            pl.BlockSpec((tc, ct), rev(0)),
            pl.BlockSpec((8, ct), halo(0)),
            pl.BlockSpec((tc, ct), rev(0)),
            pl.BlockSpec((CONV_W, ct), lambda j, t: (0, j)),
            mat, mat, vec, vec, vec,
        ],
        out_specs=[
            pl.BlockSpec((tc, ct), rev(0)),
            pl.BlockSpec((tc, ct), rev(0)),
            mat, mat,
            pl.BlockSpec((8, ct), lambda j, t: (0, j)),
        ],
        out_shape=[_sds((T, D_RNN), BF16), _sds((T, D_RNN), BF16), _sds((D_RNN // ct, ct, ct), F32),
                   _sds((D_RNN // ct, ct, ct), F32), _sds((8, D_RNN), F32)],
        scratch_shapes=[pltpu.VMEM((8, ct), F32)] * 3 + [pltpu.VMEM((ct // LANES, tc, LANES), F32)] * 4,
        compiler_params=_params(("parallel", "arbitrary"), 32),
    )(*_hbm(proj, conv, proj, y_rnn, y_rnn, dz_rnn, conv_w, wa_bd, wx_bd, b_a, b_x, lam))


def _attn_bias():
    qi = jnp.arange(BLOCK)[:, None]
    kj = jnp.arange(BLOCK)[None, :]
    dist_cur = (qi - kj).astype(F32)
    slopes = 2.0 ** (-ALIBI_MAX_BIAS * jnp.arange(1, N_Q_HEADS + 1, dtype=F32) / N_Q_HEADS)
    slopes = slopes[:, None, None]
    prev = jnp.where(kj > qi, -slopes * (dist_cur + float(BLOCK)), NEG_BIG)
    cur = jnp.where(kj <= qi, -slopes * dist_cur, NEG_BIG)
    later = jnp.concatenate([prev, cur], axis=-1)
    first = jnp.concatenate([jnp.full_like(prev, NEG_BIG), cur], axis=-1)
    return jnp.stack([first, later])


def _attn_exps(s_prev, s_cur, sink, bias):
    s_prev = s_prev + bias[:, 0:BLOCK]
    s_cur = s_cur + bias[:, BLOCK:2 * BLOCK]
    m = jnp.maximum(jnp.max(jnp.maximum(s_prev, s_cur), axis=-1, keepdims=True), sink)
    p_prev = jnp.exp(s_prev - m)
    p_cur = jnp.exp(s_cur - m)
    total = jnp.sum(p_prev + p_cur, axis=-1, keepdims=True) + jnp.exp(sink - m)
    return p_prev, p_cur, 1.0 / total, m + jnp.log(total)


def _attn_probs(s_prev, s_cur, sink, bias, lse):
    p_prev = jnp.exp((s_prev + bias[:, 0:BLOCK]) - lse)
    p_cur = jnp.exp((s_cur + bias[:, BLOCK:2 * BLOCK]) - lse)
    return p_prev, p_cur, jnp.exp(sink - lse)


def _stack_heads(ref_or_val, hk, dtype):
    parts = [ref_or_val[:, (GROUP * hk + g) * HEAD_DIM:(GROUP * hk + g + 1) * HEAD_DIM] for g in range(GROUP)]
    return jnp.concatenate(parts, axis=0).astype(dtype)


ATTN_SCALE = HEAD_DIM ** -0.5


def _bias_spec():
    return pl.BlockSpec((None, N_Q_HEADS, BLOCK, 2 * BLOCK), lambda i: (jnp.minimum(i, 1), 0, 0, 0))


def _attn_fwd(proj, sinks, bias):
    T = proj.shape[0]
    nb = T // BLOCK

    def body(sink_ref, bias_ref, q_ref, kp_ref, kc_ref, vp_ref, vc_ref, ag0_ref, ag1_ref, y_ref, z_ref, lse_ref):
        kvs = [slice(hk * HEAD_DIM, (hk + 1) * HEAD_DIM) for hk in range(N_KV_HEADS)]
        qgs = [(_stack_heads(q_ref, hk, F32) * ATTN_SCALE).astype(BF16) for hk in range(N_KV_HEADS)]
        s_prev = [_dot_nt(qgs[hk], kp_ref[:, kvs[hk]].astype(BF16)) for hk in range(N_KV_HEADS)]
        s_cur = [_dot_nt(qgs[hk], kc_ref[:, kvs[hk]].astype(BF16)) for hk in range(N_KV_HEADS)]
        for hk in range(N_KV_HEADS):
            pp, pc, invs = [], [], []
            for g in range(GROUP):
                h = GROUP * hk + g
                rows = slice(g * BLOCK, (g + 1) * BLOCK)
                p_prev, p_cur, inv, lse = _attn_exps(s_prev[hk][rows], s_cur[hk][rows], sink_ref[h], bias_ref[h])
                pp.append(p_prev.astype(BF16))
                pc.append(p_cur.astype(BF16))
                invs.append(inv)
                lse_ref[:, h:h + 1] = lse
            og = _dot(jnp.concatenate(pp, axis=0), vp_ref[:, kvs[hk]].astype(BF16)) + _dot(
                jnp.concatenate(pc, axis=0), vc_ref[:, kvs[hk]].astype(BF16))
            for g in range(GROUP):
                h = GROUP * hk + g
                y_ref[:, h * HEAD_DIM:(h + 1) * HEAD_DIM] = og[g * BLOCK:(g + 1) * BLOCK] * invs[g]
        ag = jnp.concatenate([ag0_ref[...], ag1_ref[...]], axis=1).astype(F32)
        z_ref[...] = (y_ref[...] * (ag * _sigmoid(ag))).astype(BF16)

    prev = lambda c: (lambda i: (jnp.maximum(i - 1, 0), c))
    cur = lambda c: (lambda i: (i, c))
    return pl.pallas_call(
        body,
        name="attn_fwd",
        grid=(nb,),
        in_specs=[
            pl.BlockSpec(memory_space=pltpu.SMEM),
            _bias_spec(),
            pl.BlockSpec((BLOCK, 1024), lambda i: (i, COL_Q // 4)),
            pl.BlockSpec((BLOCK, D_KV), prev(COL_K)),
            pl.BlockSpec((BLOCK, D_KV), cur(COL_K)),
            pl.BlockSpec((BLOCK, D_KV), prev(COL_V)),
            pl.BlockSpec((BLOCK, D_KV), cur(COL_V)),
            pl.BlockSpec((BLOCK, 512), lambda i: (i, COL_ATTN_GATE // 2)),
            pl.BlockSpec((BLOCK, 512), lambda i: (i, COL_ATTN_GATE // 2 + 1)),
        ],
        out_specs=[pl.BlockSpec((BLOCK, 1024), lambda i: (i, 0)), pl.BlockSpec((BLOCK, 1024), lambda i: (i, 0)),
                   pl.BlockSpec((BLOCK, N_Q_HEADS), lambda i: (i, 0))],
        out_shape=[_sds((T, 1024), F32), _sds((T, 1024), BF16), _sds((T, N_Q_HEADS), F32)],
        compiler_params=_params(("arbitrary",), 32),
    )(sinks, *_hbm(bias, proj, proj, proj, proj, proj, proj, proj))


def _attn_bwd(proj, y_attn, lse, dz_attn, sinks, bias, token):
    T = proj.shape[0]
    nb = T // BLOCK

    def body(sink_ref, bias_ref, q_ref, kp_ref, kc_ref, vp_ref, vc_ref, ag0_ref, ag1_ref, y_ref, lse_ref, dz_ref,
             token_ref, dq_ref, dk_ref, dv_ref, dag_ref, ds_ref, dy_s):
        i = pl.program_id(0)

        @pl.when(i == 0)
        def _():
            ds_ref[...] = jnp.zeros_like(ds_ref)

        lane = lax.broadcasted_iota(jnp.int32, (8, 128), 1)
        sub = lax.broadcasted_iota(jnp.int32, (8, 128), 0)
        ag = jnp.concatenate([ag0_ref[...], ag1_ref[...]], axis=1).astype(F32)
        dz = dz_ref[...]
        sg = _sigmoid(ag)
        dag_ref[...] = (dz * y_ref[...] * (sg * (1.0 + ag * (1.0 - sg)))).astype(BF16)
        dy_s[...] = dz * (ag * sg)
        r_cur = pl.multiple_of(i * BLOCK, BLOCK)
        r_prev = pl.multiple_of(jnp.maximum(i - 1, 0) * BLOCK, BLOCK)
        dk_cur, dv_cur, dk_prev, dv_prev = [], [], [], []
        ds_acc = jnp.zeros((8, 128), F32)
        for hk in range(N_KV_HEADS):
            ks = slice(hk * HEAD_DIM, (hk + 1) * HEAD_DIM)
            qg = (_stack_heads(q_ref, hk, F32) * ATTN_SCALE).astype(BF16)
            dog = _stack_heads(dy_s, hk, F32)
            og = _stack_heads(y_ref, hk, F32)
            dog_b = dog.astype(BF16)
            kp = kp_ref[:, ks].astype(BF16)
            kc = kc_ref[:, ks].astype(BF16)
            vp = vp_ref[:, ks].astype(BF16)
            vc = vc_ref[:, ks].astype(BF16)
            s_prev = _dot_nt(qg, kp)
            s_cur = _dot_nt(qg, kc)
            dp_prev = _dot_nt(dog_b, vp)
            dp_cur = _dot_nt(dog_b, vc)
            dvec = jnp.sum(dog * og, axis=-1, keepdims=True)
            pp, pc, dsp, dsc = [], [], [], []
            for g in range(GROUP):
                h = GROUP * hk + g
                rows = slice(g * BLOCK, (g + 1) * BLOCK)
                p_prev, p_cur, p_sink = _attn_probs(
                    s_prev[rows], s_cur[rows], sink_ref[h], bias_ref[h], lse_ref[:, h:h + 1])
                d_h = dvec[rows]
                pp.append(p_prev.astype(BF16))
                pc.append(p_cur.astype(BF16))
                dsp.append((p_prev * (dp_prev[rows] - d_h)).astype(BF16))
                dsc.append((p_cur * (dp_cur[rows] - d_h)).astype(BF16))
                dsink = -jnp.sum(p_sink * d_h, axis=0, keepdims=True)
                ds_acc = ds_acc + jnp.where(jnp.logical_and(lane == h, sub == 1), dsink, 0.0)
            pp = jnp.concatenate(pp, axis=0)
            pc = jnp.concatenate(pc, axis=0)
            dsp = jnp.concatenate(dsp, axis=0)
            dsc = jnp.concatenate(dsc, axis=0)
            dqg = (_dot(dsp, kp) + _dot(dsc, kc)) * ATTN_SCALE
            for g in range(GROUP):
                h = GROUP * hk + g
                dq_ref[:, h * HEAD_DIM:(h + 1) * HEAD_DIM] = dqg[g * BLOCK:(g + 1) * BLOCK].astype(BF16)
            dk_ref[pl.ds(r_cur, BLOCK), ks] = _dot_tn(dsc, qg)
            dv_ref[pl.ds(r_cur, BLOCK), ks] = _dot_tn(pc, dog_b)
            dk_prev.append(_dot_tn(dsp, qg))
            dv_prev.append(_dot_tn(pp, dog_b))
        ds_ref[:, 0:128] += ds_acc

        @pl.when(i > 0)
        def _():
            for hk in range(N_KV_HEADS):
                ks = slice(hk * HEAD_DIM, (hk + 1) * HEAD_DIM)
                dk_ref[pl.ds(r_prev, BLOCK), ks] += dk_prev[hk]
                dv_ref[pl.ds(r_prev, BLOCK), ks] += dv_prev[hk]

    prev = lambda c: (lambda i: (jnp.maximum(i - 1, 0), c))
    cur = lambda c: (lambda i: (i, c))
    blk = pl.BlockSpec((BLOCK, 1024), lambda i: (i, 0))
    whole = pl.BlockSpec((T, D_KV), lambda i: (0, 0))
    return pl.pallas_call(
        body,
        name="attn_bwd",
        grid=(nb,),
        in_specs=[
            pl.BlockSpec(memory_space=pltpu.SMEM),
            _bias_spec(),
            pl.BlockSpec((BLOCK, 1024), lambda i: (i, COL_Q // 4)),
            pl.BlockSpec((BLOCK, D_KV), prev(COL_K)),
            pl.BlockSpec((BLOCK, D_KV), cur(COL_K)),
            pl.BlockSpec((BLOCK, D_KV), prev(COL_V)),
            pl.BlockSpec((BLOCK, D_KV), cur(COL_V)),
            pl.BlockSpec((BLOCK, 512), lambda i: (i, COL_ATTN_GATE // 2)),
            pl.BlockSpec((BLOCK, 512), lambda i: (i, COL_ATTN_GATE // 2 + 1)),
            blk,
            pl.BlockSpec((BLOCK, N_Q_HEADS), lambda i: (i, 0)),
            blk,
            pl.BlockSpec((8, 128), lambda i: (0, 0)),
        ],
        out_specs=[blk, whole, whole, blk, pl.BlockSpec((8, 1024), lambda i: (0, 0))],
        out_shape=[_sds((T, 1024), BF16), _sds((T, D_KV), F32), _sds((T, D_KV), F32), _sds((T, 1024), BF16),
                   _sds((8, 1024), F32)],
        scratch_shapes=[pltpu.VMEM((BLOCK, 1024), F32)],
        compiler_params=_params(("arbitrary",), 48),
    )(sinks, *_hbm(bias, proj, proj, proj, proj, proj, proj, proj, y_attn, lse, dz_attn, token))


def _head(x, target, z_rnn, z_attn, proj, b_gate, g_post, w_rnn_out, w_attn_out, w_out):
    T = x.shape[0]
    tm = 256

    def body(x_ref, t_ref, zr_ref, za_ref, ml0_ref, ml1_ref, ml2_ref, ml3_ref, bg_ref, gp_ref, wr_ref, wa_ref, wo_ref,
             dyx_ref, dzr_ref, dza_ref, dml_ref, dout_ref, dbr_ref, dba_ref, mt_ref, zat_ref, sm_ref):
        @pl.when(pl.program_id(0) == 0)
        def _():
            sm_ref[...] = jnp.zeros_like(sm_ref)

        wr, wa, wo = wr_ref[...], wa_ref[...], wo_ref[...]
        br_rnn = _dot(zr_ref[...], wr)
        br_attn = _dot(za_ref[...], wa)
        zat_ref[...] = za_ref[...].astype(F32).T.astype(BF16)
        ml_rnn = jnp.concatenate([ml0_ref[...], ml1_ref[...]], axis=1).astype(F32)
        ml_attn = jnp.concatenate([ml2_ref[...], ml3_ref[...]], axis=1).astype(F32)
        g_rnn = _sigmoid(ml_rnn + bg_ref[:, 0:D_MODEL])
        g_attn = _sigmoid(ml_attn + bg_ref[:, D_MODEL:2 * D_MODEL])
        merged = g_rnn * br_rnn + g_attn * br_attn
        mb = merged.astype(BF16)
        mt_ref[...] = merged.T.astype(BF16)
        out = _dot(mb, wo)
        rstd = lax.rsqrt(jnp.mean(out * out, axis=-1, keepdims=True) + EPS)
        n = out * rstd
        gp = gp_ref[...]
        err = (x_ref[...] + n * gp) - t_ref[...]
        sm_ref[pl.ds(3, 1), :] += 0.5 * jnp.sum(jnp.mean(err * err, axis=-1, keepdims=True), axis=0, keepdims=True)
        dy = err * (1.0 / D_MODEL)
        dyx_ref[...] = dy
        sm_ref[pl.ds(0, 1), :] += jnp.sum(dy * n, axis=0, keepdims=True)
        dn = dy * gp
        dout = (rstd * (dn - n * jnp.mean(dn * n, axis=-1, keepdims=True))).astype(BF16)
        dout_ref[...] = dout
        dmerged = _dot_nt(dout, wo)
        dml_r = (dmerged * br_rnn) * (g_rnn * (1.0 - g_rnn))
        dml_a = (dmerged * br_attn) * (g_attn * (1.0 - g_attn))
        dml_ref[:, 0:D_MODEL] = dml_r.astype(BF16)
        dml_ref[:, D_MODEL:2 * D_MODEL] = dml_a.astype(BF16)
        sm_ref[pl.ds(1, 1), :] += jnp.sum(dml_r, axis=0, keepdims=True)
        sm_ref[pl.ds(2, 1), :] += jnp.sum(dml_a, axis=0, keepdims=True)
        dbr = (dmerged * g_rnn).astype(BF16)
        dba = (dmerged * g_attn).astype(BF16)
        dbr_ref[...] = dbr
        dba_ref[...] = dba
        dzr_ref[...] = _dot_nt(dbr, wr)
        dza_ref[...] = _dot_nt(dba, wa)

    tile = pl.BlockSpec((tm, D_MODEL), lambda i: (i, 0))
    wspec = pl.BlockSpec((D_MODEL, D_MODEL), lambda i: (0, 0))
    ml = lambda q: pl.BlockSpec((tm, 512), lambda i: (i, COL_MERGE // 2 + q))
    return pl.pallas_call(
        body,
        name="head",
        grid=(T // tm,),
        in_specs=[
            tile, tile, tile, tile,
            ml(0), ml(1), ml(2), ml(3),
            pl.BlockSpec((1, 2 * D_MODEL), lambda i: (0, 0)),
            pl.BlockSpec((1, D_MODEL), lambda i: (0, 0)),
            wspec, wspec, wspec,
        ],
        out_specs=[
            tile, tile, tile,
            pl.BlockSpec((tm, 2 * D_MODEL), lambda i: (i, 0)),
            tile, tile, tile,
            pl.BlockSpec((D_MODEL, tm), lambda i: (0, i)), pl.BlockSpec((D_MODEL, tm), lambda i: (0, i)),
            pl.BlockSpec((8, D_MODEL), lambda i: (0, 0)),
        ],
        out_shape=[
            _sds((T, D_MODEL), F32), _sds((T, D_MODEL), F32), _sds((T, D_MODEL), F32),
            _sds((T, 2 * D_MODEL), BF16),
            _sds((T, D_MODEL), BF16), _sds((T, D_MODEL), BF16), _sds((T, D_MODEL), BF16),
            _sds((D_MODEL, T), BF16), _sds((D_MODEL, T), BF16),
            _sds((8, D_MODEL), F32),
        ],
        compiler_params=_params(("arbitrary",), 56),
    )(*_hbm(x, target, z_rnn, z_attn, proj, proj, proj, proj, b_gate, g_post, w_rnn_out, w_attn_out, w_out))


def _matmul_t(at, b, name):
    M, T = at.shape
    N = b.shape[1]
    tk = min(1024, T)
    nt = T // tk

    def body(a_ref, b_ref, o_ref, ob_ref):
        @pl.when(pl.program_id(0) == 0)
        def _():
            o_ref[...] = jnp.zeros_like(o_ref)

        o_ref[...] += _dot(a_ref[...], b_ref[...])

        @pl.when(pl.program_id(0) == nt - 1)
        def _():
            ob_ref[...] = o_ref[...].astype(BF16)

    whole = pl.BlockSpec((M, N), lambda t: (0, 0))
    return pl.pallas_call(
        body,
        name=name,
        grid=(nt,),
        in_specs=[pl.BlockSpec((M, tk), lambda t: (0, t)), pl.BlockSpec((tk, N), lambda t: (t, 0))],
        out_specs=[whole, whole],
        out_shape=[_sds((M, N), F32), _sds((M, N), BF16)],
        compiler_params=_params(("arbitrary",), 48),
    )(*_hbm(at, b))


DPROJ_WIDTHS = (D_RNN, D_RNN, 1024, D_KV, D_KV, 1024, 2 * D_MODEL)


def _dproj_segments():
    segs, start = [[] for _ in range(N_CHIPS)], 0
    for p, width in enumerate(DPROJ_WIDTHS):
        for c in range(N_CHIPS):
            lo, hi = max(start, c * W_IN_SHARD), min(start + width, (c + 1) * W_IN_SHARD)
            if lo < hi:
                segs[c].append((p, lo - start, hi - start, lo - c * W_IN_SHARD, hi - c * W_IN_SHARD))
        start += width
    return segs


def _dh_bwd(pieces, w_in_g, x, dyx, g_pre, token):
    T = x.shape[0]
    tm = min(512, T)
    n = len(pieces)
    segs = _dproj_segments()

    def body(*refs):
        p_refs, w_hbm, x_ref, dyx_ref, g_ref = refs[0:n], refs[n], refs[n + 1], refs[n + 2], refs[n + 3]
        gx_ref, dg_ref, w_ref, w_sems = refs[n + 5], refs[n + 6], refs[n + 7], refs[n + 8]
        first = pl.program_id(0) == 0
        w_copies = [pltpu.make_async_copy(w_hbm.at[c], w_ref.at[c], w_sems.at[c]) for c in range(N_CHIPS)]

        @pl.when(first)
        def _():
            for cp in w_copies:
                cp.start()
            dg_ref[...] = jnp.zeros_like(dg_ref)

        dh = None
        for c in range(N_CHIPS):
            pl.when(first)(w_copies[c].wait)
            for p, a0, a1, u0, u1 in segs[c]:
                part = _dot_nt(p_refs[p][:, a0:a1].astype(BF16), w_ref[c, :, u0:u1])
                dh = part if dh is None else dh + part
        xv = x_ref[...]
        rstd = lax.rsqrt(jnp.mean(xv * xv, axis=-1, keepdims=True) + EPS)
        nx = xv * rstd
        dhg = dh * g_ref[...]
        gx_ref[...] = dyx_ref[...] + rstd * (dhg - nx * jnp.mean(dhg * nx, axis=-1, keepdims=True))
        dg_ref[pl.ds(0, 1), :] += jnp.sum(dh * nx, axis=0, keepdims=True)

    tile = pl.BlockSpec((tm, D_MODEL), lambda i: (i, 0))
    return pl.pallas_call(
        body,
        name="dh_bwd",
        grid=(T // tm,),
        in_specs=[pl.BlockSpec((tm, w), lambda i: (i, 0)) for w in DPROJ_WIDTHS] + [
            ANY, tile, tile,
            pl.BlockSpec((1, D_MODEL), lambda i: (0, 0)),
            pl.BlockSpec((8, 128), lambda i: (0, 0)),
        ],
        out_specs=[tile, pl.BlockSpec((8, D_MODEL), lambda i: (0, 0))],
        out_shape=[_sds((T, D_MODEL), F32), _sds((8, D_MODEL), F32)],
        scratch_shapes=[pltpu.VMEM(w_in_g.shape, BF16), pltpu.SemaphoreType.DMA((N_CHIPS,))],
        compiler_params=_params(("arbitrary",), 56),
    )(*_hbm(*pieces, w_in_g, x, dyx, g_pre, token))


def _dw_in(ht, pieces):
    T = ht.shape[1]
    tk = min(512, T)
    nt = T // tk
    n = len(pieces)
    segs = _dproj_segments()

    def body(*refs):
        h_ref, p_refs, o_ref, ob_ref = refs[0], refs[1:n + 1], refs[n + 1], refs[n + 2]

        @pl.when(pl.program_id(1) == 0)
        def _():
            o_ref[...] = jnp.zeros_like(o_ref)

        for c in range(N_CHIPS):
            @pl.when(pl.program_id(0) == c)
            def _():
                for p, a0, a1, u0, u1 in segs[c]:
                    o_ref[:, u0:u1] += _dot(h_ref[...], p_refs[p][:, a0:a1].astype(BF16))

        @pl.when(pl.program_id(1) == nt - 1)
        def _():
            ob_ref[...] = o_ref[...].astype(BF16)

    def piece_spec(p):
        chips = [c for c in range(N_CHIPS) if any(s[0] == p for s in segs[c])]

        def index(c, t):
            used = functools.reduce(jnp.logical_or, [c == k for k in chips])
            return (jnp.where(used, t, 0), 0)

        return pl.BlockSpec((tk, DPROJ_WIDTHS[p]), index)

    return pl.pallas_call(
        body,
        name="dw_in",
        grid=(N_CHIPS, nt),
        in_specs=[pl.BlockSpec((D_MODEL, tk), lambda c, t: (0, t))] + [piece_spec(p) for p in range(n)],
        out_specs=[pl.BlockSpec((None, D_MODEL, W_IN_SHARD), lambda c, t: (c, 0, 0))] * 2,
        out_shape=[_sds((N_CHIPS, D_MODEL, W_IN_SHARD), F32), _sds((N_CHIPS, D_MODEL, W_IN_SHARD), BF16)],
        compiler_params=_params(("parallel", "arbitrary"), 56),
    )(*_hbm(ht, *pieces))


ELEMENTWISE_TILE_BYTES = MIB


def _row_tile(rows, cols):
    if rows * cols * 4 <= ELEMENTWISE_TILE_BYTES:
        return rows
    for t in (512, 256, 128, 64, 32, 16, 8):
        if rows % t == 0 and t * cols * 4 <= ELEMENTWISE_TILE_BYTES:
            return t
    return rows


def _pair_sum(g, got, chip_core, name):
    nch, R, C = g.shape
    h = R // 2
    tr = _row_tile(h, C)
    nt = h // tr

    def body(jc_ref, g_ref, got_ref, p_ref, pb_ref):
        s = g_ref[...] + got_ref[...].astype(F32)
        pb_ref[...] = s.astype(BF16)

        @pl.when(pl.program_id(1) == jc_ref[0])
        def _():
            p_ref[...] = s

    return pl.pallas_call(
        body,
        name=name,
        grid_spec=pltpu.PrefetchScalarGridSpec(
            num_scalar_prefetch=1,
            grid=(nt, nch),
            in_specs=[pl.BlockSpec((None, tr, C), lambda i, j, jc_ref: (j, jc_ref[1] * nt + i, 0)),
                      pl.BlockSpec((None, tr, C), lambda i, j, jc_ref: (j, i, 0))],
            out_specs=[pl.BlockSpec((tr, C), lambda i, j, jc_ref: (i, 0)),
                       pl.BlockSpec((None, tr, C), lambda i, j, jc_ref: (j, i, 0))],
        ),
        out_shape=[_sds((h, C), F32), _sds((nch, h, C), BF16)],
        compiler_params=_params(("parallel", "arbitrary"), 48),
    )(chip_core, *_hbm(g, got))


def _chip_sum(p, got, chip_core, name):
    h, C = p.shape
    tr = _row_tile(h, C)
    nt = h // tr

    def body(jc_ref, p_ref, g0_ref, g1_ref, g2_ref, o_ref):
        o_ref[...] = ((p_ref[...] + g0_ref[...].astype(F32)) + g1_ref[...].astype(F32)) + g2_ref[...].astype(F32)

    rel = lambda r: pl.BlockSpec((None, tr, C), lambda i, jc_ref: (r, i, 0))
    return pl.pallas_call(
        body,
        name=name,
        grid_spec=pltpu.PrefetchScalarGridSpec(
            num_scalar_prefetch=1,
            grid=(nt,),
            in_specs=[pl.BlockSpec((tr, C), lambda i, jc_ref: (i, 0)), rel(0), rel(1), rel(2)],
            out_specs=pl.BlockSpec((tr, C), lambda i, jc_ref: (jc_ref[1] * nt + i, 0)),
        ),
        out_shape=_sds((2 * h, C), F32),
        compiler_params=_params(("parallel",), 48),
    )(chip_core, *_hbm(p, got, got, got))


def _place_shards(shards, chip, name):
    n = len(shards)
    tiles = [_row_tile(s.shape[0], s.shape[1]) for s in shards]
    steps = max(s.shape[0] // t for s, t in zip(shards, tiles))
    tiles = [s.shape[0] // steps for s in shards]

    def body(j_ref, *refs):
        for a in range(n):
            refs[n + a][...] = refs[a][...].astype(BF16)
        refs[2 * n][...] = refs[0][...].astype(BF16)

    outs = pl.pallas_call(
        body,
        name=name,
        grid_spec=pltpu.PrefetchScalarGridSpec(
            num_scalar_prefetch=1,
            grid=(steps,),
            in_specs=[pl.BlockSpec((t, s.shape[1]), lambda i, j_ref: (i, 0)) for s, t in zip(shards, tiles)],
            out_specs=[pl.BlockSpec((None, t, s.shape[1]), lambda i, j_ref: (j_ref[0], i, 0))
                       for s, t in zip(shards, tiles)] + [pl.BlockSpec((tiles[0], shards[0].shape[1]),
                                                                       lambda i, j_ref: (i, 0))],
        ),
        out_shape=[_sds((N_CHIPS,) + s.shape, BF16) for s in shards] + [_sds(shards[0].shape, BF16)],
        compiler_params=_params(("parallel",), 48),
    )(chip, *_hbm(*shards))
    return list(outs[:n]), outs[n]


def _adamw(w, g, m, v, name):
    R, C = w.shape
    tr = _row_tile(R, C)
    c1 = 1.0 - ADAM_B1 ** ADAM_STEP
    c2 = 1.0 - ADAM_B2 ** ADAM_STEP

    def body(w_ref, g_ref, m_ref, v_ref, d_ref, nm_ref, nv_ref):
        g = g_ref[...]
        nm = ADAM_B1 * m_ref[...] + (1.0 - ADAM_B1) * g
        nv = ADAM_B2 * v_ref[...] + (1.0 - ADAM_B2) * (g * g)
        nm_ref[...] = nm
        nv_ref[...] = nv
        d_ref[...] = (-ADAM_LR) * ((nm / c1) / (jnp.sqrt(nv / c2) + ADAM_EPS) + ADAM_WD * w_ref[...])

    spec = pl.BlockSpec((tr, C), lambda i: (i, 0))
    return pl.pallas_call(
        body, name=name, grid=(R // tr,), in_specs=[spec] * 4, out_specs=[spec] * 3,
        out_shape=[_sds((R, C), F32)] * 3, compiler_params=_params(("parallel",), 48),
    )(*_hbm(w, g, m, v))


def _place():
    return lax.axis_index("x"), lax.axis_index("y"), lax.axis_index("c")


def _chip_of(x, y, r):
    return (x ^ (r >> 1), y ^ (r & 1))


ANY = pl.BlockSpec(memory_space=pl.ANY)


def _gather_weights(placed, cw8):
    nbig = len(placed)
    halves = [s.shape[1] // 2 for s in placed]
    pieces = [max(1, h // 64) for h in halves]
    rows = [h // p for h, p in zip(halves, pieces)]
    order = [(a, q) for q in range(max(pieces)) for a in range(nbig) if q < pieces[a]]
    ici_sem = {(a, q, r): 3 * i + (r - 1) for i, (a, q) in enumerate(order) for r in (1, 2, 3)}
    cw_sem = {r: 3 * len(order) + (r - 1) for r in (1, 2, 3)}
    d2d_sem = {key: 3 * len(order) + 3 + k for key, k in ici_sem.items()}
    nsem = 6 * len(order) + 3

    def body(*refs):
        cw_ref, dsts, gcw_ref = refs[nbig], refs[nbig + 1:2 * nbig + 1], refs[2 * nbig + 1]
        send_sems, recv_sems = refs[2 * nbig + 2:]
        x, y, c = _place()
        j = 2 * x + y

        def piece_rows(a, q, core):
            return pl.ds(pl.multiple_of(core * halves[a] + q * rows[a], 16), rows[a])

        def ici(a, q, r):
            tx, ty = _chip_of(x, y, r)
            k = ici_sem[(a, q, r)]
            region = dsts[a].at[j, piece_rows(a, q, c), :]
            return pltpu.make_async_remote_copy(
                src_ref=region, dst_ref=region, send_sem=send_sems.at[k], recv_sem=recv_sems.at[k],
                device_id=(tx, ty, c), device_id_type=MESH)

        def ici_landed(a, q, r):
            tx, ty = _chip_of(x, y, r)
            k = ici_sem[(a, q, r)]
            region = dsts[a].at[2 * tx + ty, piece_rows(a, q, c), :]
            return pltpu.make_async_remote_copy(
                src_ref=region, dst_ref=region, send_sem=send_sems.at[k], recv_sem=recv_sems.at[k],
                device_id=(tx, ty, c), device_id_type=MESH)

        def d2d(a, q, r, core):
            tx, ty = _chip_of(x, y, r)
            k = d2d_sem[(a, q, r)]
            region = dsts[a].at[2 * tx + ty, piece_rows(a, q, core), :]
            return pltpu.make_async_remote_copy(
                src_ref=region, dst_ref=region, send_sem=send_sems.at[k], recv_sem=recv_sems.at[k],
                device_id=(x, y, 1 - c), device_id_type=MESH)

        def cw_copy(r):
            tx, ty = _chip_of(x, y, r)
            k = cw_sem[r]
            return pltpu.make_async_remote_copy(
                src_ref=cw_ref, dst_ref=gcw_ref.at[j], send_sem=send_sems.at[k], recv_sem=recv_sems.at[k],
                device_id=(tx, ty, c), device_id_type=MESH)

        def cw_landed(r):
            tx, ty = _chip_of(x, y, r)
            k = cw_sem[r]
            region = gcw_ref.at[2 * tx + ty]
            return pltpu.make_async_remote_copy(
                src_ref=region, dst_ref=region, send_sem=send_sems.at[k], recv_sem=recv_sems.at[k],
                device_id=(tx, ty, c), device_id_type=MESH)

        def relay(a, q, origin, to):
            ox, oy = _chip_of(x, y, origin)
            tx, ty = _chip_of(x, y, to)
            k = ici_sem[(a, q, 3)]
            region = dsts[a].at[2 * ox + oy, piece_rows(a, q, c), :]
            return pltpu.make_async_remote_copy(
                src_ref=region, dst_ref=region, send_sem=send_sems.at[k], recv_sem=recv_sems.at[k],
                device_id=(tx, ty, c), device_id_type=MESH)

        first = [ici(a, q, r) for (a, q) in order for r in (1, 2)] + [cw_copy(r) for r in (1, 2, 3)]
        for cp in first:
            cp.start()
        passed = []
        for (a, q) in order:
            for r in (1, 2):
                ici_landed(a, q, r).wait_recv()
                if q % 2 == r - 1:
                    cp = relay(a, q, r, 3 - r)
                    cp.start()
                    passed.append(cp)
                cp = d2d(a, q, r, c)
                cp.start()
                passed.append(cp)
        for (a, q) in order:
            ici_landed(a, q, 3).wait_recv()
            cp = d2d(a, q, 3, c)
            cp.start()
            passed.append(cp)
        for r in (1, 2, 3):
            cw_landed(r).wait_recv()
        for (a, q) in order:
            for r in (1, 2, 3):
                d2d(a, q, r, 1 - c).wait_recv()
        for cp in first + passed:
            cp.wait_send()

    return pl.pallas_call(
        body,
        name="gather_weights",
        in_specs=[ANY] * (nbig + 1),
        out_specs=[ANY] * (nbig + 1),
        out_shape=[_sds(s.shape, s.dtype) for s in placed] + [_sds((N_CHIPS,) + cw8.shape, cw8.dtype)],
        input_output_aliases={a: a for a in range(nbig)},
        scratch_shapes=[pltpu.SemaphoreType.DMA((nsem,)), pltpu.SemaphoreType.DMA((nsem,))],
    )(*placed, cw8)


def _w_in_copies(p_ref, cw_ref, g_ref, send_sems, recv_sems, r):
    x, y, c = _place()
    tx, ty = _chip_of(x, y, r)
    half = p_ref.shape[1] // 2
    rows = pl.ds(pl.multiple_of(c * half, 16), half)

    def pair(chip_index):
        w = pltpu.make_async_remote_copy(
            src_ref=p_ref.at[chip_index, rows, :], dst_ref=p_ref.at[chip_index, rows, :],
            send_sem=send_sems.at[r - 1], recv_sem=recv_sems.at[r - 1], device_id=(tx, ty, c), device_id_type=MESH)
        cw = pltpu.make_async_remote_copy(
            src_ref=cw_ref, dst_ref=g_ref.at[chip_index],
            send_sem=send_sems.at[2 + r], recv_sem=recv_sems.at[2 + r], device_id=(tx, ty, c), device_id_type=MESH)
        return w, cw

    return pair(2 * x + y), pair(2 * tx + ty)


def _gather_in_start(placed, cw8, name):
    landing = lax.empty((N_CHIPS,) + cw8.shape, cw8.dtype)

    def body(p_ref, cw_ref, g_ref, send_sems, recv_sems, p_thru, cw_thru, g_thru, token):
        for r in (1, 2, 3):
            for cp in _w_in_copies(p_ref, cw_ref, g_ref, send_sems, recv_sems, r)[0]:
                cp.start()
        token[...] = jnp.zeros_like(token)

    arrays = [placed, cw8, landing]
    hbm = lambda t: pltpu.HBM(t.shape, t.dtype)
    keep = lambda t: pltpu.with_memory_space_constraint(t, pltpu.HBM)
    outs = pl.pallas_call(
        body,
        name=name,
        in_specs=[HBM] * 3,
        out_specs=(SEM, SEM, HBM, HBM, HBM, pl.BlockSpec(memory_space=pltpu.VMEM)),
        out_shape=(pltpu.SemaphoreType.DMA((6,)), pltpu.SemaphoreType.DMA((6,)), *[hbm(t) for t in arrays],
                   jax.ShapeDtypeStruct((8, 128), F32)),
        input_output_aliases={0: 2, 1: 3, 2: 4},
        compiler_params=pltpu.CompilerParams(has_side_effects=DATAFLOW),
    )(*[keep(t) for t in arrays])
    return outs[0], outs[1], list(outs[2:5]), outs[5]


def _gather_in_wait(send_sems, recv_sems, thru, after, rels, name):
    def body(p_ref, cw_ref, g_ref, send_sems, recv_sems, after_ref, p_out, cw_out, g_out):
        for r in rels:
            sent, landed = _w_in_copies(p_ref, cw_ref, g_ref, send_sems, recv_sems, r)
            for cp in sent:
                cp.wait_send()
            for cp in landed:
                cp.wait_recv()

    hbm = lambda t: pltpu.HBM(t.shape, t.dtype)
    outs = pl.pallas_call(
        body,
        name=name,
        in_specs=[HBM] * 3 + [SEM, SEM, ANY],
        out_specs=[HBM] * 3,
        out_shape=[hbm(t) for t in thru],
        input_output_aliases={0: 0, 1: 1, 2: 2},
        compiler_params=pltpu.CompilerParams(has_side_effects=DATAFLOW),
    )(*thru, send_sems, recv_sems, after)
    return list(outs)


def _pair_forward(placed, rels, name):
    half = placed.shape[1] // 2

    def body(p_in, p_ref, send_sems, recv_sems):
        x, y, c = _place()

        def copy(k, region):
            return pltpu.make_async_remote_copy(
                src_ref=region, dst_ref=region, send_sem=send_sems.at[k], recv_sem=recv_sems.at[k],
                device_id=(x, y, 1 - c), device_id_type=MESH)

        blocks = [2 * tx + ty for tx, ty in (_chip_of(x, y, r) for r in rels)]
        for k, b in enumerate(blocks):
            for q in range(half // D2D_PIECE_ROWS):
                rows = pl.ds(pl.multiple_of(c * half + q * D2D_PIECE_ROWS, 16), D2D_PIECE_ROWS)
                copy(k, p_ref.at[b, rows, :]).start()
        for k, b in enumerate(blocks):
            copy(k, p_ref.at[b, pl.ds(pl.multiple_of(c * half, 16), half), :]).wait_send()
            copy(k, p_ref.at[b, pl.ds(pl.multiple_of((1 - c) * half, 16), half), :]).wait_recv()

    return pl.pallas_call(
        body,
        name=name,
        in_specs=[ANY],
        out_specs=ANY,
        out_shape=_sds(placed.shape, placed.dtype),
        input_output_aliases={0: 0},
        scratch_shapes=[pltpu.SemaphoreType.DMA((len(rels),)), pltpu.SemaphoreType.DMA((len(rels),))],
    )(placed)


def _gather_late_start(placed, after, name):
    n = len(placed)
    halves = [s.shape[1] // 2 for s in placed]

    def body(*refs):
        g_refs = refs[0:n]
        send_sems, recv_sems, token = refs[n + 1], refs[n + 2], refs[-1]
        x, y, c = _place()
        j = 2 * x + y
        for a in range(n):
            mine = g_refs[a].at[j, pl.ds(pl.multiple_of(c * halves[a], 16), halves[a]), :]
            for r in (1, 2, 3):
                tx, ty = _chip_of(x, y, r)
                for to_core in (0, 1):
                    k = ((a * 3 + (r - 1)) * 2 + c) * 2 + to_core
                    pltpu.make_async_remote_copy(
                        src_ref=mine, dst_ref=mine, send_sem=send_sems.at[k], recv_sem=recv_sems.at[k],
                        device_id=(tx, ty, to_core), device_id_type=MESH).start()
        token[...] = jnp.zeros_like(token)

    hbm = lambda t: pltpu.HBM(t.shape, t.dtype)
    keep = lambda t: pltpu.with_memory_space_constraint(t, pltpu.HBM)
    nsem = 12 * n
    outs = pl.pallas_call(
        body,
        name=name,
        in_specs=[HBM] * n + [ANY],
        out_specs=(SEM, SEM, *[HBM] * n, pl.BlockSpec(memory_space=pltpu.VMEM)),
        out_shape=(pltpu.SemaphoreType.DMA((nsem,)), pltpu.SemaphoreType.DMA((nsem,)), *[hbm(p) for p in placed],
                   jax.ShapeDtypeStruct((8, 128), F32)),
        input_output_aliases={i: 2 + i for i in range(n)},
        compiler_params=pltpu.CompilerParams(has_side_effects=DATAFLOW),
    )(*[keep(p) for p in placed], after)
    return outs[0], outs[1], list(outs[2:2 + n]), outs[-1]


def _gather_late_wait(send_sems, recv_sems, thru, after, name):
    n = len(thru)
    halves = [s.shape[1] // 2 for s in thru]

    def body(*refs):
        g_refs = refs[0:n]
        send_sems, recv_sems = refs[n], refs[n + 1]
        x, y, c = _place()
        j = 2 * x + y
        for a in range(n):
            mine = g_refs[a].at[j, pl.ds(pl.multiple_of(c * halves[a], 16), halves[a]), :]
            for r in (1, 2, 3):
                tx, ty = _chip_of(x, y, r)
                for other in (0, 1):
                    k_out = ((a * 3 + (r - 1)) * 2 + c) * 2 + other
                    pltpu.make_async_remote_copy(
                        src_ref=mine, dst_ref=mine, send_sem=send_sems.at[k_out], recv_sem=recv_sems.at[k_out],
                        device_id=(tx, ty, other), device_id_type=MESH).wait_send()
                    k_in = ((a * 3 + (r - 1)) * 2 + other) * 2 + c
                    theirs = g_refs[a].at[2 * tx + ty, pl.ds(other * halves[a], halves[a]), :]
                    pltpu.make_async_remote_copy(
                        src_ref=theirs, dst_ref=theirs, send_sem=send_sems.at[k_in], recv_sem=recv_sems.at[k_in],
                        device_id=(tx, ty, other), device_id_type=MESH).wait_recv()

    hbm = lambda t: pltpu.HBM(t.shape, t.dtype)
    outs = pl.pallas_call(
        body,
        name=name,
        in_specs=[HBM] * n + [SEM, SEM, ANY],
        out_specs=[HBM] * n,
        out_shape=[hbm(t) for t in thru],
        input_output_aliases={i: i for i in range(n)},
        compiler_params=pltpu.CompilerParams(has_side_effects=DATAFLOW),
    )(*thru, send_sems, recv_sems, after)
    return list(outs)


D2D_PIECE_ROWS = 64


def _pair_exchange(grads, name):
    n = len(grads)
    halves = [g.shape[1] // 2 for g in grads]

    def body(*refs):
        g_refs, got_refs = refs[0:n], refs[n:2 * n]
        send_sems, recv_sems = refs[2 * n:]
        x, y, c = _place()

        def copy(a, src, dst):
            return pltpu.make_async_remote_copy(
                src_ref=src, dst_ref=dst, send_sem=send_sems.at[a], recv_sem=recv_sems.at[a],
                device_id=(x, y, 1 - c), device_id_type=MESH)

        for a in range(n):
            for jj in range(N_CHIPS):
                for q in range(halves[a] // D2D_PIECE_ROWS):
                    src_rows = pl.ds(pl.multiple_of((1 - c) * halves[a] + q * D2D_PIECE_ROWS, 16), D2D_PIECE_ROWS)
                    dst_rows = pl.ds(q * D2D_PIECE_ROWS, D2D_PIECE_ROWS)
                    copy(a, g_refs[a].at[jj, src_rows, :], got_refs[a].at[jj, dst_rows, :]).start()
        for a in range(n):
            sent = g_refs[a].at[:, pl.ds(pl.multiple_of((1 - c) * halves[a], 16), halves[a]), :]
            copy(a, sent, got_refs[a]).wait()

    return pl.pallas_call(
        body,
        name=name,
        in_specs=[ANY] * n,
        out_specs=[ANY] * n,
        out_shape=[_sds((N_CHIPS, h, g.shape[2]), g.dtype) for g, h in zip(grads, halves)],
        scratch_shapes=[pltpu.SemaphoreType.DMA((n,)), pltpu.SemaphoreType.DMA((n,))],
    )(*grads)


HBM = pl.BlockSpec(memory_space=pltpu.HBM)
SEM = pl.BlockSpec(memory_space=pltpu.SEMAPHORE)
DATAFLOW = pltpu.SideEffectType.DATAFLOW_SIDE_EFFECTING


def _chip_copy(p_refs, land_refs, send_sems, recv_sems, a, r, blocked):
    x, y, c = _place()
    tx, ty = _chip_of(x, y, r)
    k = a * 3 + (r - 1)
    return pltpu.make_async_remote_copy(
        src_ref=p_refs[a].at[2 * tx + ty] if blocked else p_refs[a], dst_ref=land_refs[a].at[r - 1],
        send_sem=send_sems.at[k], recv_sem=recv_sems.at[k], device_id=(tx, ty, c), device_id_type=MESH)


def _chip_exchange_start(psums, name, blocked=True):
    n = len(psums)
    lands = [lax.empty((3,) + (p.shape[1:] if blocked else p.shape), p.dtype) for p in psums]

    def body(*refs):
        p_refs, land_refs = refs[0:n], refs[n:2 * n]
        send_sems, recv_sems, token = refs[2 * n], refs[2 * n + 1], refs[-1]
        for a in range(n):
            for r in (1, 2, 3):
                _chip_copy(p_refs, land_refs, send_sems, recv_sems, a, r, blocked).start()
        token[...] = jnp.zeros_like(token)

    hbm = lambda t: pltpu.HBM(t.shape, t.dtype)
    keep = lambda t: pltpu.with_memory_space_constraint(t, pltpu.HBM)
    outs = pl.pallas_call(
        body,
        name=name,
        in_specs=[HBM] * (2 * n),
        out_specs=(SEM, SEM, *[HBM] * (2 * n), pl.BlockSpec(memory_space=pltpu.VMEM)),
        out_shape=(pltpu.SemaphoreType.DMA((3 * n,)), pltpu.SemaphoreType.DMA((3 * n,)),
                   *[hbm(p) for p in psums], *[hbm(l) for l in lands], _sds((8, 128), F32)),
        input_output_aliases={i: 2 + i for i in range(2 * n)},
        compiler_params=pltpu.CompilerParams(has_side_effects=DATAFLOW),
    )(*[keep(p) for p in psums], *[keep(l) for l in lands])
    return outs[0], outs[1], list(outs[2:2 + n]), list(outs[2 + n:2 + 2 * n]), outs[-1]


def _chip_exchange_wait(send_sems, recv_sems, p_thru, land_thru, after, name, blocked=True):
    n = len(p_thru)

    def body(*refs):
        p_refs, land_refs = refs[0:n], refs[n:2 * n]
        send_sems, recv_sems = refs[2 * n], refs[2 * n + 1]
        for a in range(n):
            for r in (1, 2, 3):
                copy = _chip_copy(p_refs, land_refs, send_sems, recv_sems, a, r, blocked)
                copy.wait_send()
                copy.wait_recv()

    hbm = lambda t: pltpu.HBM(t.shape, t.dtype)
    outs = pl.pallas_call(
        body,
        name=name,
        in_specs=[HBM] * (2 * n) + [SEM, SEM, ANY],
        out_specs=[HBM] * (2 * n),
        out_shape=[hbm(p) for p in p_thru] + [hbm(l) for l in land_thru],
        input_output_aliases={i: i for i in range(2 * n)},
        compiler_params=pltpu.CompilerParams(has_side_effects=DATAFLOW),
    )(*p_thru, *land_thru, send_sems, recv_sems, after)
    return list(outs[n:2 * n])


def _pair_share(fulls):
    n = len(fulls)
    halves = [f.shape[0] // 2 for f in fulls]

    def body(*refs):
        full_refs = refs[n:2 * n]
        send_sems, recv_sems = refs[2 * n:]
        x, y, c = _place()

        def half_of(a, core):
            return full_refs[a].at[pl.ds(pl.multiple_of(core * halves[a], 8), halves[a]), :]

        def remote(a, src, dst):
            return pltpu.make_async_remote_copy(
                src_ref=src, dst_ref=dst, send_sem=send_sems.at[a], recv_sem=recv_sems.at[a],
                device_id=(x, y, 1 - c), device_id_type=MESH)

        for a in range(n):
            for q in range(halves[a] // D2D_PIECE_ROWS):
                piece = full_refs[a].at[
                    pl.ds(pl.multiple_of(c * halves[a] + q * D2D_PIECE_ROWS, 8), D2D_PIECE_ROWS), :]
                remote(a, piece, piece).start()
        for a in range(n):
            remote(a, half_of(a, c), half_of(a, c)).wait_send()
            remote(a, half_of(a, 1 - c), half_of(a, 1 - c)).wait_recv()

    return pl.pallas_call(
        body,
        name="pair_share",
        in_specs=[ANY] * n,
        out_specs=[ANY] * n,
        out_shape=[_sds(f.shape, F32) for f in fulls],
        input_output_aliases={a: a for a in range(n)},
        scratch_shapes=[pltpu.SemaphoreType.DMA((n,)), pltpu.SemaphoreType.DMA((n,))],
    )(*fulls)


def _small_pair_sum(s):
    R, C = s.shape
    V = SMALL_VECTOR_ROWS

    def body(s_ref, v_ref, m_ref, sib, send_sem, recv_sem):
        x, y, c = _place()

        def to_sib(src, dst):
            return pltpu.make_async_remote_copy(
                src_ref=src, dst_ref=dst, send_sem=send_sem, recv_sem=recv_sem,
                device_id=(x, y, 1 - c), device_id_type=MESH)

        for q in range(R // 8):
            to_sib(s_ref.at[pl.ds(8 * q, 8), :], sib.at[pl.ds(8 * q, 8), :]).start()
        to_sib(s_ref, sib).wait()
        v_ref[...] = s_ref[pl.ds(0, V), :] + sib[pl.ds(0, V), :]
        m_ref[...] = (s_ref[pl.ds(V, R - V), :] + sib[pl.ds(V, R - V), :]).astype(BF16)

    return pl.pallas_call(
        body,
        name="small_pair_sum",
        in_specs=[pl.BlockSpec(memory_space=pltpu.VMEM)],
        out_specs=[pl.BlockSpec(memory_space=pltpu.VMEM)] * 2,
        out_shape=[jax.ShapeDtypeStruct((V, C), F32), jax.ShapeDtypeStruct((R - V, C), BF16)],
        scratch_shapes=[pltpu.VMEM((R, C), F32), pltpu.SemaphoreType.DMA, pltpu.SemaphoreType.DMA],
    )(s)


def _small_total(chip, own, landed):
    V, C = own[0].shape
    M = own[1].shape[0]

    def body(j_ref, v_ref, m_ref, lv_ref, lm_ref, o_ref, chips_v, chips_m):
        j = j_ref[0]
        chips_v[j] = v_ref[...]
        chips_m[j] = m_ref[...]
        for r in (1, 2, 3):
            chips_v[j ^ r] = lv_ref[r - 1]
            chips_m[j ^ r] = lm_ref[r - 1]
        o_ref[pl.ds(0, V), :] = (chips_v[0] + chips_v[1]) + (chips_v[2] + chips_v[3])
        o_ref[pl.ds(V, M), :] = (chips_m[0].astype(F32) + chips_m[1].astype(F32)) + (
            chips_m[2].astype(F32) + chips_m[3].astype(F32))

    vmem = pl.BlockSpec(memory_space=pltpu.VMEM)
    return pl.pallas_call(
        body,
        name="small_total",
        in_specs=[pl.BlockSpec(memory_space=pltpu.SMEM), vmem, vmem, vmem, vmem],
        out_specs=vmem,
        out_shape=jax.ShapeDtypeStruct((V + M, C), F32),
        scratch_shapes=[pltpu.VMEM((N_CHIPS, V, C), F32), pltpu.VMEM((N_CHIPS, M, C), BF16)],
    )(chip, own[0], own[1], landed[0], landed[1])


def _block_diag(w):
    w4 = w.reshape(4, 4, RNN_BLOCK_W, RNN_BLOCK_W)
    eye = jnp.eye(4, dtype=w.dtype)
    return jnp.einsum("jaik,ab->jaibk", w4, eye).reshape(4, RNN_TILE, RNN_TILE)


def _block_diag_part(d):
    d5 = d.reshape(4, 4, RNN_BLOCK_W, 4, RNN_BLOCK_W)
    return jnp.stack([d5[:, a, :, a, :] for a in range(4)], axis=1).reshape(RNN_BLOCKS, RNN_BLOCK_W, RNN_BLOCK_W)


def _local_grads(x, target, g_pre, project, b_gate, conv_b, w_rg_a, b_rg_a, w_rg_x, b_rg_x, lam, sinks,
                 out_weights, g_post, on_out_grads, on_w_in_grad):
    wa_bd = _block_diag(w_rg_a).astype(BF16)
    wx_bd = _block_diag(w_rg_x).astype(BF16)
    b_a = b_rg_a.reshape(1, D_RNN)
    b_x = b_rg_x.reshape(1, D_RNN)

    proj, ht, w_in_g, conv_w, fwd_token = project(x, g_pre)
    y_rnn, z_rnn, conv, z_rnn_t = _rnn_fwd(proj, conv_w, conv_b, wa_bd, wx_bd, b_a, b_x, lam, fwd_token)
    bias = _attn_bias()
    y_attn, z_attn, lse = _attn_fwd(proj, sinks, bias)
    w_rnn_out, w_attn_out, w_out = out_weights(z_attn)
    dyx, dz_rnn, dz_attn, dml, dout, dbr_rnn, dbr_attn, merged_t, z_attn_t, head_small = _head(
        x, target, z_rnn, z_attn, proj, b_gate, g_post, w_rnn_out, w_attn_out, w_out)
    out_grads = [_matmul_t(z_rnn_t, dbr_rnn, "dw_rnn_out"), _matmul_t(z_attn_t, dbr_attn, "dw_attn_out"),
                 _matmul_t(merged_t, dout, "dw_out")]
    shard_rows = lambda d: d.reshape(N_CHIPS, OUT_SHARD, D_MODEL)
    token = on_out_grads([shard_rows(g) for g, _ in out_grads], [shard_rows(gb) for _, gb in out_grads])
    dq, dk, dv, dag, attn_small = _attn_bwd(proj, y_attn, lse, dz_attn, sinks, bias, token)
    drx, drg, dwa_t, dwx_t, rnn_small = _rnn_bwd(proj, conv, y_rnn, dz_rnn, conv_w, wa_bd, wx_bd, b_a, b_x, lam)
    dproj = [drx, drg, dq, dk, dv, dag, dml]
    token = on_w_in_grad(*_dw_in(ht, dproj))
    grad_x, dh_small = _dh_bwd(dproj, w_in_g, x, dyx, g_pre, token)
    small = jnp.concatenate([rnn_small, head_small, dh_small + attn_small,
                             _block_diag_part(dwa_t).reshape(64, 1024), _block_diag_part(dwx_t).reshape(64, 1024)], axis=0)
    return grad_x, small


ROW_LOSS = 11


def _rows8(parts):
    out = None
    for r, a in parts:
        p = jnp.pad(a, ((r, 8 - r - a.shape[0]), (0, 1024 - a.shape[1])))
        out = p if out is None else out + p
    return out


def _pack_small(p):
    g0 = _rows8([(0, p["b_rg_a"].reshape(1, 1024)), (1, p["b_rg_x"].reshape(1, 1024)), (2, p["lru_lambda"]),
                 (3, p["conv_b"]), (4, p["conv_w"][0])])
    g1 = _rows8([(0, p["post_norm_g"]), (1, p["b_gate"].reshape(2, 1024))])
    g2 = _rows8([(0, p["pre_norm_g"]), (1, p["attn_sinks"])])
    return jnp.concatenate([g0, g1, g2, p["w_rg_a"].reshape(64, 1024), p["w_rg_x"].reshape(64, 1024)], axis=0)


def _unpack_small(s, conv_cols):
    return {
        "b_rg_a": s[0:1].reshape(1, 16, 64), "b_rg_x": s[1:2].reshape(1, 16, 64), "lru_lambda": s[2:3],
        "conv_b": s[3:4], "conv_w": s[4:8, 0:conv_cols].reshape(1, CONV_W, conv_cols),
        "post_norm_g": s[8:9], "b_gate": s[9:11].reshape(1, 2048),
        "pre_norm_g": s[16:17], "attn_sinks": s[17:18, 0:N_Q_HEADS],
        "w_rg_a": s[24:88].reshape(1, 16, 64, 64), "w_rg_x": s[88:152].reshape(1, 16, 64, 64),
    }


WEIGHTS = ["pre_norm_g", "w_in", "b_gate", "conv_w", "conv_b", "w_rg_a", "b_rg_a", "w_rg_x", "b_rg_x", "lru_lambda",
           "attn_sinks", "w_rnn_out", "w_attn_out", "w_out", "post_norm_g"]
BIG = ["w_in", "w_rnn_out", "w_attn_out", "w_out"]


def kernel(x, pre_norm_g, w_in, b_gate, conv_w, conv_b, w_rg_a, b_rg_a, w_rg_x, b_rg_x, lru_lambda, attn_sinks, w_rnn_out, w_attn_out, w_out, post_norm_g, loss_target, m_pre_norm_g, m_w_in, m_b_gate, m_conv_w, m_conv_b, m_w_rg_a, m_b_rg_a, m_w_rg_x, m_b_rg_x, m_lru_lambda, m_attn_sinks, m_w_rnn_out, m_w_attn_out, m_w_out, m_post_norm_g, v_pre_norm_g, v_w_in, v_b_gate, v_conv_w, v_conv_b, v_w_rg_a, v_b_rg_a, v_w_rg_x, v_b_rg_x, v_lru_lambda, v_attn_sinks, v_w_rnn_out, v_w_attn_out, v_w_out, v_post_norm_g):
    w = dict(pre_norm_g=pre_norm_g, w_in=w_in, b_gate=b_gate, conv_w=conv_w, conv_b=conv_b, w_rg_a=w_rg_a,
             b_rg_a=b_rg_a, w_rg_x=w_rg_x, b_rg_x=b_rg_x, lru_lambda=lru_lambda, attn_sinks=attn_sinks,
             w_rnn_out=w_rnn_out, w_attn_out=w_attn_out, w_out=w_out, post_norm_g=post_norm_g)
    m = dict(pre_norm_g=m_pre_norm_g, w_in=m_w_in, b_gate=m_b_gate, conv_w=m_conv_w, conv_b=m_conv_b, w_rg_a=m_w_rg_a,
             b_rg_a=m_b_rg_a, w_rg_x=m_w_rg_x, b_rg_x=m_b_rg_x, lru_lambda=m_lru_lambda, attn_sinks=m_attn_sinks,
             w_rnn_out=m_w_rnn_out, w_attn_out=m_w_attn_out, w_out=m_w_out, post_norm_g=m_post_norm_g)
    v = dict(pre_norm_g=v_pre_norm_g, w_in=v_w_in, b_gate=v_b_gate, conv_w=v_conv_w, conv_b=v_conv_b, w_rg_a=v_w_rg_a,
             b_rg_a=v_b_rg_a, w_rg_x=v_w_rg_x, b_rg_x=v_b_rg_x, lru_lambda=v_lru_lambda, attn_sinks=v_attn_sinks,
             w_rnn_out=v_w_rnn_out, w_attn_out=v_w_attn_out, w_out=v_w_out, post_norm_g=v_post_norm_g)
    chip = 2 * lax.axis_index("x") + lax.axis_index("y")

    chip_idx = chip.astype(jnp.int32).reshape(1)
    chip_core = jnp.stack([chip, lax.axis_index("c")]).astype(jnp.int32)
    cw8 = jnp.pad(conv_w[0], ((0, 8 - CONV_W), (0, 0)))
    placed, w_in_own = _place_shards([w_in[0], w_rnn_out[0], w_attn_out[0], w_out[0]], chip_idx, "place_shards")
    late = {}

    def project(x_local, g_pre):
        send, recv, thru, token = _gather_in_start(placed[0], cw8, "gather_in_start")
        proj, h, ht = _proj_own(x_local, g_pre, w_in_own, chip_idx, token)
        for tag, rels in (("near", (1, 2)), ("far", (3,))):
            thru = _gather_in_wait(send, recv, thru, proj, rels, "gather_in_wait_" + tag)
            thru[0] = _pair_forward(thru[0], rels, "pair_forward_" + tag)
            blocks = jnp.stack([jnp.bitwise_xor(chip, r) for r in rels]).astype(jnp.int32)
            proj = _proj_more(h, thru[0], proj, blocks, "proj_" + tag)
        w_in_g, _, cw_g = thru
        late["send"], late["recv"], late["thru"], late_token = _gather_late_start(placed[1:], w_in_g, "gather_late_start")
        cw_g = lax.dynamic_update_slice_in_dim(cw_g, cw8[None], chip, axis=0)
        conv_w_full = jnp.transpose(cw_g[:, 0:CONV_W, :], (1, 0, 2)).reshape(CONV_W, D_RNN)
        return proj, ht, w_in_g, conv_w_full, late_token

    started = {}

    def start_reduction(tag, grads, grads_b16):
        got = _pair_exchange(grads_b16, "pair_exchange_" + tag)
        sums = [_pair_sum(g, o, chip_core, "pair_sum_%s_%d" % (tag, a)) for a, (g, o) in enumerate(zip(grads, got))]
        send_sems, recv_sems, p_thru, land_thru, token = _chip_exchange_start(
            [pb for _, pb in sums], "chip_exchange_start_" + tag)
        started[tag] = ([p for p, _ in sums], send_sems, recv_sems, p_thru, land_thru)
        return token

    def end_reduction(tag, after):
        psums, send_sems, recv_sems, p_thru, land_thru = started[tag]
        landed = _chip_exchange_wait(send_sems, recv_sems, p_thru, land_thru, after, "chip_exchange_wait_" + tag)
        return [_chip_sum(p, l, chip_core, "chip_sum_%s_%d" % (tag, a)) for a, (p, l) in enumerate(zip(psums, landed))]

    def out_weights(after):
        gathered = _gather_late_wait(late["send"], late["recv"], late["thru"], after, "gather_late_wait")
        return [g.reshape(D_MODEL, D_MODEL) for g in gathered]

    grad_x, small = _local_grads(
        x[0], loss_target[0], pre_norm_g, project, b_gate, conv_b, w_rg_a[0], b_rg_a[0], w_rg_x[0],
        b_rg_x[0], lru_lambda, attn_sinks[0], out_weights, post_norm_g,
        on_out_grads=lambda grads, grads_b16: start_reduction("out", grads, grads_b16),
        on_w_in_grad=lambda grad, grad_b16: start_reduction("in", [grad], [grad_b16]))

    small_chip = _small_pair_sum(small)
    small_send, small_recv, small_thru, small_land, small_token = _chip_exchange_start(
        list(small_chip), "small_exchange_start", blocked=False)

    halves = end_reduction("in", small_token) + end_reduction("out", small_token)
    gbig = dict(zip(BIG, _pair_share(halves)))

    grads, delta, new_m, new_v = {}, {}, {}, {}
    for n in BIG:
        grads[n] = gbig[n][None]
        d, nm, nv = _adamw(w[n][0], gbig[n], m[n][0], v[n][0], "adamw_" + n)
        delta[n], new_m[n], new_v[n] = d[None], nm[None], nv[None]

    small_landed = _chip_exchange_wait(small_send, small_recv, small_thru, small_land, delta[BIG[-1]],
                                       "small_exchange_wait", blocked=False)
    small_sum = _small_total(chip_idx, small_thru, small_landed)
    total_loss = small_sum[ROW_LOSS, 0]
    gsmall = _unpack_small(small_sum, D_RNN)
    conv_shard = D_RNN // N_CHIPS
    gsmall["conv_w"] = lax.dynamic_slice_in_dim(gsmall["conv_w"], chip * conv_shard, conv_shard, axis=2)
    pick = lambda t: {k: t[k] for k in gsmall}
    d, nm, nv = _adamw(_pack_small(pick(w)), _pack_small(gsmall), _pack_small(pick(m)), _pack_small(pick(v)),
                       "adamw_small")
    ud, um, uv = _unpack_small(d, conv_shard), _unpack_small(nm, conv_shard), _unpack_small(nv, conv_shard)
    for n in gsmall:
        grads[n] = gsmall[n].reshape(w[n].shape)
        delta[n] = ud[n].reshape(w[n].shape)
        new_m[n] = um[n].reshape(w[n].shape)
        new_v[n] = uv[n].reshape(w[n].shape)

    return (total_loss, grad_x[None], *[grads[n] for n in WEIGHTS], *[delta[n] for n in WEIGHTS],
            *[new_m[n] for n in WEIGHTS], *[new_v[n] for n in WEIGHTS])
```

```python
import functools
import math

import jax
import jax.numpy as jnp
from jax import lax
from jax.experimental import pallas as pl
from jax.experimental.pallas import tpu as pltpu

F32 = jnp.float32
BF16 = jnp.bfloat16

D_MODEL = 1024
D_RNN = 1024
RNN_BLOCKS = 16
RNN_BLOCK_W = 64
CONV_W = 4
LRU_C = 8.0
N_Q_HEADS = 16
N_KV_HEADS = 4
GROUP = 4
HEAD_DIM = 64
D_KV = 256
BLOCK = 128
ALIBI_MAX_BIAS = 8.0
EPS = 1e-6
D_IN = 6656
N_CHIPS = 4
W_IN_SHARD = D_IN // N_CHIPS
OUT_SHARD = D_MODEL // N_CHIPS
ADAM_LR = 0.001
ADAM_B1 = 0.9
ADAM_B2 = 0.999
ADAM_EPS = 1e-08
ADAM_WD = 0.01
ADAM_STEP = 10
NEG_BIG = -1e30
MIB = 1 << 20

COL_RNN_X = 0
COL_RNN_GATE = 4
COL_Q = 8
COL_K = 12
COL_V = 13
COL_ATTN_GATE = 14
COL_MERGE = 18

RNN_TILE = 256
RNN_CHUNK = 512
SMALL_ROWS = 152
SMALL_VECTOR_ROWS = 24
MESH = pl.DeviceIdType.MESH


def _sds(shape, dtype):
    return pltpu.HBM(shape, dtype)


def _params(sem=None, vmem_mib=None):
    kw = {}
    if sem is not None:
        kw["dimension_semantics"] = sem
    if vmem_mib is not None:
        kw["vmem_limit_bytes"] = vmem_mib * MIB
    return pltpu.CompilerParams(**kw)


def _hbm(*arrays):
    return [pltpu.with_memory_space_constraint(a, pltpu.HBM) for a in arrays]


def _dot(a, b):
    return jnp.dot(a, b, preferred_element_type=F32)


def _dot_nt(a, b):
    return lax.dot_general(a, b, (((1,), (1,)), ((), ())), preferred_element_type=F32)


def _dot_tn(a, b):
    return lax.dot_general(a, b, (((0,), (0,)), ((), ())), preferred_element_type=F32)


def _sigmoid(x):
    return 0.5 * jnp.tanh(0.5 * x) + 0.5


def _sigmoid_small(x):
    return 1.0 / (1.0 + jnp.exp(-x))


def _softplus(x):
    return jnp.maximum(x, 0.0) + jnp.log(1.0 + jnp.exp(-jnp.abs(x)))


def _one_minus_square(a, log_a):
    return -jnp.tanh(log_a) * (a * a + 1.0)


def _proj_fwd(x, g_pre, w_in_g):
    T = x.shape[0]
    tm = min(1024, T)

    def body(x_ref, g_ref, w_ref, proj_ref, ht_ref, h_s):
        @pl.when(pl.program_id(1) == 0)
        def _():
            xv = x_ref[...]
            rstd = lax.rsqrt(jnp.mean(xv * xv, axis=-1, keepdims=True) + EPS)
            hf = (xv * rstd) * g_ref[...]
            h_s[...] = hf.astype(BF16)
            ht_ref[...] = hf.T.astype(BF16)

        proj_ref[...] = _dot(h_s[...], w_ref[...]).astype(BF16)

    return pl.pallas_call(
        body,
        name="proj_fwd",
        grid=(T // tm, N_CHIPS),
        in_specs=[
            pl.BlockSpec((tm, D_MODEL), lambda i, j: (i, 0)),
            pl.BlockSpec((1, D_MODEL), lambda i, j: (0, 0)),
            pl.BlockSpec((None, D_MODEL, W_IN_SHARD), lambda i, j: (j, 0, 0)),
        ],
        out_specs=[
            pl.BlockSpec((tm, W_IN_SHARD), lambda i, j: (i, j)),
            pl.BlockSpec((D_MODEL, tm), lambda i, j: (0, i)),
        ],
        out_shape=[_sds((T, D_IN), BF16), _sds((D_MODEL, T), BF16)],
        scratch_shapes=[pltpu.VMEM((tm, D_MODEL), BF16)],
        compiler_params=_params(("parallel", "arbitrary"), 48),
    )(*_hbm(x, g_pre, w_in_g))


def _shift_down(x, tail, s, row):
    n = x.shape[0]
    xs = pltpu.roll(x, s, 0)
    tail_t = jnp.tile(pltpu.roll(tail, s, 0), (n // 8, 1))
    return jnp.where(row < s, tail_t, xs)


def _shift_up(x, head, s, row):
    n = x.shape[0]
    xs = pltpu.roll(x, n - s, 0)
    head_t = jnp.tile(pltpu.roll(head, 8 - s, 0), (n // 8, 1))
    return jnp.where(row >= n - s, head_t, xs)


def _conv_taps(x, tail, row):
    return [_shift_down(x, tail, 3, row), _shift_down(x, tail, 2, row), _shift_down(x, tail, 1, row), x]


def _rglru_gates(c, wa, wx, ba, bx, lam):
    cb = c.astype(BF16)
    r = _sigmoid_small(_dot(cb, wa) + ba)
    i = _sigmoid(_dot(cb, wx) + bx)
    log_a = (-LRU_C) * r * _softplus(-lam)
    a = jnp.exp(log_a)
    w = _one_minus_square(a, log_a)
    inv_mult = lax.rsqrt(w)
    return cb, r, i, a, w * inv_mult, inv_mult


SUBLANES = 8


def _scan_down(a, u, row):
    n = a.shape[0]
    s = 1
    while s < SUBLANES:
        a_sh = jnp.where(row >= s, pltpu.roll(a, s, 0), 1.0)
        u_sh = jnp.where(row >= s, pltpu.roll(u, s, 0), 0.0)
        u = a * u_sh + u
        a = a * a_sh
        s *= 2
    while s < n:
        u = jnp.concatenate([u[:s], a[s:] * u[:n - s] + u[s:]], axis=0)
        a = jnp.concatenate([a[:s], a[s:] * a[:n - s]], axis=0)
        s *= 2
    return a, u


def _scan_up(b, u, row):
    n = b.shape[0]
    s = 1
    while s < SUBLANES:
        b_sh = jnp.where(row < n - s, pltpu.roll(b, n - s, 0), 1.0)
        u_sh = jnp.where(row < n - s, pltpu.roll(u, n - s, 0), 0.0)
        u = b * u_sh + u
        b = b * b_sh
        s *= 2
    while s < n:
        u = jnp.concatenate([b[:n - s] * u[s:] + u[:n - s], u[n - s:]], axis=0)
        b = jnp.concatenate([b[:n - s] * b[s:], b[n - s:]], axis=0)
        s *= 2
    return b, u


LANES = 128


def _chunk_scan(a, u, a_s, u_s, hl_s, al_s, carry, reverse):
    n, width = a.shape
    groups = n // SUBLANES
    order = range(SUBLANES - 1, -1, -1) if reverse else range(SUBLANES)
    row = lax.broadcasted_iota(jnp.int32, (groups, LANES), 0)
    for l in range(width // LANES):
        lanes = slice(l * LANES, (l + 1) * LANES)
        a_l, u_l, hl_l, al_l = a_s.at[l], u_s.at[l], hl_s.at[l], al_s.at[l]
        a_l[...] = a[:, lanes]
        u_l[...] = u[:, lanes]
        h_loc = a_loc = None
        for r in order:
            rows = pl.ds(r, groups, stride=SUBLANES)
            a_r, u_r = a_l[rows, :], u_l[rows, :]
            h_loc, a_loc = (u_r, a_r) if h_loc is None else (a_r * h_loc + u_r, a_r * a_loc)
            hl_l[rows, :] = h_loc
            al_l[rows, :] = a_loc
        if reverse:
            a_cum, ends = _scan_up(a_loc, h_loc, row)
            ends = ends + a_cum * carry[:, lanes]
            enters = jnp.where(row == groups - 1, carry[:, lanes], pltpu.roll(ends, groups - 1, 0))
        else:
            a_cum, ends = _scan_down(a_loc, h_loc, row)
            ends = ends + a_cum * carry[:, lanes]
            enters = jnp.where(row == 0, carry[:, lanes], pltpu.roll(ends, 1, 0))
        for r in range(SUBLANES):
            rows = pl.ds(r, groups, stride=SUBLANES)
            hl_l[rows, :] = hl_l[rows, :] + al_l[rows, :] * enters
    return jnp.concatenate([hl_s[l] for l in range(width // LANES)], axis=1)


def _rnn_fwd(proj, conv_w, conv_b, wa_bd, wx_bd, b_a, b_x, lam, token):
    T = proj.shape[0]
    tc, ct = RNN_CHUNK, RNN_TILE
    nt = T // tc

    def body(x_ref, rg_ref, cw_ref, cb_ref, wa_ref, wx_ref, ba_ref, bx_ref, lam_ref, token_ref, h_ref, z_ref, c_ref,
             zt_ref, xtail, hcarry, a_s, u_s, hl_s, al_s):
        @pl.when(pl.program_id(1) == 0)
        def _():
            xtail[...] = jnp.zeros_like(xtail)
            hcarry[...] = jnp.zeros_like(hcarry)

        row = lax.broadcasted_iota(jnp.int32, (tc, ct), 0)
        x = x_ref[...].astype(F32)
        taps = _conv_taps(x, xtail[...], row)
        c = cb_ref[...] + cw_ref[pl.ds(0, 1), :] * taps[0]
        for k in range(1, CONV_W):
            c = c + cw_ref[pl.ds(k, 1), :] * taps[k]
        xtail[...] = x[tc - 8:, :]
        c_ref[...] = c
        _, _, i, a, mult, _ = _rglru_gates(c, wa_ref[...], wx_ref[...], ba_ref[...], bx_ref[...], lam_ref[...])
        h = _chunk_scan(a, mult * (i * c), a_s, u_s, hl_s, al_s, hcarry[...], reverse=False)
        h_ref[...] = h
        hcarry[...] = h_ref[pl.ds(tc - 1, 1), :]
        rg = rg_ref[...].astype(F32)
        z = h * (rg * _sigmoid(rg))
        z_ref[...] = z.astype(BF16)
        zt_ref[...] = z.T.astype(BF16)

    col = lambda off: (lambda j, t: (t, off + j))
    vec = pl.BlockSpec((1, ct), lambda j, t: (0, j))
    mat = pl.BlockSpec((None, ct, ct), lambda j, t: (j, 0, 0))
    return pl.pallas_call(
        body,
        name="rnn_fwd",
        grid=(D_RNN // ct, nt),
        in_specs=[
            pl.BlockSpec((tc, ct), col(COL_RNN_X)),
            pl.BlockSpec((tc, ct), col(COL_RNN_GATE)),
            pl.BlockSpec((CONV_W, ct), lambda j, t: (0, j)),
            vec, mat, mat, vec, vec, vec,
            pl.BlockSpec((8, 128), lambda j, t: (0, 0)),
        ],
        out_specs=[pl.BlockSpec((tc, ct), lambda j, t: (t, j))] * 3 + [pl.BlockSpec((ct, tc), lambda j, t: (j, t))],
        out_shape=[_sds((T, D_RNN), F32), _sds((T, D_RNN), BF16), _sds((T, D_RNN), F32), _sds((D_RNN, T), BF16)],
        scratch_shapes=[pltpu.VMEM((8, ct), F32), pltpu.VMEM((1, ct), F32)] + [
            pltpu.VMEM((ct // LANES, tc, LANES), F32)] * 4,
        compiler_params=_params(("parallel", "arbitrary"), 32),
    )(*_hbm(proj, proj, conv_w, conv_b, wa_bd, wx_bd, b_a, b_x, lam, token))


def _rnn_bwd(proj, conv, y_rnn, dz_rnn, conv_w, wa_bd, wx_bd, b_a, b_x, lam):
    T = proj.shape[0]
    tc, ct = RNN_CHUNK, RNN_TILE
    nt = T // tc
    hb = tc // 8

    def body(x_ref, c_ref, rg_ref, h_ref, hh_ref, dz_ref, cw_ref, wa_ref, wx_ref, ba_ref, bx_ref, lam_ref,
             dx_ref, drg_ref, dwa_ref, dwx_ref, sm_ref, lam_carry, a_carry, dc_head, b_s, dy_s, hl_s, al_s):
        t = pl.program_id(1)
        first_chunk = t == nt - 1

        @pl.when(t == 0)
        def _():
            lam_carry[...] = jnp.zeros_like(lam_carry)
            a_carry[...] = jnp.zeros_like(a_carry)
            dc_head[...] = jnp.zeros_like(dc_head)
            dwa_ref[...] = jnp.zeros_like(dwa_ref)
            dwx_ref[...] = jnp.zeros_like(dwx_ref)
            sm_ref[...] = jnp.zeros_like(sm_ref)

        row = lax.broadcasted_iota(jnp.int32, (tc, ct), 0)
        keep = jnp.where(first_chunk, 0.0, 1.0)
        x = x_ref[...].astype(F32)
        c = c_ref[...]
        lam = lam_ref[...]
        cb, r, i, a, mult, inv_mult = _rglru_gates(c, wa_ref[...], wx_ref[...], ba_ref[...], bx_ref[...], lam)
        h = h_ref[...]
        h_prev = _shift_down(h, hh_ref[...] * keep, 1, row)
        rg = rg_ref[...].astype(F32)
        dz = dz_ref[...]
        sg = _sigmoid(rg)
        drg_ref[...] = (dz * h * (sg * (1.0 + rg * (1.0 - sg)))).astype(BF16)
        dy = dz * (rg * sg)
        b = jnp.where(row >= tc - 1, a_carry[pl.ds(0, 1), :], pltpu.roll(a, tc - 1, 0))
        lt = _chunk_scan(b, dy, b_s, dy_s, hl_s, al_s, lam_carry[pl.ds(0, 1), :], reverse=True)
        lam_carry[...] = lt[0:8, :]
        a_carry[...] = a[0:8, :]
        ic = i * c
        dmult = lt * ic
        di = lt * mult * c
        dc = lt * mult * i
        dlog_a = a * (lt * h_prev - dmult * a * inv_mult)
        sp = _softplus(-lam)
        dpre_r = dlog_a * ((-LRU_C) * sp) * (r * (1.0 - r))
        dpre_i = di * (i * (1.0 - i))
        dlam_row = jnp.sum(dlog_a * r, axis=0, keepdims=True) * (LRU_C * _sigmoid(-lam))
        dpr_b = dpre_r.astype(BF16)
        dpi_b = dpre_i.astype(BF16)
        dwa_ref[...] += _dot_tn(cb, dpr_b)
        dwx_ref[...] += _dot_tn(cb, dpi_b)
        dc = dc + _dot_nt(dpr_b, wa_ref[...]) + _dot_nt(dpi_b, wx_ref[...])
        head = dc_head[...]
        dx = cw_ref[pl.ds(3, 1), :] * dc
        sm_ref[pl.ds(4 + 3, 1), :] += jnp.sum(dc * x, axis=0, keepdims=True)
        for m in range(1, CONV_W):
            up = _shift_up(dc, head, m, row)
            dx = dx + cw_ref[pl.ds(3 - m, 1), :] * up
            sm_ref[pl.ds(4 + 3 - m, 1), :] += jnp.sum(up * x, axis=0, keepdims=True)
        dx_ref[...] = dx.astype(BF16)
        dc_head[...] = dc[0:8, :]
        sm_ref[pl.ds(0, 1), :] += jnp.sum(dpre_r, axis=0, keepdims=True)
        sm_ref[pl.ds(1, 1), :] += jnp.sum(dpre_i, axis=0, keepdims=True)
        sm_ref[pl.ds(2, 1), :] += dlam_row
        sm_ref[pl.ds(3, 1), :] += jnp.sum(dc, axis=0, keepdims=True)

    rev = lambda off: (lambda j, t: (nt - 1 - t, off + j))
    halo = lambda off: (lambda j, t: (jnp.maximum((nt - 1 - t) * hb - 1, 0), off + j))
    vec = pl.BlockSpec((1, ct), lambda j, t: (0, j))
    mat = pl.BlockSpec((None, ct, ct), lambda j, t: (j, 0, 0))
    return pl.pallas_call(
        body,
        name="rnn_bwd",
        grid=(D_RNN // ct, nt),
        in_specs=[
            pl.BlockSpec((tc, ct), rev(COL_RNN_X)),
            pl.BlockSpec((tc, ct), rev(0)),
            pl.BlockSpec((tc, ct), rev(COL_RNN_GATE)),
            pl.BlockSpec((tc, ct), rev(0)),
            pl.BlockSpec((8, ct), halo(0)),
            pl.BlockSpec((tc, ct), rev(0)),
            pl.BlockSpec((CONV_W, ct), lambda j, t: (0, j)),
            mat, mat, vec, vec, vec,
        ],
        out_specs=[
            pl.BlockSpec((tc, ct), rev(0)),
            pl.BlockSpec((tc, ct), rev(0)),
            mat, mat,
            pl.BlockSpec((8, ct), lambda j, t: (0, j)),
        ],
        out_shape=[_sds((T, D_RNN), BF16), _sds((T, D_RNN), BF16), _sds((D_RNN // ct, ct, ct), F32),
                   _sds((D_RNN // ct, ct, ct), F32), _sds((8, D_RNN), F32)],
        scratch_shapes=[pltpu.VMEM((8, ct), F32)] * 3 + [pltpu.VMEM((ct // LANES, tc, LANES), F32)] * 4,
        compiler_params=_params(("parallel", "arbitrary"), 32),
    )(*_hbm(proj, conv, proj, y_rnn, y_rnn, dz_rnn, conv_w, wa_bd, wx_bd, b_a, b_x, lam))


def _attn_bias():
    qi = jnp.arange(BLOCK)[:, None]
    kj = jnp.arange(BLOCK)[None, :]
    dist_cur = (qi - kj).astype(F32)
    slopes = 2.0 ** (-ALIBI_MAX_BIAS * jnp.arange(1, N_Q_HEADS + 1, dtype=F32) / N_Q_HEADS)
    slopes = slopes[:, None, None]
    prev = jnp.where(kj > qi, -slopes * (dist_cur + float(BLOCK)), NEG_BIG)
    cur = jnp.where(kj <= qi, -slopes * dist_cur, NEG_BIG)
    later = jnp.concatenate([prev, cur], axis=-1)
    first = jnp.concatenate([jnp.full_like(prev, NEG_BIG), cur], axis=-1)
    return jnp.stack([first, later])


def _attn_exps(s_prev, s_cur, sink, bias):
    s_prev = s_prev + bias[:, 0:BLOCK]
    s_cur = s_cur + bias[:, BLOCK:2 * BLOCK]
    m = jnp.maximum(jnp.max(jnp.maximum(s_prev, s_cur), axis=-1, keepdims=True), sink)
    p_prev = jnp.exp(s_prev - m)
    p_cur = jnp.exp(s_cur - m)
    total = jnp.sum(p_prev + p_cur, axis=-1, keepdims=True) + jnp.exp(sink - m)
    return p_prev, p_cur, 1.0 / total, m + jnp.log(total)


def _attn_probs(s_prev, s_cur, sink, bias, lse):
    p_prev = jnp.exp((s_prev + bias[:, 0:BLOCK]) - lse)
    p_cur = jnp.exp((s_cur + bias[:, BLOCK:2 * BLOCK]) - lse)
    return p_prev, p_cur, jnp.exp(sink - lse)


def _stack_heads(ref_or_val, hk, dtype):
    parts = [ref_or_val[:, (GROUP * hk + g) * HEAD_DIM:(GROUP * hk + g + 1) * HEAD_DIM] for g in range(GROUP)]
    return jnp.concatenate(parts, axis=0).astype(dtype)


ATTN_SCALE = HEAD_DIM ** -0.5


def _bias_spec():
    return pl.BlockSpec((None, N_Q_HEADS, BLOCK, 2 * BLOCK), lambda i: (jnp.minimum(i, 1), 0, 0, 0))


def _attn_fwd(proj, sinks, bias):
    T = proj.shape[0]
    nb = T // BLOCK

    def body(sink_ref, bias_ref, q_ref, kp_ref, kc_ref, vp_ref, vc_ref, ag0_ref, ag1_ref, y_ref, z_ref, lse_ref):
        kvs = [slice(hk * HEAD_DIM, (hk + 1) * HEAD_DIM) for hk in range(N_KV_HEADS)]
        qgs = [(_stack_heads(q_ref, hk, F32) * ATTN_SCALE).astype(BF16) for hk in range(N_KV_HEADS)]
        s_prev = [_dot_nt(qgs[hk], kp_ref[:, kvs[hk]].astype(BF16)) for hk in range(N_KV_HEADS)]
        s_cur = [_dot_nt(qgs[hk], kc_ref[:, kvs[hk]].astype(BF16)) for hk in range(N_KV_HEADS)]
        for hk in range(N_KV_HEADS):
            pp, pc, invs = [], [], []
            for g in range(GROUP):
                h = GROUP * hk + g
                rows = slice(g * BLOCK, (g + 1) * BLOCK)
                p_prev, p_cur, inv, lse = _attn_exps(s_prev[hk][rows], s_cur[hk][rows], sink_ref[h], bias_ref[h])
                pp.append(p_prev.astype(BF16))
                pc.append(p_cur.astype(BF16))
                invs.append(inv)
                lse_ref[:, h:h + 1] = lse
            og = _dot(jnp.concatenate(pp, axis=0), vp_ref[:, kvs[hk]].astype(BF16)) + _dot(
                jnp.concatenate(pc, axis=0), vc_ref[:, kvs[hk]].astype(BF16))
            for g in range(GROUP):
                h = GROUP * hk + g
                y_ref[:, h * HEAD_DIM:(h + 1) * HEAD_DIM] = og[g * BLOCK:(g + 1) * BLOCK] * invs[g]
        ag = jnp.concatenate([ag0_ref[...], ag1_ref[...]], axis=1).astype(F32)
        z_ref[...] = (y_ref[...] * (ag * _sigmoid(ag))).astype(BF16)

    prev = lambda c: (lambda i: (jnp.maximum(i - 1, 0), c))
    cur = lambda c: (lambda i: (i, c))
    return pl.pallas_call(
        body,
        name="attn_fwd",
        grid=(nb,),
        in_specs=[
            pl.BlockSpec(memory_space=pltpu.SMEM),
            _bias_spec(),
            pl.BlockSpec((BLOCK, 1024), lambda i: (i, COL_Q // 4)),
            pl.BlockSpec((BLOCK, D_KV), prev(COL_K)),
            pl.BlockSpec((BLOCK, D_KV), cur(COL_K)),
            pl.BlockSpec((BLOCK, D_KV), prev(COL_V)),
            pl.BlockSpec((BLOCK, D_KV), cur(COL_V)),
            pl.BlockSpec((BLOCK, 512), lambda i: (i, COL_ATTN_GATE // 2)),
            pl.BlockSpec((BLOCK, 512), lambda i: (i, COL_ATTN_GATE // 2 + 1)),
        ],
        out_specs=[pl.BlockSpec((BLOCK, 1024), lambda i: (i, 0)), pl.BlockSpec((BLOCK, 1024), lambda i: (i, 0)),
                   pl.BlockSpec((BLOCK, N_Q_HEADS), lambda i: (i, 0))],
        out_shape=[_sds((T, 1024), F32), _sds((T, 1024), BF16), _sds((T, N_Q_HEADS), F32)],
        compiler_params=_params(("arbitrary",), 32),
    )(sinks, *_hbm(bias, proj, proj, proj, proj, proj, proj, proj))


def _attn_bwd(proj, y_attn, lse, dz_attn, sinks, bias, token):
    T = proj.shape[0]
    nb = T // BLOCK

    def body(sink_ref, bias_ref, q_ref, kp_ref, kc_ref, vp_ref, vc_ref, ag0_ref, ag1_ref, y_ref, lse_ref, dz_ref,
             token_ref, dq_ref, dk_ref, dv_ref, dag_ref, ds_ref, dy_s):
        i = pl.program_id(0)

        @pl.when(i == 0)
        def _():
            ds_ref[...] = jnp.zeros_like(ds_ref)

        lane = lax.broadcasted_iota(jnp.int32, (8, 128), 1)
        sub = lax.broadcasted_iota(jnp.int32, (8, 128), 0)
        ag = jnp.concatenate([ag0_ref[...], ag1_ref[...]], axis=1).astype(F32)
        dz = dz_ref[...]
        sg = _sigmoid(ag)
        dag_ref[...] = (dz * y_ref[...] * (sg * (1.0 + ag * (1.0 - sg)))).astype(BF16)
        dy_s[...] = dz * (ag * sg)
        r_cur = pl.multiple_of(i * BLOCK, BLOCK)
        r_prev = pl.multiple_of(jnp.maximum(i - 1, 0) * BLOCK, BLOCK)
        dk_cur, dv_cur, dk_prev, dv_prev = [], [], [], []
        ds_acc = jnp.zeros((8, 128), F32)
        for hk in range(N_KV_HEADS):
            ks = slice(hk * HEAD_DIM, (hk + 1) * HEAD_DIM)
            qg = (_stack_heads(q_ref, hk, F32) * ATTN_SCALE).astype(BF16)
            dog = _stack_heads(dy_s, hk, F32)
            og = _stack_heads(y_ref, hk, F32)
            dog_b = dog.astype(BF16)
            kp = kp_ref[:, ks].astype(BF16)
            kc = kc_ref[:, ks].astype(BF16)
            vp = vp_ref[:, ks].astype(BF16)
            vc = vc_ref[:, ks].astype(BF16)
            s_prev = _dot_nt(qg, kp)
            s_cur = _dot_nt(qg, kc)
            dp_prev = _dot_nt(dog_b, vp)
            dp_cur = _dot_nt(dog_b, vc)
            dvec = jnp.sum(dog * og, axis=-1, keepdims=True)
            pp, pc, dsp, dsc = [], [], [], []
            for g in range(GROUP):
                h = GROUP * hk + g
                rows = slice(g * BLOCK, (g + 1) * BLOCK)
                p_prev, p_cur, p_sink = _attn_probs(
                    s_prev[rows], s_cur[rows], sink_ref[h], bias_ref[h], lse_ref[:, h:h + 1])
                d_h = dvec[rows]
                pp.append(p_prev.astype(BF16))
                pc.append(p_cur.astype(BF16))
                dsp.append((p_prev * (dp_prev[rows] - d_h)).astype(BF16))
                dsc.append((p_cur * (dp_cur[rows] - d_h)).astype(BF16))
                dsink = -jnp.sum(p_sink * d_h, axis=0, keepdims=True)
                ds_acc = ds_acc + jnp.where(jnp.logical_and(lane == h, sub == 1), dsink, 0.0)
            pp = jnp.concatenate(pp, axis=0)
            pc = jnp.concatenate(pc, axis=0)
            dsp = jnp.concatenate(dsp, axis=0)
            dsc = jnp.concatenate(dsc, axis=0)
            dqg = (_dot(dsp, kp) + _dot(dsc, kc)) * ATTN_SCALE
            for g in range(GROUP):
                h = GROUP * hk + g
                dq_ref[:, h * HEAD_DIM:(h + 1) * HEAD_DIM] = dqg[g * BLOCK:(g + 1) * BLOCK].astype(BF16)
            dk_ref[pl.ds(r_cur, BLOCK), ks] = _dot_tn(dsc, qg)
            dv_ref[pl.ds(r_cur, BLOCK), ks] = _dot_tn(pc, dog_b)
            dk_prev.append(_dot_tn(dsp, qg))
            dv_prev.append(_dot_tn(pp, dog_b))
        ds_ref[:, 0:128] += ds_acc

        @pl.when(i > 0)
        def _():
            for hk in range(N_KV_HEADS):
                ks = slice(hk * HEAD_DIM, (hk + 1) * HEAD_DIM)
                dk_ref[pl.ds(r_prev, BLOCK), ks] += dk_prev[hk]
                dv_ref[pl.ds(r_prev, BLOCK), ks] += dv_prev[hk]

    prev = lambda c: (lambda i: (jnp.maximum(i - 1, 0), c))
    cur = lambda c: (lambda i: (i, c))
    blk = pl.BlockSpec((BLOCK, 1024), lambda i: (i, 0))
    whole = pl.BlockSpec((T, D_KV), lambda i: (0, 0))
    return pl.pallas_call(
        body,
        name="attn_bwd",
        grid=(nb,),
        in_specs=[
            pl.BlockSpec(memory_space=pltpu.SMEM),
            _bias_spec(),
            pl.BlockSpec((BLOCK, 1024), lambda i: (i, COL_Q // 4)),
            pl.BlockSpec((BLOCK, D_KV), prev(COL_K)),
            pl.BlockSpec((BLOCK, D_KV), cur(COL_K)),
            pl.BlockSpec((BLOCK, D_KV), prev(COL_V)),
            pl.BlockSpec((BLOCK, D_KV), cur(COL_V)),
            pl.BlockSpec((BLOCK, 512), lambda i: (i, COL_ATTN_GATE // 2)),
            pl.BlockSpec((BLOCK, 512), lambda i: (i, COL_ATTN_GATE // 2 + 1)),
            blk,
            pl.BlockSpec((BLOCK, N_Q_HEADS), lambda i: (i, 0)),
            blk,
            pl.BlockSpec((8, 128), lambda i: (0, 0)),
        ],
        out_specs=[blk, whole, whole, blk, pl.BlockSpec((8, 1024), lambda i: (0, 0))],
        out_shape=[_sds((T, 1024), BF16), _sds((T, D_KV), F32), _sds((T, D_KV), F32), _sds((T, 1024), BF16),
                   _sds((8, 1024), F32)],
        scratch_shapes=[pltpu.VMEM((BLOCK, 1024), F32)],
        compiler_params=_params(("arbitrary",), 48),
    )(sinks, *_hbm(bias, proj, proj, proj, proj, proj, proj, proj, y_attn, lse, dz_attn, token))


def _head(x, target, z_rnn, z_attn, proj, b_gate, g_post, w_rnn_out, w_attn_out, w_out):
    T = x.shape[0]
    tm = 256

    def body(x_ref, t_ref, zr_ref, za_ref, ml0_ref, ml1_ref, ml2_ref, ml3_ref, bg_ref, gp_ref, wr_ref, wa_ref, wo_ref,
             dyx_ref, dzr_ref, dza_ref, dml_ref, dout_ref, dbr_ref, dba_ref, mt_ref, zat_ref, sm_ref):
        @pl.when(pl.program_id(0) == 0)
        def _():
            sm_ref[...] = jnp.zeros_like(sm_ref)

        wr, wa, wo = wr_ref[...], wa_ref[...], wo_ref[...]
        br_rnn = _dot(zr_ref[...], wr)
        br_attn = _dot(za_ref[...], wa)
        zat_ref[...] = za_ref[...].astype(F32).T.astype(BF16)
        ml_rnn = jnp.concatenate([ml0_ref[...], ml1_ref[...]], axis=1).astype(F32)
        ml_attn = jnp.concatenate([ml2_ref[...], ml3_ref[...]], axis=1).astype(F32)
        g_rnn = _sigmoid(ml_rnn + bg_ref[:, 0:D_MODEL])
        g_attn = _sigmoid(ml_attn + bg_ref[:, D_MODEL:2 * D_MODEL])
        merged = g_rnn * br_rnn + g_attn * br_attn
        mb = merged.astype(BF16)
        mt_ref[...] = merged.T.astype(BF16)
        out = _dot(mb, wo)
        rstd = lax.rsqrt(jnp.mean(out * out, axis=-1, keepdims=True) + EPS)
        n = out * rstd
        gp = gp_ref[...]
        err = (x_ref[...] + n * gp) - t_ref[...]
        sm_ref[pl.ds(3, 1), :] += 0.5 * jnp.sum(jnp.mean(err * err, axis=-1, keepdims=True), axis=0, keepdims=True)
        dy = err * (1.0 / D_MODEL)
        dyx_ref[...] = dy
        sm_ref[pl.ds(0, 1), :] += jnp.sum(dy * n, axis=0, keepdims=True)
        dn = dy * gp
        dout = (rstd * (dn - n * jnp.mean(dn * n, axis=-1, keepdims=True))).astype(BF16)
        dout_ref[...] = dout
        dmerged = _dot_nt(dout, wo)
        dml_r = (dmerged * br_rnn) * (g_rnn * (1.0 - g_rnn))
        dml_a = (dmerged * br_attn) * (g_attn * (1.0 - g_attn))
        dml_ref[:, 0:D_MODEL] = dml_r.astype(BF16)
        dml_ref[:, D_MODEL:2 * D_MODEL] = dml_a.astype(BF16)
        sm_ref[pl.ds(1, 1), :] += jnp.sum(dml_r, axis=0, keepdims=True)
        sm_ref[pl.ds(2, 1), :] += jnp.sum(dml_a, axis=0, keepdims=True)
        dbr = (dmerged * g_rnn).astype(BF16)
        dba = (dmerged * g_attn).astype(BF16)
        dbr_ref[...] = dbr
        dba_ref[...] = dba
        dzr_ref[...] = _dot_nt(dbr, wr)
        dza_ref[...] = _dot_nt(dba, wa)

    tile = pl.BlockSpec((tm, D_MODEL), lambda i: (i, 0))
    wspec = pl.BlockSpec((D_MODEL, D_MODEL), lambda i: (0, 0))
    ml = lambda q: pl.BlockSpec((tm, 512), lambda i: (i, COL_MERGE // 2 + q))
    return pl.pallas_call(
        body,
        name="head",
        grid=(T // tm,),
        in_specs=[
            tile, tile, tile, tile,
            ml(0), ml(1), ml(2), ml(3),
            pl.BlockSpec((1, 2 * D_MODEL), lambda i: (0, 0)),
            pl.BlockSpec((1, D_MODEL), lambda i: (0, 0)),
            wspec, wspec, wspec,
        ],
        out_specs=[
            tile, tile, tile,
            pl.BlockSpec((tm, 2 * D_MODEL), lambda i: (i, 0)),
            tile, tile, tile,
            pl.BlockSpec((D_MODEL, tm), lambda i: (0, i)), pl.BlockSpec((D_MODEL, tm), lambda i: (0, i)),
            pl.BlockSpec((8, D_MODEL), lambda i: (0, 0)),
        ],
        out_shape=[
            _sds((T, D_MODEL), F32), _sds((T, D_MODEL), F32), _sds((T, D_MODEL), F32),
            _sds((T, 2 * D_MODEL), BF16),
            _sds((T, D_MODEL), BF16), _sds((T, D_MODEL), BF16), _sds((T, D_MODEL), BF16),
            _sds((D_MODEL, T), BF16), _sds((D_MODEL, T), BF16),
            _sds((8, D_MODEL), F32),
        ],
        compiler_params=_params(("arbitrary",), 56),
    )(*_hbm(x, target, z_rnn, z_attn, proj, proj, proj, proj, b_gate, g_post, w_rnn_out, w_attn_out, w_out))


def _matmul_t(at, b, name):
    M, T = at.shape
    N = b.shape[1]
    tk = min(1024, T)
    nt = T // tk

    def body(a_ref, b_ref, o_ref, ob_ref):
        @pl.when(pl.program_id(0) == 0)
        def _():
            o_ref[...] = jnp.zeros_like(o_ref)

        o_ref[...] += _dot(a_ref[...], b_ref[...])

        @pl.when(pl.program_id(0) == nt - 1)
        def _():
            ob_ref[...] = o_ref[...].astype(BF16)

    whole = pl.BlockSpec((M, N), lambda t: (0, 0))
    return pl.pallas_call(
        body,
        name=name,
        grid=(nt,),
        in_specs=[pl.BlockSpec((M, tk), lambda t: (0, t)), pl.BlockSpec((tk, N), lambda t: (t, 0))],
        out_specs=[whole, whole],
        out_shape=[_sds((M, N), F32), _sds((M, N), BF16)],
        compiler_params=_params(("arbitrary",), 48),
    )(*_hbm(at, b))


DPROJ_WIDTHS = (D_RNN, D_RNN, 1024, D_KV, D_KV, 1024, 2 * D_MODEL)


def _dproj_segments():
    segs, start = [[] for _ in range(N_CHIPS)], 0
    for p, width in enumerate(DPROJ_WIDTHS):
        for c in range(N_CHIPS):
            lo, hi = max(start, c * W_IN_SHARD), min(start + width, (c + 1) * W_IN_SHARD)
            if lo < hi:
                segs[c].append((p, lo - start, hi - start, lo - c * W_IN_SHARD, hi - c * W_IN_SHARD))
        start += width
    return segs


def _dh_bwd(pieces, w_in_g, x, dyx, g_pre, token):
    T = x.shape[0]
    tm = min(512, T)
    n = len(pieces)
    segs = _dproj_segments()

    def body(*refs):
        p_refs, w_hbm, x_ref, dyx_ref, g_ref = refs[0:n], refs[n], refs[n + 1], refs[n + 2], refs[n + 3]
        gx_ref, dg_ref, w_ref, w_sems = refs[n + 5], refs[n + 6], refs[n + 7], refs[n + 8]
        first = pl.program_id(0) == 0
        w_copies = [pltpu.make_async_copy(w_hbm.at[c], w_ref.at[c], w_sems.at[c]) for c in range(N_CHIPS)]

        @pl.when(first)
        def _():
            for cp in w_copies:
                cp.start()
            dg_ref[...] = jnp.zeros_like(dg_ref)

        dh = None
        for c in range(N_CHIPS):
            pl.when(first)(w_copies[c].wait)
            for p, a0, a1, u0, u1 in segs[c]:
                part = _dot_nt(p_refs[p][:, a0:a1].astype(BF16), w_ref[c, :, u0:u1])
                dh = part if dh is None else dh + part
        xv = x_ref[...]
        rstd = lax.rsqrt(jnp.mean(xv * xv, axis=-1, keepdims=True) + EPS)
        nx = xv * rstd
        dhg = dh * g_ref[...]
        gx_ref[...] = dyx_ref[...] + rstd * (dhg - nx * jnp.mean(dhg * nx, axis=-1, keepdims=True))
        dg_ref[pl.ds(0, 1), :] += jnp.sum(dh * nx, axis=0, keepdims=True)

    tile = pl.BlockSpec((tm, D_MODEL), lambda i: (i, 0))
    return pl.pallas_call(
        body,
        name="dh_bwd",
        grid=(T // tm,),
        in_specs=[pl.BlockSpec((tm, w), lambda i: (i, 0)) for w in DPROJ_WIDTHS] + [
            ANY, tile, tile,
            pl.BlockSpec((1, D_MODEL), lambda i: (0, 0)),
            pl.BlockSpec((8, 128), lambda i: (0, 0)),
        ],
        out_specs=[tile, pl.BlockSpec((8, D_MODEL), lambda i: (0, 0))],
        out_shape=[_sds((T, D_MODEL), F32), _sds((8, D_MODEL), F32)],
        scratch_shapes=[pltpu.VMEM(w_in_g.shape, BF16), pltpu.SemaphoreType.DMA((N_CHIPS,))],
        compiler_params=_params(("arbitrary",), 56),
    )(*_hbm(*pieces, w_in_g, x, dyx, g_pre, token))


def _dw_in(ht, pieces):
    T = ht.shape[1]
    tk = min(512, T)
    nt = T // tk
    n = len(pieces)
    segs = _dproj_segments()

    def body(*refs):
        h_ref, p_refs, o_ref, ob_ref = refs[0], refs[1:n + 1], refs[n + 1], refs[n + 2]

        @pl.when(pl.program_id(1) == 0)
        def _():
            o_ref[...] = jnp.zeros_like(o_ref)

        for c in range(N_CHIPS):
            @pl.when(pl.program_id(0) == c)
            def _():
                for p, a0, a1, u0, u1 in segs[c]:
                    o_ref[:, u0:u1] += _dot(h_ref[...], p_refs[p][:, a0:a1].astype(BF16))

        @pl.when(pl.program_id(1) == nt - 1)
        def _():
            ob_ref[...] = o_ref[...].astype(BF16)

    def piece_spec(p):
        chips = [c for c in range(N_CHIPS) if any(s[0] == p for s in segs[c])]

        def index(c, t):
            used = functools.reduce(jnp.logical_or, [c == k for k in chips])
            return (jnp.where(used, t, 0), 0)

        return pl.BlockSpec((tk, DPROJ_WIDTHS[p]), index)

    return pl.pallas_call(
        body,
        name="dw_in",
        grid=(N_CHIPS, nt),
        in_specs=[pl.BlockSpec((D_MODEL, tk), lambda c, t: (0, t))] + [piece_spec(p) for p in range(n)],
        out_specs=[pl.BlockSpec((None, D_MODEL, W_IN_SHARD), lambda c, t: (c, 0, 0))] * 2,
        out_shape=[_sds((N_CHIPS, D_MODEL, W_IN_SHARD), F32), _sds((N_CHIPS, D_MODEL, W_IN_SHARD), BF16)],
        compiler_params=_params(("parallel", "arbitrary"), 56),
    )(*_hbm(ht, *pieces))


ELEMENTWISE_TILE_BYTES = MIB


def _row_tile(rows, cols):
    if rows * cols * 4 <= ELEMENTWISE_TILE_BYTES:
        return rows
    for t in (512, 256, 128, 64, 32, 16, 8):
        if rows % t == 0 and t * cols * 4 <= ELEMENTWISE_TILE_BYTES:
            return t
    return rows


def _pair_sum(gs, gots, chip_core, name):
    n = len(gs)
    nch, R, C = gs[0].shape
    h = R // 2
    tr = _row_tile(h, C * n)
    nt = h // tr

    def body(jc_ref, *refs):
        for a in range(n):
            g_ref, got_ref, p_ref, pb_ref = refs[a], refs[n + a], refs[2 * n + a], refs[3 * n + a]
            s = g_ref[...] + got_ref[...].astype(F32)
            pb_ref[...] = s.astype(BF16)

            @pl.when(pl.program_id(1) == jc_ref[0])
            def _():
                p_ref[...] = s

    by_chip = pl.BlockSpec((None, tr, C), lambda i, j, jc_ref: (j, i, 0))
    outs = pl.pallas_call(
        body,
        name=name,
        grid_spec=pltpu.PrefetchScalarGridSpec(
            num_scalar_prefetch=1,
            grid=(nt, nch),
            in_specs=[pl.BlockSpec((None, tr, C), lambda i, j, jc_ref: (j, jc_ref[1] * nt + i, 0))] * n + [by_chip] * n,
            out_specs=[pl.BlockSpec((tr, C), lambda i, j, jc_ref: (i, 0))] * n + [by_chip] * n,
        ),
        out_shape=[_sds((h, C), F32)] * n + [_sds((nch, h, C), BF16)] * n,
        compiler_params=_params(("parallel", "arbitrary"), 48),
    )(chip_core, *_hbm(*gs, *gots))
    return list(outs[:n]), list(outs[n:])


def _chip_sum(ps, gots, chip_core, name):
    n = len(ps)
    h, C = ps[0].shape
    tr = _row_tile(h, C * n)
    nt = h // tr

    def body(jc_ref, *refs):
        for a in range(n):
            p_ref, g0_ref, g1_ref, g2_ref, o_ref = refs[a], refs[n + 3 * a], refs[n + 3 * a + 1], refs[n + 3 * a + 2], \
                refs[4 * n + a]
            o_ref[...] = ((p_ref[...] + g0_ref[...].astype(F32)) + g1_ref[...].astype(F32)) + g2_ref[...].astype(F32)

    rel = lambda r: pl.BlockSpec((None, tr, C), lambda i, jc_ref: (r, i, 0))
    outs = pl.pallas_call(
        body,
        name=name,
        grid_spec=pltpu.PrefetchScalarGridSpec(
            num_scalar_prefetch=1,
            grid=(nt,),
            in_specs=[pl.BlockSpec((tr, C), lambda i, jc_ref: (i, 0))] * n + [rel(0), rel(1), rel(2)] * n,
            out_specs=[pl.BlockSpec((tr, C), lambda i, jc_ref: (jc_ref[1] * nt + i, 0))] * n,
        ),
        out_shape=[_sds((2 * h, C), F32)] * n,
        compiler_params=_params(("parallel",), 48),
    )(chip_core, *_hbm(*ps, *[g for got in gots for g in (got, got, got)]))
    return list(outs)


def _place_shards(shards, chip, name):
    n = len(shards)
    tiles = [_row_tile(s.shape[0], s.shape[1]) for s in shards]
    steps = max(s.shape[0] // t for s, t in zip(shards, tiles))
    tiles = [s.shape[0] // steps for s in shards]

    def body(j_ref, *refs):
        for a in range(n):
            refs[n + a][...] = refs[a][...].astype(BF16)

    return pl.pallas_call(
        body,
        name=name,
        grid_spec=pltpu.PrefetchScalarGridSpec(
            num_scalar_prefetch=1,
            grid=(steps,),
            in_specs=[pl.BlockSpec((t, s.shape[1]), lambda i, j_ref: (i, 0)) for s, t in zip(shards, tiles)],
            out_specs=[pl.BlockSpec((None, t, s.shape[1]), lambda i, j_ref: (j_ref[0], i, 0))
                       for s, t in zip(shards, tiles)],
        ),
        out_shape=[_sds((N_CHIPS,) + s.shape, BF16) for s in shards],
        compiler_params=_params(("parallel",), 48),
    )(chip, *_hbm(*shards))


def _adamw(params, name):
    n = len(params)
    R, C = params[0][0].shape
    tr = _row_tile(R, C * n)
    c1 = 1.0 - ADAM_B1 ** ADAM_STEP
    c2 = 1.0 - ADAM_B2 ** ADAM_STEP

    def body(*refs):
        for a in range(n):
            w_ref, g_ref, m_ref, v_ref = refs[4 * a:4 * a + 4]
            d_ref, nm_ref, nv_ref = refs[4 * n + 3 * a:4 * n + 3 * a + 3]
            g = g_ref[...]
            nm = ADAM_B1 * m_ref[...] + (1.0 - ADAM_B1) * g
            nv = ADAM_B2 * v_ref[...] + (1.0 - ADAM_B2) * (g * g)
            nm_ref[...] = nm
            nv_ref[...] = nv
            d_ref[...] = (-ADAM_LR) * ((nm / c1) / (jnp.sqrt(nv / c2) + ADAM_EPS) + ADAM_WD * w_ref[...])

    spec = pl.BlockSpec((tr, C), lambda i: (i, 0))
    outs = pl.pallas_call(
        body, name=name, grid=(R // tr,), in_specs=[spec] * (4 * n), out_specs=[spec] * (3 * n),
        out_shape=[_sds((R, C), F32)] * (3 * n), compiler_params=_params(("parallel",), 48),
    )(*_hbm(*[t for p in params for t in p]))
    return [tuple(outs[3 * a:3 * a + 3]) for a in range(n)]


def _place():
    return lax.axis_index("x"), lax.axis_index("y"), lax.axis_index("c")


def _chip_of(x, y, r):
    return (x ^ (r >> 1), y ^ (r & 1))


ANY = pl.BlockSpec(memory_space=pl.ANY)


def _gather_weights(placed, cw8):
    nbig = len(placed)
    halves = [s.shape[1] // 2 for s in placed]
    pieces = [max(1, h // 64) for h in halves]
    rows = [h // p for h, p in zip(halves, pieces)]
    order = [(a, q) for q in range(max(pieces)) for a in range(nbig) if q < pieces[a]]
    ici_sem = {(a, q, r): 3 * i + (r - 1) for i, (a, q) in enumerate(order) for r in (1, 2, 3)}
    cw_sem = {r: 3 * len(order) + (r - 1) for r in (1, 2, 3)}
    d2d_sem = {key: 3 * len(order) + 3 + k for key, k in ici_sem.items()}
    nsem = 6 * len(order) + 3

    def body(*refs):
        cw_ref, dsts, gcw_ref = refs[nbig], refs[nbig + 1:2 * nbig + 1], refs[2 * nbig + 1]
        send_sems, recv_sems = refs[2 * nbig + 2:]
        x, y, c = _place()
        j = 2 * x + y

        def piece_rows(a, q, core):
            return pl.ds(pl.multiple_of(core * halves[a] + q * rows[a], 16), rows[a])

        def ici(a, q, r):
            tx, ty = _chip_of(x, y, r)
            k = ici_sem[(a, q, r)]
            region = dsts[a].at[j, piece_rows(a, q, c), :]
            return pltpu.make_async_remote_copy(
                src_ref=region, dst_ref=region, send_sem=send_sems.at[k], recv_sem=recv_sems.at[k],
                device_id=(tx, ty, c), device_id_type=MESH)

        def ici_landed(a, q, r):
            tx, ty = _chip_of(x, y, r)
            k = ici_sem[(a, q, r)]
            region = dsts[a].at[2 * tx + ty, piece_rows(a, q, c), :]
            return pltpu.make_async_remote_copy(
                src_ref=region, dst_ref=region, send_sem=send_sems.at[k], recv_sem=recv_sems.at[k],
                device_id=(tx, ty, c), device_id_type=MESH)

        def d2d(a, q, r, core):
            tx, ty = _chip_of(x, y, r)
            k = d2d_sem[(a, q, r)]
            region = dsts[a].at[2 * tx + ty, piece_rows(a, q, core), :]
            return pltpu.make_async_remote_copy(
                src_ref=region, dst_ref=region, send_sem=send_sems.at[k], recv_sem=recv_sems.at[k],
                device_id=(x, y, 1 - c), device_id_type=MESH)

        def cw_copy(r):
            tx, ty = _chip_of(x, y, r)
            k = cw_sem[r]
            return pltpu.make_async_remote_copy(
                src_ref=cw_ref, dst_ref=gcw_ref.at[j], send_sem=send_sems.at[k], recv_sem=recv_sems.at[k],
                device_id=(tx, ty, c), device_id_type=MESH)

        def cw_landed(r):
            tx, ty = _chip_of(x, y, r)
            k = cw_sem[r]
            region = gcw_ref.at[2 * tx + ty]
            return pltpu.make_async_remote_copy(
                src_ref=region, dst_ref=region, send_sem=send_sems.at[k], recv_sem=recv_sems.at[k],
                device_id=(tx, ty, c), device_id_type=MESH)

        def relay(a, q, origin, to):
            ox, oy = _chip_of(x, y, origin)
            tx, ty = _chip_of(x, y, to)
            k = ici_sem[(a, q, 3)]
            region = dsts[a].at[2 * ox + oy, piece_rows(a, q, c), :]
            return pltpu.make_async_remote_copy(
                src_ref=region, dst_ref=region, send_sem=send_sems.at[k], recv_sem=recv_sems.at[k],
                device_id=(tx, ty, c), device_id_type=MESH)

        first = [ici(a, q, r) for (a, q) in order for r in (1, 2)] + [cw_copy(r) for r in (1, 2, 3)]
        for cp in first:
            cp.start()
        passed = []
        for (a, q) in order:
            for r in (1, 2):
                ici_landed(a, q, r).wait_recv()
                if q % 2 == r - 1:
                    cp = relay(a, q, r, 3 - r)
                    cp.start()
                    passed.append(cp)
                cp = d2d(a, q, r, c)
                cp.start()
                passed.append(cp)
        for (a, q) in order:
            ici_landed(a, q, 3).wait_recv()
            cp = d2d(a, q, 3, c)
            cp.start()
            passed.append(cp)
        for r in (1, 2, 3):
            cw_landed(r).wait_recv()
        for (a, q) in order:
            for r in (1, 2, 3):
                d2d(a, q, r, 1 - c).wait_recv()
        for cp in first + passed:
            cp.wait_send()

    return pl.pallas_call(
        body,
        name="gather_weights",
        in_specs=[ANY] * (nbig + 1),
        out_specs=[ANY] * (nbig + 1),
        out_shape=[_sds(s.shape, s.dtype) for s in placed] + [_sds((N_CHIPS,) + cw8.shape, cw8.dtype)],
        input_output_aliases={a: a for a in range(nbig)},
        scratch_shapes=[pltpu.SemaphoreType.DMA((nsem,)), pltpu.SemaphoreType.DMA((nsem,))],
    )(*placed, cw8)


def _gather_late_start(placed, after, name):
    n = len(placed)
    halves = [s.shape[1] // 2 for s in placed]

    def body(*refs):
        g_refs = refs[0:n]
        send_sems, recv_sems, token = refs[n + 1], refs[n + 2], refs[-1]
        x, y, c = _place()
        j = 2 * x + y
        for a in range(n):
            mine = g_refs[a].at[j, pl.ds(pl.multiple_of(c * halves[a], 16), halves[a]), :]
            for r in (1, 2, 3):
                tx, ty = _chip_of(x, y, r)
                for to_core in (0, 1):
                    k = ((a * 3 + (r - 1)) * 2 + c) * 2 + to_core
                    pltpu.make_async_remote_copy(
                        src_ref=mine, dst_ref=mine, send_sem=send_sems.at[k], recv_sem=recv_sems.at[k],
                        device_id=(tx, ty, to_core), device_id_type=MESH).start()
        token[...] = jnp.zeros_like(token)

    hbm = lambda t: pltpu.HBM(t.shape, t.dtype)
    keep = lambda t: pltpu.with_memory_space_constraint(t, pltpu.HBM)
    nsem = 12 * n
    outs = pl.pallas_call(
        body,
        name=name,
        in_specs=[HBM] * n + [ANY],
        out_specs=(SEM, SEM, *[HBM] * n, pl.BlockSpec(memory_space=pltpu.VMEM)),
        out_shape=(pltpu.SemaphoreType.DMA((nsem,)), pltpu.SemaphoreType.DMA((nsem,)), *[hbm(p) for p in placed],
                   jax.ShapeDtypeStruct((8, 128), F32)),
        input_output_aliases={i: 2 + i for i in range(n)},
        compiler_params=pltpu.CompilerParams(has_side_effects=DATAFLOW),
    )(*[keep(p) for p in placed], after)
    return outs[0], outs[1], list(outs[2:2 + n]), outs[-1]


def _gather_late_wait(send_sems, recv_sems, thru, after, name):
    n = len(thru)
    halves = [s.shape[1] // 2 for s in thru]

    def body(*refs):
        g_refs = refs[0:n]
        send_sems, recv_sems = refs[n], refs[n + 1]
        x, y, c = _place()
        j = 2 * x + y
        for a in range(n):
            mine = g_refs[a].at[j, pl.ds(pl.multiple_of(c * halves[a], 16), halves[a]), :]
            for r in (1, 2, 3):
                tx, ty = _chip_of(x, y, r)
                for other in (0, 1):
                    k_out = ((a * 3 + (r - 1)) * 2 + c) * 2 + other
                    pltpu.make_async_remote_copy(
                        src_ref=mine, dst_ref=mine, send_sem=send_sems.at[k_out], recv_sem=recv_sems.at[k_out],
                        device_id=(tx, ty, other), device_id_type=MESH).wait_send()
                    k_in = ((a * 3 + (r - 1)) * 2 + other) * 2 + c
                    theirs = g_refs[a].at[2 * tx + ty, pl.ds(other * halves[a], halves[a]), :]
                    pltpu.make_async_remote_copy(
                        src_ref=theirs, dst_ref=theirs, send_sem=send_sems.at[k_in], recv_sem=recv_sems.at[k_in],
                        device_id=(tx, ty, other), device_id_type=MESH).wait_recv()

    hbm = lambda t: pltpu.HBM(t.shape, t.dtype)
    outs = pl.pallas_call(
        body,
        name=name,
        in_specs=[HBM] * n + [SEM, SEM, ANY],
        out_specs=[HBM] * n,
        out_shape=[hbm(t) for t in thru],
        input_output_aliases={i: i for i in range(n)},
        compiler_params=pltpu.CompilerParams(has_side_effects=DATAFLOW),
    )(*thru, send_sems, recv_sems, after)
    return list(outs)


D2D_PIECE_ROWS = 64


def _pair_exchange(grads, name):
    n = len(grads)
    halves = [g.shape[1] // 2 for g in grads]

    def body(*refs):
        g_refs, got_refs = refs[0:n], refs[n:2 * n]
        send_sems, recv_sems = refs[2 * n:]
        x, y, c = _place()

        def copy(a, src, dst):
            return pltpu.make_async_remote_copy(
                src_ref=src, dst_ref=dst, send_sem=send_sems.at[a], recv_sem=recv_sems.at[a],
                device_id=(x, y, 1 - c), device_id_type=MESH)

        for a in range(n):
            for jj in range(N_CHIPS):
                for q in range(halves[a] // D2D_PIECE_ROWS):
                    src_rows = pl.ds(pl.multiple_of((1 - c) * halves[a] + q * D2D_PIECE_ROWS, 16), D2D_PIECE_ROWS)
                    dst_rows = pl.ds(q * D2D_PIECE_ROWS, D2D_PIECE_ROWS)
                    copy(a, g_refs[a].at[jj, src_rows, :], got_refs[a].at[jj, dst_rows, :]).start()
        for a in range(n):
            sent = g_refs[a].at[:, pl.ds(pl.multiple_of((1 - c) * halves[a], 16), halves[a]), :]
            copy(a, sent, got_refs[a]).wait()

    return pl.pallas_call(
        body,
        name=name,
        in_specs=[ANY] * n,
        out_specs=[ANY] * n,
        out_shape=[_sds((N_CHIPS, h, g.shape[2]), g.dtype) for g, h in zip(grads, halves)],
        scratch_shapes=[pltpu.SemaphoreType.DMA((n,)), pltpu.SemaphoreType.DMA((n,))],
    )(*grads)


HBM = pl.BlockSpec(memory_space=pltpu.HBM)
SEM = pl.BlockSpec(memory_space=pltpu.SEMAPHORE)
DATAFLOW = pltpu.SideEffectType.DATAFLOW_SIDE_EFFECTING


def _chip_copy(p_refs, land_refs, send_sems, recv_sems, a, r, blocked):
    x, y, c = _place()
    tx, ty = _chip_of(x, y, r)
    k = a * 3 + (r - 1)
    return pltpu.make_async_remote_copy(
        src_ref=p_refs[a].at[2 * tx + ty] if blocked else p_refs[a], dst_ref=land_refs[a].at[r - 1],
        send_sem=send_sems.at[k], recv_sem=recv_sems.at[k], device_id=(tx, ty, c), device_id_type=MESH)


def _chip_exchange_start(psums, name, blocked=True):
    n = len(psums)
    lands = [lax.empty((3,) + (p.shape[1:] if blocked else p.shape), p.dtype) for p in psums]

    def body(*refs):
        p_refs, land_refs = refs[0:n], refs[n:2 * n]
        send_sems, recv_sems, token = refs[2 * n], refs[2 * n + 1], refs[-1]
        for a in range(n):
            for r in (1, 2, 3):
                _chip_copy(p_refs, land_refs, send_sems, recv_sems, a, r, blocked).start()
        token[...] = jnp.zeros_like(token)

    hbm = lambda t: pltpu.HBM(t.shape, t.dtype)
    keep = lambda t: pltpu.with_memory_space_constraint(t, pltpu.HBM)
    outs = pl.pallas_call(
        body,
        name=name,
        in_specs=[HBM] * (2 * n),
        out_specs=(SEM, SEM, *[HBM] * (2 * n), pl.BlockSpec(memory_space=pltpu.VMEM)),
        out_shape=(pltpu.SemaphoreType.DMA((3 * n,)), pltpu.SemaphoreType.DMA((3 * n,)),
                   *[hbm(p) for p in psums], *[hbm(l) for l in lands], _sds((8, 128), F32)),
        input_output_aliases={i: 2 + i for i in range(2 * n)},
        compiler_params=pltpu.CompilerParams(has_side_effects=DATAFLOW),
    )(*[keep(p) for p in psums], *[keep(l) for l in lands])
    return outs[0], outs[1], list(outs[2:2 + n]), list(outs[2 + n:2 + 2 * n]), outs[-1]


def _chip_exchange_wait(send_sems, recv_sems, p_thru, land_thru, after, name, blocked=True):
    n = len(p_thru)

    def body(*refs):
        p_refs, land_refs = refs[0:n], refs[n:2 * n]
        send_sems, recv_sems = refs[2 * n], refs[2 * n + 1]
        for a in range(n):
            for r in (1, 2, 3):
                copy = _chip_copy(p_refs, land_refs, send_sems, recv_sems, a, r, blocked)
                copy.wait_send()
                copy.wait_recv()

    hbm = lambda t: pltpu.HBM(t.shape, t.dtype)
    outs = pl.pallas_call(
        body,
        name=name,
        in_specs=[HBM] * (2 * n) + [SEM, SEM, ANY],
        out_specs=[HBM] * (2 * n),
        out_shape=[hbm(p) for p in p_thru] + [hbm(l) for l in land_thru],
        input_output_aliases={i: i for i in range(2 * n)},
        compiler_params=pltpu.CompilerParams(has_side_effects=DATAFLOW),
    )(*p_thru, *land_thru, send_sems, recv_sems, after)
    return list(outs[n:2 * n])


def _pair_share(fulls):
    n = len(fulls)
    halves = [f.shape[0] // 2 for f in fulls]

    def body(*refs):
        full_refs = refs[n:2 * n]
        send_sems, recv_sems = refs[2 * n:]
        x, y, c = _place()

        def half_of(a, core):
            return full_refs[a].at[pl.ds(pl.multiple_of(core * halves[a], 8), halves[a]), :]

        def remote(a, src, dst):
            return pltpu.make_async_remote_copy(
                src_ref=src, dst_ref=dst, send_sem=send_sems.at[a], recv_sem=recv_sems.at[a],
                device_id=(x, y, 1 - c), device_id_type=MESH)

        for a in range(n):
            for q in range(halves[a] // D2D_PIECE_ROWS):
                piece = full_refs[a].at[
                    pl.ds(pl.multiple_of(c * halves[a] + q * D2D_PIECE_ROWS, 8), D2D_PIECE_ROWS), :]
                remote(a, piece, piece).start()
        for a in range(n):
            remote(a, half_of(a, c), half_of(a, c)).wait_send()
            remote(a, half_of(a, 1 - c), half_of(a, 1 - c)).wait_recv()

    return pl.pallas_call(
        body,
        name="pair_share",
        in_specs=[ANY] * n,
        out_specs=[ANY] * n,
        out_shape=[_sds(f.shape, F32) for f in fulls],
        input_output_aliases={a: a for a in range(n)},
        scratch_shapes=[pltpu.SemaphoreType.DMA((n,)), pltpu.SemaphoreType.DMA((n,))],
    )(*fulls)


def _small_pair_sum(s):
    R, C = s.shape
    V = SMALL_VECTOR_ROWS

    def body(s_ref, v_ref, m_ref, sib, send_sem, recv_sem):
        x, y, c = _place()

        def to_sib(src, dst):
            return pltpu.make_async_remote_copy(
                src_ref=src, dst_ref=dst, send_sem=send_sem, recv_sem=recv_sem,
                device_id=(x, y, 1 - c), device_id_type=MESH)

        for q in range(R // 8):
            to_sib(s_ref.at[pl.ds(8 * q, 8), :], sib.at[pl.ds(8 * q, 8), :]).start()
        to_sib(s_ref, sib).wait()
        v_ref[...] = s_ref[pl.ds(0, V), :] + sib[pl.ds(0, V), :]
        m_ref[...] = (s_ref[pl.ds(V, R - V), :] + sib[pl.ds(V, R - V), :]).astype(BF16)

    return pl.pallas_call(
        body,
        name="small_pair_sum",
        in_specs=[pl.BlockSpec(memory_space=pltpu.VMEM)],
        out_specs=[pl.BlockSpec(memory_space=pltpu.VMEM)] * 2,
        out_shape=[jax.ShapeDtypeStruct((V, C), F32), jax.ShapeDtypeStruct((R - V, C), BF16)],
        scratch_shapes=[pltpu.VMEM((R, C), F32), pltpu.SemaphoreType.DMA, pltpu.SemaphoreType.DMA],
    )(s)


def _small_total(chip, own, landed):
    V, C = own[0].shape
    M = own[1].shape[0]

    def body(j_ref, v_ref, m_ref, lv_ref, lm_ref, o_ref, chips_v, chips_m):
        j = j_ref[0]
        chips_v[j] = v_ref[...]
        chips_m[j] = m_ref[...]
        for r in (1, 2, 3):
            chips_v[j ^ r] = lv_ref[r - 1]
            chips_m[j ^ r] = lm_ref[r - 1]
        o_ref[pl.ds(0, V), :] = (chips_v[0] + chips_v[1]) + (chips_v[2] + chips_v[3])
        o_ref[pl.ds(V, M), :] = (chips_m[0].astype(F32) + chips_m[1].astype(F32)) + (
            chips_m[2].astype(F32) + chips_m[3].astype(F32))

    vmem = pl.BlockSpec(memory_space=pltpu.VMEM)
    return pl.pallas_call(
        body,
        name="small_total",
        in_specs=[pl.BlockSpec(memory_space=pltpu.SMEM), vmem, vmem, vmem, vmem],
        out_specs=vmem,
        out_shape=jax.ShapeDtypeStruct((V + M, C), F32),
        scratch_shapes=[pltpu.VMEM((N_CHIPS, V, C), F32), pltpu.VMEM((N_CHIPS, M, C), BF16)],
    )(chip, own[0], own[1], landed[0], landed[1])


def _block_diag(w):
    w4 = w.reshape(4, 4, RNN_BLOCK_W, RNN_BLOCK_W)
    eye = jnp.eye(4, dtype=w.dtype)
    return jnp.einsum("jaik,ab->jaibk", w4, eye).reshape(4, RNN_TILE, RNN_TILE)


def _block_diag_part(d):
    d5 = d.reshape(4, 4, RNN_BLOCK_W, 4, RNN_BLOCK_W)
    return jnp.stack([d5[:, a, :, a, :] for a in range(4)], axis=1).reshape(RNN_BLOCKS, RNN_BLOCK_W, RNN_BLOCK_W)


def _local_grads(x, target, g_pre, w_in_g, b_gate, conv_w, conv_b, w_rg_a, b_rg_a, w_rg_x, b_rg_x, lam, sinks,
                 out_weights, fwd_token, g_post, on_out_grads, on_w_in_grad):
    wa_bd = _block_diag(w_rg_a).astype(BF16)
    wx_bd = _block_diag(w_rg_x).astype(BF16)
    b_a = b_rg_a.reshape(1, D_RNN)
    b_x = b_rg_x.reshape(1, D_RNN)

    proj, ht = _proj_fwd(x, g_pre, w_in_g)
    y_rnn, z_rnn, conv, z_rnn_t = _rnn_fwd(proj, conv_w, conv_b, wa_bd, wx_bd, b_a, b_x, lam, fwd_token)
    bias = _attn_bias()
    y_attn, z_attn, lse = _attn_fwd(proj, sinks, bias)
    w_rnn_out, w_attn_out, w_out = out_weights(z_attn)
    dyx, dz_rnn, dz_attn, dml, dout, dbr_rnn, dbr_attn, merged_t, z_attn_t, head_small = _head(
        x, target, z_rnn, z_attn, proj, b_gate, g_post, w_rnn_out, w_attn_out, w_out)
    out_grads = [_matmul_t(z_rnn_t, dbr_rnn, "dw_rnn_out"), _matmul_t(z_attn_t, dbr_attn, "dw_attn_out"),
                 _matmul_t(merged_t, dout, "dw_out")]
    shard_rows = lambda d: d.reshape(N_CHIPS, OUT_SHARD, D_MODEL)
    token = on_out_grads([shard_rows(g) for g, _ in out_grads], [shard_rows(gb) for _, gb in out_grads])
    dq, dk, dv, dag, attn_small = _attn_bwd(proj, y_attn, lse, dz_attn, sinks, bias, token)
    drx, drg, dwa_t, dwx_t, rnn_small = _rnn_bwd(proj, conv, y_rnn, dz_rnn, conv_w, wa_bd, wx_bd, b_a, b_x, lam)
    dproj = [drx, drg, dq, dk, dv, dag, dml]
    token = on_w_in_grad(*_dw_in(ht, dproj))
    grad_x, dh_small = _dh_bwd(dproj, w_in_g, x, dyx, g_pre, token)
    small = jnp.concatenate([rnn_small, head_small, dh_small + attn_small,
                             _block_diag_part(dwa_t).reshape(64, 1024), _block_diag_part(dwx_t).reshape(64, 1024)], axis=0)
    return grad_x, small


ROW_LOSS = 11


def _rows8(parts):
    out = None
    for r, a in parts:
        p = jnp.pad(a, ((r, 8 - r - a.shape[0]), (0, 1024 - a.shape[1])))
        out = p if out is None else out + p
    return out


def _pack_small(p):
    g0 = _rows8([(0, p["b_rg_a"].reshape(1, 1024)), (1, p["b_rg_x"].reshape(1, 1024)), (2, p["lru_lambda"]),
                 (3, p["conv_b"]), (4, p["conv_w"][0])])
    g1 = _rows8([(0, p["post_norm_g"]), (1, p["b_gate"].reshape(2, 1024))])
    g2 = _rows8([(0, p["pre_norm_g"]), (1, p["attn_sinks"])])
    return jnp.concatenate([g0, g1, g2, p["w_rg_a"].reshape(64, 1024), p["w_rg_x"].reshape(64, 1024)], axis=0)


def _unpack_small(s, conv_cols):
    return {
        "b_rg_a": s[0:1].reshape(1, 16, 64), "b_rg_x": s[1:2].reshape(1, 16, 64), "lru_lambda": s[2:3],
        "conv_b": s[3:4], "conv_w": s[4:8, 0:conv_cols].reshape(1, CONV_W, conv_cols),
        "post_norm_g": s[8:9], "b_gate": s[9:11].reshape(1, 2048),
        "pre_norm_g": s[16:17], "attn_sinks": s[17:18, 0:N_Q_HEADS],
        "w_rg_a": s[24:88].reshape(1, 16, 64, 64), "w_rg_x": s[88:152].reshape(1, 16, 64, 64),
    }


WEIGHTS = ["pre_norm_g", "w_in", "b_gate", "conv_w", "conv_b", "w_rg_a", "b_rg_a", "w_rg_x", "b_rg_x", "lru_lambda",
           "attn_sinks", "w_rnn_out", "w_attn_out", "w_out", "post_norm_g"]
BIG = ["w_in", "w_rnn_out", "w_attn_out", "w_out"]


def kernel(x, pre_norm_g, w_in, b_gate, conv_w, conv_b, w_rg_a, b_rg_a, w_rg_x, b_rg_x, lru_lambda, attn_sinks, w_rnn_out, w_attn_out, w_out, post_norm_g, loss_target, m_pre_norm_g, m_w_in, m_b_gate, m_conv_w, m_conv_b, m_w_rg_a, m_b_rg_a, m_w_rg_x, m_b_rg_x, m_lru_lambda, m_attn_sinks, m_w_rnn_out, m_w_attn_out, m_w_out, m_post_norm_g, v_pre_norm_g, v_w_in, v_b_gate, v_conv_w, v_conv_b, v_w_rg_a, v_b_rg_a, v_w_rg_x, v_b_rg_x, v_lru_lambda, v_attn_sinks, v_w_rnn_out, v_w_attn_out, v_w_out, v_post_norm_g):
    w = dict(pre_norm_g=pre_norm_g, w_in=w_in, b_gate=b_gate, conv_w=conv_w, conv_b=conv_b, w_rg_a=w_rg_a,
             b_rg_a=b_rg_a, w_rg_x=w_rg_x, b_rg_x=b_rg_x, lru_lambda=lru_lambda, attn_sinks=attn_sinks,
             w_rnn_out=w_rnn_out, w_attn_out=w_attn_out, w_out=w_out, post_norm_g=post_norm_g)
    m = dict(pre_norm_g=m_pre_norm_g, w_in=m_w_in, b_gate=m_b_gate, conv_w=m_conv_w, conv_b=m_conv_b, w_rg_a=m_w_rg_a,
             b_rg_a=m_b_rg_a, w_rg_x=m_w_rg_x, b_rg_x=m_b_rg_x, lru_lambda=m_lru_lambda, attn_sinks=m_attn_sinks,
             w_rnn_out=m_w_rnn_out, w_attn_out=m_w_attn_out, w_out=m_w_out, post_norm_g=m_post_norm_g)
    v = dict(pre_norm_g=v_pre_norm_g, w_in=v_w_in, b_gate=v_b_gate, conv_w=v_conv_w, conv_b=v_conv_b, w_rg_a=v_w_rg_a,
             b_rg_a=v_b_rg_a, w_rg_x=v_w_rg_x, b_rg_x=v_b_rg_x, lru_lambda=v_lru_lambda, attn_sinks=v_attn_sinks,
             w_rnn_out=v_w_rnn_out, w_attn_out=v_w_attn_out, w_out=v_w_out, post_norm_g=v_post_norm_g)
    chip = 2 * lax.axis_index("x") + lax.axis_index("y")

    chip_idx = chip.astype(jnp.int32).reshape(1)
    chip_core = jnp.stack([chip, lax.axis_index("c")]).astype(jnp.int32)
    cw8 = jnp.pad(conv_w[0], ((0, 8 - CONV_W), (0, 0)))
    placed = _place_shards([w_in[0], w_rnn_out[0], w_attn_out[0], w_out[0]], chip_idx, "place_shards")
    win_g, cw_g = _gather_weights(placed[:1], cw8)
    late_send, late_recv, late_thru, late_token = _gather_late_start(placed[1:], win_g, "gather_late_start")
    cw_g = lax.dynamic_update_slice_in_dim(cw_g, cw8[None], chip, axis=0)
    conv_w_full = jnp.transpose(cw_g[:, 0:CONV_W, :], (1, 0, 2)).reshape(CONV_W, D_RNN)

    started = {}

    def start_reduction(tag, grads, grads_b16):
        got = _pair_exchange(grads_b16, "pair_exchange_" + tag)
        psums, psums_b16 = _pair_sum(grads, got, chip_core, "pair_sum_" + tag)
        send_sems, recv_sems, p_thru, land_thru, token = _chip_exchange_start(psums_b16, "chip_exchange_start_" + tag)
        started[tag] = (psums, send_sems, recv_sems, p_thru, land_thru)
        return token

    def end_reduction(tag, after):
        psums, send_sems, recv_sems, p_thru, land_thru = started[tag]
        landed = _chip_exchange_wait(send_sems, recv_sems, p_thru, land_thru, after, "chip_exchange_wait_" + tag)
        return _chip_sum(psums, landed, chip_core, "chip_sum_" + tag)

    def out_weights(after):
        gathered = _gather_late_wait(late_send, late_recv, late_thru, after, "gather_late_wait")
        return [g.reshape(D_MODEL, D_MODEL) for g in gathered]

    grad_x, small = _local_grads(
        x[0], loss_target[0], pre_norm_g, win_g, b_gate, conv_w_full, conv_b, w_rg_a[0], b_rg_a[0], w_rg_x[0],
        b_rg_x[0], lru_lambda, attn_sinks[0], out_weights, late_token, post_norm_g,
        on_out_grads=lambda grads, grads_b16: start_reduction("out", grads, grads_b16),
        on_w_in_grad=lambda grad, grad_b16: start_reduction("in", [grad], [grad_b16]))

    small_chip = _small_pair_sum(small)
    small_send, small_recv, small_thru, small_land, small_token = _chip_exchange_start(
        list(small_chip), "small_exchange_start", blocked=False)

    halves = end_reduction("in", small_token) + end_reduction("out", small_token)
    gbig = dict(zip(BIG, _pair_share(halves)))

    grads, delta, new_m, new_v = {}, {}, {}, {}
    for names, tag in ((BIG[:1], "adamw_in"), (BIG[1:], "adamw_out")):
        updates = _adamw([(w[n][0], gbig[n], m[n][0], v[n][0]) for n in names], tag)
        for n, (d, nm, nv) in zip(names, updates):
            grads[n] = gbig[n][None]
            delta[n], new_m[n], new_v[n] = d[None], nm[None], nv[None]

    small_landed = _chip_exchange_wait(small_send, small_recv, small_thru, small_land, delta[BIG[-1]],
                                       "small_exchange_wait", blocked=False)
    small_sum = _small_total(chip_idx, small_thru, small_landed)
    total_loss = small_sum[ROW_LOSS, 0]
    gsmall = _unpack_small(small_sum, D_RNN)
    conv_shard = D_RNN // N_CHIPS
    gsmall["conv_w"] = lax.dynamic_slice_in_dim(gsmall["conv_w"], chip * conv_shard, conv_shard, axis=2)
    pick = lambda t: {k: t[k] for k in gsmall}
    (d, nm, nv), = _adamw([(_pack_small(pick(w)), _pack_small(gsmall), _pack_small(pick(m)), _pack_small(pick(v)))],
                          "adamw_small")
    ud, um, uv = _unpack_small(d, conv_shard), _unpack_small(nm, conv_shard), _unpack_small(nv, conv_shard)
    for n in gsmall:
        grads[n] = gsmall[n].reshape(w[n].shape)
        delta[n] = ud[n].reshape(w[n].shape)
        new_m[n] = um[n].reshape(w[n].shape)
        new_v[n] = uv[n].reshape(w[n].shape)

    return (total_loss, grad_x[None], *[grads[n] for n in WEIGHTS], *[delta[n] for n in WEIGHTS],
            *[new_m[n] for n in WEIGHTS], *[new_v[n] for n in WEIGHTS])
```

```python
import functools
import math

import jax
import jax.numpy as jnp
from jax import lax
from jax.experimental import pallas as pl
from jax.experimental.pallas import tpu as pltpu

F32 = jnp.float32
BF16 = jnp.bfloat16

D_MODEL = 1024
D_RNN = 1024
RNN_BLOCKS = 16
RNN_BLOCK_W = 64
CONV_W = 4
LRU_C = 8.0
N_Q_HEADS = 16
N_KV_HEADS = 4
GROUP = 4
HEAD_DIM = 64
D_KV = 256
BLOCK = 128
ALIBI_MAX_BIAS = 8.0
EPS = 1e-6
D_IN = 6656
N_CHIPS = 4
W_IN_SHARD = D_IN // N_CHIPS
OUT_SHARD = D_MODEL // N_CHIPS
ADAM_LR = 0.001
ADAM_B1 = 0.9
ADAM_B2 = 0.999
ADAM_EPS = 1e-08
ADAM_WD = 0.01
ADAM_STEP = 10
NEG_BIG = -1e30
MIB = 1 << 20

COL_RNN_X = 0
COL_RNN_GATE = 4
COL_Q = 8
COL_K = 12
COL_V = 13
COL_ATTN_GATE = 14
COL_MERGE = 18

RNN_TILE = 256
RNN_CHUNK = 512
SMALL_ROWS = 152
SMALL_VECTOR_ROWS = 24
MESH = pl.DeviceIdType.MESH


def _sds(shape, dtype):
    return pltpu.HBM(shape, dtype)


def _params(sem=None, vmem_mib=None):
    kw = {}
    if sem is not None:
        kw["dimension_semantics"] = sem
    if vmem_mib is not None:
        kw["vmem_limit_bytes"] = vmem_mib * MIB
    return pltpu.CompilerParams(**kw)


def _hbm(*arrays):
    return [pltpu.with_memory_space_constraint(a, pltpu.HBM) for a in arrays]


def _dot(a, b):
    return jnp.dot(a, b, preferred_element_type=F32)


def _dot_nt(a, b):
    return lax.dot_general(a, b, (((1,), (1,)), ((), ())), preferred_element_type=F32)


def _dot_tn(a, b):
    return lax.dot_general(a, b, (((0,), (0,)), ((), ())), preferred_element_type=F32)


def _sigmoid(x):
    return 0.5 * jnp.tanh(0.5 * x) + 0.5


def _sigmoid_small(x):
    return 1.0 / (1.0 + jnp.exp(-x))


def _softplus(x):
    return jnp.maximum(x, 0.0) + jnp.log(1.0 + jnp.exp(-jnp.abs(x)))


def _one_minus_square(a, log_a):
    return -jnp.tanh(log_a) * (a * a + 1.0)


def _proj_fwd(x, g_pre, w_in_g):
    T = x.shape[0]
    tm = min(1024, T)

    def body(x_ref, g_ref, w_ref, proj_ref, ht_ref, h_s):
        @pl.when(pl.program_id(1) == 0)
        def _():
            xv = x_ref[...]
            rstd = lax.rsqrt(jnp.mean(xv * xv, axis=-1, keepdims=True) + EPS)
            hf = (xv * rstd) * g_ref[...]
            h_s[...] = hf.astype(BF16)
            ht_ref[...] = hf.T.astype(BF16)

        proj_ref[...] = _dot(h_s[...], w_ref[...]).astype(BF16)

    return pl.pallas_call(
        body,
        name="proj_fwd",
        grid=(T // tm, N_CHIPS),
        in_specs=[
            pl.BlockSpec((tm, D_MODEL), lambda i, j: (i, 0)),
            pl.BlockSpec((1, D_MODEL), lambda i, j: (0, 0)),
            pl.BlockSpec((None, D_MODEL, W_IN_SHARD), lambda i, j: (j, 0, 0)),
        ],
        out_specs=[
            pl.BlockSpec((tm, W_IN_SHARD), lambda i, j: (i, j)),
            pl.BlockSpec((D_MODEL, tm), lambda i, j: (0, i)),
        ],
        out_shape=[_sds((T, D_IN), BF16), _sds((D_MODEL, T), BF16)],
        scratch_shapes=[pltpu.VMEM((tm, D_MODEL), BF16)],
        compiler_params=_params(("parallel", "arbitrary"), 48),
    )(*_hbm(x, g_pre, w_in_g))


def _shift_down(x, tail, s, row):
    n = x.shape[0]
    xs = pltpu.roll(x, s, 0)
    tail_t = jnp.tile(pltpu.roll(tail, s, 0), (n // 8, 1))
    return jnp.where(row < s, tail_t, xs)


def _shift_up(x, head, s, row):
    n = x.shape[0]
    xs = pltpu.roll(x, n - s, 0)
    head_t = jnp.tile(pltpu.roll(head, 8 - s, 0), (n // 8, 1))
    return jnp.where(row >= n - s, head_t, xs)


def _conv_taps(x, tail, row):
    return [_shift_down(x, tail, 3, row), _shift_down(x, tail, 2, row), _shift_down(x, tail, 1, row), x]


def _rglru_gates(c, wa, wx, ba, bx, lam):
    cb = c.astype(BF16)
    r = _sigmoid_small(_dot(cb, wa) + ba)
    i = _sigmoid(_dot(cb, wx) + bx)
    log_a = (-LRU_C) * r * _softplus(-lam)
    a = jnp.exp(log_a)
    w = _one_minus_square(a, log_a)
    inv_mult = lax.rsqrt(w)
    return cb, r, i, a, w * inv_mult, inv_mult


SUBLANES = 8


def _scan_down(a, u, row):
    n = a.shape[0]
    s = 1
    while s < SUBLANES:
        a_sh = jnp.where(row >= s, pltpu.roll(a, s, 0), 1.0)
        u_sh = jnp.where(row >= s, pltpu.roll(u, s, 0), 0.0)
        u = a * u_sh + u
        a = a * a_sh
        s *= 2
    while s < n:
        u = jnp.concatenate([u[:s], a[s:] * u[:n - s] + u[s:]], axis=0)
        a = jnp.concatenate([a[:s], a[s:] * a[:n - s]], axis=0)
        s *= 2
    return a, u


def _scan_up(b, u, row):
    n = b.shape[0]
    s = 1
    while s < SUBLANES:
        b_sh = jnp.where(row < n - s, pltpu.roll(b, n - s, 0), 1.0)
        u_sh = jnp.where(row < n - s, pltpu.roll(u, n - s, 0), 0.0)
        u = b * u_sh + u
        b = b * b_sh
        s *= 2
    while s < n:
        u = jnp.concatenate([b[:n - s] * u[s:] + u[:n - s], u[n - s:]], axis=0)
        b = jnp.concatenate([b[:n - s] * b[s:], b[n - s:]], axis=0)
        s *= 2
    return b, u


LANES = 128


def _chunk_scan(a, u, a_s, u_s, hl_s, al_s, carry, reverse):
    n, width = a.shape
    groups = n // SUBLANES
    order = range(SUBLANES - 1, -1, -1) if reverse else range(SUBLANES)
    row = lax.broadcasted_iota(jnp.int32, (groups, LANES), 0)
    for l in range(width // LANES):
        lanes = slice(l * LANES, (l + 1) * LANES)
        a_l, u_l, hl_l, al_l = a_s.at[l], u_s.at[l], hl_s.at[l], al_s.at[l]
        a_l[...] = a[:, lanes]
        u_l[...] = u[:, lanes]
        h_loc = a_loc = None
        for r in order:
            rows = pl.ds(r, groups, stride=SUBLANES)
            a_r, u_r = a_l[rows, :], u_l[rows, :]
            h_loc, a_loc = (u_r, a_r) if h_loc is None else (a_r * h_loc + u_r, a_r * a_loc)
            hl_l[rows, :] = h_loc
            al_l[rows, :] = a_loc
        if reverse:
            a_cum, ends = _scan_up(a_loc, h_loc, row)
            ends = ends + a_cum * carry[:, lanes]
            enters = jnp.where(row == groups - 1, carry[:, lanes], pltpu.roll(ends, groups - 1, 0))
        else:
            a_cum, ends = _scan_down(a_loc, h_loc, row)
            ends = ends + a_cum * carry[:, lanes]
            enters = jnp.where(row == 0, carry[:, lanes], pltpu.roll(ends, 1, 0))
        for r in range(SUBLANES):
            rows = pl.ds(r, groups, stride=SUBLANES)
            hl_l[rows, :] = hl_l[rows, :] + al_l[rows, :] * enters
    return jnp.concatenate([hl_s[l] for l in range(width // LANES)], axis=1)


def _rnn_fwd(proj, conv_w, conv_b, wa_bd, wx_bd, b_a, b_x, lam, token):
    T = proj.shape[0]
    tc, ct = RNN_CHUNK, RNN_TILE
    nt = T // tc

    def body(x_ref, rg_ref, cw_ref, cb_ref, wa_ref, wx_ref, ba_ref, bx_ref, lam_ref, token_ref, h_ref, z_ref, c_ref,
             zt_ref, xtail, hcarry, a_s, u_s, hl_s, al_s):
        @pl.when(pl.program_id(1) == 0)
        def _():
            xtail[...] = jnp.zeros_like(xtail)
            hcarry[...] = jnp.zeros_like(hcarry)

        row = lax.broadcasted_iota(jnp.int32, (tc, ct), 0)
        x = x_ref[...].astype(F32)
        taps = _conv_taps(x, xtail[...], row)
        c = cb_ref[...] + cw_ref[pl.ds(0, 1), :] * taps[0]
        for k in range(1, CONV_W):
            c = c + cw_ref[pl.ds(k, 1), :] * taps[k]
        xtail[...] = x[tc - 8:, :]
        c_ref[...] = c
        _, _, i, a, mult, _ = _rglru_gates(c, wa_ref[...], wx_ref[...], ba_ref[...], bx_ref[...], lam_ref[...])
        h = _chunk_scan(a, mult * (i * c), a_s, u_s, hl_s, al_s, hcarry[...], reverse=False)
        h_ref[...] = h
        hcarry[...] = h_ref[pl.ds(tc - 1, 1), :]
        rg = rg_ref[...].astype(F32)
        z = h * (rg * _sigmoid(rg))
        z_ref[...] = z.astype(BF16)
        zt_ref[...] = z.T.astype(BF16)

    col = lambda off: (lambda j, t: (t, off + j))
    vec = pl.BlockSpec((1, ct), lambda j, t: (0, j))
    mat = pl.BlockSpec((None, ct, ct), lambda j, t: (j, 0, 0))
    return pl.pallas_call(
        body,
        name="rnn_fwd",
        grid=(D_RNN // ct, nt),
        in_specs=[
            pl.BlockSpec((tc, ct), col(COL_RNN_X)),
            pl.BlockSpec((tc, ct), col(COL_RNN_GATE)),
            pl.BlockSpec((CONV_W, ct), lambda j, t: (0, j)),
            vec, mat, mat, vec, vec, vec,
            pl.BlockSpec((8, 128), lambda j, t: (0, 0)),
        ],
        out_specs=[pl.BlockSpec((tc, ct), lambda j, t: (t, j))] * 3 + [pl.BlockSpec((ct, tc), lambda j, t: (j, t))],
        out_shape=[_sds((T, D_RNN), F32), _sds((T, D_RNN), BF16), _sds((T, D_RNN), F32), _sds((D_RNN, T), BF16)],
        scratch_shapes=[pltpu.VMEM((8, ct), F32), pltpu.VMEM((1, ct), F32)] + [
            pltpu.VMEM((ct // LANES, tc, LANES), F32)] * 4,
        compiler_params=_params(("parallel", "arbitrary"), 32),
    )(*_hbm(proj, proj, conv_w, conv_b, wa_bd, wx_bd, b_a, b_x, lam, token))


def _rnn_bwd(proj, conv, y_rnn, dz_rnn, conv_w, wa_bd, wx_bd, b_a, b_x, lam):
    T = proj.shape[0]
    tc, ct = RNN_CHUNK, RNN_TILE
    nt = T // tc
    hb = tc // 8

    def body(x_ref, c_ref, rg_ref, h_ref, hh_ref, dz_ref, cw_ref, wa_ref, wx_ref, ba_ref, bx_ref, lam_ref,
             dx_ref, drg_ref, dwa_ref, dwx_ref, sm_ref, lam_carry, a_carry, dc_head, b_s, dy_s, hl_s, al_s):
        t = pl.program_id(1)
        first_chunk = t == nt - 1

        @pl.when(t == 0)
        def _():
            lam_carry[...] = jnp.zeros_like(lam_carry)
            a_carry[...] = jnp.zeros_like(a_carry)
            dc_head[...] = jnp.zeros_like(dc_head)
            dwa_ref[...] = jnp.zeros_like(dwa_ref)
            dwx_ref[...] = jnp.zeros_like(dwx_ref)
            sm_ref[...] = jnp.zeros_like(sm_ref)

        row = lax.broadcasted_iota(jnp.int32, (tc, ct), 0)
        keep = jnp.where(first_chunk, 0.0, 1.0)
        x = x_ref[...].astype(F32)
        c = c_ref[...]
        lam = lam_ref[...]
        cb, r, i, a, mult, inv_mult = _rglru_gates(c, wa_ref[...], wx_ref[...], ba_ref[...], bx_ref[...], lam)
        h = h_ref[...]
        h_prev = _shift_down(h, hh_ref[...] * keep, 1, row)
        rg = rg_ref[...].astype(F32)
        dz = dz_ref[...]
        sg = _sigmoid(rg)
        drg_ref[...] = (dz * h * (sg * (1.0 + rg * (1.0 - sg)))).astype(BF16)
        dy = dz * (rg * sg)
        b = jnp.where(row >= tc - 1, a_carry[pl.ds(0, 1), :], pltpu.roll(a, tc - 1, 0))
        lt = _chunk_scan(b, dy, b_s, dy_s, hl_s, al_s, lam_carry[pl.ds(0, 1), :], reverse=True)
        lam_carry[...] = lt[0:8, :]
        a_carry[...] = a[0:8, :]
        ic = i * c
        dmult = lt * ic
        di = lt * mult * c
        dc = lt * mult * i
        dlog_a = a * (lt * h_prev - dmult * a * inv_mult)
        sp = _softplus(-lam)
        dpre_r = dlog_a * ((-LRU_C) * sp) * (r * (1.0 - r))
        dpre_i = di * (i * (1.0 - i))
        dlam_row = jnp.sum(dlog_a * r, axis=0, keepdims=True) * (LRU_C * _sigmoid(-lam))
        dpr_b = dpre_r.astype(BF16)
        dpi_b = dpre_i.astype(BF16)
        dwa_ref[...] += _dot_tn(cb, dpr_b)
        dwx_ref[...] += _dot_tn(cb, dpi_b)
        dc = dc + _dot_nt(dpr_b, wa_ref[...]) + _dot_nt(dpi_b, wx_ref[...])
        head = dc_head[...]
        dx = cw_ref[pl.ds(3, 1), :] * dc
        sm_ref[pl.ds(4 + 3, 1), :] += jnp.sum(dc * x, axis=0, keepdims=True)
        for m in range(1, CONV_W):
            up = _shift_up(dc, head, m, row)
            dx = dx + cw_ref[pl.ds(3 - m, 1), :] * up
            sm_ref[pl.ds(4 + 3 - m, 1), :] += jnp.sum(up * x, axis=0, keepdims=True)
        dx_ref[...] = dx.astype(BF16)
        dc_head[...] = dc[0:8, :]
        sm_ref[pl.ds(0, 1), :] += jnp.sum(dpre_r, axis=0, keepdims=True)
        sm_ref[pl.ds(1, 1), :] += jnp.sum(dpre_i, axis=0, keepdims=True)
        sm_ref[pl.ds(2, 1), :] += dlam_row
        sm_ref[pl.ds(3, 1), :] += jnp.sum(dc, axis=0, keepdims=True)

    rev = lambda off: (lambda j, t: (nt - 1 - t, off + j))
    halo = lambda off: (lambda j, t: (jnp.maximum((nt - 1 - t) * hb - 1, 0), off + j))
    vec = pl.BlockSpec((1, ct), lambda j, t: (0, j))
    mat = pl.BlockSpec((None, ct, ct), lambda j, t: (j, 0, 0))
    return pl.pallas_call(
        body,
        name="rnn_bwd",
        grid=(D_RNN // ct, nt),
        in_specs=[
            pl.BlockSpec((tc, ct), rev(COL_RNN_X)),
            pl.BlockSpec((tc, ct), rev(0)),
            pl.BlockSpec((tc, ct), rev(COL_RNN_GATE)),
            pl.BlockSpec((tc, ct), rev(0)),
            pl.BlockSpec((8, ct), halo(0)),
            pl.BlockSpec((tc, ct), rev(0)),
            pl.BlockSpec((CONV_W, ct), lambda j, t: (0, j)),
            mat, mat, vec, vec, vec,
        ],
        out_specs=[
            pl.BlockSpec((tc, ct), rev(0)),
            pl.BlockSpec((tc, ct), rev(0)),
            mat, mat,
            pl.BlockSpec((8, ct), lambda j, t: (0, j)),
        ],
        out_shape=[_sds((T, D_RNN), BF16), _sds((T, D_RNN), BF16), _sds((D_RNN // ct, ct, ct), F32),
                   _sds((D_RNN // ct, ct, ct), F32), _sds((8, D_RNN), F32)],
        scratch_shapes=[pltpu.VMEM((8, ct), F32)] * 3 + [pltpu.VMEM((ct // LANES, tc, LANES), F32)] * 4,
        compiler_params=_params(("parallel", "arbitrary"), 32),
    )(*_hbm(proj, conv, proj, y_rnn, y_rnn, dz_rnn, conv_w, wa_bd, wx_bd, b_a, b_x, lam))


def _attn_bias():
    qi = jnp.arange(BLOCK)[:, None]
    kj = jnp.arange(BLOCK)[None, :]
    dist_cur = (qi - kj).astype(F32)
    slopes = 2.0 ** (-ALIBI_MAX_BIAS * jnp.arange(1, N_Q_HEADS + 1, dtype=F32) / N_Q_HEADS)
    slopes = slopes[:, None, None]
    prev = jnp.where(kj > qi, -slopes * (dist_cur + float(BLOCK)), NEG_BIG)
    cur = jnp.where(kj <= qi, -slopes * dist_cur, NEG_BIG)
    later = jnp.concatenate([prev, cur], axis=-1)
    first = jnp.concatenate([jnp.full_like(prev, NEG_BIG), cur], axis=-1)
    return jnp.stack([first, later])


def _attn_exps(s_prev, s_cur, sink, bias):
    s_prev = s_prev + bias[:, 0:BLOCK]
    s_cur = s_cur + bias[:, BLOCK:2 * BLOCK]
    m = jnp.maximum(jnp.max(jnp.maximum(s_prev, s_cur), axis=-1, keepdims=True), sink)
    p_prev = jnp.exp(s_prev - m)
    p_cur = jnp.exp(s_cur - m)
    total = jnp.sum(p_prev + p_cur, axis=-1, keepdims=True) + jnp.exp(sink - m)
    return p_prev, p_cur, 1.0 / total, m + jnp.log(total)


def _attn_probs(s_prev, s_cur, sink, bias, lse):
    p_prev = jnp.exp((s_prev + bias[:, 0:BLOCK]) - lse)
    p_cur = jnp.exp((s_cur + bias[:, BLOCK:2 * BLOCK]) - lse)
    return p_prev, p_cur, jnp.exp(sink - lse)


def _stack_heads(ref_or_val, hk, dtype):
    parts = [ref_or_val[:, (GROUP * hk + g) * HEAD_DIM:(GROUP * hk + g + 1) * HEAD_DIM] for g in range(GROUP)]
    return jnp.concatenate(parts, axis=0).astype(dtype)


ATTN_SCALE = HEAD_DIM ** -0.5


def _bias_spec():
    return pl.BlockSpec((None, N_Q_HEADS, BLOCK, 2 * BLOCK), lambda i: (jnp.minimum(i, 1), 0, 0, 0))


def _attn_fwd(proj, sinks, bias):
    T = proj.shape[0]
    nb = T // BLOCK

    def body(sink_ref, bias_ref, q_ref, kp_ref, kc_ref, vp_ref, vc_ref, ag0_ref, ag1_ref, y_ref, z_ref, lse_ref):
        kvs = [slice(hk * HEAD_DIM, (hk + 1) * HEAD_DIM) for hk in range(N_KV_HEADS)]
        qgs = [(_stack_heads(q_ref, hk, F32) * ATTN_SCALE).astype(BF16) for hk in range(N_KV_HEADS)]
        s_prev = [_dot_nt(qgs[hk], kp_ref[:, kvs[hk]].astype(BF16)) for hk in range(N_KV_HEADS)]
        s_cur = [_dot_nt(qgs[hk], kc_ref[:, kvs[hk]].astype(BF16)) for hk in range(N_KV_HEADS)]
        for hk in range(N_KV_HEADS):
            pp, pc, invs = [], [], []
            for g in range(GROUP):
                h = GROUP * hk + g
                rows = slice(g * BLOCK, (g + 1) * BLOCK)
                p_prev, p_cur, inv, lse = _attn_exps(s_prev[hk][rows], s_cur[hk][rows], sink_ref[h], bias_ref[h])
                pp.append(p_prev.astype(BF16))
                pc.append(p_cur.astype(BF16))
                invs.append(inv)
                lse_ref[:, h:h + 1] = lse
            og = _dot(jnp.concatenate(pp, axis=0), vp_ref[:, kvs[hk]].astype(BF16)) + _dot(
                jnp.concatenate(pc, axis=0), vc_ref[:, kvs[hk]].astype(BF16))
            for g in range(GROUP):
                h = GROUP * hk + g
                y_ref[:, h * HEAD_DIM:(h + 1) * HEAD_DIM] = og[g * BLOCK:(g + 1) * BLOCK] * invs[g]
        ag = jnp.concatenate([ag0_ref[...], ag1_ref[...]], axis=1).astype(F32)
        z_ref[...] = (y_ref[...] * (ag * _sigmoid(ag))).astype(BF16)

    prev = lambda c: (lambda i: (jnp.maximum(i - 1, 0), c))
    cur = lambda c: (lambda i: (i, c))
    return pl.pallas_call(
        body,
        name="attn_fwd",
        grid=(nb,),
        in_specs=[
            pl.BlockSpec(memory_space=pltpu.SMEM),
            _bias_spec(),
            pl.BlockSpec((BLOCK, 1024), lambda i: (i, COL_Q // 4)),
            pl.BlockSpec((BLOCK, D_KV), prev(COL_K)),
            pl.BlockSpec((BLOCK, D_KV), cur(COL_K)),
            pl.BlockSpec((BLOCK, D_KV), prev(COL_V)),
            pl.BlockSpec((BLOCK, D_KV), cur(COL_V)),
            pl.BlockSpec((BLOCK, 512), lambda i: (i, COL_ATTN_GATE // 2)),
            pl.BlockSpec((BLOCK, 512), lambda i: (i, COL_ATTN_GATE // 2 + 1)),
        ],
        out_specs=[pl.BlockSpec((BLOCK, 1024), lambda i: (i, 0)), pl.BlockSpec((BLOCK, 1024), lambda i: (i, 0)),
                   pl.BlockSpec((BLOCK, N_Q_HEADS), lambda i: (i, 0))],
        out_shape=[_sds((T, 1024), F32), _sds((T, 1024), BF16), _sds((T, N_Q_HEADS), F32)],
        compiler_params=_params(("arbitrary",), 32),
    )(sinks, *_hbm(bias, proj, proj, proj, proj, proj, proj, proj))


def _attn_bwd(proj, y_attn, lse, dz_attn, sinks, bias, token):
    T = proj.shape[0]
    nb = T // BLOCK

    def body(sink_ref, bias_ref, q_ref, kp_ref, kc_ref, vp_ref, vc_ref, ag0_ref, ag1_ref, y_ref, lse_ref, dz_ref,
             token_ref, dq_ref, dk_ref, dv_ref, dag_ref, ds_ref, dy_s):
        i = pl.program_id(0)

        @pl.when(i == 0)
        def _():
            ds_ref[...] = jnp.zeros_like(ds_ref)

        lane = lax.broadcasted_iota(jnp.int32, (8, 128), 1)
        sub = lax.broadcasted_iota(jnp.int32, (8, 128), 0)
        ag = jnp.concatenate([ag0_ref[...], ag1_ref[...]], axis=1).astype(F32)
        dz = dz_ref[...]
        sg = _sigmoid(ag)
        dag_ref[...] = (dz * y_ref[...] * (sg * (1.0 + ag * (1.0 - sg)))).astype(BF16)
        dy_s[...] = dz * (ag * sg)
        r_cur = pl.multiple_of(i * BLOCK, BLOCK)
        r_prev = pl.multiple_of(jnp.maximum(i - 1, 0) * BLOCK, BLOCK)
        dk_cur, dv_cur, dk_prev, dv_prev = [], [], [], []
        ds_acc = jnp.zeros((8, 128), F32)
        for hk in range(N_KV_HEADS):
            ks = slice(hk * HEAD_DIM, (hk + 1) * HEAD_DIM)
            qg = (_stack_heads(q_ref, hk, F32) * ATTN_SCALE).astype(BF16)
            dog = _stack_heads(dy_s, hk, F32)
            og = _stack_heads(y_ref, hk, F32)
            dog_b = dog.astype(BF16)
            kp = kp_ref[:, ks].astype(BF16)
            kc = kc_ref[:, ks].astype(BF16)
            vp = vp_ref[:, ks].astype(BF16)
            vc = vc_ref[:, ks].astype(BF16)
            s_prev = _dot_nt(qg, kp)
            s_cur = _dot_nt(qg, kc)
            dp_prev = _dot_nt(dog_b, vp)
            dp_cur = _dot_nt(dog_b, vc)
            dvec = jnp.sum(dog * og, axis=-1, keepdims=True)
            pp, pc, dsp, dsc = [], [], [], []
            for g in range(GROUP):
                h = GROUP * hk + g
                rows = slice(g * BLOCK, (g + 1) * BLOCK)
                p_prev, p_cur, p_sink = _attn_probs(
                    s_prev[rows], s_cur[rows], sink_ref[h], bias_ref[h], lse_ref[:, h:h + 1])
                d_h = dvec[rows]
                pp.append(p_prev.astype(BF16))
                pc.append(p_cur.astype(BF16))
                dsp.append((p_prev * (dp_prev[rows] - d_h)).astype(BF16))
                dsc.append((p_cur * (dp_cur[rows] - d_h)).astype(BF16))
                dsink = -jnp.sum(p_sink * d_h, axis=0, keepdims=True)
                ds_acc = ds_acc + jnp.where(jnp.logical_and(lane == h, sub == 1), dsink, 0.0)
            pp = jnp.concatenate(pp, axis=0)
            pc = jnp.concatenate(pc, axis=0)
            dsp = jnp.concatenate(dsp, axis=0)
            dsc = jnp.concatenate(dsc, axis=0)
            dqg = (_dot(dsp, kp) + _dot(dsc, kc)) * ATTN_SCALE
            for g in range(GROUP):
                h = GROUP * hk + g
                dq_ref[:, h * HEAD_DIM:(h + 1) * HEAD_DIM] = dqg[g * BLOCK:(g + 1) * BLOCK].astype(BF16)
            dk_ref[pl.ds(r_cur, BLOCK), ks] = _dot_tn(dsc, qg)
            dv_ref[pl.ds(r_cur, BLOCK), ks] = _dot_tn(pc, dog_b)
            dk_prev.append(_dot_tn(dsp, qg))
            dv_prev.append(_dot_tn(pp, dog_b))
        ds_ref[:, 0:128] += ds_acc

        @pl.when(i > 0)
        def _():
            for hk in range(N_KV_HEADS):
                ks = slice(hk * HEAD_DIM, (hk + 1) * HEAD_DIM)
                dk_ref[pl.ds(r_prev, BLOCK), ks] += dk_prev[hk]
                dv_ref[pl.ds(r_prev, BLOCK), ks] += dv_prev[hk]

    prev = lambda c: (lambda i: (jnp.maximum(i - 1, 0), c))
    cur = lambda c: (lambda i: (i, c))
    blk = pl.BlockSpec((BLOCK, 1024), lambda i: (i, 0))
    whole = pl.BlockSpec((T, D_KV), lambda i: (0, 0))
    return pl.pallas_call(
        body,
        name="attn_bwd",
        grid=(nb,),
        in_specs=[
            pl.BlockSpec(memory_space=pltpu.SMEM),
            _bias_spec(),
            pl.BlockSpec((BLOCK, 1024), lambda i: (i, COL_Q // 4)),
            pl.BlockSpec((BLOCK, D_KV), prev(COL_K)),
            pl.BlockSpec((BLOCK, D_KV), cur(COL_K)),
            pl.BlockSpec((BLOCK, D_KV), prev(COL_V)),
            pl.BlockSpec((BLOCK, D_KV), cur(COL_V)),
            pl.BlockSpec((BLOCK, 512), lambda i: (i, COL_ATTN_GATE // 2)),
            pl.BlockSpec((BLOCK, 512), lambda i: (i, COL_ATTN_GATE // 2 + 1)),
            blk,
            pl.BlockSpec((BLOCK, N_Q_HEADS), lambda i: (i, 0)),
            blk,
            pl.BlockSpec((8, 128), lambda i: (0, 0)),
        ],
        out_specs=[blk, whole, whole, blk, pl.BlockSpec((8, 1024), lambda i: (0, 0))],
        out_shape=[_sds((T, 1024), BF16), _sds((T, D_KV), F32), _sds((T, D_KV), F32), _sds((T, 1024), BF16),
                   _sds((8, 1024), F32)],
        scratch_shapes=[pltpu.VMEM((BLOCK, 1024), F32)],
        compiler_params=_params(("arbitrary",), 48),
    )(sinks, *_hbm(bias, proj, proj, proj, proj, proj, proj, proj, y_attn, lse, dz_attn, token))


def _head(x, target, z_rnn, z_attn, proj, b_gate, g_post, w_rnn_out, w_attn_out, w_out):
    T = x.shape[0]
    tm = 256

    def body(x_ref, t_ref, zr_ref, za_ref, ml0_ref, ml1_ref, ml2_ref, ml3_ref, bg_ref, gp_ref, wr_ref, wa_ref, wo_ref,
             dyx_ref, dzr_ref, dza_ref, dml_ref, dout_ref, dbr_ref, dba_ref, mt_ref, zat_ref, sm_ref):
        @pl.when(pl.program_id(0) == 0)
        def _():
            sm_ref[...] = jnp.zeros_like(sm_ref)

        wr, wa, wo = wr_ref[...], wa_ref[...], wo_ref[...]
        br_rnn = _dot(zr_ref[...], wr)
        br_attn = _dot(za_ref[...], wa)
        zat_ref[...] = za_ref[...].astype(F32).T.astype(BF16)
        ml_rnn = jnp.concatenate([ml0_ref[...], ml1_ref[...]], axis=1).astype(F32)
        ml_attn = jnp.concatenate([ml2_ref[...], ml3_ref[...]], axis=1).astype(F32)
        g_rnn = _sigmoid(ml_rnn + bg_ref[:, 0:D_MODEL])
        g_attn = _sigmoid(ml_attn + bg_ref[:, D_MODEL:2 * D_MODEL])
        merged = g_rnn * br_rnn + g_attn * br_attn
        mb = merged.astype(BF16)
        mt_ref[...] = merged.T.astype(BF16)
        out = _dot(mb, wo)
        rstd = lax.rsqrt(jnp.mean(out * out, axis=-1, keepdims=True) + EPS)
        n = out * rstd
        gp = gp_ref[...]
        err = (x_ref[...] + n * gp) - t_ref[...]
        sm_ref[pl.ds(3, 1), :] += 0.5 * jnp.sum(jnp.mean(err * err, axis=-1, keepdims=True), axis=0, keepdims=True)
        dy = err * (1.0 / D_MODEL)
        dyx_ref[...] = dy
        sm_ref[pl.ds(0, 1), :] += jnp.sum(dy * n, axis=0, keepdims=True)
        dn = dy * gp
        dout = (rstd * (dn - n * jnp.mean(dn * n, axis=-1, keepdims=True))).astype(BF16)
        dout_ref[...] = dout
        dmerged = _dot_nt(dout, wo)
        dml_r = (dmerged * br_rnn) * (g_rnn * (1.0 - g_rnn))
        dml_a = (dmerged * br_attn) * (g_attn * (1.0 - g_attn))
        dml_ref[:, 0:D_MODEL] = dml_r.astype(BF16)
        dml_ref[:, D_MODEL:2 * D_MODEL] = dml_a.astype(BF16)
        sm_ref[pl.ds(1, 1), :] += jnp.sum(dml_r, axis=0, keepdims=True)
        sm_ref[pl.ds(2, 1), :] += jnp.sum(dml_a, axis=0, keepdims=True)
        dbr = (dmerged * g_rnn).astype(BF16)
        dba = (dmerged * g_attn).astype(BF16)
        dbr_ref[...] = dbr
        dba_ref[...] = dba
        dzr_ref[...] = _dot_nt(dbr, wr)
        dza_ref[...] = _dot_nt(dba, wa)

    tile = pl.BlockSpec((tm, D_MODEL), lambda i: (i, 0))
    wspec = pl.BlockSpec((D_MODEL, D_MODEL), lambda i: (0, 0))
    ml = lambda q: pl.BlockSpec((tm, 512), lambda i: (i, COL_MERGE // 2 + q))
    return pl.pallas_call(
        body,
        name="head",
        grid=(T // tm,),
        in_specs=[
            tile, tile, tile, tile,
            ml(0), ml(1), ml(2), ml(3),
            pl.BlockSpec((1, 2 * D_MODEL), lambda i: (0, 0)),
            pl.BlockSpec((1, D_MODEL), lambda i: (0, 0)),
            wspec, wspec, wspec,
        ],
        out_specs=[
            tile, tile, tile,
            pl.BlockSpec((tm, 2 * D_MODEL), lambda i: (i, 0)),
            tile, tile, tile,
            pl.BlockSpec((D_MODEL, tm), lambda i: (0, i)), pl.BlockSpec((D_MODEL, tm), lambda i: (0, i)),
            pl.BlockSpec((8, D_MODEL), lambda i: (0, 0)),
        ],
        out_shape=[
            _sds((T, D_MODEL), F32), _sds((T, D_MODEL), F32), _sds((T, D_MODEL), F32),
            _sds((T, 2 * D_MODEL), BF16),
            _sds((T, D_MODEL), BF16), _sds((T, D_MODEL), BF16), _sds((T, D_MODEL), BF16),
            _sds((D_MODEL, T), BF16), _sds((D_MODEL, T), BF16),
            _sds((8, D_MODEL), F32),
        ],
        compiler_params=_params(("arbitrary",), 56),
    )(*_hbm(x, target, z_rnn, z_attn, proj, proj, proj, proj, b_gate, g_post, w_rnn_out, w_attn_out, w_out))


def _matmul_t(at, b, name):
    M, T = at.shape
    N = b.shape[1]
    tk = min(1024, T)
    nt = T // tk

    def body(a_ref, b_ref, o_ref, ob_ref):
        @pl.when(pl.program_id(0) == 0)
        def _():
            o_ref[...] = jnp.zeros_like(o_ref)

        o_ref[...] += _dot(a_ref[...], b_ref[...])

        @pl.when(pl.program_id(0) == nt - 1)
        def _():
            ob_ref[...] = o_ref[...].astype(BF16)

    whole = pl.BlockSpec((M, N), lambda t: (0, 0))
    return pl.pallas_call(
        body,
        name=name,
        grid=(nt,),
        in_specs=[pl.BlockSpec((M, tk), lambda t: (0, t)), pl.BlockSpec((tk, N), lambda t: (t, 0))],
        out_specs=[whole, whole],
        out_shape=[_sds((M, N), F32), _sds((M, N), BF16)],
        compiler_params=_params(("arbitrary",), 48),
    )(*_hbm(at, b))


DPROJ_WIDTHS = (D_RNN, D_RNN, 1024, D_KV, D_KV, 1024, 2 * D_MODEL)


def _dproj_segments():
    segs, start = [[] for _ in range(N_CHIPS)], 0
    for p, width in enumerate(DPROJ_WIDTHS):
        for c in range(N_CHIPS):
            lo, hi = max(start, c * W_IN_SHARD), min(start + width, (c + 1) * W_IN_SHARD)
            if lo < hi:
                segs[c].append((p, lo - start, hi - start, lo - c * W_IN_SHARD, hi - c * W_IN_SHARD))
        start += width
    return segs


def _dh_bwd(pieces, w_in_g, x, dyx, g_pre, token):
    T = x.shape[0]
    tm = min(512, T)
    n = len(pieces)
    segs = _dproj_segments()

    def body(*refs):
        p_refs, w_hbm, x_ref, dyx_ref, g_ref = refs[0:n], refs[n], refs[n + 1], refs[n + 2], refs[n + 3]
        gx_ref, dg_ref, w_ref, w_sems = refs[n + 5], refs[n + 6], refs[n + 7], refs[n + 8]
        first = pl.program_id(0) == 0
        w_copies = [pltpu.make_async_copy(w_hbm.at[c], w_ref.at[c], w_sems.at[c]) for c in range(N_CHIPS)]

        @pl.when(first)
        def _():
            for cp in w_copies:
                cp.start()
            dg_ref[...] = jnp.zeros_like(dg_ref)

        dh = None
        for c in range(N_CHIPS):
            pl.when(first)(w_copies[c].wait)
            for p, a0, a1, u0, u1 in segs[c]:
                part = _dot_nt(p_refs[p][:, a0:a1].astype(BF16), w_ref[c, :, u0:u1])
                dh = part if dh is None else dh + part
        xv = x_ref[...]
        rstd = lax.rsqrt(jnp.mean(xv * xv, axis=-1, keepdims=True) + EPS)
        nx = xv * rstd
        dhg = dh * g_ref[...]
        gx_ref[...] = dyx_ref[...] + rstd * (dhg - nx * jnp.mean(dhg * nx, axis=-1, keepdims=True))
        dg_ref[pl.ds(0, 1), :] += jnp.sum(dh * nx, axis=0, keepdims=True)

    tile = pl.BlockSpec((tm, D_MODEL), lambda i: (i, 0))
    return pl.pallas_call(
        body,
        name="dh_bwd",
        grid=(T // tm,),
        in_specs=[pl.BlockSpec((tm, w), lambda i: (i, 0)) for w in DPROJ_WIDTHS] + [
            ANY, tile, tile,
            pl.BlockSpec((1, D_MODEL), lambda i: (0, 0)),
            pl.BlockSpec((8, 128), lambda i: (0, 0)),
        ],
        out_specs=[tile, pl.BlockSpec((8, D_MODEL), lambda i: (0, 0))],
        out_shape=[_sds((T, D_MODEL), F32), _sds((8, D_MODEL), F32)],
        scratch_shapes=[pltpu.VMEM(w_in_g.shape, BF16), pltpu.SemaphoreType.DMA((N_CHIPS,))],
        compiler_params=_params(("arbitrary",), 56),
    )(*_hbm(*pieces, w_in_g, x, dyx, g_pre, token))


def _dw_in(ht, pieces):
    T = ht.shape[1]
    tk = min(512, T)
    nt = T // tk
    n = len(pieces)
    segs = _dproj_segments()

    def body(*refs):
        h_ref, p_refs, o_ref, ob_ref = refs[0], refs[1:n + 1], refs[n + 1], refs[n + 2]

        @pl.when(pl.program_id(1) == 0)
        def _():
            o_ref[...] = jnp.zeros_like(o_ref)

        for c in range(N_CHIPS):
            @pl.when(pl.program_id(0) == c)
            def _():
                for p, a0, a1, u0, u1 in segs[c]:
                    o_ref[:, u0:u1] += _dot(h_ref[...], p_refs[p][:, a0:a1].astype(BF16))

        @pl.when(pl.program_id(1) == nt - 1)
        def _():
            ob_ref[...] = o_ref[...].astype(BF16)

    def piece_spec(p):
        chips = [c for c in range(N_CHIPS) if any(s[0] == p for s in segs[c])]

        def index(c, t):
            used = functools.reduce(jnp.logical_or, [c == k for k in chips])
            return (jnp.where(used, t, 0), 0)

        return pl.BlockSpec((tk, DPROJ_WIDTHS[p]), index)

    return pl.pallas_call(
        body,
        name="dw_in",
        grid=(N_CHIPS, nt),
        in_specs=[pl.BlockSpec((D_MODEL, tk), lambda c, t: (0, t))] + [piece_spec(p) for p in range(n)],
        out_specs=[pl.BlockSpec((None, D_MODEL, W_IN_SHARD), lambda c, t: (c, 0, 0))] * 2,
        out_shape=[_sds((N_CHIPS, D_MODEL, W_IN_SHARD), F32), _sds((N_CHIPS, D_MODEL, W_IN_SHARD), BF16)],
        compiler_params=_params(("parallel", "arbitrary"), 56),
    )(*_hbm(ht, *pieces))


ELEMENTWISE_TILE_BYTES = MIB


def _row_tile(rows, cols):
    if rows * cols * 4 <= ELEMENTWISE_TILE_BYTES:
        return rows
    for t in (512, 256, 128, 64, 32, 16, 8):
        if rows % t == 0 and t * cols * 4 <= ELEMENTWISE_TILE_BYTES:
            return t
    return rows


def _chip_sum(ps, gots, chip_core, name):
    n = len(ps)
    h, C = ps[0].shape
    tr = _row_tile(h, C * n)
    nt = h // tr

    def body(jc_ref, *refs):
        for a in range(n):
            p_ref, g0_ref, g1_ref, g2_ref, o_ref = refs[a], refs[n + 3 * a], refs[n + 3 * a + 1], refs[n + 3 * a + 2], \
                refs[4 * n + a]
            o_ref[...] = ((p_ref[...] + g0_ref[...].astype(F32)) + g1_ref[...].astype(F32)) + g2_ref[...].astype(F32)

    rel = lambda r: pl.BlockSpec((None, tr, C), lambda i, jc_ref: (r, i, 0))
    outs = pl.pallas_call(
        body,
        name=name,
        grid_spec=pltpu.PrefetchScalarGridSpec(
            num_scalar_prefetch=1,
            grid=(nt,),
            in_specs=[pl.BlockSpec((tr, C), lambda i, jc_ref: (i, 0))] * n + [rel(0), rel(1), rel(2)] * n,
            out_specs=[pl.BlockSpec((tr, C), lambda i, jc_ref: (jc_ref[1] * nt + i, 0))] * n,
        ),
        out_shape=[_sds((2 * h, C), F32)] * n,
        compiler_params=_params(("parallel",), 48),
    )(chip_core, *_hbm(*ps, *[g for got in gots for g in (got, got, got)]))
    return list(outs)


def _place_shards(shards, chip, name):
    n = len(shards)
    tiles = [_row_tile(s.shape[0], s.shape[1]) for s in shards]
    steps = max(s.shape[0] // t for s, t in zip(shards, tiles))
    tiles = [s.shape[0] // steps for s in shards]

    def body(j_ref, *refs):
        for a in range(n):
            refs[n + a][...] = refs[a][...].astype(BF16)

    return pl.pallas_call(
        body,
        name=name,
        grid_spec=pltpu.PrefetchScalarGridSpec(
            num_scalar_prefetch=1,
            grid=(steps,),
            in_specs=[pl.BlockSpec((t, s.shape[1]), lambda i, j_ref: (i, 0)) for s, t in zip(shards, tiles)],
            out_specs=[pl.BlockSpec((None, t, s.shape[1]), lambda i, j_ref: (j_ref[0], i, 0))
                       for s, t in zip(shards, tiles)],
        ),
        out_shape=[_sds((N_CHIPS,) + s.shape, BF16) for s in shards],
        compiler_params=_params(("parallel",), 48),
    )(chip, *_hbm(*shards))


def _adamw(params, name, with_grads):
    n = len(params)
    k = 4 if with_grads else 3
    R, C = params[0][0].shape
    tr = _row_tile(R, C * n)
    c1 = 1.0 - ADAM_B1 ** ADAM_STEP
    c2 = 1.0 - ADAM_B2 ** ADAM_STEP

    def body(*refs):
        for a in range(n):
            w_ref, g_ref, m_ref, v_ref = refs[4 * a:4 * a + 4]
            d_ref, nm_ref, nv_ref = refs[4 * n + k * a:4 * n + k * a + 3]
            g = g_ref[...]
            nm = ADAM_B1 * m_ref[...] + (1.0 - ADAM_B1) * g
            nv = ADAM_B2 * v_ref[...] + (1.0 - ADAM_B2) * (g * g)
            nm_ref[...] = nm
            nv_ref[...] = nv
            d_ref[...] = (-ADAM_LR) * ((nm / c1) / (jnp.sqrt(nv / c2) + ADAM_EPS) + ADAM_WD * w_ref[...])
            if with_grads:
                refs[4 * n + k * a + 3][...] = g

    spec = pl.BlockSpec((tr, C), lambda i: (i, 0))
    outs = pl.pallas_call(
        body, name=name, grid=(R // tr,), in_specs=[spec] * (4 * n), out_specs=[spec] * (k * n),
        out_shape=[_sds((R, C), F32)] * (k * n), compiler_params=_params(("parallel",), 48),
    )(*_hbm(*[t for p in params for t in p]))
    return [tuple(outs[k * a:k * a + k]) for a in range(n)]


def _place():
    return lax.axis_index("x"), lax.axis_index("y"), lax.axis_index("c")


def _chip_of(x, y, r):
    return (x ^ (r >> 1), y ^ (r & 1))


ANY = pl.BlockSpec(memory_space=pl.ANY)


def _gather_weights(placed, cw8):
    nbig = len(placed)
    halves = [s.shape[1] // 2 for s in placed]
    pieces = [max(1, h // 64) for h in halves]
    rows = [h // p for h, p in zip(halves, pieces)]
    order = [(a, q) for q in range(max(pieces)) for a in range(nbig) if q < pieces[a]]
    ici_sem = {(a, q, r): 3 * i + (r - 1) for i, (a, q) in enumerate(order) for r in (1, 2, 3)}
    cw_sem = {r: 3 * len(order) + (r - 1) for r in (1, 2, 3)}
    d2d_sem = {key: 3 * len(order) + 3 + k for key, k in ici_sem.items()}
    nsem = 6 * len(order) + 3

    def body(*refs):
        cw_ref, dsts, gcw_ref = refs[nbig], refs[nbig + 1:2 * nbig + 1], refs[2 * nbig + 1]
        send_sems, recv_sems = refs[2 * nbig + 2:]
        x, y, c = _place()
        j = 2 * x + y

        def piece_rows(a, q, core):
            return pl.ds(pl.multiple_of(core * halves[a] + q * rows[a], 16), rows[a])

        def ici(a, q, r):
            tx, ty = _chip_of(x, y, r)
            k = ici_sem[(a, q, r)]
            region = dsts[a].at[j, piece_rows(a, q, c), :]
            return pltpu.make_async_remote_copy(
                src_ref=region, dst_ref=region, send_sem=send_sems.at[k], recv_sem=recv_sems.at[k],
                device_id=(tx, ty, c), device_id_type=MESH)

        def ici_landed(a, q, r):
            tx, ty = _chip_of(x, y, r)
            k = ici_sem[(a, q, r)]
            region = dsts[a].at[2 * tx + ty, piece_rows(a, q, c), :]
            return pltpu.make_async_remote_copy(
                src_ref=region, dst_ref=region, send_sem=send_sems.at[k], recv_sem=recv_sems.at[k],
                device_id=(tx, ty, c), device_id_type=MESH)

        def d2d(a, q, r, core):
            tx, ty = _chip_of(x, y, r)
            k = d2d_sem[(a, q, r)]
            region = dsts[a].at[2 * tx + ty, piece_rows(a, q, core), :]
            return pltpu.make_async_remote_copy(
                src_ref=region, dst_ref=region, send_sem=send_sems.at[k], recv_sem=recv_sems.at[k],
                device_id=(x, y, 1 - c), device_id_type=MESH)

        def cw_copy(r):
            tx, ty = _chip_of(x, y, r)
            k = cw_sem[r]
            return pltpu.make_async_remote_copy(
                src_ref=cw_ref, dst_ref=gcw_ref.at[j], send_sem=send_sems.at[k], recv_sem=recv_sems.at[k],
                device_id=(tx, ty, c), device_id_type=MESH)

        def cw_landed(r):
            tx, ty = _chip_of(x, y, r)
            k = cw_sem[r]
            region = gcw_ref.at[2 * tx + ty]
            return pltpu.make_async_remote_copy(
                src_ref=region, dst_ref=region, send_sem=send_sems.at[k], recv_sem=recv_sems.at[k],
                device_id=(tx, ty, c), device_id_type=MESH)

        def relay(a, q, origin, to):
            ox, oy = _chip_of(x, y, origin)
            tx, ty = _chip_of(x, y, to)
            k = ici_sem[(a, q, 3)]
            region = dsts[a].at[2 * ox + oy, piece_rows(a, q, c), :]
            return pltpu.make_async_remote_copy(
                src_ref=region, dst_ref=region, send_sem=send_sems.at[k], recv_sem=recv_sems.at[k],
                device_id=(tx, ty, c), device_id_type=MESH)

        first = [ici(a, q, r) for (a, q) in order for r in (1, 2)] + [cw_copy(r) for r in (1, 2, 3)]
        for cp in first:
            cp.start()
        passed = []
        for (a, q) in order:
            for r in (1, 2):
                ici_landed(a, q, r).wait_recv()
                if q % 2 == r - 1:
                    cp = relay(a, q, r, 3 - r)
                    cp.start()
                    passed.append(cp)
                cp = d2d(a, q, r, c)
                cp.start()
                passed.append(cp)
        for (a, q) in order:
            ici_landed(a, q, 3).wait_recv()
            cp = d2d(a, q, 3, c)
            cp.start()
            passed.append(cp)
        for r in (1, 2, 3):
            cw_landed(r).wait_recv()
        for (a, q) in order:
            for r in (1, 2, 3):
                d2d(a, q, r, 1 - c).wait_recv()
        for cp in first + passed:
            cp.wait_send()

    return pl.pallas_call(
        body,
        name="gather_weights",
        in_specs=[ANY] * (nbig + 1),
        out_specs=[ANY] * (nbig + 1),
        out_shape=[_sds(s.shape, s.dtype) for s in placed] + [_sds((N_CHIPS,) + cw8.shape, cw8.dtype)],
        input_output_aliases={a: a for a in range(nbig)},
        scratch_shapes=[pltpu.SemaphoreType.DMA((nsem,)), pltpu.SemaphoreType.DMA((nsem,))],
    )(*placed, cw8)


def _gather_late_start(placed, after, name):
    n = len(placed)
    halves = [s.shape[1] // 2 for s in placed]

    def body(*refs):
        g_refs = refs[0:n]
        send_sems, recv_sems, token = refs[n + 1], refs[n + 2], refs[-1]
        x, y, c = _place()
        j = 2 * x + y
        for a in range(n):
            mine = g_refs[a].at[j, pl.ds(pl.multiple_of(c * halves[a], 16), halves[a]), :]
            for r in (1, 2, 3):
                tx, ty = _chip_of(x, y, r)
                for to_core in (0, 1):
                    k = ((a * 3 + (r - 1)) * 2 + c) * 2 + to_core
                    pltpu.make_async_remote_copy(
                        src_ref=mine, dst_ref=mine, send_sem=send_sems.at[k], recv_sem=recv_sems.at[k],
                        device_id=(tx, ty, to_core), device_id_type=MESH).start()
        token[...] = jnp.zeros_like(token)

    hbm = lambda t: pltpu.HBM(t.shape, t.dtype)
    keep = lambda t: pltpu.with_memory_space_constraint(t, pltpu.HBM)
    nsem = 12 * n
    outs = pl.pallas_call(
        body,
        name=name,
        in_specs=[HBM] * n + [ANY],
        out_specs=(SEM, SEM, *[HBM] * n, pl.BlockSpec(memory_space=pltpu.VMEM)),
        out_shape=(pltpu.SemaphoreType.DMA((nsem,)), pltpu.SemaphoreType.DMA((nsem,)), *[hbm(p) for p in placed],
                   jax.ShapeDtypeStruct((8, 128), F32)),
        input_output_aliases={i: 2 + i for i in range(n)},
        compiler_params=pltpu.CompilerParams(has_side_effects=DATAFLOW),
    )(*[keep(p) for p in placed], after)
    return outs[0], outs[1], list(outs[2:2 + n]), outs[-1]


def _gather_late_wait(send_sems, recv_sems, thru, after, name):
    n = len(thru)
    halves = [s.shape[1] // 2 for s in thru]

    def body(*refs):
        g_refs = refs[0:n]
        send_sems, recv_sems = refs[n], refs[n + 1]
        x, y, c = _place()
        j = 2 * x + y
        for a in range(n):
            mine = g_refs[a].at[j, pl.ds(pl.multiple_of(c * halves[a], 16), halves[a]), :]
            for r in (1, 2, 3):
                tx, ty = _chip_of(x, y, r)
                for other in (0, 1):
                    k_out = ((a * 3 + (r - 1)) * 2 + c) * 2 + other
                    pltpu.make_async_remote_copy(
                        src_ref=mine, dst_ref=mine, send_sem=send_sems.at[k_out], recv_sem=recv_sems.at[k_out],
                        device_id=(tx, ty, other), device_id_type=MESH).wait_send()
                    k_in = ((a * 3 + (r - 1)) * 2 + other) * 2 + c
                    theirs = g_refs[a].at[2 * tx + ty, pl.ds(other * halves[a], halves[a]), :]
                    pltpu.make_async_remote_copy(
                        src_ref=theirs, dst_ref=theirs, send_sem=send_sems.at[k_in], recv_sem=recv_sems.at[k_in],
                        device_id=(tx, ty, other), device_id_type=MESH).wait_recv()

    hbm = lambda t: pltpu.HBM(t.shape, t.dtype)
    outs = pl.pallas_call(
        body,
        name=name,
        in_specs=[HBM] * n + [SEM, SEM, ANY],
        out_specs=[HBM] * n,
        out_shape=[hbm(t) for t in thru],
        input_output_aliases={i: i for i in range(n)},
        compiler_params=pltpu.CompilerParams(has_side_effects=DATAFLOW),
    )(*thru, send_sems, recv_sems, after)
    return list(outs)


D2D_PIECE_ROWS = 64


def _pair_sum(gs, gbs, chip_core, name):
    n = len(gs)
    nch, R, C = gs[0].shape
    h = R // 2
    tr = _row_tile(h, C * n)
    nt = h // tr

    def body(jc_ref, *refs):
        g_refs, gb_refs, p_refs, pb_refs = refs[0:n], refs[n:2 * n], refs[2 * n:3 * n], refs[3 * n:4 * n]
        got_refs, send_sems, recv_sems = refs[4 * n:5 * n], refs[5 * n], refs[5 * n + 1]
        i, j = pl.program_id(0), pl.program_id(1)
        x, y, c = _place()

        def copy(a, ti, tj):
            src_rows = pl.ds(pl.multiple_of((1 - c) * h + ti * tr, 16), tr)
            dst_rows = pl.ds(pl.multiple_of(ti * tr, 16), tr)
            return pltpu.make_async_remote_copy(
                src_ref=gb_refs[a].at[tj, src_rows, :], dst_ref=got_refs[a].at[tj, dst_rows, :],
                send_sem=send_sems.at[a, ti, tj], recv_sem=recv_sems.at[a, ti, tj],
                device_id=(x, y, 1 - c), device_id_type=MESH)

        @pl.when((i == 0) & (j == 0))
        def _():
            for ti in range(nt):
                for tj in range(nch):
                    for a in range(n):
                        copy(a, ti, tj).start()

        for a in range(n):
            copy(a, i, j).wait()
            s = g_refs[a][...] + got_refs[a][j, pl.ds(pl.multiple_of(i * tr, 16), tr), :].astype(F32)
            pb_refs[a][...] = s.astype(BF16)

            @pl.when(j == jc_ref[0])
            def _():
                p_refs[a][...] = s

    by_chip = pl.BlockSpec((None, tr, C), lambda i, j, jc_ref: (j, i, 0))
    outs = pl.pallas_call(
        body,
        name=name,
        grid_spec=pltpu.PrefetchScalarGridSpec(
            num_scalar_prefetch=1,
            grid=(nt, nch),
            in_specs=[pl.BlockSpec((None, tr, C), lambda i, j, jc_ref: (j, jc_ref[1] * nt + i, 0))] * n + [ANY] * n,
            out_specs=[pl.BlockSpec((tr, C), lambda i, j, jc_ref: (i, 0))] * n + [by_chip] * n,
            scratch_shapes=[pltpu.VMEM((nch, h, C), BF16)] * n + [pltpu.SemaphoreType.DMA((n, nt, nch))] * 2,
        ),
        out_shape=[_sds((h, C), F32)] * n + [_sds((nch, h, C), BF16)] * n,
        compiler_params=_params(("arbitrary", "arbitrary"), 48),
    )(chip_core, *_hbm(*gs, *gbs))
    return list(outs[:n]), list(outs[n:])


HBM = pl.BlockSpec(memory_space=pltpu.HBM)
SEM = pl.BlockSpec(memory_space=pltpu.SEMAPHORE)
DATAFLOW = pltpu.SideEffectType.DATAFLOW_SIDE_EFFECTING


def _chip_copy(p_refs, land_refs, send_sems, recv_sems, a, r, blocked):
    x, y, c = _place()
    tx, ty = _chip_of(x, y, r)
    k = a * 3 + (r - 1)
    return pltpu.make_async_remote_copy(
        src_ref=p_refs[a].at[2 * tx + ty] if blocked else p_refs[a], dst_ref=land_refs[a].at[r - 1],
        send_sem=send_sems.at[k], recv_sem=recv_sems.at[k], device_id=(tx, ty, c), device_id_type=MESH)


def _chip_exchange_start(psums, name, blocked=True):
    n = len(psums)
    lands = [lax.empty((3,) + (p.shape[1:] if blocked else p.shape), p.dtype) for p in psums]

    def body(*refs):
        p_refs, land_refs = refs[0:n], refs[n:2 * n]
        send_sems, recv_sems, token = refs[2 * n], refs[2 * n + 1], refs[-1]
        for a in range(n):
            for r in (1, 2, 3):
                _chip_copy(p_refs, land_refs, send_sems, recv_sems, a, r, blocked).start()
        token[...] = jnp.zeros_like(token)

    hbm = lambda t: pltpu.HBM(t.shape, t.dtype)
    keep = lambda t: pltpu.with_memory_space_constraint(t, pltpu.HBM)
    outs = pl.pallas_call(
        body,
        name=name,
        in_specs=[HBM] * (2 * n),
        out_specs=(SEM, SEM, *[HBM] * (2 * n), pl.BlockSpec(memory_space=pltpu.VMEM)),
        out_shape=(pltpu.SemaphoreType.DMA((3 * n,)), pltpu.SemaphoreType.DMA((3 * n,)),
                   *[hbm(p) for p in psums], *[hbm(l) for l in lands], _sds((8, 128), F32)),
        input_output_aliases={i: 2 + i for i in range(2 * n)},
        compiler_params=pltpu.CompilerParams(has_side_effects=DATAFLOW),
    )(*[keep(p) for p in psums], *[keep(l) for l in lands])
    return outs[0], outs[1], list(outs[2:2 + n]), list(outs[2 + n:2 + 2 * n]), outs[-1]


def _chip_exchange_wait(send_sems, recv_sems, p_thru, land_thru, after, name, blocked=True):
    n = len(p_thru)

    def body(*refs):
        p_refs, land_refs = refs[0:n], refs[n:2 * n]
        send_sems, recv_sems = refs[2 * n], refs[2 * n + 1]
        for a in range(n):
            for r in (1, 2, 3):
                copy = _chip_copy(p_refs, land_refs, send_sems, recv_sems, a, r, blocked)
                copy.wait_send()
                copy.wait_recv()

    hbm = lambda t: pltpu.HBM(t.shape, t.dtype)
    outs = pl.pallas_call(
        body,
        name=name,
        in_specs=[HBM] * (2 * n) + [SEM, SEM, ANY],
        out_specs=[HBM] * (2 * n),
        out_shape=[hbm(p) for p in p_thru] + [hbm(l) for l in land_thru],
        input_output_aliases={i: i for i in range(2 * n)},
        compiler_params=pltpu.CompilerParams(has_side_effects=DATAFLOW),
    )(*p_thru, *land_thru, send_sems, recv_sems, after)
    return list(outs[0:n]), list(outs[n:2 * n])


def _pair_share(fulls):
    n = len(fulls)
    halves = [f.shape[0] // 2 for f in fulls]

    def body(*refs):
        full_refs = refs[n:2 * n]
        send_sems, recv_sems = refs[2 * n:]
        x, y, c = _place()

        def half_of(a, core):
            return full_refs[a].at[pl.ds(pl.multiple_of(core * halves[a], 8), halves[a]), :]

        def remote(a, src, dst):
            return pltpu.make_async_remote_copy(
                src_ref=src, dst_ref=dst, send_sem=send_sems.at[a], recv_sem=recv_sems.at[a],
                device_id=(x, y, 1 - c), device_id_type=MESH)

        for a in range(n):
            for q in range(halves[a] // D2D_PIECE_ROWS):
                piece = full_refs[a].at[
                    pl.ds(pl.multiple_of(c * halves[a] + q * D2D_PIECE_ROWS, 8), D2D_PIECE_ROWS), :]
                remote(a, piece, piece).start()
        for a in range(n):
            remote(a, half_of(a, c), half_of(a, c)).wait_send()
            remote(a, half_of(a, 1 - c), half_of(a, 1 - c)).wait_recv()

    return pl.pallas_call(
        body,
        name="pair_share",
        in_specs=[ANY] * n,
        out_specs=[ANY] * n,
        out_shape=[_sds(f.shape, F32) for f in fulls],
        input_output_aliases={a: a for a in range(n)},
        scratch_shapes=[pltpu.SemaphoreType.DMA((n,)), pltpu.SemaphoreType.DMA((n,))],
    )(*fulls)


def _small_pair_sum(s):
    R, C = s.shape
    V = SMALL_VECTOR_ROWS

    def body(s_ref, v_ref, m_ref, sib, send_sem, recv_sem):
        x, y, c = _place()

        def to_sib(src, dst):
            return pltpu.make_async_remote_copy(
                src_ref=src, dst_ref=dst, send_sem=send_sem, recv_sem=recv_sem,
                device_id=(x, y, 1 - c), device_id_type=MESH)

        for q in range(R // 8):
            to_sib(s_ref.at[pl.ds(8 * q, 8), :], sib.at[pl.ds(8 * q, 8), :]).start()
        to_sib(s_ref, sib).wait()
        v_ref[...] = s_ref[pl.ds(0, V), :] + sib[pl.ds(0, V), :]
        m_ref[...] = (s_ref[pl.ds(V, R - V), :] + sib[pl.ds(V, R - V), :]).astype(BF16)

    return pl.pallas_call(
        body,
        name="small_pair_sum",
        in_specs=[pl.BlockSpec(memory_space=pltpu.VMEM)],
        out_specs=[pl.BlockSpec(memory_space=pltpu.VMEM)] * 2,
        out_shape=[jax.ShapeDtypeStruct((V, C), F32), jax.ShapeDtypeStruct((R - V, C), BF16)],
        scratch_shapes=[pltpu.VMEM((R, C), F32), pltpu.SemaphoreType.DMA, pltpu.SemaphoreType.DMA],
    )(s)


def _small_total(chip, own, landed):
    V, C = own[0].shape
    M = own[1].shape[0]

    def body(j_ref, v_ref, m_ref, lv_ref, lm_ref, o_ref, chips_v, chips_m):
        j = j_ref[0]
        chips_v[j] = v_ref[...]
        chips_m[j] = m_ref[...]
        for r in (1, 2, 3):
            chips_v[j ^ r] = lv_ref[r - 1]
            chips_m[j ^ r] = lm_ref[r - 1]
        o_ref[pl.ds(0, V), :] = (chips_v[0] + chips_v[1]) + (chips_v[2] + chips_v[3])
        o_ref[pl.ds(V, M), :] = (chips_m[0].astype(F32) + chips_m[1].astype(F32)) + (
            chips_m[2].astype(F32) + chips_m[3].astype(F32))

    vmem = pl.BlockSpec(memory_space=pltpu.VMEM)
    return pl.pallas_call(
        body,
        name="small_total",
        in_specs=[pl.BlockSpec(memory_space=pltpu.SMEM), vmem, vmem, vmem, vmem],
        out_specs=vmem,
        out_shape=jax.ShapeDtypeStruct((V + M, C), F32),
        scratch_shapes=[pltpu.VMEM((N_CHIPS, V, C), F32), pltpu.VMEM((N_CHIPS, M, C), BF16)],
    )(chip, own[0], own[1], landed[0], landed[1])


def _block_diag(w):
    w4 = w.reshape(4, 4, RNN_BLOCK_W, RNN_BLOCK_W)
    eye = jnp.eye(4, dtype=w.dtype)
    return jnp.einsum("jaik,ab->jaibk", w4, eye).reshape(4, RNN_TILE, RNN_TILE)


def _block_diag_part(d):
    d5 = d.reshape(4, 4, RNN_BLOCK_W, 4, RNN_BLOCK_W)
    return jnp.stack([d5[:, a, :, a, :] for a in range(4)], axis=1).reshape(RNN_BLOCKS, RNN_BLOCK_W, RNN_BLOCK_W)


def _local_grads(x, target, g_pre, w_in_g, b_gate, conv_w, conv_b, w_rg_a, b_rg_a, w_rg_x, b_rg_x, lam, sinks,
                 out_weights, fwd_token, g_post, on_out_grads, on_w_in_grad):
    wa_bd = _block_diag(w_rg_a).astype(BF16)
    wx_bd = _block_diag(w_rg_x).astype(BF16)
    b_a = b_rg_a.reshape(1, D_RNN)
    b_x = b_rg_x.reshape(1, D_RNN)

    proj, ht = _proj_fwd(x, g_pre, w_in_g)
    y_rnn, z_rnn, conv, z_rnn_t = _rnn_fwd(proj, conv_w, conv_b, wa_bd, wx_bd, b_a, b_x, lam, fwd_token)
    bias = _attn_bias()
    y_attn, z_attn, lse = _attn_fwd(proj, sinks, bias)
    w_rnn_out, w_attn_out, w_out = out_weights(z_attn)
    dyx, dz_rnn, dz_attn, dml, dout, dbr_rnn, dbr_attn, merged_t, z_attn_t, head_small = _head(
        x, target, z_rnn, z_attn, proj, b_gate, g_post, w_rnn_out, w_attn_out, w_out)
    out_grads = [_matmul_t(z_rnn_t, dbr_rnn, "dw_rnn_out"), _matmul_t(z_attn_t, dbr_attn, "dw_attn_out"),
                 _matmul_t(merged_t, dout, "dw_out")]
    shard_rows = lambda d: d.reshape(N_CHIPS, OUT_SHARD, D_MODEL)
    token = on_out_grads([shard_rows(g) for g, _ in out_grads], [shard_rows(gb) for _, gb in out_grads])
    dq, dk, dv, dag, attn_small = _attn_bwd(proj, y_attn, lse, dz_attn, sinks, bias, token)
    drx, drg, dwa_t, dwx_t, rnn_small = _rnn_bwd(proj, conv, y_rnn, dz_rnn, conv_w, wa_bd, wx_bd, b_a, b_x, lam)
    dproj = [drx, drg, dq, dk, dv, dag, dml]
    token = on_w_in_grad(*_dw_in(ht, dproj))
    grad_x, dh_small = _dh_bwd(dproj, w_in_g, x, dyx, g_pre, token)
    small = jnp.concatenate([rnn_small, head_small, dh_small + attn_small,
                             _block_diag_part(dwa_t).reshape(64, 1024), _block_diag_part(dwx_t).reshape(64, 1024)], axis=0)
    return grad_x, small


ROW_LOSS = 11


def _rows8(parts):
    out = None
    for r, a in parts:
        p = jnp.pad(a, ((r, 8 - r - a.shape[0]), (0, 1024 - a.shape[1])))
        out = p if out is None else out + p
    return out


def _pack_small(p):
    g0 = _rows8([(0, p["b_rg_a"].reshape(1, 1024)), (1, p["b_rg_x"].reshape(1, 1024)), (2, p["lru_lambda"]),
                 (3, p["conv_b"]), (4, p["conv_w"][0])])
    g1 = _rows8([(0, p["post_norm_g"]), (1, p["b_gate"].reshape(2, 1024))])
    g2 = _rows8([(0, p["pre_norm_g"]), (1, p["attn_sinks"])])
    return jnp.concatenate([g0, g1, g2, p["w_rg_a"].reshape(64, 1024), p["w_rg_x"].reshape(64, 1024)], axis=0)


def _unpack_small(s, conv_cols):
    return {
        "b_rg_a": s[0:1].reshape(1, 16, 64), "b_rg_x": s[1:2].reshape(1, 16, 64), "lru_lambda": s[2:3],
        "conv_b": s[3:4], "conv_w": s[4:8, 0:conv_cols].reshape(1, CONV_W, conv_cols),
        "post_norm_g": s[8:9], "b_gate": s[9:11].reshape(1, 2048),
        "pre_norm_g": s[16:17], "attn_sinks": s[17:18, 0:N_Q_HEADS],
        "w_rg_a": s[24:88].reshape(1, 16, 64, 64), "w_rg_x": s[88:152].reshape(1, 16, 64, 64),
    }


WEIGHTS = ["pre_norm_g", "w_in", "b_gate", "conv_w", "conv_b", "w_rg_a", "b_rg_a", "w_rg_x", "b_rg_x", "lru_lambda",
           "attn_sinks", "w_rnn_out", "w_attn_out", "w_out", "post_norm_g"]
BIG = ["w_in", "w_rnn_out", "w_attn_out", "w_out"]


def kernel(x, pre_norm_g, w_in, b_gate, conv_w, conv_b, w_rg_a, b_rg_a, w_rg_x, b_rg_x, lru_lambda, attn_sinks, w_rnn_out, w_attn_out, w_out, post_norm_g, loss_target, m_pre_norm_g, m_w_in, m_b_gate, m_conv_w, m_conv_b, m_w_rg_a, m_b_rg_a, m_w_rg_x, m_b_rg_x, m_lru_lambda, m_attn_sinks, m_w_rnn_out, m_w_attn_out, m_w_out, m_post_norm_g, v_pre_norm_g, v_w_in, v_b_gate, v_conv_w, v_conv_b, v_w_rg_a, v_b_rg_a, v_w_rg_x, v_b_rg_x, v_lru_lambda, v_attn_sinks, v_w_rnn_out, v_w_attn_out, v_w_out, v_post_norm_g):
    w = dict(pre_norm_g=pre_norm_g, w_in=w_in, b_gate=b_gate, conv_w=conv_w, conv_b=conv_b, w_rg_a=w_rg_a,
             b_rg_a=b_rg_a, w_rg_x=w_rg_x, b_rg_x=b_rg_x, lru_lambda=lru_lambda, attn_sinks=attn_sinks,
             w_rnn_out=w_rnn_out, w_attn_out=w_attn_out, w_out=w_out, post_norm_g=post_norm_g)
    m = dict(pre_norm_g=m_pre_norm_g, w_in=m_w_in, b_gate=m_b_gate, conv_w=m_conv_w, conv_b=m_conv_b, w_rg_a=m_w_rg_a,
             b_rg_a=m_b_rg_a, w_rg_x=m_w_rg_x, b_rg_x=m_b_rg_x, lru_lambda=m_lru_lambda, attn_sinks=m_attn_sinks,
             w_rnn_out=m_w_rnn_out, w_attn_out=m_w_attn_out, w_out=m_w_out, post_norm_g=m_post_norm_g)
    v = dict(pre_norm_g=v_pre_norm_g, w_in=v_w_in, b_gate=v_b_gate, conv_w=v_conv_w, conv_b=v_conv_b, w_rg_a=v_w_rg_a,
             b_rg_a=v_b_rg_a, w_rg_x=v_w_rg_x, b_rg_x=v_b_rg_x, lru_lambda=v_lru_lambda, attn_sinks=v_attn_sinks,
             w_rnn_out=v_w_rnn_out, w_attn_out=v_w_attn_out, w_out=v_w_out, post_norm_g=v_post_norm_g)
    chip = 2 * lax.axis_index("x") + lax.axis_index("y")

    chip_idx = chip.astype(jnp.int32).reshape(1)
    chip_core = jnp.stack([chip, lax.axis_index("c")]).astype(jnp.int32)
    cw8 = jnp.pad(conv_w[0], ((0, 8 - CONV_W), (0, 0)))
    placed = _place_shards([w_in[0], w_rnn_out[0], w_attn_out[0], w_out[0]], chip_idx, "place_shards")
    win_g, cw_g = _gather_weights(placed[:1], cw8)
    late_send, late_recv, late_thru, late_token = _gather_late_start(placed[1:], win_g, "gather_late_start")
    cw_g = lax.dynamic_update_slice_in_dim(cw_g, cw8[None], chip, axis=0)
    conv_w_full = jnp.transpose(cw_g[:, 0:CONV_W, :], (1, 0, 2)).reshape(CONV_W, D_RNN)

    started = {}

    def start_reduction(tag, grads, grads_b16):
        psums, psums_b16 = _pair_sum(grads, grads_b16, chip_core, "pair_sum_" + tag)
        send_sems, recv_sems, p_thru, land_thru, token = _chip_exchange_start(psums_b16, "chip_exchange_start_" + tag)
        started[tag] = (psums, send_sems, recv_sems, p_thru, land_thru)
        return token

    def end_reduction(tag, after):
        psums, send_sems, recv_sems, p_thru, land_thru = started[tag]
        _, landed = _chip_exchange_wait(send_sems, recv_sems, p_thru, land_thru, after, "chip_exchange_wait_" + tag)
        return _chip_sum(psums, landed, chip_core, "chip_sum_" + tag)

    def out_weights(after):
        gathered = _gather_late_wait(late_send, late_recv, late_thru, after, "gather_late_wait")
        return [g.reshape(D_MODEL, D_MODEL) for g in gathered]

    grad_x, small = _local_grads(
        x[0], loss_target[0], pre_norm_g, win_g, b_gate, conv_w_full, conv_b, w_rg_a[0], b_rg_a[0], w_rg_x[0],
        b_rg_x[0], lru_lambda, attn_sinks[0], out_weights, late_token, post_norm_g,
        on_out_grads=lambda grads, grads_b16: start_reduction("out", grads, grads_b16),
        on_w_in_grad=lambda grad, grad_b16: start_reduction("in", [grad], [grad_b16]))

    small_chip = _small_pair_sum(small)
    small_send, small_recv, small_thru, small_land, small_token = _chip_exchange_start(
        list(small_chip), "small_exchange_start", blocked=False)

    halves = end_reduction("in", small_token) + end_reduction("out", small_token)
    gbig = dict(zip(BIG, _pair_share(halves)))

    grads, delta, new_m, new_v = {}, {}, {}, {}
    for names, tag in ((BIG[:1], "adamw_in"), (BIG[1:], "adamw_out")):
        updates = _adamw([(w[n][0], gbig[n], m[n][0], v[n][0]) for n in names], tag, with_grads=True)
        for n, (d, nm, nv, g) in zip(names, updates):
            grads[n], delta[n], new_m[n], new_v[n] = g[None], d[None], nm[None], nv[None]

    small_own, small_landed = _chip_exchange_wait(small_send, small_recv, small_thru, small_land, delta[BIG[-1]],
                                                  "small_exchange_wait", blocked=False)
    small_sum = _small_total(chip_idx, small_own, small_landed)
    total_loss = small_sum[ROW_LOSS, 0]
    gsmall = _unpack_small(small_sum, D_RNN)
    conv_shard = D_RNN // N_CHIPS
    gsmall["conv_w"] = lax.dynamic_slice_in_dim(gsmall["conv_w"], chip * conv_shard, conv_shard, axis=2)
    pick = lambda t: {k: t[k] for k in gsmall}
    (d, nm, nv), = _adamw([(_pack_small(pick(w)), _pack_small(gsmall), _pack_small(pick(m)), _pack_small(pick(v)))],
                          "adamw_small", with_grads=False)
    ud, um, uv = _unpack_small(d, conv_shard), _unpack_small(nm, conv_shard), _unpack_small(nv, conv_shard)
    for n in gsmall:
        grads[n] = gsmall[n].reshape(w[n].shape)
        delta[n] = ud[n].reshape(w[n].shape)
        new_m[n] = um[n].reshape(w[n].shape)
        new_v[n] = uv[n].reshape(w[n].shape)

    return (total_loss, grad_x[None], *[grads[n] for n in WEIGHTS], *[delta[n] for n in WEIGHTS],
            *[new_m[n] for n in WEIGHTS], *[new_v[n] for n in WEIGHTS])
```

```python
import functools
import math

import jax
import jax.numpy as jnp
from jax import lax
from jax.experimental import pallas as pl
from jax.experimental.pallas import tpu as pltpu

F32 = jnp.float32
BF16 = jnp.bfloat16

D_MODEL = 1024
D_RNN = 1024
RNN_BLOCKS = 16
RNN_BLOCK_W = 64
CONV_W = 4
LRU_C = 8.0
N_Q_HEADS = 16
N_KV_HEADS = 4
GROUP = 4
HEAD_DIM = 64
D_KV = 256
BLOCK = 128
ALIBI_MAX_BIAS = 8.0
EPS = 1e-6
D_IN = 6656
N_CHIPS = 4
W_IN_SHARD = D_IN // N_CHIPS
OUT_SHARD = D_MODEL // N_CHIPS
ADAM_LR = 0.001
ADAM_B1 = 0.9
ADAM_B2 = 0.999
ADAM_EPS = 1e-08
ADAM_WD = 0.01
ADAM_STEP = 10
NEG_BIG = -1e30
MIB = 1 << 20

COL_RNN_X = 0
COL_RNN_GATE = 4
COL_Q = 8
COL_K = 12
COL_V = 13
COL_ATTN_GATE = 14
COL_MERGE = 18

RNN_TILE = 256
RNN_CHUNK = 512
SMALL_ROWS = 152
SMALL_VECTOR_ROWS = 24
MESH = pl.DeviceIdType.MESH


def _sds(shape, dtype):
    return pltpu.HBM(shape, dtype)


def _params(sem=None, vmem_mib=None):
    kw = {}
    if sem is not None:
        kw["dimension_semantics"] = sem
    if vmem_mib is not None:
        kw["vmem_limit_bytes"] = vmem_mib * MIB
    return pltpu.CompilerParams(**kw)


def _hbm(*arrays):
    return [pltpu.with_memory_space_constraint(a, pltpu.HBM) for a in arrays]


def _dot(a, b):
    return jnp.dot(a, b, preferred_element_type=F32)


def _dot_nt(a, b):
    return lax.dot_general(a, b, (((1,), (1,)), ((), ())), preferred_element_type=F32)


def _dot_tn(a, b):
    return lax.dot_general(a, b, (((0,), (0,)), ((), ())), preferred_element_type=F32)


def _sigmoid(x):
    return 0.5 * jnp.tanh(0.5 * x) + 0.5


def _sigmoid_small(x):
    return 1.0 / (1.0 + jnp.exp(-x))


def _softplus(x):
    return jnp.maximum(x, 0.0) + jnp.log(1.0 + jnp.exp(-jnp.abs(x)))


def _one_minus_square(a, log_a):
    return -jnp.tanh(log_a) * (a * a + 1.0)


def _proj_fwd(x, g_pre, w_in_g):
    T = x.shape[0]
    tm = min(1024, T)

    def body(x_ref, g_ref, w_ref, proj_ref, ht_ref, h_s):
        @pl.when(pl.program_id(1) == 0)
        def _():
            xv = x_ref[...]
            rstd = lax.rsqrt(jnp.mean(xv * xv, axis=-1, keepdims=True) + EPS)
            hf = (xv * rstd) * g_ref[...]
            h_s[...] = hf.astype(BF16)
            ht_ref[...] = hf.T.astype(BF16)

        proj_ref[...] = _dot(h_s[...], w_ref[...]).astype(BF16)

    return pl.pallas_call(
        body,
        name="proj_fwd",
        grid=(T // tm, N_CHIPS),
        in_specs=[
            pl.BlockSpec((tm, D_MODEL), lambda i, j: (i, 0)),
            pl.BlockSpec((1, D_MODEL), lambda i, j: (0, 0)),
            pl.BlockSpec((None, D_MODEL, W_IN_SHARD), lambda i, j: (j, 0, 0)),
        ],
        out_specs=[
            pl.BlockSpec((tm, W_IN_SHARD), lambda i, j: (i, j)),
            pl.BlockSpec((D_MODEL, tm), lambda i, j: (0, i)),
        ],
        out_shape=[_sds((T, D_IN), BF16), _sds((D_MODEL, T), BF16)],
        scratch_shapes=[pltpu.VMEM((tm, D_MODEL), BF16)],
        compiler_params=_params(("parallel", "arbitrary"), 48),
    )(*_hbm(x, g_pre, w_in_g))


def _shift_down(x, tail, s, row):
    n = x.shape[0]
    xs = pltpu.roll(x, s, 0)
    tail_t = jnp.tile(pltpu.roll(tail, s, 0), (n // 8, 1))
    return jnp.where(row < s, tail_t, xs)


def _shift_up(x, head, s, row):
    n = x.shape[0]
    xs = pltpu.roll(x, n - s, 0)
    head_t = jnp.tile(pltpu.roll(head, 8 - s, 0), (n // 8, 1))
    return jnp.where(row >= n - s, head_t, xs)


def _conv_taps(x, tail, row):
    return [_shift_down(x, tail, 3, row), _shift_down(x, tail, 2, row), _shift_down(x, tail, 1, row), x]


def _rglru_gates(c, wa, wx, ba, bx, lam):
    cb = c.astype(BF16)
    r = _sigmoid_small(_dot(cb, wa) + ba)
    i = _sigmoid(_dot(cb, wx) + bx)
    log_a = (-LRU_C) * r * _softplus(-lam)
    a = jnp.exp(log_a)
    w = _one_minus_square(a, log_a)
    inv_mult = lax.rsqrt(w)
    return cb, r, i, a, w * inv_mult, inv_mult


SUBLANES = 8


def _scan_down(a, u, row):
    n = a.shape[0]
    s = 1
    while s < SUBLANES:
        a_sh = jnp.where(row >= s, pltpu.roll(a, s, 0), 1.0)
        u_sh = jnp.where(row >= s, pltpu.roll(u, s, 0), 0.0)
        u = a * u_sh + u
        a = a * a_sh
        s *= 2
    while s < n:
        u = jnp.concatenate([u[:s], a[s:] * u[:n - s] + u[s:]], axis=0)
        a = jnp.concatenate([a[:s], a[s:] * a[:n - s]], axis=0)
        s *= 2
    return a, u


def _scan_up(b, u, row):
    n = b.shape[0]
    s = 1
    while s < SUBLANES:
        b_sh = jnp.where(row < n - s, pltpu.roll(b, n - s, 0), 1.0)
        u_sh = jnp.where(row < n - s, pltpu.roll(u, n - s, 0), 0.0)
        u = b * u_sh + u
        b = b * b_sh
        s *= 2
    while s < n:
        u = jnp.concatenate([b[:n - s] * u[s:] + u[:n - s], u[n - s:]], axis=0)
        b = jnp.concatenate([b[:n - s] * b[s:], b[n - s:]], axis=0)
        s *= 2
    return b, u


LANES = 128


def _chunk_scan(a, u, a_s, u_s, hl_s, al_s, carry, reverse):
    n, width = a.shape
    groups = n // SUBLANES
    order = range(SUBLANES - 1, -1, -1) if reverse else range(SUBLANES)
    row = lax.broadcasted_iota(jnp.int32, (groups, LANES), 0)
    for l in range(width // LANES):
        lanes = slice(l * LANES, (l + 1) * LANES)
        a_l, u_l, hl_l, al_l = a_s.at[l], u_s.at[l], hl_s.at[l], al_s.at[l]
        a_l[...] = a[:, lanes]
        u_l[...] = u[:, lanes]
        h_loc = a_loc = None
        for r in order:
            rows = pl.ds(r, groups, stride=SUBLANES)
            a_r, u_r = a_l[rows, :], u_l[rows, :]
            h_loc, a_loc = (u_r, a_r) if h_loc is None else (a_r * h_loc + u_r, a_r * a_loc)
            hl_l[rows, :] = h_loc
            al_l[rows, :] = a_loc
        if reverse:
            a_cum, ends = _scan_up(a_loc, h_loc, row)
            ends = ends + a_cum * carry[:, lanes]
            enters = jnp.where(row == groups - 1, carry[:, lanes], pltpu.roll(ends, groups - 1, 0))
        else:
            a_cum, ends = _scan_down(a_loc, h_loc, row)
            ends = ends + a_cum * carry[:, lanes]
            enters = jnp.where(row == 0, carry[:, lanes], pltpu.roll(ends, 1, 0))
        for r in range(SUBLANES):
            rows = pl.ds(r, groups, stride=SUBLANES)
            hl_l[rows, :] = hl_l[rows, :] + al_l[rows, :] * enters
    return jnp.concatenate([hl_s[l] for l in range(width // LANES)], axis=1)


def _rnn_fwd(proj, conv_w, conv_b, wa_bd, wx_bd, b_a, b_x, lam, token):
    T = proj.shape[0]
    tc, ct = RNN_CHUNK, RNN_TILE
    nt = T // tc

    def body(x_ref, rg_ref, cw_ref, cb_ref, wa_ref, wx_ref, ba_ref, bx_ref, lam_ref, token_ref, h_ref, z_ref, c_ref,
             zt_ref, xtail, hcarry, a_s, u_s, hl_s, al_s):
        @pl.when(pl.program_id(1) == 0)
        def _():
            xtail[...] = jnp.zeros_like(xtail)
            hcarry[...] = jnp.zeros_like(hcarry)

        row = lax.broadcasted_iota(jnp.int32, (tc, ct), 0)
        x = x_ref[...].astype(F32)
        taps = _conv_taps(x, xtail[...], row)
        c = cb_ref[...] + cw_ref[pl.ds(0, 1), :] * taps[0]
        for k in range(1, CONV_W):
            c = c + cw_ref[pl.ds(k, 1), :] * taps[k]
        xtail[...] = x[tc - 8:, :]
        c_ref[...] = c
        _, _, i, a, mult, _ = _rglru_gates(c, wa_ref[...], wx_ref[...], ba_ref[...], bx_ref[...], lam_ref[...])
        h = _chunk_scan(a, mult * (i * c), a_s, u_s, hl_s, al_s, hcarry[...], reverse=False)
        h_ref[...] = h
        hcarry[...] = h_ref[pl.ds(tc - 1, 1), :]
        rg = rg_ref[...].astype(F32)
        z = h * (rg * _sigmoid(rg))
        z_ref[...] = z.astype(BF16)
        zt_ref[...] = z.T.astype(BF16)

    col = lambda off: (lambda j, t: (t, off + j))
    vec = pl.BlockSpec((1, ct), lambda j, t: (0, j))
    mat = pl.BlockSpec((None, ct, ct), lambda j, t: (j, 0, 0))
    return pl.pallas_call(
        body,
        name="rnn_fwd",
        grid=(D_RNN // ct, nt),
        in_specs=[
            pl.BlockSpec((tc, ct), col(COL_RNN_X)),
            pl.BlockSpec((tc, ct), col(COL_RNN_GATE)),
            pl.BlockSpec((CONV_W, ct), lambda j, t: (0, j)),
            vec, mat, mat, vec, vec, vec,
            pl.BlockSpec((8, 128), lambda j, t: (0, 0)),
        ],
        out_specs=[pl.BlockSpec((tc, ct), lambda j, t: (t, j))] * 3 + [pl.BlockSpec((ct, tc), lambda j, t: (j, t))],
        out_shape=[_sds((T, D_RNN), F32), _sds((T, D_RNN), BF16), _sds((T, D_RNN), F32), _sds((D_RNN, T), BF16)],
        scratch_shapes=[pltpu.VMEM((8, ct), F32), pltpu.VMEM((1, ct), F32)] + [
            pltpu.VMEM((ct // LANES, tc, LANES), F32)] * 4,
        compiler_params=_params(("parallel", "arbitrary"), 32),
    )(*_hbm(proj, proj, conv_w, conv_b, wa_bd, wx_bd, b_a, b_x, lam, token))


def _rnn_bwd(proj, conv, y_rnn, dz_rnn, conv_w, wa_bd, wx_bd, b_a, b_x, lam):
    T = proj.shape[0]
    tc, ct = RNN_CHUNK, RNN_TILE
    nt = T // tc
    hb = tc // 8

    def body(x_ref, c_ref, rg_ref, h_ref, hh_ref, dz_ref, cw_ref, wa_ref, wx_ref, ba_ref, bx_ref, lam_ref,
             dx_ref, drg_ref, dwa_ref, dwx_ref, sm_ref, lam_carry, a_carry, dc_head, b_s, dy_s, hl_s, al_s):
        t = pl.program_id(1)
        first_chunk = t == nt - 1

        @pl.when(t == 0)
        def _():
            lam_carry[...] = jnp.zeros_like(lam_carry)
            a_carry[...] = jnp.zeros_like(a_carry)
            dc_head[...] = jnp.zeros_like(dc_head)
            dwa_ref[...] = jnp.zeros_like(dwa_ref)
            dwx_ref[...] = jnp.zeros_like(dwx_ref)
            sm_ref[...] = jnp.zeros_like(sm_ref)

        row = lax.broadcasted_iota(jnp.int32, (tc, ct), 0)
        keep = jnp.where(first_chunk, 0.0, 1.0)
        x = x_ref[...].astype(F32)
        c = c_ref[...]
        lam = lam_ref[...]
        cb, r, i, a, mult, inv_mult = _rglru_gates(c, wa_ref[...], wx_ref[...], ba_ref[...], bx_ref[...], lam)
        h = h_ref[...]
        h_prev = _shift_down(h, hh_ref[...] * keep, 1, row)
        rg = rg_ref[...].astype(F32)
        dz = dz_ref[...]
        sg = _sigmoid(rg)
        drg_ref[...] = (dz * h * (sg * (1.0 + rg * (1.0 - sg)))).astype(BF16)
        dy = dz * (rg * sg)
        b = jnp.where(row >= tc - 1, a_carry[pl.ds(0, 1), :], pltpu.roll(a, tc - 1, 0))
        lt = _chunk_scan(b, dy, b_s, dy_s, hl_s, al_s, lam_carry[pl.ds(0, 1), :], reverse=True)
        lam_carry[...] = lt[0:8, :]
        a_carry[...] = a[0:8, :]
        ic = i * c
        dmult = lt * ic
        di = lt * mult * c
        dc = lt * mult * i
        dlog_a = a * (lt * h_prev - dmult * a * inv_mult)
        sp = _softplus(-lam)
        dpre_r = dlog_a * ((-LRU_C) * sp) * (r * (1.0 - r))
        dpre_i = di * (i * (1.0 - i))
        dlam_row = jnp.sum(dlog_a * r, axis=0, keepdims=True) * (LRU_C * _sigmoid(-lam))
        dpr_b = dpre_r.astype(BF16)
        dpi_b = dpre_i.astype(BF16)
        dwa_ref[...] += _dot_tn(cb, dpr_b)
        dwx_ref[...] += _dot_tn(cb, dpi_b)
        dc = dc + _dot_nt(dpr_b, wa_ref[...]) + _dot_nt(dpi_b, wx_ref[...])
        head = dc_head[...]
        dx = cw_ref[pl.ds(3, 1), :] * dc
        sm_ref[pl.ds(4 + 3, 1), :] += jnp.sum(dc * x, axis=0, keepdims=True)
        for m in range(1, CONV_W):
            up = _shift_up(dc, head, m, row)
            dx = dx + cw_ref[pl.ds(3 - m, 1), :] * up
            sm_ref[pl.ds(4 + 3 - m, 1), :] += jnp.sum(up * x, axis=0, keepdims=True)
        dx_ref[...] = dx.astype(BF16)
        dc_head[...] = dc[0:8, :]
        sm_ref[pl.ds(0, 1), :] += jnp.sum(dpre_r, axis=0, keepdims=True)
        sm_ref[pl.ds(1, 1), :] += jnp.sum(dpre_i, axis=0, keepdims=True)
        sm_ref[pl.ds(2, 1), :] += dlam_row
        sm_ref[pl.ds(3, 1), :] += jnp.sum(dc, axis=0, keepdims=True)

    rev = lambda off: (lambda j, t: (nt - 1 - t, off + j))
    halo = lambda off: (lambda j, t: (jnp.maximum((nt - 1 - t) * hb - 1, 0), off + j))
    vec = pl.BlockSpec((1, ct), lambda j, t: (0, j))
    mat = pl.BlockSpec((None, ct, ct), lambda j, t: (j, 0, 0))
    return pl.pallas_call(
        body,
        name="rnn_bwd",
        grid=(D_RNN // ct, nt),
        in_specs=[
            pl.BlockSpec((tc, ct), rev(COL_RNN_X)),
            pl.BlockSpec((tc, ct), rev(0)),
            pl.BlockSpec((tc, ct), rev(COL_RNN_GATE)),
            pl.BlockSpec((tc, ct), rev(0)),
            pl.BlockSpec((8, ct), halo(0)),
            pl.BlockSpec((tc, ct), rev(0)),
            pl.BlockSpec((CONV_W, ct), lambda j, t: (0, j)),
            mat, mat, vec, vec, vec,
        ],
        out_specs=[
            pl.BlockSpec((tc, ct), rev(0)),
            pl.BlockSpec((tc, ct), rev(0)),
            mat, mat,
            pl.BlockSpec((8, ct), lambda j, t: (0, j)),
        ],
        out_shape=[_sds((T, D_RNN), BF16), _sds((T, D_RNN), BF16), _sds((D_RNN // ct, ct, ct), F32),
                   _sds((D_RNN // ct, ct, ct), F32), _sds((8, D_RNN), F32)],
        scratch_shapes=[pltpu.VMEM((8, ct), F32)] * 3 + [pltpu.VMEM((ct // LANES, tc, LANES), F32)] * 4,
        compiler_params=_params(("parallel", "arbitrary"), 32),
    )(*_hbm(proj, conv, proj, y_rnn, y_rnn, dz_rnn, conv_w, wa_bd, wx_bd, b_a, b_x, lam))


def _attn_bias():
    qi = jnp.arange(BLOCK)[:, None]
    kj = jnp.arange(BLOCK)[None, :]
    dist_cur = (qi - kj).astype(F32)
    slopes = 2.0 ** (-ALIBI_MAX_BIAS * jnp.arange(1, N_Q_HEADS + 1, dtype=F32) / N_Q_HEADS)
    slopes = slopes[:, None, None]
    prev = jnp.where(kj > qi, -slopes * (dist_cur + float(BLOCK)), NEG_BIG)
    cur = jnp.where(kj <= qi, -slopes * dist_cur, NEG_BIG)
    later = jnp.concatenate([prev, cur], axis=-1)
    first = jnp.concatenate([jnp.full_like(prev, NEG_BIG), cur], axis=-1)
    return jnp.stack([first, later])


def _attn_exps(s_prev, s_cur, sink, bias):
    s_prev = s_prev + bias[:, 0:BLOCK]
    s_cur = s_cur + bias[:, BLOCK:2 * BLOCK]
    m = jnp.maximum(jnp.max(jnp.maximum(s_prev, s_cur), axis=-1, keepdims=True), sink)
    p_prev = jnp.exp(s_prev - m)
    p_cur = jnp.exp(s_cur - m)
    total = jnp.sum(p_prev + p_cur, axis=-1, keepdims=True) + jnp.exp(sink - m)
    return p_prev, p_cur, 1.0 / total, m + jnp.log(total)


def _attn_probs(s_prev, s_cur, sink, bias, lse):
    p_prev = jnp.exp((s_prev + bias[:, 0:BLOCK]) - lse)
    p_cur = jnp.exp((s_cur + bias[:, BLOCK:2 * BLOCK]) - lse)
    return p_prev, p_cur, jnp.exp(sink - lse)


def _stack_heads(ref_or_val, hk, dtype):
    parts = [ref_or_val[:, (GROUP * hk + g) * HEAD_DIM:(GROUP * hk + g + 1) * HEAD_DIM] for g in range(GROUP)]
    return jnp.concatenate(parts, axis=0).astype(dtype)


ATTN_SCALE = HEAD_DIM ** -0.5


def _bias_spec():
    return pl.BlockSpec((None, N_Q_HEADS, BLOCK, 2 * BLOCK), lambda i: (jnp.minimum(i, 1), 0, 0, 0))


def _attn_fwd(proj, sinks, bias):
    T = proj.shape[0]
    nb = T // BLOCK

    def body(sink_ref, bias_ref, q_ref, kp_ref, kc_ref, vp_ref, vc_ref, ag0_ref, ag1_ref, y_ref, z_ref, lse_ref):
        kvs = [slice(hk * HEAD_DIM, (hk + 1) * HEAD_DIM) for hk in range(N_KV_HEADS)]
        qgs = [(_stack_heads(q_ref, hk, F32) * ATTN_SCALE).astype(BF16) for hk in range(N_KV_HEADS)]
        s_prev = [_dot_nt(qgs[hk], kp_ref[:, kvs[hk]].astype(BF16)) for hk in range(N_KV_HEADS)]
        s_cur = [_dot_nt(qgs[hk], kc_ref[:, kvs[hk]].astype(BF16)) for hk in range(N_KV_HEADS)]
        for hk in range(N_KV_HEADS):
            pp, pc, invs = [], [], []
            for g in range(GROUP):
                h = GROUP * hk + g
                rows = slice(g * BLOCK, (g + 1) * BLOCK)
                p_prev, p_cur, inv, lse = _attn_exps(s_prev[hk][rows], s_cur[hk][rows], sink_ref[h], bias_ref[h])
                pp.append(p_prev.astype(BF16))
                pc.append(p_cur.astype(BF16))
                invs.append(inv)
                lse_ref[:, h:h + 1] = lse
            og = _dot(jnp.concatenate(pp, axis=0), vp_ref[:, kvs[hk]].astype(BF16)) + _dot(
                jnp.concatenate(pc, axis=0), vc_ref[:, kvs[hk]].astype(BF16))
            for g in range(GROUP):
                h = GROUP * hk + g
                y_ref[:, h * HEAD_DIM:(h + 1) * HEAD_DIM] = og[g * BLOCK:(g + 1) * BLOCK] * invs[g]
        ag = jnp.concatenate([ag0_ref[...], ag1_ref[...]], axis=1).astype(F32)
        z_ref[...] = (y_ref[...] * (ag * _sigmoid(ag))).astype(BF16)

    prev = lambda c: (lambda i: (jnp.maximum(i - 1, 0), c))
    cur = lambda c: (lambda i: (i, c))
    return pl.pallas_call(
        body,
        name="attn_fwd",
        grid=(nb,),
        in_specs=[
            pl.BlockSpec(memory_space=pltpu.SMEM),
            _bias_spec(),
            pl.BlockSpec((BLOCK, 1024), lambda i: (i, COL_Q // 4)),
            pl.BlockSpec((BLOCK, D_KV), prev(COL_K)),
            pl.BlockSpec((BLOCK, D_KV), cur(COL_K)),
            pl.BlockSpec((BLOCK, D_KV), prev(COL_V)),
            pl.BlockSpec((BLOCK, D_KV), cur(COL_V)),
            pl.BlockSpec((BLOCK, 512), lambda i: (i, COL_ATTN_GATE // 2)),
            pl.BlockSpec((BLOCK, 512), lambda i: (i, COL_ATTN_GATE // 2 + 1)),
        ],
        out_specs=[pl.BlockSpec((BLOCK, 1024), lambda i: (i, 0)), pl.BlockSpec((BLOCK, 1024), lambda i: (i, 0)),
                   pl.BlockSpec((BLOCK, N_Q_HEADS), lambda i: (i, 0))],
        out_shape=[_sds((T, 1024), F32), _sds((T, 1024), BF16), _sds((T, N_Q_HEADS), F32)],
        compiler_params=_params(("arbitrary",), 32),
    )(sinks, *_hbm(bias, proj, proj, proj, proj, proj, proj, proj))


def _attn_bwd(proj, y_attn, lse, dz_attn, sinks, bias, token):
    T = proj.shape[0]
    nb = T // BLOCK

    def body(sink_ref, bias_ref, q_ref, kp_ref, kc_ref, vp_ref, vc_ref, ag0_ref, ag1_ref, y_ref, lse_ref, dz_ref,
             token_ref, dq_ref, dk_ref, dv_ref, dag_ref, ds_ref, dy_s):
        i = pl.program_id(0)

        @pl.when(i == 0)
        def _():
            ds_ref[...] = jnp.zeros_like(ds_ref)

        lane = lax.broadcasted_iota(jnp.int32, (8, 128), 1)
        sub = lax.broadcasted_iota(jnp.int32, (8, 128), 0)
        ag = jnp.concatenate([ag0_ref[...], ag1_ref[...]], axis=1).astype(F32)
        dz = dz_ref[...]
        sg = _sigmoid(ag)
        dag_ref[...] = (dz * y_ref[...] * (sg * (1.0 + ag * (1.0 - sg)))).astype(BF16)
        dy_s[...] = dz * (ag * sg)
        r_cur = pl.multiple_of(i * BLOCK, BLOCK)
        r_prev = pl.multiple_of(jnp.maximum(i - 1, 0) * BLOCK, BLOCK)
        dk_cur, dv_cur, dk_prev, dv_prev = [], [], [], []
        ds_acc = jnp.zeros((8, 128), F32)
        for hk in range(N_KV_HEADS):
            ks = slice(hk * HEAD_DIM, (hk + 1) * HEAD_DIM)
            qg = (_stack_heads(q_ref, hk, F32) * ATTN_SCALE).astype(BF16)
            dog = _stack_heads(dy_s, hk, F32)
            og = _stack_heads(y_ref, hk, F32)
            dog_b = dog.astype(BF16)
            kp = kp_ref[:, ks].astype(BF16)
            kc = kc_ref[:, ks].astype(BF16)
            vp = vp_ref[:, ks].astype(BF16)
            vc = vc_ref[:, ks].astype(BF16)
            s_prev = _dot_nt(qg, kp)
            s_cur = _dot_nt(qg, kc)
            dp_prev = _dot_nt(dog_b, vp)
            dp_cur = _dot_nt(dog_b, vc)
            dvec = jnp.sum(dog * og, axis=-1, keepdims=True)
            pp, pc, dsp, dsc = [], [], [], []
            for g in range(GROUP):
                h = GROUP * hk + g
                rows = slice(g * BLOCK, (g + 1) * BLOCK)
                p_prev, p_cur, p_sink = _attn_probs(
                    s_prev[rows], s_cur[rows], sink_ref[h], bias_ref[h], lse_ref[:, h:h + 1])
                d_h = dvec[rows]
                pp.append(p_prev.astype(BF16))
                pc.append(p_cur.astype(BF16))
                dsp.append((p_prev * (dp_prev[rows] - d_h)).astype(BF16))
                dsc.append((p_cur * (dp_cur[rows] - d_h)).astype(BF16))
                dsink = -jnp.sum(p_sink * d_h, axis=0, keepdims=True)
                ds_acc = ds_acc + jnp.where(jnp.logical_and(lane == h, sub == 1), dsink, 0.0)
            pp = jnp.concatenate(pp, axis=0)
            pc = jnp.concatenate(pc, axis=0)
            dsp = jnp.concatenate(dsp, axis=0)
            dsc = jnp.concatenate(dsc, axis=0)
            dqg = (_dot(dsp, kp) + _dot(dsc, kc)) * ATTN_SCALE
            for g in range(GROUP):
                h = GROUP * hk + g
                dq_ref[:, h * HEAD_DIM:(h + 1) * HEAD_DIM] = dqg[g * BLOCK:(g + 1) * BLOCK].astype(BF16)
            dk_ref[pl.ds(r_cur, BLOCK), ks] = _dot_tn(dsc, qg)
            dv_ref[pl.ds(r_cur, BLOCK), ks] = _dot_tn(pc, dog_b)
            dk_prev.append(_dot_tn(dsp, qg))
            dv_prev.append(_dot_tn(pp, dog_b))
        ds_ref[:, 0:128] += ds_acc

        @pl.when(i > 0)
        def _():
            for hk in range(N_KV_HEADS):
                ks = slice(hk * HEAD_DIM, (hk + 1) * HEAD_DIM)
                dk_ref[pl.ds(r_prev, BLOCK), ks] += dk_prev[hk]
                dv_ref[pl.ds(r_prev, BLOCK), ks] += dv_prev[hk]

    prev = lambda c: (lambda i: (jnp.maximum(i - 1, 0), c))
    cur = lambda c: (lambda i: (i, c))
    blk = pl.BlockSpec((BLOCK, 1024), lambda i: (i, 0))
    whole = pl.BlockSpec((T, D_KV), lambda i: (0, 0))
    return pl.pallas_call(
        body,
        name="attn_bwd",
        grid=(nb,),
        in_specs=[
            pl.BlockSpec(memory_space=pltpu.SMEM),
            _bias_spec(),
            pl.BlockSpec((BLOCK, 1024), lambda i: (i, COL_Q // 4)),
            pl.BlockSpec((BLOCK, D_KV), prev(COL_K)),
            pl.BlockSpec((BLOCK, D_KV), cur(COL_K)),
            pl.BlockSpec((BLOCK, D_KV), prev(COL_V)),
            pl.BlockSpec((BLOCK, D_KV), cur(COL_V)),
            pl.BlockSpec((BLOCK, 512), lambda i: (i, COL_ATTN_GATE // 2)),
            pl.BlockSpec((BLOCK, 512), lambda i: (i, COL_ATTN_GATE // 2 + 1)),
            blk,
            pl.BlockSpec((BLOCK, N_Q_HEADS), lambda i: (i, 0)),
            blk,
            pl.BlockSpec((8, 128), lambda i: (0, 0)),
        ],
        out_specs=[blk, whole, whole, blk, pl.BlockSpec((8, 1024), lambda i: (0, 0))],
        out_shape=[_sds((T, 1024), BF16), _sds((T, D_KV), F32), _sds((T, D_KV), F32), _sds((T, 1024), BF16),
                   _sds((8, 1024), F32)],
        scratch_shapes=[pltpu.VMEM((BLOCK, 1024), F32)],
        compiler_params=_params(("arbitrary",), 48),
    )(sinks, *_hbm(bias, proj, proj, proj, proj, proj, proj, proj, y_attn, lse, dz_attn, token))


def _head(x, target, z_rnn, z_attn, proj, b_gate, g_post, w_rnn_out, w_attn_out, w_out):
    T = x.shape[0]
    tm = 256

    def body(x_ref, t_ref, zr_ref, za_ref, ml0_ref, ml1_ref, ml2_ref, ml3_ref, bg_ref, gp_ref, wr_ref, wa_ref, wo_ref,
             dyx_ref, dzr_ref, dza_ref, dml_ref, dout_ref, dbr_ref, dba_ref, mt_ref, zat_ref, sm_ref):
        @pl.when(pl.program_id(0) == 0)
        def _():
            sm_ref[...] = jnp.zeros_like(sm_ref)

        wr, wa, wo = wr_ref[...], wa_ref[...], wo_ref[...]
        br_rnn = _dot(zr_ref[...], wr)
        br_attn = _dot(za_ref[...], wa)
        zat_ref[...] = za_ref[...].astype(F32).T.astype(BF16)
        ml_rnn = jnp.concatenate([ml0_ref[...], ml1_ref[...]], axis=1).astype(F32)
        ml_attn = jnp.concatenate([ml2_ref[...], ml3_ref[...]], axis=1).astype(F32)
        g_rnn = _sigmoid(ml_rnn + bg_ref[:, 0:D_MODEL])
        g_attn = _sigmoid(ml_attn + bg_ref[:, D_MODEL:2 * D_MODEL])
        merged = g_rnn * br_rnn + g_attn * br_attn
        mb = merged.astype(BF16)
        mt_ref[...] = merged.T.astype(BF16)
        out = _dot(mb, wo)
        rstd = lax.rsqrt(jnp.mean(out * out, axis=-1, keepdims=True) + EPS)
        n = out * rstd
        gp = gp_ref[...]
        err = (x_ref[...] + n * gp) - t_ref[...]
        sm_ref[pl.ds(3, 1), :] += 0.5 * jnp.sum(jnp.mean(err * err, axis=-1, keepdims=True), axis=0, keepdims=True)
        dy = err * (1.0 / D_MODEL)
        dyx_ref[...] = dy
        sm_ref[pl.ds(0, 1), :] += jnp.sum(dy * n, axis=0, keepdims=True)
        dn = dy * gp
        dout = (rstd * (dn - n * jnp.mean(dn * n, axis=-1, keepdims=True))).astype(BF16)
        dout_ref[...] = dout
        dmerged = _dot_nt(dout, wo)
        dml_r = (dmerged * br_rnn) * (g_rnn * (1.0 - g_rnn))
        dml_a = (dmerged * br_attn) * (g_attn * (1.0 - g_attn))
        dml_ref[:, 0:D_MODEL] = dml_r.astype(BF16)
        dml_ref[:, D_MODEL:2 * D_MODEL] = dml_a.astype(BF16)
        sm_ref[pl.ds(1, 1), :] += jnp.sum(dml_r, axis=0, keepdims=True)
        sm_ref[pl.ds(2, 1), :] += jnp.sum(dml_a, axis=0, keepdims=True)
        dbr = (dmerged * g_rnn).astype(BF16)
        dba = (dmerged * g_attn).astype(BF16)
        dbr_ref[...] = dbr
        dba_ref[...] = dba
        dzr_ref[...] = _dot_nt(dbr, wr)
        dza_ref[...] = _dot_nt(dba, wa)

    tile = pl.BlockSpec((tm, D_MODEL), lambda i: (i, 0))
    wspec = pl.BlockSpec((D_MODEL, D_MODEL), lambda i: (0, 0))
    ml = lambda q: pl.BlockSpec((tm, 512), lambda i: (i, COL_MERGE // 2 + q))
    return pl.pallas_call(
        body,
        name="head",
        grid=(T // tm,),
        in_specs=[
            tile, tile, tile, tile,
            ml(0), ml(1), ml(2), ml(3),
            pl.BlockSpec((1, 2 * D_MODEL), lambda i: (0, 0)),
            pl.BlockSpec((1, D_MODEL), lambda i: (0, 0)),
            wspec, wspec, wspec,
        ],
        out_specs=[
            tile, tile, tile,
            pl.BlockSpec((tm, 2 * D_MODEL), lambda i: (i, 0)),
            tile, tile, tile,
            pl.BlockSpec((D_MODEL, tm), lambda i: (0, i)), pl.BlockSpec((D_MODEL, tm), lambda i: (0, i)),
            pl.BlockSpec((8, D_MODEL), lambda i: (0, 0)),
        ],
        out_shape=[
            _sds((T, D_MODEL), F32), _sds((T, D_MODEL), F32), _sds((T, D_MODEL), F32),
            _sds((T, 2 * D_MODEL), BF16),
            _sds((T, D_MODEL), BF16), _sds((T, D_MODEL), BF16), _sds((T, D_MODEL), BF16),
            _sds((D_MODEL, T), BF16), _sds((D_MODEL, T), BF16),
            _sds((8, D_MODEL), F32),
        ],
        compiler_params=_params(("arbitrary",), 56),
    )(*_hbm(x, target, z_rnn, z_attn, proj, proj, proj, proj, b_gate, g_post, w_rnn_out, w_attn_out, w_out))


def _matmul_t(at, b, name):
    M, T = at.shape
    N = b.shape[1]
    tk = min(1024, T)
    nt = T // tk

    def body(a_ref, b_ref, o_ref, ob_ref):
        @pl.when(pl.program_id(0) == 0)
        def _():
            o_ref[...] = jnp.zeros_like(o_ref)

        o_ref[...] += _dot(a_ref[...], b_ref[...])

        @pl.when(pl.program_id(0) == nt - 1)
        def _():
            ob_ref[...] = o_ref[...].astype(BF16)

    whole = pl.BlockSpec((M, N), lambda t: (0, 0))
    return pl.pallas_call(
        body,
        name=name,
        grid=(nt,),
        in_specs=[pl.BlockSpec((M, tk), lambda t: (0, t)), pl.BlockSpec((tk, N), lambda t: (t, 0))],
        out_specs=[whole, whole],
        out_shape=[_sds((M, N), F32), _sds((M, N), BF16)],
        compiler_params=_params(("arbitrary",), 48),
    )(*_hbm(at, b))


DPROJ_WIDTHS = (D_RNN, D_RNN, 1024, D_KV, D_KV, 1024, 2 * D_MODEL)


def _dproj_segments():
    segs, start = [[] for _ in range(N_CHIPS)], 0
    for p, width in enumerate(DPROJ_WIDTHS):
        for c in range(N_CHIPS):
            lo, hi = max(start, c * W_IN_SHARD), min(start + width, (c + 1) * W_IN_SHARD)
            if lo < hi:
                segs[c].append((p, lo - start, hi - start, lo - c * W_IN_SHARD, hi - c * W_IN_SHARD))
        start += width
    return segs


def _dh_bwd(pieces, w_in_g, x, dyx, g_pre, token):
    T = x.shape[0]
    tm = min(512, T)
    n = len(pieces)
    segs = _dproj_segments()

    def body(*refs):
        p_refs, w_hbm, x_ref, dyx_ref, g_ref = refs[0:n], refs[n], refs[n + 1], refs[n + 2], refs[n + 3]
        gx_ref, dg_ref, w_ref, w_sems = refs[n + 5], refs[n + 6], refs[n + 7], refs[n + 8]
        first = pl.program_id(0) == 0
        w_copies = [pltpu.make_async_copy(w_hbm.at[c], w_ref.at[c], w_sems.at[c]) for c in range(N_CHIPS)]

        @pl.when(first)
        def _():
            for cp in w_copies:
                cp.start()
            dg_ref[...] = jnp.zeros_like(dg_ref)

        dh = None
        for c in range(N_CHIPS):
            pl.when(first)(w_copies[c].wait)
            for p, a0, a1, u0, u1 in segs[c]:
                part = _dot_nt(p_refs[p][:, a0:a1].astype(BF16), w_ref[c, :, u0:u1])
                dh = part if dh is None else dh + part
        xv = x_ref[...]
        rstd = lax.rsqrt(jnp.mean(xv * xv, axis=-1, keepdims=True) + EPS)
        nx = xv * rstd
        dhg = dh * g_ref[...]
        gx_ref[...] = dyx_ref[...] + rstd * (dhg - nx * jnp.mean(dhg * nx, axis=-1, keepdims=True))
        dg_ref[pl.ds(0, 1), :] += jnp.sum(dh * nx, axis=0, keepdims=True)

    tile = pl.BlockSpec((tm, D_MODEL), lambda i: (i, 0))
    return pl.pallas_call(
        body,
        name="dh_bwd",
        grid=(T // tm,),
        in_specs=[pl.BlockSpec((tm, w), lambda i: (i, 0)) for w in DPROJ_WIDTHS] + [
            ANY, tile, tile,
            pl.BlockSpec((1, D_MODEL), lambda i: (0, 0)),
            pl.BlockSpec((8, 128), lambda i: (0, 0)),
        ],
        out_specs=[tile, pl.BlockSpec((8, D_MODEL), lambda i: (0, 0))],
        out_shape=[_sds((T, D_MODEL), F32), _sds((8, D_MODEL), F32)],
        scratch_shapes=[pltpu.VMEM(w_in_g.shape, BF16), pltpu.SemaphoreType.DMA((N_CHIPS,))],
        compiler_params=_params(("arbitrary",), 56),
    )(*_hbm(*pieces, w_in_g, x, dyx, g_pre, token))


def _dw_in(ht, pieces):
    T = ht.shape[1]
    tk = min(512, T)
    nt = T // tk
    n = len(pieces)
    segs = _dproj_segments()

    def body(*refs):
        h_ref, p_refs, o_ref, ob_ref = refs[0], refs[1:n + 1], refs[n + 1], refs[n + 2]

        @pl.when(pl.program_id(1) == 0)
        def _():
            o_ref[...] = jnp.zeros_like(o_ref)

        for c in range(N_CHIPS):
            @pl.when(pl.program_id(0) == c)
            def _():
                for p, a0, a1, u0, u1 in segs[c]:
                    o_ref[:, u0:u1] += _dot(h_ref[...], p_refs[p][:, a0:a1].astype(BF16))

        @pl.when(pl.program_id(1) == nt - 1)
        def _():
            ob_ref[...] = o_ref[...].astype(BF16)

    def piece_spec(p):
        chips = [c for c in range(N_CHIPS) if any(s[0] == p for s in segs[c])]

        def index(c, t):
            used = functools.reduce(jnp.logical_or, [c == k for k in chips])
            return (jnp.where(used, t, 0), 0)

        return pl.BlockSpec((tk, DPROJ_WIDTHS[p]), index)

    return pl.pallas_call(
        body,
        name="dw_in",
        grid=(N_CHIPS, nt),
        in_specs=[pl.BlockSpec((D_MODEL, tk), lambda c, t: (0, t))] + [piece_spec(p) for p in range(n)],
        out_specs=[pl.BlockSpec((None, D_MODEL, W_IN_SHARD), lambda c, t: (c, 0, 0))] * 2,
        out_shape=[_sds((N_CHIPS, D_MODEL, W_IN_SHARD), F32), _sds((N_CHIPS, D_MODEL, W_IN_SHARD), BF16)],
        compiler_params=_params(("parallel", "arbitrary"), 56),
    )(*_hbm(ht, *pieces))


ELEMENTWISE_TILE_BYTES = MIB


def _row_tile(rows, cols, limit=ELEMENTWISE_TILE_BYTES):
    if rows * cols * 4 <= limit:
        return rows
    for t in (512, 256, 128, 64, 32, 16, 8):
        if rows % t == 0 and t * cols * 4 <= limit:
            return t
    return rows


def _chip_sum(ps, gots, chip_core, name):
    n = len(ps)
    h, C = ps[0].shape
    tr = _row_tile(h, C * n)
    nt = h // tr

    def body(jc_ref, *refs):
        for a in range(n):
            p_ref, g0_ref, g1_ref, g2_ref, o_ref = refs[a], refs[n + 3 * a], refs[n + 3 * a + 1], refs[n + 3 * a + 2], \
                refs[4 * n + a]
            o_ref[...] = ((p_ref[...] + g0_ref[...].astype(F32)) + g1_ref[...].astype(F32)) + g2_ref[...].astype(F32)

    rel = lambda r: pl.BlockSpec((None, tr, C), lambda i, jc_ref: (r, i, 0))
    outs = pl.pallas_call(
        body,
        name=name,
        grid_spec=pltpu.PrefetchScalarGridSpec(
            num_scalar_prefetch=1,
            grid=(nt,),
            in_specs=[pl.BlockSpec((tr, C), lambda i, jc_ref: (i, 0))] * n + [rel(0), rel(1), rel(2)] * n,
            out_specs=[pl.BlockSpec((tr, C), lambda i, jc_ref: (jc_ref[1] * nt + i, 0))] * n,
        ),
        out_shape=[_sds((2 * h, C), F32)] * n,
        compiler_params=_params(("parallel",), 48),
    )(chip_core, *_hbm(*ps, *[g for got in gots for g in (got, got, got)]))
    return list(outs)


def _place_shards(shards, chip, name):
    n = len(shards)
    tiles = [_row_tile(s.shape[0], s.shape[1]) for s in shards]
    steps = max(s.shape[0] // t for s, t in zip(shards, tiles))
    tiles = [s.shape[0] // steps for s in shards]

    def body(j_ref, *refs):
        for a in range(n):
            refs[n + a][...] = refs[a][...].astype(BF16)

    return pl.pallas_call(
        body,
        name=name,
        grid_spec=pltpu.PrefetchScalarGridSpec(
            num_scalar_prefetch=1,
            grid=(steps,),
            in_specs=[pl.BlockSpec((t, s.shape[1]), lambda i, j_ref: (i, 0)) for s, t in zip(shards, tiles)],
            out_specs=[pl.BlockSpec((None, t, s.shape[1]), lambda i, j_ref: (j_ref[0], i, 0))
                       for s, t in zip(shards, tiles)],
        ),
        out_shape=[_sds((N_CHIPS,) + s.shape, BF16) for s in shards],
        compiler_params=_params(("parallel",), 48),
    )(chip, *_hbm(*shards))


def _adamw(params, name, with_grads):
    n = len(params)
    k = 4 if with_grads else 3
    R, C = params[0][0].shape
    tr = _row_tile(R, C * n)
    c1 = 1.0 - ADAM_B1 ** ADAM_STEP
    c2 = 1.0 - ADAM_B2 ** ADAM_STEP

    def body(*refs):
        for a in range(n):
            w_ref, g_ref, m_ref, v_ref = refs[4 * a:4 * a + 4]
            d_ref, nm_ref, nv_ref = refs[4 * n + k * a:4 * n + k * a + 3]
            g = g_ref[...]
            nm = ADAM_B1 * m_ref[...] + (1.0 - ADAM_B1) * g
            nv = ADAM_B2 * v_ref[...] + (1.0 - ADAM_B2) * (g * g)
            nm_ref[...] = nm
            nv_ref[...] = nv
            d_ref[...] = (-ADAM_LR) * ((nm / c1) / (jnp.sqrt(nv / c2) + ADAM_EPS) + ADAM_WD * w_ref[...])
            if with_grads:
                refs[4 * n + k * a + 3][...] = g

    spec = pl.BlockSpec((tr, C), lambda i: (i, 0))
    outs = pl.pallas_call(
        body, name=name, grid=(R // tr,), in_specs=[spec] * (4 * n), out_specs=[spec] * (k * n),
        out_shape=[_sds((R, C), F32)] * (k * n), compiler_params=_params(("parallel",), 48),
    )(*_hbm(*[t for p in params for t in p]))
    return [tuple(outs[k * a:k * a + k]) for a in range(n)]


def _place():
    return lax.axis_index("x"), lax.axis_index("y"), lax.axis_index("c")


def _chip_of(x, y, r):
    return (x ^ (r >> 1), y ^ (r & 1))


ANY = pl.BlockSpec(memory_space=pl.ANY)


def _gather_weights(placed, cw8):
    nbig = len(placed)
    halves = [s.shape[1] // 2 for s in placed]
    pieces = [max(1, h // 64) for h in halves]
    rows = [h // p for h, p in zip(halves, pieces)]
    order = [(a, q) for q in range(max(pieces)) for a in range(nbig) if q < pieces[a]]
    ici_sem = {(a, q, r): 3 * i + (r - 1) for i, (a, q) in enumerate(order) for r in (1, 2, 3)}
    cw_sem = {r: 3 * len(order) + (r - 1) for r in (1, 2, 3)}
    d2d_sem = {key: 3 * len(order) + 3 + k for key, k in ici_sem.items()}
    nsem = 6 * len(order) + 3

    def body(*refs):
        cw_ref, dsts, gcw_ref = refs[nbig], refs[nbig + 1:2 * nbig + 1], refs[2 * nbig + 1]
        send_sems, recv_sems = refs[2 * nbig + 2:]
        x, y, c = _place()
        j = 2 * x + y

        def piece_rows(a, q, core):
            return pl.ds(pl.multiple_of(core * halves[a] + q * rows[a], 16), rows[a])

        def ici(a, q, r):
            tx, ty = _chip_of(x, y, r)
            k = ici_sem[(a, q, r)]
            region = dsts[a].at[j, piece_rows(a, q, c), :]
            return pltpu.make_async_remote_copy(
                src_ref=region, dst_ref=region, send_sem=send_sems.at[k], recv_sem=recv_sems.at[k],
                device_id=(tx, ty, c), device_id_type=MESH)

        def ici_landed(a, q, r):
            tx, ty = _chip_of(x, y, r)
            k = ici_sem[(a, q, r)]
            region = dsts[a].at[2 * tx + ty, piece_rows(a, q, c), :]
            return pltpu.make_async_remote_copy(
                src_ref=region, dst_ref=region, send_sem=send_sems.at[k], recv_sem=recv_sems.at[k],
                device_id=(tx, ty, c), device_id_type=MESH)

        def d2d(a, q, r, core):
            tx, ty = _chip_of(x, y, r)
            k = d2d_sem[(a, q, r)]
            region = dsts[a].at[2 * tx + ty, piece_rows(a, q, core), :]
            return pltpu.make_async_remote_copy(
                src_ref=region, dst_ref=region, send_sem=send_sems.at[k], recv_sem=recv_sems.at[k],
                device_id=(x, y, 1 - c), device_id_type=MESH)

        def cw_copy(r):
            tx, ty = _chip_of(x, y, r)
            k = cw_sem[r]
            return pltpu.make_async_remote_copy(
                src_ref=cw_ref, dst_ref=gcw_ref.at[j], send_sem=send_sems.at[k], recv_sem=recv_sems.at[k],
                device_id=(tx, ty, c), device_id_type=MESH)

        def cw_landed(r):
            tx, ty = _chip_of(x, y, r)
            k = cw_sem[r]
            region = gcw_ref.at[2 * tx + ty]
            return pltpu.make_async_remote_copy(
                src_ref=region, dst_ref=region, send_sem=send_sems.at[k], recv_sem=recv_sems.at[k],
                device_id=(tx, ty, c), device_id_type=MESH)

        def relay(a, q, origin, to):
            ox, oy = _chip_of(x, y, origin)
            tx, ty = _chip_of(x, y, to)
            k = ici_sem[(a, q, 3)]
            region = dsts[a].at[2 * ox + oy, piece_rows(a, q, c), :]
            return pltpu.make_async_remote_copy(
                src_ref=region, dst_ref=region, send_sem=send_sems.at[k], recv_sem=recv_sems.at[k],
                device_id=(tx, ty, c), device_id_type=MESH)

        first = [ici(a, q, r) for (a, q) in order for r in (1, 2)] + [cw_copy(r) for r in (1, 2, 3)]
        for cp in first:
            cp.start()
        passed = []
        for (a, q) in order:
            for r in (1, 2):
                ici_landed(a, q, r).wait_recv()
                if q % 2 == r - 1:
                    cp = relay(a, q, r, 3 - r)
                    cp.start()
                    passed.append(cp)
                cp = d2d(a, q, r, c)
                cp.start()
                passed.append(cp)
        for (a, q) in order:
            ici_landed(a, q, 3).wait_recv()
            cp = d2d(a, q, 3, c)
            cp.start()
            passed.append(cp)
        for r in (1, 2, 3):
            cw_landed(r).wait_recv()
        for (a, q) in order:
            for r in (1, 2, 3):
                d2d(a, q, r, 1 - c).wait_recv()
        for cp in first + passed:
            cp.wait_send()

    return pl.pallas_call(
        body,
        name="gather_weights",
        in_specs=[ANY] * (nbig + 1),
        out_specs=[ANY] * (nbig + 1),
        out_shape=[_sds(s.shape, s.dtype) for s in placed] + [_sds((N_CHIPS,) + cw8.shape, cw8.dtype)],
        input_output_aliases={a: a for a in range(nbig)},
        scratch_shapes=[pltpu.SemaphoreType.DMA((nsem,)), pltpu.SemaphoreType.DMA((nsem,))],
    )(*placed, cw8)


def _gather_late_start(placed, after, name):
    n = len(placed)
    halves = [s.shape[1] // 2 for s in placed]

    def body(*refs):
        g_refs = refs[0:n]
        send_sems, recv_sems, token = refs[n + 1], refs[n + 2], refs[-1]
        x, y, c = _place()
        j = 2 * x + y
        for a in range(n):
            mine = g_refs[a].at[j, pl.ds(pl.multiple_of(c * halves[a], 16), halves[a]), :]
            for r in (1, 2, 3):
                tx, ty = _chip_of(x, y, r)
                for to_core in (0, 1):
                    k = ((a * 3 + (r - 1)) * 2 + c) * 2 + to_core
                    pltpu.make_async_remote_copy(
                        src_ref=mine, dst_ref=mine, send_sem=send_sems.at[k], recv_sem=recv_sems.at[k],
                        device_id=(tx, ty, to_core), device_id_type=MESH).start()
        token[...] = jnp.zeros_like(token)

    hbm = lambda t: pltpu.HBM(t.shape, t.dtype)
    keep = lambda t: pltpu.with_memory_space_constraint(t, pltpu.HBM)
    nsem = 12 * n
    outs = pl.pallas_call(
        body,
        name=name,
        in_specs=[HBM] * n + [ANY],
        out_specs=(SEM, SEM, *[HBM] * n, pl.BlockSpec(memory_space=pltpu.VMEM)),
        out_shape=(pltpu.SemaphoreType.DMA((nsem,)), pltpu.SemaphoreType.DMA((nsem,)), *[hbm(p) for p in placed],
                   jax.ShapeDtypeStruct((8, 128), F32)),
        input_output_aliases={i: 2 + i for i in range(n)},
        compiler_params=pltpu.CompilerParams(has_side_effects=DATAFLOW),
    )(*[keep(p) for p in placed], after)
    return outs[0], outs[1], list(outs[2:2 + n]), outs[-1]


def _gather_late_wait(send_sems, recv_sems, thru, after, name):
    n = len(thru)
    halves = [s.shape[1] // 2 for s in thru]

    def body(*refs):
        g_refs = refs[0:n]
        send_sems, recv_sems = refs[n], refs[n + 1]
        x, y, c = _place()
        j = 2 * x + y
        for a in range(n):
            mine = g_refs[a].at[j, pl.ds(pl.multiple_of(c * halves[a], 16), halves[a]), :]
            for r in (1, 2, 3):
                tx, ty = _chip_of(x, y, r)
                for other in (0, 1):
                    k_out = ((a * 3 + (r - 1)) * 2 + c) * 2 + other
                    pltpu.make_async_remote_copy(
                        src_ref=mine, dst_ref=mine, send_sem=send_sems.at[k_out], recv_sem=recv_sems.at[k_out],
                        device_id=(tx, ty, other), device_id_type=MESH).wait_send()
                    k_in = ((a * 3 + (r - 1)) * 2 + other) * 2 + c
                    theirs = g_refs[a].at[2 * tx + ty, pl.ds(other * halves[a], halves[a]), :]
                    pltpu.make_async_remote_copy(
                        src_ref=theirs, dst_ref=theirs, send_sem=send_sems.at[k_in], recv_sem=recv_sems.at[k_in],
                        device_id=(tx, ty, other), device_id_type=MESH).wait_recv()

    hbm = lambda t: pltpu.HBM(t.shape, t.dtype)
    outs = pl.pallas_call(
        body,
        name=name,
        in_specs=[HBM] * n + [SEM, SEM, ANY],
        out_specs=[HBM] * n,
        out_shape=[hbm(t) for t in thru],
        input_output_aliases={i: i for i in range(n)},
        compiler_params=pltpu.CompilerParams(has_side_effects=DATAFLOW),
    )(*thru, send_sems, recv_sems, after)
    return list(outs)


D2D_PIECE_ROWS = 64
PAIR_SUM_TILE_BYTES = 2 * MIB


def _pair_sum(gs, gbs, chip_core, name):
    n = len(gs)
    nch, R, C = gs[0].shape
    h = R // 2
    tr = _row_tile(h, C * n, PAIR_SUM_TILE_BYTES)
    nt = h // tr
    rows = min(D2D_PIECE_ROWS, tr)

    def body(jc_ref, *refs):
        g_refs, gb_refs, p_refs, pb_refs = refs[0:n], refs[n:2 * n], refs[2 * n:3 * n], refs[3 * n:4 * n]
        got_refs, send_sems, recv_sems = refs[4 * n:5 * n], refs[5 * n], refs[5 * n + 1]
        i, j = pl.program_id(0), pl.program_id(1)
        x, y, c = _place()

        def copy(a, ti, tj, first, count):
            src_rows = pl.ds(pl.multiple_of((1 - c) * h + ti * tr + first, 16), count)
            dst_rows = pl.ds(pl.multiple_of(ti * tr + first, 16), count)
            return pltpu.make_async_remote_copy(
                src_ref=gb_refs[a].at[tj, src_rows, :], dst_ref=got_refs[a].at[tj, dst_rows, :],
                send_sem=send_sems.at[a, ti, tj], recv_sem=recv_sems.at[a, ti, tj],
                device_id=(x, y, 1 - c), device_id_type=MESH)

        @pl.when((i == 0) & (j == 0))
        def _():
            for ti in range(nt):
                for tj in range(nch):
                    for a in range(n):
                        for q in range(tr // rows):
                            copy(a, ti, tj, q * rows, rows).start()

        for a in range(n):
            copy(a, i, j, 0, tr).wait()
            s = g_refs[a][...] + got_refs[a][j, pl.ds(pl.multiple_of(i * tr, 16), tr), :].astype(F32)
            pb_refs[a][...] = s.astype(BF16)

            @pl.when(j == jc_ref[0])
            def _():
                p_refs[a][...] = s

    by_chip = pl.BlockSpec((None, tr, C), lambda i, j, jc_ref: (j, i, 0))
    outs = pl.pallas_call(
        body,
        name=name,
        grid_spec=pltpu.PrefetchScalarGridSpec(
            num_scalar_prefetch=1,
            grid=(nt, nch),
            in_specs=[pl.BlockSpec((None, tr, C), lambda i, j, jc_ref: (j, jc_ref[1] * nt + i, 0))] * n + [ANY] * n,
            out_specs=[pl.BlockSpec((tr, C), lambda i, j, jc_ref: (i, 0))] * n + [by_chip] * n,
            scratch_shapes=[pltpu.VMEM((nch, h, C), BF16)] * n + [pltpu.SemaphoreType.DMA((n, nt, nch))] * 2,
        ),
        out_shape=[_sds((h, C), F32)] * n + [_sds((nch, h, C), BF16)] * n,
        compiler_params=_params(("arbitrary", "arbitrary"), 48),
    )(chip_core, *_hbm(*gs, *gbs))
    return list(outs[:n]), list(outs[n:])


HBM = pl.BlockSpec(memory_space=pltpu.HBM)
SEM = pl.BlockSpec(memory_space=pltpu.SEMAPHORE)
DATAFLOW = pltpu.SideEffectType.DATAFLOW_SIDE_EFFECTING


def _chip_copy(p_refs, land_refs, send_sems, recv_sems, a, r, blocked):
    x, y, c = _place()
    tx, ty = _chip_of(x, y, r)
    k = a * 3 + (r - 1)
    return pltpu.make_async_remote_copy(
        src_ref=p_refs[a].at[2 * tx + ty] if blocked else p_refs[a], dst_ref=land_refs[a].at[r - 1],
        send_sem=send_sems.at[k], recv_sem=recv_sems.at[k], device_id=(tx, ty, c), device_id_type=MESH)


def _chip_exchange_start(psums, name, blocked=True):
    n = len(psums)
    lands = [lax.empty((3,) + (p.shape[1:] if blocked else p.shape), p.dtype) for p in psums]

    def body(*refs):
        p_refs, land_refs = refs[0:n], refs[n:2 * n]
        send_sems, recv_sems, token = refs[2 * n], refs[2 * n + 1], refs[-1]
        for a in range(n):
            for r in (1, 2, 3):
                _chip_copy(p_refs, land_refs, send_sems, recv_sems, a, r, blocked).start()
        token[...] = jnp.zeros_like(token)

    hbm = lambda t: pltpu.HBM(t.shape, t.dtype)
    keep = lambda t: pltpu.with_memory_space_constraint(t, pltpu.HBM)
    outs = pl.pallas_call(
        body,
        name=name,
        in_specs=[HBM] * (2 * n),
        out_specs=(SEM, SEM, *[HBM] * (2 * n), pl.BlockSpec(memory_space=pltpu.VMEM)),
        out_shape=(pltpu.SemaphoreType.DMA((3 * n,)), pltpu.SemaphoreType.DMA((3 * n,)),
                   *[hbm(p) for p in psums], *[hbm(l) for l in lands], _sds((8, 128), F32)),
        input_output_aliases={i: 2 + i for i in range(2 * n)},
        compiler_params=pltpu.CompilerParams(has_side_effects=DATAFLOW),
    )(*[keep(p) for p in psums], *[keep(l) for l in lands])
    return outs[0], outs[1], list(outs[2:2 + n]), list(outs[2 + n:2 + 2 * n]), outs[-1]


def _chip_exchange_wait(send_sems, recv_sems, p_thru, land_thru, after, name, blocked=True):
    n = len(p_thru)

    def body(*refs):
        p_refs, land_refs = refs[0:n], refs[n:2 * n]
        send_sems, recv_sems = refs[2 * n], refs[2 * n + 1]
        for a in range(n):
            for r in (1, 2, 3):
                copy = _chip_copy(p_refs, land_refs, send_sems, recv_sems, a, r, blocked)
                copy.wait_send()
                copy.wait_recv()

    hbm = lambda t: pltpu.HBM(t.shape, t.dtype)
    outs = pl.pallas_call(
        body,
        name=name,
        in_specs=[HBM] * (2 * n) + [SEM, SEM, ANY],
        out_specs=[HBM] * (2 * n),
        out_shape=[hbm(p) for p in p_thru] + [hbm(l) for l in land_thru],
        input_output_aliases={i: i for i in range(2 * n)},
        compiler_params=pltpu.CompilerParams(has_side_effects=DATAFLOW),
    )(*p_thru, *land_thru, send_sems, recv_sems, after)
    return list(outs[0:n]), list(outs[n:2 * n])


def _pair_share(fulls):
    n = len(fulls)
    halves = [f.shape[0] // 2 for f in fulls]

    def body(*refs):
        full_refs = refs[n:2 * n]
        send_sems, recv_sems = refs[2 * n:]
        x, y, c = _place()

        def half_of(a, core):
            return full_refs[a].at[pl.ds(pl.multiple_of(core * halves[a], 8), halves[a]), :]

        def remote(a, src, dst):
            return pltpu.make_async_remote_copy(
                src_ref=src, dst_ref=dst, send_sem=send_sems.at[a], recv_sem=recv_sems.at[a],
                device_id=(x, y, 1 - c), device_id_type=MESH)

        for a in range(n):
            for q in range(halves[a] // D2D_PIECE_ROWS):
                piece = full_refs[a].at[
                    pl.ds(pl.multiple_of(c * halves[a] + q * D2D_PIECE_ROWS, 8), D2D_PIECE_ROWS), :]
                remote(a, piece, piece).start()
        for a in range(n):
            remote(a, half_of(a, c), half_of(a, c)).wait_send()
            remote(a, half_of(a, 1 - c), half_of(a, 1 - c)).wait_recv()

    return pl.pallas_call(
        body,
        name="pair_share",
        in_specs=[ANY] * n,
        out_specs=[ANY] * n,
        out_shape=[_sds(f.shape, F32) for f in fulls],
        input_output_aliases={a: a for a in range(n)},
        scratch_shapes=[pltpu.SemaphoreType.DMA((n,)), pltpu.SemaphoreType.DMA((n,))],
    )(*fulls)


def _small_pair_sum(s):
    R, C = s.shape
    V = SMALL_VECTOR_ROWS

    def body(s_ref, v_ref, m_ref, sib, send_sem, recv_sem):
        x, y, c = _place()

        def to_sib(src, dst):
            return pltpu.make_async_remote_copy(
                src_ref=src, dst_ref=dst, send_sem=send_sem, recv_sem=recv_sem,
                device_id=(x, y, 1 - c), device_id_type=MESH)

        for q in range(R // 8):
            to_sib(s_ref.at[pl.ds(8 * q, 8), :], sib.at[pl.ds(8 * q, 8), :]).start()
        to_sib(s_ref, sib).wait()
        v_ref[...] = s_ref[pl.ds(0, V), :] + sib[pl.ds(0, V), :]
        m_ref[...] = (s_ref[pl.ds(V, R - V), :] + sib[pl.ds(V, R - V), :]).astype(BF16)

    return pl.pallas_call(
        body,
        name="small_pair_sum",
        in_specs=[pl.BlockSpec(memory_space=pltpu.VMEM)],
        out_specs=[pl.BlockSpec(memory_space=pltpu.VMEM)] * 2,
        out_shape=[jax.ShapeDtypeStruct((V, C), F32), jax.ShapeDtypeStruct((R - V, C), BF16)],
        scratch_shapes=[pltpu.VMEM((R, C), F32), pltpu.SemaphoreType.DMA, pltpu.SemaphoreType.DMA],
    )(s)


def _small_total(chip, own, landed):
    V, C = own[0].shape
    M = own[1].shape[0]

    def body(j_ref, v_ref, m_ref, lv_ref, lm_ref, o_ref, chips_v, chips_m):
        j = j_ref[0]
        chips_v[j] = v_ref[...]
        chips_m[j] = m_ref[...]
        for r in (1, 2, 3):
            chips_v[j ^ r] = lv_ref[r - 1]
            chips_m[j ^ r] = lm_ref[r - 1]
        o_ref[pl.ds(0, V), :] = (chips_v[0] + chips_v[1]) + (chips_v[2] + chips_v[3])
        o_ref[pl.ds(V, M), :] = (chips_m[0].astype(F32) + chips_m[1].astype(F32)) + (
            chips_m[2].astype(F32) + chips_m[3].astype(F32))

    vmem = pl.BlockSpec(memory_space=pltpu.VMEM)
    return pl.pallas_call(
        body,
        name="small_total",
        in_specs=[pl.BlockSpec(memory_space=pltpu.SMEM), vmem, vmem, vmem, vmem],
        out_specs=vmem,
        out_shape=jax.ShapeDtypeStruct((V + M, C), F32),
        scratch_shapes=[pltpu.VMEM((N_CHIPS, V, C), F32), pltpu.VMEM((N_CHIPS, M, C), BF16)],
    )(chip, own[0], own[1], landed[0], landed[1])


def _block_diag(w):
    w4 = w.reshape(4, 4, RNN_BLOCK_W, RNN_BLOCK_W)
    eye = jnp.eye(4, dtype=w.dtype)
    return jnp.einsum("jaik,ab->jaibk", w4, eye).reshape(4, RNN_TILE, RNN_TILE)


def _block_diag_part(d):
    d5 = d.reshape(4, 4, RNN_BLOCK_W, 4, RNN_BLOCK_W)
    return jnp.stack([d5[:, a, :, a, :] for a in range(4)], axis=1).reshape(RNN_BLOCKS, RNN_BLOCK_W, RNN_BLOCK_W)


def _local_grads(x, target, g_pre, w_in_g, b_gate, conv_w, conv_b, w_rg_a, b_rg_a, w_rg_x, b_rg_x, lam, sinks,
                 out_weights, fwd_token, g_post, on_out_grads, on_w_in_grad):
    wa_bd = _block_diag(w_rg_a).astype(BF16)
    wx_bd = _block_diag(w_rg_x).astype(BF16)
    b_a = b_rg_a.reshape(1, D_RNN)
    b_x = b_rg_x.reshape(1, D_RNN)

    proj, ht = _proj_fwd(x, g_pre, w_in_g)
    y_rnn, z_rnn, conv, z_rnn_t = _rnn_fwd(proj, conv_w, conv_b, wa_bd, wx_bd, b_a, b_x, lam, fwd_token)
    bias = _attn_bias()
    y_attn, z_attn, lse = _attn_fwd(proj, sinks, bias)
    w_rnn_out, w_attn_out, w_out = out_weights(z_attn)
    dyx, dz_rnn, dz_attn, dml, dout, dbr_rnn, dbr_attn, merged_t, z_attn_t, head_small = _head(
        x, target, z_rnn, z_attn, proj, b_gate, g_post, w_rnn_out, w_attn_out, w_out)
    out_grads = [_matmul_t(z_rnn_t, dbr_rnn, "dw_rnn_out"), _matmul_t(z_attn_t, dbr_attn, "dw_attn_out"),
                 _matmul_t(merged_t, dout, "dw_out")]
    shard_rows = lambda d: d.reshape(N_CHIPS, OUT_SHARD, D_MODEL)
    token = on_out_grads([shard_rows(g) for g, _ in out_grads], [shard_rows(gb) for _, gb in out_grads])
    dq, dk, dv, dag, attn_small = _attn_bwd(proj, y_attn, lse, dz_attn, sinks, bias, token)
    drx, drg, dwa_t, dwx_t, rnn_small = _rnn_bwd(proj, conv, y_rnn, dz_rnn, conv_w, wa_bd, wx_bd, b_a, b_x, lam)
    dproj = [drx, drg, dq, dk, dv, dag, dml]
    token = on_w_in_grad(*_dw_in(ht, dproj))
    grad_x, dh_small = _dh_bwd(dproj, w_in_g, x, dyx, g_pre, token)
    small = jnp.concatenate([rnn_small, head_small, dh_small + attn_small,
                             _block_diag_part(dwa_t).reshape(64, 1024), _block_diag_part(dwx_t).reshape(64, 1024)], axis=0)
    return grad_x, small


ROW_LOSS = 11


def _rows8(parts):
    out = None
    for r, a in parts:
        p = jnp.pad(a, ((r, 8 - r - a.shape[0]), (0, 1024 - a.shape[1])))
        out = p if out is None else out + p
    return out


def _pack_small(p):
    g0 = _rows8([(0, p["b_rg_a"].reshape(1, 1024)), (1, p["b_rg_x"].reshape(1, 1024)), (2, p["lru_lambda"]),
                 (3, p["conv_b"]), (4, p["conv_w"][0])])
    g1 = _rows8([(0, p["post_norm_g"]), (1, p["b_gate"].reshape(2, 1024))])
    g2 = _rows8([(0, p["pre_norm_g"]), (1, p["attn_sinks"])])
    return jnp.concatenate([g0, g1, g2, p["w_rg_a"].reshape(64, 1024), p["w_rg_x"].reshape(64, 1024)], axis=0)


def _unpack_small(s, conv_cols):
    return {
        "b_rg_a": s[0:1].reshape(1, 16, 64), "b_rg_x": s[1:2].reshape(1, 16, 64), "lru_lambda": s[2:3],
        "conv_b": s[3:4], "conv_w": s[4:8, 0:conv_cols].reshape(1, CONV_W, conv_cols),
        "post_norm_g": s[8:9], "b_gate": s[9:11].reshape(1, 2048),
        "pre_norm_g": s[16:17], "attn_sinks": s[17:18, 0:N_Q_HEADS],
        "w_rg_a": s[24:88].reshape(1, 16, 64, 64), "w_rg_x": s[88:152].reshape(1, 16, 64, 64),
    }


WEIGHTS = ["pre_norm_g", "w_in", "b_gate", "conv_w", "conv_b", "w_rg_a", "b_rg_a", "w_rg_x", "b_rg_x", "lru_lambda",
           "attn_sinks", "w_rnn_out", "w_attn_out", "w_out", "post_norm_g"]
BIG = ["w_in", "w_rnn_out", "w_attn_out", "w_out"]


def kernel(x, pre_norm_g, w_in, b_gate, conv_w, conv_b, w_rg_a, b_rg_a, w_rg_x, b_rg_x, lru_lambda, attn_sinks, w_rnn_out, w_attn_out, w_out, post_norm_g, loss_target, m_pre_norm_g, m_w_in, m_b_gate, m_conv_w, m_conv_b, m_w_rg_a, m_b_rg_a, m_w_rg_x, m_b_rg_x, m_lru_lambda, m_attn_sinks, m_w_rnn_out, m_w_attn_out, m_w_out, m_post_norm_g, v_pre_norm_g, v_w_in, v_b_gate, v_conv_w, v_conv_b, v_w_rg_a, v_b_rg_a, v_w_rg_x, v_b_rg_x, v_lru_lambda, v_attn_sinks, v_w_rnn_out, v_w_attn_out, v_w_out, v_post_norm_g):
    w = dict(pre_norm_g=pre_norm_g, w_in=w_in, b_gate=b_gate, conv_w=conv_w, conv_b=conv_b, w_rg_a=w_rg_a,
             b_rg_a=b_rg_a, w_rg_x=w_rg_x, b_rg_x=b_rg_x, lru_lambda=lru_lambda, attn_sinks=attn_sinks,
             w_rnn_out=w_rnn_out, w_attn_out=w_attn_out, w_out=w_out, post_norm_g=post_norm_g)
    m = dict(pre_norm_g=m_pre_norm_g, w_in=m_w_in, b_gate=m_b_gate, conv_w=m_conv_w, conv_b=m_conv_b, w_rg_a=m_w_rg_a,
             b_rg_a=m_b_rg_a, w_rg_x=m_w_rg_x, b_rg_x=m_b_rg_x, lru_lambda=m_lru_lambda, attn_sinks=m_attn_sinks,
             w_rnn_out=m_w_rnn_out, w_attn_out=m_w_attn_out, w_out=m_w_out, post_norm_g=m_post_norm_g)
    v = dict(pre_norm_g=v_pre_norm_g, w_in=v_w_in, b_gate=v_b_gate, conv_w=v_conv_w, conv_b=v_conv_b, w_rg_a=v_w_rg_a,
             b_rg_a=v_b_rg_a, w_rg_x=v_w_rg_x, b_rg_x=v_b_rg_x, lru_lambda=v_lru_lambda, attn_sinks=v_attn_sinks,
             w_rnn_out=v_w_rnn_out, w_attn_out=v_w_attn_out, w_out=v_w_out, post_norm_g=v_post_norm_g)
    chip = 2 * lax.axis_index("x") + lax.axis_index("y")

    chip_idx = chip.astype(jnp.int32).reshape(1)
    chip_core = jnp.stack([chip, lax.axis_index("c")]).astype(jnp.int32)
    cw8 = jnp.pad(conv_w[0], ((0, 8 - CONV_W), (0, 0)))
    placed = _place_shards([w_in[0], w_rnn_out[0], w_attn_out[0], w_out[0]], chip_idx, "place_shards")
    win_g, cw_g = _gather_weights(placed[:1], cw8)
    late_send, late_recv, late_thru, late_token = _gather_late_start(placed[1:], win_g, "gather_late_start")
    cw_g = lax.dynamic_update_slice_in_dim(cw_g, cw8[None], chip, axis=0)
    conv_w_full = jnp.transpose(cw_g[:, 0:CONV_W, :], (1, 0, 2)).reshape(CONV_W, D_RNN)

    started = {}

    def start_reduction(tag, grads, grads_b16):
        psums, psums_b16 = _pair_sum(grads, grads_b16, chip_core, "pair_sum_" + tag)
        send_sems, recv_sems, p_thru, land_thru, token = _chip_exchange_start(psums_b16, "chip_exchange_start_" + tag)
        started[tag] = (psums, send_sems, recv_sems, p_thru, land_thru)
        return token

    def end_reduction(tag, after):
        psums, send_sems, recv_sems, p_thru, land_thru = started[tag]
        _, landed = _chip_exchange_wait(send_sems, recv_sems, p_thru, land_thru, after, "chip_exchange_wait_" + tag)
        return _chip_sum(psums, landed, chip_core, "chip_sum_" + tag)

    def out_weights(after):
        gathered = _gather_late_wait(late_send, late_recv, late_thru, after, "gather_late_wait")
        return [g.reshape(D_MODEL, D_MODEL) for g in gathered]

    grad_x, small = _local_grads(
        x[0], loss_target[0], pre_norm_g, win_g, b_gate, conv_w_full, conv_b, w_rg_a[0], b_rg_a[0], w_rg_x[0],
        b_rg_x[0], lru_lambda, attn_sinks[0], out_weights, late_token, post_norm_g,
        on_out_grads=lambda grads, grads_b16: start_reduction("out", grads, grads_b16),
        on_w_in_grad=lambda grad, grad_b16: start_reduction("in", [grad], [grad_b16]))

    small_chip = _small_pair_sum(small)
    small_send, small_recv, small_thru, small_land, small_token = _chip_exchange_start(
        list(small_chip), "small_exchange_start", blocked=False)

    halves = end_reduction("in", small_token) + end_reduction("out", small_token)
    gbig = dict(zip(BIG, _pair_share(halves)))

    grads, delta, new_m, new_v = {}, {}, {}, {}
    for names, tag in ((BIG[:1], "adamw_in"), (BIG[1:], "adamw_out")):
        updates = _adamw([(w[n][0], gbig[n], m[n][0], v[n][0]) for n in names], tag, with_grads=True)
        for n, (d, nm, nv, g) in zip(names, updates):
            grads[n], delta[n], new_m[n], new_v[n] = g[None], d[None], nm[None], nv[None]

    small_own, small_landed = _chip_exchange_wait(small_send, small_recv, small_thru, small_land, delta[BIG[-1]],
                                                  "small_exchange_wait", blocked=False)
    small_sum = _small_total(chip_idx, small_own, small_landed)
    total_loss = small_sum[ROW_LOSS, 0]
    gsmall = _unpack_small(small_sum, D_RNN)
    conv_shard = D_RNN // N_CHIPS
    gsmall["conv_w"] = lax.dynamic_slice_in_dim(gsmall["conv_w"], chip * conv_shard, conv_shard, axis=2)
    pick = lambda t: {k: t[k] for k in gsmall}
    (d, nm, nv), = _adamw([(_pack_small(pick(w)), _pack_small(gsmall), _pack_small(pick(m)), _pack_small(pick(v)))],
                          "adamw_small", with_grads=False)
    ud, um, uv = _unpack_small(d, conv_shard), _unpack_small(nm, conv_shard), _unpack_small(nv, conv_shard)
    for n in gsmall:
        grads[n] = gsmall[n].reshape(w[n].shape)
        delta[n] = ud[n].reshape(w[n].shape)
        new_m[n] = um[n].reshape(w[n].shape)
        new_v[n] = uv[n].reshape(w[n].shape)

    return (total_loss, grad_x[None], *[grads[n] for n in WEIGHTS], *[delta[n] for n in WEIGHTS],
            *[new_m[n] for n in WEIGHTS], *[new_v[n] for n in WEIGHTS])
```

```python
import functools
import math

import jax
import jax.numpy as jnp
from jax import lax
from jax.experimental import pallas as pl
from jax.experimental.pallas import tpu as pltpu

F32 = jnp.float32
BF16 = jnp.bfloat16

D_MODEL = 1024
D_RNN = 1024
RNN_BLOCKS = 16
RNN_BLOCK_W = 64
CONV_W = 4
LRU_C = 8.0
N_Q_HEADS = 16
N_KV_HEADS = 4
GROUP = 4
HEAD_DIM = 64
D_KV = 256
BLOCK = 128
ALIBI_MAX_BIAS = 8.0
EPS = 1e-6
D_IN = 6656
N_CHIPS = 4
W_IN_SHARD = D_IN // N_CHIPS
OUT_SHARD = D_MODEL // N_CHIPS
ADAM_LR = 0.001
ADAM_B1 = 0.9
ADAM_B2 = 0.999
ADAM_EPS = 1e-08
ADAM_WD = 0.01
ADAM_STEP = 10
NEG_BIG = -1e30
MIB = 1 << 20

COL_RNN_X = 0
COL_RNN_GATE = 4
COL_Q = 8
COL_K = 12
COL_V = 13
COL_ATTN_GATE = 14
COL_MERGE = 18

RNN_TILE = 256
RNN_CHUNK = 512
SMALL_ROWS = 152
SMALL_VECTOR_ROWS = 24
MESH = pl.DeviceIdType.MESH


def _sds(shape, dtype):
    return pltpu.HBM(shape, dtype)


def _params(sem=None, vmem_mib=None):
    kw = {}
    if sem is not None:
        kw["dimension_semantics"] = sem
    if vmem_mib is not None:
        kw["vmem_limit_bytes"] = vmem_mib * MIB
    return pltpu.CompilerParams(**kw)


def _hbm(*arrays):
    return [pltpu.with_memory_space_constraint(a, pltpu.HBM) for a in arrays]


def _dot(a, b):
    return jnp.dot(a, b, preferred_element_type=F32)


def _dot_nt(a, b):
    return lax.dot_general(a, b, (((1,), (1,)), ((), ())), preferred_element_type=F32)


def _dot_tn(a, b):
    return lax.dot_general(a, b, (((0,), (0,)), ((), ())), preferred_element_type=F32)


def _sigmoid(x):
    return 0.5 * jnp.tanh(0.5 * x) + 0.5


def _sigmoid_small(x):
    return 1.0 / (1.0 + jnp.exp(-x))


def _softplus(x):
    return jnp.maximum(x, 0.0) + jnp.log(1.0 + jnp.exp(-jnp.abs(x)))


def _one_minus_square(a, log_a):
    return -jnp.tanh(log_a) * (a * a + 1.0)


def _proj_fwd(x, g_pre, w_in_g):
    T = x.shape[0]
    tm = min(1024, T)

    def body(x_ref, g_ref, w_ref, proj_ref, ht_ref, h_s):
        @pl.when(pl.program_id(1) == 0)
        def _():
            xv = x_ref[...]
            rstd = lax.rsqrt(jnp.mean(xv * xv, axis=-1, keepdims=True) + EPS)
            hf = (xv * rstd) * g_ref[...]
            h_s[...] = hf.astype(BF16)
            ht_ref[...] = hf.T.astype(BF16)

        proj_ref[...] = _dot(h_s[...], w_ref[...]).astype(BF16)

    return pl.pallas_call(
        body,
        name="proj_fwd",
        grid=(T // tm, N_CHIPS),
        in_specs=[
            pl.BlockSpec((tm, D_MODEL), lambda i, j: (i, 0)),
            pl.BlockSpec((1, D_MODEL), lambda i, j: (0, 0)),
            pl.BlockSpec((None, D_MODEL, W_IN_SHARD), lambda i, j: (j, 0, 0)),
        ],
        out_specs=[
            pl.BlockSpec((tm, W_IN_SHARD), lambda i, j: (i, j)),
            pl.BlockSpec((D_MODEL, tm), lambda i, j: (0, i)),
        ],
        out_shape=[_sds((T, D_IN), BF16), _sds((D_MODEL, T), BF16)],
        scratch_shapes=[pltpu.VMEM((tm, D_MODEL), BF16)],
        compiler_params=_params(("parallel", "arbitrary"), 48),
    )(*_hbm(x, g_pre, w_in_g))


def _shift_down(x, tail, s, row):
    n = x.shape[0]
    xs = pltpu.roll(x, s, 0)
    tail_t = jnp.tile(pltpu.roll(tail, s, 0), (n // 8, 1))
    return jnp.where(row < s, tail_t, xs)


def _shift_up(x, head, s, row):
    n = x.shape[0]
    xs = pltpu.roll(x, n - s, 0)
    head_t = jnp.tile(pltpu.roll(head, 8 - s, 0), (n // 8, 1))
    return jnp.where(row >= n - s, head_t, xs)


def _conv_taps(x, tail, row):
    return [_shift_down(x, tail, 3, row), _shift_down(x, tail, 2, row), _shift_down(x, tail, 1, row), x]


def _rglru_gates(c, wa, wx, ba, bx, lam):
    cb = c.astype(BF16)
    r = _sigmoid_small(_dot(cb, wa) + ba)
    i = _sigmoid(_dot(cb, wx) + bx)
    log_a = (-LRU_C) * r * _softplus(-lam)
    a = jnp.exp(log_a)
    w = _one_minus_square(a, log_a)
    inv_mult = lax.rsqrt(w)
    return cb, r, i, a, w * inv_mult, inv_mult


SUBLANES = 8


def _scan_down(a, u, row):
    n = a.shape[0]
    s = 1
    while s < SUBLANES:
        a_sh = jnp.where(row >= s, pltpu.roll(a, s, 0), 1.0)
        u_sh = jnp.where(row >= s, pltpu.roll(u, s, 0), 0.0)
        u = a * u_sh + u
        a = a * a_sh
        s *= 2
    while s < n:
        u = jnp.concatenate([u[:s], a[s:] * u[:n - s] + u[s:]], axis=0)
        a = jnp.concatenate([a[:s], a[s:] * a[:n - s]], axis=0)
        s *= 2
    return a, u


def _scan_up(b, u, row):
    n = b.shape[0]
    s = 1
    while s < SUBLANES:
        b_sh = jnp.where(row < n - s, pltpu.roll(b, n - s, 0), 1.0)
        u_sh = jnp.where(row < n - s, pltpu.roll(u, n - s, 0), 0.0)
        u = b * u_sh + u
        b = b * b_sh
        s *= 2
    while s < n:
        u = jnp.concatenate([b[:n - s] * u[s:] + u[:n - s], u[n - s:]], axis=0)
        b = jnp.concatenate([b[:n - s] * b[s:], b[n - s:]], axis=0)
        s *= 2
    return b, u


LANES = 128


def _chunk_scan(a, u, a_s, u_s, hl_s, al_s, carry, reverse):
    n, width = a.shape
    groups = n // SUBLANES
    order = range(SUBLANES - 1, -1, -1) if reverse else range(SUBLANES)
    row = lax.broadcasted_iota(jnp.int32, (groups, LANES), 0)
    for l in range(width // LANES):
        lanes = slice(l * LANES, (l + 1) * LANES)
        a_l, u_l, hl_l, al_l = a_s.at[l], u_s.at[l], hl_s.at[l], al_s.at[l]
        a_l[...] = a[:, lanes]
        u_l[...] = u[:, lanes]
        h_loc = a_loc = None
        for r in order:
            rows = pl.ds(r, groups, stride=SUBLANES)
            a_r, u_r = a_l[rows, :], u_l[rows, :]
            h_loc, a_loc = (u_r, a_r) if h_loc is None else (a_r * h_loc + u_r, a_r * a_loc)
            hl_l[rows, :] = h_loc
            al_l[rows, :] = a_loc
        if reverse:
            a_cum, ends = _scan_up(a_loc, h_loc, row)
            ends = ends + a_cum * carry[:, lanes]
            enters = jnp.where(row == groups - 1, carry[:, lanes], pltpu.roll(ends, groups - 1, 0))
        else:
            a_cum, ends = _scan_down(a_loc, h_loc, row)
            ends = ends + a_cum * carry[:, lanes]
            enters = jnp.where(row == 0, carry[:, lanes], pltpu.roll(ends, 1, 0))
        for r in range(SUBLANES):
            rows = pl.ds(r, groups, stride=SUBLANES)
            hl_l[rows, :] = hl_l[rows, :] + al_l[rows, :] * enters
    return jnp.concatenate([hl_s[l] for l in range(width // LANES)], axis=1)


def _rnn_fwd(proj, conv_w, conv_b, wa_bd, wx_bd, b_a, b_x, lam, token):
    T = proj.shape[0]
    tc, ct = RNN_CHUNK, RNN_TILE
    nt = T // tc

    def body(x_ref, rg_ref, cw_ref, cb_ref, wa_ref, wx_ref, ba_ref, bx_ref, lam_ref, token_ref, h_ref, z_ref, c_ref,
             zt_ref, xtail, hcarry, a_s, u_s, hl_s, al_s):
        @pl.when(pl.program_id(1) == 0)
        def _():
            xtail[...] = jnp.zeros_like(xtail)
            hcarry[...] = jnp.zeros_like(hcarry)

        row = lax.broadcasted_iota(jnp.int32, (tc, ct), 0)
        x = x_ref[...].astype(F32)
        taps = _conv_taps(x, xtail[...], row)
        c = cb_ref[...] + cw_ref[pl.ds(0, 1), :] * taps[0]
        for k in range(1, CONV_W):
            c = c + cw_ref[pl.ds(k, 1), :] * taps[k]
        xtail[...] = x[tc - 8:, :]
        c_ref[...] = c
        _, _, i, a, mult, _ = _rglru_gates(c, wa_ref[...], wx_ref[...], ba_ref[...], bx_ref[...], lam_ref[...])
        h = _chunk_scan(a, mult * (i * c), a_s, u_s, hl_s, al_s, hcarry[...], reverse=False)
        h_ref[...] = h
        hcarry[...] = h_ref[pl.ds(tc - 1, 1), :]
        rg = rg_ref[...].astype(F32)
        z = h * (rg * _sigmoid(rg))
        z_ref[...] = z.astype(BF16)
        zt_ref[...] = z.T.astype(BF16)

    col = lambda off: (lambda j, t: (t, off + j))
    vec = pl.BlockSpec((1, ct), lambda j, t: (0, j))
    mat = pl.BlockSpec((None, ct, ct), lambda j, t: (j, 0, 0))
    return pl.pallas_call(
        body,
        name="rnn_fwd",
        grid=(D_RNN // ct, nt),
        in_specs=[
            pl.BlockSpec((tc, ct), col(COL_RNN_X)),
            pl.BlockSpec((tc, ct), col(COL_RNN_GATE)),
            pl.BlockSpec((CONV_W, ct), lambda j, t: (0, j)),
            vec, mat, mat, vec, vec, vec,
            pl.BlockSpec((8, 128), lambda j, t: (0, 0)),
        ],
        out_specs=[pl.BlockSpec((tc, ct), lambda j, t: (t, j))] * 3 + [pl.BlockSpec((ct, tc), lambda j, t: (j, t))],
        out_shape=[_sds((T, D_RNN), F32), _sds((T, D_RNN), BF16), _sds((T, D_RNN), F32), _sds((D_RNN, T), BF16)],
        scratch_shapes=[pltpu.VMEM((8, ct), F32), pltpu.VMEM((1, ct), F32)] + [
            pltpu.VMEM((ct // LANES, tc, LANES), F32)] * 4,
        compiler_params=_params(("parallel", "arbitrary"), 32),
    )(*_hbm(proj, proj, conv_w, conv_b, wa_bd, wx_bd, b_a, b_x, lam, token))


def _rnn_bwd(proj, conv, y_rnn, dz_rnn, conv_w, wa_bd, wx_bd, b_a, b_x, lam):
    T = proj.shape[0]
    tc, ct = RNN_CHUNK, RNN_TILE
    nt = T // tc
    hb = tc // 8

    def body(x_ref, c_ref, rg_ref, h_ref, hh_ref, dz_ref, cw_ref, wa_ref, wx_ref, ba_ref, bx_ref, lam_ref,
             dx_ref, drg_ref, dwa_ref, dwx_ref, sm_ref, lam_carry, a_carry, dc_head, b_s, dy_s, hl_s, al_s):
        t = pl.program_id(1)
        first_chunk = t == nt - 1

        @pl.when(t == 0)
        def _():
            lam_carry[...] = jnp.zeros_like(lam_carry)
            a_carry[...] = jnp.zeros_like(a_carry)
            dc_head[...] = jnp.zeros_like(dc_head)
            dwa_ref[...] = jnp.zeros_like(dwa_ref)
            dwx_ref[...] = jnp.zeros_like(dwx_ref)
            sm_ref[...] = jnp.zeros_like(sm_ref)

        row = lax.broadcasted_iota(jnp.int32, (tc, ct), 0)
        keep = jnp.where(first_chunk, 0.0, 1.0)
        x = x_ref[...].astype(F32)
        c = c_ref[...]
        lam = lam_ref[...]
        cb, r, i, a, mult, inv_mult = _rglru_gates(c, wa_ref[...], wx_ref[...], ba_ref[...], bx_ref[...], lam)
        h = h_ref[...]
        h_prev = _shift_down(h, hh_ref[...] * keep, 1, row)
        rg = rg_ref[...].astype(F32)
        dz = dz_ref[...]
        sg = _sigmoid(rg)
        drg_ref[...] = (dz * h * (sg * (1.0 + rg * (1.0 - sg)))).astype(BF16)
        dy = dz * (rg * sg)
        b = jnp.where(row >= tc - 1, a_carry[pl.ds(0, 1), :], pltpu.roll(a, tc - 1, 0))
        lt = _chunk_scan(b, dy, b_s, dy_s, hl_s, al_s, lam_carry[pl.ds(0, 1), :], reverse=True)
        lam_carry[...] = lt[0:8, :]
        a_carry[...] = a[0:8, :]
        ic = i * c
        dmult = lt * ic
        di = lt * mult * c
        dc = lt * mult * i
        dlog_a = a * (lt * h_prev - dmult * a * inv_mult)
        sp = _softplus(-lam)
        dpre_r = dlog_a * ((-LRU_C) * sp) * (r * (1.0 - r))
        dpre_i = di * (i * (1.0 - i))
        dlam_row = jnp.sum(dlog_a * r, axis=0, keepdims=True) * (LRU_C * _sigmoid(-lam))
        dpr_b = dpre_r.astype(BF16)
        dpi_b = dpre_i.astype(BF16)
        dwa_ref[...] += _dot_tn(cb, dpr_b)
        dwx_ref[...] += _dot_tn(cb, dpi_b)
        dc = dc + _dot_nt(dpr_b, wa_ref[...]) + _dot_nt(dpi_b, wx_ref[...])
        head = dc_head[...]
        dx = cw_ref[pl.ds(3, 1), :] * dc
        sm_ref[pl.ds(4 + 3, 1), :] += jnp.sum(dc * x, axis=0, keepdims=True)
        for m in range(1, CONV_W):
            up = _shift_up(dc, head, m, row)
            dx = dx + cw_ref[pl.ds(3 - m, 1), :] * up
            sm_ref[pl.ds(4 + 3 - m, 1), :] += jnp.sum(up * x, axis=0, keepdims=True)
        dx_ref[...] = dx.astype(BF16)
        dc_head[...] = dc[0:8, :]
        sm_ref[pl.ds(0, 1), :] += jnp.sum(dpre_r, axis=0, keepdims=True)
        sm_ref[pl.ds(1, 1), :] += jnp.sum(dpre_i, axis=0, keepdims=True)
        sm_ref[pl.ds(2, 1), :] += dlam_row
        sm_ref[pl.ds(3, 1), :] += jnp.sum(dc, axis=0, keepdims=True)

    rev = lambda off: (lambda j, t: (nt - 1 - t, off + j))
    halo = lambda off: (lambda j, t: (jnp.maximum((nt - 1 - t) * hb - 1, 0), off + j))
    vec = pl.BlockSpec((1, ct), lambda j, t: (0, j))
    mat = pl.BlockSpec((None, ct, ct), lambda j, t: (j, 0, 0))
    return pl.pallas_call(
        body,
        name="rnn_bwd",
        grid=(D_RNN // ct, nt),
        in_specs=[
            pl.BlockSpec((tc, ct), rev(COL_RNN_X)),
            pl.BlockSpec((tc, ct), rev(0)),
            pl.BlockSpec((tc, ct), rev(COL_RNN_GATE)),
            pl.BlockSpec((tc, ct), rev(0)),
            pl.BlockSpec((8, ct), halo(0)),
            pl.BlockSpec((tc, ct), rev(0)),
            pl.BlockSpec((CONV_W, ct), lambda j, t: (0, j)),
            mat, mat, vec, vec, vec,
        ],
        out_specs=[
            pl.BlockSpec((tc, ct), rev(0)),
            pl.BlockSpec((tc, ct), rev(0)),
            mat, mat,
            pl.BlockSpec((8, ct), lambda j, t: (0, j)),
        ],
        out_shape=[_sds((T, D_RNN), BF16), _sds((T, D_RNN), BF16), _sds((D_RNN // ct, ct, ct), F32),
                   _sds((D_RNN // ct, ct, ct), F32), _sds((8, D_RNN), F32)],
        scratch_shapes=[pltpu.VMEM((8, ct), F32)] * 3 + [pltpu.VMEM((ct // LANES, tc, LANES), F32)] * 4,
        compiler_params=_params(("parallel", "arbitrary"), 32),
    )(*_hbm(proj, conv, proj, y_rnn, y_rnn, dz_rnn, conv_w, wa_bd, wx_bd, b_a, b_x, lam))


def _attn_bias():
    qi = jnp.arange(BLOCK)[:, None]
    kj = jnp.arange(BLOCK)[None, :]
    dist_cur = (qi - kj).astype(F32)
    slopes = 2.0 ** (-ALIBI_MAX_BIAS * jnp.arange(1, N_Q_HEADS + 1, dtype=F32) / N_Q_HEADS)
    slopes = slopes[:, None, None]
    prev = jnp.where(kj > qi, -slopes * (dist_cur + float(BLOCK)), NEG_BIG)
    cur = jnp.where(kj <= qi, -slopes * dist_cur, NEG_BIG)
    later = jnp.concatenate([prev, cur], axis=-1)
    first = jnp.concatenate([jnp.full_like(prev, NEG_BIG), cur], axis=-1)
    return jnp.stack([first, later])


def _attn_exps(s_prev, s_cur, sink, bias):
    s_prev = s_prev + bias[:, 0:BLOCK]
    s_cur = s_cur + bias[:, BLOCK:2 * BLOCK]
    m = jnp.maximum(jnp.max(jnp.maximum(s_prev, s_cur), axis=-1, keepdims=True), sink)
    p_prev = jnp.exp(s_prev - m)
    p_cur = jnp.exp(s_cur - m)
    total = jnp.sum(p_prev + p_cur, axis=-1, keepdims=True) + jnp.exp(sink - m)
    return p_prev, p_cur, 1.0 / total, m + jnp.log(total)


def _attn_probs(s_prev, s_cur, sink, bias, lse):
    p_prev = jnp.exp((s_prev + bias[:, 0:BLOCK]) - lse)
    p_cur = jnp.exp((s_cur + bias[:, BLOCK:2 * BLOCK]) - lse)
    return p_prev, p_cur, jnp.exp(sink - lse)


def _stack_heads(ref_or_val, hk, dtype):
    parts = [ref_or_val[:, (GROUP * hk + g) * HEAD_DIM:(GROUP * hk + g + 1) * HEAD_DIM] for g in range(GROUP)]
    return jnp.concatenate(parts, axis=0).astype(dtype)


ATTN_SCALE = HEAD_DIM ** -0.5


def _bias_spec():
    return pl.BlockSpec((None, N_Q_HEADS, BLOCK, 2 * BLOCK), lambda i: (jnp.minimum(i, 1), 0, 0, 0))


def _attn_fwd(proj, sinks, bias):
    T = proj.shape[0]
    nb = T // BLOCK

    def body(sink_ref, bias_ref, q_ref, kp_ref, kc_ref, vp_ref, vc_ref, ag0_ref, ag1_ref, y_ref, z_ref, lse_ref):
        kvs = [slice(hk * HEAD_DIM, (hk + 1) * HEAD_DIM) for hk in range(N_KV_HEADS)]
        qgs = [(_stack_heads(q_ref, hk, F32) * ATTN_SCALE).astype(BF16) for hk in range(N_KV_HEADS)]
        s_prev = [_dot_nt(qgs[hk], kp_ref[:, kvs[hk]].astype(BF16)) for hk in range(N_KV_HEADS)]
        s_cur = [_dot_nt(qgs[hk], kc_ref[:, kvs[hk]].astype(BF16)) for hk in range(N_KV_HEADS)]
        for hk in range(N_KV_HEADS):
            pp, pc, invs = [], [], []
            for g in range(GROUP):
                h = GROUP * hk + g
                rows = slice(g * BLOCK, (g + 1) * BLOCK)
                p_prev, p_cur, inv, lse = _attn_exps(s_prev[hk][rows], s_cur[hk][rows], sink_ref[h], bias_ref[h])
                pp.append(p_prev.astype(BF16))
                pc.append(p_cur.astype(BF16))
                invs.append(inv)
                lse_ref[:, h:h + 1] = lse
            og = _dot(jnp.concatenate(pp, axis=0), vp_ref[:, kvs[hk]].astype(BF16)) + _dot(
                jnp.concatenate(pc, axis=0), vc_ref[:, kvs[hk]].astype(BF16))
            for g in range(GROUP):
                h = GROUP * hk + g
                y_ref[:, h * HEAD_DIM:(h + 1) * HEAD_DIM] = og[g * BLOCK:(g + 1) * BLOCK] * invs[g]
        ag = jnp.concatenate([ag0_ref[...], ag1_ref[...]], axis=1).astype(F32)
        z_ref[...] = (y_ref[...] * (ag * _sigmoid(ag))).astype(BF16)

    prev = lambda c: (lambda i: (jnp.maximum(i - 1, 0), c))
    cur = lambda c: (lambda i: (i, c))
    return pl.pallas_call(
        body,
        name="attn_fwd",
        grid=(nb,),
        in_specs=[
            pl.BlockSpec(memory_space=pltpu.SMEM),
            _bias_spec(),
            pl.BlockSpec((BLOCK, 1024), lambda i: (i, COL_Q // 4)),
            pl.BlockSpec((BLOCK, D_KV), prev(COL_K)),
            pl.BlockSpec((BLOCK, D_KV), cur(COL_K)),
            pl.BlockSpec((BLOCK, D_KV), prev(COL_V)),
            pl.BlockSpec((BLOCK, D_KV), cur(COL_V)),
            pl.BlockSpec((BLOCK, 512), lambda i: (i, COL_ATTN_GATE // 2)),
            pl.BlockSpec((BLOCK, 512), lambda i: (i, COL_ATTN_GATE // 2 + 1)),
        ],
        out_specs=[pl.BlockSpec((BLOCK, 1024), lambda i: (i, 0)), pl.BlockSpec((BLOCK, 1024), lambda i: (i, 0)),
                   pl.BlockSpec((BLOCK, N_Q_HEADS), lambda i: (i, 0))],
        out_shape=[_sds((T, 1024), F32), _sds((T, 1024), BF16), _sds((T, N_Q_HEADS), F32)],
        compiler_params=_params(("arbitrary",), 32),
    )(sinks, *_hbm(bias, proj, proj, proj, proj, proj, proj, proj))


def _attn_bwd(proj, y_attn, lse, dz_attn, sinks, bias, token):
    T = proj.shape[0]
    nb = T // BLOCK

    def body(sink_ref, bias_ref, q_ref, kp_ref, kc_ref, vp_ref, vc_ref, ag0_ref, ag1_ref, y_ref, lse_ref, dz_ref,
             token_ref, dq_ref, dk_ref, dv_ref, dag_ref, ds_ref, dy_s):
        i = pl.program_id(0)

        @pl.when(i == 0)
        def _():
            ds_ref[...] = jnp.zeros_like(ds_ref)

        lane = lax.broadcasted_iota(jnp.int32, (8, 128), 1)
        sub = lax.broadcasted_iota(jnp.int32, (8, 128), 0)
        ag = jnp.concatenate([ag0_ref[...], ag1_ref[...]], axis=1).astype(F32)
        dz = dz_ref[...]
        sg = _sigmoid(ag)
        dag_ref[...] = (dz * y_ref[...] * (sg * (1.0 + ag * (1.0 - sg)))).astype(BF16)
        dy_s[...] = dz * (ag * sg)
        r_cur = pl.multiple_of(i * BLOCK, BLOCK)
        r_prev = pl.multiple_of(jnp.maximum(i - 1, 0) * BLOCK, BLOCK)
        dk_cur, dv_cur, dk_prev, dv_prev = [], [], [], []
        ds_acc = jnp.zeros((8, 128), F32)
        for hk in range(N_KV_HEADS):
            ks = slice(hk * HEAD_DIM, (hk + 1) * HEAD_DIM)
            qg = (_stack_heads(q_ref, hk, F32) * ATTN_SCALE).astype(BF16)
            dog = _stack_heads(dy_s, hk, F32)
            og = _stack_heads(y_ref, hk, F32)
            dog_b = dog.astype(BF16)
            kp = kp_ref[:, ks].astype(BF16)
            kc = kc_ref[:, ks].astype(BF16)
            vp = vp_ref[:, ks].astype(BF16)
            vc = vc_ref[:, ks].astype(BF16)
            s_prev = _dot_nt(qg, kp)
            s_cur = _dot_nt(qg, kc)
            dp_prev = _dot_nt(dog_b, vp)
            dp_cur = _dot_nt(dog_b, vc)
            dvec = jnp.sum(dog * og, axis=-1, keepdims=True)
            pp, pc, dsp, dsc = [], [], [], []
            for g in range(GROUP):
                h = GROUP * hk + g
                rows = slice(g * BLOCK, (g + 1) * BLOCK)
                p_prev, p_cur, p_sink = _attn_probs(
                    s_prev[rows], s_cur[rows], sink_ref[h], bias_ref[h], lse_ref[:, h:h + 1])
                d_h = dvec[rows]
                pp.append(p_prev.astype(BF16))
                pc.append(p_cur.astype(BF16))
                dsp.append((p_prev * (dp_prev[rows] - d_h)).astype(BF16))
                dsc.append((p_cur * (dp_cur[rows] - d_h)).astype(BF16))
                dsink = -jnp.sum(p_sink * d_h, axis=0, keepdims=True)
                ds_acc = ds_acc + jnp.where(jnp.logical_and(lane == h, sub == 1), dsink, 0.0)
            pp = jnp.concatenate(pp, axis=0)
            pc = jnp.concatenate(pc, axis=0)
            dsp = jnp.concatenate(dsp, axis=0)
            dsc = jnp.concatenate(dsc, axis=0)
            dqg = (_dot(dsp, kp) + _dot(dsc, kc)) * ATTN_SCALE
            for g in range(GROUP):
                h = GROUP * hk + g
                dq_ref[:, h * HEAD_DIM:(h + 1) * HEAD_DIM] = dqg[g * BLOCK:(g + 1) * BLOCK].astype(BF16)
            dk_ref[pl.ds(r_cur, BLOCK), ks] = _dot_tn(dsc, qg)
            dv_ref[pl.ds(r_cur, BLOCK), ks] = _dot_tn(pc, dog_b)
            dk_prev.append(_dot_tn(dsp, qg))
            dv_prev.append(_dot_tn(pp, dog_b))
        ds_ref[:, 0:128] += ds_acc

        @pl.when(i > 0)
        def _():
            for hk in range(N_KV_HEADS):
                ks = slice(hk * HEAD_DIM, (hk + 1) * HEAD_DIM)
                dk_ref[pl.ds(r_prev, BLOCK), ks] += dk_prev[hk]
                dv_ref[pl.ds(r_prev, BLOCK), ks] += dv_prev[hk]

    prev = lambda c: (lambda i: (jnp.maximum(i - 1, 0), c))
    cur = lambda c: (lambda i: (i, c))
    blk = pl.BlockSpec((BLOCK, 1024), lambda i: (i, 0))
    whole = pl.BlockSpec((T, D_KV), lambda i: (0, 0))
    return pl.pallas_call(
        body,
        name="attn_bwd",
        grid=(nb,),
        in_specs=[
            pl.BlockSpec(memory_space=pltpu.SMEM),
            _bias_spec(),
            pl.BlockSpec((BLOCK, 1024), lambda i: (i, COL_Q // 4)),
            pl.BlockSpec((BLOCK, D_KV), prev(COL_K)),
            pl.BlockSpec((BLOCK, D_KV), cur(COL_K)),
            pl.BlockSpec((BLOCK, D_KV), prev(COL_V)),
            pl.BlockSpec((BLOCK, D_KV), cur(COL_V)),
            pl.BlockSpec((BLOCK, 512), lambda i: (i, COL_ATTN_GATE // 2)),
            pl.BlockSpec((BLOCK, 512), lambda i: (i, COL_ATTN_GATE // 2 + 1)),
            blk,
            pl.BlockSpec((BLOCK, N_Q_HEADS), lambda i: (i, 0)),
            blk,
            pl.BlockSpec((8, 128), lambda i: (0, 0)),
        ],
        out_specs=[blk, whole, whole, blk, pl.BlockSpec((8, 1024), lambda i: (0, 0))],
        out_shape=[_sds((T, 1024), BF16), _sds((T, D_KV), F32), _sds((T, D_KV), F32), _sds((T, 1024), BF16),
                   _sds((8, 1024), F32)],
        scratch_shapes=[pltpu.VMEM((BLOCK, 1024), F32)],
        compiler_params=_params(("arbitrary",), 48),
    )(sinks, *_hbm(bias, proj, proj, proj, proj, proj, proj, proj, y_attn, lse, dz_attn, token))


def _head(x, target, z_rnn, z_attn, proj, b_gate, g_post, w_rnn_out, w_attn_out, w_out):
    T = x.shape[0]
    tm = 256

    def body(x_ref, t_ref, zr_ref, za_ref, ml0_ref, ml1_ref, ml2_ref, ml3_ref, bg_ref, gp_ref, wr_ref, wa_ref, wo_ref,
             dyx_ref, dzr_ref, dza_ref, dml_ref, dout_ref, dbr_ref, dba_ref, mt_ref, zat_ref, sm_ref):
        @pl.when(pl.program_id(0) == 0)
        def _():
            sm_ref[...] = jnp.zeros_like(sm_ref)

        wr, wa, wo = wr_ref[...], wa_ref[...], wo_ref[...]
        br_rnn = _dot(zr_ref[...], wr)
        br_attn = _dot(za_ref[...], wa)
        zat_ref[...] = za_ref[...].astype(F32).T.astype(BF16)
        ml_rnn = jnp.concatenate([ml0_ref[...], ml1_ref[...]], axis=1).astype(F32)
        ml_attn = jnp.concatenate([ml2_ref[...], ml3_ref[...]], axis=1).astype(F32)
        g_rnn = _sigmoid(ml_rnn + bg_ref[:, 0:D_MODEL])
        g_attn = _sigmoid(ml_attn + bg_ref[:, D_MODEL:2 * D_MODEL])
        merged = g_rnn * br_rnn + g_attn * br_attn
        mb = merged.astype(BF16)
        mt_ref[...] = merged.T.astype(BF16)
        out = _dot(mb, wo)
        rstd = lax.rsqrt(jnp.mean(out * out, axis=-1, keepdims=True) + EPS)
        n = out * rstd
        gp = gp_ref[...]
        err = (x_ref[...] + n * gp) - t_ref[...]
        sm_ref[pl.ds(3, 1), :] += 0.5 * jnp.sum(jnp.mean(err * err, axis=-1, keepdims=True), axis=0, keepdims=True)
        dy = err * (1.0 / D_MODEL)
        dyx_ref[...] = dy
        sm_ref[pl.ds(0, 1), :] += jnp.sum(dy * n, axis=0, keepdims=True)
        dn = dy * gp
        dout = (rstd * (dn - n * jnp.mean(dn * n, axis=-1, keepdims=True))).astype(BF16)
        dout_ref[...] = dout
        dmerged = _dot_nt(dout, wo)
        dml_r = (dmerged * br_rnn) * (g_rnn * (1.0 - g_rnn))
        dml_a = (dmerged * br_attn) * (g_attn * (1.0 - g_attn))
        dml_ref[:, 0:D_MODEL] = dml_r.astype(BF16)
        dml_ref[:, D_MODEL:2 * D_MODEL] = dml_a.astype(BF16)
        sm_ref[pl.ds(1, 1), :] += jnp.sum(dml_r, axis=0, keepdims=True)
        sm_ref[pl.ds(2, 1), :] += jnp.sum(dml_a, axis=0, keepdims=True)
        dbr = (dmerged * g_rnn).astype(BF16)
        dba = (dmerged * g_attn).astype(BF16)
        dbr_ref[...] = dbr
        dba_ref[...] = dba
        dzr_ref[...] = _dot_nt(dbr, wr)
        dza_ref[...] = _dot_nt(dba, wa)

    tile = pl.BlockSpec((tm, D_MODEL), lambda i: (i, 0))
    wspec = pl.BlockSpec((D_MODEL, D_MODEL), lambda i: (0, 0))
    ml = lambda q: pl.BlockSpec((tm, 512), lambda i: (i, COL_MERGE // 2 + q))
    return pl.pallas_call(
        body,
        name="head",
        grid=(T // tm,),
        in_specs=[
            tile, tile, tile, tile,
            ml(0), ml(1), ml(2), ml(3),
            pl.BlockSpec((1, 2 * D_MODEL), lambda i: (0, 0)),
            pl.BlockSpec((1, D_MODEL), lambda i: (0, 0)),
            wspec, wspec, wspec,
        ],
        out_specs=[
            tile, tile, tile,
            pl.BlockSpec((tm, 2 * D_MODEL), lambda i: (i, 0)),
            tile, tile, tile,
            pl.BlockSpec((D_MODEL, tm), lambda i: (0, i)), pl.BlockSpec((D_MODEL, tm), lambda i: (0, i)),
            pl.BlockSpec((8, D_MODEL), lambda i: (0, 0)),
        ],
        out_shape=[
            _sds((T, D_MODEL), F32), _sds((T, D_MODEL), F32), _sds((T, D_MODEL), F32),
            _sds((T, 2 * D_MODEL), BF16),
            _sds((T, D_MODEL), BF16), _sds((T, D_MODEL), BF16), _sds((T, D_MODEL), BF16),
            _sds((D_MODEL, T), BF16), _sds((D_MODEL, T), BF16),
            _sds((8, D_MODEL), F32),
        ],
        compiler_params=_params(("arbitrary",), 56),
    )(*_hbm(x, target, z_rnn, z_attn, proj, proj, proj, proj, b_gate, g_post, w_rnn_out, w_attn_out, w_out))


def _matmul_t(at, b, name):
    M, T = at.shape
    N = b.shape[1]
    tk = min(1024, T)
    nt = T // tk

    def body(a_ref, b_ref, o_ref, ob_ref):
        @pl.when(pl.program_id(0) == 0)
        def _():
            o_ref[...] = jnp.zeros_like(o_ref)

        o_ref[...] += _dot(a_ref[...], b_ref[...])

        @pl.when(pl.program_id(0) == nt - 1)
        def _():
            ob_ref[...] = o_ref[...].astype(BF16)

    whole = pl.BlockSpec((M, N), lambda t: (0, 0))
    return pl.pallas_call(
        body,
        name=name,
        grid=(nt,),
        in_specs=[pl.BlockSpec((M, tk), lambda t: (0, t)), pl.BlockSpec((tk, N), lambda t: (t, 0))],
        out_specs=[whole, whole],
        out_shape=[_sds((M, N), F32), _sds((M, N), BF16)],
        compiler_params=_params(("arbitrary",), 48),
    )(*_hbm(at, b))


DPROJ_WIDTHS = (D_RNN, D_RNN, 1024, D_KV, D_KV, 1024, 2 * D_MODEL)


def _dproj_segments():
    segs, start = [[] for _ in range(N_CHIPS)], 0
    for p, width in enumerate(DPROJ_WIDTHS):
        for c in range(N_CHIPS):
            lo, hi = max(start, c * W_IN_SHARD), min(start + width, (c + 1) * W_IN_SHARD)
            if lo < hi:
                segs[c].append((p, lo - start, hi - start, lo - c * W_IN_SHARD, hi - c * W_IN_SHARD))
        start += width
    return segs


def _dh_bwd(pieces, w_in_g, x, dyx, g_pre, token):
    T = x.shape[0]
    tm = min(512, T)
    n = len(pieces)
    segs = _dproj_segments()

    def body(*refs):
        p_refs, w_hbm, x_ref, dyx_ref, g_ref = refs[0:n], refs[n], refs[n + 1], refs[n + 2], refs[n + 3]
        gx_ref, dg_ref, w_ref, w_sems = refs[n + 5], refs[n + 6], refs[n + 7], refs[n + 8]
        first = pl.program_id(0) == 0
        w_copies = [pltpu.make_async_copy(w_hbm.at[c], w_ref.at[c], w_sems.at[c]) for c in range(N_CHIPS)]

        @pl.when(first)
        def _():
            for cp in w_copies:
                cp.start()
            dg_ref[...] = jnp.zeros_like(dg_ref)

        dh = None
        for c in range(N_CHIPS):
            pl.when(first)(w_copies[c].wait)
            for p, a0, a1, u0, u1 in segs[c]:
                part = _dot_nt(p_refs[p][:, a0:a1].astype(BF16), w_ref[c, :, u0:u1])
                dh = part if dh is None else dh + part
        xv = x_ref[...]
        rstd = lax.rsqrt(jnp.mean(xv * xv, axis=-1, keepdims=True) + EPS)
        nx = xv * rstd
        dhg = dh * g_ref[...]
        gx_ref[...] = dyx_ref[...] + rstd * (dhg - nx * jnp.mean(dhg * nx, axis=-1, keepdims=True))
        dg_ref[pl.ds(0, 1), :] += jnp.sum(dh * nx, axis=0, keepdims=True)

    tile = pl.BlockSpec((tm, D_MODEL), lambda i: (i, 0))
    return pl.pallas_call(
        body,
        name="dh_bwd",
        grid=(T // tm,),
        in_specs=[pl.BlockSpec((tm, w), lambda i: (i, 0)) for w in DPROJ_WIDTHS] + [
            ANY, tile, tile,
            pl.BlockSpec((1, D_MODEL), lambda i: (0, 0)),
            pl.BlockSpec((8, 128), lambda i: (0, 0)),
        ],
        out_specs=[tile, pl.BlockSpec((8, D_MODEL), lambda i: (0, 0))],
        out_shape=[_sds((T, D_MODEL), F32), _sds((8, D_MODEL), F32)],
        scratch_shapes=[pltpu.VMEM(w_in_g.shape, BF16), pltpu.SemaphoreType.DMA((N_CHIPS,))],
        compiler_params=_params(("arbitrary",), 56),
    )(*_hbm(*pieces, w_in_g, x, dyx, g_pre, token))


def _dw_in(ht, pieces):
    T = ht.shape[1]
    tk = min(512, T)
    nt = T // tk
    n = len(pieces)
    segs = _dproj_segments()

    def body(*refs):
        h_ref, p_refs, o_ref, ob_ref = refs[0], refs[1:n + 1], refs[n + 1], refs[n + 2]

        @pl.when(pl.program_id(1) == 0)
        def _():
            o_ref[...] = jnp.zeros_like(o_ref)

        for c in range(N_CHIPS):
            @pl.when(pl.program_id(0) == c)
            def _():
                for p, a0, a1, u0, u1 in segs[c]:
                    o_ref[:, u0:u1] += _dot(h_ref[...], p_refs[p][:, a0:a1].astype(BF16))

        @pl.when(pl.program_id(1) == nt - 1)
        def _():
            ob_ref[...] = o_ref[...].astype(BF16)

    def piece_spec(p):
        chips = [c for c in range(N_CHIPS) if any(s[0] == p for s in segs[c])]

        def index(c, t):
            used = functools.reduce(jnp.logical_or, [c == k for k in chips])
            return (jnp.where(used, t, 0), 0)

        return pl.BlockSpec((tk, DPROJ_WIDTHS[p]), index)

    return pl.pallas_call(
        body,
        name="dw_in",
        grid=(N_CHIPS, nt),
        in_specs=[pl.BlockSpec((D_MODEL, tk), lambda c, t: (0, t))] + [piece_spec(p) for p in range(n)],
        out_specs=[pl.BlockSpec((None, D_MODEL, W_IN_SHARD), lambda c, t: (c, 0, 0))] * 2,
        out_shape=[_sds((N_CHIPS, D_MODEL, W_IN_SHARD), F32), _sds((N_CHIPS, D_MODEL, W_IN_SHARD), BF16)],
        compiler_params=_params(("parallel", "arbitrary"), 56),
    )(*_hbm(ht, *pieces))


ELEMENTWISE_TILE_BYTES = MIB


def _row_tile(rows, cols, limit=ELEMENTWISE_TILE_BYTES):
    if rows * cols * 4 <= limit:
        return rows
    for t in (512, 256, 128, 64, 32, 16, 8):
        if rows % t == 0 and t * cols * 4 <= limit:
            return t
    return rows


def _chip_sum(ps, gots, chip_core, name):
    n = len(ps)
    h, C = ps[0].shape
    tr = _row_tile(h, C * n)
    nt = h // tr

    def body(jc_ref, *refs):
        for a in range(n):
            p_ref, g0_ref, g1_ref, g2_ref, o_ref = refs[a], refs[n + 3 * a], refs[n + 3 * a + 1], refs[n + 3 * a + 2], \
                refs[4 * n + a]
            o_ref[...] = ((p_ref[...] + g0_ref[...].astype(F32)) + g1_ref[...].astype(F32)) + g2_ref[...].astype(F32)

    rel = lambda r: pl.BlockSpec((None, tr, C), lambda i, jc_ref: (r, i, 0))
    outs = pl.pallas_call(
        body,
        name=name,
        grid_spec=pltpu.PrefetchScalarGridSpec(
            num_scalar_prefetch=1,
            grid=(nt,),
            in_specs=[pl.BlockSpec((tr, C), lambda i, jc_ref: (i, 0))] * n + [rel(0), rel(1), rel(2)] * n,
            out_specs=[pl.BlockSpec((tr, C), lambda i, jc_ref: (jc_ref[1] * nt + i, 0))] * n,
        ),
        out_shape=[_sds((2 * h, C), F32)] * n,
        compiler_params=_params(("parallel",), 48),
    )(chip_core, *_hbm(*ps, *[g for got in gots for g in (got, got, got)]))
    return list(outs)


def _place_shards(shards, chip, name):
    n = len(shards)
    tiles = [_row_tile(s.shape[0], s.shape[1]) for s in shards]
    steps = max(s.shape[0] // t for s, t in zip(shards, tiles))
    tiles = [s.shape[0] // steps for s in shards]

    def body(j_ref, *refs):
        for a in range(n):
            refs[n + a][...] = refs[a][...].astype(BF16)

    return pl.pallas_call(
        body,
        name=name,
        grid_spec=pltpu.PrefetchScalarGridSpec(
            num_scalar_prefetch=1,
            grid=(steps,),
            in_specs=[pl.BlockSpec((t, s.shape[1]), lambda i, j_ref: (i, 0)) for s, t in zip(shards, tiles)],
            out_specs=[pl.BlockSpec((None, t, s.shape[1]), lambda i, j_ref: (j_ref[0], i, 0))
                       for s, t in zip(shards, tiles)],
        ),
        out_shape=[_sds((N_CHIPS,) + s.shape, BF16) for s in shards],
        compiler_params=_params(("parallel",), 48),
    )(chip, *_hbm(*shards))


def _adamw_update(w, g, m, v):
    c1 = 1.0 - ADAM_B1 ** ADAM_STEP
    c2 = 1.0 - ADAM_B2 ** ADAM_STEP
    nm = ADAM_B1 * m + (1.0 - ADAM_B1) * g
    nv = ADAM_B2 * v + (1.0 - ADAM_B2) * (g * g)
    return (-ADAM_LR) * ((nm / c1) / (jnp.sqrt(nv / c2) + ADAM_EPS) + ADAM_WD * w), nm, nv


def _adamw(params, name):
    n = len(params)
    R, C = params[0][0].shape
    tr = _row_tile(R, C * n)

    def body(*refs):
        for a in range(n):
            w_ref, g_ref, m_ref, v_ref = refs[4 * a:4 * a + 4]
            d_ref, nm_ref, nv_ref, go_ref = refs[4 * n + 4 * a:4 * n + 4 * a + 4]
            g = g_ref[...]
            d_ref[...], nm_ref[...], nv_ref[...] = _adamw_update(w_ref[...], g, m_ref[...], v_ref[...])
            go_ref[...] = g

    spec = pl.BlockSpec((tr, C), lambda i: (i, 0))
    outs = pl.pallas_call(
        body, name=name, grid=(R // tr,), in_specs=[spec] * (4 * n), out_specs=[spec] * (4 * n),
        out_shape=[_sds((R, C), F32)] * (4 * n), compiler_params=_params(("parallel",), 48),
    )(*_hbm(*[t for p in params for t in p]))
    return [tuple(outs[4 * a:4 * a + 4]) for a in range(n)]


def _adamw_whole(params, name):
    n = len(params)

    def body(*refs):
        for a in range(n):
            w_ref, g_ref, m_ref, v_ref = refs[4 * a:4 * a + 4]
            d_ref, nm_ref, nv_ref = refs[4 * n + 3 * a:4 * n + 3 * a + 3]
            d_ref[...], nm_ref[...], nv_ref[...] = _adamw_update(w_ref[...], g_ref[...], m_ref[...], v_ref[...])

    def whole(t):
        return pl.BlockSpec(t.shape, lambda i: (0,) * t.ndim)

    flat = [t for p in params for t in p]
    like = [p[0] for p in params for _ in range(3)]
    outs = pl.pallas_call(
        body, name=name, grid=(1,), in_specs=[whole(t) for t in flat], out_specs=[whole(t) for t in like],
        out_shape=[_sds(t.shape, F32) for t in like], compiler_params=_params(("arbitrary",), 48),
    )(*_hbm(*flat))
    return [tuple(outs[3 * a:3 * a + 3]) for a in range(n)]


def _place():
    return lax.axis_index("x"), lax.axis_index("y"), lax.axis_index("c")


def _chip_of(x, y, r):
    return (x ^ (r >> 1), y ^ (r & 1))


ANY = pl.BlockSpec(memory_space=pl.ANY)


def _gather_weights(placed, cw8):
    nbig = len(placed)
    halves = [s.shape[1] // 2 for s in placed]
    pieces = [max(1, h // 64) for h in halves]
    rows = [h // p for h, p in zip(halves, pieces)]
    order = [(a, q) for q in range(max(pieces)) for a in range(nbig) if q < pieces[a]]
    ici_sem = {(a, q, r): 3 * i + (r - 1) for i, (a, q) in enumerate(order) for r in (1, 2, 3)}
    cw_sem = {r: 3 * len(order) + (r - 1) for r in (1, 2, 3)}
    d2d_sem = {key: 3 * len(order) + 3 + k for key, k in ici_sem.items()}
    nsem = 6 * len(order) + 3

    def body(*refs):
        cw_ref, dsts, gcw_ref = refs[nbig], refs[nbig + 1:2 * nbig + 1], refs[2 * nbig + 1]
        send_sems, recv_sems = refs[2 * nbig + 2:]
        x, y, c = _place()
        j = 2 * x + y

        def piece_rows(a, q, core):
            return pl.ds(pl.multiple_of(core * halves[a] + q * rows[a], 16), rows[a])

        def ici(a, q, r):
            tx, ty = _chip_of(x, y, r)
            k = ici_sem[(a, q, r)]
            region = dsts[a].at[j, piece_rows(a, q, c), :]
            return pltpu.make_async_remote_copy(
                src_ref=region, dst_ref=region, send_sem=send_sems.at[k], recv_sem=recv_sems.at[k],
                device_id=(tx, ty, c), device_id_type=MESH)

        def ici_landed(a, q, r):
            tx, ty = _chip_of(x, y, r)
            k = ici_sem[(a, q, r)]
            region = dsts[a].at[2 * tx + ty, piece_rows(a, q, c), :]
            return pltpu.make_async_remote_copy(
                src_ref=region, dst_ref=region, send_sem=send_sems.at[k], recv_sem=recv_sems.at[k],
                device_id=(tx, ty, c), device_id_type=MESH)

        def d2d(a, q, r, core):
            tx, ty = _chip_of(x, y, r)
            k = d2d_sem[(a, q, r)]
            region = dsts[a].at[2 * tx + ty, piece_rows(a, q, core), :]
            return pltpu.make_async_remote_copy(
                src_ref=region, dst_ref=region, send_sem=send_sems.at[k], recv_sem=recv_sems.at[k],
                device_id=(x, y, 1 - c), device_id_type=MESH)

        def cw_copy(r):
            tx, ty = _chip_of(x, y, r)
            k = cw_sem[r]
            return pltpu.make_async_remote_copy(
                src_ref=cw_ref, dst_ref=gcw_ref.at[j], send_sem=send_sems.at[k], recv_sem=recv_sems.at[k],
                device_id=(tx, ty, c), device_id_type=MESH)

        def cw_landed(r):
            tx, ty = _chip_of(x, y, r)
            k = cw_sem[r]
            region = gcw_ref.at[2 * tx + ty]
            return pltpu.make_async_remote_copy(
                src_ref=region, dst_ref=region, send_sem=send_sems.at[k], recv_sem=recv_sems.at[k],
                device_id=(tx, ty, c), device_id_type=MESH)

        def relay(a, q, origin, to):
            ox, oy = _chip_of(x, y, origin)
            tx, ty = _chip_of(x, y, to)
            k = ici_sem[(a, q, 3)]
            region = dsts[a].at[2 * ox + oy, piece_rows(a, q, c), :]
            return pltpu.make_async_remote_copy(
                src_ref=region, dst_ref=region, send_sem=send_sems.at[k], recv_sem=recv_sems.at[k],
                device_id=(tx, ty, c), device_id_type=MESH)

        first = [ici(a, q, r) for (a, q) in order for r in (1, 2)] + [cw_copy(r) for r in (1, 2, 3)]
        for cp in first:
            cp.start()
        passed = []
        for (a, q) in order:
            for r in (1, 2):
                ici_landed(a, q, r).wait_recv()
                if q % 2 == r - 1:
                    cp = relay(a, q, r, 3 - r)
                    cp.start()
                    passed.append(cp)
                cp = d2d(a, q, r, c)
                cp.start()
                passed.append(cp)
        for (a, q) in order:
            ici_landed(a, q, 3).wait_recv()
            cp = d2d(a, q, 3, c)
            cp.start()
            passed.append(cp)
        for r in (1, 2, 3):
            cw_landed(r).wait_recv()
        for (a, q) in order:
            for r in (1, 2, 3):
                d2d(a, q, r, 1 - c).wait_recv()
        for cp in first + passed:
            cp.wait_send()

    return pl.pallas_call(
        body,
        name="gather_weights",
        in_specs=[ANY] * (nbig + 1),
        out_specs=[ANY] * (nbig + 1),
        out_shape=[_sds(s.shape, s.dtype) for s in placed] + [_sds((N_CHIPS,) + cw8.shape, cw8.dtype)],
        input_output_aliases={a: a for a in range(nbig)},
        scratch_shapes=[pltpu.SemaphoreType.DMA((nsem,)), pltpu.SemaphoreType.DMA((nsem,))],
    )(*placed, cw8)


def _gather_late_start(placed, after, name):
    n = len(placed)
    halves = [s.shape[1] // 2 for s in placed]

    def body(*refs):
        g_refs = refs[0:n]
        send_sems, recv_sems, token = refs[n + 1], refs[n + 2], refs[-1]
        x, y, c = _place()
        j = 2 * x + y
        for a in range(n):
            mine = g_refs[a].at[j, pl.ds(pl.multiple_of(c * halves[a], 16), halves[a]), :]
            for r in (1, 2, 3):
                tx, ty = _chip_of(x, y, r)
                for to_core in (0, 1):
                    k = ((a * 3 + (r - 1)) * 2 + c) * 2 + to_core
                    pltpu.make_async_remote_copy(
                        src_ref=mine, dst_ref=mine, send_sem=send_sems.at[k], recv_sem=recv_sems.at[k],
                        device_id=(tx, ty, to_core), device_id_type=MESH).start()
        token[...] = jnp.zeros_like(token)

    hbm = lambda t: pltpu.HBM(t.shape, t.dtype)
    keep = lambda t: pltpu.with_memory_space_constraint(t, pltpu.HBM)
    nsem = 12 * n
    outs = pl.pallas_call(
        body,
        name=name,
        in_specs=[HBM] * n + [ANY],
        out_specs=(SEM, SEM, *[HBM] * n, pl.BlockSpec(memory_space=pltpu.VMEM)),
        out_shape=(pltpu.SemaphoreType.DMA((nsem,)), pltpu.SemaphoreType.DMA((nsem,)), *[hbm(p) for p in placed],
                   jax.ShapeDtypeStruct((8, 128), F32)),
        input_output_aliases={i: 2 + i for i in range(n)},
        compiler_params=pltpu.CompilerParams(has_side_effects=DATAFLOW),
    )(*[keep(p) for p in placed], after)
    return outs[0], outs[1], list(outs[2:2 + n]), outs[-1]


def _gather_late_wait(send_sems, recv_sems, thru, after, name):
    n = len(thru)
    halves = [s.shape[1] // 2 for s in thru]

    def body(*refs):
        g_refs = refs[0:n]
        send_sems, recv_sems = refs[n], refs[n + 1]
        x, y, c = _place()
        j = 2 * x + y
        for a in range(n):
            mine = g_refs[a].at[j, pl.ds(pl.multiple_of(c * halves[a], 16), halves[a]), :]
            for r in (1, 2, 3):
                tx, ty = _chip_of(x, y, r)
                for other in (0, 1):
                    k_out = ((a * 3 + (r - 1)) * 2 + c) * 2 + other
                    pltpu.make_async_remote_copy(
                        src_ref=mine, dst_ref=mine, send_sem=send_sems.at[k_out], recv_sem=recv_sems.at[k_out],
                        device_id=(tx, ty, other), device_id_type=MESH).wait_send()
                    k_in = ((a * 3 + (r - 1)) * 2 + other) * 2 + c
                    theirs = g_refs[a].at[2 * tx + ty, pl.ds(other * halves[a], halves[a]), :]
                    pltpu.make_async_remote_copy(
                        src_ref=theirs, dst_ref=theirs, send_sem=send_sems.at[k_in], recv_sem=recv_sems.at[k_in],
                        device_id=(tx, ty, other), device_id_type=MESH).wait_recv()

    hbm = lambda t: pltpu.HBM(t.shape, t.dtype)
    outs = pl.pallas_call(
        body,
        name=name,
        in_specs=[HBM] * n + [SEM, SEM, ANY],
        out_specs=[HBM] * n,
        out_shape=[hbm(t) for t in thru],
        input_output_aliases={i: i for i in range(n)},
        compiler_params=pltpu.CompilerParams(has_side_effects=DATAFLOW),
    )(*thru, send_sems, recv_sems, after)
    return list(outs)


D2D_PIECE_ROWS = 64
PAIR_SUM_TILE_BYTES = 2 * MIB


def _pair_sum(gs, gbs, chip_core, name):
    n = len(gs)
    nch, R, C = gs[0].shape
    h = R // 2
    tr = _row_tile(h, C * n, PAIR_SUM_TILE_BYTES)
    nt = h // tr
    rows = min(D2D_PIECE_ROWS, tr)

    def body(jc_ref, *refs):
        g_refs, gb_refs, p_refs, pb_refs = refs[0:n], refs[n:2 * n], refs[2 * n:3 * n], refs[3 * n:4 * n]
        got_refs, send_sems, recv_sems = refs[4 * n:5 * n], refs[5 * n], refs[5 * n + 1]
        i, j = pl.program_id(0), pl.program_id(1)
        x, y, c = _place()

        def copy(a, ti, tj, first, count):
            src_rows = pl.ds(pl.multiple_of((1 - c) * h + ti * tr + first, 16), count)
            dst_rows = pl.ds(pl.multiple_of(ti * tr + first, 16), count)
            return pltpu.make_async_remote_copy(
                src_ref=gb_refs[a].at[tj, src_rows, :], dst_ref=got_refs[a].at[tj, dst_rows, :],
                send_sem=send_sems.at[a, ti, tj], recv_sem=recv_sems.at[a, ti, tj],
                device_id=(x, y, 1 - c), device_id_type=MESH)

        @pl.when((i == 0) & (j == 0))
        def _():
            for ti in range(nt):
                for tj in range(nch):
                    for a in range(n):
                        for q in range(tr // rows):
                            copy(a, ti, tj, q * rows, rows).start()

        for a in range(n):
            copy(a, i, j, 0, tr).wait()
            s = g_refs[a][...] + got_refs[a][j, pl.ds(pl.multiple_of(i * tr, 16), tr), :].astype(F32)
            pb_refs[a][...] = s.astype(BF16)

            @pl.when(j == jc_ref[0])
            def _():
                p_refs[a][...] = s

    by_chip = pl.BlockSpec((None, tr, C), lambda i, j, jc_ref: (j, i, 0))
    outs = pl.pallas_call(
        body,
        name=name,
        grid_spec=pltpu.PrefetchScalarGridSpec(
            num_scalar_prefetch=1,
            grid=(nt, nch),
            in_specs=[pl.BlockSpec((None, tr, C), lambda i, j, jc_ref: (j, jc_ref[1] * nt + i, 0))] * n + [ANY] * n,
            out_specs=[pl.BlockSpec((tr, C), lambda i, j, jc_ref: (i, 0))] * n + [by_chip] * n,
            scratch_shapes=[pltpu.VMEM((nch, h, C), BF16)] * n + [pltpu.SemaphoreType.DMA((n, nt, nch))] * 2,
        ),
        out_shape=[_sds((h, C), F32)] * n + [_sds((nch, h, C), BF16)] * n,
        compiler_params=_params(("arbitrary", "arbitrary"), 48),
    )(chip_core, *_hbm(*gs, *gbs))
    return list(outs[:n]), list(outs[n:])


HBM = pl.BlockSpec(memory_space=pltpu.HBM)
SEM = pl.BlockSpec(memory_space=pltpu.SEMAPHORE)
DATAFLOW = pltpu.SideEffectType.DATAFLOW_SIDE_EFFECTING


def _chip_copy(p_refs, land_refs, send_sems, recv_sems, a, r, blocked):
    x, y, c = _place()
    tx, ty = _chip_of(x, y, r)
    k = a * 3 + (r - 1)
    return pltpu.make_async_remote_copy(
        src_ref=p_refs[a].at[2 * tx + ty] if blocked else p_refs[a], dst_ref=land_refs[a].at[r - 1],
        send_sem=send_sems.at[k], recv_sem=recv_sems.at[k], device_id=(tx, ty, c), device_id_type=MESH)


def _chip_exchange_start(psums, name, blocked=True):
    n = len(psums)
    lands = [lax.empty((3,) + (p.shape[1:] if blocked else p.shape), p.dtype) for p in psums]

    def body(*refs):
        p_refs, land_refs = refs[0:n], refs[n:2 * n]
        send_sems, recv_sems, token = refs[2 * n], refs[2 * n + 1], refs[-1]
        for a in range(n):
            for r in (1, 2, 3):
                _chip_copy(p_refs, land_refs, send_sems, recv_sems, a, r, blocked).start()
        token[...] = jnp.zeros_like(token)

    hbm = lambda t: pltpu.HBM(t.shape, t.dtype)
    keep = lambda t: pltpu.with_memory_space_constraint(t, pltpu.HBM)
    outs = pl.pallas_call(
        body,
        name=name,
        in_specs=[HBM] * (2 * n),
        out_specs=(SEM, SEM, *[HBM] * (2 * n), pl.BlockSpec(memory_space=pltpu.VMEM)),
        out_shape=(pltpu.SemaphoreType.DMA((3 * n,)), pltpu.SemaphoreType.DMA((3 * n,)),
                   *[hbm(p) for p in psums], *[hbm(l) for l in lands], _sds((8, 128), F32)),
        input_output_aliases={i: 2 + i for i in range(2 * n)},
        compiler_params=pltpu.CompilerParams(has_side_effects=DATAFLOW),
    )(*[keep(p) for p in psums], *[keep(l) for l in lands])
    return outs[0], outs[1], list(outs[2:2 + n]), list(outs[2 + n:2 + 2 * n]), outs[-1]


def _chip_exchange_wait(send_sems, recv_sems, p_thru, land_thru, after, name, blocked=True):
    n = len(p_thru)

    def body(*refs):
        p_refs, land_refs = refs[0:n], refs[n:2 * n]
        send_sems, recv_sems = refs[2 * n], refs[2 * n + 1]
        for a in range(n):
            for r in (1, 2, 3):
                copy = _chip_copy(p_refs, land_refs, send_sems, recv_sems, a, r, blocked)
                copy.wait_send()
                copy.wait_recv()

    hbm = lambda t: pltpu.HBM(t.shape, t.dtype)
    outs = pl.pallas_call(
        body,
        name=name,
        in_specs=[HBM] * (2 * n) + [SEM, SEM, ANY],
        out_specs=[HBM] * (2 * n),
        out_shape=[hbm(p) for p in p_thru] + [hbm(l) for l in land_thru],
        input_output_aliases={i: i for i in range(2 * n)},
        compiler_params=pltpu.CompilerParams(has_side_effects=DATAFLOW),
    )(*p_thru, *land_thru, send_sems, recv_sems, after)
    return list(outs[0:n]), list(outs[n:2 * n])


def _pair_share(fulls):
    n = len(fulls)
    halves = [f.shape[0] // 2 for f in fulls]

    def body(*refs):
        full_refs = refs[n:2 * n]
        send_sems, recv_sems = refs[2 * n:]
        x, y, c = _place()

        def half_of(a, core):
            return full_refs[a].at[pl.ds(pl.multiple_of(core * halves[a], 8), halves[a]), :]

        def remote(a, src, dst):
            return pltpu.make_async_remote_copy(
                src_ref=src, dst_ref=dst, send_sem=send_sems.at[a], recv_sem=recv_sems.at[a],
                device_id=(x, y, 1 - c), device_id_type=MESH)

        for a in range(n):
            for q in range(halves[a] // D2D_PIECE_ROWS):
                piece = full_refs[a].at[
                    pl.ds(pl.multiple_of(c * halves[a] + q * D2D_PIECE_ROWS, 8), D2D_PIECE_ROWS), :]
                remote(a, piece, piece).start()
        for a in range(n):
            remote(a, half_of(a, c), half_of(a, c)).wait_send()
            remote(a, half_of(a, 1 - c), half_of(a, 1 - c)).wait_recv()

    return pl.pallas_call(
        body,
        name="pair_share",
        in_specs=[ANY] * n,
        out_specs=[ANY] * n,
        out_shape=[_sds(f.shape, F32) for f in fulls],
        input_output_aliases={a: a for a in range(n)},
        scratch_shapes=[pltpu.SemaphoreType.DMA((n,)), pltpu.SemaphoreType.DMA((n,))],
    )(*fulls)


def _small_pair_sum(s):
    R, C = s.shape
    V = SMALL_VECTOR_ROWS

    def body(s_ref, v_ref, m_ref, sib, send_sem, recv_sem):
        x, y, c = _place()

        def to_sib(src, dst):
            return pltpu.make_async_remote_copy(
                src_ref=src, dst_ref=dst, send_sem=send_sem, recv_sem=recv_sem,
                device_id=(x, y, 1 - c), device_id_type=MESH)

        for q in range(R // 8):
            to_sib(s_ref.at[pl.ds(8 * q, 8), :], sib.at[pl.ds(8 * q, 8), :]).start()
        to_sib(s_ref, sib).wait()
        v_ref[...] = s_ref[pl.ds(0, V), :] + sib[pl.ds(0, V), :]
        m_ref[...] = (s_ref[pl.ds(V, R - V), :] + sib[pl.ds(V, R - V), :]).astype(BF16)

    return pl.pallas_call(
        body,
        name="small_pair_sum",
        in_specs=[pl.BlockSpec(memory_space=pltpu.VMEM)],
        out_specs=[pl.BlockSpec(memory_space=pltpu.VMEM)] * 2,
        out_shape=[jax.ShapeDtypeStruct((V, C), F32), jax.ShapeDtypeStruct((R - V, C), BF16)],
        scratch_shapes=[pltpu.VMEM((R, C), F32), pltpu.SemaphoreType.DMA, pltpu.SemaphoreType.DMA],
    )(s)


def _small_total(chip, own, landed):
    V, C = own[0].shape
    M = own[1].shape[0]

    def body(j_ref, v_ref, m_ref, lv_ref, lm_ref, o_ref, chips_v, chips_m):
        j = j_ref[0]
        chips_v[j] = v_ref[...]
        chips_m[j] = m_ref[...]
        for r in (1, 2, 3):
            chips_v[j ^ r] = lv_ref[r - 1]
            chips_m[j ^ r] = lm_ref[r - 1]
        o_ref[pl.ds(0, V), :] = (chips_v[0] + chips_v[1]) + (chips_v[2] + chips_v[3])
        o_ref[pl.ds(V, M), :] = (chips_m[0].astype(F32) + chips_m[1].astype(F32)) + (
            chips_m[2].astype(F32) + chips_m[3].astype(F32))

    vmem = pl.BlockSpec(memory_space=pltpu.VMEM)
    return pl.pallas_call(
        body,
        name="small_total",
        in_specs=[pl.BlockSpec(memory_space=pltpu.SMEM), vmem, vmem, vmem, vmem],
        out_specs=vmem,
        out_shape=jax.ShapeDtypeStruct((V + M, C), F32),
        scratch_shapes=[pltpu.VMEM((N_CHIPS, V, C), F32), pltpu.VMEM((N_CHIPS, M, C), BF16)],
    )(chip, own[0], own[1], landed[0], landed[1])


def _block_diag(w):
    w4 = w.reshape(4, 4, RNN_BLOCK_W, RNN_BLOCK_W)
    eye = jnp.eye(4, dtype=w.dtype)
    return jnp.einsum("jaik,ab->jaibk", w4, eye).reshape(4, RNN_TILE, RNN_TILE)


def _block_diag_part(d):
    d5 = d.reshape(4, 4, RNN_BLOCK_W, 4, RNN_BLOCK_W)
    return jnp.stack([d5[:, a, :, a, :] for a in range(4)], axis=1).reshape(RNN_BLOCKS, RNN_BLOCK_W, RNN_BLOCK_W)


def _local_grads(x, target, g_pre, w_in_g, b_gate, conv_w, conv_b, w_rg_a, b_rg_a, w_rg_x, b_rg_x, lam, sinks,
                 out_weights, fwd_token, g_post, on_out_grads, on_w_in_grad):
    wa_bd = _block_diag(w_rg_a).astype(BF16)
    wx_bd = _block_diag(w_rg_x).astype(BF16)
    b_a = b_rg_a.reshape(1, D_RNN)
    b_x = b_rg_x.reshape(1, D_RNN)

    proj, ht = _proj_fwd(x, g_pre, w_in_g)
    y_rnn, z_rnn, conv, z_rnn_t = _rnn_fwd(proj, conv_w, conv_b, wa_bd, wx_bd, b_a, b_x, lam, fwd_token)
    bias = _attn_bias()
    y_attn, z_attn, lse = _attn_fwd(proj, sinks, bias)
    w_rnn_out, w_attn_out, w_out = out_weights(z_attn)
    dyx, dz_rnn, dz_attn, dml, dout, dbr_rnn, dbr_attn, merged_t, z_attn_t, head_small = _head(
        x, target, z_rnn, z_attn, proj, b_gate, g_post, w_rnn_out, w_attn_out, w_out)
    out_grads = [_matmul_t(z_rnn_t, dbr_rnn, "dw_rnn_out"), _matmul_t(z_attn_t, dbr_attn, "dw_attn_out"),
                 _matmul_t(merged_t, dout, "dw_out")]
    shard_rows = lambda d: d.reshape(N_CHIPS, OUT_SHARD, D_MODEL)
    token = on_out_grads([shard_rows(g) for g, _ in out_grads], [shard_rows(gb) for _, gb in out_grads])
    dq, dk, dv, dag, attn_small = _attn_bwd(proj, y_attn, lse, dz_attn, sinks, bias, token)
    drx, drg, dwa_t, dwx_t, rnn_small = _rnn_bwd(proj, conv, y_rnn, dz_rnn, conv_w, wa_bd, wx_bd, b_a, b_x, lam)
    dproj = [drx, drg, dq, dk, dv, dag, dml]
    token = on_w_in_grad(*_dw_in(ht, dproj))
    grad_x, dh_small = _dh_bwd(dproj, w_in_g, x, dyx, g_pre, token)
    small = jnp.concatenate([rnn_small, head_small, dh_small + attn_small,
                             _block_diag_part(dwa_t).reshape(64, 1024), _block_diag_part(dwx_t).reshape(64, 1024)], axis=0)
    return grad_x, small


ROW_LOSS = 11


def _unpack_small(s, conv_cols):
    return {
        "b_rg_a": s[0:1].reshape(1, 16, 64), "b_rg_x": s[1:2].reshape(1, 16, 64), "lru_lambda": s[2:3],
        "conv_b": s[3:4], "conv_w": s[4:8, 0:conv_cols].reshape(1, CONV_W, conv_cols),
        "post_norm_g": s[8:9], "b_gate": s[9:11].reshape(1, 2048),
        "pre_norm_g": s[16:17], "attn_sinks": s[17:18, 0:N_Q_HEADS],
        "w_rg_a": s[24:88].reshape(1, 16, 64, 64), "w_rg_x": s[88:152].reshape(1, 16, 64, 64),
    }


WEIGHTS = ["pre_norm_g", "w_in", "b_gate", "conv_w", "conv_b", "w_rg_a", "b_rg_a", "w_rg_x", "b_rg_x", "lru_lambda",
           "attn_sinks", "w_rnn_out", "w_attn_out", "w_out", "post_norm_g"]
BIG = ["w_in", "w_rnn_out", "w_attn_out", "w_out"]


def kernel(x, pre_norm_g, w_in, b_gate, conv_w, conv_b, w_rg_a, b_rg_a, w_rg_x, b_rg_x, lru_lambda, attn_sinks, w_rnn_out, w_attn_out, w_out, post_norm_g, loss_target, m_pre_norm_g, m_w_in, m_b_gate, m_conv_w, m_conv_b, m_w_rg_a, m_b_rg_a, m_w_rg_x, m_b_rg_x, m_lru_lambda, m_attn_sinks, m_w_rnn_out, m_w_attn_out, m_w_out, m_post_norm_g, v_pre_norm_g, v_w_in, v_b_gate, v_conv_w, v_conv_b, v_w_rg_a, v_b_rg_a, v_w_rg_x, v_b_rg_x, v_lru_lambda, v_attn_sinks, v_w_rnn_out, v_w_attn_out, v_w_out, v_post_norm_g):
    w = dict(pre_norm_g=pre_norm_g, w_in=w_in, b_gate=b_gate, conv_w=conv_w, conv_b=conv_b, w_rg_a=w_rg_a,
             b_rg_a=b_rg_a, w_rg_x=w_rg_x, b_rg_x=b_rg_x, lru_lambda=lru_lambda, attn_sinks=attn_sinks,
             w_rnn_out=w_rnn_out, w_attn_out=w_attn_out, w_out=w_out, post_norm_g=post_norm_g)
    m = dict(pre_norm_g=m_pre_norm_g, w_in=m_w_in, b_gate=m_b_gate, conv_w=m_conv_w, conv_b=m_conv_b, w_rg_a=m_w_rg_a,
             b_rg_a=m_b_rg_a, w_rg_x=m_w_rg_x, b_rg_x=m_b_rg_x, lru_lambda=m_lru_lambda, attn_sinks=m_attn_sinks,
             w_rnn_out=m_w_rnn_out, w_attn_out=m_w_attn_out, w_out=m_w_out, post_norm_g=m_post_norm_g)
    v = dict(pre_norm_g=v_pre_norm_g, w_in=v_w_in, b_gate=v_b_gate, conv_w=v_conv_w, conv_b=v_conv_b, w_rg_a=v_w_rg_a,
             b_rg_a=v_b_rg_a, w_rg_x=v_w_rg_x, b_rg_x=v_b_rg_x, lru_lambda=v_lru_lambda, attn_sinks=v_attn_sinks,
             w_rnn_out=v_w_rnn_out, w_attn_out=v_w_attn_out, w_out=v_w_out, post_norm_g=v_post_norm_g)
    chip = 2 * lax.axis_index("x") + lax.axis_index("y")

    chip_idx = chip.astype(jnp.int32).reshape(1)
    chip_core = jnp.stack([chip, lax.axis_index("c")]).astype(jnp.int32)
    cw8 = jnp.pad(conv_w[0], ((0, 8 - CONV_W), (0, 0)))
    placed = _place_shards([w_in[0], w_rnn_out[0], w_attn_out[0], w_out[0]], chip_idx, "place_shards")
    win_g, cw_g = _gather_weights(placed[:1], cw8)
    late_send, late_recv, late_thru, late_token = _gather_late_start(placed[1:], win_g, "gather_late_start")
    cw_g = lax.dynamic_update_slice_in_dim(cw_g, cw8[None], chip, axis=0)
    conv_w_full = jnp.transpose(cw_g[:, 0:CONV_W, :], (1, 0, 2)).reshape(CONV_W, D_RNN)

    started = {}

    def start_reduction(tag, grads, grads_b16):
        psums, psums_b16 = _pair_sum(grads, grads_b16, chip_core, "pair_sum_" + tag)
        send_sems, recv_sems, p_thru, land_thru, token = _chip_exchange_start(psums_b16, "chip_exchange_start_" + tag)
        started[tag] = (psums, send_sems, recv_sems, p_thru, land_thru)
        return token

    def end_reduction(tag, after):
        psums, send_sems, recv_sems, p_thru, land_thru = started[tag]
        _, landed = _chip_exchange_wait(send_sems, recv_sems, p_thru, land_thru, after, "chip_exchange_wait_" + tag)
        return _chip_sum(psums, landed, chip_core, "chip_sum_" + tag)

    def out_weights(after):
        gathered = _gather_late_wait(late_send, late_recv, late_thru, after, "gather_late_wait")
        return [g.reshape(D_MODEL, D_MODEL) for g in gathered]

    grad_x, small = _local_grads(
        x[0], loss_target[0], pre_norm_g, win_g, b_gate, conv_w_full, conv_b, w_rg_a[0], b_rg_a[0], w_rg_x[0],
        b_rg_x[0], lru_lambda, attn_sinks[0], out_weights, late_token, post_norm_g,
        on_out_grads=lambda grads, grads_b16: start_reduction("out", grads, grads_b16),
        on_w_in_grad=lambda grad, grad_b16: start_reduction("in", [grad], [grad_b16]))

    small_chip = _small_pair_sum(small)
    small_send, small_recv, small_thru, small_land, small_token = _chip_exchange_start(
        list(small_chip), "small_exchange_start", blocked=False)

    halves = end_reduction("in", small_token) + end_reduction("out", small_token)
    gbig = dict(zip(BIG, _pair_share(halves)))

    grads, delta, new_m, new_v = {}, {}, {}, {}
    for names, tag in ((BIG[:1], "adamw_in"), (BIG[1:], "adamw_out")):
        updates = _adamw([(w[n][0], gbig[n], m[n][0], v[n][0]) for n in names], tag)
        for n, (d, nm, nv, g) in zip(names, updates):
            grads[n], delta[n], new_m[n], new_v[n] = g[None], d[None], nm[None], nv[None]

    small_own, small_landed = _chip_exchange_wait(small_send, small_recv, small_thru, small_land, delta[BIG[-1]],
                                                  "small_exchange_wait", blocked=False)
    small_sum = _small_total(chip_idx, small_own, small_landed)
    total_loss = small_sum[ROW_LOSS, 0]
    gsmall = _unpack_small(small_sum, D_RNN)
    conv_shard = D_RNN // N_CHIPS
    gsmall["conv_w"] = lax.dynamic_slice_in_dim(gsmall["conv_w"], chip * conv_shard, conv_shard, axis=2)
    for n in gsmall:
        grads[n] = gsmall[n].reshape(w[n].shape)
    updates = _adamw_whole([(w[n], grads[n], m[n], v[n]) for n in gsmall], "adamw_small")
    for n, (d, nm, nv) in zip(gsmall, updates):
        delta[n], new_m[n], new_v[n] = d, nm, nv

    return (total_loss, grad_x[None], *[grads[n] for n in WEIGHTS], *[delta[n] for n in WEIGHTS],
            *[new_m[n] for n in WEIGHTS], *[new_v[n] for n in WEIGHTS])
```

```python
import functools
import math

import jax
import jax.numpy as jnp
import numpy as np
from jax import lax
from jax.experimental import pallas as pl
from jax.experimental.pallas import tpu as pltpu

F32 = jnp.float32
BF16 = jnp.bfloat16

D_MODEL = 1024
D_RNN = 1024
RNN_BLOCKS = 16
RNN_BLOCK_W = 64
CONV_W = 4
LRU_C = 8.0
N_Q_HEADS = 16
N_KV_HEADS = 4
GROUP = 4
HEAD_DIM = 64
D_KV = 256
BLOCK = 128
ALIBI_MAX_BIAS = 8.0
EPS = 1e-6
D_IN = 6656
N_CHIPS = 4
W_IN_SHARD = D_IN // N_CHIPS
OUT_SHARD = D_MODEL // N_CHIPS
ADAM_LR = 0.001
ADAM_B1 = 0.9
ADAM_B2 = 0.999
ADAM_EPS = 1e-08
ADAM_WD = 0.01
ADAM_STEP = 10
NEG_BIG = -1e30
MIB = 1 << 20

COL_RNN_X = 0
COL_RNN_GATE = 4
COL_Q = 8
COL_K = 12
COL_V = 13
COL_ATTN_GATE = 14
COL_MERGE = 18

RNN_TILE = 256
RNN_CHUNK = 512
SMALL_ROWS = 152
SMALL_VECTOR_ROWS = 24
MESH = pl.DeviceIdType.MESH


def _sds(shape, dtype):
    return pltpu.HBM(shape, dtype)


def _params(sem=None, vmem_mib=None):
    kw = {}
    if sem is not None:
        kw["dimension_semantics"] = sem
    if vmem_mib is not None:
        kw["vmem_limit_bytes"] = vmem_mib * MIB
    return pltpu.CompilerParams(**kw)


def _hbm(*arrays):
    return [pltpu.with_memory_space_constraint(a, pltpu.HBM) for a in arrays]


def _dot(a, b):
    return jnp.dot(a, b, preferred_element_type=F32)


def _dot_nt(a, b):
    return lax.dot_general(a, b, (((1,), (1,)), ((), ())), preferred_element_type=F32)


def _dot_tn(a, b):
    return lax.dot_general(a, b, (((0,), (0,)), ((), ())), preferred_element_type=F32)


def _sigmoid(x):
    return 0.5 * jnp.tanh(0.5 * x) + 0.5


def _sigmoid_small(x):
    return 1.0 / (1.0 + jnp.exp(-x))


def _softplus(x):
    return jnp.maximum(x, 0.0) + jnp.log(1.0 + jnp.exp(-jnp.abs(x)))


def _one_minus_square(a, log_a):
    return -jnp.tanh(log_a) * (a * a + 1.0)


def _proj_fwd(x, g_pre, w_in_g):
    T = x.shape[0]
    tm = min(1024, T)

    def body(x_ref, g_ref, w_ref, proj_ref, ht_ref, h_s):
        @pl.when(pl.program_id(1) == 0)
        def _():
            xv = x_ref[...]
            rstd = lax.rsqrt(jnp.mean(xv * xv, axis=-1, keepdims=True) + EPS)
            hf = (xv * rstd) * g_ref[...]
            h_s[...] = hf.astype(BF16)
            ht_ref[...] = hf.T.astype(BF16)

        proj_ref[...] = _dot(h_s[...], w_ref[...]).astype(BF16)

    return pl.pallas_call(
        body,
        name="proj_fwd",
        grid=(T // tm, N_CHIPS),
        in_specs=[
            pl.BlockSpec((tm, D_MODEL), lambda i, j: (i, 0)),
            pl.BlockSpec((1, D_MODEL), lambda i, j: (0, 0)),
            pl.BlockSpec((None, D_MODEL, W_IN_SHARD), lambda i, j: (j, 0, 0)),
        ],
        out_specs=[
            pl.BlockSpec((tm, W_IN_SHARD), lambda i, j: (i, j)),
            pl.BlockSpec((D_MODEL, tm), lambda i, j: (0, i)),
        ],
        out_shape=[_sds((T, D_IN), BF16), _sds((D_MODEL, T), BF16)],
        scratch_shapes=[pltpu.VMEM((tm, D_MODEL), BF16)],
        compiler_params=_params(("parallel", "arbitrary"), 48),
    )(*_hbm(x, g_pre, w_in_g))


def _shift_down(x, tail, s, row):
    n = x.shape[0]
    xs = pltpu.roll(x, s, 0)
    tail_t = jnp.tile(pltpu.roll(tail, s, 0), (n // 8, 1))
    return jnp.where(row < s, tail_t, xs)


def _shift_up(x, head, s, row):
    n = x.shape[0]
    xs = pltpu.roll(x, n - s, 0)
    head_t = jnp.tile(pltpu.roll(head, 8 - s, 0), (n // 8, 1))
    return jnp.where(row >= n - s, head_t, xs)


def _conv_taps(x, tail, row):
    return [_shift_down(x, tail, 3, row), _shift_down(x, tail, 2, row), _shift_down(x, tail, 1, row), x]


def _rglru_gates(c, wa, wx, ba, bx, lam):
    cb = c.astype(BF16)
    r = _sigmoid_small(_dot(cb, wa) + ba)
    i = _sigmoid(_dot(cb, wx) + bx)
    log_a = (-LRU_C) * r * _softplus(-lam)
    a = jnp.exp(log_a)
    w = _one_minus_square(a, log_a)
    inv_mult = lax.rsqrt(w)
    return cb, r, i, a, w * inv_mult, inv_mult


GATE_BLOCKS_PER_TILE = RNN_TILE // RNN_BLOCK_W
GATE_BLOCKS = pl.BlockSpec((GATE_BLOCKS_PER_TILE, RNN_BLOCK_W, RNN_BLOCK_W), lambda j, t: (j, 0, 0))


def _fill_block_diag(bd_ref, w_ref):
    bd_ref[...] = jnp.zeros_like(bd_ref)
    for a in range(GATE_BLOCKS_PER_TILE):
        lo = a * RNN_BLOCK_W
        bd_ref[lo:lo + RNN_BLOCK_W, lo:lo + RNN_BLOCK_W] = w_ref[a].astype(BF16)


SUBLANES = 8


def _scan_down(a, u, row):
    n = a.shape[0]
    s = 1
    while s < SUBLANES:
        a_sh = jnp.where(row >= s, pltpu.roll(a, s, 0), 1.0)
        u_sh = jnp.where(row >= s, pltpu.roll(u, s, 0), 0.0)
        u = a * u_sh + u
        a = a * a_sh
        s *= 2
    while s < n:
        u = jnp.concatenate([u[:s], a[s:] * u[:n - s] + u[s:]], axis=0)
        a = jnp.concatenate([a[:s], a[s:] * a[:n - s]], axis=0)
        s *= 2
    return a, u


def _scan_up(b, u, row):
    n = b.shape[0]
    s = 1
    while s < SUBLANES:
        b_sh = jnp.where(row < n - s, pltpu.roll(b, n - s, 0), 1.0)
        u_sh = jnp.where(row < n - s, pltpu.roll(u, n - s, 0), 0.0)
        u = b * u_sh + u
        b = b * b_sh
        s *= 2
    while s < n:
        u = jnp.concatenate([b[:n - s] * u[s:] + u[:n - s], u[n - s:]], axis=0)
        b = jnp.concatenate([b[:n - s] * b[s:], b[n - s:]], axis=0)
        s *= 2
    return b, u


LANES = 128


def _chunk_scan(a, u, a_s, u_s, hl_s, al_s, carry, reverse):
    n, width = a.shape
    groups = n // SUBLANES
    order = range(SUBLANES - 1, -1, -1) if reverse else range(SUBLANES)
    row = lax.broadcasted_iota(jnp.int32, (groups, LANES), 0)
    for l in range(width // LANES):
        lanes = slice(l * LANES, (l + 1) * LANES)
        a_l, u_l, hl_l, al_l = a_s.at[l], u_s.at[l], hl_s.at[l], al_s.at[l]
        a_l[...] = a[:, lanes]
        u_l[...] = u[:, lanes]
        h_loc = a_loc = None
        for r in order:
            rows = pl.ds(r, groups, stride=SUBLANES)
            a_r, u_r = a_l[rows, :], u_l[rows, :]
            h_loc, a_loc = (u_r, a_r) if h_loc is None else (a_r * h_loc + u_r, a_r * a_loc)
            hl_l[rows, :] = h_loc
            al_l[rows, :] = a_loc
        if reverse:
            a_cum, ends = _scan_up(a_loc, h_loc, row)
            ends = ends + a_cum * carry[:, lanes]
            enters = jnp.where(row == groups - 1, carry[:, lanes], pltpu.roll(ends, groups - 1, 0))
        else:
            a_cum, ends = _scan_down(a_loc, h_loc, row)
            ends = ends + a_cum * carry[:, lanes]
            enters = jnp.where(row == 0, carry[:, lanes], pltpu.roll(ends, 1, 0))
        for r in range(SUBLANES):
            rows = pl.ds(r, groups, stride=SUBLANES)
            hl_l[rows, :] = hl_l[rows, :] + al_l[rows, :] * enters
    return jnp.concatenate([hl_s[l] for l in range(width // LANES)], axis=1)


def _rnn_fwd(proj, conv_w, conv_b, w_a, w_x, b_a, b_x, lam, token):
    T = proj.shape[0]
    tc, ct = RNN_CHUNK, RNN_TILE
    nt = T // tc

    def body(x_ref, rg_ref, cw_ref, cb_ref, wa_ref, wx_ref, ba_ref, bx_ref, lam_ref, token_ref, h_ref, z_ref, c_ref,
             zt_ref, xtail, hcarry, wa_s, wx_s, a_s, u_s, hl_s, al_s):
        @pl.when(pl.program_id(1) == 0)
        def _():
            xtail[...] = jnp.zeros_like(xtail)
            hcarry[...] = jnp.zeros_like(hcarry)
            _fill_block_diag(wa_s, wa_ref)
            _fill_block_diag(wx_s, wx_ref)

        row = lax.broadcasted_iota(jnp.int32, (tc, ct), 0)
        x = x_ref[...].astype(F32)
        taps = _conv_taps(x, xtail[...], row)
        c = cb_ref[...] + cw_ref[pl.ds(0, 1), :] * taps[0]
        for k in range(1, CONV_W):
            c = c + cw_ref[pl.ds(k, 1), :] * taps[k]
        xtail[...] = x[tc - 8:, :]
        c_ref[...] = c
        _, _, i, a, mult, _ = _rglru_gates(c, wa_s[...], wx_s[...], ba_ref[...], bx_ref[...], lam_ref[...])
        h = _chunk_scan(a, mult * (i * c), a_s, u_s, hl_s, al_s, hcarry[...], reverse=False)
        h_ref[...] = h
        hcarry[...] = h_ref[pl.ds(tc - 1, 1), :]
        rg = rg_ref[...].astype(F32)
        z = h * (rg * _sigmoid(rg))
        z_ref[...] = z.astype(BF16)
        zt_ref[...] = z.T.astype(BF16)

    col = lambda off: (lambda j, t: (t, off + j))
    vec = pl.BlockSpec((1, ct), lambda j, t: (0, j))
    return pl.pallas_call(
        body,
        name="rnn_fwd",
        grid=(D_RNN // ct, nt),
        in_specs=[
            pl.BlockSpec((tc, ct), col(COL_RNN_X)),
            pl.BlockSpec((tc, ct), col(COL_RNN_GATE)),
            pl.BlockSpec((CONV_W, ct), lambda j, t: (0, j)),
            vec, GATE_BLOCKS, GATE_BLOCKS, vec, vec, vec,
            pl.BlockSpec((8, 128), lambda j, t: (0, 0)),
        ],
        out_specs=[pl.BlockSpec((tc, ct), lambda j, t: (t, j))] * 3 + [pl.BlockSpec((ct, tc), lambda j, t: (j, t))],
        out_shape=[_sds((T, D_RNN), F32), _sds((T, D_RNN), BF16), _sds((T, D_RNN), F32), _sds((D_RNN, T), BF16)],
        scratch_shapes=[pltpu.VMEM((8, ct), F32), pltpu.VMEM((1, ct), F32)] + [pltpu.VMEM((ct, ct), BF16)] * 2 + [
            pltpu.VMEM((ct // LANES, tc, LANES), F32)] * 4,
        compiler_params=_params(("parallel", "arbitrary"), 32),
    )(*_hbm(proj, proj, conv_w, conv_b, w_a, w_x, b_a, b_x, lam, token))


def _rnn_bwd(proj, conv, y_rnn, dz_rnn, conv_w, w_a, w_x, b_a, b_x, lam):
    T = proj.shape[0]
    tc, ct = RNN_CHUNK, RNN_TILE
    nt = T // tc
    hb = tc // 8

    def body(x_ref, c_ref, rg_ref, h_ref, hh_ref, dz_ref, cw_ref, wa_ref, wx_ref, ba_ref, bx_ref, lam_ref,
             dx_ref, drg_ref, dwa_ref, dwx_ref, sm_ref, lam_carry, a_carry, dc_head, wa_s, wx_s, dwa_s, dwx_s,
             b_s, dy_s, hl_s, al_s):
        t = pl.program_id(1)
        first_chunk = t == nt - 1

        @pl.when(t == 0)
        def _():
            lam_carry[...] = jnp.zeros_like(lam_carry)
            a_carry[...] = jnp.zeros_like(a_carry)
            dc_head[...] = jnp.zeros_like(dc_head)
            dwa_s[...] = jnp.zeros_like(dwa_s)
            dwx_s[...] = jnp.zeros_like(dwx_s)
            sm_ref[...] = jnp.zeros_like(sm_ref)
            _fill_block_diag(wa_s, wa_ref)
            _fill_block_diag(wx_s, wx_ref)

        row = lax.broadcasted_iota(jnp.int32, (tc, ct), 0)
        keep = jnp.where(first_chunk, 0.0, 1.0)
        x = x_ref[...].astype(F32)
        c = c_ref[...]
        lam = lam_ref[...]
        cb, r, i, a, mult, inv_mult = _rglru_gates(c, wa_s[...], wx_s[...], ba_ref[...], bx_ref[...], lam)
        h = h_ref[...]
        h_prev = _shift_down(h, hh_ref[...] * keep, 1, row)
        rg = rg_ref[...].astype(F32)
        dz = dz_ref[...]
        sg = _sigmoid(rg)
        drg_ref[...] = (dz * h * (sg * (1.0 + rg * (1.0 - sg)))).astype(BF16)
        dy = dz * (rg * sg)
        b = jnp.where(row >= tc - 1, a_carry[pl.ds(0, 1), :], pltpu.roll(a, tc - 1, 0))
        lt = _chunk_scan(b, dy, b_s, dy_s, hl_s, al_s, lam_carry[pl.ds(0, 1), :], reverse=True)
        lam_carry[...] = lt[0:8, :]
        a_carry[...] = a[0:8, :]
        ic = i * c
        dmult = lt * ic
        di = lt * mult * c
        dc = lt * mult * i
        dlog_a = a * (lt * h_prev - dmult * a * inv_mult)
        sp = _softplus(-lam)
        dpre_r = dlog_a * ((-LRU_C) * sp) * (r * (1.0 - r))
        dpre_i = di * (i * (1.0 - i))
        dlam_row = jnp.sum(dlog_a * r, axis=0, keepdims=True) * (LRU_C * _sigmoid(-lam))
        dpr_b = dpre_r.astype(BF16)
        dpi_b = dpre_i.astype(BF16)
        dwa_s[...] += _dot_tn(cb, dpr_b)
        dwx_s[...] += _dot_tn(cb, dpi_b)
        dc = dc + _dot_nt(dpr_b, wa_s[...]) + _dot_nt(dpi_b, wx_s[...])
        head = dc_head[...]
        dx = cw_ref[pl.ds(3, 1), :] * dc
        sm_ref[pl.ds(4 + 3, 1), :] += jnp.sum(dc * x, axis=0, keepdims=True)
        for m in range(1, CONV_W):
            up = _shift_up(dc, head, m, row)
            dx = dx + cw_ref[pl.ds(3 - m, 1), :] * up
            sm_ref[pl.ds(4 + 3 - m, 1), :] += jnp.sum(up * x, axis=0, keepdims=True)
        dx_ref[...] = dx.astype(BF16)
        dc_head[...] = dc[0:8, :]
        sm_ref[pl.ds(0, 1), :] += jnp.sum(dpre_r, axis=0, keepdims=True)
        sm_ref[pl.ds(1, 1), :] += jnp.sum(dpre_i, axis=0, keepdims=True)
        sm_ref[pl.ds(2, 1), :] += dlam_row
        sm_ref[pl.ds(3, 1), :] += jnp.sum(dc, axis=0, keepdims=True)

        @pl.when(first_chunk)
        def _():
            for k in range(GATE_BLOCKS_PER_TILE):
                lo = k * RNN_BLOCK_W
                dwa_ref[k] = dwa_s[lo:lo + RNN_BLOCK_W, lo:lo + RNN_BLOCK_W]
                dwx_ref[k] = dwx_s[lo:lo + RNN_BLOCK_W, lo:lo + RNN_BLOCK_W]

    rev = lambda off: (lambda j, t: (nt - 1 - t, off + j))
    halo = lambda off: (lambda j, t: (jnp.maximum((nt - 1 - t) * hb - 1, 0), off + j))
    vec = pl.BlockSpec((1, ct), lambda j, t: (0, j))
    mat = GATE_BLOCKS
    return pl.pallas_call(
        body,
        name="rnn_bwd",
        grid=(D_RNN // ct, nt),
        in_specs=[
            pl.BlockSpec((tc, ct), rev(COL_RNN_X)),
            pl.BlockSpec((tc, ct), rev(0)),
            pl.BlockSpec((tc, ct), rev(COL_RNN_GATE)),
            pl.BlockSpec((tc, ct), rev(0)),
            pl.BlockSpec((8, ct), halo(0)),
            pl.BlockSpec((tc, ct), rev(0)),
            pl.BlockSpec((CONV_W, ct), lambda j, t: (0, j)),
            mat, mat, vec, vec, vec,
        ],
        out_specs=[
            pl.BlockSpec((tc, ct), rev(0)),
            pl.BlockSpec((tc, ct), rev(0)),
            mat, mat,
            pl.BlockSpec((8, ct), lambda j, t: (0, j)),
        ],
        out_shape=[_sds((T, D_RNN), BF16), _sds((T, D_RNN), BF16), _sds(w_a.shape, F32), _sds(w_x.shape, F32),
                   _sds((8, D_RNN), F32)],
        scratch_shapes=[pltpu.VMEM((8, ct), F32)] * 3 + [pltpu.VMEM((ct, ct), BF16)] * 2 + [
            pltpu.VMEM((ct, ct), F32)] * 2 + [pltpu.VMEM((ct // LANES, tc, LANES), F32)] * 4,
        compiler_params=_params(("parallel", "arbitrary"), 32),
    )(*_hbm(proj, conv, proj, y_rnn, y_rnn, dz_rnn, conv_w, w_a, w_x, b_a, b_x, lam))


def _attn_bias():
    qi = np.arange(BLOCK)[:, None]
    kj = np.arange(BLOCK)[None, :]
    dist_cur = (qi - kj).astype(np.float32)
    slopes = np.float32(2.0) ** (-ALIBI_MAX_BIAS * np.arange(1, N_Q_HEADS + 1, dtype=np.float32) / N_Q_HEADS)
    slopes = slopes[:, None, None]
    prev = np.where(kj > qi, -slopes * (dist_cur + np.float32(BLOCK)), np.float32(NEG_BIG))
    cur = np.where(kj <= qi, -slopes * dist_cur, np.float32(NEG_BIG))
    later = np.concatenate([prev, cur], axis=-1)
    first = np.concatenate([np.full_like(prev, NEG_BIG), cur], axis=-1)
    return jnp.asarray(np.stack([first, later]).astype(np.float32))


def _attn_exps(s_prev, s_cur, sink, bias):
    s_prev = s_prev + bias[:, 0:BLOCK]
    s_cur = s_cur + bias[:, BLOCK:2 * BLOCK]
    m = jnp.maximum(jnp.max(jnp.maximum(s_prev, s_cur), axis=-1, keepdims=True), sink)
    p_prev = jnp.exp(s_prev - m)
    p_cur = jnp.exp(s_cur - m)
    total = jnp.sum(p_prev + p_cur, axis=-1, keepdims=True) + jnp.exp(sink - m)
    return p_prev, p_cur, 1.0 / total, m + jnp.log(total)


def _attn_probs(s_prev, s_cur, sink, bias, lse):
    p_prev = jnp.exp((s_prev + bias[:, 0:BLOCK]) - lse)
    p_cur = jnp.exp((s_cur + bias[:, BLOCK:2 * BLOCK]) - lse)
    return p_prev, p_cur, jnp.exp(sink - lse)


def _stack_heads(ref_or_val, hk, dtype):
    parts = [ref_or_val[:, (GROUP * hk + g) * HEAD_DIM:(GROUP * hk + g + 1) * HEAD_DIM] for g in range(GROUP)]
    return jnp.concatenate(parts, axis=0).astype(dtype)


ATTN_SCALE = HEAD_DIM ** -0.5


def _bias_spec():
    return pl.BlockSpec((None, N_Q_HEADS, BLOCK, 2 * BLOCK), lambda i: (jnp.minimum(i, 1), 0, 0, 0))


def _attn_fwd(proj, sinks, bias):
    T = proj.shape[0]
    nb = T // BLOCK

    def body(sink_ref, bias_ref, q_ref, kp_ref, kc_ref, vp_ref, vc_ref, ag0_ref, ag1_ref, y_ref, z_ref, lse_ref):
        kvs = [slice(hk * HEAD_DIM, (hk + 1) * HEAD_DIM) for hk in range(N_KV_HEADS)]
        qgs = [(_stack_heads(q_ref, hk, F32) * ATTN_SCALE).astype(BF16) for hk in range(N_KV_HEADS)]
        s_prev = [_dot_nt(qgs[hk], kp_ref[:, kvs[hk]].astype(BF16)) for hk in range(N_KV_HEADS)]
        s_cur = [_dot_nt(qgs[hk], kc_ref[:, kvs[hk]].astype(BF16)) for hk in range(N_KV_HEADS)]
        for hk in range(N_KV_HEADS):
            pp, pc, invs = [], [], []
            for g in range(GROUP):
                h = GROUP * hk + g
                rows = slice(g * BLOCK, (g + 1) * BLOCK)
                p_prev, p_cur, inv, lse = _attn_exps(s_prev[hk][rows], s_cur[hk][rows], sink_ref[h], bias_ref[h])
                pp.append(p_prev.astype(BF16))
                pc.append(p_cur.astype(BF16))
                invs.append(inv)
                lse_ref[:, h:h + 1] = lse
            og = _dot(jnp.concatenate(pp, axis=0), vp_ref[:, kvs[hk]].astype(BF16)) + _dot(
                jnp.concatenate(pc, axis=0), vc_ref[:, kvs[hk]].astype(BF16))
            for g in range(GROUP):
                h = GROUP * hk + g
                y_ref[:, h * HEAD_DIM:(h + 1) * HEAD_DIM] = og[g * BLOCK:(g + 1) * BLOCK] * invs[g]
        ag = jnp.concatenate([ag0_ref[...], ag1_ref[...]], axis=1).astype(F32)
        z_ref[...] = (y_ref[...] * (ag * _sigmoid(ag))).astype(BF16)

    prev = lambda c: (lambda i: (jnp.maximum(i - 1, 0), c))
    cur = lambda c: (lambda i: (i, c))
    return pl.pallas_call(
        body,
        name="attn_fwd",
        grid=(nb,),
        in_specs=[
            pl.BlockSpec(memory_space=pltpu.SMEM),
            _bias_spec(),
            pl.BlockSpec((BLOCK, 1024), lambda i: (i, COL_Q // 4)),
            pl.BlockSpec((BLOCK, D_KV), prev(COL_K)),
            pl.BlockSpec((BLOCK, D_KV), cur(COL_K)),
            pl.BlockSpec((BLOCK, D_KV), prev(COL_V)),
            pl.BlockSpec((BLOCK, D_KV), cur(COL_V)),
            pl.BlockSpec((BLOCK, 512), lambda i: (i, COL_ATTN_GATE // 2)),
            pl.BlockSpec((BLOCK, 512), lambda i: (i, COL_ATTN_GATE // 2 + 1)),
        ],
        out_specs=[pl.BlockSpec((BLOCK, 1024), lambda i: (i, 0)), pl.BlockSpec((BLOCK, 1024), lambda i: (i, 0)),
                   pl.BlockSpec((BLOCK, N_Q_HEADS), lambda i: (i, 0))],
        out_shape=[_sds((T, 1024), F32), _sds((T, 1024), BF16), _sds((T, N_Q_HEADS), F32)],
        compiler_params=_params(("arbitrary",), 32),
    )(sinks, *_hbm(bias, proj, proj, proj, proj, proj, proj, proj))


def _attn_bwd(proj, y_attn, lse, dz_attn, sinks, bias, token):
    T = proj.shape[0]
    nb = T // BLOCK

    def body(sink_ref, bias_ref, q_ref, kp_ref, kc_ref, vp_ref, vc_ref, ag0_ref, ag1_ref, y_ref, lse_ref, dz_ref,
             token_ref, dq_ref, dk_ref, dv_ref, dag_ref, ds_ref, dy_s):
        i = pl.program_id(0)

        @pl.when(i == 0)
        def _():
            ds_ref[...] = jnp.zeros_like(ds_ref)

        lane = lax.broadcasted_iota(jnp.int32, (8, 128), 1)
        sub = lax.broadcasted_iota(jnp.int32, (8, 128), 0)
        ag = jnp.concatenate([ag0_ref[...], ag1_ref[...]], axis=1).astype(F32)
        dz = dz_ref[...]
        sg = _sigmoid(ag)
        dag_ref[...] = (dz * y_ref[...] * (sg * (1.0 + ag * (1.0 - sg)))).astype(BF16)
        dy_s[...] = dz * (ag * sg)
        r_cur = pl.multiple_of(i * BLOCK, BLOCK)
        r_prev = pl.multiple_of(jnp.maximum(i - 1, 0) * BLOCK, BLOCK)
        dk_cur, dv_cur, dk_prev, dv_prev = [], [], [], []
        ds_acc = jnp.zeros((8, 128), F32)
        for hk in range(N_KV_HEADS):
            ks = slice(hk * HEAD_DIM, (hk + 1) * HEAD_DIM)
            qg = (_stack_heads(q_ref, hk, F32) * ATTN_SCALE).astype(BF16)
            dog = _stack_heads(dy_s, hk, F32)
            og = _stack_heads(y_ref, hk, F32)
            dog_b = dog.astype(BF16)
            kp = kp_ref[:, ks].astype(BF16)
            kc = kc_ref[:, ks].astype(BF16)
            vp = vp_ref[:, ks].astype(BF16)
            vc = vc_ref[:, ks].astype(BF16)
            s_prev = _dot_nt(qg, kp)
            s_cur = _dot_nt(qg, kc)
            dp_prev = _dot_nt(dog_b, vp)
            dp_cur = _dot_nt(dog_b, vc)
            dvec = jnp.sum(dog * og, axis=-1, keepdims=True)
            pp, pc, dsp, dsc = [], [], [], []
            for g in range(GROUP):
                h = GROUP * hk + g
                rows = slice(g * BLOCK, (g + 1) * BLOCK)
                p_prev, p_cur, p_sink = _attn_probs(
                    s_prev[rows], s_cur[rows], sink_ref[h], bias_ref[h], lse_ref[:, h:h + 1])
                d_h = dvec[rows]
                pp.append(p_prev.astype(BF16))
                pc.append(p_cur.astype(BF16))
                dsp.append((p_prev * (dp_prev[rows] - d_h)).astype(BF16))
                dsc.append((p_cur * (dp_cur[rows] - d_h)).astype(BF16))
                dsink = -jnp.sum(p_sink * d_h, axis=0, keepdims=True)
                ds_acc = ds_acc + jnp.where(jnp.logical_and(lane == h, sub == 1), dsink, 0.0)
            pp = jnp.concatenate(pp, axis=0)
            pc = jnp.concatenate(pc, axis=0)
            dsp = jnp.concatenate(dsp, axis=0)
            dsc = jnp.concatenate(dsc, axis=0)
            dqg = (_dot(dsp, kp) + _dot(dsc, kc)) * ATTN_SCALE
            for g in range(GROUP):
                h = GROUP * hk + g
                dq_ref[:, h * HEAD_DIM:(h + 1) * HEAD_DIM] = dqg[g * BLOCK:(g + 1) * BLOCK].astype(BF16)
            dk_ref[pl.ds(r_cur, BLOCK), ks] = _dot_tn(dsc, qg)
            dv_ref[pl.ds(r_cur, BLOCK), ks] = _dot_tn(pc, dog_b)
            dk_prev.append(_dot_tn(dsp, qg))
            dv_prev.append(_dot_tn(pp, dog_b))
        ds_ref[:, 0:128] += ds_acc

        @pl.when(i > 0)
        def _():
            for hk in range(N_KV_HEADS):
                ks = slice(hk * HEAD_DIM, (hk + 1) * HEAD_DIM)
                dk_ref[pl.ds(r_prev, BLOCK), ks] += dk_prev[hk]
                dv_ref[pl.ds(r_prev, BLOCK), ks] += dv_prev[hk]

    prev = lambda c: (lambda i: (jnp.maximum(i - 1, 0), c))
    cur = lambda c: (lambda i: (i, c))
    blk = pl.BlockSpec((BLOCK, 1024), lambda i: (i, 0))
    whole = pl.BlockSpec((T, D_KV), lambda i: (0, 0))
    return pl.pallas_call(
        body,
        name="attn_bwd",
        grid=(nb,),
        in_specs=[
            pl.BlockSpec(memory_space=pltpu.SMEM),
            _bias_spec(),
            pl.BlockSpec((BLOCK, 1024), lambda i: (i, COL_Q // 4)),
            pl.BlockSpec((BLOCK, D_KV), prev(COL_K)),
            pl.BlockSpec((BLOCK, D_KV), cur(COL_K)),
            pl.BlockSpec((BLOCK, D_KV), prev(COL_V)),
            pl.BlockSpec((BLOCK, D_KV), cur(COL_V)),
            pl.BlockSpec((BLOCK, 512), lambda i: (i, COL_ATTN_GATE // 2)),
            pl.BlockSpec((BLOCK, 512), lambda i: (i, COL_ATTN_GATE // 2 + 1)),
            blk,
            pl.BlockSpec((BLOCK, N_Q_HEADS), lambda i: (i, 0)),
            blk,
            pl.BlockSpec((8, 128), lambda i: (0, 0)),
        ],
        out_specs=[blk, whole, whole, blk, pl.BlockSpec((8, 1024), lambda i: (0, 0))],
        out_shape=[_sds((T, 1024), BF16), _sds((T, D_KV), F32), _sds((T, D_KV), F32), _sds((T, 1024), BF16),
                   _sds((8, 1024), F32)],
        scratch_shapes=[pltpu.VMEM((BLOCK, 1024), F32)],
        compiler_params=_params(("arbitrary",), 48),
    )(sinks, *_hbm(bias, proj, proj, proj, proj, proj, proj, proj, y_attn, lse, dz_attn, token))


def _head(x, target, z_rnn, z_attn, proj, b_gate, g_post, w_rnn_out, w_attn_out, w_out):
    T = x.shape[0]
    tm = 256

    def body(x_ref, t_ref, zr_ref, za_ref, ml0_ref, ml1_ref, ml2_ref, ml3_ref, bg_ref, gp_ref, wr_ref, wa_ref, wo_ref,
             dyx_ref, dzr_ref, dza_ref, dml_ref, dout_ref, dbr_ref, dba_ref, mt_ref, zat_ref, sm_ref):
        @pl.when(pl.program_id(0) == 0)
        def _():
            sm_ref[...] = jnp.zeros_like(sm_ref)

        wr, wa, wo = wr_ref[...], wa_ref[...], wo_ref[...]
        br_rnn = _dot(zr_ref[...], wr)
        br_attn = _dot(za_ref[...], wa)
        zat_ref[...] = za_ref[...].astype(F32).T.astype(BF16)
        ml_rnn = jnp.concatenate([ml0_ref[...], ml1_ref[...]], axis=1).astype(F32)
        ml_attn = jnp.concatenate([ml2_ref[...], ml3_ref[...]], axis=1).astype(F32)
        g_rnn = _sigmoid(ml_rnn + bg_ref[:, 0:D_MODEL])
        g_attn = _sigmoid(ml_attn + bg_ref[:, D_MODEL:2 * D_MODEL])
        merged = g_rnn * br_rnn + g_attn * br_attn
        mb = merged.astype(BF16)
        mt_ref[...] = merged.T.astype(BF16)
        out = _dot(mb, wo)
        rstd = lax.rsqrt(jnp.mean(out * out, axis=-1, keepdims=True) + EPS)
        n = out * rstd
        gp = gp_ref[...]
        err = (x_ref[...] + n * gp) - t_ref[...]
        sm_ref[pl.ds(3, 1), :] += 0.5 * jnp.sum(jnp.mean(err * err, axis=-1, keepdims=True), axis=0, keepdims=True)
        dy = err * (1.0 / D_MODEL)
        dyx_ref[...] = dy
        sm_ref[pl.ds(0, 1), :] += jnp.sum(dy * n, axis=0, keepdims=True)
        dn = dy * gp
        dout = (rstd * (dn - n * jnp.mean(dn * n, axis=-1, keepdims=True))).astype(BF16)
        dout_ref[...] = dout
        dmerged = _dot_nt(dout, wo)
        dml_r = (dmerged * br_rnn) * (g_rnn * (1.0 - g_rnn))
        dml_a = (dmerged * br_attn) * (g_attn * (1.0 - g_attn))
        dml_ref[:, 0:D_MODEL] = dml_r.astype(BF16)
        dml_ref[:, D_MODEL:2 * D_MODEL] = dml_a.astype(BF16)
        sm_ref[pl.ds(1, 1), :] += jnp.sum(dml_r, axis=0, keepdims=True)
        sm_ref[pl.ds(2, 1), :] += jnp.sum(dml_a, axis=0, keepdims=True)
        dbr = (dmerged * g_rnn).astype(BF16)
        dba = (dmerged * g_attn).astype(BF16)
        dbr_ref[...] = dbr
        dba_ref[...] = dba
        dzr_ref[...] = _dot_nt(dbr, wr)
        dza_ref[...] = _dot_nt(dba, wa)

    tile = pl.BlockSpec((tm, D_MODEL), lambda i: (i, 0))
    wspec = pl.BlockSpec((D_MODEL, D_MODEL), lambda i: (0, 0))
    ml = lambda q: pl.BlockSpec((tm, 512), lambda i: (i, COL_MERGE // 2 + q))
    return pl.pallas_call(
        body,
        name="head",
        grid=(T // tm,),
        in_specs=[
            tile, tile, tile, tile,
            ml(0), ml(1), ml(2), ml(3),
            pl.BlockSpec((1, 2 * D_MODEL), lambda i: (0, 0)),
            pl.BlockSpec((1, D_MODEL), lambda i: (0, 0)),
            wspec, wspec, wspec,
        ],
        out_specs=[
            tile, tile, tile,
            pl.BlockSpec((tm, 2 * D_MODEL), lambda i: (i, 0)),
            tile, tile, tile,
            pl.BlockSpec((D_MODEL, tm), lambda i: (0, i)), pl.BlockSpec((D_MODEL, tm), lambda i: (0, i)),
            pl.BlockSpec((8, D_MODEL), lambda i: (0, 0)),
        ],
        out_shape=[
            _sds((T, D_MODEL), F32), _sds((T, D_MODEL), F32), _sds((T, D_MODEL), F32),
            _sds((T, 2 * D_MODEL), BF16),
            _sds((T, D_MODEL), BF16), _sds((T, D_MODEL), BF16), _sds((T, D_MODEL), BF16),
            _sds((D_MODEL, T), BF16), _sds((D_MODEL, T), BF16),
            _sds((8, D_MODEL), F32),
        ],
        compiler_params=_params(("arbitrary",), 56),
    )(*_hbm(x, target, z_rnn, z_attn, proj, proj, proj, proj, b_gate, g_post, w_rnn_out, w_attn_out, w_out))


def _matmul_t(at, b, name):
    M, T = at.shape
    N = b.shape[1]
    tk = min(1024, T)
    nt = T // tk

    def body(a_ref, b_ref, o_ref, ob_ref):
        @pl.when(pl.program_id(0) == 0)
        def _():
            o_ref[...] = jnp.zeros_like(o_ref)

        o_ref[...] += _dot(a_ref[...], b_ref[...])

        @pl.when(pl.program_id(0) == nt - 1)
        def _():
            ob_ref[...] = o_ref[...].astype(BF16)

    whole = pl.BlockSpec((M, N), lambda t: (0, 0))
    return pl.pallas_call(
        body,
        name=name,
        grid=(nt,),
        in_specs=[pl.BlockSpec((M, tk), lambda t: (0, t)), pl.BlockSpec((tk, N), lambda t: (t, 0))],
        out_specs=[whole, whole],
        out_shape=[_sds((M, N), F32), _sds((M, N), BF16)],
        compiler_params=_params(("arbitrary",), 48),
    )(*_hbm(at, b))


DPROJ_WIDTHS = (D_RNN, D_RNN, 1024, D_KV, D_KV, 1024, 2 * D_MODEL)


def _dproj_segments():
    segs, start = [[] for _ in range(N_CHIPS)], 0
    for p, width in enumerate(DPROJ_WIDTHS):
        for c in range(N_CHIPS):
            lo, hi = max(start, c * W_IN_SHARD), min(start + width, (c + 1) * W_IN_SHARD)
            if lo < hi:
                segs[c].append((p, lo - start, hi - start, lo - c * W_IN_SHARD, hi - c * W_IN_SHARD))
        start += width
    return segs


def _dh_bwd(pieces, w_in_g, x, dyx, g_pre, token):
    T = x.shape[0]
    tm = min(512, T)
    n = len(pieces)
    segs = _dproj_segments()

    def body(*refs):
        p_refs, w_hbm, x_ref, dyx_ref, g_ref = refs[0:n], refs[n], refs[n + 1], refs[n + 2], refs[n + 3]
        gx_ref, dg_ref, w_ref, w_sems = refs[n + 5], refs[n + 6], refs[n + 7], refs[n + 8]
        first = pl.program_id(0) == 0
        w_copies = [pltpu.make_async_copy(w_hbm.at[c], w_ref.at[c], w_sems.at[c]) for c in range(N_CHIPS)]

        @pl.when(first)
        def _():
            for cp in w_copies:
                cp.start()
            dg_ref[...] = jnp.zeros_like(dg_ref)

        dh = None
        for c in range(N_CHIPS):
            pl.when(first)(w_copies[c].wait)
            for p, a0, a1, u0, u1 in segs[c]:
                part = _dot_nt(p_refs[p][:, a0:a1].astype(BF16), w_ref[c, :, u0:u1])
                dh = part if dh is None else dh + part
        xv = x_ref[...]
        rstd = lax.rsqrt(jnp.mean(xv * xv, axis=-1, keepdims=True) + EPS)
        nx = xv * rstd
        dhg = dh * g_ref[...]
        gx_ref[...] = dyx_ref[...] + rstd * (dhg - nx * jnp.mean(dhg * nx, axis=-1, keepdims=True))
        dg_ref[pl.ds(0, 1), :] += jnp.sum(dh * nx, axis=0, keepdims=True)

    tile = pl.BlockSpec((tm, D_MODEL), lambda i: (i, 0))
    return pl.pallas_call(
        body,
        name="dh_bwd",
        grid=(T // tm,),
        in_specs=[pl.BlockSpec((tm, w), lambda i: (i, 0)) for w in DPROJ_WIDTHS] + [
            ANY, tile, tile,
            pl.BlockSpec((1, D_MODEL), lambda i: (0, 0)),
            pl.BlockSpec((8, 128), lambda i: (0, 0)),
        ],
        out_specs=[tile, pl.BlockSpec((8, D_MODEL), lambda i: (0, 0))],
        out_shape=[_sds((T, D_MODEL), F32), _sds((8, D_MODEL), F32)],
        scratch_shapes=[pltpu.VMEM(w_in_g.shape, BF16), pltpu.SemaphoreType.DMA((N_CHIPS,))],
        compiler_params=_params(("arbitrary",), 56),
    )(*_hbm(*pieces, w_in_g, x, dyx, g_pre, token))


def _dw_in(ht, pieces):
    T = ht.shape[1]
    tk = min(512, T)
    nt = T // tk
    n = len(pieces)
    segs = _dproj_segments()

    def body(*refs):
        h_ref, p_refs, o_ref, ob_ref = refs[0], refs[1:n + 1], refs[n + 1], refs[n + 2]

        @pl.when(pl.program_id(1) == 0)
        def _():
            o_ref[...] = jnp.zeros_like(o_ref)

        for c in range(N_CHIPS):
            @pl.when(pl.program_id(0) == c)
            def _():
                for p, a0, a1, u0, u1 in segs[c]:
                    o_ref[:, u0:u1] += _dot(h_ref[...], p_refs[p][:, a0:a1].astype(BF16))

        @pl.when(pl.program_id(1) == nt - 1)
        def _():
            ob_ref[...] = o_ref[...].astype(BF16)

    def piece_spec(p):
        chips = [c for c in range(N_CHIPS) if any(s[0] == p for s in segs[c])]

        def index(c, t):
            used = functools.reduce(jnp.logical_or, [c == k for k in chips])
            return (jnp.where(used, t, 0), 0)

        return pl.BlockSpec((tk, DPROJ_WIDTHS[p]), index)

    return pl.pallas_call(
        body,
        name="dw_in",
        grid=(N_CHIPS, nt),
        in_specs=[pl.BlockSpec((D_MODEL, tk), lambda c, t: (0, t))] + [piece_spec(p) for p in range(n)],
        out_specs=[pl.BlockSpec((None, D_MODEL, W_IN_SHARD), lambda c, t: (c, 0, 0))] * 2,
        out_shape=[_sds((N_CHIPS, D_MODEL, W_IN_SHARD), F32), _sds((N_CHIPS, D_MODEL, W_IN_SHARD), BF16)],
        compiler_params=_params(("parallel", "arbitrary"), 56),
    )(*_hbm(ht, *pieces))


ELEMENTWISE_TILE_BYTES = MIB


def _row_tile(rows, cols, limit=ELEMENTWISE_TILE_BYTES):
    if rows * cols * 4 <= limit:
        return rows
    for t in (512, 256, 128, 64, 32, 16, 8):
        if rows % t == 0 and t * cols * 4 <= limit:
            return t
    return rows


def _chip_sum(ps, gots, chip_core, name):
    n = len(ps)
    h, C = ps[0].shape
    tr = _row_tile(h, C * n)
    nt = h // tr

    def body(jc_ref, *refs):
        for a in range(n):
            p_ref, g0_ref, g1_ref, g2_ref, o_ref = refs[a], refs[n + 3 * a], refs[n + 3 * a + 1], refs[n + 3 * a + 2], \
                refs[4 * n + a]
            o_ref[...] = ((p_ref[...] + g0_ref[...].astype(F32)) + g1_ref[...].astype(F32)) + g2_ref[...].astype(F32)

    rel = lambda r: pl.BlockSpec((None, tr, C), lambda i, jc_ref: (r, i, 0))
    outs = pl.pallas_call(
        body,
        name=name,
        grid_spec=pltpu.PrefetchScalarGridSpec(
            num_scalar_prefetch=1,
            grid=(nt,),
            in_specs=[pl.BlockSpec((tr, C), lambda i, jc_ref: (i, 0))] * n + [rel(0), rel(1), rel(2)] * n,
            out_specs=[pl.BlockSpec((tr, C), lambda i, jc_ref: (jc_ref[1] * nt + i, 0))] * n,
        ),
        out_shape=[_sds((2 * h, C), F32)] * n,
        compiler_params=_params(("parallel",), 48),
    )(chip_core, *_hbm(*ps, *[g for got in gots for g in (got, got, got)]))
    return list(outs)


def _place_shards(shards, chip, name):
    n = len(shards)
    tiles = [_row_tile(s.shape[0], s.shape[1]) for s in shards]
    steps = max(s.shape[0] // t for s, t in zip(shards, tiles))
    tiles = [s.shape[0] // steps for s in shards]

    def body(j_ref, *refs):
        for a in range(n):
            refs[n + a][...] = refs[a][...].astype(BF16)

    return pl.pallas_call(
        body,
        name=name,
        grid_spec=pltpu.PrefetchScalarGridSpec(
            num_scalar_prefetch=1,
            grid=(steps,),
            in_specs=[pl.BlockSpec((t, s.shape[1]), lambda i, j_ref: (i, 0)) for s, t in zip(shards, tiles)],
            out_specs=[pl.BlockSpec((None, t, s.shape[1]), lambda i, j_ref: (j_ref[0], i, 0))
                       for s, t in zip(shards, tiles)],
        ),
        out_shape=[_sds((N_CHIPS,) + s.shape, BF16) for s in shards],
        compiler_params=_params(("parallel",), 48),
    )(chip, *_hbm(*shards))


def _adamw_update(w, g, m, v):
    c1 = 1.0 - ADAM_B1 ** ADAM_STEP
    c2 = 1.0 - ADAM_B2 ** ADAM_STEP
    nm = ADAM_B1 * m + (1.0 - ADAM_B1) * g
    nv = ADAM_B2 * v + (1.0 - ADAM_B2) * (g * g)
    return (-ADAM_LR) * ((nm / c1) / (jnp.sqrt(nv / c2) + ADAM_EPS) + ADAM_WD * w), nm, nv


def _adamw(params, name):
    n = len(params)
    R, C = params[0][0].shape
    tr = _row_tile(R, C * n)

    def body(*refs):
        for a in range(n):
            w_ref, g_ref, m_ref, v_ref = refs[4 * a:4 * a + 4]
            d_ref, nm_ref, nv_ref, go_ref = refs[4 * n + 4 * a:4 * n + 4 * a + 4]
            g = g_ref[...]
            d_ref[...], nm_ref[...], nv_ref[...] = _adamw_update(w_ref[...], g, m_ref[...], v_ref[...])
            go_ref[...] = g

    spec = pl.BlockSpec((tr, C), lambda i: (i, 0))
    outs = pl.pallas_call(
        body, name=name, grid=(R // tr,), in_specs=[spec] * (4 * n), out_specs=[spec] * (4 * n),
        out_shape=[_sds((R, C), F32)] * (4 * n), compiler_params=_params(("parallel",), 48),
    )(*_hbm(*[t for p in params for t in p]))
    return [tuple(outs[4 * a:4 * a + 4]) for a in range(n)]


def _adamw_whole(params, name):
    n = len(params)

    def body(*refs):
        for a in range(n):
            w_ref, g_ref, m_ref, v_ref = refs[4 * a:4 * a + 4]
            d_ref, nm_ref, nv_ref = refs[4 * n + 3 * a:4 * n + 3 * a + 3]
            d_ref[...], nm_ref[...], nv_ref[...] = _adamw_update(w_ref[...], g_ref[...], m_ref[...], v_ref[...])

    def whole(t):
        return pl.BlockSpec(t.shape, lambda i: (0,) * t.ndim)

    flat = [t for p in params for t in p]
    like = [p[0] for p in params for _ in range(3)]
    outs = pl.pallas_call(
        body, name=name, grid=(1,), in_specs=[whole(t) for t in flat], out_specs=[whole(t) for t in like],
        out_shape=[_sds(t.shape, F32) for t in like], compiler_params=_params(("arbitrary",), 48),
    )(*_hbm(*flat))
    return [tuple(outs[3 * a:3 * a + 3]) for a in range(n)]


def _place():
    return lax.axis_index("x"), lax.axis_index("y"), lax.axis_index("c")


def _chip_of(x, y, r):
    return (x ^ (r >> 1), y ^ (r & 1))


ANY = pl.BlockSpec(memory_space=pl.ANY)


def _gather_weights(placed, cw8):
    nbig = len(placed)
    halves = [s.shape[1] // 2 for s in placed]
    pieces = [max(1, h // 64) for h in halves]
    rows = [h // p for h, p in zip(halves, pieces)]
    order = [(a, q) for q in range(max(pieces)) for a in range(nbig) if q < pieces[a]]
    ici_sem = {(a, q, r): 3 * i + (r - 1) for i, (a, q) in enumerate(order) for r in (1, 2, 3)}
    cw_sem = {r: 3 * len(order) + (r - 1) for r in (1, 2, 3)}
    d2d_sem = {key: 3 * len(order) + 3 + k for key, k in ici_sem.items()}
    nsem = 6 * len(order) + 3

    def body(*refs):
        cw_ref, dsts, gcw_ref = refs[nbig], refs[nbig + 1:2 * nbig + 1], refs[2 * nbig + 1]
        send_sems, recv_sems = refs[2 * nbig + 2:]
        x, y, c = _place()
        j = 2 * x + y

        def piece_rows(a, q, core):
            return pl.ds(pl.multiple_of(core * halves[a] + q * rows[a], 16), rows[a])

        def ici(a, q, r):
            tx, ty = _chip_of(x, y, r)
            k = ici_sem[(a, q, r)]
            region = dsts[a].at[j, piece_rows(a, q, c), :]
            return pltpu.make_async_remote_copy(
                src_ref=region, dst_ref=region, send_sem=send_sems.at[k], recv_sem=recv_sems.at[k],
                device_id=(tx, ty, c), device_id_type=MESH)

        def ici_landed(a, q, r):
            tx, ty = _chip_of(x, y, r)
            k = ici_sem[(a, q, r)]
            region = dsts[a].at[2 * tx + ty, piece_rows(a, q, c), :]
            return pltpu.make_async_remote_copy(
                src_ref=region, dst_ref=region, send_sem=send_sems.at[k], recv_sem=recv_sems.at[k],
                device_id=(tx, ty, c), device_id_type=MESH)

        def d2d(a, q, r, core):
            tx, ty = _chip_of(x, y, r)
            k = d2d_sem[(a, q, r)]
            region = dsts[a].at[2 * tx + ty, piece_rows(a, q, core), :]
            return pltpu.make_async_remote_copy(
                src_ref=region, dst_ref=region, send_sem=send_sems.at[k], recv_sem=recv_sems.at[k],
                device_id=(x, y, 1 - c), device_id_type=MESH)

        def cw_copy(r):
            tx, ty = _chip_of(x, y, r)
            k = cw_sem[r]
            return pltpu.make_async_remote_copy(
                src_ref=cw_ref, dst_ref=gcw_ref.at[j], send_sem=send_sems.at[k], recv_sem=recv_sems.at[k],
                device_id=(tx, ty, c), device_id_type=MESH)

        def cw_landed(r):
            tx, ty = _chip_of(x, y, r)
            k = cw_sem[r]
            region = gcw_ref.at[2 * tx + ty]
            return pltpu.make_async_remote_copy(
                src_ref=region, dst_ref=region, send_sem=send_sems.at[k], recv_sem=recv_sems.at[k],
                device_id=(tx, ty, c), device_id_type=MESH)

        def relay(a, q, origin, to):
            ox, oy = _chip_of(x, y, origin)
            tx, ty = _chip_of(x, y, to)
            k = ici_sem[(a, q, 3)]
            region = dsts[a].at[2 * ox + oy, piece_rows(a, q, c), :]
            return pltpu.make_async_remote_copy(
                src_ref=region, dst_ref=region, send_sem=send_sems.at[k], recv_sem=recv_sems.at[k],
                device_id=(tx, ty, c), device_id_type=MESH)

        first = [ici(a, q, r) for (a, q) in order for r in (1, 2)] + [cw_copy(r) for r in (1, 2, 3)]
        for cp in first:
            cp.start()
        passed = []
        for (a, q) in order:
            for r in (1, 2):
                ici_landed(a, q, r).wait_recv()
                if q % 2 == r - 1:
                    cp = relay(a, q, r, 3 - r)
                    cp.start()
                    passed.append(cp)
                cp = d2d(a, q, r, c)
                cp.start()
                passed.append(cp)
        for (a, q) in order:
            ici_landed(a, q, 3).wait_recv()
            cp = d2d(a, q, 3, c)
            cp.start()
            passed.append(cp)
        for r in (1, 2, 3):
            cw_landed(r).wait_recv()
        for (a, q) in order:
            for r in (1, 2, 3):
                d2d(a, q, r, 1 - c).wait_recv()
        for cp in first + passed:
            cp.wait_send()

    return pl.pallas_call(
        body,
        name="gather_weights",
        in_specs=[ANY] * (nbig + 1),
        out_specs=[ANY] * (nbig + 1),
        out_shape=[_sds(s.shape, s.dtype) for s in placed] + [_sds((N_CHIPS,) + cw8.shape, cw8.dtype)],
        input_output_aliases={a: a for a in range(nbig)},
        scratch_shapes=[pltpu.SemaphoreType.DMA((nsem,)), pltpu.SemaphoreType.DMA((nsem,))],
    )(*placed, cw8)


def _gather_late_start(placed, after, name):
    n = len(placed)
    halves = [s.shape[1] // 2 for s in placed]

    def body(*refs):
        g_refs = refs[0:n]
        send_sems, recv_sems, token = refs[n + 1], refs[n + 2], refs[-1]
        x, y, c = _place()
        j = 2 * x + y
        for a in range(n):
            mine = g_refs[a].at[j, pl.ds(pl.multiple_of(c * halves[a], 16), halves[a]), :]
            for r in (1, 2, 3):
                tx, ty = _chip_of(x, y, r)
                for to_core in (0, 1):
                    k = ((a * 3 + (r - 1)) * 2 + c) * 2 + to_core
                    pltpu.make_async_remote_copy(
                        src_ref=mine, dst_ref=mine, send_sem=send_sems.at[k], recv_sem=recv_sems.at[k],
                        device_id=(tx, ty, to_core), device_id_type=MESH).start()
        token[...] = jnp.zeros_like(token)

    hbm = lambda t: pltpu.HBM(t.shape, t.dtype)
    keep = lambda t: pltpu.with_memory_space_constraint(t, pltpu.HBM)
    nsem = 12 * n
    outs = pl.pallas_call(
        body,
        name=name,
        in_specs=[HBM] * n + [ANY],
        out_specs=(SEM, SEM, *[HBM] * n, pl.BlockSpec(memory_space=pltpu.VMEM)),
        out_shape=(pltpu.SemaphoreType.DMA((nsem,)), pltpu.SemaphoreType.DMA((nsem,)), *[hbm(p) for p in placed],
                   jax.ShapeDtypeStruct((8, 128), F32)),
        input_output_aliases={i: 2 + i for i in range(n)},
        compiler_params=pltpu.CompilerParams(has_side_effects=DATAFLOW),
    )(*[keep(p) for p in placed], after)
    return outs[0], outs[1], list(outs[2:2 + n]), outs[-1]


def _gather_late_wait(send_sems, recv_sems, thru, after, name):
    n = len(thru)
    halves = [s.shape[1] // 2 for s in thru]

    def body(*refs):
        g_refs = refs[0:n]
        send_sems, recv_sems = refs[n], refs[n + 1]
        x, y, c = _place()
        j = 2 * x + y
        for a in range(n):
            mine = g_refs[a].at[j, pl.ds(pl.multiple_of(c * halves[a], 16), halves[a]), :]
            for r in (1, 2, 3):
                tx, ty = _chip_of(x, y, r)
                for other in (0, 1):
                    k_out = ((a * 3 + (r - 1)) * 2 + c) * 2 + other
                    pltpu.make_async_remote_copy(
                        src_ref=mine, dst_ref=mine, send_sem=send_sems.at[k_out], recv_sem=recv_sems.at[k_out],
                        device_id=(tx, ty, other), device_id_type=MESH).wait_send()
                    k_in = ((a * 3 + (r - 1)) * 2 + other) * 2 + c
                    theirs = g_refs[a].at[2 * tx + ty, pl.ds(other * halves[a], halves[a]), :]
                    pltpu.make_async_remote_copy(
                        src_ref=theirs, dst_ref=theirs, send_sem=send_sems.at[k_in], recv_sem=recv_sems.at[k_in],
                        device_id=(tx, ty, other), device_id_type=MESH).wait_recv()

    hbm = lambda t: pltpu.HBM(t.shape, t.dtype)
    outs = pl.pallas_call(
        body,
        name=name,
        in_specs=[HBM] * n + [SEM, SEM, ANY],
        out_specs=[HBM] * n,
        out_shape=[hbm(t) for t in thru],
        input_output_aliases={i: i for i in range(n)},
        compiler_params=pltpu.CompilerParams(has_side_effects=DATAFLOW),
    )(*thru, send_sems, recv_sems, after)
    return list(outs)


D2D_PIECE_ROWS = 64
PAIR_SUM_TILE_BYTES = 2 * MIB


def _pair_sum(gs, gbs, chip_core, name):
    n = len(gs)
    nch, R, C = gs[0].shape
    h = R // 2
    tr = _row_tile(h, C * n, PAIR_SUM_TILE_BYTES)
    nt = h // tr
    rows = min(D2D_PIECE_ROWS, tr)

    def body(jc_ref, *refs):
        g_refs, gb_refs, p_refs, pb_refs = refs[0:n], refs[n:2 * n], refs[2 * n:3 * n], refs[3 * n:4 * n]
        got_refs, send_sems, recv_sems = refs[4 * n:5 * n], refs[5 * n], refs[5 * n + 1]
        i, j = pl.program_id(0), pl.program_id(1)
        x, y, c = _place()

        def copy(a, ti, tj, first, count):
            src_rows = pl.ds(pl.multiple_of((1 - c) * h + ti * tr + first, 16), count)
            dst_rows = pl.ds(pl.multiple_of(ti * tr + first, 16), count)
            return pltpu.make_async_remote_copy(
                src_ref=gb_refs[a].at[tj, src_rows, :], dst_ref=got_refs[a].at[tj, dst_rows, :],
                send_sem=send_sems.at[a, ti, tj], recv_sem=recv_sems.at[a, ti, tj],
                device_id=(x, y, 1 - c), device_id_type=MESH)

        @pl.when((i == 0) & (j == 0))
        def _():
            for ti in range(nt):
                for tj in range(nch):
                    for a in range(n):
                        for q in range(tr // rows):
                            copy(a, ti, tj, q * rows, rows).start()

        for a in range(n):
            copy(a, i, j, 0, tr).wait()
            s = g_refs[a][...] + got_refs[a][j, pl.ds(pl.multiple_of(i * tr, 16), tr), :].astype(F32)
            pb_refs[a][...] = s.astype(BF16)

            @pl.when(j == jc_ref[0])
            def _():
                p_refs[a][...] = s

    by_chip = pl.BlockSpec((None, tr, C), lambda i, j, jc_ref: (j, i, 0))
    outs = pl.pallas_call(
        body,
        name=name,
        grid_spec=pltpu.PrefetchScalarGridSpec(
            num_scalar_prefetch=1,
            grid=(nt, nch),
            in_specs=[pl.BlockSpec((None, tr, C), lambda i, j, jc_ref: (j, jc_ref[1] * nt + i, 0))] * n + [ANY] * n,
            out_specs=[pl.BlockSpec((tr, C), lambda i, j, jc_ref: (i, 0))] * n + [by_chip] * n,
            scratch_shapes=[pltpu.VMEM((nch, h, C), BF16)] * n + [pltpu.SemaphoreType.DMA((n, nt, nch))] * 2,
        ),
        out_shape=[_sds((h, C), F32)] * n + [_sds((nch, h, C), BF16)] * n,
        compiler_params=_params(("arbitrary", "arbitrary"), 48),
    )(chip_core, *_hbm(*gs, *gbs))
    return list(outs[:n]), list(outs[n:])


HBM = pl.BlockSpec(memory_space=pltpu.HBM)
SEM = pl.BlockSpec(memory_space=pltpu.SEMAPHORE)
DATAFLOW = pltpu.SideEffectType.DATAFLOW_SIDE_EFFECTING


def _chip_copy(p_refs, land_refs, send_sems, recv_sems, a, r, blocked):
    x, y, c = _place()
    tx, ty = _chip_of(x, y, r)
    k = a * 3 + (r - 1)
    return pltpu.make_async_remote_copy(
        src_ref=p_refs[a].at[2 * tx + ty] if blocked else p_refs[a], dst_ref=land_refs[a].at[r - 1],
        send_sem=send_sems.at[k], recv_sem=recv_sems.at[k], device_id=(tx, ty, c), device_id_type=MESH)


def _chip_exchange_start(psums, name, blocked=True):
    n = len(psums)
    lands = [lax.empty((3,) + (p.shape[1:] if blocked else p.shape), p.dtype) for p in psums]

    def body(*refs):
        p_refs, land_refs = refs[0:n], refs[n:2 * n]
        send_sems, recv_sems, token = refs[2 * n], refs[2 * n + 1], refs[-1]
        for a in range(n):
            for r in (1, 2, 3):
                _chip_copy(p_refs, land_refs, send_sems, recv_sems, a, r, blocked).start()
        token[...] = jnp.zeros_like(token)

    hbm = lambda t: pltpu.HBM(t.shape, t.dtype)
    keep = lambda t: pltpu.with_memory_space_constraint(t, pltpu.HBM)
    outs = pl.pallas_call(
        body,
        name=name,
        in_specs=[HBM] * (2 * n),
        out_specs=(SEM, SEM, *[HBM] * (2 * n), pl.BlockSpec(memory_space=pltpu.VMEM)),
        out_shape=(pltpu.SemaphoreType.DMA((3 * n,)), pltpu.SemaphoreType.DMA((3 * n,)),
                   *[hbm(p) for p in psums], *[hbm(l) for l in lands], _sds((8, 128), F32)),
        input_output_aliases={i: 2 + i for i in range(2 * n)},
        compiler_params=pltpu.CompilerParams(has_side_effects=DATAFLOW),
    )(*[keep(p) for p in psums], *[keep(l) for l in lands])
    return outs[0], outs[1], list(outs[2:2 + n]), list(outs[2 + n:2 + 2 * n]), outs[-1]


def _chip_exchange_wait(send_sems, recv_sems, p_thru, land_thru, after, name, blocked=True):
    n = len(p_thru)

    def body(*refs):
        p_refs, land_refs = refs[0:n], refs[n:2 * n]
        send_sems, recv_sems = refs[2 * n], refs[2 * n + 1]
        for a in range(n):
            for r in (1, 2, 3):
                copy = _chip_copy(p_refs, land_refs, send_sems, recv_sems, a, r, blocked)
                copy.wait_send()
                copy.wait_recv()

    hbm = lambda t: pltpu.HBM(t.shape, t.dtype)
    outs = pl.pallas_call(
        body,
        name=name,
        in_specs=[HBM] * (2 * n) + [SEM, SEM, ANY],
        out_specs=[HBM] * (2 * n),
        out_shape=[hbm(p) for p in p_thru] + [hbm(l) for l in land_thru],
        input_output_aliases={i: i for i in range(2 * n)},
        compiler_params=pltpu.CompilerParams(has_side_effects=DATAFLOW),
    )(*p_thru, *land_thru, send_sems, recv_sems, after)
    return list(outs[0:n]), list(outs[n:2 * n])


def _pair_share(fulls):
    n = len(fulls)
    halves = [f.shape[0] // 2 for f in fulls]

    def body(*refs):
        full_refs = refs[n:2 * n]
        send_sems, recv_sems = refs[2 * n:]
        x, y, c = _place()

        def half_of(a, core):
            return full_refs[a].at[pl.ds(pl.multiple_of(core * halves[a], 8), halves[a]), :]

        def remote(a, src, dst):
            return pltpu.make_async_remote_copy(
                src_ref=src, dst_ref=dst, send_sem=send_sems.at[a], recv_sem=recv_sems.at[a],
                device_id=(x, y, 1 - c), device_id_type=MESH)

        for a in range(n):
            for q in range(halves[a] // D2D_PIECE_ROWS):
                piece = full_refs[a].at[
                    pl.ds(pl.multiple_of(c * halves[a] + q * D2D_PIECE_ROWS, 8), D2D_PIECE_ROWS), :]
                remote(a, piece, piece).start()
        for a in range(n):
            remote(a, half_of(a, c), half_of(a, c)).wait_send()
            remote(a, half_of(a, 1 - c), half_of(a, 1 - c)).wait_recv()

    return pl.pallas_call(
        body,
        name="pair_share",
        in_specs=[ANY] * n,
        out_specs=[ANY] * n,
        out_shape=[_sds(f.shape, F32) for f in fulls],
        input_output_aliases={a: a for a in range(n)},
        scratch_shapes=[pltpu.SemaphoreType.DMA((n,)), pltpu.SemaphoreType.DMA((n,))],
    )(*fulls)


def _small_pair_sum(s):
    R, C = s.shape
    V = SMALL_VECTOR_ROWS

    def body(s_ref, v_ref, m_ref, sib, send_sem, recv_sem):
        x, y, c = _place()

        def to_sib(src, dst):
            return pltpu.make_async_remote_copy(
                src_ref=src, dst_ref=dst, send_sem=send_sem, recv_sem=recv_sem,
                device_id=(x, y, 1 - c), device_id_type=MESH)

        for q in range(R // 8):
            to_sib(s_ref.at[pl.ds(8 * q, 8), :], sib.at[pl.ds(8 * q, 8), :]).start()
        to_sib(s_ref, sib).wait()
        v_ref[...] = s_ref[pl.ds(0, V), :] + sib[pl.ds(0, V), :]
        m_ref[...] = (s_ref[pl.ds(V, R - V), :] + sib[pl.ds(V, R - V), :]).astype(BF16)

    return pl.pallas_call(
        body,
        name="small_pair_sum",
        in_specs=[pl.BlockSpec(memory_space=pltpu.VMEM)],
        out_specs=[pl.BlockSpec(memory_space=pltpu.VMEM)] * 2,
        out_shape=[jax.ShapeDtypeStruct((V, C), F32), jax.ShapeDtypeStruct((R - V, C), BF16)],
        scratch_shapes=[pltpu.VMEM((R, C), F32), pltpu.SemaphoreType.DMA, pltpu.SemaphoreType.DMA],
    )(s)


def _small_total(chip, own, landed):
    V, C = own[0].shape
    M = own[1].shape[0]

    def body(j_ref, v_ref, m_ref, lv_ref, lm_ref, o_ref, chips_v, chips_m):
        j = j_ref[0]
        chips_v[j] = v_ref[...]
        chips_m[j] = m_ref[...]
        for r in (1, 2, 3):
            chips_v[j ^ r] = lv_ref[r - 1]
            chips_m[j ^ r] = lm_ref[r - 1]
        o_ref[pl.ds(0, V), :] = (chips_v[0] + chips_v[1]) + (chips_v[2] + chips_v[3])
        o_ref[pl.ds(V, M), :] = (chips_m[0].astype(F32) + chips_m[1].astype(F32)) + (
            chips_m[2].astype(F32) + chips_m[3].astype(F32))

    vmem = pl.BlockSpec(memory_space=pltpu.VMEM)
    return pl.pallas_call(
        body,
        name="small_total",
        in_specs=[pl.BlockSpec(memory_space=pltpu.SMEM), vmem, vmem, vmem, vmem],
        out_specs=vmem,
        out_shape=jax.ShapeDtypeStruct((V + M, C), F32),
        scratch_shapes=[pltpu.VMEM((N_CHIPS, V, C), F32), pltpu.VMEM((N_CHIPS, M, C), BF16)],
    )(chip, own[0], own[1], landed[0], landed[1])


def _local_grads(x, target, g_pre, w_in_g, b_gate, conv_w, conv_b, w_rg_a, b_rg_a, w_rg_x, b_rg_x, lam, sinks,
                 out_weights, fwd_token, g_post, on_out_grads, on_w_in_grad):
    b_a = b_rg_a.reshape(1, D_RNN)
    b_x = b_rg_x.reshape(1, D_RNN)

    proj, ht = _proj_fwd(x, g_pre, w_in_g)
    y_rnn, z_rnn, conv, z_rnn_t = _rnn_fwd(proj, conv_w, conv_b, w_rg_a, w_rg_x, b_a, b_x, lam, fwd_token)
    bias = _attn_bias()
    y_attn, z_attn, lse = _attn_fwd(proj, sinks, bias)
    w_rnn_out, w_attn_out, w_out = out_weights(z_attn)
    dyx, dz_rnn, dz_attn, dml, dout, dbr_rnn, dbr_attn, merged_t, z_attn_t, head_small = _head(
        x, target, z_rnn, z_attn, proj, b_gate, g_post, w_rnn_out, w_attn_out, w_out)
    out_grads = [_matmul_t(z_rnn_t, dbr_rnn, "dw_rnn_out"), _matmul_t(z_attn_t, dbr_attn, "dw_attn_out"),
                 _matmul_t(merged_t, dout, "dw_out")]
    shard_rows = lambda d: d.reshape(N_CHIPS, OUT_SHARD, D_MODEL)
    token = on_out_grads([shard_rows(g) for g, _ in out_grads], [shard_rows(gb) for _, gb in out_grads])
    dq, dk, dv, dag, attn_small = _attn_bwd(proj, y_attn, lse, dz_attn, sinks, bias, token)
    drx, drg, dwa, dwx, rnn_small = _rnn_bwd(proj, conv, y_rnn, dz_rnn, conv_w, w_rg_a, w_rg_x, b_a, b_x, lam)
    dproj = [drx, drg, dq, dk, dv, dag, dml]
    token = on_w_in_grad(*_dw_in(ht, dproj))
    grad_x, dh_small = _dh_bwd(dproj, w_in_g, x, dyx, g_pre, token)
    small = jnp.concatenate([rnn_small, head_small, dh_small + attn_small,
                             dwa.reshape(64, 1024), dwx.reshape(64, 1024)], axis=0)
    return grad_x, small


ROW_LOSS = 11


def _unpack_small(s, conv_cols):
    return {
        "b_rg_a": s[0:1].reshape(1, 16, 64), "b_rg_x": s[1:2].reshape(1, 16, 64), "lru_lambda": s[2:3],
        "conv_b": s[3:4], "conv_w": s[4:8, 0:conv_cols].reshape(1, CONV_W, conv_cols),
        "post_norm_g": s[8:9], "b_gate": s[9:11].reshape(1, 2048),
        "pre_norm_g": s[16:17], "attn_sinks": s[17:18, 0:N_Q_HEADS],
        "w_rg_a": s[24:88].reshape(1, 16, 64, 64), "w_rg_x": s[88:152].reshape(1, 16, 64, 64),
    }


WEIGHTS = ["pre_norm_g", "w_in", "b_gate", "conv_w", "conv_b", "w_rg_a", "b_rg_a", "w_rg_x", "b_rg_x", "lru_lambda",
           "attn_sinks", "w_rnn_out", "w_attn_out", "w_out", "post_norm_g"]
BIG = ["w_in", "w_rnn_out", "w_attn_out", "w_out"]


def kernel(x, pre_norm_g, w_in, b_gate, conv_w, conv_b, w_rg_a, b_rg_a, w_rg_x, b_rg_x, lru_lambda, attn_sinks, w_rnn_out, w_attn_out, w_out, post_norm_g, loss_target, m_pre_norm_g, m_w_in, m_b_gate, m_conv_w, m_conv_b, m_w_rg_a, m_b_rg_a, m_w_rg_x, m_b_rg_x, m_lru_lambda, m_attn_sinks, m_w_rnn_out, m_w_attn_out, m_w_out, m_post_norm_g, v_pre_norm_g, v_w_in, v_b_gate, v_conv_w, v_conv_b, v_w_rg_a, v_b_rg_a, v_w_rg_x, v_b_rg_x, v_lru_lambda, v_attn_sinks, v_w_rnn_out, v_w_attn_out, v_w_out, v_post_norm_g):
    w = dict(pre_norm_g=pre_norm_g, w_in=w_in, b_gate=b_gate, conv_w=conv_w, conv_b=conv_b, w_rg_a=w_rg_a,
             b_rg_a=b_rg_a, w_rg_x=w_rg_x, b_rg_x=b_rg_x, lru_lambda=lru_lambda, attn_sinks=attn_sinks,
             w_rnn_out=w_rnn_out, w_attn_out=w_attn_out, w_out=w_out, post_norm_g=post_norm_g)
    m = dict(pre_norm_g=m_pre_norm_g, w_in=m_w_in, b_gate=m_b_gate, conv_w=m_conv_w, conv_b=m_conv_b, w_rg_a=m_w_rg_a,
             b_rg_a=m_b_rg_a, w_rg_x=m_w_rg_x, b_rg_x=m_b_rg_x, lru_lambda=m_lru_lambda, attn_sinks=m_attn_sinks,
             w_rnn_out=m_w_rnn_out, w_attn_out=m_w_attn_out, w_out=m_w_out, post_norm_g=m_post_norm_g)
    v = dict(pre_norm_g=v_pre_norm_g, w_in=v_w_in, b_gate=v_b_gate, conv_w=v_conv_w, conv_b=v_conv_b, w_rg_a=v_w_rg_a,
             b_rg_a=v_b_rg_a, w_rg_x=v_w_rg_x, b_rg_x=v_b_rg_x, lru_lambda=v_lru_lambda, attn_sinks=v_attn_sinks,
             w_rnn_out=v_w_rnn_out, w_attn_out=v_w_attn_out, w_out=v_w_out, post_norm_g=v_post_norm_g)
    chip = 2 * lax.axis_index("x") + lax.axis_index("y")

    chip_idx = chip.astype(jnp.int32).reshape(1)
    chip_core = jnp.stack([chip, lax.axis_index("c")]).astype(jnp.int32)
    cw8 = jnp.pad(conv_w[0], ((0, 8 - CONV_W), (0, 0)))
    placed = _place_shards([w_in[0], w_rnn_out[0], w_attn_out[0], w_out[0]], chip_idx, "place_shards")
    win_g, cw_g = _gather_weights(placed[:1], cw8)
    late_send, late_recv, late_thru, late_token = _gather_late_start(placed[1:], win_g, "gather_late_start")
    cw_g = lax.dynamic_update_slice_in_dim(cw_g, cw8[None], chip, axis=0)
    conv_w_full = jnp.transpose(cw_g[:, 0:CONV_W, :], (1, 0, 2)).reshape(CONV_W, D_RNN)

    started = {}

    def start_reduction(tag, grads, grads_b16):
        psums, psums_b16 = _pair_sum(grads, grads_b16, chip_core, "pair_sum_" + tag)
        send_sems, recv_sems, p_thru, land_thru, token = _chip_exchange_start(psums_b16, "chip_exchange_start_" + tag)
        started[tag] = (psums, send_sems, recv_sems, p_thru, land_thru)
        return token

    def end_reduction(tag, after):
        psums, send_sems, recv_sems, p_thru, land_thru = started[tag]
        _, landed = _chip_exchange_wait(send_sems, recv_sems, p_thru, land_thru, after, "chip_exchange_wait_" + tag)
        return _chip_sum(psums, landed, chip_core, "chip_sum_" + tag)

    def out_weights(after):
        gathered = _gather_late_wait(late_send, late_recv, late_thru, after, "gather_late_wait")
        return [g.reshape(D_MODEL, D_MODEL) for g in gathered]

    grad_x, small = _local_grads(
        x[0], loss_target[0], pre_norm_g, win_g, b_gate, conv_w_full, conv_b, w_rg_a[0], b_rg_a[0], w_rg_x[0],
        b_rg_x[0], lru_lambda, attn_sinks[0], out_weights, late_token, post_norm_g,
        on_out_grads=lambda grads, grads_b16: start_reduction("out", grads, grads_b16),
        on_w_in_grad=lambda grad, grad_b16: start_reduction("in", [grad], [grad_b16]))

    small_chip = _small_pair_sum(small)
    small_send, small_recv, small_thru, small_land, small_token = _chip_exchange_start(
        list(small_chip), "small_exchange_start", blocked=False)

    halves = end_reduction("in", small_token) + end_reduction("out", small_token)
    gbig = dict(zip(BIG, _pair_share(halves)))

    grads, delta, new_m, new_v = {}, {}, {}, {}
    for names, tag in ((BIG[:1], "adamw_in"), (BIG[1:], "adamw_out")):
        updates = _adamw([(w[n][0], gbig[n], m[n][0], v[n][0]) for n in names], tag)
        for n, (d, nm, nv, g) in zip(names, updates):
            grads[n], delta[n], new_m[n], new_v[n] = g[None], d[None], nm[None], nv[None]

    small_own, small_landed = _chip_exchange_wait(small_send, small_recv, small_thru, small_land, delta[BIG[-1]],
                                                  "small_exchange_wait", blocked=False)
    small_sum = _small_total(chip_idx, small_own, small_landed)
    total_loss = small_sum[ROW_LOSS, 0]
    gsmall = _unpack_small(small_sum, D_RNN)
    conv_shard = D_RNN // N_CHIPS
    gsmall["conv_w"] = lax.dynamic_slice_in_dim(gsmall["conv_w"], chip * conv_shard, conv_shard, axis=2)
    for n in gsmall:
        grads[n] = gsmall[n].reshape(w[n].shape)
    updates = _adamw_whole([(w[n], grads[n], m[n], v[n]) for n in gsmall], "adamw_small")
    for n, (d, nm, nv) in zip(gsmall, updates):
        delta[n], new_m[n], new_v[n] = d, nm, nv

    return (total_loss, grad_x[None], *[grads[n] for n in WEIGHTS], *[delta[n] for n in WEIGHTS],
            *[new_m[n] for n in WEIGHTS], *[new_v[n] for n in WEIGHTS])
```

```python
import functools
import math

import jax
import jax.numpy as jnp
import numpy as np
from jax import lax
from jax.experimental import pallas as pl
from jax.experimental.pallas import tpu as pltpu

F32 = jnp.float32
BF16 = jnp.bfloat16

D_MODEL = 1024
D_RNN = 1024
RNN_BLOCKS = 16
RNN_BLOCK_W = 64
CONV_W = 4
LRU_C = 8.0
N_Q_HEADS = 16
N_KV_HEADS = 4
GROUP = 4
HEAD_DIM = 64
D_KV = 256
BLOCK = 128
ALIBI_MAX_BIAS = 8.0
EPS = 1e-6
D_IN = 6656
N_CHIPS = 4
W_IN_SHARD = D_IN // N_CHIPS
OUT_SHARD = D_MODEL // N_CHIPS
ADAM_LR = 0.001
ADAM_B1 = 0.9
ADAM_B2 = 0.999
ADAM_EPS = 1e-08
ADAM_WD = 0.01
ADAM_STEP = 10
NEG_BIG = -1e30
MIB = 1 << 20

COL_RNN_X = 0
COL_RNN_GATE = 4
COL_Q = 8
COL_K = 12
COL_V = 13
COL_ATTN_GATE = 14
COL_MERGE = 18

RNN_TILE = 256
RNN_CHUNK = 512
SMALL_ROWS = 152
SMALL_VECTOR_ROWS = 24
MESH = pl.DeviceIdType.MESH


def _sds(shape, dtype):
    return pltpu.HBM(shape, dtype)


def _params(sem=None, vmem_mib=None):
    kw = {}
    if sem is not None:
        kw["dimension_semantics"] = sem
    if vmem_mib is not None:
        kw["vmem_limit_bytes"] = vmem_mib * MIB
    return pltpu.CompilerParams(**kw)


def _hbm(*arrays):
    return [pltpu.with_memory_space_constraint(a, pltpu.HBM) for a in arrays]


def _dot(a, b):
    return jnp.dot(a, b, preferred_element_type=F32)


def _dot_nt(a, b):
    return lax.dot_general(a, b, (((1,), (1,)), ((), ())), preferred_element_type=F32)


def _dot_tn(a, b):
    return lax.dot_general(a, b, (((0,), (0,)), ((), ())), preferred_element_type=F32)


def _sigmoid(x):
    return 0.5 * jnp.tanh(0.5 * x) + 0.5


def _sigmoid_small(x):
    return 1.0 / (1.0 + jnp.exp(-x))


def _softplus(x):
    return jnp.maximum(x, 0.0) + jnp.log(1.0 + jnp.exp(-jnp.abs(x)))


def _one_minus_square(a, log_a):
    return -jnp.tanh(log_a) * (a * a + 1.0)


def _proj_fwd(x, g_pre, w_in_g):
    T = x.shape[0]
    tm = min(1024, T)

    def body(x_ref, g_ref, w_ref, proj_ref, ht_ref, h_s):
        @pl.when(pl.program_id(1) == 0)
        def _():
            xv = x_ref[...]
            rstd = lax.rsqrt(jnp.mean(xv * xv, axis=-1, keepdims=True) + EPS)
            hf = (xv * rstd) * g_ref[...]
            h_s[...] = hf.astype(BF16)
            ht_ref[...] = hf.T.astype(BF16)

        proj_ref[...] = _dot(h_s[...], w_ref[...]).astype(BF16)

    return pl.pallas_call(
        body,
        name="proj_fwd",
        grid=(T // tm, N_CHIPS),
        in_specs=[
            pl.BlockSpec((tm, D_MODEL), lambda i, j: (i, 0)),
            pl.BlockSpec((1, D_MODEL), lambda i, j: (0, 0)),
            pl.BlockSpec((None, D_MODEL, W_IN_SHARD), lambda i, j: (j, 0, 0)),
        ],
        out_specs=[
            pl.BlockSpec((tm, W_IN_SHARD), lambda i, j: (i, j)),
            pl.BlockSpec((D_MODEL, tm), lambda i, j: (0, i)),
        ],
        out_shape=[_sds((T, D_IN), BF16), _sds((D_MODEL, T), BF16)],
        scratch_shapes=[pltpu.VMEM((tm, D_MODEL), BF16)],
        compiler_params=_params(("parallel", "arbitrary"), 48),
    )(*_hbm(x, g_pre, w_in_g))


def _shift_down(x, tail, s, row):
    n = x.shape[0]
    xs = pltpu.roll(x, s, 0)
    tail_t = jnp.tile(pltpu.roll(tail, s, 0), (n // 8, 1))
    return jnp.where(row < s, tail_t, xs)


def _shift_up(x, head, s, row):
    n = x.shape[0]
    xs = pltpu.roll(x, n - s, 0)
    head_t = jnp.tile(pltpu.roll(head, 8 - s, 0), (n // 8, 1))
    return jnp.where(row >= n - s, head_t, xs)


def _conv_taps(x, tail, row):
    return [_shift_down(x, tail, 3, row), _shift_down(x, tail, 2, row), _shift_down(x, tail, 1, row), x]


def _rglru_gates(c, wa, wx, ba, bx, lam):
    cb = c.astype(BF16)
    r = _sigmoid_small(_dot(cb, wa) + ba)
    i = _sigmoid(_dot(cb, wx) + bx)
    log_a = (-LRU_C) * r * _softplus(-lam)
    a = jnp.exp(log_a)
    w = _one_minus_square(a, log_a)
    inv_mult = lax.rsqrt(w)
    return cb, r, i, a, w * inv_mult, inv_mult


GATE_BLOCKS_PER_TILE = RNN_TILE // RNN_BLOCK_W
GATE_BLOCKS = pl.BlockSpec((GATE_BLOCKS_PER_TILE, RNN_BLOCK_W, RNN_BLOCK_W), lambda j, t: (j, 0, 0))


def _fill_block_diag(bd_ref, w_ref):
    bd_ref[...] = jnp.zeros_like(bd_ref)
    for a in range(GATE_BLOCKS_PER_TILE):
        lo = a * RNN_BLOCK_W
        bd_ref[lo:lo + RNN_BLOCK_W, lo:lo + RNN_BLOCK_W] = w_ref[a].astype(BF16)


SUBLANES = 8


def _scan_down(a, u, row):
    n = a.shape[0]
    s = 1
    while s < SUBLANES:
        a_sh = jnp.where(row >= s, pltpu.roll(a, s, 0), 1.0)
        u_sh = jnp.where(row >= s, pltpu.roll(u, s, 0), 0.0)
        u = a * u_sh + u
        a = a * a_sh
        s *= 2
    while s < n:
        u = jnp.concatenate([u[:s], a[s:] * u[:n - s] + u[s:]], axis=0)
        a = jnp.concatenate([a[:s], a[s:] * a[:n - s]], axis=0)
        s *= 2
    return a, u


def _scan_up(b, u, row):
    n = b.shape[0]
    s = 1
    while s < SUBLANES:
        b_sh = jnp.where(row < n - s, pltpu.roll(b, n - s, 0), 1.0)
        u_sh = jnp.where(row < n - s, pltpu.roll(u, n - s, 0), 0.0)
        u = b * u_sh + u
        b = b * b_sh
        s *= 2
    while s < n:
        u = jnp.concatenate([b[:n - s] * u[s:] + u[:n - s], u[n - s:]], axis=0)
        b = jnp.concatenate([b[:n - s] * b[s:], b[n - s:]], axis=0)
        s *= 2
    return b, u


LANES = 128


def _chunk_scan(a, u, a_s, u_s, hl_s, al_s, carry, reverse):
    n, width = a.shape
    groups = n // SUBLANES
    order = range(SUBLANES - 1, -1, -1) if reverse else range(SUBLANES)
    row = lax.broadcasted_iota(jnp.int32, (groups, LANES), 0)
    for l in range(width // LANES):
        lanes = slice(l * LANES, (l + 1) * LANES)
        a_l, u_l, hl_l, al_l = a_s.at[l], u_s.at[l], hl_s.at[l], al_s.at[l]
        a_l[...] = a[:, lanes]
        u_l[...] = u[:, lanes]
        h_loc = a_loc = None
        for r in order:
            rows = pl.ds(r, groups, stride=SUBLANES)
            a_r, u_r = a_l[rows, :], u_l[rows, :]
            h_loc, a_loc = (u_r, a_r) if h_loc is None else (a_r * h_loc + u_r, a_r * a_loc)
            hl_l[rows, :] = h_loc
            al_l[rows, :] = a_loc
        if reverse:
            a_cum, ends = _scan_up(a_loc, h_loc, row)
            ends = ends + a_cum * carry[:, lanes]
            enters = jnp.where(row == groups - 1, carry[:, lanes], pltpu.roll(ends, groups - 1, 0))
        else:
            a_cum, ends = _scan_down(a_loc, h_loc, row)
            ends = ends + a_cum * carry[:, lanes]
            enters = jnp.where(row == 0, carry[:, lanes], pltpu.roll(ends, 1, 0))
        for r in range(SUBLANES):
            rows = pl.ds(r, groups, stride=SUBLANES)
            hl_l[rows, :] = hl_l[rows, :] + al_l[rows, :] * enters
    return jnp.concatenate([hl_s[l] for l in range(width // LANES)], axis=1)


def _rnn_fwd(proj, conv_w, conv_b, w_a, w_x, b_a, b_x, lam, token):
    T = proj.shape[0]
    tc, ct = RNN_CHUNK, RNN_TILE
    nt = T // tc

    def body(x_ref, rg_ref, cw_ref, cb_ref, wa_ref, wx_ref, ba_ref, bx_ref, lam_ref, token_ref, h_ref, z_ref, c_ref,
             zt_ref, xtail, hcarry, wa_s, wx_s, a_s, u_s, hl_s, al_s):
        @pl.when(pl.program_id(1) == 0)
        def _():
            xtail[...] = jnp.zeros_like(xtail)
            hcarry[...] = jnp.zeros_like(hcarry)
            _fill_block_diag(wa_s, wa_ref)
            _fill_block_diag(wx_s, wx_ref)

        row = lax.broadcasted_iota(jnp.int32, (tc, ct), 0)
        x = x_ref[...].astype(F32)
        taps = _conv_taps(x, xtail[...], row)
        c = cb_ref[...] + cw_ref[pl.ds(0, 1), :] * taps[0]
        for k in range(1, CONV_W):
            c = c + cw_ref[pl.ds(k, 1), :] * taps[k]
        xtail[...] = x[tc - 8:, :]
        c_ref[...] = c
        _, _, i, a, mult, _ = _rglru_gates(c, wa_s[...], wx_s[...], ba_ref[...], bx_ref[...], lam_ref[...])
        h = _chunk_scan(a, mult * (i * c), a_s, u_s, hl_s, al_s, hcarry[...], reverse=False)
        h_ref[...] = h
        hcarry[...] = h_ref[pl.ds(tc - 1, 1), :]
        rg = rg_ref[...].astype(F32)
        z = h * (rg * _sigmoid(rg))
        z_ref[...] = z.astype(BF16)
        zt_ref[...] = z.T.astype(BF16)

    col = lambda off: (lambda j, t: (t, off + j))
    vec = pl.BlockSpec((1, ct), lambda j, t: (0, j))
    return pl.pallas_call(
        body,
        name="rnn_fwd",
        grid=(D_RNN // ct, nt),
        in_specs=[
            pl.BlockSpec((tc, ct), col(COL_RNN_X)),
            pl.BlockSpec((tc, ct), col(COL_RNN_GATE)),
            pl.BlockSpec((CONV_W, ct), lambda j, t: (0, j)),
            vec, GATE_BLOCKS, GATE_BLOCKS, vec, vec, vec,
            pl.BlockSpec((8, 128), lambda j, t: (0, 0)),
        ],
        out_specs=[pl.BlockSpec((tc, ct), lambda j, t: (t, j))] * 3 + [pl.BlockSpec((ct, tc), lambda j, t: (j, t))],
        out_shape=[_sds((T, D_RNN), F32), _sds((T, D_RNN), BF16), _sds((T, D_RNN), F32), _sds((D_RNN, T), BF16)],
        scratch_shapes=[pltpu.VMEM((8, ct), F32), pltpu.VMEM((1, ct), F32)] + [pltpu.VMEM((ct, ct), BF16)] * 2 + [
            pltpu.VMEM((ct // LANES, tc, LANES), F32)] * 4,
        compiler_params=_params(("parallel", "arbitrary"), 32),
    )(*_hbm(proj, proj, conv_w, conv_b, w_a, w_x, b_a, b_x, lam, token))


def _rnn_bwd(proj, conv, y_rnn, dz_rnn, conv_w, w_a, w_x, b_a, b_x, lam):
    T = proj.shape[0]
    tc, ct = RNN_CHUNK, RNN_TILE
    nt = T // tc
    hb = tc // 8

    def body(x_ref, c_ref, rg_ref, h_ref, hh_ref, dz_ref, cw_ref, wa_ref, wx_ref, ba_ref, bx_ref, lam_ref,
             dx_ref, drg_ref, dwa_ref, dwx_ref, sm_ref, lam_carry, a_carry, dc_head, wa_s, wx_s, dwa_s, dwx_s,
             b_s, dy_s, hl_s, al_s):
        t = pl.program_id(1)
        first_chunk = t == nt - 1

        @pl.when(t == 0)
        def _():
            lam_carry[...] = jnp.zeros_like(lam_carry)
            a_carry[...] = jnp.zeros_like(a_carry)
            dc_head[...] = jnp.zeros_like(dc_head)
            dwa_s[...] = jnp.zeros_like(dwa_s)
            dwx_s[...] = jnp.zeros_like(dwx_s)
            sm_ref[...] = jnp.zeros_like(sm_ref)
            _fill_block_diag(wa_s, wa_ref)
            _fill_block_diag(wx_s, wx_ref)

        row = lax.broadcasted_iota(jnp.int32, (tc, ct), 0)
        keep = jnp.where(first_chunk, 0.0, 1.0)
        x = x_ref[...].astype(F32)
        c = c_ref[...]
        lam = lam_ref[...]
        cb, r, i, a, mult, inv_mult = _rglru_gates(c, wa_s[...], wx_s[...], ba_ref[...], bx_ref[...], lam)
        h = h_ref[...]
        h_prev = _shift_down(h, hh_ref[...] * keep, 1, row)
        rg = rg_ref[...].astype(F32)
        dz = dz_ref[...]
        sg = _sigmoid(rg)
        drg_ref[...] = (dz * h * (sg * (1.0 + rg * (1.0 - sg)))).astype(BF16)
        dy = dz * (rg * sg)
        b = jnp.where(row >= tc - 1, a_carry[pl.ds(0, 1), :], pltpu.roll(a, tc - 1, 0))
        lt = _chunk_scan(b, dy, b_s, dy_s, hl_s, al_s, lam_carry[pl.ds(0, 1), :], reverse=True)
        lam_carry[...] = lt[0:8, :]
        a_carry[...] = a[0:8, :]
        ic = i * c
        dmult = lt * ic
        di = lt * mult * c
        dc = lt * mult * i
        dlog_a = a * (lt * h_prev - dmult * a * inv_mult)
        sp = _softplus(-lam)
        dpre_r = dlog_a * ((-LRU_C) * sp) * (r * (1.0 - r))
        dpre_i = di * (i * (1.0 - i))
        dlam_row = jnp.sum(dlog_a * r, axis=0, keepdims=True) * (LRU_C * _sigmoid(-lam))
        dpr_b = dpre_r.astype(BF16)
        dpi_b = dpre_i.astype(BF16)
        dwa_s[...] += _dot_tn(cb, dpr_b)
        dwx_s[...] += _dot_tn(cb, dpi_b)
        dc = dc + _dot_nt(dpr_b, wa_s[...]) + _dot_nt(dpi_b, wx_s[...])
        head = dc_head[...]
        dx = cw_ref[pl.ds(3, 1), :] * dc
        sm_ref[pl.ds(4 + 3, 1), :] += jnp.sum(dc * x, axis=0, keepdims=True)
        for m in range(1, CONV_W):
            up = _shift_up(dc, head, m, row)
            dx = dx + cw_ref[pl.ds(3 - m, 1), :] * up
            sm_ref[pl.ds(4 + 3 - m, 1), :] += jnp.sum(up * x, axis=0, keepdims=True)
        dx_ref[...] = dx.astype(BF16)
        dc_head[...] = dc[0:8, :]
        sm_ref[pl.ds(0, 1), :] += jnp.sum(dpre_r, axis=0, keepdims=True)
        sm_ref[pl.ds(1, 1), :] += jnp.sum(dpre_i, axis=0, keepdims=True)
        sm_ref[pl.ds(2, 1), :] += dlam_row
        sm_ref[pl.ds(3, 1), :] += jnp.sum(dc, axis=0, keepdims=True)

        @pl.when(first_chunk)
        def _():
            for k in range(GATE_BLOCKS_PER_TILE):
                lo = k * RNN_BLOCK_W
                dwa_ref[k] = dwa_s[lo:lo + RNN_BLOCK_W, lo:lo + RNN_BLOCK_W]
                dwx_ref[k] = dwx_s[lo:lo + RNN_BLOCK_W, lo:lo + RNN_BLOCK_W]

    rev = lambda off: (lambda j, t: (nt - 1 - t, off + j))
    halo = lambda off: (lambda j, t: (jnp.maximum((nt - 1 - t) * hb - 1, 0), off + j))
    vec = pl.BlockSpec((1, ct), lambda j, t: (0, j))
    mat = GATE_BLOCKS
    return pl.pallas_call(
        body,
        name="rnn_bwd",
        grid=(D_RNN // ct, nt),
        in_specs=[
            pl.BlockSpec((tc, ct), rev(COL_RNN_X)),
            pl.BlockSpec((tc, ct), rev(0)),
            pl.BlockSpec((tc, ct), rev(COL_RNN_GATE)),
            pl.BlockSpec((tc, ct), rev(0)),
            pl.BlockSpec((8, ct), halo(0)),
            pl.BlockSpec((tc, ct), rev(0)),
            pl.BlockSpec((CONV_W, ct), lambda j, t: (0, j)),
            mat, mat, vec, vec, vec,
        ],
        out_specs=[
            pl.BlockSpec((tc, ct), rev(0)),
            pl.BlockSpec((tc, ct), rev(0)),
            mat, mat,
            pl.BlockSpec((8, ct), lambda j, t: (0, j)),
        ],
        out_shape=[_sds((T, D_RNN), BF16), _sds((T, D_RNN), BF16), _sds(w_a.shape, F32), _sds(w_x.shape, F32),
                   _sds((8, D_RNN), F32)],
        scratch_shapes=[pltpu.VMEM((8, ct), F32)] * 3 + [pltpu.VMEM((ct, ct), BF16)] * 2 + [
            pltpu.VMEM((ct, ct), F32)] * 2 + [pltpu.VMEM((ct // LANES, tc, LANES), F32)] * 4,
        compiler_params=_params(("parallel", "arbitrary"), 32),
    )(*_hbm(proj, conv, proj, y_rnn, y_rnn, dz_rnn, conv_w, w_a, w_x, b_a, b_x, lam))


def _attn_bias():
    qi = np.arange(BLOCK)[:, None]
    kj = np.arange(BLOCK)[None, :]
    dist_cur = (qi - kj).astype(np.float32)
    slopes = np.float32(2.0) ** (-ALIBI_MAX_BIAS * np.arange(1, N_Q_HEADS + 1, dtype=np.float32) / N_Q_HEADS)
    slopes = slopes[:, None, None]
    prev = np.where(kj > qi, -slopes * (dist_cur + np.float32(BLOCK)), np.float32(NEG_BIG))
    cur = np.where(kj <= qi, -slopes * dist_cur, np.float32(NEG_BIG))
    later = np.concatenate([prev, cur], axis=-1)
    first = np.concatenate([np.full_like(prev, NEG_BIG), cur], axis=-1)
    return jnp.asarray(np.stack([first, later]).astype(np.float32))


def _attn_exps(s_prev, s_cur, sink, bias):
    s_prev = s_prev + bias[:, 0:BLOCK]
    s_cur = s_cur + bias[:, BLOCK:2 * BLOCK]
    m = jnp.maximum(jnp.max(jnp.maximum(s_prev, s_cur), axis=-1, keepdims=True), sink)
    p_prev = jnp.exp(s_prev - m)
    p_cur = jnp.exp(s_cur - m)
    total = jnp.sum(p_prev + p_cur, axis=-1, keepdims=True) + jnp.exp(sink - m)
    return p_prev, p_cur, 1.0 / total, m + jnp.log(total)


def _attn_probs(s_prev, s_cur, sink, bias, lse):
    p_prev = jnp.exp((s_prev + bias[:, 0:BLOCK]) - lse)
    p_cur = jnp.exp((s_cur + bias[:, BLOCK:2 * BLOCK]) - lse)
    return p_prev, p_cur, jnp.exp(sink - lse)


def _stack_heads(ref_or_val, hk, dtype):
    parts = [ref_or_val[:, (GROUP * hk + g) * HEAD_DIM:(GROUP * hk + g + 1) * HEAD_DIM] for g in range(GROUP)]
    return jnp.concatenate(parts, axis=0).astype(dtype)


ATTN_SCALE = HEAD_DIM ** -0.5


def _bias_spec():
    return pl.BlockSpec((None, N_Q_HEADS, BLOCK, 2 * BLOCK), lambda i: (jnp.minimum(i, 1), 0, 0, 0))


def _attn_fwd(proj, sinks, bias):
    T = proj.shape[0]
    nb = T // BLOCK

    def body(sink_ref, bias_ref, q_ref, kp_ref, kc_ref, vp_ref, vc_ref, ag0_ref, ag1_ref, y_ref, z_ref, lse_ref):
        kvs = [slice(hk * HEAD_DIM, (hk + 1) * HEAD_DIM) for hk in range(N_KV_HEADS)]
        qgs = [(_stack_heads(q_ref, hk, F32) * ATTN_SCALE).astype(BF16) for hk in range(N_KV_HEADS)]
        s_prev = [_dot_nt(qgs[hk], kp_ref[:, kvs[hk]].astype(BF16)) for hk in range(N_KV_HEADS)]
        s_cur = [_dot_nt(qgs[hk], kc_ref[:, kvs[hk]].astype(BF16)) for hk in range(N_KV_HEADS)]
        for hk in range(N_KV_HEADS):
            pp, pc, invs = [], [], []
            for g in range(GROUP):
                h = GROUP * hk + g
                rows = slice(g * BLOCK, (g + 1) * BLOCK)
                p_prev, p_cur, inv, lse = _attn_exps(s_prev[hk][rows], s_cur[hk][rows], sink_ref[h], bias_ref[h])
                pp.append(p_prev.astype(BF16))
                pc.append(p_cur.astype(BF16))
                invs.append(inv)
                lse_ref[:, h:h + 1] = lse
            og = _dot(jnp.concatenate(pp, axis=0), vp_ref[:, kvs[hk]].astype(BF16)) + _dot(
                jnp.concatenate(pc, axis=0), vc_ref[:, kvs[hk]].astype(BF16))
            for g in range(GROUP):
                h = GROUP * hk + g
                y_ref[:, h * HEAD_DIM:(h + 1) * HEAD_DIM] = og[g * BLOCK:(g + 1) * BLOCK] * invs[g]
        ag = jnp.concatenate([ag0_ref[...], ag1_ref[...]], axis=1).astype(F32)
        z_ref[...] = (y_ref[...] * (ag * _sigmoid(ag))).astype(BF16)

    prev = lambda c: (lambda i: (jnp.maximum(i - 1, 0), c))
    cur = lambda c: (lambda i: (i, c))
    return pl.pallas_call(
        body,
        name="attn_fwd",
        grid=(nb,),
        in_specs=[
            pl.BlockSpec(memory_space=pltpu.SMEM),
            _bias_spec(),
            pl.BlockSpec((BLOCK, 1024), lambda i: (i, COL_Q // 4)),
            pl.BlockSpec((BLOCK, D_KV), prev(COL_K)),
            pl.BlockSpec((BLOCK, D_KV), cur(COL_K)),
            pl.BlockSpec((BLOCK, D_KV), prev(COL_V)),
            pl.BlockSpec((BLOCK, D_KV), cur(COL_V)),
            pl.BlockSpec((BLOCK, 512), lambda i: (i, COL_ATTN_GATE // 2)),
            pl.BlockSpec((BLOCK, 512), lambda i: (i, COL_ATTN_GATE // 2 + 1)),
        ],
        out_specs=[pl.BlockSpec((BLOCK, 1024), lambda i: (i, 0)), pl.BlockSpec((BLOCK, 1024), lambda i: (i, 0)),
                   pl.BlockSpec((BLOCK, N_Q_HEADS), lambda i: (i, 0))],
        out_shape=[_sds((T, 1024), F32), _sds((T, 1024), BF16), _sds((T, N_Q_HEADS), F32)],
        compiler_params=_params(("arbitrary",), 32),
    )(sinks, *_hbm(bias, proj, proj, proj, proj, proj, proj, proj))


def _attn_bwd(proj, y_attn, lse, dz_attn, sinks, bias, token):
    T = proj.shape[0]
    nb = T // BLOCK

    def body(sink_ref, bias_ref, q_ref, kp_ref, kc_ref, vp_ref, vc_ref, ag0_ref, ag1_ref, y_ref, lse_ref, dz_ref,
             token_ref, dq_ref, dk_ref, dv_ref, dag_ref, ds_ref, dy_s):
        i = pl.program_id(0)

        @pl.when(i == 0)
        def _():
            ds_ref[...] = jnp.zeros_like(ds_ref)

        lane = lax.broadcasted_iota(jnp.int32, (8, 128), 1)
        sub = lax.broadcasted_iota(jnp.int32, (8, 128), 0)
        ag = jnp.concatenate([ag0_ref[...], ag1_ref[...]], axis=1).astype(F32)
        dz = dz_ref[...]
        sg = _sigmoid(ag)
        dag_ref[...] = (dz * y_ref[...] * (sg * (1.0 + ag * (1.0 - sg)))).astype(BF16)
        dy_s[...] = dz * (ag * sg)
        r_cur = pl.multiple_of(i * BLOCK, BLOCK)
        r_prev = pl.multiple_of(jnp.maximum(i - 1, 0) * BLOCK, BLOCK)
        dk_cur, dv_cur, dk_prev, dv_prev = [], [], [], []
        ds_acc = jnp.zeros((8, 128), F32)
        for hk in range(N_KV_HEADS):
            ks = slice(hk * HEAD_DIM, (hk + 1) * HEAD_DIM)
            qg = (_stack_heads(q_ref, hk, F32) * ATTN_SCALE).astype(BF16)
            dog = _stack_heads(dy_s, hk, F32)
            og = _stack_heads(y_ref, hk, F32)
            dog_b = dog.astype(BF16)
            kp = kp_ref[:, ks].astype(BF16)
            kc = kc_ref[:, ks].astype(BF16)
            vp = vp_ref[:, ks].astype(BF16)
            vc = vc_ref[:, ks].astype(BF16)
            s_prev = _dot_nt(qg, kp)
            s_cur = _dot_nt(qg, kc)
            dp_prev = _dot_nt(dog_b, vp)
            dp_cur = _dot_nt(dog_b, vc)
            dvec = jnp.sum(dog * og, axis=-1, keepdims=True)
            pp, pc, dsp, dsc = [], [], [], []
            for g in range(GROUP):
                h = GROUP * hk + g
                rows = slice(g * BLOCK, (g + 1) * BLOCK)
                p_prev, p_cur, p_sink = _attn_probs(
                    s_prev[rows], s_cur[rows], sink_ref[h], bias_ref[h], lse_ref[:, h:h + 1])
                d_h = dvec[rows]
                pp.append(p_prev.astype(BF16))
                pc.append(p_cur.astype(BF16))
                dsp.append((p_prev * (dp_prev[rows] - d_h)).astype(BF16))
                dsc.append((p_cur * (dp_cur[rows] - d_h)).astype(BF16))
                dsink = -jnp.sum(p_sink * d_h, axis=0, keepdims=True)
                ds_acc = ds_acc + jnp.where(jnp.logical_and(lane == h, sub == 1), dsink, 0.0)
            pp = jnp.concatenate(pp, axis=0)
            pc = jnp.concatenate(pc, axis=0)
            dsp = jnp.concatenate(dsp, axis=0)
            dsc = jnp.concatenate(dsc, axis=0)
            dqg = (_dot(dsp, kp) + _dot(dsc, kc)) * ATTN_SCALE
            for g in range(GROUP):
                h = GROUP * hk + g
                dq_ref[:, h * HEAD_DIM:(h + 1) * HEAD_DIM] = dqg[g * BLOCK:(g + 1) * BLOCK].astype(BF16)
            dk_ref[pl.ds(r_cur, BLOCK), ks] = _dot_tn(dsc, qg)
            dv_ref[pl.ds(r_cur, BLOCK), ks] = _dot_tn(pc, dog_b)
            dk_prev.append(_dot_tn(dsp, qg))
            dv_prev.append(_dot_tn(pp, dog_b))
        ds_ref[:, 0:128] += ds_acc

        @pl.when(i > 0)
        def _():
            for hk in range(N_KV_HEADS):
                ks = slice(hk * HEAD_DIM, (hk + 1) * HEAD_DIM)
                dk_ref[pl.ds(r_prev, BLOCK), ks] += dk_prev[hk]
                dv_ref[pl.ds(r_prev, BLOCK), ks] += dv_prev[hk]

    prev = lambda c: (lambda i: (jnp.maximum(i - 1, 0), c))
    cur = lambda c: (lambda i: (i, c))
    blk = pl.BlockSpec((BLOCK, 1024), lambda i: (i, 0))
    whole = pl.BlockSpec((T, D_KV), lambda i: (0, 0))
    return pl.pallas_call(
        body,
        name="attn_bwd",
        grid=(nb,),
        in_specs=[
            pl.BlockSpec(memory_space=pltpu.SMEM),
            _bias_spec(),
            pl.BlockSpec((BLOCK, 1024), lambda i: (i, COL_Q // 4)),
            pl.BlockSpec((BLOCK, D_KV), prev(COL_K)),
            pl.BlockSpec((BLOCK, D_KV), cur(COL_K)),
            pl.BlockSpec((BLOCK, D_KV), prev(COL_V)),
            pl.BlockSpec((BLOCK, D_KV), cur(COL_V)),
            pl.BlockSpec((BLOCK, 512), lambda i: (i, COL_ATTN_GATE // 2)),
            pl.BlockSpec((BLOCK, 512), lambda i: (i, COL_ATTN_GATE // 2 + 1)),
            blk,
            pl.BlockSpec((BLOCK, N_Q_HEADS), lambda i: (i, 0)),
            blk,
            pl.BlockSpec((8, 128), lambda i: (0, 0)),
        ],
        out_specs=[blk, whole, whole, blk, pl.BlockSpec((8, 1024), lambda i: (0, 0))],
        out_shape=[_sds((T, 1024), BF16), _sds((T, D_KV), F32), _sds((T, D_KV), F32), _sds((T, 1024), BF16),
                   _sds((8, 1024), F32)],
        scratch_shapes=[pltpu.VMEM((BLOCK, 1024), F32)],
        compiler_params=_params(("arbitrary",), 48),
    )(sinks, *_hbm(bias, proj, proj, proj, proj, proj, proj, proj, y_attn, lse, dz_attn, token))


def _head(x, target, z_rnn, z_attn, proj, b_gate, g_post, w_rnn_out, w_attn_out, w_out):
    T = x.shape[0]
    tm = 256

    def body(x_ref, t_ref, zr_ref, za_ref, ml0_ref, ml1_ref, ml2_ref, ml3_ref, bg_ref, gp_ref, wr_ref, wa_ref, wo_ref,
             dyx_ref, dzr_ref, dza_ref, dml_ref, dout_ref, dbr_ref, dba_ref, mt_ref, zat_ref, sm_ref):
        @pl.when(pl.program_id(0) == 0)
        def _():
            sm_ref[...] = jnp.zeros_like(sm_ref)

        wr, wa, wo = wr_ref[...], wa_ref[...], wo_ref[...]
        br_rnn = _dot(zr_ref[...], wr)
        br_attn = _dot(za_ref[...], wa)
        zat_ref[...] = za_ref[...].astype(F32).T.astype(BF16)
        ml_rnn = jnp.concatenate([ml0_ref[...], ml1_ref[...]], axis=1).astype(F32)
        ml_attn = jnp.concatenate([ml2_ref[...], ml3_ref[...]], axis=1).astype(F32)
        g_rnn = _sigmoid(ml_rnn + bg_ref[:, 0:D_MODEL])
        g_attn = _sigmoid(ml_attn + bg_ref[:, D_MODEL:2 * D_MODEL])
        merged = g_rnn * br_rnn + g_attn * br_attn
        mb = merged.astype(BF16)
        mt_ref[...] = merged.T.astype(BF16)
        out = _dot(mb, wo)
        rstd = lax.rsqrt(jnp.mean(out * out, axis=-1, keepdims=True) + EPS)
        n = out * rstd
        gp = gp_ref[...]
        err = (x_ref[...] + n * gp) - t_ref[...]
        sm_ref[pl.ds(3, 1), :] += 0.5 * jnp.sum(jnp.mean(err * err, axis=-1, keepdims=True), axis=0, keepdims=True)
        dy = err * (1.0 / D_MODEL)
        dyx_ref[...] = dy
        sm_ref[pl.ds(0, 1), :] += jnp.sum(dy * n, axis=0, keepdims=True)
        dn = dy * gp
        dout = (rstd * (dn - n * jnp.mean(dn * n, axis=-1, keepdims=True))).astype(BF16)
        dout_ref[...] = dout
        dmerged = _dot_nt(dout, wo)
        dml_r = (dmerged * br_rnn) * (g_rnn * (1.0 - g_rnn))
        dml_a = (dmerged * br_attn) * (g_attn * (1.0 - g_attn))
        dml_ref[:, 0:D_MODEL] = dml_r.astype(BF16)
        dml_ref[:, D_MODEL:2 * D_MODEL] = dml_a.astype(BF16)
        sm_ref[pl.ds(1, 1), :] += jnp.sum(dml_r, axis=0, keepdims=True)
        sm_ref[pl.ds(2, 1), :] += jnp.sum(dml_a, axis=0, keepdims=True)
        dbr = (dmerged * g_rnn).astype(BF16)
        dba = (dmerged * g_attn).astype(BF16)
        dbr_ref[...] = dbr
        dba_ref[...] = dba
        dzr_ref[...] = _dot_nt(dbr, wr)
        dza_ref[...] = _dot_nt(dba, wa)

    tile = pl.BlockSpec((tm, D_MODEL), lambda i: (i, 0))
    wspec = pl.BlockSpec((D_MODEL, D_MODEL), lambda i: (0, 0))
    ml = lambda q: pl.BlockSpec((tm, 512), lambda i: (i, COL_MERGE // 2 + q))
    return pl.pallas_call(
        body,
        name="head",
        grid=(T // tm,),
        in_specs=[
            tile, tile, tile, tile,
            ml(0), ml(1), ml(2), ml(3),
            pl.BlockSpec((1, 2 * D_MODEL), lambda i: (0, 0)),
            pl.BlockSpec((1, D_MODEL), lambda i: (0, 0)),
            wspec, wspec, wspec,
        ],
        out_specs=[
            tile, tile, tile,
            pl.BlockSpec((tm, 2 * D_MODEL), lambda i: (i, 0)),
            tile, tile, tile,
            pl.BlockSpec((D_MODEL, tm), lambda i: (0, i)), pl.BlockSpec((D_MODEL, tm), lambda i: (0, i)),
            pl.BlockSpec((8, D_MODEL), lambda i: (0, 0)),
        ],
        out_shape=[
            _sds((T, D_MODEL), F32), _sds((T, D_MODEL), F32), _sds((T, D_MODEL), F32),
            _sds((T, 2 * D_MODEL), BF16),
            _sds((T, D_MODEL), BF16), _sds((T, D_MODEL), BF16), _sds((T, D_MODEL), BF16),
            _sds((D_MODEL, T), BF16), _sds((D_MODEL, T), BF16),
            _sds((8, D_MODEL), F32),
        ],
        compiler_params=_params(("arbitrary",), 56),
    )(*_hbm(x, target, z_rnn, z_attn, proj, proj, proj, proj, b_gate, g_post, w_rnn_out, w_attn_out, w_out))


def _matmul_t(at, b, name):
    M, T = at.shape
    N = b.shape[1]
    tk = min(1024, T)
    nt = T // tk

    def body(a_ref, b_ref, o_ref, ob_ref):
        @pl.when(pl.program_id(0) == 0)
        def _():
            o_ref[...] = jnp.zeros_like(o_ref)

        o_ref[...] += _dot(a_ref[...], b_ref[...])

        @pl.when(pl.program_id(0) == nt - 1)
        def _():
            ob_ref[...] = o_ref[...].astype(BF16)

    whole = pl.BlockSpec((M, N), lambda t: (0, 0))
    return pl.pallas_call(
        body,
        name=name,
        grid=(nt,),
        in_specs=[pl.BlockSpec((M, tk), lambda t: (0, t)), pl.BlockSpec((tk, N), lambda t: (t, 0))],
        out_specs=[whole, whole],
        out_shape=[_sds((M, N), F32), _sds((M, N), BF16)],
        compiler_params=_params(("arbitrary",), 48),
    )(*_hbm(at, b))


DPROJ_WIDTHS = (D_RNN, D_RNN, 1024, D_KV, D_KV, 1024, 2 * D_MODEL)


def _dproj_segments():
    segs, start = [[] for _ in range(N_CHIPS)], 0
    for p, width in enumerate(DPROJ_WIDTHS):
        for c in range(N_CHIPS):
            lo, hi = max(start, c * W_IN_SHARD), min(start + width, (c + 1) * W_IN_SHARD)
            if lo < hi:
                segs[c].append((p, lo - start, hi - start, lo - c * W_IN_SHARD, hi - c * W_IN_SHARD))
        start += width
    return segs


def _dh_bwd(pieces, w_in_g, x, dyx, g_pre, token):
    T = x.shape[0]
    tm = min(512, T)
    n = len(pieces)
    segs = _dproj_segments()

    def body(*refs):
        p_refs, w_hbm, x_ref, dyx_ref, g_ref = refs[0:n], refs[n], refs[n + 1], refs[n + 2], refs[n + 3]
        gx_ref, dg_ref, w_ref, w_sems = refs[n + 5], refs[n + 6], refs[n + 7], refs[n + 8]
        first = pl.program_id(0) == 0
        w_copies = [pltpu.make_async_copy(w_hbm.at[c], w_ref.at[c], w_sems.at[c]) for c in range(N_CHIPS)]

        @pl.when(first)
        def _():
            for cp in w_copies:
                cp.start()
            dg_ref[...] = jnp.zeros_like(dg_ref)

        dh = None
        for c in range(N_CHIPS):
            pl.when(first)(w_copies[c].wait)
            for p, a0, a1, u0, u1 in segs[c]:
                part = _dot_nt(p_refs[p][:, a0:a1].astype(BF16), w_ref[c, :, u0:u1])
                dh = part if dh is None else dh + part
        xv = x_ref[...]
        rstd = lax.rsqrt(jnp.mean(xv * xv, axis=-1, keepdims=True) + EPS)
        nx = xv * rstd
        dhg = dh * g_ref[...]
        gx_ref[...] = dyx_ref[...] + rstd * (dhg - nx * jnp.mean(dhg * nx, axis=-1, keepdims=True))
        dg_ref[pl.ds(0, 1), :] += jnp.sum(dh * nx, axis=0, keepdims=True)

    tile = pl.BlockSpec((tm, D_MODEL), lambda i: (i, 0))
    return pl.pallas_call(
        body,
        name="dh_bwd",
        grid=(T // tm,),
        in_specs=[pl.BlockSpec((tm, w), lambda i: (i, 0)) for w in DPROJ_WIDTHS] + [
            ANY, tile, tile,
            pl.BlockSpec((1, D_MODEL), lambda i: (0, 0)),
            pl.BlockSpec((8, 128), lambda i: (0, 0)),
        ],
        out_specs=[tile, pl.BlockSpec((8, D_MODEL), lambda i: (0, 0))],
        out_shape=[_sds((T, D_MODEL), F32), _sds((8, D_MODEL), F32)],
        scratch_shapes=[pltpu.VMEM(w_in_g.shape, BF16), pltpu.SemaphoreType.DMA((N_CHIPS,))],
        compiler_params=_params(("arbitrary",), 56),
    )(*_hbm(*pieces, w_in_g, x, dyx, g_pre, token))


def _dw_in(ht, pieces):
    T = ht.shape[1]
    tk = min(512, T)
    nt = T // tk
    n = len(pieces)
    segs = _dproj_segments()

    def body(*refs):
        h_ref, p_refs, o_ref, ob_ref = refs[0], refs[1:n + 1], refs[n + 1], refs[n + 2]

        @pl.when(pl.program_id(1) == 0)
        def _():
            o_ref[...] = jnp.zeros_like(o_ref)

        for c in range(N_CHIPS):
            @pl.when(pl.program_id(0) == c)
            def _():
                for p, a0, a1, u0, u1 in segs[c]:
                    o_ref[:, u0:u1] += _dot(h_ref[...], p_refs[p][:, a0:a1].astype(BF16))

        @pl.when(pl.program_id(1) == nt - 1)
        def _():
            ob_ref[...] = o_ref[...].astype(BF16)

    def piece_spec(p):
        chips = [c for c in range(N_CHIPS) if any(s[0] == p for s in segs[c])]

        def index(c, t):
            used = functools.reduce(jnp.logical_or, [c == k for k in chips])
            return (jnp.where(used, t, 0), 0)

        return pl.BlockSpec((tk, DPROJ_WIDTHS[p]), index)

    return pl.pallas_call(
        body,
        name="dw_in",
        grid=(N_CHIPS, nt),
        in_specs=[pl.BlockSpec((D_MODEL, tk), lambda c, t: (0, t))] + [piece_spec(p) for p in range(n)],
        out_specs=[pl.BlockSpec((None, D_MODEL, W_IN_SHARD), lambda c, t: (c, 0, 0))] * 2,
        out_shape=[_sds((N_CHIPS, D_MODEL, W_IN_SHARD), F32), _sds((N_CHIPS, D_MODEL, W_IN_SHARD), BF16)],
        compiler_params=_params(("parallel", "arbitrary"), 56),
    )(*_hbm(ht, *pieces))


ELEMENTWISE_TILE_BYTES = MIB


def _row_tile(rows, cols, limit=ELEMENTWISE_TILE_BYTES):
    if rows * cols * 4 <= limit:
        return rows
    for t in (512, 256, 128, 64, 32, 16, 8):
        if rows % t == 0 and t * cols * 4 <= limit:
            return t
    return rows


def _chip_sum(ps, gots, chip_core, name):
    n = len(ps)
    h, C = ps[0].shape
    tr = _row_tile(h, C * n)
    nt = h // tr

    def body(jc_ref, *refs):
        for a in range(n):
            p_ref, g0_ref, g1_ref, g2_ref, o_ref = refs[a], refs[n + 3 * a], refs[n + 3 * a + 1], refs[n + 3 * a + 2], \
                refs[4 * n + a]
            o_ref[...] = ((p_ref[...] + g0_ref[...].astype(F32)) + g1_ref[...].astype(F32)) + g2_ref[...].astype(F32)

    rel = lambda r: pl.BlockSpec((None, tr, C), lambda i, jc_ref: (r, i, 0))
    outs = pl.pallas_call(
        body,
        name=name,
        grid_spec=pltpu.PrefetchScalarGridSpec(
            num_scalar_prefetch=1,
            grid=(nt,),
            in_specs=[pl.BlockSpec((tr, C), lambda i, jc_ref: (i, 0))] * n + [rel(0), rel(1), rel(2)] * n,
            out_specs=[pl.BlockSpec((tr, C), lambda i, jc_ref: (jc_ref[1] * nt + i, 0))] * n,
        ),
        out_shape=[_sds((2 * h, C), F32)] * n,
        compiler_params=_params(("parallel",), 48),
    )(chip_core, *_hbm(*ps, *[g for got in gots for g in (got, got, got)]))
    return list(outs)


def _place_shards(shards, chip, name):
    n = len(shards)
    tiles = [_row_tile(s.shape[0], s.shape[1]) for s in shards]
    steps = max(s.shape[0] // t for s, t in zip(shards, tiles))
    tiles = [s.shape[0] // steps for s in shards]

    def body(j_ref, *refs):
        for a in range(n):
            refs[n + a][...] = refs[a][...].astype(BF16)

    return pl.pallas_call(
        body,
        name=name,
        grid_spec=pltpu.PrefetchScalarGridSpec(
            num_scalar_prefetch=1,
            grid=(steps,),
            in_specs=[pl.BlockSpec((t, s.shape[1]), lambda i, j_ref: (i, 0)) for s, t in zip(shards, tiles)],
            out_specs=[pl.BlockSpec((None, t, s.shape[1]), lambda i, j_ref: (j_ref[0], i, 0))
                       for s, t in zip(shards, tiles)],
        ),
        out_shape=[_sds((N_CHIPS,) + s.shape, BF16) for s in shards],
        compiler_params=_params(("parallel",), 48),
    )(chip, *_hbm(*shards))


def _adamw_update(w, g, m, v):
    c1 = 1.0 - ADAM_B1 ** ADAM_STEP
    c2 = 1.0 - ADAM_B2 ** ADAM_STEP
    nm = ADAM_B1 * m + (1.0 - ADAM_B1) * g
    nv = ADAM_B2 * v + (1.0 - ADAM_B2) * (g * g)
    return (-ADAM_LR) * ((nm / c1) / (jnp.sqrt(nv / c2) + ADAM_EPS) + ADAM_WD * w), nm, nv


def _adamw(params, name):
    n = len(params)
    R, C = params[0][0].shape
    tr = _row_tile(R, C * n)

    def body(*refs):
        for a in range(n):
            w_ref, g_ref, m_ref, v_ref = refs[4 * a:4 * a + 4]
            d_ref, nm_ref, nv_ref, go_ref = refs[4 * n + 4 * a:4 * n + 4 * a + 4]
            g = g_ref[...]
            d_ref[...], nm_ref[...], nv_ref[...] = _adamw_update(w_ref[...], g, m_ref[...], v_ref[...])
            go_ref[...] = g

    spec = pl.BlockSpec((tr, C), lambda i: (i, 0))
    outs = pl.pallas_call(
        body, name=name, grid=(R // tr,), in_specs=[spec] * (4 * n), out_specs=[spec] * (4 * n),
        out_shape=[_sds((R, C), F32)] * (4 * n), compiler_params=_params(("parallel",), 48),
    )(*_hbm(*[t for p in params for t in p]))
    return [tuple(outs[4 * a:4 * a + 4]) for a in range(n)]


def _adamw_whole(params, name):
    n = len(params)

    def body(*refs):
        for a in range(n):
            w_ref, g_ref, m_ref, v_ref = refs[4 * a:4 * a + 4]
            d_ref, nm_ref, nv_ref = refs[4 * n + 3 * a:4 * n + 3 * a + 3]
            d_ref[...], nm_ref[...], nv_ref[...] = _adamw_update(w_ref[...], g_ref[...], m_ref[...], v_ref[...])

    def whole(t):
        return pl.BlockSpec(t.shape, lambda i: (0,) * t.ndim)

    flat = [t for p in params for t in p]
    like = [p[0] for p in params for _ in range(3)]
    outs = pl.pallas_call(
        body, name=name, grid=(1,), in_specs=[whole(t) for t in flat], out_specs=[whole(t) for t in like],
        out_shape=[_sds(t.shape, F32) for t in like], compiler_params=_params(("arbitrary",), 48),
    )(*_hbm(*flat))
    return [tuple(outs[3 * a:3 * a + 3]) for a in range(n)]


def _place():
    return lax.axis_index("x"), lax.axis_index("y"), lax.axis_index("c")


def _chip_of(x, y, r):
    return (x ^ (r >> 1), y ^ (r & 1))


ANY = pl.BlockSpec(memory_space=pl.ANY)


def _gather_weights(placed, cw8):
    nbig = len(placed)
    halves = [s.shape[1] // 2 for s in placed]
    pieces = [max(1, h // 64) for h in halves]
    rows = [h // p for h, p in zip(halves, pieces)]
    order = [(a, q) for q in range(max(pieces)) for a in range(nbig) if q < pieces[a]]
    ici_sem = {(a, q, r): 3 * i + (r - 1) for i, (a, q) in enumerate(order) for r in (1, 2, 3)}
    cw_sem = {r: 3 * len(order) + (r - 1) for r in (1, 2, 3)}
    d2d_sem = {key: 3 * len(order) + 3 + k for key, k in ici_sem.items()}
    nsem = 6 * len(order) + 3

    def body(*refs):
        cw_ref, dsts, gcw_ref = refs[nbig], refs[nbig + 1:2 * nbig + 1], refs[2 * nbig + 1]
        send_sems, recv_sems = refs[2 * nbig + 2:]
        x, y, c = _place()
        j = 2 * x + y

        def piece_rows(a, q, core):
            return pl.ds(pl.multiple_of(core * halves[a] + q * rows[a], 16), rows[a])

        def ici(a, q, r):
            tx, ty = _chip_of(x, y, r)
            k = ici_sem[(a, q, r)]
            region = dsts[a].at[j, piece_rows(a, q, c), :]
            return pltpu.make_async_remote_copy(
                src_ref=region, dst_ref=region, send_sem=send_sems.at[k], recv_sem=recv_sems.at[k],
                device_id=(tx, ty, c), device_id_type=MESH)

        def ici_landed(a, q, r):
            tx, ty = _chip_of(x, y, r)
            k = ici_sem[(a, q, r)]
            region = dsts[a].at[2 * tx + ty, piece_rows(a, q, c), :]
            return pltpu.make_async_remote_copy(
                src_ref=region, dst_ref=region, send_sem=send_sems.at[k], recv_sem=recv_sems.at[k],
                device_id=(tx, ty, c), device_id_type=MESH)

        def d2d(a, q, r, core):
            tx, ty = _chip_of(x, y, r)
            k = d2d_sem[(a, q, r)]
            region = dsts[a].at[2 * tx + ty, piece_rows(a, q, core), :]
            return pltpu.make_async_remote_copy(
                src_ref=region, dst_ref=region, send_sem=send_sems.at[k], recv_sem=recv_sems.at[k],
                device_id=(x, y, 1 - c), device_id_type=MESH)

        def cw_copy(r):
            tx, ty = _chip_of(x, y, r)
            k = cw_sem[r]
            return pltpu.make_async_remote_copy(
                src_ref=cw_ref, dst_ref=gcw_ref.at[j], send_sem=send_sems.at[k], recv_sem=recv_sems.at[k],
                device_id=(tx, ty, c), device_id_type=MESH)

        def cw_landed(r):
            tx, ty = _chip_of(x, y, r)
            k = cw_sem[r]
            region = gcw_ref.at[2 * tx + ty]
            return pltpu.make_async_remote_copy(
                src_ref=region, dst_ref=region, send_sem=send_sems.at[k], recv_sem=recv_sems.at[k],
                device_id=(tx, ty, c), device_id_type=MESH)

        def relay(a, q, origin, to):
            ox, oy = _chip_of(x, y, origin)
            tx, ty = _chip_of(x, y, to)
            k = ici_sem[(a, q, 3)]
            region = dsts[a].at[2 * ox + oy, piece_rows(a, q, c), :]
            return pltpu.make_async_remote_copy(
                src_ref=region, dst_ref=region, send_sem=send_sems.at[k], recv_sem=recv_sems.at[k],
                device_id=(tx, ty, c), device_id_type=MESH)

        first = [ici(a, q, r) for (a, q) in order for r in (1, 2)] + [cw_copy(r) for r in (1, 2, 3)]
        for cp in first:
            cp.start()
        passed = []
        for (a, q) in order:
            for r in (1, 2):
                ici_landed(a, q, r).wait_recv()
                if q % 2 == r - 1:
                    cp = relay(a, q, r, 3 - r)
                    cp.start()
                    passed.append(cp)
                cp = d2d(a, q, r, c)
                cp.start()
                passed.append(cp)
        for (a, q) in order:
            ici_landed(a, q, 3).wait_recv()
            cp = d2d(a, q, 3, c)
            cp.start()
            passed.append(cp)
        for r in (1, 2, 3):
            cw_landed(r).wait_recv()
        for (a, q) in order:
            for r in (1, 2, 3):
                d2d(a, q, r, 1 - c).wait_recv()
        for cp in first + passed:
            cp.wait_send()

    return pl.pallas_call(
        body,
        name="gather_weights",
        in_specs=[ANY] * (nbig + 1),
        out_specs=[ANY] * (nbig + 1),
        out_shape=[_sds(s.shape, s.dtype) for s in placed] + [_sds((N_CHIPS,) + cw8.shape, cw8.dtype)],
        input_output_aliases={a: a for a in range(nbig)},
        scratch_shapes=[pltpu.SemaphoreType.DMA((nsem,)), pltpu.SemaphoreType.DMA((nsem,))],
    )(*placed, cw8)


def _gather_late_start(placed, after, name):
    n = len(placed)
    halves = [s.shape[1] // 2 for s in placed]

    def body(*refs):
        g_refs = refs[0:n]
        send_sems, recv_sems, token = refs[n + 1], refs[n + 2], refs[-1]
        x, y, c = _place()
        j = 2 * x + y
        for a in range(n):
            mine = g_refs[a].at[j, pl.ds(pl.multiple_of(c * halves[a], 16), halves[a]), :]
            for r in (1, 2, 3):
                tx, ty = _chip_of(x, y, r)
                for to_core in (0, 1):
                    k = ((a * 3 + (r - 1)) * 2 + c) * 2 + to_core
                    pltpu.make_async_remote_copy(
                        src_ref=mine, dst_ref=mine, send_sem=send_sems.at[k], recv_sem=recv_sems.at[k],
                        device_id=(tx, ty, to_core), device_id_type=MESH).start()
        token[...] = jnp.zeros_like(token)

    hbm = lambda t: pltpu.HBM(t.shape, t.dtype)
    keep = lambda t: pltpu.with_memory_space_constraint(t, pltpu.HBM)
    nsem = 12 * n
    outs = pl.pallas_call(
        body,
        name=name,
        in_specs=[HBM] * n + [ANY],
        out_specs=(SEM, SEM, *[HBM] * n, pl.BlockSpec(memory_space=pltpu.VMEM)),
        out_shape=(pltpu.SemaphoreType.DMA((nsem,)), pltpu.SemaphoreType.DMA((nsem,)), *[hbm(p) for p in placed],
                   jax.ShapeDtypeStruct((8, 128), F32)),
        input_output_aliases={i: 2 + i for i in range(n)},
        compiler_params=pltpu.CompilerParams(has_side_effects=DATAFLOW),
    )(*[keep(p) for p in placed], after)
    return outs[0], outs[1], list(outs[2:2 + n]), outs[-1]


def _gather_late_wait(send_sems, recv_sems, thru, after, name):
    n = len(thru)
    halves = [s.shape[1] // 2 for s in thru]

    def body(*refs):
        g_refs = refs[0:n]
        send_sems, recv_sems = refs[n], refs[n + 1]
        x, y, c = _place()
        j = 2 * x + y
        for a in range(n):
            mine = g_refs[a].at[j, pl.ds(pl.multiple_of(c * halves[a], 16), halves[a]), :]
            for r in (1, 2, 3):
                tx, ty = _chip_of(x, y, r)
                for other in (0, 1):
                    k_out = ((a * 3 + (r - 1)) * 2 + c) * 2 + other
                    pltpu.make_async_remote_copy(
                        src_ref=mine, dst_ref=mine, send_sem=send_sems.at[k_out], recv_sem=recv_sems.at[k_out],
                        device_id=(tx, ty, other), device_id_type=MESH).wait_send()
                    k_in = ((a * 3 + (r - 1)) * 2 + other) * 2 + c
                    theirs = g_refs[a].at[2 * tx + ty, pl.ds(other * halves[a], halves[a]), :]
                    pltpu.make_async_remote_copy(
                        src_ref=theirs, dst_ref=theirs, send_sem=send_sems.at[k_in], recv_sem=recv_sems.at[k_in],
                        device_id=(tx, ty, other), device_id_type=MESH).wait_recv()

    hbm = lambda t: pltpu.HBM(t.shape, t.dtype)
    outs = pl.pallas_call(
        body,
        name=name,
        in_specs=[HBM] * n + [SEM, SEM, ANY],
        out_specs=[HBM] * n,
        out_shape=[hbm(t) for t in thru],
        input_output_aliases={i: i for i in range(n)},
        compiler_params=pltpu.CompilerParams(has_side_effects=DATAFLOW),
    )(*thru, send_sems, recv_sems, after)
    return list(outs)


D2D_PIECE_ROWS = 64
PAIR_SUM_TILE_BYTES = 2 * MIB


def _pair_sum(gs, gbs, chip_core, name):
    n = len(gs)
    nch, R, C = gs[0].shape
    h = R // 2
    tr = _row_tile(h, C * n, PAIR_SUM_TILE_BYTES)
    nt = h // tr
    rows = min(D2D_PIECE_ROWS, tr)

    def body(jc_ref, *refs):
        g_refs, gb_refs, p_refs, pb_refs = refs[0:n], refs[n:2 * n], refs[2 * n:3 * n], refs[3 * n:4 * n]
        got_refs, send_sems, recv_sems = refs[4 * n:5 * n], refs[5 * n], refs[5 * n + 1]
        i, j = pl.program_id(0), pl.program_id(1)
        x, y, c = _place()

        def copy(a, ti, tj, first, count):
            src_rows = pl.ds(pl.multiple_of((1 - c) * h + ti * tr + first, 16), count)
            dst_rows = pl.ds(pl.multiple_of(ti * tr + first, 16), count)
            return pltpu.make_async_remote_copy(
                src_ref=gb_refs[a].at[tj, src_rows, :], dst_ref=got_refs[a].at[tj, dst_rows, :],
                send_sem=send_sems.at[a, ti, tj], recv_sem=recv_sems.at[a, ti, tj],
                device_id=(x, y, 1 - c), device_id_type=MESH)

        @pl.when((i == 0) & (j == 0))
        def _():
            for ti in range(nt):
                for tj in range(nch):
                    for a in range(n):
                        for q in range(tr // rows):
                            copy(a, ti, tj, q * rows, rows).start()

        for a in range(n):
            copy(a, i, j, 0, tr).wait()
            s = g_refs[a][...] + got_refs[a][j, pl.ds(pl.multiple_of(i * tr, 16), tr), :].astype(F32)
            pb_refs[a][...] = s.astype(BF16)

            @pl.when(j == jc_ref[0])
            def _():
                p_refs[a][...] = s

    by_chip = pl.BlockSpec((None, tr, C), lambda i, j, jc_ref: (j, i, 0))
    outs = pl.pallas_call(
        body,
        name=name,
        grid_spec=pltpu.PrefetchScalarGridSpec(
            num_scalar_prefetch=1,
            grid=(nt, nch),
            in_specs=[pl.BlockSpec((None, tr, C), lambda i, j, jc_ref: (j, jc_ref[1] * nt + i, 0))] * n + [ANY] * n,
            out_specs=[pl.BlockSpec((tr, C), lambda i, j, jc_ref: (i, 0))] * n + [by_chip] * n,
            scratch_shapes=[pltpu.VMEM((nch, h, C), BF16)] * n + [pltpu.SemaphoreType.DMA((n, nt, nch))] * 2,
        ),
        out_shape=[_sds((h, C), F32)] * n + [_sds((nch, h, C), BF16)] * n,
        compiler_params=_params(("arbitrary", "arbitrary"), 48),
    )(chip_core, *_hbm(*gs, *gbs))
    return list(outs[:n]), list(outs[n:])


HBM = pl.BlockSpec(memory_space=pltpu.HBM)
SEM = pl.BlockSpec(memory_space=pltpu.SEMAPHORE)
DATAFLOW = pltpu.SideEffectType.DATAFLOW_SIDE_EFFECTING


def _chip_copy(p_refs, land_refs, send_sems, recv_sems, a, r, blocked):
    x, y, c = _place()
    tx, ty = _chip_of(x, y, r)
    k = a * 3 + (r - 1)
    return pltpu.make_async_remote_copy(
        src_ref=p_refs[a].at[2 * tx + ty] if blocked else p_refs[a], dst_ref=land_refs[a].at[r - 1],
        send_sem=send_sems.at[k], recv_sem=recv_sems.at[k], device_id=(tx, ty, c), device_id_type=MESH)


def _chip_exchange_start(psums, name, blocked=True):
    n = len(psums)
    lands = [lax.empty((3,) + (p.shape[1:] if blocked else p.shape), p.dtype) for p in psums]

    def body(*refs):
        p_refs, land_refs = refs[0:n], refs[n:2 * n]
        send_sems, recv_sems, token = refs[2 * n], refs[2 * n + 1], refs[-1]
        for a in range(n):
            for r in (1, 2, 3):
                _chip_copy(p_refs, land_refs, send_sems, recv_sems, a, r, blocked).start()
        token[...] = jnp.zeros_like(token)

    hbm = lambda t: pltpu.HBM(t.shape, t.dtype)
    keep = lambda t: pltpu.with_memory_space_constraint(t, pltpu.HBM)
    outs = pl.pallas_call(
        body,
        name=name,
        in_specs=[HBM] * (2 * n),
        out_specs=(SEM, SEM, *[HBM] * (2 * n), pl.BlockSpec(memory_space=pltpu.VMEM)),
        out_shape=(pltpu.SemaphoreType.DMA((3 * n,)), pltpu.SemaphoreType.DMA((3 * n,)),
                   *[hbm(p) for p in psums], *[hbm(l) for l in lands], _sds((8, 128), F32)),
        input_output_aliases={i: 2 + i for i in range(2 * n)},
        compiler_params=pltpu.CompilerParams(has_side_effects=DATAFLOW),
    )(*[keep(p) for p in psums], *[keep(l) for l in lands])
    return outs[0], outs[1], list(outs[2:2 + n]), list(outs[2 + n:2 + 2 * n]), outs[-1]


def _chip_exchange_wait(send_sems, recv_sems, p_thru, land_thru, after, name, blocked=True):
    n = len(p_thru)

    def body(*refs):
        p_refs, land_refs = refs[0:n], refs[n:2 * n]
        send_sems, recv_sems = refs[2 * n], refs[2 * n + 1]
        for a in range(n):
            for r in (1, 2, 3):
                copy = _chip_copy(p_refs, land_refs, send_sems, recv_sems, a, r, blocked)
                copy.wait_send()
                copy.wait_recv()

    hbm = lambda t: pltpu.HBM(t.shape, t.dtype)
    outs = pl.pallas_call(
        body,
        name=name,
        in_specs=[HBM] * (2 * n) + [SEM, SEM, ANY],
        out_specs=[HBM] * (2 * n),
        out_shape=[hbm(p) for p in p_thru] + [hbm(l) for l in land_thru],
        input_output_aliases={i: i for i in range(2 * n)},
        compiler_params=pltpu.CompilerParams(has_side_effects=DATAFLOW),
    )(*p_thru, *land_thru, send_sems, recv_sems, after)
    return list(outs[0:n]), list(outs[n:2 * n])


def _pair_share(fulls):
    n = len(fulls)
    halves = [f.shape[0] // 2 for f in fulls]

    def body(*refs):
        full_refs = refs[n:2 * n]
        send_sems, recv_sems = refs[2 * n:]
        x, y, c = _place()

        def half_of(a, core):
            return full_refs[a].at[pl.ds(pl.multiple_of(core * halves[a], 8), halves[a]), :]

        def remote(a, src, dst):
            return pltpu.make_async_remote_copy(
                src_ref=src, dst_ref=dst, send_sem=send_sems.at[a], recv_sem=recv_sems.at[a],
                device_id=(x, y, 1 - c), device_id_type=MESH)

        for a in range(n):
            for q in range(halves[a] // D2D_PIECE_ROWS):
                piece = full_refs[a].at[
                    pl.ds(pl.multiple_of(c * halves[a] + q * D2D_PIECE_ROWS, 8), D2D_PIECE_ROWS), :]
                remote(a, piece, piece).start()
        for a in range(n):
            remote(a, half_of(a, c), half_of(a, c)).wait_send()
            remote(a, half_of(a, 1 - c), half_of(a, 1 - c)).wait_recv()

    return pl.pallas_call(
        body,
        name="pair_share",
        in_specs=[ANY] * n,
        out_specs=[ANY] * n,
        out_shape=[_sds(f.shape, F32) for f in fulls],
        input_output_aliases={a: a for a in range(n)},
        scratch_shapes=[pltpu.SemaphoreType.DMA((n,)), pltpu.SemaphoreType.DMA((n,))],
    )(*fulls)


def _small_pair_sum(s):
    R, C = s.shape
    V = SMALL_VECTOR_ROWS

    def body(s_ref, v_ref, m_ref, sib, send_sem, recv_sem):
        x, y, c = _place()

        def to_sib(src, dst):
            return pltpu.make_async_remote_copy(
                src_ref=src, dst_ref=dst, send_sem=send_sem, recv_sem=recv_sem,
                device_id=(x, y, 1 - c), device_id_type=MESH)

        for q in range(R // 8):
            to_sib(s_ref.at[pl.ds(8 * q, 8), :], sib.at[pl.ds(8 * q, 8), :]).start()
        to_sib(s_ref, sib).wait()
        v_ref[...] = s_ref[pl.ds(0, V), :] + sib[pl.ds(0, V), :]
        m_ref[...] = (s_ref[pl.ds(V, R - V), :] + sib[pl.ds(V, R - V), :]).astype(BF16)

    return pl.pallas_call(
        body,
        name="small_pair_sum",
        in_specs=[pl.BlockSpec(memory_space=pltpu.VMEM)],
        out_specs=[pl.BlockSpec(memory_space=pltpu.VMEM)] * 2,
        out_shape=[jax.ShapeDtypeStruct((V, C), F32), jax.ShapeDtypeStruct((R - V, C), BF16)],
        scratch_shapes=[pltpu.VMEM((R, C), F32), pltpu.SemaphoreType.DMA, pltpu.SemaphoreType.DMA],
    )(s)


def _small_total(chip, own, landed):
    V, C = own[0].shape
    M = own[1].shape[0] // 2
    nv = len(SMALL_VECTORS)

    def body(j_ref, v_ref, m_ref, lv_ref, lm_ref, *refs):
        vec_refs, mat_refs, (chips_v, chips_m, total_v) = refs[0:nv], refs[nv:nv + 2], refs[nv + 2:]
        j = j_ref[0]
        chips_v[j] = v_ref[...]
        chips_m[j] = m_ref[...]
        for r in (1, 2, 3):
            chips_v[j ^ r] = lv_ref[r - 1]
            chips_m[j ^ r] = lm_ref[r - 1]
        total_v[...] = (chips_v[0] + chips_v[1]) + (chips_v[2] + chips_v[3])
        for o_ref, (_, row, rows, lanes, flat) in zip(vec_refs, SMALL_VECTORS):
            if flat:
                for k in range(rows):
                    o_ref[:, k * lanes:(k + 1) * lanes] = total_v[pl.ds(row + k, 1), 0:lanes]
            else:
                o_ref[...] = total_v[pl.ds(row, rows), 0:lanes]
        for half, o_ref in enumerate(mat_refs):
            part = lambda k: chips_m[k, pl.ds(half * M, M), :].astype(F32)
            o_ref[...] = (part(0) + part(1)) + (part(2) + part(3))

    vmem = pl.BlockSpec(memory_space=pltpu.VMEM)
    shapes = [(1, rows * lanes) if flat else (rows, lanes) for _, _, rows, lanes, flat in SMALL_VECTORS] + [(M, C)] * 2
    outs = pl.pallas_call(
        body,
        name="small_total",
        grid=(1,),
        in_specs=[pl.BlockSpec(memory_space=pltpu.SMEM), vmem, vmem, vmem, vmem],
        out_specs=[pl.BlockSpec(s, lambda i: (0, 0)) for s in shapes],
        out_shape=[_sds(s, F32) for s in shapes],
        scratch_shapes=[pltpu.VMEM((N_CHIPS, V, C), F32), pltpu.VMEM((N_CHIPS, 2 * M, C), BF16),
                        pltpu.VMEM((V, C), F32)],
        compiler_params=_params(("arbitrary",), 32),
    )(chip, own[0], own[1], landed[0], landed[1])
    return dict(zip([name for name, *_ in SMALL_VECTORS] + ["w_rg_a", "w_rg_x"], outs))


def _local_grads(x, target, g_pre, w_in_g, b_gate, conv_w, conv_b, w_rg_a, b_rg_a, w_rg_x, b_rg_x, lam, sinks,
                 out_weights, fwd_token, g_post, on_out_grads, on_w_in_grad):
    b_a = b_rg_a.reshape(1, D_RNN)
    b_x = b_rg_x.reshape(1, D_RNN)

    proj, ht = _proj_fwd(x, g_pre, w_in_g)
    y_rnn, z_rnn, conv, z_rnn_t = _rnn_fwd(proj, conv_w, conv_b, w_rg_a, w_rg_x, b_a, b_x, lam, fwd_token)
    bias = _attn_bias()
    y_attn, z_attn, lse = _attn_fwd(proj, sinks, bias)
    w_rnn_out, w_attn_out, w_out = out_weights(z_attn)
    dyx, dz_rnn, dz_attn, dml, dout, dbr_rnn, dbr_attn, merged_t, z_attn_t, head_small = _head(
        x, target, z_rnn, z_attn, proj, b_gate, g_post, w_rnn_out, w_attn_out, w_out)
    out_grads = [_matmul_t(z_rnn_t, dbr_rnn, "dw_rnn_out"), _matmul_t(z_attn_t, dbr_attn, "dw_attn_out"),
                 _matmul_t(merged_t, dout, "dw_out")]
    shard_rows = lambda d: d.reshape(N_CHIPS, OUT_SHARD, D_MODEL)
    token = on_out_grads([shard_rows(g) for g, _ in out_grads], [shard_rows(gb) for _, gb in out_grads])
    dq, dk, dv, dag, attn_small = _attn_bwd(proj, y_attn, lse, dz_attn, sinks, bias, token)
    drx, drg, dwa, dwx, rnn_small = _rnn_bwd(proj, conv, y_rnn, dz_rnn, conv_w, w_rg_a, w_rg_x, b_a, b_x, lam)
    dproj = [drx, drg, dq, dk, dv, dag, dml]
    token = on_w_in_grad(*_dw_in(ht, dproj))
    grad_x, dh_small = _dh_bwd(dproj, w_in_g, x, dyx, g_pre, token)
    small = jnp.concatenate([rnn_small, head_small, dh_small + attn_small,
                             dwa.reshape(64, 1024), dwx.reshape(64, 1024)], axis=0)
    return grad_x, small


ROW_LOSS = 11
SMALL_VECTORS = (
    ("b_rg_a", 0, 1, D_RNN, False), ("b_rg_x", 1, 1, D_RNN, False), ("lru_lambda", 2, 1, D_RNN, False),
    ("conv_b", 3, 1, D_RNN, False), ("conv_w", 4, CONV_W, D_RNN, False), ("post_norm_g", 8, 1, D_MODEL, False),
    ("b_gate", 9, 2, D_MODEL, True), ("loss", ROW_LOSS, 1, 1, False), ("pre_norm_g", 16, 1, D_MODEL, False),
    ("attn_sinks", 17, 1, N_Q_HEADS, False))


WEIGHTS = ["pre_norm_g", "w_in", "b_gate", "conv_w", "conv_b", "w_rg_a", "b_rg_a", "w_rg_x", "b_rg_x", "lru_lambda",
           "attn_sinks", "w_rnn_out", "w_attn_out", "w_out", "post_norm_g"]
BIG = ["w_in", "w_rnn_out", "w_attn_out", "w_out"]


def kernel(x, pre_norm_g, w_in, b_gate, conv_w, conv_b, w_rg_a, b_rg_a, w_rg_x, b_rg_x, lru_lambda, attn_sinks, w_rnn_out, w_attn_out, w_out, post_norm_g, loss_target, m_pre_norm_g, m_w_in, m_b_gate, m_conv_w, m_conv_b, m_w_rg_a, m_b_rg_a, m_w_rg_x, m_b_rg_x, m_lru_lambda, m_attn_sinks, m_w_rnn_out, m_w_attn_out, m_w_out, m_post_norm_g, v_pre_norm_g, v_w_in, v_b_gate, v_conv_w, v_conv_b, v_w_rg_a, v_b_rg_a, v_w_rg_x, v_b_rg_x, v_lru_lambda, v_attn_sinks, v_w_rnn_out, v_w_attn_out, v_w_out, v_post_norm_g):
    w = dict(pre_norm_g=pre_norm_g, w_in=w_in, b_gate=b_gate, conv_w=conv_w, conv_b=conv_b, w_rg_a=w_rg_a,
             b_rg_a=b_rg_a, w_rg_x=w_rg_x, b_rg_x=b_rg_x, lru_lambda=lru_lambda, attn_sinks=attn_sinks,
             w_rnn_out=w_rnn_out, w_attn_out=w_attn_out, w_out=w_out, post_norm_g=post_norm_g)
    m = dict(pre_norm_g=m_pre_norm_g, w_in=m_w_in, b_gate=m_b_gate, conv_w=m_conv_w, conv_b=m_conv_b, w_rg_a=m_w_rg_a,
             b_rg_a=m_b_rg_a, w_rg_x=m_w_rg_x, b_rg_x=m_b_rg_x, lru_lambda=m_lru_lambda, attn_sinks=m_attn_sinks,
             w_rnn_out=m_w_rnn_out, w_attn_out=m_w_attn_out, w_out=m_w_out, post_norm_g=m_post_norm_g)
    v = dict(pre_norm_g=v_pre_norm_g, w_in=v_w_in, b_gate=v_b_gate, conv_w=v_conv_w, conv_b=v_conv_b, w_rg_a=v_w_rg_a,
             b_rg_a=v_b_rg_a, w_rg_x=v_w_rg_x, b_rg_x=v_b_rg_x, lru_lambda=v_lru_lambda, attn_sinks=v_attn_sinks,
             w_rnn_out=v_w_rnn_out, w_attn_out=v_w_attn_out, w_out=v_w_out, post_norm_g=v_post_norm_g)
    chip = 2 * lax.axis_index("x") + lax.axis_index("y")

    chip_idx = chip.astype(jnp.int32).reshape(1)
    chip_core = jnp.stack([chip, lax.axis_index("c")]).astype(jnp.int32)
    cw8 = jnp.pad(conv_w[0], ((0, 8 - CONV_W), (0, 0)))
    placed = _place_shards([w_in[0], w_rnn_out[0], w_attn_out[0], w_out[0]], chip_idx, "place_shards")
    win_g, cw_g = _gather_weights(placed[:1], cw8)
    late_send, late_recv, late_thru, late_token = _gather_late_start(placed[1:], win_g, "gather_late_start")
    cw_g = lax.dynamic_update_slice_in_dim(cw_g, cw8[None], chip, axis=0)
    conv_w_full = jnp.transpose(cw_g[:, 0:CONV_W, :], (1, 0, 2)).reshape(CONV_W, D_RNN)

    started = {}

    def start_reduction(tag, grads, grads_b16):
        psums, psums_b16 = _pair_sum(grads, grads_b16, chip_core, "pair_sum_" + tag)
        send_sems, recv_sems, p_thru, land_thru, token = _chip_exchange_start(psums_b16, "chip_exchange_start_" + tag)
        started[tag] = (psums, send_sems, recv_sems, p_thru, land_thru)
        return token

    def end_reduction(tag, after):
        psums, send_sems, recv_sems, p_thru, land_thru = started[tag]
        _, landed = _chip_exchange_wait(send_sems, recv_sems, p_thru, land_thru, after, "chip_exchange_wait_" + tag)
        return _chip_sum(psums, landed, chip_core, "chip_sum_" + tag)

    def out_weights(after):
        gathered = _gather_late_wait(late_send, late_recv, late_thru, after, "gather_late_wait")
        return [g.reshape(D_MODEL, D_MODEL) for g in gathered]

    grad_x, small = _local_grads(
        x[0], loss_target[0], pre_norm_g, win_g, b_gate, conv_w_full, conv_b, w_rg_a[0], b_rg_a[0], w_rg_x[0],
        b_rg_x[0], lru_lambda, attn_sinks[0], out_weights, late_token, post_norm_g,
        on_out_grads=lambda grads, grads_b16: start_reduction("out", grads, grads_b16),
        on_w_in_grad=lambda grad, grad_b16: start_reduction("in", [grad], [grad_b16]))

    small_chip = _small_pair_sum(small)
    small_send, small_recv, small_thru, small_land, small_token = _chip_exchange_start(
        list(small_chip), "small_exchange_start", blocked=False)

    halves = end_reduction("in", small_token) + end_reduction("out", small_token)
    gbig = dict(zip(BIG, _pair_share(halves)))

    grads, delta, new_m, new_v = {}, {}, {}, {}
    for names, tag in ((BIG[:1], "adamw_in"), (BIG[1:], "adamw_out")):
        updates = _adamw([(w[n][0], gbig[n], m[n][0], v[n][0]) for n in names], tag)
        for n, (d, nm, nv, g) in zip(names, updates):
            grads[n], delta[n], new_m[n], new_v[n] = g[None], d[None], nm[None], nv[None]

    small_own, small_landed = _chip_exchange_wait(small_send, small_recv, small_thru, small_land, delta[BIG[-1]],
                                                  "small_exchange_wait", blocked=False)
    gsmall = _small_total(chip_idx, small_own, small_landed)
    total_loss = gsmall.pop("loss")[0, 0]
    conv_shard = D_RNN // N_CHIPS
    gsmall["conv_w"] = lax.dynamic_slice_in_dim(gsmall["conv_w"], chip * conv_shard, conv_shard, axis=1)
    for n in gsmall:
        grads[n] = gsmall[n].reshape(w[n].shape)
    updates = _adamw_whole([(w[n], grads[n], m[n], v[n]) for n in gsmall], "adamw_small")
    for n, (d, nm, nv) in zip(gsmall, updates):
        delta[n], new_m[n], new_v[n] = d, nm, nv

    return (total_loss, grad_x[None], *[grads[n] for n in WEIGHTS], *[delta[n] for n in WEIGHTS],
            *[new_m[n] for n in WEIGHTS], *[new_v[n] for n in WEIGHTS])
```

```python
import functools
import math

import jax
import jax.numpy as jnp
import numpy as np
from jax import lax
from jax.experimental import pallas as pl
from jax.experimental.pallas import tpu as pltpu

F32 = jnp.float32
BF16 = jnp.bfloat16

D_MODEL = 1024
D_RNN = 1024
RNN_BLOCKS = 16
RNN_BLOCK_W = 64
CONV_W = 4
LRU_C = 8.0
N_Q_HEADS = 16
N_KV_HEADS = 4
GROUP = 4
HEAD_DIM = 64
D_KV = 256
BLOCK = 128
ALIBI_MAX_BIAS = 8.0
EPS = 1e-6
D_IN = 6656
N_CHIPS = 4
W_IN_SHARD = D_IN // N_CHIPS
OUT_SHARD = D_MODEL // N_CHIPS
ADAM_LR = 0.001
ADAM_B1 = 0.9
ADAM_B2 = 0.999
ADAM_EPS = 1e-08
ADAM_WD = 0.01
ADAM_STEP = 10
NEG_BIG = -1e30
MIB = 1 << 20

COL_RNN_X = 0
COL_RNN_GATE = 4
COL_Q = 8
COL_K = 12
COL_V = 13
COL_ATTN_GATE = 14
COL_MERGE = 18

RNN_TILE = 256
RNN_CHUNK = 512
SMALL_ROWS = 152
SMALL_VECTOR_ROWS = 24
MESH = pl.DeviceIdType.MESH


def _sds(shape, dtype):
    return pltpu.HBM(shape, dtype)


def _params(sem=None, vmem_mib=None):
    kw = {}
    if sem is not None:
        kw["dimension_semantics"] = sem
    if vmem_mib is not None:
        kw["vmem_limit_bytes"] = vmem_mib * MIB
    return pltpu.CompilerParams(**kw)


def _hbm(*arrays):
    return [pltpu.with_memory_space_constraint(a, pltpu.HBM) for a in arrays]


def _dot(a, b):
    return jnp.dot(a, b, preferred_element_type=F32)


def _dot_nt(a, b):
    return lax.dot_general(a, b, (((1,), (1,)), ((), ())), preferred_element_type=F32)


def _dot_tn(a, b):
    return lax.dot_general(a, b, (((0,), (0,)), ((), ())), preferred_element_type=F32)


def _sigmoid(x):
    return 0.5 * jnp.tanh(0.5 * x) + 0.5


def _sigmoid_small(x):
    return 1.0 / (1.0 + jnp.exp(-x))


def _softplus(x):
    return jnp.maximum(x, 0.0) + jnp.log(1.0 + jnp.exp(-jnp.abs(x)))


def _one_minus_square(a, log_a):
    return -jnp.tanh(log_a) * (a * a + 1.0)


def _proj_fwd(x, g_pre, w_in_g):
    T = x.shape[0]
    tm = min(1024, T)

    def body(x_ref, g_ref, w_ref, proj_ref, ht_ref, h_s):
        @pl.when(pl.program_id(1) == 0)
        def _():
            xv = x_ref[...]
            rstd = lax.rsqrt(jnp.mean(xv * xv, axis=-1, keepdims=True) + EPS)
            hf = (xv * rstd) * g_ref[...]
            h_s[...] = hf.astype(BF16)
            ht_ref[...] = hf.T.astype(BF16)

        proj_ref[...] = _dot(h_s[...], w_ref[...]).astype(BF16)

    return pl.pallas_call(
        body,
        name="proj_fwd",
        grid=(T // tm, N_CHIPS),
        in_specs=[
            pl.BlockSpec((tm, D_MODEL), lambda i, j: (i, 0)),
            pl.BlockSpec((1, D_MODEL), lambda i, j: (0, 0)),
            pl.BlockSpec((None, D_MODEL, W_IN_SHARD), lambda i, j: (j, 0, 0)),
        ],
        out_specs=[
            pl.BlockSpec((tm, W_IN_SHARD), lambda i, j: (i, j)),
            pl.BlockSpec((D_MODEL, tm), lambda i, j: (0, i)),
        ],
        out_shape=[_sds((T, D_IN), BF16), _sds((D_MODEL, T), BF16)],
        scratch_shapes=[pltpu.VMEM((tm, D_MODEL), BF16)],
        compiler_params=_params(("parallel", "arbitrary"), 48),
    )(*_hbm(x, g_pre, w_in_g))


def _shift_down(x, tail, s, row):
    n = x.shape[0]
    xs = pltpu.roll(x, s, 0)
    tail_t = jnp.tile(pltpu.roll(tail, s, 0), (n // 8, 1))
    return jnp.where(row < s, tail_t, xs)


def _shift_up(x, head, s, row):
    n = x.shape[0]
    xs = pltpu.roll(x, n - s, 0)
    head_t = jnp.tile(pltpu.roll(head, 8 - s, 0), (n // 8, 1))
    return jnp.where(row >= n - s, head_t, xs)


def _conv_taps(x, tail, row):
    return [_shift_down(x, tail, 3, row), _shift_down(x, tail, 2, row), _shift_down(x, tail, 1, row), x]


def _rglru_gates(c, wa, wx, ba, bx, lam):
    cb = c.astype(BF16)
    r = _sigmoid_small(_dot(cb, wa) + ba)
    i = _sigmoid(_dot(cb, wx) + bx)
    log_a = (-LRU_C) * r * _softplus(-lam)
    a = jnp.exp(log_a)
    w = _one_minus_square(a, log_a)
    inv_mult = lax.rsqrt(w)
    return cb, r, i, a, w * inv_mult, inv_mult


GATE_BLOCKS_PER_TILE = RNN_TILE // RNN_BLOCK_W
GATE_BLOCKS = pl.BlockSpec((GATE_BLOCKS_PER_TILE, RNN_BLOCK_W, RNN_BLOCK_W), lambda j, t: (j, 0, 0))


def _fill_block_diag(bd_ref, w_ref):
    bd_ref[...] = jnp.zeros_like(bd_ref)
    for a in range(GATE_BLOCKS_PER_TILE):
        lo = a * RNN_BLOCK_W
        bd_ref[lo:lo + RNN_BLOCK_W, lo:lo + RNN_BLOCK_W] = w_ref[a].astype(BF16)


SUBLANES = 8


def _scan_down(a, u, row):
    n = a.shape[0]
    s = 1
    while s < SUBLANES:
        a_sh = jnp.where(row >= s, pltpu.roll(a, s, 0), 1.0)
        u_sh = jnp.where(row >= s, pltpu.roll(u, s, 0), 0.0)
        u = a * u_sh + u
        a = a * a_sh
        s *= 2
    while s < n:
        u = jnp.concatenate([u[:s], a[s:] * u[:n - s] + u[s:]], axis=0)
        a = jnp.concatenate([a[:s], a[s:] * a[:n - s]], axis=0)
        s *= 2
    return a, u


def _scan_up(b, u, row):
    n = b.shape[0]
    s = 1
    while s < SUBLANES:
        b_sh = jnp.where(row < n - s, pltpu.roll(b, n - s, 0), 1.0)
        u_sh = jnp.where(row < n - s, pltpu.roll(u, n - s, 0), 0.0)
        u = b * u_sh + u
        b = b * b_sh
        s *= 2
    while s < n:
        u = jnp.concatenate([b[:n - s] * u[s:] + u[:n - s], u[n - s:]], axis=0)
        b = jnp.concatenate([b[:n - s] * b[s:], b[n - s:]], axis=0)
        s *= 2
    return b, u


LANES = 128


def _chunk_scan(a, u, a_s, u_s, hl_s, al_s, carry, reverse):
    n, width = a.shape
    groups = n // SUBLANES
    order = range(SUBLANES - 1, -1, -1) if reverse else range(SUBLANES)
    row = lax.broadcasted_iota(jnp.int32, (groups, LANES), 0)
    for l in range(width // LANES):
        lanes = slice(l * LANES, (l + 1) * LANES)
        a_l, u_l, hl_l, al_l = a_s.at[l], u_s.at[l], hl_s.at[l], al_s.at[l]
        a_l[...] = a[:, lanes]
        u_l[...] = u[:, lanes]
        h_loc = a_loc = None
        for r in order:
            rows = pl.ds(r, groups, stride=SUBLANES)
            a_r, u_r = a_l[rows, :], u_l[rows, :]
            h_loc, a_loc = (u_r, a_r) if h_loc is None else (a_r * h_loc + u_r, a_r * a_loc)
            hl_l[rows, :] = h_loc
            al_l[rows, :] = a_loc
        if reverse:
            a_cum, ends = _scan_up(a_loc, h_loc, row)
            ends = ends + a_cum * carry[:, lanes]
            enters = jnp.where(row == groups - 1, carry[:, lanes], pltpu.roll(ends, groups - 1, 0))
        else:
            a_cum, ends = _scan_down(a_loc, h_loc, row)
            ends = ends + a_cum * carry[:, lanes]
            enters = jnp.where(row == 0, carry[:, lanes], pltpu.roll(ends, 1, 0))
        for r in range(SUBLANES):
            rows = pl.ds(r, groups, stride=SUBLANES)
            hl_l[rows, :] = hl_l[rows, :] + al_l[rows, :] * enters
    return jnp.concatenate([hl_s[l] for l in range(width // LANES)], axis=1)


def _rnn_fwd(proj, conv_w, conv_b, w_a, w_x, b_a, b_x, lam, token):
    T = proj.shape[0]
    tc, ct = RNN_CHUNK, RNN_TILE
    nt = T // tc

    def body(x_ref, rg_ref, cw_ref, cb_ref, wa_ref, wx_ref, ba_ref, bx_ref, lam_ref, token_ref, h_ref, z_ref, c_ref,
             zt_ref, xtail, hcarry, wa_s, wx_s, a_s, u_s, hl_s, al_s):
        @pl.when(pl.program_id(1) == 0)
        def _():
            xtail[...] = jnp.zeros_like(xtail)
            hcarry[...] = jnp.zeros_like(hcarry)
            _fill_block_diag(wa_s, wa_ref)
            _fill_block_diag(wx_s, wx_ref)

        row = lax.broadcasted_iota(jnp.int32, (tc, ct), 0)
        x = x_ref[...].astype(F32)
        taps = _conv_taps(x, xtail[...], row)
        c = cb_ref[...] + cw_ref[pl.ds(0, 1), :] * taps[0]
        for k in range(1, CONV_W):
            c = c + cw_ref[pl.ds(k, 1), :] * taps[k]
        xtail[...] = x[tc - 8:, :]
        c_ref[...] = c
        _, _, i, a, mult, _ = _rglru_gates(c, wa_s[...], wx_s[...], ba_ref[...], bx_ref[...], lam_ref[...])
        h = _chunk_scan(a, mult * (i * c), a_s, u_s, hl_s, al_s, hcarry[...], reverse=False)
        h_ref[...] = h
        hcarry[...] = h_ref[pl.ds(tc - 1, 1), :]
        rg = rg_ref[...].astype(F32)
        z = h * (rg * _sigmoid(rg))
        z_ref[...] = z.astype(BF16)
        zt_ref[...] = z.T.astype(BF16)

    col = lambda off: (lambda j, t: (t, off + j))
    vec = pl.BlockSpec((1, ct), lambda j, t: (0, j))
    return pl.pallas_call(
        body,
        name="rnn_fwd",
        grid=(D_RNN // ct, nt),
        in_specs=[
            pl.BlockSpec((tc, ct), col(COL_RNN_X)),
            pl.BlockSpec((tc, ct), col(COL_RNN_GATE)),
            pl.BlockSpec((CONV_W, ct), lambda j, t: (0, j)),
            vec, GATE_BLOCKS, GATE_BLOCKS, vec, vec, vec,
            pl.BlockSpec((8, 128), lambda j, t: (0, 0)),
        ],
        out_specs=[pl.BlockSpec((tc, ct), lambda j, t: (t, j))] * 3 + [pl.BlockSpec((ct, tc), lambda j, t: (j, t))],
        out_shape=[_sds((T, D_RNN), F32), _sds((T, D_RNN), BF16), _sds((T, D_RNN), F32), _sds((D_RNN, T), BF16)],
        scratch_shapes=[pltpu.VMEM((8, ct), F32), pltpu.VMEM((1, ct), F32)] + [pltpu.VMEM((ct, ct), BF16)] * 2 + [
            pltpu.VMEM((ct // LANES, tc, LANES), F32)] * 4,
        compiler_params=_params(("parallel", "arbitrary"), 32),
    )(*_hbm(proj, proj, conv_w, conv_b, w_a, w_x, b_a, b_x, lam, token))


def _rnn_bwd(proj, conv, y_rnn, dz_rnn, conv_w, w_a, w_x, b_a, b_x, lam):
    T = proj.shape[0]
    tc, ct = RNN_CHUNK, RNN_TILE
    nt = T // tc
    hb = tc // 8

    def body(x_ref, c_ref, rg_ref, h_ref, hh_ref, dz_ref, cw_ref, wa_ref, wx_ref, ba_ref, bx_ref, lam_ref,
             dx_ref, drg_ref, dwa_ref, dwx_ref, sm_ref, lam_carry, a_carry, dc_head, wa_s, wx_s, dwa_s, dwx_s,
             b_s, dy_s, hl_s, al_s):
        t = pl.program_id(1)
        first_chunk = t == nt - 1

        @pl.when(t == 0)
        def _():
            lam_carry[...] = jnp.zeros_like(lam_carry)
            a_carry[...] = jnp.zeros_like(a_carry)
            dc_head[...] = jnp.zeros_like(dc_head)
            dwa_s[...] = jnp.zeros_like(dwa_s)
            dwx_s[...] = jnp.zeros_like(dwx_s)
            sm_ref[...] = jnp.zeros_like(sm_ref)
            _fill_block_diag(wa_s, wa_ref)
            _fill_block_diag(wx_s, wx_ref)

        row = lax.broadcasted_iota(jnp.int32, (tc, ct), 0)
        keep = jnp.where(first_chunk, 0.0, 1.0)
        x = x_ref[...].astype(F32)
        c = c_ref[...]
        lam = lam_ref[...]
        cb, r, i, a, mult, inv_mult = _rglru_gates(c, wa_s[...], wx_s[...], ba_ref[...], bx_ref[...], lam)
        h = h_ref[...]
        h_prev = _shift_down(h, hh_ref[...] * keep, 1, row)
        rg = rg_ref[...].astype(F32)
        dz = dz_ref[...]
        sg = _sigmoid(rg)
        drg_ref[...] = (dz * h * (sg * (1.0 + rg * (1.0 - sg)))).astype(BF16)
        dy = dz * (rg * sg)
        b = jnp.where(row >= tc - 1, a_carry[pl.ds(0, 1), :], pltpu.roll(a, tc - 1, 0))
        lt = _chunk_scan(b, dy, b_s, dy_s, hl_s, al_s, lam_carry[pl.ds(0, 1), :], reverse=True)
        lam_carry[...] = lt[0:8, :]
        a_carry[...] = a[0:8, :]
        ic = i * c
        dmult = lt * ic
        di = lt * mult * c
        dc = lt * mult * i
        dlog_a = a * (lt * h_prev - dmult * a * inv_mult)
        sp = _softplus(-lam)
        dpre_r = dlog_a * ((-LRU_C) * sp) * (r * (1.0 - r))
        dpre_i = di * (i * (1.0 - i))
        dlam_row = jnp.sum(dlog_a * r, axis=0, keepdims=True) * (LRU_C * _sigmoid(-lam))
        dpr_b = dpre_r.astype(BF16)
        dpi_b = dpre_i.astype(BF16)
        dwa_s[...] += _dot_tn(cb, dpr_b)
        dwx_s[...] += _dot_tn(cb, dpi_b)
        dc = dc + _dot_nt(dpr_b, wa_s[...]) + _dot_nt(dpi_b, wx_s[...])
        head = dc_head[...]
        dx = cw_ref[pl.ds(3, 1), :] * dc
        sm_ref[pl.ds(4 + 3, 1), :] += jnp.sum(dc * x, axis=0, keepdims=True)
        for m in range(1, CONV_W):
            up = _shift_up(dc, head, m, row)
            dx = dx + cw_ref[pl.ds(3 - m, 1), :] * up
            sm_ref[pl.ds(4 + 3 - m, 1), :] += jnp.sum(up * x, axis=0, keepdims=True)
        dx_ref[...] = dx.astype(BF16)
        dc_head[...] = dc[0:8, :]
        sm_ref[pl.ds(0, 1), :] += jnp.sum(dpre_r, axis=0, keepdims=True)
        sm_ref[pl.ds(1, 1), :] += jnp.sum(dpre_i, axis=0, keepdims=True)
        sm_ref[pl.ds(2, 1), :] += dlam_row
        sm_ref[pl.ds(3, 1), :] += jnp.sum(dc, axis=0, keepdims=True)

        @pl.when(first_chunk)
        def _():
            for k in range(GATE_BLOCKS_PER_TILE):
                lo = k * RNN_BLOCK_W
                dwa_ref[k] = dwa_s[lo:lo + RNN_BLOCK_W, lo:lo + RNN_BLOCK_W]
                dwx_ref[k] = dwx_s[lo:lo + RNN_BLOCK_W, lo:lo + RNN_BLOCK_W]

    rev = lambda off: (lambda j, t: (nt - 1 - t, off + j))
    halo = lambda off: (lambda j, t: (jnp.maximum((nt - 1 - t) * hb - 1, 0), off + j))
    vec = pl.BlockSpec((1, ct), lambda j, t: (0, j))
    mat = GATE_BLOCKS
    return pl.pallas_call(
        body,
        name="rnn_bwd",
        grid=(D_RNN // ct, nt),
        in_specs=[
            pl.BlockSpec((tc, ct), rev(COL_RNN_X)),
            pl.BlockSpec((tc, ct), rev(0)),
            pl.BlockSpec((tc, ct), rev(COL_RNN_GATE)),
            pl.BlockSpec((tc, ct), rev(0)),
            pl.BlockSpec((8, ct), halo(0)),
            pl.BlockSpec((tc, ct), rev(0)),
            pl.BlockSpec((CONV_W, ct), lambda j, t: (0, j)),
            mat, mat, vec, vec, vec,
        ],
        out_specs=[
            pl.BlockSpec((tc, ct), rev(0)),
            pl.BlockSpec((tc, ct), rev(0)),
            mat, mat,
            pl.BlockSpec((8, ct), lambda j, t: (0, j)),
        ],
        out_shape=[_sds((T, D_RNN), BF16), _sds((T, D_RNN), BF16), _sds(w_a.shape, F32), _sds(w_x.shape, F32),
                   _sds((8, D_RNN), F32)],
        scratch_shapes=[pltpu.VMEM((8, ct), F32)] * 3 + [pltpu.VMEM((ct, ct), BF16)] * 2 + [
            pltpu.VMEM((ct, ct), F32)] * 2 + [pltpu.VMEM((ct // LANES, tc, LANES), F32)] * 4,
        compiler_params=_params(("parallel", "arbitrary"), 32),
    )(*_hbm(proj, conv, proj, y_rnn, y_rnn, dz_rnn, conv_w, w_a, w_x, b_a, b_x, lam))


def _attn_bias():
    qi = np.arange(BLOCK)[:, None]
    kj = np.arange(BLOCK)[None, :]
    dist_cur = (qi - kj).astype(np.float32)
    slopes = np.float32(2.0) ** (-ALIBI_MAX_BIAS * np.arange(1, N_Q_HEADS + 1, dtype=np.float32) / N_Q_HEADS)
    slopes = slopes[:, None, None]
    prev = np.where(kj > qi, -slopes * (dist_cur + np.float32(BLOCK)), np.float32(NEG_BIG))
    cur = np.where(kj <= qi, -slopes * dist_cur, np.float32(NEG_BIG))
    later = np.concatenate([prev, cur], axis=-1)
    first = np.concatenate([np.full_like(prev, NEG_BIG), cur], axis=-1)
    return jnp.asarray(np.stack([first, later]).astype(np.float32))


def _attn_exps(s_prev, s_cur, sink, bias):
    s_prev = s_prev + bias[:, 0:BLOCK]
    s_cur = s_cur + bias[:, BLOCK:2 * BLOCK]
    m = jnp.maximum(jnp.max(jnp.maximum(s_prev, s_cur), axis=-1, keepdims=True), sink)
    p_prev = jnp.exp(s_prev - m)
    p_cur = jnp.exp(s_cur - m)
    total = jnp.sum(p_prev + p_cur, axis=-1, keepdims=True) + jnp.exp(sink - m)
    return p_prev, p_cur, 1.0 / total, m + jnp.log(total)


def _attn_probs(s_prev, s_cur, sink, bias, lse):
    p_prev = jnp.exp((s_prev + bias[:, 0:BLOCK]) - lse)
    p_cur = jnp.exp((s_cur + bias[:, BLOCK:2 * BLOCK]) - lse)
    return p_prev, p_cur, jnp.exp(sink - lse)


def _stack_heads(ref_or_val, hk, dtype):
    parts = [ref_or_val[:, (GROUP * hk + g) * HEAD_DIM:(GROUP * hk + g + 1) * HEAD_DIM] for g in range(GROUP)]
    return jnp.concatenate(parts, axis=0).astype(dtype)


ATTN_SCALE = HEAD_DIM ** -0.5


def _bias_spec():
    return pl.BlockSpec((None, N_Q_HEADS, BLOCK, 2 * BLOCK), lambda i: (jnp.minimum(i, 1), 0, 0, 0))


def _attn_fwd(proj, sinks, bias):
    T = proj.shape[0]
    nb = T // BLOCK

    def body(sink_ref, bias_ref, q_ref, kp_ref, kc_ref, vp_ref, vc_ref, ag0_ref, ag1_ref, y_ref, z_ref, lse_ref):
        kvs = [slice(hk * HEAD_DIM, (hk + 1) * HEAD_DIM) for hk in range(N_KV_HEADS)]
        qgs = [(_stack_heads(q_ref, hk, F32) * ATTN_SCALE).astype(BF16) for hk in range(N_KV_HEADS)]
        s_prev = [_dot_nt(qgs[hk], kp_ref[:, kvs[hk]].astype(BF16)) for hk in range(N_KV_HEADS)]
        s_cur = [_dot_nt(qgs[hk], kc_ref[:, kvs[hk]].astype(BF16)) for hk in range(N_KV_HEADS)]
        for hk in range(N_KV_HEADS):
            pp, pc, invs = [], [], []
            for g in range(GROUP):
                h = GROUP * hk + g
                rows = slice(g * BLOCK, (g + 1) * BLOCK)
                p_prev, p_cur, inv, lse = _attn_exps(s_prev[hk][rows], s_cur[hk][rows], sink_ref[h], bias_ref[h])
                pp.append(p_prev.astype(BF16))
                pc.append(p_cur.astype(BF16))
                invs.append(inv)
                lse_ref[:, h:h + 1] = lse
            og = _dot(jnp.concatenate(pp, axis=0), vp_ref[:, kvs[hk]].astype(BF16)) + _dot(
                jnp.concatenate(pc, axis=0), vc_ref[:, kvs[hk]].astype(BF16))
            for g in range(GROUP):
                h = GROUP * hk + g
                y_ref[:, h * HEAD_DIM:(h + 1) * HEAD_DIM] = og[g * BLOCK:(g + 1) * BLOCK] * invs[g]
        ag = jnp.concatenate([ag0_ref[...], ag1_ref[...]], axis=1).astype(F32)
        z_ref[...] = (y_ref[...] * (ag * _sigmoid(ag))).astype(BF16)

    prev = lambda c: (lambda i: (jnp.maximum(i - 1, 0), c))
    cur = lambda c: (lambda i: (i, c))
    return pl.pallas_call(
        body,
        name="attn_fwd",
        grid=(nb,),
        in_specs=[
            pl.BlockSpec(memory_space=pltpu.SMEM),
            _bias_spec(),
            pl.BlockSpec((BLOCK, 1024), lambda i: (i, COL_Q // 4)),
            pl.BlockSpec((BLOCK, D_KV), prev(COL_K)),
            pl.BlockSpec((BLOCK, D_KV), cur(COL_K)),
            pl.BlockSpec((BLOCK, D_KV), prev(COL_V)),
            pl.BlockSpec((BLOCK, D_KV), cur(COL_V)),
            pl.BlockSpec((BLOCK, 512), lambda i: (i, COL_ATTN_GATE // 2)),
            pl.BlockSpec((BLOCK, 512), lambda i: (i, COL_ATTN_GATE // 2 + 1)),
        ],
        out_specs=[pl.BlockSpec((BLOCK, 1024), lambda i: (i, 0)), pl.BlockSpec((BLOCK, 1024), lambda i: (i, 0)),
                   pl.BlockSpec((BLOCK, N_Q_HEADS), lambda i: (i, 0))],
        out_shape=[_sds((T, 1024), F32), _sds((T, 1024), BF16), _sds((T, N_Q_HEADS), F32)],
        compiler_params=_params(("arbitrary",), 32),
    )(sinks, *_hbm(bias, proj, proj, proj, proj, proj, proj, proj))


def _attn_bwd(proj, y_attn, lse, dz_attn, sinks, bias, token):
    T = proj.shape[0]
    nb = T // BLOCK

    def body(sink_ref, bias_ref, q_ref, kp_ref, kc_ref, vp_ref, vc_ref, ag0_ref, ag1_ref, y_ref, lse_ref, dz_ref,
             token_ref, dq_ref, dk_ref, dv_ref, dag_ref, ds_ref, dy_s):
        i = pl.program_id(0)

        @pl.when(i == 0)
        def _():
            ds_ref[...] = jnp.zeros_like(ds_ref)

        lane = lax.broadcasted_iota(jnp.int32, (8, 128), 1)
        sub = lax.broadcasted_iota(jnp.int32, (8, 128), 0)
        ag = jnp.concatenate([ag0_ref[...], ag1_ref[...]], axis=1).astype(F32)
        dz = dz_ref[...]
        sg = _sigmoid(ag)
        dag_ref[...] = (dz * y_ref[...] * (sg * (1.0 + ag * (1.0 - sg)))).astype(BF16)
        dy_s[...] = dz * (ag * sg)
        r_cur = pl.multiple_of(i * BLOCK, BLOCK)
        r_prev = pl.multiple_of(jnp.maximum(i - 1, 0) * BLOCK, BLOCK)
        dk_cur, dv_cur, dk_prev, dv_prev = [], [], [], []
        ds_acc = jnp.zeros((8, 128), F32)
        for hk in range(N_KV_HEADS):
            ks = slice(hk * HEAD_DIM, (hk + 1) * HEAD_DIM)
            qg = (_stack_heads(q_ref, hk, F32) * ATTN_SCALE).astype(BF16)
            dog = _stack_heads(dy_s, hk, F32)
            og = _stack_heads(y_ref, hk, F32)
            dog_b = dog.astype(BF16)
            kp = kp_ref[:, ks].astype(BF16)
            kc = kc_ref[:, ks].astype(BF16)
            vp = vp_ref[:, ks].astype(BF16)
            vc = vc_ref[:, ks].astype(BF16)
            s_prev = _dot_nt(qg, kp)
            s_cur = _dot_nt(qg, kc)
            dp_prev = _dot_nt(dog_b, vp)
            dp_cur = _dot_nt(dog_b, vc)
            dvec = jnp.sum(dog * og, axis=-1, keepdims=True)
            pp, pc, dsp, dsc = [], [], [], []
            for g in range(GROUP):
                h = GROUP * hk + g
                rows = slice(g * BLOCK, (g + 1) * BLOCK)
                p_prev, p_cur, p_sink = _attn_probs(
                    s_prev[rows], s_cur[rows], sink_ref[h], bias_ref[h], lse_ref[:, h:h + 1])
                d_h = dvec[rows]
                pp.append(p_prev.astype(BF16))
                pc.append(p_cur.astype(BF16))
                dsp.append((p_prev * (dp_prev[rows] - d_h)).astype(BF16))
                dsc.append((p_cur * (dp_cur[rows] - d_h)).astype(BF16))
                dsink = -jnp.sum(p_sink * d_h, axis=0, keepdims=True)
                ds_acc = ds_acc + jnp.where(jnp.logical_and(lane == h, sub == 1), dsink, 0.0)
            pp = jnp.concatenate(pp, axis=0)
            pc = jnp.concatenate(pc, axis=0)
            dsp = jnp.concatenate(dsp, axis=0)
            dsc = jnp.concatenate(dsc, axis=0)
            dqg = (_dot(dsp, kp) + _dot(dsc, kc)) * ATTN_SCALE
            for g in range(GROUP):
                h = GROUP * hk + g
                dq_ref[:, h * HEAD_DIM:(h + 1) * HEAD_DIM] = dqg[g * BLOCK:(g + 1) * BLOCK].astype(BF16)
            dk_ref[pl.ds(r_cur, BLOCK), ks] = _dot_tn(dsc, qg)
            dv_ref[pl.ds(r_cur, BLOCK), ks] = _dot_tn(pc, dog_b)
            dk_prev.append(_dot_tn(dsp, qg))
            dv_prev.append(_dot_tn(pp, dog_b))
        ds_ref[:, 0:128] += ds_acc

        @pl.when(i > 0)
        def _():
            for hk in range(N_KV_HEADS):
                ks = slice(hk * HEAD_DIM, (hk + 1) * HEAD_DIM)
                dk_ref[pl.ds(r_prev, BLOCK), ks] += dk_prev[hk]
                dv_ref[pl.ds(r_prev, BLOCK), ks] += dv_prev[hk]

    prev = lambda c: (lambda i: (jnp.maximum(i - 1, 0), c))
    cur = lambda c: (lambda i: (i, c))
    blk = pl.BlockSpec((BLOCK, 1024), lambda i: (i, 0))
    whole = pl.BlockSpec((T, D_KV), lambda i: (0, 0))
    return pl.pallas_call(
        body,
        name="attn_bwd",
        grid=(nb,),
        in_specs=[
            pl.BlockSpec(memory_space=pltpu.SMEM),
            _bias_spec(),
            pl.BlockSpec((BLOCK, 1024), lambda i: (i, COL_Q // 4)),
            pl.BlockSpec((BLOCK, D_KV), prev(COL_K)),
            pl.BlockSpec((BLOCK, D_KV), cur(COL_K)),
            pl.BlockSpec((BLOCK, D_KV), prev(COL_V)),
            pl.BlockSpec((BLOCK, D_KV), cur(COL_V)),
            pl.BlockSpec((BLOCK, 512), lambda i: (i, COL_ATTN_GATE // 2)),
            pl.BlockSpec((BLOCK, 512), lambda i: (i, COL_ATTN_GATE // 2 + 1)),
            blk,
            pl.BlockSpec((BLOCK, N_Q_HEADS), lambda i: (i, 0)),
            blk,
            pl.BlockSpec((8, 128), lambda i: (0, 0)),
        ],
        out_specs=[blk, whole, whole, blk, pl.BlockSpec((8, 1024), lambda i: (0, 0))],
        out_shape=[_sds((T, 1024), BF16), _sds((T, D_KV), F32), _sds((T, D_KV), F32), _sds((T, 1024), BF16),
                   _sds((8, 1024), F32)],
        scratch_shapes=[pltpu.VMEM((BLOCK, 1024), F32)],
        compiler_params=_params(("arbitrary",), 48),
    )(sinks, *_hbm(bias, proj, proj, proj, proj, proj, proj, proj, y_attn, lse, dz_attn, token))


def _head(x, target, z_rnn, z_attn, proj, b_gate, g_post, w_rnn_out, w_attn_out, w_out):
    T = x.shape[0]
    tm = 256

    def body(x_ref, t_ref, zr_ref, za_ref, ml0_ref, ml1_ref, ml2_ref, ml3_ref, bg_ref, gp_ref, wr_ref, wa_ref, wo_ref,
             dyx_ref, dzr_ref, dza_ref, dml_ref, dout_ref, dbr_ref, dba_ref, mt_ref, zat_ref, sm_ref):
        @pl.when(pl.program_id(0) == 0)
        def _():
            sm_ref[...] = jnp.zeros_like(sm_ref)

        wr, wa, wo = wr_ref[...], wa_ref[...], wo_ref[...]
        br_rnn = _dot(zr_ref[...], wr)
        br_attn = _dot(za_ref[...], wa)
        zat_ref[...] = za_ref[...].astype(F32).T.astype(BF16)
        ml_rnn = jnp.concatenate([ml0_ref[...], ml1_ref[...]], axis=1).astype(F32)
        ml_attn = jnp.concatenate([ml2_ref[...], ml3_ref[...]], axis=1).astype(F32)
        g_rnn = _sigmoid(ml_rnn + bg_ref[:, 0:D_MODEL])
        g_attn = _sigmoid(ml_attn + bg_ref[:, D_MODEL:2 * D_MODEL])
        merged = g_rnn * br_rnn + g_attn * br_attn
        mb = merged.astype(BF16)
        mt_ref[...] = merged.T.astype(BF16)
        out = _dot(mb, wo)
        rstd = lax.rsqrt(jnp.mean(out * out, axis=-1, keepdims=True) + EPS)
        n = out * rstd
        gp = gp_ref[...]
        err = (x_ref[...] + n * gp) - t_ref[...]
        sm_ref[pl.ds(3, 1), :] += 0.5 * jnp.sum(jnp.mean(err * err, axis=-1, keepdims=True), axis=0, keepdims=True)
        dy = err * (1.0 / D_MODEL)
        dyx_ref[...] = dy
        sm_ref[pl.ds(0, 1), :] += jnp.sum(dy * n, axis=0, keepdims=True)
        dn = dy * gp
        dout = (rstd * (dn - n * jnp.mean(dn * n, axis=-1, keepdims=True))).astype(BF16)
        dout_ref[...] = dout
        dmerged = _dot_nt(dout, wo)
        dml_r = (dmerged * br_rnn) * (g_rnn * (1.0 - g_rnn))
        dml_a = (dmerged * br_attn) * (g_attn * (1.0 - g_attn))
        dml_ref[:, 0:D_MODEL] = dml_r.astype(BF16)
        dml_ref[:, D_MODEL:2 * D_MODEL] = dml_a.astype(BF16)
        sm_ref[pl.ds(1, 1), :] += jnp.sum(dml_r, axis=0, keepdims=True)
        sm_ref[pl.ds(2, 1), :] += jnp.sum(dml_a, axis=0, keepdims=True)
        dbr = (dmerged * g_rnn).astype(BF16)
        dba = (dmerged * g_attn).astype(BF16)
        dbr_ref[...] = dbr
        dba_ref[...] = dba
        dzr_ref[...] = _dot_nt(dbr, wr)
        dza_ref[...] = _dot_nt(dba, wa)

    tile = pl.BlockSpec((tm, D_MODEL), lambda i: (i, 0))
    wspec = pl.BlockSpec((D_MODEL, D_MODEL), lambda i: (0, 0))
    ml = lambda q: pl.BlockSpec((tm, 512), lambda i: (i, COL_MERGE // 2 + q))
    return pl.pallas_call(
        body,
        name="head",
        grid=(T // tm,),
        in_specs=[
            tile, tile, tile, tile,
            ml(0), ml(1), ml(2), ml(3),
            pl.BlockSpec((1, 2 * D_MODEL), lambda i: (0, 0)),
            pl.BlockSpec((1, D_MODEL), lambda i: (0, 0)),
            wspec, wspec, wspec,
        ],
        out_specs=[
            tile, tile, tile,
            pl.BlockSpec((tm, 2 * D_MODEL), lambda i: (i, 0)),
            tile, tile, tile,
            pl.BlockSpec((D_MODEL, tm), lambda i: (0, i)), pl.BlockSpec((D_MODEL, tm), lambda i: (0, i)),
            pl.BlockSpec((8, D_MODEL), lambda i: (0, 0)),
        ],
        out_shape=[
            _sds((T, D_MODEL), F32), _sds((T, D_MODEL), F32), _sds((T, D_MODEL), F32),
            _sds((T, 2 * D_MODEL), BF16),
            _sds((T, D_MODEL), BF16), _sds((T, D_MODEL), BF16), _sds((T, D_MODEL), BF16),
            _sds((D_MODEL, T), BF16), _sds((D_MODEL, T), BF16),
            _sds((8, D_MODEL), F32),
        ],
        compiler_params=_params(("arbitrary",), 56),
    )(*_hbm(x, target, z_rnn, z_attn, proj, proj, proj, proj, b_gate, g_post, w_rnn_out, w_attn_out, w_out))


def _matmul_t(at, b, name):
    M, T = at.shape
    N = b.shape[1]
    tk = min(1024, T)
    nt = T // tk

    def body(a_ref, b_ref, o_ref, ob_ref):
        @pl.when(pl.program_id(0) == 0)
        def _():
            o_ref[...] = jnp.zeros_like(o_ref)

        o_ref[...] += _dot(a_ref[...], b_ref[...])

        @pl.when(pl.program_id(0) == nt - 1)
        def _():
            ob_ref[...] = o_ref[...].astype(BF16)

    whole = pl.BlockSpec((M, N), lambda t: (0, 0))
    return pl.pallas_call(
        body,
        name=name,
        grid=(nt,),
        in_specs=[pl.BlockSpec((M, tk), lambda t: (0, t)), pl.BlockSpec((tk, N), lambda t: (t, 0))],
        out_specs=[whole, whole],
        out_shape=[_sds((M, N), F32), _sds((M, N), BF16)],
        compiler_params=_params(("arbitrary",), 48),
    )(*_hbm(at, b))


DPROJ_WIDTHS = (D_RNN, D_RNN, 1024, D_KV, D_KV, 1024, 2 * D_MODEL)


def _dproj_segments():
    segs, start = [[] for _ in range(N_CHIPS)], 0
    for p, width in enumerate(DPROJ_WIDTHS):
        for c in range(N_CHIPS):
            lo, hi = max(start, c * W_IN_SHARD), min(start + width, (c + 1) * W_IN_SHARD)
            if lo < hi:
                segs[c].append((p, lo - start, hi - start, lo - c * W_IN_SHARD, hi - c * W_IN_SHARD))
        start += width
    return segs


def _dh_bwd(pieces, w_in_g, x, dyx, g_pre, token):
    T = x.shape[0]
    tm = min(512, T)
    n = len(pieces)
    segs = _dproj_segments()

    def body(*refs):
        p_refs, w_hbm, x_ref, dyx_ref, g_ref = refs[0:n], refs[n], refs[n + 1], refs[n + 2], refs[n + 3]
        gx_ref, dg_ref, w_ref, w_sems = refs[n + 5], refs[n + 6], refs[n + 7], refs[n + 8]
        first = pl.program_id(0) == 0
        w_copies = [pltpu.make_async_copy(w_hbm.at[c], w_ref.at[c], w_sems.at[c]) for c in range(N_CHIPS)]

        @pl.when(first)
        def _():
            for cp in w_copies:
                cp.start()
            dg_ref[...] = jnp.zeros_like(dg_ref)

        dh = None
        for c in range(N_CHIPS):
            pl.when(first)(w_copies[c].wait)
            for p, a0, a1, u0, u1 in segs[c]:
                part = _dot_nt(p_refs[p][:, a0:a1].astype(BF16), w_ref[c, :, u0:u1])
                dh = part if dh is None else dh + part
        xv = x_ref[...]
        rstd = lax.rsqrt(jnp.mean(xv * xv, axis=-1, keepdims=True) + EPS)
        nx = xv * rstd
        dhg = dh * g_ref[...]
        gx_ref[...] = dyx_ref[...] + rstd * (dhg - nx * jnp.mean(dhg * nx, axis=-1, keepdims=True))
        dg_ref[pl.ds(0, 1), :] += jnp.sum(dh * nx, axis=0, keepdims=True)

    tile = pl.BlockSpec((tm, D_MODEL), lambda i: (i, 0))
    return pl.pallas_call(
        body,
        name="dh_bwd",
        grid=(T // tm,),
        in_specs=[pl.BlockSpec((tm, w), lambda i: (i, 0)) for w in DPROJ_WIDTHS] + [
            ANY, tile, tile,
            pl.BlockSpec((1, D_MODEL), lambda i: (0, 0)),
            pl.BlockSpec((8, 128), lambda i: (0, 0)),
        ],
        out_specs=[tile, pl.BlockSpec((8, D_MODEL), lambda i: (0, 0))],
        out_shape=[_sds((T, D_MODEL), F32), _sds((8, D_MODEL), F32)],
        scratch_shapes=[pltpu.VMEM(w_in_g.shape, BF16), pltpu.SemaphoreType.DMA((N_CHIPS,))],
        compiler_params=_params(("arbitrary",), 56),
    )(*_hbm(*pieces, w_in_g, x, dyx, g_pre, token))


def _dw_in(ht, pieces):
    T = ht.shape[1]
    tk = min(1024, T)
    nt = T // tk
    n = len(pieces)
    segs = _dproj_segments()

    def body(*refs):
        h_ref, p_refs, o_ref, ob_ref = refs[0], refs[1:n + 1], refs[n + 1], refs[n + 2]

        @pl.when(pl.program_id(1) == 0)
        def _():
            o_ref[...] = jnp.zeros_like(o_ref)

        for c in range(N_CHIPS):
            @pl.when(pl.program_id(0) == c)
            def _():
                for p, a0, a1, u0, u1 in segs[c]:
                    o_ref[:, u0:u1] += _dot(h_ref[...], p_refs[p][:, a0:a1].astype(BF16))

        @pl.when(pl.program_id(1) == nt - 1)
        def _():
            ob_ref[...] = o_ref[...].astype(BF16)

    def piece_spec(p):
        chips = [c for c in range(N_CHIPS) if any(s[0] == p for s in segs[c])]

        def index(c, t):
            used = functools.reduce(jnp.logical_or, [c == k for k in chips])
            return (jnp.where(used, t, 0), 0)

        return pl.BlockSpec((tk, DPROJ_WIDTHS[p]), index)

    return pl.pallas_call(
        body,
        name="dw_in",
        grid=(N_CHIPS, nt),
        in_specs=[pl.BlockSpec((D_MODEL, tk), lambda c, t: (0, t))] + [piece_spec(p) for p in range(n)],
        out_specs=[pl.BlockSpec((None, D_MODEL, W_IN_SHARD), lambda c, t: (c, 0, 0))] * 2,
        out_shape=[_sds((N_CHIPS, D_MODEL, W_IN_SHARD), F32), _sds((N_CHIPS, D_MODEL, W_IN_SHARD), BF16)],
        compiler_params=_params(("parallel", "arbitrary"), 56),
    )(*_hbm(ht, *pieces))


ELEMENTWISE_TILE_BYTES = MIB


def _row_tile(rows, cols, limit=ELEMENTWISE_TILE_BYTES):
    if rows * cols * 4 <= limit:
        return rows
    for t in (512, 256, 128, 64, 32, 16, 8):
        if rows % t == 0 and t * cols * 4 <= limit:
            return t
    return rows


def _chip_sum(ps, gots, chip_core, name):
    n = len(ps)
    h, C = ps[0].shape
    tr = _row_tile(h, C * n)
    nt = h // tr

    def body(jc_ref, *refs):
        for a in range(n):
            p_ref, g0_ref, g1_ref, g2_ref, o_ref = refs[a], refs[n + 3 * a], refs[n + 3 * a + 1], refs[n + 3 * a + 2], \
                refs[4 * n + a]
            o_ref[...] = ((p_ref[...] + g0_ref[...].astype(F32)) + g1_ref[...].astype(F32)) + g2_ref[...].astype(F32)

    rel = lambda r: pl.BlockSpec((None, tr, C), lambda i, jc_ref: (r, i, 0))
    outs = pl.pallas_call(
        body,
        name=name,
        grid_spec=pltpu.PrefetchScalarGridSpec(
            num_scalar_prefetch=1,
            grid=(nt,),
            in_specs=[pl.BlockSpec((tr, C), lambda i, jc_ref: (i, 0))] * n + [rel(0), rel(1), rel(2)] * n,
            out_specs=[pl.BlockSpec((tr, C), lambda i, jc_ref: (jc_ref[1] * nt + i, 0))] * n,
        ),
        out_shape=[_sds((2 * h, C), F32)] * n,
        compiler_params=_params(("parallel",), 48),
    )(chip_core, *_hbm(*ps, *[g for got in gots for g in (got, got, got)]))
    return list(outs)


def _place_shards(shards, chip, name):
    n = len(shards)
    tiles = [_row_tile(s.shape[0], s.shape[1]) for s in shards]
    steps = max(s.shape[0] // t for s, t in zip(shards, tiles))
    tiles = [s.shape[0] // steps for s in shards]

    def body(j_ref, *refs):
        for a in range(n):
            refs[n + a][...] = refs[a][...].astype(BF16)

    return pl.pallas_call(
        body,
        name=name,
        grid_spec=pltpu.PrefetchScalarGridSpec(
            num_scalar_prefetch=1,
            grid=(steps,),
            in_specs=[pl.BlockSpec((t, s.shape[1]), lambda i, j_ref: (i, 0)) for s, t in zip(shards, tiles)],
            out_specs=[pl.BlockSpec((None, t, s.shape[1]), lambda i, j_ref: (j_ref[0], i, 0))
                       for s, t in zip(shards, tiles)],
        ),
        out_shape=[_sds((N_CHIPS,) + s.shape, BF16) for s in shards],
        compiler_params=_params(("parallel",), 48),
    )(chip, *_hbm(*shards))


def _adamw_update(w, g, m, v):
    c1 = 1.0 - ADAM_B1 ** ADAM_STEP
    c2 = 1.0 - ADAM_B2 ** ADAM_STEP
    nm = ADAM_B1 * m + (1.0 - ADAM_B1) * g
    nv = ADAM_B2 * v + (1.0 - ADAM_B2) * (g * g)
    return (-ADAM_LR) * ((nm / c1) / (jnp.sqrt(nv / c2) + ADAM_EPS) + ADAM_WD * w), nm, nv


def _adamw(params, name):
    n = len(params)
    R, C = params[0][0].shape
    tr = _row_tile(R, C * n)

    def body(*refs):
        for a in range(n):
            w_ref, g_ref, m_ref, v_ref = refs[4 * a:4 * a + 4]
            d_ref, nm_ref, nv_ref, go_ref = refs[4 * n + 4 * a:4 * n + 4 * a + 4]
            g = g_ref[...]
            d_ref[...], nm_ref[...], nv_ref[...] = _adamw_update(w_ref[...], g, m_ref[...], v_ref[...])
            go_ref[...] = g

    spec = pl.BlockSpec((tr, C), lambda i: (i, 0))
    outs = pl.pallas_call(
        body, name=name, grid=(R // tr,), in_specs=[spec] * (4 * n), out_specs=[spec] * (4 * n),
        out_shape=[_sds((R, C), F32)] * (4 * n), compiler_params=_params(("parallel",), 48),
    )(*_hbm(*[t for p in params for t in p]))
    return [tuple(outs[4 * a:4 * a + 4]) for a in range(n)]


def _adamw_whole(params, name):
    n = len(params)

    def body(*refs):
        for a in range(n):
            w_ref, g_ref, m_ref, v_ref = refs[4 * a:4 * a + 4]
            d_ref, nm_ref, nv_ref = refs[4 * n + 3 * a:4 * n + 3 * a + 3]
            d_ref[...], nm_ref[...], nv_ref[...] = _adamw_update(w_ref[...], g_ref[...], m_ref[...], v_ref[...])

    def whole(t):
        return pl.BlockSpec(t.shape, lambda i: (0,) * t.ndim)

    flat = [t for p in params for t in p]
    like = [p[0] for p in params for _ in range(3)]
    outs = pl.pallas_call(
        body, name=name, grid=(1,), in_specs=[whole(t) for t in flat], out_specs=[whole(t) for t in like],
        out_shape=[_sds(t.shape, F32) for t in like], compiler_params=_params(("arbitrary",), 48),
    )(*_hbm(*flat))
    return [tuple(outs[3 * a:3 * a + 3]) for a in range(n)]


def _place():
    return lax.axis_index("x"), lax.axis_index("y"), lax.axis_index("c")


def _chip_of(x, y, r):
    return (x ^ (r >> 1), y ^ (r & 1))


ANY = pl.BlockSpec(memory_space=pl.ANY)


def _gather_weights(placed, cw8):
    nbig = len(placed)
    halves = [s.shape[1] // 2 for s in placed]
    pieces = [max(1, h // 64) for h in halves]
    rows = [h // p for h, p in zip(halves, pieces)]
    order = [(a, q) for q in range(max(pieces)) for a in range(nbig) if q < pieces[a]]
    ici_sem = {(a, q, r): 3 * i + (r - 1) for i, (a, q) in enumerate(order) for r in (1, 2, 3)}
    cw_sem = {r: 3 * len(order) + (r - 1) for r in (1, 2, 3)}
    d2d_sem = {key: 3 * len(order) + 3 + k for key, k in ici_sem.items()}
    nsem = 6 * len(order) + 3

    def body(*refs):
        cw_ref, dsts, gcw_ref = refs[nbig], refs[nbig + 1:2 * nbig + 1], refs[2 * nbig + 1]
        send_sems, recv_sems = refs[2 * nbig + 2:]
        x, y, c = _place()
        j = 2 * x + y

        def piece_rows(a, q, core):
            return pl.ds(pl.multiple_of(core * halves[a] + q * rows[a], 16), rows[a])

        def ici(a, q, r):
            tx, ty = _chip_of(x, y, r)
            k = ici_sem[(a, q, r)]
            region = dsts[a].at[j, piece_rows(a, q, c), :]
            return pltpu.make_async_remote_copy(
                src_ref=region, dst_ref=region, send_sem=send_sems.at[k], recv_sem=recv_sems.at[k],
                device_id=(tx, ty, c), device_id_type=MESH)

        def ici_landed(a, q, r):
            tx, ty = _chip_of(x, y, r)
            k = ici_sem[(a, q, r)]
            region = dsts[a].at[2 * tx + ty, piece_rows(a, q, c), :]
            return pltpu.make_async_remote_copy(
                src_ref=region, dst_ref=region, send_sem=send_sems.at[k], recv_sem=recv_sems.at[k],
                device_id=(tx, ty, c), device_id_type=MESH)

        def d2d(a, q, r, core):
            tx, ty = _chip_of(x, y, r)
            k = d2d_sem[(a, q, r)]
            region = dsts[a].at[2 * tx + ty, piece_rows(a, q, core), :]
            return pltpu.make_async_remote_copy(
                src_ref=region, dst_ref=region, send_sem=send_sems.at[k], recv_sem=recv_sems.at[k],
                device_id=(x, y, 1 - c), device_id_type=MESH)

        def cw_copy(r):
            tx, ty = _chip_of(x, y, r)
            k = cw_sem[r]
            return pltpu.make_async_remote_copy(
                src_ref=cw_ref, dst_ref=gcw_ref.at[j], send_sem=send_sems.at[k], recv_sem=recv_sems.at[k],
                device_id=(tx, ty, c), device_id_type=MESH)

        def cw_landed(r):
            tx, ty = _chip_of(x, y, r)
            k = cw_sem[r]
            region = gcw_ref.at[2 * tx + ty]
            return pltpu.make_async_remote_copy(
                src_ref=region, dst_ref=region, send_sem=send_sems.at[k], recv_sem=recv_sems.at[k],
                device_id=(tx, ty, c), device_id_type=MESH)

        def relay(a, q, origin, to):
            ox, oy = _chip_of(x, y, origin)
            tx, ty = _chip_of(x, y, to)
            k = ici_sem[(a, q, 3)]
            region = dsts[a].at[2 * ox + oy, piece_rows(a, q, c), :]
            return pltpu.make_async_remote_copy(
                src_ref=region, dst_ref=region, send_sem=send_sems.at[k], recv_sem=recv_sems.at[k],
                device_id=(tx, ty, c), device_id_type=MESH)

        first = [ici(a, q, r) for (a, q) in order for r in (1, 2)] + [cw_copy(r) for r in (1, 2, 3)]
        for cp in first:
            cp.start()
        passed = []
        for (a, q) in order:
            for r in (1, 2):
                ici_landed(a, q, r).wait_recv()
                if q % 2 == r - 1:
                    cp = relay(a, q, r, 3 - r)
                    cp.start()
                    passed.append(cp)
                cp = d2d(a, q, r, c)
                cp.start()
                passed.append(cp)
        for (a, q) in order:
            ici_landed(a, q, 3).wait_recv()
            cp = d2d(a, q, 3, c)
            cp.start()
            passed.append(cp)
        for r in (1, 2, 3):
            cw_landed(r).wait_recv()
        for (a, q) in order:
            for r in (1, 2, 3):
                d2d(a, q, r, 1 - c).wait_recv()
        for cp in first + passed:
            cp.wait_send()

    return pl.pallas_call(
        body,
        name="gather_weights",
        in_specs=[ANY] * (nbig + 1),
        out_specs=[ANY] * (nbig + 1),
        out_shape=[_sds(s.shape, s.dtype) for s in placed] + [_sds((N_CHIPS,) + cw8.shape, cw8.dtype)],
        input_output_aliases={a: a for a in range(nbig)},
        scratch_shapes=[pltpu.SemaphoreType.DMA((nsem,)), pltpu.SemaphoreType.DMA((nsem,))],
    )(*placed, cw8)


def _gather_late_start(placed, after, name):
    n = len(placed)
    halves = [s.shape[1] // 2 for s in placed]

    def body(*refs):
        g_refs = refs[0:n]
        send_sems, recv_sems, token = refs[n + 1], refs[n + 2], refs[-1]
        x, y, c = _place()
        j = 2 * x + y
        for a in range(n):
            mine = g_refs[a].at[j, pl.ds(pl.multiple_of(c * halves[a], 16), halves[a]), :]
            for r in (1, 2, 3):
                tx, ty = _chip_of(x, y, r)
                for to_core in (0, 1):
                    k = ((a * 3 + (r - 1)) * 2 + c) * 2 + to_core
                    pltpu.make_async_remote_copy(
                        src_ref=mine, dst_ref=mine, send_sem=send_sems.at[k], recv_sem=recv_sems.at[k],
                        device_id=(tx, ty, to_core), device_id_type=MESH).start()
        token[...] = jnp.zeros_like(token)

    hbm = lambda t: pltpu.HBM(t.shape, t.dtype)
    keep = lambda t: pltpu.with_memory_space_constraint(t, pltpu.HBM)
    nsem = 12 * n
    outs = pl.pallas_call(
        body,
        name=name,
        in_specs=[HBM] * n + [ANY],
        out_specs=(SEM, SEM, *[HBM] * n, pl.BlockSpec(memory_space=pltpu.VMEM)),
        out_shape=(pltpu.SemaphoreType.DMA((nsem,)), pltpu.SemaphoreType.DMA((nsem,)), *[hbm(p) for p in placed],
                   jax.ShapeDtypeStruct((8, 128), F32)),
        input_output_aliases={i: 2 + i for i in range(n)},
        compiler_params=pltpu.CompilerParams(has_side_effects=DATAFLOW),
    )(*[keep(p) for p in placed], after)
    return outs[0], outs[1], list(outs[2:2 + n]), outs[-1]


def _gather_late_wait(send_sems, recv_sems, thru, after, name):
    n = len(thru)
    halves = [s.shape[1] // 2 for s in thru]

    def body(*refs):
        g_refs = refs[0:n]
        send_sems, recv_sems = refs[n], refs[n + 1]
        x, y, c = _place()
        j = 2 * x + y
        for a in range(n):
            mine = g_refs[a].at[j, pl.ds(pl.multiple_of(c * halves[a], 16), halves[a]), :]
            for r in (1, 2, 3):
                tx, ty = _chip_of(x, y, r)
                for other in (0, 1):
                    k_out = ((a * 3 + (r - 1)) * 2 + c) * 2 + other
                    pltpu.make_async_remote_copy(
                        src_ref=mine, dst_ref=mine, send_sem=send_sems.at[k_out], recv_sem=recv_sems.at[k_out],
                        device_id=(tx, ty, other), device_id_type=MESH).wait_send()
                    k_in = ((a * 3 + (r - 1)) * 2 + other) * 2 + c
                    theirs = g_refs[a].at[2 * tx + ty, pl.ds(other * halves[a], halves[a]), :]
                    pltpu.make_async_remote_copy(
                        src_ref=theirs, dst_ref=theirs, send_sem=send_sems.at[k_in], recv_sem=recv_sems.at[k_in],
                        device_id=(tx, ty, other), device_id_type=MESH).wait_recv()

    hbm = lambda t: pltpu.HBM(t.shape, t.dtype)
    outs = pl.pallas_call(
        body,
        name=name,
        in_specs=[HBM] * n + [SEM, SEM, ANY],
        out_specs=[HBM] * n,
        out_shape=[hbm(t) for t in thru],
        input_output_aliases={i: i for i in range(n)},
        compiler_params=pltpu.CompilerParams(has_side_effects=DATAFLOW),
    )(*thru, send_sems, recv_sems, after)
    return list(outs)


D2D_PIECE_ROWS = 64
PAIR_SUM_TILE_BYTES = 2 * MIB


def _pair_sum(gs, gbs, chip_core, name):
    n = len(gs)
    nch, R, C = gs[0].shape
    h = R // 2
    tr = _row_tile(h, C * n, PAIR_SUM_TILE_BYTES)
    nt = h // tr
    rows = min(D2D_PIECE_ROWS, tr)

    def body(jc_ref, *refs):
        g_refs, gb_refs, p_refs, pb_refs = refs[0:n], refs[n:2 * n], refs[2 * n:3 * n], refs[3 * n:4 * n]
        got_refs, send_sems, recv_sems = refs[4 * n:5 * n], refs[5 * n], refs[5 * n + 1]
        i, j = pl.program_id(0), pl.program_id(1)
        x, y, c = _place()

        def copy(a, ti, tj, first, count):
            src_rows = pl.ds(pl.multiple_of((1 - c) * h + ti * tr + first, 16), count)
            dst_rows = pl.ds(pl.multiple_of(ti * tr + first, 16), count)
            return pltpu.make_async_remote_copy(
                src_ref=gb_refs[a].at[tj, src_rows, :], dst_ref=got_refs[a].at[tj, dst_rows, :],
                send_sem=send_sems.at[a, ti, tj], recv_sem=recv_sems.at[a, ti, tj],
                device_id=(x, y, 1 - c), device_id_type=MESH)

        @pl.when((i == 0) & (j == 0))
        def _():
            for ti in range(nt):
                for tj in range(nch):
                    for a in range(n):
                        for q in range(tr // rows):
                            copy(a, ti, tj, q * rows, rows).start()

        for a in range(n):
            copy(a, i, j, 0, tr).wait()
            s = g_refs[a][...] + got_refs[a][j, pl.ds(pl.multiple_of(i * tr, 16), tr), :].astype(F32)
            pb_refs[a][...] = s.astype(BF16)

            @pl.when(j == jc_ref[0])
            def _():
                p_refs[a][...] = s

    by_chip = pl.BlockSpec((None, tr, C), lambda i, j, jc_ref: (j, i, 0))
    outs = pl.pallas_call(
        body,
        name=name,
        grid_spec=pltpu.PrefetchScalarGridSpec(
            num_scalar_prefetch=1,
            grid=(nt, nch),
            in_specs=[pl.BlockSpec((None, tr, C), lambda i, j, jc_ref: (j, jc_ref[1] * nt + i, 0))] * n + [ANY] * n,
            out_specs=[pl.BlockSpec((tr, C), lambda i, j, jc_ref: (i, 0))] * n + [by_chip] * n,
            scratch_shapes=[pltpu.VMEM((nch, h, C), BF16)] * n + [pltpu.SemaphoreType.DMA((n, nt, nch))] * 2,
        ),
        out_shape=[_sds((h, C), F32)] * n + [_sds((nch, h, C), BF16)] * n,
        compiler_params=_params(("arbitrary", "arbitrary"), 48),
    )(chip_core, *_hbm(*gs, *gbs))
    return list(outs[:n]), list(outs[n:])


HBM = pl.BlockSpec(memory_space=pltpu.HBM)
SEM = pl.BlockSpec(memory_space=pltpu.SEMAPHORE)
DATAFLOW = pltpu.SideEffectType.DATAFLOW_SIDE_EFFECTING


def _chip_copy(p_refs, land_refs, send_sems, recv_sems, a, r, blocked):
    x, y, c = _place()
    tx, ty = _chip_of(x, y, r)
    k = a * 3 + (r - 1)
    return pltpu.make_async_remote_copy(
        src_ref=p_refs[a].at[2 * tx + ty] if blocked else p_refs[a], dst_ref=land_refs[a].at[r - 1],
        send_sem=send_sems.at[k], recv_sem=recv_sems.at[k], device_id=(tx, ty, c), device_id_type=MESH)


def _chip_exchange_start(psums, name, blocked=True):
    n = len(psums)
    lands = [lax.empty((3,) + (p.shape[1:] if blocked else p.shape), p.dtype) for p in psums]

    def body(*refs):
        p_refs, land_refs = refs[0:n], refs[n:2 * n]
        send_sems, recv_sems, token = refs[2 * n], refs[2 * n + 1], refs[-1]
        for a in range(n):
            for r in (1, 2, 3):
                _chip_copy(p_refs, land_refs, send_sems, recv_sems, a, r, blocked).start()
        token[...] = jnp.zeros_like(token)

    hbm = lambda t: pltpu.HBM(t.shape, t.dtype)
    keep = lambda t: pltpu.with_memory_space_constraint(t, pltpu.HBM)
    outs = pl.pallas_call(
        body,
        name=name,
        in_specs=[HBM] * (2 * n),
        out_specs=(SEM, SEM, *[HBM] * (2 * n), pl.BlockSpec(memory_space=pltpu.VMEM)),
        out_shape=(pltpu.SemaphoreType.DMA((3 * n,)), pltpu.SemaphoreType.DMA((3 * n,)),
                   *[hbm(p) for p in psums], *[hbm(l) for l in lands], _sds((8, 128), F32)),
        input_output_aliases={i: 2 + i for i in range(2 * n)},
        compiler_params=pltpu.CompilerParams(has_side_effects=DATAFLOW),
    )(*[keep(p) for p in psums], *[keep(l) for l in lands])
    return outs[0], outs[1], list(outs[2:2 + n]), list(outs[2 + n:2 + 2 * n]), outs[-1]


def _chip_exchange_wait(send_sems, recv_sems, p_thru, land_thru, after, name, blocked=True):
    n = len(p_thru)

    def body(*refs):
        p_refs, land_refs = refs[0:n], refs[n:2 * n]
        send_sems, recv_sems = refs[2 * n], refs[2 * n + 1]
        for a in range(n):
            for r in (1, 2, 3):
                copy = _chip_copy(p_refs, land_refs, send_sems, recv_sems, a, r, blocked)
                copy.wait_send()
                copy.wait_recv()

    hbm = lambda t: pltpu.HBM(t.shape, t.dtype)
    outs = pl.pallas_call(
        body,
        name=name,
        in_specs=[HBM] * (2 * n) + [SEM, SEM, ANY],
        out_specs=[HBM] * (2 * n),
        out_shape=[hbm(p) for p in p_thru] + [hbm(l) for l in land_thru],
        input_output_aliases={i: i for i in range(2 * n)},
        compiler_params=pltpu.CompilerParams(has_side_effects=DATAFLOW),
    )(*p_thru, *land_thru, send_sems, recv_sems, after)
    return list(outs[0:n]), list(outs[n:2 * n])


def _pair_share(fulls):
    n = len(fulls)
    halves = [f.shape[0] // 2 for f in fulls]

    def body(*refs):
        full_refs = refs[n:2 * n]
        send_sems, recv_sems = refs[2 * n:]
        x, y, c = _place()

        def half_of(a, core):
            return full_refs[a].at[pl.ds(pl.multiple_of(core * halves[a], 8), halves[a]), :]

        def remote(a, src, dst):
            return pltpu.make_async_remote_copy(
                src_ref=src, dst_ref=dst, send_sem=send_sems.at[a], recv_sem=recv_sems.at[a],
                device_id=(x, y, 1 - c), device_id_type=MESH)

        for a in range(n):
            for q in range(halves[a] // D2D_PIECE_ROWS):
                piece = full_refs[a].at[
                    pl.ds(pl.multiple_of(c * halves[a] + q * D2D_PIECE_ROWS, 8), D2D_PIECE_ROWS), :]
                remote(a, piece, piece).start()
        for a in range(n):
            remote(a, half_of(a, c), half_of(a, c)).wait_send()
            remote(a, half_of(a, 1 - c), half_of(a, 1 - c)).wait_recv()

    return pl.pallas_call(
        body,
        name="pair_share",
        in_specs=[ANY] * n,
        out_specs=[ANY] * n,
        out_shape=[_sds(f.shape, F32) for f in fulls],
        input_output_aliases={a: a for a in range(n)},
        scratch_shapes=[pltpu.SemaphoreType.DMA((n,)), pltpu.SemaphoreType.DMA((n,))],
    )(*fulls)


def _small_pair_sum(s):
    R, C = s.shape
    V = SMALL_VECTOR_ROWS

    def body(s_ref, v_ref, m_ref, sib, send_sem, recv_sem):
        x, y, c = _place()

        def to_sib(src, dst):
            return pltpu.make_async_remote_copy(
                src_ref=src, dst_ref=dst, send_sem=send_sem, recv_sem=recv_sem,
                device_id=(x, y, 1 - c), device_id_type=MESH)

        for q in range(R // 8):
            to_sib(s_ref.at[pl.ds(8 * q, 8), :], sib.at[pl.ds(8 * q, 8), :]).start()
        to_sib(s_ref, sib).wait()
        v_ref[...] = s_ref[pl.ds(0, V), :] + sib[pl.ds(0, V), :]
        m_ref[...] = (s_ref[pl.ds(V, R - V), :] + sib[pl.ds(V, R - V), :]).astype(BF16)

    return pl.pallas_call(
        body,
        name="small_pair_sum",
        in_specs=[pl.BlockSpec(memory_space=pltpu.VMEM)],
        out_specs=[pl.BlockSpec(memory_space=pltpu.VMEM)] * 2,
        out_shape=[jax.ShapeDtypeStruct((V, C), F32), jax.ShapeDtypeStruct((R - V, C), BF16)],
        scratch_shapes=[pltpu.VMEM((R, C), F32), pltpu.SemaphoreType.DMA, pltpu.SemaphoreType.DMA],
    )(s)


def _small_total(chip, own, landed):
    V, C = own[0].shape
    M = own[1].shape[0]

    def body(j_ref, v_ref, m_ref, lv_ref, lm_ref, o_ref, chips_v, chips_m):
        j = j_ref[0]
        chips_v[j] = v_ref[...]
        chips_m[j] = m_ref[...]
        for r in (1, 2, 3):
            chips_v[j ^ r] = lv_ref[r - 1]
            chips_m[j ^ r] = lm_ref[r - 1]
        o_ref[pl.ds(0, V), :] = (chips_v[0] + chips_v[1]) + (chips_v[2] + chips_v[3])
        o_ref[pl.ds(V, M), :] = (chips_m[0].astype(F32) + chips_m[1].astype(F32)) + (
            chips_m[2].astype(F32) + chips_m[3].astype(F32))

    vmem = pl.BlockSpec(memory_space=pltpu.VMEM)
    return pl.pallas_call(
        body,
        name="small_total",
        in_specs=[pl.BlockSpec(memory_space=pltpu.SMEM), vmem, vmem, vmem, vmem],
        out_specs=vmem,
        out_shape=jax.ShapeDtypeStruct((V + M, C), F32),
        scratch_shapes=[pltpu.VMEM((N_CHIPS, V, C), F32), pltpu.VMEM((N_CHIPS, M, C), BF16)],
    )(chip, own[0], own[1], landed[0], landed[1])


def _local_grads(x, target, g_pre, w_in_g, b_gate, conv_w, conv_b, w_rg_a, b_rg_a, w_rg_x, b_rg_x, lam, sinks,
                 out_weights, fwd_token, g_post, on_out_grads, on_w_in_grad):
    b_a = b_rg_a.reshape(1, D_RNN)
    b_x = b_rg_x.reshape(1, D_RNN)

    proj, ht = _proj_fwd(x, g_pre, w_in_g)
    y_rnn, z_rnn, conv, z_rnn_t = _rnn_fwd(proj, conv_w, conv_b, w_rg_a, w_rg_x, b_a, b_x, lam, fwd_token)
    bias = _attn_bias()
    y_attn, z_attn, lse = _attn_fwd(proj, sinks, bias)
    w_rnn_out, w_attn_out, w_out = out_weights(z_attn)
    dyx, dz_rnn, dz_attn, dml, dout, dbr_rnn, dbr_attn, merged_t, z_attn_t, head_small = _head(
        x, target, z_rnn, z_attn, proj, b_gate, g_post, w_rnn_out, w_attn_out, w_out)
    out_grads = [_matmul_t(z_rnn_t, dbr_rnn, "dw_rnn_out"), _matmul_t(z_attn_t, dbr_attn, "dw_attn_out"),
                 _matmul_t(merged_t, dout, "dw_out")]
    shard_rows = lambda d: d.reshape(N_CHIPS, OUT_SHARD, D_MODEL)
    token = on_out_grads([shard_rows(g) for g, _ in out_grads], [shard_rows(gb) for _, gb in out_grads])
    dq, dk, dv, dag, attn_small = _attn_bwd(proj, y_attn, lse, dz_attn, sinks, bias, token)
    drx, drg, dwa, dwx, rnn_small = _rnn_bwd(proj, conv, y_rnn, dz_rnn, conv_w, w_rg_a, w_rg_x, b_a, b_x, lam)
    dproj = [drx, drg, dq, dk, dv, dag, dml]
    token = on_w_in_grad(*_dw_in(ht, dproj))
    grad_x, dh_small = _dh_bwd(dproj, w_in_g, x, dyx, g_pre, token)
    small = jnp.concatenate([rnn_small, head_small, dh_small + attn_small,
                             dwa.reshape(64, 1024), dwx.reshape(64, 1024)], axis=0)
    return grad_x, small


ROW_LOSS = 11


def _unpack_small(s, conv_cols):
    return {
        "b_rg_a": s[0:1].reshape(1, 16, 64), "b_rg_x": s[1:2].reshape(1, 16, 64), "lru_lambda": s[2:3],
        "conv_b": s[3:4], "conv_w": s[4:8, 0:conv_cols].reshape(1, CONV_W, conv_cols),
        "post_norm_g": s[8:9], "b_gate": s[9:11].reshape(1, 2048),
        "pre_norm_g": s[16:17], "attn_sinks": s[17:18, 0:N_Q_HEADS],
        "w_rg_a": s[24:88].reshape(1, 16, 64, 64), "w_rg_x": s[88:152].reshape(1, 16, 64, 64),
    }


WEIGHTS = ["pre_norm_g", "w_in", "b_gate", "conv_w", "conv_b", "w_rg_a", "b_rg_a", "w_rg_x", "b_rg_x", "lru_lambda",
           "attn_sinks", "w_rnn_out", "w_attn_out", "w_out", "post_norm_g"]
BIG = ["w_in", "w_rnn_out", "w_attn_out", "w_out"]


def kernel(x, pre_norm_g, w_in, b_gate, conv_w, conv_b, w_rg_a, b_rg_a, w_rg_x, b_rg_x, lru_lambda, attn_sinks, w_rnn_out, w_attn_out, w_out, post_norm_g, loss_target, m_pre_norm_g, m_w_in, m_b_gate, m_conv_w, m_conv_b, m_w_rg_a, m_b_rg_a, m_w_rg_x, m_b_rg_x, m_lru_lambda, m_attn_sinks, m_w_rnn_out, m_w_attn_out, m_w_out, m_post_norm_g, v_pre_norm_g, v_w_in, v_b_gate, v_conv_w, v_conv_b, v_w_rg_a, v_b_rg_a, v_w_rg_x, v_b_rg_x, v_lru_lambda, v_attn_sinks, v_w_rnn_out, v_w_attn_out, v_w_out, v_post_norm_g):
    w = dict(pre_norm_g=pre_norm_g, w_in=w_in, b_gate=b_gate, conv_w=conv_w, conv_b=conv_b, w_rg_a=w_rg_a,
             b_rg_a=b_rg_a, w_rg_x=w_rg_x, b_rg_x=b_rg_x, lru_lambda=lru_lambda, attn_sinks=attn_sinks,
             w_rnn_out=w_rnn_out, w_attn_out=w_attn_out, w_out=w_out, post_norm_g=post_norm_g)
    m = dict(pre_norm_g=m_pre_norm_g, w_in=m_w_in, b_gate=m_b_gate, conv_w=m_conv_w, conv_b=m_conv_b, w_rg_a=m_w_rg_a,
             b_rg_a=m_b_rg_a, w_rg_x=m_w_rg_x, b_rg_x=m_b_rg_x, lru_lambda=m_lru_lambda, attn_sinks=m_attn_sinks,
             w_rnn_out=m_w_rnn_out, w_attn_out=m_w_attn_out, w_out=m_w_out, post_norm_g=m_post_norm_g)
    v = dict(pre_norm_g=v_pre_norm_g, w_in=v_w_in, b_gate=v_b_gate, conv_w=v_conv_w, conv_b=v_conv_b, w_rg_a=v_w_rg_a,
             b_rg_a=v_b_rg_a, w_rg_x=v_w_rg_x, b_rg_x=v_b_rg_x, lru_lambda=v_lru_lambda, attn_sinks=v_attn_sinks,
             w_rnn_out=v_w_rnn_out, w_attn_out=v_w_attn_out, w_out=v_w_out, post_norm_g=v_post_norm_g)
    chip = 2 * lax.axis_index("x") + lax.axis_index("y")

    chip_idx = chip.astype(jnp.int32).reshape(1)
    chip_core = jnp.stack([chip, lax.axis_index("c")]).astype(jnp.int32)
    cw8 = jnp.pad(conv_w[0], ((0, 8 - CONV_W), (0, 0)))
    placed = _place_shards([w_in[0], w_rnn_out[0], w_attn_out[0], w_out[0]], chip_idx, "place_shards")
    win_g, cw_g = _gather_weights(placed[:1], cw8)
    late_send, late_recv, late_thru, late_token = _gather_late_start(placed[1:], win_g, "gather_late_start")
    cw_g = lax.dynamic_update_slice_in_dim(cw_g, cw8[None], chip, axis=0)
    conv_w_full = jnp.transpose(cw_g[:, 0:CONV_W, :], (1, 0, 2)).reshape(CONV_W, D_RNN)

    started = {}

    def start_reduction(tag, grads, grads_b16):
        psums, psums_b16 = _pair_sum(grads, grads_b16, chip_core, "pair_sum_" + tag)
        send_sems, recv_sems, p_thru, land_thru, token = _chip_exchange_start(psums_b16, "chip_exchange_start_" + tag)
        started[tag] = (psums, send_sems, recv_sems, p_thru, land_thru)
        return token

    def end_reduction(tag, after):
        psums, send_sems, recv_sems, p_thru, land_thru = started[tag]
        _, landed = _chip_exchange_wait(send_sems, recv_sems, p_thru, land_thru, after, "chip_exchange_wait_" + tag)
        return _chip_sum(psums, landed, chip_core, "chip_sum_" + tag)

    def out_weights(after):
        gathered = _gather_late_wait(late_send, late_recv, late_thru, after, "gather_late_wait")
        return [g.reshape(D_MODEL, D_MODEL) for g in gathered]

    grad_x, small = _local_grads(
        x[0], loss_target[0], pre_norm_g, win_g, b_gate, conv_w_full, conv_b, w_rg_a[0], b_rg_a[0], w_rg_x[0],
        b_rg_x[0], lru_lambda, attn_sinks[0], out_weights, late_token, post_norm_g,
        on_out_grads=lambda grads, grads_b16: start_reduction("out", grads, grads_b16),
        on_w_in_grad=lambda grad, grad_b16: start_reduction("in", [grad], [grad_b16]))

    small_chip = _small_pair_sum(small)
    small_send, small_recv, small_thru, small_land, small_token = _chip_exchange_start(
        list(small_chip), "small_exchange_start", blocked=False)

    halves = end_reduction("in", small_token) + end_reduction("out", small_token)
    gbig = dict(zip(BIG, _pair_share(halves)))

    grads, delta, new_m, new_v = {}, {}, {}, {}
    for names, tag in ((BIG[:1], "adamw_in"), (BIG[1:], "adamw_out")):
        updates = _adamw([(w[n][0], gbig[n], m[n][0], v[n][0]) for n in names], tag)
        for n, (d, nm, nv, g) in zip(names, updates):
            grads[n], delta[n], new_m[n], new_v[n] = g[None], d[None], nm[None], nv[None]

    small_own, small_landed = _chip_exchange_wait(small_send, small_recv, small_thru, small_land, delta[BIG[-1]],
                                                  "small_exchange_wait", blocked=False)
    small_sum = _small_total(chip_idx, small_own, small_landed)
    total_loss = small_sum[ROW_LOSS, 0]
    gsmall = _unpack_small(small_sum, D_RNN)
    conv_shard = D_RNN // N_CHIPS
    gsmall["conv_w"] = lax.dynamic_slice_in_dim(gsmall["conv_w"], chip * conv_shard, conv_shard, axis=2)
    for n in gsmall:
        grads[n] = gsmall[n].reshape(w[n].shape)
    updates = _adamw_whole([(w[n], grads[n], m[n], v[n]) for n in gsmall], "adamw_small")
    for n, (d, nm, nv) in zip(gsmall, updates):
        delta[n], new_m[n], new_v[n] = d, nm, nv

    return (total_loss, grad_x[None], *[grads[n] for n in WEIGHTS], *[delta[n] for n in WEIGHTS],
            *[new_m[n] for n in WEIGHTS], *[new_v[n] for n in WEIGHTS])
```

```python
import functools
import math

import jax
import jax.numpy as jnp
import numpy as np
from jax import lax
from jax.experimental import pallas as pl
from jax.experimental.pallas import tpu as pltpu

F32 = jnp.float32
BF16 = jnp.bfloat16

D_MODEL = 1024
D_RNN = 1024
RNN_BLOCKS = 16
RNN_BLOCK_W = 64
CONV_W = 4
LRU_C = 8.0
N_Q_HEADS = 16
N_KV_HEADS = 4
GROUP = 4
HEAD_DIM = 64
D_KV = 256
BLOCK = 128
ALIBI_MAX_BIAS = 8.0
EPS = 1e-6
D_IN = 6656
N_CHIPS = 4
W_IN_SHARD = D_IN // N_CHIPS
OUT_SHARD = D_MODEL // N_CHIPS
ADAM_LR = 0.001
ADAM_B1 = 0.9
ADAM_B2 = 0.999
ADAM_EPS = 1e-08
ADAM_WD = 0.01
ADAM_STEP = 10
NEG_BIG = -1e30
MIB = 1 << 20

COL_RNN_X = 0
COL_RNN_GATE = 4
COL_Q = 8
COL_K = 12
COL_V = 13
COL_ATTN_GATE = 14
COL_MERGE = 18

RNN_TILE = 256
RNN_CHUNK = 512
SMALL_ROWS = 152
SMALL_VECTOR_ROWS = 24
MESH = pl.DeviceIdType.MESH


def _sds(shape, dtype):
    return pltpu.HBM(shape, dtype)


def _params(sem=None, vmem_mib=None):
    kw = {}
    if sem is not None:
        kw["dimension_semantics"] = sem
    if vmem_mib is not None:
        kw["vmem_limit_bytes"] = vmem_mib * MIB
    return pltpu.CompilerParams(**kw)


def _hbm(*arrays):
    return [pltpu.with_memory_space_constraint(a, pltpu.HBM) for a in arrays]


def _dot(a, b):
    return jnp.dot(a, b, preferred_element_type=F32)


def _dot_nt(a, b):
    return lax.dot_general(a, b, (((1,), (1,)), ((), ())), preferred_element_type=F32)


def _dot_tn(a, b):
    return lax.dot_general(a, b, (((0,), (0,)), ((), ())), preferred_element_type=F32)


def _sigmoid(x):
    return 0.5 * jnp.tanh(0.5 * x) + 0.5


def _sigmoid_small(x):
    return 1.0 / (1.0 + jnp.exp(-x))


def _softplus(x):
    return jnp.maximum(x, 0.0) + jnp.log(1.0 + jnp.exp(-jnp.abs(x)))


def _one_minus_square(a, log_a):
    return -jnp.tanh(log_a) * (a * a + 1.0)


def _proj_fwd(x, g_pre, w_in_g):
    T = x.shape[0]
    tm = min(1024, T)

    def body(x_ref, g_ref, w_ref, proj_ref, ht_ref, h_s):
        @pl.when(pl.program_id(1) == 0)
        def _():
            xv = x_ref[...]
            rstd = lax.rsqrt(jnp.mean(xv * xv, axis=-1, keepdims=True) + EPS)
            hf = (xv * rstd) * g_ref[...]
            h_s[...] = hf.astype(BF16)
            ht_ref[...] = hf.T.astype(BF16)

        proj_ref[...] = _dot(h_s[...], w_ref[...]).astype(BF16)

    return pl.pallas_call(
        body,
        name="proj_fwd",
        grid=(T // tm, N_CHIPS),
        in_specs=[
            pl.BlockSpec((tm, D_MODEL), lambda i, j: (i, 0)),
            pl.BlockSpec((1, D_MODEL), lambda i, j: (0, 0)),
            pl.BlockSpec((None, D_MODEL, W_IN_SHARD), lambda i, j: (j, 0, 0)),
        ],
        out_specs=[
            pl.BlockSpec((tm, W_IN_SHARD), lambda i, j: (i, j)),
            pl.BlockSpec((D_MODEL, tm), lambda i, j: (0, i)),
        ],
        out_shape=[_sds((T, D_IN), BF16), _sds((D_MODEL, T), BF16)],
        scratch_shapes=[pltpu.VMEM((tm, D_MODEL), BF16)],
        compiler_params=_params(("parallel", "arbitrary"), 48),
    )(*_hbm(x, g_pre, w_in_g))


def _shift_down(x, tail, s, row):
    n = x.shape[0]
    xs = pltpu.roll(x, s, 0)
    tail_t = jnp.tile(pltpu.roll(tail, s, 0), (n // 8, 1))
    return jnp.where(row < s, tail_t, xs)


def _shift_up(x, head, s, row):
    n = x.shape[0]
    xs = pltpu.roll(x, n - s, 0)
    head_t = jnp.tile(pltpu.roll(head, 8 - s, 0), (n // 8, 1))
    return jnp.where(row >= n - s, head_t, xs)


def _conv_taps(x, tail, row):
    return [_shift_down(x, tail, 3, row), _shift_down(x, tail, 2, row), _shift_down(x, tail, 1, row), x]


def _rglru_gates(c, wa, wx, ba, bx, lam):
    cb = c.astype(BF16)
    r = _sigmoid_small(_dot(cb, wa) + ba)
    i = _sigmoid(_dot(cb, wx) + bx)
    log_a = (-LRU_C) * r * _softplus(-lam)
    a = jnp.exp(log_a)
    w = _one_minus_square(a, log_a)
    inv_mult = lax.rsqrt(w)
    return cb, r, i, a, w * inv_mult, inv_mult


GATE_BLOCKS_PER_TILE = RNN_TILE // RNN_BLOCK_W
GATE_BLOCKS = pl.BlockSpec((GATE_BLOCKS_PER_TILE, RNN_BLOCK_W, RNN_BLOCK_W), lambda j, t: (j, 0, 0))


def _fill_block_diag(bd_ref, w_ref):
    bd_ref[...] = jnp.zeros_like(bd_ref)
    for a in range(GATE_BLOCKS_PER_TILE):
        lo = a * RNN_BLOCK_W
        bd_ref[lo:lo + RNN_BLOCK_W, lo:lo + RNN_BLOCK_W] = w_ref[a].astype(BF16)


SUBLANES = 8


def _scan_down(a, u, row):
    n = a.shape[0]
    s = 1
    while s < SUBLANES:
        a_sh = jnp.where(row >= s, pltpu.roll(a, s, 0), 1.0)
        u_sh = jnp.where(row >= s, pltpu.roll(u, s, 0), 0.0)
        u = a * u_sh + u
        a = a * a_sh
        s *= 2
    while s < n:
        u = jnp.concatenate([u[:s], a[s:] * u[:n - s] + u[s:]], axis=0)
        a = jnp.concatenate([a[:s], a[s:] * a[:n - s]], axis=0)
        s *= 2
    return a, u


def _scan_up(b, u, row):
    n = b.shape[0]
    s = 1
    while s < SUBLANES:
        b_sh = jnp.where(row < n - s, pltpu.roll(b, n - s, 0), 1.0)
        u_sh = jnp.where(row < n - s, pltpu.roll(u, n - s, 0), 0.0)
        u = b * u_sh + u
        b = b * b_sh
        s *= 2
    while s < n:
        u = jnp.concatenate([b[:n - s] * u[s:] + u[:n - s], u[n - s:]], axis=0)
        b = jnp.concatenate([b[:n - s] * b[s:], b[n - s:]], axis=0)
        s *= 2
    return b, u


LANES = 128


def _chunk_scan(a, u, a_s, u_s, hl_s, al_s, carry, reverse):
    n, width = a.shape
    groups = n // SUBLANES
    order = range(SUBLANES - 1, -1, -1) if reverse else range(SUBLANES)
    row = lax.broadcasted_iota(jnp.int32, (groups, LANES), 0)
    for l in range(width // LANES):
        lanes = slice(l * LANES, (l + 1) * LANES)
        a_l, u_l, hl_l, al_l = a_s.at[l], u_s.at[l], hl_s.at[l], al_s.at[l]
        a_l[...] = a[:, lanes]
        u_l[...] = u[:, lanes]
        h_loc = a_loc = None
        for r in order:
            rows = pl.ds(r, groups, stride=SUBLANES)
            a_r, u_r = a_l[rows, :], u_l[rows, :]
            h_loc, a_loc = (u_r, a_r) if h_loc is None else (a_r * h_loc + u_r, a_r * a_loc)
            hl_l[rows, :] = h_loc
            al_l[rows, :] = a_loc
        if reverse:
            a_cum, ends = _scan_up(a_loc, h_loc, row)
            ends = ends + a_cum * carry[:, lanes]
            enters = jnp.where(row == groups - 1, carry[:, lanes], pltpu.roll(ends, groups - 1, 0))
        else:
            a_cum, ends = _scan_down(a_loc, h_loc, row)
            ends = ends + a_cum * carry[:, lanes]
            enters = jnp.where(row == 0, carry[:, lanes], pltpu.roll(ends, 1, 0))
        for r in range(SUBLANES):
            rows = pl.ds(r, groups, stride=SUBLANES)
            hl_l[rows, :] = hl_l[rows, :] + al_l[rows, :] * enters
    return jnp.concatenate([hl_s[l] for l in range(width // LANES)], axis=1)


def _rnn_fwd(proj, conv_w, conv_b, w_a, w_x, b_a, b_x, lam, token):
    T = proj.shape[0]
    tc, ct = RNN_CHUNK, RNN_TILE
    nt = T // tc

    def body(x_ref, rg_ref, cw_ref, cb_ref, wa_ref, wx_ref, ba_ref, bx_ref, lam_ref, token_ref, h_ref, z_ref, c_ref,
             zt_ref, xtail, hcarry, wa_s, wx_s, a_s, u_s, hl_s, al_s):
        @pl.when(pl.program_id(1) == 0)
        def _():
            xtail[...] = jnp.zeros_like(xtail)
            hcarry[...] = jnp.zeros_like(hcarry)
            _fill_block_diag(wa_s, wa_ref)
            _fill_block_diag(wx_s, wx_ref)

        row = lax.broadcasted_iota(jnp.int32, (tc, ct), 0)
        x = x_ref[...].astype(F32)
        taps = _conv_taps(x, xtail[...], row)
        c = cb_ref[...] + cw_ref[pl.ds(0, 1), :] * taps[0]
        for k in range(1, CONV_W):
            c = c + cw_ref[pl.ds(k, 1), :] * taps[k]
        xtail[...] = x[tc - 8:, :]
        c_ref[...] = c
        _, _, i, a, mult, _ = _rglru_gates(c, wa_s[...], wx_s[...], ba_ref[...], bx_ref[...], lam_ref[...])
        h = _chunk_scan(a, mult * (i * c), a_s, u_s, hl_s, al_s, hcarry[...], reverse=False)
        h_ref[...] = h
        hcarry[...] = h_ref[pl.ds(tc - 1, 1), :]
        rg = rg_ref[...].astype(F32)
        z = h * (rg * _sigmoid(rg))
        z_ref[...] = z.astype(BF16)
        zt_ref[...] = z.T.astype(BF16)

    col = lambda off: (lambda j, t: (t, off + j))
    vec = pl.BlockSpec((1, ct), lambda j, t: (0, j))
    return pl.pallas_call(
        body,
        name="rnn_fwd",
        grid=(D_RNN // ct, nt),
        in_specs=[
            pl.BlockSpec((tc, ct), col(COL_RNN_X)),
            pl.BlockSpec((tc, ct), col(COL_RNN_GATE)),
            pl.BlockSpec((CONV_W, ct), lambda j, t: (0, j)),
            vec, GATE_BLOCKS, GATE_BLOCKS, vec, vec, vec,
            pl.BlockSpec((8, 128), lambda j, t: (0, 0)),
        ],
        out_specs=[pl.BlockSpec((tc, ct), lambda j, t: (t, j))] * 3 + [pl.BlockSpec((ct, tc), lambda j, t: (j, t))],
        out_shape=[_sds((T, D_RNN), F32), _sds((T, D_RNN), BF16), _sds((T, D_RNN), F32), _sds((D_RNN, T), BF16)],
        scratch_shapes=[pltpu.VMEM((8, ct), F32), pltpu.VMEM((1, ct), F32)] + [pltpu.VMEM((ct, ct), BF16)] * 2 + [
            pltpu.VMEM((ct // LANES, tc, LANES), F32)] * 4,
        compiler_params=_params(("parallel", "arbitrary"), 32),
    )(*_hbm(proj, proj, conv_w, conv_b, w_a, w_x, b_a, b_x, lam, token))


def _rnn_bwd(proj, conv, y_rnn, dz_rnn, conv_w, w_a, w_x, b_a, b_x, lam):
    T = proj.shape[0]
    tc, ct = RNN_CHUNK, RNN_TILE
    nt = T // tc
    hb = tc // 8

    def body(x_ref, c_ref, rg_ref, h_ref, hh_ref, dz_ref, cw_ref, wa_ref, wx_ref, ba_ref, bx_ref, lam_ref,
             dx_ref, drg_ref, dwa_ref, dwx_ref, sm_ref, lam_carry, a_carry, dc_head, wa_s, wx_s, dwa_s, dwx_s,
             b_s, dy_s, hl_s, al_s):
        t = pl.program_id(1)
        first_chunk = t == nt - 1

        @pl.when(t == 0)
        def _():
            lam_carry[...] = jnp.zeros_like(lam_carry)
            a_carry[...] = jnp.zeros_like(a_carry)
            dc_head[...] = jnp.zeros_like(dc_head)
            dwa_s[...] = jnp.zeros_like(dwa_s)
            dwx_s[...] = jnp.zeros_like(dwx_s)
            sm_ref[...] = jnp.zeros_like(sm_ref)
            _fill_block_diag(wa_s, wa_ref)
            _fill_block_diag(wx_s, wx_ref)

        row = lax.broadcasted_iota(jnp.int32, (tc, ct), 0)
        keep = jnp.where(first_chunk, 0.0, 1.0)
        x = x_ref[...].astype(F32)
        c = c_ref[...]
        lam = lam_ref[...]
        cb, r, i, a, mult, inv_mult = _rglru_gates(c, wa_s[...], wx_s[...], ba_ref[...], bx_ref[...], lam)
        h = h_ref[...]
        h_prev = _shift_down(h, hh_ref[...] * keep, 1, row)
        rg = rg_ref[...].astype(F32)
        dz = dz_ref[...]
        sg = _sigmoid(rg)
        drg_ref[...] = (dz * h * (sg * (1.0 + rg * (1.0 - sg)))).astype(BF16)
        dy = dz * (rg * sg)
        b = jnp.where(row >= tc - 1, a_carry[pl.ds(0, 1), :], pltpu.roll(a, tc - 1, 0))
        lt = _chunk_scan(b, dy, b_s, dy_s, hl_s, al_s, lam_carry[pl.ds(0, 1), :], reverse=True)
        lam_carry[...] = lt[0:8, :]
        a_carry[...] = a[0:8, :]
        ic = i * c
        dmult = lt * ic
        di = lt * mult * c
        dc = lt * mult * i
        dlog_a = a * (lt * h_prev - dmult * a * inv_mult)
        sp = _softplus(-lam)
        dpre_r = dlog_a * ((-LRU_C) * sp) * (r * (1.0 - r))
        dpre_i = di * (i * (1.0 - i))
        dlam_row = jnp.sum(dlog_a * r, axis=0, keepdims=True) * (LRU_C * _sigmoid(-lam))
        dpr_b = dpre_r.astype(BF16)
        dpi_b = dpre_i.astype(BF16)
        dwa_s[...] += _dot_tn(cb, dpr_b)
        dwx_s[...] += _dot_tn(cb, dpi_b)
        dc = dc + _dot_nt(dpr_b, wa_s[...]) + _dot_nt(dpi_b, wx_s[...])
        head = dc_head[...]
        dx = cw_ref[pl.ds(3, 1), :] * dc
        sm_ref[pl.ds(4 + 3, 1), :] += jnp.sum(dc * x, axis=0, keepdims=True)
        for m in range(1, CONV_W):
            up = _shift_up(dc, head, m, row)
            dx = dx + cw_ref[pl.ds(3 - m, 1), :] * up
            sm_ref[pl.ds(4 + 3 - m, 1), :] += jnp.sum(up * x, axis=0, keepdims=True)
        dx_ref[...] = dx.astype(BF16)
        dc_head[...] = dc[0:8, :]
        sm_ref[pl.ds(0, 1), :] += jnp.sum(dpre_r, axis=0, keepdims=True)
        sm_ref[pl.ds(1, 1), :] += jnp.sum(dpre_i, axis=0, keepdims=True)
        sm_ref[pl.ds(2, 1), :] += dlam_row
        sm_ref[pl.ds(3, 1), :] += jnp.sum(dc, axis=0, keepdims=True)

        @pl.when(first_chunk)
        def _():
            for k in range(GATE_BLOCKS_PER_TILE):
                lo = k * RNN_BLOCK_W
                dwa_ref[k] = dwa_s[lo:lo + RNN_BLOCK_W, lo:lo + RNN_BLOCK_W]
                dwx_ref[k] = dwx_s[lo:lo + RNN_BLOCK_W, lo:lo + RNN_BLOCK_W]

    rev = lambda off: (lambda j, t: (nt - 1 - t, off + j))
    halo = lambda off: (lambda j, t: (jnp.maximum((nt - 1 - t) * hb - 1, 0), off + j))
    vec = pl.BlockSpec((1, ct), lambda j, t: (0, j))
    mat = GATE_BLOCKS
    return pl.pallas_call(
        body,
        name="rnn_bwd",
        grid=(D_RNN // ct, nt),
        in_specs=[
            pl.BlockSpec((tc, ct), rev(COL_RNN_X)),
            pl.BlockSpec((tc, ct), rev(0)),
            pl.BlockSpec((tc, ct), rev(COL_RNN_GATE)),
            pl.BlockSpec((tc, ct), rev(0)),
            pl.BlockSpec((8, ct), halo(0)),
            pl.BlockSpec((tc, ct), rev(0)),
            pl.BlockSpec((CONV_W, ct), lambda j, t: (0, j)),
            mat, mat, vec, vec, vec,
        ],
        out_specs=[
            pl.BlockSpec((tc, ct), rev(0)),
            pl.BlockSpec((tc, ct), rev(0)),
            mat, mat,
            pl.BlockSpec((8, ct), lambda j, t: (0, j)),
        ],
        out_shape=[_sds((T, D_RNN), BF16), _sds((T, D_RNN), BF16), _sds(w_a.shape, F32), _sds(w_x.shape, F32),
                   _sds((8, D_RNN), F32)],
        scratch_shapes=[pltpu.VMEM((8, ct), F32)] * 3 + [pltpu.VMEM((ct, ct), BF16)] * 2 + [
            pltpu.VMEM((ct, ct), F32)] * 2 + [pltpu.VMEM((ct // LANES, tc, LANES), F32)] * 4,
        compiler_params=_params(("parallel", "arbitrary"), 32),
    )(*_hbm(proj, conv, proj, y_rnn, y_rnn, dz_rnn, conv_w, w_a, w_x, b_a, b_x, lam))


def _attn_bias():
    qi = np.arange(BLOCK)[:, None]
    kj = np.arange(BLOCK)[None, :]
    dist_cur = (qi - kj).astype(np.float32)
    slopes = np.float32(2.0) ** (-ALIBI_MAX_BIAS * np.arange(1, N_Q_HEADS + 1, dtype=np.float32) / N_Q_HEADS)
    slopes = slopes[:, None, None]
    prev = np.where(kj > qi, -slopes * (dist_cur + np.float32(BLOCK)), np.float32(NEG_BIG))
    cur = np.where(kj <= qi, -slopes * dist_cur, np.float32(NEG_BIG))
    later = np.concatenate([prev, cur], axis=-1)
    first = np.concatenate([np.full_like(prev, NEG_BIG), cur], axis=-1)
    return jnp.asarray(np.stack([first, later]).astype(np.float32))


def _attn_exps(s_prev, s_cur, sink, bias):
    s_prev = s_prev + bias[:, 0:BLOCK]
    s_cur = s_cur + bias[:, BLOCK:2 * BLOCK]
    m = jnp.maximum(jnp.max(jnp.maximum(s_prev, s_cur), axis=-1, keepdims=True), sink)
    p_prev = jnp.exp(s_prev - m)
    p_cur = jnp.exp(s_cur - m)
    total = jnp.sum(p_prev + p_cur, axis=-1, keepdims=True) + jnp.exp(sink - m)
    return p_prev, p_cur, 1.0 / total, m + jnp.log(total)


def _attn_probs(s_prev, s_cur, sink, bias, lse):
    p_prev = jnp.exp((s_prev + bias[:, 0:BLOCK]) - lse)
    p_cur = jnp.exp((s_cur + bias[:, BLOCK:2 * BLOCK]) - lse)
    return p_prev, p_cur, jnp.exp(sink - lse)


def _stack_heads(ref_or_val, hk, dtype):
    parts = [ref_or_val[:, (GROUP * hk + g) * HEAD_DIM:(GROUP * hk + g + 1) * HEAD_DIM] for g in range(GROUP)]
    return jnp.concatenate(parts, axis=0).astype(dtype)


ATTN_SCALE = HEAD_DIM ** -0.5


def _bias_spec():
    return pl.BlockSpec((None, N_Q_HEADS, BLOCK, 2 * BLOCK), lambda i: (jnp.minimum(i, 1), 0, 0, 0))


def _attn_fwd(proj, sinks, bias):
    T = proj.shape[0]
    nb = T // BLOCK

    def body(sink_ref, bias_ref, q_ref, kp_ref, kc_ref, vp_ref, vc_ref, ag0_ref, ag1_ref, y_ref, z_ref, lse_ref):
        kvs = [slice(hk * HEAD_DIM, (hk + 1) * HEAD_DIM) for hk in range(N_KV_HEADS)]
        qgs = [(_stack_heads(q_ref, hk, F32) * ATTN_SCALE).astype(BF16) for hk in range(N_KV_HEADS)]
        s_prev = [_dot_nt(qgs[hk], kp_ref[:, kvs[hk]].astype(BF16)) for hk in range(N_KV_HEADS)]
        s_cur = [_dot_nt(qgs[hk], kc_ref[:, kvs[hk]].astype(BF16)) for hk in range(N_KV_HEADS)]
        for hk in range(N_KV_HEADS):
            pp, pc, invs = [], [], []
            for g in range(GROUP):
                h = GROUP * hk + g
                rows = slice(g * BLOCK, (g + 1) * BLOCK)
                p_prev, p_cur, inv, lse = _attn_exps(s_prev[hk][rows], s_cur[hk][rows], sink_ref[h], bias_ref[h])
                pp.append(p_prev.astype(BF16))
                pc.append(p_cur.astype(BF16))
                invs.append(inv)
                lse_ref[:, h:h + 1] = lse
            og = _dot(jnp.concatenate(pp, axis=0), vp_ref[:, kvs[hk]].astype(BF16)) + _dot(
                jnp.concatenate(pc, axis=0), vc_ref[:, kvs[hk]].astype(BF16))
            for g in range(GROUP):
                h = GROUP * hk + g
                y_ref[:, h * HEAD_DIM:(h + 1) * HEAD_DIM] = og[g * BLOCK:(g + 1) * BLOCK] * invs[g]
        ag = jnp.concatenate([ag0_ref[...], ag1_ref[...]], axis=1).astype(F32)
        z_ref[...] = (y_ref[...] * (ag * _sigmoid(ag))).astype(BF16)

    prev = lambda c: (lambda i: (jnp.maximum(i - 1, 0), c))
    cur = lambda c: (lambda i: (i, c))
    return pl.pallas_call(
        body,
        name="attn_fwd",
        grid=(nb,),
        in_specs=[
            pl.BlockSpec(memory_space=pltpu.SMEM),
            _bias_spec(),
            pl.BlockSpec((BLOCK, 1024), lambda i: (i, COL_Q // 4)),
            pl.BlockSpec((BLOCK, D_KV), prev(COL_K)),
            pl.BlockSpec((BLOCK, D_KV), cur(COL_K)),
            pl.BlockSpec((BLOCK, D_KV), prev(COL_V)),
            pl.BlockSpec((BLOCK, D_KV), cur(COL_V)),
            pl.BlockSpec((BLOCK, 512), lambda i: (i, COL_ATTN_GATE // 2)),
            pl.BlockSpec((BLOCK, 512), lambda i: (i, COL_ATTN_GATE // 2 + 1)),
        ],
        out_specs=[pl.BlockSpec((BLOCK, 1024), lambda i: (i, 0)), pl.BlockSpec((BLOCK, 1024), lambda i: (i, 0)),
                   pl.BlockSpec((BLOCK, N_Q_HEADS), lambda i: (i, 0))],
        out_shape=[_sds((T, 1024), F32), _sds((T, 1024), BF16), _sds((T, N_Q_HEADS), F32)],
        compiler_params=_params(("arbitrary",), 32),
    )(sinks, *_hbm(bias, proj, proj, proj, proj, proj, proj, proj))


def _attn_bwd(proj, y_attn, lse, dz_attn, sinks, bias, token):
    T = proj.shape[0]
    nb = T // BLOCK

    def body(sink_ref, bias_ref, q_ref, kp_ref, kc_ref, vp_ref, vc_ref, ag0_ref, ag1_ref, y_ref, lse_ref, dz_ref,
             token_ref, dq_ref, dk_ref, dv_ref, dag_ref, ds_ref, dy_s):
        i = pl.program_id(0)

        @pl.when(i == 0)
        def _():
            ds_ref[...] = jnp.zeros_like(ds_ref)

        lane = lax.broadcasted_iota(jnp.int32, (8, 128), 1)
        sub = lax.broadcasted_iota(jnp.int32, (8, 128), 0)
        ag = jnp.concatenate([ag0_ref[...], ag1_ref[...]], axis=1).astype(F32)
        dz = dz_ref[...]
        sg = _sigmoid(ag)
        dag_ref[...] = (dz * y_ref[...] * (sg * (1.0 + ag * (1.0 - sg)))).astype(BF16)
        dy_s[...] = dz * (ag * sg)
        r_cur = pl.multiple_of(i * BLOCK, BLOCK)
        r_prev = pl.multiple_of(jnp.maximum(i - 1, 0) * BLOCK, BLOCK)
        dk_cur, dv_cur, dk_prev, dv_prev = [], [], [], []
        ds_acc = jnp.zeros((8, 128), F32)
        for hk in range(N_KV_HEADS):
            ks = slice(hk * HEAD_DIM, (hk + 1) * HEAD_DIM)
            qg = (_stack_heads(q_ref, hk, F32) * ATTN_SCALE).astype(BF16)
            dog = _stack_heads(dy_s, hk, F32)
            og = _stack_heads(y_ref, hk, F32)
            dog_b = dog.astype(BF16)
            kp = kp_ref[:, ks].astype(BF16)
            kc = kc_ref[:, ks].astype(BF16)
            vp = vp_ref[:, ks].astype(BF16)
            vc = vc_ref[:, ks].astype(BF16)
            s_prev = _dot_nt(qg, kp)
            s_cur = _dot_nt(qg, kc)
            dp_prev = _dot_nt(dog_b, vp)
            dp_cur = _dot_nt(dog_b, vc)
            dvec = jnp.sum(dog * og, axis=-1, keepdims=True)
            pp, pc, dsp, dsc = [], [], [], []
            for g in range(GROUP):
                h = GROUP * hk + g
                rows = slice(g * BLOCK, (g + 1) * BLOCK)
                p_prev, p_cur, p_sink = _attn_probs(
                    s_prev[rows], s_cur[rows], sink_ref[h], bias_ref[h], lse_ref[:, h:h + 1])
                d_h = dvec[rows]
                pp.append(p_prev.astype(BF16))
                pc.append(p_cur.astype(BF16))
                dsp.append((p_prev * (dp_prev[rows] - d_h)).astype(BF16))
                dsc.append((p_cur * (dp_cur[rows] - d_h)).astype(BF16))
                dsink = -jnp.sum(p_sink * d_h, axis=0, keepdims=True)
                ds_acc = ds_acc + jnp.where(jnp.logical_and(lane == h, sub == 1), dsink, 0.0)
            pp = jnp.concatenate(pp, axis=0)
            pc = jnp.concatenate(pc, axis=0)
            dsp = jnp.concatenate(dsp, axis=0)
            dsc = jnp.concatenate(dsc, axis=0)
            dqg = (_dot(dsp, kp) + _dot(dsc, kc)) * ATTN_SCALE
            for g in range(GROUP):
                h = GROUP * hk + g
                dq_ref[:, h * HEAD_DIM:(h + 1) * HEAD_DIM] = dqg[g * BLOCK:(g + 1) * BLOCK].astype(BF16)
            dk_ref[pl.ds(r_cur, BLOCK), ks] = _dot_tn(dsc, qg)
            dv_ref[pl.ds(r_cur, BLOCK), ks] = _dot_tn(pc, dog_b)
            dk_prev.append(_dot_tn(dsp, qg))
            dv_prev.append(_dot_tn(pp, dog_b))
        ds_ref[:, 0:128] += ds_acc

        @pl.when(i > 0)
        def _():
            for hk in range(N_KV_HEADS):
                ks = slice(hk * HEAD_DIM, (hk + 1) * HEAD_DIM)
                dk_ref[pl.ds(r_prev, BLOCK), ks] += dk_prev[hk]
                dv_ref[pl.ds(r_prev, BLOCK), ks] += dv_prev[hk]

    prev = lambda c: (lambda i: (jnp.maximum(i - 1, 0), c))
    cur = lambda c: (lambda i: (i, c))
    blk = pl.BlockSpec((BLOCK, 1024), lambda i: (i, 0))
    whole = pl.BlockSpec((T, D_KV), lambda i: (0, 0))
    return pl.pallas_call(
        body,
        name="attn_bwd",
        grid=(nb,),
        in_specs=[
            pl.BlockSpec(memory_space=pltpu.SMEM),
            _bias_spec(),
            pl.BlockSpec((BLOCK, 1024), lambda i: (i, COL_Q // 4)),
            pl.BlockSpec((BLOCK, D_KV), prev(COL_K)),
            pl.BlockSpec((BLOCK, D_KV), cur(COL_K)),
            pl.BlockSpec((BLOCK, D_KV), prev(COL_V)),
            pl.BlockSpec((BLOCK, D_KV), cur(COL_V)),
            pl.BlockSpec((BLOCK, 512), lambda i: (i, COL_ATTN_GATE // 2)),
            pl.BlockSpec((BLOCK, 512), lambda i: (i, COL_ATTN_GATE // 2 + 1)),
            blk,
            pl.BlockSpec((BLOCK, N_Q_HEADS), lambda i: (i, 0)),
            blk,
            pl.BlockSpec((8, 128), lambda i: (0, 0)),
        ],
        out_specs=[blk, whole, whole, blk, pl.BlockSpec((8, 1024), lambda i: (0, 0))],
        out_shape=[_sds((T, 1024), BF16), _sds((T, D_KV), F32), _sds((T, D_KV), F32), _sds((T, 1024), BF16),
                   _sds((8, 1024), F32)],
        scratch_shapes=[pltpu.VMEM((BLOCK, 1024), F32)],
        compiler_params=_params(("arbitrary",), 48),
    )(sinks, *_hbm(bias, proj, proj, proj, proj, proj, proj, proj, y_attn, lse, dz_attn, token))


def _head(x, target, z_rnn, z_attn, proj, b_gate, g_post, w_rnn_out, w_attn_out, w_out):
    T = x.shape[0]
    tm = 256

    def body(x_ref, t_ref, zr_ref, za_ref, ml0_ref, ml1_ref, ml2_ref, ml3_ref, bg_ref, gp_ref, wr_ref, wa_ref, wo_ref,
             dyx_ref, dzr_ref, dza_ref, dml_ref, dout_ref, dbr_ref, dba_ref, mt_ref, zat_ref, sm_ref):
        @pl.when(pl.program_id(0) == 0)
        def _():
            sm_ref[...] = jnp.zeros_like(sm_ref)

        wr, wa, wo = wr_ref[...], wa_ref[...], wo_ref[...]
        br_rnn = _dot(zr_ref[...], wr)
        br_attn = _dot(za_ref[...], wa)
        zat_ref[...] = za_ref[...].astype(F32).T.astype(BF16)
        ml_rnn = jnp.concatenate([ml0_ref[...], ml1_ref[...]], axis=1).astype(F32)
        ml_attn = jnp.concatenate([ml2_ref[...], ml3_ref[...]], axis=1).astype(F32)
        g_rnn = _sigmoid(ml_rnn + bg_ref[:, 0:D_MODEL])
        g_attn = _sigmoid(ml_attn + bg_ref[:, D_MODEL:2 * D_MODEL])
        merged = g_rnn * br_rnn + g_attn * br_attn
        mb = merged.astype(BF16)
        mt_ref[...] = merged.T.astype(BF16)
        out = _dot(mb, wo)
        rstd = lax.rsqrt(jnp.mean(out * out, axis=-1, keepdims=True) + EPS)
        n = out * rstd
        gp = gp_ref[...]
        err = (x_ref[...] + n * gp) - t_ref[...]
        sm_ref[pl.ds(3, 1), :] += 0.5 * jnp.sum(jnp.mean(err * err, axis=-1, keepdims=True), axis=0, keepdims=True)
        dy = err * (1.0 / D_MODEL)
        dyx_ref[...] = dy
        sm_ref[pl.ds(0, 1), :] += jnp.sum(dy * n, axis=0, keepdims=True)
        dn = dy * gp
        dout = (rstd * (dn - n * jnp.mean(dn * n, axis=-1, keepdims=True))).astype(BF16)
        dout_ref[...] = dout
        dmerged = _dot_nt(dout, wo)
        dml_r = (dmerged * br_rnn) * (g_rnn * (1.0 - g_rnn))
        dml_a = (dmerged * br_attn) * (g_attn * (1.0 - g_attn))
        dml_ref[:, 0:D_MODEL] = dml_r.astype(BF16)
        dml_ref[:, D_MODEL:2 * D_MODEL] = dml_a.astype(BF16)
        sm_ref[pl.ds(1, 1), :] += jnp.sum(dml_r, axis=0, keepdims=True)
        sm_ref[pl.ds(2, 1), :] += jnp.sum(dml_a, axis=0, keepdims=True)
        dbr = (dmerged * g_rnn).astype(BF16)
        dba = (dmerged * g_attn).astype(BF16)
        dbr_ref[...] = dbr
        dba_ref[...] = dba
        dzr_ref[...] = _dot_nt(dbr, wr)
        dza_ref[...] = _dot_nt(dba, wa)

    tile = pl.BlockSpec((tm, D_MODEL), lambda i: (i, 0))
    wspec = pl.BlockSpec((D_MODEL, D_MODEL), lambda i: (0, 0))
    ml = lambda q: pl.BlockSpec((tm, 512), lambda i: (i, COL_MERGE // 2 + q))
    return pl.pallas_call(
        body,
        name="head",
        grid=(T // tm,),
        in_specs=[
            tile, tile, tile, tile,
            ml(0), ml(1), ml(2), ml(3),
            pl.BlockSpec((1, 2 * D_MODEL), lambda i: (0, 0)),
            pl.BlockSpec((1, D_MODEL), lambda i: (0, 0)),
            wspec, wspec, wspec,
        ],
        out_specs=[
            tile, tile, tile,
            pl.BlockSpec((tm, 2 * D_MODEL), lambda i: (i, 0)),
            tile, tile, tile,
            pl.BlockSpec((D_MODEL, tm), lambda i: (0, i)), pl.BlockSpec((D_MODEL, tm), lambda i: (0, i)),
            pl.BlockSpec((8, D_MODEL), lambda i: (0, 0)),
        ],
        out_shape=[
            _sds((T, D_MODEL), F32), _sds((T, D_MODEL), F32), _sds((T, D_MODEL), F32),
            _sds((T, 2 * D_MODEL), BF16),
            _sds((T, D_MODEL), BF16), _sds((T, D_MODEL), BF16), _sds((T, D_MODEL), BF16),
            _sds((D_MODEL, T), BF16), _sds((D_MODEL, T), BF16),
            _sds((8, D_MODEL), F32),
        ],
        compiler_params=_params(("arbitrary",), 56),
    )(*_hbm(x, target, z_rnn, z_attn, proj, proj, proj, proj, b_gate, g_post, w_rnn_out, w_attn_out, w_out))


def _matmul_t(at, b, name):
    M, T = at.shape
    N = b.shape[1]
    tk = min(1024, T)
    nt = T // tk

    def body(a_ref, b_ref, o_ref, ob_ref):
        @pl.when(pl.program_id(0) == 0)
        def _():
            o_ref[...] = jnp.zeros_like(o_ref)

        o_ref[...] += _dot(a_ref[...], b_ref[...])

        @pl.when(pl.program_id(0) == nt - 1)
        def _():
            ob_ref[...] = o_ref[...].astype(BF16)

    whole = pl.BlockSpec((M, N), lambda t: (0, 0))
    return pl.pallas_call(
        body,
        name=name,
        grid=(nt,),
        in_specs=[pl.BlockSpec((M, tk), lambda t: (0, t)), pl.BlockSpec((tk, N), lambda t: (t, 0))],
        out_specs=[whole, whole],
        out_shape=[_sds((M, N), F32), _sds((M, N), BF16)],
        compiler_params=_params(("arbitrary",), 48),
    )(*_hbm(at, b))


DPROJ_WIDTHS = (D_RNN, D_RNN, 1024, D_KV, D_KV, 1024, 2 * D_MODEL)


def _dproj_segments():
    segs, start = [[] for _ in range(N_CHIPS)], 0
    for p, width in enumerate(DPROJ_WIDTHS):
        for c in range(N_CHIPS):
            lo, hi = max(start, c * W_IN_SHARD), min(start + width, (c + 1) * W_IN_SHARD)
            if lo < hi:
                segs[c].append((p, lo - start, hi - start, lo - c * W_IN_SHARD, hi - c * W_IN_SHARD))
        start += width
    return segs


def _dh_bwd(pieces, w_in_g, x, dyx, g_pre, token):
    T = x.shape[0]
    tm = min(512, T)
    n = len(pieces)
    segs = _dproj_segments()

    def body(*refs):
        p_refs, w_hbm, x_ref, dyx_ref, g_ref = refs[0:n], refs[n], refs[n + 1], refs[n + 2], refs[n + 3]
        gx_ref, dg_ref, w_ref, w_sems = refs[n + 5], refs[n + 6], refs[n + 7], refs[n + 8]
        first = pl.program_id(0) == 0
        w_copies = [pltpu.make_async_copy(w_hbm.at[c], w_ref.at[c], w_sems.at[c]) for c in range(N_CHIPS)]

        @pl.when(first)
        def _():
            for cp in w_copies:
                cp.start()
            dg_ref[...] = jnp.zeros_like(dg_ref)

        dh = None
        for c in range(N_CHIPS):
            pl.when(first)(w_copies[c].wait)
            for p, a0, a1, u0, u1 in segs[c]:
                part = _dot_nt(p_refs[p][:, a0:a1].astype(BF16), w_ref[c, :, u0:u1])
                dh = part if dh is None else dh + part
        xv = x_ref[...]
        rstd = lax.rsqrt(jnp.mean(xv * xv, axis=-1, keepdims=True) + EPS)
        nx = xv * rstd
        dhg = dh * g_ref[...]
        gx_ref[...] = dyx_ref[...] + rstd * (dhg - nx * jnp.mean(dhg * nx, axis=-1, keepdims=True))
        dg_ref[pl.ds(0, 1), :] += jnp.sum(dh * nx, axis=0, keepdims=True)

    tile = pl.BlockSpec((tm, D_MODEL), lambda i: (i, 0))
    return pl.pallas_call(
        body,
        name="dh_bwd",
        grid=(T // tm,),
        in_specs=[pl.BlockSpec((tm, w), lambda i: (i, 0)) for w in DPROJ_WIDTHS] + [
            ANY, tile, tile,
            pl.BlockSpec((1, D_MODEL), lambda i: (0, 0)),
            pl.BlockSpec((8, 128), lambda i: (0, 0)),
        ],
        out_specs=[tile, pl.BlockSpec((8, D_MODEL), lambda i: (0, 0))],
        out_shape=[_sds((T, D_MODEL), F32), _sds((8, D_MODEL), F32)],
        scratch_shapes=[pltpu.VMEM(w_in_g.shape, BF16), pltpu.SemaphoreType.DMA((N_CHIPS,))],
        compiler_params=_params(("arbitrary",), 56),
    )(*_hbm(*pieces, w_in_g, x, dyx, g_pre, token))


def _dw_in(ht, pieces):
    T = ht.shape[1]
    tk = min(1024, T)
    nt = T // tk
    n = len(pieces)
    segs = _dproj_segments()

    def body(*refs):
        h_ref, p_refs, o_ref, ob_ref = refs[0], refs[1:n + 1], refs[n + 1], refs[n + 2]

        @pl.when(pl.program_id(1) == 0)
        def _():
            o_ref[...] = jnp.zeros_like(o_ref)

        for c in range(N_CHIPS):
            @pl.when(pl.program_id(0) == c)
            def _():
                for p, a0, a1, u0, u1 in segs[c]:
                    o_ref[:, u0:u1] += _dot(h_ref[...], p_refs[p][:, a0:a1].astype(BF16))

        @pl.when(pl.program_id(1) == nt - 1)
        def _():
            ob_ref[...] = o_ref[...].astype(BF16)

    def piece_spec(p):
        chips = [c for c in range(N_CHIPS) if any(s[0] == p for s in segs[c])]

        def index(c, t):
            used = functools.reduce(jnp.logical_or, [c == k for k in chips])
            return (jnp.where(used, t, 0), 0)

        return pl.BlockSpec((tk, DPROJ_WIDTHS[p]), index)

    return pl.pallas_call(
        body,
        name="dw_in",
        grid=(N_CHIPS, nt),
        in_specs=[pl.BlockSpec((D_MODEL, tk), lambda c, t: (0, t))] + [piece_spec(p) for p in range(n)],
        out_specs=[pl.BlockSpec((None, D_MODEL, W_IN_SHARD), lambda c, t: (c, 0, 0))] * 2,
        out_shape=[_sds((N_CHIPS, D_MODEL, W_IN_SHARD), F32), _sds((N_CHIPS, D_MODEL, W_IN_SHARD), BF16)],
        compiler_params=_params(("parallel", "arbitrary"), 56),
    )(*_hbm(ht, *pieces))


ELEMENTWISE_TILE_BYTES = MIB


def _row_tile(rows, cols, limit=ELEMENTWISE_TILE_BYTES):
    if rows * cols * 4 <= limit:
        return rows
    for t in (512, 256, 128, 64, 32, 16, 8):
        if rows % t == 0 and t * cols * 4 <= limit:
            return t
    return rows


def _group_tiles(groups):
    tiles = [_row_tile(g[0].shape[0], g[0].shape[1] * len(g)) for g in groups]
    steps = max(g[0].shape[0] // t for g, t in zip(groups, tiles))
    return steps, [g[0].shape[0] // steps for g in groups]


def _chip_sum(groups, chip_core, name):
    ps = [p for group_ps, _ in groups for p in group_ps]
    gots = [g for _, group_gots in groups for g in group_gots]
    n = len(ps)
    steps, group_rows = _group_tiles([group_ps for group_ps, _ in groups])
    rows = [tr for (group_ps, _), tr in zip(groups, group_rows) for _ in group_ps]

    def body(jc_ref, *refs):
        for a in range(n):
            p_ref, g0_ref, g1_ref, g2_ref, o_ref = refs[a], refs[n + 3 * a], refs[n + 3 * a + 1], refs[n + 3 * a + 2], \
                refs[4 * n + a]
            o_ref[...] = ((p_ref[...] + g0_ref[...].astype(F32)) + g1_ref[...].astype(F32)) + g2_ref[...].astype(F32)

    tile = lambda p, tr: pl.BlockSpec((tr, p.shape[1]), lambda i, jc_ref: (i, 0))
    rel = lambda p, tr, r: pl.BlockSpec((None, tr, p.shape[1]), lambda i, jc_ref: (r, i, 0))
    half = lambda p, tr: pl.BlockSpec((tr, p.shape[1]), lambda i, jc_ref: (jc_ref[1] * steps + i, 0))
    outs = pl.pallas_call(
        body,
        name=name,
        grid_spec=pltpu.PrefetchScalarGridSpec(
            num_scalar_prefetch=1,
            grid=(steps,),
            in_specs=[tile(p, tr) for p, tr in zip(ps, rows)] + [
                rel(p, tr, r) for p, tr in zip(ps, rows) for r in range(3)],
            out_specs=[half(p, tr) for p, tr in zip(ps, rows)],
        ),
        out_shape=[_sds((2 * p.shape[0], p.shape[1]), F32) for p in ps],
        compiler_params=_params(("parallel",), 48),
    )(chip_core, *_hbm(*ps, *[g for got in gots for g in (got, got, got)]))
    return list(outs)


def _place_shards(shards, chip, name):
    n = len(shards)
    tiles = [_row_tile(s.shape[0], s.shape[1]) for s in shards]
    steps = max(s.shape[0] // t for s, t in zip(shards, tiles))
    tiles = [s.shape[0] // steps for s in shards]

    def body(j_ref, *refs):
        for a in range(n):
            refs[n + a][...] = refs[a][...].astype(BF16)

    return pl.pallas_call(
        body,
        name=name,
        grid_spec=pltpu.PrefetchScalarGridSpec(
            num_scalar_prefetch=1,
            grid=(steps,),
            in_specs=[pl.BlockSpec((t, s.shape[1]), lambda i, j_ref: (i, 0)) for s, t in zip(shards, tiles)],
            out_specs=[pl.BlockSpec((None, t, s.shape[1]), lambda i, j_ref: (j_ref[0], i, 0))
                       for s, t in zip(shards, tiles)],
        ),
        out_shape=[_sds((N_CHIPS,) + s.shape, BF16) for s in shards],
        compiler_params=_params(("parallel",), 48),
    )(chip, *_hbm(*shards))


def _adamw_update(w, g, m, v):
    c1 = 1.0 - ADAM_B1 ** ADAM_STEP
    c2 = 1.0 - ADAM_B2 ** ADAM_STEP
    nm = ADAM_B1 * m + (1.0 - ADAM_B1) * g
    nv = ADAM_B2 * v + (1.0 - ADAM_B2) * (g * g)
    return (-ADAM_LR) * ((nm / c1) / (jnp.sqrt(nv / c2) + ADAM_EPS) + ADAM_WD * w), nm, nv


def _adamw(groups, name):
    params = [p for group in groups for p in group]
    n = len(params)
    steps, group_rows = _group_tiles([[p[0] for p in group] for group in groups])
    rows = [tr for group, tr in zip(groups, group_rows) for _ in group]

    def body(*refs):
        for a in range(n):
            w_ref, g_ref, m_ref, v_ref = refs[4 * a:4 * a + 4]
            d_ref, nm_ref, nv_ref, go_ref = refs[4 * n + 4 * a:4 * n + 4 * a + 4]
            g = g_ref[...]
            d_ref[...], nm_ref[...], nv_ref[...] = _adamw_update(w_ref[...], g, m_ref[...], v_ref[...])
            go_ref[...] = g

    specs = [pl.BlockSpec((tr, p[0].shape[1]), lambda i: (i, 0)) for p, tr in zip(params, rows) for _ in range(4)]
    outs = pl.pallas_call(
        body, name=name, grid=(steps,), in_specs=specs, out_specs=specs,
        out_shape=[_sds(p[0].shape, F32) for p in params for _ in range(4)],
        compiler_params=_params(("parallel",), 48),
    )(*_hbm(*[t for p in params for t in p]))
    return [tuple(outs[4 * a:4 * a + 4]) for a in range(n)]


def _adamw_whole(params, name):
    n = len(params)

    def body(*refs):
        for a in range(n):
            w_ref, g_ref, m_ref, v_ref = refs[4 * a:4 * a + 4]
            d_ref, nm_ref, nv_ref = refs[4 * n + 3 * a:4 * n + 3 * a + 3]
            d_ref[...], nm_ref[...], nv_ref[...] = _adamw_update(w_ref[...], g_ref[...], m_ref[...], v_ref[...])

    def whole(t):
        return pl.BlockSpec(t.shape, lambda i: (0,) * t.ndim)

    flat = [t for p in params for t in p]
    like = [p[0] for p in params for _ in range(3)]
    outs = pl.pallas_call(
        body, name=name, grid=(1,), in_specs=[whole(t) for t in flat], out_specs=[whole(t) for t in like],
        out_shape=[_sds(t.shape, F32) for t in like], compiler_params=_params(("arbitrary",), 48),
    )(*_hbm(*flat))
    return [tuple(outs[3 * a:3 * a + 3]) for a in range(n)]


def _place():
    return lax.axis_index("x"), lax.axis_index("y"), lax.axis_index("c")


def _chip_of(x, y, r):
    return (x ^ (r >> 1), y ^ (r & 1))


ANY = pl.BlockSpec(memory_space=pl.ANY)


def _gather_weights(placed, cw8):
    nbig = len(placed)
    halves = [s.shape[1] // 2 for s in placed]
    pieces = [max(1, h // 64) for h in halves]
    rows = [h // p for h, p in zip(halves, pieces)]
    order = [(a, q) for q in range(max(pieces)) for a in range(nbig) if q < pieces[a]]
    ici_sem = {(a, q, r): 3 * i + (r - 1) for i, (a, q) in enumerate(order) for r in (1, 2, 3)}
    cw_sem = {r: 3 * len(order) + (r - 1) for r in (1, 2, 3)}
    d2d_sem = {key: 3 * len(order) + 3 + k for key, k in ici_sem.items()}
    nsem = 6 * len(order) + 3

    def body(*refs):
        cw_ref, dsts, gcw_ref = refs[nbig], refs[nbig + 1:2 * nbig + 1], refs[2 * nbig + 1]
        send_sems, recv_sems = refs[2 * nbig + 2:]
        x, y, c = _place()
        j = 2 * x + y

        def piece_rows(a, q, core):
            return pl.ds(pl.multiple_of(core * halves[a] + q * rows[a], 16), rows[a])

        def ici(a, q, r):
            tx, ty = _chip_of(x, y, r)
            k = ici_sem[(a, q, r)]
            region = dsts[a].at[j, piece_rows(a, q, c), :]
            return pltpu.make_async_remote_copy(
                src_ref=region, dst_ref=region, send_sem=send_sems.at[k], recv_sem=recv_sems.at[k],
                device_id=(tx, ty, c), device_id_type=MESH)

        def ici_landed(a, q, r):
            tx, ty = _chip_of(x, y, r)
            k = ici_sem[(a, q, r)]
            region = dsts[a].at[2 * tx + ty, piece_rows(a, q, c), :]
            return pltpu.make_async_remote_copy(
                src_ref=region, dst_ref=region, send_sem=send_sems.at[k], recv_sem=recv_sems.at[k],
                device_id=(tx, ty, c), device_id_type=MESH)

        def d2d(a, q, r, core):
            tx, ty = _chip_of(x, y, r)
            k = d2d_sem[(a, q, r)]
            region = dsts[a].at[2 * tx + ty, piece_rows(a, q, core), :]
            return pltpu.make_async_remote_copy(
                src_ref=region, dst_ref=region, send_sem=send_sems.at[k], recv_sem=recv_sems.at[k],
                device_id=(x, y, 1 - c), device_id_type=MESH)

        def cw_copy(r):
            tx, ty = _chip_of(x, y, r)
            k = cw_sem[r]
            return pltpu.make_async_remote_copy(
                src_ref=cw_ref, dst_ref=gcw_ref.at[j], send_sem=send_sems.at[k], recv_sem=recv_sems.at[k],
                device_id=(tx, ty, c), device_id_type=MESH)

        def cw_landed(r):
            tx, ty = _chip_of(x, y, r)
            k = cw_sem[r]
            region = gcw_ref.at[2 * tx + ty]
            return pltpu.make_async_remote_copy(
                src_ref=region, dst_ref=region, send_sem=send_sems.at[k], recv_sem=recv_sems.at[k],
                device_id=(tx, ty, c), device_id_type=MESH)

        def relay(a, q, origin, to):
            ox, oy = _chip_of(x, y, origin)
            tx, ty = _chip_of(x, y, to)
            k = ici_sem[(a, q, 3)]
            region = dsts[a].at[2 * ox + oy, piece_rows(a, q, c), :]
            return pltpu.make_async_remote_copy(
                src_ref=region, dst_ref=region, send_sem=send_sems.at[k], recv_sem=recv_sems.at[k],
                device_id=(tx, ty, c), device_id_type=MESH)

        first = [ici(a, q, r) for (a, q) in order for r in (1, 2)] + [cw_copy(r) for r in (1, 2, 3)]
        for cp in first:
            cp.start()
        passed = []
        for (a, q) in order:
            for r in (1, 2):
                ici_landed(a, q, r).wait_recv()
                if q % 2 == r - 1:
                    cp = relay(a, q, r, 3 - r)
                    cp.start()
                    passed.append(cp)
                cp = d2d(a, q, r, c)
                cp.start()
                passed.append(cp)
        for (a, q) in order:
            ici_landed(a, q, 3).wait_recv()
            cp = d2d(a, q, 3, c)
            cp.start()
            passed.append(cp)
        for r in (1, 2, 3):
            cw_landed(r).wait_recv()
        for (a, q) in order:
            for r in (1, 2, 3):
                d2d(a, q, r, 1 - c).wait_recv()
        for cp in first + passed:
            cp.wait_send()

    return pl.pallas_call(
        body,
        name="gather_weights",
        in_specs=[ANY] * (nbig + 1),
        out_specs=[ANY] * (nbig + 1),
        out_shape=[_sds(s.shape, s.dtype) for s in placed] + [_sds((N_CHIPS,) + cw8.shape, cw8.dtype)],
        input_output_aliases={a: a for a in range(nbig)},
        scratch_shapes=[pltpu.SemaphoreType.DMA((nsem,)), pltpu.SemaphoreType.DMA((nsem,))],
    )(*placed, cw8)


def _gather_late_start(placed, after, name):
    n = len(placed)
    halves = [s.shape[1] // 2 for s in placed]

    def body(*refs):
        g_refs = refs[0:n]
        send_sems, recv_sems, token = refs[n + 1], refs[n + 2], refs[-1]
        x, y, c = _place()
        j = 2 * x + y
        for a in range(n):
            mine = g_refs[a].at[j, pl.ds(pl.multiple_of(c * halves[a], 16), halves[a]), :]
            for r in (1, 2, 3):
                tx, ty = _chip_of(x, y, r)
                for to_core in (0, 1):
                    k = ((a * 3 + (r - 1)) * 2 + c) * 2 + to_core
                    pltpu.make_async_remote_copy(
                        src_ref=mine, dst_ref=mine, send_sem=send_sems.at[k], recv_sem=recv_sems.at[k],
                        device_id=(tx, ty, to_core), device_id_type=MESH).start()
        token[...] = jnp.zeros_like(token)

    hbm = lambda t: pltpu.HBM(t.shape, t.dtype)
    keep = lambda t: pltpu.with_memory_space_constraint(t, pltpu.HBM)
    nsem = 12 * n
    outs = pl.pallas_call(
        body,
        name=name,
        in_specs=[HBM] * n + [ANY],
        out_specs=(SEM, SEM, *[HBM] * n, pl.BlockSpec(memory_space=pltpu.VMEM)),
        out_shape=(pltpu.SemaphoreType.DMA((nsem,)), pltpu.SemaphoreType.DMA((nsem,)), *[hbm(p) for p in placed],
                   jax.ShapeDtypeStruct((8, 128), F32)),
        input_output_aliases={i: 2 + i for i in range(n)},
        compiler_params=pltpu.CompilerParams(has_side_effects=DATAFLOW),
    )(*[keep(p) for p in placed], after)
    return outs[0], outs[1], list(outs[2:2 + n]), outs[-1]


def _gather_late_wait(send_sems, recv_sems, thru, after, name):
    n = len(thru)
    halves = [s.shape[1] // 2 for s in thru]

    def body(*refs):
        g_refs = refs[0:n]
        send_sems, recv_sems = refs[n], refs[n + 1]
        x, y, c = _place()
        j = 2 * x + y
        for a in range(n):
            mine = g_refs[a].at[j, pl.ds(pl.multiple_of(c * halves[a], 16), halves[a]), :]
            for r in (1, 2, 3):
                tx, ty = _chip_of(x, y, r)
                for other in (0, 1):
                    k_out = ((a * 3 + (r - 1)) * 2 + c) * 2 + other
                    pltpu.make_async_remote_copy(
                        src_ref=mine, dst_ref=mine, send_sem=send_sems.at[k_out], recv_sem=recv_sems.at[k_out],
                        device_id=(tx, ty, other), device_id_type=MESH).wait_send()
                    k_in = ((a * 3 + (r - 1)) * 2 + other) * 2 + c
                    theirs = g_refs[a].at[2 * tx + ty, pl.ds(other * halves[a], halves[a]), :]
                    pltpu.make_async_remote_copy(
                        src_ref=theirs, dst_ref=theirs, send_sem=send_sems.at[k_in], recv_sem=recv_sems.at[k_in],
                        device_id=(tx, ty, other), device_id_type=MESH).wait_recv()

    hbm = lambda t: pltpu.HBM(t.shape, t.dtype)
    outs = pl.pallas_call(
        body,
        name=name,
        in_specs=[HBM] * n + [SEM, SEM, ANY],
        out_specs=[HBM] * n,
        out_shape=[hbm(t) for t in thru],
        input_output_aliases={i: i for i in range(n)},
        compiler_params=pltpu.CompilerParams(has_side_effects=DATAFLOW),
    )(*thru, send_sems, recv_sems, after)
    return list(outs)


D2D_PIECE_ROWS = 64
PAIR_SUM_TILE_BYTES = 2 * MIB


def _pair_sum(gs, gbs, chip_core, name):
    n = len(gs)
    nch, R, C = gs[0].shape
    h = R // 2
    tr = _row_tile(h, C * n, PAIR_SUM_TILE_BYTES)
    nt = h // tr
    rows = min(D2D_PIECE_ROWS, tr)

    def body(jc_ref, *refs):
        g_refs, gb_refs, p_refs, pb_refs = refs[0:n], refs[n:2 * n], refs[2 * n:3 * n], refs[3 * n:4 * n]
        got_refs, send_sems, recv_sems = refs[4 * n:5 * n], refs[5 * n], refs[5 * n + 1]
        i, j = pl.program_id(0), pl.program_id(1)
        x, y, c = _place()

        def copy(a, ti, tj, first, count):
            src_rows = pl.ds(pl.multiple_of((1 - c) * h + ti * tr + first, 16), count)
            dst_rows = pl.ds(pl.multiple_of(ti * tr + first, 16), count)
            return pltpu.make_async_remote_copy(
                src_ref=gb_refs[a].at[tj, src_rows, :], dst_ref=got_refs[a].at[tj, dst_rows, :],
                send_sem=send_sems.at[a, ti, tj], recv_sem=recv_sems.at[a, ti, tj],
                device_id=(x, y, 1 - c), device_id_type=MESH)

        @pl.when((i == 0) & (j == 0))
        def _():
            for ti in range(nt):
                for tj in range(nch):
                    for a in range(n):
                        for q in range(tr // rows):
                            copy(a, ti, tj, q * rows, rows).start()

        for a in range(n):
            copy(a, i, j, 0, tr).wait()
            s = g_refs[a][...] + got_refs[a][j, pl.ds(pl.multiple_of(i * tr, 16), tr), :].astype(F32)
            pb_refs[a][...] = s.astype(BF16)

            @pl.when(j == jc_ref[0])
            def _():
                p_refs[a][...] = s

    by_chip = pl.BlockSpec((None, tr, C), lambda i, j, jc_ref: (j, i, 0))
    outs = pl.pallas_call(
        body,
        name=name,
        grid_spec=pltpu.PrefetchScalarGridSpec(
            num_scalar_prefetch=1,
            grid=(nt, nch),
            in_specs=[pl.BlockSpec((None, tr, C), lambda i, j, jc_ref: (j, jc_ref[1] * nt + i, 0))] * n + [ANY] * n,
            out_specs=[pl.BlockSpec((tr, C), lambda i, j, jc_ref: (i, 0))] * n + [by_chip] * n,
            scratch_shapes=[pltpu.VMEM((nch, h, C), BF16)] * n + [pltpu.SemaphoreType.DMA((n, nt, nch))] * 2,
        ),
        out_shape=[_sds((h, C), F32)] * n + [_sds((nch, h, C), BF16)] * n,
        compiler_params=_params(("arbitrary", "arbitrary"), 48),
    )(chip_core, *_hbm(*gs, *gbs))
    return list(outs[:n]), list(outs[n:])


HBM = pl.BlockSpec(memory_space=pltpu.HBM)
SEM = pl.BlockSpec(memory_space=pltpu.SEMAPHORE)
DATAFLOW = pltpu.SideEffectType.DATAFLOW_SIDE_EFFECTING


def _chip_copy(p_refs, land_refs, send_sems, recv_sems, a, r, blocked):
    x, y, c = _place()
    tx, ty = _chip_of(x, y, r)
    k = a * 3 + (r - 1)
    return pltpu.make_async_remote_copy(
        src_ref=p_refs[a].at[2 * tx + ty] if blocked else p_refs[a], dst_ref=land_refs[a].at[r - 1],
        send_sem=send_sems.at[k], recv_sem=recv_sems.at[k], device_id=(tx, ty, c), device_id_type=MESH)


def _chip_exchange_start(psums, name, blocked=True):
    n = len(psums)
    lands = [lax.empty((3,) + (p.shape[1:] if blocked else p.shape), p.dtype) for p in psums]

    def body(*refs):
        p_refs, land_refs = refs[0:n], refs[n:2 * n]
        send_sems, recv_sems, token = refs[2 * n], refs[2 * n + 1], refs[-1]
        for a in range(n):
            for r in (1, 2, 3):
                _chip_copy(p_refs, land_refs, send_sems, recv_sems, a, r, blocked).start()
        token[...] = jnp.zeros_like(token)

    hbm = lambda t: pltpu.HBM(t.shape, t.dtype)
    keep = lambda t: pltpu.with_memory_space_constraint(t, pltpu.HBM)
    outs = pl.pallas_call(
        body,
        name=name,
        in_specs=[HBM] * (2 * n),
        out_specs=(SEM, SEM, *[HBM] * (2 * n), pl.BlockSpec(memory_space=pltpu.VMEM)),
        out_shape=(pltpu.SemaphoreType.DMA((3 * n,)), pltpu.SemaphoreType.DMA((3 * n,)),
                   *[hbm(p) for p in psums], *[hbm(l) for l in lands], _sds((8, 128), F32)),
        input_output_aliases={i: 2 + i for i in range(2 * n)},
        compiler_params=pltpu.CompilerParams(has_side_effects=DATAFLOW),
    )(*[keep(p) for p in psums], *[keep(l) for l in lands])
    return outs[0], outs[1], list(outs[2:2 + n]), list(outs[2 + n:2 + 2 * n]), outs[-1]


def _chip_exchange_wait(send_sems, recv_sems, p_thru, land_thru, after, name, blocked=True):
    n = len(p_thru)

    def body(*refs):
        p_refs, land_refs = refs[0:n], refs[n:2 * n]
        send_sems, recv_sems = refs[2 * n], refs[2 * n + 1]
        for a in range(n):
            for r in (1, 2, 3):
                copy = _chip_copy(p_refs, land_refs, send_sems, recv_sems, a, r, blocked)
                copy.wait_send()
                copy.wait_recv()

    hbm = lambda t: pltpu.HBM(t.shape, t.dtype)
    outs = pl.pallas_call(
        body,
        name=name,
        in_specs=[HBM] * (2 * n) + [SEM, SEM, ANY],
        out_specs=[HBM] * (2 * n),
        out_shape=[hbm(p) for p in p_thru] + [hbm(l) for l in land_thru],
        input_output_aliases={i: i for i in range(2 * n)},
        compiler_params=pltpu.CompilerParams(has_side_effects=DATAFLOW),
    )(*p_thru, *land_thru, send_sems, recv_sems, after)
    return list(outs[0:n]), list(outs[n:2 * n])


def _pair_share(fulls):
    n = len(fulls)
    halves = [f.shape[0] // 2 for f in fulls]

    def body(*refs):
        full_refs = refs[n:2 * n]
        send_sems, recv_sems = refs[2 * n:]
        x, y, c = _place()

        def half_of(a, core):
            return full_refs[a].at[pl.ds(pl.multiple_of(core * halves[a], 8), halves[a]), :]

        def remote(a, src, dst):
            return pltpu.make_async_remote_copy(
                src_ref=src, dst_ref=dst, send_sem=send_sems.at[a], recv_sem=recv_sems.at[a],
                device_id=(x, y, 1 - c), device_id_type=MESH)

        for a in range(n):
            for q in range(halves[a] // D2D_PIECE_ROWS):
                piece = full_refs[a].at[
                    pl.ds(pl.multiple_of(c * halves[a] + q * D2D_PIECE_ROWS, 8), D2D_PIECE_ROWS), :]
                remote(a, piece, piece).start()
        for a in range(n):
            remote(a, half_of(a, c), half_of(a, c)).wait_send()
            remote(a, half_of(a, 1 - c), half_of(a, 1 - c)).wait_recv()

    return pl.pallas_call(
        body,
        name="pair_share",
        in_specs=[ANY] * n,
        out_specs=[ANY] * n,
        out_shape=[_sds(f.shape, F32) for f in fulls],
        input_output_aliases={a: a for a in range(n)},
        scratch_shapes=[pltpu.SemaphoreType.DMA((n,)), pltpu.SemaphoreType.DMA((n,))],
    )(*fulls)


def _small_pair_sum(s):
    R, C = s.shape
    V = SMALL_VECTOR_ROWS

    def body(s_ref, v_ref, m_ref, sib, send_sem, recv_sem):
        x, y, c = _place()

        def to_sib(src, dst):
            return pltpu.make_async_remote_copy(
                src_ref=src, dst_ref=dst, send_sem=send_sem, recv_sem=recv_sem,
                device_id=(x, y, 1 - c), device_id_type=MESH)

        for q in range(R // 8):
            to_sib(s_ref.at[pl.ds(8 * q, 8), :], sib.at[pl.ds(8 * q, 8), :]).start()
        to_sib(s_ref, sib).wait()
        v_ref[...] = s_ref[pl.ds(0, V), :] + sib[pl.ds(0, V), :]
        m_ref[...] = (s_ref[pl.ds(V, R - V), :] + sib[pl.ds(V, R - V), :]).astype(BF16)

    return pl.pallas_call(
        body,
        name="small_pair_sum",
        in_specs=[pl.BlockSpec(memory_space=pltpu.VMEM)],
        out_specs=[pl.BlockSpec(memory_space=pltpu.VMEM)] * 2,
        out_shape=[jax.ShapeDtypeStruct((V, C), F32), jax.ShapeDtypeStruct((R - V, C), BF16)],
        scratch_shapes=[pltpu.VMEM((R, C), F32), pltpu.SemaphoreType.DMA, pltpu.SemaphoreType.DMA],
    )(s)


def _small_total(chip, own, landed):
    V, C = own[0].shape
    M = own[1].shape[0]

    def body(j_ref, v_ref, m_ref, lv_ref, lm_ref, o_ref, chips_v, chips_m):
        j = j_ref[0]
        chips_v[j] = v_ref[...]
        chips_m[j] = m_ref[...]
        for r in (1, 2, 3):
            chips_v[j ^ r] = lv_ref[r - 1]
            chips_m[j ^ r] = lm_ref[r - 1]
        o_ref[pl.ds(0, V), :] = (chips_v[0] + chips_v[1]) + (chips_v[2] + chips_v[3])
        o_ref[pl.ds(V, M), :] = (chips_m[0].astype(F32) + chips_m[1].astype(F32)) + (
            chips_m[2].astype(F32) + chips_m[3].astype(F32))

    vmem = pl.BlockSpec(memory_space=pltpu.VMEM)
    return pl.pallas_call(
        body,
        name="small_total",
        in_specs=[pl.BlockSpec(memory_space=pltpu.SMEM), vmem, vmem, vmem, vmem],
        out_specs=vmem,
        out_shape=jax.ShapeDtypeStruct((V + M, C), F32),
        scratch_shapes=[pltpu.VMEM((N_CHIPS, V, C), F32), pltpu.VMEM((N_CHIPS, M, C), BF16)],
    )(chip, own[0], own[1], landed[0], landed[1])


def _local_grads(x, target, g_pre, w_in_g, b_gate, conv_w, conv_b, w_rg_a, b_rg_a, w_rg_x, b_rg_x, lam, sinks,
                 out_weights, fwd_token, g_post, on_out_grads, on_w_in_grad):
    b_a = b_rg_a.reshape(1, D_RNN)
    b_x = b_rg_x.reshape(1, D_RNN)

    proj, ht = _proj_fwd(x, g_pre, w_in_g)
    y_rnn, z_rnn, conv, z_rnn_t = _rnn_fwd(proj, conv_w, conv_b, w_rg_a, w_rg_x, b_a, b_x, lam, fwd_token)
    bias = _attn_bias()
    y_attn, z_attn, lse = _attn_fwd(proj, sinks, bias)
    w_rnn_out, w_attn_out, w_out = out_weights(z_attn)
    dyx, dz_rnn, dz_attn, dml, dout, dbr_rnn, dbr_attn, merged_t, z_attn_t, head_small = _head(
        x, target, z_rnn, z_attn, proj, b_gate, g_post, w_rnn_out, w_attn_out, w_out)
    out_grads = [_matmul_t(z_rnn_t, dbr_rnn, "dw_rnn_out"), _matmul_t(z_attn_t, dbr_attn, "dw_attn_out"),
                 _matmul_t(merged_t, dout, "dw_out")]
    shard_rows = lambda d: d.reshape(N_CHIPS, OUT_SHARD, D_MODEL)
    token = on_out_grads([shard_rows(g) for g, _ in out_grads], [shard_rows(gb) for _, gb in out_grads])
    dq, dk, dv, dag, attn_small = _attn_bwd(proj, y_attn, lse, dz_attn, sinks, bias, token)
    drx, drg, dwa, dwx, rnn_small = _rnn_bwd(proj, conv, y_rnn, dz_rnn, conv_w, w_rg_a, w_rg_x, b_a, b_x, lam)
    dproj = [drx, drg, dq, dk, dv, dag, dml]
    token = on_w_in_grad(*_dw_in(ht, dproj))
    grad_x, dh_small = _dh_bwd(dproj, w_in_g, x, dyx, g_pre, token)
    small = jnp.concatenate([rnn_small, head_small, dh_small + attn_small,
                             dwa.reshape(64, 1024), dwx.reshape(64, 1024)], axis=0)
    return grad_x, small


ROW_LOSS = 11


def _unpack_small(s, conv_cols):
    return {
        "b_rg_a": s[0:1].reshape(1, 16, 64), "b_rg_x": s[1:2].reshape(1, 16, 64), "lru_lambda": s[2:3],
        "conv_b": s[3:4], "conv_w": s[4:8, 0:conv_cols].reshape(1, CONV_W, conv_cols),
        "post_norm_g": s[8:9], "b_gate": s[9:11].reshape(1, 2048),
        "pre_norm_g": s[16:17], "attn_sinks": s[17:18, 0:N_Q_HEADS],
        "w_rg_a": s[24:88].reshape(1, 16, 64, 64), "w_rg_x": s[88:152].reshape(1, 16, 64, 64),
    }


WEIGHTS = ["pre_norm_g", "w_in", "b_gate", "conv_w", "conv_b", "w_rg_a", "b_rg_a", "w_rg_x", "b_rg_x", "lru_lambda",
           "attn_sinks", "w_rnn_out", "w_attn_out", "w_out", "post_norm_g"]
BIG = ["w_in", "w_rnn_out", "w_attn_out", "w_out"]


def kernel(x, pre_norm_g, w_in, b_gate, conv_w, conv_b, w_rg_a, b_rg_a, w_rg_x, b_rg_x, lru_lambda, attn_sinks, w_rnn_out, w_attn_out, w_out, post_norm_g, loss_target, m_pre_norm_g, m_w_in, m_b_gate, m_conv_w, m_conv_b, m_w_rg_a, m_b_rg_a, m_w_rg_x, m_b_rg_x, m_lru_lambda, m_attn_sinks, m_w_rnn_out, m_w_attn_out, m_w_out, m_post_norm_g, v_pre_norm_g, v_w_in, v_b_gate, v_conv_w, v_conv_b, v_w_rg_a, v_b_rg_a, v_w_rg_x, v_b_rg_x, v_lru_lambda, v_attn_sinks, v_w_rnn_out, v_w_attn_out, v_w_out, v_post_norm_g):
    w = dict(pre_norm_g=pre_norm_g, w_in=w_in, b_gate=b_gate, conv_w=conv_w, conv_b=conv_b, w_rg_a=w_rg_a,
             b_rg_a=b_rg_a, w_rg_x=w_rg_x, b_rg_x=b_rg_x, lru_lambda=lru_lambda, attn_sinks=attn_sinks,
             w_rnn_out=w_rnn_out, w_attn_out=w_attn_out, w_out=w_out, post_norm_g=post_norm_g)
    m = dict(pre_norm_g=m_pre_norm_g, w_in=m_w_in, b_gate=m_b_gate, conv_w=m_conv_w, conv_b=m_conv_b, w_rg_a=m_w_rg_a,
             b_rg_a=m_b_rg_a, w_rg_x=m_w_rg_x, b_rg_x=m_b_rg_x, lru_lambda=m_lru_lambda, attn_sinks=m_attn_sinks,
             w_rnn_out=m_w_rnn_out, w_attn_out=m_w_attn_out, w_out=m_w_out, post_norm_g=m_post_norm_g)
    v = dict(pre_norm_g=v_pre_norm_g, w_in=v_w_in, b_gate=v_b_gate, conv_w=v_conv_w, conv_b=v_conv_b, w_rg_a=v_w_rg_a,
             b_rg_a=v_b_rg_a, w_rg_x=v_w_rg_x, b_rg_x=v_b_rg_x, lru_lambda=v_lru_lambda, attn_sinks=v_attn_sinks,
             w_rnn_out=v_w_rnn_out, w_attn_out=v_w_attn_out, w_out=v_w_out, post_norm_g=v_post_norm_g)
    chip = 2 * lax.axis_index("x") + lax.axis_index("y")

    chip_idx = chip.astype(jnp.int32).reshape(1)
    chip_core = jnp.stack([chip, lax.axis_index("c")]).astype(jnp.int32)
    cw8 = jnp.pad(conv_w[0], ((0, 8 - CONV_W), (0, 0)))
    placed = _place_shards([w_in[0], w_rnn_out[0], w_attn_out[0], w_out[0]], chip_idx, "place_shards")
    win_g, cw_g = _gather_weights(placed[:1], cw8)
    late_send, late_recv, late_thru, late_token = _gather_late_start(placed[1:], win_g, "gather_late_start")
    cw_g = lax.dynamic_update_slice_in_dim(cw_g, cw8[None], chip, axis=0)
    conv_w_full = jnp.transpose(cw_g[:, 0:CONV_W, :], (1, 0, 2)).reshape(CONV_W, D_RNN)

    started = {}

    def start_reduction(tag, grads, grads_b16):
        psums, psums_b16 = _pair_sum(grads, grads_b16, chip_core, "pair_sum_" + tag)
        send_sems, recv_sems, p_thru, land_thru, token = _chip_exchange_start(psums_b16, "chip_exchange_start_" + tag)
        started[tag] = (psums, send_sems, recv_sems, p_thru, land_thru)
        return token

    def end_reduction(tag, after):
        psums, send_sems, recv_sems, p_thru, land_thru = started[tag]
        _, landed = _chip_exchange_wait(send_sems, recv_sems, p_thru, land_thru, after, "chip_exchange_wait_" + tag)
        return psums, landed

    def out_weights(after):
        gathered = _gather_late_wait(late_send, late_recv, late_thru, after, "gather_late_wait")
        return [g.reshape(D_MODEL, D_MODEL) for g in gathered]

    grad_x, small = _local_grads(
        x[0], loss_target[0], pre_norm_g, win_g, b_gate, conv_w_full, conv_b, w_rg_a[0], b_rg_a[0], w_rg_x[0],
        b_rg_x[0], lru_lambda, attn_sinks[0], out_weights, late_token, post_norm_g,
        on_out_grads=lambda grads, grads_b16: start_reduction("out", grads, grads_b16),
        on_w_in_grad=lambda grad, grad_b16: start_reduction("in", [grad], [grad_b16]))

    small_chip = _small_pair_sum(small)
    small_send, small_recv, small_thru, small_land, small_token = _chip_exchange_start(
        list(small_chip), "small_exchange_start", blocked=False)

    halves = _chip_sum([end_reduction("in", small_token), end_reduction("out", small_token)], chip_core, "chip_sum")
    gbig = dict(zip(BIG, _pair_share(halves)))

    grads, delta, new_m, new_v = {}, {}, {}, {}
    updates = _adamw([[(w[n][0], gbig[n], m[n][0], v[n][0]) for n in names] for names in (BIG[:1], BIG[1:])],
                     "adamw_big")
    for n, (d, nm, nv, g) in zip(BIG, updates):
        grads[n], delta[n], new_m[n], new_v[n] = g[None], d[None], nm[None], nv[None]

    small_own, small_landed = _chip_exchange_wait(small_send, small_recv, small_thru, small_land, delta[BIG[-1]],
                                                  "small_exchange_wait", blocked=False)
    small_sum = _small_total(chip_idx, small_own, small_landed)
    total_loss = small_sum[ROW_LOSS, 0]
    gsmall = _unpack_small(small_sum, D_RNN)
    conv_shard = D_RNN // N_CHIPS
    gsmall["conv_w"] = lax.dynamic_slice_in_dim(gsmall["conv_w"], chip * conv_shard, conv_shard, axis=2)
    for n in gsmall:
        grads[n] = gsmall[n].reshape(w[n].shape)
    updates = _adamw_whole([(w[n], grads[n], m[n], v[n]) for n in gsmall], "adamw_small")
    for n, (d, nm, nv) in zip(gsmall, updates):
        delta[n], new_m[n], new_v[n] = d, nm, nv

    return (total_loss, grad_x[None], *[grads[n] for n in WEIGHTS], *[delta[n] for n in WEIGHTS],
            *[new_m[n] for n in WEIGHTS], *[new_v[n] for n in WEIGHTS])
```

```python
import functools
import math

import jax
import jax.numpy as jnp
import numpy as np
from jax import lax
from jax.experimental import pallas as pl
from jax.experimental.pallas import tpu as pltpu

F32 = jnp.float32
BF16 = jnp.bfloat16

D_MODEL = 1024
D_RNN = 1024
RNN_BLOCKS = 16
RNN_BLOCK_W = 64
CONV_W = 4
LRU_C = 8.0
N_Q_HEADS = 16
N_KV_HEADS = 4
GROUP = 4
HEAD_DIM = 64
D_KV = 256
BLOCK = 128
ALIBI_MAX_BIAS = 8.0
EPS = 1e-6
D_IN = 6656
N_CHIPS = 4
W_IN_SHARD = D_IN // N_CHIPS
OUT_SHARD = D_MODEL // N_CHIPS
ADAM_LR = 0.001
ADAM_B1 = 0.9
ADAM_B2 = 0.999
ADAM_EPS = 1e-08
ADAM_WD = 0.01
ADAM_STEP = 10
NEG_BIG = -1e30
MIB = 1 << 20

COL_RNN_X = 0
COL_RNN_GATE = 4
COL_Q = 8
COL_K = 12
COL_V = 13
COL_ATTN_GATE = 14
COL_MERGE = 18

RNN_TILE = 256
RNN_CHUNK = 512
SMALL_ROWS = 152
SMALL_VECTOR_ROWS = 24
MESH = pl.DeviceIdType.MESH


def _sds(shape, dtype):
    return pltpu.HBM(shape, dtype)


def _params(sem=None, vmem_mib=None):
    kw = {}
    if sem is not None:
        kw["dimension_semantics"] = sem
    if vmem_mib is not None:
        kw["vmem_limit_bytes"] = vmem_mib * MIB
    return pltpu.CompilerParams(**kw)


def _hbm(*arrays):
    return [pltpu.with_memory_space_constraint(a, pltpu.HBM) for a in arrays]


def _dot(a, b):
    return jnp.dot(a, b, preferred_element_type=F32)


def _dot_nt(a, b):
    return lax.dot_general(a, b, (((1,), (1,)), ((), ())), preferred_element_type=F32)


def _dot_tn(a, b):
    return lax.dot_general(a, b, (((0,), (0,)), ((), ())), preferred_element_type=F32)


def _sigmoid(x):
    return 0.5 * jnp.tanh(0.5 * x) + 0.5


def _sigmoid_small(x):
    return 1.0 / (1.0 + jnp.exp(-x))


def _softplus(x):
    return jnp.maximum(x, 0.0) + jnp.log(1.0 + jnp.exp(-jnp.abs(x)))


def _one_minus_square(a, log_a):
    return -jnp.tanh(log_a) * (a * a + 1.0)


def _proj_fwd(x, g_pre, w_in_g):
    T = x.shape[0]
    tm = min(1024, T)

    def body(x_ref, g_ref, w_ref, proj_ref, ht_ref, h_s):
        @pl.when(pl.program_id(1) == 0)
        def _():
            xv = x_ref[...]
            rstd = lax.rsqrt(jnp.mean(xv * xv, axis=-1, keepdims=True) + EPS)
            hf = (xv * rstd) * g_ref[...]
            h_s[...] = hf.astype(BF16)
            ht_ref[...] = hf.T.astype(BF16)

        proj_ref[...] = _dot(h_s[...], w_ref[...]).astype(BF16)

    return pl.pallas_call(
        body,
        name="proj_fwd",
        grid=(T // tm, N_CHIPS),
        in_specs=[
            pl.BlockSpec((tm, D_MODEL), lambda i, j: (i, 0)),
            pl.BlockSpec((1, D_MODEL), lambda i, j: (0, 0)),
            pl.BlockSpec((None, D_MODEL, W_IN_SHARD), lambda i, j: (j, 0, 0)),
        ],
        out_specs=[
            pl.BlockSpec((tm, W_IN_SHARD), lambda i, j: (i, j)),
            pl.BlockSpec((D_MODEL, tm), lambda i, j: (0, i)),
        ],
        out_shape=[_sds((T, D_IN), BF16), _sds((D_MODEL, T), BF16)],
        scratch_shapes=[pltpu.VMEM((tm, D_MODEL), BF16)],
        compiler_params=_params(("parallel", "arbitrary"), 48),
    )(*_hbm(x, g_pre, w_in_g))


def _shift_down(x, tail, s, row):
    n = x.shape[0]
    xs = pltpu.roll(x, s, 0)
    tail_t = jnp.tile(pltpu.roll(tail, s, 0), (n // 8, 1))
    return jnp.where(row < s, tail_t, xs)


def _shift_up(x, head, s, row):
    n = x.shape[0]
    xs = pltpu.roll(x, n - s, 0)
    head_t = jnp.tile(pltpu.roll(head, 8 - s, 0), (n // 8, 1))
    return jnp.where(row >= n - s, head_t, xs)


def _conv_taps(x, tail, row):
    return [_shift_down(x, tail, 3, row), _shift_down(x, tail, 2, row), _shift_down(x, tail, 1, row), x]


def _rglru_gates(c, wa, wx, ba, bx, lam):
    cb = c.astype(BF16)
    r = _sigmoid_small(_dot(cb, wa) + ba)
    i = _sigmoid(_dot(cb, wx) + bx)
    log_a = (-LRU_C) * r * _softplus(-lam)
    a = jnp.exp(log_a)
    w = _one_minus_square(a, log_a)
    inv_mult = lax.rsqrt(w)
    return cb, r, i, a, w * inv_mult, inv_mult


GATE_BLOCKS_PER_TILE = RNN_TILE // RNN_BLOCK_W
GATE_BLOCKS = pl.BlockSpec((GATE_BLOCKS_PER_TILE, RNN_BLOCK_W, RNN_BLOCK_W), lambda j, t: (j, 0, 0))


def _fill_block_diag(bd_ref, w_ref):
    bd_ref[...] = jnp.zeros_like(bd_ref)
    for a in range(GATE_BLOCKS_PER_TILE):
        lo = a * RNN_BLOCK_W
        bd_ref[lo:lo + RNN_BLOCK_W, lo:lo + RNN_BLOCK_W] = w_ref[a].astype(BF16)


SUBLANES = 8


def _scan_down(a, u, row):
    n = a.shape[0]
    s = 1
    while s < SUBLANES:
        a_sh = jnp.where(row >= s, pltpu.roll(a, s, 0), 1.0)
        u_sh = jnp.where(row >= s, pltpu.roll(u, s, 0), 0.0)
        u = a * u_sh + u
        a = a * a_sh
        s *= 2
    while s < n:
        u = jnp.concatenate([u[:s], a[s:] * u[:n - s] + u[s:]], axis=0)
        a = jnp.concatenate([a[:s], a[s:] * a[:n - s]], axis=0)
        s *= 2
    return a, u


def _scan_up(b, u, row):
    n = b.shape[0]
    s = 1
    while s < SUBLANES:
        b_sh = jnp.where(row < n - s, pltpu.roll(b, n - s, 0), 1.0)
        u_sh = jnp.where(row < n - s, pltpu.roll(u, n - s, 0), 0.0)
        u = b * u_sh + u
        b = b * b_sh
        s *= 2
    while s < n:
        u = jnp.concatenate([b[:n - s] * u[s:] + u[:n - s], u[n - s:]], axis=0)
        b = jnp.concatenate([b[:n - s] * b[s:], b[n - s:]], axis=0)
        s *= 2
    return b, u


LANES = 128


def _chunk_scan(a, u, a_s, u_s, hl_s, al_s, carry, reverse):
    n, width = a.shape
    groups = n // SUBLANES
    order = range(SUBLANES - 1, -1, -1) if reverse else range(SUBLANES)
    row = lax.broadcasted_iota(jnp.int32, (groups, LANES), 0)
    for l in range(width // LANES):
        lanes = slice(l * LANES, (l + 1) * LANES)
        a_l, u_l, hl_l, al_l = a_s.at[l], u_s.at[l], hl_s.at[l], al_s.at[l]
        a_l[...] = a[:, lanes]
        u_l[...] = u[:, lanes]
        h_loc = a_loc = None
        for r in order:
            rows = pl.ds(r, groups, stride=SUBLANES)
            a_r, u_r = a_l[rows, :], u_l[rows, :]
            h_loc, a_loc = (u_r, a_r) if h_loc is None else (a_r * h_loc + u_r, a_r * a_loc)
            hl_l[rows, :] = h_loc
            al_l[rows, :] = a_loc
        if reverse:
            a_cum, ends = _scan_up(a_loc, h_loc, row)
            ends = ends + a_cum * carry[:, lanes]
            enters = jnp.where(row == groups - 1, carry[:, lanes], pltpu.roll(ends, groups - 1, 0))
        else:
            a_cum, ends = _scan_down(a_loc, h_loc, row)
            ends = ends + a_cum * carry[:, lanes]
            enters = jnp.where(row == 0, carry[:, lanes], pltpu.roll(ends, 1, 0))
        for r in range(SUBLANES):
            rows = pl.ds(r, groups, stride=SUBLANES)
            hl_l[rows, :] = hl_l[rows, :] + al_l[rows, :] * enters
    return jnp.concatenate([hl_s[l] for l in range(width // LANES)], axis=1)


def _rnn_fwd(proj, conv_w, conv_b, w_a, w_x, b_a, b_x, lam, token):
    T = proj.shape[0]
    tc, ct = RNN_CHUNK, RNN_TILE
    nt = T // tc

    def body(x_ref, rg_ref, cw_ref, cb_ref, wa_ref, wx_ref, ba_ref, bx_ref, lam_ref, token_ref, h_ref, z_ref, c_ref,
             zt_ref, xtail, hcarry, wa_s, wx_s, a_s, u_s, hl_s, al_s):
        @pl.when(pl.program_id(1) == 0)
        def _():
            xtail[...] = jnp.zeros_like(xtail)
            hcarry[...] = jnp.zeros_like(hcarry)
            _fill_block_diag(wa_s, wa_ref)
            _fill_block_diag(wx_s, wx_ref)

        row = lax.broadcasted_iota(jnp.int32, (tc, ct), 0)
        x = x_ref[...].astype(F32)
        taps = _conv_taps(x, xtail[...], row)
        c = cb_ref[...] + cw_ref[pl.ds(0, 1), :] * taps[0]
        for k in range(1, CONV_W):
            c = c + cw_ref[pl.ds(k, 1), :] * taps[k]
        xtail[...] = x[tc - 8:, :]
        c_ref[...] = c
        _, _, i, a, mult, _ = _rglru_gates(c, wa_s[...], wx_s[...], ba_ref[...], bx_ref[...], lam_ref[...])
        h = _chunk_scan(a, mult * (i * c), a_s, u_s, hl_s, al_s, hcarry[...], reverse=False)
        h_ref[...] = h
        hcarry[...] = h_ref[pl.ds(tc - 1, 1), :]
        rg = rg_ref[...].astype(F32)
        z = h * (rg * _sigmoid(rg))
        z_ref[...] = z.astype(BF16)
        zt_ref[...] = z.T.astype(BF16)

    col = lambda off: (lambda j, t: (t, off + j))
    vec = pl.BlockSpec((1, ct), lambda j, t: (0, j))
    return pl.pallas_call(
        body,
        name="rnn_fwd",
        grid=(D_RNN // ct, nt),
        in_specs=[
            pl.BlockSpec((tc, ct), col(COL_RNN_X)),
            pl.BlockSpec((tc, ct), col(COL_RNN_GATE)),
            pl.BlockSpec((CONV_W, ct), lambda j, t: (0, j)),
            vec, GATE_BLOCKS, GATE_BLOCKS, vec, vec, vec,
            pl.BlockSpec((8, 128), lambda j, t: (0, 0)),
        ],
        out_specs=[pl.BlockSpec((tc, ct), lambda j, t: (t, j))] * 3 + [pl.BlockSpec((ct, tc), lambda j, t: (j, t))],
        out_shape=[_sds((T, D_RNN), F32), _sds((T, D_RNN), BF16), _sds((T, D_RNN), F32), _sds((D_RNN, T), BF16)],
        scratch_shapes=[pltpu.VMEM((8, ct), F32), pltpu.VMEM((1, ct), F32)] + [pltpu.VMEM((ct, ct), BF16)] * 2 + [
            pltpu.VMEM((ct // LANES, tc, LANES), F32)] * 4,
        compiler_params=_params(("parallel", "arbitrary"), 32),
    )(*_hbm(proj, proj, conv_w, conv_b, w_a, w_x, b_a, b_x, lam, token))


def _rnn_bwd(proj, conv, y_rnn, dz_rnn, conv_w, w_a, w_x, b_a, b_x, lam):
    T = proj.shape[0]
    tc, ct = RNN_CHUNK, RNN_TILE
    nt = T // tc
    hb = tc // 8

    def body(x_ref, c_ref, rg_ref, h_ref, hh_ref, dz_ref, cw_ref, wa_ref, wx_ref, ba_ref, bx_ref, lam_ref,
             dx_ref, drg_ref, dwa_ref, dwx_ref, sm_ref, lam_carry, a_carry, dc_head, wa_s, wx_s, dwa_s, dwx_s,
             b_s, dy_s, hl_s, al_s):
        t = pl.program_id(1)
        first_chunk = t == nt - 1

        @pl.when(t == 0)
        def _():
            lam_carry[...] = jnp.zeros_like(lam_carry)
            a_carry[...] = jnp.zeros_like(a_carry)
            dc_head[...] = jnp.zeros_like(dc_head)
            dwa_s[...] = jnp.zeros_like(dwa_s)
            dwx_s[...] = jnp.zeros_like(dwx_s)
            sm_ref[...] = jnp.zeros_like(sm_ref)
            _fill_block_diag(wa_s, wa_ref)
            _fill_block_diag(wx_s, wx_ref)

        row = lax.broadcasted_iota(jnp.int32, (tc, ct), 0)
        keep = jnp.where(first_chunk, 0.0, 1.0)
        x = x_ref[...].astype(F32)
        c = c_ref[...]
        lam = lam_ref[...]
        cb, r, i, a, mult, inv_mult = _rglru_gates(c, wa_s[...], wx_s[...], ba_ref[...], bx_ref[...], lam)
        h = h_ref[...]
        h_prev = _shift_down(h, hh_ref[...] * keep, 1, row)
        rg = rg_ref[...].astype(F32)
        dz = dz_ref[...]
        sg = _sigmoid(rg)
        drg_ref[...] = (dz * h * (sg * (1.0 + rg * (1.0 - sg)))).astype(BF16)
        dy = dz * (rg * sg)
        b = jnp.where(row >= tc - 1, a_carry[pl.ds(0, 1), :], pltpu.roll(a, tc - 1, 0))
        lt = _chunk_scan(b, dy, b_s, dy_s, hl_s, al_s, lam_carry[pl.ds(0, 1), :], reverse=True)
        lam_carry[...] = lt[0:8, :]
        a_carry[...] = a[0:8, :]
        ic = i * c
        dmult = lt * ic
        di = lt * mult * c
        dc = lt * mult * i
        dlog_a = a * (lt * h_prev - dmult * a * inv_mult)
        sp = _softplus(-lam)
        dpre_r = dlog_a * ((-LRU_C) * sp) * (r * (1.0 - r))
        dpre_i = di * (i * (1.0 - i))
        dlam_row = jnp.sum(dlog_a * r, axis=0, keepdims=True) * (LRU_C * _sigmoid(-lam))
        dpr_b = dpre_r.astype(BF16)
        dpi_b = dpre_i.astype(BF16)
        dwa_s[...] += _dot_tn(cb, dpr_b)
        dwx_s[...] += _dot_tn(cb, dpi_b)
        dc = dc + _dot_nt(dpr_b, wa_s[...]) + _dot_nt(dpi_b, wx_s[...])
        head = dc_head[...]
        dx = cw_ref[pl.ds(3, 1), :] * dc
        sm_ref[pl.ds(4 + 3, 1), :] += jnp.sum(dc * x, axis=0, keepdims=True)
        for m in range(1, CONV_W):
            up = _shift_up(dc, head, m, row)
            dx = dx + cw_ref[pl.ds(3 - m, 1), :] * up
            sm_ref[pl.ds(4 + 3 - m, 1), :] += jnp.sum(up * x, axis=0, keepdims=True)
        dx_ref[...] = dx.astype(BF16)
        dc_head[...] = dc[0:8, :]
        sm_ref[pl.ds(0, 1), :] += jnp.sum(dpre_r, axis=0, keepdims=True)
        sm_ref[pl.ds(1, 1), :] += jnp.sum(dpre_i, axis=0, keepdims=True)
        sm_ref[pl.ds(2, 1), :] += dlam_row
        sm_ref[pl.ds(3, 1), :] += jnp.sum(dc, axis=0, keepdims=True)

        @pl.when(first_chunk)
        def _():
            for k in range(GATE_BLOCKS_PER_TILE):
                lo = k * RNN_BLOCK_W
                dwa_ref[k] = dwa_s[lo:lo + RNN_BLOCK_W, lo:lo + RNN_BLOCK_W]
                dwx_ref[k] = dwx_s[lo:lo + RNN_BLOCK_W, lo:lo + RNN_BLOCK_W]

    rev = lambda off: (lambda j, t: (nt - 1 - t, off + j))
    halo = lambda off: (lambda j, t: (jnp.maximum((nt - 1 - t) * hb - 1, 0), off + j))
    vec = pl.BlockSpec((1, ct), lambda j, t: (0, j))
    mat = GATE_BLOCKS
    return pl.pallas_call(
        body,
        name="rnn_bwd",
        grid=(D_RNN // ct, nt),
        in_specs=[
            pl.BlockSpec((tc, ct), rev(COL_RNN_X)),
            pl.BlockSpec((tc, ct), rev(0)),
            pl.BlockSpec((tc, ct), rev(COL_RNN_GATE)),
            pl.BlockSpec((tc, ct), rev(0)),
            pl.BlockSpec((8, ct), halo(0)),
            pl.BlockSpec((tc, ct), rev(0)),
            pl.BlockSpec((CONV_W, ct), lambda j, t: (0, j)),
            mat, mat, vec, vec, vec,
        ],
        out_specs=[
            pl.BlockSpec((tc, ct), rev(0)),
            pl.BlockSpec((tc, ct), rev(0)),
            mat, mat,
            pl.BlockSpec((8, ct), lambda j, t: (0, j)),
        ],
        out_shape=[_sds((T, D_RNN), BF16), _sds((T, D_RNN), BF16), _sds(w_a.shape, F32), _sds(w_x.shape, F32),
                   _sds((8, D_RNN), F32)],
        scratch_shapes=[pltpu.VMEM((8, ct), F32)] * 3 + [pltpu.VMEM((ct, ct), BF16)] * 2 + [
            pltpu.VMEM((ct, ct), F32)] * 2 + [pltpu.VMEM((ct // LANES, tc, LANES), F32)] * 4,
        compiler_params=_params(("parallel", "arbitrary"), 32),
    )(*_hbm(proj, conv, proj, y_rnn, y_rnn, dz_rnn, conv_w, w_a, w_x, b_a, b_x, lam))


def _attn_bias():
    qi = np.arange(BLOCK)[:, None]
    kj = np.arange(BLOCK)[None, :]
    dist_cur = (qi - kj).astype(np.float32)
    slopes = np.float32(2.0) ** (-ALIBI_MAX_BIAS * np.arange(1, N_Q_HEADS + 1, dtype=np.float32) / N_Q_HEADS)
    slopes = slopes[:, None, None]
    prev = np.where(kj > qi, -slopes * (dist_cur + np.float32(BLOCK)), np.float32(NEG_BIG))
    cur = np.where(kj <= qi, -slopes * dist_cur, np.float32(NEG_BIG))
    later = np.concatenate([prev, cur], axis=-1)
    first = np.concatenate([np.full_like(prev, NEG_BIG), cur], axis=-1)
    return jnp.asarray(np.stack([first, later]).astype(np.float32))


def _attn_exps(s_prev, s_cur, sink, bias):
    s_prev = s_prev + bias[:, 0:BLOCK]
    s_cur = s_cur + bias[:, BLOCK:2 * BLOCK]
    m = jnp.maximum(jnp.max(jnp.maximum(s_prev, s_cur), axis=-1, keepdims=True), sink)
    p_prev = jnp.exp(s_prev - m)
    p_cur = jnp.exp(s_cur - m)
    total = jnp.sum(p_prev + p_cur, axis=-1, keepdims=True) + jnp.exp(sink - m)
    return p_prev, p_cur, 1.0 / total, m + jnp.log(total)


def _attn_probs(s_prev, s_cur, sink, bias, lse):
    p_prev = jnp.exp((s_prev + bias[:, 0:BLOCK]) - lse)
    p_cur = jnp.exp((s_cur + bias[:, BLOCK:2 * BLOCK]) - lse)
    return p_prev, p_cur, jnp.exp(sink - lse)


def _stack_heads(ref_or_val, hk, dtype):
    parts = [ref_or_val[:, (GROUP * hk + g) * HEAD_DIM:(GROUP * hk + g + 1) * HEAD_DIM] for g in range(GROUP)]
    return jnp.concatenate(parts, axis=0).astype(dtype)


ATTN_SCALE = HEAD_DIM ** -0.5


def _bias_spec():
    return pl.BlockSpec((None, N_Q_HEADS, BLOCK, 2 * BLOCK), lambda i: (jnp.minimum(i, 1), 0, 0, 0))


def _attn_fwd(proj, sinks, bias):
    T = proj.shape[0]
    nb = T // BLOCK

    def body(sink_ref, bias_ref, q_ref, kp_ref, kc_ref, vp_ref, vc_ref, ag0_ref, ag1_ref, y_ref, z_ref, lse_ref):
        kvs = [slice(hk * HEAD_DIM, (hk + 1) * HEAD_DIM) for hk in range(N_KV_HEADS)]
        qgs = [(_stack_heads(q_ref, hk, F32) * ATTN_SCALE).astype(BF16) for hk in range(N_KV_HEADS)]
        s_prev = [_dot_nt(qgs[hk], kp_ref[:, kvs[hk]].astype(BF16)) for hk in range(N_KV_HEADS)]
        s_cur = [_dot_nt(qgs[hk], kc_ref[:, kvs[hk]].astype(BF16)) for hk in range(N_KV_HEADS)]
        for hk in range(N_KV_HEADS):
            pp, pc, invs = [], [], []
            for g in range(GROUP):
                h = GROUP * hk + g
                rows = slice(g * BLOCK, (g + 1) * BLOCK)
                p_prev, p_cur, inv, lse = _attn_exps(s_prev[hk][rows], s_cur[hk][rows], sink_ref[h], bias_ref[h])
                pp.append(p_prev.astype(BF16))
                pc.append(p_cur.astype(BF16))
                invs.append(inv)
                lse_ref[:, h:h + 1] = lse
            og = _dot(jnp.concatenate(pp, axis=0), vp_ref[:, kvs[hk]].astype(BF16)) + _dot(
                jnp.concatenate(pc, axis=0), vc_ref[:, kvs[hk]].astype(BF16))
            for g in range(GROUP):
                h = GROUP * hk + g
                y_ref[:, h * HEAD_DIM:(h + 1) * HEAD_DIM] = og[g * BLOCK:(g + 1) * BLOCK] * invs[g]
        ag = jnp.concatenate([ag0_ref[...], ag1_ref[...]], axis=1).astype(F32)
        z_ref[...] = (y_ref[...] * (ag * _sigmoid(ag))).astype(BF16)

    prev = lambda c: (lambda i: (jnp.maximum(i - 1, 0), c))
    cur = lambda c: (lambda i: (i, c))
    return pl.pallas_call(
        body,
        name="attn_fwd",
        grid=(nb,),
        in_specs=[
            pl.BlockSpec(memory_space=pltpu.SMEM),
            _bias_spec(),
            pl.BlockSpec((BLOCK, 1024), lambda i: (i, COL_Q // 4)),
            pl.BlockSpec((BLOCK, D_KV), prev(COL_K)),
            pl.BlockSpec((BLOCK, D_KV), cur(COL_K)),
            pl.BlockSpec((BLOCK, D_KV), prev(COL_V)),
            pl.BlockSpec((BLOCK, D_KV), cur(COL_V)),
            pl.BlockSpec((BLOCK, 512), lambda i: (i, COL_ATTN_GATE // 2)),
            pl.BlockSpec((BLOCK, 512), lambda i: (i, COL_ATTN_GATE // 2 + 1)),
        ],
        out_specs=[pl.BlockSpec((BLOCK, 1024), lambda i: (i, 0)), pl.BlockSpec((BLOCK, 1024), lambda i: (i, 0)),
                   pl.BlockSpec((BLOCK, N_Q_HEADS), lambda i: (i, 0))],
        out_shape=[_sds((T, 1024), F32), _sds((T, 1024), BF16), _sds((T, N_Q_HEADS), F32)],
        compiler_params=_params(("arbitrary",), 32),
    )(sinks, *_hbm(bias, proj, proj, proj, proj, proj, proj, proj))


def _attn_bwd(proj, y_attn, lse, dz_attn, sinks, bias, token):
    T = proj.shape[0]
    nb = T // BLOCK

    def body(sink_ref, bias_ref, q_ref, kp_ref, kc_ref, vp_ref, vc_ref, ag0_ref, ag1_ref, y_ref, lse_ref, dz_ref,
             token_ref, dq_ref, dk_ref, dv_ref, dag_ref, ds_ref, dy_s):
        i = pl.program_id(0)

        @pl.when(i == 0)
        def _():
            ds_ref[...] = jnp.zeros_like(ds_ref)

        lane = lax.broadcasted_iota(jnp.int32, (8, 128), 1)
        sub = lax.broadcasted_iota(jnp.int32, (8, 128), 0)
        ag = jnp.concatenate([ag0_ref[...], ag1_ref[...]], axis=1).astype(F32)
        dz = dz_ref[...]
        sg = _sigmoid(ag)
        dag_ref[...] = (dz * y_ref[...] * (sg * (1.0 + ag * (1.0 - sg)))).astype(BF16)
        dy_s[...] = dz * (ag * sg)
        r_cur = pl.multiple_of(i * BLOCK, BLOCK)
        r_prev = pl.multiple_of(jnp.maximum(i - 1, 0) * BLOCK, BLOCK)
        dk_cur, dv_cur, dk_prev, dv_prev = [], [], [], []
        ds_acc = jnp.zeros((8, 128), F32)
        for hk in range(N_KV_HEADS):
            ks = slice(hk * HEAD_DIM, (hk + 1) * HEAD_DIM)
            qg = (_stack_heads(q_ref, hk, F32) * ATTN_SCALE).astype(BF16)
            dog = _stack_heads(dy_s, hk, F32)
            og = _stack_heads(y_ref, hk, F32)
            dog_b = dog.astype(BF16)
            kp = kp_ref[:, ks].astype(BF16)
            kc = kc_ref[:, ks].astype(BF16)
            vp = vp_ref[:, ks].astype(BF16)
            vc = vc_ref[:, ks].astype(BF16)
            s_prev = _dot_nt(qg, kp)
            s_cur = _dot_nt(qg, kc)
            dp_prev = _dot_nt(dog_b, vp)
            dp_cur = _dot_nt(dog_b, vc)
            dvec = jnp.sum(dog * og, axis=-1, keepdims=True)
            pp, pc, dsp, dsc = [], [], [], []
            for g in range(GROUP):
                h = GROUP * hk + g
                rows = slice(g * BLOCK, (g + 1) * BLOCK)
                p_prev, p_cur, p_sink = _attn_probs(
                    s_prev[rows], s_cur[rows], sink_ref[h], bias_ref[h], lse_ref[:, h:h + 1])
                d_h = dvec[rows]
                pp.append(p_prev.astype(BF16))
                pc.append(p_cur.astype(BF16))
                dsp.append((p_prev * (dp_prev[rows] - d_h)).astype(BF16))
                dsc.append((p_cur * (dp_cur[rows] - d_h)).astype(BF16))
                dsink = -jnp.sum(p_sink * d_h, axis=0, keepdims=True)
                ds_acc = ds_acc + jnp.where(jnp.logical_and(lane == h, sub == 1), dsink, 0.0)
            pp = jnp.concatenate(pp, axis=0)
            pc = jnp.concatenate(pc, axis=0)
            dsp = jnp.concatenate(dsp, axis=0)
            dsc = jnp.concatenate(dsc, axis=0)
            dqg = (_dot(dsp, kp) + _dot(dsc, kc)) * ATTN_SCALE
            for g in range(GROUP):
                h = GROUP * hk + g
                dq_ref[:, h * HEAD_DIM:(h + 1) * HEAD_DIM] = dqg[g * BLOCK:(g + 1) * BLOCK].astype(BF16)
            dk_ref[pl.ds(r_cur, BLOCK), ks] = _dot_tn(dsc, qg)
            dv_ref[pl.ds(r_cur, BLOCK), ks] = _dot_tn(pc, dog_b)
            dk_prev.append(_dot_tn(dsp, qg))
            dv_prev.append(_dot_tn(pp, dog_b))
        ds_ref[:, 0:128] += ds_acc

        @pl.when(i > 0)
        def _():
            for hk in range(N_KV_HEADS):
                ks = slice(hk * HEAD_DIM, (hk + 1) * HEAD_DIM)
                dk_ref[pl.ds(r_prev, BLOCK), ks] += dk_prev[hk]
                dv_ref[pl.ds(r_prev, BLOCK), ks] += dv_prev[hk]

    prev = lambda c: (lambda i: (jnp.maximum(i - 1, 0), c))
    cur = lambda c: (lambda i: (i, c))
    blk = pl.BlockSpec((BLOCK, 1024), lambda i: (i, 0))
    whole = pl.BlockSpec((T, D_KV), lambda i: (0, 0))
    return pl.pallas_call(
        body,
        name="attn_bwd",
        grid=(nb,),
        in_specs=[
            pl.BlockSpec(memory_space=pltpu.SMEM),
            _bias_spec(),
            pl.BlockSpec((BLOCK, 1024), lambda i: (i, COL_Q // 4)),
            pl.BlockSpec((BLOCK, D_KV), prev(COL_K)),
            pl.BlockSpec((BLOCK, D_KV), cur(COL_K)),
            pl.BlockSpec((BLOCK, D_KV), prev(COL_V)),
            pl.BlockSpec((BLOCK, D_KV), cur(COL_V)),
            pl.BlockSpec((BLOCK, 512), lambda i: (i, COL_ATTN_GATE // 2)),
            pl.BlockSpec((BLOCK, 512), lambda i: (i, COL_ATTN_GATE // 2 + 1)),
            blk,
            pl.BlockSpec((BLOCK, N_Q_HEADS), lambda i: (i, 0)),
            blk,
            pl.BlockSpec((8, 128), lambda i: (0, 0)),
        ],
        out_specs=[blk, whole, whole, blk, pl.BlockSpec((8, 1024), lambda i: (0, 0))],
        out_shape=[_sds((T, 1024), BF16), _sds((T, D_KV), F32), _sds((T, D_KV), F32), _sds((T, 1024), BF16),
                   _sds((8, 1024), F32)],
        scratch_shapes=[pltpu.VMEM((BLOCK, 1024), F32)],
        compiler_params=_params(("arbitrary",), 48),
    )(sinks, *_hbm(bias, proj, proj, proj, proj, proj, proj, proj, y_attn, lse, dz_attn, token))


def _head(x, target, z_rnn, z_attn, proj, b_gate, g_post, w_rnn_out, w_attn_out, w_out):
    T = x.shape[0]
    tm = 256

    def body(x_ref, t_ref, zr_ref, za_ref, ml0_ref, ml1_ref, ml2_ref, ml3_ref, bg_ref, gp_ref, wr_ref, wa_ref, wo_ref,
             dyx_ref, dzr_ref, dza_ref, dml_ref, dout_ref, dbr_ref, dba_ref, mt_ref, zat_ref, sm_ref):
        @pl.when(pl.program_id(0) == 0)
        def _():
            sm_ref[...] = jnp.zeros_like(sm_ref)

        wr, wa, wo = wr_ref[...], wa_ref[...], wo_ref[...]
        br_rnn = _dot(zr_ref[...], wr)
        br_attn = _dot(za_ref[...], wa)
        zat_ref[...] = za_ref[...].astype(F32).T.astype(BF16)
        ml_rnn = jnp.concatenate([ml0_ref[...], ml1_ref[...]], axis=1).astype(F32)
        ml_attn = jnp.concatenate([ml2_ref[...], ml3_ref[...]], axis=1).astype(F32)
        g_rnn = _sigmoid(ml_rnn + bg_ref[:, 0:D_MODEL])
        g_attn = _sigmoid(ml_attn + bg_ref[:, D_MODEL:2 * D_MODEL])
        merged = g_rnn * br_rnn + g_attn * br_attn
        mb = merged.astype(BF16)
        mt_ref[...] = merged.T.astype(BF16)
        out = _dot(mb, wo)
        rstd = lax.rsqrt(jnp.mean(out * out, axis=-1, keepdims=True) + EPS)
        n = out * rstd
        gp = gp_ref[...]
        err = (x_ref[...] + n * gp) - t_ref[...]
        sm_ref[pl.ds(3, 1), :] += 0.5 * jnp.sum(jnp.mean(err * err, axis=-1, keepdims=True), axis=0, keepdims=True)
        dy = err * (1.0 / D_MODEL)
        dyx_ref[...] = dy
        sm_ref[pl.ds(0, 1), :] += jnp.sum(dy * n, axis=0, keepdims=True)
        dn = dy * gp
        dout = (rstd * (dn - n * jnp.mean(dn * n, axis=-1, keepdims=True))).astype(BF16)
        dout_ref[...] = dout
        dmerged = _dot_nt(dout, wo)
        dml_r = (dmerged * br_rnn) * (g_rnn * (1.0 - g_rnn))
        dml_a = (dmerged * br_attn) * (g_attn * (1.0 - g_attn))
        dml_ref[:, 0:D_MODEL] = dml_r.astype(BF16)
        dml_ref[:, D_MODEL:2 * D_MODEL] = dml_a.astype(BF16)
        sm_ref[pl.ds(1, 1), :] += jnp.sum(dml_r, axis=0, keepdims=True)
        sm_ref[pl.ds(2, 1), :] += jnp.sum(dml_a, axis=0, keepdims=True)
        dbr = (dmerged * g_rnn).astype(BF16)
        dba = (dmerged * g_attn).astype(BF16)
        dbr_ref[...] = dbr
        dba_ref[...] = dba
        dzr_ref[...] = _dot_nt(dbr, wr)
        dza_ref[...] = _dot_nt(dba, wa)

    tile = pl.BlockSpec((tm, D_MODEL), lambda i: (i, 0))
    wspec = pl.BlockSpec((D_MODEL, D_MODEL), lambda i: (0, 0))
    ml = lambda q: pl.BlockSpec((tm, 512), lambda i: (i, COL_MERGE // 2 + q))
    return pl.pallas_call(
        body,
        name="head",
        grid=(T // tm,),
        in_specs=[
            tile, tile, tile, tile,
            ml(0), ml(1), ml(2), ml(3),
            pl.BlockSpec((1, 2 * D_MODEL), lambda i: (0, 0)),
            pl.BlockSpec((1, D_MODEL), lambda i: (0, 0)),
            wspec, wspec, wspec,
        ],
        out_specs=[
            tile, tile, tile,
            pl.BlockSpec((tm, 2 * D_MODEL), lambda i: (i, 0)),
            tile, tile, tile,
            pl.BlockSpec((D_MODEL, tm), lambda i: (0, i)), pl.BlockSpec((D_MODEL, tm), lambda i: (0, i)),
            pl.BlockSpec((8, D_MODEL), lambda i: (0, 0)),
        ],
        out_shape=[
            _sds((T, D_MODEL), F32), _sds((T, D_MODEL), F32), _sds((T, D_MODEL), F32),
            _sds((T, 2 * D_MODEL), BF16),
            _sds((T, D_MODEL), BF16), _sds((T, D_MODEL), BF16), _sds((T, D_MODEL), BF16),
            _sds((D_MODEL, T), BF16), _sds((D_MODEL, T), BF16),
            _sds((8, D_MODEL), F32),
        ],
        compiler_params=_params(("arbitrary",), 56),
    )(*_hbm(x, target, z_rnn, z_attn, proj, proj, proj, proj, b_gate, g_post, w_rnn_out, w_attn_out, w_out))


def _matmul_t(at, b, name):
    M, T = at.shape
    N = b.shape[1]
    tk = min(1024, T)
    nt = T // tk

    def body(a_ref, b_ref, o_ref, ob_ref):
        @pl.when(pl.program_id(0) == 0)
        def _():
            o_ref[...] = jnp.zeros_like(o_ref)

        o_ref[...] += _dot(a_ref[...], b_ref[...])

        @pl.when(pl.program_id(0) == nt - 1)
        def _():
            ob_ref[...] = o_ref[...].astype(BF16)

    whole = pl.BlockSpec((M, N), lambda t: (0, 0))
    return pl.pallas_call(
        body,
        name=name,
        grid=(nt,),
        in_specs=[pl.BlockSpec((M, tk), lambda t: (0, t)), pl.BlockSpec((tk, N), lambda t: (t, 0))],
        out_specs=[whole, whole],
        out_shape=[_sds((M, N), F32), _sds((M, N), BF16)],
        compiler_params=_params(("arbitrary",), 48),
    )(*_hbm(at, b))


DPROJ_WIDTHS = (D_RNN, D_RNN, 1024, D_KV, D_KV, 1024, 2 * D_MODEL)


def _dproj_segments():
    segs, start = [[] for _ in range(N_CHIPS)], 0
    for p, width in enumerate(DPROJ_WIDTHS):
        for c in range(N_CHIPS):
            lo, hi = max(start, c * W_IN_SHARD), min(start + width, (c + 1) * W_IN_SHARD)
            if lo < hi:
                segs[c].append((p, lo - start, hi - start, lo - c * W_IN_SHARD, hi - c * W_IN_SHARD))
        start += width
    return segs


def _dh_bwd(pieces, w_in_g, x, dyx, g_pre, token):
    T = x.shape[0]
    tm = min(512, T)
    n = len(pieces)
    segs = _dproj_segments()

    def body(*refs):
        p_refs, w_hbm, x_ref, dyx_ref, g_ref = refs[0:n], refs[n], refs[n + 1], refs[n + 2], refs[n + 3]
        gx_ref, dg_ref, w_ref, w_sems = refs[n + 5], refs[n + 6], refs[n + 7], refs[n + 8]
        first = pl.program_id(0) == 0
        w_copies = [pltpu.make_async_copy(w_hbm.at[c], w_ref.at[c], w_sems.at[c]) for c in range(N_CHIPS)]

        @pl.when(first)
        def _():
            for cp in w_copies:
                cp.start()
            dg_ref[...] = jnp.zeros_like(dg_ref)

        dh = None
        for c in range(N_CHIPS):
            pl.when(first)(w_copies[c].wait)
            for p, a0, a1, u0, u1 in segs[c]:
                part = _dot_nt(p_refs[p][:, a0:a1].astype(BF16), w_ref[c, :, u0:u1])
                dh = part if dh is None else dh + part
        xv = x_ref[...]
        rstd = lax.rsqrt(jnp.mean(xv * xv, axis=-1, keepdims=True) + EPS)
        nx = xv * rstd
        dhg = dh * g_ref[...]
        gx_ref[...] = dyx_ref[...] + rstd * (dhg - nx * jnp.mean(dhg * nx, axis=-1, keepdims=True))
        dg_ref[pl.ds(0, 1), :] += jnp.sum(dh * nx, axis=0, keepdims=True)

    tile = pl.BlockSpec((tm, D_MODEL), lambda i: (i, 0))
    return pl.pallas_call(
        body,
        name="dh_bwd",
        grid=(T // tm,),
        in_specs=[pl.BlockSpec((tm, w), lambda i: (i, 0)) for w in DPROJ_WIDTHS] + [
            ANY, tile, tile,
            pl.BlockSpec((1, D_MODEL), lambda i: (0, 0)),
            pl.BlockSpec((8, 128), lambda i: (0, 0)),
        ],
        out_specs=[tile, pl.BlockSpec((8, D_MODEL), lambda i: (0, 0))],
        out_shape=[_sds((T, D_MODEL), F32), _sds((8, D_MODEL), F32)],
        scratch_shapes=[pltpu.VMEM(w_in_g.shape, BF16), pltpu.SemaphoreType.DMA((N_CHIPS,))],
        compiler_params=_params(("arbitrary",), 56),
    )(*_hbm(*pieces, w_in_g, x, dyx, g_pre, token))


def _dw_in(ht, pieces):
    T = ht.shape[1]
    tk = min(1024, T)
    nt = T // tk
    n = len(pieces)
    segs = _dproj_segments()

    def body(*refs):
        h_ref, p_refs, o_ref, ob_ref = refs[0], refs[1:n + 1], refs[n + 1], refs[n + 2]

        @pl.when(pl.program_id(1) == 0)
        def _():
            o_ref[...] = jnp.zeros_like(o_ref)

        for c in range(N_CHIPS):
            @pl.when(pl.program_id(0) == c)
            def _():
                for p, a0, a1, u0, u1 in segs[c]:
                    o_ref[:, u0:u1] += _dot(h_ref[...], p_refs[p][:, a0:a1].astype(BF16))

        @pl.when(pl.program_id(1) == nt - 1)
        def _():
            ob_ref[...] = o_ref[...].astype(BF16)

    def piece_spec(p):
        chips = [c for c in range(N_CHIPS) if any(s[0] == p for s in segs[c])]

        def index(c, t):
            used = functools.reduce(jnp.logical_or, [c == k for k in chips])
            return (jnp.where(used, t, 0), 0)

        return pl.BlockSpec((tk, DPROJ_WIDTHS[p]), index)

    return pl.pallas_call(
        body,
        name="dw_in",
        grid=(N_CHIPS, nt),
        in_specs=[pl.BlockSpec((D_MODEL, tk), lambda c, t: (0, t))] + [piece_spec(p) for p in range(n)],
        out_specs=[pl.BlockSpec((None, D_MODEL, W_IN_SHARD), lambda c, t: (c, 0, 0))] * 2,
        out_shape=[_sds((N_CHIPS, D_MODEL, W_IN_SHARD), F32), _sds((N_CHIPS, D_MODEL, W_IN_SHARD), BF16)],
        compiler_params=_params(("parallel", "arbitrary"), 56),
    )(*_hbm(ht, *pieces))


ELEMENTWISE_TILE_BYTES = MIB


def _row_tile(rows, cols, limit=ELEMENTWISE_TILE_BYTES):
    if rows * cols * 4 <= limit:
        return rows
    for t in (512, 256, 128, 64, 32, 16, 8):
        if rows % t == 0 and t * cols * 4 <= limit:
            return t
    return rows


def _group_tiles(groups):
    tiles = [_row_tile(g[0].shape[0], g[0].shape[1] * len(g)) for g in groups]
    steps = max(g[0].shape[0] // t for g, t in zip(groups, tiles))
    return steps, [g[0].shape[0] // steps for g in groups]


def _chip_sum(groups, chip_core, name):
    ps = [p for group_ps, _ in groups for p in group_ps]
    gots = [g for _, group_gots in groups for g in group_gots]
    n = len(ps)
    steps, group_rows = _group_tiles([group_ps for group_ps, _ in groups])
    rows = [tr for (group_ps, _), tr in zip(groups, group_rows) for _ in group_ps]

    def body(jc_ref, *refs):
        for a in range(n):
            p_ref, g0_ref, g1_ref, g2_ref, o_ref = refs[a], refs[n + 3 * a], refs[n + 3 * a + 1], refs[n + 3 * a + 2], \
                refs[4 * n + a]
            o_ref[...] = ((p_ref[...] + g0_ref[...].astype(F32)) + g1_ref[...].astype(F32)) + g2_ref[...].astype(F32)

    tile = lambda p, tr: pl.BlockSpec((tr, p.shape[1]), lambda i, jc_ref: (i, 0))
    rel = lambda p, tr, r: pl.BlockSpec((None, tr, p.shape[1]), lambda i, jc_ref: (r, i, 0))
    half = lambda p, tr: pl.BlockSpec((tr, p.shape[1]), lambda i, jc_ref: (jc_ref[1] * steps + i, 0))
    outs = pl.pallas_call(
        body,
        name=name,
        grid_spec=pltpu.PrefetchScalarGridSpec(
            num_scalar_prefetch=1,
            grid=(steps,),
            in_specs=[tile(p, tr) for p, tr in zip(ps, rows)] + [
                rel(p, tr, r) for p, tr in zip(ps, rows) for r in range(3)],
            out_specs=[half(p, tr) for p, tr in zip(ps, rows)],
        ),
        out_shape=[_sds((2 * p.shape[0], p.shape[1]), F32) for p in ps],
        compiler_params=_params(("parallel",), 48),
    )(chip_core, *_hbm(*ps, *[g for got in gots for g in (got, got, got)]))
    return list(outs)


def _place_shards(shards, chip, name):
    n = len(shards)
    tiles = [_row_tile(s.shape[0], s.shape[1]) for s in shards]
    steps = max(s.shape[0] // t for s, t in zip(shards, tiles))
    tiles = [s.shape[0] // steps for s in shards]

    def body(j_ref, *refs):
        for a in range(n):
            refs[n + a][...] = refs[a][...].astype(BF16)

    return pl.pallas_call(
        body,
        name=name,
        grid_spec=pltpu.PrefetchScalarGridSpec(
            num_scalar_prefetch=1,
            grid=(steps,),
            in_specs=[pl.BlockSpec((t, s.shape[1]), lambda i, j_ref: (i, 0)) for s, t in zip(shards, tiles)],
            out_specs=[pl.BlockSpec((None, t, s.shape[1]), lambda i, j_ref: (j_ref[0], i, 0))
                       for s, t in zip(shards, tiles)],
        ),
        out_shape=[_sds((N_CHIPS,) + s.shape, BF16) for s in shards],
        compiler_params=_params(("parallel",), 48),
    )(chip, *_hbm(*shards))


def _adamw_update(w, g, m, v):
    c1 = 1.0 - ADAM_B1 ** ADAM_STEP
    c2 = 1.0 - ADAM_B2 ** ADAM_STEP
    nm = ADAM_B1 * m + (1.0 - ADAM_B1) * g
    nv = ADAM_B2 * v + (1.0 - ADAM_B2) * (g * g)
    return (-ADAM_LR) * ((nm / c1) / (jnp.sqrt(nv / c2) + ADAM_EPS) + ADAM_WD * w), nm, nv


def _adamw(groups, chip_core, name):
    params = [p for group in groups for p in group]
    n = len(params)
    steps, group_rows = _group_tiles([[p[0] for p in group] for group in groups])
    rows = [tr for group, tr in zip(groups, group_rows) for _ in group]
    assert steps % 2 == 0
    half = steps // 2

    def body(jc_ref, *refs):
        ins, g_all, outs, lands = refs[0:4 * n], refs[4 * n:5 * n], refs[5 * n:9 * n], refs[9 * n:10 * n]
        send_sems, recv_sems = refs[10 * n], refs[10 * n + 1]
        s = pl.program_id(0)
        x, y, c = _place()

        def copy(a, t, first, count):
            src_rows = pl.ds(pl.multiple_of((c * half + t) * rows[a] + first, 8), count)
            dst_rows = pl.ds(pl.multiple_of(t * rows[a] + first, 8), count)
            return pltpu.make_async_remote_copy(
                src_ref=g_all[a].at[src_rows, :], dst_ref=lands[a].at[dst_rows, :],
                send_sem=send_sems.at[a, t], recv_sem=recv_sems.at[a, t],
                device_id=(x, y, 1 - c), device_id_type=MESH)

        @pl.when(s == 0)
        def _():
            for t in range(half):
                for a in range(n):
                    piece = min(D2D_PIECE_ROWS, rows[a])
                    for q in range(rows[a] // piece):
                        copy(a, t, q * piece, piece).start()

        def update(a, g):
            w_ref, _, m_ref, v_ref = ins[4 * a:4 * a + 4]
            d_ref, nm_ref, nv_ref, go_ref = outs[4 * a:4 * a + 4]
            d_ref[...], nm_ref[...], nv_ref[...] = _adamw_update(w_ref[...], g, m_ref[...], v_ref[...])
            go_ref[...] = g

        @pl.when(s < half)
        def _():
            for a in range(n):
                update(a, ins[4 * a + 1][...])

        @pl.when(s >= half)
        def _():
            t = s - half
            for a in range(n):
                copy(a, t, 0, rows[a]).wait()
                update(a, lands[a][pl.ds(pl.multiple_of(t * rows[a], 8), rows[a]), :])

    def specs(p, tr):
        tile = pl.BlockSpec((tr, p[0].shape[1]), lambda i, jc_ref: ((jc_ref[1] * half + i) % steps, 0))
        own = pl.BlockSpec((tr, p[0].shape[1]), lambda i, jc_ref: (jc_ref[1] * half + jnp.minimum(i, half - 1), 0))
        return [tile, own, tile, tile], [tile] * 4

    in_specs = [s for p, tr in zip(params, rows) for s in specs(p, tr)[0]]
    out_specs = [s for p, tr in zip(params, rows) for s in specs(p, tr)[1]]
    outs = pl.pallas_call(
        body,
        name=name,
        grid_spec=pltpu.PrefetchScalarGridSpec(
            num_scalar_prefetch=1,
            grid=(steps,),
            in_specs=in_specs + [ANY] * n,
            out_specs=out_specs,
            scratch_shapes=[pltpu.VMEM((tr * half, p[0].shape[1]), F32) for p, tr in zip(params, rows)] + [
                pltpu.SemaphoreType.DMA((n, half))] * 2,
        ),
        out_shape=[_sds(p[0].shape, F32) for p in params for _ in range(4)],
        compiler_params=_params(("arbitrary",), 48),
    )(chip_core, *_hbm(*[t for p in params for t in p], *[p[1] for p in params]))
    return [tuple(outs[4 * a:4 * a + 4]) for a in range(n)]


def _adamw_whole(params, name):
    n = len(params)

    def body(*refs):
        for a in range(n):
            w_ref, g_ref, m_ref, v_ref = refs[4 * a:4 * a + 4]
            d_ref, nm_ref, nv_ref = refs[4 * n + 3 * a:4 * n + 3 * a + 3]
            d_ref[...], nm_ref[...], nv_ref[...] = _adamw_update(w_ref[...], g_ref[...], m_ref[...], v_ref[...])

    def whole(t):
        return pl.BlockSpec(t.shape, lambda i: (0,) * t.ndim)

    flat = [t for p in params for t in p]
    like = [p[0] for p in params for _ in range(3)]
    outs = pl.pallas_call(
        body, name=name, grid=(1,), in_specs=[whole(t) for t in flat], out_specs=[whole(t) for t in like],
        out_shape=[_sds(t.shape, F32) for t in like], compiler_params=_params(("arbitrary",), 48),
    )(*_hbm(*flat))
    return [tuple(outs[3 * a:3 * a + 3]) for a in range(n)]


def _place():
    return lax.axis_index("x"), lax.axis_index("y"), lax.axis_index("c")


def _chip_of(x, y, r):
    return (x ^ (r >> 1), y ^ (r & 1))


ANY = pl.BlockSpec(memory_space=pl.ANY)


def _gather_weights(placed, cw8):
    nbig = len(placed)
    halves = [s.shape[1] // 2 for s in placed]
    pieces = [max(1, h // 64) for h in halves]
    rows = [h // p for h, p in zip(halves, pieces)]
    order = [(a, q) for q in range(max(pieces)) for a in range(nbig) if q < pieces[a]]
    ici_sem = {(a, q, r): 3 * i + (r - 1) for i, (a, q) in enumerate(order) for r in (1, 2, 3)}
    cw_sem = {r: 3 * len(order) + (r - 1) for r in (1, 2, 3)}
    d2d_sem = {key: 3 * len(order) + 3 + k for key, k in ici_sem.items()}
    nsem = 6 * len(order) + 3

    def body(*refs):
        cw_ref, dsts, gcw_ref = refs[nbig], refs[nbig + 1:2 * nbig + 1], refs[2 * nbig + 1]
        send_sems, recv_sems = refs[2 * nbig + 2:]
        x, y, c = _place()
        j = 2 * x + y

        def piece_rows(a, q, core):
            return pl.ds(pl.multiple_of(core * halves[a] + q * rows[a], 16), rows[a])

        def ici(a, q, r):
            tx, ty = _chip_of(x, y, r)
            k = ici_sem[(a, q, r)]
            region = dsts[a].at[j, piece_rows(a, q, c), :]
            return pltpu.make_async_remote_copy(
                src_ref=region, dst_ref=region, send_sem=send_sems.at[k], recv_sem=recv_sems.at[k],
                device_id=(tx, ty, c), device_id_type=MESH)

        def ici_landed(a, q, r):
            tx, ty = _chip_of(x, y, r)
            k = ici_sem[(a, q, r)]
            region = dsts[a].at[2 * tx + ty, piece_rows(a, q, c), :]
            return pltpu.make_async_remote_copy(
                src_ref=region, dst_ref=region, send_sem=send_sems.at[k], recv_sem=recv_sems.at[k],
                device_id=(tx, ty, c), device_id_type=MESH)

        def d2d(a, q, r, core):
            tx, ty = _chip_of(x, y, r)
            k = d2d_sem[(a, q, r)]
            region = dsts[a].at[2 * tx + ty, piece_rows(a, q, core), :]
            return pltpu.make_async_remote_copy(
                src_ref=region, dst_ref=region, send_sem=send_sems.at[k], recv_sem=recv_sems.at[k],
                device_id=(x, y, 1 - c), device_id_type=MESH)

        def cw_copy(r):
            tx, ty = _chip_of(x, y, r)
            k = cw_sem[r]
            return pltpu.make_async_remote_copy(
                src_ref=cw_ref, dst_ref=gcw_ref.at[j], send_sem=send_sems.at[k], recv_sem=recv_sems.at[k],
                device_id=(tx, ty, c), device_id_type=MESH)

        def cw_landed(r):
            tx, ty = _chip_of(x, y, r)
            k = cw_sem[r]
            region = gcw_ref.at[2 * tx + ty]
            return pltpu.make_async_remote_copy(
                src_ref=region, dst_ref=region, send_sem=send_sems.at[k], recv_sem=recv_sems.at[k],
                device_id=(tx, ty, c), device_id_type=MESH)

        def relay(a, q, origin, to):
            ox, oy = _chip_of(x, y, origin)
            tx, ty = _chip_of(x, y, to)
            k = ici_sem[(a, q, 3)]
            region = dsts[a].at[2 * ox + oy, piece_rows(a, q, c), :]
            return pltpu.make_async_remote_copy(
                src_ref=region, dst_ref=region, send_sem=send_sems.at[k], recv_sem=recv_sems.at[k],
                device_id=(tx, ty, c), device_id_type=MESH)

        first = [ici(a, q, r) for (a, q) in order for r in (1, 2)] + [cw_copy(r) for r in (1, 2, 3)]
        for cp in first:
            cp.start()
        passed = []
        for (a, q) in order:
            for r in (1, 2):
                ici_landed(a, q, r).wait_recv()
                if q % 2 == r - 1:
                    cp = relay(a, q, r, 3 - r)
                    cp.start()
                    passed.append(cp)
                cp = d2d(a, q, r, c)
                cp.start()
                passed.append(cp)
        for (a, q) in order:
            ici_landed(a, q, 3).wait_recv()
            cp = d2d(a, q, 3, c)
            cp.start()
            passed.append(cp)
        for r in (1, 2, 3):
            cw_landed(r).wait_recv()
        for (a, q) in order:
            for r in (1, 2, 3):
                d2d(a, q, r, 1 - c).wait_recv()
        for cp in first + passed:
            cp.wait_send()

    return pl.pallas_call(
        body,
        name="gather_weights",
        in_specs=[ANY] * (nbig + 1),
        out_specs=[ANY] * (nbig + 1),
        out_shape=[_sds(s.shape, s.dtype) for s in placed] + [_sds((N_CHIPS,) + cw8.shape, cw8.dtype)],
        input_output_aliases={a: a for a in range(nbig)},
        scratch_shapes=[pltpu.SemaphoreType.DMA((nsem,)), pltpu.SemaphoreType.DMA((nsem,))],
    )(*placed, cw8)


def _gather_late_start(placed, after, name):
    n = len(placed)
    halves = [s.shape[1] // 2 for s in placed]

    def body(*refs):
        g_refs = refs[0:n]
        send_sems, recv_sems, token = refs[n + 1], refs[n + 2], refs[-1]
        x, y, c = _place()
        j = 2 * x + y
        for a in range(n):
            mine = g_refs[a].at[j, pl.ds(pl.multiple_of(c * halves[a], 16), halves[a]), :]
            for r in (1, 2, 3):
                tx, ty = _chip_of(x, y, r)
                for to_core in (0, 1):
                    k = ((a * 3 + (r - 1)) * 2 + c) * 2 + to_core
                    pltpu.make_async_remote_copy(
                        src_ref=mine, dst_ref=mine, send_sem=send_sems.at[k], recv_sem=recv_sems.at[k],
                        device_id=(tx, ty, to_core), device_id_type=MESH).start()
        token[...] = jnp.zeros_like(token)

    hbm = lambda t: pltpu.HBM(t.shape, t.dtype)
    keep = lambda t: pltpu.with_memory_space_constraint(t, pltpu.HBM)
    nsem = 12 * n
    outs = pl.pallas_call(
        body,
        name=name,
        in_specs=[HBM] * n + [ANY],
        out_specs=(SEM, SEM, *[HBM] * n, pl.BlockSpec(memory_space=pltpu.VMEM)),
        out_shape=(pltpu.SemaphoreType.DMA((nsem,)), pltpu.SemaphoreType.DMA((nsem,)), *[hbm(p) for p in placed],
                   jax.ShapeDtypeStruct((8, 128), F32)),
        input_output_aliases={i: 2 + i for i in range(n)},
        compiler_params=pltpu.CompilerParams(has_side_effects=DATAFLOW),
    )(*[keep(p) for p in placed], after)
    return outs[0], outs[1], list(outs[2:2 + n]), outs[-1]


def _gather_late_wait(send_sems, recv_sems, thru, after, name):
    n = len(thru)
    halves = [s.shape[1] // 2 for s in thru]

    def body(*refs):
        g_refs = refs[0:n]
        send_sems, recv_sems = refs[n], refs[n + 1]
        x, y, c = _place()
        j = 2 * x + y
        for a in range(n):
            mine = g_refs[a].at[j, pl.ds(pl.multiple_of(c * halves[a], 16), halves[a]), :]
            for r in (1, 2, 3):
                tx, ty = _chip_of(x, y, r)
                for other in (0, 1):
                    k_out = ((a * 3 + (r - 1)) * 2 + c) * 2 + other
                    pltpu.make_async_remote_copy(
                        src_ref=mine, dst_ref=mine, send_sem=send_sems.at[k_out], recv_sem=recv_sems.at[k_out],
                        device_id=(tx, ty, other), device_id_type=MESH).wait_send()
                    k_in = ((a * 3 + (r - 1)) * 2 + other) * 2 + c
                    theirs = g_refs[a].at[2 * tx + ty, pl.ds(other * halves[a], halves[a]), :]
                    pltpu.make_async_remote_copy(
                        src_ref=theirs, dst_ref=theirs, send_sem=send_sems.at[k_in], recv_sem=recv_sems.at[k_in],
                        device_id=(tx, ty, other), device_id_type=MESH).wait_recv()

    hbm = lambda t: pltpu.HBM(t.shape, t.dtype)
    outs = pl.pallas_call(
        body,
        name=name,
        in_specs=[HBM] * n + [SEM, SEM, ANY],
        out_specs=[HBM] * n,
        out_shape=[hbm(t) for t in thru],
        input_output_aliases={i: i for i in range(n)},
        compiler_params=pltpu.CompilerParams(has_side_effects=DATAFLOW),
    )(*thru, send_sems, recv_sems, after)
    return list(outs)


D2D_PIECE_ROWS = 64
PAIR_SUM_TILE_BYTES = 2 * MIB


def _pair_sum(gs, gbs, chip_core, name):
    n = len(gs)
    nch, R, C = gs[0].shape
    h = R // 2
    tr = _row_tile(h, C * n, PAIR_SUM_TILE_BYTES)
    nt = h // tr
    rows = min(D2D_PIECE_ROWS, tr)

    def body(jc_ref, *refs):
        g_refs, gb_refs, p_refs, pb_refs = refs[0:n], refs[n:2 * n], refs[2 * n:3 * n], refs[3 * n:4 * n]
        got_refs, send_sems, recv_sems = refs[4 * n:5 * n], refs[5 * n], refs[5 * n + 1]
        i, j = pl.program_id(0), pl.program_id(1)
        x, y, c = _place()

        def copy(a, ti, tj, first, count):
            src_rows = pl.ds(pl.multiple_of((1 - c) * h + ti * tr + first, 16), count)
            dst_rows = pl.ds(pl.multiple_of(ti * tr + first, 16), count)
            return pltpu.make_async_remote_copy(
                src_ref=gb_refs[a].at[tj, src_rows, :], dst_ref=got_refs[a].at[tj, dst_rows, :],
                send_sem=send_sems.at[a, ti, tj], recv_sem=recv_sems.at[a, ti, tj],
                device_id=(x, y, 1 - c), device_id_type=MESH)

        @pl.when((i == 0) & (j == 0))
        def _():
            for ti in range(nt):
                for tj in range(nch):
                    for a in range(n):
                        for q in range(tr // rows):
                            copy(a, ti, tj, q * rows, rows).start()

        for a in range(n):
            copy(a, i, j, 0, tr).wait()
            s = g_refs[a][...] + got_refs[a][j, pl.ds(pl.multiple_of(i * tr, 16), tr), :].astype(F32)
            pb_refs[a][...] = s.astype(BF16)

            @pl.when(j == jc_ref[0])
            def _():
                p_refs[a][...] = s

    by_chip = pl.BlockSpec((None, tr, C), lambda i, j, jc_ref: (j, i, 0))
    outs = pl.pallas_call(
        body,
        name=name,
        grid_spec=pltpu.PrefetchScalarGridSpec(
            num_scalar_prefetch=1,
            grid=(nt, nch),
            in_specs=[pl.BlockSpec((None, tr, C), lambda i, j, jc_ref: (j, jc_ref[1] * nt + i, 0))] * n + [ANY] * n,
            out_specs=[pl.BlockSpec((tr, C), lambda i, j, jc_ref: (i, 0))] * n + [by_chip] * n,
            scratch_shapes=[pltpu.VMEM((nch, h, C), BF16)] * n + [pltpu.SemaphoreType.DMA((n, nt, nch))] * 2,
        ),
        out_shape=[_sds((h, C), F32)] * n + [_sds((nch, h, C), BF16)] * n,
        compiler_params=_params(("arbitrary", "arbitrary"), 48),
    )(chip_core, *_hbm(*gs, *gbs))
    return list(outs[:n]), list(outs[n:])


HBM = pl.BlockSpec(memory_space=pltpu.HBM)
SEM = pl.BlockSpec(memory_space=pltpu.SEMAPHORE)
DATAFLOW = pltpu.SideEffectType.DATAFLOW_SIDE_EFFECTING


def _chip_copy(p_refs, land_refs, send_sems, recv_sems, a, r, blocked):
    x, y, c = _place()
    tx, ty = _chip_of(x, y, r)
    k = a * 3 + (r - 1)
    return pltpu.make_async_remote_copy(
        src_ref=p_refs[a].at[2 * tx + ty] if blocked else p_refs[a], dst_ref=land_refs[a].at[r - 1],
        send_sem=send_sems.at[k], recv_sem=recv_sems.at[k], device_id=(tx, ty, c), device_id_type=MESH)


def _chip_exchange_start(psums, name, blocked=True):
    n = len(psums)
    lands = [lax.empty((3,) + (p.shape[1:] if blocked else p.shape), p.dtype) for p in psums]

    def body(*refs):
        p_refs, land_refs = refs[0:n], refs[n:2 * n]
        send_sems, recv_sems, token = refs[2 * n], refs[2 * n + 1], refs[-1]
        for a in range(n):
            for r in (1, 2, 3):
                _chip_copy(p_refs, land_refs, send_sems, recv_sems, a, r, blocked).start()
        token[...] = jnp.zeros_like(token)

    hbm = lambda t: pltpu.HBM(t.shape, t.dtype)
    keep = lambda t: pltpu.with_memory_space_constraint(t, pltpu.HBM)
    outs = pl.pallas_call(
        body,
        name=name,
        in_specs=[HBM] * (2 * n),
        out_specs=(SEM, SEM, *[HBM] * (2 * n), pl.BlockSpec(memory_space=pltpu.VMEM)),
        out_shape=(pltpu.SemaphoreType.DMA((3 * n,)), pltpu.SemaphoreType.DMA((3 * n,)),
                   *[hbm(p) for p in psums], *[hbm(l) for l in lands], _sds((8, 128), F32)),
        input_output_aliases={i: 2 + i for i in range(2 * n)},
        compiler_params=pltpu.CompilerParams(has_side_effects=DATAFLOW),
    )(*[keep(p) for p in psums], *[keep(l) for l in lands])
    return outs[0], outs[1], list(outs[2:2 + n]), list(outs[2 + n:2 + 2 * n]), outs[-1]


def _chip_exchange_wait(send_sems, recv_sems, p_thru, land_thru, after, name, blocked=True):
    n = len(p_thru)

    def body(*refs):
        p_refs, land_refs = refs[0:n], refs[n:2 * n]
        send_sems, recv_sems = refs[2 * n], refs[2 * n + 1]
        for a in range(n):
            for r in (1, 2, 3):
                copy = _chip_copy(p_refs, land_refs, send_sems, recv_sems, a, r, blocked)
                copy.wait_send()
                copy.wait_recv()

    hbm = lambda t: pltpu.HBM(t.shape, t.dtype)
    outs = pl.pallas_call(
        body,
        name=name,
        in_specs=[HBM] * (2 * n) + [SEM, SEM, ANY],
        out_specs=[HBM] * (2 * n),
        out_shape=[hbm(p) for p in p_thru] + [hbm(l) for l in land_thru],
        input_output_aliases={i: i for i in range(2 * n)},
        compiler_params=pltpu.CompilerParams(has_side_effects=DATAFLOW),
    )(*p_thru, *land_thru, send_sems, recv_sems, after)
    return list(outs[0:n]), list(outs[n:2 * n])


def _small_pair_sum(s):
    R, C = s.shape
    V = SMALL_VECTOR_ROWS

    def body(s_ref, v_ref, m_ref, sib, send_sem, recv_sem):
        x, y, c = _place()

        def to_sib(src, dst):
            return pltpu.make_async_remote_copy(
                src_ref=src, dst_ref=dst, send_sem=send_sem, recv_sem=recv_sem,
                device_id=(x, y, 1 - c), device_id_type=MESH)

        for q in range(R // 8):
            to_sib(s_ref.at[pl.ds(8 * q, 8), :], sib.at[pl.ds(8 * q, 8), :]).start()
        to_sib(s_ref, sib).wait()
        v_ref[...] = s_ref[pl.ds(0, V), :] + sib[pl.ds(0, V), :]
        m_ref[...] = (s_ref[pl.ds(V, R - V), :] + sib[pl.ds(V, R - V), :]).astype(BF16)

    return pl.pallas_call(
        body,
        name="small_pair_sum",
        in_specs=[pl.BlockSpec(memory_space=pltpu.VMEM)],
        out_specs=[pl.BlockSpec(memory_space=pltpu.VMEM)] * 2,
        out_shape=[jax.ShapeDtypeStruct((V, C), F32), jax.ShapeDtypeStruct((R - V, C), BF16)],
        scratch_shapes=[pltpu.VMEM((R, C), F32), pltpu.SemaphoreType.DMA, pltpu.SemaphoreType.DMA],
    )(s)


def _small_total(chip, own, landed):
    V, C = own[0].shape
    M = own[1].shape[0]

    def body(j_ref, v_ref, m_ref, lv_ref, lm_ref, o_ref, chips_v, chips_m):
        j = j_ref[0]
        chips_v[j] = v_ref[...]
        chips_m[j] = m_ref[...]
        for r in (1, 2, 3):
            chips_v[j ^ r] = lv_ref[r - 1]
            chips_m[j ^ r] = lm_ref[r - 1]
        o_ref[pl.ds(0, V), :] = (chips_v[0] + chips_v[1]) + (chips_v[2] + chips_v[3])
        o_ref[pl.ds(V, M), :] = (chips_m[0].astype(F32) + chips_m[1].astype(F32)) + (
            chips_m[2].astype(F32) + chips_m[3].astype(F32))

    vmem = pl.BlockSpec(memory_space=pltpu.VMEM)
    return pl.pallas_call(
        body,
        name="small_total",
        in_specs=[pl.BlockSpec(memory_space=pltpu.SMEM), vmem, vmem, vmem, vmem],
        out_specs=vmem,
        out_shape=jax.ShapeDtypeStruct((V + M, C), F32),
        scratch_shapes=[pltpu.VMEM((N_CHIPS, V, C), F32), pltpu.VMEM((N_CHIPS, M, C), BF16)],
    )(chip, own[0], own[1], landed[0], landed[1])


def _local_grads(x, target, g_pre, w_in_g, b_gate, conv_w, conv_b, w_rg_a, b_rg_a, w_rg_x, b_rg_x, lam, sinks,
                 out_weights, fwd_token, g_post, on_out_grads, on_w_in_grad):
    b_a = b_rg_a.reshape(1, D_RNN)
    b_x = b_rg_x.reshape(1, D_RNN)

    proj, ht = _proj_fwd(x, g_pre, w_in_g)
    y_rnn, z_rnn, conv, z_rnn_t = _rnn_fwd(proj, conv_w, conv_b, w_rg_a, w_rg_x, b_a, b_x, lam, fwd_token)
    bias = _attn_bias()
    y_attn, z_attn, lse = _attn_fwd(proj, sinks, bias)
    w_rnn_out, w_attn_out, w_out = out_weights(z_attn)
    dyx, dz_rnn, dz_attn, dml, dout, dbr_rnn, dbr_attn, merged_t, z_attn_t, head_small = _head(
        x, target, z_rnn, z_attn, proj, b_gate, g_post, w_rnn_out, w_attn_out, w_out)
    out_grads = [_matmul_t(z_rnn_t, dbr_rnn, "dw_rnn_out"), _matmul_t(z_attn_t, dbr_attn, "dw_attn_out"),
                 _matmul_t(merged_t, dout, "dw_out")]
    shard_rows = lambda d: d.reshape(N_CHIPS, OUT_SHARD, D_MODEL)
    token = on_out_grads([shard_rows(g) for g, _ in out_grads], [shard_rows(gb) for _, gb in out_grads])
    dq, dk, dv, dag, attn_small = _attn_bwd(proj, y_attn, lse, dz_attn, sinks, bias, token)
    drx, drg, dwa, dwx, rnn_small = _rnn_bwd(proj, conv, y_rnn, dz_rnn, conv_w, w_rg_a, w_rg_x, b_a, b_x, lam)
    dproj = [drx, drg, dq, dk, dv, dag, dml]
    token = on_w_in_grad(*_dw_in(ht, dproj))
    grad_x, dh_small = _dh_bwd(dproj, w_in_g, x, dyx, g_pre, token)
    small = jnp.concatenate([rnn_small, head_small, dh_small + attn_small,
                             dwa.reshape(64, 1024), dwx.reshape(64, 1024)], axis=0)
    return grad_x, small


ROW_LOSS = 11


def _unpack_small(s, conv_cols):
    return {
        "b_rg_a": s[0:1].reshape(1, 16, 64), "b_rg_x": s[1:2].reshape(1, 16, 64), "lru_lambda": s[2:3],
        "conv_b": s[3:4], "conv_w": s[4:8, 0:conv_cols].reshape(1, CONV_W, conv_cols),
        "post_norm_g": s[8:9], "b_gate": s[9:11].reshape(1, 2048),
        "pre_norm_g": s[16:17], "attn_sinks": s[17:18, 0:N_Q_HEADS],
        "w_rg_a": s[24:88].reshape(1, 16, 64, 64), "w_rg_x": s[88:152].reshape(1, 16, 64, 64),
    }


WEIGHTS = ["pre_norm_g", "w_in", "b_gate", "conv_w", "conv_b", "w_rg_a", "b_rg_a", "w_rg_x", "b_rg_x", "lru_lambda",
           "attn_sinks", "w_rnn_out", "w_attn_out", "w_out", "post_norm_g"]
BIG = ["w_in", "w_rnn_out", "w_attn_out", "w_out"]


def kernel(x, pre_norm_g, w_in, b_gate, conv_w, conv_b, w_rg_a, b_rg_a, w_rg_x, b_rg_x, lru_lambda, attn_sinks, w_rnn_out, w_attn_out, w_out, post_norm_g, loss_target, m_pre_norm_g, m_w_in, m_b_gate, m_conv_w, m_conv_b, m_w_rg_a, m_b_rg_a, m_w_rg_x, m_b_rg_x, m_lru_lambda, m_attn_sinks, m_w_rnn_out, m_w_attn_out, m_w_out, m_post_norm_g, v_pre_norm_g, v_w_in, v_b_gate, v_conv_w, v_conv_b, v_w_rg_a, v_b_rg_a, v_w_rg_x, v_b_rg_x, v_lru_lambda, v_attn_sinks, v_w_rnn_out, v_w_attn_out, v_w_out, v_post_norm_g):
    w = dict(pre_norm_g=pre_norm_g, w_in=w_in, b_gate=b_gate, conv_w=conv_w, conv_b=conv_b, w_rg_a=w_rg_a,
             b_rg_a=b_rg_a, w_rg_x=w_rg_x, b_rg_x=b_rg_x, lru_lambda=lru_lambda, attn_sinks=attn_sinks,
             w_rnn_out=w_rnn_out, w_attn_out=w_attn_out, w_out=w_out, post_norm_g=post_norm_g)
    m = dict(pre_norm_g=m_pre_norm_g, w_in=m_w_in, b_gate=m_b_gate, conv_w=m_conv_w, conv_b=m_conv_b, w_rg_a=m_w_rg_a,
             b_rg_a=m_b_rg_a, w_rg_x=m_w_rg_x, b_rg_x=m_b_rg_x, lru_lambda=m_lru_lambda, attn_sinks=m_attn_sinks,
             w_rnn_out=m_w_rnn_out, w_attn_out=m_w_attn_out, w_out=m_w_out, post_norm_g=m_post_norm_g)
    v = dict(pre_norm_g=v_pre_norm_g, w_in=v_w_in, b_gate=v_b_gate, conv_w=v_conv_w, conv_b=v_conv_b, w_rg_a=v_w_rg_a,
             b_rg_a=v_b_rg_a, w_rg_x=v_w_rg_x, b_rg_x=v_b_rg_x, lru_lambda=v_lru_lambda, attn_sinks=v_attn_sinks,
             w_rnn_out=v_w_rnn_out, w_attn_out=v_w_attn_out, w_out=v_w_out, post_norm_g=v_post_norm_g)
    chip = 2 * lax.axis_index("x") + lax.axis_index("y")

    chip_idx = chip.astype(jnp.int32).reshape(1)
    chip_core = jnp.stack([chip, lax.axis_index("c")]).astype(jnp.int32)
    cw8 = jnp.pad(conv_w[0], ((0, 8 - CONV_W), (0, 0)))
    placed = _place_shards([w_in[0], w_rnn_out[0], w_attn_out[0], w_out[0]], chip_idx, "place_shards")
    win_g, cw_g = _gather_weights(placed[:1], cw8)
    late_send, late_recv, late_thru, late_token = _gather_late_start(placed[1:], win_g, "gather_late_start")
    cw_g = lax.dynamic_update_slice_in_dim(cw_g, cw8[None], chip, axis=0)
    conv_w_full = jnp.transpose(cw_g[:, 0:CONV_W, :], (1, 0, 2)).reshape(CONV_W, D_RNN)

    started = {}

    def start_reduction(tag, grads, grads_b16):
        psums, psums_b16 = _pair_sum(grads, grads_b16, chip_core, "pair_sum_" + tag)
        send_sems, recv_sems, p_thru, land_thru, token = _chip_exchange_start(psums_b16, "chip_exchange_start_" + tag)
        started[tag] = (psums, send_sems, recv_sems, p_thru, land_thru)
        return token

    def end_reduction(tag, after):
        psums, send_sems, recv_sems, p_thru, land_thru = started[tag]
        _, landed = _chip_exchange_wait(send_sems, recv_sems, p_thru, land_thru, after, "chip_exchange_wait_" + tag)
        return psums, landed

    def out_weights(after):
        gathered = _gather_late_wait(late_send, late_recv, late_thru, after, "gather_late_wait")
        return [g.reshape(D_MODEL, D_MODEL) for g in gathered]

    grad_x, small = _local_grads(
        x[0], loss_target[0], pre_norm_g, win_g, b_gate, conv_w_full, conv_b, w_rg_a[0], b_rg_a[0], w_rg_x[0],
        b_rg_x[0], lru_lambda, attn_sinks[0], out_weights, late_token, post_norm_g,
        on_out_grads=lambda grads, grads_b16: start_reduction("out", grads, grads_b16),
        on_w_in_grad=lambda grad, grad_b16: start_reduction("in", [grad], [grad_b16]))

    small_chip = _small_pair_sum(small)
    small_send, small_recv, small_thru, small_land, small_token = _chip_exchange_start(
        list(small_chip), "small_exchange_start", blocked=False)

    halves = _chip_sum([end_reduction("in", small_token), end_reduction("out", small_token)], chip_core, "chip_sum")
    gbig = dict(zip(BIG, halves))

    grads, delta, new_m, new_v = {}, {}, {}, {}
    updates = _adamw([[(w[n][0], gbig[n], m[n][0], v[n][0]) for n in names] for names in (BIG[:1], BIG[1:])],
                     chip_core, "adamw_big")
    for n, (d, nm, nv, g) in zip(BIG, updates):
        grads[n], delta[n], new_m[n], new_v[n] = g[None], d[None], nm[None], nv[None]

    small_own, small_landed = _chip_exchange_wait(small_send, small_recv, small_thru, small_land, delta[BIG[-1]],
                                                  "small_exchange_wait", blocked=False)
    small_sum = _small_total(chip_idx, small_own, small_landed)
    total_loss = small_sum[ROW_LOSS, 0]
    gsmall = _unpack_small(small_sum, D_RNN)
    conv_shard = D_RNN // N_CHIPS
    gsmall["conv_w"] = lax.dynamic_slice_in_dim(gsmall["conv_w"], chip * conv_shard, conv_shard, axis=2)
    for n in gsmall:
        grads[n] = gsmall[n].reshape(w[n].shape)
    updates = _adamw_whole([(w[n], grads[n], m[n], v[n]) for n in gsmall], "adamw_small")
    for n, (d, nm, nv) in zip(gsmall, updates):
        delta[n], new_m[n], new_v[n] = d, nm, nv

    return (total_loss, grad_x[None], *[grads[n] for n in WEIGHTS], *[delta[n] for n in WEIGHTS],
            *[new_m[n] for n in WEIGHTS], *[new_v[n] for n in WEIGHTS])
```

```python
import functools
import math

import jax
import jax.numpy as jnp
import numpy as np
from jax import lax
from jax.experimental import pallas as pl
from jax.experimental.pallas import tpu as pltpu

F32 = jnp.float32
BF16 = jnp.bfloat16

D_MODEL = 1024
D_RNN = 1024
RNN_BLOCKS = 16
RNN_BLOCK_W = 64
CONV_W = 4
LRU_C = 8.0
N_Q_HEADS = 16
N_KV_HEADS = 4
GROUP = 4
HEAD_DIM = 64
D_KV = 256
BLOCK = 128
ALIBI_MAX_BIAS = 8.0
EPS = 1e-6
D_IN = 6656
N_CHIPS = 4
W_IN_SHARD = D_IN // N_CHIPS
OUT_SHARD = D_MODEL // N_CHIPS
ADAM_LR = 0.001
ADAM_B1 = 0.9
ADAM_B2 = 0.999
ADAM_EPS = 1e-08
ADAM_WD = 0.01
ADAM_STEP = 10
NEG_BIG = -1e30
MIB = 1 << 20

COL_RNN_X = 0
COL_RNN_GATE = 4
COL_Q = 8
COL_K = 12
COL_V = 13
COL_ATTN_GATE = 14
COL_MERGE = 18

RNN_TILE = 256
RNN_CHUNK = 512
SMALL_ROWS = 152
SMALL_VECTOR_ROWS = 24
MESH = pl.DeviceIdType.MESH


def _sds(shape, dtype):
    return pltpu.HBM(shape, dtype)


def _params(sem=None, vmem_mib=None):
    kw = {}
    if sem is not None:
        kw["dimension_semantics"] = sem
    if vmem_mib is not None:
        kw["vmem_limit_bytes"] = vmem_mib * MIB
    return pltpu.CompilerParams(**kw)


def _hbm(*arrays):
    return [pltpu.with_memory_space_constraint(a, pltpu.HBM) for a in arrays]


def _dot(a, b):
    return jnp.dot(a, b, preferred_element_type=F32)


def _dot_nt(a, b):
    return lax.dot_general(a, b, (((1,), (1,)), ((), ())), preferred_element_type=F32)


def _dot_tn(a, b):
    return lax.dot_general(a, b, (((0,), (0,)), ((), ())), preferred_element_type=F32)


def _sigmoid(x):
    return 0.5 * jnp.tanh(0.5 * x) + 0.5


def _sigmoid_small(x):
    return 1.0 / (1.0 + jnp.exp(-x))


def _softplus(x):
    return jnp.maximum(x, 0.0) + jnp.log(1.0 + jnp.exp(-jnp.abs(x)))


def _one_minus_square(a, log_a):
    return -jnp.tanh(log_a) * (a * a + 1.0)


def _proj_fwd(x, g_pre, w_in_g):
    T = x.shape[0]
    tm = min(1024, T)

    def body(x_ref, g_ref, w_ref, proj_ref, ht_ref, h_s):
        @pl.when(pl.program_id(1) == 0)
        def _():
            xv = x_ref[...]
            rstd = lax.rsqrt(jnp.mean(xv * xv, axis=-1, keepdims=True) + EPS)
            hf = (xv * rstd) * g_ref[...]
            h_s[...] = hf.astype(BF16)
            ht_ref[...] = hf.T.astype(BF16)

        proj_ref[...] = _dot(h_s[...], w_ref[...]).astype(BF16)

    return pl.pallas_call(
        body,
        name="proj_fwd",
        grid=(T // tm, N_CHIPS),
        in_specs=[
            pl.BlockSpec((tm, D_MODEL), lambda i, j: (i, 0)),
            pl.BlockSpec((1, D_MODEL), lambda i, j: (0, 0)),
            pl.BlockSpec((None, D_MODEL, W_IN_SHARD), lambda i, j: (j, 0, 0)),
        ],
        out_specs=[
            pl.BlockSpec((tm, W_IN_SHARD), lambda i, j: (i, j)),
            pl.BlockSpec((D_MODEL, tm), lambda i, j: (0, i)),
        ],
        out_shape=[_sds((T, D_IN), BF16), _sds((D_MODEL, T), BF16)],
        scratch_shapes=[pltpu.VMEM((tm, D_MODEL), BF16)],
        compiler_params=_params(("parallel", "arbitrary"), 48),
    )(*_hbm(x, g_pre, w_in_g))


def _shift_down(x, tail, s, row):
    n = x.shape[0]
    xs = pltpu.roll(x, s, 0)
    tail_t = jnp.tile(pltpu.roll(tail, s, 0), (n // 8, 1))
    return jnp.where(row < s, tail_t, xs)


def _shift_up(x, head, s, row):
    n = x.shape[0]
    xs = pltpu.roll(x, n - s, 0)
    head_t = jnp.tile(pltpu.roll(head, 8 - s, 0), (n // 8, 1))
    return jnp.where(row >= n - s, head_t, xs)


def _conv_taps(x, tail, row):
    return [_shift_down(x, tail, 3, row), _shift_down(x, tail, 2, row), _shift_down(x, tail, 1, row), x]


def _rglru_gates(c, wa, wx, ba, bx, lam):
    cb = c.astype(BF16)
    r = _sigmoid_small(_dot(cb, wa) + ba)
    i = _sigmoid(_dot(cb, wx) + bx)
    log_a = (-LRU_C) * r * _softplus(-lam)
    a = jnp.exp(log_a)
    w = _one_minus_square(a, log_a)
    inv_mult = lax.rsqrt(w)
    return cb, r, i, a, w * inv_mult, inv_mult


GATE_BLOCKS_PER_TILE = RNN_TILE // RNN_BLOCK_W
GATE_BLOCKS = pl.BlockSpec((GATE_BLOCKS_PER_TILE, RNN_BLOCK_W, RNN_BLOCK_W), lambda j, t: (j, 0, 0))


def _fill_block_diag(bd_ref, w_ref):
    bd_ref[...] = jnp.zeros_like(bd_ref)
    for a in range(GATE_BLOCKS_PER_TILE):
        lo = a * RNN_BLOCK_W
        bd_ref[lo:lo + RNN_BLOCK_W, lo:lo + RNN_BLOCK_W] = w_ref[a].astype(BF16)


SUBLANES = 8


def _scan_down(a, u, row):
    n = a.shape[0]
    s = 1
    while s < SUBLANES:
        a_sh = jnp.where(row >= s, pltpu.roll(a, s, 0), 1.0)
        u_sh = jnp.where(row >= s, pltpu.roll(u, s, 0), 0.0)
        u = a * u_sh + u
        a = a * a_sh
        s *= 2
    while s < n:
        u = jnp.concatenate([u[:s], a[s:] * u[:n - s] + u[s:]], axis=0)
        a = jnp.concatenate([a[:s], a[s:] * a[:n - s]], axis=0)
        s *= 2
    return a, u


def _scan_up(b, u, row):
    n = b.shape[0]
    s = 1
    while s < SUBLANES:
        b_sh = jnp.where(row < n - s, pltpu.roll(b, n - s, 0), 1.0)
        u_sh = jnp.where(row < n - s, pltpu.roll(u, n - s, 0), 0.0)
        u = b * u_sh + u
        b = b * b_sh
        s *= 2
    while s < n:
        u = jnp.concatenate([b[:n - s] * u[s:] + u[:n - s], u[n - s:]], axis=0)
        b = jnp.concatenate([b[:n - s] * b[s:], b[n - s:]], axis=0)
        s *= 2
    return b, u


LANES = 128


def _chunk_scan(a, u, a_s, u_s, hl_s, al_s, carry, reverse):
    n, width = a.shape
    groups = n // SUBLANES
    order = range(SUBLANES - 1, -1, -1) if reverse else range(SUBLANES)
    row = lax.broadcasted_iota(jnp.int32, (groups, LANES), 0)
    for l in range(width // LANES):
        lanes = slice(l * LANES, (l + 1) * LANES)
        a_l, u_l, hl_l, al_l = a_s.at[l], u_s.at[l], hl_s.at[l], al_s.at[l]
        a_l[...] = a[:, lanes]
        u_l[...] = u[:, lanes]
        h_loc = a_loc = None
        for r in order:
            rows = pl.ds(r, groups, stride=SUBLANES)
            a_r, u_r = a_l[rows, :], u_l[rows, :]
            h_loc, a_loc = (u_r, a_r) if h_loc is None else (a_r * h_loc + u_r, a_r * a_loc)
            hl_l[rows, :] = h_loc
            al_l[rows, :] = a_loc
        if reverse:
            a_cum, ends = _scan_up(a_loc, h_loc, row)
            ends = ends + a_cum * carry[:, lanes]
            enters = jnp.where(row == groups - 1, carry[:, lanes], pltpu.roll(ends, groups - 1, 0))
        else:
            a_cum, ends = _scan_down(a_loc, h_loc, row)
            ends = ends + a_cum * carry[:, lanes]
            enters = jnp.where(row == 0, carry[:, lanes], pltpu.roll(ends, 1, 0))
        for r in range(SUBLANES):
            rows = pl.ds(r, groups, stride=SUBLANES)
            hl_l[rows, :] = hl_l[rows, :] + al_l[rows, :] * enters
    return jnp.concatenate([hl_s[l] for l in range(width // LANES)], axis=1)


def _rnn_fwd(proj, conv_w, conv_b, w_a, w_x, b_a, b_x, lam, token):
    T = proj.shape[0]
    tc, ct = RNN_CHUNK, RNN_TILE
    nt = T // tc

    def body(x_ref, rg_ref, cw_ref, cb_ref, wa_ref, wx_ref, ba_ref, bx_ref, lam_ref, token_ref, h_ref, z_ref, c_ref,
             zt_ref, xtail, hcarry, wa_s, wx_s, a_s, u_s, hl_s, al_s):
        @pl.when(pl.program_id(1) == 0)
        def _():
            xtail[...] = jnp.zeros_like(xtail)
            hcarry[...] = jnp.zeros_like(hcarry)
            _fill_block_diag(wa_s, wa_ref)
            _fill_block_diag(wx_s, wx_ref)

        row = lax.broadcasted_iota(jnp.int32, (tc, ct), 0)
        x = x_ref[...].astype(F32)
        taps = _conv_taps(x, xtail[...], row)
        c = cb_ref[...] + cw_ref[pl.ds(0, 1), :] * taps[0]
        for k in range(1, CONV_W):
            c = c + cw_ref[pl.ds(k, 1), :] * taps[k]
        xtail[...] = x[tc - 8:, :]
        c_ref[...] = c
        _, _, i, a, mult, _ = _rglru_gates(c, wa_s[...], wx_s[...], ba_ref[...], bx_ref[...], lam_ref[...])
        h = _chunk_scan(a, mult * (i * c), a_s, u_s, hl_s, al_s, hcarry[...], reverse=False)
        h_ref[...] = h
        hcarry[...] = h_ref[pl.ds(tc - 1, 1), :]
        rg = rg_ref[...].astype(F32)
        z = h * (rg * _sigmoid(rg))
        z_ref[...] = z.astype(BF16)
        zt_ref[...] = z.T.astype(BF16)

    col = lambda off: (lambda j, t: (t, off + j))
    vec = pl.BlockSpec((1, ct), lambda j, t: (0, j))
    return pl.pallas_call(
        body,
        name="rnn_fwd",
        grid=(D_RNN // ct, nt),
        in_specs=[
            pl.BlockSpec((tc, ct), col(COL_RNN_X)),
            pl.BlockSpec((tc, ct), col(COL_RNN_GATE)),
            pl.BlockSpec((CONV_W, ct), lambda j, t: (0, j)),
            vec, GATE_BLOCKS, GATE_BLOCKS, vec, vec, vec,
            pl.BlockSpec((8, 128), lambda j, t: (0, 0)),
        ],
        out_specs=[pl.BlockSpec((tc, ct), lambda j, t: (t, j))] * 3 + [pl.BlockSpec((ct, tc), lambda j, t: (j, t))],
        out_shape=[_sds((T, D_RNN), F32), _sds((T, D_RNN), BF16), _sds((T, D_RNN), F32), _sds((D_RNN, T), BF16)],
        scratch_shapes=[pltpu.VMEM((8, ct), F32), pltpu.VMEM((1, ct), F32)] + [pltpu.VMEM((ct, ct), BF16)] * 2 + [
            pltpu.VMEM((ct // LANES, tc, LANES), F32)] * 4,
        compiler_params=_params(("parallel", "arbitrary"), 32),
    )(*_hbm(proj, proj, conv_w, conv_b, w_a, w_x, b_a, b_x, lam, token))


def _rnn_bwd(proj, conv, y_rnn, dz_rnn, conv_w, w_a, w_x, b_a, b_x, lam):
    T = proj.shape[0]
    tc, ct = RNN_CHUNK, RNN_TILE
    nt = T // tc
    hb = tc // 8

    def body(x_ref, c_ref, rg_ref, h_ref, hh_ref, dz_ref, cw_ref, wa_ref, wx_ref, ba_ref, bx_ref, lam_ref,
             dx_ref, drg_ref, dwa_ref, dwx_ref, sm_ref, lam_carry, a_carry, dc_head, wa_s, wx_s, dwa_s, dwx_s,
             b_s, dy_s, hl_s, al_s):
        t = pl.program_id(1)
        first_chunk = t == nt - 1

        @pl.when(t == 0)
        def _():
            lam_carry[...] = jnp.zeros_like(lam_carry)
            a_carry[...] = jnp.zeros_like(a_carry)
            dc_head[...] = jnp.zeros_like(dc_head)
            dwa_s[...] = jnp.zeros_like(dwa_s)
            dwx_s[...] = jnp.zeros_like(dwx_s)
            sm_ref[...] = jnp.zeros_like(sm_ref)
            _fill_block_diag(wa_s, wa_ref)
            _fill_block_diag(wx_s, wx_ref)

        row = lax.broadcasted_iota(jnp.int32, (tc, ct), 0)
        keep = jnp.where(first_chunk, 0.0, 1.0)
        x = x_ref[...].astype(F32)
        c = c_ref[...]
        lam = lam_ref[...]
        cb, r, i, a, mult, inv_mult = _rglru_gates(c, wa_s[...], wx_s[...], ba_ref[...], bx_ref[...], lam)
        h = h_ref[...]
        h_prev = _shift_down(h, hh_ref[...] * keep, 1, row)
        rg = rg_ref[...].astype(F32)
        dz = dz_ref[...]
        sg = _sigmoid(rg)
        drg_ref[...] = (dz * h * (sg * (1.0 + rg * (1.0 - sg)))).astype(BF16)
        dy = dz * (rg * sg)
        b = jnp.where(row >= tc - 1, a_carry[pl.ds(0, 1), :], pltpu.roll(a, tc - 1, 0))
        lt = _chunk_scan(b, dy, b_s, dy_s, hl_s, al_s, lam_carry[pl.ds(0, 1), :], reverse=True)
        lam_carry[...] = lt[0:8, :]
        a_carry[...] = a[0:8, :]
        ic = i * c
        dmult = lt * ic
        di = lt * mult * c
        dc = lt * mult * i
        dlog_a = a * (lt * h_prev - dmult * a * inv_mult)
        sp = _softplus(-lam)
        dpre_r = dlog_a * ((-LRU_C) * sp) * (r * (1.0 - r))
        dpre_i = di * (i * (1.0 - i))
        dlam_row = jnp.sum(dlog_a * r, axis=0, keepdims=True) * (LRU_C * _sigmoid(-lam))
        dpr_b = dpre_r.astype(BF16)
        dpi_b = dpre_i.astype(BF16)
        dwa_s[...] += _dot_tn(cb, dpr_b)
        dwx_s[...] += _dot_tn(cb, dpi_b)
        dc = dc + _dot_nt(dpr_b, wa_s[...]) + _dot_nt(dpi_b, wx_s[...])
        head = dc_head[...]
        dx = cw_ref[pl.ds(3, 1), :] * dc
        sm_ref[pl.ds(4 + 3, 1), :] += jnp.sum(dc * x, axis=0, keepdims=True)
        for m in range(1, CONV_W):
            up = _shift_up(dc, head, m, row)
            dx = dx + cw_ref[pl.ds(3 - m, 1), :] * up
            sm_ref[pl.ds(4 + 3 - m, 1), :] += jnp.sum(up * x, axis=0, keepdims=True)
        dx_ref[...] = dx.astype(BF16)
        dc_head[...] = dc[0:8, :]
        sm_ref[pl.ds(0, 1), :] += jnp.sum(dpre_r, axis=0, keepdims=True)
        sm_ref[pl.ds(1, 1), :] += jnp.sum(dpre_i, axis=0, keepdims=True)
        sm_ref[pl.ds(2, 1), :] += dlam_row
        sm_ref[pl.ds(3, 1), :] += jnp.sum(dc, axis=0, keepdims=True)

        @pl.when(first_chunk)
        def _():
            for k in range(GATE_BLOCKS_PER_TILE):
                lo = k * RNN_BLOCK_W
                dwa_ref[k] = dwa_s[lo:lo + RNN_BLOCK_W, lo:lo + RNN_BLOCK_W]
                dwx_ref[k] = dwx_s[lo:lo + RNN_BLOCK_W, lo:lo + RNN_BLOCK_W]

    rev = lambda off: (lambda j, t: (nt - 1 - t, off + j))
    halo = lambda off: (lambda j, t: (jnp.maximum((nt - 1 - t) * hb - 1, 0), off + j))
    vec = pl.BlockSpec((1, ct), lambda j, t: (0, j))
    mat = GATE_BLOCKS
    return pl.pallas_call(
        body,
        name="rnn_bwd",
        grid=(D_RNN // ct, nt),
        in_specs=[
            pl.BlockSpec((tc, ct), rev(COL_RNN_X)),
            pl.BlockSpec((tc, ct), rev(0)),
            pl.BlockSpec((tc, ct), rev(COL_RNN_GATE)),
            pl.BlockSpec((tc, ct), rev(0)),
            pl.BlockSpec((8, ct), halo(0)),
            pl.BlockSpec((tc, ct), rev(0)),
            pl.BlockSpec((CONV_W, ct), lambda j, t: (0, j)),
            mat, mat, vec, vec, vec,
        ],
        out_specs=[
            pl.BlockSpec((tc, ct), rev(0)),
            pl.BlockSpec((tc, ct), rev(0)),
            mat, mat,
            pl.BlockSpec((8, ct), lambda j, t: (0, j)),
        ],
        out_shape=[_sds((T, D_RNN), BF16), _sds((T, D_RNN), BF16), _sds(w_a.shape, F32), _sds(w_x.shape, F32),
                   _sds((8, D_RNN), F32)],
        scratch_shapes=[pltpu.VMEM((8, ct), F32)] * 3 + [pltpu.VMEM((ct, ct), BF16)] * 2 + [
            pltpu.VMEM((ct, ct), F32)] * 2 + [pltpu.VMEM((ct // LANES, tc, LANES), F32)] * 4,
        compiler_params=_params(("parallel", "arbitrary"), 32),
    )(*_hbm(proj, conv, proj, y_rnn, y_rnn, dz_rnn, conv_w, w_a, w_x, b_a, b_x, lam))


def _attn_bias():
    qi = np.arange(BLOCK)[:, None]
    kj = np.arange(BLOCK)[None, :]
    dist_cur = (qi - kj).astype(np.float32)
    slopes = np.float32(2.0) ** (-ALIBI_MAX_BIAS * np.arange(1, N_Q_HEADS + 1, dtype=np.float32) / N_Q_HEADS)
    slopes = slopes[:, None, None]
    prev = np.where(kj > qi, -slopes * (dist_cur + np.float32(BLOCK)), np.float32(NEG_BIG))
    cur = np.where(kj <= qi, -slopes * dist_cur, np.float32(NEG_BIG))
    later = np.concatenate([prev, cur], axis=-1)
    first = np.concatenate([np.full_like(prev, NEG_BIG), cur], axis=-1)
    return jnp.asarray(np.stack([first, later]).astype(np.float32))


def _attn_exps(s_prev, s_cur, sink, bias):
    s_prev = s_prev + bias[:, 0:BLOCK]
    s_cur = s_cur + bias[:, BLOCK:2 * BLOCK]
    m = jnp.maximum(jnp.max(jnp.maximum(s_prev, s_cur), axis=-1, keepdims=True), sink)
    p_prev = jnp.exp(s_prev - m)
    p_cur = jnp.exp(s_cur - m)
    total = jnp.sum(p_prev + p_cur, axis=-1, keepdims=True) + jnp.exp(sink - m)
    return p_prev, p_cur, 1.0 / total, m + jnp.log(total)


def _attn_probs(s_prev, s_cur, sink, bias, lse):
    p_prev = jnp.exp((s_prev + bias[:, 0:BLOCK]) - lse)
    p_cur = jnp.exp((s_cur + bias[:, BLOCK:2 * BLOCK]) - lse)
    return p_prev, p_cur, jnp.exp(sink - lse)


def _stack_heads(ref_or_val, hk, dtype):
    parts = [ref_or_val[:, (GROUP * hk + g) * HEAD_DIM:(GROUP * hk + g + 1) * HEAD_DIM] for g in range(GROUP)]
    return jnp.concatenate(parts, axis=0).astype(dtype)


ATTN_SCALE = HEAD_DIM ** -0.5


def _bias_spec():
    return pl.BlockSpec((None, N_Q_HEADS, BLOCK, 2 * BLOCK), lambda i: (jnp.minimum(i, 1), 0, 0, 0))


def _attn_fwd(proj, sinks, bias):
    T = proj.shape[0]
    nb = T // BLOCK

    def body(sink_ref, bias_ref, q_ref, kp_ref, kc_ref, vp_ref, vc_ref, ag0_ref, ag1_ref, y_ref, z_ref, lse_ref):
        kvs = [slice(hk * HEAD_DIM, (hk + 1) * HEAD_DIM) for hk in range(N_KV_HEADS)]
        qgs = [(_stack_heads(q_ref, hk, F32) * ATTN_SCALE).astype(BF16) for hk in range(N_KV_HEADS)]
        s_prev = [_dot_nt(qgs[hk], kp_ref[:, kvs[hk]].astype(BF16)) for hk in range(N_KV_HEADS)]
        s_cur = [_dot_nt(qgs[hk], kc_ref[:, kvs[hk]].astype(BF16)) for hk in range(N_KV_HEADS)]
        for hk in range(N_KV_HEADS):
            pp, pc, invs = [], [], []
            for g in range(GROUP):
                h = GROUP * hk + g
                rows = slice(g * BLOCK, (g + 1) * BLOCK)
                p_prev, p_cur, inv, lse = _attn_exps(s_prev[hk][rows], s_cur[hk][rows], sink_ref[h], bias_ref[h])
                pp.append(p_prev.astype(BF16))
                pc.append(p_cur.astype(BF16))
                invs.append(inv)
                lse_ref[:, h:h + 1] = lse
            og = _dot(jnp.concatenate(pp, axis=0), vp_ref[:, kvs[hk]].astype(BF16)) + _dot(
                jnp.concatenate(pc, axis=0), vc_ref[:, kvs[hk]].astype(BF16))
            for g in range(GROUP):
                h = GROUP * hk + g
                y_ref[:, h * HEAD_DIM:(h + 1) * HEAD_DIM] = og[g * BLOCK:(g + 1) * BLOCK] * invs[g]
        ag = jnp.concatenate([ag0_ref[...], ag1_ref[...]], axis=1).astype(F32)
        z_ref[...] = (y_ref[...] * (ag * _sigmoid(ag))).astype(BF16)

    prev = lambda c: (lambda i: (jnp.maximum(i - 1, 0), c))
    cur = lambda c: (lambda i: (i, c))
    return pl.pallas_call(
        body,
        name="attn_fwd",
        grid=(nb,),
        in_specs=[
            pl.BlockSpec(memory_space=pltpu.SMEM),
            _bias_spec(),
            pl.BlockSpec((BLOCK, 1024), lambda i: (i, COL_Q // 4)),
            pl.BlockSpec((BLOCK, D_KV), prev(COL_K)),
            pl.BlockSpec((BLOCK, D_KV), cur(COL_K)),
            pl.BlockSpec((BLOCK, D_KV), prev(COL_V)),
            pl.BlockSpec((BLOCK, D_KV), cur(COL_V)),
            pl.BlockSpec((BLOCK, 512), lambda i: (i, COL_ATTN_GATE // 2)),
            pl.BlockSpec((BLOCK, 512), lambda i: (i, COL_ATTN_GATE // 2 + 1)),
        ],
        out_specs=[pl.BlockSpec((BLOCK, 1024), lambda i: (i, 0)), pl.BlockSpec((BLOCK, 1024), lambda i: (i, 0)),
                   pl.BlockSpec((BLOCK, N_Q_HEADS), lambda i: (i, 0))],
        out_shape=[_sds((T, 1024), F32), _sds((T, 1024), BF16), _sds((T, N_Q_HEADS), F32)],
        compiler_params=_params(("arbitrary",), 32),
    )(sinks, *_hbm(bias, proj, proj, proj, proj, proj, proj, proj))


def _attn_bwd(proj, y_attn, lse, dz_attn, sinks, bias, token):
    T = proj.shape[0]
    nb = T // BLOCK

    def body(sink_ref, bias_ref, q_ref, kp_ref, kc_ref, vp_ref, vc_ref, ag0_ref, ag1_ref, y_ref, lse_ref, dz_ref,
             token_ref, dq_ref, dk_ref, dv_ref, dag_ref, ds_ref, dy_s):
        i = pl.program_id(0)

        @pl.when(i == 0)
        def _():
            ds_ref[...] = jnp.zeros_like(ds_ref)

        lane = lax.broadcasted_iota(jnp.int32, (8, 128), 1)
        sub = lax.broadcasted_iota(jnp.int32, (8, 128), 0)
        ag = jnp.concatenate([ag0_ref[...], ag1_ref[...]], axis=1).astype(F32)
        dz = dz_ref[...]
        sg = _sigmoid(ag)
        dag_ref[...] = (dz * y_ref[...] * (sg * (1.0 + ag * (1.0 - sg)))).astype(BF16)
        dy_s[...] = dz * (ag * sg)
        r_cur = pl.multiple_of(i * BLOCK, BLOCK)
        r_prev = pl.multiple_of(jnp.maximum(i - 1, 0) * BLOCK, BLOCK)
        dk_cur, dv_cur, dk_prev, dv_prev = [], [], [], []
        ds_acc = jnp.zeros((8, 128), F32)
        for hk in range(N_KV_HEADS):
            ks = slice(hk * HEAD_DIM, (hk + 1) * HEAD_DIM)
            qg = (_stack_heads(q_ref, hk, F32) * ATTN_SCALE).astype(BF16)
            dog = _stack_heads(dy_s, hk, F32)
            og = _stack_heads(y_ref, hk, F32)
            dog_b = dog.astype(BF16)
            kp = kp_ref[:, ks].astype(BF16)
            kc = kc_ref[:, ks].astype(BF16)
            vp = vp_ref[:, ks].astype(BF16)
            vc = vc_ref[:, ks].astype(BF16)
            s_prev = _dot_nt(qg, kp)
            s_cur = _dot_nt(qg, kc)
            dp_prev = _dot_nt(dog_b, vp)
            dp_cur = _dot_nt(dog_b, vc)
            dvec = jnp.sum(dog * og, axis=-1, keepdims=True)
            pp, pc, dsp, dsc = [], [], [], []
            for g in range(GROUP):
                h = GROUP * hk + g
                rows = slice(g * BLOCK, (g + 1) * BLOCK)
                p_prev, p_cur, p_sink = _attn_probs(
                    s_prev[rows], s_cur[rows], sink_ref[h], bias_ref[h], lse_ref[:, h:h + 1])
                d_h = dvec[rows]
                pp.append(p_prev.astype(BF16))
                pc.append(p_cur.astype(BF16))
                dsp.append((p_prev * (dp_prev[rows] - d_h)).astype(BF16))
                dsc.append((p_cur * (dp_cur[rows] - d_h)).astype(BF16))
                dsink = -jnp.sum(p_sink * d_h, axis=0, keepdims=True)
                ds_acc = ds_acc + jnp.where(jnp.logical_and(lane == h, sub == 1), dsink, 0.0)
            pp = jnp.concatenate(pp, axis=0)
            pc = jnp.concatenate(pc, axis=0)
            dsp = jnp.concatenate(dsp, axis=0)
            dsc = jnp.concatenate(dsc, axis=0)
            dqg = (_dot(dsp, kp) + _dot(dsc, kc)) * ATTN_SCALE
            for g in range(GROUP):
                h = GROUP * hk + g
                dq_ref[:, h * HEAD_DIM:(h + 1) * HEAD_DIM] = dqg[g * BLOCK:(g + 1) * BLOCK].astype(BF16)
            dk_ref[pl.ds(r_cur, BLOCK), ks] = _dot_tn(dsc, qg)
            dv_ref[pl.ds(r_cur, BLOCK), ks] = _dot_tn(pc, dog_b)
            dk_prev.append(_dot_tn(dsp, qg))
            dv_prev.append(_dot_tn(pp, dog_b))
        ds_ref[:, 0:128] += ds_acc

        @pl.when(i > 0)
        def _():
            for hk in range(N_KV_HEADS):
                ks = slice(hk * HEAD_DIM, (hk + 1) * HEAD_DIM)
                dk_ref[pl.ds(r_prev, BLOCK), ks] += dk_prev[hk]
                dv_ref[pl.ds(r_prev, BLOCK), ks] += dv_prev[hk]

    prev = lambda c: (lambda i: (jnp.maximum(i - 1, 0), c))
    cur = lambda c: (lambda i: (i, c))
    blk = pl.BlockSpec((BLOCK, 1024), lambda i: (i, 0))
    whole = pl.BlockSpec((T, D_KV), lambda i: (0, 0))
    return pl.pallas_call(
        body,
        name="attn_bwd",
        grid=(nb,),
        in_specs=[
            pl.BlockSpec(memory_space=pltpu.SMEM),
            _bias_spec(),
            pl.BlockSpec((BLOCK, 1024), lambda i: (i, COL_Q // 4)),
            pl.BlockSpec((BLOCK, D_KV), prev(COL_K)),
            pl.BlockSpec((BLOCK, D_KV), cur(COL_K)),
            pl.BlockSpec((BLOCK, D_KV), prev(COL_V)),
            pl.BlockSpec((BLOCK, D_KV), cur(COL_V)),
            pl.BlockSpec((BLOCK, 512), lambda i: (i, COL_ATTN_GATE // 2)),
            pl.BlockSpec((BLOCK, 512), lambda i: (i, COL_ATTN_GATE // 2 + 1)),
            blk,
            pl.BlockSpec((BLOCK, N_Q_HEADS), lambda i: (i, 0)),
            blk,
            pl.BlockSpec((8, 128), lambda i: (0, 0)),
        ],
        out_specs=[blk, whole, whole, blk, pl.BlockSpec((8, 1024), lambda i: (0, 0))],
        out_shape=[_sds((T, 1024), BF16), _sds((T, D_KV), F32), _sds((T, D_KV), F32), _sds((T, 1024), BF16),
                   _sds((8, 1024), F32)],
        scratch_shapes=[pltpu.VMEM((BLOCK, 1024), F32)],
        compiler_params=_params(("arbitrary",), 48),
    )(sinks, *_hbm(bias, proj, proj, proj, proj, proj, proj, proj, y_attn, lse, dz_attn, token))


def _head(x, target, z_rnn, z_attn, proj, b_gate, g_post, w_rnn_out, w_attn_out, w_out):
    T = x.shape[0]
    tm = 256

    def body(x_ref, t_ref, zr_ref, za_ref, ml0_ref, ml1_ref, ml2_ref, ml3_ref, bg_ref, gp_ref, wr_ref, wa_ref, wo_ref,
             dyx_ref, dzr_ref, dza_ref, dml_ref, dout_ref, dbr_ref, dba_ref, mt_ref, zat_ref, sm_ref):
        @pl.when(pl.program_id(0) == 0)
        def _():
            sm_ref[...] = jnp.zeros_like(sm_ref)

        wr, wa, wo = wr_ref[...], wa_ref[...], wo_ref[...]
        br_rnn = _dot(zr_ref[...], wr)
        br_attn = _dot(za_ref[...], wa)
        zat_ref[...] = za_ref[...].astype(F32).T.astype(BF16)
        ml_rnn = jnp.concatenate([ml0_ref[...], ml1_ref[...]], axis=1).astype(F32)
        ml_attn = jnp.concatenate([ml2_ref[...], ml3_ref[...]], axis=1).astype(F32)
        g_rnn = _sigmoid(ml_rnn + bg_ref[:, 0:D_MODEL])
        g_attn = _sigmoid(ml_attn + bg_ref[:, D_MODEL:2 * D_MODEL])
        merged = g_rnn * br_rnn + g_attn * br_attn
        mb = merged.astype(BF16)
        mt_ref[...] = merged.T.astype(BF16)
        out = _dot(mb, wo)
        rstd = lax.rsqrt(jnp.mean(out * out, axis=-1, keepdims=True) + EPS)
        n = out * rstd
        gp = gp_ref[...]
        err = (x_ref[...] + n * gp) - t_ref[...]
        sm_ref[pl.ds(3, 1), :] += 0.5 * jnp.sum(jnp.mean(err * err, axis=-1, keepdims=True), axis=0, keepdims=True)
        dy = err * (1.0 / D_MODEL)
        dyx_ref[...] = dy
        sm_ref[pl.ds(0, 1), :] += jnp.sum(dy * n, axis=0, keepdims=True)
        dn = dy * gp
        dout = (rstd * (dn - n * jnp.mean(dn * n, axis=-1, keepdims=True))).astype(BF16)
        dout_ref[...] = dout
        dmerged = _dot_nt(dout, wo)
        dml_r = (dmerged * br_rnn) * (g_rnn * (1.0 - g_rnn))
        dml_a = (dmerged * br_attn) * (g_attn * (1.0 - g_attn))
        dml_ref[:, 0:D_MODEL] = dml_r.astype(BF16)
        dml_ref[:, D_MODEL:2 * D_MODEL] = dml_a.astype(BF16)
        sm_ref[pl.ds(1, 1), :] += jnp.sum(dml_r, axis=0, keepdims=True)
        sm_ref[pl.ds(2, 1), :] += jnp.sum(dml_a, axis=0, keepdims=True)
        dbr = (dmerged * g_rnn).astype(BF16)
        dba = (dmerged * g_attn).astype(BF16)
        dbr_ref[...] = dbr
        dba_ref[...] = dba
        dzr_ref[...] = _dot_nt(dbr, wr)
        dza_ref[...] = _dot_nt(dba, wa)

    tile = pl.BlockSpec((tm, D_MODEL), lambda i: (i, 0))
    wspec = pl.BlockSpec((D_MODEL, D_MODEL), lambda i: (0, 0))
    ml = lambda q: pl.BlockSpec((tm, 512), lambda i: (i, COL_MERGE // 2 + q))
    return pl.pallas_call(
        body,
        name="head",
        grid=(T // tm,),
        in_specs=[
            tile, tile, tile, tile,
            ml(0), ml(1), ml(2), ml(3),
            pl.BlockSpec((1, 2 * D_MODEL), lambda i: (0, 0)),
            pl.BlockSpec((1, D_MODEL), lambda i: (0, 0)),
            wspec, wspec, wspec,
        ],
        out_specs=[
            tile, tile, tile,
            pl.BlockSpec((tm, 2 * D_MODEL), lambda i: (i, 0)),
            tile, tile, tile,
            pl.BlockSpec((D_MODEL, tm), lambda i: (0, i)), pl.BlockSpec((D_MODEL, tm), lambda i: (0, i)),
            pl.BlockSpec((8, D_MODEL), lambda i: (0, 0)),
        ],
        out_shape=[
            _sds((T, D_MODEL), F32), _sds((T, D_MODEL), F32), _sds((T, D_MODEL), F32),
            _sds((T, 2 * D_MODEL), BF16),
            _sds((T, D_MODEL), BF16), _sds((T, D_MODEL), BF16), _sds((T, D_MODEL), BF16),
            _sds((D_MODEL, T), BF16), _sds((D_MODEL, T), BF16),
            _sds((8, D_MODEL), F32),
        ],
        compiler_params=_params(("arbitrary",), 56),
    )(*_hbm(x, target, z_rnn, z_attn, proj, proj, proj, proj, b_gate, g_post, w_rnn_out, w_attn_out, w_out))


def _matmul_t(at, b, name):
    M, T = at.shape
    N = b.shape[1]
    tk = min(2048, T)
    nt = T // tk

    def body(a_ref, b_ref, o_ref, ob_ref):
        @pl.when(pl.program_id(0) == 0)
        def _():
            o_ref[...] = jnp.zeros_like(o_ref)

        o_ref[...] += _dot(a_ref[...], b_ref[...])

        @pl.when(pl.program_id(0) == nt - 1)
        def _():
            ob_ref[...] = o_ref[...].astype(BF16)

    whole = pl.BlockSpec((M, N), lambda t: (0, 0))
    return pl.pallas_call(
        body,
        name=name,
        grid=(nt,),
        in_specs=[pl.BlockSpec((M, tk), lambda t: (0, t)), pl.BlockSpec((tk, N), lambda t: (t, 0))],
        out_specs=[whole, whole],
        out_shape=[_sds((M, N), F32), _sds((M, N), BF16)],
        compiler_params=_params(("arbitrary",), 48),
    )(*_hbm(at, b))


DPROJ_WIDTHS = (D_RNN, D_RNN, 1024, D_KV, D_KV, 1024, 2 * D_MODEL)


def _dproj_segments():
    segs, start = [[] for _ in range(N_CHIPS)], 0
    for p, width in enumerate(DPROJ_WIDTHS):
        for c in range(N_CHIPS):
            lo, hi = max(start, c * W_IN_SHARD), min(start + width, (c + 1) * W_IN_SHARD)
            if lo < hi:
                segs[c].append((p, lo - start, hi - start, lo - c * W_IN_SHARD, hi - c * W_IN_SHARD))
        start += width
    return segs


def _dh_bwd(pieces, w_in_g, x, dyx, g_pre, token):
    T = x.shape[0]
    tm = min(512, T)
    n = len(pieces)
    segs = _dproj_segments()

    def body(*refs):
        p_refs, w_hbm, x_ref, dyx_ref, g_ref = refs[0:n], refs[n], refs[n + 1], refs[n + 2], refs[n + 3]
        gx_ref, dg_ref, w_ref, w_sems = refs[n + 5], refs[n + 6], refs[n + 7], refs[n + 8]
        first = pl.program_id(0) == 0
        w_copies = [pltpu.make_async_copy(w_hbm.at[c], w_ref.at[c], w_sems.at[c]) for c in range(N_CHIPS)]

        @pl.when(first)
        def _():
            for cp in w_copies:
                cp.start()
            dg_ref[...] = jnp.zeros_like(dg_ref)

        dh = None
        for c in range(N_CHIPS):
            pl.when(first)(w_copies[c].wait)
            for p, a0, a1, u0, u1 in segs[c]:
                part = _dot_nt(p_refs[p][:, a0:a1].astype(BF16), w_ref[c, :, u0:u1])
                dh = part if dh is None else dh + part
        xv = x_ref[...]
        rstd = lax.rsqrt(jnp.mean(xv * xv, axis=-1, keepdims=True) + EPS)
        nx = xv * rstd
        dhg = dh * g_ref[...]
        gx_ref[...] = dyx_ref[...] + rstd * (dhg - nx * jnp.mean(dhg * nx, axis=-1, keepdims=True))
        dg_ref[pl.ds(0, 1), :] += jnp.sum(dh * nx, axis=0, keepdims=True)

    tile = pl.BlockSpec((tm, D_MODEL), lambda i: (i, 0))
    return pl.pallas_call(
        body,
        name="dh_bwd",
        grid=(T // tm,),
        in_specs=[pl.BlockSpec((tm, w), lambda i: (i, 0)) for w in DPROJ_WIDTHS] + [
            ANY, tile, tile,
            pl.BlockSpec((1, D_MODEL), lambda i: (0, 0)),
            pl.BlockSpec((8, 128), lambda i: (0, 0)),
        ],
        out_specs=[tile, pl.BlockSpec((8, D_MODEL), lambda i: (0, 0))],
        out_shape=[_sds((T, D_MODEL), F32), _sds((8, D_MODEL), F32)],
        scratch_shapes=[pltpu.VMEM(w_in_g.shape, BF16), pltpu.SemaphoreType.DMA((N_CHIPS,))],
        compiler_params=_params(("arbitrary",), 56),
    )(*_hbm(*pieces, w_in_g, x, dyx, g_pre, token))


def _dw_in(ht, pieces):
    T = ht.shape[1]
    tk = min(1024, T)
    nt = T // tk
    n = len(pieces)
    segs = _dproj_segments()

    def body(*refs):
        h_ref, p_refs, o_ref, ob_ref = refs[0], refs[1:n + 1], refs[n + 1], refs[n + 2]

        @pl.when(pl.program_id(1) == 0)
        def _():
            o_ref[...] = jnp.zeros_like(o_ref)

        for c in range(N_CHIPS):
            @pl.when(pl.program_id(0) == c)
            def _():
                for p, a0, a1, u0, u1 in segs[c]:
                    o_ref[:, u0:u1] += _dot(h_ref[...], p_refs[p][:, a0:a1].astype(BF16))

        @pl.when(pl.program_id(1) == nt - 1)
        def _():
            ob_ref[...] = o_ref[...].astype(BF16)

    def piece_spec(p):
        chips = [c for c in range(N_CHIPS) if any(s[0] == p for s in segs[c])]

        def index(c, t):
            used = functools.reduce(jnp.logical_or, [c == k for k in chips])
            return (jnp.where(used, t, 0), 0)

        return pl.BlockSpec((tk, DPROJ_WIDTHS[p]), index)

    return pl.pallas_call(
        body,
        name="dw_in",
        grid=(N_CHIPS, nt),
        in_specs=[pl.BlockSpec((D_MODEL, tk), lambda c, t: (0, t))] + [piece_spec(p) for p in range(n)],
        out_specs=[pl.BlockSpec((None, D_MODEL, W_IN_SHARD), lambda c, t: (c, 0, 0))] * 2,
        out_shape=[_sds((N_CHIPS, D_MODEL, W_IN_SHARD), F32), _sds((N_CHIPS, D_MODEL, W_IN_SHARD), BF16)],
        compiler_params=_params(("parallel", "arbitrary"), 56),
    )(*_hbm(ht, *pieces))


ELEMENTWISE_TILE_BYTES = MIB


def _row_tile(rows, cols, limit=ELEMENTWISE_TILE_BYTES):
    if rows * cols * 4 <= limit:
        return rows
    for t in (512, 256, 128, 64, 32, 16, 8):
        if rows % t == 0 and t * cols * 4 <= limit:
            return t
    return rows


def _group_tiles(groups):
    tiles = [_row_tile(g[0].shape[0], g[0].shape[1] * len(g)) for g in groups]
    steps = max(g[0].shape[0] // t for g, t in zip(groups, tiles))
    return steps, [g[0].shape[0] // steps for g in groups]


def _chip_sum(groups, chip_core, name):
    ps = [p for group_ps, _ in groups for p in group_ps]
    gots = [g for _, group_gots in groups for g in group_gots]
    n = len(ps)
    steps, group_rows = _group_tiles([group_ps for group_ps, _ in groups])
    rows = [tr for (group_ps, _), tr in zip(groups, group_rows) for _ in group_ps]

    def body(jc_ref, *refs):
        for a in range(n):
            p_ref, g0_ref, g1_ref, g2_ref, o_ref = refs[a], refs[n + 3 * a], refs[n + 3 * a + 1], refs[n + 3 * a + 2], \
                refs[4 * n + a]
            o_ref[...] = ((p_ref[...] + g0_ref[...].astype(F32)) + g1_ref[...].astype(F32)) + g2_ref[...].astype(F32)

    tile = lambda p, tr: pl.BlockSpec((tr, p.shape[1]), lambda i, jc_ref: (i, 0))
    rel = lambda p, tr, r: pl.BlockSpec((None, tr, p.shape[1]), lambda i, jc_ref: (r, i, 0))
    half = lambda p, tr: pl.BlockSpec((tr, p.shape[1]), lambda i, jc_ref: (jc_ref[1] * steps + i, 0))
    outs = pl.pallas_call(
        body,
        name=name,
        grid_spec=pltpu.PrefetchScalarGridSpec(
            num_scalar_prefetch=1,
            grid=(steps,),
            in_specs=[tile(p, tr) for p, tr in zip(ps, rows)] + [
                rel(p, tr, r) for p, tr in zip(ps, rows) for r in range(3)],
            out_specs=[half(p, tr) for p, tr in zip(ps, rows)],
        ),
        out_shape=[_sds((2 * p.shape[0], p.shape[1]), F32) for p in ps],
        compiler_params=_params(("parallel",), 48),
    )(chip_core, *_hbm(*ps, *[g for got in gots for g in (got, got, got)]))
    return list(outs)


def _place_shards(shards, chip, name):
    n = len(shards)
    tiles = [_row_tile(s.shape[0], s.shape[1]) for s in shards]
    steps = max(s.shape[0] // t for s, t in zip(shards, tiles))
    tiles = [s.shape[0] // steps for s in shards]

    def body(j_ref, *refs):
        for a in range(n):
            refs[n + a][...] = refs[a][...].astype(BF16)

    return pl.pallas_call(
        body,
        name=name,
        grid_spec=pltpu.PrefetchScalarGridSpec(
            num_scalar_prefetch=1,
            grid=(steps,),
            in_specs=[pl.BlockSpec((t, s.shape[1]), lambda i, j_ref: (i, 0)) for s, t in zip(shards, tiles)],
            out_specs=[pl.BlockSpec((None, t, s.shape[1]), lambda i, j_ref: (j_ref[0], i, 0))
                       for s, t in zip(shards, tiles)],
        ),
        out_shape=[_sds((N_CHIPS,) + s.shape, BF16) for s in shards],
        compiler_params=_params(("parallel",), 48),
    )(chip, *_hbm(*shards))


def _adamw_update(w, g, m, v):
    c1 = 1.0 - ADAM_B1 ** ADAM_STEP
    c2 = 1.0 - ADAM_B2 ** ADAM_STEP
    nm = ADAM_B1 * m + (1.0 - ADAM_B1) * g
    nv = ADAM_B2 * v + (1.0 - ADAM_B2) * (g * g)
    return (-ADAM_LR) * ((nm / c1) / (jnp.sqrt(nv / c2) + ADAM_EPS) + ADAM_WD * w), nm, nv


def _adamw(groups, name):
    params = [p for group in groups for p in group]
    n = len(params)
    steps, group_rows = _group_tiles([[p[0] for p in group] for group in groups])
    rows = [tr for group, tr in zip(groups, group_rows) for _ in group]

    def body(*refs):
        for a in range(n):
            w_ref, g_ref, m_ref, v_ref = refs[4 * a:4 * a + 4]
            d_ref, nm_ref, nv_ref, go_ref = refs[4 * n + 4 * a:4 * n + 4 * a + 4]
            g = g_ref[...]
            d_ref[...], nm_ref[...], nv_ref[...] = _adamw_update(w_ref[...], g, m_ref[...], v_ref[...])
            go_ref[...] = g

    specs = [pl.BlockSpec((tr, p[0].shape[1]), lambda i: (i, 0)) for p, tr in zip(params, rows) for _ in range(4)]
    outs = pl.pallas_call(
        body, name=name, grid=(steps,), in_specs=specs, out_specs=specs,
        out_shape=[_sds(p[0].shape, F32) for p in params for _ in range(4)],
        compiler_params=_params(("parallel",), 48),
    )(*_hbm(*[t for p in params for t in p]))
    return [tuple(outs[4 * a:4 * a + 4]) for a in range(n)]


def _adamw_whole(params, name):
    n = len(params)

    def body(*refs):
        for a in range(n):
            w_ref, g_ref, m_ref, v_ref = refs[4 * a:4 * a + 4]
            d_ref, nm_ref, nv_ref = refs[4 * n + 3 * a:4 * n + 3 * a + 3]
            d_ref[...], nm_ref[...], nv_ref[...] = _adamw_update(w_ref[...], g_ref[...], m_ref[...], v_ref[...])

    def whole(t):
        return pl.BlockSpec(t.shape, lambda i: (0,) * t.ndim)

    flat = [t for p in params for t in p]
    like = [p[0] for p in params for _ in range(3)]
    outs = pl.pallas_call(
        body, name=name, grid=(1,), in_specs=[whole(t) for t in flat], out_specs=[whole(t) for t in like],
        out_shape=[_sds(t.shape, F32) for t in like], compiler_params=_params(("arbitrary",), 48),
    )(*_hbm(*flat))
    return [tuple(outs[3 * a:3 * a + 3]) for a in range(n)]


def _place():
    return lax.axis_index("x"), lax.axis_index("y"), lax.axis_index("c")


def _chip_of(x, y, r):
    return (x ^ (r >> 1), y ^ (r & 1))


ANY = pl.BlockSpec(memory_space=pl.ANY)


def _gather_weights(placed, cw8):
    nbig = len(placed)
    halves = [s.shape[1] // 2 for s in placed]
    pieces = [max(1, h // 64) for h in halves]
    rows = [h // p for h, p in zip(halves, pieces)]
    order = [(a, q) for q in range(max(pieces)) for a in range(nbig) if q < pieces[a]]
    ici_sem = {(a, q, r): 3 * i + (r - 1) for i, (a, q) in enumerate(order) for r in (1, 2, 3)}
    cw_sem = {r: 3 * len(order) + (r - 1) for r in (1, 2, 3)}
    d2d_sem = {key: 3 * len(order) + 3 + k for key, k in ici_sem.items()}
    nsem = 6 * len(order) + 3

    def body(*refs):
        cw_ref, dsts, gcw_ref = refs[nbig], refs[nbig + 1:2 * nbig + 1], refs[2 * nbig + 1]
        send_sems, recv_sems = refs[2 * nbig + 2:]
        x, y, c = _place()
        j = 2 * x + y

        def piece_rows(a, q, core):
            return pl.ds(pl.multiple_of(core * halves[a] + q * rows[a], 16), rows[a])

        def ici(a, q, r):
            tx, ty = _chip_of(x, y, r)
            k = ici_sem[(a, q, r)]
            region = dsts[a].at[j, piece_rows(a, q, c), :]
            return pltpu.make_async_remote_copy(
                src_ref=region, dst_ref=region, send_sem=send_sems.at[k], recv_sem=recv_sems.at[k],
                device_id=(tx, ty, c), device_id_type=MESH)

        def ici_landed(a, q, r):
            tx, ty = _chip_of(x, y, r)
            k = ici_sem[(a, q, r)]
            region = dsts[a].at[2 * tx + ty, piece_rows(a, q, c), :]
            return pltpu.make_async_remote_copy(
                src_ref=region, dst_ref=region, send_sem=send_sems.at[k], recv_sem=recv_sems.at[k],
                device_id=(tx, ty, c), device_id_type=MESH)

        def d2d(a, q, r, core):
            tx, ty = _chip_of(x, y, r)
            k = d2d_sem[(a, q, r)]
            region = dsts[a].at[2 * tx + ty, piece_rows(a, q, core), :]
            return pltpu.make_async_remote_copy(
                src_ref=region, dst_ref=region, send_sem=send_sems.at[k], recv_sem=recv_sems.at[k],
                device_id=(x, y, 1 - c), device_id_type=MESH)

        def cw_copy(r):
            tx, ty = _chip_of(x, y, r)
            k = cw_sem[r]
            return pltpu.make_async_remote_copy(
                src_ref=cw_ref, dst_ref=gcw_ref.at[j], send_sem=send_sems.at[k], recv_sem=recv_sems.at[k],
                device_id=(tx, ty, c), device_id_type=MESH)

        def cw_landed(r):
            tx, ty = _chip_of(x, y, r)
            k = cw_sem[r]
            region = gcw_ref.at[2 * tx + ty]
            return pltpu.make_async_remote_copy(
                src_ref=region, dst_ref=region, send_sem=send_sems.at[k], recv_sem=recv_sems.at[k],
                device_id=(tx, ty, c), device_id_type=MESH)

        def relay(a, q, origin, to):
            ox, oy = _chip_of(x, y, origin)
            tx, ty = _chip_of(x, y, to)
            k = ici_sem[(a, q, 3)]
            region = dsts[a].at[2 * ox + oy, piece_rows(a, q, c), :]
            return pltpu.make_async_remote_copy(
                src_ref=region, dst_ref=region, send_sem=send_sems.at[k], recv_sem=recv_sems.at[k],
                device_id=(tx, ty, c), device_id_type=MESH)

        first = [ici(a, q, r) for (a, q) in order for r in (1, 2)] + [cw_copy(r) for r in (1, 2, 3)]
        for cp in first:
            cp.start()
        passed = []
        for (a, q) in order:
            for r in (1, 2):
                ici_landed(a, q, r).wait_recv()
                if q % 2 == r - 1:
                    cp = relay(a, q, r, 3 - r)
                    cp.start()
                    passed.append(cp)
                cp = d2d(a, q, r, c)
                cp.start()
                passed.append(cp)
        for (a, q) in order:
            ici_landed(a, q, 3).wait_recv()
            cp = d2d(a, q, 3, c)
            cp.start()
            passed.append(cp)
        for r in (1, 2, 3):
            cw_landed(r).wait_recv()
        for (a, q) in order:
            for r in (1, 2, 3):
                d2d(a, q, r, 1 - c).wait_recv()
        for cp in first + passed:
            cp.wait_send()

    return pl.pallas_call(
        body,
        name="gather_weights",
        in_specs=[ANY] * (nbig + 1),
        out_specs=[ANY] * (nbig + 1),
        out_shape=[_sds(s.shape, s.dtype) for s in placed] + [_sds((N_CHIPS,) + cw8.shape, cw8.dtype)],
        input_output_aliases={a: a for a in range(nbig)},
        scratch_shapes=[pltpu.SemaphoreType.DMA((nsem,)), pltpu.SemaphoreType.DMA((nsem,))],
    )(*placed, cw8)


def _gather_late_start(placed, after, name):
    n = len(placed)
    halves = [s.shape[1] // 2 for s in placed]

    def body(*refs):
        g_refs = refs[0:n]
        send_sems, recv_sems, token = refs[n + 1], refs[n + 2], refs[-1]
        x, y, c = _place()
        j = 2 * x + y
        for a in range(n):
            mine = g_refs[a].at[j, pl.ds(pl.multiple_of(c * halves[a], 16), halves[a]), :]
            for r in (1, 2, 3):
                tx, ty = _chip_of(x, y, r)
                for to_core in (0, 1):
                    k = ((a * 3 + (r - 1)) * 2 + c) * 2 + to_core
                    pltpu.make_async_remote_copy(
                        src_ref=mine, dst_ref=mine, send_sem=send_sems.at[k], recv_sem=recv_sems.at[k],
                        device_id=(tx, ty, to_core), device_id_type=MESH).start()
        token[...] = jnp.zeros_like(token)

    hbm = lambda t: pltpu.HBM(t.shape, t.dtype)
    keep = lambda t: pltpu.with_memory_space_constraint(t, pltpu.HBM)
    nsem = 12 * n
    outs = pl.pallas_call(
        body,
        name=name,
        in_specs=[HBM] * n + [ANY],
        out_specs=(SEM, SEM, *[HBM] * n, pl.BlockSpec(memory_space=pltpu.VMEM)),
        out_shape=(pltpu.SemaphoreType.DMA((nsem,)), pltpu.SemaphoreType.DMA((nsem,)), *[hbm(p) for p in placed],
                   jax.ShapeDtypeStruct((8, 128), F32)),
        input_output_aliases={i: 2 + i for i in range(n)},
        compiler_params=pltpu.CompilerParams(has_side_effects=DATAFLOW),
    )(*[keep(p) for p in placed], after)
    return outs[0], outs[1], list(outs[2:2 + n]), outs[-1]


def _gather_late_wait(send_sems, recv_sems, thru, after, name):
    n = len(thru)
    halves = [s.shape[1] // 2 for s in thru]

    def body(*refs):
        g_refs = refs[0:n]
        send_sems, recv_sems = refs[n], refs[n + 1]
        x, y, c = _place()
        j = 2 * x + y
        for a in range(n):
            mine = g_refs[a].at[j, pl.ds(pl.multiple_of(c * halves[a], 16), halves[a]), :]
            for r in (1, 2, 3):
                tx, ty = _chip_of(x, y, r)
                for other in (0, 1):
                    k_out = ((a * 3 + (r - 1)) * 2 + c) * 2 + other
                    pltpu.make_async_remote_copy(
                        src_ref=mine, dst_ref=mine, send_sem=send_sems.at[k_out], recv_sem=recv_sems.at[k_out],
                        device_id=(tx, ty, other), device_id_type=MESH).wait_send()
                    k_in = ((a * 3 + (r - 1)) * 2 + other) * 2 + c
                    theirs = g_refs[a].at[2 * tx + ty, pl.ds(other * halves[a], halves[a]), :]
                    pltpu.make_async_remote_copy(
                        src_ref=theirs, dst_ref=theirs, send_sem=send_sems.at[k_in], recv_sem=recv_sems.at[k_in],
                        device_id=(tx, ty, other), device_id_type=MESH).wait_recv()

    hbm = lambda t: pltpu.HBM(t.shape, t.dtype)
    outs = pl.pallas_call(
        body,
        name=name,
        in_specs=[HBM] * n + [SEM, SEM, ANY],
        out_specs=[HBM] * n,
        out_shape=[hbm(t) for t in thru],
        input_output_aliases={i: i for i in range(n)},
        compiler_params=pltpu.CompilerParams(has_side_effects=DATAFLOW),
    )(*thru, send_sems, recv_sems, after)
    return list(outs)


D2D_PIECE_ROWS = 64
PAIR_SUM_TILE_BYTES = 2 * MIB


def _pair_sum(gs, gbs, chip_core, name):
    n = len(gs)
    nch, R, C = gs[0].shape
    h = R // 2
    tr = _row_tile(h, C * n, PAIR_SUM_TILE_BYTES)
    nt = h // tr
    rows = min(D2D_PIECE_ROWS, tr)

    def body(jc_ref, *refs):
        g_refs, gb_refs, p_refs, pb_refs = refs[0:n], refs[n:2 * n], refs[2 * n:3 * n], refs[3 * n:4 * n]
        got_refs, send_sems, recv_sems = refs[4 * n:5 * n], refs[5 * n], refs[5 * n + 1]
        i, j = pl.program_id(0), pl.program_id(1)
        x, y, c = _place()

        def copy(a, ti, tj, first, count):
            src_rows = pl.ds(pl.multiple_of((1 - c) * h + ti * tr + first, 16), count)
            dst_rows = pl.ds(pl.multiple_of(ti * tr + first, 16), count)
            return pltpu.make_async_remote_copy(
                src_ref=gb_refs[a].at[tj, src_rows, :], dst_ref=got_refs[a].at[tj, dst_rows, :],
                send_sem=send_sems.at[a, ti, tj], recv_sem=recv_sems.at[a, ti, tj],
                device_id=(x, y, 1 - c), device_id_type=MESH)

        @pl.when((i == 0) & (j == 0))
        def _():
            for ti in range(nt):
                for tj in range(nch):
                    for a in range(n):
                        for q in range(tr // rows):
                            copy(a, ti, tj, q * rows, rows).start()

        for a in range(n):
            copy(a, i, j, 0, tr).wait()
            s = g_refs[a][...] + got_refs[a][j, pl.ds(pl.multiple_of(i * tr, 16), tr), :].astype(F32)
            pb_refs[a][...] = s.astype(BF16)

            @pl.when(j == jc_ref[0])
            def _():
                p_refs[a][...] = s

    by_chip = pl.BlockSpec((None, tr, C), lambda i, j, jc_ref: (j, i, 0))
    outs = pl.pallas_call(
        body,
        name=name,
        grid_spec=pltpu.PrefetchScalarGridSpec(
            num_scalar_prefetch=1,
            grid=(nt, nch),
            in_specs=[pl.BlockSpec((None, tr, C), lambda i, j, jc_ref: (j, jc_ref[1] * nt + i, 0))] * n + [ANY] * n,
            out_specs=[pl.BlockSpec((tr, C), lambda i, j, jc_ref: (i, 0))] * n + [by_chip] * n,
            scratch_shapes=[pltpu.VMEM((nch, h, C), BF16)] * n + [pltpu.SemaphoreType.DMA((n, nt, nch))] * 2,
        ),
        out_shape=[_sds((h, C), F32)] * n + [_sds((nch, h, C), BF16)] * n,
        compiler_params=_params(("arbitrary", "arbitrary"), 48),
    )(chip_core, *_hbm(*gs, *gbs))
    return list(outs[:n]), list(outs[n:])


HBM = pl.BlockSpec(memory_space=pltpu.HBM)
SEM = pl.BlockSpec(memory_space=pltpu.SEMAPHORE)
DATAFLOW = pltpu.SideEffectType.DATAFLOW_SIDE_EFFECTING


def _chip_copy(p_refs, land_refs, send_sems, recv_sems, a, r, blocked):
    x, y, c = _place()
    tx, ty = _chip_of(x, y, r)
    k = a * 3 + (r - 1)
    return pltpu.make_async_remote_copy(
        src_ref=p_refs[a].at[2 * tx + ty] if blocked else p_refs[a], dst_ref=land_refs[a].at[r - 1],
        send_sem=send_sems.at[k], recv_sem=recv_sems.at[k], device_id=(tx, ty, c), device_id_type=MESH)


def _chip_exchange_start(psums, name, blocked=True):
    n = len(psums)
    lands = [lax.empty((3,) + (p.shape[1:] if blocked else p.shape), p.dtype) for p in psums]

    def body(*refs):
        p_refs, land_refs = refs[0:n], refs[n:2 * n]
        send_sems, recv_sems, token = refs[2 * n], refs[2 * n + 1], refs[-1]
        for a in range(n):
            for r in (1, 2, 3):
                _chip_copy(p_refs, land_refs, send_sems, recv_sems, a, r, blocked).start()
        token[...] = jnp.zeros_like(token)

    hbm = lambda t: pltpu.HBM(t.shape, t.dtype)
    keep = lambda t: pltpu.with_memory_space_constraint(t, pltpu.HBM)
    outs = pl.pallas_call(
        body,
        name=name,
        in_specs=[HBM] * (2 * n),
        out_specs=(SEM, SEM, *[HBM] * (2 * n), pl.BlockSpec(memory_space=pltpu.VMEM)),
        out_shape=(pltpu.SemaphoreType.DMA((3 * n,)), pltpu.SemaphoreType.DMA((3 * n,)),
                   *[hbm(p) for p in psums], *[hbm(l) for l in lands], _sds((8, 128), F32)),
        input_output_aliases={i: 2 + i for i in range(2 * n)},
        compiler_params=pltpu.CompilerParams(has_side_effects=DATAFLOW),
    )(*[keep(p) for p in psums], *[keep(l) for l in lands])
    return outs[0], outs[1], list(outs[2:2 + n]), list(outs[2 + n:2 + 2 * n]), outs[-1]


def _chip_exchange_wait(send_sems, recv_sems, p_thru, land_thru, after, name, blocked=True):
    n = len(p_thru)

    def body(*refs):
        p_refs, land_refs = refs[0:n], refs[n:2 * n]
        send_sems, recv_sems = refs[2 * n], refs[2 * n + 1]
        for a in range(n):
            for r in (1, 2, 3):
                copy = _chip_copy(p_refs, land_refs, send_sems, recv_sems, a, r, blocked)
                copy.wait_send()
                copy.wait_recv()

    hbm = lambda t: pltpu.HBM(t.shape, t.dtype)
    outs = pl.pallas_call(
        body,
        name=name,
        in_specs=[HBM] * (2 * n) + [SEM, SEM, ANY],
        out_specs=[HBM] * (2 * n),
        out_shape=[hbm(p) for p in p_thru] + [hbm(l) for l in land_thru],
        input_output_aliases={i: i for i in range(2 * n)},
        compiler_params=pltpu.CompilerParams(has_side_effects=DATAFLOW),
    )(*p_thru, *land_thru, send_sems, recv_sems, after)
    return list(outs[0:n]), list(outs[n:2 * n])


def _pair_share(fulls):
    n = len(fulls)
    halves = [f.shape[0] // 2 for f in fulls]

    def body(*refs):
        full_refs = refs[n:2 * n]
        send_sems, recv_sems = refs[2 * n:]
        x, y, c = _place()

        def half_of(a, core):
            return full_refs[a].at[pl.ds(pl.multiple_of(core * halves[a], 8), halves[a]), :]

        def remote(a, src, dst):
            return pltpu.make_async_remote_copy(
                src_ref=src, dst_ref=dst, send_sem=send_sems.at[a], recv_sem=recv_sems.at[a],
                device_id=(x, y, 1 - c), device_id_type=MESH)

        for a in range(n):
            for q in range(halves[a] // D2D_PIECE_ROWS):
                piece = full_refs[a].at[
                    pl.ds(pl.multiple_of(c * halves[a] + q * D2D_PIECE_ROWS, 8), D2D_PIECE_ROWS), :]
                remote(a, piece, piece).start()
        for a in range(n):
            remote(a, half_of(a, c), half_of(a, c)).wait_send()
            remote(a, half_of(a, 1 - c), half_of(a, 1 - c)).wait_recv()

    return pl.pallas_call(
        body,
        name="pair_share",
        in_specs=[ANY] * n,
        out_specs=[ANY] * n,
        out_shape=[_sds(f.shape, F32) for f in fulls],
        input_output_aliases={a: a for a in range(n)},
        scratch_shapes=[pltpu.SemaphoreType.DMA((n,)), pltpu.SemaphoreType.DMA((n,))],
    )(*fulls)


def _small_pair_sum(s):
    R, C = s.shape
    V = SMALL_VECTOR_ROWS

    def body(s_ref, v_ref, m_ref, sib, send_sem, recv_sem):
        x, y, c = _place()

        def to_sib(src, dst):
            return pltpu.make_async_remote_copy(
                src_ref=src, dst_ref=dst, send_sem=send_sem, recv_sem=recv_sem,
                device_id=(x, y, 1 - c), device_id_type=MESH)

        for q in range(R // 8):
            to_sib(s_ref.at[pl.ds(8 * q, 8), :], sib.at[pl.ds(8 * q, 8), :]).start()
        to_sib(s_ref, sib).wait()
        v_ref[...] = s_ref[pl.ds(0, V), :] + sib[pl.ds(0, V), :]
        m_ref[...] = (s_ref[pl.ds(V, R - V), :] + sib[pl.ds(V, R - V), :]).astype(BF16)

    return pl.pallas_call(
        body,
        name="small_pair_sum",
        in_specs=[pl.BlockSpec(memory_space=pltpu.VMEM)],
        out_specs=[pl.BlockSpec(memory_space=pltpu.VMEM)] * 2,
        out_shape=[jax.ShapeDtypeStruct((V, C), F32), jax.ShapeDtypeStruct((R - V, C), BF16)],
        scratch_shapes=[pltpu.VMEM((R, C), F32), pltpu.SemaphoreType.DMA, pltpu.SemaphoreType.DMA],
    )(s)


def _small_total(chip, own, landed):
    V, C = own[0].shape
    M = own[1].shape[0]

    def body(j_ref, v_ref, m_ref, lv_ref, lm_ref, o_ref, chips_v, chips_m):
        j = j_ref[0]
        chips_v[j] = v_ref[...]
        chips_m[j] = m_ref[...]
        for r in (1, 2, 3):
            chips_v[j ^ r] = lv_ref[r - 1]
            chips_m[j ^ r] = lm_ref[r - 1]
        o_ref[pl.ds(0, V), :] = (chips_v[0] + chips_v[1]) + (chips_v[2] + chips_v[3])
        o_ref[pl.ds(V, M), :] = (chips_m[0].astype(F32) + chips_m[1].astype(F32)) + (
            chips_m[2].astype(F32) + chips_m[3].astype(F32))

    vmem = pl.BlockSpec(memory_space=pltpu.VMEM)
    return pl.pallas_call(
        body,
        name="small_total",
        in_specs=[pl.BlockSpec(memory_space=pltpu.SMEM), vmem, vmem, vmem, vmem],
        out_specs=vmem,
        out_shape=jax.ShapeDtypeStruct((V + M, C), F32),
        scratch_shapes=[pltpu.VMEM((N_CHIPS, V, C), F32), pltpu.VMEM((N_CHIPS, M, C), BF16)],
    )(chip, own[0], own[1], landed[0], landed[1])


def _local_grads(x, target, g_pre, w_in_g, b_gate, conv_w, conv_b, w_rg_a, b_rg_a, w_rg_x, b_rg_x, lam, sinks,
                 out_weights, fwd_token, g_post, on_out_grads, on_w_in_grad):
    b_a = b_rg_a.reshape(1, D_RNN)
    b_x = b_rg_x.reshape(1, D_RNN)

    proj, ht = _proj_fwd(x, g_pre, w_in_g)
    y_rnn, z_rnn, conv, z_rnn_t = _rnn_fwd(proj, conv_w, conv_b, w_rg_a, w_rg_x, b_a, b_x, lam, fwd_token)
    bias = _attn_bias()
    y_attn, z_attn, lse = _attn_fwd(proj, sinks, bias)
    w_rnn_out, w_attn_out, w_out = out_weights(z_attn)
    dyx, dz_rnn, dz_attn, dml, dout, dbr_rnn, dbr_attn, merged_t, z_attn_t, head_small = _head(
        x, target, z_rnn, z_attn, proj, b_gate, g_post, w_rnn_out, w_attn_out, w_out)
    out_grads = [_matmul_t(z_rnn_t, dbr_rnn, "dw_rnn_out"), _matmul_t(z_attn_t, dbr_attn, "dw_attn_out"),
                 _matmul_t(merged_t, dout, "dw_out")]
    shard_rows = lambda d: d.reshape(N_CHIPS, OUT_SHARD, D_MODEL)
    token = on_out_grads([shard_rows(g) for g, _ in out_grads], [shard_rows(gb) for _, gb in out_grads])
    dq, dk, dv, dag, attn_small = _attn_bwd(proj, y_attn, lse, dz_attn, sinks, bias, token)
    drx, drg, dwa, dwx, rnn_small = _rnn_bwd(proj, conv, y_rnn, dz_rnn, conv_w, w_rg_a, w_rg_x, b_a, b_x, lam)
    dproj = [drx, drg, dq, dk, dv, dag, dml]
    token = on_w_in_grad(*_dw_in(ht, dproj))
    grad_x, dh_small = _dh_bwd(dproj, w_in_g, x, dyx, g_pre, token)
    small = jnp.concatenate([rnn_small, head_small, dh_small + attn_small,
                             dwa.reshape(64, 1024), dwx.reshape(64, 1024)], axis=0)
    return grad_x, small


ROW_LOSS = 11


def _unpack_small(s, conv_cols):
    return {
        "b_rg_a": s[0:1].reshape(1, 16, 64), "b_rg_x": s[1:2].reshape(1, 16, 64), "lru_lambda": s[2:3],
        "conv_b": s[3:4], "conv_w": s[4:8, 0:conv_cols].reshape(1, CONV_W, conv_cols),
        "post_norm_g": s[8:9], "b_gate": s[9:11].reshape(1, 2048),
        "pre_norm_g": s[16:17], "attn_sinks": s[17:18, 0:N_Q_HEADS],
        "w_rg_a": s[24:88].reshape(1, 16, 64, 64), "w_rg_x": s[88:152].reshape(1, 16, 64, 64),
    }


WEIGHTS = ["pre_norm_g", "w_in", "b_gate", "conv_w", "conv_b", "w_rg_a", "b_rg_a", "w_rg_x", "b_rg_x", "lru_lambda",
           "attn_sinks", "w_rnn_out", "w_attn_out", "w_out", "post_norm_g"]
BIG = ["w_in", "w_rnn_out", "w_attn_out", "w_out"]


def kernel(x, pre_norm_g, w_in, b_gate, conv_w, conv_b, w_rg_a, b_rg_a, w_rg_x, b_rg_x, lru_lambda, attn_sinks, w_rnn_out, w_attn_out, w_out, post_norm_g, loss_target, m_pre_norm_g, m_w_in, m_b_gate, m_conv_w, m_conv_b, m_w_rg_a, m_b_rg_a, m_w_rg_x, m_b_rg_x, m_lru_lambda, m_attn_sinks, m_w_rnn_out, m_w_attn_out, m_w_out, m_post_norm_g, v_pre_norm_g, v_w_in, v_b_gate, v_conv_w, v_conv_b, v_w_rg_a, v_b_rg_a, v_w_rg_x, v_b_rg_x, v_lru_lambda, v_attn_sinks, v_w_rnn_out, v_w_attn_out, v_w_out, v_post_norm_g):
    w = dict(pre_norm_g=pre_norm_g, w_in=w_in, b_gate=b_gate, conv_w=conv_w, conv_b=conv_b, w_rg_a=w_rg_a,
             b_rg_a=b_rg_a, w_rg_x=w_rg_x, b_rg_x=b_rg_x, lru_lambda=lru_lambda, attn_sinks=attn_sinks,
             w_rnn_out=w_rnn_out, w_attn_out=w_attn_out, w_out=w_out, post_norm_g=post_norm_g)
    m = dict(pre_norm_g=m_pre_norm_g, w_in=m_w_in, b_gate=m_b_gate, conv_w=m_conv_w, conv_b=m_conv_b, w_rg_a=m_w_rg_a,
             b_rg_a=m_b_rg_a, w_rg_x=m_w_rg_x, b_rg_x=m_b_rg_x, lru_lambda=m_lru_lambda, attn_sinks=m_attn_sinks,
             w_rnn_out=m_w_rnn_out, w_attn_out=m_w_attn_out, w_out=m_w_out, post_norm_g=m_post_norm_g)
    v = dict(pre_norm_g=v_pre_norm_g, w_in=v_w_in, b_gate=v_b_gate, conv_w=v_conv_w, conv_b=v_conv_b, w_rg_a=v_w_rg_a,
             b_rg_a=v_b_rg_a, w_rg_x=v_w_rg_x, b_rg_x=v_b_rg_x, lru_lambda=v_lru_lambda, attn_sinks=v_attn_sinks,
             w_rnn_out=v_w_rnn_out, w_attn_out=v_w_attn_out, w_out=v_w_out, post_norm_g=v_post_norm_g)
    chip = 2 * lax.axis_index("x") + lax.axis_index("y")

    chip_idx = chip.astype(jnp.int32).reshape(1)
    chip_core = jnp.stack([chip, lax.axis_index("c")]).astype(jnp.int32)
    cw8 = jnp.pad(conv_w[0], ((0, 8 - CONV_W), (0, 0)))
    placed = _place_shards([w_in[0], w_rnn_out[0], w_attn_out[0], w_out[0]], chip_idx, "place_shards")
    win_g, cw_g = _gather_weights(placed[:1], cw8)
    late_send, late_recv, late_thru, late_token = _gather_late_start(placed[1:], win_g, "gather_late_start")
    cw_g = lax.dynamic_update_slice_in_dim(cw_g, cw8[None], chip, axis=0)
    conv_w_full = jnp.transpose(cw_g[:, 0:CONV_W, :], (1, 0, 2)).reshape(CONV_W, D_RNN)

    started = {}

    def start_reduction(tag, grads, grads_b16):
        psums, psums_b16 = _pair_sum(grads, grads_b16, chip_core, "pair_sum_" + tag)
        send_sems, recv_sems, p_thru, land_thru, token = _chip_exchange_start(psums_b16, "chip_exchange_start_" + tag)
        started[tag] = (psums, send_sems, recv_sems, p_thru, land_thru)
        return token

    def end_reduction(tag, after):
        psums, send_sems, recv_sems, p_thru, land_thru = started[tag]
        _, landed = _chip_exchange_wait(send_sems, recv_sems, p_thru, land_thru, after, "chip_exchange_wait_" + tag)
        return psums, landed

    def out_weights(after):
        gathered = _gather_late_wait(late_send, late_recv, late_thru, after, "gather_late_wait")
        return [g.reshape(D_MODEL, D_MODEL) for g in gathered]

    grad_x, small = _local_grads(
        x[0], loss_target[0], pre_norm_g, win_g, b_gate, conv_w_full, conv_b, w_rg_a[0], b_rg_a[0], w_rg_x[0],
        b_rg_x[0], lru_lambda, attn_sinks[0], out_weights, late_token, post_norm_g,
        on_out_grads=lambda grads, grads_b16: start_reduction("out", grads, grads_b16),
        on_w_in_grad=lambda grad, grad_b16: start_reduction("in", [grad], [grad_b16]))

    small_chip = _small_pair_sum(small)
    small_send, small_recv, small_thru, small_land, small_token = _chip_exchange_start(
        list(small_chip), "small_exchange_start", blocked=False)

    halves = _chip_sum([end_reduction("in", small_token), end_reduction("out", small_token)], chip_core, "chip_sum")
    gbig = dict(zip(BIG, _pair_share(halves)))

    grads, delta, new_m, new_v = {}, {}, {}, {}
    updates = _adamw([[(w[n][0], gbig[n], m[n][0], v[n][0]) for n in names] for names in (BIG[:1], BIG[1:])],
                     "adamw_big")
    for n, (d, nm, nv, g) in zip(BIG, updates):
        grads[n], delta[n], new_m[n], new_v[n] = g[None], d[None], nm[None], nv[None]

    small_own, small_landed = _chip_exchange_wait(small_send, small_recv, small_thru, small_land, delta[BIG[-1]],
                                                  "small_exchange_wait", blocked=False)
    small_sum = _small_total(chip_idx, small_own, small_landed)
    total_loss = small_sum[ROW_LOSS, 0]
    gsmall = _unpack_small(small_sum, D_RNN)
    conv_shard = D_RNN // N_CHIPS
    gsmall["conv_w"] = lax.dynamic_slice_in_dim(gsmall["conv_w"], chip * conv_shard, conv_shard, axis=2)
    for n in gsmall:
        grads[n] = gsmall[n].reshape(w[n].shape)
    updates = _adamw_whole([(w[n], grads[n], m[n], v[n]) for n in gsmall], "adamw_small")
    for n, (d, nm, nv) in zip(gsmall, updates):
        delta[n], new_m[n], new_v[n] = d, nm, nv

    return (total_loss, grad_x[None], *[grads[n] for n in WEIGHTS], *[delta[n] for n in WEIGHTS],
            *[new_m[n] for n in WEIGHTS], *[new_v[n] for n in WEIGHTS])
```

```python
import functools
import math

import jax
import jax.numpy as jnp
import numpy as np
from jax import lax
from jax.experimental import pallas as pl
from jax.experimental.pallas import tpu as pltpu

F32 = jnp.float32
BF16 = jnp.bfloat16

D_MODEL = 1024
D_RNN = 1024
RNN_BLOCKS = 16
RNN_BLOCK_W = 64
CONV_W = 4
LRU_C = 8.0
N_Q_HEADS = 16
N_KV_HEADS = 4
GROUP = 4
HEAD_DIM = 64
D_KV = 256
BLOCK = 128
ALIBI_MAX_BIAS = 8.0
EPS = 1e-6
D_IN = 6656
N_CHIPS = 4
W_IN_SHARD = D_IN // N_CHIPS
OUT_SHARD = D_MODEL // N_CHIPS
ADAM_LR = 0.001
ADAM_B1 = 0.9
ADAM_B2 = 0.999
ADAM_EPS = 1e-08
ADAM_WD = 0.01
ADAM_STEP = 10
NEG_BIG = -1e30
MIB = 1 << 20

COL_RNN_X = 0
COL_RNN_GATE = 4
COL_Q = 8
COL_K = 12
COL_V = 13
COL_ATTN_GATE = 14
COL_MERGE = 18

RNN_TILE = 256
RNN_CHUNK = 512
SMALL_ROWS = 152
SMALL_VECTOR_ROWS = 24
MESH = pl.DeviceIdType.MESH


def _sds(shape, dtype):
    return pltpu.HBM(shape, dtype)


def _params(sem=None, vmem_mib=None):
    kw = {}
    if sem is not None:
        kw["dimension_semantics"] = sem
    if vmem_mib is not None:
        kw["vmem_limit_bytes"] = vmem_mib * MIB
    return pltpu.CompilerParams(**kw)


def _hbm(*arrays):
    return [pltpu.with_memory_space_constraint(a, pltpu.HBM) for a in arrays]


def _dot(a, b):
    return jnp.dot(a, b, preferred_element_type=F32)


def _dot_nt(a, b):
    return lax.dot_general(a, b, (((1,), (1,)), ((), ())), preferred_element_type=F32)


def _dot_tn(a, b):
    return lax.dot_general(a, b, (((0,), (0,)), ((), ())), preferred_element_type=F32)


def _sigmoid(x):
    return 0.5 * jnp.tanh(0.5 * x) + 0.5


def _sigmoid_small(x):
    return 1.0 / (1.0 + jnp.exp(-x))


def _softplus(x):
    return jnp.maximum(x, 0.0) + jnp.log(1.0 + jnp.exp(-jnp.abs(x)))


def _one_minus_square(a, log_a):
    return -jnp.tanh(log_a) * (a * a + 1.0)


def _proj_fwd(x, g_pre, w_in_g):
    T = x.shape[0]
    tm = min(1024, T)

    def body(x_ref, g_ref, w_ref, proj_ref, ht_ref, h_s):
        @pl.when(pl.program_id(1) == 0)
        def _():
            xv = x_ref[...]
            rstd = lax.rsqrt(jnp.mean(xv * xv, axis=-1, keepdims=True) + EPS)
            hf = (xv * rstd) * g_ref[...]
            h_s[...] = hf.astype(BF16)
            ht_ref[...] = hf.T.astype(BF16)

        proj_ref[...] = _dot(h_s[...], w_ref[...]).astype(BF16)

    return pl.pallas_call(
        body,
        name="proj_fwd",
        grid=(T // tm, N_CHIPS),
        in_specs=[
            pl.BlockSpec((tm, D_MODEL), lambda i, j: (i, 0)),
            pl.BlockSpec((1, D_MODEL), lambda i, j: (0, 0)),
            pl.BlockSpec((None, D_MODEL, W_IN_SHARD), lambda i, j: (j, 0, 0)),
        ],
        out_specs=[
            pl.BlockSpec((tm, W_IN_SHARD), lambda i, j: (i, j)),
            pl.BlockSpec((D_MODEL, tm), lambda i, j: (0, i)),
        ],
        out_shape=[_sds((T, D_IN), BF16), _sds((D_MODEL, T), BF16)],
        scratch_shapes=[pltpu.VMEM((tm, D_MODEL), BF16)],
        compiler_params=_params(("parallel", "arbitrary"), 48),
    )(*_hbm(x, g_pre, w_in_g))


def _shift_down(x, tail, s, row):
    n = x.shape[0]
    xs = pltpu.roll(x, s, 0)
    tail_t = jnp.tile(pltpu.roll(tail, s, 0), (n // 8, 1))
    return jnp.where(row < s, tail_t, xs)


def _shift_up(x, head, s, row):
    n = x.shape[0]
    xs = pltpu.roll(x, n - s, 0)
    head_t = jnp.tile(pltpu.roll(head, 8 - s, 0), (n // 8, 1))
    return jnp.where(row >= n - s, head_t, xs)


def _conv_taps(x, tail, row):
    return [_shift_down(x, tail, 3, row), _shift_down(x, tail, 2, row), _shift_down(x, tail, 1, row), x]


def _rglru_gates(c, wa, wx, ba, bx, lam):
    cb = c.astype(BF16)
    r = _sigmoid_small(_dot(cb, wa) + ba)
    i = _sigmoid(_dot(cb, wx) + bx)
    log_a = (-LRU_C) * r * _softplus(-lam)
    a = jnp.exp(log_a)
    w = _one_minus_square(a, log_a)
    inv_mult = lax.rsqrt(w)
    return cb, r, i, a, w * inv_mult, inv_mult


GATE_BLOCKS_PER_TILE = RNN_TILE // RNN_BLOCK_W
GATE_BLOCKS = pl.BlockSpec((GATE_BLOCKS_PER_TILE, RNN_BLOCK_W, RNN_BLOCK_W), lambda j, t: (j, 0, 0))


def _fill_block_diag(bd_ref, w_ref):
    bd_ref[...] = jnp.zeros_like(bd_ref)
    for a in range(GATE_BLOCKS_PER_TILE):
        lo = a * RNN_BLOCK_W
        bd_ref[lo:lo + RNN_BLOCK_W, lo:lo + RNN_BLOCK_W] = w_ref[a].astype(BF16)


SUBLANES = 8


def _scan_down(a, u, row):
    n = a.shape[0]
    s = 1
    while s < SUBLANES:
        a_sh = jnp.where(row >= s, pltpu.roll(a, s, 0), 1.0)
        u_sh = jnp.where(row >= s, pltpu.roll(u, s, 0), 0.0)
        u = a * u_sh + u
        a = a * a_sh
        s *= 2
    while s < n:
        u = jnp.concatenate([u[:s], a[s:] * u[:n - s] + u[s:]], axis=0)
        a = jnp.concatenate([a[:s], a[s:] * a[:n - s]], axis=0)
        s *= 2
    return a, u


def _scan_up(b, u, row):
    n = b.shape[0]
    s = 1
    while s < SUBLANES:
        b_sh = jnp.where(row < n - s, pltpu.roll(b, n - s, 0), 1.0)
        u_sh = jnp.where(row < n - s, pltpu.roll(u, n - s, 0), 0.0)
        u = b * u_sh + u
        b = b * b_sh
        s *= 2
    while s < n:
        u = jnp.concatenate([b[:n - s] * u[s:] + u[:n - s], u[n - s:]], axis=0)
        b = jnp.concatenate([b[:n - s] * b[s:], b[n - s:]], axis=0)
        s *= 2
    return b, u


LANES = 128


def _chunk_scan(a, u, a_s, u_s, hl_s, al_s, carry, reverse):
    n, width = a.shape
    groups = n // SUBLANES
    order = range(SUBLANES - 1, -1, -1) if reverse else range(SUBLANES)
    row = lax.broadcasted_iota(jnp.int32, (groups, LANES), 0)
    for l in range(width // LANES):
        lanes = slice(l * LANES, (l + 1) * LANES)
        a_l, u_l, hl_l, al_l = a_s.at[l], u_s.at[l], hl_s.at[l], al_s.at[l]
        a_l[...] = a[:, lanes]
        u_l[...] = u[:, lanes]
        h_loc = a_loc = None
        for r in order:
            rows = pl.ds(r, groups, stride=SUBLANES)
            a_r, u_r = a_l[rows, :], u_l[rows, :]
            h_loc, a_loc = (u_r, a_r) if h_loc is None else (a_r * h_loc + u_r, a_r * a_loc)
            hl_l[rows, :] = h_loc
            al_l[rows, :] = a_loc
        if reverse:
            a_cum, ends = _scan_up(a_loc, h_loc, row)
            ends = ends + a_cum * carry[:, lanes]
            enters = jnp.where(row == groups - 1, carry[:, lanes], pltpu.roll(ends, groups - 1, 0))
        else:
            a_cum, ends = _scan_down(a_loc, h_loc, row)
            ends = ends + a_cum * carry[:, lanes]
            enters = jnp.where(row == 0, carry[:, lanes], pltpu.roll(ends, 1, 0))
        for r in range(SUBLANES):
            rows = pl.ds(r, groups, stride=SUBLANES)
            hl_l[rows, :] = hl_l[rows, :] + al_l[rows, :] * enters
    return jnp.concatenate([hl_s[l] for l in range(width // LANES)], axis=1)


def _rnn_fwd(proj, conv_w, conv_b, w_a, w_x, b_a, b_x, lam, token):
    T = proj.shape[0]
    tc, ct = RNN_CHUNK, RNN_TILE
    nt = T // tc

    def body(x_ref, rg_ref, cw_ref, cb_ref, wa_ref, wx_ref, ba_ref, bx_ref, lam_ref, token_ref, h_ref, z_ref, c_ref,
             zt_ref, xtail, hcarry, wa_s, wx_s, a_s, u_s, hl_s, al_s):
        @pl.when(pl.program_id(1) == 0)
        def _():
            xtail[...] = jnp.zeros_like(xtail)
            hcarry[...] = jnp.zeros_like(hcarry)
            _fill_block_diag(wa_s, wa_ref)
            _fill_block_diag(wx_s, wx_ref)

        row = lax.broadcasted_iota(jnp.int32, (tc, ct), 0)
        x = x_ref[...].astype(F32)
        taps = _conv_taps(x, xtail[...], row)
        c = cb_ref[...] + cw_ref[pl.ds(0, 1), :] * taps[0]
        for k in range(1, CONV_W):
            c = c + cw_ref[pl.ds(k, 1), :] * taps[k]
        xtail[...] = x[tc - 8:, :]
        c_ref[...] = c
        _, _, i, a, mult, _ = _rglru_gates(c, wa_s[...], wx_s[...], ba_ref[...], bx_ref[...], lam_ref[...])
        h = _chunk_scan(a, mult * (i * c), a_s, u_s, hl_s, al_s, hcarry[...], reverse=False)
        h_ref[...] = h
        hcarry[...] = h_ref[pl.ds(tc - 1, 1), :]
        rg = rg_ref[...].astype(F32)
        z = h * (rg * _sigmoid(rg))
        z_ref[...] = z.astype(BF16)
        zt_ref[...] = z.T.astype(BF16)

    col = lambda off: (lambda j, t: (t, off + j))
    vec = pl.BlockSpec((1, ct), lambda j, t: (0, j))
    return pl.pallas_call(
        body,
        name="rnn_fwd",
        grid=(D_RNN // ct, nt),
        in_specs=[
            pl.BlockSpec((tc, ct), col(COL_RNN_X)),
            pl.BlockSpec((tc, ct), col(COL_RNN_GATE)),
            pl.BlockSpec((CONV_W, ct), lambda j, t: (0, j)),
            vec, GATE_BLOCKS, GATE_BLOCKS, vec, vec, vec,
            pl.BlockSpec((8, 128), lambda j, t: (0, 0)),
        ],
        out_specs=[pl.BlockSpec((tc, ct), lambda j, t: (t, j))] * 3 + [pl.BlockSpec((ct, tc), lambda j, t: (j, t))],
        out_shape=[_sds((T, D_RNN), F32), _sds((T, D_RNN), BF16), _sds((T, D_RNN), F32), _sds((D_RNN, T), BF16)],
        scratch_shapes=[pltpu.VMEM((8, ct), F32), pltpu.VMEM((1, ct), F32)] + [pltpu.VMEM((ct, ct), BF16)] * 2 + [
            pltpu.VMEM((ct // LANES, tc, LANES), F32)] * 4,
        compiler_params=_params(("parallel", "arbitrary"), 32),
    )(*_hbm(proj, proj, conv_w, conv_b, w_a, w_x, b_a, b_x, lam, token))


def _rnn_bwd(proj, conv, y_rnn, dz_rnn, conv_w, w_a, w_x, b_a, b_x, lam):
    T = proj.shape[0]
    tc, ct = RNN_CHUNK, RNN_TILE
    nt = T // tc
    hb = tc // 8

    def body(x_ref, c_ref, rg_ref, h_ref, hh_ref, dz_ref, cw_ref, wa_ref, wx_ref, ba_ref, bx_ref, lam_ref,
             dx_ref, drg_ref, dwa_ref, dwx_ref, sm_ref, lam_carry, a_carry, dc_head, wa_s, wx_s, dwa_s, dwx_s,
             b_s, dy_s, hl_s, al_s):
        t = pl.program_id(1)
        first_chunk = t == nt - 1

        @pl.when(t == 0)
        def _():
            lam_carry[...] = jnp.zeros_like(lam_carry)
            a_carry[...] = jnp.zeros_like(a_carry)
            dc_head[...] = jnp.zeros_like(dc_head)
            dwa_s[...] = jnp.zeros_like(dwa_s)
            dwx_s[...] = jnp.zeros_like(dwx_s)
            sm_ref[...] = jnp.zeros_like(sm_ref)
            _fill_block_diag(wa_s, wa_ref)
            _fill_block_diag(wx_s, wx_ref)

        row = lax.broadcasted_iota(jnp.int32, (tc, ct), 0)
        keep = jnp.where(first_chunk, 0.0, 1.0)
        x = x_ref[...].astype(F32)
        c = c_ref[...]
        lam = lam_ref[...]
        cb, r, i, a, mult, inv_mult = _rglru_gates(c, wa_s[...], wx_s[...], ba_ref[...], bx_ref[...], lam)
        h = h_ref[...]
        h_prev = _shift_down(h, hh_ref[...] * keep, 1, row)
        rg = rg_ref[...].astype(F32)
        dz = dz_ref[...]
        sg = _sigmoid(rg)
        drg_ref[...] = (dz * h * (sg * (1.0 + rg * (1.0 - sg)))).astype(BF16)
        dy = dz * (rg * sg)
        b = jnp.where(row >= tc - 1, a_carry[pl.ds(0, 1), :], pltpu.roll(a, tc - 1, 0))
        lt = _chunk_scan(b, dy, b_s, dy_s, hl_s, al_s, lam_carry[pl.ds(0, 1), :], reverse=True)
        lam_carry[...] = lt[0:8, :]
        a_carry[...] = a[0:8, :]
        ic = i * c
        dmult = lt * ic
        di = lt * mult * c
        dc = lt * mult * i
        dlog_a = a * (lt * h_prev - dmult * a * inv_mult)
        sp = _softplus(-lam)
        dpre_r = dlog_a * ((-LRU_C) * sp) * (r * (1.0 - r))
        dpre_i = di * (i * (1.0 - i))
        dlam_row = jnp.sum(dlog_a * r, axis=0, keepdims=True) * (LRU_C * _sigmoid(-lam))
        dpr_b = dpre_r.astype(BF16)
        dpi_b = dpre_i.astype(BF16)
        dwa_s[...] += _dot_tn(cb, dpr_b)
        dwx_s[...] += _dot_tn(cb, dpi_b)
        dc = dc + _dot_nt(dpr_b, wa_s[...]) + _dot_nt(dpi_b, wx_s[...])
        head = dc_head[...]
        dx = cw_ref[pl.ds(3, 1), :] * dc
        sm_ref[pl.ds(4 + 3, 1), :] += jnp.sum(dc * x, axis=0, keepdims=True)
        for m in range(1, CONV_W):
            up = _shift_up(dc, head, m, row)
            dx = dx + cw_ref[pl.ds(3 - m, 1), :] * up
            sm_ref[pl.ds(4 + 3 - m, 1), :] += jnp.sum(up * x, axis=0, keepdims=True)
        dx_ref[...] = dx.astype(BF16)
        dc_head[...] = dc[0:8, :]
        sm_ref[pl.ds(0, 1), :] += jnp.sum(dpre_r, axis=0, keepdims=True)
        sm_ref[pl.ds(1, 1), :] += jnp.sum(dpre_i, axis=0, keepdims=True)
        sm_ref[pl.ds(2, 1), :] += dlam_row
        sm_ref[pl.ds(3, 1), :] += jnp.sum(dc, axis=0, keepdims=True)

        @pl.when(first_chunk)
        def _():
            for k in range(GATE_BLOCKS_PER_TILE):
                lo = k * RNN_BLOCK_W
                dwa_ref[k] = dwa_s[lo:lo + RNN_BLOCK_W, lo:lo + RNN_BLOCK_W]
                dwx_ref[k] = dwx_s[lo:lo + RNN_BLOCK_W, lo:lo + RNN_BLOCK_W]

    rev = lambda off: (lambda j, t: (nt - 1 - t, off + j))
    halo = lambda off: (lambda j, t: (jnp.maximum((nt - 1 - t) * hb - 1, 0), off + j))
    vec = pl.BlockSpec((1, ct), lambda j, t: (0, j))
    mat = GATE_BLOCKS
    return pl.pallas_call(
        body,
        name="rnn_bwd",
        grid=(D_RNN // ct, nt),
        in_specs=[
            pl.BlockSpec((tc, ct), rev(COL_RNN_X)),
            pl.BlockSpec((tc, ct), rev(0)),
            pl.BlockSpec((tc, ct), rev(COL_RNN_GATE)),
            pl.BlockSpec((tc, ct), rev(0)),
            pl.BlockSpec((8, ct), halo(0)),
            pl.BlockSpec((tc, ct), rev(0)),
            pl.BlockSpec((CONV_W, ct), lambda j, t: (0, j)),
            mat, mat, vec, vec, vec,
        ],
        out_specs=[
            pl.BlockSpec((tc, ct), rev(0)),
            pl.BlockSpec((tc, ct), rev(0)),
            mat, mat,
            pl.BlockSpec((8, ct), lambda j, t: (0, j)),
        ],
        out_shape=[_sds((T, D_RNN), BF16), _sds((T, D_RNN), BF16), _sds(w_a.shape, F32), _sds(w_x.shape, F32),
                   _sds((8, D_RNN), F32)],
        scratch_shapes=[pltpu.VMEM((8, ct), F32)] * 3 + [pltpu.VMEM((ct, ct), BF16)] * 2 + [
            pltpu.VMEM((ct, ct), F32)] * 2 + [pltpu.VMEM((ct // LANES, tc, LANES), F32)] * 4,
        compiler_params=_params(("parallel", "arbitrary"), 32),
    )(*_hbm(proj, conv, proj, y_rnn, y_rnn, dz_rnn, conv_w, w_a, w_x, b_a, b_x, lam))


def _attn_bias():
    qi = np.arange(BLOCK)[:, None]
    kj = np.arange(BLOCK)[None, :]
    dist_cur = (qi - kj).astype(np.float32)
    slopes = np.float32(2.0) ** (-ALIBI_MAX_BIAS * np.arange(1, N_Q_HEADS + 1, dtype=np.float32) / N_Q_HEADS)
    slopes = slopes[:, None, None]
    prev = np.where(kj > qi, -slopes * (dist_cur + np.float32(BLOCK)), np.float32(NEG_BIG))
    cur = np.where(kj <= qi, -slopes * dist_cur, np.float32(NEG_BIG))
    later = np.concatenate([prev, cur], axis=-1)
    first = np.concatenate([np.full_like(prev, NEG_BIG), cur], axis=-1)
    return jnp.asarray(np.stack([first, later]).astype(np.float32))


def _attn_exps(s_prev, s_cur, sink, bias):
    s_prev = s_prev + bias[:, 0:BLOCK]
    s_cur = s_cur + bias[:, BLOCK:2 * BLOCK]
    m = jnp.maximum(jnp.max(jnp.maximum(s_prev, s_cur), axis=-1, keepdims=True), sink)
    p_prev = jnp.exp(s_prev - m)
    p_cur = jnp.exp(s_cur - m)
    total = jnp.sum(p_prev + p_cur, axis=-1, keepdims=True) + jnp.exp(sink - m)
    return p_prev, p_cur, 1.0 / total, m + jnp.log(total)


def _attn_probs(s_prev, s_cur, sink, bias, lse):
    p_prev = jnp.exp((s_prev + bias[:, 0:BLOCK]) - lse)
    p_cur = jnp.exp((s_cur + bias[:, BLOCK:2 * BLOCK]) - lse)
    return p_prev, p_cur, jnp.exp(sink - lse)


def _stack_heads(ref_or_val, hk, dtype):
    parts = [ref_or_val[:, (GROUP * hk + g) * HEAD_DIM:(GROUP * hk + g + 1) * HEAD_DIM] for g in range(GROUP)]
    return jnp.concatenate(parts, axis=0).astype(dtype)


ATTN_SCALE = HEAD_DIM ** -0.5


def _bias_spec():
    return pl.BlockSpec((None, N_Q_HEADS, BLOCK, 2 * BLOCK), lambda i: (jnp.minimum(i, 1), 0, 0, 0))


def _attn_fwd(proj, sinks, bias):
    T = proj.shape[0]
    nb = T // BLOCK

    def body(sink_ref, bias_ref, q_ref, kp_ref, kc_ref, vp_ref, vc_ref, ag0_ref, ag1_ref, y_ref, z_ref, lse_ref):
        kvs = [slice(hk * HEAD_DIM, (hk + 1) * HEAD_DIM) for hk in range(N_KV_HEADS)]
        qgs = [(_stack_heads(q_ref, hk, F32) * ATTN_SCALE).astype(BF16) for hk in range(N_KV_HEADS)]
        s_prev = [_dot_nt(qgs[hk], kp_ref[:, kvs[hk]].astype(BF16)) for hk in range(N_KV_HEADS)]
        s_cur = [_dot_nt(qgs[hk], kc_ref[:, kvs[hk]].astype(BF16)) for hk in range(N_KV_HEADS)]
        for hk in range(N_KV_HEADS):
            pp, pc, invs = [], [], []
            for g in range(GROUP):
                h = GROUP * hk + g
                rows = slice(g * BLOCK, (g + 1) * BLOCK)
                p_prev, p_cur, inv, lse = _attn_exps(s_prev[hk][rows], s_cur[hk][rows], sink_ref[h], bias_ref[h])
                pp.append(p_prev.astype(BF16))
                pc.append(p_cur.astype(BF16))
                invs.append(inv)
                lse_ref[:, h:h + 1] = lse
            og = _dot(jnp.concatenate(pp, axis=0), vp_ref[:, kvs[hk]].astype(BF16)) + _dot(
                jnp.concatenate(pc, axis=0), vc_ref[:, kvs[hk]].astype(BF16))
            for g in range(GROUP):
                h = GROUP * hk + g
                y_ref[:, h * HEAD_DIM:(h + 1) * HEAD_DIM] = og[g * BLOCK:(g + 1) * BLOCK] * invs[g]
        ag = jnp.concatenate([ag0_ref[...], ag1_ref[...]], axis=1).astype(F32)
        z_ref[...] = (y_ref[...] * (ag * _sigmoid(ag))).astype(BF16)

    prev = lambda c: (lambda i: (jnp.maximum(i - 1, 0), c))
    cur = lambda c: (lambda i: (i, c))
    return pl.pallas_call(
        body,
        name="attn_fwd",
        grid=(nb,),
        in_specs=[
            pl.BlockSpec(memory_space=pltpu.SMEM),
            _bias_spec(),
            pl.BlockSpec((BLOCK, 1024), lambda i: (i, COL_Q // 4)),
            pl.BlockSpec((BLOCK, D_KV), prev(COL_K)),
            pl.BlockSpec((BLOCK, D_KV), cur(COL_K)),
            pl.BlockSpec((BLOCK, D_KV), prev(COL_V)),
            pl.BlockSpec((BLOCK, D_KV), cur(COL_V)),
            pl.BlockSpec((BLOCK, 512), lambda i: (i, COL_ATTN_GATE // 2)),
            pl.BlockSpec((BLOCK, 512), lambda i: (i, COL_ATTN_GATE // 2 + 1)),
        ],
        out_specs=[pl.BlockSpec((BLOCK, 1024), lambda i: (i, 0)), pl.BlockSpec((BLOCK, 1024), lambda i: (i, 0)),
                   pl.BlockSpec((BLOCK, N_Q_HEADS), lambda i: (i, 0))],
        out_shape=[_sds((T, 1024), F32), _sds((T, 1024), BF16), _sds((T, N_Q_HEADS), F32)],
        compiler_params=_params(("arbitrary",), 32),
    )(sinks, *_hbm(bias, proj, proj, proj, proj, proj, proj, proj))


def _attn_bwd(proj, y_attn, lse, dz_attn, sinks, bias, token):
    T = proj.shape[0]
    nb = T // BLOCK

    def body(sink_ref, bias_ref, q_ref, kp_ref, kc_ref, vp_ref, vc_ref, ag0_ref, ag1_ref, y_ref, lse_ref, dz_ref,
             token_ref, dq_ref, dk_ref, dv_ref, dag_ref, ds_ref, dy_s):
        i = pl.program_id(0)

        @pl.when(i == 0)
        def _():
            ds_ref[...] = jnp.zeros_like(ds_ref)

        lane = lax.broadcasted_iota(jnp.int32, (8, 128), 1)
        sub = lax.broadcasted_iota(jnp.int32, (8, 128), 0)
        ag = jnp.concatenate([ag0_ref[...], ag1_ref[...]], axis=1).astype(F32)
        dz = dz_ref[...]
        sg = _sigmoid(ag)
        dag_ref[...] = (dz * y_ref[...] * (sg * (1.0 + ag * (1.0 - sg)))).astype(BF16)
        dy_s[...] = dz * (ag * sg)
        r_cur = pl.multiple_of(i * BLOCK, BLOCK)
        r_prev = pl.multiple_of(jnp.maximum(i - 1, 0) * BLOCK, BLOCK)
        dk_cur, dv_cur, dk_prev, dv_prev = [], [], [], []
        ds_acc = jnp.zeros((8, 128), F32)
        for hk in range(N_KV_HEADS):
            ks = slice(hk * HEAD_DIM, (hk + 1) * HEAD_DIM)
            qg = (_stack_heads(q_ref, hk, F32) * ATTN_SCALE).astype(BF16)
            dog = _stack_heads(dy_s, hk, F32)
            og = _stack_heads(y_ref, hk, F32)
            dog_b = dog.astype(BF16)
            kp = kp_ref[:, ks].astype(BF16)
            kc = kc_ref[:, ks].astype(BF16)
            vp = vp_ref[:, ks].astype(BF16)
            vc = vc_ref[:, ks].astype(BF16)
            s_prev = _dot_nt(qg, kp)
            s_cur = _dot_nt(qg, kc)
            dp_prev = _dot_nt(dog_b, vp)
            dp_cur = _dot_nt(dog_b, vc)
            dvec = jnp.sum(dog * og, axis=-1, keepdims=True)
            pp, pc, dsp, dsc = [], [], [], []
            for g in range(GROUP):
                h = GROUP * hk + g
                rows = slice(g * BLOCK, (g + 1) * BLOCK)
                p_prev, p_cur, p_sink = _attn_probs(
                    s_prev[rows], s_cur[rows], sink_ref[h], bias_ref[h], lse_ref[:, h:h + 1])
                d_h = dvec[rows]
                pp.append(p_prev.astype(BF16))
                pc.append(p_cur.astype(BF16))
                dsp.append((p_prev * (dp_prev[rows] - d_h)).astype(BF16))
                dsc.append((p_cur * (dp_cur[rows] - d_h)).astype(BF16))
                dsink = -jnp.sum(p_sink * d_h, axis=0, keepdims=True)
                ds_acc = ds_acc + jnp.where(jnp.logical_and(lane == h, sub == 1), dsink, 0.0)
            pp = jnp.concatenate(pp, axis=0)
            pc = jnp.concatenate(pc, axis=0)
            dsp = jnp.concatenate(dsp, axis=0)
            dsc = jnp.concatenate(dsc, axis=0)
            dqg = (_dot(dsp, kp) + _dot(dsc, kc)) * ATTN_SCALE
            for g in range(GROUP):
                h = GROUP * hk + g
                dq_ref[:, h * HEAD_DIM:(h + 1) * HEAD_DIM] = dqg[g * BLOCK:(g + 1) * BLOCK].astype(BF16)
            dk_ref[pl.ds(r_cur, BLOCK), ks] = _dot_tn(dsc, qg)
            dv_ref[pl.ds(r_cur, BLOCK), ks] = _dot_tn(pc, dog_b)
            dk_prev.append(_dot_tn(dsp, qg))
            dv_prev.append(_dot_tn(pp, dog_b))
        ds_ref[:, 0:128] += ds_acc

        @pl.when(i > 0)
        def _():
            for hk in range(N_KV_HEADS):
                ks = slice(hk * HEAD_DIM, (hk + 1) * HEAD_DIM)
                dk_ref[pl.ds(r_prev, BLOCK), ks] += dk_prev[hk]
                dv_ref[pl.ds(r_prev, BLOCK), ks] += dv_prev[hk]

    prev = lambda c: (lambda i: (jnp.maximum(i - 1, 0), c))
    cur = lambda c: (lambda i: (i, c))
    blk = pl.BlockSpec((BLOCK, 1024), lambda i: (i, 0))
    whole = pl.BlockSpec((T, D_KV), lambda i: (0, 0))
    return pl.pallas_call(
        body,
        name="attn_bwd",
        grid=(nb,),
        in_specs=[
            pl.BlockSpec(memory_space=pltpu.SMEM),
            _bias_spec(),
            pl.BlockSpec((BLOCK, 1024), lambda i: (i, COL_Q // 4)),
            pl.BlockSpec((BLOCK, D_KV), prev(COL_K)),
            pl.BlockSpec((BLOCK, D_KV), cur(COL_K)),
            pl.BlockSpec((BLOCK, D_KV), prev(COL_V)),
            pl.BlockSpec((BLOCK, D_KV), cur(COL_V)),
            pl.BlockSpec((BLOCK, 512), lambda i: (i, COL_ATTN_GATE // 2)),
            pl.BlockSpec((BLOCK, 512), lambda i: (i, COL_ATTN_GATE // 2 + 1)),
            blk,
            pl.BlockSpec((BLOCK, N_Q_HEADS), lambda i: (i, 0)),
            blk,
            pl.BlockSpec((8, 128), lambda i: (0, 0)),
        ],
        out_specs=[blk, whole, whole, blk, pl.BlockSpec((8, 1024), lambda i: (0, 0))],
        out_shape=[_sds((T, 1024), BF16), _sds((T, D_KV), F32), _sds((T, D_KV), F32), _sds((T, 1024), BF16),
                   _sds((8, 1024), F32)],
        scratch_shapes=[pltpu.VMEM((BLOCK, 1024), F32)],
        compiler_params=_params(("arbitrary",), 48),
    )(sinks, *_hbm(bias, proj, proj, proj, proj, proj, proj, proj, y_attn, lse, dz_attn, token))


def _head(x, target, z_rnn, z_attn, proj, b_gate, g_post, w_rnn_out, w_attn_out, w_out):
    T = x.shape[0]
    tm = 256

    def body(x_ref, t_ref, zr_ref, za_ref, ml0_ref, ml1_ref, ml2_ref, ml3_ref, bg_ref, gp_ref, wr_ref, wa_ref, wo_ref,
             dyx_ref, dzr_ref, dza_ref, dml_ref, dout_ref, dbr_ref, dba_ref, mt_ref, zat_ref, sm_ref):
        @pl.when(pl.program_id(0) == 0)
        def _():
            sm_ref[...] = jnp.zeros_like(sm_ref)

        wr, wa, wo = wr_ref[...], wa_ref[...], wo_ref[...]
        br_rnn = _dot(zr_ref[...], wr)
        br_attn = _dot(za_ref[...], wa)
        zat_ref[...] = za_ref[...].astype(F32).T.astype(BF16)
        ml_rnn = jnp.concatenate([ml0_ref[...], ml1_ref[...]], axis=1).astype(F32)
        ml_attn = jnp.concatenate([ml2_ref[...], ml3_ref[...]], axis=1).astype(F32)
        g_rnn = _sigmoid(ml_rnn + bg_ref[:, 0:D_MODEL])
        g_attn = _sigmoid(ml_attn + bg_ref[:, D_MODEL:2 * D_MODEL])
        merged = g_rnn * br_rnn + g_attn * br_attn
        mb = merged.astype(BF16)
        mt_ref[...] = merged.T.astype(BF16)
        out = _dot(mb, wo)
        rstd = lax.rsqrt(jnp.mean(out * out, axis=-1, keepdims=True) + EPS)
        n = out * rstd
        gp = gp_ref[...]
        err = (x_ref[...] + n * gp) - t_ref[...]
        sm_ref[pl.ds(3, 1), :] += 0.5 * jnp.sum(jnp.mean(err * err, axis=-1, keepdims=True), axis=0, keepdims=True)
        dy = err * (1.0 / D_MODEL)
        dyx_ref[...] = dy
        sm_ref[pl.ds(0, 1), :] += jnp.sum(dy * n, axis=0, keepdims=True)
        dn = dy * gp
        dout = (rstd * (dn - n * jnp.mean(dn * n, axis=-1, keepdims=True))).astype(BF16)
        dout_ref[...] = dout
        dmerged = _dot_nt(dout, wo)
        dml_r = (dmerged * br_rnn) * (g_rnn * (1.0 - g_rnn))
        dml_a = (dmerged * br_attn) * (g_attn * (1.0 - g_attn))
        dml_ref[:, 0:D_MODEL] = dml_r.astype(BF16)
        dml_ref[:, D_MODEL:2 * D_MODEL] = dml_a.astype(BF16)
        sm_ref[pl.ds(1, 1), :] += jnp.sum(dml_r, axis=0, keepdims=True)
        sm_ref[pl.ds(2, 1), :] += jnp.sum(dml_a, axis=0, keepdims=True)
        dbr = (dmerged * g_rnn).astype(BF16)
        dba = (dmerged * g_attn).astype(BF16)
        dbr_ref[...] = dbr
        dba_ref[...] = dba
        dzr_ref[...] = _dot_nt(dbr, wr)
        dza_ref[...] = _dot_nt(dba, wa)

    tile = pl.BlockSpec((tm, D_MODEL), lambda i: (i, 0))
    wspec = pl.BlockSpec((D_MODEL, D_MODEL), lambda i: (0, 0))
    ml = lambda q: pl.BlockSpec((tm, 512), lambda i: (i, COL_MERGE // 2 + q))
    return pl.pallas_call(
        body,
        name="head",
        grid=(T // tm,),
        in_specs=[
            tile, tile, tile, tile,
            ml(0), ml(1), ml(2), ml(3),
            pl.BlockSpec((1, 2 * D_MODEL), lambda i: (0, 0)),
            pl.BlockSpec((1, D_MODEL), lambda i: (0, 0)),
            wspec, wspec, wspec,
        ],
        out_specs=[
            tile, tile, tile,
            pl.BlockSpec((tm, 2 * D_MODEL), lambda i: (i, 0)),
            tile, tile, tile,
            pl.BlockSpec((D_MODEL, tm), lambda i: (0, i)), pl.BlockSpec((D_MODEL, tm), lambda i: (0, i)),
            pl.BlockSpec((8, D_MODEL), lambda i: (0, 0)),
        ],
        out_shape=[
            _sds((T, D_MODEL), F32), _sds((T, D_MODEL), F32), _sds((T, D_MODEL), F32),
            _sds((T, 2 * D_MODEL), BF16),
            _sds((T, D_MODEL), BF16), _sds((T, D_MODEL), BF16), _sds((T, D_MODEL), BF16),
            _sds((D_MODEL, T), BF16), _sds((D_MODEL, T), BF16),
            _sds((8, D_MODEL), F32),
        ],
        compiler_params=_params(("arbitrary",), 56),
    )(*_hbm(x, target, z_rnn, z_attn, proj, proj, proj, proj, b_gate, g_post, w_rnn_out, w_attn_out, w_out))


def _matmul_t(at, b, name):
    M, T = at.shape
    N = b.shape[1]
    tk = min(1024, T)
    nt = T // tk

    def body(a_ref, b_ref, o_ref, ob_ref):
        @pl.when(pl.program_id(0) == 0)
        def _():
            o_ref[...] = jnp.zeros_like(o_ref)

        o_ref[...] += _dot(a_ref[...], b_ref[...])

        @pl.when(pl.program_id(0) == nt - 1)
        def _():
            ob_ref[...] = o_ref[...].astype(BF16)

    whole = pl.BlockSpec((M, N), lambda t: (0, 0))
    return pl.pallas_call(
        body,
        name=name,
        grid=(nt,),
        in_specs=[pl.BlockSpec((M, tk), lambda t: (0, t)), pl.BlockSpec((tk, N), lambda t: (t, 0))],
        out_specs=[whole, whole],
        out_shape=[_sds((M, N), F32), _sds((M, N), BF16)],
        compiler_params=_params(("arbitrary",), 48),
    )(*_hbm(at, b))


DPROJ_WIDTHS = (D_RNN, D_RNN, 1024, D_KV, D_KV, 1024, 2 * D_MODEL)


def _dproj_segments():
    segs, start = [[] for _ in range(N_CHIPS)], 0
    for p, width in enumerate(DPROJ_WIDTHS):
        for c in range(N_CHIPS):
            lo, hi = max(start, c * W_IN_SHARD), min(start + width, (c + 1) * W_IN_SHARD)
            if lo < hi:
                segs[c].append((p, lo - start, hi - start, lo - c * W_IN_SHARD, hi - c * W_IN_SHARD))
        start += width
    return segs


def _dh_bwd(pieces, w_in_g, x, dyx, g_pre, token):
    T = x.shape[0]
    tm = min(512, T)
    n = len(pieces)
    segs = _dproj_segments()

    def body(*refs):
        p_refs, w_hbm, x_ref, dyx_ref, g_ref = refs[0:n], refs[n], refs[n + 1], refs[n + 2], refs[n + 3]
        gx_ref, dg_ref, w_ref, w_sems = refs[n + 5], refs[n + 6], refs[n + 7], refs[n + 8]
        first = pl.program_id(0) == 0
        w_copies = [pltpu.make_async_copy(w_hbm.at[c], w_ref.at[c], w_sems.at[c]) for c in range(N_CHIPS)]

        @pl.when(first)
        def _():
            for cp in w_copies:
                cp.start()
            dg_ref[...] = jnp.zeros_like(dg_ref)

        dh = None
        for c in range(N_CHIPS):
            pl.when(first)(w_copies[c].wait)
            for p, a0, a1, u0, u1 in segs[c]:
                part = _dot_nt(p_refs[p][:, a0:a1].astype(BF16), w_ref[c, :, u0:u1])
                dh = part if dh is None else dh + part
        xv = x_ref[...]
        rstd = lax.rsqrt(jnp.mean(xv * xv, axis=-1, keepdims=True) + EPS)
        nx = xv * rstd
        dhg = dh * g_ref[...]
        gx_ref[...] = dyx_ref[...] + rstd * (dhg - nx * jnp.mean(dhg * nx, axis=-1, keepdims=True))
        dg_ref[pl.ds(0, 1), :] += jnp.sum(dh * nx, axis=0, keepdims=True)

    tile = pl.BlockSpec((tm, D_MODEL), lambda i: (i, 0))
    return pl.pallas_call(
        body,
        name="dh_bwd",
        grid=(T // tm,),
        in_specs=[pl.BlockSpec((tm, w), lambda i: (i, 0)) for w in DPROJ_WIDTHS] + [
            ANY, tile, tile,
            pl.BlockSpec((1, D_MODEL), lambda i: (0, 0)),
            pl.BlockSpec((8, 128), lambda i: (0, 0)),
        ],
        out_specs=[tile, pl.BlockSpec((8, D_MODEL), lambda i: (0, 0))],
        out_shape=[_sds((T, D_MODEL), F32), _sds((8, D_MODEL), F32)],
        scratch_shapes=[pltpu.VMEM(w_in_g.shape, BF16), pltpu.SemaphoreType.DMA((N_CHIPS,))],
        compiler_params=_params(("arbitrary",), 56),
    )(*_hbm(*pieces, w_in_g, x, dyx, g_pre, token))


def _dw_in(ht, pieces):
    T = ht.shape[1]
    tk = min(1024, T)
    nt = T // tk
    n = len(pieces)
    segs = _dproj_segments()

    def body(*refs):
        h_ref, p_refs, o_ref, ob_ref = refs[0], refs[1:n + 1], refs[n + 1], refs[n + 2]

        @pl.when(pl.program_id(1) == 0)
        def _():
            o_ref[...] = jnp.zeros_like(o_ref)

        for c in range(N_CHIPS):
            @pl.when(pl.program_id(0) == c)
            def _():
                for p, a0, a1, u0, u1 in segs[c]:
                    o_ref[:, u0:u1] += _dot(h_ref[...], p_refs[p][:, a0:a1].astype(BF16))

        @pl.when(pl.program_id(1) == nt - 1)
        def _():
            ob_ref[...] = o_ref[...].astype(BF16)

    def piece_spec(p):
        chips = [c for c in range(N_CHIPS) if any(s[0] == p for s in segs[c])]

        def index(c, t):
            used = functools.reduce(jnp.logical_or, [c == k for k in chips])
            return (jnp.where(used, t, 0), 0)

        return pl.BlockSpec((tk, DPROJ_WIDTHS[p]), index)

    return pl.pallas_call(
        body,
        name="dw_in",
        grid=(N_CHIPS, nt),
        in_specs=[pl.BlockSpec((D_MODEL, tk), lambda c, t: (0, t))] + [piece_spec(p) for p in range(n)],
        out_specs=[pl.BlockSpec((None, D_MODEL, W_IN_SHARD), lambda c, t: (c, 0, 0))] * 2,
        out_shape=[_sds((N_CHIPS, D_MODEL, W_IN_SHARD), F32), _sds((N_CHIPS, D_MODEL, W_IN_SHARD), BF16)],
        compiler_params=_params(("parallel", "arbitrary"), 56),
    )(*_hbm(ht, *pieces))


ELEMENTWISE_TILE_BYTES = MIB


def _row_tile(rows, cols, limit=ELEMENTWISE_TILE_BYTES):
    if rows * cols * 4 <= limit:
        return rows
    for t in (512, 256, 128, 64, 32, 16, 8):
        if rows % t == 0 and t * cols * 4 <= limit:
            return t
    return rows


def _group_tiles(groups):
    tiles = [_row_tile(g[0].shape[0], g[0].shape[1] * len(g)) for g in groups]
    steps = max(g[0].shape[0] // t for g, t in zip(groups, tiles))
    return steps, [g[0].shape[0] // steps for g in groups]


def _chip_sum(groups, chip_core, name):
    ps = [p for group_ps, _ in groups for p in group_ps]
    gots = [g for _, group_gots in groups for g in group_gots]
    n = len(ps)
    steps, group_rows = _group_tiles([group_ps for group_ps, _ in groups])
    rows = [tr for (group_ps, _), tr in zip(groups, group_rows) for _ in group_ps]

    def body(jc_ref, *refs):
        for a in range(n):
            p_ref, g0_ref, g1_ref, g2_ref, o_ref = refs[a], refs[n + 3 * a], refs[n + 3 * a + 1], refs[n + 3 * a + 2], \
                refs[4 * n + a]
            o_ref[...] = ((p_ref[...] + g0_ref[...].astype(F32)) + g1_ref[...].astype(F32)) + g2_ref[...].astype(F32)

    tile = lambda p, tr: pl.BlockSpec((tr, p.shape[1]), lambda i, jc_ref: (i, 0))
    rel = lambda p, tr, r: pl.BlockSpec((None, tr, p.shape[1]), lambda i, jc_ref: (r, i, 0))
    half = lambda p, tr: pl.BlockSpec((tr, p.shape[1]), lambda i, jc_ref: (jc_ref[1] * steps + i, 0))
    outs = pl.pallas_call(
        body,
        name=name,
        grid_spec=pltpu.PrefetchScalarGridSpec(
            num_scalar_prefetch=1,
            grid=(steps,),
            in_specs=[tile(p, tr) for p, tr in zip(ps, rows)] + [
                rel(p, tr, r) for p, tr in zip(ps, rows) for r in range(3)],
            out_specs=[half(p, tr) for p, tr in zip(ps, rows)],
        ),
        out_shape=[_sds((2 * p.shape[0], p.shape[1]), F32) for p in ps],
        compiler_params=_params(("parallel",), 48),
    )(chip_core, *_hbm(*ps, *[g for got in gots for g in (got, got, got)]))
    return list(outs)


def _place_shards(shards, chip, name):
    n = len(shards)
    tiles = [_row_tile(s.shape[0], s.shape[1]) for s in shards]
    steps = max(s.shape[0] // t for s, t in zip(shards, tiles))
    tiles = [s.shape[0] // steps for s in shards]

    def body(j_ref, *refs):
        for a in range(n):
            refs[n + a][...] = refs[a][...].astype(BF16)

    return pl.pallas_call(
        body,
        name=name,
        grid_spec=pltpu.PrefetchScalarGridSpec(
            num_scalar_prefetch=1,
            grid=(steps,),
            in_specs=[pl.BlockSpec((t, s.shape[1]), lambda i, j_ref: (i, 0)) for s, t in zip(shards, tiles)],
            out_specs=[pl.BlockSpec((None, t, s.shape[1]), lambda i, j_ref: (j_ref[0], i, 0))
                       for s, t in zip(shards, tiles)],
        ),
        out_shape=[_sds((N_CHIPS,) + s.shape, BF16) for s in shards],
        compiler_params=_params(("parallel",), 48),
    )(chip, *_hbm(*shards))


def _adamw_update(w, g, m, v):
    c1 = 1.0 - ADAM_B1 ** ADAM_STEP
    c2 = 1.0 - ADAM_B2 ** ADAM_STEP
    nm = ADAM_B1 * m + (1.0 - ADAM_B1) * g
    nv = ADAM_B2 * v + (1.0 - ADAM_B2) * (g * g)
    return (-ADAM_LR) * ((nm / c1) / (jnp.sqrt(nv / c2) + ADAM_EPS) + ADAM_WD * w), nm, nv


def _adamw(groups, name):
    params = [p for group in groups for p in group]
    n = len(params)
    steps, group_rows = _group_tiles([[p[0] for p in group] for group in groups])
    rows = [tr for group, tr in zip(groups, group_rows) for _ in group]

    def body(*refs):
        for a in range(n):
            w_ref, g_ref, m_ref, v_ref = refs[4 * a:4 * a + 4]
            d_ref, nm_ref, nv_ref, go_ref = refs[4 * n + 4 * a:4 * n + 4 * a + 4]
            g = g_ref[...]
            d_ref[...], nm_ref[...], nv_ref[...] = _adamw_update(w_ref[...], g, m_ref[...], v_ref[...])
            go_ref[...] = g

    specs = [pl.BlockSpec((tr, p[0].shape[1]), lambda i: (i, 0)) for p, tr in zip(params, rows) for _ in range(4)]
    outs = pl.pallas_call(
        body, name=name, grid=(steps,), in_specs=specs, out_specs=specs,
        out_shape=[_sds(p[0].shape, F32) for p in params for _ in range(4)],
        compiler_params=_params(("parallel",), 48),
    )(*_hbm(*[t for p in params for t in p]))
    return [tuple(outs[4 * a:4 * a + 4]) for a in range(n)]


def _adamw_whole(params, rows_params, packed, name):
    n, k = len(params), len(rows_params)

    def body(*refs):
        ins, packed_ref, outs = refs[0:4 * n + 3 * k], refs[4 * n + 3 * k], refs[4 * n + 3 * k + 1:]
        for a in range(n):
            w_ref, g_ref, m_ref, v_ref = ins[4 * a:4 * a + 4]
            d_ref, nm_ref, nv_ref = outs[3 * a:3 * a + 3]
            d_ref[...], nm_ref[...], nv_ref[...] = _adamw_update(w_ref[...], g_ref[...], m_ref[...], v_ref[...])
        for b, (_, (first, rows, lanes), _, _) in enumerate(rows_params):
            w_ref, m_ref, v_ref = ins[4 * n + 3 * b:4 * n + 3 * b + 3]
            d_ref, nm_ref, nv_ref, g_ref = outs[3 * n + 4 * b:3 * n + 4 * b + 4]
            for r in range(rows):
                g_ref[:, r * lanes:(r + 1) * lanes] = packed_ref[pl.ds(first + r, 1), 0:lanes]
            d_ref[...], nm_ref[...], nv_ref[...] = _adamw_update(w_ref[...], g_ref[...], m_ref[...], v_ref[...])

    def whole(t):
        return pl.BlockSpec(t.shape, lambda i: (0,) * t.ndim)

    flat = [t for p in params for t in p] + [t for w, _, m, v in rows_params for t in (w, m, v)] + [packed]
    like = [p[0] for p in params for _ in range(3)] + [p[0] for p in rows_params for _ in range(4)]
    outs = pl.pallas_call(
        body, name=name, grid=(1,), in_specs=[whole(t) for t in flat], out_specs=[whole(t) for t in like],
        out_shape=[_sds(t.shape, F32) for t in like], compiler_params=_params(("arbitrary",), 48),
    )(*_hbm(*flat))
    return [tuple(outs[3 * a:3 * a + 3]) for a in range(n)] + [
        tuple(outs[3 * n + 4 * b:3 * n + 4 * b + 4]) for b in range(k)]


def _place():
    return lax.axis_index("x"), lax.axis_index("y"), lax.axis_index("c")


def _chip_of(x, y, r):
    return (x ^ (r >> 1), y ^ (r & 1))


ANY = pl.BlockSpec(memory_space=pl.ANY)


def _gather_weights(placed, cw8):
    nbig = len(placed)
    halves = [s.shape[1] // 2 for s in placed]
    pieces = [max(1, h // 64) for h in halves]
    rows = [h // p for h, p in zip(halves, pieces)]
    order = [(a, q) for q in range(max(pieces)) for a in range(nbig) if q < pieces[a]]
    ici_sem = {(a, q, r): 3 * i + (r - 1) for i, (a, q) in enumerate(order) for r in (1, 2, 3)}
    cw_sem = {r: 3 * len(order) + (r - 1) for r in (1, 2, 3)}
    d2d_sem = {key: 3 * len(order) + 3 + k for key, k in ici_sem.items()}
    nsem = 6 * len(order) + 3

    def body(*refs):
        cw_ref, dsts, gcw_ref = refs[nbig], refs[nbig + 1:2 * nbig + 1], refs[2 * nbig + 1]
        send_sems, recv_sems = refs[2 * nbig + 2:]
        x, y, c = _place()
        j = 2 * x + y

        def piece_rows(a, q, core):
            return pl.ds(pl.multiple_of(core * halves[a] + q * rows[a], 16), rows[a])

        def ici(a, q, r):
            tx, ty = _chip_of(x, y, r)
            k = ici_sem[(a, q, r)]
            region = dsts[a].at[j, piece_rows(a, q, c), :]
            return pltpu.make_async_remote_copy(
                src_ref=region, dst_ref=region, send_sem=send_sems.at[k], recv_sem=recv_sems.at[k],
                device_id=(tx, ty, c), device_id_type=MESH)

        def ici_landed(a, q, r):
            tx, ty = _chip_of(x, y, r)
            k = ici_sem[(a, q, r)]
            region = dsts[a].at[2 * tx + ty, piece_rows(a, q, c), :]
            return pltpu.make_async_remote_copy(
                src_ref=region, dst_ref=region, send_sem=send_sems.at[k], recv_sem=recv_sems.at[k],
                device_id=(tx, ty, c), device_id_type=MESH)

        def d2d(a, q, r, core):
            tx, ty = _chip_of(x, y, r)
            k = d2d_sem[(a, q, r)]
            region = dsts[a].at[2 * tx + ty, piece_rows(a, q, core), :]
            return pltpu.make_async_remote_copy(
                src_ref=region, dst_ref=region, send_sem=send_sems.at[k], recv_sem=recv_sems.at[k],
                device_id=(x, y, 1 - c), device_id_type=MESH)

        def cw_copy(r):
            tx, ty = _chip_of(x, y, r)
            k = cw_sem[r]
            return pltpu.make_async_remote_copy(
                src_ref=cw_ref, dst_ref=gcw_ref.at[j], send_sem=send_sems.at[k], recv_sem=recv_sems.at[k],
                device_id=(tx, ty, c), device_id_type=MESH)

        def cw_landed(r):
            tx, ty = _chip_of(x, y, r)
            k = cw_sem[r]
            region = gcw_ref.at[2 * tx + ty]
            return pltpu.make_async_remote_copy(
                src_ref=region, dst_ref=region, send_sem=send_sems.at[k], recv_sem=recv_sems.at[k],
                device_id=(tx, ty, c), device_id_type=MESH)

        def relay(a, q, origin, to):
            ox, oy = _chip_of(x, y, origin)
            tx, ty = _chip_of(x, y, to)
            k = ici_sem[(a, q, 3)]
            region = dsts[a].at[2 * ox + oy, piece_rows(a, q, c), :]
            return pltpu.make_async_remote_copy(
                src_ref=region, dst_ref=region, send_sem=send_sems.at[k], recv_sem=recv_sems.at[k],
                device_id=(tx, ty, c), device_id_type=MESH)

        first = [ici(a, q, r) for (a, q) in order for r in (1, 2)] + [cw_copy(r) for r in (1, 2, 3)]
        for cp in first:
            cp.start()
        passed = []
        for (a, q) in order:
            for r in (1, 2):
                ici_landed(a, q, r).wait_recv()
                if q % 2 == r - 1:
                    cp = relay(a, q, r, 3 - r)
                    cp.start()
                    passed.append(cp)
                cp = d2d(a, q, r, c)
                cp.start()
                passed.append(cp)
        for (a, q) in order:
            ici_landed(a, q, 3).wait_recv()
            cp = d2d(a, q, 3, c)
            cp.start()
            passed.append(cp)
        for r in (1, 2, 3):
            cw_landed(r).wait_recv()
        for (a, q) in order:
            for r in (1, 2, 3):
                d2d(a, q, r, 1 - c).wait_recv()
        for cp in first + passed:
            cp.wait_send()

    return pl.pallas_call(
        body,
        name="gather_weights",
        in_specs=[ANY] * (nbig + 1),
        out_specs=[ANY] * (nbig + 1),
        out_shape=[_sds(s.shape, s.dtype) for s in placed] + [_sds((N_CHIPS,) + cw8.shape, cw8.dtype)],
        input_output_aliases={a: a for a in range(nbig)},
        scratch_shapes=[pltpu.SemaphoreType.DMA((nsem,)), pltpu.SemaphoreType.DMA((nsem,))],
    )(*placed, cw8)


def _gather_late_start(placed, after, name):
    n = len(placed)
    halves = [s.shape[1] // 2 for s in placed]

    def body(*refs):
        g_refs = refs[0:n]
        send_sems, recv_sems, token = refs[n + 1], refs[n + 2], refs[-1]
        x, y, c = _place()
        j = 2 * x + y
        for a in range(n):
            mine = g_refs[a].at[j, pl.ds(pl.multiple_of(c * halves[a], 16), halves[a]), :]
            for r in (1, 2, 3):
                tx, ty = _chip_of(x, y, r)
                for to_core in (0, 1):
                    k = ((a * 3 + (r - 1)) * 2 + c) * 2 + to_core
                    pltpu.make_async_remote_copy(
                        src_ref=mine, dst_ref=mine, send_sem=send_sems.at[k], recv_sem=recv_sems.at[k],
                        device_id=(tx, ty, to_core), device_id_type=MESH).start()
        token[...] = jnp.zeros_like(token)

    hbm = lambda t: pltpu.HBM(t.shape, t.dtype)
    keep = lambda t: pltpu.with_memory_space_constraint(t, pltpu.HBM)
    nsem = 12 * n
    outs = pl.pallas_call(
        body,
        name=name,
        in_specs=[HBM] * n + [ANY],
        out_specs=(SEM, SEM, *[HBM] * n, pl.BlockSpec(memory_space=pltpu.VMEM)),
        out_shape=(pltpu.SemaphoreType.DMA((nsem,)), pltpu.SemaphoreType.DMA((nsem,)), *[hbm(p) for p in placed],
                   jax.ShapeDtypeStruct((8, 128), F32)),
        input_output_aliases={i: 2 + i for i in range(n)},
        compiler_params=pltpu.CompilerParams(has_side_effects=DATAFLOW),
    )(*[keep(p) for p in placed], after)
    return outs[0], outs[1], list(outs[2:2 + n]), outs[-1]


def _gather_late_wait(send_sems, recv_sems, thru, after, name):
    n = len(thru)
    halves = [s.shape[1] // 2 for s in thru]

    def body(*refs):
        g_refs = refs[0:n]
        send_sems, recv_sems = refs[n], refs[n + 1]
        x, y, c = _place()
        j = 2 * x + y
        for a in range(n):
            mine = g_refs[a].at[j, pl.ds(pl.multiple_of(c * halves[a], 16), halves[a]), :]
            for r in (1, 2, 3):
                tx, ty = _chip_of(x, y, r)
                for other in (0, 1):
                    k_out = ((a * 3 + (r - 1)) * 2 + c) * 2 + other
                    pltpu.make_async_remote_copy(
                        src_ref=mine, dst_ref=mine, send_sem=send_sems.at[k_out], recv_sem=recv_sems.at[k_out],
                        device_id=(tx, ty, other), device_id_type=MESH).wait_send()
                    k_in = ((a * 3 + (r - 1)) * 2 + other) * 2 + c
                    theirs = g_refs[a].at[2 * tx + ty, pl.ds(other * halves[a], halves[a]), :]
                    pltpu.make_async_remote_copy(
                        src_ref=theirs, dst_ref=theirs, send_sem=send_sems.at[k_in], recv_sem=recv_sems.at[k_in],
                        device_id=(tx, ty, other), device_id_type=MESH).wait_recv()

    hbm = lambda t: pltpu.HBM(t.shape, t.dtype)
    outs = pl.pallas_call(
        body,
        name=name,
        in_specs=[HBM] * n + [SEM, SEM, ANY],
        out_specs=[HBM] * n,
        out_shape=[hbm(t) for t in thru],
        input_output_aliases={i: i for i in range(n)},
        compiler_params=pltpu.CompilerParams(has_side_effects=DATAFLOW),
    )(*thru, send_sems, recv_sems, after)
    return list(outs)


D2D_PIECE_ROWS = 64
PAIR_SUM_TILE_BYTES = 2 * MIB


def _pair_sum(gs, gbs, chip_core, name):
    n = len(gs)
    nch, R, C = gs[0].shape
    h = R // 2
    tr = _row_tile(h, C * n, PAIR_SUM_TILE_BYTES)
    nt = h // tr
    rows = min(D2D_PIECE_ROWS, tr)

    def body(jc_ref, *refs):
        g_refs, gb_refs, p_refs, pb_refs = refs[0:n], refs[n:2 * n], refs[2 * n:3 * n], refs[3 * n:4 * n]
        got_refs, send_sems, recv_sems = refs[4 * n:5 * n], refs[5 * n], refs[5 * n + 1]
        i, j = pl.program_id(0), pl.program_id(1)
        x, y, c = _place()

        def copy(a, ti, tj, first, count):
            src_rows = pl.ds(pl.multiple_of((1 - c) * h + ti * tr + first, 16), count)
            dst_rows = pl.ds(pl.multiple_of(ti * tr + first, 16), count)
            return pltpu.make_async_remote_copy(
                src_ref=gb_refs[a].at[tj, src_rows, :], dst_ref=got_refs[a].at[tj, dst_rows, :],
                send_sem=send_sems.at[a, ti, tj], recv_sem=recv_sems.at[a, ti, tj],
                device_id=(x, y, 1 - c), device_id_type=MESH)

        @pl.when((i == 0) & (j == 0))
        def _():
            for ti in range(nt):
                for tj in range(nch):
                    for a in range(n):
                        for q in range(tr // rows):
                            copy(a, ti, tj, q * rows, rows).start()

        for a in range(n):
            copy(a, i, j, 0, tr).wait()
            s = g_refs[a][...] + got_refs[a][j, pl.ds(pl.multiple_of(i * tr, 16), tr), :].astype(F32)
            pb_refs[a][...] = s.astype(BF16)

            @pl.when(j == jc_ref[0])
            def _():
                p_refs[a][...] = s

    by_chip = pl.BlockSpec((None, tr, C), lambda i, j, jc_ref: (j, i, 0))
    outs = pl.pallas_call(
        body,
        name=name,
        grid_spec=pltpu.PrefetchScalarGridSpec(
            num_scalar_prefetch=1,
            grid=(nt, nch),
            in_specs=[pl.BlockSpec((None, tr, C), lambda i, j, jc_ref: (j, jc_ref[1] * nt + i, 0))] * n + [ANY] * n,
            out_specs=[pl.BlockSpec((tr, C), lambda i, j, jc_ref: (i, 0))] * n + [by_chip] * n,
            scratch_shapes=[pltpu.VMEM((nch, h, C), BF16)] * n + [pltpu.SemaphoreType.DMA((n, nt, nch))] * 2,
        ),
        out_shape=[_sds((h, C), F32)] * n + [_sds((nch, h, C), BF16)] * n,
        compiler_params=_params(("arbitrary", "arbitrary"), 48),
    )(chip_core, *_hbm(*gs, *gbs))
    return list(outs[:n]), list(outs[n:])


HBM = pl.BlockSpec(memory_space=pltpu.HBM)
SEM = pl.BlockSpec(memory_space=pltpu.SEMAPHORE)
DATAFLOW = pltpu.SideEffectType.DATAFLOW_SIDE_EFFECTING


def _chip_copy(p_refs, land_refs, send_sems, recv_sems, a, r, blocked):
    x, y, c = _place()
    tx, ty = _chip_of(x, y, r)
    k = a * 3 + (r - 1)
    return pltpu.make_async_remote_copy(
        src_ref=p_refs[a].at[2 * tx + ty] if blocked else p_refs[a], dst_ref=land_refs[a].at[r - 1],
        send_sem=send_sems.at[k], recv_sem=recv_sems.at[k], device_id=(tx, ty, c), device_id_type=MESH)


def _chip_exchange_start(psums, name, blocked=True):
    n = len(psums)
    lands = [lax.empty((3,) + (p.shape[1:] if blocked else p.shape), p.dtype) for p in psums]

    def body(*refs):
        p_refs, land_refs = refs[0:n], refs[n:2 * n]
        send_sems, recv_sems, token = refs[2 * n], refs[2 * n + 1], refs[-1]
        for a in range(n):
            for r in (1, 2, 3):
                _chip_copy(p_refs, land_refs, send_sems, recv_sems, a, r, blocked).start()
        token[...] = jnp.zeros_like(token)

    hbm = lambda t: pltpu.HBM(t.shape, t.dtype)
    keep = lambda t: pltpu.with_memory_space_constraint(t, pltpu.HBM)
    outs = pl.pallas_call(
        body,
        name=name,
        in_specs=[HBM] * (2 * n),
        out_specs=(SEM, SEM, *[HBM] * (2 * n), pl.BlockSpec(memory_space=pltpu.VMEM)),
        out_shape=(pltpu.SemaphoreType.DMA((3 * n,)), pltpu.SemaphoreType.DMA((3 * n,)),
                   *[hbm(p) for p in psums], *[hbm(l) for l in lands], _sds((8, 128), F32)),
        input_output_aliases={i: 2 + i for i in range(2 * n)},
        compiler_params=pltpu.CompilerParams(has_side_effects=DATAFLOW),
    )(*[keep(p) for p in psums], *[keep(l) for l in lands])
    return outs[0], outs[1], list(outs[2:2 + n]), list(outs[2 + n:2 + 2 * n]), outs[-1]


def _chip_exchange_wait(send_sems, recv_sems, p_thru, land_thru, after, name, blocked=True):
    n = len(p_thru)

    def body(*refs):
        p_refs, land_refs = refs[0:n], refs[n:2 * n]
        send_sems, recv_sems = refs[2 * n], refs[2 * n + 1]
        for a in range(n):
            for r in (1, 2, 3):
                copy = _chip_copy(p_refs, land_refs, send_sems, recv_sems, a, r, blocked)
                copy.wait_send()
                copy.wait_recv()

    hbm = lambda t: pltpu.HBM(t.shape, t.dtype)
    outs = pl.pallas_call(
        body,
        name=name,
        in_specs=[HBM] * (2 * n) + [SEM, SEM, ANY],
        out_specs=[HBM] * (2 * n),
        out_shape=[hbm(p) for p in p_thru] + [hbm(l) for l in land_thru],
        input_output_aliases={i: i for i in range(2 * n)},
        compiler_params=pltpu.CompilerParams(has_side_effects=DATAFLOW),
    )(*p_thru, *land_thru, send_sems, recv_sems, after)
    return list(outs[0:n]), list(outs[n:2 * n])


def _pair_share(fulls):
    n = len(fulls)
    halves = [f.shape[0] // 2 for f in fulls]

    def body(*refs):
        full_refs = refs[n:2 * n]
        send_sems, recv_sems = refs[2 * n:]
        x, y, c = _place()

        def half_of(a, core):
            return full_refs[a].at[pl.ds(pl.multiple_of(core * halves[a], 8), halves[a]), :]

        def remote(a, src, dst):
            return pltpu.make_async_remote_copy(
                src_ref=src, dst_ref=dst, send_sem=send_sems.at[a], recv_sem=recv_sems.at[a],
                device_id=(x, y, 1 - c), device_id_type=MESH)

        for a in range(n):
            for q in range(halves[a] // D2D_PIECE_ROWS):
                piece = full_refs[a].at[
                    pl.ds(pl.multiple_of(c * halves[a] + q * D2D_PIECE_ROWS, 8), D2D_PIECE_ROWS), :]
                remote(a, piece, piece).start()
        for a in range(n):
            remote(a, half_of(a, c), half_of(a, c)).wait_send()
            remote(a, half_of(a, 1 - c), half_of(a, 1 - c)).wait_recv()

    return pl.pallas_call(
        body,
        name="pair_share",
        in_specs=[ANY] * n,
        out_specs=[ANY] * n,
        out_shape=[_sds(f.shape, F32) for f in fulls],
        input_output_aliases={a: a for a in range(n)},
        scratch_shapes=[pltpu.SemaphoreType.DMA((n,)), pltpu.SemaphoreType.DMA((n,))],
    )(*fulls)


def _small_pair_sum(s):
    R, C = s.shape
    V = SMALL_VECTOR_ROWS

    def body(s_ref, v_ref, m_ref, sib, send_sem, recv_sem):
        x, y, c = _place()

        def to_sib(src, dst):
            return pltpu.make_async_remote_copy(
                src_ref=src, dst_ref=dst, send_sem=send_sem, recv_sem=recv_sem,
                device_id=(x, y, 1 - c), device_id_type=MESH)

        for q in range(R // 8):
            to_sib(s_ref.at[pl.ds(8 * q, 8), :], sib.at[pl.ds(8 * q, 8), :]).start()
        to_sib(s_ref, sib).wait()
        v_ref[...] = s_ref[pl.ds(0, V), :] + sib[pl.ds(0, V), :]
        m_ref[...] = (s_ref[pl.ds(V, R - V), :] + sib[pl.ds(V, R - V), :]).astype(BF16)

    return pl.pallas_call(
        body,
        name="small_pair_sum",
        in_specs=[pl.BlockSpec(memory_space=pltpu.VMEM)],
        out_specs=[pl.BlockSpec(memory_space=pltpu.VMEM)] * 2,
        out_shape=[jax.ShapeDtypeStruct((V, C), F32), jax.ShapeDtypeStruct((R - V, C), BF16)],
        scratch_shapes=[pltpu.VMEM((R, C), F32), pltpu.SemaphoreType.DMA, pltpu.SemaphoreType.DMA],
    )(s)


def _small_total(chip, own, landed):
    V, C = own[0].shape
    M = own[1].shape[0]

    def body(j_ref, v_ref, m_ref, lv_ref, lm_ref, o_ref, chips_v, chips_m):
        j = j_ref[0]
        chips_v[j] = v_ref[...]
        chips_m[j] = m_ref[...]
        for r in (1, 2, 3):
            chips_v[j ^ r] = lv_ref[r - 1]
            chips_m[j ^ r] = lm_ref[r - 1]
        o_ref[pl.ds(0, V), :] = (chips_v[0] + chips_v[1]) + (chips_v[2] + chips_v[3])
        o_ref[pl.ds(V, M), :] = (chips_m[0].astype(F32) + chips_m[1].astype(F32)) + (
            chips_m[2].astype(F32) + chips_m[3].astype(F32))

    vmem = pl.BlockSpec(memory_space=pltpu.VMEM)
    return pl.pallas_call(
        body,
        name="small_total",
        in_specs=[pl.BlockSpec(memory_space=pltpu.SMEM), vmem, vmem, vmem, vmem],
        out_specs=vmem,
        out_shape=jax.ShapeDtypeStruct((V + M, C), F32),
        scratch_shapes=[pltpu.VMEM((N_CHIPS, V, C), F32), pltpu.VMEM((N_CHIPS, M, C), BF16)],
    )(chip, own[0], own[1], landed[0], landed[1])


def _local_grads(x, target, g_pre, w_in_g, b_gate, conv_w, conv_b, w_rg_a, b_rg_a, w_rg_x, b_rg_x, lam, sinks,
                 out_weights, fwd_token, g_post, on_out_grads, on_w_in_grad):
    b_a = b_rg_a.reshape(1, D_RNN)
    b_x = b_rg_x.reshape(1, D_RNN)

    proj, ht = _proj_fwd(x, g_pre, w_in_g)
    y_rnn, z_rnn, conv, z_rnn_t = _rnn_fwd(proj, conv_w, conv_b, w_rg_a, w_rg_x, b_a, b_x, lam, fwd_token)
    bias = _attn_bias()
    y_attn, z_attn, lse = _attn_fwd(proj, sinks, bias)
    w_rnn_out, w_attn_out, w_out = out_weights(z_attn)
    dyx, dz_rnn, dz_attn, dml, dout, dbr_rnn, dbr_attn, merged_t, z_attn_t, head_small = _head(
        x, target, z_rnn, z_attn, proj, b_gate, g_post, w_rnn_out, w_attn_out, w_out)
    out_grads = [_matmul_t(z_rnn_t, dbr_rnn, "dw_rnn_out"), _matmul_t(z_attn_t, dbr_attn, "dw_attn_out"),
                 _matmul_t(merged_t, dout, "dw_out")]
    shard_rows = lambda d: d.reshape(N_CHIPS, OUT_SHARD, D_MODEL)
    token = on_out_grads([shard_rows(g) for g, _ in out_grads], [shard_rows(gb) for _, gb in out_grads])
    dq, dk, dv, dag, attn_small = _attn_bwd(proj, y_attn, lse, dz_attn, sinks, bias, token)
    drx, drg, dwa, dwx, rnn_small = _rnn_bwd(proj, conv, y_rnn, dz_rnn, conv_w, w_rg_a, w_rg_x, b_a, b_x, lam)
    dproj = [drx, drg, dq, dk, dv, dag, dml]
    token = on_w_in_grad(*_dw_in(ht, dproj))
    grad_x, dh_small = _dh_bwd(dproj, w_in_g, x, dyx, g_pre, token)
    small = jnp.concatenate([rnn_small, head_small, dh_small + attn_small,
                             dwa.reshape(64, 1024), dwx.reshape(64, 1024)], axis=0)
    return grad_x, small


ROW_LOSS = 11


SMALL_ROW_TENSORS = {"lru_lambda": (2, 1, D_RNN), "conv_b": (3, 1, D_RNN), "post_norm_g": (8, 1, D_MODEL),
                     "b_gate": (9, 2, D_MODEL), "pre_norm_g": (16, 1, D_MODEL), "attn_sinks": (17, 1, N_Q_HEADS)}


def _unpack_small(s, conv_cols):
    return {
        "b_rg_a": s[0:1].reshape(1, 16, 64), "b_rg_x": s[1:2].reshape(1, 16, 64),
        "conv_w": s[4:8, 0:conv_cols].reshape(1, CONV_W, conv_cols),
        "w_rg_a": s[24:88].reshape(1, 16, 64, 64), "w_rg_x": s[88:152].reshape(1, 16, 64, 64),
    }


WEIGHTS = ["pre_norm_g", "w_in", "b_gate", "conv_w", "conv_b", "w_rg_a", "b_rg_a", "w_rg_x", "b_rg_x", "lru_lambda",
           "attn_sinks", "w_rnn_out", "w_attn_out", "w_out", "post_norm_g"]
BIG = ["w_in", "w_rnn_out", "w_attn_out", "w_out"]


def kernel(x, pre_norm_g, w_in, b_gate, conv_w, conv_b, w_rg_a, b_rg_a, w_rg_x, b_rg_x, lru_lambda, attn_sinks, w_rnn_out, w_attn_out, w_out, post_norm_g, loss_target, m_pre_norm_g, m_w_in, m_b_gate, m_conv_w, m_conv_b, m_w_rg_a, m_b_rg_a, m_w_rg_x, m_b_rg_x, m_lru_lambda, m_attn_sinks, m_w_rnn_out, m_w_attn_out, m_w_out, m_post_norm_g, v_pre_norm_g, v_w_in, v_b_gate, v_conv_w, v_conv_b, v_w_rg_a, v_b_rg_a, v_w_rg_x, v_b_rg_x, v_lru_lambda, v_attn_sinks, v_w_rnn_out, v_w_attn_out, v_w_out, v_post_norm_g):
    w = dict(pre_norm_g=pre_norm_g, w_in=w_in, b_gate=b_gate, conv_w=conv_w, conv_b=conv_b, w_rg_a=w_rg_a,
             b_rg_a=b_rg_a, w_rg_x=w_rg_x, b_rg_x=b_rg_x, lru_lambda=lru_lambda, attn_sinks=attn_sinks,
             w_rnn_out=w_rnn_out, w_attn_out=w_attn_out, w_out=w_out, post_norm_g=post_norm_g)
    m = dict(pre_norm_g=m_pre_norm_g, w_in=m_w_in, b_gate=m_b_gate, conv_w=m_conv_w, conv_b=m_conv_b, w_rg_a=m_w_rg_a,
             b_rg_a=m_b_rg_a, w_rg_x=m_w_rg_x, b_rg_x=m_b_rg_x, lru_lambda=m_lru_lambda, attn_sinks=m_attn_sinks,
             w_rnn_out=m_w_rnn_out, w_attn_out=m_w_attn_out, w_out=m_w_out, post_norm_g=m_post_norm_g)
    v = dict(pre_norm_g=v_pre_norm_g, w_in=v_w_in, b_gate=v_b_gate, conv_w=v_conv_w, conv_b=v_conv_b, w_rg_a=v_w_rg_a,
             b_rg_a=v_b_rg_a, w_rg_x=v_w_rg_x, b_rg_x=v_b_rg_x, lru_lambda=v_lru_lambda, attn_sinks=v_attn_sinks,
             w_rnn_out=v_w_rnn_out, w_attn_out=v_w_attn_out, w_out=v_w_out, post_norm_g=v_post_norm_g)
    chip = 2 * lax.axis_index("x") + lax.axis_index("y")

    chip_idx = chip.astype(jnp.int32).reshape(1)
    chip_core = jnp.stack([chip, lax.axis_index("c")]).astype(jnp.int32)
    cw8 = jnp.pad(conv_w[0], ((0, 8 - CONV_W), (0, 0)))
    placed = _place_shards([w_in[0], w_rnn_out[0], w_attn_out[0], w_out[0]], chip_idx, "place_shards")
    win_g, cw_g = _gather_weights(placed[:1], cw8)
    late_send, late_recv, late_thru, late_token = _gather_late_start(placed[1:], win_g, "gather_late_start")
    cw_g = lax.dynamic_update_slice_in_dim(cw_g, cw8[None], chip, axis=0)
    conv_w_full = jnp.transpose(cw_g[:, 0:CONV_W, :], (1, 0, 2)).reshape(CONV_W, D_RNN)

    started = {}

    def start_reduction(tag, grads, grads_b16):
        psums, psums_b16 = _pair_sum(grads, grads_b16, chip_core, "pair_sum_" + tag)
        send_sems, recv_sems, p_thru, land_thru, token = _chip_exchange_start(psums_b16, "chip_exchange_start_" + tag)
        started[tag] = (psums, send_sems, recv_sems, p_thru, land_thru)
        return token

    def end_reduction(tag, after):
        psums, send_sems, recv_sems, p_thru, land_thru = started[tag]
        _, landed = _chip_exchange_wait(send_sems, recv_sems, p_thru, land_thru, after, "chip_exchange_wait_" + tag)
        return psums, landed

    def out_weights(after):
        gathered = _gather_late_wait(late_send, late_recv, late_thru, after, "gather_late_wait")
        return [g.reshape(D_MODEL, D_MODEL) for g in gathered]

    grad_x, small = _local_grads(
        x[0], loss_target[0], pre_norm_g, win_g, b_gate, conv_w_full, conv_b, w_rg_a[0], b_rg_a[0], w_rg_x[0],
        b_rg_x[0], lru_lambda, attn_sinks[0], out_weights, late_token, post_norm_g,
        on_out_grads=lambda grads, grads_b16: start_reduction("out", grads, grads_b16),
        on_w_in_grad=lambda grad, grad_b16: start_reduction("in", [grad], [grad_b16]))

    small_chip = _small_pair_sum(small)
    small_send, small_recv, small_thru, small_land, small_token = _chip_exchange_start(
        list(small_chip), "small_exchange_start", blocked=False)

    halves = _chip_sum([end_reduction("in", small_token), end_reduction("out", small_token)], chip_core, "chip_sum")
    gbig = dict(zip(BIG, _pair_share(halves)))

    grads, delta, new_m, new_v = {}, {}, {}, {}
    updates = _adamw([[(w[n][0], gbig[n], m[n][0], v[n][0]) for n in names] for names in (BIG[:1], BIG[1:])],
                     "adamw_big")
    for n, (d, nm, nv, g) in zip(BIG, updates):
        grads[n], delta[n], new_m[n], new_v[n] = g[None], d[None], nm[None], nv[None]

    small_own, small_landed = _chip_exchange_wait(small_send, small_recv, small_thru, small_land, delta[BIG[-1]],
                                                  "small_exchange_wait", blocked=False)
    small_sum = _small_total(chip_idx, small_own, small_landed)
    total_loss = small_sum[ROW_LOSS, 0]
    gsmall = _unpack_small(small_sum, D_RNN)
    conv_shard = D_RNN // N_CHIPS
    gsmall["conv_w"] = lax.dynamic_slice_in_dim(gsmall["conv_w"], chip * conv_shard, conv_shard, axis=2)
    for n in gsmall:
        grads[n] = gsmall[n].reshape(w[n].shape)
    updates = _adamw_whole([(w[n], grads[n], m[n], v[n]) for n in gsmall],
                           [(w[n], rows, m[n], v[n]) for n, rows in SMALL_ROW_TENSORS.items()], small_sum, "adamw_small")
    for n, (d, nm, nv, *g) in zip([*gsmall, *SMALL_ROW_TENSORS], updates):
        delta[n], new_m[n], new_v[n] = d, nm, nv
        grads.update({n: g[0]} if g else {})

    return (total_loss, grad_x[None], *[grads[n] for n in WEIGHTS], *[delta[n] for n in WEIGHTS],
            *[new_m[n] for n in WEIGHTS], *[new_v[n] for n in WEIGHTS])
```

```python
import functools
import math

import jax
import jax.numpy as jnp
import numpy as np
from jax import lax
from jax.experimental import pallas as pl
from jax.experimental.pallas import tpu as pltpu

F32 = jnp.float32
BF16 = jnp.bfloat16

D_MODEL = 1024
D_RNN = 1024
RNN_BLOCKS = 16
RNN_BLOCK_W = 64
CONV_W = 4
LRU_C = 8.0
N_Q_HEADS = 16
N_KV_HEADS = 4
GROUP = 4
HEAD_DIM = 64
D_KV = 256
BLOCK = 128
ALIBI_MAX_BIAS = 8.0
EPS = 1e-6
D_IN = 6656
N_CHIPS = 4
W_IN_SHARD = D_IN // N_CHIPS
OUT_SHARD = D_MODEL // N_CHIPS
ADAM_LR = 0.001
ADAM_B1 = 0.9
ADAM_B2 = 0.999
ADAM_EPS = 1e-08
ADAM_WD = 0.01
ADAM_STEP = 10
NEG_BIG = -1e30
MIB = 1 << 20

COL_RNN_X = 0
COL_RNN_GATE = 4
COL_Q = 8
COL_K = 12
COL_V = 13
COL_ATTN_GATE = 14
COL_MERGE = 18

RNN_TILE = 256
RNN_CHUNK = 512
SMALL_ROWS = 152
SMALL_VECTOR_ROWS = 24
MESH = pl.DeviceIdType.MESH


def _sds(shape, dtype):
    return pltpu.HBM(shape, dtype)


def _params(sem=None, vmem_mib=None):
    kw = {}
    if sem is not None:
        kw["dimension_semantics"] = sem
    if vmem_mib is not None:
        kw["vmem_limit_bytes"] = vmem_mib * MIB
    return pltpu.CompilerParams(**kw)


def _hbm(*arrays):
    return [pltpu.with_memory_space_constraint(a, pltpu.HBM) for a in arrays]


def _dot(a, b):
    return jnp.dot(a, b, preferred_element_type=F32)


def _dot_nt(a, b):
    return lax.dot_general(a, b, (((1,), (1,)), ((), ())), preferred_element_type=F32)


def _dot_tn(a, b):
    return lax.dot_general(a, b, (((0,), (0,)), ((), ())), preferred_element_type=F32)


def _sigmoid(x):
    return 0.5 * jnp.tanh(0.5 * x) + 0.5


def _sigmoid_small(x):
    return 1.0 / (1.0 + jnp.exp(-x))


def _softplus(x):
    return jnp.maximum(x, 0.0) + jnp.log(1.0 + jnp.exp(-jnp.abs(x)))


def _one_minus_square(a, log_a):
    return -jnp.tanh(log_a) * (a * a + 1.0)


def _proj_fwd(x, g_pre, w_in_g):
    T = x.shape[0]
    tm = min(1024, T)

    def body(x_ref, g_ref, w_ref, proj_ref, ht_ref, h_s):
        @pl.when(pl.program_id(1) == 0)
        def _():
            xv = x_ref[...]
            rstd = lax.rsqrt(jnp.mean(xv * xv, axis=-1, keepdims=True) + EPS)
            hf = (xv * rstd) * g_ref[...]
            h_s[...] = hf.astype(BF16)
            ht_ref[...] = hf.T.astype(BF16)

        proj_ref[...] = _dot(h_s[...], w_ref[...]).astype(BF16)

    return pl.pallas_call(
        body,
        name="proj_fwd",
        grid=(T // tm, N_CHIPS),
        in_specs=[
            pl.BlockSpec((tm, D_MODEL), lambda i, j: (i, 0)),
            pl.BlockSpec((1, D_MODEL), lambda i, j: (0, 0)),
            pl.BlockSpec((None, D_MODEL, W_IN_SHARD), lambda i, j: (j, 0, 0)),
        ],
        out_specs=[
            pl.BlockSpec((tm, W_IN_SHARD), lambda i, j: (i, j)),
            pl.BlockSpec((D_MODEL, tm), lambda i, j: (0, i)),
        ],
        out_shape=[_sds((T, D_IN), BF16), _sds((D_MODEL, T), BF16)],
        scratch_shapes=[pltpu.VMEM((tm, D_MODEL), BF16)],
        compiler_params=_params(("parallel", "arbitrary"), 48),
    )(*_hbm(x, g_pre, w_in_g))


def _shift_down(x, tail, s, row):
    n = x.shape[0]
    xs = pltpu.roll(x, s, 0)
    tail_t = jnp.tile(pltpu.roll(tail, s, 0), (n // 8, 1))
    return jnp.where(row < s, tail_t, xs)


def _shift_up(x, head, s, row):
    n = x.shape[0]
    xs = pltpu.roll(x, n - s, 0)
    head_t = jnp.tile(pltpu.roll(head, 8 - s, 0), (n // 8, 1))
    return jnp.where(row >= n - s, head_t, xs)


def _conv_taps(x, tail, row):
    return [_shift_down(x, tail, 3, row), _shift_down(x, tail, 2, row), _shift_down(x, tail, 1, row), x]


def _rglru_gates(c, wa, wx, ba, bx, lam):
    cb = c.astype(BF16)
    r = _sigmoid_small(_dot(cb, wa) + ba)
    i = _sigmoid(_dot(cb, wx) + bx)
    log_a = (-LRU_C) * r * _softplus(-lam)
    a = jnp.exp(log_a)
    w = _one_minus_square(a, log_a)
    inv_mult = lax.rsqrt(w)
    return cb, r, i, a, w * inv_mult, inv_mult


GATE_BLOCKS_PER_TILE = RNN_TILE // RNN_BLOCK_W
GATE_BLOCKS = pl.BlockSpec((GATE_BLOCKS_PER_TILE, RNN_BLOCK_W, RNN_BLOCK_W), lambda j, t: (j, 0, 0))


def _fill_block_diag(bd_ref, w_ref):
    bd_ref[...] = jnp.zeros_like(bd_ref)
    for a in range(GATE_BLOCKS_PER_TILE):
        lo = a * RNN_BLOCK_W
        bd_ref[lo:lo + RNN_BLOCK_W, lo:lo + RNN_BLOCK_W] = w_ref[a].astype(BF16)


SUBLANES = 8


def _scan_down(a, u, row):
    n = a.shape[0]
    s = 1
    while s < SUBLANES:
        a_sh = jnp.where(row >= s, pltpu.roll(a, s, 0), 1.0)
        u_sh = jnp.where(row >= s, pltpu.roll(u, s, 0), 0.0)
        u = a * u_sh + u
        a = a * a_sh
        s *= 2
    while s < n:
        u = jnp.concatenate([u[:s], a[s:] * u[:n - s] + u[s:]], axis=0)
        a = jnp.concatenate([a[:s], a[s:] * a[:n - s]], axis=0)
        s *= 2
    return a, u


def _scan_up(b, u, row):
    n = b.shape[0]
    s = 1
    while s < SUBLANES:
        b_sh = jnp.where(row < n - s, pltpu.roll(b, n - s, 0), 1.0)
        u_sh = jnp.where(row < n - s, pltpu.roll(u, n - s, 0), 0.0)
        u = b * u_sh + u
        b = b * b_sh
        s *= 2
    while s < n:
        u = jnp.concatenate([b[:n - s] * u[s:] + u[:n - s], u[n - s:]], axis=0)
        b = jnp.concatenate([b[:n - s] * b[s:], b[n - s:]], axis=0)
        s *= 2
    return b, u


LANES = 128


def _chunk_scan(a, u, a_s, u_s, hl_s, al_s, carry, reverse):
    n, width = a.shape
    groups = n // SUBLANES
    order = range(SUBLANES - 1, -1, -1) if reverse else range(SUBLANES)
    row = lax.broadcasted_iota(jnp.int32, (groups, LANES), 0)
    for l in range(width // LANES):
        lanes = slice(l * LANES, (l + 1) * LANES)
        a_l, u_l, hl_l, al_l = a_s.at[l], u_s.at[l], hl_s.at[l], al_s.at[l]
        a_l[...] = a[:, lanes]
        u_l[...] = u[:, lanes]
        h_loc = a_loc = None
        for r in order:
            rows = pl.ds(r, groups, stride=SUBLANES)
            a_r, u_r = a_l[rows, :], u_l[rows, :]
            h_loc, a_loc = (u_r, a_r) if h_loc is None else (a_r * h_loc + u_r, a_r * a_loc)
            hl_l[rows, :] = h_loc
            al_l[rows, :] = a_loc
        if reverse:
            a_cum, ends = _scan_up(a_loc, h_loc, row)
            ends = ends + a_cum * carry[:, lanes]
            enters = jnp.where(row == groups - 1, carry[:, lanes], pltpu.roll(ends, groups - 1, 0))
        else:
            a_cum, ends = _scan_down(a_loc, h_loc, row)
            ends = ends + a_cum * carry[:, lanes]
            enters = jnp.where(row == 0, carry[:, lanes], pltpu.roll(ends, 1, 0))
        for r in range(SUBLANES):
            rows = pl.ds(r, groups, stride=SUBLANES)
            hl_l[rows, :] = hl_l[rows, :] + al_l[rows, :] * enters
    return jnp.concatenate([hl_s[l] for l in range(width // LANES)], axis=1)


def _rnn_fwd(proj, conv_w, conv_b, w_a, w_x, b_a, b_x, lam, token):
    T = proj.shape[0]
    tc, ct = RNN_CHUNK, RNN_TILE
    nt = T // tc

    def body(x_ref, rg_ref, cw_ref, cb_ref, wa_ref, wx_ref, ba_ref, bx_ref, lam_ref, token_ref, h_ref, z_ref, c_ref,
             zt_ref, xtail, hcarry, wa_s, wx_s, a_s, u_s, hl_s, al_s):
        @pl.when(pl.program_id(1) == 0)
        def _():
            xtail[...] = jnp.zeros_like(xtail)
            hcarry[...] = jnp.zeros_like(hcarry)
            _fill_block_diag(wa_s, wa_ref)
            _fill_block_diag(wx_s, wx_ref)

        row = lax.broadcasted_iota(jnp.int32, (tc, ct), 0)
        x = x_ref[...].astype(F32)
        taps = _conv_taps(x, xtail[...], row)
        c = cb_ref[...] + cw_ref[pl.ds(0, 1), :] * taps[0]
        for k in range(1, CONV_W):
            c = c + cw_ref[pl.ds(k, 1), :] * taps[k]
        xtail[...] = x[tc - 8:, :]
        c_ref[...] = c
        _, _, i, a, mult, _ = _rglru_gates(c, wa_s[...], wx_s[...], ba_ref[...], bx_ref[...], lam_ref[...])
        h = _chunk_scan(a, mult * (i * c), a_s, u_s, hl_s, al_s, hcarry[...], reverse=False)
        h_ref[...] = h
        hcarry[...] = h_ref[pl.ds(tc - 1, 1), :]
        rg = rg_ref[...].astype(F32)
        z = h * (rg * _sigmoid(rg))
        z_ref[...] = z.astype(BF16)
        zt_ref[...] = z.T.astype(BF16)

    col = lambda off: (lambda j, t: (t, off + j))
    vec = pl.BlockSpec((1, ct), lambda j, t: (0, j))
    return pl.pallas_call(
        body,
        name="rnn_fwd",
        grid=(D_RNN // ct, nt),
        in_specs=[
            pl.BlockSpec((tc, ct), col(COL_RNN_X)),
            pl.BlockSpec((tc, ct), col(COL_RNN_GATE)),
            pl.BlockSpec((CONV_W, ct), lambda j, t: (0, j)),
            vec, GATE_BLOCKS, GATE_BLOCKS, vec, vec, vec,
            pl.BlockSpec((8, 128), lambda j, t: (0, 0)),
        ],
        out_specs=[pl.BlockSpec((tc, ct), lambda j, t: (t, j))] * 3 + [pl.BlockSpec((ct, tc), lambda j, t: (j, t))],
        out_shape=[_sds((T, D_RNN), F32), _sds((T, D_RNN), BF16), _sds((T, D_RNN), F32), _sds((D_RNN, T), BF16)],
        scratch_shapes=[pltpu.VMEM((8, ct), F32), pltpu.VMEM((1, ct), F32)] + [pltpu.VMEM((ct, ct), BF16)] * 2 + [
            pltpu.VMEM((ct // LANES, tc, LANES), F32)] * 4,
        compiler_params=_params(("parallel", "arbitrary"), 32),
    )(*_hbm(proj, proj, conv_w, conv_b, w_a, w_x, b_a, b_x, lam, token))


def _rnn_bwd(proj, conv, y_rnn, dz_rnn, conv_w, w_a, w_x, b_a, b_x, lam):
    T = proj.shape[0]
    tc, ct = RNN_CHUNK, RNN_TILE
    nt = T // tc
    hb = tc // 8

    def body(x_ref, c_ref, rg_ref, h_ref, hh_ref, dz_ref, cw_ref, wa_ref, wx_ref, ba_ref, bx_ref, lam_ref,
             dx_ref, drg_ref, dwa_ref, dwx_ref, sm_ref, lam_carry, a_carry, dc_head, wa_s, wx_s, dwa_s, dwx_s,
             b_s, dy_s, hl_s, al_s):
        t = pl.program_id(1)
        first_chunk = t == nt - 1

        @pl.when(t == 0)
        def _():
            lam_carry[...] = jnp.zeros_like(lam_carry)
            a_carry[...] = jnp.zeros_like(a_carry)
            dc_head[...] = jnp.zeros_like(dc_head)
            dwa_s[...] = jnp.zeros_like(dwa_s)
            dwx_s[...] = jnp.zeros_like(dwx_s)
            sm_ref[...] = jnp.zeros_like(sm_ref)
            _fill_block_diag(wa_s, wa_ref)
            _fill_block_diag(wx_s, wx_ref)

        row = lax.broadcasted_iota(jnp.int32, (tc, ct), 0)
        keep = jnp.where(first_chunk, 0.0, 1.0)
        x = x_ref[...].astype(F32)
        c = c_ref[...]
        lam = lam_ref[...]
        cb, r, i, a, mult, inv_mult = _rglru_gates(c, wa_s[...], wx_s[...], ba_ref[...], bx_ref[...], lam)
        h = h_ref[...]
        h_prev = _shift_down(h, hh_ref[...] * keep, 1, row)
        rg = rg_ref[...].astype(F32)
        dz = dz_ref[...]
        sg = _sigmoid(rg)
        drg_ref[...] = (dz * h * (sg * (1.0 + rg * (1.0 - sg)))).astype(BF16)
        dy = dz * (rg * sg)
        b = jnp.where(row >= tc - 1, a_carry[pl.ds(0, 1), :], pltpu.roll(a, tc - 1, 0))
        lt = _chunk_scan(b, dy, b_s, dy_s, hl_s, al_s, lam_carry[pl.ds(0, 1), :], reverse=True)
        lam_carry[...] = lt[0:8, :]
        a_carry[...] = a[0:8, :]
        ic = i * c
        dmult = lt * ic
        di = lt * mult * c
        dc = lt * mult * i
        dlog_a = a * (lt * h_prev - dmult * a * inv_mult)
        sp = _softplus(-lam)
        dpre_r = dlog_a * ((-LRU_C) * sp) * (r * (1.0 - r))
        dpre_i = di * (i * (1.0 - i))
        dlam_row = jnp.sum(dlog_a * r, axis=0, keepdims=True) * (LRU_C * _sigmoid(-lam))
        dpr_b = dpre_r.astype(BF16)
        dpi_b = dpre_i.astype(BF16)
        dwa_s[...] += _dot_tn(cb, dpr_b)
        dwx_s[...] += _dot_tn(cb, dpi_b)
        dc = dc + _dot_nt(dpr_b, wa_s[...]) + _dot_nt(dpi_b, wx_s[...])
        head = dc_head[...]
        dx = cw_ref[pl.ds(3, 1), :] * dc
        sm_ref[pl.ds(4 + 3, 1), :] += jnp.sum(dc * x, axis=0, keepdims=True)
        for m in range(1, CONV_W):
            up = _shift_up(dc, head, m, row)
            dx = dx + cw_ref[pl.ds(3 - m, 1), :] * up
            sm_ref[pl.ds(4 + 3 - m, 1), :] += jnp.sum(up * x, axis=0, keepdims=True)
        dx_ref[...] = dx.astype(BF16)
        dc_head[...] = dc[0:8, :]
        sm_ref[pl.ds(0, 1), :] += jnp.sum(dpre_r, axis=0, keepdims=True)
        sm_ref[pl.ds(1, 1), :] += jnp.sum(dpre_i, axis=0, keepdims=True)
        sm_ref[pl.ds(2, 1), :] += dlam_row
        sm_ref[pl.ds(3, 1), :] += jnp.sum(dc, axis=0, keepdims=True)

        @pl.when(first_chunk)
        def _():
            for k in range(GATE_BLOCKS_PER_TILE):
                lo = k * RNN_BLOCK_W
                dwa_ref[k] = dwa_s[lo:lo + RNN_BLOCK_W, lo:lo + RNN_BLOCK_W]
                dwx_ref[k] = dwx_s[lo:lo + RNN_BLOCK_W, lo:lo + RNN_BLOCK_W]

    rev = lambda off: (lambda j, t: (nt - 1 - t, off + j))
    halo = lambda off: (lambda j, t: (jnp.maximum((nt - 1 - t) * hb - 1, 0), off + j))
    vec = pl.BlockSpec((1, ct), lambda j, t: (0, j))
    mat = GATE_BLOCKS
    return pl.pallas_call(
        body,
        name="rnn_bwd",
        grid=(D_RNN // ct, nt),
        in_specs=[
            pl.BlockSpec((tc, ct), rev(COL_RNN_X)),
            pl.BlockSpec((tc, ct), rev(0)),
            pl.BlockSpec((tc, ct), rev(COL_RNN_GATE)),
            pl.BlockSpec((tc, ct), rev(0)),
            pl.BlockSpec((8, ct), halo(0)),
            pl.BlockSpec((tc, ct), rev(0)),
            pl.BlockSpec((CONV_W, ct), lambda j, t: (0, j)),
            mat, mat, vec, vec, vec,
        ],
        out_specs=[
            pl.BlockSpec((tc, ct), rev(0)),
            pl.BlockSpec((tc, ct), rev(0)),
            mat, mat,
            pl.BlockSpec((8, ct), lambda j, t: (0, j)),
        ],
        out_shape=[_sds((T, D_RNN), BF16), _sds((T, D_RNN), BF16), _sds(w_a.shape, F32), _sds(w_x.shape, F32),
                   _sds((8, D_RNN), F32)],
        scratch_shapes=[pltpu.VMEM((8, ct), F32)] * 3 + [pltpu.VMEM((ct, ct), BF16)] * 2 + [
            pltpu.VMEM((ct, ct), F32)] * 2 + [pltpu.VMEM((ct // LANES, tc, LANES), F32)] * 4,
        compiler_params=_params(("parallel", "arbitrary"), 32),
    )(*_hbm(proj, conv, proj, y_rnn, y_rnn, dz_rnn, conv_w, w_a, w_x, b_a, b_x, lam))


def _attn_bias():
    qi = np.arange(BLOCK)[:, None]
    kj = np.arange(BLOCK)[None, :]
    dist_cur = (qi - kj).astype(np.float32)
    slopes = np.float32(2.0) ** (-ALIBI_MAX_BIAS * np.arange(1, N_Q_HEADS + 1, dtype=np.float32) / N_Q_HEADS)
    slopes = slopes[:, None, None]
    prev = np.where(kj > qi, -slopes * (dist_cur + np.float32(BLOCK)), np.float32(NEG_BIG))
    cur = np.where(kj <= qi, -slopes * dist_cur, np.float32(NEG_BIG))
    later = np.concatenate([prev, cur], axis=-1)
    first = np.concatenate([np.full_like(prev, NEG_BIG), cur], axis=-1)
    return jnp.asarray(np.stack([first, later]).astype(np.float32))


def _attn_exps(s_prev, s_cur, sink, bias):
    s_prev = s_prev + bias[:, 0:BLOCK]
    s_cur = s_cur + bias[:, BLOCK:2 * BLOCK]
    m = jnp.maximum(jnp.max(jnp.maximum(s_prev, s_cur), axis=-1, keepdims=True), sink)
    p_prev = jnp.exp(s_prev - m)
    p_cur = jnp.exp(s_cur - m)
    total = jnp.sum(p_prev + p_cur, axis=-1, keepdims=True) + jnp.exp(sink - m)
    return p_prev, p_cur, 1.0 / total, m + jnp.log(total)


def _attn_probs(s_prev, s_cur, sink, bias, lse):
    p_prev = jnp.exp((s_prev + bias[:, 0:BLOCK]) - lse)
    p_cur = jnp.exp((s_cur + bias[:, BLOCK:2 * BLOCK]) - lse)
    return p_prev, p_cur, jnp.exp(sink - lse)


def _stack_heads(ref_or_val, hk, dtype):
    parts = [ref_or_val[:, (GROUP * hk + g) * HEAD_DIM:(GROUP * hk + g + 1) * HEAD_DIM] for g in range(GROUP)]
    return jnp.concatenate(parts, axis=0).astype(dtype)


ATTN_SCALE = HEAD_DIM ** -0.5


def _bias_spec():
    return pl.BlockSpec((None, N_Q_HEADS, BLOCK, 2 * BLOCK), lambda i: (jnp.minimum(i, 1), 0, 0, 0))


def _attn_fwd(proj, sinks, bias):
    T = proj.shape[0]
    nb = T // BLOCK

    def body(sink_ref, bias_ref, q_ref, kp_ref, kc_ref, vp_ref, vc_ref, ag0_ref, ag1_ref, y_ref, z_ref, lse_ref):
        kvs = [slice(hk * HEAD_DIM, (hk + 1) * HEAD_DIM) for hk in range(N_KV_HEADS)]
        qgs = [(_stack_heads(q_ref, hk, F32) * ATTN_SCALE).astype(BF16) for hk in range(N_KV_HEADS)]
        s_prev = [_dot_nt(qgs[hk], kp_ref[:, kvs[hk]].astype(BF16)) for hk in range(N_KV_HEADS)]
        s_cur = [_dot_nt(qgs[hk], kc_ref[:, kvs[hk]].astype(BF16)) for hk in range(N_KV_HEADS)]
        for hk in range(N_KV_HEADS):
            pp, pc, invs = [], [], []
            for g in range(GROUP):
                h = GROUP * hk + g
                rows = slice(g * BLOCK, (g + 1) * BLOCK)
                p_prev, p_cur, inv, lse = _attn_exps(s_prev[hk][rows], s_cur[hk][rows], sink_ref[h], bias_ref[h])
                pp.append(p_prev.astype(BF16))
                pc.append(p_cur.astype(BF16))
                invs.append(inv)
                lse_ref[:, h:h + 1] = lse
            og = _dot(jnp.concatenate(pp, axis=0), vp_ref[:, kvs[hk]].astype(BF16)) + _dot(
                jnp.concatenate(pc, axis=0), vc_ref[:, kvs[hk]].astype(BF16))
            for g in range(GROUP):
                h = GROUP * hk + g
                y_ref[:, h * HEAD_DIM:(h + 1) * HEAD_DIM] = og[g * BLOCK:(g + 1) * BLOCK] * invs[g]
        ag = jnp.concatenate([ag0_ref[...], ag1_ref[...]], axis=1).astype(F32)
        z_ref[...] = (y_ref[...] * (ag * _sigmoid(ag))).astype(BF16)

    prev = lambda c: (lambda i: (jnp.maximum(i - 1, 0), c))
    cur = lambda c: (lambda i: (i, c))
    return pl.pallas_call(
        body,
        name="attn_fwd",
        grid=(nb,),
        in_specs=[
            pl.BlockSpec(memory_space=pltpu.SMEM),
            _bias_spec(),
            pl.BlockSpec((BLOCK, 1024), lambda i: (i, COL_Q // 4)),
            pl.BlockSpec((BLOCK, D_KV), prev(COL_K)),
            pl.BlockSpec((BLOCK, D_KV), cur(COL_K)),
            pl.BlockSpec((BLOCK, D_KV), prev(COL_V)),
            pl.BlockSpec((BLOCK, D_KV), cur(COL_V)),
            pl.BlockSpec((BLOCK, 512), lambda i: (i, COL_ATTN_GATE // 2)),
            pl.BlockSpec((BLOCK, 512), lambda i: (i, COL_ATTN_GATE // 2 + 1)),
        ],
        out_specs=[pl.BlockSpec((BLOCK, 1024), lambda i: (i, 0)), pl.BlockSpec((BLOCK, 1024), lambda i: (i, 0)),
                   pl.BlockSpec((BLOCK, N_Q_HEADS), lambda i: (i, 0))],
        out_shape=[_sds((T, 1024), F32), _sds((T, 1024), BF16), _sds((T, N_Q_HEADS), F32)],
        compiler_params=_params(("arbitrary",), 32),
    )(sinks, *_hbm(bias, proj, proj, proj, proj, proj, proj, proj))


def _attn_bwd(proj, y_attn, lse, dz_attn, sinks, bias, token):
    T = proj.shape[0]
    nb = T // BLOCK

    def body(sink_ref, bias_ref, q_ref, kp_ref, kc_ref, vp_ref, vc_ref, ag0_ref, ag1_ref, y_ref, lse_ref, dz_ref,
             token_ref, dq_ref, dk_ref, dv_ref, dag_ref, ds_ref, dy_s):
        i = pl.program_id(0)

        @pl.when(i == 0)
        def _():
            ds_ref[...] = jnp.zeros_like(ds_ref)

        lane = lax.broadcasted_iota(jnp.int32, (8, 128), 1)
        sub = lax.broadcasted_iota(jnp.int32, (8, 128), 0)
        ag = jnp.concatenate([ag0_ref[...], ag1_ref[...]], axis=1).astype(F32)
        dz = dz_ref[...]
        sg = _sigmoid(ag)
        dag_ref[...] = (dz * y_ref[...] * (sg * (1.0 + ag * (1.0 - sg)))).astype(BF16)
        dy_s[...] = dz * (ag * sg)
        r_cur = pl.multiple_of(i * BLOCK, BLOCK)
        r_prev = pl.multiple_of(jnp.maximum(i - 1, 0) * BLOCK, BLOCK)
        dk_cur, dv_cur, dk_prev, dv_prev = [], [], [], []
        ds_acc = jnp.zeros((8, 128), F32)
        for hk in range(N_KV_HEADS):
            ks = slice(hk * HEAD_DIM, (hk + 1) * HEAD_DIM)
            qg = (_stack_heads(q_ref, hk, F32) * ATTN_SCALE).astype(BF16)
            dog = _stack_heads(dy_s, hk, F32)
            og = _stack_heads(y_ref, hk, F32)
            dog_b = dog.astype(BF16)
            kp = kp_ref[:, ks].astype(BF16)
            kc = kc_ref[:, ks].astype(BF16)
            vp = vp_ref[:, ks].astype(BF16)
            vc = vc_ref[:, ks].astype(BF16)
            s_prev = _dot_nt(qg, kp)
            s_cur = _dot_nt(qg, kc)
            dp_prev = _dot_nt(dog_b, vp)
            dp_cur = _dot_nt(dog_b, vc)
            dvec = jnp.sum(dog * og, axis=-1, keepdims=True)
            pp, pc, dsp, dsc = [], [], [], []
            for g in range(GROUP):
                h = GROUP * hk + g
                rows = slice(g * BLOCK, (g + 1) * BLOCK)
                p_prev, p_cur, p_sink = _attn_probs(
                    s_prev[rows], s_cur[rows], sink_ref[h], bias_ref[h], lse_ref[:, h:h + 1])
                d_h = dvec[rows]
                pp.append(p_prev.astype(BF16))
                pc.append(p_cur.astype(BF16))
                dsp.append((p_prev * (dp_prev[rows] - d_h)).astype(BF16))
                dsc.append((p_cur * (dp_cur[rows] - d_h)).astype(BF16))
                dsink = -jnp.sum(p_sink * d_h, axis=0, keepdims=True)
                ds_acc = ds_acc + jnp.where(jnp.logical_and(lane == h, sub == 1), dsink, 0.0)
            pp = jnp.concatenate(pp, axis=0)
            pc = jnp.concatenate(pc, axis=0)
            dsp = jnp.concatenate(dsp, axis=0)
            dsc = jnp.concatenate(dsc, axis=0)
            dqg = (_dot(dsp, kp) + _dot(dsc, kc)) * ATTN_SCALE
            for g in range(GROUP):
                h = GROUP * hk + g
                dq_ref[:, h * HEAD_DIM:(h + 1) * HEAD_DIM] = dqg[g * BLOCK:(g + 1) * BLOCK].astype(BF16)
            dk_ref[pl.ds(r_cur, BLOCK), ks] = _dot_tn(dsc, qg)
            dv_ref[pl.ds(r_cur, BLOCK), ks] = _dot_tn(pc, dog_b)
            dk_prev.append(_dot_tn(dsp, qg))
            dv_prev.append(_dot_tn(pp, dog_b))
        ds_ref[:, 0:128] += ds_acc

        @pl.when(i > 0)
        def _():
            for hk in range(N_KV_HEADS):
                ks = slice(hk * HEAD_DIM, (hk + 1) * HEAD_DIM)
                dk_ref[pl.ds(r_prev, BLOCK), ks] += dk_prev[hk]
                dv_ref[pl.ds(r_prev, BLOCK), ks] += dv_prev[hk]

    prev = lambda c: (lambda i: (jnp.maximum(i - 1, 0), c))
    cur = lambda c: (lambda i: (i, c))
    blk = pl.BlockSpec((BLOCK, 1024), lambda i: (i, 0))
    whole = pl.BlockSpec((T, D_KV), lambda i: (0, 0))
    return pl.pallas_call(
        body,
        name="attn_bwd",
        grid=(nb,),
        in_specs=[
            pl.BlockSpec(memory_space=pltpu.SMEM),
            _bias_spec(),
            pl.BlockSpec((BLOCK, 1024), lambda i: (i, COL_Q // 4)),
            pl.BlockSpec((BLOCK, D_KV), prev(COL_K)),
            pl.BlockSpec((BLOCK, D_KV), cur(COL_K)),
            pl.BlockSpec((BLOCK, D_KV), prev(COL_V)),
            pl.BlockSpec((BLOCK, D_KV), cur(COL_V)),
            pl.BlockSpec((BLOCK, 512), lambda i: (i, COL_ATTN_GATE // 2)),
            pl.BlockSpec((BLOCK, 512), lambda i: (i, COL_ATTN_GATE // 2 + 1)),
            blk,
            pl.BlockSpec((BLOCK, N_Q_HEADS), lambda i: (i, 0)),
            blk,
            pl.BlockSpec((8, 128), lambda i: (0, 0)),
        ],
        out_specs=[blk, whole, whole, blk, pl.BlockSpec((8, 1024), lambda i: (0, 0))],
        out_shape=[_sds((T, 1024), BF16), _sds((T, D_KV), F32), _sds((T, D_KV), F32), _sds((T, 1024), BF16),
                   _sds((8, 1024), F32)],
        scratch_shapes=[pltpu.VMEM((BLOCK, 1024), F32)],
        compiler_params=_params(("arbitrary",), 48),
    )(sinks, *_hbm(bias, proj, proj, proj, proj, proj, proj, proj, y_attn, lse, dz_attn, token))


def _head(x, target, z_rnn, z_attn, proj, b_gate, g_post, w_rnn_out, w_attn_out, w_out):
    T = x.shape[0]
    tm = 256

    def body(x_ref, t_ref, zr_ref, za_ref, ml0_ref, ml1_ref, ml2_ref, ml3_ref, bg_ref, gp_ref, wr_ref, wa_ref, wo_ref,
             dyx_ref, dzr_ref, dza_ref, dml_ref, dout_ref, dbr_ref, dba_ref, mt_ref, zat_ref, sm_ref):
        @pl.when(pl.program_id(0) == 0)
        def _():
            sm_ref[...] = jnp.zeros_like(sm_ref)

        wr, wa, wo = wr_ref[...], wa_ref[...], wo_ref[...]
        br_rnn = _dot(zr_ref[...], wr)
        br_attn = _dot(za_ref[...], wa)
        zat_ref[...] = za_ref[...].astype(F32).T.astype(BF16)
        ml_rnn = jnp.concatenate([ml0_ref[...], ml1_ref[...]], axis=1).astype(F32)
        ml_attn = jnp.concatenate([ml2_ref[...], ml3_ref[...]], axis=1).astype(F32)
        g_rnn = _sigmoid(ml_rnn + bg_ref[:, 0:D_MODEL])
        g_attn = _sigmoid(ml_attn + bg_ref[:, D_MODEL:2 * D_MODEL])
        merged = g_rnn * br_rnn + g_attn * br_attn
        mb = merged.astype(BF16)
        mt_ref[...] = merged.T.astype(BF16)
        out = _dot(mb, wo)
        rstd = lax.rsqrt(jnp.mean(out * out, axis=-1, keepdims=True) + EPS)
        n = out * rstd
        gp = gp_ref[...]
        err = (x_ref[...] + n * gp) - t_ref[...]
        sm_ref[pl.ds(3, 1), :] += 0.5 * jnp.sum(jnp.mean(err * err, axis=-1, keepdims=True), axis=0, keepdims=True)
        dy = err * (1.0 / D_MODEL)
        dyx_ref[...] = dy
        sm_ref[pl.ds(0, 1), :] += jnp.sum(dy * n, axis=0, keepdims=True)
        dn = dy * gp
        dout = (rstd * (dn - n * jnp.mean(dn * n, axis=-1, keepdims=True))).astype(BF16)
        dout_ref[...] = dout
        dmerged = _dot_nt(dout, wo)
        dml_r = (dmerged * br_rnn) * (g_rnn * (1.0 - g_rnn))
        dml_a = (dmerged * br_attn) * (g_attn * (1.0 - g_attn))
        dml_ref[:, 0:D_MODEL] = dml_r.astype(BF16)
        dml_ref[:, D_MODEL:2 * D_MODEL] = dml_a.astype(BF16)
        sm_ref[pl.ds(1, 1), :] += jnp.sum(dml_r, axis=0, keepdims=True)
        sm_ref[pl.ds(2, 1), :] += jnp.sum(dml_a, axis=0, keepdims=True)
        dbr = (dmerged * g_rnn).astype(BF16)
        dba = (dmerged * g_attn).astype(BF16)
        dbr_ref[...] = dbr
        dba_ref[...] = dba
        dzr_ref[...] = _dot_nt(dbr, wr)
        dza_ref[...] = _dot_nt(dba, wa)

    tile = pl.BlockSpec((tm, D_MODEL), lambda i: (i, 0))
    wspec = pl.BlockSpec((D_MODEL, D_MODEL), lambda i: (0, 0))
    ml = lambda q: pl.BlockSpec((tm, 512), lambda i: (i, COL_MERGE // 2 + q))
    return pl.pallas_call(
        body,
        name="head",
        grid=(T // tm,),
        in_specs=[
            tile, tile, tile, tile,
            ml(0), ml(1), ml(2), ml(3),
            pl.BlockSpec((1, 2 * D_MODEL), lambda i: (0, 0)),
            pl.BlockSpec((1, D_MODEL), lambda i: (0, 0)),
            wspec, wspec, wspec,
        ],
        out_specs=[
            tile, tile, tile,
            pl.BlockSpec((tm, 2 * D_MODEL), lambda i: (i, 0)),
            tile, tile, tile,
            pl.BlockSpec((D_MODEL, tm), lambda i: (0, i)), pl.BlockSpec((D_MODEL, tm), lambda i: (0, i)),
            pl.BlockSpec((8, D_MODEL), lambda i: (0, 0)),
        ],
        out_shape=[
            _sds((T, D_MODEL), F32), _sds((T, D_MODEL), F32), _sds((T, D_MODEL), F32),
            _sds((T, 2 * D_MODEL), BF16),
            _sds((T, D_MODEL), BF16), _sds((T, D_MODEL), BF16), _sds((T, D_MODEL), BF16),
            _sds((D_MODEL, T), BF16), _sds((D_MODEL, T), BF16),
            _sds((8, D_MODEL), F32),
        ],
        compiler_params=_params(("arbitrary",), 56),
    )(*_hbm(x, target, z_rnn, z_attn, proj, proj, proj, proj, b_gate, g_post, w_rnn_out, w_attn_out, w_out))


def _matmul_t(at, b, name):
    M, T = at.shape
    N = b.shape[1]
    tk = min(1024, T)
    nt = T // tk

    def body(a_ref, b_ref, o_ref, ob_ref):
        @pl.when(pl.program_id(0) == 0)
        def _():
            o_ref[...] = jnp.zeros_like(o_ref)

        o_ref[...] += _dot(a_ref[...], b_ref[...])

        @pl.when(pl.program_id(0) == nt - 1)
        def _():
            ob_ref[...] = o_ref[...].astype(BF16)

    whole = pl.BlockSpec((M, N), lambda t: (0, 0))
    return pl.pallas_call(
        body,
        name=name,
        grid=(nt,),
        in_specs=[pl.BlockSpec((M, tk), lambda t: (0, t)), pl.BlockSpec((tk, N), lambda t: (t, 0))],
        out_specs=[whole, whole],
        out_shape=[_sds((M, N), F32), _sds((M, N), BF16)],
        compiler_params=_params(("arbitrary",), 48),
    )(*_hbm(at, b))


DPROJ_WIDTHS = (D_RNN, D_RNN, 1024, D_KV, D_KV, 1024, 2 * D_MODEL)


def _dproj_segments():
    segs, start = [[] for _ in range(N_CHIPS)], 0
    for p, width in enumerate(DPROJ_WIDTHS):
        for c in range(N_CHIPS):
            lo, hi = max(start, c * W_IN_SHARD), min(start + width, (c + 1) * W_IN_SHARD)
            if lo < hi:
                segs[c].append((p, lo - start, hi - start, lo - c * W_IN_SHARD, hi - c * W_IN_SHARD))
        start += width
    return segs


def _dh_bwd(pieces, w_in_g, x, dyx, g_pre, token):
    T = x.shape[0]
    tm = min(512, T)
    n = len(pieces)
    segs = _dproj_segments()

    def body(*refs):
        p_refs, w_hbm, x_ref, dyx_ref, g_ref = refs[0:n], refs[n], refs[n + 1], refs[n + 2], refs[n + 3]
        gx_ref, dg_ref, w_ref, w_sems = refs[n + 5], refs[n + 6], refs[n + 7], refs[n + 8]
        first = pl.program_id(0) == 0
        w_copies = [pltpu.make_async_copy(w_hbm.at[c], w_ref.at[c], w_sems.at[c]) for c in range(N_CHIPS)]

        @pl.when(first)
        def _():
            for cp in w_copies:
                cp.start()
            dg_ref[...] = jnp.zeros_like(dg_ref)

        dh = None
        for c in range(N_CHIPS):
            pl.when(first)(w_copies[c].wait)
            for p, a0, a1, u0, u1 in segs[c]:
                part = _dot_nt(p_refs[p][:, a0:a1].astype(BF16), w_ref[c, :, u0:u1])
                dh = part if dh is None else dh + part
        xv = x_ref[...]
        rstd = lax.rsqrt(jnp.mean(xv * xv, axis=-1, keepdims=True) + EPS)
        nx = xv * rstd
        dhg = dh * g_ref[...]
        gx_ref[...] = dyx_ref[...] + rstd * (dhg - nx * jnp.mean(dhg * nx, axis=-1, keepdims=True))
        dg_ref[pl.ds(0, 1), :] += jnp.sum(dh * nx, axis=0, keepdims=True)

    tile = pl.BlockSpec((tm, D_MODEL), lambda i: (i, 0))
    return pl.pallas_call(
        body,
        name="dh_bwd",
        grid=(T // tm,),
        in_specs=[pl.BlockSpec((tm, w), lambda i: (i, 0)) for w in DPROJ_WIDTHS] + [
            ANY, tile, tile,
            pl.BlockSpec((1, D_MODEL), lambda i: (0, 0)),
            pl.BlockSpec((8, 128), lambda i: (0, 0)),
        ],
        out_specs=[tile, pl.BlockSpec((8, D_MODEL), lambda i: (0, 0))],
        out_shape=[_sds((T, D_MODEL), F32), _sds((8, D_MODEL), F32)],
        scratch_shapes=[pltpu.VMEM(w_in_g.shape, BF16), pltpu.SemaphoreType.DMA((N_CHIPS,))],
        compiler_params=_params(("arbitrary",), 56),
    )(*_hbm(*pieces, w_in_g, x, dyx, g_pre, token))


def _dw_in(ht, pieces):
    T = ht.shape[1]
    tk = min(1024, T)
    nt = T // tk
    n = len(pieces)
    segs = _dproj_segments()

    def body(*refs):
        h_ref, p_refs, o_ref, ob_ref = refs[0], refs[1:n + 1], refs[n + 1], refs[n + 2]

        @pl.when(pl.program_id(1) == 0)
        def _():
            o_ref[...] = jnp.zeros_like(o_ref)

        for c in range(N_CHIPS):
            @pl.when(pl.program_id(0) == c)
            def _():
                for p, a0, a1, u0, u1 in segs[c]:
                    o_ref[:, u0:u1] += _dot(h_ref[...], p_refs[p][:, a0:a1].astype(BF16))

        @pl.when(pl.program_id(1) == nt - 1)
        def _():
            ob_ref[...] = o_ref[...].astype(BF16)

    def piece_spec(p):
        chips = [c for c in range(N_CHIPS) if any(s[0] == p for s in segs[c])]

        def index(c, t):
            used = functools.reduce(jnp.logical_or, [c == k for k in chips])
            return (jnp.where(used, t, 0), 0)

        return pl.BlockSpec((tk, DPROJ_WIDTHS[p]), index)

    return pl.pallas_call(
        body,
        name="dw_in",
        grid=(N_CHIPS, nt),
        in_specs=[pl.BlockSpec((D_MODEL, tk), lambda c, t: (0, t))] + [piece_spec(p) for p in range(n)],
        out_specs=[pl.BlockSpec((None, D_MODEL, W_IN_SHARD), lambda c, t: (c, 0, 0))] * 2,
        out_shape=[_sds((N_CHIPS, D_MODEL, W_IN_SHARD), F32), _sds((N_CHIPS, D_MODEL, W_IN_SHARD), BF16)],
        compiler_params=_params(("parallel", "arbitrary"), 56),
    )(*_hbm(ht, *pieces))


ELEMENTWISE_TILE_BYTES = MIB


def _row_tile(rows, cols, limit=ELEMENTWISE_TILE_BYTES):
    if rows * cols * 4 <= limit:
        return rows
    for t in (512, 256, 128, 64, 32, 16, 8):
        if rows % t == 0 and t * cols * 4 <= limit:
            return t
    return rows


def _group_tiles(groups):
    tiles = [_row_tile(g[0].shape[0], g[0].shape[1] * len(g)) for g in groups]
    steps = max(g[0].shape[0] // t for g, t in zip(groups, tiles))
    return steps, [g[0].shape[0] // steps for g in groups]


def _chip_sum(groups, chip_core, name):
    ps = [p for group_ps, _ in groups for p in group_ps]
    gots = [g for _, group_gots in groups for g in group_gots]
    n = len(ps)
    steps, group_rows = _group_tiles([group_ps for group_ps, _ in groups])
    rows = [tr for (group_ps, _), tr in zip(groups, group_rows) for _ in group_ps]

    def body(jc_ref, *refs):
        for a in range(n):
            p_ref, g0_ref, g1_ref, g2_ref, o_ref = refs[a], refs[n + 3 * a], refs[n + 3 * a + 1], refs[n + 3 * a + 2], \
                refs[4 * n + a]
            o_ref[...] = ((p_ref[...] + g0_ref[...].astype(F32)) + g1_ref[...].astype(F32)) + g2_ref[...].astype(F32)

    tile = lambda p, tr: pl.BlockSpec((tr, p.shape[1]), lambda i, jc_ref: (i, 0))
    rel = lambda p, tr, r: pl.BlockSpec((None, tr, p.shape[1]), lambda i, jc_ref: (r, i, 0))
    half = lambda p, tr: pl.BlockSpec((tr, p.shape[1]), lambda i, jc_ref: (jc_ref[1] * steps + i, 0))
    outs = pl.pallas_call(
        body,
        name=name,
        grid_spec=pltpu.PrefetchScalarGridSpec(
            num_scalar_prefetch=1,
            grid=(steps,),
            in_specs=[tile(p, tr) for p, tr in zip(ps, rows)] + [
                rel(p, tr, r) for p, tr in zip(ps, rows) for r in range(3)],
            out_specs=[half(p, tr) for p, tr in zip(ps, rows)],
        ),
        out_shape=[_sds((2 * p.shape[0], p.shape[1]), F32) for p in ps],
        compiler_params=_params(("parallel",), 48),
    )(chip_core, *_hbm(*ps, *[g for got in gots for g in (got, got, got)]))
    return list(outs)


def _place_shards(shards, chip, name):
    n = len(shards)
    tiles = [_row_tile(s.shape[0], s.shape[1]) for s in shards]
    steps = max(s.shape[0] // t for s, t in zip(shards, tiles))
    tiles = [s.shape[0] // steps for s in shards]

    def body(j_ref, *refs):
        for a in range(n):
            refs[n + a][...] = refs[a][...].astype(BF16)

    return pl.pallas_call(
        body,
        name=name,
        grid_spec=pltpu.PrefetchScalarGridSpec(
            num_scalar_prefetch=1,
            grid=(steps,),
            in_specs=[pl.BlockSpec((t, s.shape[1]), lambda i, j_ref: (i, 0)) for s, t in zip(shards, tiles)],
            out_specs=[pl.BlockSpec((None, t, s.shape[1]), lambda i, j_ref: (j_ref[0], i, 0))
                       for s, t in zip(shards, tiles)],
        ),
        out_shape=[_sds((N_CHIPS,) + s.shape, BF16) for s in shards],
        compiler_params=_params(("parallel",), 48),
    )(chip, *_hbm(*shards))


def _adamw_update(w, g, m, v):
    c1 = 1.0 - ADAM_B1 ** ADAM_STEP
    c2 = 1.0 - ADAM_B2 ** ADAM_STEP
    nm = ADAM_B1 * m + (1.0 - ADAM_B1) * g
    nv = ADAM_B2 * v + (1.0 - ADAM_B2) * (g * g)
    return (-ADAM_LR) * ((nm / c1) / (jnp.sqrt(nv / c2) + ADAM_EPS) + ADAM_WD * w), nm, nv


def _adamw(groups, name):
    params = [p for group in groups for p in group]
    n = len(params)
    steps, group_rows = _group_tiles([[p[0] for p in group] for group in groups])
    rows = [tr for group, tr in zip(groups, group_rows) for _ in group]

    def body(*refs):
        for a in range(n):
            w_ref, g_ref, m_ref, v_ref = refs[4 * a:4 * a + 4]
            d_ref, nm_ref, nv_ref, go_ref = refs[4 * n + 4 * a:4 * n + 4 * a + 4]
            g = g_ref[...]
            d_ref[...], nm_ref[...], nv_ref[...] = _adamw_update(w_ref[...], g, m_ref[...], v_ref[...])
            go_ref[...] = g

    specs = [pl.BlockSpec((tr, p[0].shape[1]), lambda i: (i, 0)) for p, tr in zip(params, rows) for _ in range(4)]
    outs = pl.pallas_call(
        body, name=name, grid=(steps,), in_specs=specs, out_specs=specs,
        out_shape=[_sds(p[0].shape, F32) for p in params for _ in range(4)],
        compiler_params=_params(("parallel",), 48),
    )(*_hbm(*[t for p in params for t in p]))
    return [tuple(outs[4 * a:4 * a + 4]) for a in range(n)]


def _adamw_whole(params, rows_params, packed, name):
    n, k = len(params), len(rows_params)

    def body(*refs):
        ins, packed_ref, outs = refs[0:4 * n + 3 * k], refs[4 * n + 3 * k], refs[4 * n + 3 * k + 1:]
        for a in range(n):
            w_ref, g_ref, m_ref, v_ref = ins[4 * a:4 * a + 4]
            d_ref, nm_ref, nv_ref = outs[3 * a:3 * a + 3]
            d_ref[...], nm_ref[...], nv_ref[...] = _adamw_update(w_ref[...], g_ref[...], m_ref[...], v_ref[...])
        for b, (_, (first, rows, lanes), _, _) in enumerate(rows_params):
            w_ref, m_ref, v_ref = ins[4 * n + 3 * b:4 * n + 3 * b + 3]
            d_ref, nm_ref, nv_ref, g_ref = outs[3 * n + 4 * b:3 * n + 4 * b + 4]
            if len(g_ref.shape) == 3:
                pieces, width = g_ref.shape[1:]
                for q in range(pieces):
                    g_ref[:, q:q + 1, :] = packed_ref[pl.ds(first, 1), q * width:(q + 1) * width][None]
            else:
                for r in range(rows):
                    g_ref[:, r * lanes:(r + 1) * lanes] = packed_ref[pl.ds(first + r, 1), 0:lanes]
            d_ref[...], nm_ref[...], nv_ref[...] = _adamw_update(w_ref[...], g_ref[...], m_ref[...], v_ref[...])

    def whole(t):
        return pl.BlockSpec(t.shape, lambda i: (0,) * t.ndim)

    flat = [t for p in params for t in p] + [t for w, _, m, v in rows_params for t in (w, m, v)] + [packed]
    like = [p[0] for p in params for _ in range(3)] + [p[0] for p in rows_params for _ in range(4)]
    outs = pl.pallas_call(
        body, name=name, grid=(1,), in_specs=[whole(t) for t in flat], out_specs=[whole(t) for t in like],
        out_shape=[_sds(t.shape, F32) for t in like], compiler_params=_params(("arbitrary",), 48),
    )(*_hbm(*flat))
    return [tuple(outs[3 * a:3 * a + 3]) for a in range(n)] + [
        tuple(outs[3 * n + 4 * b:3 * n + 4 * b + 4]) for b in range(k)]


def _place():
    return lax.axis_index("x"), lax.axis_index("y"), lax.axis_index("c")


def _chip_of(x, y, r):
    return (x ^ (r >> 1), y ^ (r & 1))


ANY = pl.BlockSpec(memory_space=pl.ANY)


def _gather_weights(placed, cw8):
    nbig = len(placed)
    halves = [s.shape[1] // 2 for s in placed]
    pieces = [max(1, h // 64) for h in halves]
    rows = [h // p for h, p in zip(halves, pieces)]
    order = [(a, q) for q in range(max(pieces)) for a in range(nbig) if q < pieces[a]]
    ici_sem = {(a, q, r): 3 * i + (r - 1) for i, (a, q) in enumerate(order) for r in (1, 2, 3)}
    cw_sem = {r: 3 * len(order) + (r - 1) for r in (1, 2, 3)}
    d2d_sem = {key: 3 * len(order) + 3 + k for key, k in ici_sem.items()}
    nsem = 6 * len(order) + 3

    def body(*refs):
        cw_ref, dsts, gcw_ref = refs[nbig], refs[nbig + 1:2 * nbig + 1], refs[2 * nbig + 1]
        send_sems, recv_sems = refs[2 * nbig + 2:]
        x, y, c = _place()
        j = 2 * x + y

        def piece_rows(a, q, core):
            return pl.ds(pl.multiple_of(core * halves[a] + q * rows[a], 16), rows[a])

        def ici(a, q, r):
            tx, ty = _chip_of(x, y, r)
            k = ici_sem[(a, q, r)]
            region = dsts[a].at[j, piece_rows(a, q, c), :]
            return pltpu.make_async_remote_copy(
                src_ref=region, dst_ref=region, send_sem=send_sems.at[k], recv_sem=recv_sems.at[k],
                device_id=(tx, ty, c), device_id_type=MESH)

        def ici_landed(a, q, r):
            tx, ty = _chip_of(x, y, r)
            k = ici_sem[(a, q, r)]
            region = dsts[a].at[2 * tx + ty, piece_rows(a, q, c), :]
            return pltpu.make_async_remote_copy(
                src_ref=region, dst_ref=region, send_sem=send_sems.at[k], recv_sem=recv_sems.at[k],
                device_id=(tx, ty, c), device_id_type=MESH)

        def d2d(a, q, r, core):
            tx, ty = _chip_of(x, y, r)
            k = d2d_sem[(a, q, r)]
            region = dsts[a].at[2 * tx + ty, piece_rows(a, q, core), :]
            return pltpu.make_async_remote_copy(
                src_ref=region, dst_ref=region, send_sem=send_sems.at[k], recv_sem=recv_sems.at[k],
                device_id=(x, y, 1 - c), device_id_type=MESH)

        def cw_copy(r):
            tx, ty = _chip_of(x, y, r)
            k = cw_sem[r]
            return pltpu.make_async_remote_copy(
                src_ref=cw_ref, dst_ref=gcw_ref.at[j], send_sem=send_sems.at[k], recv_sem=recv_sems.at[k],
                device_id=(tx, ty, c), device_id_type=MESH)

        def cw_landed(r):
            tx, ty = _chip_of(x, y, r)
            k = cw_sem[r]
            region = gcw_ref.at[2 * tx + ty]
            return pltpu.make_async_remote_copy(
                src_ref=region, dst_ref=region, send_sem=send_sems.at[k], recv_sem=recv_sems.at[k],
                device_id=(tx, ty, c), device_id_type=MESH)

        def relay(a, q, origin, to):
            ox, oy = _chip_of(x, y, origin)
            tx, ty = _chip_of(x, y, to)
            k = ici_sem[(a, q, 3)]
            region = dsts[a].at[2 * ox + oy, piece_rows(a, q, c), :]
            return pltpu.make_async_remote_copy(
                src_ref=region, dst_ref=region, send_sem=send_sems.at[k], recv_sem=recv_sems.at[k],
                device_id=(tx, ty, c), device_id_type=MESH)

        first = [ici(a, q, r) for (a, q) in order for r in (1, 2)] + [cw_copy(r) for r in (1, 2, 3)]
        for cp in first:
            cp.start()
        passed = []
        for (a, q) in order:
            for r in (1, 2):
                ici_landed(a, q, r).wait_recv()
                if q % 2 == r - 1:
                    cp = relay(a, q, r, 3 - r)
                    cp.start()
                    passed.append(cp)
                cp = d2d(a, q, r, c)
                cp.start()
                passed.append(cp)
        for (a, q) in order:
            ici_landed(a, q, 3).wait_recv()
            cp = d2d(a, q, 3, c)
            cp.start()
            passed.append(cp)
        for r in (1, 2, 3):
            cw_landed(r).wait_recv()
        for (a, q) in order:
            for r in (1, 2, 3):
                d2d(a, q, r, 1 - c).wait_recv()
        for cp in first + passed:
            cp.wait_send()

    return pl.pallas_call(
        body,
        name="gather_weights",
        in_specs=[ANY] * (nbig + 1),
        out_specs=[ANY] * (nbig + 1),
        out_shape=[_sds(s.shape, s.dtype) for s in placed] + [_sds((N_CHIPS,) + cw8.shape, cw8.dtype)],
        input_output_aliases={a: a for a in range(nbig)},
        scratch_shapes=[pltpu.SemaphoreType.DMA((nsem,)), pltpu.SemaphoreType.DMA((nsem,))],
    )(*placed, cw8)


def _gather_late_start(placed, after, name):
    n = len(placed)
    halves = [s.shape[1] // 2 for s in placed]

    def body(*refs):
        g_refs = refs[0:n]
        send_sems, recv_sems, token = refs[n + 1], refs[n + 2], refs[-1]
        x, y, c = _place()
        j = 2 * x + y
        for a in range(n):
            mine = g_refs[a].at[j, pl.ds(pl.multiple_of(c * halves[a], 16), halves[a]), :]
            for r in (1, 2, 3):
                tx, ty = _chip_of(x, y, r)
                for to_core in (0, 1):
                    k = ((a * 3 + (r - 1)) * 2 + c) * 2 + to_core
                    pltpu.make_async_remote_copy(
                        src_ref=mine, dst_ref=mine, send_sem=send_sems.at[k], recv_sem=recv_sems.at[k],
                        device_id=(tx, ty, to_core), device_id_type=MESH).start()
        token[...] = jnp.zeros_like(token)

    hbm = lambda t: pltpu.HBM(t.shape, t.dtype)
    keep = lambda t: pltpu.with_memory_space_constraint(t, pltpu.HBM)
    nsem = 12 * n
    outs = pl.pallas_call(
        body,
        name=name,
        in_specs=[HBM] * n + [ANY],
        out_specs=(SEM, SEM, *[HBM] * n, pl.BlockSpec(memory_space=pltpu.VMEM)),
        out_shape=(pltpu.SemaphoreType.DMA((nsem,)), pltpu.SemaphoreType.DMA((nsem,)), *[hbm(p) for p in placed],
                   jax.ShapeDtypeStruct((8, 128), F32)),
        input_output_aliases={i: 2 + i for i in range(n)},
        compiler_params=pltpu.CompilerParams(has_side_effects=DATAFLOW),
    )(*[keep(p) for p in placed], after)
    return outs[0], outs[1], list(outs[2:2 + n]), outs[-1]


def _gather_late_wait(send_sems, recv_sems, thru, after, name):
    n = len(thru)
    halves = [s.shape[1] // 2 for s in thru]

    def body(*refs):
        g_refs = refs[0:n]
        send_sems, recv_sems = refs[n], refs[n + 1]
        x, y, c = _place()
        j = 2 * x + y
        for a in range(n):
            mine = g_refs[a].at[j, pl.ds(pl.multiple_of(c * halves[a], 16), halves[a]), :]
            for r in (1, 2, 3):
                tx, ty = _chip_of(x, y, r)
                for other in (0, 1):
                    k_out = ((a * 3 + (r - 1)) * 2 + c) * 2 + other
                    pltpu.make_async_remote_copy(
                        src_ref=mine, dst_ref=mine, send_sem=send_sems.at[k_out], recv_sem=recv_sems.at[k_out],
                        device_id=(tx, ty, other), device_id_type=MESH).wait_send()
                    k_in = ((a * 3 + (r - 1)) * 2 + other) * 2 + c
                    theirs = g_refs[a].at[2 * tx + ty, pl.ds(other * halves[a], halves[a]), :]
                    pltpu.make_async_remote_copy(
                        src_ref=theirs, dst_ref=theirs, send_sem=send_sems.at[k_in], recv_sem=recv_sems.at[k_in],
                        device_id=(tx, ty, other), device_id_type=MESH).wait_recv()

    hbm = lambda t: pltpu.HBM(t.shape, t.dtype)
    outs = pl.pallas_call(
        body,
        name=name,
        in_specs=[HBM] * n + [SEM, SEM, ANY],
        out_specs=[HBM] * n,
        out_shape=[hbm(t) for t in thru],
        input_output_aliases={i: i for i in range(n)},
        compiler_params=pltpu.CompilerParams(has_side_effects=DATAFLOW),
    )(*thru, send_sems, recv_sems, after)
    return list(outs)


D2D_PIECE_ROWS = 64
PAIR_SUM_TILE_BYTES = 2 * MIB


def _pair_sum(gs, gbs, chip_core, name):
    n = len(gs)
    nch, R, C = gs[0].shape
    h = R // 2
    tr = _row_tile(h, C * n, PAIR_SUM_TILE_BYTES)
    nt = h // tr
    rows = min(D2D_PIECE_ROWS, tr)

    def body(jc_ref, *refs):
        g_refs, gb_refs, p_refs, pb_refs = refs[0:n], refs[n:2 * n], refs[2 * n:3 * n], refs[3 * n:4 * n]
        got_refs, send_sems, recv_sems = refs[4 * n:5 * n], refs[5 * n], refs[5 * n + 1]
        i, j = pl.program_id(0), pl.program_id(1)
        x, y, c = _place()

        def copy(a, ti, tj, first, count):
            src_rows = pl.ds(pl.multiple_of((1 - c) * h + ti * tr + first, 16), count)
            dst_rows = pl.ds(pl.multiple_of(ti * tr + first, 16), count)
            return pltpu.make_async_remote_copy(
                src_ref=gb_refs[a].at[tj, src_rows, :], dst_ref=got_refs[a].at[tj, dst_rows, :],
                send_sem=send_sems.at[a, ti, tj], recv_sem=recv_sems.at[a, ti, tj],
                device_id=(x, y, 1 - c), device_id_type=MESH)

        @pl.when((i == 0) & (j == 0))
        def _():
            for ti in range(nt):
                for tj in range(nch):
                    for a in range(n):
                        for q in range(tr // rows):
                            copy(a, ti, tj, q * rows, rows).start()

        for a in range(n):
            copy(a, i, j, 0, tr).wait()
            s = g_refs[a][...] + got_refs[a][j, pl.ds(pl.multiple_of(i * tr, 16), tr), :].astype(F32)
            pb_refs[a][...] = s.astype(BF16)

            @pl.when(j == jc_ref[0])
            def _():
                p_refs[a][...] = s

    by_chip = pl.BlockSpec((None, tr, C), lambda i, j, jc_ref: (j, i, 0))
    outs = pl.pallas_call(
        body,
        name=name,
        grid_spec=pltpu.PrefetchScalarGridSpec(
            num_scalar_prefetch=1,
            grid=(nt, nch),
            in_specs=[pl.BlockSpec((None, tr, C), lambda i, j, jc_ref: (j, jc_ref[1] * nt + i, 0))] * n + [ANY] * n,
            out_specs=[pl.BlockSpec((tr, C), lambda i, j, jc_ref: (i, 0))] * n + [by_chip] * n,
            scratch_shapes=[pltpu.VMEM((nch, h, C), BF16)] * n + [pltpu.SemaphoreType.DMA((n, nt, nch))] * 2,
        ),
        out_shape=[_sds((h, C), F32)] * n + [_sds((nch, h, C), BF16)] * n,
        compiler_params=_params(("arbitrary", "arbitrary"), 48),
    )(chip_core, *_hbm(*gs, *gbs))
    return list(outs[:n]), list(outs[n:])


HBM = pl.BlockSpec(memory_space=pltpu.HBM)
SEM = pl.BlockSpec(memory_space=pltpu.SEMAPHORE)
DATAFLOW = pltpu.SideEffectType.DATAFLOW_SIDE_EFFECTING


def _chip_copy(p_refs, land_refs, send_sems, recv_sems, a, r, blocked):
    x, y, c = _place()
    tx, ty = _chip_of(x, y, r)
    k = a * 3 + (r - 1)
    return pltpu.make_async_remote_copy(
        src_ref=p_refs[a].at[2 * tx + ty] if blocked else p_refs[a], dst_ref=land_refs[a].at[r - 1],
        send_sem=send_sems.at[k], recv_sem=recv_sems.at[k], device_id=(tx, ty, c), device_id_type=MESH)


def _chip_exchange_start(psums, name, blocked=True):
    n = len(psums)
    lands = [lax.empty((3,) + (p.shape[1:] if blocked else p.shape), p.dtype) for p in psums]

    def body(*refs):
        p_refs, land_refs = refs[0:n], refs[n:2 * n]
        send_sems, recv_sems, token = refs[2 * n], refs[2 * n + 1], refs[-1]
        for a in range(n):
            for r in (1, 2, 3):
                _chip_copy(p_refs, land_refs, send_sems, recv_sems, a, r, blocked).start()
        token[...] = jnp.zeros_like(token)

    hbm = lambda t: pltpu.HBM(t.shape, t.dtype)
    keep = lambda t: pltpu.with_memory_space_constraint(t, pltpu.HBM)
    outs = pl.pallas_call(
        body,
        name=name,
        in_specs=[HBM] * (2 * n),
        out_specs=(SEM, SEM, *[HBM] * (2 * n), pl.BlockSpec(memory_space=pltpu.VMEM)),
        out_shape=(pltpu.SemaphoreType.DMA((3 * n,)), pltpu.SemaphoreType.DMA((3 * n,)),
                   *[hbm(p) for p in psums], *[hbm(l) for l in lands], _sds((8, 128), F32)),
        input_output_aliases={i: 2 + i for i in range(2 * n)},
        compiler_params=pltpu.CompilerParams(has_side_effects=DATAFLOW),
    )(*[keep(p) for p in psums], *[keep(l) for l in lands])
    return outs[0], outs[1], list(outs[2:2 + n]), list(outs[2 + n:2 + 2 * n]), outs[-1]


def _chip_exchange_wait(send_sems, recv_sems, p_thru, land_thru, after, name, blocked=True):
    n = len(p_thru)

    def body(*refs):
        p_refs, land_refs = refs[0:n], refs[n:2 * n]
        send_sems, recv_sems = refs[2 * n], refs[2 * n + 1]
        for a in range(n):
            for r in (1, 2, 3):
                copy = _chip_copy(p_refs, land_refs, send_sems, recv_sems, a, r, blocked)
                copy.wait_send()
                copy.wait_recv()

    hbm = lambda t: pltpu.HBM(t.shape, t.dtype)
    outs = pl.pallas_call(
        body,
        name=name,
        in_specs=[HBM] * (2 * n) + [SEM, SEM, ANY],
        out_specs=[HBM] * (2 * n),
        out_shape=[hbm(p) for p in p_thru] + [hbm(l) for l in land_thru],
        input_output_aliases={i: i for i in range(2 * n)},
        compiler_params=pltpu.CompilerParams(has_side_effects=DATAFLOW),
    )(*p_thru, *land_thru, send_sems, recv_sems, after)
    return list(outs[0:n]), list(outs[n:2 * n])


def _pair_share(fulls):
    n = len(fulls)
    halves = [f.shape[0] // 2 for f in fulls]

    def body(*refs):
        full_refs = refs[n:2 * n]
        send_sems, recv_sems = refs[2 * n:]
        x, y, c = _place()

        def half_of(a, core):
            return full_refs[a].at[pl.ds(pl.multiple_of(core * halves[a], 8), halves[a]), :]

        def remote(a, src, dst):
            return pltpu.make_async_remote_copy(
                src_ref=src, dst_ref=dst, send_sem=send_sems.at[a], recv_sem=recv_sems.at[a],
                device_id=(x, y, 1 - c), device_id_type=MESH)

        for a in range(n):
            for q in range(halves[a] // D2D_PIECE_ROWS):
                piece = full_refs[a].at[
                    pl.ds(pl.multiple_of(c * halves[a] + q * D2D_PIECE_ROWS, 8), D2D_PIECE_ROWS), :]
                remote(a, piece, piece).start()
        for a in range(n):
            remote(a, half_of(a, c), half_of(a, c)).wait_send()
            remote(a, half_of(a, 1 - c), half_of(a, 1 - c)).wait_recv()

    return pl.pallas_call(
        body,
        name="pair_share",
        in_specs=[ANY] * n,
        out_specs=[ANY] * n,
        out_shape=[_sds(f.shape, F32) for f in fulls],
        input_output_aliases={a: a for a in range(n)},
        scratch_shapes=[pltpu.SemaphoreType.DMA((n,)), pltpu.SemaphoreType.DMA((n,))],
    )(*fulls)


def _small_pair_sum(s):
    R, C = s.shape
    V = SMALL_VECTOR_ROWS

    def body(s_ref, v_ref, m_ref, sib, send_sem, recv_sem):
        x, y, c = _place()

        def to_sib(src, dst):
            return pltpu.make_async_remote_copy(
                src_ref=src, dst_ref=dst, send_sem=send_sem, recv_sem=recv_sem,
                device_id=(x, y, 1 - c), device_id_type=MESH)

        for q in range(R // 8):
            to_sib(s_ref.at[pl.ds(8 * q, 8), :], sib.at[pl.ds(8 * q, 8), :]).start()
        to_sib(s_ref, sib).wait()
        v_ref[...] = s_ref[pl.ds(0, V), :] + sib[pl.ds(0, V), :]
        m_ref[...] = (s_ref[pl.ds(V, R - V), :] + sib[pl.ds(V, R - V), :]).astype(BF16)

    return pl.pallas_call(
        body,
        name="small_pair_sum",
        in_specs=[pl.BlockSpec(memory_space=pltpu.VMEM)],
        out_specs=[pl.BlockSpec(memory_space=pltpu.VMEM)] * 2,
        out_shape=[jax.ShapeDtypeStruct((V, C), F32), jax.ShapeDtypeStruct((R - V, C), BF16)],
        scratch_shapes=[pltpu.VMEM((R, C), F32), pltpu.SemaphoreType.DMA, pltpu.SemaphoreType.DMA],
    )(s)


def _small_total(chip, own, landed):
    V, C = own[0].shape
    M = own[1].shape[0]

    def body(j_ref, v_ref, m_ref, lv_ref, lm_ref, o_ref, chips_v, chips_m):
        j = j_ref[0]
        chips_v[j] = v_ref[...]
        chips_m[j] = m_ref[...]
        for r in (1, 2, 3):
            chips_v[j ^ r] = lv_ref[r - 1]
            chips_m[j ^ r] = lm_ref[r - 1]
        o_ref[pl.ds(0, V), :] = (chips_v[0] + chips_v[1]) + (chips_v[2] + chips_v[3])
        o_ref[pl.ds(V, M), :] = (chips_m[0].astype(F32) + chips_m[1].astype(F32)) + (
            chips_m[2].astype(F32) + chips_m[3].astype(F32))

    vmem = pl.BlockSpec(memory_space=pltpu.VMEM)
    return pl.pallas_call(
        body,
        name="small_total",
        in_specs=[pl.BlockSpec(memory_space=pltpu.SMEM), vmem, vmem, vmem, vmem],
        out_specs=vmem,
        out_shape=jax.ShapeDtypeStruct((V + M, C), F32),
        scratch_shapes=[pltpu.VMEM((N_CHIPS, V, C), F32), pltpu.VMEM((N_CHIPS, M, C), BF16)],
    )(chip, own[0], own[1], landed[0], landed[1])


def _local_grads(x, target, g_pre, w_in_g, b_gate, conv_w, conv_b, w_rg_a, b_rg_a, w_rg_x, b_rg_x, lam, sinks,
                 out_weights, fwd_token, g_post, on_out_grads, on_w_in_grad):
    b_a = b_rg_a.reshape(1, D_RNN)
    b_x = b_rg_x.reshape(1, D_RNN)

    proj, ht = _proj_fwd(x, g_pre, w_in_g)
    y_rnn, z_rnn, conv, z_rnn_t = _rnn_fwd(proj, conv_w, conv_b, w_rg_a, w_rg_x, b_a, b_x, lam, fwd_token)
    bias = _attn_bias()
    y_attn, z_attn, lse = _attn_fwd(proj, sinks, bias)
    w_rnn_out, w_attn_out, w_out = out_weights(z_attn)
    dyx, dz_rnn, dz_attn, dml, dout, dbr_rnn, dbr_attn, merged_t, z_attn_t, head_small = _head(
        x, target, z_rnn, z_attn, proj, b_gate, g_post, w_rnn_out, w_attn_out, w_out)
    out_grads = [_matmul_t(z_rnn_t, dbr_rnn, "dw_rnn_out"), _matmul_t(z_attn_t, dbr_attn, "dw_attn_out"),
                 _matmul_t(merged_t, dout, "dw_out")]
    shard_rows = lambda d: d.reshape(N_CHIPS, OUT_SHARD, D_MODEL)
    token = on_out_grads([shard_rows(g) for g, _ in out_grads], [shard_rows(gb) for _, gb in out_grads])
    dq, dk, dv, dag, attn_small = _attn_bwd(proj, y_attn, lse, dz_attn, sinks, bias, token)
    drx, drg, dwa, dwx, rnn_small = _rnn_bwd(proj, conv, y_rnn, dz_rnn, conv_w, w_rg_a, w_rg_x, b_a, b_x, lam)
    dproj = [drx, drg, dq, dk, dv, dag, dml]
    token = on_w_in_grad(*_dw_in(ht, dproj))
    grad_x, dh_small = _dh_bwd(dproj, w_in_g, x, dyx, g_pre, token)
    small = jnp.concatenate([rnn_small, head_small, dh_small + attn_small,
                             dwa.reshape(64, 1024), dwx.reshape(64, 1024)], axis=0)
    return grad_x, small


ROW_LOSS = 11


SMALL_ROW_TENSORS = {"b_rg_a": (0, 1, D_RNN), "b_rg_x": (1, 1, D_RNN), "lru_lambda": (2, 1, D_RNN),
                     "conv_b": (3, 1, D_RNN), "post_norm_g": (8, 1, D_MODEL), "b_gate": (9, 2, D_MODEL),
                     "pre_norm_g": (16, 1, D_MODEL), "attn_sinks": (17, 1, N_Q_HEADS)}


def _unpack_small(s, conv_cols):
    return {
        "conv_w": s[4:8, 0:conv_cols].reshape(1, CONV_W, conv_cols),
        "w_rg_a": s[24:88].reshape(1, 16, 64, 64), "w_rg_x": s[88:152].reshape(1, 16, 64, 64),
    }


WEIGHTS = ["pre_norm_g", "w_in", "b_gate", "conv_w", "conv_b", "w_rg_a", "b_rg_a", "w_rg_x", "b_rg_x", "lru_lambda",
           "attn_sinks", "w_rnn_out", "w_attn_out", "w_out", "post_norm_g"]
BIG = ["w_in", "w_rnn_out", "w_attn_out", "w_out"]


def kernel(x, pre_norm_g, w_in, b_gate, conv_w, conv_b, w_rg_a, b_rg_a, w_rg_x, b_rg_x, lru_lambda, attn_sinks, w_rnn_out, w_attn_out, w_out, post_norm_g, loss_target, m_pre_norm_g, m_w_in, m_b_gate, m_conv_w, m_conv_b, m_w_rg_a, m_b_rg_a, m_w_rg_x, m_b_rg_x, m_lru_lambda, m_attn_sinks, m_w_rnn_out, m_w_attn_out, m_w_out, m_post_norm_g, v_pre_norm_g, v_w_in, v_b_gate, v_conv_w, v_conv_b, v_w_rg_a, v_b_rg_a, v_w_rg_x, v_b_rg_x, v_lru_lambda, v_attn_sinks, v_w_rnn_out, v_w_attn_out, v_w_out, v_post_norm_g):
    w = dict(pre_norm_g=pre_norm_g, w_in=w_in, b_gate=b_gate, conv_w=conv_w, conv_b=conv_b, w_rg_a=w_rg_a,
             b_rg_a=b_rg_a, w_rg_x=w_rg_x, b_rg_x=b_rg_x, lru_lambda=lru_lambda, attn_sinks=attn_sinks,
             w_rnn_out=w_rnn_out, w_attn_out=w_attn_out, w_out=w_out, post_norm_g=post_norm_g)
    m = dict(pre_norm_g=m_pre_norm_g, w_in=m_w_in, b_gate=m_b_gate, conv_w=m_conv_w, conv_b=m_conv_b, w_rg_a=m_w_rg_a,
             b_rg_a=m_b_rg_a, w_rg_x=m_w_rg_x, b_rg_x=m_b_rg_x, lru_lambda=m_lru_lambda, attn_sinks=m_attn_sinks,
             w_rnn_out=m_w_rnn_out, w_attn_out=m_w_attn_out, w_out=m_w_out, post_norm_g=m_post_norm_g)
    v = dict(pre_norm_g=v_pre_norm_g, w_in=v_w_in, b_gate=v_b_gate, conv_w=v_conv_w, conv_b=v_conv_b, w_rg_a=v_w_rg_a,
             b_rg_a=v_b_rg_a, w_rg_x=v_w_rg_x, b_rg_x=v_b_rg_x, lru_lambda=v_lru_lambda, attn_sinks=v_attn_sinks,
             w_rnn_out=v_w_rnn_out, w_attn_out=v_w_attn_out, w_out=v_w_out, post_norm_g=v_post_norm_g)
    chip = 2 * lax.axis_index("x") + lax.axis_index("y")

    chip_idx = chip.astype(jnp.int32).reshape(1)
    chip_core = jnp.stack([chip, lax.axis_index("c")]).astype(jnp.int32)
    cw8 = jnp.pad(conv_w[0], ((0, 8 - CONV_W), (0, 0)))
    placed = _place_shards([w_in[0], w_rnn_out[0], w_attn_out[0], w_out[0]], chip_idx, "place_shards")
    win_g, cw_g = _gather_weights(placed[:1], cw8)
    late_send, late_recv, late_thru, late_token = _gather_late_start(placed[1:], win_g, "gather_late_start")
    cw_g = lax.dynamic_update_slice_in_dim(cw_g, cw8[None], chip, axis=0)
    conv_w_full = jnp.transpose(cw_g[:, 0:CONV_W, :], (1, 0, 2)).reshape(CONV_W, D_RNN)

    started = {}

    def start_reduction(tag, grads, grads_b16):
        psums, psums_b16 = _pair_sum(grads, grads_b16, chip_core, "pair_sum_" + tag)
        send_sems, recv_sems, p_thru, land_thru, token = _chip_exchange_start(psums_b16, "chip_exchange_start_" + tag)
        started[tag] = (psums, send_sems, recv_sems, p_thru, land_thru)
        return token

    def end_reduction(tag, after):
        psums, send_sems, recv_sems, p_thru, land_thru = started[tag]
        _, landed = _chip_exchange_wait(send_sems, recv_sems, p_thru, land_thru, after, "chip_exchange_wait_" + tag)
        return psums, landed

    def out_weights(after):
        gathered = _gather_late_wait(late_send, late_recv, late_thru, after, "gather_late_wait")
        return [g.reshape(D_MODEL, D_MODEL) for g in gathered]

    grad_x, small = _local_grads(
        x[0], loss_target[0], pre_norm_g, win_g, b_gate, conv_w_full, conv_b, w_rg_a[0], b_rg_a[0], w_rg_x[0],
        b_rg_x[0], lru_lambda, attn_sinks[0], out_weights, late_token, post_norm_g,
        on_out_grads=lambda grads, grads_b16: start_reduction("out", grads, grads_b16),
        on_w_in_grad=lambda grad, grad_b16: start_reduction("in", [grad], [grad_b16]))

    small_chip = _small_pair_sum(small)
    small_send, small_recv, small_thru, small_land, small_token = _chip_exchange_start(
        list(small_chip), "small_exchange_start", blocked=False)

    halves = _chip_sum([end_reduction("in", small_token), end_reduction("out", small_token)], chip_core, "chip_sum")
    gbig = dict(zip(BIG, _pair_share(halves)))

    grads, delta, new_m, new_v = {}, {}, {}, {}
    updates = _adamw([[(w[n][0], gbig[n], m[n][0], v[n][0]) for n in names] for names in (BIG[:1], BIG[1:])],
                     "adamw_big")
    for n, (d, nm, nv, g) in zip(BIG, updates):
        grads[n], delta[n], new_m[n], new_v[n] = g[None], d[None], nm[None], nv[None]

    small_own, small_landed = _chip_exchange_wait(small_send, small_recv, small_thru, small_land, delta[BIG[-1]],
                                                  "small_exchange_wait", blocked=False)
    small_sum = _small_total(chip_idx, small_own, small_landed)
    total_loss = small_sum[ROW_LOSS, 0]
    gsmall = _unpack_small(small_sum, D_RNN)
    conv_shard = D_RNN // N_CHIPS
    gsmall["conv_w"] = lax.dynamic_slice_in_dim(gsmall["conv_w"], chip * conv_shard, conv_shard, axis=2)
    for n in gsmall:
        grads[n] = gsmall[n].reshape(w[n].shape)
    updates = _adamw_whole([(w[n], grads[n], m[n], v[n]) for n in gsmall],
                           [(w[n], rows, m[n], v[n]) for n, rows in SMALL_ROW_TENSORS.items()], small_sum, "adamw_small")
    for n, (d, nm, nv, *g) in zip([*gsmall, *SMALL_ROW_TENSORS], updates):
        delta[n], new_m[n], new_v[n] = d, nm, nv
        grads.update({n: g[0]} if g else {})

    return (total_loss, grad_x[None], *[grads[n] for n in WEIGHTS], *[delta[n] for n in WEIGHTS],
            *[new_m[n] for n in WEIGHTS], *[new_v[n] for n in WEIGHTS])
```

```python
import functools
import math

import jax
import jax.numpy as jnp
import numpy as np
from jax import lax
from jax.experimental import pallas as pl
from jax.experimental.pallas import tpu as pltpu

F32 = jnp.float32
BF16 = jnp.bfloat16

D_MODEL = 1024
D_RNN = 1024
RNN_BLOCKS = 16
RNN_BLOCK_W = 64
CONV_W = 4
LRU_C = 8.0
N_Q_HEADS = 16
N_KV_HEADS = 4
GROUP = 4
HEAD_DIM = 64
D_KV = 256
BLOCK = 128
ALIBI_MAX_BIAS = 8.0
EPS = 1e-6
D_IN = 6656
N_CHIPS = 4
W_IN_SHARD = D_IN // N_CHIPS
OUT_SHARD = D_MODEL // N_CHIPS
ADAM_LR = 0.001
ADAM_B1 = 0.9
ADAM_B2 = 0.999
ADAM_EPS = 1e-08
ADAM_WD = 0.01
ADAM_STEP = 10
NEG_BIG = -1e30
MIB = 1 << 20

COL_RNN_X = 0
COL_RNN_GATE = 4
COL_Q = 8
COL_K = 12
COL_V = 13
COL_ATTN_GATE = 14
COL_MERGE = 18

RNN_TILE = 256
RNN_CHUNK = 512
SMALL_ROWS = 152
SMALL_VECTOR_ROWS = 24
MESH = pl.DeviceIdType.MESH


def _sds(shape, dtype):
    return pltpu.HBM(shape, dtype)


def _params(sem=None, vmem_mib=None):
    kw = {}
    if sem is not None:
        kw["dimension_semantics"] = sem
    if vmem_mib is not None:
        kw["vmem_limit_bytes"] = vmem_mib * MIB
    return pltpu.CompilerParams(**kw)


def _hbm(*arrays):
    return [pltpu.with_memory_space_constraint(a, pltpu.HBM) for a in arrays]


def _dot(a, b):
    return jnp.dot(a, b, preferred_element_type=F32)


def _dot_nt(a, b):
    return lax.dot_general(a, b, (((1,), (1,)), ((), ())), preferred_element_type=F32)


def _dot_tn(a, b):
    return lax.dot_general(a, b, (((0,), (0,)), ((), ())), preferred_element_type=F32)


def _sigmoid(x):
    return 0.5 * jnp.tanh(0.5 * x) + 0.5


def _sigmoid_small(x):
    return 1.0 / (1.0 + jnp.exp(-x))


def _softplus(x):
    return jnp.maximum(x, 0.0) + jnp.log(1.0 + jnp.exp(-jnp.abs(x)))


def _one_minus_square(a, log_a):
    return -jnp.tanh(log_a) * (a * a + 1.0)


def _proj_fwd(x, g_pre, w_in_g):
    T = x.shape[0]
    tm = min(1024, T)

    def body(x_ref, g_ref, w_ref, proj_ref, ht_ref, h_s):
        @pl.when(pl.program_id(1) == 0)
        def _():
            xv = x_ref[...]
            rstd = lax.rsqrt(jnp.mean(xv * xv, axis=-1, keepdims=True) + EPS)
            hf = (xv * rstd) * g_ref[...]
            h_s[...] = hf.astype(BF16)
            ht_ref[...] = hf.T.astype(BF16)

        proj_ref[...] = _dot(h_s[...], w_ref[...]).astype(BF16)

    return pl.pallas_call(
        body,
        name="proj_fwd",
        grid=(T // tm, N_CHIPS),
        in_specs=[
            pl.BlockSpec((tm, D_MODEL), lambda i, j: (i, 0)),
            pl.BlockSpec((1, D_MODEL), lambda i, j: (0, 0)),
            pl.BlockSpec((None, D_MODEL, W_IN_SHARD), lambda i, j: (j, 0, 0)),
        ],
        out_specs=[
            pl.BlockSpec((tm, W_IN_SHARD), lambda i, j: (i, j)),
            pl.BlockSpec((D_MODEL, tm), lambda i, j: (0, i)),
        ],
        out_shape=[_sds((T, D_IN), BF16), _sds((D_MODEL, T), BF16)],
        scratch_shapes=[pltpu.VMEM((tm, D_MODEL), BF16)],
        compiler_params=_params(("parallel", "arbitrary"), 48),
    )(*_hbm(x, g_pre, w_in_g))


def _shift_down(x, tail, s, row):
    n = x.shape[0]
    xs = pltpu.roll(x, s, 0)
    tail_t = jnp.tile(pltpu.roll(tail, s, 0), (n // 8, 1))
    return jnp.where(row < s, tail_t, xs)


def _shift_up(x, head, s, row):
    n = x.shape[0]
    xs = pltpu.roll(x, n - s, 0)
    head_t = jnp.tile(pltpu.roll(head, 8 - s, 0), (n // 8, 1))
    return jnp.where(row >= n - s, head_t, xs)


def _conv_taps(x, tail, row):
    return [_shift_down(x, tail, 3, row), _shift_down(x, tail, 2, row), _shift_down(x, tail, 1, row), x]


def _rglru_gates(c, wa, wx, ba, bx, lam):
    cb = c.astype(BF16)
    r = _sigmoid_small(_dot(cb, wa) + ba)
    i = _sigmoid(_dot(cb, wx) + bx)
    log_a = (-LRU_C) * r * _softplus(-lam)
    a = jnp.exp(log_a)
    w = _one_minus_square(a, log_a)
    inv_mult = lax.rsqrt(w)
    return cb, r, i, a, w * inv_mult, inv_mult


GATE_BLOCKS_PER_TILE = RNN_TILE // RNN_BLOCK_W
GATE_BLOCKS = pl.BlockSpec((GATE_BLOCKS_PER_TILE, RNN_BLOCK_W, RNN_BLOCK_W), lambda j, t: (j, 0, 0))


def _fill_block_diag(bd_ref, w_ref):
    bd_ref[...] = jnp.zeros_like(bd_ref)
    for a in range(GATE_BLOCKS_PER_TILE):
        lo = a * RNN_BLOCK_W
        bd_ref[lo:lo + RNN_BLOCK_W, lo:lo + RNN_BLOCK_W] = w_ref[a].astype(BF16)


SUBLANES = 8


def _scan_down(a, u, row):
    n = a.shape[0]
    s = 1
    while s < SUBLANES:
        a_sh = jnp.where(row >= s, pltpu.roll(a, s, 0), 1.0)
        u_sh = jnp.where(row >= s, pltpu.roll(u, s, 0), 0.0)
        u = a * u_sh + u
        a = a * a_sh
        s *= 2
    while s < n:
        u = jnp.concatenate([u[:s], a[s:] * u[:n - s] + u[s:]], axis=0)
        a = jnp.concatenate([a[:s], a[s:] * a[:n - s]], axis=0)
        s *= 2
    return a, u


def _scan_up(b, u, row):
    n = b.shape[0]
    s = 1
    while s < SUBLANES:
        b_sh = jnp.where(row < n - s, pltpu.roll(b, n - s, 0), 1.0)
        u_sh = jnp.where(row < n - s, pltpu.roll(u, n - s, 0), 0.0)
        u = b * u_sh + u
        b = b * b_sh
        s *= 2
    while s < n:
        u = jnp.concatenate([b[:n - s] * u[s:] + u[:n - s], u[n - s:]], axis=0)
        b = jnp.concatenate([b[:n - s] * b[s:], b[n - s:]], axis=0)
        s *= 2
    return b, u


LANES = 128


def _chunk_scan(a, u, a_s, u_s, hl_s, al_s, carry, reverse):
    n, width = a.shape
    groups = n // SUBLANES
    order = range(SUBLANES - 1, -1, -1) if reverse else range(SUBLANES)
    row = lax.broadcasted_iota(jnp.int32, (groups, LANES), 0)
    for l in range(width // LANES):
        lanes = slice(l * LANES, (l + 1) * LANES)
        a_l, u_l, hl_l, al_l = a_s.at[l], u_s.at[l], hl_s.at[l], al_s.at[l]
        a_l[...] = a[:, lanes]
        u_l[...] = u[:, lanes]
        h_loc = a_loc = None
        for r in order:
            rows = pl.ds(r, groups, stride=SUBLANES)
            a_r, u_r = a_l[rows, :], u_l[rows, :]
            h_loc, a_loc = (u_r, a_r) if h_loc is None else (a_r * h_loc + u_r, a_r * a_loc)
            hl_l[rows, :] = h_loc
            al_l[rows, :] = a_loc
        if reverse:
            a_cum, ends = _scan_up(a_loc, h_loc, row)
            ends = ends + a_cum * carry[:, lanes]
            enters = jnp.where(row == groups - 1, carry[:, lanes], pltpu.roll(ends, groups - 1, 0))
        else:
            a_cum, ends = _scan_down(a_loc, h_loc, row)
            ends = ends + a_cum * carry[:, lanes]
            enters = jnp.where(row == 0, carry[:, lanes], pltpu.roll(ends, 1, 0))
        for r in range(SUBLANES):
            rows = pl.ds(r, groups, stride=SUBLANES)
            hl_l[rows, :] = hl_l[rows, :] + al_l[rows, :] * enters
    return jnp.concatenate([hl_s[l] for l in range(width // LANES)], axis=1)


def _rnn_fwd(proj, conv_w, conv_b, w_a, w_x, b_a, b_x, lam, token):
    T = proj.shape[0]
    tc, ct = RNN_CHUNK, RNN_TILE
    nt = T // tc

    def body(x_ref, rg_ref, cw_ref, cb_ref, wa_ref, wx_ref, ba_ref, bx_ref, lam_ref, token_ref, h_ref, z_ref, c_ref,
             zt_ref, xtail, hcarry, wa_s, wx_s, a_s, u_s, hl_s, al_s):
        @pl.when(pl.program_id(1) == 0)
        def _():
            xtail[...] = jnp.zeros_like(xtail)
            hcarry[...] = jnp.zeros_like(hcarry)
            _fill_block_diag(wa_s, wa_ref)
            _fill_block_diag(wx_s, wx_ref)

        row = lax.broadcasted_iota(jnp.int32, (tc, ct), 0)
        x = x_ref[...].astype(F32)
        taps = _conv_taps(x, xtail[...], row)
        c = cb_ref[...] + cw_ref[pl.ds(0, 1), :] * taps[0]
        for k in range(1, CONV_W):
            c = c + cw_ref[pl.ds(k, 1), :] * taps[k]
        xtail[...] = x[tc - 8:, :]
        c_ref[...] = c
        _, _, i, a, mult, _ = _rglru_gates(c, wa_s[...], wx_s[...], ba_ref[...], bx_ref[...], lam_ref[...])
        h = _chunk_scan(a, mult * (i * c), a_s, u_s, hl_s, al_s, hcarry[...], reverse=False)
        h_ref[...] = h
        hcarry[...] = h_ref[pl.ds(tc - 1, 1), :]
        rg = rg_ref[...].astype(F32)
        z = h * (rg * _sigmoid(rg))
        z_ref[...] = z.astype(BF16)
        zt_ref[...] = z.T.astype(BF16)

    col = lambda off: (lambda j, t: (t, off + j))
    vec = pl.BlockSpec((1, ct), lambda j, t: (0, j))
    return pl.pallas_call(
        body,
        name="rnn_fwd",
        grid=(D_RNN // ct, nt),
        in_specs=[
            pl.BlockSpec((tc, ct), col(COL_RNN_X)),
            pl.BlockSpec((tc, ct), col(COL_RNN_GATE)),
            pl.BlockSpec((CONV_W, ct), lambda j, t: (0, j)),
            vec, GATE_BLOCKS, GATE_BLOCKS, vec, vec, vec,
            pl.BlockSpec((8, 128), lambda j, t: (0, 0)),
        ],
        out_specs=[pl.BlockSpec((tc, ct), lambda j, t: (t, j))] * 3 + [pl.BlockSpec((ct, tc), lambda j, t: (j, t))],
        out_shape=[_sds((T, D_RNN), F32), _sds((T, D_RNN), BF16), _sds((T, D_RNN), F32), _sds((D_RNN, T), BF16)],
        scratch_shapes=[pltpu.VMEM((8, ct), F32), pltpu.VMEM((1, ct), F32)] + [pltpu.VMEM((ct, ct), BF16)] * 2 + [
            pltpu.VMEM((ct // LANES, tc, LANES), F32)] * 4,
        compiler_params=_params(("parallel", "arbitrary"), 32),
    )(*_hbm(proj, proj, conv_w, conv_b, w_a, w_x, b_a, b_x, lam, token))


def _rnn_bwd(proj, conv, y_rnn, dz_rnn, conv_w, w_a, w_x, b_a, b_x, lam):
    T = proj.shape[0]
    tc, ct = RNN_CHUNK, RNN_TILE
    nt = T // tc
    hb = tc // 8

    def body(x_ref, c_ref, rg_ref, h_ref, hh_ref, dz_ref, cw_ref, wa_ref, wx_ref, ba_ref, bx_ref, lam_ref,
             dx_ref, drg_ref, dwa_ref, dwx_ref, sm_ref, lam_carry, a_carry, dc_head, wa_s, wx_s, dwa_s, dwx_s,
             b_s, dy_s, hl_s, al_s):
        t = pl.program_id(1)
        first_chunk = t == nt - 1

        @pl.when(t == 0)
        def _():
            lam_carry[...] = jnp.zeros_like(lam_carry)
            a_carry[...] = jnp.zeros_like(a_carry)
            dc_head[...] = jnp.zeros_like(dc_head)
            dwa_s[...] = jnp.zeros_like(dwa_s)
            dwx_s[...] = jnp.zeros_like(dwx_s)
            sm_ref[...] = jnp.zeros_like(sm_ref)
            _fill_block_diag(wa_s, wa_ref)
            _fill_block_diag(wx_s, wx_ref)

        row = lax.broadcasted_iota(jnp.int32, (tc, ct), 0)
        keep = jnp.where(first_chunk, 0.0, 1.0)
        x = x_ref[...].astype(F32)
        c = c_ref[...]
        lam = lam_ref[...]
        cb, r, i, a, mult, inv_mult = _rglru_gates(c, wa_s[...], wx_s[...], ba_ref[...], bx_ref[...], lam)
        h = h_ref[...]
        h_prev = _shift_down(h, hh_ref[...] * keep, 1, row)
        rg = rg_ref[...].astype(F32)
        dz = dz_ref[...]
        sg = _sigmoid(rg)
        drg_ref[...] = (dz * h * (sg * (1.0 + rg * (1.0 - sg)))).astype(BF16)
        dy = dz * (rg * sg)
        b = jnp.where(row >= tc - 1, a_carry[pl.ds(0, 1), :], pltpu.roll(a, tc - 1, 0))
        lt = _chunk_scan(b, dy, b_s, dy_s, hl_s, al_s, lam_carry[pl.ds(0, 1), :], reverse=True)
        lam_carry[...] = lt[0:8, :]
        a_carry[...] = a[0:8, :]
        ic = i * c
        dmult = lt * ic
        di = lt * mult * c
        dc = lt * mult * i
        dlog_a = a * (lt * h_prev - dmult * a * inv_mult)
        sp = _softplus(-lam)
        dpre_r = dlog_a * ((-LRU_C) * sp) * (r * (1.0 - r))
        dpre_i = di * (i * (1.0 - i))
        dlam_row = jnp.sum(dlog_a * r, axis=0, keepdims=True) * (LRU_C * _sigmoid(-lam))
        dpr_b = dpre_r.astype(BF16)
        dpi_b = dpre_i.astype(BF16)
        dwa_s[...] += _dot_tn(cb, dpr_b)
        dwx_s[...] += _dot_tn(cb, dpi_b)
        dc = dc + _dot_nt(dpr_b, wa_s[...]) + _dot_nt(dpi_b, wx_s[...])
        head = dc_head[...]
        dx = cw_ref[pl.ds(3, 1), :] * dc
        sm_ref[pl.ds(4 + 3, 1), :] += jnp.sum(dc * x, axis=0, keepdims=True)
        for m in range(1, CONV_W):
            up = _shift_up(dc, head, m, row)
            dx = dx + cw_ref[pl.ds(3 - m, 1), :] * up
            sm_ref[pl.ds(4 + 3 - m, 1), :] += jnp.sum(up * x, axis=0, keepdims=True)
        dx_ref[...] = dx.astype(BF16)
        dc_head[...] = dc[0:8, :]
        sm_ref[pl.ds(0, 1), :] += jnp.sum(dpre_r, axis=0, keepdims=True)
        sm_ref[pl.ds(1, 1), :] += jnp.sum(dpre_i, axis=0, keepdims=True)
        sm_ref[pl.ds(2, 1), :] += dlam_row
        sm_ref[pl.ds(3, 1), :] += jnp.sum(dc, axis=0, keepdims=True)

        @pl.when(first_chunk)
        def _():
            for k in range(GATE_BLOCKS_PER_TILE):
                lo = k * RNN_BLOCK_W
                dwa_ref[k] = dwa_s[lo:lo + RNN_BLOCK_W, lo:lo + RNN_BLOCK_W]
                dwx_ref[k] = dwx_s[lo:lo + RNN_BLOCK_W, lo:lo + RNN_BLOCK_W]

    rev = lambda off: (lambda j, t: (nt - 1 - t, off + j))
    halo = lambda off: (lambda j, t: (jnp.maximum((nt - 1 - t) * hb - 1, 0), off + j))
    vec = pl.BlockSpec((1, ct), lambda j, t: (0, j))
    mat = GATE_BLOCKS
    return pl.pallas_call(
        body,
        name="rnn_bwd",
        grid=(D_RNN // ct, nt),
        in_specs=[
            pl.BlockSpec((tc, ct), rev(COL_RNN_X)),
            pl.BlockSpec((tc, ct), rev(0)),
            pl.BlockSpec((tc, ct), rev(COL_RNN_GATE)),
            pl.BlockSpec((tc, ct), rev(0)),
            pl.BlockSpec((8, ct), halo(0)),
            pl.BlockSpec((tc, ct), rev(0)),
            pl.BlockSpec((CONV_W, ct), lambda j, t: (0, j)),
            mat, mat, vec, vec, vec,
        ],
        out_specs=[
            pl.BlockSpec((tc, ct), rev(0)),
            pl.BlockSpec((tc, ct), rev(0)),
            mat, mat,
            pl.BlockSpec((8, ct), lambda j, t: (0, j)),
        ],
        out_shape=[_sds((T, D_RNN), BF16), _sds((T, D_RNN), BF16), _sds(w_a.shape, F32), _sds(w_x.shape, F32),
                   _sds((8, D_RNN), F32)],
        scratch_shapes=[pltpu.VMEM((8, ct), F32)] * 3 + [pltpu.VMEM((ct, ct), BF16)] * 2 + [
            pltpu.VMEM((ct, ct), F32)] * 2 + [pltpu.VMEM((ct // LANES, tc, LANES), F32)] * 4,
        compiler_params=_params(("parallel", "arbitrary"), 32),
    )(*_hbm(proj, conv, proj, y_rnn, y_rnn, dz_rnn, conv_w, w_a, w_x, b_a, b_x, lam))


def _attn_bias():
    qi = np.arange(BLOCK)[:, None]
    kj = np.arange(BLOCK)[None, :]
    dist_cur = (qi - kj).astype(np.float32)
    slopes = np.float32(2.0) ** (-ALIBI_MAX_BIAS * np.arange(1, N_Q_HEADS + 1, dtype=np.float32) / N_Q_HEADS)
    slopes = slopes[:, None, None]
    prev = np.where(kj > qi, -slopes * (dist_cur + np.float32(BLOCK)), np.float32(NEG_BIG))
    cur = np.where(kj <= qi, -slopes * dist_cur, np.float32(NEG_BIG))
    later = np.concatenate([prev, cur], axis=-1)
    first = np.concatenate([np.full_like(prev, NEG_BIG), cur], axis=-1)
    return jnp.asarray(np.stack([first, later]).astype(np.float32))


def _attn_exps(s_prev, s_cur, sink, bias):
    s_prev = s_prev + bias[:, 0:BLOCK]
    s_cur = s_cur + bias[:, BLOCK:2 * BLOCK]
    m = jnp.maximum(jnp.max(jnp.maximum(s_prev, s_cur), axis=-1, keepdims=True), sink)
    p_prev = jnp.exp(s_prev - m)
    p_cur = jnp.exp(s_cur - m)
    total = jnp.sum(p_prev + p_cur, axis=-1, keepdims=True) + jnp.exp(sink - m)
    return p_prev, p_cur, 1.0 / total, m + jnp.log(total)


def _attn_probs(s_prev, s_cur, sink, bias, lse):
    p_prev = jnp.exp((s_prev + bias[:, 0:BLOCK]) - lse)
    p_cur = jnp.exp((s_cur + bias[:, BLOCK:2 * BLOCK]) - lse)
    return p_prev, p_cur, jnp.exp(sink - lse)


def _stack_heads(ref_or_val, hk, dtype):
    parts = [ref_or_val[:, (GROUP * hk + g) * HEAD_DIM:(GROUP * hk + g + 1) * HEAD_DIM] for g in range(GROUP)]
    return jnp.concatenate(parts, axis=0).astype(dtype)


ATTN_SCALE = HEAD_DIM ** -0.5


def _bias_spec():
    return pl.BlockSpec((None, N_Q_HEADS, BLOCK, 2 * BLOCK), lambda i: (jnp.minimum(i, 1), 0, 0, 0))


def _attn_fwd(proj, sinks, bias):
    T = proj.shape[0]
    nb = T // BLOCK

    def body(sink_ref, bias_ref, q_ref, kp_ref, kc_ref, vp_ref, vc_ref, ag0_ref, ag1_ref, y_ref, z_ref, lse_ref):
        kvs = [slice(hk * HEAD_DIM, (hk + 1) * HEAD_DIM) for hk in range(N_KV_HEADS)]
        qgs = [(_stack_heads(q_ref, hk, F32) * ATTN_SCALE).astype(BF16) for hk in range(N_KV_HEADS)]
        s_prev = [_dot_nt(qgs[hk], kp_ref[:, kvs[hk]].astype(BF16)) for hk in range(N_KV_HEADS)]
        s_cur = [_dot_nt(qgs[hk], kc_ref[:, kvs[hk]].astype(BF16)) for hk in range(N_KV_HEADS)]
        for hk in range(N_KV_HEADS):
            pp, pc, invs = [], [], []
            for g in range(GROUP):
                h = GROUP * hk + g
                rows = slice(g * BLOCK, (g + 1) * BLOCK)
                p_prev, p_cur, inv, lse = _attn_exps(s_prev[hk][rows], s_cur[hk][rows], sink_ref[h], bias_ref[h])
                pp.append(p_prev.astype(BF16))
                pc.append(p_cur.astype(BF16))
                invs.append(inv)
                lse_ref[:, h:h + 1] = lse
            og = _dot(jnp.concatenate(pp, axis=0), vp_ref[:, kvs[hk]].astype(BF16)) + _dot(
                jnp.concatenate(pc, axis=0), vc_ref[:, kvs[hk]].astype(BF16))
            for g in range(GROUP):
                h = GROUP * hk + g
                y_ref[:, h * HEAD_DIM:(h + 1) * HEAD_DIM] = og[g * BLOCK:(g + 1) * BLOCK] * invs[g]
        ag = jnp.concatenate([ag0_ref[...], ag1_ref[...]], axis=1).astype(F32)
        z_ref[...] = (y_ref[...] * (ag * _sigmoid(ag))).astype(BF16)

    prev = lambda c: (lambda i: (jnp.maximum(i - 1, 0), c))
    cur = lambda c: (lambda i: (i, c))
    return pl.pallas_call(
        body,
        name="attn_fwd",
        grid=(nb,),
        in_specs=[
            pl.BlockSpec(memory_space=pltpu.SMEM),
            _bias_spec(),
            pl.BlockSpec((BLOCK, 1024), lambda i: (i, COL_Q // 4)),
            pl.BlockSpec((BLOCK, D_KV), prev(COL_K)),
            pl.BlockSpec((BLOCK, D_KV), cur(COL_K)),
            pl.BlockSpec((BLOCK, D_KV), prev(COL_V)),
            pl.BlockSpec((BLOCK, D_KV), cur(COL_V)),
            pl.BlockSpec((BLOCK, 512), lambda i: (i, COL_ATTN_GATE // 2)),
            pl.BlockSpec((BLOCK, 512), lambda i: (i, COL_ATTN_GATE // 2 + 1)),
        ],
        out_specs=[pl.BlockSpec((BLOCK, 1024), lambda i: (i, 0)), pl.BlockSpec((BLOCK, 1024), lambda i: (i, 0)),
                   pl.BlockSpec((BLOCK, N_Q_HEADS), lambda i: (i, 0))],
        out_shape=[_sds((T, 1024), F32), _sds((T, 1024), BF16), _sds((T, N_Q_HEADS), F32)],
        compiler_params=_params(("arbitrary",), 32),
    )(sinks, *_hbm(bias, proj, proj, proj, proj, proj, proj, proj))


def _attn_bwd(proj, y_attn, lse, dz_attn, sinks, bias, token):
    T = proj.shape[0]
    nb = T // BLOCK

    def body(sink_ref, bias_ref, q_ref, kp_ref, kc_ref, vp_ref, vc_ref, ag0_ref, ag1_ref, y_ref, lse_ref, dz_ref,
             token_ref, dq_ref, dk_ref, dv_ref, dag_ref, ds_ref, dy_s):
        i = pl.program_id(0)

        @pl.when(i == 0)
        def _():
            ds_ref[...] = jnp.zeros_like(ds_ref)

        lane = lax.broadcasted_iota(jnp.int32, (8, 128), 1)
        sub = lax.broadcasted_iota(jnp.int32, (8, 128), 0)
        ag = jnp.concatenate([ag0_ref[...], ag1_ref[...]], axis=1).astype(F32)
        dz = dz_ref[...]
        sg = _sigmoid(ag)
        dag_ref[...] = (dz * y_ref[...] * (sg * (1.0 + ag * (1.0 - sg)))).astype(BF16)
        dy_s[...] = dz * (ag * sg)
        r_cur = pl.multiple_of(i * BLOCK, BLOCK)
        r_prev = pl.multiple_of(jnp.maximum(i - 1, 0) * BLOCK, BLOCK)
        dk_cur, dv_cur, dk_prev, dv_prev = [], [], [], []
        ds_acc = jnp.zeros((8, 128), F32)
        for hk in range(N_KV_HEADS):
            ks = slice(hk * HEAD_DIM, (hk + 1) * HEAD_DIM)
            qg = (_stack_heads(q_ref, hk, F32) * ATTN_SCALE).astype(BF16)
            dog = _stack_heads(dy_s, hk, F32)
            og = _stack_heads(y_ref, hk, F32)
            dog_b = dog.astype(BF16)
            kp = kp_ref[:, ks].astype(BF16)
            kc = kc_ref[:, ks].astype(BF16)
            vp = vp_ref[:, ks].astype(BF16)
            vc = vc_ref[:, ks].astype(BF16)
            s_prev = _dot_nt(qg, kp)
            s_cur = _dot_nt(qg, kc)
            dp_prev = _dot_nt(dog_b, vp)
            dp_cur = _dot_nt(dog_b, vc)
            dvec = jnp.sum(dog * og, axis=-1, keepdims=True)
            pp, pc, dsp, dsc = [], [], [], []
            for g in range(GROUP):
                h = GROUP * hk + g
                rows = slice(g * BLOCK, (g + 1) * BLOCK)
                p_prev, p_cur, p_sink = _attn_probs(
                    s_prev[rows], s_cur[rows], sink_ref[h], bias_ref[h], lse_ref[:, h:h + 1])
                d_h = dvec[rows]
                pp.append(p_prev.astype(BF16))
                pc.append(p_cur.astype(BF16))
                dsp.append((p_prev * (dp_prev[rows] - d_h)).astype(BF16))
                dsc.append((p_cur * (dp_cur[rows] - d_h)).astype(BF16))
                dsink = -jnp.sum(p_sink * d_h, axis=0, keepdims=True)
                ds_acc = ds_acc + jnp.where(jnp.logical_and(lane == h, sub == 1), dsink, 0.0)
            pp = jnp.concatenate(pp, axis=0)
            pc = jnp.concatenate(pc, axis=0)
            dsp = jnp.concatenate(dsp, axis=0)
            dsc = jnp.concatenate(dsc, axis=0)
            dqg = (_dot(dsp, kp) + _dot(dsc, kc)) * ATTN_SCALE
            for g in range(GROUP):
                h = GROUP * hk + g
                dq_ref[:, h * HEAD_DIM:(h + 1) * HEAD_DIM] = dqg[g * BLOCK:(g + 1) * BLOCK].astype(BF16)
            dk_ref[pl.ds(r_cur, BLOCK), ks] = _dot_tn(dsc, qg)
            dv_ref[pl.ds(r_cur, BLOCK), ks] = _dot_tn(pc, dog_b)
            dk_prev.append(_dot_tn(dsp, qg))
            dv_prev.append(_dot_tn(pp, dog_b))
        ds_ref[:, 0:128] += ds_acc

        @pl.when(i > 0)
        def _():
            for hk in range(N_KV_HEADS):
                ks = slice(hk * HEAD_DIM, (hk + 1) * HEAD_DIM)
                dk_ref[pl.ds(r_prev, BLOCK), ks] += dk_prev[hk]
                dv_ref[pl.ds(r_prev, BLOCK), ks] += dv_prev[hk]

    prev = lambda c: (lambda i: (jnp.maximum(i - 1, 0), c))
    cur = lambda c: (lambda i: (i, c))
    blk = pl.BlockSpec((BLOCK, 1024), lambda i: (i, 0))
    whole = pl.BlockSpec((T, D_KV), lambda i: (0, 0))
    return pl.pallas_call(
        body,
        name="attn_bwd",
        grid=(nb,),
        in_specs=[
            pl.BlockSpec(memory_space=pltpu.SMEM),
            _bias_spec(),
            pl.BlockSpec((BLOCK, 1024), lambda i: (i, COL_Q // 4)),
            pl.BlockSpec((BLOCK, D_KV), prev(COL_K)),
            pl.BlockSpec((BLOCK, D_KV), cur(COL_K)),
            pl.BlockSpec((BLOCK, D_KV), prev(COL_V)),
            pl.BlockSpec((BLOCK, D_KV), cur(COL_V)),
            pl.BlockSpec((BLOCK, 512), lambda i: (i, COL_ATTN_GATE // 2)),
            pl.BlockSpec((BLOCK, 512), lambda i: (i, COL_ATTN_GATE // 2 + 1)),
            blk,
            pl.BlockSpec((BLOCK, N_Q_HEADS), lambda i: (i, 0)),
            blk,
            pl.BlockSpec((8, 128), lambda i: (0, 0)),
        ],
        out_specs=[blk, whole, whole, blk, pl.BlockSpec((8, 1024), lambda i: (0, 0))],
        out_shape=[_sds((T, 1024), BF16), _sds((T, D_KV), F32), _sds((T, D_KV), F32), _sds((T, 1024), BF16),
                   _sds((8, 1024), F32)],
        scratch_shapes=[pltpu.VMEM((BLOCK, 1024), F32)],
        compiler_params=_params(("arbitrary",), 48),
    )(sinks, *_hbm(bias, proj, proj, proj, proj, proj, proj, proj, y_attn, lse, dz_attn, token))


def _head(x, target, z_rnn, z_attn, proj, b_gate, g_post, w_rnn_out, w_attn_out, w_out):
    T = x.shape[0]
    tm = 256

    def body(x_ref, t_ref, zr_ref, za_ref, ml0_ref, ml1_ref, ml2_ref, ml3_ref, bg_ref, gp_ref, wr_ref, wa_ref, wo_ref,
             dyx_ref, dzr_ref, dza_ref, dml_ref, dout_ref, dbr_ref, dba_ref, mt_ref, zat_ref, sm_ref):
        @pl.when(pl.program_id(0) == 0)
        def _():
            sm_ref[...] = jnp.zeros_like(sm_ref)

        wr, wa, wo = wr_ref[...], wa_ref[...], wo_ref[...]
        br_rnn = _dot(zr_ref[...], wr)
        br_attn = _dot(za_ref[...], wa)
        zat_ref[...] = za_ref[...].astype(F32).T.astype(BF16)
        ml_rnn = jnp.concatenate([ml0_ref[...], ml1_ref[...]], axis=1).astype(F32)
        ml_attn = jnp.concatenate([ml2_ref[...], ml3_ref[...]], axis=1).astype(F32)
        g_rnn = _sigmoid(ml_rnn + bg_ref[:, 0:D_MODEL])
        g_attn = _sigmoid(ml_attn + bg_ref[:, D_MODEL:2 * D_MODEL])
        merged = g_rnn * br_rnn + g_attn * br_attn
        mb = merged.astype(BF16)
        mt_ref[...] = merged.T.astype(BF16)
        out = _dot(mb, wo)
        rstd = lax.rsqrt(jnp.mean(out * out, axis=-1, keepdims=True) + EPS)
        n = out * rstd
        gp = gp_ref[...]
        err = (x_ref[...] + n * gp) - t_ref[...]
        sm_ref[pl.ds(3, 1), :] += 0.5 * jnp.sum(jnp.mean(err * err, axis=-1, keepdims=True), axis=0, keepdims=True)
        dy = err * (1.0 / D_MODEL)
        dyx_ref[...] = dy
        sm_ref[pl.ds(0, 1), :] += jnp.sum(dy * n, axis=0, keepdims=True)
        dn = dy * gp
        dout = (rstd * (dn - n * jnp.mean(dn * n, axis=-1, keepdims=True))).astype(BF16)
        dout_ref[...] = dout
        dmerged = _dot_nt(dout, wo)
        dml_r = (dmerged * br_rnn) * (g_rnn * (1.0 - g_rnn))
        dml_a = (dmerged * br_attn) * (g_attn * (1.0 - g_attn))
        dml_ref[:, 0:D_MODEL] = dml_r.astype(BF16)
        dml_ref[:, D_MODEL:2 * D_MODEL] = dml_a.astype(BF16)
        sm_ref[pl.ds(1, 1), :] += jnp.sum(dml_r, axis=0, keepdims=True)
        sm_ref[pl.ds(2, 1), :] += jnp.sum(dml_a, axis=0, keepdims=True)
        dbr = (dmerged * g_rnn).astype(BF16)
        dba = (dmerged * g_attn).astype(BF16)
        dbr_ref[...] = dbr
        dba_ref[...] = dba
        dzr_ref[...] = _dot_nt(dbr, wr)
        dza_ref[...] = _dot_nt(dba, wa)

    tile = pl.BlockSpec((tm, D_MODEL), lambda i: (i, 0))
    wspec = pl.BlockSpec((D_MODEL, D_MODEL), lambda i: (0, 0))
    ml = lambda q: pl.BlockSpec((tm, 512), lambda i: (i, COL_MERGE // 2 + q))
    return pl.pallas_call(
        body,
        name="head",
        grid=(T // tm,),
        in_specs=[
            tile, tile, tile, tile,
            ml(0), ml(1), ml(2), ml(3),
            pl.BlockSpec((1, 2 * D_MODEL), lambda i: (0, 0)),
            pl.BlockSpec((1, D_MODEL), lambda i: (0, 0)),
            wspec, wspec, wspec,
        ],
        out_specs=[
            tile, tile, tile,
            pl.BlockSpec((tm, 2 * D_MODEL), lambda i: (i, 0)),
            tile, tile, tile,
            pl.BlockSpec((D_MODEL, tm), lambda i: (0, i)), pl.BlockSpec((D_MODEL, tm), lambda i: (0, i)),
            pl.BlockSpec((8, D_MODEL), lambda i: (0, 0)),
        ],
        out_shape=[
            _sds((T, D_MODEL), F32), _sds((T, D_MODEL), F32), _sds((T, D_MODEL), F32),
            _sds((T, 2 * D_MODEL), BF16),
            _sds((T, D_MODEL), BF16), _sds((T, D_MODEL), BF16), _sds((T, D_MODEL), BF16),
            _sds((D_MODEL, T), BF16), _sds((D_MODEL, T), BF16),
            _sds((8, D_MODEL), F32),
        ],
        compiler_params=_params(("arbitrary",), 56),
    )(*_hbm(x, target, z_rnn, z_attn, proj, proj, proj, proj, b_gate, g_post, w_rnn_out, w_attn_out, w_out))


def _matmul_t(at, b, name):
    M, T = at.shape
    N = b.shape[1]
    tk = min(1024, T)
    nt = T // tk

    def body(a_ref, b_ref, o_ref, ob_ref):
        @pl.when(pl.program_id(0) == 0)
        def _():
            o_ref[...] = jnp.zeros_like(o_ref)

        o_ref[...] += _dot(a_ref[...], b_ref[...])

        @pl.when(pl.program_id(0) == nt - 1)
        def _():
            ob_ref[...] = o_ref[...].astype(BF16)

    whole = pl.BlockSpec((M, N), lambda t: (0, 0))
    return pl.pallas_call(
        body,
        name=name,
        grid=(nt,),
        in_specs=[pl.BlockSpec((M, tk), lambda t: (0, t)), pl.BlockSpec((tk, N), lambda t: (t, 0))],
        out_specs=[whole, whole],
        out_shape=[_sds((M, N), F32), _sds((M, N), BF16)],
        compiler_params=_params(("arbitrary",), 48),
    )(*_hbm(at, b))


DPROJ_WIDTHS = (D_RNN, D_RNN, 1024, D_KV, D_KV, 1024, 2 * D_MODEL)


def _dproj_segments():
    segs, start = [[] for _ in range(N_CHIPS)], 0
    for p, width in enumerate(DPROJ_WIDTHS):
        for c in range(N_CHIPS):
            lo, hi = max(start, c * W_IN_SHARD), min(start + width, (c + 1) * W_IN_SHARD)
            if lo < hi:
                segs[c].append((p, lo - start, hi - start, lo - c * W_IN_SHARD, hi - c * W_IN_SHARD))
        start += width
    return segs


def _dh_bwd(pieces, w_in_g, x, dyx, g_pre, token):
    T = x.shape[0]
    tm = min(512, T)
    n = len(pieces)
    segs = _dproj_segments()

    def body(*refs):
        p_refs, w_hbm, x_ref, dyx_ref, g_ref = refs[0:n], refs[n], refs[n + 1], refs[n + 2], refs[n + 3]
        gx_ref, dg_ref, w_ref, w_sems = refs[n + 5], refs[n + 6], refs[n + 7], refs[n + 8]
        first = pl.program_id(0) == 0
        w_copies = [pltpu.make_async_copy(w_hbm.at[c], w_ref.at[c], w_sems.at[c]) for c in range(N_CHIPS)]

        @pl.when(first)
        def _():
            for cp in w_copies:
                cp.start()
            dg_ref[...] = jnp.zeros_like(dg_ref)

        dh = None
        for c in range(N_CHIPS):
            pl.when(first)(w_copies[c].wait)
            for p, a0, a1, u0, u1 in segs[c]:
                part = _dot_nt(p_refs[p][:, a0:a1].astype(BF16), w_ref[c, :, u0:u1])
                dh = part if dh is None else dh + part
        xv = x_ref[...]
        rstd = lax.rsqrt(jnp.mean(xv * xv, axis=-1, keepdims=True) + EPS)
        nx = xv * rstd
        dhg = dh * g_ref[...]
        gx_ref[...] = dyx_ref[...] + rstd * (dhg - nx * jnp.mean(dhg * nx, axis=-1, keepdims=True))
        dg_ref[pl.ds(0, 1), :] += jnp.sum(dh * nx, axis=0, keepdims=True)

    tile = pl.BlockSpec((tm, D_MODEL), lambda i: (i, 0))
    return pl.pallas_call(
        body,
        name="dh_bwd",
        grid=(T // tm,),
        in_specs=[pl.BlockSpec((tm, w), lambda i: (i, 0)) for w in DPROJ_WIDTHS] + [
            ANY, tile, tile,
            pl.BlockSpec((1, D_MODEL), lambda i: (0, 0)),
            pl.BlockSpec((8, 128), lambda i: (0, 0)),
        ],
        out_specs=[tile, pl.BlockSpec((8, D_MODEL), lambda i: (0, 0))],
        out_shape=[_sds((T, D_MODEL), F32), _sds((8, D_MODEL), F32)],
        scratch_shapes=[pltpu.VMEM(w_in_g.shape, BF16), pltpu.SemaphoreType.DMA((N_CHIPS,))],
        compiler_params=_params(("arbitrary",), 56),
    )(*_hbm(*pieces, w_in_g, x, dyx, g_pre, token))


def _dw_in(ht, pieces):
    T = ht.shape[1]
    tk = min(1024, T)
    nt = T // tk
    n = len(pieces)
    segs = _dproj_segments()

    def body(*refs):
        h_ref, p_refs, o_ref, ob_ref = refs[0], refs[1:n + 1], refs[n + 1], refs[n + 2]

        @pl.when(pl.program_id(1) == 0)
        def _():
            o_ref[...] = jnp.zeros_like(o_ref)

        for c in range(N_CHIPS):
            @pl.when(pl.program_id(0) == c)
            def _():
                for p, a0, a1, u0, u1 in segs[c]:
                    o_ref[:, u0:u1] += _dot(h_ref[...], p_refs[p][:, a0:a1].astype(BF16))

        @pl.when(pl.program_id(1) == nt - 1)
        def _():
            ob_ref[...] = o_ref[...].astype(BF16)

    def piece_spec(p):
        chips = [c for c in range(N_CHIPS) if any(s[0] == p for s in segs[c])]

        def index(c, t):
            used = functools.reduce(jnp.logical_or, [c == k for k in chips])
            return (jnp.where(used, t, 0), 0)

        return pl.BlockSpec((tk, DPROJ_WIDTHS[p]), index)

    return pl.pallas_call(
        body,
        name="dw_in",
        grid=(N_CHIPS, nt),
        in_specs=[pl.BlockSpec((D_MODEL, tk), lambda c, t: (0, t))] + [piece_spec(p) for p in range(n)],
        out_specs=[pl.BlockSpec((None, D_MODEL, W_IN_SHARD), lambda c, t: (c, 0, 0))] * 2,
        out_shape=[_sds((N_CHIPS, D_MODEL, W_IN_SHARD), F32), _sds((N_CHIPS, D_MODEL, W_IN_SHARD), BF16)],
        compiler_params=_params(("parallel", "arbitrary"), 56),
    )(*_hbm(ht, *pieces))


ELEMENTWISE_TILE_BYTES = 2 * MIB


def _row_tile(rows, cols, limit=ELEMENTWISE_TILE_BYTES):
    if rows * cols * 4 <= limit:
        return rows
    for t in (512, 256, 128, 64, 32, 16, 8):
        if rows % t == 0 and t * cols * 4 <= limit:
            return t
    return rows


def _group_tiles(groups):
    tiles = [_row_tile(g[0].shape[0], g[0].shape[1] * len(g)) for g in groups]
    steps = max(g[0].shape[0] // t for g, t in zip(groups, tiles))
    return steps, [g[0].shape[0] // steps for g in groups]


def _chip_sum(groups, chip_core, name):
    ps = [p for group_ps, _ in groups for p in group_ps]
    gots = [g for _, group_gots in groups for g in group_gots]
    n = len(ps)
    steps, group_rows = _group_tiles([group_ps for group_ps, _ in groups])
    rows = [tr for (group_ps, _), tr in zip(groups, group_rows) for _ in group_ps]

    def body(jc_ref, *refs):
        for a in range(n):
            p_ref, g0_ref, g1_ref, g2_ref, o_ref = refs[a], refs[n + 3 * a], refs[n + 3 * a + 1], refs[n + 3 * a + 2], \
                refs[4 * n + a]
            o_ref[...] = ((p_ref[...] + g0_ref[...].astype(F32)) + g1_ref[...].astype(F32)) + g2_ref[...].astype(F32)

    tile = lambda p, tr: pl.BlockSpec((tr, p.shape[1]), lambda i, jc_ref: (i, 0))
    rel = lambda p, tr, r: pl.BlockSpec((None, tr, p.shape[1]), lambda i, jc_ref: (r, i, 0))
    half = lambda p, tr: pl.BlockSpec((tr, p.shape[1]), lambda i, jc_ref: (jc_ref[1] * steps + i, 0))
    outs = pl.pallas_call(
        body,
        name=name,
        grid_spec=pltpu.PrefetchScalarGridSpec(
            num_scalar_prefetch=1,
            grid=(steps,),
            in_specs=[tile(p, tr) for p, tr in zip(ps, rows)] + [
                rel(p, tr, r) for p, tr in zip(ps, rows) for r in range(3)],
            out_specs=[half(p, tr) for p, tr in zip(ps, rows)],
        ),
        out_shape=[_sds((2 * p.shape[0], p.shape[1]), F32) for p in ps],
        compiler_params=_params(("parallel",), 48),
    )(chip_core, *_hbm(*ps, *[g for got in gots for g in (got, got, got)]))
    return list(outs)


def _place_shards(shards, chip, name):
    n = len(shards)
    tiles = [_row_tile(s.shape[0], s.shape[1]) for s in shards]
    steps = max(s.shape[0] // t for s, t in zip(shards, tiles))
    tiles = [s.shape[0] // steps for s in shards]

    def body(j_ref, *refs):
        for a in range(n):
            refs[n + a][...] = refs[a][...].astype(BF16)

    return pl.pallas_call(
        body,
        name=name,
        grid_spec=pltpu.PrefetchScalarGridSpec(
            num_scalar_prefetch=1,
            grid=(steps,),
            in_specs=[pl.BlockSpec((t, s.shape[1]), lambda i, j_ref: (i, 0)) for s, t in zip(shards, tiles)],
            out_specs=[pl.BlockSpec((None, t, s.shape[1]), lambda i, j_ref: (j_ref[0], i, 0))
                       for s, t in zip(shards, tiles)],
        ),
        out_shape=[_sds((N_CHIPS,) + s.shape, BF16) for s in shards],
        compiler_params=_params(("parallel",), 48),
    )(chip, *_hbm(*shards))


def _adamw_update(w, g, m, v):
    c1 = 1.0 - ADAM_B1 ** ADAM_STEP
    c2 = 1.0 - ADAM_B2 ** ADAM_STEP
    nm = ADAM_B1 * m + (1.0 - ADAM_B1) * g
    nv = ADAM_B2 * v + (1.0 - ADAM_B2) * (g * g)
    return (-ADAM_LR) * ((nm / c1) / (jnp.sqrt(nv / c2) + ADAM_EPS) + ADAM_WD * w), nm, nv


def _adamw(groups, name):
    params = [p for group in groups for p in group]
    n = len(params)
    steps, group_rows = _group_tiles([[p[0] for p in group] for group in groups])
    rows = [tr for group, tr in zip(groups, group_rows) for _ in group]

    def body(*refs):
        for a in range(n):
            w_ref, g_ref, m_ref, v_ref = refs[4 * a:4 * a + 4]
            d_ref, nm_ref, nv_ref, go_ref = refs[4 * n + 4 * a:4 * n + 4 * a + 4]
            g = g_ref[...]
            d_ref[...], nm_ref[...], nv_ref[...] = _adamw_update(w_ref[...], g, m_ref[...], v_ref[...])
            go_ref[...] = g

    specs = [pl.BlockSpec((tr, p[0].shape[1]), lambda i: (i, 0)) for p, tr in zip(params, rows) for _ in range(4)]
    outs = pl.pallas_call(
        body, name=name, grid=(steps,), in_specs=specs, out_specs=specs,
        out_shape=[_sds(p[0].shape, F32) for p in params for _ in range(4)],
        compiler_params=_params(("parallel",), 48),
    )(*_hbm(*[t for p in params for t in p]))
    return [tuple(outs[4 * a:4 * a + 4]) for a in range(n)]


def _adamw_whole(params, rows_params, packed, name):
    n, k = len(params), len(rows_params)

    def body(*refs):
        ins, packed_ref, outs = refs[0:4 * n + 3 * k], refs[4 * n + 3 * k], refs[4 * n + 3 * k + 1:]
        for a in range(n):
            w_ref, g_ref, m_ref, v_ref = ins[4 * a:4 * a + 4]
            d_ref, nm_ref, nv_ref = outs[3 * a:3 * a + 3]
            d_ref[...], nm_ref[...], nv_ref[...] = _adamw_update(w_ref[...], g_ref[...], m_ref[...], v_ref[...])
        for b, (_, (first, rows, lanes), _, _) in enumerate(rows_params):
            w_ref, m_ref, v_ref = ins[4 * n + 3 * b:4 * n + 3 * b + 3]
            d_ref, nm_ref, nv_ref, g_ref = outs[3 * n + 4 * b:3 * n + 4 * b + 4]
            if len(g_ref.shape) == 3:
                pieces, width = g_ref.shape[1:]
                for q in range(pieces):
                    g_ref[:, q:q + 1, :] = packed_ref[pl.ds(first, 1), q * width:(q + 1) * width][None]
            else:
                for r in range(rows):
                    g_ref[:, r * lanes:(r + 1) * lanes] = packed_ref[pl.ds(first + r, 1), 0:lanes]
            d_ref[...], nm_ref[...], nv_ref[...] = _adamw_update(w_ref[...], g_ref[...], m_ref[...], v_ref[...])

    def whole(t):
        return pl.BlockSpec(t.shape, lambda i: (0,) * t.ndim)

    flat = [t for p in params for t in p] + [t for w, _, m, v in rows_params for t in (w, m, v)] + [packed]
    like = [p[0] for p in params for _ in range(3)] + [p[0] for p in rows_params for _ in range(4)]
    outs = pl.pallas_call(
        body, name=name, grid=(1,), in_specs=[whole(t) for t in flat], out_specs=[whole(t) for t in like],
        out_shape=[_sds(t.shape, F32) for t in like], compiler_params=_params(("arbitrary",), 48),
    )(*_hbm(*flat))
    return [tuple(outs[3 * a:3 * a + 3]) for a in range(n)] + [
        tuple(outs[3 * n + 4 * b:3 * n + 4 * b + 4]) for b in range(k)]


def _place():
    return lax.axis_index("x"), lax.axis_index("y"), lax.axis_index("c")


def _chip_of(x, y, r):
    return (x ^ (r >> 1), y ^ (r & 1))


ANY = pl.BlockSpec(memory_space=pl.ANY)


def _gather_weights(placed, cw8):
    nbig = len(placed)
    halves = [s.shape[1] // 2 for s in placed]
    pieces = [max(1, h // 64) for h in halves]
    rows = [h // p for h, p in zip(halves, pieces)]
    order = [(a, q) for q in range(max(pieces)) for a in range(nbig) if q < pieces[a]]
    ici_sem = {(a, q, r): 3 * i + (r - 1) for i, (a, q) in enumerate(order) for r in (1, 2, 3)}
    cw_sem = {r: 3 * len(order) + (r - 1) for r in (1, 2, 3)}
    d2d_sem = {key: 3 * len(order) + 3 + k for key, k in ici_sem.items()}
    nsem = 6 * len(order) + 3

    def body(*refs):
        cw_ref, dsts, gcw_ref = refs[nbig], refs[nbig + 1:2 * nbig + 1], refs[2 * nbig + 1]
        send_sems, recv_sems = refs[2 * nbig + 2:]
        x, y, c = _place()
        j = 2 * x + y

        def piece_rows(a, q, core):
            return pl.ds(pl.multiple_of(core * halves[a] + q * rows[a], 16), rows[a])

        def ici(a, q, r):
            tx, ty = _chip_of(x, y, r)
            k = ici_sem[(a, q, r)]
            region = dsts[a].at[j, piece_rows(a, q, c), :]
            return pltpu.make_async_remote_copy(
                src_ref=region, dst_ref=region, send_sem=send_sems.at[k], recv_sem=recv_sems.at[k],
                device_id=(tx, ty, c), device_id_type=MESH)

        def ici_landed(a, q, r):
            tx, ty = _chip_of(x, y, r)
            k = ici_sem[(a, q, r)]
            region = dsts[a].at[2 * tx + ty, piece_rows(a, q, c), :]
            return pltpu.make_async_remote_copy(
                src_ref=region, dst_ref=region, send_sem=send_sems.at[k], recv_sem=recv_sems.at[k],
                device_id=(tx, ty, c), device_id_type=MESH)

        def d2d(a, q, r, core):
            tx, ty = _chip_of(x, y, r)
            k = d2d_sem[(a, q, r)]
            region = dsts[a].at[2 * tx + ty, piece_rows(a, q, core), :]
            return pltpu.make_async_remote_copy(
                src_ref=region, dst_ref=region, send_sem=send_sems.at[k], recv_sem=recv_sems.at[k],
                device_id=(x, y, 1 - c), device_id_type=MESH)

        def cw_copy(r):
            tx, ty = _chip_of(x, y, r)
            k = cw_sem[r]
            return pltpu.make_async_remote_copy(
                src_ref=cw_ref, dst_ref=gcw_ref.at[j], send_sem=send_sems.at[k], recv_sem=recv_sems.at[k],
                device_id=(tx, ty, c), device_id_type=MESH)

        def cw_landed(r):
            tx, ty = _chip_of(x, y, r)
            k = cw_sem[r]
            region = gcw_ref.at[2 * tx + ty]
            return pltpu.make_async_remote_copy(
                src_ref=region, dst_ref=region, send_sem=send_sems.at[k], recv_sem=recv_sems.at[k],
                device_id=(tx, ty, c), device_id_type=MESH)

        def relay(a, q, origin, to):
            ox, oy = _chip_of(x, y, origin)
            tx, ty = _chip_of(x, y, to)
            k = ici_sem[(a, q, 3)]
            region = dsts[a].at[2 * ox + oy, piece_rows(a, q, c), :]
            return pltpu.make_async_remote_copy(
                src_ref=region, dst_ref=region, send_sem=send_sems.at[k], recv_sem=recv_sems.at[k],
                device_id=(tx, ty, c), device_id_type=MESH)

        first = [ici(a, q, r) for (a, q) in order for r in (1, 2)] + [cw_copy(r) for r in (1, 2, 3)]
        for cp in first:
            cp.start()
        passed = []
        for (a, q) in order:
            for r in (1, 2):
                ici_landed(a, q, r).wait_recv()
                if q % 2 == r - 1:
                    cp = relay(a, q, r, 3 - r)
                    cp.start()
                    passed.append(cp)
                cp = d2d(a, q, r, c)
                cp.start()
                passed.append(cp)
        for (a, q) in order:
            ici_landed(a, q, 3).wait_recv()
            cp = d2d(a, q, 3, c)
            cp.start()
            passed.append(cp)
        for r in (1, 2, 3):
            cw_landed(r).wait_recv()
        for (a, q) in order:
            for r in (1, 2, 3):
                d2d(a, q, r, 1 - c).wait_recv()
        for cp in first + passed:
            cp.wait_send()

    return pl.pallas_call(
        body,
        name="gather_weights",
        in_specs=[ANY] * (nbig + 1),
        out_specs=[ANY] * (nbig + 1),
        out_shape=[_sds(s.shape, s.dtype) for s in placed] + [_sds((N_CHIPS,) + cw8.shape, cw8.dtype)],
        input_output_aliases={a: a for a in range(nbig)},
        scratch_shapes=[pltpu.SemaphoreType.DMA((nsem,)), pltpu.SemaphoreType.DMA((nsem,))],
    )(*placed, cw8)


def _gather_late_start(placed, after, name):
    n = len(placed)
    halves = [s.shape[1] // 2 for s in placed]

    def body(*refs):
        g_refs = refs[0:n]
        send_sems, recv_sems, token = refs[n + 1], refs[n + 2], refs[-1]
        x, y, c = _place()
        j = 2 * x + y
        for a in range(n):
            mine = g_refs[a].at[j, pl.ds(pl.multiple_of(c * halves[a], 16), halves[a]), :]
            for r in (1, 2, 3):
                tx, ty = _chip_of(x, y, r)
                for to_core in (0, 1):
                    k = ((a * 3 + (r - 1)) * 2 + c) * 2 + to_core
                    pltpu.make_async_remote_copy(
                        src_ref=mine, dst_ref=mine, send_sem=send_sems.at[k], recv_sem=recv_sems.at[k],
                        device_id=(tx, ty, to_core), device_id_type=MESH).start()
        token[...] = jnp.zeros_like(token)

    hbm = lambda t: pltpu.HBM(t.shape, t.dtype)
    keep = lambda t: pltpu.with_memory_space_constraint(t, pltpu.HBM)
    nsem = 12 * n
    outs = pl.pallas_call(
        body,
        name=name,
        in_specs=[HBM] * n + [ANY],
        out_specs=(SEM, SEM, *[HBM] * n, pl.BlockSpec(memory_space=pltpu.VMEM)),
        out_shape=(pltpu.SemaphoreType.DMA((nsem,)), pltpu.SemaphoreType.DMA((nsem,)), *[hbm(p) for p in placed],
                   jax.ShapeDtypeStruct((8, 128), F32)),
        input_output_aliases={i: 2 + i for i in range(n)},
        compiler_params=pltpu.CompilerParams(has_side_effects=DATAFLOW),
    )(*[keep(p) for p in placed], after)
    return outs[0], outs[1], list(outs[2:2 + n]), outs[-1]


def _gather_late_wait(send_sems, recv_sems, thru, after, name):
    n = len(thru)
    halves = [s.shape[1] // 2 for s in thru]

    def body(*refs):
        g_refs = refs[0:n]
        send_sems, recv_sems = refs[n], refs[n + 1]
        x, y, c = _place()
        j = 2 * x + y
        for a in range(n):
            mine = g_refs[a].at[j, pl.ds(pl.multiple_of(c * halves[a], 16), halves[a]), :]
            for r in (1, 2, 3):
                tx, ty = _chip_of(x, y, r)
                for other in (0, 1):
                    k_out = ((a * 3 + (r - 1)) * 2 + c) * 2 + other
                    pltpu.make_async_remote_copy(
                        src_ref=mine, dst_ref=mine, send_sem=send_sems.at[k_out], recv_sem=recv_sems.at[k_out],
                        device_id=(tx, ty, other), device_id_type=MESH).wait_send()
                    k_in = ((a * 3 + (r - 1)) * 2 + other) * 2 + c
                    theirs = g_refs[a].at[2 * tx + ty, pl.ds(other * halves[a], halves[a]), :]
                    pltpu.make_async_remote_copy(
                        src_ref=theirs, dst_ref=theirs, send_sem=send_sems.at[k_in], recv_sem=recv_sems.at[k_in],
                        device_id=(tx, ty, other), device_id_type=MESH).wait_recv()

    hbm = lambda t: pltpu.HBM(t.shape, t.dtype)
    outs = pl.pallas_call(
        body,
        name=name,
        in_specs=[HBM] * n + [SEM, SEM, ANY],
        out_specs=[HBM] * n,
        out_shape=[hbm(t) for t in thru],
        input_output_aliases={i: i for i in range(n)},
        compiler_params=pltpu.CompilerParams(has_side_effects=DATAFLOW),
    )(*thru, send_sems, recv_sems, after)
    return list(outs)


D2D_PIECE_ROWS = 64
PAIR_SUM_TILE_BYTES = 2 * MIB


def _pair_sum(gs, gbs, chip_core, name):
    n = len(gs)
    nch, R, C = gs[0].shape
    h = R // 2
    tr = _row_tile(h, C * n, PAIR_SUM_TILE_BYTES)
    nt = h // tr
    rows = min(D2D_PIECE_ROWS, tr)

    def body(jc_ref, *refs):
        g_refs, gb_refs, p_refs, pb_refs = refs[0:n], refs[n:2 * n], refs[2 * n:3 * n], refs[3 * n:4 * n]
        got_refs, send_sems, recv_sems = refs[4 * n:5 * n], refs[5 * n], refs[5 * n + 1]
        i, j = pl.program_id(0), pl.program_id(1)
        x, y, c = _place()

        def copy(a, ti, tj, first, count):
            src_rows = pl.ds(pl.multiple_of((1 - c) * h + ti * tr + first, 16), count)
            dst_rows = pl.ds(pl.multiple_of(ti * tr + first, 16), count)
            return pltpu.make_async_remote_copy(
                src_ref=gb_refs[a].at[tj, src_rows, :], dst_ref=got_refs[a].at[tj, dst_rows, :],
                send_sem=send_sems.at[a, ti, tj], recv_sem=recv_sems.at[a, ti, tj],
                device_id=(x, y, 1 - c), device_id_type=MESH)

        @pl.when((i == 0) & (j == 0))
        def _():
            for ti in range(nt):
                for tj in range(nch):
                    for a in range(n):
                        for q in range(tr // rows):
                            copy(a, ti, tj, q * rows, rows).start()

        for a in range(n):
            copy(a, i, j, 0, tr).wait()
            s = g_refs[a][...] + got_refs[a][j, pl.ds(pl.multiple_of(i * tr, 16), tr), :].astype(F32)
            pb_refs[a][...] = s.astype(BF16)

            @pl.when(j == jc_ref[0])
            def _():
                p_refs[a][...] = s

    by_chip = pl.BlockSpec((None, tr, C), lambda i, j, jc_ref: (j, i, 0))
    outs = pl.pallas_call(
        body,
        name=name,
        grid_spec=pltpu.PrefetchScalarGridSpec(
            num_scalar_prefetch=1,
            grid=(nt, nch),
            in_specs=[pl.BlockSpec((None, tr, C), lambda i, j, jc_ref: (j, jc_ref[1] * nt + i, 0))] * n + [ANY] * n,
            out_specs=[pl.BlockSpec((tr, C), lambda i, j, jc_ref: (i, 0))] * n + [by_chip] * n,
            scratch_shapes=[pltpu.VMEM((nch, h, C), BF16)] * n + [pltpu.SemaphoreType.DMA((n, nt, nch))] * 2,
        ),
        out_shape=[_sds((h, C), F32)] * n + [_sds((nch, h, C), BF16)] * n,
        compiler_params=_params(("arbitrary", "arbitrary"), 48),
    )(chip_core, *_hbm(*gs, *gbs))
    return list(outs[:n]), list(outs[n:])


HBM = pl.BlockSpec(memory_space=pltpu.HBM)
SEM = pl.BlockSpec(memory_space=pltpu.SEMAPHORE)
DATAFLOW = pltpu.SideEffectType.DATAFLOW_SIDE_EFFECTING


def _chip_copy(p_refs, land_refs, send_sems, recv_sems, a, r, blocked):
    x, y, c = _place()
    tx, ty = _chip_of(x, y, r)
    k = a * 3 + (r - 1)
    return pltpu.make_async_remote_copy(
        src_ref=p_refs[a].at[2 * tx + ty] if blocked else p_refs[a], dst_ref=land_refs[a].at[r - 1],
        send_sem=send_sems.at[k], recv_sem=recv_sems.at[k], device_id=(tx, ty, c), device_id_type=MESH)


def _chip_exchange_start(psums, name, blocked=True):
    n = len(psums)
    lands = [lax.empty((3,) + (p.shape[1:] if blocked else p.shape), p.dtype) for p in psums]

    def body(*refs):
        p_refs, land_refs = refs[0:n], refs[n:2 * n]
        send_sems, recv_sems, token = refs[2 * n], refs[2 * n + 1], refs[-1]
        for a in range(n):
            for r in (1, 2, 3):
                _chip_copy(p_refs, land_refs, send_sems, recv_sems, a, r, blocked).start()
        token[...] = jnp.zeros_like(token)

    hbm = lambda t: pltpu.HBM(t.shape, t.dtype)
    keep = lambda t: pltpu.with_memory_space_constraint(t, pltpu.HBM)
    outs = pl.pallas_call(
        body,
        name=name,
        in_specs=[HBM] * (2 * n),
        out_specs=(SEM, SEM, *[HBM] * (2 * n), pl.BlockSpec(memory_space=pltpu.VMEM)),
        out_shape=(pltpu.SemaphoreType.DMA((3 * n,)), pltpu.SemaphoreType.DMA((3 * n,)),
                   *[hbm(p) for p in psums], *[hbm(l) for l in lands], _sds((8, 128), F32)),
        input_output_aliases={i: 2 + i for i in range(2 * n)},
        compiler_params=pltpu.CompilerParams(has_side_effects=DATAFLOW),
    )(*[keep(p) for p in psums], *[keep(l) for l in lands])
    return outs[0], outs[1], list(outs[2:2 + n]), list(outs[2 + n:2 + 2 * n]), outs[-1]


def _chip_exchange_wait(send_sems, recv_sems, p_thru, land_thru, after, name, blocked=True):
    n = len(p_thru)

    def body(*refs):
        p_refs, land_refs = refs[0:n], refs[n:2 * n]
        send_sems, recv_sems = refs[2 * n], refs[2 * n + 1]
        for a in range(n):
            for r in (1, 2, 3):
                copy = _chip_copy(p_refs, land_refs, send_sems, recv_sems, a, r, blocked)
                copy.wait_send()
                copy.wait_recv()

    hbm = lambda t: pltpu.HBM(t.shape, t.dtype)
    outs = pl.pallas_call(
        body,
        name=name,
        in_specs=[HBM] * (2 * n) + [SEM, SEM, ANY],
        out_specs=[HBM] * (2 * n),
        out_shape=[hbm(p) for p in p_thru] + [hbm(l) for l in land_thru],
        input_output_aliases={i: i for i in range(2 * n)},
        compiler_params=pltpu.CompilerParams(has_side_effects=DATAFLOW),
    )(*p_thru, *land_thru, send_sems, recv_sems, after)
    return list(outs[0:n]), list(outs[n:2 * n])


def _pair_share(fulls):
    n = len(fulls)
    halves = [f.shape[0] // 2 for f in fulls]

    def body(*refs):
        full_refs = refs[n:2 * n]
        send_sems, recv_sems = refs[2 * n:]
        x, y, c = _place()

        def half_of(a, core):
            return full_refs[a].at[pl.ds(pl.multiple_of(core * halves[a], 8), halves[a]), :]

        def remote(a, src, dst):
            return pltpu.make_async_remote_copy(
                src_ref=src, dst_ref=dst, send_sem=send_sems.at[a], recv_sem=recv_sems.at[a],
                device_id=(x, y, 1 - c), device_id_type=MESH)

        for a in range(n):
            for q in range(halves[a] // D2D_PIECE_ROWS):
                piece = full_refs[a].at[
                    pl.ds(pl.multiple_of(c * halves[a] + q * D2D_PIECE_ROWS, 8), D2D_PIECE_ROWS), :]
                remote(a, piece, piece).start()
        for a in range(n):
            remote(a, half_of(a, c), half_of(a, c)).wait_send()
            remote(a, half_of(a, 1 - c), half_of(a, 1 - c)).wait_recv()

    return pl.pallas_call(
        body,
        name="pair_share",
        in_specs=[ANY] * n,
        out_specs=[ANY] * n,
        out_shape=[_sds(f.shape, F32) for f in fulls],
        input_output_aliases={a: a for a in range(n)},
        scratch_shapes=[pltpu.SemaphoreType.DMA((n,)), pltpu.SemaphoreType.DMA((n,))],
    )(*fulls)


def _small_pair_sum(s):
    R, C = s.shape
    V = SMALL_VECTOR_ROWS

    def body(s_ref, v_ref, m_ref, sib, send_sem, recv_sem):
        x, y, c = _place()

        def to_sib(src, dst):
            return pltpu.make_async_remote_copy(
                src_ref=src, dst_ref=dst, send_sem=send_sem, recv_sem=recv_sem,
                device_id=(x, y, 1 - c), device_id_type=MESH)

        for q in range(R // 8):
            to_sib(s_ref.at[pl.ds(8 * q, 8), :], sib.at[pl.ds(8 * q, 8), :]).start()
        to_sib(s_ref, sib).wait()
        v_ref[...] = s_ref[pl.ds(0, V), :] + sib[pl.ds(0, V), :]
        m_ref[...] = (s_ref[pl.ds(V, R - V), :] + sib[pl.ds(V, R - V), :]).astype(BF16)

    return pl.pallas_call(
        body,
        name="small_pair_sum",
        in_specs=[pl.BlockSpec(memory_space=pltpu.VMEM)],
        out_specs=[pl.BlockSpec(memory_space=pltpu.VMEM)] * 2,
        out_shape=[jax.ShapeDtypeStruct((V, C), F32), jax.ShapeDtypeStruct((R - V, C), BF16)],
        scratch_shapes=[pltpu.VMEM((R, C), F32), pltpu.SemaphoreType.DMA, pltpu.SemaphoreType.DMA],
    )(s)


def _small_total(chip, own, landed):
    V, C = own[0].shape
    M = own[1].shape[0]

    def body(j_ref, v_ref, m_ref, lv_ref, lm_ref, o_ref, chips_v, chips_m):
        j = j_ref[0]
        chips_v[j] = v_ref[...]
        chips_m[j] = m_ref[...]
        for r in (1, 2, 3):
            chips_v[j ^ r] = lv_ref[r - 1]
            chips_m[j ^ r] = lm_ref[r - 1]
        o_ref[pl.ds(0, V), :] = (chips_v[0] + chips_v[1]) + (chips_v[2] + chips_v[3])
        o_ref[pl.ds(V, M), :] = (chips_m[0].astype(F32) + chips_m[1].astype(F32)) + (
            chips_m[2].astype(F32) + chips_m[3].astype(F32))

    vmem = pl.BlockSpec(memory_space=pltpu.VMEM)
    return pl.pallas_call(
        body,
        name="small_total",
        in_specs=[pl.BlockSpec(memory_space=pltpu.SMEM), vmem, vmem, vmem, vmem],
        out_specs=vmem,
        out_shape=jax.ShapeDtypeStruct((V + M, C), F32),
        scratch_shapes=[pltpu.VMEM((N_CHIPS, V, C), F32), pltpu.VMEM((N_CHIPS, M, C), BF16)],
    )(chip, own[0], own[1], landed[0], landed[1])


def _local_grads(x, target, g_pre, w_in_g, b_gate, conv_w, conv_b, w_rg_a, b_rg_a, w_rg_x, b_rg_x, lam, sinks,
                 out_weights, fwd_token, g_post, on_out_grads, on_w_in_grad):
    b_a = b_rg_a.reshape(1, D_RNN)
    b_x = b_rg_x.reshape(1, D_RNN)

    proj, ht = _proj_fwd(x, g_pre, w_in_g)
    y_rnn, z_rnn, conv, z_rnn_t = _rnn_fwd(proj, conv_w, conv_b, w_rg_a, w_rg_x, b_a, b_x, lam, fwd_token)
    bias = _attn_bias()
    y_attn, z_attn, lse = _attn_fwd(proj, sinks, bias)
    w_rnn_out, w_attn_out, w_out = out_weights(z_attn)
    dyx, dz_rnn, dz_attn, dml, dout, dbr_rnn, dbr_attn, merged_t, z_attn_t, head_small = _head(
        x, target, z_rnn, z_attn, proj, b_gate, g_post, w_rnn_out, w_attn_out, w_out)
    out_grads = [_matmul_t(z_rnn_t, dbr_rnn, "dw_rnn_out"), _matmul_t(z_attn_t, dbr_attn, "dw_attn_out"),
                 _matmul_t(merged_t, dout, "dw_out")]
    shard_rows = lambda d: d.reshape(N_CHIPS, OUT_SHARD, D_MODEL)
    token = on_out_grads([shard_rows(g) for g, _ in out_grads], [shard_rows(gb) for _, gb in out_grads])
    dq, dk, dv, dag, attn_small = _attn_bwd(proj, y_attn, lse, dz_attn, sinks, bias, token)
    drx, drg, dwa, dwx, rnn_small = _rnn_bwd(proj, conv, y_rnn, dz_rnn, conv_w, w_rg_a, w_rg_x, b_a, b_x, lam)
    dproj = [drx, drg, dq, dk, dv, dag, dml]
    token = on_w_in_grad(*_dw_in(ht, dproj))
    grad_x, dh_small = _dh_bwd(dproj, w_in_g, x, dyx, g_pre, token)
    small = jnp.concatenate([rnn_small, head_small, dh_small + attn_small,
                             dwa.reshape(64, 1024), dwx.reshape(64, 1024)], axis=0)
    return grad_x, small


ROW_LOSS = 11


SMALL_ROW_TENSORS = {"b_rg_a": (0, 1, D_RNN), "b_rg_x": (1, 1, D_RNN), "lru_lambda": (2, 1, D_RNN),
                     "conv_b": (3, 1, D_RNN), "post_norm_g": (8, 1, D_MODEL), "b_gate": (9, 2, D_MODEL),
                     "pre_norm_g": (16, 1, D_MODEL), "attn_sinks": (17, 1, N_Q_HEADS)}


def _unpack_small(s, conv_cols):
    return {
        "conv_w": s[4:8, 0:conv_cols].reshape(1, CONV_W, conv_cols),
        "w_rg_a": s[24:88].reshape(1, 16, 64, 64), "w_rg_x": s[88:152].reshape(1, 16, 64, 64),
    }


WEIGHTS = ["pre_norm_g", "w_in", "b_gate", "conv_w", "conv_b", "w_rg_a", "b_rg_a", "w_rg_x", "b_rg_x", "lru_lambda",
           "attn_sinks", "w_rnn_out", "w_attn_out", "w_out", "post_norm_g"]
BIG = ["w_in", "w_rnn_out", "w_attn_out", "w_out"]


def kernel(x, pre_norm_g, w_in, b_gate, conv_w, conv_b, w_rg_a, b_rg_a, w_rg_x, b_rg_x, lru_lambda, attn_sinks, w_rnn_out, w_attn_out, w_out, post_norm_g, loss_target, m_pre_norm_g, m_w_in, m_b_gate, m_conv_w, m_conv_b, m_w_rg_a, m_b_rg_a, m_w_rg_x, m_b_rg_x, m_lru_lambda, m_attn_sinks, m_w_rnn_out, m_w_attn_out, m_w_out, m_post_norm_g, v_pre_norm_g, v_w_in, v_b_gate, v_conv_w, v_conv_b, v_w_rg_a, v_b_rg_a, v_w_rg_x, v_b_rg_x, v_lru_lambda, v_attn_sinks, v_w_rnn_out, v_w_attn_out, v_w_out, v_post_norm_g):
    w = dict(pre_norm_g=pre_norm_g, w_in=w_in, b_gate=b_gate, conv_w=conv_w, conv_b=conv_b, w_rg_a=w_rg_a,
             b_rg_a=b_rg_a, w_rg_x=w_rg_x, b_rg_x=b_rg_x, lru_lambda=lru_lambda, attn_sinks=attn_sinks,
             w_rnn_out=w_rnn_out, w_attn_out=w_attn_out, w_out=w_out, post_norm_g=post_norm_g)
    m = dict(pre_norm_g=m_pre_norm_g, w_in=m_w_in, b_gate=m_b_gate, conv_w=m_conv_w, conv_b=m_conv_b, w_rg_a=m_w_rg_a,
             b_rg_a=m_b_rg_a, w_rg_x=m_w_rg_x, b_rg_x=m_b_rg_x, lru_lambda=m_lru_lambda, attn_sinks=m_attn_sinks,
             w_rnn_out=m_w_rnn_out, w_attn_out=m_w_attn_out, w_out=m_w_out, post_norm_g=m_post_norm_g)
    v = dict(pre_norm_g=v_pre_norm_g, w_in=v_w_in, b_gate=v_b_gate, conv_w=v_conv_w, conv_b=v_conv_b, w_rg_a=v_w_rg_a,
             b_rg_a=v_b_rg_a, w_rg_x=v_w_rg_x, b_rg_x=v_b_rg_x, lru_lambda=v_lru_lambda, attn_sinks=v_attn_sinks,
             w_rnn_out=v_w_rnn_out, w_attn_out=v_w_attn_out, w_out=v_w_out, post_norm_g=v_post_norm_g)
    chip = 2 * lax.axis_index("x") + lax.axis_index("y")

    chip_idx = chip.astype(jnp.int32).reshape(1)
    chip_core = jnp.stack([chip, lax.axis_index("c")]).astype(jnp.int32)
    cw8 = jnp.pad(conv_w[0], ((0, 8 - CONV_W), (0, 0)))
    placed = _place_shards([w_in[0], w_rnn_out[0], w_attn_out[0], w_out[0]], chip_idx, "place_shards")
    win_g, cw_g = _gather_weights(placed[:1], cw8)
    late_send, late_recv, late_thru, late_token = _gather_late_start(placed[1:], win_g, "gather_late_start")
    cw_g = lax.dynamic_update_slice_in_dim(cw_g, cw8[None], chip, axis=0)
    conv_w_full = jnp.transpose(cw_g[:, 0:CONV_W, :], (1, 0, 2)).reshape(CONV_W, D_RNN)

    started = {}

    def start_reduction(tag, grads, grads_b16):
        psums, psums_b16 = _pair_sum(grads, grads_b16, chip_core, "pair_sum_" + tag)
        send_sems, recv_sems, p_thru, land_thru, token = _chip_exchange_start(psums_b16, "chip_exchange_start_" + tag)
        started[tag] = (psums, send_sems, recv_sems, p_thru, land_thru)
        return token

    def end_reduction(tag, after):
        psums, send_sems, recv_sems, p_thru, land_thru = started[tag]
        _, landed = _chip_exchange_wait(send_sems, recv_sems, p_thru, land_thru, after, "chip_exchange_wait_" + tag)
        return psums, landed

    def out_weights(after):
        gathered = _gather_late_wait(late_send, late_recv, late_thru, after, "gather_late_wait")
        return [g.reshape(D_MODEL, D_MODEL) for g in gathered]

    grad_x, small = _local_grads(
        x[0], loss_target[0], pre_norm_g, win_g, b_gate, conv_w_full, conv_b, w_rg_a[0], b_rg_a[0], w_rg_x[0],
        b_rg_x[0], lru_lambda, attn_sinks[0], out_weights, late_token, post_norm_g,
        on_out_grads=lambda grads, grads_b16: start_reduction("out", grads, grads_b16),
        on_w_in_grad=lambda grad, grad_b16: start_reduction("in", [grad], [grad_b16]))

    small_chip = _small_pair_sum(small)
    small_send, small_recv, small_thru, small_land, small_token = _chip_exchange_start(
        list(small_chip), "small_exchange_start", blocked=False)

    halves = _chip_sum([end_reduction("in", small_token), end_reduction("out", small_token)], chip_core, "chip_sum")
    gbig = dict(zip(BIG, _pair_share(halves)))

    grads, delta, new_m, new_v = {}, {}, {}, {}
    updates = _adamw([[(w[n][0], gbig[n], m[n][0], v[n][0]) for n in names] for names in (BIG[:1], BIG[1:])],
                     "adamw_big")
    for n, (d, nm, nv, g) in zip(BIG, updates):
        grads[n], delta[n], new_m[n], new_v[n] = g[None], d[None], nm[None], nv[None]

    small_own, small_landed = _chip_exchange_wait(small_send, small_recv, small_thru, small_land, delta[BIG[-1]],
                                                  "small_exchange_wait", blocked=False)
    small_sum = _small_total(chip_idx, small_own, small_landed)
    total_loss = small_sum[ROW_LOSS, 0]
    gsmall = _unpack_small(small_sum, D_RNN)
    conv_shard = D_RNN // N_CHIPS
    gsmall["conv_w"] = lax.dynamic_slice_in_dim(gsmall["conv_w"], chip * conv_shard, conv_shard, axis=2)
    for n in gsmall:
        grads[n] = gsmall[n].reshape(w[n].shape)
    updates = _adamw_whole([(w[n], grads[n], m[n], v[n]) for n in gsmall],
                           [(w[n], rows, m[n], v[n]) for n, rows in SMALL_ROW_TENSORS.items()], small_sum, "adamw_small")
    for n, (d, nm, nv, *g) in zip([*gsmall, *SMALL_ROW_TENSORS], updates):
        delta[n], new_m[n], new_v[n] = d, nm, nv
        grads.update({n: g[0]} if g else {})

    return (total_loss, grad_x[None], *[grads[n] for n in WEIGHTS], *[delta[n] for n in WEIGHTS],
            *[new_m[n] for n in WEIGHTS], *[new_v[n] for n in WEIGHTS])
```

```python
import functools
import math

import jax
import jax.numpy as jnp
import numpy as np
from jax import lax
from jax.experimental import pallas as pl
from jax.experimental.pallas import tpu as pltpu

F32 = jnp.float32
BF16 = jnp.bfloat16

D_MODEL = 1024
D_RNN = 1024
RNN_BLOCKS = 16
RNN_BLOCK_W = 64
CONV_W = 4
LRU_C = 8.0
N_Q_HEADS = 16
N_KV_HEADS = 4
GROUP = 4
HEAD_DIM = 64
D_KV = 256
BLOCK = 128
ALIBI_MAX_BIAS = 8.0
EPS = 1e-6
D_IN = 6656
N_CHIPS = 4
W_IN_SHARD = D_IN // N_CHIPS
OUT_SHARD = D_MODEL // N_CHIPS
ADAM_LR = 0.001
ADAM_B1 = 0.9
ADAM_B2 = 0.999
ADAM_EPS = 1e-08
ADAM_WD = 0.01
ADAM_STEP = 10
NEG_BIG = -1e30
MIB = 1 << 20

COL_RNN_X = 0
COL_RNN_GATE = 4
COL_Q = 8
COL_K = 12
COL_V = 13
COL_ATTN_GATE = 14
COL_MERGE = 18

RNN_TILE = 256
RNN_CHUNK = 512
SMALL_ROWS = 152
SMALL_VECTOR_ROWS = 24
MESH = pl.DeviceIdType.MESH


def _sds(shape, dtype):
    return pltpu.HBM(shape, dtype)


def _params(sem=None, vmem_mib=None):
    kw = {}
    if sem is not None:
        kw["dimension_semantics"] = sem
    if vmem_mib is not None:
        kw["vmem_limit_bytes"] = vmem_mib * MIB
    return pltpu.CompilerParams(**kw)


def _hbm(*arrays):
    return [pltpu.with_memory_space_constraint(a, pltpu.HBM) for a in arrays]


def _dot(a, b):
    return jnp.dot(a, b, preferred_element_type=F32)


def _dot_nt(a, b):
    return lax.dot_general(a, b, (((1,), (1,)), ((), ())), preferred_element_type=F32)


def _dot_tn(a, b):
    return lax.dot_general(a, b, (((0,), (0,)), ((), ())), preferred_element_type=F32)


def _sigmoid(x):
    return 0.5 * jnp.tanh(0.5 * x) + 0.5


def _sigmoid_small(x):
    return 1.0 / (1.0 + jnp.exp(-x))


def _softplus(x):
    return jnp.maximum(x, 0.0) + jnp.log(1.0 + jnp.exp(-jnp.abs(x)))


def _one_minus_square(a, log_a):
    return -jnp.tanh(log_a) * (a * a + 1.0)


def _proj_fwd(x, g_pre, w_in_g):
    T = x.shape[0]
    tm = min(1024, T)

    def body(x_ref, g_ref, w_hbm, proj_ref, ht_ref, h_s, w_s, w_sems):
        i, j = pl.program_id(0), pl.program_id(1)
        w_copies = [pltpu.make_async_copy(w_hbm.at[c], w_s.at[c], w_sems.at[c]) for c in range(N_CHIPS)]

        @pl.when((i == 0) & (j == 0))
        def _():
            for cp in w_copies:
                cp.start()

        @pl.when(j == 0)
        def _():
            xv = x_ref[...]
            rstd = lax.rsqrt(jnp.mean(xv * xv, axis=-1, keepdims=True) + EPS)
            hf = (xv * rstd) * g_ref[...]
            h_s[...] = hf.astype(BF16)
            ht_ref[...] = hf.T.astype(BF16)

        for c in range(N_CHIPS):
            pl.when((i == 0) & (j == c))(w_copies[c].wait)
        proj_ref[...] = _dot(h_s[...], w_s[j]).astype(BF16)

    return pl.pallas_call(
        body,
        name="proj_fwd",
        grid=(T // tm, N_CHIPS),
        in_specs=[
            pl.BlockSpec((tm, D_MODEL), lambda i, j: (i, 0)),
            pl.BlockSpec((1, D_MODEL), lambda i, j: (0, 0)),
            pl.BlockSpec(memory_space=pl.ANY),
        ],
        out_specs=[
            pl.BlockSpec((tm, W_IN_SHARD), lambda i, j: (i, j)),
            pl.BlockSpec((D_MODEL, tm), lambda i, j: (0, i)),
        ],
        out_shape=[_sds((T, D_IN), BF16), _sds((D_MODEL, T), BF16)],
        scratch_shapes=[pltpu.VMEM((tm, D_MODEL), BF16), pltpu.VMEM(w_in_g.shape, BF16),
                        pltpu.SemaphoreType.DMA((N_CHIPS,))],
        compiler_params=_params(("arbitrary", "arbitrary"), 48),
    )(*_hbm(x, g_pre, w_in_g))


def _shift_down(x, tail, s, row):
    n = x.shape[0]
    xs = pltpu.roll(x, s, 0)
    tail_t = jnp.tile(pltpu.roll(tail, s, 0), (n // 8, 1))
    return jnp.where(row < s, tail_t, xs)


def _shift_up(x, head, s, row):
    n = x.shape[0]
    xs = pltpu.roll(x, n - s, 0)
    head_t = jnp.tile(pltpu.roll(head, 8 - s, 0), (n // 8, 1))
    return jnp.where(row >= n - s, head_t, xs)


def _conv_taps(x, tail, row):
    return [_shift_down(x, tail, 3, row), _shift_down(x, tail, 2, row), _shift_down(x, tail, 1, row), x]


def _rglru_gates(c, wa, wx, ba, bx, lam):
    cb = c.astype(BF16)
    r = _sigmoid_small(_dot(cb, wa) + ba)
    i = _sigmoid(_dot(cb, wx) + bx)
    log_a = (-LRU_C) * r * _softplus(-lam)
    a = jnp.exp(log_a)
    w = _one_minus_square(a, log_a)
    inv_mult = lax.rsqrt(w)
    return cb, r, i, a, w * inv_mult, inv_mult


GATE_BLOCKS_PER_TILE = RNN_TILE // RNN_BLOCK_W
GATE_BLOCKS = pl.BlockSpec((GATE_BLOCKS_PER_TILE, RNN_BLOCK_W, RNN_BLOCK_W), lambda j, t: (j, 0, 0))


def _fill_block_diag(bd_ref, w_ref):
    bd_ref[...] = jnp.zeros_like(bd_ref)
    for a in range(GATE_BLOCKS_PER_TILE):
        lo = a * RNN_BLOCK_W
        bd_ref[lo:lo + RNN_BLOCK_W, lo:lo + RNN_BLOCK_W] = w_ref[a].astype(BF16)


SUBLANES = 8


def _scan_down(a, u, row):
    n = a.shape[0]
    s = 1
    while s < SUBLANES:
        a_sh = jnp.where(row >= s, pltpu.roll(a, s, 0), 1.0)
        u_sh = jnp.where(row >= s, pltpu.roll(u, s, 0), 0.0)
        u = a * u_sh + u
        a = a * a_sh
        s *= 2
    while s < n:
        u = jnp.concatenate([u[:s], a[s:] * u[:n - s] + u[s:]], axis=0)
        a = jnp.concatenate([a[:s], a[s:] * a[:n - s]], axis=0)
        s *= 2
    return a, u


def _scan_up(b, u, row):
    n = b.shape[0]
    s = 1
    while s < SUBLANES:
        b_sh = jnp.where(row < n - s, pltpu.roll(b, n - s, 0), 1.0)
        u_sh = jnp.where(row < n - s, pltpu.roll(u, n - s, 0), 0.0)
        u = b * u_sh + u
        b = b * b_sh
        s *= 2
    while s < n:
        u = jnp.concatenate([b[:n - s] * u[s:] + u[:n - s], u[n - s:]], axis=0)
        b = jnp.concatenate([b[:n - s] * b[s:], b[n - s:]], axis=0)
        s *= 2
    return b, u


LANES = 128


def _chunk_scan(a, u, a_s, u_s, hl_s, al_s, carry, reverse):
    n, width = a.shape
    groups = n // SUBLANES
    order = range(SUBLANES - 1, -1, -1) if reverse else range(SUBLANES)
    row = lax.broadcasted_iota(jnp.int32, (groups, LANES), 0)
    for l in range(width // LANES):
        lanes = slice(l * LANES, (l + 1) * LANES)
        a_l, u_l, hl_l, al_l = a_s.at[l], u_s.at[l], hl_s.at[l], al_s.at[l]
        a_l[...] = a[:, lanes]
        u_l[...] = u[:, lanes]
        h_loc = a_loc = None
        for r in order:
            rows = pl.ds(r, groups, stride=SUBLANES)
            a_r, u_r = a_l[rows, :], u_l[rows, :]
            h_loc, a_loc = (u_r, a_r) if h_loc is None else (a_r * h_loc + u_r, a_r * a_loc)
            hl_l[rows, :] = h_loc
            al_l[rows, :] = a_loc
        if reverse:
            a_cum, ends = _scan_up(a_loc, h_loc, row)
            ends = ends + a_cum * carry[:, lanes]
            enters = jnp.where(row == groups - 1, carry[:, lanes], pltpu.roll(ends, groups - 1, 0))
        else:
            a_cum, ends = _scan_down(a_loc, h_loc, row)
            ends = ends + a_cum * carry[:, lanes]
            enters = jnp.where(row == 0, carry[:, lanes], pltpu.roll(ends, 1, 0))
        for r in range(SUBLANES):
            rows = pl.ds(r, groups, stride=SUBLANES)
            hl_l[rows, :] = hl_l[rows, :] + al_l[rows, :] * enters
    return jnp.concatenate([hl_s[l] for l in range(width // LANES)], axis=1)


def _rnn_fwd(proj, conv_w, conv_b, w_a, w_x, b_a, b_x, lam, token):
    T = proj.shape[0]
    tc, ct = RNN_CHUNK, RNN_TILE
    nt = T // tc

    def body(x_ref, rg_ref, cw_ref, cb_ref, wa_ref, wx_ref, ba_ref, bx_ref, lam_ref, token_ref, h_ref, z_ref, c_ref,
             zt_ref, xtail, hcarry, wa_s, wx_s, a_s, u_s, hl_s, al_s):
        @pl.when(pl.program_id(1) == 0)
        def _():
            xtail[...] = jnp.zeros_like(xtail)
            hcarry[...] = jnp.zeros_like(hcarry)
            _fill_block_diag(wa_s, wa_ref)
            _fill_block_diag(wx_s, wx_ref)

        row = lax.broadcasted_iota(jnp.int32, (tc, ct), 0)
        x = x_ref[...].astype(F32)
        taps = _conv_taps(x, xtail[...], row)
        c = cb_ref[...] + cw_ref[pl.ds(0, 1), :] * taps[0]
        for k in range(1, CONV_W):
            c = c + cw_ref[pl.ds(k, 1), :] * taps[k]
        xtail[...] = x[tc - 8:, :]
        c_ref[...] = c
        _, _, i, a, mult, _ = _rglru_gates(c, wa_s[...], wx_s[...], ba_ref[...], bx_ref[...], lam_ref[...])
        h = _chunk_scan(a, mult * (i * c), a_s, u_s, hl_s, al_s, hcarry[...], reverse=False)
        h_ref[...] = h
        hcarry[...] = h_ref[pl.ds(tc - 1, 1), :]
        rg = rg_ref[...].astype(F32)
        z = h * (rg * _sigmoid(rg))
        z_ref[...] = z.astype(BF16)
        zt_ref[...] = z.T.astype(BF16)

    col = lambda off: (lambda j, t: (t, off + j))
    vec = pl.BlockSpec((1, ct), lambda j, t: (0, j))
    return pl.pallas_call(
        body,
        name="rnn_fwd",
        grid=(D_RNN // ct, nt),
        in_specs=[
            pl.BlockSpec((tc, ct), col(COL_RNN_X)),
            pl.BlockSpec((tc, ct), col(COL_RNN_GATE)),
            pl.BlockSpec((CONV_W, ct), lambda j, t: (0, j)),
            vec, GATE_BLOCKS, GATE_BLOCKS, vec, vec, vec,
            pl.BlockSpec((8, 128), lambda j, t: (0, 0)),
        ],
        out_specs=[pl.BlockSpec((tc, ct), lambda j, t: (t, j))] * 3 + [pl.BlockSpec((ct, tc), lambda j, t: (j, t))],
        out_shape=[_sds((T, D_RNN), F32), _sds((T, D_RNN), BF16), _sds((T, D_RNN), F32), _sds((D_RNN, T), BF16)],
        scratch_shapes=[pltpu.VMEM((8, ct), F32), pltpu.VMEM((1, ct), F32)] + [pltpu.VMEM((ct, ct), BF16)] * 2 + [
            pltpu.VMEM((ct // LANES, tc, LANES), F32)] * 4,
        compiler_params=_params(("parallel", "arbitrary"), 32),
    )(*_hbm(proj, proj, conv_w, conv_b, w_a, w_x, b_a, b_x, lam, token))


def _rnn_bwd(proj, conv, y_rnn, dz_rnn, conv_w, w_a, w_x, b_a, b_x, lam):
    T = proj.shape[0]
    tc, ct = RNN_CHUNK, RNN_TILE
    nt = T // tc
    hb = tc // 8

    def body(x_ref, c_ref, rg_ref, h_ref, hh_ref, dz_ref, cw_ref, wa_ref, wx_ref, ba_ref, bx_ref, lam_ref,
             dx_ref, drg_ref, dwa_ref, dwx_ref, sm_ref, lam_carry, a_carry, dc_head, wa_s, wx_s, dwa_s, dwx_s,
             b_s, dy_s, hl_s, al_s):
        t = pl.program_id(1)
        first_chunk = t == nt - 1

        @pl.when(t == 0)
        def _():
            lam_carry[...] = jnp.zeros_like(lam_carry)
            a_carry[...] = jnp.zeros_like(a_carry)
            dc_head[...] = jnp.zeros_like(dc_head)
            dwa_s[...] = jnp.zeros_like(dwa_s)
            dwx_s[...] = jnp.zeros_like(dwx_s)
            sm_ref[...] = jnp.zeros_like(sm_ref)
            _fill_block_diag(wa_s, wa_ref)
            _fill_block_diag(wx_s, wx_ref)

        row = lax.broadcasted_iota(jnp.int32, (tc, ct), 0)
        keep = jnp.where(first_chunk, 0.0, 1.0)
        x = x_ref[...].astype(F32)
        c = c_ref[...]
        lam = lam_ref[...]
        cb, r, i, a, mult, inv_mult = _rglru_gates(c, wa_s[...], wx_s[...], ba_ref[...], bx_ref[...], lam)
        h = h_ref[...]
        h_prev = _shift_down(h, hh_ref[...] * keep, 1, row)
        rg = rg_ref[...].astype(F32)
        dz = dz_ref[...]
        sg = _sigmoid(rg)
        drg_ref[...] = (dz * h * (sg * (1.0 + rg * (1.0 - sg)))).astype(BF16)
        dy = dz * (rg * sg)
        b = jnp.where(row >= tc - 1, a_carry[pl.ds(0, 1), :], pltpu.roll(a, tc - 1, 0))
        lt = _chunk_scan(b, dy, b_s, dy_s, hl_s, al_s, lam_carry[pl.ds(0, 1), :], reverse=True)
        lam_carry[...] = lt[0:8, :]
        a_carry[...] = a[0:8, :]
        ic = i * c
        dmult = lt * ic
        di = lt * mult * c
        dc = lt * mult * i
        dlog_a = a * (lt * h_prev - dmult * a * inv_mult)
        sp = _softplus(-lam)
        dpre_r = dlog_a * ((-LRU_C) * sp) * (r * (1.0 - r))
        dpre_i = di * (i * (1.0 - i))
        dlam_row = jnp.sum(dlog_a * r, axis=0, keepdims=True) * (LRU_C * _sigmoid(-lam))
        dpr_b = dpre_r.astype(BF16)
        dpi_b = dpre_i.astype(BF16)
        dwa_s[...] += _dot_tn(cb, dpr_b)
        dwx_s[...] += _dot_tn(cb, dpi_b)
        dc = dc + _dot_nt(dpr_b, wa_s[...]) + _dot_nt(dpi_b, wx_s[...])
        head = dc_head[...]
        dx = cw_ref[pl.ds(3, 1), :] * dc
        sm_ref[pl.ds(4 + 3, 1), :] += jnp.sum(dc * x, axis=0, keepdims=True)
        for m in range(1, CONV_W):
            up = _shift_up(dc, head, m, row)
            dx = dx + cw_ref[pl.ds(3 - m, 1), :] * up
            sm_ref[pl.ds(4 + 3 - m, 1), :] += jnp.sum(up * x, axis=0, keepdims=True)
        dx_ref[...] = dx.astype(BF16)
        dc_head[...] = dc[0:8, :]
        sm_ref[pl.ds(0, 1), :] += jnp.sum(dpre_r, axis=0, keepdims=True)
        sm_ref[pl.ds(1, 1), :] += jnp.sum(dpre_i, axis=0, keepdims=True)
        sm_ref[pl.ds(2, 1), :] += dlam_row
        sm_ref[pl.ds(3, 1), :] += jnp.sum(dc, axis=0, keepdims=True)

        @pl.when(first_chunk)
        def _():
            for k in range(GATE_BLOCKS_PER_TILE):
                lo = k * RNN_BLOCK_W
                dwa_ref[k] = dwa_s[lo:lo + RNN_BLOCK_W, lo:lo + RNN_BLOCK_W]
                dwx_ref[k] = dwx_s[lo:lo + RNN_BLOCK_W, lo:lo + RNN_BLOCK_W]

    rev = lambda off: (lambda j, t: (nt - 1 - t, off + j))
    halo = lambda off: (lambda j, t: (jnp.maximum((nt - 1 - t) * hb - 1, 0), off + j))
    vec = pl.BlockSpec((1, ct), lambda j, t: (0, j))
    mat = GATE_BLOCKS
    return pl.pallas_call(
        body,
        name="rnn_bwd",
        grid=(D_RNN // ct, nt),
        in_specs=[
            pl.BlockSpec((tc, ct), rev(COL_RNN_X)),
            pl.BlockSpec((tc, ct), rev(0)),
            pl.BlockSpec((tc, ct), rev(COL_RNN_GATE)),
            pl.BlockSpec((tc, ct), rev(0)),
            pl.BlockSpec((8, ct), halo(0)),
            pl.BlockSpec((tc, ct), rev(0)),
            pl.BlockSpec((CONV_W, ct), lambda j, t: (0, j)),
            mat, mat, vec, vec, vec,
        ],
        out_specs=[
            pl.BlockSpec((tc, ct), rev(0)),
            pl.BlockSpec((tc, ct), rev(0)),
            mat, mat,
            pl.BlockSpec((8, ct), lambda j, t: (0, j)),
        ],
        out_shape=[_sds((T, D_RNN), BF16), _sds((T, D_RNN), BF16), _sds(w_a.shape, F32), _sds(w_x.shape, F32),
                   _sds((8, D_RNN), F32)],
        scratch_shapes=[pltpu.VMEM((8, ct), F32)] * 3 + [pltpu.VMEM((ct, ct), BF16)] * 2 + [
            pltpu.VMEM((ct, ct), F32)] * 2 + [pltpu.VMEM((ct // LANES, tc, LANES), F32)] * 4,
        compiler_params=_params(("parallel", "arbitrary"), 32),
    )(*_hbm(proj, conv, proj, y_rnn, y_rnn, dz_rnn, conv_w, w_a, w_x, b_a, b_x, lam))


def _attn_bias():
    qi = np.arange(BLOCK)[:, None]
    kj = np.arange(BLOCK)[None, :]
    dist_cur = (qi - kj).astype(np.float32)
    slopes = np.float32(2.0) ** (-ALIBI_MAX_BIAS * np.arange(1, N_Q_HEADS + 1, dtype=np.float32) / N_Q_HEADS)
    slopes = slopes[:, None, None]
    prev = np.where(kj > qi, -slopes * (dist_cur + np.float32(BLOCK)), np.float32(NEG_BIG))
    cur = np.where(kj <= qi, -slopes * dist_cur, np.float32(NEG_BIG))
    later = np.concatenate([prev, cur], axis=-1)
    first = np.concatenate([np.full_like(prev, NEG_BIG), cur], axis=-1)
    return jnp.asarray(np.stack([first, later]).astype(np.float32))


def _attn_exps(s_prev, s_cur, sink, bias):
    s_prev = s_prev + bias[:, 0:BLOCK]
    s_cur = s_cur + bias[:, BLOCK:2 * BLOCK]
    m = jnp.maximum(jnp.max(jnp.maximum(s_prev, s_cur), axis=-1, keepdims=True), sink)
    p_prev = jnp.exp(s_prev - m)
    p_cur = jnp.exp(s_cur - m)
    total = jnp.sum(p_prev + p_cur, axis=-1, keepdims=True) + jnp.exp(sink - m)
    return p_prev, p_cur, 1.0 / total, m + jnp.log(total)


def _attn_probs(s_prev, s_cur, sink, bias, lse):
    p_prev = jnp.exp((s_prev + bias[:, 0:BLOCK]) - lse)
    p_cur = jnp.exp((s_cur + bias[:, BLOCK:2 * BLOCK]) - lse)
    return p_prev, p_cur, jnp.exp(sink - lse)


def _stack_heads(ref_or_val, hk, dtype):
    parts = [ref_or_val[:, (GROUP * hk + g) * HEAD_DIM:(GROUP * hk + g + 1) * HEAD_DIM] for g in range(GROUP)]
    return jnp.concatenate(parts, axis=0).astype(dtype)


ATTN_SCALE = HEAD_DIM ** -0.5


def _bias_spec():
    return pl.BlockSpec((None, N_Q_HEADS, BLOCK, 2 * BLOCK), lambda i: (jnp.minimum(i, 1), 0, 0, 0))


def _attn_fwd(proj, sinks, bias):
    T = proj.shape[0]
    nb = T // BLOCK

    def body(sink_ref, bias_ref, q_ref, kp_ref, kc_ref, vp_ref, vc_ref, ag0_ref, ag1_ref, y_ref, z_ref, lse_ref):
        kvs = [slice(hk * HEAD_DIM, (hk + 1) * HEAD_DIM) for hk in range(N_KV_HEADS)]
        qgs = [(_stack_heads(q_ref, hk, F32) * ATTN_SCALE).astype(BF16) for hk in range(N_KV_HEADS)]
        s_prev = [_dot_nt(qgs[hk], kp_ref[:, kvs[hk]].astype(BF16)) for hk in range(N_KV_HEADS)]
        s_cur = [_dot_nt(qgs[hk], kc_ref[:, kvs[hk]].astype(BF16)) for hk in range(N_KV_HEADS)]
        for hk in range(N_KV_HEADS):
            pp, pc, invs = [], [], []
            for g in range(GROUP):
                h = GROUP * hk + g
                rows = slice(g * BLOCK, (g + 1) * BLOCK)
                p_prev, p_cur, inv, lse = _attn_exps(s_prev[hk][rows], s_cur[hk][rows], sink_ref[h], bias_ref[h])
                pp.append(p_prev.astype(BF16))
                pc.append(p_cur.astype(BF16))
                invs.append(inv)
                lse_ref[:, h:h + 1] = lse
            og = _dot(jnp.concatenate(pp, axis=0), vp_ref[:, kvs[hk]].astype(BF16)) + _dot(
                jnp.concatenate(pc, axis=0), vc_ref[:, kvs[hk]].astype(BF16))
            for g in range(GROUP):
                h = GROUP * hk + g
                y_ref[:, h * HEAD_DIM:(h + 1) * HEAD_DIM] = og[g * BLOCK:(g + 1) * BLOCK] * invs[g]
        ag = jnp.concatenate([ag0_ref[...], ag1_ref[...]], axis=1).astype(F32)
        z_ref[...] = (y_ref[...] * (ag * _sigmoid(ag))).astype(BF16)

    prev = lambda c: (lambda i: (jnp.maximum(i - 1, 0), c))
    cur = lambda c: (lambda i: (i, c))
    return pl.pallas_call(
        body,
        name="attn_fwd",
        grid=(nb,),
        in_specs=[
            pl.BlockSpec(memory_space=pltpu.SMEM),
            _bias_spec(),
            pl.BlockSpec((BLOCK, 1024), lambda i: (i, COL_Q // 4)),
            pl.BlockSpec((BLOCK, D_KV), prev(COL_K)),
            pl.BlockSpec((BLOCK, D_KV), cur(COL_K)),
            pl.BlockSpec((BLOCK, D_KV), prev(COL_V)),
            pl.BlockSpec((BLOCK, D_KV), cur(COL_V)),
            pl.BlockSpec((BLOCK, 512), lambda i: (i, COL_ATTN_GATE // 2)),
            pl.BlockSpec((BLOCK, 512), lambda i: (i, COL_ATTN_GATE // 2 + 1)),
        ],
        out_specs=[pl.BlockSpec((BLOCK, 1024), lambda i: (i, 0)), pl.BlockSpec((BLOCK, 1024), lambda i: (i, 0)),
                   pl.BlockSpec((BLOCK, N_Q_HEADS), lambda i: (i, 0))],
        out_shape=[_sds((T, 1024), F32), _sds((T, 1024), BF16), _sds((T, N_Q_HEADS), F32)],
        compiler_params=_params(("arbitrary",), 32),
    )(sinks, *_hbm(bias, proj, proj, proj, proj, proj, proj, proj))


def _attn_bwd(proj, y_attn, lse, dz_attn, sinks, bias, token):
    T = proj.shape[0]
    nb = T // BLOCK

    def body(sink_ref, bias_ref, q_ref, kp_ref, kc_ref, vp_ref, vc_ref, ag0_ref, ag1_ref, y_ref, lse_ref, dz_ref,
             token_ref, dq_ref, dk_ref, dv_ref, dag_ref, ds_ref, dy_s):
        i = pl.program_id(0)

        @pl.when(i == 0)
        def _():
            ds_ref[...] = jnp.zeros_like(ds_ref)

        lane = lax.broadcasted_iota(jnp.int32, (8, 128), 1)
        sub = lax.broadcasted_iota(jnp.int32, (8, 128), 0)
        ag = jnp.concatenate([ag0_ref[...], ag1_ref[...]], axis=1).astype(F32)
        dz = dz_ref[...]
        sg = _sigmoid(ag)
        dag_ref[...] = (dz * y_ref[...] * (sg * (1.0 + ag * (1.0 - sg)))).astype(BF16)
        dy_s[...] = dz * (ag * sg)
        r_cur = pl.multiple_of(i * BLOCK, BLOCK)
        r_prev = pl.multiple_of(jnp.maximum(i - 1, 0) * BLOCK, BLOCK)
        dk_cur, dv_cur, dk_prev, dv_prev = [], [], [], []
        ds_acc = jnp.zeros((8, 128), F32)
        for hk in range(N_KV_HEADS):
            ks = slice(hk * HEAD_DIM, (hk + 1) * HEAD_DIM)
            qg = (_stack_heads(q_ref, hk, F32) * ATTN_SCALE).astype(BF16)
            dog = _stack_heads(dy_s, hk, F32)
            og = _stack_heads(y_ref, hk, F32)
            dog_b = dog.astype(BF16)
            kp = kp_ref[:, ks].astype(BF16)
            kc = kc_ref[:, ks].astype(BF16)
            vp = vp_ref[:, ks].astype(BF16)
            vc = vc_ref[:, ks].astype(BF16)
            s_prev = _dot_nt(qg, kp)
            s_cur = _dot_nt(qg, kc)
            dp_prev = _dot_nt(dog_b, vp)
            dp_cur = _dot_nt(dog_b, vc)
            dvec = jnp.sum(dog * og, axis=-1, keepdims=True)
            pp, pc, dsp, dsc = [], [], [], []
            for g in range(GROUP):
                h = GROUP * hk + g
                rows = slice(g * BLOCK, (g + 1) * BLOCK)
                p_prev, p_cur, p_sink = _attn_probs(
                    s_prev[rows], s_cur[rows], sink_ref[h], bias_ref[h], lse_ref[:, h:h + 1])
                d_h = dvec[rows]
                pp.append(p_prev.astype(BF16))
                pc.append(p_cur.astype(BF16))
                dsp.append((p_prev * (dp_prev[rows] - d_h)).astype(BF16))
                dsc.append((p_cur * (dp_cur[rows] - d_h)).astype(BF16))
                dsink = -jnp.sum(p_sink * d_h, axis=0, keepdims=True)
                ds_acc = ds_acc + jnp.where(jnp.logical_and(lane == h, sub == 1), dsink, 0.0)
            pp = jnp.concatenate(pp, axis=0)
            pc = jnp.concatenate(pc, axis=0)
            dsp = jnp.concatenate(dsp, axis=0)
            dsc = jnp.concatenate(dsc, axis=0)
            dqg = (_dot(dsp, kp) + _dot(dsc, kc)) * ATTN_SCALE
            for g in range(GROUP):
                h = GROUP * hk + g
                dq_ref[:, h * HEAD_DIM:(h + 1) * HEAD_DIM] = dqg[g * BLOCK:(g + 1) * BLOCK].astype(BF16)
            dk_ref[pl.ds(r_cur, BLOCK), ks] = _dot_tn(dsc, qg)
            dv_ref[pl.ds(r_cur, BLOCK), ks] = _dot_tn(pc, dog_b)
            dk_prev.append(_dot_tn(dsp, qg))
            dv_prev.append(_dot_tn(pp, dog_b))
        ds_ref[:, 0:128] += ds_acc

        @pl.when(i > 0)
        def _():
            for hk in range(N_KV_HEADS):
                ks = slice(hk * HEAD_DIM, (hk + 1) * HEAD_DIM)
                dk_ref[pl.ds(r_prev, BLOCK), ks] += dk_prev[hk]
                dv_ref[pl.ds(r_prev, BLOCK), ks] += dv_prev[hk]

    prev = lambda c: (lambda i: (jnp.maximum(i - 1, 0), c))
    cur = lambda c: (lambda i: (i, c))
    blk = pl.BlockSpec((BLOCK, 1024), lambda i: (i, 0))
    whole = pl.BlockSpec((T, D_KV), lambda i: (0, 0))
    return pl.pallas_call(
        body,
        name="attn_bwd",
        grid=(nb,),
        in_specs=[
            pl.BlockSpec(memory_space=pltpu.SMEM),
            _bias_spec(),
            pl.BlockSpec((BLOCK, 1024), lambda i: (i, COL_Q // 4)),
            pl.BlockSpec((BLOCK, D_KV), prev(COL_K)),
            pl.BlockSpec((BLOCK, D_KV), cur(COL_K)),
            pl.BlockSpec((BLOCK, D_KV), prev(COL_V)),
            pl.BlockSpec((BLOCK, D_KV), cur(COL_V)),
            pl.BlockSpec((BLOCK, 512), lambda i: (i, COL_ATTN_GATE // 2)),
            pl.BlockSpec((BLOCK, 512), lambda i: (i, COL_ATTN_GATE // 2 + 1)),
            blk,
            pl.BlockSpec((BLOCK, N_Q_HEADS), lambda i: (i, 0)),
            blk,
            pl.BlockSpec((8, 128), lambda i: (0, 0)),
        ],
        out_specs=[blk, whole, whole, blk, pl.BlockSpec((8, 1024), lambda i: (0, 0))],
        out_shape=[_sds((T, 1024), BF16), _sds((T, D_KV), F32), _sds((T, D_KV), F32), _sds((T, 1024), BF16),
                   _sds((8, 1024), F32)],
        scratch_shapes=[pltpu.VMEM((BLOCK, 1024), F32)],
        compiler_params=_params(("arbitrary",), 48),
    )(sinks, *_hbm(bias, proj, proj, proj, proj, proj, proj, proj, y_attn, lse, dz_attn, token))


def _head(x, target, z_rnn, z_attn, proj, b_gate, g_post, w_rnn_out, w_attn_out, w_out):
    T = x.shape[0]
    tm = 256

    def body(x_ref, t_ref, zr_ref, za_ref, ml0_ref, ml1_ref, ml2_ref, ml3_ref, bg_ref, gp_ref, wr_ref, wa_ref, wo_ref,
             dyx_ref, dzr_ref, dza_ref, dml_ref, dout_ref, dbr_ref, dba_ref, mt_ref, zat_ref, sm_ref):
        @pl.when(pl.program_id(0) == 0)
        def _():
            sm_ref[...] = jnp.zeros_like(sm_ref)

        wr, wa, wo = wr_ref[...], wa_ref[...], wo_ref[...]
        br_rnn = _dot(zr_ref[...], wr)
        br_attn = _dot(za_ref[...], wa)
        zat_ref[...] = za_ref[...].astype(F32).T.astype(BF16)
        ml_rnn = jnp.concatenate([ml0_ref[...], ml1_ref[...]], axis=1).astype(F32)
        ml_attn = jnp.concatenate([ml2_ref[...], ml3_ref[...]], axis=1).astype(F32)
        g_rnn = _sigmoid(ml_rnn + bg_ref[:, 0:D_MODEL])
        g_attn = _sigmoid(ml_attn + bg_ref[:, D_MODEL:2 * D_MODEL])
        merged = g_rnn * br_rnn + g_attn * br_attn
        mb = merged.astype(BF16)
        mt_ref[...] = merged.T.astype(BF16)
        out = _dot(mb, wo)
        rstd = lax.rsqrt(jnp.mean(out * out, axis=-1, keepdims=True) + EPS)
        n = out * rstd
        gp = gp_ref[...]
        err = (x_ref[...] + n * gp) - t_ref[...]
        sm_ref[pl.ds(3, 1), :] += 0.5 * jnp.sum(jnp.mean(err * err, axis=-1, keepdims=True), axis=0, keepdims=True)
        dy = err * (1.0 / D_MODEL)
        dyx_ref[...] = dy
        sm_ref[pl.ds(0, 1), :] += jnp.sum(dy * n, axis=0, keepdims=True)
        dn = dy * gp
        dout = (rstd * (dn - n * jnp.mean(dn * n, axis=-1, keepdims=True))).astype(BF16)
        dout_ref[...] = dout
        dmerged = _dot_nt(dout, wo)
        dml_r = (dmerged * br_rnn) * (g_rnn * (1.0 - g_rnn))
        dml_a = (dmerged * br_attn) * (g_attn * (1.0 - g_attn))
        dml_ref[:, 0:D_MODEL] = dml_r.astype(BF16)
        dml_ref[:, D_MODEL:2 * D_MODEL] = dml_a.astype(BF16)
        sm_ref[pl.ds(1, 1), :] += jnp.sum(dml_r, axis=0, keepdims=True)
        sm_ref[pl.ds(2, 1), :] += jnp.sum(dml_a, axis=0, keepdims=True)
        dbr = (dmerged * g_rnn).astype(BF16)
        dba = (dmerged * g_attn).astype(BF16)
        dbr_ref[...] = dbr
        dba_ref[...] = dba
        dzr_ref[...] = _dot_nt(dbr, wr)
        dza_ref[...] = _dot_nt(dba, wa)

    tile = pl.BlockSpec((tm, D_MODEL), lambda i: (i, 0))
    wspec = pl.BlockSpec((D_MODEL, D_MODEL), lambda i: (0, 0))
    ml = lambda q: pl.BlockSpec((tm, 512), lambda i: (i, COL_MERGE // 2 + q))
    return pl.pallas_call(
        body,
        name="head",
        grid=(T // tm,),
        in_specs=[
            tile, tile, tile, tile,
            ml(0), ml(1), ml(2), ml(3),
            pl.BlockSpec((1, 2 * D_MODEL), lambda i: (0, 0)),
            pl.BlockSpec((1, D_MODEL), lambda i: (0, 0)),
            wspec, wspec, wspec,
        ],
        out_specs=[
            tile, tile, tile,
            pl.BlockSpec((tm, 2 * D_MODEL), lambda i: (i, 0)),
            tile, tile, tile,
            pl.BlockSpec((D_MODEL, tm), lambda i: (0, i)), pl.BlockSpec((D_MODEL, tm), lambda i: (0, i)),
            pl.BlockSpec((8, D_MODEL), lambda i: (0, 0)),
        ],
        out_shape=[
            _sds((T, D_MODEL), F32), _sds((T, D_MODEL), F32), _sds((T, D_MODEL), F32),
            _sds((T, 2 * D_MODEL), BF16),
            _sds((T, D_MODEL), BF16), _sds((T, D_MODEL), BF16), _sds((T, D_MODEL), BF16),
            _sds((D_MODEL, T), BF16), _sds((D_MODEL, T), BF16),
            _sds((8, D_MODEL), F32),
        ],
        compiler_params=_params(("arbitrary",), 56),
    )(*_hbm(x, target, z_rnn, z_attn, proj, proj, proj, proj, b_gate, g_post, w_rnn_out, w_attn_out, w_out))


def _matmul_t(at, b, name):
    M, T = at.shape
    N = b.shape[1]
    tk = min(1024, T)
    nt = T // tk

    def body(a_ref, b_ref, o_ref, ob_ref):
        @pl.when(pl.program_id(0) == 0)
        def _():
            o_ref[...] = jnp.zeros_like(o_ref)

        o_ref[...] += _dot(a_ref[...], b_ref[...])

        @pl.when(pl.program_id(0) == nt - 1)
        def _():
            ob_ref[...] = o_ref[...].astype(BF16)

    whole = pl.BlockSpec((M, N), lambda t: (0, 0))
    return pl.pallas_call(
        body,
        name=name,
        grid=(nt,),
        in_specs=[pl.BlockSpec((M, tk), lambda t: (0, t)), pl.BlockSpec((tk, N), lambda t: (t, 0))],
        out_specs=[whole, whole],
        out_shape=[_sds((M, N), F32), _sds((M, N), BF16)],
        compiler_params=_params(("arbitrary",), 48),
    )(*_hbm(at, b))


DPROJ_WIDTHS = (D_RNN, D_RNN, 1024, D_KV, D_KV, 1024, 2 * D_MODEL)


def _dproj_segments():
    segs, start = [[] for _ in range(N_CHIPS)], 0
    for p, width in enumerate(DPROJ_WIDTHS):
        for c in range(N_CHIPS):
            lo, hi = max(start, c * W_IN_SHARD), min(start + width, (c + 1) * W_IN_SHARD)
            if lo < hi:
                segs[c].append((p, lo - start, hi - start, lo - c * W_IN_SHARD, hi - c * W_IN_SHARD))
        start += width
    return segs


def _dh_bwd(pieces, w_in_g, x, dyx, g_pre, token):
    T = x.shape[0]
    tm = min(512, T)
    n = len(pieces)
    segs = _dproj_segments()

    def body(*refs):
        p_refs, w_hbm, x_ref, dyx_ref, g_ref = refs[0:n], refs[n], refs[n + 1], refs[n + 2], refs[n + 3]
        gx_ref, dg_ref, w_ref, w_sems = refs[n + 5], refs[n + 6], refs[n + 7], refs[n + 8]
        first = pl.program_id(0) == 0
        w_copies = [pltpu.make_async_copy(w_hbm.at[c], w_ref.at[c], w_sems.at[c]) for c in range(N_CHIPS)]

        @pl.when(first)
        def _():
            for cp in w_copies:
                cp.start()
            dg_ref[...] = jnp.zeros_like(dg_ref)

        dh = None
        for c in range(N_CHIPS):
            pl.when(first)(w_copies[c].wait)
            for p, a0, a1, u0, u1 in segs[c]:
                part = _dot_nt(p_refs[p][:, a0:a1].astype(BF16), w_ref[c, :, u0:u1])
                dh = part if dh is None else dh + part
        xv = x_ref[...]
        rstd = lax.rsqrt(jnp.mean(xv * xv, axis=-1, keepdims=True) + EPS)
        nx = xv * rstd
        dhg = dh * g_ref[...]
        gx_ref[...] = dyx_ref[...] + rstd * (dhg - nx * jnp.mean(dhg * nx, axis=-1, keepdims=True))
        dg_ref[pl.ds(0, 1), :] += jnp.sum(dh * nx, axis=0, keepdims=True)

    tile = pl.BlockSpec((tm, D_MODEL), lambda i: (i, 0))
    return pl.pallas_call(
        body,
        name="dh_bwd",
        grid=(T // tm,),
        in_specs=[pl.BlockSpec((tm, w), lambda i: (i, 0)) for w in DPROJ_WIDTHS] + [
            ANY, tile, tile,
            pl.BlockSpec((1, D_MODEL), lambda i: (0, 0)),
            pl.BlockSpec((8, 128), lambda i: (0, 0)),
        ],
        out_specs=[tile, pl.BlockSpec((8, D_MODEL), lambda i: (0, 0))],
        out_shape=[_sds((T, D_MODEL), F32), _sds((8, D_MODEL), F32)],
        scratch_shapes=[pltpu.VMEM(w_in_g.shape, BF16), pltpu.SemaphoreType.DMA((N_CHIPS,))],
        compiler_params=_params(("arbitrary",), 56),
    )(*_hbm(*pieces, w_in_g, x, dyx, g_pre, token))


def _dw_in(ht, pieces):
    T = ht.shape[1]
    tk = min(1024, T)
    nt = T // tk
    n = len(pieces)
    segs = _dproj_segments()

    def body(*refs):
        h_ref, p_refs, o_ref, ob_ref = refs[0], refs[1:n + 1], refs[n + 1], refs[n + 2]

        @pl.when(pl.program_id(1) == 0)
        def _():
            o_ref[...] = jnp.zeros_like(o_ref)

        for c in range(N_CHIPS):
            @pl.when(pl.program_id(0) == c)
            def _():
                for p, a0, a1, u0, u1 in segs[c]:
                    o_ref[:, u0:u1] += _dot(h_ref[...], p_refs[p][:, a0:a1].astype(BF16))

        @pl.when(pl.program_id(1) == nt - 1)
        def _():
            ob_ref[...] = o_ref[...].astype(BF16)

    def piece_spec(p):
        chips = [c for c in range(N_CHIPS) if any(s[0] == p for s in segs[c])]

        def index(c, t):
            used = functools.reduce(jnp.logical_or, [c == k for k in chips])
            return (jnp.where(used, t, 0), 0)

        return pl.BlockSpec((tk, DPROJ_WIDTHS[p]), index)

    return pl.pallas_call(
        body,
        name="dw_in",
        grid=(N_CHIPS, nt),
        in_specs=[pl.BlockSpec((D_MODEL, tk), lambda c, t: (0, t))] + [piece_spec(p) for p in range(n)],
        out_specs=[pl.BlockSpec((None, D_MODEL, W_IN_SHARD), lambda c, t: (c, 0, 0))] * 2,
        out_shape=[_sds((N_CHIPS, D_MODEL, W_IN_SHARD), F32), _sds((N_CHIPS, D_MODEL, W_IN_SHARD), BF16)],
        compiler_params=_params(("parallel", "arbitrary"), 56),
    )(*_hbm(ht, *pieces))


ELEMENTWISE_TILE_BYTES = MIB


def _row_tile(rows, cols, limit=ELEMENTWISE_TILE_BYTES):
    if rows * cols * 4 <= limit:
        return rows
    for t in (512, 256, 128, 64, 32, 16, 8):
        if rows % t == 0 and t * cols * 4 <= limit:
            return t
    return rows


def _group_tiles(groups):
    tiles = [_row_tile(g[0].shape[0], g[0].shape[1] * len(g)) for g in groups]
    steps = max(g[0].shape[0] // t for g, t in zip(groups, tiles))
    return steps, [g[0].shape[0] // steps for g in groups]


def _chip_sum(groups, chip_core, name):
    ps = [p for group_ps, _ in groups for p in group_ps]
    gots = [g for _, group_gots in groups for g in group_gots]
    n = len(ps)
    steps, group_rows = _group_tiles([group_ps for group_ps, _ in groups])
    rows = [tr for (group_ps, _), tr in zip(groups, group_rows) for _ in group_ps]

    def body(jc_ref, *refs):
        for a in range(n):
            p_ref, g0_ref, g1_ref, g2_ref, o_ref = refs[a], refs[n + 3 * a], refs[n + 3 * a + 1], refs[n + 3 * a + 2], \
                refs[4 * n + a]
            o_ref[...] = ((p_ref[...] + g0_ref[...].astype(F32)) + g1_ref[...].astype(F32)) + g2_ref[...].astype(F32)

    tile = lambda p, tr: pl.BlockSpec((tr, p.shape[1]), lambda i, jc_ref: (i, 0))
    rel = lambda p, tr, r: pl.BlockSpec((None, tr, p.shape[1]), lambda i, jc_ref: (r, i, 0))
    half = lambda p, tr: pl.BlockSpec((tr, p.shape[1]), lambda i, jc_ref: (jc_ref[1] * steps + i, 0))
    outs = pl.pallas_call(
        body,
        name=name,
        grid_spec=pltpu.PrefetchScalarGridSpec(
            num_scalar_prefetch=1,
            grid=(steps,),
            in_specs=[tile(p, tr) for p, tr in zip(ps, rows)] + [
                rel(p, tr, r) for p, tr in zip(ps, rows) for r in range(3)],
            out_specs=[half(p, tr) for p, tr in zip(ps, rows)],
        ),
        out_shape=[_sds((2 * p.shape[0], p.shape[1]), F32) for p in ps],
        compiler_params=_params(("parallel",), 48),
    )(chip_core, *_hbm(*ps, *[g for got in gots for g in (got, got, got)]))
    return list(outs)


def _place_shards(shards, chip, name):
    n = len(shards)
    tiles = [_row_tile(s.shape[0], s.shape[1]) for s in shards]
    steps = max(s.shape[0] // t for s, t in zip(shards, tiles))
    tiles = [s.shape[0] // steps for s in shards]

    def body(j_ref, *refs):
        for a in range(n):
            refs[n + a][...] = refs[a][...].astype(BF16)

    return pl.pallas_call(
        body,
        name=name,
        grid_spec=pltpu.PrefetchScalarGridSpec(
            num_scalar_prefetch=1,
            grid=(steps,),
            in_specs=[pl.BlockSpec((t, s.shape[1]), lambda i, j_ref: (i, 0)) for s, t in zip(shards, tiles)],
            out_specs=[pl.BlockSpec((None, t, s.shape[1]), lambda i, j_ref: (j_ref[0], i, 0))
                       for s, t in zip(shards, tiles)],
        ),
        out_shape=[_sds((N_CHIPS,) + s.shape, BF16) for s in shards],
        compiler_params=_params(("parallel",), 48),
    )(chip, *_hbm(*shards))


def _adamw_update(w, g, m, v):
    c1 = 1.0 - ADAM_B1 ** ADAM_STEP
    c2 = 1.0 - ADAM_B2 ** ADAM_STEP
    nm = ADAM_B1 * m + (1.0 - ADAM_B1) * g
    nv = ADAM_B2 * v + (1.0 - ADAM_B2) * (g * g)
    return (-ADAM_LR) * ((nm / c1) / (jnp.sqrt(nv / c2) + ADAM_EPS) + ADAM_WD * w), nm, nv


def _adamw(groups, name):
    params = [p for group in groups for p in group]
    n = len(params)
    steps, group_rows = _group_tiles([[p[0] for p in group] for group in groups])
    rows = [tr for group, tr in zip(groups, group_rows) for _ in group]

    def body(*refs):
        for a in range(n):
            w_ref, g_ref, m_ref, v_ref = refs[4 * a:4 * a + 4]
            d_ref, nm_ref, nv_ref, go_ref = refs[4 * n + 4 * a:4 * n + 4 * a + 4]
            g = g_ref[...]
            d_ref[...], nm_ref[...], nv_ref[...] = _adamw_update(w_ref[...], g, m_ref[...], v_ref[...])
            go_ref[...] = g

    specs = [pl.BlockSpec((tr, p[0].shape[1]), lambda i: (i, 0)) for p, tr in zip(params, rows) for _ in range(4)]
    outs = pl.pallas_call(
        body, name=name, grid=(steps,), in_specs=specs, out_specs=specs,
        out_shape=[_sds(p[0].shape, F32) for p in params for _ in range(4)],
        compiler_params=_params(("parallel",), 48),
    )(*_hbm(*[t for p in params for t in p]))
    return [tuple(outs[4 * a:4 * a + 4]) for a in range(n)]


def _adamw_whole(params, rows_params, packed, name):
    n, k = len(params), len(rows_params)

    def body(*refs):
        ins, packed_ref, outs = refs[0:4 * n + 3 * k], refs[4 * n + 3 * k], refs[4 * n + 3 * k + 1:]
        for a in range(n):
            w_ref, g_ref, m_ref, v_ref = ins[4 * a:4 * a + 4]
            d_ref, nm_ref, nv_ref = outs[3 * a:3 * a + 3]
            d_ref[...], nm_ref[...], nv_ref[...] = _adamw_update(w_ref[...], g_ref[...], m_ref[...], v_ref[...])
        for b, (_, (first, rows, lanes), _, _) in enumerate(rows_params):
            w_ref, m_ref, v_ref = ins[4 * n + 3 * b:4 * n + 3 * b + 3]
            d_ref, nm_ref, nv_ref, g_ref = outs[3 * n + 4 * b:3 * n + 4 * b + 4]
            if len(g_ref.shape) == 3:
                pieces, width = g_ref.shape[1:]
                for q in range(pieces):
                    g_ref[:, q:q + 1, :] = packed_ref[pl.ds(first, 1), q * width:(q + 1) * width][None]
            else:
                for r in range(rows):
                    g_ref[:, r * lanes:(r + 1) * lanes] = packed_ref[pl.ds(first + r, 1), 0:lanes]
            d_ref[...], nm_ref[...], nv_ref[...] = _adamw_update(w_ref[...], g_ref[...], m_ref[...], v_ref[...])

    def whole(t):
        return pl.BlockSpec(t.shape, lambda i: (0,) * t.ndim)

    flat = [t for p in params for t in p] + [t for w, _, m, v in rows_params for t in (w, m, v)] + [packed]
    like = [p[0] for p in params for _ in range(3)] + [p[0] for p in rows_params for _ in range(4)]
    outs = pl.pallas_call(
        body, name=name, grid=(1,), in_specs=[whole(t) for t in flat], out_specs=[whole(t) for t in like],
        out_shape=[_sds(t.shape, F32) for t in like], compiler_params=_params(("arbitrary",), 48),
    )(*_hbm(*flat))
    return [tuple(outs[3 * a:3 * a + 3]) for a in range(n)] + [
        tuple(outs[3 * n + 4 * b:3 * n + 4 * b + 4]) for b in range(k)]


def _place():
    return lax.axis_index("x"), lax.axis_index("y"), lax.axis_index("c")


def _chip_of(x, y, r):
    return (x ^ (r >> 1), y ^ (r & 1))


ANY = pl.BlockSpec(memory_space=pl.ANY)


def _gather_weights(placed, cw8):
    nbig = len(placed)
    halves = [s.shape[1] // 2 for s in placed]
    pieces = [max(1, h // 64) for h in halves]
    rows = [h // p for h, p in zip(halves, pieces)]
    order = [(a, q) for q in range(max(pieces)) for a in range(nbig) if q < pieces[a]]
    ici_sem = {(a, q, r): 3 * i + (r - 1) for i, (a, q) in enumerate(order) for r in (1, 2, 3)}
    cw_sem = {r: 3 * len(order) + (r - 1) for r in (1, 2, 3)}
    d2d_sem = {key: 3 * len(order) + 3 + k for key, k in ici_sem.items()}
    nsem = 6 * len(order) + 3

    def body(*refs):
        cw_ref, dsts, gcw_ref = refs[nbig], refs[nbig + 1:2 * nbig + 1], refs[2 * nbig + 1]
        send_sems, recv_sems = refs[2 * nbig + 2:]
        x, y, c = _place()
        j = 2 * x + y

        def piece_rows(a, q, core):
            return pl.ds(pl.multiple_of(core * halves[a] + q * rows[a], 16), rows[a])

        def ici(a, q, r):
            tx, ty = _chip_of(x, y, r)
            k = ici_sem[(a, q, r)]
            region = dsts[a].at[j, piece_rows(a, q, c), :]
            return pltpu.make_async_remote_copy(
                src_ref=region, dst_ref=region, send_sem=send_sems.at[k], recv_sem=recv_sems.at[k],
                device_id=(tx, ty, c), device_id_type=MESH)

        def ici_landed(a, q, r):
            tx, ty = _chip_of(x, y, r)
            k = ici_sem[(a, q, r)]
            region = dsts[a].at[2 * tx + ty, piece_rows(a, q, c), :]
            return pltpu.make_async_remote_copy(
                src_ref=region, dst_ref=region, send_sem=send_sems.at[k], recv_sem=recv_sems.at[k],
                device_id=(tx, ty, c), device_id_type=MESH)

        def d2d(a, q, r, core):
            tx, ty = _chip_of(x, y, r)
            k = d2d_sem[(a, q, r)]
            region = dsts[a].at[2 * tx + ty, piece_rows(a, q, core), :]
            return pltpu.make_async_remote_copy(
                src_ref=region, dst_ref=region, send_sem=send_sems.at[k], recv_sem=recv_sems.at[k],
                device_id=(x, y, 1 - c), device_id_type=MESH)

        def cw_copy(r):
            tx, ty = _chip_of(x, y, r)
            k = cw_sem[r]
            return pltpu.make_async_remote_copy(
                src_ref=cw_ref, dst_ref=gcw_ref.at[j], send_sem=send_sems.at[k], recv_sem=recv_sems.at[k],
                device_id=(tx, ty, c), device_id_type=MESH)

        def cw_landed(r):
            tx, ty = _chip_of(x, y, r)
            k = cw_sem[r]
            region = gcw_ref.at[2 * tx + ty]
            return pltpu.make_async_remote_copy(
                src_ref=region, dst_ref=region, send_sem=send_sems.at[k], recv_sem=recv_sems.at[k],
                device_id=(tx, ty, c), device_id_type=MESH)

        def relay(a, q, origin, to):
            ox, oy = _chip_of(x, y, origin)
            tx, ty = _chip_of(x, y, to)
            k = ici_sem[(a, q, 3)]
            region = dsts[a].at[2 * ox + oy, piece_rows(a, q, c), :]
            return pltpu.make_async_remote_copy(
                src_ref=region, dst_ref=region, send_sem=send_sems.at[k], recv_sem=recv_sems.at[k],
                device_id=(tx, ty, c), device_id_type=MESH)

        first = [ici(a, q, r) for (a, q) in order for r in (1, 2)] + [cw_copy(r) for r in (1, 2, 3)]
        for cp in first:
            cp.start()
        passed = []
        for (a, q) in order:
            for r in (1, 2):
                ici_landed(a, q, r).wait_recv()
                if q % 2 == r - 1:
                    cp = relay(a, q, r, 3 - r)
                    cp.start()
                    passed.append(cp)
                cp = d2d(a, q, r, c)
                cp.start()
                passed.append(cp)
        for (a, q) in order:
            ici_landed(a, q, 3).wait_recv()
            cp = d2d(a, q, 3, c)
            cp.start()
            passed.append(cp)
        for r in (1, 2, 3):
            cw_landed(r).wait_recv()
        for (a, q) in order:
            for r in (1, 2, 3):
                d2d(a, q, r, 1 - c).wait_recv()
        for cp in first + passed:
            cp.wait_send()

    return pl.pallas_call(
        body,
        name="gather_weights",
        in_specs=[ANY] * (nbig + 1),
        out_specs=[ANY] * (nbig + 1),
        out_shape=[_sds(s.shape, s.dtype) for s in placed] + [_sds((N_CHIPS,) + cw8.shape, cw8.dtype)],
        input_output_aliases={a: a for a in range(nbig)},
        scratch_shapes=[pltpu.SemaphoreType.DMA((nsem,)), pltpu.SemaphoreType.DMA((nsem,))],
    )(*placed, cw8)


def _gather_late_start(placed, after, name):
    n = len(placed)
    halves = [s.shape[1] // 2 for s in placed]

    def body(*refs):
        g_refs = refs[0:n]
        send_sems, recv_sems, token = refs[n + 1], refs[n + 2], refs[-1]
        x, y, c = _place()
        j = 2 * x + y
        for a in range(n):
            mine = g_refs[a].at[j, pl.ds(pl.multiple_of(c * halves[a], 16), halves[a]), :]
            for r in (1, 2, 3):
                tx, ty = _chip_of(x, y, r)
                for to_core in (0, 1):
                    k = ((a * 3 + (r - 1)) * 2 + c) * 2 + to_core
                    pltpu.make_async_remote_copy(
                        src_ref=mine, dst_ref=mine, send_sem=send_sems.at[k], recv_sem=recv_sems.at[k],
                        device_id=(tx, ty, to_core), device_id_type=MESH).start()
        token[...] = jnp.zeros_like(token)

    hbm = lambda t: pltpu.HBM(t.shape, t.dtype)
    keep = lambda t: pltpu.with_memory_space_constraint(t, pltpu.HBM)
    nsem = 12 * n
    outs = pl.pallas_call(
        body,
        name=name,
        in_specs=[HBM] * n + [ANY],
        out_specs=(SEM, SEM, *[HBM] * n, pl.BlockSpec(memory_space=pltpu.VMEM)),
        out_shape=(pltpu.SemaphoreType.DMA((nsem,)), pltpu.SemaphoreType.DMA((nsem,)), *[hbm(p) for p in placed],
                   jax.ShapeDtypeStruct((8, 128), F32)),
        input_output_aliases={i: 2 + i for i in range(n)},
        compiler_params=pltpu.CompilerParams(has_side_effects=DATAFLOW),
    )(*[keep(p) for p in placed], after)
    return outs[0], outs[1], list(outs[2:2 + n]), outs[-1]


def _gather_late_wait(send_sems, recv_sems, thru, after, name):
    n = len(thru)
    halves = [s.shape[1] // 2 for s in thru]

    def body(*refs):
        g_refs = refs[0:n]
        send_sems, recv_sems = refs[n], refs[n + 1]
        x, y, c = _place()
        j = 2 * x + y
        for a in range(n):
            mine = g_refs[a].at[j, pl.ds(pl.multiple_of(c * halves[a], 16), halves[a]), :]
            for r in (1, 2, 3):
                tx, ty = _chip_of(x, y, r)
                for other in (0, 1):
                    k_out = ((a * 3 + (r - 1)) * 2 + c) * 2 + other
                    pltpu.make_async_remote_copy(
                        src_ref=mine, dst_ref=mine, send_sem=send_sems.at[k_out], recv_sem=recv_sems.at[k_out],
                        device_id=(tx, ty, other), device_id_type=MESH).wait_send()
                    k_in = ((a * 3 + (r - 1)) * 2 + other) * 2 + c
                    theirs = g_refs[a].at[2 * tx + ty, pl.ds(other * halves[a], halves[a]), :]
                    pltpu.make_async_remote_copy(
                        src_ref=theirs, dst_ref=theirs, send_sem=send_sems.at[k_in], recv_sem=recv_sems.at[k_in],
                        device_id=(tx, ty, other), device_id_type=MESH).wait_recv()

    hbm = lambda t: pltpu.HBM(t.shape, t.dtype)
    outs = pl.pallas_call(
        body,
        name=name,
        in_specs=[HBM] * n + [SEM, SEM, ANY],
        out_specs=[HBM] * n,
        out_shape=[hbm(t) for t in thru],
        input_output_aliases={i: i for i in range(n)},
        compiler_params=pltpu.CompilerParams(has_side_effects=DATAFLOW),
    )(*thru, send_sems, recv_sems, after)
    return list(outs)


D2D_PIECE_ROWS = 64
PAIR_SUM_TILE_BYTES = 2 * MIB


def _pair_sum(gs, gbs, chip_core, name):
    n = len(gs)
    nch, R, C = gs[0].shape
    h = R // 2
    tr = _row_tile(h, C * n, PAIR_SUM_TILE_BYTES)
    nt = h // tr
    rows = min(D2D_PIECE_ROWS, tr)

    def body(jc_ref, *refs):
        g_refs, gb_refs, p_refs, pb_refs = refs[0:n], refs[n:2 * n], refs[2 * n:3 * n], refs[3 * n:4 * n]
        got_refs, send_sems, recv_sems = refs[4 * n:5 * n], refs[5 * n], refs[5 * n + 1]
        i, j = pl.program_id(0), pl.program_id(1)
        x, y, c = _place()

        def copy(a, ti, tj, first, count):
            src_rows = pl.ds(pl.multiple_of((1 - c) * h + ti * tr + first, 16), count)
            dst_rows = pl.ds(pl.multiple_of(ti * tr + first, 16), count)
            return pltpu.make_async_remote_copy(
                src_ref=gb_refs[a].at[tj, src_rows, :], dst_ref=got_refs[a].at[tj, dst_rows, :],
                send_sem=send_sems.at[a, ti, tj], recv_sem=recv_sems.at[a, ti, tj],
                device_id=(x, y, 1 - c), device_id_type=MESH)

        @pl.when((i == 0) & (j == 0))
        def _():
            for ti in range(nt):
                for tj in range(nch):
                    for a in range(n):
                        for q in range(tr // rows):
                            copy(a, ti, tj, q * rows, rows).start()

        for a in range(n):
            copy(a, i, j, 0, tr).wait()
            s = g_refs[a][...] + got_refs[a][j, pl.ds(pl.multiple_of(i * tr, 16), tr), :].astype(F32)
            pb_refs[a][...] = s.astype(BF16)

            @pl.when(j == jc_ref[0])
            def _():
                p_refs[a][...] = s

    by_chip = pl.BlockSpec((None, tr, C), lambda i, j, jc_ref: (j, i, 0))
    outs = pl.pallas_call(
        body,
        name=name,
        grid_spec=pltpu.PrefetchScalarGridSpec(
            num_scalar_prefetch=1,
            grid=(nt, nch),
            in_specs=[pl.BlockSpec((None, tr, C), lambda i, j, jc_ref: (j, jc_ref[1] * nt + i, 0))] * n + [ANY] * n,
            out_specs=[pl.BlockSpec((tr, C), lambda i, j, jc_ref: (i, 0))] * n + [by_chip] * n,
            scratch_shapes=[pltpu.VMEM((nch, h, C), BF16)] * n + [pltpu.SemaphoreType.DMA((n, nt, nch))] * 2,
        ),
        out_shape=[_sds((h, C), F32)] * n + [_sds((nch, h, C), BF16)] * n,
        compiler_params=_params(("arbitrary", "arbitrary"), 48),
    )(chip_core, *_hbm(*gs, *gbs))
    return list(outs[:n]), list(outs[n:])


HBM = pl.BlockSpec(memory_space=pltpu.HBM)
SEM = pl.BlockSpec(memory_space=pltpu.SEMAPHORE)
DATAFLOW = pltpu.SideEffectType.DATAFLOW_SIDE_EFFECTING


def _chip_copy(p_refs, land_refs, send_sems, recv_sems, a, r, blocked):
    x, y, c = _place()
    tx, ty = _chip_of(x, y, r)
    k = a * 3 + (r - 1)
    return pltpu.make_async_remote_copy(
        src_ref=p_refs[a].at[2 * tx + ty] if blocked else p_refs[a], dst_ref=land_refs[a].at[r - 1],
        send_sem=send_sems.at[k], recv_sem=recv_sems.at[k], device_id=(tx, ty, c), device_id_type=MESH)


def _chip_exchange_start(psums, name, blocked=True):
    n = len(psums)
    lands = [lax.empty((3,) + (p.shape[1:] if blocked else p.shape), p.dtype) for p in psums]

    def body(*refs):
        p_refs, land_refs = refs[0:n], refs[n:2 * n]
        send_sems, recv_sems, token = refs[2 * n], refs[2 * n + 1], refs[-1]
        for a in range(n):
            for r in (1, 2, 3):
                _chip_copy(p_refs, land_refs, send_sems, recv_sems, a, r, blocked).start()
        token[...] = jnp.zeros_like(token)

    hbm = lambda t: pltpu.HBM(t.shape, t.dtype)
    keep = lambda t: pltpu.with_memory_space_constraint(t, pltpu.HBM)
    outs = pl.pallas_call(
        body,
        name=name,
        in_specs=[HBM] * (2 * n),
        out_specs=(SEM, SEM, *[HBM] * (2 * n), pl.BlockSpec(memory_space=pltpu.VMEM)),
        out_shape=(pltpu.SemaphoreType.DMA((3 * n,)), pltpu.SemaphoreType.DMA((3 * n,)),
                   *[hbm(p) for p in psums], *[hbm(l) for l in lands], _sds((8, 128), F32)),
        input_output_aliases={i: 2 + i for i in range(2 * n)},
        compiler_params=pltpu.CompilerParams(has_side_effects=DATAFLOW),
    )(*[keep(p) for p in psums], *[keep(l) for l in lands])
    return outs[0], outs[1], list(outs[2:2 + n]), list(outs[2 + n:2 + 2 * n]), outs[-1]


def _chip_exchange_wait(send_sems, recv_sems, p_thru, land_thru, after, name, blocked=True):
    n = len(p_thru)

    def body(*refs):
        p_refs, land_refs = refs[0:n], refs[n:2 * n]
        send_sems, recv_sems = refs[2 * n], refs[2 * n + 1]
        for a in range(n):
            for r in (1, 2, 3):
                copy = _chip_copy(p_refs, land_refs, send_sems, recv_sems, a, r, blocked)
                copy.wait_send()
                copy.wait_recv()

    hbm = lambda t: pltpu.HBM(t.shape, t.dtype)
    outs = pl.pallas_call(
        body,
        name=name,
        in_specs=[HBM] * (2 * n) + [SEM, SEM, ANY],
        out_specs=[HBM] * (2 * n),
        out_shape=[hbm(p) for p in p_thru] + [hbm(l) for l in land_thru],
        input_output_aliases={i: i for i in range(2 * n)},
        compiler_params=pltpu.CompilerParams(has_side_effects=DATAFLOW),
    )(*p_thru, *land_thru, send_sems, recv_sems, after)
    return list(outs[0:n]), list(outs[n:2 * n])


def _pair_share(fulls):
    n = len(fulls)
    halves = [f.shape[0] // 2 for f in fulls]

    def body(*refs):
        full_refs = refs[n:2 * n]
        send_sems, recv_sems = refs[2 * n:]
        x, y, c = _place()

        def half_of(a, core):
            return full_refs[a].at[pl.ds(pl.multiple_of(core * halves[a], 8), halves[a]), :]

        def remote(a, src, dst):
            return pltpu.make_async_remote_copy(
                src_ref=src, dst_ref=dst, send_sem=send_sems.at[a], recv_sem=recv_sems.at[a],
                device_id=(x, y, 1 - c), device_id_type=MESH)

        for a in range(n):
            for q in range(halves[a] // D2D_PIECE_ROWS):
                piece = full_refs[a].at[
                    pl.ds(pl.multiple_of(c * halves[a] + q * D2D_PIECE_ROWS, 8), D2D_PIECE_ROWS), :]
                remote(a, piece, piece).start()
        for a in range(n):
            remote(a, half_of(a, c), half_of(a, c)).wait_send()
            remote(a, half_of(a, 1 - c), half_of(a, 1 - c)).wait_recv()

    return pl.pallas_call(
        body,
        name="pair_share",
        in_specs=[ANY] * n,
        out_specs=[ANY] * n,
        out_shape=[_sds(f.shape, F32) for f in fulls],
        input_output_aliases={a: a for a in range(n)},
        scratch_shapes=[pltpu.SemaphoreType.DMA((n,)), pltpu.SemaphoreType.DMA((n,))],
    )(*fulls)


def _small_pair_sum(s):
    R, C = s.shape
    V = SMALL_VECTOR_ROWS

    def body(s_ref, v_ref, m_ref, sib, send_sem, recv_sem):
        x, y, c = _place()

        def to_sib(src, dst):
            return pltpu.make_async_remote_copy(
                src_ref=src, dst_ref=dst, send_sem=send_sem, recv_sem=recv_sem,
                device_id=(x, y, 1 - c), device_id_type=MESH)

        for q in range(R // 8):
            to_sib(s_ref.at[pl.ds(8 * q, 8), :], sib.at[pl.ds(8 * q, 8), :]).start()
        to_sib(s_ref, sib).wait()
        v_ref[...] = s_ref[pl.ds(0, V), :] + sib[pl.ds(0, V), :]
        m_ref[...] = (s_ref[pl.ds(V, R - V), :] + sib[pl.ds(V, R - V), :]).astype(BF16)

    return pl.pallas_call(
        body,
        name="small_pair_sum",
        in_specs=[pl.BlockSpec(memory_space=pltpu.VMEM)],
        out_specs=[pl.BlockSpec(memory_space=pltpu.VMEM)] * 2,
        out_shape=[jax.ShapeDtypeStruct((V, C), F32), jax.ShapeDtypeStruct((R - V, C), BF16)],
        scratch_shapes=[pltpu.VMEM((R, C), F32), pltpu.SemaphoreType.DMA, pltpu.SemaphoreType.DMA],
    )(s)


def _small_total(chip, own, landed):
    V, C = own[0].shape
    M = own[1].shape[0]

    def body(j_ref, v_ref, m_ref, lv_ref, lm_ref, o_ref, chips_v, chips_m):
        j = j_ref[0]
        chips_v[j] = v_ref[...]
        chips_m[j] = m_ref[...]
        for r in (1, 2, 3):
            chips_v[j ^ r] = lv_ref[r - 1]
            chips_m[j ^ r] = lm_ref[r - 1]
        o_ref[pl.ds(0, V), :] = (chips_v[0] + chips_v[1]) + (chips_v[2] + chips_v[3])
        o_ref[pl.ds(V, M), :] = (chips_m[0].astype(F32) + chips_m[1].astype(F32)) + (
            chips_m[2].astype(F32) + chips_m[3].astype(F32))

    vmem = pl.BlockSpec(memory_space=pltpu.VMEM)
    return pl.pallas_call(
        body,
        name="small_total",
        in_specs=[pl.BlockSpec(memory_space=pltpu.SMEM), vmem, vmem, vmem, vmem],
        out_specs=vmem,
        out_shape=jax.ShapeDtypeStruct((V + M, C), F32),
        scratch_shapes=[pltpu.VMEM((N_CHIPS, V, C), F32), pltpu.VMEM((N_CHIPS, M, C), BF16)],
    )(chip, own[0], own[1], landed[0], landed[1])


def _local_grads(x, target, g_pre, w_in_g, b_gate, conv_w, conv_b, w_rg_a, b_rg_a, w_rg_x, b_rg_x, lam, sinks,
                 out_weights, fwd_token, g_post, on_out_grads, on_w_in_grad):
    b_a = b_rg_a.reshape(1, D_RNN)
    b_x = b_rg_x.reshape(1, D_RNN)

    proj, ht = _proj_fwd(x, g_pre, w_in_g)
    y_rnn, z_rnn, conv, z_rnn_t = _rnn_fwd(proj, conv_w, conv_b, w_rg_a, w_rg_x, b_a, b_x, lam, fwd_token)
    bias = _attn_bias()
    y_attn, z_attn, lse = _attn_fwd(proj, sinks, bias)
    w_rnn_out, w_attn_out, w_out = out_weights(z_attn)
    dyx, dz_rnn, dz_attn, dml, dout, dbr_rnn, dbr_attn, merged_t, z_attn_t, head_small = _head(
        x, target, z_rnn, z_attn, proj, b_gate, g_post, w_rnn_out, w_attn_out, w_out)
    out_grads = [_matmul_t(z_rnn_t, dbr_rnn, "dw_rnn_out"), _matmul_t(z_attn_t, dbr_attn, "dw_attn_out"),
                 _matmul_t(merged_t, dout, "dw_out")]
    shard_rows = lambda d: d.reshape(N_CHIPS, OUT_SHARD, D_MODEL)
    token = on_out_grads([shard_rows(g) for g, _ in out_grads], [shard_rows(gb) for _, gb in out_grads])
    dq, dk, dv, dag, attn_small = _attn_bwd(proj, y_attn, lse, dz_attn, sinks, bias, token)
    drx, drg, dwa, dwx, rnn_small = _rnn_bwd(proj, conv, y_rnn, dz_rnn, conv_w, w_rg_a, w_rg_x, b_a, b_x, lam)
    dproj = [drx, drg, dq, dk, dv, dag, dml]
    token = on_w_in_grad(*_dw_in(ht, dproj))
    grad_x, dh_small = _dh_bwd(dproj, w_in_g, x, dyx, g_pre, token)
    small = jnp.concatenate([rnn_small, head_small, dh_small + attn_small,
                             dwa.reshape(64, 1024), dwx.reshape(64, 1024)], axis=0)
    return grad_x, small


ROW_LOSS = 11


SMALL_ROW_TENSORS = {"b_rg_a": (0, 1, D_RNN), "b_rg_x": (1, 1, D_RNN), "lru_lambda": (2, 1, D_RNN),
                     "conv_b": (3, 1, D_RNN), "post_norm_g": (8, 1, D_MODEL), "b_gate": (9, 2, D_MODEL),
                     "pre_norm_g": (16, 1, D_MODEL), "attn_sinks": (17, 1, N_Q_HEADS)}


def _unpack_small(s, conv_cols):
    return {
        "conv_w": s[4:8, 0:conv_cols].reshape(1, CONV_W, conv_cols),
        "w_rg_a": s[24:88].reshape(1, 16, 64, 64), "w_rg_x": s[88:152].reshape(1, 16, 64, 64),
    }


WEIGHTS = ["pre_norm_g", "w_in", "b_gate", "conv_w", "conv_b", "w_rg_a", "b_rg_a", "w_rg_x", "b_rg_x", "lru_lambda",
           "attn_sinks", "w_rnn_out", "w_attn_out", "w_out", "post_norm_g"]
BIG = ["w_in", "w_rnn_out", "w_attn_out", "w_out"]


def kernel(x, pre_norm_g, w_in, b_gate, conv_w, conv_b, w_rg_a, b_rg_a, w_rg_x, b_rg_x, lru_lambda, attn_sinks, w_rnn_out, w_attn_out, w_out, post_norm_g, loss_target, m_pre_norm_g, m_w_in, m_b_gate, m_conv_w, m_conv_b, m_w_rg_a, m_b_rg_a, m_w_rg_x, m_b_rg_x, m_lru_lambda, m_attn_sinks, m_w_rnn_out, m_w_attn_out, m_w_out, m_post_norm_g, v_pre_norm_g, v_w_in, v_b_gate, v_conv_w, v_conv_b, v_w_rg_a, v_b_rg_a, v_w_rg_x, v_b_rg_x, v_lru_lambda, v_attn_sinks, v_w_rnn_out, v_w_attn_out, v_w_out, v_post_norm_g):
    w = dict(pre_norm_g=pre_norm_g, w_in=w_in, b_gate=b_gate, conv_w=conv_w, conv_b=conv_b, w_rg_a=w_rg_a,
             b_rg_a=b_rg_a, w_rg_x=w_rg_x, b_rg_x=b_rg_x, lru_lambda=lru_lambda, attn_sinks=attn_sinks,
             w_rnn_out=w_rnn_out, w_attn_out=w_attn_out, w_out=w_out, post_norm_g=post_norm_g)
    m = dict(pre_norm_g=m_pre_norm_g, w_in=m_w_in, b_gate=m_b_gate, conv_w=m_conv_w, conv_b=m_conv_b, w_rg_a=m_w_rg_a,
             b_rg_a=m_b_rg_a, w_rg_x=m_w_rg_x, b_rg_x=m_b_rg_x, lru_lambda=m_lru_lambda, attn_sinks=m_attn_sinks,
             w_rnn_out=m_w_rnn_out, w_attn_out=m_w_attn_out, w_out=m_w_out, post_norm_g=m_post_norm_g)
    v = dict(pre_norm_g=v_pre_norm_g, w_in=v_w_in, b_gate=v_b_gate, conv_w=v_conv_w, conv_b=v_conv_b, w_rg_a=v_w_rg_a,
             b_rg_a=v_b_rg_a, w_rg_x=v_w_rg_x, b_rg_x=v_b_rg_x, lru_lambda=v_lru_lambda, attn_sinks=v_attn_sinks,
             w_rnn_out=v_w_rnn_out, w_attn_out=v_w_attn_out, w_out=v_w_out, post_norm_g=v_post_norm_g)
    chip = 2 * lax.axis_index("x") + lax.axis_index("y")

    chip_idx = chip.astype(jnp.int32).reshape(1)
    chip_core = jnp.stack([chip, lax.axis_index("c")]).astype(jnp.int32)
    cw8 = jnp.pad(conv_w[0], ((0, 8 - CONV_W), (0, 0)))
    placed = _place_shards([w_in[0], w_rnn_out[0], w_attn_out[0], w_out[0]], chip_idx, "place_shards")
    win_g, cw_g = _gather_weights(placed[:1], cw8)
    late_send, late_recv, late_thru, late_token = _gather_late_start(placed[1:], win_g, "gather_late_start")
    cw_g = lax.dynamic_update_slice_in_dim(cw_g, cw8[None], chip, axis=0)
    conv_w_full = jnp.transpose(cw_g[:, 0:CONV_W, :], (1, 0, 2)).reshape(CONV_W, D_RNN)

    started = {}

    def start_reduction(tag, grads, grads_b16):
        psums, psums_b16 = _pair_sum(grads, grads_b16, chip_core, "pair_sum_" + tag)
        send_sems, recv_sems, p_thru, land_thru, token = _chip_exchange_start(psums_b16, "chip_exchange_start_" + tag)
        started[tag] = (psums, send_sems, recv_sems, p_thru, land_thru)
        return token

    def end_reduction(tag, after):
        psums, send_sems, recv_sems, p_thru, land_thru = started[tag]
        _, landed = _chip_exchange_wait(send_sems, recv_sems, p_thru, land_thru, after, "chip_exchange_wait_" + tag)
        return psums, landed

    def out_weights(after):
        gathered = _gather_late_wait(late_send, late_recv, late_thru, after, "gather_late_wait")
        return [g.reshape(D_MODEL, D_MODEL) for g in gathered]

    grad_x, small = _local_grads(
        x[0], loss_target[0], pre_norm_g, win_g, b_gate, conv_w_full, conv_b, w_rg_a[0], b_rg_a[0], w_rg_x[0],
        b_rg_x[0], lru_lambda, attn_sinks[0], out_weights, late_token, post_norm_g,
        on_out_grads=lambda grads, grads_b16: start_reduction("out", grads, grads_b16),
        on_w_in_grad=lambda grad, grad_b16: start_reduction("in", [grad], [grad_b16]))

    small_chip = _small_pair_sum(small)
    small_send, small_recv, small_thru, small_land, small_token = _chip_exchange_start(
        list(small_chip), "small_exchange_start", blocked=False)

    halves = _chip_sum([end_reduction("in", small_token), end_reduction("out", small_token)], chip_core, "chip_sum")
    gbig = dict(zip(BIG, _pair_share(halves)))

    grads, delta, new_m, new_v = {}, {}, {}, {}
    updates = _adamw([[(w[n][0], gbig[n], m[n][0], v[n][0]) for n in names] for names in (BIG[:1], BIG[1:])],
                     "adamw_big")
    for n, (d, nm, nv, g) in zip(BIG, updates):
        grads[n], delta[n], new_m[n], new_v[n] = g[None], d[None], nm[None], nv[None]

    small_own, small_landed = _chip_exchange_wait(small_send, small_recv, small_thru, small_land, delta[BIG[-1]],
                                                  "small_exchange_wait", blocked=False)
    small_sum = _small_total(chip_idx, small_own, small_landed)
    total_loss = small_sum[ROW_LOSS, 0]
    gsmall = _unpack_small(small_sum, D_RNN)
    conv_shard = D_RNN // N_CHIPS
    gsmall["conv_w"] = lax.dynamic_slice_in_dim(gsmall["conv_w"], chip * conv_shard, conv_shard, axis=2)
    for n in gsmall:
        grads[n] = gsmall[n].reshape(w[n].shape)
    updates = _adamw_whole([(w[n], grads[n], m[n], v[n]) for n in gsmall],
                           [(w[n], rows, m[n], v[n]) for n, rows in SMALL_ROW_TENSORS.items()], small_sum, "adamw_small")
    for n, (d, nm, nv, *g) in zip([*gsmall, *SMALL_ROW_TENSORS], updates):
        delta[n], new_m[n], new_v[n] = d, nm, nv
        grads.update({n: g[0]} if g else {})

    return (total_loss, grad_x[None], *[grads[n] for n in WEIGHTS], *[delta[n] for n in WEIGHTS],
            *[new_m[n] for n in WEIGHTS], *[new_v[n] for n in WEIGHTS])
```

```python
import functools
import math

import jax
import jax.numpy as jnp
import numpy as np
from jax import lax
from jax.experimental import pallas as pl
from jax.experimental.pallas import tpu as pltpu

F32 = jnp.float32
BF16 = jnp.bfloat16

D_MODEL = 1024
D_RNN = 1024
RNN_BLOCKS = 16
RNN_BLOCK_W = 64
CONV_W = 4
LRU_C = 8.0
N_Q_HEADS = 16
N_KV_HEADS = 4
GROUP = 4
HEAD_DIM = 64
D_KV = 256
BLOCK = 128
ALIBI_MAX_BIAS = 8.0
EPS = 1e-6
D_IN = 6656
N_CHIPS = 4
W_IN_SHARD = D_IN // N_CHIPS
OUT_SHARD = D_MODEL // N_CHIPS
ADAM_LR = 0.001
ADAM_B1 = 0.9
ADAM_B2 = 0.999
ADAM_EPS = 1e-08
ADAM_WD = 0.01
ADAM_STEP = 10
NEG_BIG = -1e30
MIB = 1 << 20

COL_RNN_X = 0
COL_RNN_GATE = 4
COL_Q = 8
COL_K = 12
COL_V = 13
COL_ATTN_GATE = 14
COL_MERGE = 18

RNN_TILE = 256
RNN_CHUNK = 512
SMALL_ROWS = 152
SMALL_VECTOR_ROWS = 24
MESH = pl.DeviceIdType.MESH


def _sds(shape, dtype):
    return pltpu.HBM(shape, dtype)


def _params(sem=None, vmem_mib=None):
    kw = {}
    if sem is not None:
        kw["dimension_semantics"] = sem
    if vmem_mib is not None:
        kw["vmem_limit_bytes"] = vmem_mib * MIB
    return pltpu.CompilerParams(**kw)


def _hbm(*arrays):
    return [pltpu.with_memory_space_constraint(a, pltpu.HBM) for a in arrays]


def _dot(a, b):
    return jnp.dot(a, b, preferred_element_type=F32)


def _dot_nt(a, b):
    return lax.dot_general(a, b, (((1,), (1,)), ((), ())), preferred_element_type=F32)


def _dot_tn(a, b):
    return lax.dot_general(a, b, (((0,), (0,)), ((), ())), preferred_element_type=F32)


def _sigmoid(x):
    return 0.5 * jnp.tanh(0.5 * x) + 0.5


def _sigmoid_small(x):
    return 1.0 / (1.0 + jnp.exp(-x))


def _softplus(x):
    return jnp.maximum(x, 0.0) + jnp.log(1.0 + jnp.exp(-jnp.abs(x)))


def _one_minus_square(a, log_a):
    return -jnp.tanh(log_a) * (a * a + 1.0)


def _proj_fwd(x, g_pre, w_in_g):
    T = x.shape[0]
    tm = min(1024, T)

    def body(x_ref, g_ref, w_ref, proj_ref, ht_ref, h_s):
        @pl.when(pl.program_id(1) == 0)
        def _():
            xv = x_ref[...]
            rstd = lax.rsqrt(jnp.mean(xv * xv, axis=-1, keepdims=True) + EPS)
            hf = (xv * rstd) * g_ref[...]
            h_s[...] = hf.astype(BF16)
            ht_ref[...] = hf.T.astype(BF16)

        proj_ref[...] = _dot(h_s[...], w_ref[...]).astype(BF16)

    return pl.pallas_call(
        body,
        name="proj_fwd",
        grid=(T // tm, N_CHIPS),
        in_specs=[
            pl.BlockSpec((tm, D_MODEL), lambda i, j: (i, 0)),
            pl.BlockSpec((1, D_MODEL), lambda i, j: (0, 0)),
            pl.BlockSpec((None, D_MODEL, W_IN_SHARD), lambda i, j: (j, 0, 0)),
        ],
        out_specs=[
            pl.BlockSpec((tm, W_IN_SHARD), lambda i, j: (i, j)),
            pl.BlockSpec((D_MODEL, tm), lambda i, j: (0, i)),
        ],
        out_shape=[_sds((T, D_IN), BF16), _sds((D_MODEL, T), BF16)],
        scratch_shapes=[pltpu.VMEM((tm, D_MODEL), BF16)],
        compiler_params=_params(("parallel", "arbitrary"), 48),
    )(*_hbm(x, g_pre, w_in_g))


def _shift_down(x, tail, s, row):
    n = x.shape[0]
    xs = pltpu.roll(x, s, 0)
    tail_t = jnp.tile(pltpu.roll(tail, s, 0), (n // 8, 1))
    return jnp.where(row < s, tail_t, xs)


def _shift_up(x, head, s, row):
    n = x.shape[0]
    xs = pltpu.roll(x, n - s, 0)
    head_t = jnp.tile(pltpu.roll(head, 8 - s, 0), (n // 8, 1))
    return jnp.where(row >= n - s, head_t, xs)


def _conv_taps(x, tail, row):
    return [_shift_down(x, tail, 3, row), _shift_down(x, tail, 2, row), _shift_down(x, tail, 1, row), x]


def _rglru_gates(c, wa, wx, ba, bx, lam):
    cb = c.astype(BF16)
    r = _sigmoid_small(_dot(cb, wa) + ba)
    i = _sigmoid(_dot(cb, wx) + bx)
    log_a = (-LRU_C) * r * _softplus(-lam)
    a = jnp.exp(log_a)
    w = _one_minus_square(a, log_a)
    inv_mult = lax.rsqrt(w)
    return cb, r, i, a, w * inv_mult, inv_mult


GATE_BLOCKS_PER_TILE = RNN_TILE // RNN_BLOCK_W
GATE_BLOCKS = pl.BlockSpec((GATE_BLOCKS_PER_TILE, RNN_BLOCK_W, RNN_BLOCK_W), lambda j, t: (j, 0, 0))


def _fill_block_diag(bd_ref, w_ref):
    bd_ref[...] = jnp.zeros_like(bd_ref)
    for a in range(GATE_BLOCKS_PER_TILE):
        lo = a * RNN_BLOCK_W
        bd_ref[lo:lo + RNN_BLOCK_W, lo:lo + RNN_BLOCK_W] = w_ref[a].astype(BF16)


SUBLANES = 8


def _scan_down(a, u, row):
    n = a.shape[0]
    s = 1
    while s < SUBLANES:
        a_sh = jnp.where(row >= s, pltpu.roll(a, s, 0), 1.0)
        u_sh = jnp.where(row >= s, pltpu.roll(u, s, 0), 0.0)
        u = a * u_sh + u
        a = a * a_sh
        s *= 2
    while s < n:
        u = jnp.concatenate([u[:s], a[s:] * u[:n - s] + u[s:]], axis=0)
        a = jnp.concatenate([a[:s], a[s:] * a[:n - s]], axis=0)
        s *= 2
    return a, u


def _scan_up(b, u, row):
    n = b.shape[0]
    s = 1
    while s < SUBLANES:
        b_sh = jnp.where(row < n - s, pltpu.roll(b, n - s, 0), 1.0)
        u_sh = jnp.where(row < n - s, pltpu.roll(u, n - s, 0), 0.0)
        u = b * u_sh + u
        b = b * b_sh
        s *= 2
    while s < n:
        u = jnp.concatenate([b[:n - s] * u[s:] + u[:n - s], u[n - s:]], axis=0)
        b = jnp.concatenate([b[:n - s] * b[s:], b[n - s:]], axis=0)
        s *= 2
    return b, u


LANES = 128


def _chunk_scan(a, u, a_s, u_s, hl_s, al_s, carry, reverse):
    n, width = a.shape
    groups = n // SUBLANES
    order = range(SUBLANES - 1, -1, -1) if reverse else range(SUBLANES)
    row = lax.broadcasted_iota(jnp.int32, (groups, LANES), 0)
    for l in range(width // LANES):
        lanes = slice(l * LANES, (l + 1) * LANES)
        a_l, u_l, hl_l, al_l = a_s.at[l], u_s.at[l], hl_s.at[l], al_s.at[l]
        a_l[...] = a[:, lanes]
        u_l[...] = u[:, lanes]
        h_loc = a_loc = None
        for r in order:
            rows = pl.ds(r, groups, stride=SUBLANES)
            a_r, u_r = a_l[rows, :], u_l[rows, :]
            h_loc, a_loc = (u_r, a_r) if h_loc is None else (a_r * h_loc + u_r, a_r * a_loc)
            hl_l[rows, :] = h_loc
            al_l[rows, :] = a_loc
        if reverse:
            a_cum, ends = _scan_up(a_loc, h_loc, row)
            ends = ends + a_cum * carry[:, lanes]
            enters = jnp.where(row == groups - 1, carry[:, lanes], pltpu.roll(ends, groups - 1, 0))
        else:
            a_cum, ends = _scan_down(a_loc, h_loc, row)
            ends = ends + a_cum * carry[:, lanes]
            enters = jnp.where(row == 0, carry[:, lanes], pltpu.roll(ends, 1, 0))
        for r in range(SUBLANES):
            rows = pl.ds(r, groups, stride=SUBLANES)
            hl_l[rows, :] = hl_l[rows, :] + al_l[rows, :] * enters
    return jnp.concatenate([hl_s[l] for l in range(width // LANES)], axis=1)


def _rnn_fwd(proj, conv_w, conv_b, w_a, w_x, b_a, b_x, lam, token):
    T = proj.shape[0]
    tc, ct = RNN_CHUNK, RNN_TILE
    nt = T // tc

    def body(x_ref, rg_ref, cw_ref, cb_ref, wa_ref, wx_ref, ba_ref, bx_ref, lam_ref, token_ref, h_ref, z_ref, c_ref,
             zt_ref, xtail, hcarry, wa_s, wx_s, a_s, u_s, hl_s, al_s):
        @pl.when(pl.program_id(1) == 0)
        def _():
            xtail[...] = jnp.zeros_like(xtail)
            hcarry[...] = jnp.zeros_like(hcarry)
            _fill_block_diag(wa_s, wa_ref)
            _fill_block_diag(wx_s, wx_ref)

        row = lax.broadcasted_iota(jnp.int32, (tc, ct), 0)
        x = x_ref[...].astype(F32)
        taps = _conv_taps(x, xtail[...], row)
        c = cb_ref[...] + cw_ref[pl.ds(0, 1), :] * taps[0]
        for k in range(1, CONV_W):
            c = c + cw_ref[pl.ds(k, 1), :] * taps[k]
        xtail[...] = x[tc - 8:, :]
        c_ref[...] = c
        _, _, i, a, mult, _ = _rglru_gates(c, wa_s[...], wx_s[...], ba_ref[...], bx_ref[...], lam_ref[...])
        h = _chunk_scan(a, mult * (i * c), a_s, u_s, hl_s, al_s, hcarry[...], reverse=False)
        h_ref[...] = h
        hcarry[...] = h_ref[pl.ds(tc - 1, 1), :]
        rg = rg_ref[...].astype(F32)
        z = h * (rg * _sigmoid(rg))
        z_ref[...] = z.astype(BF16)
        zt_ref[...] = z.T.astype(BF16)

    col = lambda off: (lambda j, t: (t, off + j))
    vec = pl.BlockSpec((1, ct), lambda j, t: (0, j))
    return pl.pallas_call(
        body,
        name="rnn_fwd",
        grid=(D_RNN // ct, nt),
        in_specs=[
            pl.BlockSpec((tc, ct), col(COL_RNN_X)),
            pl.BlockSpec((tc, ct), col(COL_RNN_GATE)),
            pl.BlockSpec((CONV_W, ct), lambda j, t: (0, j)),
            vec, GATE_BLOCKS, GATE_BLOCKS, vec, vec, vec,
            pl.BlockSpec((8, 128), lambda j, t: (0, 0)),
        ],
        out_specs=[pl.BlockSpec((tc, ct), lambda j, t: (t, j))] * 3 + [pl.BlockSpec((ct, tc), lambda j, t: (j, t))],
        out_shape=[_sds((T, D_RNN), F32), _sds((T, D_RNN), BF16), _sds((T, D_RNN), F32), _sds((D_RNN, T), BF16)],
        scratch_shapes=[pltpu.VMEM((8, ct), F32), pltpu.VMEM((1, ct), F32)] + [pltpu.VMEM((ct, ct), BF16)] * 2 + [
            pltpu.VMEM((ct // LANES, tc, LANES), F32)] * 4,
        compiler_params=_params(("parallel", "arbitrary"), 32),
    )(*_hbm(proj, proj, conv_w, conv_b, w_a, w_x, b_a, b_x, lam, token))


def _rnn_bwd(proj, conv, y_rnn, dz_rnn, conv_w, w_a, w_x, b_a, b_x, lam):
    T = proj.shape[0]
    tc, ct = RNN_CHUNK, RNN_TILE
    nt = T // tc
    hb = tc // 8

    def body(x_ref, c_ref, rg_ref, h_ref, hh_ref, dz_ref, cw_ref, wa_ref, wx_ref, ba_ref, bx_ref, lam_ref,
             dx_ref, drg_ref, dwa_ref, dwx_ref, sm_ref, lam_carry, a_carry, dc_head, wa_s, wx_s, dwa_s, dwx_s,
             b_s, dy_s, hl_s, al_s):
        t = pl.program_id(1)
        first_chunk = t == nt - 1

        @pl.when(t == 0)
        def _():
            lam_carry[...] = jnp.zeros_like(lam_carry)
            a_carry[...] = jnp.zeros_like(a_carry)
            dc_head[...] = jnp.zeros_like(dc_head)
            dwa_s[...] = jnp.zeros_like(dwa_s)
            dwx_s[...] = jnp.zeros_like(dwx_s)
            sm_ref[...] = jnp.zeros_like(sm_ref)
            _fill_block_diag(wa_s, wa_ref)
            _fill_block_diag(wx_s, wx_ref)

        row = lax.broadcasted_iota(jnp.int32, (tc, ct), 0)
        keep = jnp.where(first_chunk, 0.0, 1.0)
        x = x_ref[...].astype(F32)
        c = c_ref[...]
        lam = lam_ref[...]
        cb, r, i, a, mult, inv_mult = _rglru_gates(c, wa_s[...], wx_s[...], ba_ref[...], bx_ref[...], lam)
        h = h_ref[...]
        h_prev = _shift_down(h, hh_ref[...] * keep, 1, row)
        rg = rg_ref[...].astype(F32)
        dz = dz_ref[...]
        sg = _sigmoid(rg)
        drg_ref[...] = (dz * h * (sg * (1.0 + rg * (1.0 - sg)))).astype(BF16)
        dy = dz * (rg * sg)
        b = jnp.where(row >= tc - 1, a_carry[pl.ds(0, 1), :], pltpu.roll(a, tc - 1, 0))
        lt = _chunk_scan(b, dy, b_s, dy_s, hl_s, al_s, lam_carry[pl.ds(0, 1), :], reverse=True)
        lam_carry[...] = lt[0:8, :]
        a_carry[...] = a[0:8, :]
        ic = i * c
        dmult = lt * ic
        di = lt * mult * c
        dc = lt * mult * i
        dlog_a = a * (lt * h_prev - dmult * a * inv_mult)
        sp = _softplus(-lam)
        dpre_r = dlog_a * ((-LRU_C) * sp) * (r * (1.0 - r))
        dpre_i = di * (i * (1.0 - i))
        dlam_row = jnp.sum(dlog_a * r, axis=0, keepdims=True) * (LRU_C * _sigmoid(-lam))
        dpr_b = dpre_r.astype(BF16)
        dpi_b = dpre_i.astype(BF16)
        dwa_s[...] += _dot_tn(cb, dpr_b)
        dwx_s[...] += _dot_tn(cb, dpi_b)
        dc = dc + _dot_nt(dpr_b, wa_s[...]) + _dot_nt(dpi_b, wx_s[...])
        head = dc_head[...]
        dx = cw_ref[pl.ds(3, 1), :] * dc
        sm_ref[pl.ds(4 + 3, 1), :] += jnp.sum(dc * x, axis=0, keepdims=True)
        for m in range(1, CONV_W):
            up = _shift_up(dc, head, m, row)
            dx = dx + cw_ref[pl.ds(3 - m, 1), :] * up
            sm_ref[pl.ds(4 + 3 - m, 1), :] += jnp.sum(up * x, axis=0, keepdims=True)
        dx_ref[...] = dx.astype(BF16)
        dc_head[...] = dc[0:8, :]
        sm_ref[pl.ds(0, 1), :] += jnp.sum(dpre_r, axis=0, keepdims=True)
        sm_ref[pl.ds(1, 1), :] += jnp.sum(dpre_i, axis=0, keepdims=True)
        sm_ref[pl.ds(2, 1), :] += dlam_row
        sm_ref[pl.ds(3, 1), :] += jnp.sum(dc, axis=0, keepdims=True)

        @pl.when(first_chunk)
        def _():
            for k in range(GATE_BLOCKS_PER_TILE):
                lo = k * RNN_BLOCK_W
                dwa_ref[k] = dwa_s[lo:lo + RNN_BLOCK_W, lo:lo + RNN_BLOCK_W]
                dwx_ref[k] = dwx_s[lo:lo + RNN_BLOCK_W, lo:lo + RNN_BLOCK_W]

    rev = lambda off: (lambda j, t: (nt - 1 - t, off + j))
    halo = lambda off: (lambda j, t: (jnp.maximum((nt - 1 - t) * hb - 1, 0), off + j))
    vec = pl.BlockSpec((1, ct), lambda j, t: (0, j))
    mat = GATE_BLOCKS
    return pl.pallas_call(
        body,
        name="rnn_bwd",
        grid=(D_RNN // ct, nt),
        in_specs=[
            pl.BlockSpec((tc, ct), rev(COL_RNN_X)),
            pl.BlockSpec((tc, ct), rev(0)),
            pl.BlockSpec((tc, ct), rev(COL_RNN_GATE)),
            pl.BlockSpec((tc, ct), rev(0)),
            pl.BlockSpec((8, ct), halo(0)),
            pl.BlockSpec((tc, ct), rev(0)),
            pl.BlockSpec((CONV_W, ct), lambda j, t: (0, j)),
            mat, mat, vec, vec, vec,
        ],
        out_specs=[
            pl.BlockSpec((tc, ct), rev(0)),
            pl.BlockSpec((tc, ct), rev(0)),
            mat, mat,
            pl.BlockSpec((8, ct), lambda j, t: (0, j)),
        ],
        out_shape=[_sds((T, D_RNN), BF16), _sds((T, D_RNN), BF16), _sds(w_a.shape, F32), _sds(w_x.shape, F32),
                   _sds((8, D_RNN), F32)],
        scratch_shapes=[pltpu.VMEM((8, ct), F32)] * 3 + [pltpu.VMEM((ct, ct), BF16)] * 2 + [
            pltpu.VMEM((ct, ct), F32)] * 2 + [pltpu.VMEM((ct // LANES, tc, LANES), F32)] * 4,
        compiler_params=_params(("parallel", "arbitrary"), 32),
    )(*_hbm(proj, conv, proj, y_rnn, y_rnn, dz_rnn, conv_w, w_a, w_x, b_a, b_x, lam))


def _attn_bias():
    qi = np.arange(BLOCK)[:, None]
    kj = np.arange(BLOCK)[None, :]
    dist_cur = (qi - kj).astype(np.float32)
    slopes = np.float32(2.0) ** (-ALIBI_MAX_BIAS * np.arange(1, N_Q_HEADS + 1, dtype=np.float32) / N_Q_HEADS)
    slopes = slopes[:, None, None]
    prev = np.where(kj > qi, -slopes * (dist_cur + np.float32(BLOCK)), np.float32(NEG_BIG))
    cur = np.where(kj <= qi, -slopes * dist_cur, np.float32(NEG_BIG))
    later = np.concatenate([prev, cur], axis=-1)
    first = np.concatenate([np.full_like(prev, NEG_BIG), cur], axis=-1)
    return jnp.asarray(np.stack([first, later]).astype(np.float32))


def _attn_exps(s_prev, s_cur, sink, bias):
    s_prev = s_prev + bias[:, 0:BLOCK]
    s_cur = s_cur + bias[:, BLOCK:2 * BLOCK]
    m = jnp.maximum(jnp.max(jnp.maximum(s_prev, s_cur), axis=-1, keepdims=True), sink)
    p_prev = jnp.exp(s_prev - m)
    p_cur = jnp.exp(s_cur - m)
    total = jnp.sum(p_prev + p_cur, axis=-1, keepdims=True) + jnp.exp(sink - m)
    return p_prev, p_cur, 1.0 / total, m + jnp.log(total)


def _attn_probs(s_prev, s_cur, sink, bias, lse):
    p_prev = jnp.exp((s_prev + bias[:, 0:BLOCK]) - lse)
    p_cur = jnp.exp((s_cur + bias[:, BLOCK:2 * BLOCK]) - lse)
    return p_prev, p_cur, jnp.exp(sink - lse)


def _stack_heads(ref_or_val, hk, dtype):
    parts = [ref_or_val[:, (GROUP * hk + g) * HEAD_DIM:(GROUP * hk + g + 1) * HEAD_DIM] for g in range(GROUP)]
    return jnp.concatenate(parts, axis=0).astype(dtype)


ATTN_SCALE = HEAD_DIM ** -0.5


def _bias_spec():
    return pl.BlockSpec((None, N_Q_HEADS, BLOCK, 2 * BLOCK), lambda i: (jnp.minimum(i, 1), 0, 0, 0))


def _attn_fwd(proj, sinks, bias):
    T = proj.shape[0]
    nb = T // BLOCK

    def body(sink_ref, bias_ref, q_ref, kp_ref, kc_ref, vp_ref, vc_ref, ag0_ref, ag1_ref, y_ref, z_ref, lse_ref):
        kvs = [slice(hk * HEAD_DIM, (hk + 1) * HEAD_DIM) for hk in range(N_KV_HEADS)]
        qgs = [(_stack_heads(q_ref, hk, F32) * ATTN_SCALE).astype(BF16) for hk in range(N_KV_HEADS)]
        s_prev = [_dot_nt(qgs[hk], kp_ref[:, kvs[hk]].astype(BF16)) for hk in range(N_KV_HEADS)]
        s_cur = [_dot_nt(qgs[hk], kc_ref[:, kvs[hk]].astype(BF16)) for hk in range(N_KV_HEADS)]
        for hk in range(N_KV_HEADS):
            pp, pc, invs = [], [], []
            for g in range(GROUP):
                h = GROUP * hk + g
                rows = slice(g * BLOCK, (g + 1) * BLOCK)
                p_prev, p_cur, inv, lse = _attn_exps(s_prev[hk][rows], s_cur[hk][rows], sink_ref[h], bias_ref[h])
                pp.append(p_prev.astype(BF16))
                pc.append(p_cur.astype(BF16))
                invs.append(inv)
                lse_ref[:, h:h + 1] = lse
            og = _dot(jnp.concatenate(pp, axis=0), vp_ref[:, kvs[hk]].astype(BF16)) + _dot(
                jnp.concatenate(pc, axis=0), vc_ref[:, kvs[hk]].astype(BF16))
            for g in range(GROUP):
                h = GROUP * hk + g
                y_ref[:, h * HEAD_DIM:(h + 1) * HEAD_DIM] = og[g * BLOCK:(g + 1) * BLOCK] * invs[g]
        ag = jnp.concatenate([ag0_ref[...], ag1_ref[...]], axis=1).astype(F32)
        z_ref[...] = (y_ref[...] * (ag * _sigmoid(ag))).astype(BF16)

    prev = lambda c: (lambda i: (jnp.maximum(i - 1, 0), c))
    cur = lambda c: (lambda i: (i, c))
    return pl.pallas_call(
        body,
        name="attn_fwd",
        grid=(nb,),
        in_specs=[
            pl.BlockSpec(memory_space=pltpu.SMEM),
            _bias_spec(),
            pl.BlockSpec((BLOCK, 1024), lambda i: (i, COL_Q // 4)),
            pl.BlockSpec((BLOCK, D_KV), prev(COL_K)),
            pl.BlockSpec((BLOCK, D_KV), cur(COL_K)),
            pl.BlockSpec((BLOCK, D_KV), prev(COL_V)),
            pl.BlockSpec((BLOCK, D_KV), cur(COL_V)),
            pl.BlockSpec((BLOCK, 512), lambda i: (i, COL_ATTN_GATE // 2)),
            pl.BlockSpec((BLOCK, 512), lambda i: (i, COL_ATTN_GATE // 2 + 1)),
        ],
        out_specs=[pl.BlockSpec((BLOCK, 1024), lambda i: (i, 0)), pl.BlockSpec((BLOCK, 1024), lambda i: (i, 0)),
                   pl.BlockSpec((BLOCK, N_Q_HEADS), lambda i: (i, 0))],
        out_shape=[_sds((T, 1024), F32), _sds((T, 1024), BF16), _sds((T, N_Q_HEADS), F32)],
        compiler_params=_params(("arbitrary",), 32),
    )(sinks, *_hbm(bias, proj, proj, proj, proj, proj, proj, proj))


def _attn_bwd(proj, y_attn, lse, dz_attn, sinks, bias, token):
    T = proj.shape[0]
    nb = T // BLOCK

    def body(sink_ref, bias_ref, q_ref, kp_ref, kc_ref, vp_ref, vc_ref, ag0_ref, ag1_ref, y_ref, lse_ref, dz_ref,
             token_ref, dq_ref, dk_ref, dv_ref, dag_ref, ds_ref, dy_s):
        i = pl.program_id(0)

        @pl.when(i == 0)
        def _():
            ds_ref[...] = jnp.zeros_like(ds_ref)

        lane = lax.broadcasted_iota(jnp.int32, (8, 128), 1)
        sub = lax.broadcasted_iota(jnp.int32, (8, 128), 0)
        ag = jnp.concatenate([ag0_ref[...], ag1_ref[...]], axis=1).astype(F32)
        dz = dz_ref[...]
        sg = _sigmoid(ag)
        dag_ref[...] = (dz * y_ref[...] * (sg * (1.0 + ag * (1.0 - sg)))).astype(BF16)
        dy_s[...] = dz * (ag * sg)
        r_cur = pl.multiple_of(i * BLOCK, BLOCK)
        r_prev = pl.multiple_of(jnp.maximum(i - 1, 0) * BLOCK, BLOCK)
        dk_cur, dv_cur, dk_prev, dv_prev = [], [], [], []
        ds_acc = jnp.zeros((8, 128), F32)
        for hk in range(N_KV_HEADS):
            ks = slice(hk * HEAD_DIM, (hk + 1) * HEAD_DIM)
            qg = (_stack_heads(q_ref, hk, F32) * ATTN_SCALE).astype(BF16)
            dog = _stack_heads(dy_s, hk, F32)
            og = _stack_heads(y_ref, hk, F32)
            dog_b = dog.astype(BF16)
            kp = kp_ref[:, ks].astype(BF16)
            kc = kc_ref[:, ks].astype(BF16)
            vp = vp_ref[:, ks].astype(BF16)
            vc = vc_ref[:, ks].astype(BF16)
            s_prev = _dot_nt(qg, kp)
            s_cur = _dot_nt(qg, kc)
            dp_prev = _dot_nt(dog_b, vp)
            dp_cur = _dot_nt(dog_b, vc)
            dvec = jnp.sum(dog * og, axis=-1, keepdims=True)
            pp, pc, dsp, dsc = [], [], [], []
            for g in range(GROUP):
                h = GROUP * hk + g
                rows = slice(g * BLOCK, (g + 1) * BLOCK)
                p_prev, p_cur, p_sink = _attn_probs(
                    s_prev[rows], s_cur[rows], sink_ref[h], bias_ref[h], lse_ref[:, h:h + 1])
                d_h = dvec[rows]
                pp.append(p_prev.astype(BF16))
                pc.append(p_cur.astype(BF16))
                dsp.append((p_prev * (dp_prev[rows] - d_h)).astype(BF16))
                dsc.append((p_cur * (dp_cur[rows] - d_h)).astype(BF16))
                dsink = -jnp.sum(p_sink * d_h, axis=0, keepdims=True)
                ds_acc = ds_acc + jnp.where(jnp.logical_and(lane == h, sub == 1), dsink, 0.0)
            pp = jnp.concatenate(pp, axis=0)
            pc = jnp.concatenate(pc, axis=0)
            dsp = jnp.concatenate(dsp, axis=0)
            dsc = jnp.concatenate(dsc, axis=0)
            dqg = (_dot(dsp, kp) + _dot(dsc, kc)) * ATTN_SCALE
            for g in range(GROUP):
                h = GROUP * hk + g
                dq_ref[:, h * HEAD_DIM:(h + 1) * HEAD_DIM] = dqg[g * BLOCK:(g + 1) * BLOCK].astype(BF16)
            dk_ref[pl.ds(r_cur, BLOCK), ks] = _dot_tn(dsc, qg)
            dv_ref[pl.ds(r_cur, BLOCK), ks] = _dot_tn(pc, dog_b)
            dk_prev.append(_dot_tn(dsp, qg))
            dv_prev.append(_dot_tn(pp, dog_b))
        ds_ref[:, 0:128] += ds_acc

        @pl.when(i > 0)
        def _():
            for hk in range(N_KV_HEADS):
                ks = slice(hk * HEAD_DIM, (hk + 1) * HEAD_DIM)
                dk_ref[pl.ds(r_prev, BLOCK), ks] += dk_prev[hk]
                dv_ref[pl.ds(r_prev, BLOCK), ks] += dv_prev[hk]

    prev = lambda c: (lambda i: (jnp.maximum(i - 1, 0), c))
    cur = lambda c: (lambda i: (i, c))
    blk = pl.BlockSpec((BLOCK, 1024), lambda i: (i, 0))
    whole = pl.BlockSpec((T, D_KV), lambda i: (0, 0))
    return pl.pallas_call(
        body,
        name="attn_bwd",
        grid=(nb,),
        in_specs=[
            pl.BlockSpec(memory_space=pltpu.SMEM),
            _bias_spec(),
            pl.BlockSpec((BLOCK, 1024), lambda i: (i, COL_Q // 4)),
            pl.BlockSpec((BLOCK, D_KV), prev(COL_K)),
            pl.BlockSpec((BLOCK, D_KV), cur(COL_K)),
            pl.BlockSpec((BLOCK, D_KV), prev(COL_V)),
            pl.BlockSpec((BLOCK, D_KV), cur(COL_V)),
            pl.BlockSpec((BLOCK, 512), lambda i: (i, COL_ATTN_GATE // 2)),
            pl.BlockSpec((BLOCK, 512), lambda i: (i, COL_ATTN_GATE // 2 + 1)),
            blk,
            pl.BlockSpec((BLOCK, N_Q_HEADS), lambda i: (i, 0)),
            blk,
            pl.BlockSpec((8, 128), lambda i: (0, 0)),
        ],
        out_specs=[blk, whole, whole, blk, pl.BlockSpec((8, 1024), lambda i: (0, 0))],
        out_shape=[_sds((T, 1024), BF16), _sds((T, D_KV), F32), _sds((T, D_KV), F32), _sds((T, 1024), BF16),
                   _sds((8, 1024), F32)],
        scratch_shapes=[pltpu.VMEM((BLOCK, 1024), F32)],
        compiler_params=_params(("arbitrary",), 48),
    )(sinks, *_hbm(bias, proj, proj, proj, proj, proj, proj, proj, y_attn, lse, dz_attn, token))


def _head(x, target, z_rnn, z_attn, proj, b_gate, g_post, w_rnn_out, w_attn_out, w_out):
    T = x.shape[0]
    tm = 256

    def body(x_ref, t_ref, zr_ref, za_ref, ml0_ref, ml1_ref, ml2_ref, ml3_ref, bg_ref, gp_ref, wr_ref, wa_ref, wo_ref,
             dyx_ref, dzr_ref, dza_ref, dml_ref, dout_ref, dbr_ref, dba_ref, mt_ref, zat_ref, sm_ref):
        @pl.when(pl.program_id(0) == 0)
        def _():
            sm_ref[...] = jnp.zeros_like(sm_ref)

        wr, wa, wo = wr_ref[...], wa_ref[...], wo_ref[...]
        br_rnn = _dot(zr_ref[...], wr)
        br_attn = _dot(za_ref[...], wa)
        zat_ref[...] = za_ref[...].astype(F32).T.astype(BF16)
        ml_rnn = jnp.concatenate([ml0_ref[...], ml1_ref[...]], axis=1).astype(F32)
        ml_attn = jnp.concatenate([ml2_ref[...], ml3_ref[...]], axis=1).astype(F32)
        g_rnn = _sigmoid(ml_rnn + bg_ref[:, 0:D_MODEL])
        g_attn = _sigmoid(ml_attn + bg_ref[:, D_MODEL:2 * D_MODEL])
        merged = g_rnn * br_rnn + g_attn * br_attn
        mb = merged.astype(BF16)
        mt_ref[...] = merged.T.astype(BF16)
        out = _dot(mb, wo)
        rstd = lax.rsqrt(jnp.mean(out * out, axis=-1, keepdims=True) + EPS)
        n = out * rstd
        gp = gp_ref[...]
        err = (x_ref[...] + n * gp) - t_ref[...]
        sm_ref[pl.ds(3, 1), :] += 0.5 * jnp.sum(jnp.mean(err * err, axis=-1, keepdims=True), axis=0, keepdims=True)
        dy = err * (1.0 / D_MODEL)
        dyx_ref[...] = dy
        sm_ref[pl.ds(0, 1), :] += jnp.sum(dy * n, axis=0, keepdims=True)
        dn = dy * gp
        dout = (rstd * (dn - n * jnp.mean(dn * n, axis=-1, keepdims=True))).astype(BF16)
        dout_ref[...] = dout
        dmerged = _dot_nt(dout, wo)
        dml_r = (dmerged * br_rnn) * (g_rnn * (1.0 - g_rnn))
        dml_a = (dmerged * br_attn) * (g_attn * (1.0 - g_attn))
        dml_ref[:, 0:D_MODEL] = dml_r.astype(BF16)
        dml_ref[:, D_MODEL:2 * D_MODEL] = dml_a.astype(BF16)
        sm_ref[pl.ds(1, 1), :] += jnp.sum(dml_r, axis=0, keepdims=True)
        sm_ref[pl.ds(2, 1), :] += jnp.sum(dml_a, axis=0, keepdims=True)
        dbr = (dmerged * g_rnn).astype(BF16)
        dba = (dmerged * g_attn).astype(BF16)
        dbr_ref[...] = dbr
        dba_ref[...] = dba
        dzr_ref[...] = _dot_nt(dbr, wr)
        dza_ref[...] = _dot_nt(dba, wa)

    tile = pl.BlockSpec((tm, D_MODEL), lambda i: (i, 0))
    wspec = pl.BlockSpec((D_MODEL, D_MODEL), lambda i: (0, 0))
    ml = lambda q: pl.BlockSpec((tm, 512), lambda i: (i, COL_MERGE // 2 + q))
    return pl.pallas_call(
        body,
        name="head",
        grid=(T // tm,),
        in_specs=[
            tile, tile, tile, tile,
            ml(0), ml(1), ml(2), ml(3),
            pl.BlockSpec((1, 2 * D_MODEL), lambda i: (0, 0)),
            pl.BlockSpec((1, D_MODEL), lambda i: (0, 0)),
            wspec, wspec, wspec,
        ],
        out_specs=[
            tile, tile, tile,
            pl.BlockSpec((tm, 2 * D_MODEL), lambda i: (i, 0)),
            tile, tile, tile,
            pl.BlockSpec((D_MODEL, tm), lambda i: (0, i)), pl.BlockSpec((D_MODEL, tm), lambda i: (0, i)),
            pl.BlockSpec((8, D_MODEL), lambda i: (0, 0)),
        ],
        out_shape=[
            _sds((T, D_MODEL), F32), _sds((T, D_MODEL), F32), _sds((T, D_MODEL), F32),
            _sds((T, 2 * D_MODEL), BF16),
            _sds((T, D_MODEL), BF16), _sds((T, D_MODEL), BF16), _sds((T, D_MODEL), BF16),
            _sds((D_MODEL, T), BF16), _sds((D_MODEL, T), BF16),
            _sds((8, D_MODEL), F32),
        ],
        compiler_params=_params(("arbitrary",), 56),
    )(*_hbm(x, target, z_rnn, z_attn, proj, proj, proj, proj, b_gate, g_post, w_rnn_out, w_attn_out, w_out))


def _matmul_t(at, b, name):
    M, T = at.shape
    N = b.shape[1]
    tk = min(1024, T)
    nt = T // tk

    def body(a_ref, b_ref, o_ref, ob_ref):
        @pl.when(pl.program_id(0) == 0)
        def _():
            o_ref[...] = jnp.zeros_like(o_ref)

        o_ref[...] += _dot(a_ref[...], b_ref[...])

        @pl.when(pl.program_id(0) == nt - 1)
        def _():
            ob_ref[...] = o_ref[...].astype(BF16)

    whole = pl.BlockSpec((M, N), lambda t: (0, 0))
    return pl.pallas_call(
        body,
        name=name,
        grid=(nt,),
        in_specs=[pl.BlockSpec((M, tk), lambda t: (0, t)), pl.BlockSpec((tk, N), lambda t: (t, 0))],
        out_specs=[whole, whole],
        out_shape=[_sds((M, N), F32), _sds((M, N), BF16)],
        compiler_params=_params(("arbitrary",), 48),
    )(*_hbm(at, b))


DPROJ_WIDTHS = (D_RNN, D_RNN, 1024, D_KV, D_KV, 1024, 2 * D_MODEL)


def _dproj_segments():
    segs, start = [[] for _ in range(N_CHIPS)], 0
    for p, width in enumerate(DPROJ_WIDTHS):
        for c in range(N_CHIPS):
            lo, hi = max(start, c * W_IN_SHARD), min(start + width, (c + 1) * W_IN_SHARD)
            if lo < hi:
                segs[c].append((p, lo - start, hi - start, lo - c * W_IN_SHARD, hi - c * W_IN_SHARD))
        start += width
    return segs


def _dh_bwd(pieces, w_in_g, x, dyx, g_pre, token):
    T = x.shape[0]
    tm = min(512, T)
    n = len(pieces)
    segs = _dproj_segments()

    def body(*refs):
        p_refs, w_hbm, x_ref, dyx_ref, g_ref = refs[0:n], refs[n], refs[n + 1], refs[n + 2], refs[n + 3]
        gx_ref, dg_ref, w_ref, w_sems = refs[n + 5], refs[n + 6], refs[n + 7], refs[n + 8]
        first = pl.program_id(0) == 0
        w_copies = [pltpu.make_async_copy(w_hbm.at[c], w_ref.at[c], w_sems.at[c]) for c in range(N_CHIPS)]

        @pl.when(first)
        def _():
            for cp in w_copies:
                cp.start(priority=1)
            dg_ref[...] = jnp.zeros_like(dg_ref)

        dh = None
        for c in range(N_CHIPS):
            pl.when(first)(w_copies[c].wait)
            for p, a0, a1, u0, u1 in segs[c]:
                part = _dot_nt(p_refs[p][:, a0:a1].astype(BF16), w_ref[c, :, u0:u1])
                dh = part if dh is None else dh + part
        xv = x_ref[...]
        rstd = lax.rsqrt(jnp.mean(xv * xv, axis=-1, keepdims=True) + EPS)
        nx = xv * rstd
        dhg = dh * g_ref[...]
        gx_ref[...] = dyx_ref[...] + rstd * (dhg - nx * jnp.mean(dhg * nx, axis=-1, keepdims=True))
        dg_ref[pl.ds(0, 1), :] += jnp.sum(dh * nx, axis=0, keepdims=True)

    tile = pl.BlockSpec((tm, D_MODEL), lambda i: (i, 0))
    return pl.pallas_call(
        body,
        name="dh_bwd",
        grid=(T // tm,),
        in_specs=[pl.BlockSpec((tm, w), lambda i: (i, 0)) for w in DPROJ_WIDTHS] + [
            ANY, tile, tile,
            pl.BlockSpec((1, D_MODEL), lambda i: (0, 0)),
            pl.BlockSpec((8, 128), lambda i: (0, 0)),
        ],
        out_specs=[tile, pl.BlockSpec((8, D_MODEL), lambda i: (0, 0))],
        out_shape=[_sds((T, D_MODEL), F32), _sds((8, D_MODEL), F32)],
        scratch_shapes=[pltpu.VMEM(w_in_g.shape, BF16), pltpu.SemaphoreType.DMA((N_CHIPS,))],
        compiler_params=_params(("arbitrary",), 56),
    )(*_hbm(*pieces, w_in_g, x, dyx, g_pre, token))


def _dw_in(ht, pieces):
    T = ht.shape[1]
    tk = min(1024, T)
    nt = T // tk
    n = len(pieces)
    segs = _dproj_segments()

    def body(*refs):
        h_ref, p_refs, o_ref, ob_ref = refs[0], refs[1:n + 1], refs[n + 1], refs[n + 2]

        @pl.when(pl.program_id(1) == 0)
        def _():
            o_ref[...] = jnp.zeros_like(o_ref)

        for c in range(N_CHIPS):
            @pl.when(pl.program_id(0) == c)
            def _():
                for p, a0, a1, u0, u1 in segs[c]:
                    o_ref[:, u0:u1] += _dot(h_ref[...], p_refs[p][:, a0:a1].astype(BF16))

        @pl.when(pl.program_id(1) == nt - 1)
        def _():
            ob_ref[...] = o_ref[...].astype(BF16)

    def piece_spec(p):
        chips = [c for c in range(N_CHIPS) if any(s[0] == p for s in segs[c])]

        def index(c, t):
            used = functools.reduce(jnp.logical_or, [c == k for k in chips])
            return (jnp.where(used, t, 0), 0)

        return pl.BlockSpec((tk, DPROJ_WIDTHS[p]), index)

    return pl.pallas_call(
        body,
        name="dw_in",
        grid=(N_CHIPS, nt),
        in_specs=[pl.BlockSpec((D_MODEL, tk), lambda c, t: (0, t))] + [piece_spec(p) for p in range(n)],
        out_specs=[pl.BlockSpec((None, D_MODEL, W_IN_SHARD), lambda c, t: (c, 0, 0))] * 2,
        out_shape=[_sds((N_CHIPS, D_MODEL, W_IN_SHARD), F32), _sds((N_CHIPS, D_MODEL, W_IN_SHARD), BF16)],
        compiler_params=_params(("parallel", "arbitrary"), 56),
    )(*_hbm(ht, *pieces))


ELEMENTWISE_TILE_BYTES = MIB


def _row_tile(rows, cols, limit=ELEMENTWISE_TILE_BYTES):
    if rows * cols * 4 <= limit:
        return rows
    for t in (512, 256, 128, 64, 32, 16, 8):
        if rows % t == 0 and t * cols * 4 <= limit:
            return t
    return rows


def _group_tiles(groups):
    tiles = [_row_tile(g[0].shape[0], g[0].shape[1] * len(g)) for g in groups]
    steps = max(g[0].shape[0] // t for g, t in zip(groups, tiles))
    return steps, [g[0].shape[0] // steps for g in groups]


def _chip_sum(groups, chip_core, name):
    ps = [p for group_ps, _ in groups for p in group_ps]
    gots = [g for _, group_gots in groups for g in group_gots]
    n = len(ps)
    steps, group_rows = _group_tiles([group_ps for group_ps, _ in groups])
    rows = [tr for (group_ps, _), tr in zip(groups, group_rows) for _ in group_ps]

    def body(jc_ref, *refs):
        for a in range(n):
            p_ref, g0_ref, g1_ref, g2_ref, o_ref = refs[a], refs[n + 3 * a], refs[n + 3 * a + 1], refs[n + 3 * a + 2], \
                refs[4 * n + a]
            o_ref[...] = ((p_ref[...] + g0_ref[...].astype(F32)) + g1_ref[...].astype(F32)) + g2_ref[...].astype(F32)

    tile = lambda p, tr: pl.BlockSpec((tr, p.shape[1]), lambda i, jc_ref: (i, 0))
    rel = lambda p, tr, r: pl.BlockSpec((None, tr, p.shape[1]), lambda i, jc_ref: (r, i, 0))
    half = lambda p, tr: pl.BlockSpec((tr, p.shape[1]), lambda i, jc_ref: (jc_ref[1] * steps + i, 0))
    outs = pl.pallas_call(
        body,
        name=name,
        grid_spec=pltpu.PrefetchScalarGridSpec(
            num_scalar_prefetch=1,
            grid=(steps,),
            in_specs=[tile(p, tr) for p, tr in zip(ps, rows)] + [
                rel(p, tr, r) for p, tr in zip(ps, rows) for r in range(3)],
            out_specs=[half(p, tr) for p, tr in zip(ps, rows)],
        ),
        out_shape=[_sds((2 * p.shape[0], p.shape[1]), F32) for p in ps],
        compiler_params=_params(("parallel",), 48),
    )(chip_core, *_hbm(*ps, *[g for got in gots for g in (got, got, got)]))
    return list(outs)


def _place_shards(shards, chip, name):
    n = len(shards)
    tiles = [_row_tile(s.shape[0], s.shape[1]) for s in shards]
    steps = max(s.shape[0] // t for s, t in zip(shards, tiles))
    tiles = [s.shape[0] // steps for s in shards]

    def body(j_ref, *refs):
        for a in range(n):
            refs[n + a][...] = refs[a][...].astype(BF16)

    return pl.pallas_call(
        body,
        name=name,
        grid_spec=pltpu.PrefetchScalarGridSpec(
            num_scalar_prefetch=1,
            grid=(steps,),
            in_specs=[pl.BlockSpec((t, s.shape[1]), lambda i, j_ref: (i, 0)) for s, t in zip(shards, tiles)],
            out_specs=[pl.BlockSpec((None, t, s.shape[1]), lambda i, j_ref: (j_ref[0], i, 0))
                       for s, t in zip(shards, tiles)],
        ),
        out_shape=[_sds((N_CHIPS,) + s.shape, BF16) for s in shards],
        compiler_params=_params(("parallel",), 48),
    )(chip, *_hbm(*shards))


def _adamw_update(w, g, m, v):
    c1 = 1.0 - ADAM_B1 ** ADAM_STEP
    c2 = 1.0 - ADAM_B2 ** ADAM_STEP
    nm = ADAM_B1 * m + (1.0 - ADAM_B1) * g
    nv = ADAM_B2 * v + (1.0 - ADAM_B2) * (g * g)
    return (-ADAM_LR) * ((nm / c1) / (jnp.sqrt(nv / c2) + ADAM_EPS) + ADAM_WD * w), nm, nv


def _adamw(groups, name):
    params = [p for group in groups for p in group]
    n = len(params)
    steps, group_rows = _group_tiles([[p[0] for p in group] for group in groups])
    rows = [tr for group, tr in zip(groups, group_rows) for _ in group]

    def body(*refs):
        for a in range(n):
            w_ref, g_ref, m_ref, v_ref = refs[4 * a:4 * a + 4]
            d_ref, nm_ref, nv_ref, go_ref = refs[4 * n + 4 * a:4 * n + 4 * a + 4]
            g = g_ref[...]
            d_ref[...], nm_ref[...], nv_ref[...] = _adamw_update(w_ref[...], g, m_ref[...], v_ref[...])
            go_ref[...] = g

    specs = [pl.BlockSpec((tr, p[0].shape[1]), lambda i: (i, 0)) for p, tr in zip(params, rows) for _ in range(4)]
    outs = pl.pallas_call(
        body, name=name, grid=(steps,), in_specs=specs, out_specs=specs,
        out_shape=[_sds(p[0].shape, F32) for p in params for _ in range(4)],
        compiler_params=_params(("parallel",), 48),
    )(*_hbm(*[t for p in params for t in p]))
    return [tuple(outs[4 * a:4 * a + 4]) for a in range(n)]


def _adamw_whole(params, rows_params, packed, name):
    n, k = len(params), len(rows_params)

    def body(*refs):
        ins, packed_ref, outs = refs[0:4 * n + 3 * k], refs[4 * n + 3 * k], refs[4 * n + 3 * k + 1:]
        for a in range(n):
            w_ref, g_ref, m_ref, v_ref = ins[4 * a:4 * a + 4]
            d_ref, nm_ref, nv_ref = outs[3 * a:3 * a + 3]
            d_ref[...], nm_ref[...], nv_ref[...] = _adamw_update(w_ref[...], g_ref[...], m_ref[...], v_ref[...])
        for b, (_, (first, rows, lanes), _, _) in enumerate(rows_params):
            w_ref, m_ref, v_ref = ins[4 * n + 3 * b:4 * n + 3 * b + 3]
            d_ref, nm_ref, nv_ref, g_ref = outs[3 * n + 4 * b:3 * n + 4 * b + 4]
            if len(g_ref.shape) == 3:
                pieces, width = g_ref.shape[1:]
                for q in range(pieces):
                    g_ref[:, q:q + 1, :] = packed_ref[pl.ds(first, 1), q * width:(q + 1) * width][None]
            else:
                for r in range(rows):
                    g_ref[:, r * lanes:(r + 1) * lanes] = packed_ref[pl.ds(first + r, 1), 0:lanes]
            d_ref[...], nm_ref[...], nv_ref[...] = _adamw_update(w_ref[...], g_ref[...], m_ref[...], v_ref[...])

    def whole(t):
        return pl.BlockSpec(t.shape, lambda i: (0,) * t.ndim)

    flat = [t for p in params for t in p] + [t for w, _, m, v in rows_params for t in (w, m, v)] + [packed]
    like = [p[0] for p in params for _ in range(3)] + [p[0] for p in rows_params for _ in range(4)]
    outs = pl.pallas_call(
        body, name=name, grid=(1,), in_specs=[whole(t) for t in flat], out_specs=[whole(t) for t in like],
        out_shape=[_sds(t.shape, F32) for t in like], compiler_params=_params(("arbitrary",), 48),
    )(*_hbm(*flat))
    return [tuple(outs[3 * a:3 * a + 3]) for a in range(n)] + [
        tuple(outs[3 * n + 4 * b:3 * n + 4 * b + 4]) for b in range(k)]


def _place():
    return lax.axis_index("x"), lax.axis_index("y"), lax.axis_index("c")


def _chip_of(x, y, r):
    return (x ^ (r >> 1), y ^ (r & 1))


ANY = pl.BlockSpec(memory_space=pl.ANY)


def _gather_weights(placed, cw8):
    nbig = len(placed)
    halves = [s.shape[1] // 2 for s in placed]
    pieces = [max(1, h // 64) for h in halves]
    rows = [h // p for h, p in zip(halves, pieces)]
    order = [(a, q) for q in range(max(pieces)) for a in range(nbig) if q < pieces[a]]
    ici_sem = {(a, q, r): 3 * i + (r - 1) for i, (a, q) in enumerate(order) for r in (1, 2, 3)}
    cw_sem = {r: 3 * len(order) + (r - 1) for r in (1, 2, 3)}
    d2d_sem = {key: 3 * len(order) + 3 + k for key, k in ici_sem.items()}
    nsem = 6 * len(order) + 3

    def body(*refs):
        cw_ref, dsts, gcw_ref = refs[nbig], refs[nbig + 1:2 * nbig + 1], refs[2 * nbig + 1]
        send_sems, recv_sems = refs[2 * nbig + 2:]
        x, y, c = _place()
        j = 2 * x + y

        def piece_rows(a, q, core):
            return pl.ds(pl.multiple_of(core * halves[a] + q * rows[a], 16), rows[a])

        def ici(a, q, r):
            tx, ty = _chip_of(x, y, r)
            k = ici_sem[(a, q, r)]
            region = dsts[a].at[j, piece_rows(a, q, c), :]
            return pltpu.make_async_remote_copy(
                src_ref=region, dst_ref=region, send_sem=send_sems.at[k], recv_sem=recv_sems.at[k],
                device_id=(tx, ty, c), device_id_type=MESH)

        def ici_landed(a, q, r):
            tx, ty = _chip_of(x, y, r)
            k = ici_sem[(a, q, r)]
            region = dsts[a].at[2 * tx + ty, piece_rows(a, q, c), :]
            return pltpu.make_async_remote_copy(
                src_ref=region, dst_ref=region, send_sem=send_sems.at[k], recv_sem=recv_sems.at[k],
                device_id=(tx, ty, c), device_id_type=MESH)

        def d2d(a, q, r, core):
            tx, ty = _chip_of(x, y, r)
            k = d2d_sem[(a, q, r)]
            region = dsts[a].at[2 * tx + ty, piece_rows(a, q, core), :]
            return pltpu.make_async_remote_copy(
                src_ref=region, dst_ref=region, send_sem=send_sems.at[k], recv_sem=recv_sems.at[k],
                device_id=(x, y, 1 - c), device_id_type=MESH)

        def cw_copy(r):
            tx, ty = _chip_of(x, y, r)
            k = cw_sem[r]
            return pltpu.make_async_remote_copy(
                src_ref=cw_ref, dst_ref=gcw_ref.at[j], send_sem=send_sems.at[k], recv_sem=recv_sems.at[k],
                device_id=(tx, ty, c), device_id_type=MESH)

        def cw_landed(r):
            tx, ty = _chip_of(x, y, r)
            k = cw_sem[r]
            region = gcw_ref.at[2 * tx + ty]
            return pltpu.make_async_remote_copy(
                src_ref=region, dst_ref=region, send_sem=send_sems.at[k], recv_sem=recv_sems.at[k],
                device_id=(tx, ty, c), device_id_type=MESH)

        def relay(a, q, origin, to):
            ox, oy = _chip_of(x, y, origin)
            tx, ty = _chip_of(x, y, to)
            k = ici_sem[(a, q, 3)]
            region = dsts[a].at[2 * ox + oy, piece_rows(a, q, c), :]
            return pltpu.make_async_remote_copy(
                src_ref=region, dst_ref=region, send_sem=send_sems.at[k], recv_sem=recv_sems.at[k],
                device_id=(tx, ty, c), device_id_type=MESH)

        first = [ici(a, q, r) for (a, q) in order for r in (1, 2)] + [cw_copy(r) for r in (1, 2, 3)]
        for cp in first:
            cp.start()
        passed = []
        for (a, q) in order:
            for r in (1, 2):
                ici_landed(a, q, r).wait_recv()
                if q % 2 == r - 1:
                    cp = relay(a, q, r, 3 - r)
                    cp.start()
                    passed.append(cp)
                cp = d2d(a, q, r, c)
                cp.start()
                passed.append(cp)
        for (a, q) in order:
            ici_landed(a, q, 3).wait_recv()
            cp = d2d(a, q, 3, c)
            cp.start()
            passed.append(cp)
        for r in (1, 2, 3):
            cw_landed(r).wait_recv()
        for (a, q) in order:
            for r in (1, 2, 3):
                d2d(a, q, r, 1 - c).wait_recv()
        for cp in first + passed:
            cp.wait_send()

    return pl.pallas_call(
        body,
        name="gather_weights",
        in_specs=[ANY] * (nbig + 1),
        out_specs=[ANY] * (nbig + 1),
        out_shape=[_sds(s.shape, s.dtype) for s in placed] + [_sds((N_CHIPS,) + cw8.shape, cw8.dtype)],
        input_output_aliases={a: a for a in range(nbig)},
        scratch_shapes=[pltpu.SemaphoreType.DMA((nsem,)), pltpu.SemaphoreType.DMA((nsem,))],
    )(*placed, cw8)


def _gather_late_start(placed, after, name):
    n = len(placed)
    halves = [s.shape[1] // 2 for s in placed]

    def body(*refs):
        g_refs = refs[0:n]
        send_sems, recv_sems, token = refs[n + 1], refs[n + 2], refs[-1]
        x, y, c = _place()
        j = 2 * x + y
        for a in range(n):
            mine = g_refs[a].at[j, pl.ds(pl.multiple_of(c * halves[a], 16), halves[a]), :]
            for r in (1, 2, 3):
                tx, ty = _chip_of(x, y, r)
                for to_core in (0, 1):
                    k = ((a * 3 + (r - 1)) * 2 + c) * 2 + to_core
                    pltpu.make_async_remote_copy(
                        src_ref=mine, dst_ref=mine, send_sem=send_sems.at[k], recv_sem=recv_sems.at[k],
                        device_id=(tx, ty, to_core), device_id_type=MESH).start()
        token[...] = jnp.zeros_like(token)

    hbm = lambda t: pltpu.HBM(t.shape, t.dtype)
    keep = lambda t: pltpu.with_memory_space_constraint(t, pltpu.HBM)
    nsem = 12 * n
    outs = pl.pallas_call(
        body,
        name=name,
        in_specs=[HBM] * n + [ANY],
        out_specs=(SEM, SEM, *[HBM] * n, pl.BlockSpec(memory_space=pltpu.VMEM)),
        out_shape=(pltpu.SemaphoreType.DMA((nsem,)), pltpu.SemaphoreType.DMA((nsem,)), *[hbm(p) for p in placed],
                   jax.ShapeDtypeStruct((8, 128), F32)),
        input_output_aliases={i: 2 + i for i in range(n)},
        compiler_params=pltpu.CompilerParams(has_side_effects=DATAFLOW),
    )(*[keep(p) for p in placed], after)
    return outs[0], outs[1], list(outs[2:2 + n]), outs[-1]


def _gather_late_wait(send_sems, recv_sems, thru, after, name):
    n = len(thru)
    halves = [s.shape[1] // 2 for s in thru]

    def body(*refs):
        g_refs = refs[0:n]
        send_sems, recv_sems = refs[n], refs[n + 1]
        x, y, c = _place()
        j = 2 * x + y
        for a in range(n):
            mine = g_refs[a].at[j, pl.ds(pl.multiple_of(c * halves[a], 16), halves[a]), :]
            for r in (1, 2, 3):
                tx, ty = _chip_of(x, y, r)
                for other in (0, 1):
                    k_out = ((a * 3 + (r - 1)) * 2 + c) * 2 + other
                    pltpu.make_async_remote_copy(
                        src_ref=mine, dst_ref=mine, send_sem=send_sems.at[k_out], recv_sem=recv_sems.at[k_out],
                        device_id=(tx, ty, other), device_id_type=MESH).wait_send()
                    k_in = ((a * 3 + (r - 1)) * 2 + other) * 2 + c
                    theirs = g_refs[a].at[2 * tx + ty, pl.ds(other * halves[a], halves[a]), :]
                    pltpu.make_async_remote_copy(
                        src_ref=theirs, dst_ref=theirs, send_sem=send_sems.at[k_in], recv_sem=recv_sems.at[k_in],
                        device_id=(tx, ty, other), device_id_type=MESH).wait_recv()

    hbm = lambda t: pltpu.HBM(t.shape, t.dtype)
    outs = pl.pallas_call(
        body,
        name=name,
        in_specs=[HBM] * n + [SEM, SEM, ANY],
        out_specs=[HBM] * n,
        out_shape=[hbm(t) for t in thru],
        input_output_aliases={i: i for i in range(n)},
        compiler_params=pltpu.CompilerParams(has_side_effects=DATAFLOW),
    )(*thru, send_sems, recv_sems, after)
    return list(outs)


D2D_PIECE_ROWS = 64
PAIR_SUM_TILE_BYTES = 2 * MIB


def _pair_sum(gs, gbs, chip_core, name):
    n = len(gs)
    nch, R, C = gs[0].shape
    h = R // 2
    tr = _row_tile(h, C * n, PAIR_SUM_TILE_BYTES)
    nt = h // tr
    rows = min(D2D_PIECE_ROWS, tr)

    def body(jc_ref, *refs):
        g_refs, gb_refs, p_refs, pb_refs = refs[0:n], refs[n:2 * n], refs[2 * n:3 * n], refs[3 * n:4 * n]
        got_refs, send_sems, recv_sems = refs[4 * n:5 * n], refs[5 * n], refs[5 * n + 1]
        i, j = pl.program_id(0), pl.program_id(1)
        x, y, c = _place()

        def copy(a, ti, tj, first, count):
            src_rows = pl.ds(pl.multiple_of((1 - c) * h + ti * tr + first, 16), count)
            dst_rows = pl.ds(pl.multiple_of(ti * tr + first, 16), count)
            return pltpu.make_async_remote_copy(
                src_ref=gb_refs[a].at[tj, src_rows, :], dst_ref=got_refs[a].at[tj, dst_rows, :],
                send_sem=send_sems.at[a, ti, tj], recv_sem=recv_sems.at[a, ti, tj],
                device_id=(x, y, 1 - c), device_id_type=MESH)

        @pl.when((i == 0) & (j == 0))
        def _():
            for ti in range(nt):
                for tj in range(nch):
                    for a in range(n):
                        for q in range(tr // rows):
                            copy(a, ti, tj, q * rows, rows).start()

        for a in range(n):
            copy(a, i, j, 0, tr).wait()
            s = g_refs[a][...] + got_refs[a][j, pl.ds(pl.multiple_of(i * tr, 16), tr), :].astype(F32)
            pb_refs[a][...] = s.astype(BF16)

            @pl.when(j == jc_ref[0])
            def _():
                p_refs[a][...] = s

    by_chip = pl.BlockSpec((None, tr, C), lambda i, j, jc_ref: (j, i, 0))
    outs = pl.pallas_call(
        body,
        name=name,
        grid_spec=pltpu.PrefetchScalarGridSpec(
            num_scalar_prefetch=1,
            grid=(nt, nch),
            in_specs=[pl.BlockSpec((None, tr, C), lambda i, j, jc_ref: (j, jc_ref[1] * nt + i, 0))] * n + [ANY] * n,
            out_specs=[pl.BlockSpec((tr, C), lambda i, j, jc_ref: (i, 0))] * n + [by_chip] * n,
            scratch_shapes=[pltpu.VMEM((nch, h, C), BF16)] * n + [pltpu.SemaphoreType.DMA((n, nt, nch))] * 2,
        ),
        out_shape=[_sds((h, C), F32)] * n + [_sds((nch, h, C), BF16)] * n,
        compiler_params=_params(("arbitrary", "arbitrary"), 48),
    )(chip_core, *_hbm(*gs, *gbs))
    return list(outs[:n]), list(outs[n:])


HBM = pl.BlockSpec(memory_space=pltpu.HBM)
SEM = pl.BlockSpec(memory_space=pltpu.SEMAPHORE)
DATAFLOW = pltpu.SideEffectType.DATAFLOW_SIDE_EFFECTING


def _chip_copy(p_refs, land_refs, send_sems, recv_sems, a, r, blocked):
    x, y, c = _place()
    tx, ty = _chip_of(x, y, r)
    k = a * 3 + (r - 1)
    return pltpu.make_async_remote_copy(
        src_ref=p_refs[a].at[2 * tx + ty] if blocked else p_refs[a], dst_ref=land_refs[a].at[r - 1],
        send_sem=send_sems.at[k], recv_sem=recv_sems.at[k], device_id=(tx, ty, c), device_id_type=MESH)


def _chip_exchange_start(psums, name, blocked=True):
    n = len(psums)
    lands = [lax.empty((3,) + (p.shape[1:] if blocked else p.shape), p.dtype) for p in psums]

    def body(*refs):
        p_refs, land_refs = refs[0:n], refs[n:2 * n]
        send_sems, recv_sems, token = refs[2 * n], refs[2 * n + 1], refs[-1]
        for a in range(n):
            for r in (1, 2, 3):
                _chip_copy(p_refs, land_refs, send_sems, recv_sems, a, r, blocked).start()
        token[...] = jnp.zeros_like(token)

    hbm = lambda t: pltpu.HBM(t.shape, t.dtype)
    keep = lambda t: pltpu.with_memory_space_constraint(t, pltpu.HBM)
    outs = pl.pallas_call(
        body,
        name=name,
        in_specs=[HBM] * (2 * n),
        out_specs=(SEM, SEM, *[HBM] * (2 * n), pl.BlockSpec(memory_space=pltpu.VMEM)),
        out_shape=(pltpu.SemaphoreType.DMA((3 * n,)), pltpu.SemaphoreType.DMA((3 * n,)),
                   *[hbm(p) for p in psums], *[hbm(l) for l in lands], _sds((8, 128), F32)),
        input_output_aliases={i: 2 + i for i in range(2 * n)},
        compiler_params=pltpu.CompilerParams(has_side_effects=DATAFLOW),
    )(*[keep(p) for p in psums], *[keep(l) for l in lands])
    return outs[0], outs[1], list(outs[2:2 + n]), list(outs[2 + n:2 + 2 * n]), outs[-1]


def _chip_exchange_wait(send_sems, recv_sems, p_thru, land_thru, after, name, blocked=True):
    n = len(p_thru)

    def body(*refs):
        p_refs, land_refs = refs[0:n], refs[n:2 * n]
        send_sems, recv_sems = refs[2 * n], refs[2 * n + 1]
        for a in range(n):
            for r in (1, 2, 3):
                copy = _chip_copy(p_refs, land_refs, send_sems, recv_sems, a, r, blocked)
                copy.wait_send()
                copy.wait_recv()

    hbm = lambda t: pltpu.HBM(t.shape, t.dtype)
    outs = pl.pallas_call(
        body,
        name=name,
        in_specs=[HBM] * (2 * n) + [SEM, SEM, ANY],
        out_specs=[HBM] * (2 * n),
        out_shape=[hbm(p) for p in p_thru] + [hbm(l) for l in land_thru],
        input_output_aliases={i: i for i in range(2 * n)},
        compiler_params=pltpu.CompilerParams(has_side_effects=DATAFLOW),
    )(*p_thru, *land_thru, send_sems, recv_sems, after)
    return list(outs[0:n]), list(outs[n:2 * n])


def _pair_share(fulls):
    n = len(fulls)
    halves = [f.shape[0] // 2 for f in fulls]

    def body(*refs):
        full_refs = refs[n:2 * n]
        send_sems, recv_sems = refs[2 * n:]
        x, y, c = _place()

        def half_of(a, core):
            return full_refs[a].at[pl.ds(pl.multiple_of(core * halves[a], 8), halves[a]), :]

        def remote(a, src, dst):
            return pltpu.make_async_remote_copy(
                src_ref=src, dst_ref=dst, send_sem=send_sems.at[a], recv_sem=recv_sems.at[a],
                device_id=(x, y, 1 - c), device_id_type=MESH)

        for a in range(n):
            for q in range(halves[a] // D2D_PIECE_ROWS):
                piece = full_refs[a].at[
                    pl.ds(pl.multiple_of(c * halves[a] + q * D2D_PIECE_ROWS, 8), D2D_PIECE_ROWS), :]
                remote(a, piece, piece).start()
        for a in range(n):
            remote(a, half_of(a, c), half_of(a, c)).wait_send()
            remote(a, half_of(a, 1 - c), half_of(a, 1 - c)).wait_recv()

    return pl.pallas_call(
        body,
        name="pair_share",
        in_specs=[ANY] * n,
        out_specs=[ANY] * n,
        out_shape=[_sds(f.shape, F32) for f in fulls],
        input_output_aliases={a: a for a in range(n)},
        scratch_shapes=[pltpu.SemaphoreType.DMA((n,)), pltpu.SemaphoreType.DMA((n,))],
    )(*fulls)


def _small_pair_sum(s):
    R, C = s.shape
    V = SMALL_VECTOR_ROWS

    def body(s_ref, v_ref, m_ref, sib, send_sem, recv_sem):
        x, y, c = _place()

        def to_sib(src, dst):
            return pltpu.make_async_remote_copy(
                src_ref=src, dst_ref=dst, send_sem=send_sem, recv_sem=recv_sem,
                device_id=(x, y, 1 - c), device_id_type=MESH)

        for q in range(R // 8):
            to_sib(s_ref.at[pl.ds(8 * q, 8), :], sib.at[pl.ds(8 * q, 8), :]).start()
        to_sib(s_ref, sib).wait()
        v_ref[...] = s_ref[pl.ds(0, V), :] + sib[pl.ds(0, V), :]
        m_ref[...] = (s_ref[pl.ds(V, R - V), :] + sib[pl.ds(V, R - V), :]).astype(BF16)

    return pl.pallas_call(
        body,
        name="small_pair_sum",
        in_specs=[pl.BlockSpec(memory_space=pltpu.VMEM)],
        out_specs=[pl.BlockSpec(memory_space=pltpu.VMEM)] * 2,
        out_shape=[jax.ShapeDtypeStruct((V, C), F32), jax.ShapeDtypeStruct((R - V, C), BF16)],
        scratch_shapes=[pltpu.VMEM((R, C), F32), pltpu.SemaphoreType.DMA, pltpu.SemaphoreType.DMA],
    )(s)


def _small_total(chip, own, landed):
    V, C = own[0].shape
    M = own[1].shape[0]

    def body(j_ref, v_ref, m_ref, lv_ref, lm_ref, o_ref, chips_v, chips_m):
        j = j_ref[0]
        chips_v[j] = v_ref[...]
        chips_m[j] = m_ref[...]
        for r in (1, 2, 3):
            chips_v[j ^ r] = lv_ref[r - 1]
            chips_m[j ^ r] = lm_ref[r - 1]
        o_ref[pl.ds(0, V), :] = (chips_v[0] + chips_v[1]) + (chips_v[2] + chips_v[3])
        o_ref[pl.ds(V, M), :] = (chips_m[0].astype(F32) + chips_m[1].astype(F32)) + (
            chips_m[2].astype(F32) + chips_m[3].astype(F32))

    vmem = pl.BlockSpec(memory_space=pltpu.VMEM)
    return pl.pallas_call(
        body,
        name="small_total",
        in_specs=[pl.BlockSpec(memory_space=pltpu.SMEM), vmem, vmem, vmem, vmem],
        out_specs=vmem,
        out_shape=jax.ShapeDtypeStruct((V + M, C), F32),
        scratch_shapes=[pltpu.VMEM((N_CHIPS, V, C), F32), pltpu.VMEM((N_CHIPS, M, C), BF16)],
    )(chip, own[0], own[1], landed[0], landed[1])


def _local_grads(x, target, g_pre, w_in_g, b_gate, conv_w, conv_b, w_rg_a, b_rg_a, w_rg_x, b_rg_x, lam, sinks,
                 out_weights, fwd_token, g_post, on_out_grads, on_w_in_grad):
    b_a = b_rg_a.reshape(1, D_RNN)
    b_x = b_rg_x.reshape(1, D_RNN)

    proj, ht = _proj_fwd(x, g_pre, w_in_g)
    y_rnn, z_rnn, conv, z_rnn_t = _rnn_fwd(proj, conv_w, conv_b, w_rg_a, w_rg_x, b_a, b_x, lam, fwd_token)
    bias = _attn_bias()
    y_attn, z_attn, lse = _attn_fwd(proj, sinks, bias)
    w_rnn_out, w_attn_out, w_out = out_weights(z_attn)
    dyx, dz_rnn, dz_attn, dml, dout, dbr_rnn, dbr_attn, merged_t, z_attn_t, head_small = _head(
        x, target, z_rnn, z_attn, proj, b_gate, g_post, w_rnn_out, w_attn_out, w_out)
    out_grads = [_matmul_t(z_rnn_t, dbr_rnn, "dw_rnn_out"), _matmul_t(z_attn_t, dbr_attn, "dw_attn_out"),
                 _matmul_t(merged_t, dout, "dw_out")]
    shard_rows = lambda d: d.reshape(N_CHIPS, OUT_SHARD, D_MODEL)
    token = on_out_grads([shard_rows(g) for g, _ in out_grads], [shard_rows(gb) for _, gb in out_grads])
    dq, dk, dv, dag, attn_small = _attn_bwd(proj, y_attn, lse, dz_attn, sinks, bias, token)
    drx, drg, dwa, dwx, rnn_small = _rnn_bwd(proj, conv, y_rnn, dz_rnn, conv_w, w_rg_a, w_rg_x, b_a, b_x, lam)
    dproj = [drx, drg, dq, dk, dv, dag, dml]
    token = on_w_in_grad(*_dw_in(ht, dproj))
    grad_x, dh_small = _dh_bwd(dproj, w_in_g, x, dyx, g_pre, token)
    small = jnp.concatenate([rnn_small, head_small, dh_small + attn_small,
                             dwa.reshape(64, 1024), dwx.reshape(64, 1024)], axis=0)
    return grad_x, small


ROW_LOSS = 11


SMALL_ROW_TENSORS = {"b_rg_a": (0, 1, D_RNN), "b_rg_x": (1, 1, D_RNN), "lru_lambda": (2, 1, D_RNN),
                     "conv_b": (3, 1, D_RNN), "post_norm_g": (8, 1, D_MODEL), "b_gate": (9, 2, D_MODEL),
                     "pre_norm_g": (16, 1, D_MODEL), "attn_sinks": (17, 1, N_Q_HEADS)}


def _unpack_small(s, conv_cols):
    return {
        "conv_w": s[4:8, 0:conv_cols].reshape(1, CONV_W, conv_cols),
        "w_rg_a": s[24:88].reshape(1, 16, 64, 64), "w_rg_x": s[88:152].reshape(1, 16, 64, 64),
    }


WEIGHTS = ["pre_norm_g", "w_in", "b_gate", "conv_w", "conv_b", "w_rg_a", "b_rg_a", "w_rg_x", "b_rg_x", "lru_lambda",
           "attn_sinks", "w_rnn_out", "w_attn_out", "w_out", "post_norm_g"]
BIG = ["w_in", "w_rnn_out", "w_attn_out", "w_out"]


def kernel(x, pre_norm_g, w_in, b_gate, conv_w, conv_b, w_rg_a, b_rg_a, w_rg_x, b_rg_x, lru_lambda, attn_sinks, w_rnn_out, w_attn_out, w_out, post_norm_g, loss_target, m_pre_norm_g, m_w_in, m_b_gate, m_conv_w, m_conv_b, m_w_rg_a, m_b_rg_a, m_w_rg_x, m_b_rg_x, m_lru_lambda, m_attn_sinks, m_w_rnn_out, m_w_attn_out, m_w_out, m_post_norm_g, v_pre_norm_g, v_w_in, v_b_gate, v_conv_w, v_conv_b, v_w_rg_a, v_b_rg_a, v_w_rg_x, v_b_rg_x, v_lru_lambda, v_attn_sinks, v_w_rnn_out, v_w_attn_out, v_w_out, v_post_norm_g):
    w = dict(pre_norm_g=pre_norm_g, w_in=w_in, b_gate=b_gate, conv_w=conv_w, conv_b=conv_b, w_rg_a=w_rg_a,
             b_rg_a=b_rg_a, w_rg_x=w_rg_x, b_rg_x=b_rg_x, lru_lambda=lru_lambda, attn_sinks=attn_sinks,
             w_rnn_out=w_rnn_out, w_attn_out=w_attn_out, w_out=w_out, post_norm_g=post_norm_g)
    m = dict(pre_norm_g=m_pre_norm_g, w_in=m_w_in, b_gate=m_b_gate, conv_w=m_conv_w, conv_b=m_conv_b, w_rg_a=m_w_rg_a,
             b_rg_a=m_b_rg_a, w_rg_x=m_w_rg_x, b_rg_x=m_b_rg_x, lru_lambda=m_lru_lambda, attn_sinks=m_attn_sinks,
             w_rnn_out=m_w_rnn_out, w_attn_out=m_w_attn_out, w_out=m_w_out, post_norm_g=m_post_norm_g)
    v = dict(pre_norm_g=v_pre_norm_g, w_in=v_w_in, b_gate=v_b_gate, conv_w=v_conv_w, conv_b=v_conv_b, w_rg_a=v_w_rg_a,
             b_rg_a=v_b_rg_a, w_rg_x=v_w_rg_x, b_rg_x=v_b_rg_x, lru_lambda=v_lru_lambda, attn_sinks=v_attn_sinks,
             w_rnn_out=v_w_rnn_out, w_attn_out=v_w_attn_out, w_out=v_w_out, post_norm_g=v_post_norm_g)
    chip = 2 * lax.axis_index("x") + lax.axis_index("y")

    chip_idx = chip.astype(jnp.int32).reshape(1)
    chip_core = jnp.stack([chip, lax.axis_index("c")]).astype(jnp.int32)
    cw8 = jnp.pad(conv_w[0], ((0, 8 - CONV_W), (0, 0)))
    placed = _place_shards([w_in[0], w_rnn_out[0], w_attn_out[0], w_out[0]], chip_idx, "place_shards")
    win_g, cw_g = _gather_weights(placed[:1], cw8)
    late_send, late_recv, late_thru, late_token = _gather_late_start(placed[1:], win_g, "gather_late_start")
    cw_g = lax.dynamic_update_slice_in_dim(cw_g, cw8[None], chip, axis=0)
    conv_w_full = jnp.transpose(cw_g[:, 0:CONV_W, :], (1, 0, 2)).reshape(CONV_W, D_RNN)

    started = {}

    def start_reduction(tag, grads, grads_b16):
        psums, psums_b16 = _pair_sum(grads, grads_b16, chip_core, "pair_sum_" + tag)
        send_sems, recv_sems, p_thru, land_thru, token = _chip_exchange_start(psums_b16, "chip_exchange_start_" + tag)
        started[tag] = (psums, send_sems, recv_sems, p_thru, land_thru)
        return token

    def end_reduction(tag, after):
        psums, send_sems, recv_sems, p_thru, land_thru = started[tag]
        _, landed = _chip_exchange_wait(send_sems, recv_sems, p_thru, land_thru, after, "chip_exchange_wait_" + tag)
        return psums, landed

    def out_weights(after):
        gathered = _gather_late_wait(late_send, late_recv, late_thru, after, "gather_late_wait")
        return [g.reshape(D_MODEL, D_MODEL) for g in gathered]

    grad_x, small = _local_grads(
        x[0], loss_target[0], pre_norm_g, win_g, b_gate, conv_w_full, conv_b, w_rg_a[0], b_rg_a[0], w_rg_x[0],
        b_rg_x[0], lru_lambda, attn_sinks[0], out_weights, late_token, post_norm_g,
        on_out_grads=lambda grads, grads_b16: start_reduction("out", grads, grads_b16),
        on_w_in_grad=lambda grad, grad_b16: start_reduction("in", [grad], [grad_b16]))

    small_chip = _small_pair_sum(small)
    small_send, small_recv, small_thru, small_land, small_token = _chip_exchange_start(
        list(small_chip), "small_exchange_start", blocked=False)

    halves = _chip_sum([end_reduction("in", small_token), end_reduction("out", small_token)], chip_core, "chip_sum")
    gbig = dict(zip(BIG, _pair_share(halves)))

    grads, delta, new_m, new_v = {}, {}, {}, {}
    updates = _adamw([[(w[n][0], gbig[n], m[n][0], v[n][0]) for n in names] for names in (BIG[:1], BIG[1:])],
                     "adamw_big")
    for n, (d, nm, nv, g) in zip(BIG, updates):
        grads[n], delta[n], new_m[n], new_v[n] = g[None], d[None], nm[None], nv[None]

    small_own, small_landed = _chip_exchange_wait(small_send, small_recv, small_thru, small_land, delta[BIG[-1]],
                                                  "small_exchange_wait", blocked=False)
    small_sum = _small_total(chip_idx, small_own, small_landed)
    total_loss = small_sum[ROW_LOSS, 0]
    gsmall = _unpack_small(small_sum, D_RNN)
    conv_shard = D_RNN // N_CHIPS
    gsmall["conv_w"] = lax.dynamic_slice_in_dim(gsmall["conv_w"], chip * conv_shard, conv_shard, axis=2)
    for n in gsmall:
        grads[n] = gsmall[n].reshape(w[n].shape)
    updates = _adamw_whole([(w[n], grads[n], m[n], v[n]) for n in gsmall],
                           [(w[n], rows, m[n], v[n]) for n, rows in SMALL_ROW_TENSORS.items()], small_sum, "adamw_small")
    for n, (d, nm, nv, *g) in zip([*gsmall, *SMALL_ROW_TENSORS], updates):
        delta[n], new_m[n], new_v[n] = d, nm, nv
        grads.update({n: g[0]} if g else {})

    return (total_loss, grad_x[None], *[grads[n] for n in WEIGHTS], *[delta[n] for n in WEIGHTS],
            *[new_m[n] for n in WEIGHTS], *[new_v[n] for n in WEIGHTS])
```
